```python
import math
import jax, jax.numpy as jnp
from jax import lax
import numpy as np

D_MODEL = 1024
BATCH = 16
SEQ = 2048
DEPTH = 1

D_MIX = D_MODEL
GM_WIDTH = D_MIX // 2
GM_HEAD_DIM = 64
GM_HEADS = GM_WIDTH // GM_HEAD_DIM
GM_CHUNK = 128
SSM_WIDTH = D_MIX - GM_WIDTH
SSM_HEAD_DIM = 64
SSM_HEADS = SSM_WIDTH // SSM_HEAD_DIM
SSM_GROUPS = 2
SSM_STATE = 128
SSM_CONV = 4
SSM_CHUNK = 128
SSM_CONV_CH = SSM_WIDTH + 2 * SSM_GROUPS * SSM_STATE
D_FF = 4 * D_MODEL
EPS = 1e-6

IN_COLS = 2 * GM_WIDTH + SSM_WIDTH + SSM_CONV_CH + SSM_HEADS
SPLITS = (GM_WIDTH, 2 * GM_WIDTH, 2 * GM_WIDTH + SSM_WIDTH,
          2 * GM_WIDTH + SSM_WIDTH + SSM_CONV_CH)

kernel_name = "hybrid_gmlp_ssd_sandwich_block"


def rms_norm(x, w):
    xf = x.astype(jnp.float32)
    y = xf * lax.rsqrt(jnp.mean(xf * xf, axis=-1, keepdims=True) + EPS)
    return (y * w.astype(jnp.float32)).astype(x.dtype)


def layer_norm(x, w, b):
    xf = x.astype(jnp.float32)
    mu = jnp.mean(xf, axis=-1, keepdims=True)
    var = jnp.mean(jnp.square(xf - mu), axis=-1, keepdims=True)
    y = (xf - mu) * lax.rsqrt(var + EPS)
    return (y * w.astype(jnp.float32) + b.astype(jnp.float32)).astype(x.dtype)


def gmlp_mixer(u, v, ln_w, ln_b, w_s, b_s):
    bsz, L, _ = u.shape
    nc = L // GM_CHUNK
    u = jax.nn.gelu(u)
    v = jax.nn.gelu(v).reshape(bsz, L, GM_HEADS, GM_HEAD_DIM)
    v = layer_norm(v, ln_w, ln_b).reshape(bsz, nc, GM_CHUNK, GM_HEADS, GM_HEAD_DIM)
    causal = jnp.tril(jnp.ones((GM_CHUNK, GM_CHUNK), dtype=bool))
    w = jnp.where(causal[None], w_s, jnp.zeros((), w_s.dtype))
    mixed = jnp.einsum("hts,bcshp->bcthp", w, v) + b_s.T[None, None, :, :, None]
    return u * mixed.reshape(bsz, L, GM_WIDTH)


def causal_depthwise_conv(x, w, b):
    ch = x.shape[-1]
    y = lax.conv_general_dilated(
        x, w[:, None, :].astype(x.dtype), window_strides=(1,),
        padding=((SSM_CONV - 1, 0),), dimension_numbers=("NWC", "WIO", "NWC"),
        feature_group_count=ch)
    return y + b


def ssd_chunked(x, dt, a, bmat, cmat, d_skip):
    bsz, L = x.shape[0], x.shape[1]
    nc = L // SSM_CHUNK
    R = SSM_HEADS // SSM_GROUPS
    Q = SSM_CHUNK
    x = x.astype(jnp.float32).reshape(bsz, nc, Q, SSM_GROUPS, R, SSM_HEAD_DIM)
    dt = dt.astype(jnp.float32).reshape(bsz, nc, Q, SSM_GROUPS, R)
    bmat = bmat.astype(jnp.float32).reshape(bsz, nc, Q, SSM_GROUPS, SSM_STATE)
    cmat = cmat.astype(jnp.float32).reshape(bsz, nc, Q, SSM_GROUPS, SSM_STATE)
    a = a.astype(jnp.float32).reshape(SSM_GROUPS, R)
    d_skip = d_skip.astype(jnp.float32).reshape(SSM_GROUPS, R)

    a_cs = jnp.cumsum(dt * a, axis=2)
    x_dt = x * dt[..., None]

    causal = jnp.tril(jnp.ones((Q, Q), dtype=bool))[:, :, None, None]
    seg = a_cs[:, :, :, None] - a_cs[:, :, None, :]
    decay = jnp.exp(jnp.where(causal, seg, -jnp.inf))
    cb = jnp.einsum("bclgn,bcsgn->bclsg", cmat, bmat)
    y_diag = jnp.einsum("bclsg,bclsgr,bcsgrp->bclgrp", cb, decay, x_dt)

    decay_to_end = jnp.exp(a_cs[:, :, -1:] - a_cs)
    states = jnp.einsum("bcsgn,bcsgr,bcsgrp->bcgrpn", bmat, decay_to_end, x_dt)
    chunk_decay = jnp.exp(a_cs[:, :, -1])

    def step(h, inp):
        s, dcy = inp
        return h * dcy[..., None, None] + s, h
    h0 = jnp.zeros((bsz, SSM_GROUPS, R, SSM_HEAD_DIM, SSM_STATE), jnp.float32)
    _, prev = lax.scan(step, h0, (jnp.moveaxis(states, 1, 0), jnp.moveaxis(chunk_decay, 1, 0)))
    prev = jnp.moveaxis(prev, 0, 1)

    y_off = jnp.einsum("bclgn,bcgrpn,bclgr->bclgrp", cmat, prev, jnp.exp(a_cs))
    y = y_diag + y_off + d_skip[:, :, None] * x
    return y.reshape(bsz, L, SSM_HEADS * SSM_HEAD_DIM)


def mamba2_mixer(z, xbc, dt_raw, conv_w, conv_b, dt_bias, a_log, d_skip, norm_w):
    bsz, L, _ = z.shape
    xbc = jax.nn.silu(causal_depthwise_conv(xbc, conv_w, conv_b))
    xs = xbc[..., :SSM_WIDTH].reshape(bsz, L, SSM_HEADS, SSM_HEAD_DIM)
    bmat = xbc[..., SSM_WIDTH:SSM_WIDTH + SSM_GROUPS * SSM_STATE].reshape(bsz, L, SSM_GROUPS, SSM_STATE)
    cmat = xbc[..., SSM_WIDTH + SSM_GROUPS * SSM_STATE:].reshape(bsz, L, SSM_GROUPS, SSM_STATE)
    dt = jax.nn.softplus(dt_raw.astype(jnp.float32) + dt_bias.astype(jnp.float32))
    a = -jnp.exp(a_log.astype(jnp.float32))
    y = ssd_chunked(xs, dt, a, bmat, cmat, d_skip)
    y = y * jax.nn.silu(z.astype(jnp.float32))
    y = y.reshape(bsz, L, SSM_GROUPS, SSM_WIDTH // SSM_GROUPS)
    y = y * lax.rsqrt(jnp.mean(y * y, axis=-1, keepdims=True) + EPS)
    y = y.reshape(bsz, L, SSM_WIDTH) * norm_w.astype(jnp.float32)
    return y.astype(z.dtype)


def _fwd_setup_inputs(seed: int = 0) -> dict:
    key = jax.random.key(seed)
    ks = jax.random.split(key, 24)
    f32 = jnp.float32

    def gain(k, shape):
        return 1.0 + 0.02 * jax.random.normal(k, shape, f32)

    x = jax.random.normal(ks[0], (BATCH, SEQ, D_MODEL), f32)
    norm_mix_pre = gain(ks[1], (DEPTH, D_MODEL))
    w_in = jax.random.normal(ks[2], (DEPTH, D_MODEL, IN_COLS), f32) * D_MODEL ** -0.5
    gm_ln_w = gain(ks[3], (DEPTH, GM_HEADS, GM_HEAD_DIM))
    gm_ln_b = 0.02 * jax.random.normal(ks[4], (DEPTH, GM_HEADS, GM_HEAD_DIM), f32)
    gm_w_s = jax.random.normal(ks[5], (DEPTH, GM_HEADS, GM_CHUNK, GM_CHUNK), f32) * GM_CHUNK ** -0.5
    gm_b_s = gain(ks[6], (DEPTH, GM_HEADS, GM_CHUNK))
    conv_w = jax.random.normal(ks[7], (DEPTH, SSM_CONV, SSM_CONV_CH), f32) * SSM_CONV ** -0.5
    conv_b = 0.02 * jax.random.normal(ks[8], (DEPTH, SSM_CONV_CH), f32)
    dt_min, dt_max = 1e-3, 1e-1
    u = jax.random.uniform(ks[9], (DEPTH, SSM_HEADS), f32)
    dt0 = jnp.maximum(jnp.exp(u * (math.log(dt_max) - math.log(dt_min)) + math.log(dt_min)), 1e-4)
    dt_bias = dt0 + jnp.log(-jnp.expm1(-dt0))
    a_log = jnp.log(jax.random.uniform(ks[10], (DEPTH, SSM_HEADS), f32, 1.0, 16.0))
    d_skip = gain(ks[11], (DEPTH, SSM_HEADS))
    ssm_norm_w = gain(ks[12], (DEPTH, SSM_WIDTH))
    w_out = jax.random.normal(ks[13], (DEPTH, D_MIX, D_MODEL), f32) * D_MIX ** -0.5
    norm_mix_post = gain(ks[14], (DEPTH, D_MODEL))
    norm_ffn_pre = gain(ks[15], (DEPTH, D_MODEL))
    w_up = jax.random.normal(ks[16], (DEPTH, D_MODEL, D_FF), f32) * D_MODEL ** -0.5
    w_down = jax.random.normal(ks[17], (DEPTH, D_FF, D_MODEL), f32) * D_FF ** -0.5
    norm_ffn_post = gain(ks[18], (DEPTH, D_MODEL))
    return {"x": x, "norm_mix_pre": norm_mix_pre, "w_in": w_in, "gm_ln_w": gm_ln_w,
            "gm_ln_b": gm_ln_b, "gm_w_s": gm_w_s, "gm_b_s": gm_b_s, "conv_w": conv_w,
            "conv_b": conv_b, "dt_bias": dt_bias, "a_log": a_log, "d_skip": d_skip,
            "ssm_norm_w": ssm_norm_w, "w_out": w_out, "norm_mix_post": norm_mix_post,
            "norm_ffn_pre": norm_ffn_pre, "w_up": w_up, "w_down": w_down,
            "norm_ffn_post": norm_ffn_post}


def _fwd_reference(x, norm_mix_pre, w_in, gm_ln_w, gm_ln_b, gm_w_s, gm_b_s, conv_w, conv_b,
              dt_bias, a_log, d_skip, ssm_norm_w, w_out, norm_mix_post, norm_ffn_pre,
              w_up, w_down, norm_ffn_post):
    for i in range(DEPTH):
        h = rms_norm(x, norm_mix_pre[i])
        proj = jnp.einsum("bld,dk->blk", h, w_in[i])
        u_a, v_a, z_b, xbc_b, dt_b = jnp.split(proj, SPLITS, axis=-1)
        y_a = gmlp_mixer(u_a, v_a, gm_ln_w[i], gm_ln_b[i], gm_w_s[i], gm_b_s[i])
        y_b = mamba2_mixer(z_b, xbc_b, dt_b, conv_w[i], conv_b[i], dt_bias[i], a_log[i],
                           d_skip[i], ssm_norm_w[i])
        mix = jnp.concatenate([y_a, y_b], axis=-1)
        x = x + rms_norm(jnp.einsum("blk,kd->bld", mix, w_out[i]), norm_mix_post[i])
        h = rms_norm(x, norm_ffn_pre[i])
        f = jnp.square(jax.nn.relu(jnp.einsum("bld,df->blf", h, w_up[i])))
        x = x + rms_norm(jnp.einsum("blf,fd->bld", f, w_down[i]), norm_ffn_post[i])
    return x


import jax as _jax
import jax.numpy as _jnp

TWIN_FORMAT = 'train_step'
FWD_PARAMS = ['x', 'norm_mix_pre', 'w_in', 'gm_ln_w', 'gm_ln_b', 'gm_w_s', 'gm_b_s', 'conv_w', 'conv_b', 'dt_bias', 'a_log', 'd_skip', 'ssm_norm_w', 'w_out', 'norm_mix_post', 'norm_ffn_pre', 'w_up', 'w_down', 'norm_ffn_post']
TWIN_WEIGHTS = ['norm_mix_pre', 'w_in', 'gm_ln_w', 'gm_ln_b', 'gm_w_s', 'gm_b_s', 'conv_w', 'conv_b', 'dt_bias', 'a_log', 'd_skip', 'ssm_norm_w', 'w_out', 'norm_mix_post', 'norm_ffn_pre', 'w_up', 'w_down', 'norm_ffn_post']
TWIN_DIFF_INPUT = 'x'
TWIN_INPUTS = ['x', 'norm_mix_pre', 'w_in', 'gm_ln_w', 'gm_ln_b', 'gm_w_s', 'gm_b_s', 'conv_w', 'conv_b', 'dt_bias', 'a_log', 'd_skip', 'ssm_norm_w', 'w_out', 'norm_mix_post', 'norm_ffn_pre', 'w_up', 'w_down', 'norm_ffn_post', 'loss_target', 'm_norm_mix_pre', 'm_w_in', 'm_gm_ln_w', 'm_gm_ln_b', 'm_gm_w_s', 'm_gm_b_s', 'm_conv_w', 'm_conv_b', 'm_dt_bias', 'm_a_log', 'm_d_skip', 'm_ssm_norm_w', 'm_w_out', 'm_norm_mix_post', 'm_norm_ffn_pre', 'm_w_up', 'm_w_down', 'm_norm_ffn_post', 'v_norm_mix_pre', 'v_w_in', 'v_gm_ln_w', 'v_gm_ln_b', 'v_gm_w_s', 'v_gm_b_s', 'v_conv_w', 'v_conv_b', 'v_dt_bias', 'v_a_log', 'v_d_skip', 'v_ssm_norm_w', 'v_w_out', 'v_norm_mix_post', 'v_norm_ffn_pre', 'v_w_up', 'v_w_down', 'v_norm_ffn_post']
TWIN_OUTPUTS = ['loss', 'grad_x', 'grad_norm_mix_pre', 'grad_w_in', 'grad_gm_ln_w', 'grad_gm_ln_b', 'grad_gm_w_s', 'grad_gm_b_s', 'grad_conv_w', 'grad_conv_b', 'grad_dt_bias', 'grad_a_log', 'grad_d_skip', 'grad_ssm_norm_w', 'grad_w_out', 'grad_norm_mix_post', 'grad_norm_ffn_pre', 'grad_w_up', 'grad_w_down', 'grad_norm_ffn_post', 'delta_norm_mix_pre', 'delta_w_in', 'delta_gm_ln_w', 'delta_gm_ln_b', 'delta_gm_w_s', 'delta_gm_b_s', 'delta_conv_w', 'delta_conv_b', 'delta_dt_bias', 'delta_a_log', 'delta_d_skip', 'delta_ssm_norm_w', 'delta_w_out', 'delta_norm_mix_post', 'delta_norm_ffn_pre', 'delta_w_up', 'delta_w_down', 'delta_norm_ffn_post', 'new_m_norm_mix_pre', 'new_m_w_in', 'new_m_gm_ln_w', 'new_m_gm_ln_b', 'new_m_gm_w_s', 'new_m_gm_b_s', 'new_m_conv_w', 'new_m_conv_b', 'new_m_dt_bias', 'new_m_a_log', 'new_m_d_skip', 'new_m_ssm_norm_w', 'new_m_w_out', 'new_m_norm_mix_post', 'new_m_norm_ffn_pre', 'new_m_w_up', 'new_m_w_down', 'new_m_norm_ffn_post', 'new_v_norm_mix_pre', 'new_v_w_in', 'new_v_gm_ln_w', 'new_v_gm_ln_b', 'new_v_gm_w_s', 'new_v_gm_b_s', 'new_v_conv_w', 'new_v_conv_b', 'new_v_dt_bias', 'new_v_a_log', 'new_v_d_skip', 'new_v_ssm_norm_w', 'new_v_w_out', 'new_v_norm_mix_post', 'new_v_norm_ffn_pre', 'new_v_w_up', 'new_v_w_down', 'new_v_norm_ffn_post']
TWIN_LEAF_KINDS = {'loss': 'loss', 'grad_x': 'grad_x', 'grad_norm_mix_pre': 'grad_w', 'grad_w_in': 'grad_w', 'grad_gm_ln_w': 'grad_w', 'grad_gm_ln_b': 'grad_w', 'grad_gm_w_s': 'grad_w', 'grad_gm_b_s': 'grad_w', 'grad_conv_w': 'grad_w', 'grad_conv_b': 'grad_w', 'grad_dt_bias': 'grad_w', 'grad_a_log': 'grad_w', 'grad_d_skip': 'grad_w', 'grad_ssm_norm_w': 'grad_w', 'grad_w_out': 'grad_w', 'grad_norm_mix_post': 'grad_w', 'grad_norm_ffn_pre': 'grad_w', 'grad_w_up': 'grad_w', 'grad_w_down': 'grad_w', 'grad_norm_ffn_post': 'grad_w', 'delta_norm_mix_pre': 'delta_w', 'delta_w_in': 'delta_w', 'delta_gm_ln_w': 'delta_w', 'delta_gm_ln_b': 'delta_w', 'delta_gm_w_s': 'delta_w', 'delta_gm_b_s': 'delta_w', 'delta_conv_w': 'delta_w', 'delta_conv_b': 'delta_w', 'delta_dt_bias': 'delta_w', 'delta_a_log': 'delta_w', 'delta_d_skip': 'delta_w', 'delta_ssm_norm_w': 'delta_w', 'delta_w_out': 'delta_w', 'delta_norm_mix_post': 'delta_w', 'delta_norm_ffn_pre': 'delta_w', 'delta_w_up': 'delta_w', 'delta_w_down': 'delta_w', 'delta_norm_ffn_post': 'delta_w', 'new_m_norm_mix_pre': 'new_m', 'new_m_w_in': 'new_m', 'new_m_gm_ln_w': 'new_m', 'new_m_gm_ln_b': 'new_m', 'new_m_gm_w_s': 'new_m', 'new_m_gm_b_s': 'new_m', 'new_m_conv_w': 'new_m', 'new_m_conv_b': 'new_m', 'new_m_dt_bias': 'new_m', 'new_m_a_log': 'new_m', 'new_m_d_skip': 'new_m', 'new_m_ssm_norm_w': 'new_m', 'new_m_w_out': 'new_m', 'new_m_norm_mix_post': 'new_m', 'new_m_norm_ffn_pre': 'new_m', 'new_m_w_up': 'new_m', 'new_m_w_down': 'new_m', 'new_m_norm_ffn_post': 'new_m', 'new_v_norm_mix_pre': 'new_v', 'new_v_w_in': 'new_v', 'new_v_gm_ln_w': 'new_v', 'new_v_gm_ln_b': 'new_v', 'new_v_gm_w_s': 'new_v', 'new_v_gm_b_s': 'new_v', 'new_v_conv_w': 'new_v', 'new_v_conv_b': 'new_v', 'new_v_dt_bias': 'new_v', 'new_v_a_log': 'new_v', 'new_v_d_skip': 'new_v', 'new_v_ssm_norm_w': 'new_v', 'new_v_w_out': 'new_v', 'new_v_norm_mix_post': 'new_v', 'new_v_norm_ffn_pre': 'new_v', 'new_v_w_up': 'new_v', 'new_v_w_down': 'new_v', 'new_v_norm_ffn_post': 'new_v'}


def _forward(args):
    return _fwd_reference(*[args[k] for k in FWD_PARAMS])


def _output_shape():
    out = _jax.eval_shape(lambda: _forward(_fwd_setup_inputs(0)))
    return out.shape, out.dtype

N_MICROBATCH = 1
ADAM_LR = 0.001
ADAM_B1 = 0.9
ADAM_B2 = 0.999
ADAM_EPS = 1e-08
ADAM_WD = 0.01
ADAM_STEP = 10
PER_EXAMPLE_BATCH_AXIS = {'x': 0, 'loss_target': 0}
SHARED_INPUTS = []
_WEIGHT_DTYPES = {'norm_mix_pre': _jnp.float32, 'w_in': _jnp.float32, 'gm_ln_w': _jnp.float32, 'gm_ln_b': _jnp.float32, 'gm_w_s': _jnp.float32, 'gm_b_s': _jnp.float32, 'conv_w': _jnp.float32, 'conv_b': _jnp.float32, 'dt_bias': _jnp.float32, 'a_log': _jnp.float32, 'd_skip': _jnp.float32, 'ssm_norm_w': _jnp.float32, 'w_out': _jnp.float32, 'norm_mix_post': _jnp.float32, 'norm_ffn_pre': _jnp.float32, 'w_up': _jnp.float32, 'w_down': _jnp.float32, 'norm_ffn_post': _jnp.float32}
MOMENT_SCALE = {'norm_mix_pre': 7.008192e-01, 'w_in': 4.317972e-01, 'gm_ln_w': 2.208209e-01, 'gm_ln_b': 2.350261e-01, 'gm_w_s': 1.401357e-01, 'gm_b_s': 2.329425e-01, 'conv_w': 1.081518e+00, 'conv_b': 4.176985e+00, 'dt_bias': 7.653137e-01, 'a_log': 2.015162e+00, 'd_skip': 9.417943e+00, 'ssm_norm_w': 2.717395e+00, 'w_out': 3.480255e+00, 'norm_mix_post': 3.262616e+01, 'norm_ffn_pre': 1.251248e+00, 'w_up': 6.155681e-01, 'w_down': 3.582377e+00, 'norm_ffn_post': 3.350701e+01}


def _to_microbatches(a, axis):
    t = _jnp.moveaxis(a, axis, 0)
    t = t.reshape((N_MICROBATCH, t.shape[0] // N_MICROBATCH) + t.shape[1:])
    return _jnp.moveaxis(t, 1, axis + 1)


def setup_inputs(seed: int = 0) -> dict:
    inp = _fwd_setup_inputs(seed)
    key = _jax.random.fold_in(_jax.random.key(seed), 7919)
    shape, _ = _output_shape()
    out = dict(inp)
    out["loss_target"] = _jax.random.normal(_jax.random.fold_in(key, 0), shape, _jnp.float32)
    for i, name in enumerate(TWIN_WEIGHTS):
        w = inp[name].astype(_jnp.float32)
        if MOMENT_SCALE is None:
            s = _jnp.sqrt(_jnp.mean(_jnp.square(w)) + 1e-30)
        else:
            s = MOMENT_SCALE[name]
        km, kv = _jax.random.split(_jax.random.fold_in(key, i + 1))
        out[name] = w
        out["m_" + name] = s * _jax.random.normal(km, w.shape, _jnp.float32)
        out["v_" + name] = (s * s) * _jax.random.uniform(kv, w.shape, _jnp.float32, 0.5, 1.5)
    if N_MICROBATCH > 1:
        for name, axis in PER_EXAMPLE_BATCH_AXIS.items():
            out[name] = _to_microbatches(out[name], axis)
    return {'x': out['x'], 'norm_mix_pre': out['norm_mix_pre'], 'w_in': out['w_in'], 'gm_ln_w': out['gm_ln_w'], 'gm_ln_b': out['gm_ln_b'], 'gm_w_s': out['gm_w_s'], 'gm_b_s': out['gm_b_s'], 'conv_w': out['conv_w'], 'conv_b': out['conv_b'], 'dt_bias': out['dt_bias'], 'a_log': out['a_log'], 'd_skip': out['d_skip'], 'ssm_norm_w': out['ssm_norm_w'], 'w_out': out['w_out'], 'norm_mix_post': out['norm_mix_post'], 'norm_ffn_pre': out['norm_ffn_pre'], 'w_up': out['w_up'], 'w_down': out['w_down'], 'norm_ffn_post': out['norm_ffn_post'], 'loss_target': out['loss_target'], 'm_norm_mix_pre': out['m_norm_mix_pre'], 'm_w_in': out['m_w_in'], 'm_gm_ln_w': out['m_gm_ln_w'], 'm_gm_ln_b': out['m_gm_ln_b'], 'm_gm_w_s': out['m_gm_w_s'], 'm_gm_b_s': out['m_gm_b_s'], 'm_conv_w': out['m_conv_w'], 'm_conv_b': out['m_conv_b'], 'm_dt_bias': out['m_dt_bias'], 'm_a_log': out['m_a_log'], 'm_d_skip': out['m_d_skip'], 'm_ssm_norm_w': out['m_ssm_norm_w'], 'm_w_out': out['m_w_out'], 'm_norm_mix_post': out['m_norm_mix_post'], 'm_norm_ffn_pre': out['m_norm_ffn_pre'], 'm_w_up': out['m_w_up'], 'm_w_down': out['m_w_down'], 'm_norm_ffn_post': out['m_norm_ffn_post'], 'v_norm_mix_pre': out['v_norm_mix_pre'], 'v_w_in': out['v_w_in'], 'v_gm_ln_w': out['v_gm_ln_w'], 'v_gm_ln_b': out['v_gm_ln_b'], 'v_gm_w_s': out['v_gm_w_s'], 'v_gm_b_s': out['v_gm_b_s'], 'v_conv_w': out['v_conv_w'], 'v_conv_b': out['v_conv_b'], 'v_dt_bias': out['v_dt_bias'], 'v_a_log': out['v_a_log'], 'v_d_skip': out['v_d_skip'], 'v_ssm_norm_w': out['v_ssm_norm_w'], 'v_w_out': out['v_w_out'], 'v_norm_mix_post': out['v_norm_mix_post'], 'v_norm_ffn_pre': out['v_norm_ffn_pre'], 'v_w_up': out['v_w_up'], 'v_w_down': out['v_w_down'], 'v_norm_ffn_post': out['v_norm_ffn_post']}


def _loss(weights, diff, rest, loss_target):
    with _jax.named_scope("forward"):
        args = {**rest, TWIN_DIFF_INPUT: diff, **{k: w.astype(_WEIGHT_DTYPES[k]) for k, w in weights.items()}}
        y = _forward(args)
    with _jax.named_scope("loss_head"):
        err = _jnp.square(y.astype(_jnp.float32) - loss_target)
        return 0.5 * _jnp.sum(_jnp.mean(err, axis=-1)) if err.ndim else 0.5 * err


def _adamw(w, g, m, v):
    m = ADAM_B1 * m + (1.0 - ADAM_B1) * g
    v = ADAM_B2 * v + (1.0 - ADAM_B2) * _jnp.square(g)
    m_hat = m / (1.0 - ADAM_B1 ** ADAM_STEP)
    v_hat = v / (1.0 - ADAM_B2 ** ADAM_STEP)
    delta = -ADAM_LR * (m_hat / (_jnp.sqrt(v_hat) + ADAM_EPS) + ADAM_WD * w)
    return delta, m, v


def reference(x, norm_mix_pre, w_in, gm_ln_w, gm_ln_b, gm_w_s, gm_b_s, conv_w, conv_b, dt_bias, a_log, d_skip, ssm_norm_w, w_out, norm_mix_post, norm_ffn_pre, w_up, w_down, norm_ffn_post, loss_target, m_norm_mix_pre, m_w_in, m_gm_ln_w, m_gm_ln_b, m_gm_w_s, m_gm_b_s, m_conv_w, m_conv_b, m_dt_bias, m_a_log, m_d_skip, m_ssm_norm_w, m_w_out, m_norm_mix_post, m_norm_ffn_pre, m_w_up, m_w_down, m_norm_ffn_post, v_norm_mix_pre, v_w_in, v_gm_ln_w, v_gm_ln_b, v_gm_w_s, v_gm_b_s, v_conv_w, v_conv_b, v_dt_bias, v_a_log, v_d_skip, v_ssm_norm_w, v_w_out, v_norm_mix_post, v_norm_ffn_pre, v_w_up, v_w_down, v_norm_ffn_post):
    given = dict(x=x, norm_mix_pre=norm_mix_pre, w_in=w_in, gm_ln_w=gm_ln_w, gm_ln_b=gm_ln_b, gm_w_s=gm_w_s, gm_b_s=gm_b_s, conv_w=conv_w, conv_b=conv_b, dt_bias=dt_bias, a_log=a_log, d_skip=d_skip, ssm_norm_w=ssm_norm_w, w_out=w_out, norm_mix_post=norm_mix_post, norm_ffn_pre=norm_ffn_pre, w_up=w_up, w_down=w_down, norm_ffn_post=norm_ffn_post, loss_target=loss_target, m_norm_mix_pre=m_norm_mix_pre, m_w_in=m_w_in, m_gm_ln_w=m_gm_ln_w, m_gm_ln_b=m_gm_ln_b, m_gm_w_s=m_gm_w_s, m_gm_b_s=m_gm_b_s, m_conv_w=m_conv_w, m_conv_b=m_conv_b, m_dt_bias=m_dt_bias, m_a_log=m_a_log, m_d_skip=m_d_skip, m_ssm_norm_w=m_ssm_norm_w, m_w_out=m_w_out, m_norm_mix_post=m_norm_mix_post, m_norm_ffn_pre=m_norm_ffn_pre, m_w_up=m_w_up, m_w_down=m_w_down, m_norm_ffn_post=m_norm_ffn_post, v_norm_mix_pre=v_norm_mix_pre, v_w_in=v_w_in, v_gm_ln_w=v_gm_ln_w, v_gm_ln_b=v_gm_ln_b, v_gm_w_s=v_gm_w_s, v_gm_b_s=v_gm_b_s, v_conv_w=v_conv_w, v_conv_b=v_conv_b, v_dt_bias=v_dt_bias, v_a_log=v_a_log, v_d_skip=v_d_skip, v_ssm_norm_w=v_ssm_norm_w, v_w_out=v_w_out, v_norm_mix_post=v_norm_mix_post, v_norm_ffn_pre=v_norm_ffn_pre, v_w_up=v_w_up, v_w_down=v_w_down, v_norm_ffn_post=v_norm_ffn_post)
    weights = {n: given[n] for n in TWIN_WEIGHTS}
    shared = {n: given[n] for n in SHARED_INPUTS}
    per_example = {n: given[n] for n in ['x']}
    grad_fn = _jax.value_and_grad(_loss, argnums=(0, 1))

    def one_microbatch(ex, loss_target):
        ex = dict(ex)
        diff = ex.pop(TWIN_DIFF_INPUT)
        return grad_fn(weights, diff, {**shared, **ex}, loss_target)

    if N_MICROBATCH == 1:
        loss, (grad_w, grad_x) = one_microbatch(per_example, given["loss_target"])
    else:
        def body(carry, xs):
            loss_sum, grad_sum = carry
            l_k, (gw_k, gx_k) = one_microbatch(xs[0], xs[1])
            with _jax.named_scope("update"):
                return (loss_sum + l_k, _jax.tree.map(_jnp.add, grad_sum, gw_k)), gx_k

        init = (_jnp.zeros((), _jnp.float32), _jax.tree.map(_jnp.zeros_like, weights))
        (loss, grad_w), grad_x = _jax.lax.scan(body, init, (per_example, given["loss_target"]))
    with _jax.named_scope("update"):
        delta_w, new_m, new_v = {}, {}, {}
        for n in TWIN_WEIGHTS:
            delta_w[n], new_m[n], new_v[n] = _adamw(weights[n], grad_w[n], given["m_" + n], given["v_" + n])
    return (loss, grad_x, *[grad_w[n] for n in TWIN_WEIGHTS], *[delta_w[n] for n in TWIN_WEIGHTS],
            *[new_m[n] for n in TWIN_WEIGHTS], *[new_v[n] for n in TWIN_WEIGHTS])
```

```python
import functools

import jax
import jax.numpy as jnp
import numpy as np
from jax import lax
from jax.experimental import pallas as pl
from jax.experimental.pallas import tpu as pltpu

F32 = jnp.float32
BF16 = jnp.bfloat16

D_MODEL = 1024
GM_WIDTH = 512
SSM_WIDTH = 512
CONV_CH = 1024
N_HEADS = 8
HEAD_DIM = 64
N_STATE = 128
CHUNK = 128
D_FF = 4096
IN_COLS = 2568
IN_PAD = 2688
N_DEV = 8
EPS = 1e-6
ADAM_LR, ADAM_B1, ADAM_B2, ADAM_EPS, ADAM_WD, ADAM_STEP = 0.001, 0.9, 0.999, 1e-08, 0.01, 10
VMEM_LIMIT_BYTES = 56 * 1024 * 1024
SMALL_ROWS = 16

_NT = (((1,), (1,)), ((), ()))
_TN = (((0,), (0,)), ((), ()))


def _params(*sem):
    return pltpu.CompilerParams(dimension_semantics=sem or None, vmem_limit_bytes=VMEM_LIMIT_BYTES)


def _dot(a, b, dims=None):
    if dims is None:
        return jnp.dot(a, b, preferred_element_type=F32)
    return lax.dot_general(a, b, dims, preferred_element_type=F32)


def _split_terms(x, terms):
    out, rem = [], x
    for i in range(terms):
        hi = rem.astype(BF16)
        out.append(hi)
        if i + 1 < terms:
            rem = rem - hi.astype(F32)
    return out


def _split_dot(x, m, terms):
    acc = None
    for hi in _split_terms(x, terms):
        part = _dot(hi, m)
        acc = part if acc is None else acc + part
    return acc


def _split_dot_left(m, x, terms):
    acc = None
    for hi in _split_terms(x, terms):
        part = _dot(m, hi)
        acc = part if acc is None else acc + part
    return acc


def _gelu_and_grad(x):
    c = 0.7978845608028654
    inner = c * (x + 0.044715 * x * x * x)
    t = jnp.tanh(inner)
    g = 0.5 * x * (1.0 + t)
    dg = 0.5 * (1.0 + t) + 0.5 * x * (1.0 - t * t) * c * (1.0 + 3.0 * 0.044715 * x * x)
    return g, dg


def _softplus(x):
    return jnp.maximum(x, 0.0) + jnp.log(1.0 + jnp.exp(-jnp.abs(x)))


def _rsum(x):
    return jnp.sum(x, axis=0, keepdims=True)


def _acc_rows(ref, part, first):
    val = jnp.broadcast_to(part, ref.shape)

    @pl.when(first)
    def _():
        ref[...] = val

    @pl.when(jnp.logical_not(first))
    def _():
        ref[...] += val


def _rms_bwd(n, g, dout):
    r = lax.rsqrt(jnp.mean(n * n, axis=-1, keepdims=True) + EPS)
    nh = n * r
    dg = dout * g
    dn = r * (dg - nh * jnp.mean(dg * nh, axis=-1, keepdims=True))
    return dn, _rsum(dout * nh)


def _const_mats():
    avg = np.kron(np.eye(N_HEADS), np.full((HEAD_DIM, HEAD_DIM), 1.0 / HEAD_DIM))
    expand = np.zeros((CHUNK, SSM_WIDTH), np.float32)
    for h in range(N_HEADS):
        expand[h, h * HEAD_DIM:(h + 1) * HEAD_DIM] = 1.0
    tril = np.tril(np.ones((CHUNK, CHUNK), np.float32))
    as_bf16 = lambda a: jnp.asarray(a, dtype=BF16)
    return as_bf16(avg), as_bf16(expand), as_bf16(expand.T), as_bf16(tril), as_bf16(tril.T)


def _full(shape):
    nd = len(shape)
    return pl.BlockSpec(shape, lambda *_: (0,) * nd)


def _exchange(srcs, gather, name):
    n = len(srcs)
    out_shape = tuple(
        jax.ShapeDtypeStruct(((N_DEV,) + s.shape) if ga else s.shape, s.dtype) for s, ga in zip(srcs, gather))

    def body(*refs):
        src, dst = refs[:n], refs[n:2 * n]
        send_sems, recv_sems, local_sems = refs[2 * n:]
        x, y, c = lax.axis_index("x"), lax.axis_index("y"), lax.axis_index("c")
        me = 4 * x + 2 * y + c

        def part(t, j):
            return src[t] if gather[t] else src[t].at[j]

        local = [pltpu.make_async_copy(part(t, me), dst[t].at[me], local_sems.at[t]) for t in range(n)]
        for cp in local:
            cp.start()
        sends, recvs = [], []
        for k in range(1, N_DEV):
            px = 1 - x if k & 4 else x
            py = 1 - y if k & 2 else y
            pc = 1 - c if k & 1 else c
            peer = 4 * px + 2 * py + pc
            for t in range(n):
                mk = functools.partial(
                    pltpu.make_async_remote_copy, send_sem=send_sems.at[t, k - 1], recv_sem=recv_sems.at[t, k - 1],
                    device_id=(px, py, pc), device_id_type=pl.DeviceIdType.MESH)
                sends.append(mk(src_ref=part(t, peer), dst_ref=dst[t].at[me]))
                recvs.append(mk(src_ref=part(t, peer), dst_ref=dst[t].at[peer]))
        for cp in sends:
            cp.start()
        for cp in recvs:
            cp.wait_recv()
        for cp in sends:
            cp.wait_send()
        for cp in local:
            cp.wait()

    hbm = pl.BlockSpec(memory_space=pltpu.HBM)
    return pl.pallas_call(
        body, name=name, out_shape=out_shape, in_specs=[hbm] * n, out_specs=tuple([hbm] * n),
        scratch_shapes=[pltpu.SemaphoreType.DMA((n, N_DEV - 1)), pltpu.SemaphoreType.DMA((n, N_DEV - 1)),
                        pltpu.SemaphoreType.DMA((n,))],
    )(*srcs)


def _cast_bf16(w, rows, name):
    def body(w_ref, o_ref):
        o_ref[...] = w_ref[...].astype(BF16)

    r, cdim = w.shape
    return pl.pallas_call(
        body, name=name, grid=(r // rows,), out_shape=jax.ShapeDtypeStruct(w.shape, BF16),
        in_specs=[pl.BlockSpec((rows, cdim), lambda i: (i, 0))], out_specs=pl.BlockSpec((rows, cdim), lambda i: (i, 0)),
        compiler_params=_params("parallel"))(w)


def _adamw_math(w, g, m, v):
    m = ADAM_B1 * m + (1.0 - ADAM_B1) * g
    v = ADAM_B2 * v + (1.0 - ADAM_B2) * (g * g)
    m_hat = m / (1.0 - ADAM_B1 ** ADAM_STEP)
    v_hat = v / (1.0 - ADAM_B2 ** ADAM_STEP)
    delta = -ADAM_LR * (m_hat / (jnp.sqrt(v_hat) + ADAM_EPS) + ADAM_WD * w)
    return delta, m, v


def _adamw_reduce(parts, w, m, v, rows, name):
    r, cdim = w.shape

    def body(p_ref, w_ref, m_ref, v_ref, g_out, d_out, m_out, v_out):
        g = p_ref[0]
        for j in range(1, N_DEV):
            g = g + p_ref[j]
        d, mn, vn = _adamw_math(w_ref[...], g, m_ref[...], v_ref[...])
        g_out[...] = g
        d_out[...] = d
        m_out[...] = mn
        v_out[...] = vn

    blk = pl.BlockSpec((rows, cdim), lambda i: (i, 0))
    sds = jax.ShapeDtypeStruct(w.shape, F32)
    return pl.pallas_call(
        body, name=name, grid=(r // rows,), out_shape=(sds,) * 4,
        in_specs=[pl.BlockSpec((N_DEV, rows, cdim), lambda i: (0, i, 0)), blk, blk, blk], out_specs=(blk,) * 4,
        compiler_params=_params("parallel"))(parts, w, m, v)


def _adamw_small(parts, w, m, v, mask, name):
    nd = w.ndim

    def body(*refs):
        if mask is None:
            p_ref, w_ref, m_ref, v_ref, g_out, d_out, m_out, v_out = refs
        else:
            p_ref, w_ref, m_ref, v_ref, k_ref, g_out, d_out, m_out, v_out = refs
        g = p_ref[0]
        for j in range(1, N_DEV):
            g = g + p_ref[j]
        if mask is not None:
            g = g * k_ref[...]
        d, mn, vn = _adamw_math(w_ref[...], g, m_ref[...], v_ref[...])
        g_out[...] = g
        d_out[...] = d
        m_out[...] = mn
        v_out[...] = vn

    sds = jax.ShapeDtypeStruct(w.shape, F32)
    ins = [parts, w, m, v] + ([] if mask is None else [mask])
    return pl.pallas_call(
        body, name=name, out_shape=(sds,) * 4,
        in_specs=[_full(a.shape) for a in ins], out_specs=(_full(w.shape),) * 4, grid=(1,),
        compiler_params=_params("arbitrary"))(*ins)


_IN_SPLITS = ((0, 512), (512, 1024), (1024, 1536), (1536, 2560), (2560, IN_PAD))


def _in_proj(x, g1, w_in, tm):
    t_tok = x.shape[0]

    def body(x_ref, g_ref, w_ref, h_ref, *outs):
        xv = x_ref[...]
        r = lax.rsqrt(jnp.mean(xv * xv, axis=-1, keepdims=True) + EPS)
        h = (xv * r * g_ref[...]).astype(BF16)
        h_ref[...] = h
        for (a, b), o_ref in zip(_IN_SPLITS, outs):
            o_ref[...] = _dot(h, w_ref[:, a:b])

    row = lambda n: pl.BlockSpec((tm, n), lambda i: (i, 0))
    widths = [b - a for a, b in _IN_SPLITS]
    return pl.pallas_call(
        body, name="in_proj", grid=(t_tok // tm,),
        out_shape=(jax.ShapeDtypeStruct((t_tok, D_MODEL), BF16),) + tuple(
            jax.ShapeDtypeStruct((t_tok, n), F32) for n in widths),
        in_specs=[row(D_MODEL), _full((1, D_MODEL)), _full((D_MODEL, IN_PAD))],
        out_specs=(row(D_MODEL),) + tuple(row(n) for n in widths),
        compiler_params=_params("parallel"))(x, g1, w_in)


def _lane_masks():
    lane = lax.broadcasted_iota(jnp.int32, (1, 2 * HEAD_DIM), 1)
    left = (lane < HEAD_DIM).astype(F32)
    return left, 1.0 - left


def _stack_pair(v, m_l, m_r):
    return jnp.concatenate([v * m_l, v * m_r], axis=0).astype(BF16)


def _gmlp_common(u, v, lnw, lnb, avg, wcat_ref, bias, m_l, m_r):
    ug, dug = _gelu_and_grad(u)
    vg, dvg = _gelu_and_grad(v)
    mu = _split_dot(vg, avg, 2)
    vc = vg - mu
    var = _split_dot(vc * vc, avg, 2)
    rstd = lax.rsqrt(var + EPS)
    vhat = vc * rstd
    vn = vhat * lnw + lnb
    cols = []
    for j in range(N_HEADS // 2):
        cols.append(_dot(wcat_ref[j], _stack_pair(vn[:, 128 * j:128 * (j + 1)], m_l, m_r)))
    mixed = jnp.concatenate(cols, axis=1) + bias
    return ug, dug, dvg, rstd, vhat, vn, mixed


def _gmlp_fwd(u, v, lnw, lnb, wcat, bias, avg):
    t_tok = u.shape[0]

    def body(u_ref, v_ref, lnw_ref, lnb_ref, wcat_ref, bias_ref, avg_ref, o_ref):
        m_l, m_r = _lane_masks()
        ug, _, _, _, _, _, mixed = _gmlp_common(
            u_ref[...], v_ref[...], lnw_ref[...], lnb_ref[...], avg_ref[...], wcat_ref, bias_ref[...], m_l, m_r)
        o_ref[...] = (ug * mixed).astype(BF16)

    row = pl.BlockSpec((CHUNK, GM_WIDTH), lambda i: (i, 0))
    return pl.pallas_call(
        body, name="gmlp_fwd", grid=(t_tok // CHUNK,), out_shape=jax.ShapeDtypeStruct((t_tok, GM_WIDTH), BF16),
        in_specs=[row, row, _full((1, GM_WIDTH)), _full((1, GM_WIDTH)), _full(wcat.shape), _full(bias.shape),
                  _full(avg.shape)],
        out_specs=row, compiler_params=_params("parallel"))(u, v, lnw, lnb, wcat, bias, avg)


def _ssd_common(xext_ref, dtr, cw_ref, cb, dtb, alog, expand, tril):
    q = CHUNK
    taps = [xext_ref[pl.ds(5 + k, q), :] for k in range(4)]
    pre = cb + cw_ref[0:1, :] * taps[0] + cw_ref[1:2, :] * taps[1] + cw_ref[2:3, :] * taps[2] + cw_ref[3:4, :] * taps[3]
    sg = jax.nn.sigmoid(pre)
    act = pre * sg
    lane = lax.broadcasted_iota(jnp.int32, (1, CHUNK), 1)
    a_row = jnp.where(lane < N_HEADS, -jnp.exp(alog), 0.0)
    dtp = dtr + dtb
    dt = _softplus(dtp)
    a_cs = _split_dot_left(tril, dt * a_row, 3)
    a_cs_t = a_cs.T
    dt_exp = _split_dot(dt, expand, 3)
    a_exp = _split_dot(a_cs, expand, 3)
    a_end = a_exp[q - 1:q, :]
    li = lax.broadcasted_iota(jnp.int32, (q, q), 0)
    si = lax.broadcasted_iota(jnp.int32, (q, q), 1)
    causal = si <= li
    decay = []
    for h in range(N_HEADS):
        seg = a_cs[:, h:h + 1] - a_cs_t[h:h + 1, :]
        decay.append(jnp.where(causal, jnp.exp(jnp.minimum(seg, 0.0)), 0.0))
    return dict(taps=taps, pre=pre, sg=sg, act=act, a_row=a_row, dtp=dtp, dt=dt, dt_exp=dt_exp, a_exp=a_exp,
                e=jnp.exp(a_exp), w_end=jnp.exp(a_end - a_exp), cd=jnp.exp(a_end), decay=decay)


def _ssd_specs(t_tok, seq, reverse):
    nc = seq // CHUNK

    def chunk(b, c):
        return b * nc + (nc - 1 - c if reverse else c)

    def row(n):
        return pl.BlockSpec((CHUNK, n), lambda b, c: (chunk(b, c), 0))

    tail = pl.BlockSpec((8, CONV_CH), lambda b, c: (jnp.maximum(chunk(b, c) * (CHUNK // 8) - 1, 0), 0))
    return nc, chunk, row, tail


def _fill_xext(xext_ref, tail_ref, xbc_ref, first_chunk):
    xext_ref[0:8, :] = jnp.where(first_chunk, 0.0, tail_ref[...])
    xext_ref[8:8 + CHUNK, :] = xbc_ref[...]


def _ssd_fwd(z, xbc, dtr, cw, cb, dtb, alog, dskip_exp, nw, expand, tril, seq):
    t_tok = z.shape[0]
    nc, chunk, row, tail = _ssd_specs(t_tok, seq, False)

    def body(z_ref, xbc_ref, tail_ref, dtr_ref, cw_ref, cb_ref, dtb_ref, alog_ref, dsk_ref, nw_ref, exp_ref,
             tril_ref, o_ref, y_ref, st_ref, xext_ref, state_ref):
        c = pl.program_id(1)

        @pl.when(c == 0)
        def _():
            state_ref[...] = jnp.zeros_like(state_ref)

        _fill_xext(xext_ref, tail_ref, xbc_ref, c == 0)
        m_l, m_r = _lane_masks()
        f = _ssd_common(xext_ref, dtr_ref[...], cw_ref, cb_ref[...], dtb_ref[...], alog_ref[...], exp_ref[...],
                        tril_ref[...])
        act = f["act"]
        xs = act[:, :SSM_WIDTH]
        xdt = xs * f["dt_exp"]
        xw = xdt * f["w_end"]
        state = state_ref[...]
        st_ref[0] = state
        ydiag, yoff, snew = [], [], []
        for g in range(2):
            bg = act[:, 512 + 128 * g:640 + 128 * g].astype(BF16)
            cg = act[:, 768 + 128 * g:896 + 128 * g].astype(BF16)
            cb_mat = _dot(cg, bg, _NT)
            for pr in range(2):
                h0 = 4 * g + 2 * pr
                gcat = jnp.concatenate(
                    [(cb_mat * f["decay"][h0]).astype(BF16), (cb_mat * f["decay"][h0 + 1]).astype(BF16)], axis=1)
                ydiag.append(_dot(gcat, _stack_pair(xdt[:, 64 * h0:64 * h0 + 128], m_l, m_r)))
            yoff.append(_dot(cg, state[:, 256 * g:256 * (g + 1)].astype(BF16)))
            snew.append(_dot(bg, xw[:, 256 * g:256 * (g + 1)].astype(BF16), _TN))
        y = jnp.concatenate(ydiag, axis=1) + f["e"] * jnp.concatenate(yoff, axis=1) + dsk_ref[...] * xs
        state_ref[...] = state * f["cd"] + jnp.concatenate(snew, axis=1)
        y_ref[...] = y
        zv = z_ref[...]
        yg = y * (zv * jax.nn.sigmoid(zv))
        outs = []
        for g in range(2):
            ygg = yg[:, 256 * g:256 * (g + 1)]
            outs.append(ygg * lax.rsqrt(jnp.mean(ygg * ygg, axis=-1, keepdims=True) + EPS))
        o_ref[...] = (jnp.concatenate(outs, axis=1) * nw_ref[...]).astype(BF16)

    consts = [cw, cb, dtb, alog, dskip_exp, nw, expand, tril]
    return pl.pallas_call(
        body, name="ssd_fwd", grid=(t_tok // seq, nc),
        out_shape=(jax.ShapeDtypeStruct((t_tok, SSM_WIDTH), BF16), jax.ShapeDtypeStruct((t_tok, SSM_WIDTH), F32),
                   jax.ShapeDtypeStruct((t_tok // CHUNK, N_STATE, SSM_WIDTH), F32)),
        in_specs=[row(SSM_WIDTH), row(CONV_CH), tail, row(CHUNK)] + [_full(a.shape) for a in consts],
        out_specs=(row(SSM_WIDTH), row(SSM_WIDTH),
                   pl.BlockSpec((1, N_STATE, SSM_WIDTH), lambda b, c: (chunk(b, c), 0, 0))),
        scratch_shapes=[pltpu.VMEM((CHUNK + 16, CONV_CH), F32), pltpu.VMEM((N_STATE, SSM_WIDTH), F32)],
        compiler_params=_params("arbitrary", "arbitrary"))(z, xbc, xbc, dtr, *consts)


def _out_proj(mix_a, mix_b, w_out, x, g2, g3, tm):
    t_tok = x.shape[0]

    def body(a_ref, b_ref, w_ref, x_ref, g2_ref, g3_ref, o_ref, x2_ref, h3_ref, mix_ref):
        o = _dot(a_ref[...], w_ref[0:GM_WIDTH, :]) + _dot(b_ref[...], w_ref[GM_WIDTH:, :])
        o_ref[...] = o
        mix_ref[:, 0:GM_WIDTH] = a_ref[...]
        mix_ref[:, GM_WIDTH:] = b_ref[...]
        r2 = lax.rsqrt(jnp.mean(o * o, axis=-1, keepdims=True) + EPS)
        x2 = x_ref[...] + o * r2 * g2_ref[...]
        x2_ref[...] = x2
        r3 = lax.rsqrt(jnp.mean(x2 * x2, axis=-1, keepdims=True) + EPS)
        h3_ref[...] = (x2 * r3 * g3_ref[...]).astype(BF16)

    row = lambda n: pl.BlockSpec((tm, n), lambda i: (i, 0))
    sd = lambda dt: jax.ShapeDtypeStruct((t_tok, D_MODEL), dt)
    return pl.pallas_call(
        body, name="out_proj", grid=(t_tok // tm,), out_shape=(sd(F32), sd(F32), sd(BF16), sd(BF16)),
        in_specs=[row(GM_WIDTH), row(SSM_WIDTH), _full((D_MODEL, D_MODEL)), row(D_MODEL), _full((1, D_MODEL)),
                  _full((1, D_MODEL))],
        out_specs=(row(D_MODEL),) * 4, compiler_params=_params("parallel"))(mix_a, mix_b, w_out, x, g2, g3)


def _mlp_fwd(h3, w_up_blk, w_down, x2, target, g4, tm):
    t_tok = x2.shape[0]
    tf = w_up_blk.shape[2]
    nf = w_up_blk.shape[0]

    def body(h_ref, wu_ref, wd_ref, x2_ref, t_ref, g4_ref, ra_ref, dd_ref, dy_ref, dg4_ref, loss_ref, acc_ref):
        i, j = pl.program_id(0), pl.program_id(1)
        ra = jnp.maximum(_dot(h_ref[...], wu_ref[0]), 0.0).astype(BF16)
        ra_ref[...] = ra
        part = _dot(ra * ra, wd_ref[...])

        @pl.when(j == 0)
        def _():
            acc_ref[...] = part

        @pl.when(j > 0)
        def _():
            acc_ref[...] += part

        @pl.when(j == nf - 1)
        def _():
            dvec = acc_ref[...]
            r4 = lax.rsqrt(jnp.mean(dvec * dvec, axis=-1, keepdims=True) + EPS)
            dn = dvec * r4
            g4 = g4_ref[...]
            err = x2_ref[...] + dn * g4 - t_ref[...]
            dy = err * (1.0 / D_MODEL)
            dy_ref[...] = dy
            dg = dy * g4
            dd_ref[...] = (r4 * (dg - dn * jnp.mean(dg * dn, axis=-1, keepdims=True))).astype(BF16)
            _acc_rows(dg4_ref, _rsum(dy * dn), i == 0)
            tile_loss = 0.5 * jnp.sum(jnp.sum(err * err, axis=-1, keepdims=True), axis=0, keepdims=True) / D_MODEL
            _acc_rows(loss_ref, jnp.broadcast_to(tile_loss, (1, 128)), i == 0)

    row = pl.BlockSpec((tm, D_MODEL), lambda i, j: (i, 0))
    return pl.pallas_call(
        body, name="mlp_fwd", grid=(t_tok // tm, nf),
        out_shape=(jax.ShapeDtypeStruct((t_tok, D_FF), BF16), jax.ShapeDtypeStruct((t_tok, D_MODEL), BF16),
                   jax.ShapeDtypeStruct((t_tok, D_MODEL), F32), jax.ShapeDtypeStruct((8, D_MODEL), F32),
                   jax.ShapeDtypeStruct((8, 128), F32)),
        in_specs=[row, pl.BlockSpec((1, D_MODEL, tf), lambda i, j: (j, 0, 0)),
                  pl.BlockSpec((tf, D_MODEL), lambda i, j: (j, 0)), row, row, _full((1, D_MODEL))],
        out_specs=(pl.BlockSpec((tm, tf), lambda i, j: (i, j)), row, row, _full((8, D_MODEL)), _full((8, 128))),
        scratch_shapes=[pltpu.VMEM((tm, D_MODEL), F32)],
        compiler_params=_params("arbitrary", "arbitrary"))(h3, w_up_blk, w_down, x2, target, g4)


def _mlp_bwd(dd, w_down, ra, w_up_blk, x2, dy, o, g3, g2, tm):
    t_tok = x2.shape[0]
    tf = w_up_blk.shape[2]
    nf = w_up_blk.shape[0]

    def body(dd_ref, wd_ref, ra_ref, wu_ref, x2_ref, dy_ref, o_ref, g3_ref, g2_ref, da_ref, dx2_ref, do_ref, dg3_ref,
             dg2_ref, acc_ref):
        i, j = pl.program_id(0), pl.program_id(1)
        df = _dot(dd_ref[...], wd_ref[...], _NT)
        da = (df * (2.0 * ra_ref[...].astype(F32))).astype(BF16)
        da_ref[...] = da
        part = _dot(da, wu_ref[0], _NT)

        @pl.when(j == 0)
        def _():
            acc_ref[...] = part

        @pl.when(j > 0)
        def _():
            acc_ref[...] += part

        @pl.when(j == nf - 1)
        def _():
            dn3, dg3 = _rms_bwd(x2_ref[...], g3_ref[...], acc_ref[...])
            dx2 = dy_ref[...] + dn3
            dx2_ref[...] = dx2
            do, dg2 = _rms_bwd(o_ref[...], g2_ref[...], dx2)
            do_ref[...] = do.astype(BF16)
            _acc_rows(dg3_ref, dg3, i == 0)
            _acc_rows(dg2_ref, dg2, i == 0)

    row = pl.BlockSpec((tm, D_MODEL), lambda i, j: (i, 0))
    vec = _full((1, D_MODEL))
    acc = _full((8, D_MODEL))
    sd = lambda dt: jax.ShapeDtypeStruct((t_tok, D_MODEL), dt)
    return pl.pallas_call(
        body, name="mlp_bwd", grid=(t_tok // tm, nf),
        out_shape=(jax.ShapeDtypeStruct((t_tok, D_FF), BF16), sd(F32), sd(BF16),
                   jax.ShapeDtypeStruct((8, D_MODEL), F32), jax.ShapeDtypeStruct((8, D_MODEL), F32)),
        in_specs=[row, pl.BlockSpec((tf, D_MODEL), lambda i, j: (j, 0)), pl.BlockSpec((tm, tf), lambda i, j: (i, j)),
                  pl.BlockSpec((1, D_MODEL, tf), lambda i, j: (j, 0, 0)), row, row, row, vec, vec],
        out_specs=(pl.BlockSpec((tm, tf), lambda i, j: (i, j)), row, row, acc, acc),
        scratch_shapes=[pltpu.VMEM((tm, D_MODEL), F32)],
        compiler_params=_params("arbitrary", "arbitrary"))(dd, w_down, ra, w_up_blk, x2, dy, o, g3, g2)


def _wgrad(a, b, out_blocks, bm, bn, bk, square_a, name):
    t_tok, m = a.shape
    n = b.shape[1]
    nk = t_tok // bk

    def body(a_ref, b_ref, o_ref):
        k = pl.program_id(2)
        av = a_ref[...]
        if square_a:
            av = av * av
        part = _dot(av, b_ref[...], _TN)
        tgt = o_ref if out_blocks is None else o_ref.at[0]

        @pl.when(k == 0)
        def _():
            tgt[...] = part

        @pl.when(k > 0)
        def _():
            tgt[...] += part

    if out_blocks is None:
        out_shape = jax.ShapeDtypeStruct((m, n), F32)
        out_spec = pl.BlockSpec((bm, bn), lambda i, j, k: (i, j))
    else:
        assert n // out_blocks == bn
        out_shape = jax.ShapeDtypeStruct((out_blocks, m, bn), F32)
        out_spec = pl.BlockSpec((1, bm, bn), lambda i, j, k: (j, i, 0))
    return pl.pallas_call(
        body, name=name, grid=(m // bm, n // bn, nk), out_shape=out_shape,
        in_specs=[pl.BlockSpec((bk, bm), lambda i, j, k: (k, i)), pl.BlockSpec((bk, bn), lambda i, j, k: (k, j))],
        out_specs=out_spec, compiler_params=_params("parallel", "parallel", "arbitrary"))(a, b)


def _dmix(do, w_out, tm):
    t_tok = do.shape[0]

    def body(d_ref, w_ref, o_ref):
        o_ref[...] = _dot(d_ref[...], w_ref[...], _NT)

    row = pl.BlockSpec((tm, D_MODEL), lambda i: (i, 0))
    return pl.pallas_call(
        body, name="dmix", grid=(t_tok // tm,), out_shape=jax.ShapeDtypeStruct((t_tok, D_MODEL), F32),
        in_specs=[row, _full((D_MODEL, D_MODEL))], out_specs=row, compiler_params=_params("parallel"))(do, w_out)


def _gmlp_bwd(dmix, u, v, lnw, lnb, wcat, wtcat, bias, avg, expand_t):
    t_tok = u.shape[0]

    def body(dm_ref, u_ref, v_ref, lnw_ref, lnb_ref, wcat_ref, wtcat_ref, bias_ref, avg_ref, expt_ref, du_ref, dv_ref,
             dw_ref, db_ref, dlnw_ref, dlnb_ref):
        i = pl.program_id(0)
        m_l, m_r = _lane_masks()
        avg = avg_ref[...]
        lnw = lnw_ref[...]
        ug, dug, dvg, rstd, vhat, vn, mixed = _gmlp_common(
            u_ref[...], v_ref[...], lnw, lnb_ref[...], avg, wcat_ref, bias_ref[...], m_l, m_r)
        dya = dm_ref[...]
        du_ref[...] = (dya * mixed * dug).astype(BF16)
        dmixed = dya * ug
        dvn_cols, dws = [], []
        for j in range(N_HEADS // 2):
            dmp = dmixed[:, 128 * j:128 * (j + 1)]
            dvn_cols.append(_dot(wtcat_ref[j], _stack_pair(dmp, m_l, m_r)))
            vnp = vn[:, 128 * j:128 * (j + 1)].astype(BF16)
            dws.append(_dot((dmp * m_l).astype(BF16), vnp, _NT))
            dws.append(_dot((dmp * m_r).astype(BF16), vnp, _NT))
        dvn = jnp.concatenate(dvn_cols, axis=1)
        dvh = dvn * lnw
        dvgel = rstd * (dvh - _split_dot(dvh, avg, 2) - vhat * _split_dot(dvh * vhat, avg, 2))
        dv_ref[...] = (dvgel * dvg).astype(BF16)
        dbt = _split_dot(dmixed, expt_ref[...], 2)
        first = i == 0

        @pl.when(first)
        def _():
            for h in range(N_HEADS):
                dw_ref[h] = dws[h]
            db_ref[...] = dbt

        @pl.when(jnp.logical_not(first))
        def _():
            for h in range(N_HEADS):
                dw_ref[h] += dws[h]
            db_ref[...] += dbt

        _acc_rows(dlnw_ref, _rsum(dvn * vhat), first)
        _acc_rows(dlnb_ref, _rsum(dvn), first)

    row = pl.BlockSpec((CHUNK, GM_WIDTH), lambda i: (i, 0))
    consts = [lnw, lnb, wcat, wtcat, bias, avg, expand_t]
    return pl.pallas_call(
        body, name="gmlp_bwd", grid=(t_tok // CHUNK,),
        out_shape=(jax.ShapeDtypeStruct((t_tok, GM_WIDTH), BF16), jax.ShapeDtypeStruct((t_tok, GM_WIDTH), BF16),
                   jax.ShapeDtypeStruct((N_HEADS, CHUNK, CHUNK), F32), jax.ShapeDtypeStruct((CHUNK, CHUNK), F32),
                   jax.ShapeDtypeStruct((8, GM_WIDTH), F32), jax.ShapeDtypeStruct((8, GM_WIDTH), F32)),
        in_specs=[pl.BlockSpec((CHUNK, GM_WIDTH), lambda i: (i, 0)), row, row] + [_full(a.shape) for a in consts],
        out_specs=(row, row, _full((N_HEADS, CHUNK, CHUNK)), _full((CHUNK, CHUNK)), _full((8, GM_WIDTH)),
                   _full((8, GM_WIDTH))),
        compiler_params=_params("arbitrary"))(dmix, u, v, *consts)


def _ssd_bwd(dmix, z, xbc, dtr, y, states, cw, cb, dtb, alog, dskip_exp, nw, expand, expand_t, tril, triu, seq):
    t_tok = z.shape[0]
    nc, chunk, row, tail = _ssd_specs(t_tok, seq, True)
    q = CHUNK

    def body(dm_ref, z_ref, xbc_ref, tail_ref, dtr_ref, y_ref, st_ref, cw_ref, cb_ref, dtb_ref, alog_ref, dsk_ref,
             nw_ref, exp_ref, expt_ref, tril_ref, triu_ref, dz_ref, dxbc_ref, ddt_ref, dcw_ref, dcb_ref, ddtb_ref,
             dalog_ref, dd_ref, dnw_ref, xext_ref, dext_ref, dstate_ref):
        b, c = pl.program_id(0), pl.program_id(1)
        first = jnp.logical_and(b == 0, c == 0)

        @pl.when(c == 0)
        def _():
            dstate_ref[...] = jnp.zeros_like(dstate_ref)
            dext_ref[q:q + 8, :] = jnp.zeros((8, CONV_CH), F32)

        _fill_xext(xext_ref, tail_ref, xbc_ref, c == nc - 1)
        m_l, m_r = _lane_masks()
        expt = expt_ref[...]
        f = _ssd_common(xext_ref, dtr_ref[...], cw_ref, cb_ref[...], dtb_ref[...], alog_ref[...], exp_ref[...],
                        tril_ref[...])
        act, pre, sg = f["act"], f["pre"], f["sg"]
        xs = act[:, :SSM_WIDTH]
        xdt = xs * f["dt_exp"]
        xw = xdt * f["w_end"]
        state = st_ref[0]
        dstate = dstate_ref[...]
        zv, yv, dout, nw = z_ref[...], y_ref[...], dm_ref[...], nw_ref[...]
        sz = jax.nn.sigmoid(zv)
        sl = zv * sz
        yg = yv * sl
        tv = dout * nw
        dyg_parts, ygh_parts = [], []
        for g in range(2):
            ygg = yg[:, 256 * g:256 * (g + 1)]
            rr = lax.rsqrt(jnp.mean(ygg * ygg, axis=-1, keepdims=True) + EPS)
            ygh = ygg * rr
            tg = tv[:, 256 * g:256 * (g + 1)]
            dyg_parts.append(rr * (tg - ygh * jnp.mean(tg * ygh, axis=-1, keepdims=True)))
            ygh_parts.append(ygh)
        dyg = jnp.concatenate(dyg_parts, axis=1)
        dnw = _rsum(dout * jnp.concatenate(ygh_parts, axis=1))
        dy = dyg * sl
        dz_ref[...] = (dyg * yv * (sz * (1.0 + zv * (1.0 - sz)))).astype(BF16)
        ddsk = _rsum(dy * xs)
        dye = dy * f["e"]
        lane = lax.broadcasted_iota(jnp.int32, (q, q), 1)
        sub = lax.broadcasted_iota(jnp.int32, (q, q), 0)
        rs_mat = jnp.zeros((q, q), F32)
        cs_mat = jnp.zeros((q, q), F32)
        dxdt_cols, yoff, dst_in, dxw, d_b, d_c = [], [], [], [], [], []
        for g in range(2):
            bg = act[:, 512 + 128 * g:640 + 128 * g].astype(BF16)
            cg = act[:, 768 + 128 * g:896 + 128 * g].astype(BF16)
            cb_mat = _dot(cg, bg, _NT)
            stg = state[:, 256 * g:256 * (g + 1)].astype(BF16)
            dyeg = dye[:, 256 * g:256 * (g + 1)].astype(BF16)
            yoff.append(_dot(cg, stg))
            dcg = _dot(dyeg, stg, _NT)
            dst_in.append(_dot(cg, dyeg, _TN))
            dcb = jnp.zeros((q, q), F32)
            for pr in range(2):
                h0 = 4 * g + 2 * pr
                gf = [cb_mat * f["decay"][h0], cb_mat * f["decay"][h0 + 1]]
                gcat = jnp.concatenate([gf[0].astype(BF16), gf[1].astype(BF16)], axis=1)
                xst = _stack_pair(xdt[:, 64 * h0:64 * h0 + 128], m_l, m_r)
                dyp = dy[:, 64 * h0:64 * h0 + 128].astype(BF16)
                dgcat = _dot(dyp, xst, _NT)
                dxst = _dot(gcat, dyp, _TN)
                dxdt_cols.append(dxst[:q] * m_l + dxst[q:] * m_r)
                for i in range(2):
                    h = h0 + i
                    dg = dgcat[:, q * i:q * (i + 1)]
                    mm = dg * gf[i]
                    rs_mat = rs_mat + jnp.where(lane == h, jnp.sum(mm, axis=1, keepdims=True), 0.0)
                    cs_mat = cs_mat + jnp.where(sub == h, jnp.sum(mm, axis=0, keepdims=True), 0.0)
                    dcb = dcb + dg * f["decay"][h]
            dcb16 = dcb.astype(BF16)
            dstg = dstate[:, 256 * g:256 * (g + 1)].astype(BF16)
            d_c.append(dcg + _dot(dcb16, bg))
            dxw.append(_dot(bg, dstg))
            d_b.append(_dot(dcb16, cg, _TN) + _dot(xw[:, 256 * g:256 * (g + 1)].astype(BF16), dstg, _NT))
        dxw = jnp.concatenate(dxw, axis=1)
        dxdt = jnp.concatenate(dxdt_cols, axis=1) + dxw * f["w_end"]
        qv = dxw * xw
        end_row = _rsum(qv) + _rsum(dstate * state) * f["cd"]
        x2 = dye * jnp.concatenate(yoff, axis=1) - qv
        row_i = lax.broadcasted_iota(jnp.int32, (q, 1), 0)
        x2 = x2 + jnp.where(row_i == q - 1, end_row, 0.0)
        da_cs = _split_dot(x2, expt, 3) + rs_mat - cs_mat.T
        ddt = _split_dot(dxdt * xs, expt, 3)
        dxs = dsk_ref[...] * dy + dxdt * f["dt_exp"]
        dda = _split_dot_left(triu_ref[...], da_cs, 3)
        ddt = ddt + dda * f["a_row"]
        dalog = _rsum(dda * f["dt"]) * f["a_row"]
        draw = ddt * jax.nn.sigmoid(f["dtp"])
        ddt_ref[...] = draw.astype(BF16)
        dact = jnp.concatenate([dxs] + d_b + d_c, axis=1)
        dpre = dact * (sg * (1.0 + pre * (1.0 - sg)))
        dext_ref[0:q, :] = dpre
        dxbc = cw_ref[0:1, :] * dext_ref[pl.ds(3, q), :]
        for k in range(1, 4):
            dxbc = dxbc + cw_ref[k:k + 1, :] * dext_ref[pl.ds(3 - k, q), :]
        dxbc_ref[...] = dxbc.astype(BF16)
        dext_ref[q:q + 8, :] = dpre[0:8, :]
        dstate_ref[...] = dstate * f["cd"] + jnp.concatenate(dst_in, axis=1)
        row8 = lax.broadcasted_iota(jnp.int32, (8, 1), 0)
        dcw = jnp.zeros((8, CONV_CH), F32)
        for k in range(4):
            dcw = dcw + jnp.where(row8 == k, _rsum(dpre * f["taps"][k]), 0.0)

        @pl.when(first)
        def _():
            dcw_ref[...] = dcw

        @pl.when(jnp.logical_not(first))
        def _():
            dcw_ref[...] += dcw

        _acc_rows(dcb_ref, _rsum(dpre), first)
        _acc_rows(ddtb_ref, _rsum(draw), first)
        _acc_rows(dalog_ref, dalog, first)
        _acc_rows(dd_ref, ddsk, first)
        _acc_rows(dnw_ref, dnw, first)

    consts = [cw, cb, dtb, alog, dskip_exp, nw, expand, expand_t, tril, triu]
    acc = lambda n: jax.ShapeDtypeStruct((8, n), F32)
    return pl.pallas_call(
        body, name="ssd_bwd", grid=(t_tok // seq, nc),
        out_shape=(jax.ShapeDtypeStruct((t_tok, SSM_WIDTH), BF16), jax.ShapeDtypeStruct((t_tok, CONV_CH), BF16),
                   jax.ShapeDtypeStruct((t_tok, CHUNK), BF16), acc(CONV_CH), acc(CONV_CH), acc(CHUNK), acc(CHUNK),
                   acc(SSM_WIDTH), acc(SSM_WIDTH)),
        in_specs=[pl.BlockSpec((CHUNK, SSM_WIDTH), lambda b, c: (chunk(b, c), 1)), row(SSM_WIDTH), row(CONV_CH), tail,
                  row(CHUNK), row(SSM_WIDTH), pl.BlockSpec((1, N_STATE, SSM_WIDTH), lambda b, c: (chunk(b, c), 0, 0))]
        + [_full(a.shape) for a in consts],
        out_specs=(row(SSM_WIDTH), row(CONV_CH), row(CHUNK), _full((8, CONV_CH)), _full((8, CONV_CH)),
                   _full((8, CHUNK)), _full((8, CHUNK)), _full((8, SSM_WIDTH)), _full((8, SSM_WIDTH))),
        scratch_shapes=[pltpu.VMEM((CHUNK + 16, CONV_CH), F32), pltpu.VMEM((CHUNK + 8, CONV_CH), F32),
                        pltpu.VMEM((N_STATE, SSM_WIDTH), F32)],
        compiler_params=_params("arbitrary", "arbitrary"))(dmix, z, xbc, xbc, dtr, y, states, *consts)


def _in_bwd(du, dv, dz, dxbc, ddt, w_in, x, dx2, g1, tm):
    t_tok = x.shape[0]

    def body(du_ref, dv_ref, dz_ref, dxbc_ref, ddt_ref, w_ref, x_ref, dx2_ref, g_ref, dp_ref, gx_ref, dg_ref):
        i = pl.program_id(0)
        dh = None
        for (a, b), ref in zip(_IN_SPLITS, (du_ref, dv_ref, dz_ref, dxbc_ref, ddt_ref)):
            piece = ref[...]
            dp_ref[:, a:b] = piece
            part = _dot(piece, w_ref[:, a:b], _NT)
            dh = part if dh is None else dh + part
        dn, dg = _rms_bwd(x_ref[...], g_ref[...], dh)
        gx_ref[...] = dx2_ref[...] + dn
        _acc_rows(dg_ref, dg, i == 0)

    row = lambda n: pl.BlockSpec((tm, n), lambda i: (i, 0))
    widths = [b - a for a, b in _IN_SPLITS]
    return pl.pallas_call(
        body, name="in_bwd", grid=(t_tok // tm,),
        out_shape=(jax.ShapeDtypeStruct((t_tok, IN_PAD), BF16), jax.ShapeDtypeStruct((t_tok, D_MODEL), F32),
                   jax.ShapeDtypeStruct((8, D_MODEL), F32)),
        in_specs=[row(n) for n in widths] + [_full((D_MODEL, IN_PAD)), row(D_MODEL), row(D_MODEL), _full((1, D_MODEL))],
        out_specs=(row(IN_PAD), row(D_MODEL), _full((8, D_MODEL))),
        compiler_params=_params("arbitrary"))(du, dv, dz, dxbc, ddt, w_in, x, dx2, g1)


def _pad_lanes(a, n):
    return jnp.pad(a, ((0, 0), (0, n - a.shape[1])))


def _local_step(x, target, seq, w_in_p, w_out, w_up_blk, w_down, conv_w, small):
    t_tok = x.shape[0]
    tm = min(512, t_tok)
    avg, expand, expand_t, tril, triu = _const_mats()
    g1, g2, g3, g4 = (small[k].reshape(1, D_MODEL) for k in
                      ("norm_mix_pre", "norm_mix_post", "norm_ffn_pre", "norm_ffn_post"))
    lnw = small["gm_ln_w"].reshape(1, GM_WIDTH)
    lnb = small["gm_ln_b"].reshape(1, GM_WIDTH)
    causal = jnp.tril(jnp.ones((CHUNK, CHUNK), F32))
    wm = small["gm_w_s"] * causal
    pair = lambda w: w.reshape(4, 2, CHUNK, CHUNK).transpose(0, 2, 1, 3).reshape(4, CHUNK, 2 * CHUNK).astype(BF16)
    wcat = pair(wm)
    wtcat = pair(jnp.swapaxes(wm, 1, 2))
    bias = jnp.repeat(small["gm_b_s"].T, HEAD_DIM, axis=1)
    cb = small["conv_b"].reshape(1, CONV_CH)
    dtb = _pad_lanes(small["dt_bias"].reshape(1, N_HEADS), CHUNK)
    alog = _pad_lanes(small["a_log"].reshape(1, N_HEADS), CHUNK)
    dskip_exp = jnp.repeat(small["d_skip"].reshape(1, N_HEADS), HEAD_DIM, axis=1)
    nw = small["ssm_norm_w"].reshape(1, SSM_WIDTH)

    h1, u, v, z, xbc, dtr = _in_proj(x, g1, w_in_p, tm)
    mix_a = _gmlp_fwd(u, v, lnw, lnb, wcat, bias, avg)
    mix_b, y_pre, states = _ssd_fwd(z, xbc, dtr, conv_w, cb, dtb, alog, dskip_exp, nw, expand, tril, seq)
    o, x2, h3, mix = _out_proj(mix_a, mix_b, w_out, x, g2, g3, tm)
    ra, dd, dy, dg4, loss = _mlp_fwd(h3, w_up_blk, w_down, x2, target, g4, tm)

    da, dx2, do, dg3, dg2 = _mlp_bwd(dd, w_down, ra, w_up_blk, x2, dy, o, g3, g2, tm)
    bk = min(512, t_tok)
    g_w_down = _wgrad(ra, dd, None, 1024, D_MODEL, bk, True, "wgrad_down")
    g_w_up = _wgrad(h3, da, N_DEV, D_MODEL, D_FF // N_DEV, bk, False, "wgrad_up")
    g_w_out = _wgrad(mix, do, None, D_MODEL, D_MODEL, bk, False, "wgrad_out")
    dmix = _dmix(do, w_out, tm)
    du, dv, dws, dbt, dlnw, dlnb = _gmlp_bwd(dmix, u, v, lnw, lnb, wcat, wtcat, bias, avg, expand_t)
    dz, dxbc, ddt, dcw, dcb, ddtb, dalog, ddsk, dnw = _ssd_bwd(
        dmix, z, xbc, dtr, y_pre, states, conv_w, cb, dtb, alog, dskip_exp, nw, expand, expand_t, tril, triu, seq)
    dproj, grad_x, dg1 = _in_bwd(du, dv, dz, dxbc, ddt, w_in_p, x, dx2, g1, tm)
    g_w_in = _wgrad(h1, dproj, None, 512, IN_PAD, bk, False, "wgrad_in")

    grads = dict(
        w_in=g_w_in, w_out=g_w_out, w_up=g_w_up, w_down=g_w_down, conv_w=dcw[0:4],
        norm_mix_pre=dg1[0:1], norm_mix_post=dg2[0:1], norm_ffn_pre=dg3[0:1], norm_ffn_post=dg4[0:1],
        gm_ln_w=dlnw[0:1], gm_ln_b=dlnb[0:1], gm_w_s=dws, gm_b_s=dbt.T[0:N_HEADS], conv_b=dcb[0:1],
        dt_bias=ddtb[0:1, 0:N_HEADS], a_log=dalog[0:1, 0:N_HEADS],
        d_skip=ddsk[0:1].reshape(N_HEADS, HEAD_DIM).sum(axis=1).reshape(1, N_HEADS), ssm_norm_w=dnw[0:1])
    return loss[0, 0], grad_x, grads


_SMALL_ROW_PARAMS = ("norm_mix_pre", "norm_mix_post", "norm_ffn_pre", "norm_ffn_post", "gm_ln_w", "gm_ln_b", "gm_b_s",
                     "conv_b", "dt_bias", "a_log", "d_skip", "ssm_norm_w")
_WEIGHTS = ("norm_mix_pre", "w_in", "gm_ln_w", "gm_ln_b", "gm_w_s", "gm_b_s", "conv_w", "conv_b", "dt_bias", "a_log",
            "d_skip", "ssm_norm_w", "w_out", "norm_mix_post", "norm_ffn_pre", "w_up", "w_down", "norm_ffn_post")


def _pack_rows(tensors):
    rows = [_pad_lanes(t.reshape(1, -1), D_MODEL) for t in tensors]
    rows.append(jnp.zeros((SMALL_ROWS - len(rows), D_MODEL), F32))
    return jnp.concatenate(rows, axis=0)


def kernel(x, norm_mix_pre, w_in, gm_ln_w, gm_ln_b, gm_w_s, gm_b_s, conv_w, conv_b, dt_bias, a_log, d_skip, ssm_norm_w, w_out, norm_mix_post, norm_ffn_pre, w_up, w_down, norm_ffn_post, loss_target, m_norm_mix_pre, m_w_in, m_gm_ln_w, m_gm_ln_b, m_gm_w_s, m_gm_b_s, m_conv_w, m_conv_b, m_dt_bias, m_a_log, m_d_skip, m_ssm_norm_w, m_w_out, m_norm_mix_post, m_norm_ffn_pre, m_w_up, m_w_down, m_norm_ffn_post, v_norm_mix_pre, v_w_in, v_gm_ln_w, v_gm_ln_b, v_gm_w_s, v_gm_b_s, v_conv_w, v_conv_b, v_dt_bias, v_a_log, v_d_skip, v_ssm_norm_w, v_w_out, v_norm_mix_post, v_norm_ffn_pre, v_w_up, v_w_down, v_norm_ffn_post):
    w = dict(norm_mix_pre=norm_mix_pre, w_in=w_in, gm_ln_w=gm_ln_w, gm_ln_b=gm_ln_b, gm_w_s=gm_w_s, gm_b_s=gm_b_s, conv_w=conv_w, conv_b=conv_b, dt_bias=dt_bias, a_log=a_log, d_skip=d_skip, ssm_norm_w=ssm_norm_w, w_out=w_out, norm_mix_post=norm_mix_post, norm_ffn_pre=norm_ffn_pre, w_up=w_up, w_down=w_down, norm_ffn_post=norm_ffn_post)
    m = dict(norm_mix_pre=m_norm_mix_pre, w_in=m_w_in, gm_ln_w=m_gm_ln_w, gm_ln_b=m_gm_ln_b, gm_w_s=m_gm_w_s, gm_b_s=m_gm_b_s, conv_w=m_conv_w, conv_b=m_conv_b, dt_bias=m_dt_bias, a_log=m_a_log, d_skip=m_d_skip, ssm_norm_w=m_ssm_norm_w, w_out=m_w_out, norm_mix_post=m_norm_mix_post, norm_ffn_pre=m_norm_ffn_pre, w_up=m_w_up, w_down=m_w_down, norm_ffn_post=m_norm_ffn_post)
    v = dict(norm_mix_pre=v_norm_mix_pre, w_in=v_w_in, gm_ln_w=v_gm_ln_w, gm_ln_b=v_gm_ln_b, gm_w_s=v_gm_w_s, gm_b_s=v_gm_b_s, conv_w=v_conv_w, conv_b=v_conv_b, dt_bias=v_dt_bias, a_log=v_a_log, d_skip=v_d_skip, ssm_norm_w=v_ssm_norm_w, w_out=v_w_out, norm_mix_post=v_norm_mix_post, norm_ffn_pre=v_norm_ffn_pre, w_up=v_w_up, w_down=v_w_down, norm_ffn_post=v_norm_ffn_post)
    n_batch, seq, _ = x.shape
    shard_in = IN_COLS // N_DEV

    w_in16 = _cast_bf16(w_in[0], 256, "cast_w_in")
    w_out16 = _cast_bf16(w_out[0], 128, "cast_w_out")
    w_up16 = _cast_bf16(w_up[0], 256, "cast_w_up")
    w_down16 = _cast_bf16(w_down[0], 256, "cast_w_down")
    ag_in, ag_out, ag_up, ag_down, ag_conv = _exchange(
        [w_in16, w_out16, w_up16, w_down16, conv_w[0]], [True] * 5, "gather_weights")
    w_in_p = _pad_lanes(ag_in.transpose(1, 0, 2).reshape(D_MODEL, IN_COLS), IN_PAD)
    w_out_f = ag_out.reshape(D_MODEL, D_MODEL)
    w_down_f = ag_down.reshape(D_FF, D_MODEL)
    conv_w_f = ag_conv.transpose(1, 0, 2).reshape(4, CONV_CH)

    small = {k: w[k][0] for k in _SMALL_ROW_PARAMS + ("gm_w_s",)}
    loss_part, grad_x, g = _local_step(
        x.reshape(n_batch * seq, D_MODEL), loss_target.reshape(n_batch * seq, D_MODEL), seq, w_in_p, w_out_f, ag_up,
        w_down_f, conv_w_f, small)
    loss = lax.psum(loss_part, ("x", "y", "c"))

    g_in_blk = g["w_in"][:, :IN_COLS].reshape(D_MODEL, N_DEV, shard_in).transpose(1, 0, 2)
    g_conv_blk = g["conv_w"].reshape(4, N_DEV, CONV_CH // N_DEV).transpose(1, 0, 2)
    p_in, p_out, p_up, p_down, p_conv = _exchange(
        [g_in_blk, g["w_out"].reshape(N_DEV, D_MODEL // N_DEV, D_MODEL), g["w_up"],
         g["w_down"].reshape(N_DEV, D_FF // N_DEV, D_MODEL), g_conv_blk], [False] * 5, "scatter_grads")
    p_rows, p_ws = _exchange([_pack_rows([g[k] for k in _SMALL_ROW_PARAMS]), g["gm_w_s"]], [True, True],
                             "gather_small_grads")

    res = {}
    res["w_in"] = _adamw_reduce(p_in, w_in[0], m_w_in[0], v_w_in[0], 256, "adamw_w_in")
    res["w_out"] = _adamw_reduce(p_out, w_out[0], m_w_out[0], v_w_out[0], 128, "adamw_w_out")
    res["w_up"] = _adamw_reduce(p_up, w_up[0], m_w_up[0], v_w_up[0], 256, "adamw_w_up")
    res["w_down"] = _adamw_reduce(p_down, w_down[0], m_w_down[0], v_w_down[0], 128, "adamw_w_down")
    res["conv_w"] = _adamw_small(p_conv, conv_w[0], m_conv_w[0], v_conv_w[0], None, "adamw_conv_w")
    causal = jnp.tril(jnp.ones((1, CHUNK, CHUNK), F32))
    res["gm_w_s"] = _adamw_small(p_ws, gm_w_s[0], m_gm_w_s[0], v_gm_w_s[0], causal, "adamw_gm_w_s")
    rows = _adamw_small(p_rows, _pack_rows([w[k] for k in _SMALL_ROW_PARAMS]),
                        _pack_rows([m[k] for k in _SMALL_ROW_PARAMS]), _pack_rows([v[k] for k in _SMALL_ROW_PARAMS]),
                        None, "adamw_rows")
    for i, k in enumerate(_SMALL_ROW_PARAMS):
        size = int(np.prod(w[k].shape))
        res[k] = tuple(r[i, :size].reshape(w[k].shape) for r in rows)
    for k in ("w_in", "w_out", "w_up", "w_down", "conv_w", "gm_w_s"):
        res[k] = tuple(r.reshape(w[k].shape) for r in res[k])

    outs = [loss, grad_x.reshape(x.shape)]
    for part in range(4):
        outs.extend(res[k][part] for k in _WEIGHTS)
    return tuple(outs)
```

```python
import functools

import jax
import jax.numpy as jnp
import numpy as np
from jax import lax
from jax.experimental import pallas as pl
from jax.experimental.pallas import tpu as pltpu

F32 = jnp.float32
BF16 = jnp.bfloat16

D_MODEL = 1024
GM_WIDTH = 512
SSM_WIDTH = 512
CONV_CH = 1024
N_HEADS = 8
HEAD_DIM = 64
N_STATE = 128
CHUNK = 128
D_FF = 4096
IN_COLS = 2568
IN_PAD = 2688
N_DEV = 8
EPS = 1e-6
ADAM_LR, ADAM_B1, ADAM_B2, ADAM_EPS, ADAM_WD, ADAM_STEP = 0.001, 0.9, 0.999, 1e-08, 0.01, 10
VMEM_LIMIT_BYTES = 56 * 1024 * 1024
SMALL_ROWS = 16

_NT = (((1,), (1,)), ((), ()))
_TN = (((0,), (0,)), ((), ()))


def _params(*sem):
    return pltpu.CompilerParams(dimension_semantics=sem or None, vmem_limit_bytes=VMEM_LIMIT_BYTES)


def _dot(a, b, dims=None):
    if dims is None:
        return jnp.dot(a, b, preferred_element_type=F32)
    return lax.dot_general(a, b, dims, preferred_element_type=F32)


def _split_terms(x, terms):
    out, rem = [], x
    for i in range(terms):
        hi = rem.astype(BF16)
        out.append(hi)
        if i + 1 < terms:
            rem = rem - hi.astype(F32)
    return out


def _split_dot(x, m, terms):
    acc = None
    for hi in _split_terms(x, terms):
        part = _dot(hi, m)
        acc = part if acc is None else acc + part
    return acc


def _split_dot_left(m, x, terms):
    acc = None
    for hi in _split_terms(x, terms):
        part = _dot(m, hi)
        acc = part if acc is None else acc + part
    return acc


def _gelu_and_grad(x):
    c = 0.7978845608028654
    inner = c * (x + 0.044715 * x * x * x)
    t = jnp.tanh(inner)
    g = 0.5 * x * (1.0 + t)
    dg = 0.5 * (1.0 + t) + 0.5 * x * (1.0 - t * t) * c * (1.0 + 3.0 * 0.044715 * x * x)
    return g, dg


def _softplus(x):
    return jnp.maximum(x, 0.0) + jnp.log(1.0 + jnp.exp(-jnp.abs(x)))


def _rsum(x):
    return jnp.sum(x, axis=0, keepdims=True)


def _acc_rows(ref, part, first):
    val = jnp.broadcast_to(part, ref.shape)

    @pl.when(first)
    def _():
        ref[...] = val

    @pl.when(jnp.logical_not(first))
    def _():
        ref[...] += val


def _rms_bwd(n, g, dout):
    r = lax.rsqrt(jnp.mean(n * n, axis=-1, keepdims=True) + EPS)
    nh = n * r
    dg = dout * g
    dn = r * (dg - nh * jnp.mean(dg * nh, axis=-1, keepdims=True))
    return dn, _rsum(dout * nh)


def _const_mats():
    avg = np.kron(np.eye(N_HEADS), np.full((HEAD_DIM, HEAD_DIM), 1.0 / HEAD_DIM))
    expand = np.zeros((CHUNK, SSM_WIDTH), np.float32)
    for h in range(N_HEADS):
        expand[h, h * HEAD_DIM:(h + 1) * HEAD_DIM] = 1.0
    tril = np.tril(np.ones((CHUNK, CHUNK), np.float32))
    as_bf16 = lambda a: jnp.asarray(a, dtype=BF16)
    return as_bf16(avg), as_bf16(expand), as_bf16(expand.T), as_bf16(tril), as_bf16(tril.T)


def _full(shape):
    nd = len(shape)
    return pl.BlockSpec(shape, lambda *_: (0,) * nd)


_HBM = pl.BlockSpec(memory_space=pltpu.HBM)
_SEM = pl.BlockSpec(memory_space=pltpu.SEMAPHORE)
_ALL_PEERS = tuple(range(1, N_DEV))


def _peer_of(k):
    x, y, c = lax.axis_index("x"), lax.axis_index("y"), lax.axis_index("c")
    px = 1 - x if k & 4 else x
    py = 1 - y if k & 2 else y
    pc = 1 - c if k & 1 else c
    return (px, py, pc), 4 * px + 2 * py + pc


def _copies(src, land, send_sems, recv_sems, peers):
    x, y, c = lax.axis_index("x"), lax.axis_index("y"), lax.axis_index("c")
    me = 4 * x + 2 * y + c
    out = []
    for t in range(len(src)):
        for i, k in enumerate(peers):
            pos, peer = _peer_of(k)
            sem = t * len(peers) + i
            mk = functools.partial(pltpu.make_async_remote_copy, send_sem=send_sems.at[sem], recv_sem=recv_sems.at[sem],
                                   device_id=pos, device_id_type=pl.DeviceIdType.MESH)
            if land[t] is None:
                mine = functools.partial(mk, src_ref=src[t].at[me], dst_ref=src[t].at[me])
                theirs = functools.partial(mk, src_ref=src[t].at[peer], dst_ref=src[t].at[peer])
            else:
                mine = functools.partial(mk, src_ref=src[t].at[peer], dst_ref=land[t].at[me])
                theirs = functools.partial(mk, src_ref=src[t].at[peer], dst_ref=land[t].at[peer])
            out.append((mine, theirs))
    return out


def _exchange_start(srcs, inplace, peers, name, dep=None):
    n = len(srcs)
    lands = [None if ip else pltpu.with_memory_space_constraint(lax.empty(s.shape, s.dtype), pltpu.HBM)
             for s, ip in zip(srcs, inplace)]
    real_lands = [l for l in lands if l is not None]
    n_l = len(real_lands)
    deps = [] if dep is None else [dep]

    def body(*refs):
        src = refs[:n]
        land_refs = list(refs[n:n + n_l])
        send_sems, recv_sems = refs[n + n_l + len(deps)], refs[n + n_l + len(deps) + 1]
        token = refs[-1]
        land = [None if ip else land_refs.pop(0) for ip in inplace]
        for mine, _ in _copies(src, land, send_sems, recv_sems, peers):
            mine().start()
        token[...] = jnp.zeros_like(token)

    sem_t = pltpu.SemaphoreType.DMA((n * len(peers),))
    outs = pl.pallas_call(
        body, name=name,
        out_shape=(sem_t, sem_t) + tuple(pltpu.HBM(a.shape, a.dtype) for a in list(srcs) + real_lands)
        + (jax.ShapeDtypeStruct((8, 128), F32),),
        in_specs=[_HBM] * (n + n_l) + [pl.BlockSpec(memory_space=pl.ANY)] * len(deps),
        out_specs=(_SEM, _SEM) + (_HBM,) * (n + n_l) + (pl.BlockSpec(memory_space=pltpu.VMEM),),
        input_output_aliases={i: 2 + i for i in range(n + n_l)},
        compiler_params=pltpu.CompilerParams(has_side_effects=pltpu.SideEffectType.DATAFLOW_SIDE_EFFECTING),
    )(*[pltpu.with_memory_space_constraint(s, pltpu.HBM) for s in srcs], *real_lands, *deps)
    handle = dict(send=outs[0], recv=outs[1], srcs=outs[2:2 + n], lands=outs[2 + n:2 + n + n_l], inplace=inplace,
                  peers=peers)
    return handle, outs[-1]


def _exchange_wait(handle, after, name):
    srcs, lands, inplace, peers = handle["srcs"], handle["lands"], handle["inplace"], handle["peers"]
    n, n_l = len(srcs), len(lands)

    def body(*refs):
        src = refs[:n]
        land_refs = list(refs[n:n + n_l])
        send_sems, recv_sems = refs[n + n_l], refs[n + n_l + 1]
        land = [None if ip else land_refs.pop(0) for ip in inplace]
        for mine, theirs in _copies(src, land, send_sems, recv_sems, peers):
            mine().wait_send()
            theirs().wait_recv()

    outs = pl.pallas_call(
        body, name=name, out_shape=tuple(pltpu.HBM(a.shape, a.dtype) for a in list(srcs) + list(lands)),
        in_specs=[_HBM] * (n + n_l) + [_SEM, _SEM, pl.BlockSpec(memory_space=pl.ANY)],
        out_specs=(_HBM,) * (n + n_l), input_output_aliases={i: i for i in range(n + n_l)},
        compiler_params=pltpu.CompilerParams(has_side_effects=pltpu.SideEffectType.DATAFLOW_SIDE_EFFECTING),
    )(*srcs, *lands, handle["send"], handle["recv"], after)
    res, land_out = [], list(outs[n:])
    for t in range(n):
        res.append((outs[t], outs[t] if inplace[t] else land_out.pop(0)))
    return res


def _cast_to_slot(w, me, rows, name):
    r, cdim = w.shape

    def body(me_ref, w_ref, o_ref):
        o_ref[0] = w_ref[...].astype(BF16)

    return pl.pallas_call(
        body, name=name, out_shape=jax.ShapeDtypeStruct((N_DEV, r, cdim), BF16),
        grid_spec=pltpu.PrefetchScalarGridSpec(
            num_scalar_prefetch=1, grid=(r // rows,), in_specs=[pl.BlockSpec((rows, cdim), lambda i, me_ref: (i, 0))],
            out_specs=pl.BlockSpec((1, rows, cdim), lambda i, me_ref: (me_ref[0], i, 0))),
        compiler_params=_params("parallel"))(me, w)


def _adamw_math(w, g, m, v):
    m = ADAM_B1 * m + (1.0 - ADAM_B1) * g
    v = ADAM_B2 * v + (1.0 - ADAM_B2) * (g * g)
    m_hat = m / (1.0 - ADAM_B1 ** ADAM_STEP)
    v_hat = v / (1.0 - ADAM_B2 ** ADAM_STEP)
    delta = -ADAM_LR * (m_hat / (jnp.sqrt(v_hat) + ADAM_EPS) + ADAM_WD * w)
    return delta, m, v


def _sum_parts(me, p_ref, own):
    g = None
    for j in range(N_DEV):
        term = (p_ref[j] if own is None else jnp.where(me == j, own, p_ref[j])).astype(F32)
        g = term if g is None else g + term
    return g


def _adamw_reduce(parts, own, me, w, m, v, rows, name):
    r, cdim = w.shape

    def body(me_ref, p_ref, own_ref, w_ref, m_ref, v_ref, g_out, d_out, m_out, v_out):
        g = _sum_parts(me_ref[0], p_ref, own_ref[0])
        d, mn, vn = _adamw_math(w_ref[...], g, m_ref[...], v_ref[...])
        g_out[...] = g
        d_out[...] = d
        m_out[...] = mn
        v_out[...] = vn

    blk = pl.BlockSpec((rows, cdim), lambda i, me_ref: (i, 0))
    sds = jax.ShapeDtypeStruct(w.shape, F32)
    return pl.pallas_call(
        body, name=name, out_shape=(sds,) * 4,
        grid_spec=pltpu.PrefetchScalarGridSpec(
            num_scalar_prefetch=1, grid=(r // rows,),
            in_specs=[pl.BlockSpec((N_DEV, rows, cdim), lambda i, me_ref: (0, i, 0)),
                      pl.BlockSpec((1, rows, cdim), lambda i, me_ref: (me_ref[0], i, 0)), blk, blk, blk],
            out_specs=(blk,) * 4),
        compiler_params=_params("parallel"))(me, parts, own, w, m, v)


def _adamw_small(parts, own, me, w, m, v, mask, name):
    def body(me_ref, *refs):
        refs = list(refs)
        p_ref = refs.pop(0)
        own_ref = None if own is None else refs.pop(0)
        w_ref, m_ref, v_ref = refs[:3]
        k_ref = None if mask is None else refs[3]
        g_out, d_out, m_out, v_out = refs[-4:]
        g = _sum_parts(me_ref[0], p_ref, None if own is None else own_ref[me_ref[0]])
        if mask is not None:
            g = g * k_ref[...]
        d, mn, vn = _adamw_math(w_ref[...], g, m_ref[...], v_ref[...])
        g_out[...] = g
        d_out[...] = d
        m_out[...] = mn
        v_out[...] = vn

    def whole(shape):
        nd = len(shape)
        return pl.BlockSpec(shape, lambda i, me_ref: (0,) * nd)

    sds = jax.ShapeDtypeStruct(w.shape, F32)
    ins = [parts] + ([] if own is None else [own]) + [w, m, v] + ([] if mask is None else [mask])
    return pl.pallas_call(
        body, name=name, out_shape=(sds,) * 4,
        grid_spec=pltpu.PrefetchScalarGridSpec(
            num_scalar_prefetch=1, grid=(1,), in_specs=[whole(a.shape) for a in ins],
            out_specs=(whole(w.shape),) * 4),
        compiler_params=_params("arbitrary"))(me, *ins)


_IN_SPLITS = ((0, 512), (512, 1024), (1024, 1536), (1536, 2560), (2560, IN_PAD))


def _in_proj(x, g1, w_in, tm):
    t_tok = x.shape[0]

    def body(x_ref, g_ref, w_ref, h_ref, *outs):
        xv = x_ref[...]
        r = lax.rsqrt(jnp.mean(xv * xv, axis=-1, keepdims=True) + EPS)
        h = (xv * r * g_ref[...]).astype(BF16)
        h_ref[...] = h
        for (a, b), o_ref in zip(_IN_SPLITS, outs):
            o_ref[...] = _dot(h, w_ref[:, a:b])

    row = lambda n: pl.BlockSpec((tm, n), lambda i: (i, 0))
    widths = [b - a for a, b in _IN_SPLITS]
    return pl.pallas_call(
        body, name="in_proj", grid=(t_tok // tm,),
        out_shape=(jax.ShapeDtypeStruct((t_tok, D_MODEL), BF16),) + tuple(
            jax.ShapeDtypeStruct((t_tok, n), F32) for n in widths),
        in_specs=[row(D_MODEL), _full((1, D_MODEL)), _full((D_MODEL, IN_PAD))],
        out_specs=(row(D_MODEL),) + tuple(row(n) for n in widths),
        compiler_params=_params("parallel"))(x, g1, w_in)


def _lane_masks():
    lane = lax.broadcasted_iota(jnp.int32, (1, 2 * HEAD_DIM), 1)
    left = (lane < HEAD_DIM).astype(F32)
    return left, 1.0 - left


def _stack_pair(v, m_l, m_r):
    return jnp.concatenate([v * m_l, v * m_r], axis=0).astype(BF16)


def _gmlp_common(u, v, lnw, lnb, avg, wcat_ref, bias, m_l, m_r):
    ug, dug = _gelu_and_grad(u)
    vg, dvg = _gelu_and_grad(v)
    mu = _split_dot(vg, avg, 2)
    vc = vg - mu
    var = _split_dot(vc * vc, avg, 2)
    rstd = lax.rsqrt(var + EPS)
    vhat = vc * rstd
    vn = vhat * lnw + lnb
    cols = []
    for j in range(N_HEADS // 2):
        cols.append(_dot(wcat_ref[j], _stack_pair(vn[:, 128 * j:128 * (j + 1)], m_l, m_r)))
    mixed = jnp.concatenate(cols, axis=1) + bias
    return ug, dug, dvg, rstd, vhat, vn, mixed


def _gmlp_fwd(u, v, lnw, lnb, wcat, bias, avg):
    t_tok = u.shape[0]

    def body(u_ref, v_ref, lnw_ref, lnb_ref, wcat_ref, bias_ref, avg_ref, o_ref):
        m_l, m_r = _lane_masks()
        ug, _, _, _, _, _, mixed = _gmlp_common(
            u_ref[...], v_ref[...], lnw_ref[...], lnb_ref[...], avg_ref[...], wcat_ref, bias_ref[...], m_l, m_r)
        o_ref[...] = (ug * mixed).astype(BF16)

    row = pl.BlockSpec((CHUNK, GM_WIDTH), lambda i: (i, 0))
    return pl.pallas_call(
        body, name="gmlp_fwd", grid=(t_tok // CHUNK,), out_shape=jax.ShapeDtypeStruct((t_tok, GM_WIDTH), BF16),
        in_specs=[row, row, _full((1, GM_WIDTH)), _full((1, GM_WIDTH)), _full(wcat.shape), _full(bias.shape),
                  _full(avg.shape)],
        out_specs=row, compiler_params=_params("parallel"))(u, v, lnw, lnb, wcat, bias, avg)


def _ssd_common(xext_ref, dtr, cw_ref, cb, dtb, alog, expand, tril):
    q = CHUNK
    taps = [xext_ref[pl.ds(5 + k, q), :] for k in range(4)]
    pre = cb + cw_ref[0:1, :] * taps[0] + cw_ref[1:2, :] * taps[1] + cw_ref[2:3, :] * taps[2] + cw_ref[3:4, :] * taps[3]
    sg = jax.nn.sigmoid(pre)
    act = pre * sg
    lane = lax.broadcasted_iota(jnp.int32, (1, CHUNK), 1)
    a_row = jnp.where(lane < N_HEADS, -jnp.exp(alog), 0.0)
    dtp = dtr + dtb
    dt = _softplus(dtp)
    a_cs = _split_dot_left(tril, dt * a_row, 3)
    a_cs_t = a_cs.T
    dt_exp = _split_dot(dt, expand, 3)
    a_exp = _split_dot(a_cs, expand, 3)
    a_end = a_exp[q - 1:q, :]
    li = lax.broadcasted_iota(jnp.int32, (q, q), 0)
    si = lax.broadcasted_iota(jnp.int32, (q, q), 1)
    causal = si <= li
    decay = []
    for h in range(N_HEADS):
        seg = a_cs[:, h:h + 1] - a_cs_t[h:h + 1, :]
        decay.append(jnp.where(causal, jnp.exp(jnp.minimum(seg, 0.0)), 0.0))
    return dict(taps=taps, pre=pre, sg=sg, act=act, a_row=a_row, dtp=dtp, dt=dt, dt_exp=dt_exp, a_exp=a_exp,
                e=jnp.exp(a_exp), w_end=jnp.exp(a_end - a_exp), cd=jnp.exp(a_end), decay=decay)


def _ssd_specs(t_tok, seq, reverse):
    nc = seq // CHUNK

    def chunk(b, c):
        return b * nc + (nc - 1 - c if reverse else c)

    def row(n):
        return pl.BlockSpec((CHUNK, n), lambda b, c: (chunk(b, c), 0))

    tail = pl.BlockSpec((8, CONV_CH), lambda b, c: (jnp.maximum(chunk(b, c) * (CHUNK // 8) - 1, 0), 0))
    return nc, chunk, row, tail


def _fill_xext(xext_ref, tail_ref, xbc_ref, first_chunk):
    xext_ref[0:8, :] = jnp.where(first_chunk, 0.0, tail_ref[...])
    xext_ref[8:8 + CHUNK, :] = xbc_ref[...]


def _ssd_fwd(z, xbc, dtr, cw, cb, dtb, alog, dskip_exp, nw, expand, tril, seq):
    t_tok = z.shape[0]
    nc, chunk, row, tail = _ssd_specs(t_tok, seq, False)

    def body(z_ref, xbc_ref, tail_ref, dtr_ref, cw_ref, cb_ref, dtb_ref, alog_ref, dsk_ref, nw_ref, exp_ref,
             tril_ref, o_ref, y_ref, st_ref, xext_ref, state_ref):
        c = pl.program_id(1)

        @pl.when(c == 0)
        def _():
            state_ref[...] = jnp.zeros_like(state_ref)

        _fill_xext(xext_ref, tail_ref, xbc_ref, c == 0)
        m_l, m_r = _lane_masks()
        f = _ssd_common(xext_ref, dtr_ref[...], cw_ref, cb_ref[...], dtb_ref[...], alog_ref[...], exp_ref[...],
                        tril_ref[...])
        act = f["act"]
        xs = act[:, :SSM_WIDTH]
        xdt = xs * f["dt_exp"]
        xw = xdt * f["w_end"]
        state = state_ref[...]
        st_ref[0] = state
        ydiag, yoff, snew = [], [], []
        for g in range(2):
            bg = act[:, 512 + 128 * g:640 + 128 * g].astype(BF16)
            cg = act[:, 768 + 128 * g:896 + 128 * g].astype(BF16)
            cb_mat = _dot(cg, bg, _NT)
            for pr in range(2):
                h0 = 4 * g + 2 * pr
                gcat = jnp.concatenate(
                    [(cb_mat * f["decay"][h0]).astype(BF16), (cb_mat * f["decay"][h0 + 1]).astype(BF16)], axis=1)
                ydiag.append(_dot(gcat, _stack_pair(xdt[:, 64 * h0:64 * h0 + 128], m_l, m_r)))
            yoff.append(_dot(cg, state[:, 256 * g:256 * (g + 1)].astype(BF16)))
            snew.append(_dot(bg, xw[:, 256 * g:256 * (g + 1)].astype(BF16), _TN))
        y = jnp.concatenate(ydiag, axis=1) + f["e"] * jnp.concatenate(yoff, axis=1) + dsk_ref[...] * xs
        state_ref[...] = state * f["cd"] + jnp.concatenate(snew, axis=1)
        y_ref[...] = y
        zv = z_ref[...]
        yg = y * (zv * jax.nn.sigmoid(zv))
        outs = []
        for g in range(2):
            ygg = yg[:, 256 * g:256 * (g + 1)]
            outs.append(ygg * lax.rsqrt(jnp.mean(ygg * ygg, axis=-1, keepdims=True) + EPS))
        o_ref[...] = (jnp.concatenate(outs, axis=1) * nw_ref[...]).astype(BF16)

    consts = [cw, cb, dtb, alog, dskip_exp, nw, expand, tril]
    return pl.pallas_call(
        body, name="ssd_fwd", grid=(t_tok // seq, nc),
        out_shape=(jax.ShapeDtypeStruct((t_tok, SSM_WIDTH), BF16), jax.ShapeDtypeStruct((t_tok, SSM_WIDTH), F32),
                   jax.ShapeDtypeStruct((t_tok // CHUNK, N_STATE, SSM_WIDTH), F32)),
        in_specs=[row(SSM_WIDTH), row(CONV_CH), tail, row(CHUNK)] + [_full(a.shape) for a in consts],
        out_specs=(row(SSM_WIDTH), row(SSM_WIDTH),
                   pl.BlockSpec((1, N_STATE, SSM_WIDTH), lambda b, c: (chunk(b, c), 0, 0))),
        scratch_shapes=[pltpu.VMEM((CHUNK + 16, CONV_CH), F32), pltpu.VMEM((N_STATE, SSM_WIDTH), F32)],
        compiler_params=_params("arbitrary", "arbitrary"))(z, xbc, xbc, dtr, *consts)


def _out_proj(mix_a, mix_b, w_out, x, g2, g3, tm):
    t_tok = x.shape[0]

    def body(a_ref, b_ref, w_ref, x_ref, g2_ref, g3_ref, o_ref, x2_ref, h3_ref, mix_ref):
        o = _dot(a_ref[...], w_ref[0:GM_WIDTH, :]) + _dot(b_ref[...], w_ref[GM_WIDTH:, :])
        o_ref[...] = o
        mix_ref[:, 0:GM_WIDTH] = a_ref[...]
        mix_ref[:, GM_WIDTH:] = b_ref[...]
        r2 = lax.rsqrt(jnp.mean(o * o, axis=-1, keepdims=True) + EPS)
        x2 = x_ref[...] + o * r2 * g2_ref[...]
        x2_ref[...] = x2
        r3 = lax.rsqrt(jnp.mean(x2 * x2, axis=-1, keepdims=True) + EPS)
        h3_ref[...] = (x2 * r3 * g3_ref[...]).astype(BF16)

    row = lambda n: pl.BlockSpec((tm, n), lambda i: (i, 0))
    sd = lambda dt: jax.ShapeDtypeStruct((t_tok, D_MODEL), dt)
    return pl.pallas_call(
        body, name="out_proj", grid=(t_tok // tm,), out_shape=(sd(F32), sd(F32), sd(BF16), sd(BF16)),
        in_specs=[row(GM_WIDTH), row(SSM_WIDTH), _full((D_MODEL, D_MODEL)), row(D_MODEL), _full((1, D_MODEL)),
                  _full((1, D_MODEL))],
        out_specs=(row(D_MODEL),) * 4, compiler_params=_params("parallel"))(mix_a, mix_b, w_out, x, g2, g3)


def _mlp_fwd(h3, w_up_blk, w_down, x2, target, g4, tm):
    t_tok = x2.shape[0]
    tf = w_up_blk.shape[2]
    nf = w_up_blk.shape[0]

    def body(h_ref, wu_ref, wd_ref, x2_ref, t_ref, g4_ref, ra_ref, dd_ref, dy_ref, dg4_ref, loss_ref, acc_ref):
        i, j = pl.program_id(0), pl.program_id(1)
        ra = jnp.maximum(_dot(h_ref[...], wu_ref[0]), 0.0).astype(BF16)
        ra_ref[...] = ra
        part = _dot(ra * ra, wd_ref[...])

        @pl.when(j == 0)
        def _():
            acc_ref[...] = part

        @pl.when(j > 0)
        def _():
            acc_ref[...] += part

        @pl.when(j == nf - 1)
        def _():
            dvec = acc_ref[...]
            r4 = lax.rsqrt(jnp.mean(dvec * dvec, axis=-1, keepdims=True) + EPS)
            dn = dvec * r4
            g4 = g4_ref[...]
            err = x2_ref[...] + dn * g4 - t_ref[...]
            dy = err * (1.0 / D_MODEL)
            dy_ref[...] = dy
            dg = dy * g4
            dd_ref[...] = (r4 * (dg - dn * jnp.mean(dg * dn, axis=-1, keepdims=True))).astype(BF16)
            _acc_rows(dg4_ref, _rsum(dy * dn), i == 0)
            tile_loss = 0.5 * jnp.sum(jnp.sum(err * err, axis=-1, keepdims=True), axis=0, keepdims=True) / D_MODEL
            _acc_rows(loss_ref, jnp.broadcast_to(tile_loss, (1, 128)), i == 0)

    row = pl.BlockSpec((tm, D_MODEL), lambda i, j: (i, 0))
    return pl.pallas_call(
        body, name="mlp_fwd", grid=(t_tok // tm, nf),
        out_shape=(jax.ShapeDtypeStruct((t_tok, D_FF), BF16), jax.ShapeDtypeStruct((t_tok, D_MODEL), BF16),
                   jax.ShapeDtypeStruct((t_tok, D_MODEL), F32), jax.ShapeDtypeStruct((8, D_MODEL), F32),
                   jax.ShapeDtypeStruct((8, 128), F32)),
        in_specs=[row, pl.BlockSpec((1, D_MODEL, tf), lambda i, j: (j, 0, 0)),
                  pl.BlockSpec((tf, D_MODEL), lambda i, j: (j, 0)), row, row, _full((1, D_MODEL))],
        out_specs=(pl.BlockSpec((tm, tf), lambda i, j: (i, j)), row, row, _full((8, D_MODEL)), _full((8, 128))),
        scratch_shapes=[pltpu.VMEM((tm, D_MODEL), F32)],
        compiler_params=_params("arbitrary", "arbitrary"))(h3, w_up_blk, w_down, x2, target, g4)


def _mlp_bwd(dd, w_down, ra, w_up_blk, x2, dy, o, g3, g2, tm):
    t_tok = x2.shape[0]
    tf = w_up_blk.shape[2]
    nf = w_up_blk.shape[0]

    def body(dd_ref, wd_ref, ra_ref, wu_ref, x2_ref, dy_ref, o_ref, g3_ref, g2_ref, da_ref, dx2_ref, do_ref, dg3_ref,
             dg2_ref, acc_ref):
        i, j = pl.program_id(0), pl.program_id(1)
        df = _dot(dd_ref[...], wd_ref[...], _NT)
        da = (df * (2.0 * ra_ref[...].astype(F32))).astype(BF16)
        da_ref[...] = da
        part = _dot(da, wu_ref[0], _NT)

        @pl.when(j == 0)
        def _():
            acc_ref[...] = part

        @pl.when(j > 0)
        def _():
            acc_ref[...] += part

        @pl.when(j == nf - 1)
        def _():
            dn3, dg3 = _rms_bwd(x2_ref[...], g3_ref[...], acc_ref[...])
            dx2 = dy_ref[...] + dn3
            dx2_ref[...] = dx2
            do, dg2 = _rms_bwd(o_ref[...], g2_ref[...], dx2)
            do_ref[...] = do.astype(BF16)
            _acc_rows(dg3_ref, dg3, i == 0)
            _acc_rows(dg2_ref, dg2, i == 0)

    row = pl.BlockSpec((tm, D_MODEL), lambda i, j: (i, 0))
    vec = _full((1, D_MODEL))
    acc = _full((8, D_MODEL))
    sd = lambda dt: jax.ShapeDtypeStruct((t_tok, D_MODEL), dt)
    return pl.pallas_call(
        body, name="mlp_bwd", grid=(t_tok // tm, nf),
        out_shape=(jax.ShapeDtypeStruct((t_tok, D_FF), BF16), sd(F32), sd(BF16),
                   jax.ShapeDtypeStruct((8, D_MODEL), F32), jax.ShapeDtypeStruct((8, D_MODEL), F32)),
        in_specs=[row, pl.BlockSpec((tf, D_MODEL), lambda i, j: (j, 0)), pl.BlockSpec((tm, tf), lambda i, j: (i, j)),
                  pl.BlockSpec((1, D_MODEL, tf), lambda i, j: (j, 0, 0)), row, row, row, vec, vec],
        out_specs=(pl.BlockSpec((tm, tf), lambda i, j: (i, j)), row, row, acc, acc),
        scratch_shapes=[pltpu.VMEM((tm, D_MODEL), F32)],
        compiler_params=_params("arbitrary", "arbitrary"))(dd, w_down, ra, w_up_blk, x2, dy, o, g3, g2)


def _wgrad(a, b, out_blocks, bm, bn, bk, square_a, name, dep=None):
    t_tok, m = a.shape
    n = b.shape[1]
    nk = t_tok // bk

    def body(a_ref, b_ref, *rest):
        o_ref, acc_ref = rest[-2:]
        k = pl.program_id(2)
        av = a_ref[...]
        if square_a:
            av = av * av
        part = _dot(av, b_ref[...], _TN)

        @pl.when(k == 0)
        def _():
            acc_ref[...] = part

        @pl.when(k > 0)
        def _():
            acc_ref[...] += part

        @pl.when(k == nk - 1)
        def _():
            res = acc_ref[...].astype(BF16)
            if out_blocks is None:
                o_ref[...] = res
            else:
                o_ref[0] = res

    if out_blocks is None:
        out_shape = jax.ShapeDtypeStruct((m, n), BF16)
        out_spec = pl.BlockSpec((bm, bn), lambda i, j, k: (i, j))
    else:
        assert n // out_blocks == bn
        out_shape = jax.ShapeDtypeStruct((out_blocks, m, bn), BF16)
        out_spec = pl.BlockSpec((1, bm, bn), lambda i, j, k: (j, i, 0))
    deps = [] if dep is None else [dep]
    return pl.pallas_call(
        body, name=name, grid=(m // bm, n // bn, nk), out_shape=out_shape,
        in_specs=[pl.BlockSpec((bk, bm), lambda i, j, k: (k, i)), pl.BlockSpec((bk, bn), lambda i, j, k: (k, j))]
        + [pl.BlockSpec(memory_space=pl.ANY)] * len(deps),
        out_specs=out_spec, scratch_shapes=[pltpu.VMEM((bm, bn), F32)],
        compiler_params=_params("parallel", "parallel", "arbitrary"))(a, b, *deps)


def _dmix(do, w_out, tm, dep=None):
    t_tok = do.shape[0]

    def body(d_ref, w_ref, *rest):
        rest[-1][...] = _dot(d_ref[...], w_ref[...], _NT)

    row = pl.BlockSpec((tm, D_MODEL), lambda i: (i, 0))
    deps = [] if dep is None else [dep]
    return pl.pallas_call(
        body, name="dmix", grid=(t_tok // tm,), out_shape=jax.ShapeDtypeStruct((t_tok, D_MODEL), F32),
        in_specs=[row, _full((D_MODEL, D_MODEL))] + [pl.BlockSpec(memory_space=pl.ANY)] * len(deps), out_specs=row,
        compiler_params=_params("parallel"))(do, w_out, *deps)


def _gmlp_bwd(dmix, u, v, lnw, lnb, wcat, wtcat, bias, avg, expand_t):
    t_tok = u.shape[0]

    def body(dm_ref, u_ref, v_ref, lnw_ref, lnb_ref, wcat_ref, wtcat_ref, bias_ref, avg_ref, expt_ref, du_ref, dv_ref,
             dw_ref, db_ref, dlnw_ref, dlnb_ref):
        i = pl.program_id(0)
        m_l, m_r = _lane_masks()
        avg = avg_ref[...]
        lnw = lnw_ref[...]
        ug, dug, dvg, rstd, vhat, vn, mixed = _gmlp_common(
            u_ref[...], v_ref[...], lnw, lnb_ref[...], avg, wcat_ref, bias_ref[...], m_l, m_r)
        dya = dm_ref[...]
        du_ref[...] = (dya * mixed * dug).astype(BF16)
        dmixed = dya * ug
        dvn_cols, dws = [], []
        for j in range(N_HEADS // 2):
            dmp = dmixed[:, 128 * j:128 * (j + 1)]
            dvn_cols.append(_dot(wtcat_ref[j], _stack_pair(dmp, m_l, m_r)))
            vnp = vn[:, 128 * j:128 * (j + 1)].astype(BF16)
            dws.append(_dot((dmp * m_l).astype(BF16), vnp, _NT))
            dws.append(_dot((dmp * m_r).astype(BF16), vnp, _NT))
        dvn = jnp.concatenate(dvn_cols, axis=1)
        dvh = dvn * lnw
        dvgel = rstd * (dvh - _split_dot(dvh, avg, 2) - vhat * _split_dot(dvh * vhat, avg, 2))
        dv_ref[...] = (dvgel * dvg).astype(BF16)
        dbt = _split_dot(dmixed, expt_ref[...], 2)
        first = i == 0

        @pl.when(first)
        def _():
            for h in range(N_HEADS):
                dw_ref[h] = dws[h]
            db_ref[...] = dbt

        @pl.when(jnp.logical_not(first))
        def _():
            for h in range(N_HEADS):
                dw_ref[h] += dws[h]
            db_ref[...] += dbt

        _acc_rows(dlnw_ref, _rsum(dvn * vhat), first)
        _acc_rows(dlnb_ref, _rsum(dvn), first)

    row = pl.BlockSpec((CHUNK, GM_WIDTH), lambda i: (i, 0))
    consts = [lnw, lnb, wcat, wtcat, bias, avg, expand_t]
    return pl.pallas_call(
        body, name="gmlp_bwd", grid=(t_tok // CHUNK,),
        out_shape=(jax.ShapeDtypeStruct((t_tok, GM_WIDTH), BF16), jax.ShapeDtypeStruct((t_tok, GM_WIDTH), BF16),
                   jax.ShapeDtypeStruct((N_HEADS, CHUNK, CHUNK), F32), jax.ShapeDtypeStruct((CHUNK, CHUNK), F32),
                   jax.ShapeDtypeStruct((8, GM_WIDTH), F32), jax.ShapeDtypeStruct((8, GM_WIDTH), F32)),
        in_specs=[pl.BlockSpec((CHUNK, GM_WIDTH), lambda i: (i, 0)), row, row] + [_full(a.shape) for a in consts],
        out_specs=(row, row, _full((N_HEADS, CHUNK, CHUNK)), _full((CHUNK, CHUNK)), _full((8, GM_WIDTH)),
                   _full((8, GM_WIDTH))),
        compiler_params=_params("arbitrary"))(dmix, u, v, *consts)


def _ssd_bwd(dmix, z, xbc, dtr, y, states, cw, cb, dtb, alog, dskip_exp, nw, expand, expand_t, tril, triu, seq):
    t_tok = z.shape[0]
    nc, chunk, row, tail = _ssd_specs(t_tok, seq, True)
    q = CHUNK

    def body(dm_ref, z_ref, xbc_ref, tail_ref, dtr_ref, y_ref, st_ref, cw_ref, cb_ref, dtb_ref, alog_ref, dsk_ref,
             nw_ref, exp_ref, expt_ref, tril_ref, triu_ref, dz_ref, dxbc_ref, ddt_ref, dcw_ref, dcb_ref, ddtb_ref,
             dalog_ref, dd_ref, dnw_ref, xext_ref, dext_ref, dstate_ref):
        b, c = pl.program_id(0), pl.program_id(1)
        first = jnp.logical_and(b == 0, c == 0)

        @pl.when(c == 0)
        def _():
            dstate_ref[...] = jnp.zeros_like(dstate_ref)
            dext_ref[q:q + 8, :] = jnp.zeros((8, CONV_CH), F32)

        _fill_xext(xext_ref, tail_ref, xbc_ref, c == nc - 1)
        m_l, m_r = _lane_masks()
        expt = expt_ref[...]
        f = _ssd_common(xext_ref, dtr_ref[...], cw_ref, cb_ref[...], dtb_ref[...], alog_ref[...], exp_ref[...],
                        tril_ref[...])
        act, pre, sg = f["act"], f["pre"], f["sg"]
        xs = act[:, :SSM_WIDTH]
        xdt = xs * f["dt_exp"]
        xw = xdt * f["w_end"]
        state = st_ref[0]
        dstate = dstate_ref[...]
        zv, yv, dout, nw = z_ref[...], y_ref[...], dm_ref[...], nw_ref[...]
        sz = jax.nn.sigmoid(zv)
        sl = zv * sz
        yg = yv * sl
        tv = dout * nw
        dyg_parts, ygh_parts = [], []
        for g in range(2):
            ygg = yg[:, 256 * g:256 * (g + 1)]
            rr = lax.rsqrt(jnp.mean(ygg * ygg, axis=-1, keepdims=True) + EPS)
            ygh = ygg * rr
            tg = tv[:, 256 * g:256 * (g + 1)]
            dyg_parts.append(rr * (tg - ygh * jnp.mean(tg * ygh, axis=-1, keepdims=True)))
            ygh_parts.append(ygh)
        dyg = jnp.concatenate(dyg_parts, axis=1)
        dnw = _rsum(dout * jnp.concatenate(ygh_parts, axis=1))
        dy = dyg * sl
        dz_ref[...] = (dyg * yv * (sz * (1.0 + zv * (1.0 - sz)))).astype(BF16)
        ddsk = _rsum(dy * xs)
        dye = dy * f["e"]
        lane = lax.broadcasted_iota(jnp.int32, (q, q), 1)
        sub = lax.broadcasted_iota(jnp.int32, (q, q), 0)
        rs_mat = jnp.zeros((q, q), F32)
        cs_mat = jnp.zeros((q, q), F32)
        dxdt_cols, yoff, dst_in, dxw, d_b, d_c = [], [], [], [], [], []
        for g in range(2):
            bg = act[:, 512 + 128 * g:640 + 128 * g].astype(BF16)
            cg = act[:, 768 + 128 * g:896 + 128 * g].astype(BF16)
            cb_mat = _dot(cg, bg, _NT)
            stg = state[:, 256 * g:256 * (g + 1)].astype(BF16)
            dyeg = dye[:, 256 * g:256 * (g + 1)].astype(BF16)
            yoff.append(_dot(cg, stg))
            dcg = _dot(dyeg, stg, _NT)
            dst_in.append(_dot(cg, dyeg, _TN))
            dcb = jnp.zeros((q, q), F32)
            for pr in range(2):
                h0 = 4 * g + 2 * pr
                gf = [cb_mat * f["decay"][h0], cb_mat * f["decay"][h0 + 1]]
                gcat = jnp.concatenate([gf[0].astype(BF16), gf[1].astype(BF16)], axis=1)
                xst = _stack_pair(xdt[:, 64 * h0:64 * h0 + 128], m_l, m_r)
                dyp = dy[:, 64 * h0:64 * h0 + 128].astype(BF16)
                dgcat = _dot(dyp, xst, _NT)
                dxst = _dot(gcat, dyp, _TN)
                dxdt_cols.append(dxst[:q] * m_l + dxst[q:] * m_r)
                for i in range(2):
                    h = h0 + i
                    dg = dgcat[:, q * i:q * (i + 1)]
                    mm = dg * gf[i]
                    rs_mat = rs_mat + jnp.where(lane == h, jnp.sum(mm, axis=1, keepdims=True), 0.0)
                    cs_mat = cs_mat + jnp.where(sub == h, jnp.sum(mm, axis=0, keepdims=True), 0.0)
                    dcb = dcb + dg * f["decay"][h]
            dcb16 = dcb.astype(BF16)
            dstg = dstate[:, 256 * g:256 * (g + 1)].astype(BF16)
            d_c.append(dcg + _dot(dcb16, bg))
            dxw.append(_dot(bg, dstg))
            d_b.append(_dot(dcb16, cg, _TN) + _dot(xw[:, 256 * g:256 * (g + 1)].astype(BF16), dstg, _NT))
        dxw = jnp.concatenate(dxw, axis=1)
        dxdt = jnp.concatenate(dxdt_cols, axis=1) + dxw * f["w_end"]
        qv = dxw * xw
        end_row = _rsum(qv) + _rsum(dstate * state) * f["cd"]
        x2 = dye * jnp.concatenate(yoff, axis=1) - qv
        row_i = lax.broadcasted_iota(jnp.int32, (q, 1), 0)
        x2 = x2 + jnp.where(row_i == q - 1, end_row, 0.0)
        da_cs = _split_dot(x2, expt, 3) + rs_mat - cs_mat.T
        ddt = _split_dot(dxdt * xs, expt, 3)
        dxs = dsk_ref[...] * dy + dxdt * f["dt_exp"]
        dda = _split_dot_left(triu_ref[...], da_cs, 3)
        ddt = ddt + dda * f["a_row"]
        dalog = _rsum(dda * f["dt"]) * f["a_row"]
        draw = ddt * jax.nn.sigmoid(f["dtp"])
        ddt_ref[...] = draw.astype(BF16)
        dact = jnp.concatenate([dxs] + d_b + d_c, axis=1)
        dpre = dact * (sg * (1.0 + pre * (1.0 - sg)))
        dext_ref[0:q, :] = dpre
        dxbc = cw_ref[0:1, :] * dext_ref[pl.ds(3, q), :]
        for k in range(1, 4):
            dxbc = dxbc + cw_ref[k:k + 1, :] * dext_ref[pl.ds(3 - k, q), :]
        dxbc_ref[...] = dxbc.astype(BF16)
        dext_ref[q:q + 8, :] = dpre[0:8, :]
        dstate_ref[...] = dstate * f["cd"] + jnp.concatenate(dst_in, axis=1)
        row8 = lax.broadcasted_iota(jnp.int32, (8, 1), 0)
        dcw = jnp.zeros((8, CONV_CH), F32)
        for k in range(4):
            dcw = dcw + jnp.where(row8 == k, _rsum(dpre * f["taps"][k]), 0.0)

        @pl.when(first)
        def _():
            dcw_ref[...] = dcw

        @pl.when(jnp.logical_not(first))
        def _():
            dcw_ref[...] += dcw

        _acc_rows(dcb_ref, _rsum(dpre), first)
        _acc_rows(ddtb_ref, _rsum(draw), first)
        _acc_rows(dalog_ref, dalog, first)
        _acc_rows(dd_ref, ddsk, first)
        _acc_rows(dnw_ref, dnw, first)

    consts = [cw, cb, dtb, alog, dskip_exp, nw, expand, expand_t, tril, triu]
    acc = lambda n: jax.ShapeDtypeStruct((8, n), F32)
    return pl.pallas_call(
        body, name="ssd_bwd", grid=(t_tok // seq, nc),
        out_shape=(jax.ShapeDtypeStruct((t_tok, SSM_WIDTH), BF16), jax.ShapeDtypeStruct((t_tok, CONV_CH), BF16),
                   jax.ShapeDtypeStruct((t_tok, CHUNK), BF16), acc(CONV_CH), acc(CONV_CH), acc(CHUNK), acc(CHUNK),
                   acc(SSM_WIDTH), acc(SSM_WIDTH)),
        in_specs=[pl.BlockSpec((CHUNK, SSM_WIDTH), lambda b, c: (chunk(b, c), 1)), row(SSM_WIDTH), row(CONV_CH), tail,
                  row(CHUNK), row(SSM_WIDTH), pl.BlockSpec((1, N_STATE, SSM_WIDTH), lambda b, c: (chunk(b, c), 0, 0))]
        + [_full(a.shape) for a in consts],
        out_specs=(row(SSM_WIDTH), row(CONV_CH), row(CHUNK), _full((8, CONV_CH)), _full((8, CONV_CH)),
                   _full((8, CHUNK)), _full((8, CHUNK)), _full((8, SSM_WIDTH)), _full((8, SSM_WIDTH))),
        scratch_shapes=[pltpu.VMEM((CHUNK + 16, CONV_CH), F32), pltpu.VMEM((CHUNK + 8, CONV_CH), F32),
                        pltpu.VMEM((N_STATE, SSM_WIDTH), F32)],
        compiler_params=_params("arbitrary", "arbitrary"))(dmix, z, xbc, xbc, dtr, y, states, *consts)


def _in_bwd(du, dv, dz, dxbc, ddt, w_in, x, dx2, g1, tm):
    t_tok = x.shape[0]

    def body(du_ref, dv_ref, dz_ref, dxbc_ref, ddt_ref, w_ref, x_ref, dx2_ref, g_ref, dp_ref, gx_ref, dg_ref):
        i = pl.program_id(0)
        dh = None
        for (a, b), ref in zip(_IN_SPLITS, (du_ref, dv_ref, dz_ref, dxbc_ref, ddt_ref)):
            piece = ref[...]
            dp_ref[:, a:b] = piece
            part = _dot(piece, w_ref[:, a:b], _NT)
            dh = part if dh is None else dh + part
        dn, dg = _rms_bwd(x_ref[...], g_ref[...], dh)
        gx_ref[...] = dx2_ref[...] + dn
        _acc_rows(dg_ref, dg, i == 0)

    row = lambda n: pl.BlockSpec((tm, n), lambda i: (i, 0))
    widths = [b - a for a, b in _IN_SPLITS]
    return pl.pallas_call(
        body, name="in_bwd", grid=(t_tok // tm,),
        out_shape=(jax.ShapeDtypeStruct((t_tok, IN_PAD), BF16), jax.ShapeDtypeStruct((t_tok, D_MODEL), F32),
                   jax.ShapeDtypeStruct((8, D_MODEL), F32)),
        in_specs=[row(n) for n in widths] + [_full((D_MODEL, IN_PAD)), row(D_MODEL), row(D_MODEL), _full((1, D_MODEL))],
        out_specs=(row(IN_PAD), row(D_MODEL), _full((8, D_MODEL))),
        compiler_params=_params("arbitrary"))(du, dv, dz, dxbc, ddt, w_in, x, dx2, g1)


def _pad_lanes(a, n):
    return jnp.pad(a, ((0, 0), (0, n - a.shape[1])))


def _local_step(x, target, seq, w_in_p, conv_w, small, late_weights, after_mlp_grads):
    t_tok = x.shape[0]
    tm = min(512, t_tok)
    avg, expand, expand_t, tril, triu = _const_mats()
    g1, g2, g3, g4 = (small[k].reshape(1, D_MODEL) for k in
                      ("norm_mix_pre", "norm_mix_post", "norm_ffn_pre", "norm_ffn_post"))
    lnw = small["gm_ln_w"].reshape(1, GM_WIDTH)
    lnb = small["gm_ln_b"].reshape(1, GM_WIDTH)
    causal = jnp.tril(jnp.ones((CHUNK, CHUNK), F32))
    wm = small["gm_w_s"] * causal
    pair = lambda w: w.reshape(4, 2, CHUNK, CHUNK).transpose(0, 2, 1, 3).reshape(4, CHUNK, 2 * CHUNK).astype(BF16)
    wcat = pair(wm)
    wtcat = pair(jnp.swapaxes(wm, 1, 2))
    bias = jnp.repeat(small["gm_b_s"].T, HEAD_DIM, axis=1)
    cb = small["conv_b"].reshape(1, CONV_CH)
    dtb = _pad_lanes(small["dt_bias"].reshape(1, N_HEADS), CHUNK)
    alog = _pad_lanes(small["a_log"].reshape(1, N_HEADS), CHUNK)
    dskip_exp = jnp.repeat(small["d_skip"].reshape(1, N_HEADS), HEAD_DIM, axis=1)
    nw = small["ssm_norm_w"].reshape(1, SSM_WIDTH)

    h1, u, v, z, xbc, dtr = _in_proj(x, g1, w_in_p, tm)
    mix_a = _gmlp_fwd(u, v, lnw, lnb, wcat, bias, avg)
    mix_b, y_pre, states = _ssd_fwd(z, xbc, dtr, conv_w, cb, dtb, alog, dskip_exp, nw, expand, tril, seq)
    w_out, w_up_blk, w_down = late_weights(mix_b)
    o, x2, h3, mix = _out_proj(mix_a, mix_b, w_out, x, g2, g3, tm)
    ra, dd, dy, dg4, loss = _mlp_fwd(h3, w_up_blk, w_down, x2, target, g4, tm)

    da, dx2, do, dg3, dg2 = _mlp_bwd(dd, w_down, ra, w_up_blk, x2, dy, o, g3, g2, tm)
    bk = min(512, t_tok)
    g_w_down = _wgrad(ra, dd, None, 1024, D_MODEL, bk, True, "wgrad_down")
    g_w_up = _wgrad(h3, da, N_DEV, D_MODEL, D_FF // N_DEV, bk, False, "wgrad_up")
    dep = after_mlp_grads(g_w_down, g_w_up)
    dmix = _dmix(do, w_out, tm, dep)
    g_w_out = _wgrad(mix, do, None, D_MODEL, D_MODEL, bk, False, "wgrad_out", dep)
    du, dv, dws, dbt, dlnw, dlnb = _gmlp_bwd(dmix, u, v, lnw, lnb, wcat, wtcat, bias, avg, expand_t)
    dz, dxbc, ddt, dcw, dcb, ddtb, dalog, ddsk, dnw = _ssd_bwd(
        dmix, z, xbc, dtr, y_pre, states, conv_w, cb, dtb, alog, dskip_exp, nw, expand, expand_t, tril, triu, seq)
    dproj, grad_x, dg1 = _in_bwd(du, dv, dz, dxbc, ddt, w_in_p, x, dx2, g1, tm)
    g_w_in = _wgrad(h1, dproj, None, 512, IN_PAD, bk, False, "wgrad_in")

    grads = dict(
        w_in=g_w_in, w_out=g_w_out, w_up=g_w_up, w_down=g_w_down, conv_w=dcw[0:4],
        norm_mix_pre=dg1[0:1], norm_mix_post=dg2[0:1], norm_ffn_pre=dg3[0:1], norm_ffn_post=dg4[0:1],
        gm_ln_w=dlnw[0:1], gm_ln_b=dlnb[0:1], gm_w_s=dws, gm_b_s=dbt.T[0:N_HEADS], conv_b=dcb[0:1],
        dt_bias=ddtb[0:1, 0:N_HEADS], a_log=dalog[0:1, 0:N_HEADS],
        d_skip=ddsk[0:1].reshape(N_HEADS, HEAD_DIM).sum(axis=1).reshape(1, N_HEADS), ssm_norm_w=dnw[0:1])
    return loss[0, 0], grad_x, grads


_SMALL_ROW_PARAMS = ("norm_mix_pre", "norm_mix_post", "norm_ffn_pre", "norm_ffn_post", "gm_ln_w", "gm_ln_b", "gm_b_s",
                     "conv_b", "dt_bias", "a_log", "d_skip", "ssm_norm_w")
_WEIGHTS = ("norm_mix_pre", "w_in", "gm_ln_w", "gm_ln_b", "gm_w_s", "gm_b_s", "conv_w", "conv_b", "dt_bias", "a_log",
            "d_skip", "ssm_norm_w", "w_out", "norm_mix_post", "norm_ffn_pre", "w_up", "w_down", "norm_ffn_post")


def _pack_rows(tensors):
    rows = [_pad_lanes(t.reshape(1, -1), D_MODEL) for t in tensors]
    rows.append(jnp.zeros((SMALL_ROWS - len(rows), D_MODEL), F32))
    return jnp.concatenate(rows, axis=0)


def kernel(x, norm_mix_pre, w_in, gm_ln_w, gm_ln_b, gm_w_s, gm_b_s, conv_w, conv_b, dt_bias, a_log, d_skip, ssm_norm_w, w_out, norm_mix_post, norm_ffn_pre, w_up, w_down, norm_ffn_post, loss_target, m_norm_mix_pre, m_w_in, m_gm_ln_w, m_gm_ln_b, m_gm_w_s, m_gm_b_s, m_conv_w, m_conv_b, m_dt_bias, m_a_log, m_d_skip, m_ssm_norm_w, m_w_out, m_norm_mix_post, m_norm_ffn_pre, m_w_up, m_w_down, m_norm_ffn_post, v_norm_mix_pre, v_w_in, v_gm_ln_w, v_gm_ln_b, v_gm_w_s, v_gm_b_s, v_conv_w, v_conv_b, v_dt_bias, v_a_log, v_d_skip, v_ssm_norm_w, v_w_out, v_norm_mix_post, v_norm_ffn_pre, v_w_up, v_w_down, v_norm_ffn_post):
    w = dict(norm_mix_pre=norm_mix_pre, w_in=w_in, gm_ln_w=gm_ln_w, gm_ln_b=gm_ln_b, gm_w_s=gm_w_s, gm_b_s=gm_b_s, conv_w=conv_w, conv_b=conv_b, dt_bias=dt_bias, a_log=a_log, d_skip=d_skip, ssm_norm_w=ssm_norm_w, w_out=w_out, norm_mix_post=norm_mix_post, norm_ffn_pre=norm_ffn_pre, w_up=w_up, w_down=w_down, norm_ffn_post=norm_ffn_post)
    m = dict(norm_mix_pre=m_norm_mix_pre, w_in=m_w_in, gm_ln_w=m_gm_ln_w, gm_ln_b=m_gm_ln_b, gm_w_s=m_gm_w_s, gm_b_s=m_gm_b_s, conv_w=m_conv_w, conv_b=m_conv_b, dt_bias=m_dt_bias, a_log=m_a_log, d_skip=m_d_skip, ssm_norm_w=m_ssm_norm_w, w_out=m_w_out, norm_mix_post=m_norm_mix_post, norm_ffn_pre=m_norm_ffn_pre, w_up=m_w_up, w_down=m_w_down, norm_ffn_post=m_norm_ffn_post)
    v = dict(norm_mix_pre=v_norm_mix_pre, w_in=v_w_in, gm_ln_w=v_gm_ln_w, gm_ln_b=v_gm_ln_b, gm_w_s=v_gm_w_s, gm_b_s=v_gm_b_s, conv_w=v_conv_w, conv_b=v_conv_b, dt_bias=v_dt_bias, a_log=v_a_log, d_skip=v_d_skip, ssm_norm_w=v_ssm_norm_w, w_out=v_w_out, norm_mix_post=v_norm_mix_post, norm_ffn_pre=v_norm_ffn_pre, w_up=v_w_up, w_down=v_w_down, norm_ffn_post=v_norm_ffn_post)
    n_batch, seq, _ = x.shape
    shard_in = IN_COLS // N_DEV

    me = (4 * lax.axis_index("x") + 2 * lax.axis_index("y") + lax.axis_index("c")).astype(jnp.int32).reshape(1)

    def in_slot(own):
        return lax.dynamic_update_slice(lax.empty((N_DEV,) + own.shape, own.dtype), own[None],
                                        (me[0],) + (0,) * own.ndim)

    gat_in, _ = _exchange_start(
        [_cast_to_slot(w_in[0], me, 256, "cast_w_in"), in_slot(conv_w[0])], [True, True], _ALL_PEERS, "gather_in_start")
    (_, ag_in), (_, ag_conv) = _exchange_wait(gat_in, me, "gather_in_wait")
    gat_mlp, tok_mlp = _exchange_start(
        [_cast_to_slot(w_out[0], me, 128, "cast_w_out"), _cast_to_slot(w_up[0], me, 256, "cast_w_up"),
         _cast_to_slot(w_down[0], me, 256, "cast_w_down")], [True] * 3, _ALL_PEERS, "gather_mlp_start", dep=ag_conv)
    w_in_p = _pad_lanes(ag_in.transpose(1, 0, 2).reshape(D_MODEL, IN_COLS), IN_PAD)
    conv_w_f = ag_conv.transpose(1, 0, 2).reshape(4, CONV_CH)

    def late_weights(after):
        (_, ag_out), (_, ag_up), (_, ag_down) = _exchange_wait(gat_mlp, after, "gather_mlp_wait")
        return ag_out.reshape(D_MODEL, D_MODEL), ag_up, ag_down.reshape(D_FF, D_MODEL)

    sent = {}

    def after_mlp_grads(g_w_down, g_w_up):
        sent["mlp"], tok = _exchange_start(
            [g_w_down.reshape(N_DEV, D_FF // N_DEV, D_MODEL), g_w_up], [False, False], _ALL_PEERS, "grads_mlp_start")
        return tok

    small = {k: w[k][0] for k in _SMALL_ROW_PARAMS + ("gm_w_s",)}
    small["norm_mix_pre"] = small["norm_mix_pre"] + tok_mlp[0:1, 0:1]
    loss_part, grad_x, g = _local_step(
        x.reshape(n_batch * seq, D_MODEL), loss_target.reshape(n_batch * seq, D_MODEL), seq, w_in_p, conv_w_f, small,
        late_weights, after_mlp_grads)
    loss = lax.psum(loss_part, ("x", "y", "c"))

    g_in_blk = g["w_in"][:, :IN_COLS].reshape(D_MODEL, N_DEV, shard_in).transpose(1, 0, 2)
    g_conv_blk = g["conv_w"].reshape(4, N_DEV, CONV_CH // N_DEV).transpose(1, 0, 2)
    sent_rest, tok_rest = _exchange_start(
        [g_in_blk, g["w_out"].reshape(N_DEV, D_MODEL // N_DEV, D_MODEL), g_conv_blk,
         in_slot(_pack_rows([g[k] for k in _SMALL_ROW_PARAMS])), in_slot(g["gm_w_s"])],
        [False, False, False, True, True], _ALL_PEERS, "grads_rest_start")
    (own_down, p_down), (own_up, p_up) = _exchange_wait(sent["mlp"], tok_rest, "grads_mlp_wait")
    res = {}
    res["w_up"] = _adamw_reduce(p_up, own_up, me, w_up[0], m_w_up[0], v_w_up[0], 256, "adamw_w_up")
    res["w_down"] = _adamw_reduce(p_down, own_down, me, w_down[0], m_w_down[0], v_w_down[0], 128, "adamw_w_down")
    (own_in, p_in), (own_out, p_out), (own_conv, p_conv), (_, p_rows), (_, p_ws) = _exchange_wait(
        sent_rest, res["w_down"][1], "grads_rest_wait")

    res["w_in"] = _adamw_reduce(p_in, own_in, me, w_in[0], m_w_in[0], v_w_in[0], 256, "adamw_w_in")
    res["w_out"] = _adamw_reduce(p_out, own_out, me, w_out[0], m_w_out[0], v_w_out[0], 128, "adamw_w_out")
    res["conv_w"] = _adamw_small(p_conv, own_conv, me, conv_w[0], m_conv_w[0], v_conv_w[0], None, "adamw_conv_w")
    causal = jnp.tril(jnp.ones((1, CHUNK, CHUNK), F32))
    res["gm_w_s"] = _adamw_small(p_ws, None, me, gm_w_s[0], m_gm_w_s[0], v_gm_w_s[0], causal, "adamw_gm_w_s")
    rows = _adamw_small(p_rows, None, me, _pack_rows([w[k] for k in _SMALL_ROW_PARAMS]),
                        _pack_rows([m[k] for k in _SMALL_ROW_PARAMS]), _pack_rows([v[k] for k in _SMALL_ROW_PARAMS]),
                        None, "adamw_rows")
    for i, k in enumerate(_SMALL_ROW_PARAMS):
        size = int(np.prod(w[k].shape))
        res[k] = tuple(r[i, :size].reshape(w[k].shape) for r in rows)
    for k in ("w_in", "w_out", "w_up", "w_down", "conv_w", "gm_w_s"):
        res[k] = tuple(r.reshape(w[k].shape) for r in res[k])

    outs = [loss, grad_x.reshape(x.shape)]
    for part in range(4):
        outs.extend(res[k][part] for k in _WEIGHTS)
    return tuple(outs)
```

```python
import functools

import jax
import jax.numpy as jnp
import numpy as np
from jax import lax
from jax.experimental import pallas as pl
from jax.experimental.pallas import tpu as pltpu

F32 = jnp.float32
BF16 = jnp.bfloat16

D_MODEL = 1024
GM_WIDTH = 512
SSM_WIDTH = 512
CONV_CH = 1024
N_HEADS = 8
HEAD_DIM = 64
N_STATE = 128
CHUNK = 128
D_FF = 4096
IN_COLS = 2568
IN_PAD = 2688
N_DEV = 8
EPS = 1e-6
ADAM_LR, ADAM_B1, ADAM_B2, ADAM_EPS, ADAM_WD, ADAM_STEP = 0.001, 0.9, 0.999, 1e-08, 0.01, 10
VMEM_LIMIT_BYTES = 56 * 1024 * 1024
SMALL_ROWS = 16

_NT = (((1,), (1,)), ((), ()))
_TN = (((0,), (0,)), ((), ()))


def _params(*sem):
    return pltpu.CompilerParams(dimension_semantics=sem or None, vmem_limit_bytes=VMEM_LIMIT_BYTES)


def _dot(a, b, dims=None):
    if dims is None:
        return jnp.dot(a, b, preferred_element_type=F32)
    return lax.dot_general(a, b, dims, preferred_element_type=F32)


def _split_terms(x, terms):
    out, rem = [], x
    for i in range(terms):
        hi = rem.astype(BF16)
        out.append(hi)
        if i + 1 < terms:
            rem = rem - hi.astype(F32)
    return out


def _split_dot(x, m, terms):
    acc = None
    for hi in _split_terms(x, terms):
        part = _dot(hi, m)
        acc = part if acc is None else acc + part
    return acc


def _split_dot_left(m, x, terms):
    acc = None
    for hi in _split_terms(x, terms):
        part = _dot(m, hi)
        acc = part if acc is None else acc + part
    return acc


def _gelu_and_grad(x):
    c = 0.7978845608028654
    inner = c * (x + 0.044715 * x * x * x)
    t = jnp.tanh(inner)
    g = 0.5 * x * (1.0 + t)
    dg = 0.5 * (1.0 + t) + 0.5 * x * (1.0 - t * t) * c * (1.0 + 3.0 * 0.044715 * x * x)
    return g, dg


def _softplus(x):
    return jnp.maximum(x, 0.0) + jnp.log(1.0 + jnp.exp(-jnp.abs(x)))


def _rsum(x):
    return jnp.sum(x, axis=0, keepdims=True)


def _acc_rows(ref, part, first):
    val = jnp.broadcast_to(part, ref.shape)

    @pl.when(first)
    def _():
        ref[...] = val

    @pl.when(jnp.logical_not(first))
    def _():
        ref[...] += val


def _rms_bwd(n, g, dout):
    r = lax.rsqrt(jnp.mean(n * n, axis=-1, keepdims=True) + EPS)
    nh = n * r
    dg = dout * g
    dn = r * (dg - nh * jnp.mean(dg * nh, axis=-1, keepdims=True))
    return dn, _rsum(dout * nh)


def _const_mats():
    avg = np.kron(np.eye(N_HEADS), np.full((HEAD_DIM, HEAD_DIM), 1.0 / HEAD_DIM))
    expand = np.zeros((CHUNK, SSM_WIDTH), np.float32)
    for h in range(N_HEADS):
        expand[h, h * HEAD_DIM:(h + 1) * HEAD_DIM] = 1.0
    tril = np.tril(np.ones((CHUNK, CHUNK), np.float32))
    as_bf16 = lambda a: jnp.asarray(a, dtype=BF16)
    return as_bf16(avg), as_bf16(expand), as_bf16(expand.T), as_bf16(tril), as_bf16(tril.T)


def _full(shape):
    nd = len(shape)
    return pl.BlockSpec(shape, lambda *_: (0,) * nd)


_HBM = pl.BlockSpec(memory_space=pltpu.HBM)
_SEM = pl.BlockSpec(memory_space=pltpu.SEMAPHORE)
_ALL_PEERS = tuple(range(1, N_DEV))


def _peer_of(k):
    x, y, c = lax.axis_index("x"), lax.axis_index("y"), lax.axis_index("c")
    px = 1 - x if k & 4 else x
    py = 1 - y if k & 2 else y
    pc = 1 - c if k & 1 else c
    return (px, py, pc), 4 * px + 2 * py + pc


def _copies(src, land, send_sems, recv_sems, peers):
    x, y, c = lax.axis_index("x"), lax.axis_index("y"), lax.axis_index("c")
    me = 4 * x + 2 * y + c
    out = []
    for t in range(len(src)):
        for i, k in enumerate(peers):
            pos, peer = _peer_of(k)
            sem = t * len(peers) + i
            mk = functools.partial(pltpu.make_async_remote_copy, send_sem=send_sems.at[sem], recv_sem=recv_sems.at[sem],
                                   device_id=pos, device_id_type=pl.DeviceIdType.MESH)
            if land[t] is None:
                mine = functools.partial(mk, src_ref=src[t].at[me], dst_ref=src[t].at[me])
                theirs = functools.partial(mk, src_ref=src[t].at[peer], dst_ref=src[t].at[peer])
            else:
                mine = functools.partial(mk, src_ref=src[t].at[peer], dst_ref=land[t].at[me])
                theirs = functools.partial(mk, src_ref=src[t].at[peer], dst_ref=land[t].at[peer])
            out.append((mine, theirs))
    return out


def _exchange_start(srcs, inplace, peers, name, dep=None):
    n = len(srcs)
    lands = [None if ip else pltpu.with_memory_space_constraint(lax.empty(s.shape, s.dtype), pltpu.HBM)
             for s, ip in zip(srcs, inplace)]
    real_lands = [l for l in lands if l is not None]
    n_l = len(real_lands)
    deps = [] if dep is None else [dep]

    def body(*refs):
        src = refs[:n]
        land_refs = list(refs[n:n + n_l])
        send_sems, recv_sems = refs[n + n_l + len(deps)], refs[n + n_l + len(deps) + 1]
        token = refs[-1]
        land = [None if ip else land_refs.pop(0) for ip in inplace]
        for mine, _ in _copies(src, land, send_sems, recv_sems, peers):
            mine().start()
        token[...] = jnp.zeros_like(token)

    sem_t = pltpu.SemaphoreType.DMA((n * len(peers),))
    outs = pl.pallas_call(
        body, name=name,
        out_shape=(sem_t, sem_t) + tuple(pltpu.HBM(a.shape, a.dtype) for a in list(srcs) + real_lands)
        + (jax.ShapeDtypeStruct((8, 128), F32),),
        in_specs=[_HBM] * (n + n_l) + [pl.BlockSpec(memory_space=pl.ANY)] * len(deps),
        out_specs=(_SEM, _SEM) + (_HBM,) * (n + n_l) + (pl.BlockSpec(memory_space=pltpu.VMEM),),
        input_output_aliases={i: 2 + i for i in range(n + n_l)},
        compiler_params=pltpu.CompilerParams(has_side_effects=pltpu.SideEffectType.DATAFLOW_SIDE_EFFECTING),
    )(*[pltpu.with_memory_space_constraint(s, pltpu.HBM) for s in srcs], *real_lands, *deps)
    handle = dict(send=outs[0], recv=outs[1], srcs=outs[2:2 + n], lands=outs[2 + n:2 + n + n_l], inplace=inplace,
                  peers=peers)
    return handle, outs[-1]


def _exchange_wait(handle, after, name):
    srcs, lands, inplace, peers = handle["srcs"], handle["lands"], handle["inplace"], handle["peers"]
    n, n_l = len(srcs), len(lands)

    def body(*refs):
        src = refs[:n]
        land_refs = list(refs[n:n + n_l])
        send_sems, recv_sems = refs[n + n_l], refs[n + n_l + 1]
        land = [None if ip else land_refs.pop(0) for ip in inplace]
        for mine, theirs in _copies(src, land, send_sems, recv_sems, peers):
            mine().wait_send()
            theirs().wait_recv()

    outs = pl.pallas_call(
        body, name=name, out_shape=tuple(pltpu.HBM(a.shape, a.dtype) for a in list(srcs) + list(lands)),
        in_specs=[_HBM] * (n + n_l) + [_SEM, _SEM, pl.BlockSpec(memory_space=pl.ANY)],
        out_specs=(_HBM,) * (n + n_l), input_output_aliases={i: i for i in range(n + n_l)},
        compiler_params=pltpu.CompilerParams(has_side_effects=pltpu.SideEffectType.DATAFLOW_SIDE_EFFECTING),
    )(*srcs, *lands, handle["send"], handle["recv"], after)
    res, land_out = [], list(outs[n:])
    for t in range(n):
        res.append((outs[t], outs[t] if inplace[t] else land_out.pop(0)))
    return res


def _cast_to_slot(w, me, rows, name):
    r, cdim = w.shape

    def body(me_ref, w_ref, o_ref):
        o_ref[0] = w_ref[...].astype(BF16)

    return pl.pallas_call(
        body, name=name, out_shape=jax.ShapeDtypeStruct((N_DEV, r, cdim), BF16),
        grid_spec=pltpu.PrefetchScalarGridSpec(
            num_scalar_prefetch=1, grid=(r // rows,), in_specs=[pl.BlockSpec((rows, cdim), lambda i, me_ref: (i, 0))],
            out_specs=pl.BlockSpec((1, rows, cdim), lambda i, me_ref: (me_ref[0], i, 0))),
        compiler_params=_params("parallel"))(me, w)


def _adamw_math(w, g, m, v):
    m = ADAM_B1 * m + (1.0 - ADAM_B1) * g
    v = ADAM_B2 * v + (1.0 - ADAM_B2) * (g * g)
    m_hat = m / (1.0 - ADAM_B1 ** ADAM_STEP)
    v_hat = v / (1.0 - ADAM_B2 ** ADAM_STEP)
    delta = -ADAM_LR * (m_hat / (jnp.sqrt(v_hat) + ADAM_EPS) + ADAM_WD * w)
    return delta, m, v


def _sum_parts(me, p_ref, own):
    g = None
    for j in range(N_DEV):
        term = (p_ref[j] if own is None else jnp.where(me == j, own, p_ref[j])).astype(F32)
        g = term if g is None else g + term
    return g


def _adamw_reduce(parts, own, me, w, m, v, rows, name):
    r, cdim = w.shape

    def body(me_ref, p_ref, own_ref, w_ref, m_ref, v_ref, g_out, d_out, m_out, v_out):
        g = _sum_parts(me_ref[0], p_ref, own_ref[0])
        d, mn, vn = _adamw_math(w_ref[...], g, m_ref[...], v_ref[...])
        g_out[...] = g
        d_out[...] = d
        m_out[...] = mn
        v_out[...] = vn

    blk = pl.BlockSpec((rows, cdim), lambda i, me_ref: (i, 0))
    sds = jax.ShapeDtypeStruct(w.shape, F32)
    return pl.pallas_call(
        body, name=name, out_shape=(sds,) * 4,
        grid_spec=pltpu.PrefetchScalarGridSpec(
            num_scalar_prefetch=1, grid=(r // rows,),
            in_specs=[pl.BlockSpec((N_DEV, rows, cdim), lambda i, me_ref: (0, i, 0)),
                      pl.BlockSpec((1, rows, cdim), lambda i, me_ref: (me_ref[0], i, 0)), blk, blk, blk],
            out_specs=(blk,) * 4),
        compiler_params=_params("parallel"))(me, parts, own, w, m, v)


def _adamw_small(parts, own, me, w, m, v, mask, name):
    def body(me_ref, *refs):
        refs = list(refs)
        p_ref = refs.pop(0)
        own_ref = None if own is None else refs.pop(0)
        w_ref, m_ref, v_ref = refs[:3]
        k_ref = None if mask is None else refs[3]
        g_out, d_out, m_out, v_out = refs[-4:]
        g = _sum_parts(me_ref[0], p_ref, None if own is None else own_ref[me_ref[0]])
        if mask is not None:
            g = g * k_ref[...]
        d, mn, vn = _adamw_math(w_ref[...], g, m_ref[...], v_ref[...])
        g_out[...] = g
        d_out[...] = d
        m_out[...] = mn
        v_out[...] = vn

    def whole(shape):
        nd = len(shape)
        return pl.BlockSpec(shape, lambda i, me_ref: (0,) * nd)

    sds = jax.ShapeDtypeStruct(w.shape, F32)
    ins = [parts] + ([] if own is None else [own]) + [w, m, v] + ([] if mask is None else [mask])
    return pl.pallas_call(
        body, name=name, out_shape=(sds,) * 4,
        grid_spec=pltpu.PrefetchScalarGridSpec(
            num_scalar_prefetch=1, grid=(1,), in_specs=[whole(a.shape) for a in ins],
            out_specs=(whole(w.shape),) * 4),
        compiler_params=_params("arbitrary"))(me, *ins)


_IN_SPLITS = ((0, 512), (512, 1024), (1024, 1536), (1536, 2560), (2560, IN_PAD))


def _in_proj(x, g1, w_in, tm):
    t_tok = x.shape[0]

    def body(x_ref, g_ref, w_ref, h_ref, *outs):
        xv = x_ref[...]
        r = lax.rsqrt(jnp.mean(xv * xv, axis=-1, keepdims=True) + EPS)
        h = (xv * r * g_ref[...]).astype(BF16)
        h_ref[...] = h
        for (a, b), o_ref in zip(_IN_SPLITS, outs):
            o_ref[...] = _dot(h, w_ref[:, a:b])

    row = lambda n: pl.BlockSpec((tm, n), lambda i: (i, 0))
    widths = [b - a for a, b in _IN_SPLITS]
    return pl.pallas_call(
        body, name="in_proj", grid=(t_tok // tm,),
        out_shape=(jax.ShapeDtypeStruct((t_tok, D_MODEL), BF16),) + tuple(
            jax.ShapeDtypeStruct((t_tok, n), F32) for n in widths),
        in_specs=[row(D_MODEL), _full((1, D_MODEL)), _full((D_MODEL, IN_PAD))],
        out_specs=(row(D_MODEL),) + tuple(row(n) for n in widths),
        compiler_params=_params("parallel"))(x, g1, w_in)


def _lane_masks():
    lane = lax.broadcasted_iota(jnp.int32, (1, 2 * HEAD_DIM), 1)
    left = (lane < HEAD_DIM).astype(F32)
    return left, 1.0 - left


def _stack_pair(v, m_l, m_r):
    return jnp.concatenate([v * m_l, v * m_r], axis=0).astype(BF16)


def _gmlp_common(u, v, lnw, lnb, avg, wcat_ref, bias, m_l, m_r):
    ug, dug = _gelu_and_grad(u)
    vg, dvg = _gelu_and_grad(v)
    mu = _split_dot(vg, avg, 2)
    vc = vg - mu
    var = _split_dot(vc * vc, avg, 2)
    rstd = lax.rsqrt(var + EPS)
    vhat = vc * rstd
    vn = vhat * lnw + lnb
    cols = []
    for j in range(N_HEADS // 2):
        cols.append(_dot(wcat_ref[j], _stack_pair(vn[:, 128 * j:128 * (j + 1)], m_l, m_r)))
    mixed = jnp.concatenate(cols, axis=1) + bias
    return ug, dug, dvg, rstd, vhat, vn, mixed


def _gmlp_fwd(u, v, lnw, lnb, wcat, bias, avg):
    t_tok = u.shape[0]

    def body(u_ref, v_ref, lnw_ref, lnb_ref, wcat_ref, bias_ref, avg_ref, o_ref):
        m_l, m_r = _lane_masks()
        ug, _, _, _, _, _, mixed = _gmlp_common(
            u_ref[...], v_ref[...], lnw_ref[...], lnb_ref[...], avg_ref[...], wcat_ref, bias_ref[...], m_l, m_r)
        o_ref[...] = (ug * mixed).astype(BF16)

    row = pl.BlockSpec((CHUNK, GM_WIDTH), lambda i: (i, 0))
    return pl.pallas_call(
        body, name="gmlp_fwd", grid=(t_tok // CHUNK,), out_shape=jax.ShapeDtypeStruct((t_tok, GM_WIDTH), BF16),
        in_specs=[row, row, _full((1, GM_WIDTH)), _full((1, GM_WIDTH)), _full(wcat.shape), _full(bias.shape),
                  _full(avg.shape)],
        out_specs=row, compiler_params=_params("parallel"))(u, v, lnw, lnb, wcat, bias, avg)


def _ssd_common(xext_ref, dtr, cw_ref, cb, dtb, alog, expand, tril):
    q = CHUNK
    taps = [xext_ref[pl.ds(5 + k, q), :] for k in range(4)]
    pre = cb + cw_ref[0:1, :] * taps[0] + cw_ref[1:2, :] * taps[1] + cw_ref[2:3, :] * taps[2] + cw_ref[3:4, :] * taps[3]
    sg = jax.nn.sigmoid(pre)
    act = pre * sg
    lane = lax.broadcasted_iota(jnp.int32, (1, CHUNK), 1)
    a_row = jnp.where(lane < N_HEADS, -jnp.exp(alog), 0.0)
    dtp = dtr + dtb
    dt = _softplus(dtp)
    a_cs = _split_dot_left(tril, dt * a_row, 3)
    a_cs_t = a_cs.T
    dt_exp = _split_dot(dt, expand, 3)
    a_exp = _split_dot(a_cs, expand, 3)
    a_end = a_exp[q - 1:q, :]
    li = lax.broadcasted_iota(jnp.int32, (q, q), 0)
    si = lax.broadcasted_iota(jnp.int32, (q, q), 1)
    causal = si <= li
    decay = []
    for h in range(N_HEADS):
        seg = a_cs[:, h:h + 1] - a_cs_t[h:h + 1, :]
        decay.append(jnp.where(causal, jnp.exp(jnp.minimum(seg, 0.0)), 0.0))
    return dict(taps=taps, pre=pre, sg=sg, act=act, a_row=a_row, dtp=dtp, dt=dt, dt_exp=dt_exp, a_exp=a_exp,
                e=jnp.exp(a_exp), w_end=jnp.exp(a_end - a_exp), cd=jnp.exp(a_end), decay=decay)


def _ssd_specs(t_tok, seq, reverse):
    nc = seq // CHUNK

    def chunk(b, c):
        return b * nc + (nc - 1 - c if reverse else c)

    def row(n):
        return pl.BlockSpec((CHUNK, n), lambda b, c: (chunk(b, c), 0))

    tail = pl.BlockSpec((8, CONV_CH), lambda b, c: (jnp.maximum(chunk(b, c) * (CHUNK // 8) - 1, 0), 0))
    return nc, chunk, row, tail


def _fill_xext(xext_ref, tail_ref, xbc_ref, first_chunk):
    xext_ref[0:8, :] = jnp.where(first_chunk, 0.0, tail_ref[...])
    xext_ref[8:8 + CHUNK, :] = xbc_ref[...]


def _ssd_fwd(z, xbc, dtr, cw, cb, dtb, alog, dskip_exp, nw, expand, tril, seq):
    t_tok = z.shape[0]
    nc, chunk, row, tail = _ssd_specs(t_tok, seq, False)

    def body(z_ref, xbc_ref, tail_ref, dtr_ref, cw_ref, cb_ref, dtb_ref, alog_ref, dsk_ref, nw_ref, exp_ref,
             tril_ref, o_ref, y_ref, st_ref, xext_ref, state_ref):
        c = pl.program_id(1)

        @pl.when(c == 0)
        def _():
            state_ref[...] = jnp.zeros_like(state_ref)

        _fill_xext(xext_ref, tail_ref, xbc_ref, c == 0)
        m_l, m_r = _lane_masks()
        f = _ssd_common(xext_ref, dtr_ref[...], cw_ref, cb_ref[...], dtb_ref[...], alog_ref[...], exp_ref[...],
                        tril_ref[...])
        act = f["act"]
        xs = act[:, :SSM_WIDTH]
        xdt = xs * f["dt_exp"]
        xw = xdt * f["w_end"]
        state = state_ref[...]
        st_ref[0] = state
        ydiag, yoff, snew = [], [], []
        for g in range(2):
            bg = act[:, 512 + 128 * g:640 + 128 * g].astype(BF16)
            cg = act[:, 768 + 128 * g:896 + 128 * g].astype(BF16)
            cb_mat = _dot(cg, bg, _NT)
            for pr in range(2):
                h0 = 4 * g + 2 * pr
                gcat = jnp.concatenate(
                    [(cb_mat * f["decay"][h0]).astype(BF16), (cb_mat * f["decay"][h0 + 1]).astype(BF16)], axis=1)
                ydiag.append(_dot(gcat, _stack_pair(xdt[:, 64 * h0:64 * h0 + 128], m_l, m_r)))
            yoff.append(_dot(cg, state[:, 256 * g:256 * (g + 1)].astype(BF16)))
            snew.append(_dot(bg, xw[:, 256 * g:256 * (g + 1)].astype(BF16), _TN))
        y = jnp.concatenate(ydiag, axis=1) + f["e"] * jnp.concatenate(yoff, axis=1) + dsk_ref[...] * xs
        state_ref[...] = state * f["cd"] + jnp.concatenate(snew, axis=1)
        y_ref[...] = y
        zv = z_ref[...]
        yg = y * (zv * jax.nn.sigmoid(zv))
        outs = []
        for g in range(2):
            ygg = yg[:, 256 * g:256 * (g + 1)]
            outs.append(ygg * lax.rsqrt(jnp.mean(ygg * ygg, axis=-1, keepdims=True) + EPS))
        o_ref[...] = (jnp.concatenate(outs, axis=1) * nw_ref[...]).astype(BF16)

    consts = [cw, cb, dtb, alog, dskip_exp, nw, expand, tril]
    return pl.pallas_call(
        body, name="ssd_fwd", grid=(t_tok // seq, nc),
        out_shape=(jax.ShapeDtypeStruct((t_tok, SSM_WIDTH), BF16), jax.ShapeDtypeStruct((t_tok, SSM_WIDTH), F32),
                   jax.ShapeDtypeStruct((t_tok // CHUNK, N_STATE, SSM_WIDTH), F32)),
        in_specs=[row(SSM_WIDTH), row(CONV_CH), tail, row(CHUNK)] + [_full(a.shape) for a in consts],
        out_specs=(row(SSM_WIDTH), row(SSM_WIDTH),
                   pl.BlockSpec((1, N_STATE, SSM_WIDTH), lambda b, c: (chunk(b, c), 0, 0))),
        scratch_shapes=[pltpu.VMEM((CHUNK + 16, CONV_CH), F32), pltpu.VMEM((N_STATE, SSM_WIDTH), F32)],
        compiler_params=_params("arbitrary", "arbitrary"))(z, xbc, xbc, dtr, *consts)


def _out_proj(mix_a, mix_b, w_out, x, g2, g3, tm):
    t_tok = x.shape[0]

    def body(a_ref, b_ref, w_ref, x_ref, g2_ref, g3_ref, o_ref, x2_ref, h3_ref, mix_ref):
        o = _dot(a_ref[...], w_ref[0:GM_WIDTH, :]) + _dot(b_ref[...], w_ref[GM_WIDTH:, :])
        o_ref[...] = o
        mix_ref[:, 0:GM_WIDTH] = a_ref[...]
        mix_ref[:, GM_WIDTH:] = b_ref[...]
        r2 = lax.rsqrt(jnp.mean(o * o, axis=-1, keepdims=True) + EPS)
        x2 = x_ref[...] + o * r2 * g2_ref[...]
        x2_ref[...] = x2
        r3 = lax.rsqrt(jnp.mean(x2 * x2, axis=-1, keepdims=True) + EPS)
        h3_ref[...] = (x2 * r3 * g3_ref[...]).astype(BF16)

    row = lambda n: pl.BlockSpec((tm, n), lambda i: (i, 0))
    sd = lambda dt: jax.ShapeDtypeStruct((t_tok, D_MODEL), dt)
    return pl.pallas_call(
        body, name="out_proj", grid=(t_tok // tm,), out_shape=(sd(F32), sd(F32), sd(BF16), sd(BF16)),
        in_specs=[row(GM_WIDTH), row(SSM_WIDTH), _full((D_MODEL, D_MODEL)), row(D_MODEL), _full((1, D_MODEL)),
                  _full((1, D_MODEL))],
        out_specs=(row(D_MODEL),) * 4, compiler_params=_params("parallel"))(mix_a, mix_b, w_out, x, g2, g3)


def _mlp_fwd(h3, w_up_blk, w_down, x2, target, g4, tm):
    t_tok = x2.shape[0]
    tf = w_up_blk.shape[2]
    nf = w_up_blk.shape[0]

    def body(h_ref, wu_ref, wd_ref, x2_ref, t_ref, g4_ref, ra_ref, dd_ref, dy_ref, dg4_ref, loss_ref, acc_ref):
        i, j = pl.program_id(0), pl.program_id(1)
        ra = jnp.maximum(_dot(h_ref[...], wu_ref[0]), 0.0).astype(BF16)
        ra_ref[...] = ra
        part = _dot(ra * ra, wd_ref[...])

        @pl.when(j == 0)
        def _():
            acc_ref[...] = part

        @pl.when(j > 0)
        def _():
            acc_ref[...] += part

        @pl.when(j == nf - 1)
        def _():
            dvec = acc_ref[...]
            r4 = lax.rsqrt(jnp.mean(dvec * dvec, axis=-1, keepdims=True) + EPS)
            dn = dvec * r4
            g4 = g4_ref[...]
            err = x2_ref[...] + dn * g4 - t_ref[...]
            dy = err * (1.0 / D_MODEL)
            dy_ref[...] = dy
            dg = dy * g4
            dd_ref[...] = (r4 * (dg - dn * jnp.mean(dg * dn, axis=-1, keepdims=True))).astype(BF16)
            _acc_rows(dg4_ref, _rsum(dy * dn), i == 0)
            tile_loss = 0.5 * jnp.sum(jnp.sum(err * err, axis=-1, keepdims=True), axis=0, keepdims=True) / D_MODEL
            _acc_rows(loss_ref, jnp.broadcast_to(tile_loss, (1, 128)), i == 0)

    row = pl.BlockSpec((tm, D_MODEL), lambda i, j: (i, 0))
    return pl.pallas_call(
        body, name="mlp_fwd", grid=(t_tok // tm, nf),
        out_shape=(jax.ShapeDtypeStruct((t_tok, D_FF), BF16), jax.ShapeDtypeStruct((t_tok, D_MODEL), BF16),
                   jax.ShapeDtypeStruct((t_tok, D_MODEL), F32), jax.ShapeDtypeStruct((8, D_MODEL), F32),
                   jax.ShapeDtypeStruct((8, 128), F32)),
        in_specs=[row, pl.BlockSpec((1, D_MODEL, tf), lambda i, j: (j, 0, 0)),
                  pl.BlockSpec((tf, D_MODEL), lambda i, j: (j, 0)), row, row, _full((1, D_MODEL))],
        out_specs=(pl.BlockSpec((tm, tf), lambda i, j: (i, j)), row, row, _full((8, D_MODEL)), _full((8, 128))),
        scratch_shapes=[pltpu.VMEM((tm, D_MODEL), F32)],
        compiler_params=_params("arbitrary", "arbitrary"))(h3, w_up_blk, w_down, x2, target, g4)


def _mlp_bwd(dd, w_down, ra, w_up_blk, x2, dy, o, g3, g2, tm):
    t_tok = x2.shape[0]
    tf = w_up_blk.shape[2]
    nf = w_up_blk.shape[0]

    def body(dd_ref, wd_ref, ra_ref, wu_ref, x2_ref, dy_ref, o_ref, g3_ref, g2_ref, da_ref, dx2_ref, do_ref, dg3_ref,
             dg2_ref, acc_ref):
        i, j = pl.program_id(0), pl.program_id(1)
        df = _dot(dd_ref[...], wd_ref[...], _NT)
        da = (df * (2.0 * ra_ref[...].astype(F32))).astype(BF16)
        da_ref[...] = da
        part = _dot(da, wu_ref[0], _NT)

        @pl.when(j == 0)
        def _():
            acc_ref[...] = part

        @pl.when(j > 0)
        def _():
            acc_ref[...] += part

        @pl.when(j == nf - 1)
        def _():
            dn3, dg3 = _rms_bwd(x2_ref[...], g3_ref[...], acc_ref[...])
            dx2 = dy_ref[...] + dn3
            dx2_ref[...] = dx2
            do, dg2 = _rms_bwd(o_ref[...], g2_ref[...], dx2)
            do_ref[...] = do.astype(BF16)
            _acc_rows(dg3_ref, dg3, i == 0)
            _acc_rows(dg2_ref, dg2, i == 0)

    row = pl.BlockSpec((tm, D_MODEL), lambda i, j: (i, 0))
    vec = _full((1, D_MODEL))
    acc = _full((8, D_MODEL))
    sd = lambda dt: jax.ShapeDtypeStruct((t_tok, D_MODEL), dt)
    return pl.pallas_call(
        body, name="mlp_bwd", grid=(t_tok // tm, nf),
        out_shape=(jax.ShapeDtypeStruct((t_tok, D_FF), BF16), sd(F32), sd(BF16),
                   jax.ShapeDtypeStruct((8, D_MODEL), F32), jax.ShapeDtypeStruct((8, D_MODEL), F32)),
        in_specs=[row, pl.BlockSpec((tf, D_MODEL), lambda i, j: (j, 0)), pl.BlockSpec((tm, tf), lambda i, j: (i, j)),
                  pl.BlockSpec((1, D_MODEL, tf), lambda i, j: (j, 0, 0)), row, row, row, vec, vec],
        out_specs=(pl.BlockSpec((tm, tf), lambda i, j: (i, j)), row, row, acc, acc),
        scratch_shapes=[pltpu.VMEM((tm, D_MODEL), F32)],
        compiler_params=_params("arbitrary", "arbitrary"))(dd, w_down, ra, w_up_blk, x2, dy, o, g3, g2)


def _wgrad(a, b, out_blocks, bm, bn, bk, square_a, name, dep=None):
    t_tok, m = a.shape
    n = b.shape[1]
    nk = t_tok // bk

    def body(a_ref, b_ref, *rest):
        o_ref, acc_ref = rest[-2:]
        k = pl.program_id(2)
        av = a_ref[...]
        if square_a:
            av = av * av
        part = _dot(av, b_ref[...], _TN)

        @pl.when(k == 0)
        def _():
            acc_ref[...] = part

        @pl.when(k > 0)
        def _():
            acc_ref[...] += part

        @pl.when(k == nk - 1)
        def _():
            res = acc_ref[...].astype(BF16)
            if out_blocks is None:
                o_ref[...] = res
            else:
                o_ref[0] = res

    if out_blocks is None:
        out_shape = jax.ShapeDtypeStruct((m, n), BF16)
        out_spec = pl.BlockSpec((bm, bn), lambda i, j, k: (i, j))
    else:
        assert n // out_blocks == bn
        out_shape = jax.ShapeDtypeStruct((out_blocks, m, bn), BF16)
        out_spec = pl.BlockSpec((1, bm, bn), lambda i, j, k: (j, i, 0))
    deps = [] if dep is None else [dep]
    return pl.pallas_call(
        body, name=name, grid=(m // bm, n // bn, nk), out_shape=out_shape,
        in_specs=[pl.BlockSpec((bk, bm), lambda i, j, k: (k, i)), pl.BlockSpec((bk, bn), lambda i, j, k: (k, j))]
        + [pl.BlockSpec(memory_space=pl.ANY)] * len(deps),
        out_specs=out_spec, scratch_shapes=[pltpu.VMEM((bm, bn), F32)],
        compiler_params=_params("parallel", "parallel", "arbitrary"))(a, b, *deps)


def _wgrad_in(h1, pieces, bm, bk, dep=None):
    t_tok = h1.shape[0]
    nk = t_tok // bk
    widths = [b - a for a, b in _IN_SPLITS]

    def body(h_ref, *rest):
        piece_refs = rest[:len(widths)]
        o_ref, acc_ref = rest[-2:]
        k = pl.program_id(1)
        dproj = jnp.concatenate([r[...] for r in piece_refs], axis=1)
        part = _dot(h_ref[...], dproj, _TN)

        @pl.when(k == 0)
        def _():
            acc_ref[...] = part

        @pl.when(k > 0)
        def _():
            acc_ref[...] += part

        @pl.when(k == nk - 1)
        def _():
            o_ref[...] = acc_ref[...].astype(BF16)

    deps = [] if dep is None else [dep]
    return pl.pallas_call(
        body, name="wgrad_in", grid=(D_MODEL // bm, nk), out_shape=jax.ShapeDtypeStruct((D_MODEL, IN_PAD), BF16),
        in_specs=[pl.BlockSpec((bk, bm), lambda i, k: (k, i))] + [pl.BlockSpec((bk, n), lambda i, k: (k, 0)) for n in widths]
        + [pl.BlockSpec(memory_space=pl.ANY)] * len(deps),
        out_specs=pl.BlockSpec((bm, IN_PAD), lambda i, k: (i, 0)), scratch_shapes=[pltpu.VMEM((bm, IN_PAD), F32)],
        compiler_params=_params("parallel", "arbitrary"))(h1, *pieces, *deps)


def _dmix(do, w_out, tm, dep=None):
    t_tok = do.shape[0]

    def body(d_ref, w_ref, *rest):
        rest[-1][...] = _dot(d_ref[...], w_ref[...], _NT)

    row = pl.BlockSpec((tm, D_MODEL), lambda i: (i, 0))
    deps = [] if dep is None else [dep]
    return pl.pallas_call(
        body, name="dmix", grid=(t_tok // tm,), out_shape=jax.ShapeDtypeStruct((t_tok, D_MODEL), F32),
        in_specs=[row, _full((D_MODEL, D_MODEL))] + [pl.BlockSpec(memory_space=pl.ANY)] * len(deps), out_specs=row,
        compiler_params=_params("parallel"))(do, w_out, *deps)


def _gmlp_bwd(dmix, u, v, lnw, lnb, wcat, wtcat, bias, avg, expand_t):
    t_tok = u.shape[0]

    def body(dm_ref, u_ref, v_ref, lnw_ref, lnb_ref, wcat_ref, wtcat_ref, bias_ref, avg_ref, expt_ref, du_ref, dv_ref,
             dw_ref, db_ref, dlnw_ref, dlnb_ref):
        i = pl.program_id(0)
        m_l, m_r = _lane_masks()
        avg = avg_ref[...]
        lnw = lnw_ref[...]
        ug, dug, dvg, rstd, vhat, vn, mixed = _gmlp_common(
            u_ref[...], v_ref[...], lnw, lnb_ref[...], avg, wcat_ref, bias_ref[...], m_l, m_r)
        dya = dm_ref[...]
        du_ref[...] = (dya * mixed * dug).astype(BF16)
        dmixed = dya * ug
        dvn_cols, dws = [], []
        for j in range(N_HEADS // 2):
            dmp = dmixed[:, 128 * j:128 * (j + 1)]
            dvn_cols.append(_dot(wtcat_ref[j], _stack_pair(dmp, m_l, m_r)))
            vnp = vn[:, 128 * j:128 * (j + 1)].astype(BF16)
            dws.append(_dot((dmp * m_l).astype(BF16), vnp, _NT))
            dws.append(_dot((dmp * m_r).astype(BF16), vnp, _NT))
        dvn = jnp.concatenate(dvn_cols, axis=1)
        dvh = dvn * lnw
        dvgel = rstd * (dvh - _split_dot(dvh, avg, 2) - vhat * _split_dot(dvh * vhat, avg, 2))
        dv_ref[...] = (dvgel * dvg).astype(BF16)
        dbt = _split_dot(dmixed, expt_ref[...], 2)
        first = i == 0

        @pl.when(first)
        def _():
            for h in range(N_HEADS):
                dw_ref[h] = dws[h]
            db_ref[...] = dbt

        @pl.when(jnp.logical_not(first))
        def _():
            for h in range(N_HEADS):
                dw_ref[h] += dws[h]
            db_ref[...] += dbt

        _acc_rows(dlnw_ref, _rsum(dvn * vhat), first)
        _acc_rows(dlnb_ref, _rsum(dvn), first)

    row = pl.BlockSpec((CHUNK, GM_WIDTH), lambda i: (i, 0))
    consts = [lnw, lnb, wcat, wtcat, bias, avg, expand_t]
    return pl.pallas_call(
        body, name="gmlp_bwd", grid=(t_tok // CHUNK,),
        out_shape=(jax.ShapeDtypeStruct((t_tok, GM_WIDTH), BF16), jax.ShapeDtypeStruct((t_tok, GM_WIDTH), BF16),
                   jax.ShapeDtypeStruct((N_HEADS, CHUNK, CHUNK), F32), jax.ShapeDtypeStruct((CHUNK, CHUNK), F32),
                   jax.ShapeDtypeStruct((8, GM_WIDTH), F32), jax.ShapeDtypeStruct((8, GM_WIDTH), F32)),
        in_specs=[pl.BlockSpec((CHUNK, GM_WIDTH), lambda i: (i, 0)), row, row] + [_full(a.shape) for a in consts],
        out_specs=(row, row, _full((N_HEADS, CHUNK, CHUNK)), _full((CHUNK, CHUNK)), _full((8, GM_WIDTH)),
                   _full((8, GM_WIDTH))),
        compiler_params=_params("arbitrary"))(dmix, u, v, *consts)


def _ssd_bwd(dmix, z, xbc, dtr, y, states, cw, cb, dtb, alog, dskip_exp, nw, expand, expand_t, tril, triu, seq,
             dep=None):
    t_tok = z.shape[0]
    nc, chunk, row, tail = _ssd_specs(t_tok, seq, True)
    q = CHUNK

    def body(dm_ref, z_ref, xbc_ref, tail_ref, dtr_ref, y_ref, st_ref, cw_ref, cb_ref, dtb_ref, alog_ref, dsk_ref,
             nw_ref, exp_ref, expt_ref, tril_ref, triu_ref, dz_ref, dxbc_ref, ddt_ref, dcw_ref, dcb_ref, ddtb_ref,
             dalog_ref, dd_ref, dnw_ref, xext_ref, dext_ref, dstate_ref):
        b, c = pl.program_id(0), pl.program_id(1)
        first = jnp.logical_and(b == 0, c == 0)

        @pl.when(c == 0)
        def _():
            dstate_ref[...] = jnp.zeros_like(dstate_ref)
            dext_ref[q:q + 8, :] = jnp.zeros((8, CONV_CH), F32)

        _fill_xext(xext_ref, tail_ref, xbc_ref, c == nc - 1)
        m_l, m_r = _lane_masks()
        expt = expt_ref[...]
        f = _ssd_common(xext_ref, dtr_ref[...], cw_ref, cb_ref[...], dtb_ref[...], alog_ref[...], exp_ref[...],
                        tril_ref[...])
        act, pre, sg = f["act"], f["pre"], f["sg"]
        xs = act[:, :SSM_WIDTH]
        xdt = xs * f["dt_exp"]
        xw = xdt * f["w_end"]
        state = st_ref[0]
        dstate = dstate_ref[...]
        zv, yv, dout, nw = z_ref[...], y_ref[...], dm_ref[...], nw_ref[...]
        sz = jax.nn.sigmoid(zv)
        sl = zv * sz
        yg = yv * sl
        tv = dout * nw
        dyg_parts, ygh_parts = [], []
        for g in range(2):
            ygg = yg[:, 256 * g:256 * (g + 1)]
            rr = lax.rsqrt(jnp.mean(ygg * ygg, axis=-1, keepdims=True) + EPS)
            ygh = ygg * rr
            tg = tv[:, 256 * g:256 * (g + 1)]
            dyg_parts.append(rr * (tg - ygh * jnp.mean(tg * ygh, axis=-1, keepdims=True)))
            ygh_parts.append(ygh)
        dyg = jnp.concatenate(dyg_parts, axis=1)
        dnw = _rsum(dout * jnp.concatenate(ygh_parts, axis=1))
        dy = dyg * sl
        dz_ref[...] = (dyg * yv * (sz * (1.0 + zv * (1.0 - sz)))).astype(BF16)
        ddsk = _rsum(dy * xs)
        dye = dy * f["e"]
        lane = lax.broadcasted_iota(jnp.int32, (q, q), 1)
        sub = lax.broadcasted_iota(jnp.int32, (q, q), 0)
        rs_mat = jnp.zeros((q, q), F32)
        cs_mat = jnp.zeros((q, q), F32)
        dxdt_cols, yoff, dst_in, dxw, d_b, d_c = [], [], [], [], [], []
        for g in range(2):
            bg = act[:, 512 + 128 * g:640 + 128 * g].astype(BF16)
            cg = act[:, 768 + 128 * g:896 + 128 * g].astype(BF16)
            cb_mat = _dot(cg, bg, _NT)
            stg = state[:, 256 * g:256 * (g + 1)].astype(BF16)
            dyeg = dye[:, 256 * g:256 * (g + 1)].astype(BF16)
            yoff.append(_dot(cg, stg))
            dcg = _dot(dyeg, stg, _NT)
            dst_in.append(_dot(cg, dyeg, _TN))
            dcb = jnp.zeros((q, q), F32)
            for pr in range(2):
                h0 = 4 * g + 2 * pr
                gf = [cb_mat * f["decay"][h0], cb_mat * f["decay"][h0 + 1]]
                gcat = jnp.concatenate([gf[0].astype(BF16), gf[1].astype(BF16)], axis=1)
                xst = _stack_pair(xdt[:, 64 * h0:64 * h0 + 128], m_l, m_r)
                dyp = dy[:, 64 * h0:64 * h0 + 128].astype(BF16)
                dgcat = _dot(dyp, xst, _NT)
                dxst = _dot(gcat, dyp, _TN)
                dxdt_cols.append(dxst[:q] * m_l + dxst[q:] * m_r)
                for i in range(2):
                    h = h0 + i
                    dg = dgcat[:, q * i:q * (i + 1)]
                    mm = dg * gf[i]
                    rs_mat = rs_mat + jnp.where(lane == h, jnp.sum(mm, axis=1, keepdims=True), 0.0)
                    cs_mat = cs_mat + jnp.where(sub == h, jnp.sum(mm, axis=0, keepdims=True), 0.0)
                    dcb = dcb + dg * f["decay"][h]
            dcb16 = dcb.astype(BF16)
            dstg = dstate[:, 256 * g:256 * (g + 1)].astype(BF16)
            d_c.append(dcg + _dot(dcb16, bg))
            dxw.append(_dot(bg, dstg))
            d_b.append(_dot(dcb16, cg, _TN) + _dot(xw[:, 256 * g:256 * (g + 1)].astype(BF16), dstg, _NT))
        dxw = jnp.concatenate(dxw, axis=1)
        dxdt = jnp.concatenate(dxdt_cols, axis=1) + dxw * f["w_end"]
        qv = dxw * xw
        end_row = _rsum(qv) + _rsum(dstate * state) * f["cd"]
        x2 = dye * jnp.concatenate(yoff, axis=1) - qv
        row_i = lax.broadcasted_iota(jnp.int32, (q, 1), 0)
        x2 = x2 + jnp.where(row_i == q - 1, end_row, 0.0)
        da_cs = _split_dot(x2, expt, 3) + rs_mat - cs_mat.T
        ddt = _split_dot(dxdt * xs, expt, 3)
        dxs = dsk_ref[...] * dy + dxdt * f["dt_exp"]
        dda = _split_dot_left(triu_ref[...], da_cs, 3)
        ddt = ddt + dda * f["a_row"]
        dalog = _rsum(dda * f["dt"]) * f["a_row"]
        draw = ddt * jax.nn.sigmoid(f["dtp"])
        ddt_ref[...] = draw.astype(BF16)
        dact = jnp.concatenate([dxs] + d_b + d_c, axis=1)
        dpre = dact * (sg * (1.0 + pre * (1.0 - sg)))
        dext_ref[0:q, :] = dpre
        dxbc = cw_ref[0:1, :] * dext_ref[pl.ds(3, q), :]
        for k in range(1, 4):
            dxbc = dxbc + cw_ref[k:k + 1, :] * dext_ref[pl.ds(3 - k, q), :]
        dxbc_ref[...] = dxbc.astype(BF16)
        dext_ref[q:q + 8, :] = dpre[0:8, :]
        dstate_ref[...] = dstate * f["cd"] + jnp.concatenate(dst_in, axis=1)
        row8 = lax.broadcasted_iota(jnp.int32, (8, 1), 0)
        dcw = jnp.zeros((8, CONV_CH), F32)
        for k in range(4):
            dcw = dcw + jnp.where(row8 == k, _rsum(dpre * f["taps"][k]), 0.0)

        @pl.when(first)
        def _():
            dcw_ref[...] = dcw

        @pl.when(jnp.logical_not(first))
        def _():
            dcw_ref[...] += dcw

        _acc_rows(dcb_ref, _rsum(dpre), first)
        _acc_rows(ddtb_ref, _rsum(draw), first)
        _acc_rows(dalog_ref, dalog, first)
        _acc_rows(dd_ref, ddsk, first)
        _acc_rows(dnw_ref, dnw, first)

    consts = [cw, cb, dtb, alog, dskip_exp, nw, expand, expand_t, tril, triu]
    deps = [] if dep is None else [dep]
    n_in = 7 + len(consts)

    def body_skipping_dep(*refs):
        body(*refs[:n_in], *refs[n_in + len(deps):])

    acc = lambda n: jax.ShapeDtypeStruct((8, n), F32)
    return pl.pallas_call(
        body_skipping_dep, name="ssd_bwd", grid=(t_tok // seq, nc),
        out_shape=(jax.ShapeDtypeStruct((t_tok, SSM_WIDTH), BF16), jax.ShapeDtypeStruct((t_tok, CONV_CH), BF16),
                   jax.ShapeDtypeStruct((t_tok, CHUNK), BF16), acc(CONV_CH), acc(CONV_CH), acc(CHUNK), acc(CHUNK),
                   acc(SSM_WIDTH), acc(SSM_WIDTH)),
        in_specs=[pl.BlockSpec((CHUNK, SSM_WIDTH), lambda b, c: (chunk(b, c), 1)), row(SSM_WIDTH), row(CONV_CH), tail,
                  row(CHUNK), row(SSM_WIDTH), pl.BlockSpec((1, N_STATE, SSM_WIDTH), lambda b, c: (chunk(b, c), 0, 0))]
        + [_full(a.shape) for a in consts] + [pl.BlockSpec(memory_space=pl.ANY)] * len(deps),
        out_specs=(row(SSM_WIDTH), row(CONV_CH), row(CHUNK), _full((8, CONV_CH)), _full((8, CONV_CH)),
                   _full((8, CHUNK)), _full((8, CHUNK)), _full((8, SSM_WIDTH)), _full((8, SSM_WIDTH))),
        scratch_shapes=[pltpu.VMEM((CHUNK + 16, CONV_CH), F32), pltpu.VMEM((CHUNK + 8, CONV_CH), F32),
                        pltpu.VMEM((N_STATE, SSM_WIDTH), F32)],
        compiler_params=_params("arbitrary", "arbitrary"))(dmix, z, xbc, xbc, dtr, y, states, *consts, *deps)


def _in_bwd(du, dv, dz, dxbc, ddt, w_in, x, dx2, g1, tm, dep=None):
    t_tok = x.shape[0]

    def body(du_ref, dv_ref, dz_ref, dxbc_ref, ddt_ref, w_ref, x_ref, dx2_ref, g_ref, *rest):
        gx_ref, dg_ref = rest[-2:]
        i = pl.program_id(0)
        dh = None
        for (a, b), ref in zip(_IN_SPLITS, (du_ref, dv_ref, dz_ref, dxbc_ref, ddt_ref)):
            part = _dot(ref[...], w_ref[:, a:b], _NT)
            dh = part if dh is None else dh + part
        dn, dg = _rms_bwd(x_ref[...], g_ref[...], dh)
        gx_ref[...] = dx2_ref[...] + dn
        _acc_rows(dg_ref, dg, i == 0)

    row = lambda n: pl.BlockSpec((tm, n), lambda i: (i, 0))
    widths = [b - a for a, b in _IN_SPLITS]
    deps = [] if dep is None else [dep]
    return pl.pallas_call(
        body, name="in_bwd", grid=(t_tok // tm,),
        out_shape=(jax.ShapeDtypeStruct((t_tok, D_MODEL), F32), jax.ShapeDtypeStruct((8, D_MODEL), F32)),
        in_specs=[row(n) for n in widths] + [_full((D_MODEL, IN_PAD)), row(D_MODEL), row(D_MODEL), _full((1, D_MODEL))]
        + [pl.BlockSpec(memory_space=pl.ANY)] * len(deps),
        out_specs=(row(D_MODEL), _full((8, D_MODEL))),
        compiler_params=_params("arbitrary"))(du, dv, dz, dxbc, ddt, w_in, x, dx2, g1, *deps)


def _pad_lanes(a, n):
    return jnp.pad(a, ((0, 0), (0, n - a.shape[1])))


def _local_step(x, target, seq, w_in_p, conv_w, small, hooks):
    t_tok = x.shape[0]
    tm = min(512, t_tok)
    avg, expand, expand_t, tril, triu = _const_mats()
    g1, g2, g3, g4 = (small[k].reshape(1, D_MODEL) for k in
                      ("norm_mix_pre", "norm_mix_post", "norm_ffn_pre", "norm_ffn_post"))
    lnw = small["gm_ln_w"].reshape(1, GM_WIDTH)
    lnb = small["gm_ln_b"].reshape(1, GM_WIDTH)
    causal = jnp.tril(jnp.ones((CHUNK, CHUNK), F32))
    wm = small["gm_w_s"] * causal
    pair = lambda w: w.reshape(4, 2, CHUNK, CHUNK).transpose(0, 2, 1, 3).reshape(4, CHUNK, 2 * CHUNK).astype(BF16)
    wcat = pair(wm)
    wtcat = pair(jnp.swapaxes(wm, 1, 2))
    bias = jnp.repeat(small["gm_b_s"].T, HEAD_DIM, axis=1)
    cb = small["conv_b"].reshape(1, CONV_CH)
    dtb = _pad_lanes(small["dt_bias"].reshape(1, N_HEADS), CHUNK)
    alog = _pad_lanes(small["a_log"].reshape(1, N_HEADS), CHUNK)
    dskip_exp = jnp.repeat(small["d_skip"].reshape(1, N_HEADS), HEAD_DIM, axis=1)
    nw = small["ssm_norm_w"].reshape(1, SSM_WIDTH)

    h1, u, v, z, xbc, dtr = _in_proj(x, g1, w_in_p, tm)
    mix_a = _gmlp_fwd(u, v, lnw, lnb, wcat, bias, avg)
    mix_b, y_pre, states = _ssd_fwd(z, xbc, dtr, conv_w, cb, dtb, alog, dskip_exp, nw, expand, tril, seq)
    w_out, w_up_blk, w_down = hooks["late_weights"](mix_b)
    o, x2, h3, mix = _out_proj(mix_a, mix_b, w_out, x, g2, g3, tm)
    ra, dd, dy, dg4, loss = _mlp_fwd(h3, w_up_blk, w_down, x2, target, g4, tm)

    da, dx2, do, dg3, dg2 = _mlp_bwd(dd, w_down, ra, w_up_blk, x2, dy, o, g3, g2, tm)
    bk = min(512, t_tok)
    g_w_down = _wgrad(ra, dd, None, 1024, D_MODEL, bk, True, "wgrad_down")
    g_w_up = _wgrad(h3, da, N_DEV, D_MODEL, D_FF // N_DEV, bk, False, "wgrad_up")
    dep = hooks["mlp_grads"](g_w_down, g_w_up)
    dmix = _dmix(do, w_out, tm, dep)
    g_w_out = _wgrad(mix, do, None, D_MODEL, D_MODEL, bk, False, "wgrad_out", dep)
    du, dv, dws, dbt, dlnw, dlnb = _gmlp_bwd(dmix, u, v, lnw, lnb, wcat, wtcat, bias, avg, expand_t)
    dep = hooks["gmlp_grads"](g_w_out, dws)
    dz, dxbc, ddt, dcw, dcb, ddtb, dalog, ddsk, dnw = _ssd_bwd(
        dmix, z, xbc, dtr, y_pre, states, conv_w, cb, dtb, alog, dskip_exp, nw, expand, expand_t, tril, triu, seq, dep)
    g_w_in = _wgrad_in(h1, (du, dv, dz, dxbc, ddt), 512, bk, dep)
    dep = hooks["in_grads"](g_w_in, dcw[0:4])
    grad_x, dg1 = _in_bwd(du, dv, dz, dxbc, ddt, w_in_p, x, dx2, g1, tm, dep)

    grads = dict(
        w_in=g_w_in, w_out=g_w_out, w_up=g_w_up, w_down=g_w_down, conv_w=dcw[0:4],
        norm_mix_pre=dg1[0:1], norm_mix_post=dg2[0:1], norm_ffn_pre=dg3[0:1], norm_ffn_post=dg4[0:1],
        gm_ln_w=dlnw[0:1], gm_ln_b=dlnb[0:1], gm_w_s=dws, gm_b_s=dbt.T[0:N_HEADS], conv_b=dcb[0:1],
        dt_bias=ddtb[0:1, 0:N_HEADS], a_log=dalog[0:1, 0:N_HEADS],
        d_skip=ddsk[0:1].reshape(N_HEADS, HEAD_DIM).sum(axis=1).reshape(1, N_HEADS), ssm_norm_w=dnw[0:1])
    return loss[0, 0], grad_x, grads


_SMALL_ROW_PARAMS = ("norm_mix_pre", "norm_mix_post", "norm_ffn_pre", "norm_ffn_post", "gm_ln_w", "gm_ln_b", "gm_b_s",
                     "conv_b", "dt_bias", "a_log", "d_skip", "ssm_norm_w")
_WEIGHTS = ("norm_mix_pre", "w_in", "gm_ln_w", "gm_ln_b", "gm_w_s", "gm_b_s", "conv_w", "conv_b", "dt_bias", "a_log",
            "d_skip", "ssm_norm_w", "w_out", "norm_mix_post", "norm_ffn_pre", "w_up", "w_down", "norm_ffn_post")


def _pack_rows(tensors):
    rows = [_pad_lanes(t.reshape(1, -1), D_MODEL) for t in tensors]
    rows.append(jnp.zeros((SMALL_ROWS - len(rows), D_MODEL), F32))
    return jnp.concatenate(rows, axis=0)


def kernel(x, norm_mix_pre, w_in, gm_ln_w, gm_ln_b, gm_w_s, gm_b_s, conv_w, conv_b, dt_bias, a_log, d_skip, ssm_norm_w, w_out, norm_mix_post, norm_ffn_pre, w_up, w_down, norm_ffn_post, loss_target, m_norm_mix_pre, m_w_in, m_gm_ln_w, m_gm_ln_b, m_gm_w_s, m_gm_b_s, m_conv_w, m_conv_b, m_dt_bias, m_a_log, m_d_skip, m_ssm_norm_w, m_w_out, m_norm_mix_post, m_norm_ffn_pre, m_w_up, m_w_down, m_norm_ffn_post, v_norm_mix_pre, v_w_in, v_gm_ln_w, v_gm_ln_b, v_gm_w_s, v_gm_b_s, v_conv_w, v_conv_b, v_dt_bias, v_a_log, v_d_skip, v_ssm_norm_w, v_w_out, v_norm_mix_post, v_norm_ffn_pre, v_w_up, v_w_down, v_norm_ffn_post):
    w = dict(norm_mix_pre=norm_mix_pre, w_in=w_in, gm_ln_w=gm_ln_w, gm_ln_b=gm_ln_b, gm_w_s=gm_w_s, gm_b_s=gm_b_s, conv_w=conv_w, conv_b=conv_b, dt_bias=dt_bias, a_log=a_log, d_skip=d_skip, ssm_norm_w=ssm_norm_w, w_out=w_out, norm_mix_post=norm_mix_post, norm_ffn_pre=norm_ffn_pre, w_up=w_up, w_down=w_down, norm_ffn_post=norm_ffn_post)
    m = dict(norm_mix_pre=m_norm_mix_pre, w_in=m_w_in, gm_ln_w=m_gm_ln_w, gm_ln_b=m_gm_ln_b, gm_w_s=m_gm_w_s, gm_b_s=m_gm_b_s, conv_w=m_conv_w, conv_b=m_conv_b, dt_bias=m_dt_bias, a_log=m_a_log, d_skip=m_d_skip, ssm_norm_w=m_ssm_norm_w, w_out=m_w_out, norm_mix_post=m_norm_mix_post, norm_ffn_pre=m_norm_ffn_pre, w_up=m_w_up, w_down=m_w_down, norm_ffn_post=m_norm_ffn_post)
    v = dict(norm_mix_pre=v_norm_mix_pre, w_in=v_w_in, gm_ln_w=v_gm_ln_w, gm_ln_b=v_gm_ln_b, gm_w_s=v_gm_w_s, gm_b_s=v_gm_b_s, conv_w=v_conv_w, conv_b=v_conv_b, dt_bias=v_dt_bias, a_log=v_a_log, d_skip=v_d_skip, ssm_norm_w=v_ssm_norm_w, w_out=v_w_out, norm_mix_post=v_norm_mix_post, norm_ffn_pre=v_norm_ffn_pre, w_up=v_w_up, w_down=v_w_down, norm_ffn_post=v_norm_ffn_post)
    n_batch, seq, _ = x.shape
    shard_in = IN_COLS // N_DEV

    me = (4 * lax.axis_index("x") + 2 * lax.axis_index("y") + lax.axis_index("c")).astype(jnp.int32).reshape(1)

    def in_slot(own):
        return lax.dynamic_update_slice(lax.empty((N_DEV,) + own.shape, own.dtype), own[None],
                                        (me[0],) + (0,) * own.ndim)

    gat_in, _ = _exchange_start(
        [_cast_to_slot(w_in[0], me, 256, "cast_w_in"), in_slot(conv_w[0])], [True, True], _ALL_PEERS, "gather_in_start")
    (_, ag_in), (_, ag_conv) = _exchange_wait(gat_in, me, "gather_in_wait")
    gat_mlp, tok_mlp = _exchange_start(
        [_cast_to_slot(w_out[0], me, 128, "cast_w_out"), _cast_to_slot(w_up[0], me, 256, "cast_w_up"),
         _cast_to_slot(w_down[0], me, 256, "cast_w_down")], [True] * 3, _ALL_PEERS, "gather_mlp_start", dep=ag_conv)
    w_in_p = _pad_lanes(ag_in.transpose(1, 0, 2).reshape(D_MODEL, IN_COLS), IN_PAD)
    conv_w_f = ag_conv.transpose(1, 0, 2).reshape(4, CONV_CH)

    def late_weights(after):
        (_, ag_out), (_, ag_up), (_, ag_down) = _exchange_wait(gat_mlp, after, "gather_mlp_wait")
        return ag_out.reshape(D_MODEL, D_MODEL), ag_up, ag_down.reshape(D_FF, D_MODEL)

    sent = {}

    def mlp_grads(g_w_down, g_w_up):
        sent["mlp"], tok = _exchange_start(
            [g_w_down.reshape(N_DEV, D_FF // N_DEV, D_MODEL), g_w_up], [False, False], _ALL_PEERS, "grads_mlp_start")
        return tok

    def gmlp_grads(g_w_out, g_w_s):
        sent["gmlp"], tok = _exchange_start(
            [g_w_out.reshape(N_DEV, D_MODEL // N_DEV, D_MODEL), in_slot(g_w_s.astype(BF16))], [False, True], _ALL_PEERS,
            "grads_gmlp_start")
        return tok

    def in_grads(g_w_in, g_conv_w):
        g_in_blk = g_w_in[:, :IN_COLS].reshape(D_MODEL, N_DEV, shard_in).transpose(1, 0, 2)
        g_conv_blk = g_conv_w.reshape(4, N_DEV, CONV_CH // N_DEV).transpose(1, 0, 2)
        sent["in"], tok = _exchange_start([g_in_blk, g_conv_blk], [False, False], _ALL_PEERS, "grads_in_start")
        return tok

    small = {k: w[k][0] for k in _SMALL_ROW_PARAMS + ("gm_w_s",)}
    small["norm_mix_pre"] = small["norm_mix_pre"] + tok_mlp[0:1, 0:1]
    loss_part, grad_x, g = _local_step(
        x.reshape(n_batch * seq, D_MODEL), loss_target.reshape(n_batch * seq, D_MODEL), seq, w_in_p, conv_w_f, small,
        dict(late_weights=late_weights, mlp_grads=mlp_grads, gmlp_grads=gmlp_grads, in_grads=in_grads))
    loss = lax.psum(loss_part, ("x", "y", "c"))

    sent_rows, tok_rows = _exchange_start(
        [in_slot(_pack_rows([g[k] for k in _SMALL_ROW_PARAMS]))], [True], _ALL_PEERS, "grads_rows_start")
    (own_down, p_down), (own_up, p_up) = _exchange_wait(sent["mlp"], tok_rows, "grads_mlp_wait")
    res = {}
    res["w_up"] = _adamw_reduce(p_up, own_up, me, w_up[0], m_w_up[0], v_w_up[0], 256, "adamw_w_up")
    res["w_down"] = _adamw_reduce(p_down, own_down, me, w_down[0], m_w_down[0], v_w_down[0], 128, "adamw_w_down")
    (own_out, p_out), (_, p_ws) = _exchange_wait(sent["gmlp"], res["w_down"][1], "grads_gmlp_wait")
    res["w_out"] = _adamw_reduce(p_out, own_out, me, w_out[0], m_w_out[0], v_w_out[0], 128, "adamw_w_out")
    causal = jnp.tril(jnp.ones((1, CHUNK, CHUNK), F32))
    res["gm_w_s"] = _adamw_small(p_ws, None, me, gm_w_s[0], m_gm_w_s[0], v_gm_w_s[0], causal, "adamw_gm_w_s")
    (own_in, p_in), (own_conv, p_conv) = _exchange_wait(sent["in"], res["gm_w_s"][1], "grads_in_wait")
    res["w_in"] = _adamw_reduce(p_in, own_in, me, w_in[0], m_w_in[0], v_w_in[0], 256, "adamw_w_in")
    res["conv_w"] = _adamw_small(p_conv, own_conv, me, conv_w[0], m_conv_w[0], v_conv_w[0], None, "adamw_conv_w")
    ((_, p_rows),) = _exchange_wait(sent_rows, res["w_in"][1], "grads_rows_wait")
    rows = _adamw_small(p_rows, None, me, _pack_rows([w[k] for k in _SMALL_ROW_PARAMS]),
                        _pack_rows([m[k] for k in _SMALL_ROW_PARAMS]), _pack_rows([v[k] for k in _SMALL_ROW_PARAMS]),
                        None, "adamw_rows")
    for i, k in enumerate(_SMALL_ROW_PARAMS):
        size = int(np.prod(w[k].shape))
        res[k] = tuple(r[i, :size].reshape(w[k].shape) for r in rows)
    for k in ("w_in", "w_out", "w_up", "w_down", "conv_w", "gm_w_s"):
        res[k] = tuple(r.reshape(w[k].shape) for r in res[k])

    outs = [loss, grad_x.reshape(x.shape)]
    for part in range(4):
        outs.extend(res[k][part] for k in _WEIGHTS)
    return tuple(outs)
```

```python
import functools

import jax
import jax.numpy as jnp
import numpy as np
from jax import lax
from jax.experimental import pallas as pl
from jax.experimental.pallas import tpu as pltpu

F32 = jnp.float32
BF16 = jnp.bfloat16

D_MODEL = 1024
GM_WIDTH = 512
SSM_WIDTH = 512
CONV_CH = 1024
N_HEADS = 8
HEAD_DIM = 64
N_STATE = 128
CHUNK = 128
D_FF = 4096
IN_COLS = 2568
IN_PAD = 2688
N_DEV = 8
EPS = 1e-6
ADAM_LR, ADAM_B1, ADAM_B2, ADAM_EPS, ADAM_WD, ADAM_STEP = 0.001, 0.9, 0.999, 1e-08, 0.01, 10
VMEM_LIMIT_BYTES = 56 * 1024 * 1024
SMALL_ROWS = 16

_NT = (((1,), (1,)), ((), ()))
_TN = (((0,), (0,)), ((), ()))


def _params(*sem):
    return pltpu.CompilerParams(dimension_semantics=sem or None, vmem_limit_bytes=VMEM_LIMIT_BYTES)


def _dot(a, b, dims=None):
    if dims is None:
        return jnp.dot(a, b, preferred_element_type=F32)
    return lax.dot_general(a, b, dims, preferred_element_type=F32)


def _split_terms(x, terms):
    out, rem = [], x
    for i in range(terms):
        hi = rem.astype(BF16)
        out.append(hi)
        if i + 1 < terms:
            rem = rem - hi.astype(F32)
    return out


def _split_dot(x, m, terms):
    acc = None
    for hi in _split_terms(x, terms):
        part = _dot(hi, m)
        acc = part if acc is None else acc + part
    return acc


def _split_dot_left(m, x, terms):
    acc = None
    for hi in _split_terms(x, terms):
        part = _dot(m, hi)
        acc = part if acc is None else acc + part
    return acc


def _gelu_and_grad(x):
    c = 0.7978845608028654
    inner = c * (x + 0.044715 * x * x * x)
    t = jnp.tanh(inner)
    g = 0.5 * x * (1.0 + t)
    dg = 0.5 * (1.0 + t) + 0.5 * x * (1.0 - t * t) * c * (1.0 + 3.0 * 0.044715 * x * x)
    return g, dg


def _softplus(x):
    return jnp.maximum(x, 0.0) + jnp.log(1.0 + jnp.exp(-jnp.abs(x)))


def _rsum(x):
    return jnp.sum(x, axis=0, keepdims=True)


def _acc_rows(ref, part, first):
    val = jnp.broadcast_to(part, ref.shape)

    @pl.when(first)
    def _():
        ref[...] = val

    @pl.when(jnp.logical_not(first))
    def _():
        ref[...] += val


def _rms_bwd(n, g, dout):
    r = lax.rsqrt(jnp.mean(n * n, axis=-1, keepdims=True) + EPS)
    nh = n * r
    dg = dout * g
    dn = r * (dg - nh * jnp.mean(dg * nh, axis=-1, keepdims=True))
    return dn, _rsum(dout * nh)


def _const_mats():
    avg = np.kron(np.eye(N_HEADS), np.full((HEAD_DIM, HEAD_DIM), 1.0 / HEAD_DIM))
    expand = np.zeros((CHUNK, SSM_WIDTH), np.float32)
    for h in range(N_HEADS):
        expand[h, h * HEAD_DIM:(h + 1) * HEAD_DIM] = 1.0
    tril = np.tril(np.ones((CHUNK, CHUNK), np.float32))
    as_bf16 = lambda a: jnp.asarray(a, dtype=BF16)
    return as_bf16(avg), as_bf16(expand), as_bf16(expand.T), as_bf16(tril), as_bf16(tril.T)


def _full(shape):
    nd = len(shape)
    return pl.BlockSpec(shape, lambda *_: (0,) * nd)


_HBM = pl.BlockSpec(memory_space=pltpu.HBM)
_SEM = pl.BlockSpec(memory_space=pltpu.SEMAPHORE)
_ALL_PEERS = tuple(range(1, N_DEV))


def _peer_of(k):
    x, y, c = lax.axis_index("x"), lax.axis_index("y"), lax.axis_index("c")
    px = 1 - x if k & 4 else x
    py = 1 - y if k & 2 else y
    pc = 1 - c if k & 1 else c
    return (px, py, pc), 4 * px + 2 * py + pc


def _copies(src, land, send_sems, recv_sems, peers):
    x, y, c = lax.axis_index("x"), lax.axis_index("y"), lax.axis_index("c")
    me = 4 * x + 2 * y + c
    out = []
    for t in range(len(src)):
        for i, k in enumerate(peers):
            pos, peer = _peer_of(k)
            sem = t * len(peers) + i
            mk = functools.partial(pltpu.make_async_remote_copy, send_sem=send_sems.at[sem], recv_sem=recv_sems.at[sem],
                                   device_id=pos, device_id_type=pl.DeviceIdType.MESH)
            if land[t] is None and src[t].shape[0] != N_DEV:
                width = src[t].shape[1] // N_DEV
                slab = lambda j: src[t].at[:, pl.ds(pl.multiple_of(j * width, 128), width)]
                mine = functools.partial(mk, src_ref=slab(me), dst_ref=slab(me))
                theirs = functools.partial(mk, src_ref=slab(peer), dst_ref=slab(peer))
            elif land[t] is None:
                mine = functools.partial(mk, src_ref=src[t].at[me], dst_ref=src[t].at[me])
                theirs = functools.partial(mk, src_ref=src[t].at[peer], dst_ref=src[t].at[peer])
            else:
                mine = functools.partial(mk, src_ref=src[t].at[peer], dst_ref=land[t].at[me])
                theirs = functools.partial(mk, src_ref=src[t].at[peer], dst_ref=land[t].at[peer])
            out.append((mine, theirs))
    return out


def _exchange_start(srcs, inplace, peers, name, dep=None):
    n = len(srcs)
    lands = [None if ip else pltpu.with_memory_space_constraint(lax.empty(s.shape, s.dtype), pltpu.HBM)
             for s, ip in zip(srcs, inplace)]
    real_lands = [l for l in lands if l is not None]
    n_l = len(real_lands)
    deps = [] if dep is None else [dep]

    def body(*refs):
        src = refs[:n]
        land_refs = list(refs[n:n + n_l])
        send_sems, recv_sems = refs[n + n_l + len(deps)], refs[n + n_l + len(deps) + 1]
        token = refs[-1]
        land = [None if ip else land_refs.pop(0) for ip in inplace]
        for mine, _ in _copies(src, land, send_sems, recv_sems, peers):
            mine().start()
        token[...] = jnp.zeros_like(token)

    sem_t = pltpu.SemaphoreType.DMA((n * len(peers),))
    outs = pl.pallas_call(
        body, name=name,
        out_shape=(sem_t, sem_t) + tuple(pltpu.HBM(a.shape, a.dtype) for a in list(srcs) + real_lands)
        + (jax.ShapeDtypeStruct((8, 128), F32),),
        in_specs=[_HBM] * (n + n_l) + [pl.BlockSpec(memory_space=pl.ANY)] * len(deps),
        out_specs=(_SEM, _SEM) + (_HBM,) * (n + n_l) + (pl.BlockSpec(memory_space=pltpu.VMEM),),
        input_output_aliases={i: 2 + i for i in range(n + n_l)},
        compiler_params=pltpu.CompilerParams(has_side_effects=pltpu.SideEffectType.DATAFLOW_SIDE_EFFECTING),
    )(*[pltpu.with_memory_space_constraint(s, pltpu.HBM) for s in srcs], *real_lands, *deps)
    handle = dict(send=outs[0], recv=outs[1], srcs=outs[2:2 + n], lands=outs[2 + n:2 + n + n_l], inplace=inplace,
                  peers=peers)
    return handle, outs[-1]


def _exchange_wait(handle, after, name):
    srcs, lands, inplace, peers = handle["srcs"], handle["lands"], handle["inplace"], handle["peers"]
    n, n_l = len(srcs), len(lands)

    def body(*refs):
        src = refs[:n]
        land_refs = list(refs[n:n + n_l])
        send_sems, recv_sems = refs[n + n_l], refs[n + n_l + 1]
        land = [None if ip else land_refs.pop(0) for ip in inplace]
        for mine, theirs in _copies(src, land, send_sems, recv_sems, peers):
            mine().wait_send()
            theirs().wait_recv()

    outs = pl.pallas_call(
        body, name=name, out_shape=tuple(pltpu.HBM(a.shape, a.dtype) for a in list(srcs) + list(lands)),
        in_specs=[_HBM] * (n + n_l) + [_SEM, _SEM, pl.BlockSpec(memory_space=pl.ANY)],
        out_specs=(_HBM,) * (n + n_l), input_output_aliases={i: i for i in range(n + n_l)},
        compiler_params=pltpu.CompilerParams(has_side_effects=pltpu.SideEffectType.DATAFLOW_SIDE_EFFECTING),
    )(*srcs, *lands, handle["send"], handle["recv"], after)
    res, land_out = [], list(outs[n:])
    for t in range(n):
        res.append((outs[t], outs[t] if inplace[t] else land_out.pop(0)))
    return res


def _cast_to_slot(w, me, rows, name, cols=False):
    r, cdim = w.shape

    def body(me_ref, w_ref, o_ref):
        if cols:
            o_ref[...] = w_ref[...].astype(BF16)
        else:
            o_ref[0] = w_ref[...].astype(BF16)

    if cols:
        out_shape = jax.ShapeDtypeStruct((r, N_DEV * cdim), BF16)
        out_spec = pl.BlockSpec((rows, cdim), lambda i, me_ref: (i, me_ref[0]))
    else:
        out_shape = jax.ShapeDtypeStruct((N_DEV, r, cdim), BF16)
        out_spec = pl.BlockSpec((1, rows, cdim), lambda i, me_ref: (me_ref[0], i, 0))
    return pl.pallas_call(
        body, name=name, out_shape=out_shape,
        grid_spec=pltpu.PrefetchScalarGridSpec(
            num_scalar_prefetch=1, grid=(r // rows,), in_specs=[pl.BlockSpec((rows, cdim), lambda i, me_ref: (i, 0))],
            out_specs=out_spec),
        compiler_params=_params("parallel"))(me, w)


def _adamw_math(w, g, m, v):
    m = ADAM_B1 * m + (1.0 - ADAM_B1) * g
    v = ADAM_B2 * v + (1.0 - ADAM_B2) * (g * g)
    m_hat = m / (1.0 - ADAM_B1 ** ADAM_STEP)
    v_hat = v / (1.0 - ADAM_B2 ** ADAM_STEP)
    delta = -ADAM_LR * (m_hat / (jnp.sqrt(v_hat) + ADAM_EPS) + ADAM_WD * w)
    return delta, m, v


def _sum_parts(me, p_ref, own):
    g = None
    for j in range(N_DEV):
        term = (p_ref[j] if own is None else jnp.where(me == j, own, p_ref[j])).astype(F32)
        g = term if g is None else g + term
    return g


def _adamw_reduce(parts, own, me, w, m, v, rows, name):
    r, cdim = w.shape

    def body(me_ref, p_ref, own_ref, w_ref, m_ref, v_ref, g_out, d_out, m_out, v_out):
        g = _sum_parts(me_ref[0], p_ref, own_ref[0])
        d, mn, vn = _adamw_math(w_ref[...], g, m_ref[...], v_ref[...])
        g_out[...] = g
        d_out[...] = d
        m_out[...] = mn
        v_out[...] = vn

    blk = pl.BlockSpec((rows, cdim), lambda i, me_ref: (i, 0))
    sds = jax.ShapeDtypeStruct(w.shape, F32)
    return pl.pallas_call(
        body, name=name, out_shape=(sds,) * 4,
        grid_spec=pltpu.PrefetchScalarGridSpec(
            num_scalar_prefetch=1, grid=(r // rows,),
            in_specs=[pl.BlockSpec((N_DEV, rows, cdim), lambda i, me_ref: (0, i, 0)),
                      pl.BlockSpec((1, rows, cdim), lambda i, me_ref: (me_ref[0], i, 0)), blk, blk, blk],
            out_specs=(blk,) * 4),
        compiler_params=_params("parallel"))(me, parts, own, w, m, v)


def _adamw_small(parts, own, me, w, m, v, mask, name):
    def body(me_ref, *refs):
        refs = list(refs)
        p_ref = refs.pop(0)
        own_ref = None if own is None else refs.pop(0)
        w_ref, m_ref, v_ref = refs[:3]
        k_ref = None if mask is None else refs[3]
        g_out, d_out, m_out, v_out = refs[-4:]
        g = _sum_parts(me_ref[0], p_ref, None if own is None else own_ref[me_ref[0]])
        if mask is not None:
            g = g * k_ref[...]
        d, mn, vn = _adamw_math(w_ref[...], g, m_ref[...], v_ref[...])
        g_out[...] = g
        d_out[...] = d
        m_out[...] = mn
        v_out[...] = vn

    def whole(shape):
        nd = len(shape)
        return pl.BlockSpec(shape, lambda i, me_ref: (0,) * nd)

    sds = jax.ShapeDtypeStruct(w.shape, F32)
    ins = [parts] + ([] if own is None else [own]) + [w, m, v] + ([] if mask is None else [mask])
    return pl.pallas_call(
        body, name=name, out_shape=(sds,) * 4,
        grid_spec=pltpu.PrefetchScalarGridSpec(
            num_scalar_prefetch=1, grid=(1,), in_specs=[whole(a.shape) for a in ins],
            out_specs=(whole(w.shape),) * 4),
        compiler_params=_params("arbitrary"))(me, *ins)


_IN_SPLITS = ((0, 512), (512, 1024), (1024, 1536), (1536, 2560), (2560, IN_PAD))


def _in_proj(x, g1, w_in, tm):
    t_tok = x.shape[0]

    def body(x_ref, g_ref, w_ref, h_ref, *outs):
        xv = x_ref[...]
        r = lax.rsqrt(jnp.mean(xv * xv, axis=-1, keepdims=True) + EPS)
        h = (xv * r * g_ref[...]).astype(BF16)
        h_ref[...] = h
        for (a, b), o_ref in zip(_IN_SPLITS, outs):
            o_ref[...] = _dot(h, w_ref[:, a:b])

    row = lambda n: pl.BlockSpec((tm, n), lambda i: (i, 0))
    widths = [b - a for a, b in _IN_SPLITS]
    return pl.pallas_call(
        body, name="in_proj", grid=(t_tok // tm,),
        out_shape=(jax.ShapeDtypeStruct((t_tok, D_MODEL), BF16),) + tuple(
            jax.ShapeDtypeStruct((t_tok, n), F32) for n in widths),
        in_specs=[row(D_MODEL), _full((1, D_MODEL)), _full((D_MODEL, IN_PAD))],
        out_specs=(row(D_MODEL),) + tuple(row(n) for n in widths),
        compiler_params=_params("parallel"))(x, g1, w_in)


def _lane_masks():
    lane = lax.broadcasted_iota(jnp.int32, (1, 2 * HEAD_DIM), 1)
    left = (lane < HEAD_DIM).astype(F32)
    return left, 1.0 - left


def _stack_pair(v, m_l, m_r):
    return jnp.concatenate([v * m_l, v * m_r], axis=0).astype(BF16)


def _gmlp_common(u, v, lnw, lnb, avg, wcat_ref, bias, m_l, m_r):
    ug, dug = _gelu_and_grad(u)
    vg, dvg = _gelu_and_grad(v)
    mu = _split_dot(vg, avg, 2)
    vc = vg - mu
    var = _split_dot(vc * vc, avg, 2)
    rstd = lax.rsqrt(var + EPS)
    vhat = vc * rstd
    vn = vhat * lnw + lnb
    cols = []
    for j in range(N_HEADS // 2):
        cols.append(_dot(wcat_ref[j], _stack_pair(vn[:, 128 * j:128 * (j + 1)], m_l, m_r)))
    mixed = jnp.concatenate(cols, axis=1) + bias
    return ug, dug, dvg, rstd, vhat, vn, mixed


def _gmlp_fwd(u, v, lnw, lnb, wcat, bias, avg):
    t_tok = u.shape[0]

    def body(u_ref, v_ref, lnw_ref, lnb_ref, wcat_ref, bias_ref, avg_ref, o_ref):
        m_l, m_r = _lane_masks()
        ug, _, _, _, _, _, mixed = _gmlp_common(
            u_ref[...], v_ref[...], lnw_ref[...], lnb_ref[...], avg_ref[...], wcat_ref, bias_ref[...], m_l, m_r)
        o_ref[...] = (ug * mixed).astype(BF16)

    row = pl.BlockSpec((CHUNK, GM_WIDTH), lambda i: (i, 0))
    return pl.pallas_call(
        body, name="gmlp_fwd", grid=(t_tok // CHUNK,), out_shape=jax.ShapeDtypeStruct((t_tok, GM_WIDTH), BF16),
        in_specs=[row, row, _full((1, GM_WIDTH)), _full((1, GM_WIDTH)), _full(wcat.shape), _full(bias.shape),
                  _full(avg.shape)],
        out_specs=row, compiler_params=_params("parallel"))(u, v, lnw, lnb, wcat, bias, avg)


def _ssd_common(xext_ref, dtr, cw_ref, cb, dtb, alog, expand, tril):
    q = CHUNK
    taps = [xext_ref[pl.ds(5 + k, q), :] for k in range(4)]
    pre = cb + cw_ref[0:1, :] * taps[0] + cw_ref[1:2, :] * taps[1] + cw_ref[2:3, :] * taps[2] + cw_ref[3:4, :] * taps[3]
    sg = jax.nn.sigmoid(pre)
    act = pre * sg
    lane = lax.broadcasted_iota(jnp.int32, (1, CHUNK), 1)
    a_row = jnp.where(lane < N_HEADS, -jnp.exp(alog), 0.0)
    dtp = dtr + dtb
    dt = _softplus(dtp)
    a_cs = _split_dot_left(tril, dt * a_row, 3)
    a_cs_t = a_cs.T
    dt_exp = _split_dot(dt, expand, 3)
    a_exp = _split_dot(a_cs, expand, 3)
    a_end = a_exp[q - 1:q, :]
    li = lax.broadcasted_iota(jnp.int32, (q, q), 0)
    si = lax.broadcasted_iota(jnp.int32, (q, q), 1)
    causal = si <= li
    decay = []
    for h in range(N_HEADS):
        seg = a_cs[:, h:h + 1] - a_cs_t[h:h + 1, :]
        decay.append(jnp.where(causal, jnp.exp(jnp.minimum(seg, 0.0)), 0.0))
    return dict(taps=taps, pre=pre, sg=sg, act=act, a_row=a_row, dtp=dtp, dt=dt, dt_exp=dt_exp, a_exp=a_exp,
                e=jnp.exp(a_exp), w_end=jnp.exp(a_end - a_exp), cd=jnp.exp(a_end), decay=decay)


def _ssd_specs(t_tok, seq, reverse):
    nc = seq // CHUNK

    def chunk(b, c):
        return b * nc + (nc - 1 - c if reverse else c)

    def row(n):
        return pl.BlockSpec((CHUNK, n), lambda b, c: (chunk(b, c), 0))

    tail = pl.BlockSpec((8, CONV_CH), lambda b, c: (jnp.maximum(chunk(b, c) * (CHUNK // 8) - 1, 0), 0))
    return nc, chunk, row, tail


def _fill_xext(xext_ref, tail_ref, xbc_ref, first_chunk):
    xext_ref[0:8, :] = jnp.where(first_chunk, 0.0, tail_ref[...])
    xext_ref[8:8 + CHUNK, :] = xbc_ref[...]


def _ssd_fwd(z, xbc, dtr, cw, cb, dtb, alog, dskip_exp, nw, expand, tril, seq):
    t_tok = z.shape[0]
    nc, chunk, row, tail = _ssd_specs(t_tok, seq, False)

    def body(z_ref, xbc_ref, tail_ref, dtr_ref, cw_ref, cb_ref, dtb_ref, alog_ref, dsk_ref, nw_ref, exp_ref,
             tril_ref, o_ref, y_ref, st_ref, xext_ref, state_ref):
        c = pl.program_id(1)

        @pl.when(c == 0)
        def _():
            state_ref[...] = jnp.zeros_like(state_ref)

        _fill_xext(xext_ref, tail_ref, xbc_ref, c == 0)
        m_l, m_r = _lane_masks()
        f = _ssd_common(xext_ref, dtr_ref[...], cw_ref, cb_ref[...], dtb_ref[...], alog_ref[...], exp_ref[...],
                        tril_ref[...])
        act = f["act"]
        xs = act[:, :SSM_WIDTH]
        xdt = xs * f["dt_exp"]
        xw = xdt * f["w_end"]
        state = state_ref[...]
        st_ref[0] = state
        ydiag, yoff, snew = [], [], []
        for g in range(2):
            bg = act[:, 512 + 128 * g:640 + 128 * g].astype(BF16)
            cg = act[:, 768 + 128 * g:896 + 128 * g].astype(BF16)
            cb_mat = _dot(cg, bg, _NT)
            for pr in range(2):
                h0 = 4 * g + 2 * pr
                gcat = jnp.concatenate(
                    [(cb_mat * f["decay"][h0]).astype(BF16), (cb_mat * f["decay"][h0 + 1]).astype(BF16)], axis=1)
                ydiag.append(_dot(gcat, _stack_pair(xdt[:, 64 * h0:64 * h0 + 128], m_l, m_r)))
            yoff.append(_dot(cg, state[:, 256 * g:256 * (g + 1)].astype(BF16)))
            snew.append(_dot(bg, xw[:, 256 * g:256 * (g + 1)].astype(BF16), _TN))
        y = jnp.concatenate(ydiag, axis=1) + f["e"] * jnp.concatenate(yoff, axis=1) + dsk_ref[...] * xs
        state_ref[...] = state * f["cd"] + jnp.concatenate(snew, axis=1)
        y_ref[...] = y
        zv = z_ref[...]
        yg = y * (zv * jax.nn.sigmoid(zv))
        outs = []
        for g in range(2):
            ygg = yg[:, 256 * g:256 * (g + 1)]
            outs.append(ygg * lax.rsqrt(jnp.mean(ygg * ygg, axis=-1, keepdims=True) + EPS))
        o_ref[...] = (jnp.concatenate(outs, axis=1) * nw_ref[...]).astype(BF16)

    consts = [cw, cb, dtb, alog, dskip_exp, nw, expand, tril]
    return pl.pallas_call(
        body, name="ssd_fwd", grid=(t_tok // seq, nc),
        out_shape=(jax.ShapeDtypeStruct((t_tok, SSM_WIDTH), BF16), jax.ShapeDtypeStruct((t_tok, SSM_WIDTH), F32),
                   jax.ShapeDtypeStruct((t_tok // CHUNK, N_STATE, SSM_WIDTH), F32)),
        in_specs=[row(SSM_WIDTH), row(CONV_CH), tail, row(CHUNK)] + [_full(a.shape) for a in consts],
        out_specs=(row(SSM_WIDTH), row(SSM_WIDTH),
                   pl.BlockSpec((1, N_STATE, SSM_WIDTH), lambda b, c: (chunk(b, c), 0, 0))),
        scratch_shapes=[pltpu.VMEM((CHUNK + 16, CONV_CH), F32), pltpu.VMEM((N_STATE, SSM_WIDTH), F32)],
        compiler_params=_params("arbitrary", "arbitrary"))(z, xbc, xbc, dtr, *consts)


def _out_proj(mix_a, mix_b, w_out, x, g2, g3, tm):
    t_tok = x.shape[0]

    def body(a_ref, b_ref, w_ref, x_ref, g2_ref, g3_ref, o_ref, x2_ref, h3_ref, mix_ref):
        o = _dot(a_ref[...], w_ref[0:GM_WIDTH, :]) + _dot(b_ref[...], w_ref[GM_WIDTH:, :])
        o_ref[...] = o
        mix_ref[:, 0:GM_WIDTH] = a_ref[...]
        mix_ref[:, GM_WIDTH:] = b_ref[...]
        r2 = lax.rsqrt(jnp.mean(o * o, axis=-1, keepdims=True) + EPS)
        x2 = x_ref[...] + o * r2 * g2_ref[...]
        x2_ref[...] = x2
        r3 = lax.rsqrt(jnp.mean(x2 * x2, axis=-1, keepdims=True) + EPS)
        h3_ref[...] = (x2 * r3 * g3_ref[...]).astype(BF16)

    row = lambda n: pl.BlockSpec((tm, n), lambda i: (i, 0))
    sd = lambda dt: jax.ShapeDtypeStruct((t_tok, D_MODEL), dt)
    return pl.pallas_call(
        body, name="out_proj", grid=(t_tok // tm,), out_shape=(sd(F32), sd(F32), sd(BF16), sd(BF16)),
        in_specs=[row(GM_WIDTH), row(SSM_WIDTH), _full((D_MODEL, D_MODEL)), row(D_MODEL), _full((1, D_MODEL)),
                  _full((1, D_MODEL))],
        out_specs=(row(D_MODEL),) * 4, compiler_params=_params("parallel"))(mix_a, mix_b, w_out, x, g2, g3)


def _mlp_fwd(h3, w_up, w_down, x2, target, g4, tm, tf):
    t_tok = x2.shape[0]
    nf = D_FF // tf

    def body(h_ref, wu_ref, wd_ref, x2_ref, t_ref, g4_ref, ra_ref, dd_ref, dy_ref, dg4_ref, loss_ref, acc_ref):
        i, j = pl.program_id(0), pl.program_id(1)
        ra = jnp.maximum(_dot(h_ref[...], wu_ref[...]), 0.0).astype(BF16)
        ra_ref[...] = ra
        part = _dot(ra * ra, wd_ref[...])

        @pl.when(j == 0)
        def _():
            acc_ref[...] = part

        @pl.when(j > 0)
        def _():
            acc_ref[...] += part

        @pl.when(j == nf - 1)
        def _():
            dvec = acc_ref[...]
            r4 = lax.rsqrt(jnp.mean(dvec * dvec, axis=-1, keepdims=True) + EPS)
            dn = dvec * r4
            g4 = g4_ref[...]
            err = x2_ref[...] + dn * g4 - t_ref[...]
            dy = err * (1.0 / D_MODEL)
            dy_ref[...] = dy
            dg = dy * g4
            dd_ref[...] = (r4 * (dg - dn * jnp.mean(dg * dn, axis=-1, keepdims=True))).astype(BF16)
            _acc_rows(dg4_ref, _rsum(dy * dn), i == 0)
            tile_loss = 0.5 * jnp.sum(jnp.sum(err * err, axis=-1, keepdims=True), axis=0, keepdims=True) / D_MODEL
            _acc_rows(loss_ref, jnp.broadcast_to(tile_loss, (1, 128)), i == 0)

    row = pl.BlockSpec((tm, D_MODEL), lambda i, j: (i, 0))
    return pl.pallas_call(
        body, name="mlp_fwd", grid=(t_tok // tm, nf),
        out_shape=(jax.ShapeDtypeStruct((t_tok, D_FF), BF16), jax.ShapeDtypeStruct((t_tok, D_MODEL), BF16),
                   jax.ShapeDtypeStruct((t_tok, D_MODEL), F32), jax.ShapeDtypeStruct((8, D_MODEL), F32),
                   jax.ShapeDtypeStruct((8, 128), F32)),
        in_specs=[row, pl.BlockSpec((D_MODEL, tf), lambda i, j: (0, j)),
                  pl.BlockSpec((tf, D_MODEL), lambda i, j: (j, 0)), row, row, _full((1, D_MODEL))],
        out_specs=(pl.BlockSpec((tm, tf), lambda i, j: (i, j)), row, row, _full((8, D_MODEL)), _full((8, 128))),
        scratch_shapes=[pltpu.VMEM((tm, D_MODEL), F32)],
        compiler_params=_params("arbitrary", "arbitrary"))(h3, w_up, w_down, x2, target, g4)


def _mlp_bwd(dd, w_down, ra, w_up, x2, dy, o, g3, g2, tm, tf):
    t_tok = x2.shape[0]
    nf = D_FF // tf

    def body(dd_ref, wd_ref, ra_ref, wu_ref, x2_ref, dy_ref, o_ref, g3_ref, g2_ref, da_ref, dx2_ref, do_ref, dg3_ref,
             dg2_ref, acc_ref):
        i, j = pl.program_id(0), pl.program_id(1)
        df = _dot(dd_ref[...], wd_ref[...], _NT)
        da = (df * (2.0 * ra_ref[...].astype(F32))).astype(BF16)
        da_ref[...] = da
        part = _dot(da, wu_ref[...], _NT)

        @pl.when(j == 0)
        def _():
            acc_ref[...] = part

        @pl.when(j > 0)
        def _():
            acc_ref[...] += part

        @pl.when(j == nf - 1)
        def _():
            dn3, dg3 = _rms_bwd(x2_ref[...], g3_ref[...], acc_ref[...])
            dx2 = dy_ref[...] + dn3
            dx2_ref[...] = dx2
            do, dg2 = _rms_bwd(o_ref[...], g2_ref[...], dx2)
            do_ref[...] = do.astype(BF16)
            _acc_rows(dg3_ref, dg3, i == 0)
            _acc_rows(dg2_ref, dg2, i == 0)

    row = pl.BlockSpec((tm, D_MODEL), lambda i, j: (i, 0))
    vec = _full((1, D_MODEL))
    acc = _full((8, D_MODEL))
    sd = lambda dt: jax.ShapeDtypeStruct((t_tok, D_MODEL), dt)
    return pl.pallas_call(
        body, name="mlp_bwd", grid=(t_tok // tm, nf),
        out_shape=(jax.ShapeDtypeStruct((t_tok, D_FF), BF16), sd(F32), sd(BF16),
                   jax.ShapeDtypeStruct((8, D_MODEL), F32), jax.ShapeDtypeStruct((8, D_MODEL), F32)),
        in_specs=[row, pl.BlockSpec((tf, D_MODEL), lambda i, j: (j, 0)), pl.BlockSpec((tm, tf), lambda i, j: (i, j)),
                  pl.BlockSpec((D_MODEL, tf), lambda i, j: (0, j)), row, row, row, vec, vec],
        out_specs=(pl.BlockSpec((tm, tf), lambda i, j: (i, j)), row, row, acc, acc),
        scratch_shapes=[pltpu.VMEM((tm, D_MODEL), F32)],
        compiler_params=_params("arbitrary", "arbitrary"))(dd, w_down, ra, w_up, x2, dy, o, g3, g2)


def _wgrad(a, b, out_blocks, bm, bn, bk, square_a, name, dep=None):
    t_tok, m = a.shape
    n = b.shape[1]
    nk = t_tok // bk

    def body(a_ref, b_ref, *rest):
        o_ref, acc_ref = rest[-2:]
        k = pl.program_id(2)
        av = a_ref[...]
        if square_a:
            av = av * av
        part = _dot(av, b_ref[...], _TN)

        def emit(res):
            if out_blocks is None:
                o_ref[...] = res.astype(BF16)
            else:
                o_ref[0] = res.astype(BF16)

        if nk == 1:
            emit(part)
            return

        @pl.when(k == 0)
        def _():
            acc_ref[...] = part

        @pl.when(k > 0)
        def _():
            acc_ref[...] += part

        @pl.when(k == nk - 1)
        def _():
            emit(acc_ref[...])

    if out_blocks is None:
        out_shape = jax.ShapeDtypeStruct((m, n), BF16)
        out_spec = pl.BlockSpec((bm, bn), lambda i, j, k: (i, j))
    else:
        assert n // out_blocks == bn
        out_shape = jax.ShapeDtypeStruct((out_blocks, m, bn), BF16)
        out_spec = pl.BlockSpec((1, bm, bn), lambda i, j, k: (j, i, 0))
    deps = [] if dep is None else [dep]
    return pl.pallas_call(
        body, name=name, grid=(m // bm, n // bn, nk), out_shape=out_shape,
        in_specs=[pl.BlockSpec((bk, bm), lambda i, j, k: (k, i)), pl.BlockSpec((bk, bn), lambda i, j, k: (k, j))]
        + [pl.BlockSpec(memory_space=pl.ANY)] * len(deps),
        out_specs=out_spec, scratch_shapes=[pltpu.VMEM((bm, bn) if nk > 1 else (8, 128), F32)],
        compiler_params=_params("parallel", "parallel", "arbitrary"))(a, b, *deps)


def _wgrad_in(h1, pieces, bm, bk, dep=None):
    t_tok = h1.shape[0]
    nk = t_tok // bk
    widths = [b - a for a, b in _IN_SPLITS]

    def body(h_ref, *rest):
        piece_refs = rest[:len(widths)]
        o_ref, acc_ref = rest[-2:]
        k = pl.program_id(1)
        dproj = jnp.concatenate([r[...] for r in piece_refs], axis=1)
        part = _dot(h_ref[...], dproj, _TN)

        @pl.when(k == 0)
        def _():
            acc_ref[...] = part

        @pl.when(k > 0)
        def _():
            acc_ref[...] += part

        @pl.when(k == nk - 1)
        def _():
            o_ref[...] = acc_ref[...].astype(BF16)

    deps = [] if dep is None else [dep]
    return pl.pallas_call(
        body, name="wgrad_in", grid=(D_MODEL // bm, nk), out_shape=jax.ShapeDtypeStruct((D_MODEL, IN_PAD), BF16),
        in_specs=[pl.BlockSpec((bk, bm), lambda i, k: (k, i))] + [pl.BlockSpec((bk, n), lambda i, k: (k, 0)) for n in widths]
        + [pl.BlockSpec(memory_space=pl.ANY)] * len(deps),
        out_specs=pl.BlockSpec((bm, IN_PAD), lambda i, k: (i, 0)), scratch_shapes=[pltpu.VMEM((bm, IN_PAD), F32)],
        compiler_params=_params("parallel", "arbitrary"))(h1, *pieces, *deps)


def _dmix(do, w_out, tm, dep=None):
    t_tok = do.shape[0]

    def body(d_ref, w_ref, *rest):
        rest[-1][...] = _dot(d_ref[...], w_ref[...], _NT)

    row = pl.BlockSpec((tm, D_MODEL), lambda i: (i, 0))
    deps = [] if dep is None else [dep]
    return pl.pallas_call(
        body, name="dmix", grid=(t_tok // tm,), out_shape=jax.ShapeDtypeStruct((t_tok, D_MODEL), F32),
        in_specs=[row, _full((D_MODEL, D_MODEL))] + [pl.BlockSpec(memory_space=pl.ANY)] * len(deps), out_specs=row,
        compiler_params=_params("parallel"))(do, w_out, *deps)


def _gmlp_bwd(dmix, u, v, lnw, lnb, wcat, wtcat, bias, avg, expand_t):
    t_tok = u.shape[0]

    def body(dm_ref, u_ref, v_ref, lnw_ref, lnb_ref, wcat_ref, wtcat_ref, bias_ref, avg_ref, expt_ref, du_ref, dv_ref,
             dw_ref, db_ref, dlnw_ref, dlnb_ref):
        i = pl.program_id(0)
        m_l, m_r = _lane_masks()
        avg = avg_ref[...]
        lnw = lnw_ref[...]
        ug, dug, dvg, rstd, vhat, vn, mixed = _gmlp_common(
            u_ref[...], v_ref[...], lnw, lnb_ref[...], avg, wcat_ref, bias_ref[...], m_l, m_r)
        dya = dm_ref[...]
        du_ref[...] = (dya * mixed * dug).astype(BF16)
        dmixed = dya * ug
        dvn_cols, dws = [], []
        for j in range(N_HEADS // 2):
            dmp = dmixed[:, 128 * j:128 * (j + 1)]
            dvn_cols.append(_dot(wtcat_ref[j], _stack_pair(dmp, m_l, m_r)))
            vnp = vn[:, 128 * j:128 * (j + 1)].astype(BF16)
            dws.append(_dot((dmp * m_l).astype(BF16), vnp, _NT))
            dws.append(_dot((dmp * m_r).astype(BF16), vnp, _NT))
        dvn = jnp.concatenate(dvn_cols, axis=1)
        dvh = dvn * lnw
        dvgel = rstd * (dvh - _split_dot(dvh, avg, 2) - vhat * _split_dot(dvh * vhat, avg, 2))
        dv_ref[...] = (dvgel * dvg).astype(BF16)
        dbt = _split_dot(dmixed, expt_ref[...], 2)
        first = i == 0

        @pl.when(first)
        def _():
            for h in range(N_HEADS):
                dw_ref[h] = dws[h]
            db_ref[...] = dbt

        @pl.when(jnp.logical_not(first))
        def _():
            for h in range(N_HEADS):
                dw_ref[h] += dws[h]
            db_ref[...] += dbt

        _acc_rows(dlnw_ref, _rsum(dvn * vhat), first)
        _acc_rows(dlnb_ref, _rsum(dvn), first)

    row = pl.BlockSpec((CHUNK, GM_WIDTH), lambda i: (i, 0))
    consts = [lnw, lnb, wcat, wtcat, bias, avg, expand_t]
    return pl.pallas_call(
        body, name="gmlp_bwd", grid=(t_tok // CHUNK,),
        out_shape=(jax.ShapeDtypeStruct((t_tok, GM_WIDTH), BF16), jax.ShapeDtypeStruct((t_tok, GM_WIDTH), BF16),
                   jax.ShapeDtypeStruct((N_HEADS, CHUNK, CHUNK), F32), jax.ShapeDtypeStruct((CHUNK, CHUNK), F32),
                   jax.ShapeDtypeStruct((8, GM_WIDTH), F32), jax.ShapeDtypeStruct((8, GM_WIDTH), F32)),
        in_specs=[pl.BlockSpec((CHUNK, GM_WIDTH), lambda i: (i, 0)), row, row] + [_full(a.shape) for a in consts],
        out_specs=(row, row, _full((N_HEADS, CHUNK, CHUNK)), _full((CHUNK, CHUNK)), _full((8, GM_WIDTH)),
                   _full((8, GM_WIDTH))),
        compiler_params=_params("arbitrary"))(dmix, u, v, *consts)


def _ssd_bwd(dmix, z, xbc, dtr, y, states, cw, cb, dtb, alog, dskip_exp, nw, expand, expand_t, tril, triu, seq,
             dep=None):
    t_tok = z.shape[0]
    nc, chunk, row, tail = _ssd_specs(t_tok, seq, True)
    q = CHUNK

    def body(dm_ref, z_ref, xbc_ref, tail_ref, dtr_ref, y_ref, st_ref, cw_ref, cb_ref, dtb_ref, alog_ref, dsk_ref,
             nw_ref, exp_ref, expt_ref, tril_ref, triu_ref, dz_ref, dxbc_ref, ddt_ref, dcw_ref, dcb_ref, ddtb_ref,
             dalog_ref, dd_ref, dnw_ref, xext_ref, dext_ref, dstate_ref):
        b, c = pl.program_id(0), pl.program_id(1)
        first = jnp.logical_and(b == 0, c == 0)

        @pl.when(c == 0)
        def _():
            dstate_ref[...] = jnp.zeros_like(dstate_ref)
            dext_ref[q:q + 8, :] = jnp.zeros((8, CONV_CH), F32)

        _fill_xext(xext_ref, tail_ref, xbc_ref, c == nc - 1)
        m_l, m_r = _lane_masks()
        expt = expt_ref[...]
        f = _ssd_common(xext_ref, dtr_ref[...], cw_ref, cb_ref[...], dtb_ref[...], alog_ref[...], exp_ref[...],
                        tril_ref[...])
        act, pre, sg = f["act"], f["pre"], f["sg"]
        xs = act[:, :SSM_WIDTH]
        xdt = xs * f["dt_exp"]
        xw = xdt * f["w_end"]
        state = st_ref[0]
        dstate = dstate_ref[...]
        zv, yv, dout, nw = z_ref[...], y_ref[...], dm_ref[...], nw_ref[...]
        sz = jax.nn.sigmoid(zv)
        sl = zv * sz
        yg = yv * sl
        tv = dout * nw
        dyg_parts, ygh_parts = [], []
        for g in range(2):
            ygg = yg[:, 256 * g:256 * (g + 1)]
            rr = lax.rsqrt(jnp.mean(ygg * ygg, axis=-1, keepdims=True) + EPS)
            ygh = ygg * rr
            tg = tv[:, 256 * g:256 * (g + 1)]
            dyg_parts.append(rr * (tg - ygh * jnp.mean(tg * ygh, axis=-1, keepdims=True)))
            ygh_parts.append(ygh)
        dyg = jnp.concatenate(dyg_parts, axis=1)
        dnw = _rsum(dout * jnp.concatenate(ygh_parts, axis=1))
        dy = dyg * sl
        dz_ref[...] = (dyg * yv * (sz * (1.0 + zv * (1.0 - sz)))).astype(BF16)
        ddsk = _rsum(dy * xs)
        dye = dy * f["e"]
        lane = lax.broadcasted_iota(jnp.int32, (q, q), 1)
        sub = lax.broadcasted_iota(jnp.int32, (q, q), 0)
        rs_mat = jnp.zeros((q, q), F32)
        cs_mat = jnp.zeros((q, q), F32)
        dxdt_cols, yoff, dst_in, dxw, d_b, d_c = [], [], [], [], [], []
        for g in range(2):
            bg = act[:, 512 + 128 * g:640 + 128 * g].astype(BF16)
            cg = act[:, 768 + 128 * g:896 + 128 * g].astype(BF16)
            cb_mat = _dot(cg, bg, _NT)
            stg = state[:, 256 * g:256 * (g + 1)].astype(BF16)
            dyeg = dye[:, 256 * g:256 * (g + 1)].astype(BF16)
            yoff.append(_dot(cg, stg))
            dcg = _dot(dyeg, stg, _NT)
            dst_in.append(_dot(cg, dyeg, _TN))
            dcb = jnp.zeros((q, q), F32)
            for pr in range(2):
                h0 = 4 * g + 2 * pr
                gf = [cb_mat * f["decay"][h0], cb_mat * f["decay"][h0 + 1]]
                gcat = jnp.concatenate([gf[0].astype(BF16), gf[1].astype(BF16)], axis=1)
                xst = _stack_pair(xdt[:, 64 * h0:64 * h0 + 128], m_l, m_r)
                dyp = dy[:, 64 * h0:64 * h0 + 128].astype(BF16)
                dgcat = _dot(dyp, xst, _NT)
                dxst = _dot(gcat, dyp, _TN)
                dxdt_cols.append(dxst[:q] * m_l + dxst[q:] * m_r)
                for i in range(2):
                    h = h0 + i
                    dg = dgcat[:, q * i:q * (i + 1)]
                    mm = dg * gf[i]
                    rs_mat = rs_mat + jnp.where(lane == h, jnp.sum(mm, axis=1, keepdims=True), 0.0)
                    cs_mat = cs_mat + jnp.where(sub == h, jnp.sum(mm, axis=0, keepdims=True), 0.0)
                    dcb = dcb + dg * f["decay"][h]
            dcb16 = dcb.astype(BF16)
            dstg = dstate[:, 256 * g:256 * (g + 1)].astype(BF16)
            d_c.append(dcg + _dot(dcb16, bg))
            dxw.append(_dot(bg, dstg))
            d_b.append(_dot(dcb16, cg, _TN) + _dot(xw[:, 256 * g:256 * (g + 1)].astype(BF16), dstg, _NT))
        dxw = jnp.concatenate(dxw, axis=1)
        dxdt = jnp.concatenate(dxdt_cols, axis=1) + dxw * f["w_end"]
        qv = dxw * xw
        end_row = _rsum(qv) + _rsum(dstate * state) * f["cd"]
        x2 = dye * jnp.concatenate(yoff, axis=1) - qv
        row_i = lax.broadcasted_iota(jnp.int32, (q, 1), 0)
        x2 = x2 + jnp.where(row_i == q - 1, end_row, 0.0)
        da_cs = _split_dot(x2, expt, 3) + rs_mat - cs_mat.T
        ddt = _split_dot(dxdt * xs, expt, 3)
        dxs = dsk_ref[...] * dy + dxdt * f["dt_exp"]
        dda = _split_dot_left(triu_ref[...], da_cs, 3)
        ddt = ddt + dda * f["a_row"]
        dalog = _rsum(dda * f["dt"]) * f["a_row"]
        draw = ddt * jax.nn.sigmoid(f["dtp"])
        ddt_ref[...] = draw.astype(BF16)
        dact = jnp.concatenate([dxs] + d_b + d_c, axis=1)
        dpre = dact * (sg * (1.0 + pre * (1.0 - sg)))
        dext_ref[0:q, :] = dpre
        dxbc = cw_ref[0:1, :] * dext_ref[pl.ds(3, q), :]
        for k in range(1, 4):
            dxbc = dxbc + cw_ref[k:k + 1, :] * dext_ref[pl.ds(3 - k, q), :]
        dxbc_ref[...] = dxbc.astype(BF16)
        dext_ref[q:q + 8, :] = dpre[0:8, :]
        dstate_ref[...] = dstate * f["cd"] + jnp.concatenate(dst_in, axis=1)
        row8 = lax.broadcasted_iota(jnp.int32, (8, 1), 0)
        dcw = jnp.zeros((8, CONV_CH), F32)
        for k in range(4):
            dcw = dcw + jnp.where(row8 == k, _rsum(dpre * f["taps"][k]), 0.0)

        @pl.when(first)
        def _():
            dcw_ref[...] = dcw

        @pl.when(jnp.logical_not(first))
        def _():
            dcw_ref[...] += dcw

        _acc_rows(dcb_ref, _rsum(dpre), first)
        _acc_rows(ddtb_ref, _rsum(draw), first)
        _acc_rows(dalog_ref, dalog, first)
        _acc_rows(dd_ref, ddsk, first)
        _acc_rows(dnw_ref, dnw, first)

    consts = [cw, cb, dtb, alog, dskip_exp, nw, expand, expand_t, tril, triu]
    deps = [] if dep is None else [dep]
    n_in = 7 + len(consts)

    def body_skipping_dep(*refs):
        body(*refs[:n_in], *refs[n_in + len(deps):])

    acc = lambda n: jax.ShapeDtypeStruct((8, n), F32)
    return pl.pallas_call(
        body_skipping_dep, name="ssd_bwd", grid=(t_tok // seq, nc),
        out_shape=(jax.ShapeDtypeStruct((t_tok, SSM_WIDTH), BF16), jax.ShapeDtypeStruct((t_tok, CONV_CH), BF16),
                   jax.ShapeDtypeStruct((t_tok, CHUNK), BF16), acc(CONV_CH), acc(CONV_CH), acc(CHUNK), acc(CHUNK),
                   acc(SSM_WIDTH), acc(SSM_WIDTH)),
        in_specs=[pl.BlockSpec((CHUNK, SSM_WIDTH), lambda b, c: (chunk(b, c), 1)), row(SSM_WIDTH), row(CONV_CH), tail,
                  row(CHUNK), row(SSM_WIDTH), pl.BlockSpec((1, N_STATE, SSM_WIDTH), lambda b, c: (chunk(b, c), 0, 0))]
        + [_full(a.shape) for a in consts] + [pl.BlockSpec(memory_space=pl.ANY)] * len(deps),
        out_specs=(row(SSM_WIDTH), row(CONV_CH), row(CHUNK), _full((8, CONV_CH)), _full((8, CONV_CH)),
                   _full((8, CHUNK)), _full((8, CHUNK)), _full((8, SSM_WIDTH)), _full((8, SSM_WIDTH))),
        scratch_shapes=[pltpu.VMEM((CHUNK + 16, CONV_CH), F32), pltpu.VMEM((CHUNK + 8, CONV_CH), F32),
                        pltpu.VMEM((N_STATE, SSM_WIDTH), F32)],
        compiler_params=_params("arbitrary", "arbitrary"))(dmix, z, xbc, xbc, dtr, y, states, *consts, *deps)


def _in_bwd(du, dv, dz, dxbc, ddt, w_in, x, dx2, g1, tm, dep=None):
    t_tok = x.shape[0]

    def body(du_ref, dv_ref, dz_ref, dxbc_ref, ddt_ref, w_ref, x_ref, dx2_ref, g_ref, *rest):
        gx_ref, dg_ref = rest[-2:]
        i = pl.program_id(0)
        dh = None
        for (a, b), ref in zip(_IN_SPLITS, (du_ref, dv_ref, dz_ref, dxbc_ref, ddt_ref)):
            part = _dot(ref[...], w_ref[:, a:b], _NT)
            dh = part if dh is None else dh + part
        dn, dg = _rms_bwd(x_ref[...], g_ref[...], dh)
        gx_ref[...] = dx2_ref[...] + dn
        _acc_rows(dg_ref, dg, i == 0)

    row = lambda n: pl.BlockSpec((tm, n), lambda i: (i, 0))
    widths = [b - a for a, b in _IN_SPLITS]
    deps = [] if dep is None else [dep]
    return pl.pallas_call(
        body, name="in_bwd", grid=(t_tok // tm,),
        out_shape=(jax.ShapeDtypeStruct((t_tok, D_MODEL), F32), jax.ShapeDtypeStruct((8, D_MODEL), F32)),
        in_specs=[row(n) for n in widths] + [_full((D_MODEL, IN_PAD)), row(D_MODEL), row(D_MODEL), _full((1, D_MODEL))]
        + [pl.BlockSpec(memory_space=pl.ANY)] * len(deps),
        out_specs=(row(D_MODEL), _full((8, D_MODEL))),
        compiler_params=_params("arbitrary"))(du, dv, dz, dxbc, ddt, w_in, x, dx2, g1, *deps)


def _pad_lanes(a, n):
    return jnp.pad(a, ((0, 0), (0, n - a.shape[1])))


def _local_step(x, target, seq, w_in_p, conv_w, small, hooks):
    t_tok = x.shape[0]
    tm = min(512, t_tok)
    avg, expand, expand_t, tril, triu = _const_mats()
    g1, g2, g3, g4 = (small[k].reshape(1, D_MODEL) for k in
                      ("norm_mix_pre", "norm_mix_post", "norm_ffn_pre", "norm_ffn_post"))
    lnw = small["gm_ln_w"].reshape(1, GM_WIDTH)
    lnb = small["gm_ln_b"].reshape(1, GM_WIDTH)
    causal = jnp.tril(jnp.ones((CHUNK, CHUNK), F32))
    wm = small["gm_w_s"] * causal
    pair = lambda w: w.reshape(4, 2, CHUNK, CHUNK).transpose(0, 2, 1, 3).reshape(4, CHUNK, 2 * CHUNK).astype(BF16)
    wcat = pair(wm)
    wtcat = pair(jnp.swapaxes(wm, 1, 2))
    bias = jnp.repeat(small["gm_b_s"].T, HEAD_DIM, axis=1)
    cb = small["conv_b"].reshape(1, CONV_CH)
    dtb = _pad_lanes(small["dt_bias"].reshape(1, N_HEADS), CHUNK)
    alog = _pad_lanes(small["a_log"].reshape(1, N_HEADS), CHUNK)
    dskip_exp = jnp.repeat(small["d_skip"].reshape(1, N_HEADS), HEAD_DIM, axis=1)
    nw = small["ssm_norm_w"].reshape(1, SSM_WIDTH)

    h1, u, v, z, xbc, dtr = _in_proj(x, g1, w_in_p, tm)
    mix_a = _gmlp_fwd(u, v, lnw, lnb, wcat, bias, avg)
    mix_b, y_pre, states = _ssd_fwd(z, xbc, dtr, conv_w, cb, dtb, alog, dskip_exp, nw, expand, tril, seq)
    w_out, w_up, w_down = hooks["late_weights"](mix_b)
    o, x2, h3, mix = _out_proj(mix_a, mix_b, w_out, x, g2, g3, tm)
    tf = 2048
    ra, dd, dy, dg4, loss = _mlp_fwd(h3, w_up, w_down, x2, target, g4, tm, tf)

    da, dx2, do, dg3, dg2 = _mlp_bwd(dd, w_down, ra, w_up, x2, dy, o, g3, g2, tm, tf)
    bk = min(2048, t_tok)
    g_w_down = _wgrad(ra, dd, None, 512, 512, t_tok, True, "wgrad_down")
    g_w_up = _wgrad(h3, da, N_DEV, 512, D_FF // N_DEV, t_tok, False, "wgrad_up")
    dep = hooks["mlp_grads"](g_w_down, g_w_up)
    dmix = _dmix(do, w_out, tm, dep)
    g_w_out = _wgrad(mix, do, None, 512, 512, t_tok, False, "wgrad_out", dep)
    du, dv, dws, dbt, dlnw, dlnb = _gmlp_bwd(dmix, u, v, lnw, lnb, wcat, wtcat, bias, avg, expand_t)
    dep = hooks["gmlp_grads"](g_w_out, dws)
    dz, dxbc, ddt, dcw, dcb, ddtb, dalog, ddsk, dnw = _ssd_bwd(
        dmix, z, xbc, dtr, y_pre, states, conv_w, cb, dtb, alog, dskip_exp, nw, expand, expand_t, tril, triu, seq, dep)
    g_w_in = _wgrad_in(h1, (du, dv, dz, dxbc, ddt), 512, bk, dep)
    dep = hooks["in_grads"](g_w_in, dcw[0:4])
    grad_x, dg1 = _in_bwd(du, dv, dz, dxbc, ddt, w_in_p, x, dx2, g1, tm, dep)

    grads = dict(
        w_in=g_w_in, w_out=g_w_out, w_up=g_w_up, w_down=g_w_down, conv_w=dcw[0:4],
        norm_mix_pre=dg1[0:1], norm_mix_post=dg2[0:1], norm_ffn_pre=dg3[0:1], norm_ffn_post=dg4[0:1],
        gm_ln_w=dlnw[0:1], gm_ln_b=dlnb[0:1], gm_w_s=dws, gm_b_s=dbt.T[0:N_HEADS], conv_b=dcb[0:1],
        dt_bias=ddtb[0:1, 0:N_HEADS], a_log=dalog[0:1, 0:N_HEADS],
        d_skip=ddsk[0:1].reshape(N_HEADS, HEAD_DIM).sum(axis=1).reshape(1, N_HEADS), ssm_norm_w=dnw[0:1])
    return loss[0, 0], grad_x, grads


_SMALL_ROW_PARAMS = ("norm_mix_pre", "norm_mix_post", "norm_ffn_pre", "norm_ffn_post", "gm_ln_w", "gm_ln_b", "gm_b_s",
                     "conv_b", "dt_bias", "a_log", "d_skip", "ssm_norm_w")
_WEIGHTS = ("norm_mix_pre", "w_in", "gm_ln_w", "gm_ln_b", "gm_w_s", "gm_b_s", "conv_w", "conv_b", "dt_bias", "a_log",
            "d_skip", "ssm_norm_w", "w_out", "norm_mix_post", "norm_ffn_pre", "w_up", "w_down", "norm_ffn_post")


def _pack_rows(tensors):
    rows = [_pad_lanes(t.reshape(1, -1), D_MODEL) for t in tensors]
    rows.append(jnp.zeros((SMALL_ROWS - len(rows), D_MODEL), F32))
    return jnp.concatenate(rows, axis=0)


def kernel(x, norm_mix_pre, w_in, gm_ln_w, gm_ln_b, gm_w_s, gm_b_s, conv_w, conv_b, dt_bias, a_log, d_skip, ssm_norm_w, w_out, norm_mix_post, norm_ffn_pre, w_up, w_down, norm_ffn_post, loss_target, m_norm_mix_pre, m_w_in, m_gm_ln_w, m_gm_ln_b, m_gm_w_s, m_gm_b_s, m_conv_w, m_conv_b, m_dt_bias, m_a_log, m_d_skip, m_ssm_norm_w, m_w_out, m_norm_mix_post, m_norm_ffn_pre, m_w_up, m_w_down, m_norm_ffn_post, v_norm_mix_pre, v_w_in, v_gm_ln_w, v_gm_ln_b, v_gm_w_s, v_gm_b_s, v_conv_w, v_conv_b, v_dt_bias, v_a_log, v_d_skip, v_ssm_norm_w, v_w_out, v_norm_mix_post, v_norm_ffn_pre, v_w_up, v_w_down, v_norm_ffn_post):
    w = dict(norm_mix_pre=norm_mix_pre, w_in=w_in, gm_ln_w=gm_ln_w, gm_ln_b=gm_ln_b, gm_w_s=gm_w_s, gm_b_s=gm_b_s, conv_w=conv_w, conv_b=conv_b, dt_bias=dt_bias, a_log=a_log, d_skip=d_skip, ssm_norm_w=ssm_norm_w, w_out=w_out, norm_mix_post=norm_mix_post, norm_ffn_pre=norm_ffn_pre, w_up=w_up, w_down=w_down, norm_ffn_post=norm_ffn_post)
    m = dict(norm_mix_pre=m_norm_mix_pre, w_in=m_w_in, gm_ln_w=m_gm_ln_w, gm_ln_b=m_gm_ln_b, gm_w_s=m_gm_w_s, gm_b_s=m_gm_b_s, conv_w=m_conv_w, conv_b=m_conv_b, dt_bias=m_dt_bias, a_log=m_a_log, d_skip=m_d_skip, ssm_norm_w=m_ssm_norm_w, w_out=m_w_out, norm_mix_post=m_norm_mix_post, norm_ffn_pre=m_norm_ffn_pre, w_up=m_w_up, w_down=m_w_down, norm_ffn_post=m_norm_ffn_post)
    v = dict(norm_mix_pre=v_norm_mix_pre, w_in=v_w_in, gm_ln_w=v_gm_ln_w, gm_ln_b=v_gm_ln_b, gm_w_s=v_gm_w_s, gm_b_s=v_gm_b_s, conv_w=v_conv_w, conv_b=v_conv_b, dt_bias=v_dt_bias, a_log=v_a_log, d_skip=v_d_skip, ssm_norm_w=v_ssm_norm_w, w_out=v_w_out, norm_mix_post=v_norm_mix_post, norm_ffn_pre=v_norm_ffn_pre, w_up=v_w_up, w_down=v_w_down, norm_ffn_post=v_norm_ffn_post)
    n_batch, seq, _ = x.shape
    shard_in = IN_COLS // N_DEV

    me = (4 * lax.axis_index("x") + 2 * lax.axis_index("y") + lax.axis_index("c")).astype(jnp.int32).reshape(1)

    def in_slot(own):
        return lax.dynamic_update_slice(lax.empty((N_DEV,) + own.shape, own.dtype), own[None],
                                        (me[0],) + (0,) * own.ndim)

    gat_in, _ = _exchange_start(
        [_cast_to_slot(w_in[0], me, 256, "cast_w_in"), in_slot(conv_w[0])], [True, True], _ALL_PEERS, "gather_in_start")
    (_, ag_in), (_, ag_conv) = _exchange_wait(gat_in, me, "gather_in_wait")
    gat_mlp, tok_mlp = _exchange_start(
        [_cast_to_slot(w_out[0], me, 128, "cast_w_out"), _cast_to_slot(w_up[0], me, 256, "cast_w_up", cols=True),
         _cast_to_slot(w_down[0], me, 256, "cast_w_down")], [True] * 3, _ALL_PEERS, "gather_mlp_start", dep=ag_conv)
    w_in_p = _pad_lanes(ag_in.transpose(1, 0, 2).reshape(D_MODEL, IN_COLS), IN_PAD)
    conv_w_f = ag_conv.transpose(1, 0, 2).reshape(4, CONV_CH)

    def late_weights(after):
        (_, ag_out), (_, ag_up), (_, ag_down) = _exchange_wait(gat_mlp, after, "gather_mlp_wait")
        return ag_out.reshape(D_MODEL, D_MODEL), ag_up, ag_down.reshape(D_FF, D_MODEL)

    sent = {}

    def mlp_grads(g_w_down, g_w_up):
        sent["mlp"], tok = _exchange_start(
            [g_w_down.reshape(N_DEV, D_FF // N_DEV, D_MODEL), g_w_up], [False, False], _ALL_PEERS, "grads_mlp_start")
        return tok

    def gmlp_grads(g_w_out, g_w_s):
        sent["gmlp"], tok = _exchange_start(
            [g_w_out.reshape(N_DEV, D_MODEL // N_DEV, D_MODEL), in_slot(g_w_s.astype(BF16))], [False, True], _ALL_PEERS,
            "grads_gmlp_start")
        return tok

    def in_grads(g_w_in, g_conv_w):
        g_in_blk = g_w_in[:, :IN_COLS].reshape(D_MODEL, N_DEV, shard_in).transpose(1, 0, 2)
        g_conv_blk = g_conv_w.reshape(4, N_DEV, CONV_CH // N_DEV).transpose(1, 0, 2)
        sent["in"], tok = _exchange_start([g_in_blk, g_conv_blk], [False, False], _ALL_PEERS, "grads_in_start")
        return tok

    small = {k: w[k][0] for k in _SMALL_ROW_PARAMS + ("gm_w_s",)}
    small["norm_mix_pre"] = small["norm_mix_pre"] + tok_mlp[0:1, 0:1]
    loss_part, grad_x, g = _local_step(
        x.reshape(n_batch * seq, D_MODEL), loss_target.reshape(n_batch * seq, D_MODEL), seq, w_in_p, conv_w_f, small,
        dict(late_weights=late_weights, mlp_grads=mlp_grads, gmlp_grads=gmlp_grads, in_grads=in_grads))
    loss = lax.psum(loss_part, ("x", "y", "c"))

    sent_rows, tok_rows = _exchange_start(
        [in_slot(_pack_rows([g[k] for k in _SMALL_ROW_PARAMS]))], [True], _ALL_PEERS, "grads_rows_start")
    (own_down, p_down), (own_up, p_up) = _exchange_wait(sent["mlp"], tok_rows, "grads_mlp_wait")
    res = {}
    res["w_up"] = _adamw_reduce(p_up, own_up, me, w_up[0], m_w_up[0], v_w_up[0], 256, "adamw_w_up")
    res["w_down"] = _adamw_reduce(p_down, own_down, me, w_down[0], m_w_down[0], v_w_down[0], 128, "adamw_w_down")
    (own_out, p_out), (_, p_ws) = _exchange_wait(sent["gmlp"], res["w_down"][1], "grads_gmlp_wait")
    res["w_out"] = _adamw_reduce(p_out, own_out, me, w_out[0], m_w_out[0], v_w_out[0], 128, "adamw_w_out")
    causal = jnp.tril(jnp.ones((1, CHUNK, CHUNK), F32))
    res["gm_w_s"] = _adamw_small(p_ws, None, me, gm_w_s[0], m_gm_w_s[0], v_gm_w_s[0], causal, "adamw_gm_w_s")
    (own_in, p_in), (own_conv, p_conv) = _exchange_wait(sent["in"], res["gm_w_s"][1], "grads_in_wait")
    res["w_in"] = _adamw_reduce(p_in, own_in, me, w_in[0], m_w_in[0], v_w_in[0], 256, "adamw_w_in")
    res["conv_w"] = _adamw_small(p_conv, own_conv, me, conv_w[0], m_conv_w[0], v_conv_w[0], None, "adamw_conv_w")
    ((_, p_rows),) = _exchange_wait(sent_rows, res["w_in"][1], "grads_rows_wait")
    rows = _adamw_small(p_rows, None, me, _pack_rows([w[k] for k in _SMALL_ROW_PARAMS]),
                        _pack_rows([m[k] for k in _SMALL_ROW_PARAMS]), _pack_rows([v[k] for k in _SMALL_ROW_PARAMS]),
                        None, "adamw_rows")
    for i, k in enumerate(_SMALL_ROW_PARAMS):
        size = int(np.prod(w[k].shape))
        res[k] = tuple(r[i, :size].reshape(w[k].shape) for r in rows)
    for k in ("w_in", "w_out", "w_up", "w_down", "conv_w", "gm_w_s"):
        res[k] = tuple(r.reshape(w[k].shape) for r in res[k])

    outs = [loss, grad_x.reshape(x.shape)]
    for part in range(4):
        outs.extend(res[k][part] for k in _WEIGHTS)
    return tuple(outs)
```

```python
import functools

import jax
import jax.numpy as jnp
import numpy as np
from jax import lax
from jax.experimental import pallas as pl
from jax.experimental.pallas import tpu as pltpu

F32 = jnp.float32
BF16 = jnp.bfloat16

D_MODEL = 1024
GM_WIDTH = 512
SSM_WIDTH = 512
CONV_CH = 1024
N_HEADS = 8
HEAD_DIM = 64
N_STATE = 128
CHUNK = 128
D_FF = 4096
IN_COLS = 2568
IN_PAD = 2688
N_DEV = 8
EPS = 1e-6
ADAM_LR, ADAM_B1, ADAM_B2, ADAM_EPS, ADAM_WD, ADAM_STEP = 0.001, 0.9, 0.999, 1e-08, 0.01, 10
VMEM_LIMIT_BYTES = 56 * 1024 * 1024
SMALL_ROWS = 16

_NT = (((1,), (1,)), ((), ()))
_TN = (((0,), (0,)), ((), ()))


def _params(*sem):
    return pltpu.CompilerParams(dimension_semantics=sem or None, vmem_limit_bytes=VMEM_LIMIT_BYTES)


def _dot(a, b, dims=None):
    if dims is None:
        return jnp.dot(a, b, preferred_element_type=F32)
    return lax.dot_general(a, b, dims, preferred_element_type=F32)


def _split_terms(x, terms):
    out, rem = [], x
    for i in range(terms):
        hi = rem.astype(BF16)
        out.append(hi)
        if i + 1 < terms:
            rem = rem - hi.astype(F32)
    return out


def _split_dot(x, m, terms):
    acc = None
    for hi in _split_terms(x, terms):
        part = _dot(hi, m)
        acc = part if acc is None else acc + part
    return acc


def _split_dot_left(m, x, terms):
    acc = None
    for hi in _split_terms(x, terms):
        part = _dot(m, hi)
        acc = part if acc is None else acc + part
    return acc


def _gelu_and_grad(x):
    c = 0.7978845608028654
    inner = c * (x + 0.044715 * x * x * x)
    t = jnp.tanh(inner)
    g = 0.5 * x * (1.0 + t)
    dg = 0.5 * (1.0 + t) + 0.5 * x * (1.0 - t * t) * c * (1.0 + 3.0 * 0.044715 * x * x)
    return g, dg


def _softplus(x):
    return jnp.maximum(x, 0.0) + jnp.log(1.0 + jnp.exp(-jnp.abs(x)))


def _rsum(x):
    return jnp.sum(x, axis=0, keepdims=True)


def _acc_rows(ref, part, first):
    val = jnp.broadcast_to(part, ref.shape)

    @pl.when(first)
    def _():
        ref[...] = val

    @pl.when(jnp.logical_not(first))
    def _():
        ref[...] += val


def _rms_bwd(n, g, dout):
    r = lax.rsqrt(jnp.mean(n * n, axis=-1, keepdims=True) + EPS)
    nh = n * r
    dg = dout * g
    dn = r * (dg - nh * jnp.mean(dg * nh, axis=-1, keepdims=True))
    return dn, _rsum(dout * nh)


def _const_mats():
    avg = np.kron(np.eye(N_HEADS), np.full((HEAD_DIM, HEAD_DIM), 1.0 / HEAD_DIM))
    expand = np.zeros((CHUNK, SSM_WIDTH), np.float32)
    for h in range(N_HEADS):
        expand[h, h * HEAD_DIM:(h + 1) * HEAD_DIM] = 1.0
    tril = np.tril(np.ones((CHUNK, CHUNK), np.float32))
    as_bf16 = lambda a: jnp.asarray(a, dtype=BF16)
    return as_bf16(avg), as_bf16(expand), as_bf16(expand.T), as_bf16(tril), as_bf16(tril.T)


def _full(shape):
    nd = len(shape)
    return pl.BlockSpec(shape, lambda *_: (0,) * nd)


_HBM = pl.BlockSpec(memory_space=pltpu.HBM)
_SEM = pl.BlockSpec(memory_space=pltpu.SEMAPHORE)
_ALL_PEERS = tuple((k, 0) for k in range(1, N_DEV))
_SAME_CORE_PEERS = ((2, 0), (4, 0), (6, 0))
_SIBLING_FORWARD = ((1, 0), (1, 2), (1, 4), (1, 6))


def _flip(j, k):
    for bit in (4, 2, 1):
        if k & bit:
            j = j + bit - 2 * (j & bit)
    return j


def _copies(src, land, send_sems, recv_sems, hops):
    x, y, c = lax.axis_index("x"), lax.axis_index("y"), lax.axis_index("c")
    me = 4 * x + 2 * y + c
    out = []
    for t in range(len(src)):
        for i, (k, b) in enumerate(hops):
            pos = (1 - x if k & 4 else x, 1 - y if k & 2 else y, 1 - c if k & 1 else c)
            peer = _flip(me, k)
            sem = t * len(hops) + i
            mk = functools.partial(pltpu.make_async_remote_copy, send_sem=send_sems.at[sem], recv_sem=recv_sems.at[sem],
                                   device_id=pos, device_id_type=pl.DeviceIdType.MESH)
            if land[t] is None and src[t].shape[0] != N_DEV:
                width = src[t].shape[1] // N_DEV
                slab = lambda j: src[t].at[:, pl.ds(pl.multiple_of(j * width, 128), width)]
                mine = functools.partial(mk, src_ref=slab(_flip(me, b)), dst_ref=slab(_flip(me, b)))
                theirs = functools.partial(mk, src_ref=slab(_flip(peer, b)), dst_ref=slab(_flip(peer, b)))
            elif land[t] is None:
                mine = functools.partial(mk, src_ref=src[t].at[_flip(me, b)], dst_ref=src[t].at[_flip(me, b)])
                theirs = functools.partial(mk, src_ref=src[t].at[_flip(peer, b)], dst_ref=src[t].at[_flip(peer, b)])
            else:
                assert b == 0
                mine = functools.partial(mk, src_ref=src[t].at[peer], dst_ref=land[t].at[me])
                theirs = functools.partial(mk, src_ref=src[t].at[peer], dst_ref=land[t].at[peer])
            out.append((mine, theirs))
    return out


def _exchange_start(srcs, inplace, peers, name, dep=None):
    n = len(srcs)
    lands = [None if ip else pltpu.with_memory_space_constraint(lax.empty(s.shape, s.dtype), pltpu.HBM)
             for s, ip in zip(srcs, inplace)]
    real_lands = [l for l in lands if l is not None]
    n_l = len(real_lands)
    deps = [] if dep is None else [dep]

    def body(*refs):
        src = refs[:n]
        land_refs = list(refs[n:n + n_l])
        send_sems, recv_sems = refs[n + n_l + len(deps)], refs[n + n_l + len(deps) + 1]
        token = refs[-1]
        land = [None if ip else land_refs.pop(0) for ip in inplace]
        for mine, _ in _copies(src, land, send_sems, recv_sems, peers):
            mine().start()
        token[...] = jnp.zeros_like(token)

    sem_t = pltpu.SemaphoreType.DMA((n * len(peers),))
    outs = pl.pallas_call(
        body, name=name,
        out_shape=(sem_t, sem_t) + tuple(pltpu.HBM(a.shape, a.dtype) for a in list(srcs) + real_lands)
        + (jax.ShapeDtypeStruct((8, 128), F32),),
        in_specs=[_HBM] * (n + n_l) + [pl.BlockSpec(memory_space=pl.ANY)] * len(deps),
        out_specs=(_SEM, _SEM) + (_HBM,) * (n + n_l) + (pl.BlockSpec(memory_space=pltpu.VMEM),),
        input_output_aliases={i: 2 + i for i in range(n + n_l)},
        compiler_params=pltpu.CompilerParams(has_side_effects=pltpu.SideEffectType.DATAFLOW_SIDE_EFFECTING),
    )(*[pltpu.with_memory_space_constraint(s, pltpu.HBM) for s in srcs], *real_lands, *deps)
    handle = dict(send=outs[0], recv=outs[1], srcs=outs[2:2 + n], lands=outs[2 + n:2 + n + n_l], inplace=inplace,
                  peers=peers)
    return handle, outs[-1]


def _exchange_wait(handle, after, name):
    srcs, lands, inplace, peers = handle["srcs"], handle["lands"], handle["inplace"], handle["peers"]
    n, n_l = len(srcs), len(lands)

    def body(*refs):
        src = refs[:n]
        land_refs = list(refs[n:n + n_l])
        send_sems, recv_sems = refs[n + n_l], refs[n + n_l + 1]
        land = [None if ip else land_refs.pop(0) for ip in inplace]
        for mine, theirs in _copies(src, land, send_sems, recv_sems, peers):
            mine().wait_send()
            theirs().wait_recv()

    outs = pl.pallas_call(
        body, name=name, out_shape=tuple(pltpu.HBM(a.shape, a.dtype) for a in list(srcs) + list(lands)),
        in_specs=[_HBM] * (n + n_l) + [_SEM, _SEM, pl.BlockSpec(memory_space=pl.ANY)],
        out_specs=(_HBM,) * (n + n_l), input_output_aliases={i: i for i in range(n + n_l)},
        compiler_params=pltpu.CompilerParams(has_side_effects=pltpu.SideEffectType.DATAFLOW_SIDE_EFFECTING),
    )(*srcs, *lands, handle["send"], handle["recv"], after)
    res, land_out = [], list(outs[n:])
    for t in range(n):
        res.append((outs[t], outs[t] if inplace[t] else land_out.pop(0)))
    return res


def _cast_to_slot(w, me, rows, name, cols=False):
    r, cdim = w.shape

    def body(me_ref, w_ref, o_ref):
        if cols:
            o_ref[...] = w_ref[...].astype(BF16)
        else:
            o_ref[0] = w_ref[...].astype(BF16)

    if cols:
        out_shape = jax.ShapeDtypeStruct((r, N_DEV * cdim), BF16)
        out_spec = pl.BlockSpec((rows, cdim), lambda i, me_ref: (i, me_ref[0]))
    else:
        out_shape = jax.ShapeDtypeStruct((N_DEV, r, cdim), BF16)
        out_spec = pl.BlockSpec((1, rows, cdim), lambda i, me_ref: (me_ref[0], i, 0))
    return pl.pallas_call(
        body, name=name, out_shape=out_shape,
        grid_spec=pltpu.PrefetchScalarGridSpec(
            num_scalar_prefetch=1, grid=(r // rows,), in_specs=[pl.BlockSpec((rows, cdim), lambda i, me_ref: (i, 0))],
            out_specs=out_spec),
        compiler_params=_params("parallel"))(me, w)


def _adamw_math(w, g, m, v):
    m = ADAM_B1 * m + (1.0 - ADAM_B1) * g
    v = ADAM_B2 * v + (1.0 - ADAM_B2) * (g * g)
    m_hat = m / (1.0 - ADAM_B1 ** ADAM_STEP)
    v_hat = v / (1.0 - ADAM_B2 ** ADAM_STEP)
    delta = -ADAM_LR * (m_hat / (jnp.sqrt(v_hat) + ADAM_EPS) + ADAM_WD * w)
    return delta, m, v


def _sum_parts(me, p_ref, own):
    g = None
    for j in range(N_DEV):
        term = (p_ref[j] if own is None else jnp.where(me == j, own, p_ref[j])).astype(F32)
        g = term if g is None else g + term
    return g


def _adamw_reduce(parts, own, me, w, m, v, rows, name):
    r, cdim = w.shape

    def body(me_ref, p_ref, own_ref, w_ref, m_ref, v_ref, g_out, d_out, m_out, v_out):
        g = _sum_parts(me_ref[0], p_ref, own_ref[0])
        d, mn, vn = _adamw_math(w_ref[...], g, m_ref[...], v_ref[...])
        g_out[...] = g
        d_out[...] = d
        m_out[...] = mn
        v_out[...] = vn

    blk = pl.BlockSpec((rows, cdim), lambda i, me_ref: (i, 0))
    sds = jax.ShapeDtypeStruct(w.shape, F32)
    return pl.pallas_call(
        body, name=name, out_shape=(sds,) * 4,
        grid_spec=pltpu.PrefetchScalarGridSpec(
            num_scalar_prefetch=1, grid=(r // rows,),
            in_specs=[pl.BlockSpec((N_DEV, rows, cdim), lambda i, me_ref: (0, i, 0)),
                      pl.BlockSpec((1, rows, cdim), lambda i, me_ref: (me_ref[0], i, 0)), blk, blk, blk],
            out_specs=(blk,) * 4),
        compiler_params=_params("parallel"))(me, parts, own, w, m, v)


def _adamw_small(parts, own, me, w, m, v, mask, name):
    def body(me_ref, *refs):
        refs = list(refs)
        p_ref = refs.pop(0)
        own_ref = None if own is None else refs.pop(0)
        w_ref, m_ref, v_ref = refs[:3]
        k_ref = None if mask is None else refs[3]
        g_out, d_out, m_out, v_out = refs[-4:]
        g = _sum_parts(me_ref[0], p_ref, None if own is None else own_ref[me_ref[0]])
        if mask is not None:
            g = g * k_ref[...]
        d, mn, vn = _adamw_math(w_ref[...], g, m_ref[...], v_ref[...])
        g_out[...] = g
        d_out[...] = d
        m_out[...] = mn
        v_out[...] = vn

    def whole(shape):
        nd = len(shape)
        return pl.BlockSpec(shape, lambda i, me_ref: (0,) * nd)

    sds = jax.ShapeDtypeStruct(w.shape, F32)
    ins = [parts] + ([] if own is None else [own]) + [w, m, v] + ([] if mask is None else [mask])
    return pl.pallas_call(
        body, name=name, out_shape=(sds,) * 4,
        grid_spec=pltpu.PrefetchScalarGridSpec(
            num_scalar_prefetch=1, grid=(1,), in_specs=[whole(a.shape) for a in ins],
            out_specs=(whole(w.shape),) * 4),
        compiler_params=_params("arbitrary"))(me, *ins)


_IN_SPLITS = ((0, 512), (512, 1024), (1024, 1536), (1536, 2560), (2560, IN_PAD))


def _in_proj(x, g1, w_in, tm):
    t_tok = x.shape[0]

    def body(x_ref, g_ref, w_ref, h_ref, *outs):
        xv = x_ref[...]
        r = lax.rsqrt(jnp.mean(xv * xv, axis=-1, keepdims=True) + EPS)
        h = (xv * r * g_ref[...]).astype(BF16)
        h_ref[...] = h
        for (a, b), o_ref in zip(_IN_SPLITS, outs):
            o_ref[...] = _dot(h, w_ref[a:b, :], _NT)

    row = lambda n: pl.BlockSpec((tm, n), lambda i: (i, 0))
    widths = [b - a for a, b in _IN_SPLITS]
    return pl.pallas_call(
        body, name="in_proj", grid=(t_tok // tm,),
        out_shape=(jax.ShapeDtypeStruct((t_tok, D_MODEL), BF16),) + tuple(
            jax.ShapeDtypeStruct((t_tok, n), F32) for n in widths),
        in_specs=[row(D_MODEL), _full((1, D_MODEL)), _full((IN_PAD, D_MODEL))],
        out_specs=(row(D_MODEL),) + tuple(row(n) for n in widths),
        compiler_params=_params("parallel"))(x, g1, w_in)


def _lane_masks():
    lane = lax.broadcasted_iota(jnp.int32, (1, 2 * HEAD_DIM), 1)
    left = (lane < HEAD_DIM).astype(F32)
    return left, 1.0 - left


def _stack_pair(v, m_l, m_r):
    return jnp.concatenate([v * m_l, v * m_r], axis=0).astype(BF16)


def _gmlp_common(u, v, lnw, lnb, avg, wcat_ref, bias, m_l, m_r):
    ug, dug = _gelu_and_grad(u)
    vg, dvg = _gelu_and_grad(v)
    mu = _split_dot(vg, avg, 2)
    vc = vg - mu
    var = _split_dot(vc * vc, avg, 2)
    rstd = lax.rsqrt(var + EPS)
    vhat = vc * rstd
    vn = vhat * lnw + lnb
    cols = []
    for j in range(N_HEADS // 2):
        cols.append(_dot(wcat_ref[j], _stack_pair(vn[:, 128 * j:128 * (j + 1)], m_l, m_r)))
    mixed = jnp.concatenate(cols, axis=1) + bias
    return ug, dug, dvg, rstd, vhat, vn, mixed


def _gmlp_fwd(u, v, lnw, lnb, wcat, bias, avg):
    t_tok = u.shape[0]

    def body(u_ref, v_ref, lnw_ref, lnb_ref, wcat_ref, bias_ref, avg_ref, o_ref):
        m_l, m_r = _lane_masks()
        ug, _, _, _, _, _, mixed = _gmlp_common(
            u_ref[...], v_ref[...], lnw_ref[...], lnb_ref[...], avg_ref[...], wcat_ref, bias_ref[...], m_l, m_r)
        o_ref[...] = (ug * mixed).astype(BF16)

    row = pl.BlockSpec((CHUNK, GM_WIDTH), lambda i: (i, 0))
    return pl.pallas_call(
        body, name="gmlp_fwd", grid=(t_tok // CHUNK,), out_shape=jax.ShapeDtypeStruct((t_tok, GM_WIDTH), BF16),
        in_specs=[row, row, _full((1, GM_WIDTH)), _full((1, GM_WIDTH)), _full(wcat.shape), _full(bias.shape),
                  _full(avg.shape)],
        out_specs=row, compiler_params=_params("parallel"))(u, v, lnw, lnb, wcat, bias, avg)


def _ssd_common(xext_ref, dtr, cw_ref, cb, dtb, alog, expand, tril):
    q = CHUNK
    taps = [xext_ref[pl.ds(5 + k, q), :] for k in range(4)]
    pre = cb + cw_ref[0:1, :] * taps[0] + cw_ref[1:2, :] * taps[1] + cw_ref[2:3, :] * taps[2] + cw_ref[3:4, :] * taps[3]
    sg = jax.nn.sigmoid(pre)
    act = pre * sg
    lane = lax.broadcasted_iota(jnp.int32, (1, CHUNK), 1)
    a_row = jnp.where(lane < N_HEADS, -jnp.exp(alog), 0.0)
    dtp = dtr + dtb
    dt = _softplus(dtp)
    a_cs = _split_dot_left(tril, dt * a_row, 3)
    a_cs_t = a_cs.T
    dt_exp = _split_dot(dt, expand, 3)
    a_exp = _split_dot(a_cs, expand, 3)
    a_end = a_exp[q - 1:q, :]
    li = lax.broadcasted_iota(jnp.int32, (q, q), 0)
    si = lax.broadcasted_iota(jnp.int32, (q, q), 1)
    causal = si <= li
    decay = []
    for h in range(N_HEADS):
        seg = a_cs[:, h:h + 1] - a_cs_t[h:h + 1, :]
        decay.append(jnp.where(causal, jnp.exp(jnp.minimum(seg, 0.0)), 0.0))
    return dict(taps=taps, pre=pre, sg=sg, act=act, a_row=a_row, dtp=dtp, dt=dt, dt_exp=dt_exp, a_exp=a_exp,
                e=jnp.exp(a_exp), w_end=jnp.exp(a_end - a_exp), cd=jnp.exp(a_end), decay=decay)


def _ssd_specs(t_tok, seq, reverse):
    nc = seq // CHUNK

    def chunk(b, c):
        return b * nc + (nc - 1 - c if reverse else c)

    def row(n):
        return pl.BlockSpec((CHUNK, n), lambda b, c: (chunk(b, c), 0))

    tail = pl.BlockSpec((8, CONV_CH), lambda b, c: (jnp.maximum(chunk(b, c) * (CHUNK // 8) - 1, 0), 0))
    return nc, chunk, row, tail


def _fill_xext(xext_ref, tail_ref, xbc_ref, first_chunk):
    xext_ref[0:8, :] = jnp.where(first_chunk, 0.0, tail_ref[...])
    xext_ref[8:8 + CHUNK, :] = xbc_ref[...]


def _ssd_fwd(z, xbc, dtr, cw, cb, dtb, alog, dskip_exp, nw, expand, tril, seq):
    t_tok = z.shape[0]
    nc, chunk, row, tail = _ssd_specs(t_tok, seq, False)

    def body(z_ref, xbc_ref, tail_ref, dtr_ref, cw_ref, cb_ref, dtb_ref, alog_ref, dsk_ref, nw_ref, exp_ref,
             tril_ref, o_ref, y_ref, st_ref, xext_ref, state_ref):
        c = pl.program_id(1)

        @pl.when(c == 0)
        def _():
            state_ref[...] = jnp.zeros_like(state_ref)

        _fill_xext(xext_ref, tail_ref, xbc_ref, c == 0)
        m_l, m_r = _lane_masks()
        f = _ssd_common(xext_ref, dtr_ref[...], cw_ref, cb_ref[...], dtb_ref[...], alog_ref[...], exp_ref[...],
                        tril_ref[...])
        act = f["act"]
        xs = act[:, :SSM_WIDTH]
        xdt = xs * f["dt_exp"]
        xw = xdt * f["w_end"]
        state = state_ref[...]
        st_ref[0] = state
        ydiag, yoff, snew = [], [], []
        for g in range(2):
            bg = act[:, 512 + 128 * g:640 + 128 * g].astype(BF16)
            cg = act[:, 768 + 128 * g:896 + 128 * g].astype(BF16)
            cb_mat = _dot(cg, bg, _NT)
            for pr in range(2):
                h0 = 4 * g + 2 * pr
                gcat = jnp.concatenate(
                    [(cb_mat * f["decay"][h0]).astype(BF16), (cb_mat * f["decay"][h0 + 1]).astype(BF16)], axis=1)
                ydiag.append(_dot(gcat, _stack_pair(xdt[:, 64 * h0:64 * h0 + 128], m_l, m_r)))
            yoff.append(_dot(cg, state[:, 256 * g:256 * (g + 1)].astype(BF16)))
            snew.append(_dot(bg, xw[:, 256 * g:256 * (g + 1)].astype(BF16), _TN))
        y = jnp.concatenate(ydiag, axis=1) + f["e"] * jnp.concatenate(yoff, axis=1) + dsk_ref[...] * xs
        state_ref[...] = state * f["cd"] + jnp.concatenate(snew, axis=1)
        y_ref[...] = y
        zv = z_ref[...]
        yg = y * (zv * jax.nn.sigmoid(zv))
        outs = []
        for g in range(2):
            ygg = yg[:, 256 * g:256 * (g + 1)]
            outs.append(ygg * lax.rsqrt(jnp.mean(ygg * ygg, axis=-1, keepdims=True) + EPS))
        o_ref[...] = (jnp.concatenate(outs, axis=1) * nw_ref[...]).astype(BF16)

    consts = [cw, cb, dtb, alog, dskip_exp, nw, expand, tril]
    return pl.pallas_call(
        body, name="ssd_fwd", grid=(t_tok // seq, nc),
        out_shape=(jax.ShapeDtypeStruct((t_tok, SSM_WIDTH), BF16), jax.ShapeDtypeStruct((t_tok, SSM_WIDTH), F32),
                   jax.ShapeDtypeStruct((t_tok // CHUNK, N_STATE, SSM_WIDTH), F32)),
        in_specs=[row(SSM_WIDTH), row(CONV_CH), tail, row(CHUNK)] + [_full(a.shape) for a in consts],
        out_specs=(row(SSM_WIDTH), row(SSM_WIDTH),
                   pl.BlockSpec((1, N_STATE, SSM_WIDTH), lambda b, c: (chunk(b, c), 0, 0))),
        scratch_shapes=[pltpu.VMEM((CHUNK + 16, CONV_CH), F32), pltpu.VMEM((N_STATE, SSM_WIDTH), F32)],
        compiler_params=_params("arbitrary", "arbitrary"))(z, xbc, xbc, dtr, *consts)


def _out_proj(mix_a, mix_b, w_out, x, g2, g3, tm, dep=None):
    t_tok = x.shape[0]
    deps = [] if dep is None else [dep]

    def body(a_ref, b_ref, w_ref, x_ref, g2_ref, g3_ref, *rest):
        o_ref, x2_ref, h3_ref, mix_ref = rest[-4:]
        o = _dot(a_ref[...], w_ref[0:GM_WIDTH, :]) + _dot(b_ref[...], w_ref[GM_WIDTH:, :])
        o_ref[...] = o
        mix_ref[:, 0:GM_WIDTH] = a_ref[...]
        mix_ref[:, GM_WIDTH:] = b_ref[...]
        r2 = lax.rsqrt(jnp.mean(o * o, axis=-1, keepdims=True) + EPS)
        x2 = x_ref[...] + o * r2 * g2_ref[...]
        x2_ref[...] = x2
        r3 = lax.rsqrt(jnp.mean(x2 * x2, axis=-1, keepdims=True) + EPS)
        h3_ref[...] = (x2 * r3 * g3_ref[...]).astype(BF16)

    row = lambda n: pl.BlockSpec((tm, n), lambda i: (i, 0))
    sd = lambda dt: jax.ShapeDtypeStruct((t_tok, D_MODEL), dt)
    return pl.pallas_call(
        body, name="out_proj", grid=(t_tok // tm,), out_shape=(sd(F32), sd(F32), sd(BF16), sd(BF16)),
        in_specs=[row(GM_WIDTH), row(SSM_WIDTH), _full((D_MODEL, D_MODEL)), row(D_MODEL), _full((1, D_MODEL)),
                  _full((1, D_MODEL))] + [pl.BlockSpec(memory_space=pl.ANY)] * len(deps),
        out_specs=(row(D_MODEL),) * 4, compiler_params=_params("parallel"))(mix_a, mix_b, w_out, x, g2, g3, *deps)


def _mlp_fwd(h3, w_up, w_down, x2, target, g4, tm, tf):
    t_tok = x2.shape[0]
    nf = D_FF // tf

    def body(h_ref, wu_ref, wd_ref, x2_ref, t_ref, g4_ref, ra_ref, dd_ref, dy_ref, dg4_ref, loss_ref, acc_ref):
        i, j = pl.program_id(0), pl.program_id(1)
        ra = jnp.maximum(_dot(h_ref[...], wu_ref[...]), 0.0).astype(BF16)
        ra_ref[...] = ra
        part = _dot(ra * ra, wd_ref[...])

        @pl.when(j == 0)
        def _():
            acc_ref[...] = part

        @pl.when(j > 0)
        def _():
            acc_ref[...] += part

        @pl.when(j == nf - 1)
        def _():
            dvec = acc_ref[...]
            r4 = lax.rsqrt(jnp.mean(dvec * dvec, axis=-1, keepdims=True) + EPS)
            dn = dvec * r4
            g4 = g4_ref[...]
            err = x2_ref[...] + dn * g4 - t_ref[...]
            dy = err * (1.0 / D_MODEL)
            dy_ref[...] = dy
            dg = dy * g4
            dd_ref[...] = (r4 * (dg - dn * jnp.mean(dg * dn, axis=-1, keepdims=True))).astype(BF16)
            _acc_rows(dg4_ref, _rsum(dy * dn), i == 0)
            tile_loss = 0.5 * jnp.sum(jnp.sum(err * err, axis=-1, keepdims=True), axis=0, keepdims=True) / D_MODEL
            _acc_rows(loss_ref, jnp.broadcast_to(tile_loss, (1, 128)), i == 0)

    row = pl.BlockSpec((tm, D_MODEL), lambda i, j: (i, 0))
    return pl.pallas_call(
        body, name="mlp_fwd", grid=(t_tok // tm, nf),
        out_shape=(jax.ShapeDtypeStruct((t_tok, D_FF), BF16), jax.ShapeDtypeStruct((t_tok, D_MODEL), BF16),
                   jax.ShapeDtypeStruct((t_tok, D_MODEL), F32), jax.ShapeDtypeStruct((8, D_MODEL), F32),
                   jax.ShapeDtypeStruct((8, 128), F32)),
        in_specs=[row, pl.BlockSpec((D_MODEL, tf), lambda i, j: (0, j)),
                  pl.BlockSpec((tf, D_MODEL), lambda i, j: (j, 0)), row, row, _full((1, D_MODEL))],
        out_specs=(pl.BlockSpec((tm, tf), lambda i, j: (i, j)), row, row, _full((8, D_MODEL)), _full((8, 128))),
        scratch_shapes=[pltpu.VMEM((tm, D_MODEL), F32)],
        compiler_params=_params("arbitrary", "arbitrary"))(h3, w_up, w_down, x2, target, g4)


def _mlp_bwd(dd, w_down, ra, w_up, x2, dy, o, g3, g2, tm, tf):
    t_tok = x2.shape[0]
    nf = D_FF // tf

    def body(dd_ref, wd_ref, ra_ref, wu_ref, x2_ref, dy_ref, o_ref, g3_ref, g2_ref, da_ref, dx2_ref, do_ref, dg3_ref,
             dg2_ref, acc_ref):
        i, j = pl.program_id(0), pl.program_id(1)
        df = _dot(dd_ref[...], wd_ref[...], _NT)
        da = (df * (2.0 * ra_ref[...].astype(F32))).astype(BF16)
        da_ref[...] = da
        part = _dot(da, wu_ref[...], _NT)

        @pl.when(j == 0)
        def _():
            acc_ref[...] = part

        @pl.when(j > 0)
        def _():
            acc_ref[...] += part

        @pl.when(j == nf - 1)
        def _():
            dn3, dg3 = _rms_bwd(x2_ref[...], g3_ref[...], acc_ref[...])
            dx2 = dy_ref[...] + dn3
            dx2_ref[...] = dx2
            do, dg2 = _rms_bwd(o_ref[...], g2_ref[...], dx2)
            do_ref[...] = do.astype(BF16)
            _acc_rows(dg3_ref, dg3, i == 0)
            _acc_rows(dg2_ref, dg2, i == 0)

    row = pl.BlockSpec((tm, D_MODEL), lambda i, j: (i, 0))
    vec = _full((1, D_MODEL))
    acc = _full((8, D_MODEL))
    sd = lambda dt: jax.ShapeDtypeStruct((t_tok, D_MODEL), dt)
    return pl.pallas_call(
        body, name="mlp_bwd", grid=(t_tok // tm, nf),
        out_shape=(jax.ShapeDtypeStruct((t_tok, D_FF), BF16), sd(F32), sd(BF16),
                   jax.ShapeDtypeStruct((8, D_MODEL), F32), jax.ShapeDtypeStruct((8, D_MODEL), F32)),
        in_specs=[row, pl.BlockSpec((tf, D_MODEL), lambda i, j: (j, 0)), pl.BlockSpec((tm, tf), lambda i, j: (i, j)),
                  pl.BlockSpec((D_MODEL, tf), lambda i, j: (0, j)), row, row, row, vec, vec],
        out_specs=(pl.BlockSpec((tm, tf), lambda i, j: (i, j)), row, row, acc, acc),
        scratch_shapes=[pltpu.VMEM((tm, D_MODEL), F32)],
        compiler_params=_params("arbitrary", "arbitrary"))(dd, w_down, ra, w_up, x2, dy, o, g3, g2)


def _wgrad(a, b, out_blocks, bm, bn, bk, square_a, name, dep=None):
    t_tok, m = a.shape
    n = b.shape[1]
    nk = t_tok // bk

    def body(a_ref, b_ref, *rest):
        o_ref, acc_ref = rest[-2:]
        k = pl.program_id(2)
        av = a_ref[...]
        if square_a:
            av = av * av
        part = _dot(av, b_ref[...], _TN)

        def emit(res):
            if out_blocks is None:
                o_ref[...] = res.astype(BF16)
            else:
                o_ref[0] = res.astype(BF16)

        if nk == 1:
            emit(part)
            return

        @pl.when(k == 0)
        def _():
            acc_ref[...] = part

        @pl.when(k > 0)
        def _():
            acc_ref[...] += part

        @pl.when(k == nk - 1)
        def _():
            emit(acc_ref[...])

    if out_blocks is None:
        out_shape = jax.ShapeDtypeStruct((m, n), BF16)
        out_spec = pl.BlockSpec((bm, bn), lambda i, j, k: (i, j))
    else:
        assert n // out_blocks == bn
        out_shape = jax.ShapeDtypeStruct((out_blocks, m, bn), BF16)
        out_spec = pl.BlockSpec((1, bm, bn), lambda i, j, k: (j, i, 0))
    deps = [] if dep is None else [dep]
    return pl.pallas_call(
        body, name=name, grid=(m // bm, n // bn, nk), out_shape=out_shape,
        in_specs=[pl.BlockSpec((bk, bm), lambda i, j, k: (k, i)), pl.BlockSpec((bk, bn), lambda i, j, k: (k, j))]
        + [pl.BlockSpec(memory_space=pl.ANY)] * len(deps),
        out_specs=out_spec, scratch_shapes=[pltpu.VMEM((bm, bn) if nk > 1 else (8, 128), F32)],
        compiler_params=_params("parallel", "parallel", "arbitrary"))(a, b, *deps)


def _wgrad_in(h1, pieces, bn, bk, dep=None):
    t_tok = h1.shape[0]
    nk = t_tok // bk
    widths = [b - a for a, b in _IN_SPLITS]

    def body(h_ref, *rest):
        piece_refs = rest[:len(widths)]
        o_ref, acc_ref = rest[-2:]
        k = pl.program_id(1)
        hv = h_ref[...]
        for (a, b), r in zip(_IN_SPLITS, piece_refs):
            part = _dot(r[...], hv, _TN)
            if nk == 1:
                o_ref[a:b, :] = part.astype(BF16)
                continue

            @pl.when(k == 0)
            def _():
                acc_ref[a:b, :] = part

            @pl.when(k > 0)
            def _():
                acc_ref[a:b, :] += part

        if nk > 1:
            @pl.when(k == nk - 1)
            def _():
                o_ref[...] = acc_ref[...].astype(BF16)

    deps = [] if dep is None else [dep]
    return pl.pallas_call(
        body, name="wgrad_in", grid=(D_MODEL // bn, nk), out_shape=jax.ShapeDtypeStruct((IN_PAD, D_MODEL), BF16),
        in_specs=[pl.BlockSpec((bk, bn), lambda j, k: (k, j))] + [pl.BlockSpec((bk, n), lambda j, k: (k, 0)) for n in widths]
        + [pl.BlockSpec(memory_space=pl.ANY)] * len(deps),
        out_specs=pl.BlockSpec((IN_PAD, bn), lambda j, k: (0, j)),
        scratch_shapes=[pltpu.VMEM((IN_PAD, bn) if nk > 1 else (8, 128), F32)],
        compiler_params=_params("parallel", "arbitrary"))(h1, *pieces, *deps)


def _dmix(do, w_out, tm, dep=None):
    t_tok = do.shape[0]

    def body(d_ref, w_ref, *rest):
        rest[-1][...] = _dot(d_ref[...], w_ref[...], _NT)

    row = pl.BlockSpec((tm, D_MODEL), lambda i: (i, 0))
    deps = [] if dep is None else [dep]
    return pl.pallas_call(
        body, name="dmix", grid=(t_tok // tm,), out_shape=jax.ShapeDtypeStruct((t_tok, D_MODEL), F32),
        in_specs=[row, _full((D_MODEL, D_MODEL))] + [pl.BlockSpec(memory_space=pl.ANY)] * len(deps), out_specs=row,
        compiler_params=_params("parallel"))(do, w_out, *deps)


def _gmlp_bwd(dmix, u, v, lnw, lnb, wcat, wtcat, bias, avg, expand_t):
    t_tok = u.shape[0]

    def body(dm_ref, u_ref, v_ref, lnw_ref, lnb_ref, wcat_ref, wtcat_ref, bias_ref, avg_ref, expt_ref, du_ref, dv_ref,
             dw_ref, db_ref, dlnw_ref, dlnb_ref):
        i = pl.program_id(0)
        m_l, m_r = _lane_masks()
        avg = avg_ref[...]
        lnw = lnw_ref[...]
        ug, dug, dvg, rstd, vhat, vn, mixed = _gmlp_common(
            u_ref[...], v_ref[...], lnw, lnb_ref[...], avg, wcat_ref, bias_ref[...], m_l, m_r)
        dya = dm_ref[...]
        du_ref[...] = (dya * mixed * dug).astype(BF16)
        dmixed = dya * ug
        dvn_cols, dws = [], []
        for j in range(N_HEADS // 2):
            dmp = dmixed[:, 128 * j:128 * (j + 1)]
            dvn_cols.append(_dot(wtcat_ref[j], _stack_pair(dmp, m_l, m_r)))
            vnp = vn[:, 128 * j:128 * (j + 1)].astype(BF16)
            dws.append(_dot((dmp * m_l).astype(BF16), vnp, _NT))
            dws.append(_dot((dmp * m_r).astype(BF16), vnp, _NT))
        dvn = jnp.concatenate(dvn_cols, axis=1)
        dvh = dvn * lnw
        dvgel = rstd * (dvh - _split_dot(dvh, avg, 2) - vhat * _split_dot(dvh * vhat, avg, 2))
        dv_ref[...] = (dvgel * dvg).astype(BF16)
        dbt = _split_dot(dmixed, expt_ref[...], 2)
        first = i == 0

        @pl.when(first)
        def _():
            for h in range(N_HEADS):
                dw_ref[h] = dws[h]
            db_ref[...] = dbt

        @pl.when(jnp.logical_not(first))
        def _():
            for h in range(N_HEADS):
                dw_ref[h] += dws[h]
            db_ref[...] += dbt

        _acc_rows(dlnw_ref, _rsum(dvn * vhat), first)
        _acc_rows(dlnb_ref, _rsum(dvn), first)

    row = pl.BlockSpec((CHUNK, GM_WIDTH), lambda i: (i, 0))
    consts = [lnw, lnb, wcat, wtcat, bias, avg, expand_t]
    return pl.pallas_call(
        body, name="gmlp_bwd", grid=(t_tok // CHUNK,),
        out_shape=(jax.ShapeDtypeStruct((t_tok, GM_WIDTH), BF16), jax.ShapeDtypeStruct((t_tok, GM_WIDTH), BF16),
                   jax.ShapeDtypeStruct((N_HEADS, CHUNK, CHUNK), F32), jax.ShapeDtypeStruct((CHUNK, CHUNK), F32),
                   jax.ShapeDtypeStruct((8, GM_WIDTH), F32), jax.ShapeDtypeStruct((8, GM_WIDTH), F32)),
        in_specs=[pl.BlockSpec((CHUNK, GM_WIDTH), lambda i: (i, 0)), row, row] + [_full(a.shape) for a in consts],
        out_specs=(row, row, _full((N_HEADS, CHUNK, CHUNK)), _full((CHUNK, CHUNK)), _full((8, GM_WIDTH)),
                   _full((8, GM_WIDTH))),
        compiler_params=_params("arbitrary"))(dmix, u, v, *consts)


def _ssd_bwd(dmix, z, xbc, dtr, y, states, cw, cb, dtb, alog, dskip_exp, nw, expand, expand_t, tril, triu, seq,
             dep=None):
    t_tok = z.shape[0]
    nc, chunk, row, tail = _ssd_specs(t_tok, seq, True)
    q = CHUNK

    def body(dm_ref, z_ref, xbc_ref, tail_ref, dtr_ref, y_ref, st_ref, cw_ref, cb_ref, dtb_ref, alog_ref, dsk_ref,
             nw_ref, exp_ref, expt_ref, tril_ref, triu_ref, dz_ref, dxbc_ref, ddt_ref, dcw_ref, dcb_ref, ddtb_ref,
             dalog_ref, dd_ref, dnw_ref, xext_ref, dext_ref, dstate_ref):
        b, c = pl.program_id(0), pl.program_id(1)
        first = jnp.logical_and(b == 0, c == 0)

        @pl.when(c == 0)
        def _():
            dstate_ref[...] = jnp.zeros_like(dstate_ref)
            dext_ref[q:q + 8, :] = jnp.zeros((8, CONV_CH), F32)

        _fill_xext(xext_ref, tail_ref, xbc_ref, c == nc - 1)
        m_l, m_r = _lane_masks()
        expt = expt_ref[...]
        f = _ssd_common(xext_ref, dtr_ref[...], cw_ref, cb_ref[...], dtb_ref[...], alog_ref[...], exp_ref[...],
                        tril_ref[...])
        act, pre, sg = f["act"], f["pre"], f["sg"]
        xs = act[:, :SSM_WIDTH]
        xdt = xs * f["dt_exp"]
        xw = xdt * f["w_end"]
        state = st_ref[0]
        dstate = dstate_ref[...]
        zv, yv, dout, nw = z_ref[...], y_ref[...], dm_ref[...], nw_ref[...]
        sz = jax.nn.sigmoid(zv)
        sl = zv * sz
        yg = yv * sl
        tv = dout * nw
        dyg_parts, ygh_parts = [], []
        for g in range(2):
            ygg = yg[:, 256 * g:256 * (g + 1)]
            rr = lax.rsqrt(jnp.mean(ygg * ygg, axis=-1, keepdims=True) + EPS)
            ygh = ygg * rr
            tg = tv[:, 256 * g:256 * (g + 1)]
            dyg_parts.append(rr * (tg - ygh * jnp.mean(tg * ygh, axis=-1, keepdims=True)))
            ygh_parts.append(ygh)
        dyg = jnp.concatenate(dyg_parts, axis=1)
        dnw = _rsum(dout * jnp.concatenate(ygh_parts, axis=1))
        dy = dyg * sl
        dz_ref[...] = (dyg * yv * (sz * (1.0 + zv * (1.0 - sz)))).astype(BF16)
        ddsk = _rsum(dy * xs)
        dye = dy * f["e"]
        lane = lax.broadcasted_iota(jnp.int32, (q, q), 1)
        sub = lax.broadcasted_iota(jnp.int32, (q, q), 0)
        rs_mat = jnp.zeros((q, q), F32)
        cs_mat = jnp.zeros((q, q), F32)
        dxdt_cols, yoff, dst_in, dxw, d_b, d_c = [], [], [], [], [], []
        for g in range(2):
            bg = act[:, 512 + 128 * g:640 + 128 * g].astype(BF16)
            cg = act[:, 768 + 128 * g:896 + 128 * g].astype(BF16)
            cb_mat = _dot(cg, bg, _NT)
            stg = state[:, 256 * g:256 * (g + 1)].astype(BF16)
            dyeg = dye[:, 256 * g:256 * (g + 1)].astype(BF16)
            yoff.append(_dot(cg, stg))
            dcg = _dot(dyeg, stg, _NT)
            dst_in.append(_dot(cg, dyeg, _TN))
            dcb = jnp.zeros((q, q), F32)
            for pr in range(2):
                h0 = 4 * g + 2 * pr
                gf = [cb_mat * f["decay"][h0], cb_mat * f["decay"][h0 + 1]]
                gcat = jnp.concatenate([gf[0].astype(BF16), gf[1].astype(BF16)], axis=1)
                xst = _stack_pair(xdt[:, 64 * h0:64 * h0 + 128], m_l, m_r)
                dyp = dy[:, 64 * h0:64 * h0 + 128].astype(BF16)
                dgcat = _dot(dyp, xst, _NT)
                dxst = _dot(gcat, dyp, _TN)
                dxdt_cols.append(dxst[:q] * m_l + dxst[q:] * m_r)
                for i in range(2):
                    h = h0 + i
                    dg = dgcat[:, q * i:q * (i + 1)]
                    mm = dg * gf[i]
                    rs_mat = rs_mat + jnp.where(lane == h, jnp.sum(mm, axis=1, keepdims=True), 0.0)
                    cs_mat = cs_mat + jnp.where(sub == h, jnp.sum(mm, axis=0, keepdims=True), 0.0)
                    dcb = dcb + dg * f["decay"][h]
            dcb16 = dcb.astype(BF16)
            dstg = dstate[:, 256 * g:256 * (g + 1)].astype(BF16)
            d_c.append(dcg + _dot(dcb16, bg))
            dxw.append(_dot(bg, dstg))
            d_b.append(_dot(dcb16, cg, _TN) + _dot(xw[:, 256 * g:256 * (g + 1)].astype(BF16), dstg, _NT))
        dxw = jnp.concatenate(dxw, axis=1)
        dxdt = jnp.concatenate(dxdt_cols, axis=1) + dxw * f["w_end"]
        qv = dxw * xw
        end_row = _rsum(qv) + _rsum(dstate * state) * f["cd"]
        x2 = dye * jnp.concatenate(yoff, axis=1) - qv
        row_i = lax.broadcasted_iota(jnp.int32, (q, 1), 0)
        x2 = x2 + jnp.where(row_i == q - 1, end_row, 0.0)
        da_cs = _split_dot(x2, expt, 3) + rs_mat - cs_mat.T
        ddt = _split_dot(dxdt * xs, expt, 3)
        dxs = dsk_ref[...] * dy + dxdt * f["dt_exp"]
        dda = _split_dot_left(triu_ref[...], da_cs, 3)
        ddt = ddt + dda * f["a_row"]
        dalog = _rsum(dda * f["dt"]) * f["a_row"]
        draw = ddt * jax.nn.sigmoid(f["dtp"])
        ddt_ref[...] = draw.astype(BF16)
        dact = jnp.concatenate([dxs] + d_b + d_c, axis=1)
        dpre = dact * (sg * (1.0 + pre * (1.0 - sg)))
        dext_ref[0:q, :] = dpre
        dxbc = cw_ref[0:1, :] * dext_ref[pl.ds(3, q), :]
        for k in range(1, 4):
            dxbc = dxbc + cw_ref[k:k + 1, :] * dext_ref[pl.ds(3 - k, q), :]
        dxbc_ref[...] = dxbc.astype(BF16)
        dext_ref[q:q + 8, :] = dpre[0:8, :]
        dstate_ref[...] = dstate * f["cd"] + jnp.concatenate(dst_in, axis=1)
        row8 = lax.broadcasted_iota(jnp.int32, (8, 1), 0)
        dcw = jnp.zeros((8, CONV_CH), F32)
        for k in range(4):
            dcw = dcw + jnp.where(row8 == k, _rsum(dpre * f["taps"][k]), 0.0)

        @pl.when(first)
        def _():
            dcw_ref[...] = dcw

        @pl.when(jnp.logical_not(first))
        def _():
            dcw_ref[...] += dcw

        _acc_rows(dcb_ref, _rsum(dpre), first)
        _acc_rows(ddtb_ref, _rsum(draw), first)
        _acc_rows(dalog_ref, dalog, first)
        _acc_rows(dd_ref, ddsk, first)
        _acc_rows(dnw_ref, dnw, first)

    consts = [cw, cb, dtb, alog, dskip_exp, nw, expand, expand_t, tril, triu]
    deps = [] if dep is None else [dep]
    n_in = 7 + len(consts)

    def body_skipping_dep(*refs):
        body(*refs[:n_in], *refs[n_in + len(deps):])

    acc = lambda n: jax.ShapeDtypeStruct((8, n), F32)
    return pl.pallas_call(
        body_skipping_dep, name="ssd_bwd", grid=(t_tok // seq, nc),
        out_shape=(jax.ShapeDtypeStruct((t_tok, SSM_WIDTH), BF16), jax.ShapeDtypeStruct((t_tok, CONV_CH), BF16),
                   jax.ShapeDtypeStruct((t_tok, CHUNK), BF16), acc(CONV_CH), acc(CONV_CH), acc(CHUNK), acc(CHUNK),
                   acc(SSM_WIDTH), acc(SSM_WIDTH)),
        in_specs=[pl.BlockSpec((CHUNK, SSM_WIDTH), lambda b, c: (chunk(b, c), 1)), row(SSM_WIDTH), row(CONV_CH), tail,
                  row(CHUNK), row(SSM_WIDTH), pl.BlockSpec((1, N_STATE, SSM_WIDTH), lambda b, c: (chunk(b, c), 0, 0))]
        + [_full(a.shape) for a in consts] + [pl.BlockSpec(memory_space=pl.ANY)] * len(deps),
        out_specs=(row(SSM_WIDTH), row(CONV_CH), row(CHUNK), _full((8, CONV_CH)), _full((8, CONV_CH)),
                   _full((8, CHUNK)), _full((8, CHUNK)), _full((8, SSM_WIDTH)), _full((8, SSM_WIDTH))),
        scratch_shapes=[pltpu.VMEM((CHUNK + 16, CONV_CH), F32), pltpu.VMEM((CHUNK + 8, CONV_CH), F32),
                        pltpu.VMEM((N_STATE, SSM_WIDTH), F32)],
        compiler_params=_params("arbitrary", "arbitrary"))(dmix, z, xbc, xbc, dtr, y, states, *consts, *deps)


def _in_bwd(du, dv, dz, dxbc, ddt, w_in, x, dx2, g1, tm, dep=None):
    t_tok = x.shape[0]

    def body(du_ref, dv_ref, dz_ref, dxbc_ref, ddt_ref, w_ref, x_ref, dx2_ref, g_ref, *rest):
        gx_ref, dg_ref = rest[-2:]
        i = pl.program_id(0)
        dh = None
        for (a, b), ref in zip(_IN_SPLITS, (du_ref, dv_ref, dz_ref, dxbc_ref, ddt_ref)):
            part = _dot(ref[...], w_ref[a:b, :])
            dh = part if dh is None else dh + part
        dn, dg = _rms_bwd(x_ref[...], g_ref[...], dh)
        gx_ref[...] = dx2_ref[...] + dn
        _acc_rows(dg_ref, dg, i == 0)

    row = lambda n: pl.BlockSpec((tm, n), lambda i: (i, 0))
    widths = [b - a for a, b in _IN_SPLITS]
    deps = [] if dep is None else [dep]
    return pl.pallas_call(
        body, name="in_bwd", grid=(t_tok // tm,),
        out_shape=(jax.ShapeDtypeStruct((t_tok, D_MODEL), F32), jax.ShapeDtypeStruct((8, D_MODEL), F32)),
        in_specs=[row(n) for n in widths] + [_full((IN_PAD, D_MODEL)), row(D_MODEL), row(D_MODEL), _full((1, D_MODEL))]
        + [pl.BlockSpec(memory_space=pl.ANY)] * len(deps),
        out_specs=(row(D_MODEL), _full((8, D_MODEL))),
        compiler_params=_params("arbitrary"))(du, dv, dz, dxbc, ddt, w_in, x, dx2, g1, *deps)


def _pad_lanes(a, n):
    return jnp.pad(a, ((0, 0), (0, n - a.shape[1])))


def _local_step(x, target, seq, w_in_t, conv_w, small, hooks):
    t_tok = x.shape[0]
    tm = min(512, t_tok)
    avg, expand, expand_t, tril, triu = _const_mats()
    g1, g2, g3, g4 = (small[k].reshape(1, D_MODEL) for k in
                      ("norm_mix_pre", "norm_mix_post", "norm_ffn_pre", "norm_ffn_post"))
    lnw = small["gm_ln_w"].reshape(1, GM_WIDTH)
    lnb = small["gm_ln_b"].reshape(1, GM_WIDTH)
    causal = jnp.tril(jnp.ones((CHUNK, CHUNK), F32))
    wm = small["gm_w_s"] * causal
    pair = lambda w: w.reshape(4, 2, CHUNK, CHUNK).transpose(0, 2, 1, 3).reshape(4, CHUNK, 2 * CHUNK).astype(BF16)
    wcat = pair(wm)
    wtcat = pair(jnp.swapaxes(wm, 1, 2))
    bias = jnp.repeat(small["gm_b_s"].T, HEAD_DIM, axis=1)
    cb = small["conv_b"].reshape(1, CONV_CH)
    dtb = _pad_lanes(small["dt_bias"].reshape(1, N_HEADS), CHUNK)
    alog = _pad_lanes(small["a_log"].reshape(1, N_HEADS), CHUNK)
    dskip_exp = jnp.repeat(small["d_skip"].reshape(1, N_HEADS), HEAD_DIM, axis=1)
    nw = small["ssm_norm_w"].reshape(1, SSM_WIDTH)

    h1, u, v, z, xbc, dtr = _in_proj(x, g1, w_in_t, tm)
    mix_a = _gmlp_fwd(u, v, lnw, lnb, wcat, bias, avg)
    mix_b, y_pre, states = _ssd_fwd(z, xbc, dtr, conv_w, cb, dtb, alog, dskip_exp, nw, expand, tril, seq)
    w_out, dep = hooks["mixers_done"](mix_b)
    o, x2, h3, mix = _out_proj(mix_a, mix_b, w_out, x, g2, g3, tm, dep)
    w_up, w_down = hooks["mlp_weights"](h3)
    tf = 2048
    ra, dd, dy, dg4, loss = _mlp_fwd(h3, w_up, w_down, x2, target, g4, tm, tf)

    da, dx2, do, dg3, dg2 = _mlp_bwd(dd, w_down, ra, w_up, x2, dy, o, g3, g2, tm, tf)
    bk = min(2048, t_tok)
    g_w_down = _wgrad(ra, dd, None, 512, 512, t_tok, True, "wgrad_down")
    g_w_up = _wgrad(h3, da, N_DEV, 512, D_FF // N_DEV, t_tok, False, "wgrad_up")
    dep = hooks["mlp_grads"](g_w_down, g_w_up)
    dmix = _dmix(do, w_out, tm, dep)
    g_w_out = _wgrad(mix, do, None, 512, 512, t_tok, False, "wgrad_out", dep)
    du, dv, dws, dbt, dlnw, dlnb = _gmlp_bwd(dmix, u, v, lnw, lnb, wcat, wtcat, bias, avg, expand_t)
    dep = hooks["gmlp_grads"](g_w_out, dws)
    dz, dxbc, ddt, dcw, dcb, ddtb, dalog, ddsk, dnw = _ssd_bwd(
        dmix, z, xbc, dtr, y_pre, states, conv_w, cb, dtb, alog, dskip_exp, nw, expand, expand_t, tril, triu, seq, dep)
    g_w_in = _wgrad_in(h1, (du, dv, dz, dxbc, ddt), 512, bk, dep)
    dep = hooks["in_grads"](g_w_in, dcw[0:4])
    grad_x, dg1 = _in_bwd(du, dv, dz, dxbc, ddt, w_in_t, x, dx2, g1, tm, dep)

    grads = dict(
        w_in=g_w_in, w_out=g_w_out, w_up=g_w_up, w_down=g_w_down, conv_w=dcw[0:4],
        norm_mix_pre=dg1[0:1], norm_mix_post=dg2[0:1], norm_ffn_pre=dg3[0:1], norm_ffn_post=dg4[0:1],
        gm_ln_w=dlnw[0:1], gm_ln_b=dlnb[0:1], gm_w_s=dws, gm_b_s=dbt.T[0:N_HEADS], conv_b=dcb[0:1],
        dt_bias=ddtb[0:1, 0:N_HEADS], a_log=dalog[0:1, 0:N_HEADS],
        d_skip=ddsk[0:1].reshape(N_HEADS, HEAD_DIM).sum(axis=1).reshape(1, N_HEADS), ssm_norm_w=dnw[0:1])
    return loss[0, 0], grad_x, grads


_SMALL_ROW_PARAMS = ("norm_mix_pre", "norm_mix_post", "norm_ffn_pre", "norm_ffn_post", "gm_ln_w", "gm_ln_b", "gm_b_s",
                     "conv_b", "dt_bias", "a_log", "d_skip", "ssm_norm_w")
_WEIGHTS = ("norm_mix_pre", "w_in", "gm_ln_w", "gm_ln_b", "gm_w_s", "gm_b_s", "conv_w", "conv_b", "dt_bias", "a_log",
            "d_skip", "ssm_norm_w", "w_out", "norm_mix_post", "norm_ffn_pre", "w_up", "w_down", "norm_ffn_post")


def _pack_rows(tensors):
    rows = [_pad_lanes(t.reshape(1, -1), D_MODEL) for t in tensors]
    rows.append(jnp.zeros((SMALL_ROWS - len(rows), D_MODEL), F32))
    return jnp.concatenate(rows, axis=0)


def kernel(x, norm_mix_pre, w_in, gm_ln_w, gm_ln_b, gm_w_s, gm_b_s, conv_w, conv_b, dt_bias, a_log, d_skip, ssm_norm_w, w_out, norm_mix_post, norm_ffn_pre, w_up, w_down, norm_ffn_post, loss_target, m_norm_mix_pre, m_w_in, m_gm_ln_w, m_gm_ln_b, m_gm_w_s, m_gm_b_s, m_conv_w, m_conv_b, m_dt_bias, m_a_log, m_d_skip, m_ssm_norm_w, m_w_out, m_norm_mix_post, m_norm_ffn_pre, m_w_up, m_w_down, m_norm_ffn_post, v_norm_mix_pre, v_w_in, v_gm_ln_w, v_gm_ln_b, v_gm_w_s, v_gm_b_s, v_conv_w, v_conv_b, v_dt_bias, v_a_log, v_d_skip, v_ssm_norm_w, v_w_out, v_norm_mix_post, v_norm_ffn_pre, v_w_up, v_w_down, v_norm_ffn_post):
    w = dict(norm_mix_pre=norm_mix_pre, w_in=w_in, gm_ln_w=gm_ln_w, gm_ln_b=gm_ln_b, gm_w_s=gm_w_s, gm_b_s=gm_b_s, conv_w=conv_w, conv_b=conv_b, dt_bias=dt_bias, a_log=a_log, d_skip=d_skip, ssm_norm_w=ssm_norm_w, w_out=w_out, norm_mix_post=norm_mix_post, norm_ffn_pre=norm_ffn_pre, w_up=w_up, w_down=w_down, norm_ffn_post=norm_ffn_post)
    m = dict(norm_mix_pre=m_norm_mix_pre, w_in=m_w_in, gm_ln_w=m_gm_ln_w, gm_ln_b=m_gm_ln_b, gm_w_s=m_gm_w_s, gm_b_s=m_gm_b_s, conv_w=m_conv_w, conv_b=m_conv_b, dt_bias=m_dt_bias, a_log=m_a_log, d_skip=m_d_skip, ssm_norm_w=m_ssm_norm_w, w_out=m_w_out, norm_mix_post=m_norm_mix_post, norm_ffn_pre=m_norm_ffn_pre, w_up=m_w_up, w_down=m_w_down, norm_ffn_post=m_norm_ffn_post)
    v = dict(norm_mix_pre=v_norm_mix_pre, w_in=v_w_in, gm_ln_w=v_gm_ln_w, gm_ln_b=v_gm_ln_b, gm_w_s=v_gm_w_s, gm_b_s=v_gm_b_s, conv_w=v_conv_w, conv_b=v_conv_b, dt_bias=v_dt_bias, a_log=v_a_log, d_skip=v_d_skip, ssm_norm_w=v_ssm_norm_w, w_out=v_w_out, norm_mix_post=v_norm_mix_post, norm_ffn_pre=v_norm_ffn_pre, w_up=v_w_up, w_down=v_w_down, norm_ffn_post=v_norm_ffn_post)
    n_batch, seq, _ = x.shape
    shard_in = IN_COLS // N_DEV

    me = (4 * lax.axis_index("x") + 2 * lax.axis_index("y") + lax.axis_index("c")).astype(jnp.int32).reshape(1)

    def in_slot(own):
        return lax.dynamic_update_slice(lax.empty((N_DEV,) + own.shape, own.dtype), own[None],
                                        (me[0],) + (0,) * own.ndim)

    w_in_sh, m_in_sh, v_in_sh = w_in[0].T, m_w_in[0].T, v_w_in[0].T
    first = [_cast_to_slot(w_in_sh, me, shard_in, "cast_w_in"), in_slot(conv_w[0]),
             _cast_to_slot(w_out[0], me, 128, "cast_w_out")]
    ici_1, _ = _exchange_start(first, [True] * 3, _SAME_CORE_PEERS, "gather_mix_ici_start")
    first = [buf for buf, _ in _exchange_wait(ici_1, me, "gather_mix_ici_wait")]
    d2d_1, tok_d2d_1 = _exchange_start(first, [True] * 3, _SIBLING_FORWARD, "gather_mix_d2d_start")
    second = [_cast_to_slot(w_up[0], me, 256, "cast_w_up", cols=True), _cast_to_slot(w_down[0], me, 256, "cast_w_down")]
    ici_2, tok_ici_2 = _exchange_start(second, [True] * 2, _SAME_CORE_PEERS, "gather_mlp_ici_start", dep=tok_d2d_1)
    (_, ag_in), (_, ag_conv), (_, ag_out) = _exchange_wait(d2d_1, tok_ici_2, "gather_mix_d2d_wait")
    w_in_t = jnp.pad(ag_in.reshape(IN_COLS, D_MODEL), ((0, IN_PAD - IN_COLS), (0, 0)))
    conv_w_f = ag_conv.transpose(1, 0, 2).reshape(4, CONV_CH)
    w_out_f = ag_out.reshape(D_MODEL, D_MODEL)
    gathering = {}

    def mixers_done(after):
        bufs = [buf for buf, _ in _exchange_wait(ici_2, after, "gather_mlp_ici_wait")]
        gathering["mlp"], tok = _exchange_start(bufs, [True] * 2, _SIBLING_FORWARD, "gather_mlp_d2d_start")
        return w_out_f, tok

    def mlp_weights(after):
        (_, ag_up), (_, ag_down) = _exchange_wait(gathering["mlp"], after, "gather_mlp_d2d_wait")
        return ag_up, ag_down.reshape(D_FF, D_MODEL)

    sent = {}

    def mlp_grads(g_w_down, g_w_up):
        sent["mlp"], tok = _exchange_start(
            [g_w_down.reshape(N_DEV, D_FF // N_DEV, D_MODEL), g_w_up], [False, False], _ALL_PEERS, "grads_mlp_start")
        return tok

    def gmlp_grads(g_w_out, g_w_s):
        sent["gmlp"], tok = _exchange_start(
            [g_w_out.reshape(N_DEV, D_MODEL // N_DEV, D_MODEL), in_slot(g_w_s.astype(BF16))], [False, True], _ALL_PEERS,
            "grads_gmlp_start")
        return tok

    def in_grads(g_w_in_t, g_conv_w):
        g_in_blk = g_w_in_t[:IN_COLS].reshape(N_DEV, shard_in, D_MODEL)
        g_conv_blk = g_conv_w.reshape(4, N_DEV, CONV_CH // N_DEV).transpose(1, 0, 2)
        sent["in"], tok = _exchange_start([g_in_blk, g_conv_blk], [False, False], _ALL_PEERS, "grads_in_start")
        return tok

    small = {k: w[k][0] for k in _SMALL_ROW_PARAMS + ("gm_w_s",)}
    loss_part, grad_x, g = _local_step(
        x.reshape(n_batch * seq, D_MODEL), loss_target.reshape(n_batch * seq, D_MODEL), seq, w_in_t, conv_w_f, small,
        dict(mixers_done=mixers_done, mlp_weights=mlp_weights, mlp_grads=mlp_grads, gmlp_grads=gmlp_grads,
             in_grads=in_grads))
    loss = lax.psum(loss_part, ("x", "y", "c"))

    sent_rows, tok_rows = _exchange_start(
        [in_slot(_pack_rows([g[k] for k in _SMALL_ROW_PARAMS]))], [True], _ALL_PEERS, "grads_rows_start")
    (own_down, p_down), (own_up, p_up) = _exchange_wait(sent["mlp"], tok_rows, "grads_mlp_wait")
    res = {}
    res["w_up"] = _adamw_reduce(p_up, own_up, me, w_up[0], m_w_up[0], v_w_up[0], 256, "adamw_w_up")
    res["w_down"] = _adamw_reduce(p_down, own_down, me, w_down[0], m_w_down[0], v_w_down[0], 128, "adamw_w_down")
    (own_out, p_out), (_, p_ws) = _exchange_wait(sent["gmlp"], res["w_down"][1], "grads_gmlp_wait")
    res["w_out"] = _adamw_reduce(p_out, own_out, me, w_out[0], m_w_out[0], v_w_out[0], 128, "adamw_w_out")
    causal = jnp.tril(jnp.ones((1, CHUNK, CHUNK), F32))
    res["gm_w_s"] = _adamw_small(p_ws, None, me, gm_w_s[0], m_gm_w_s[0], v_gm_w_s[0], causal, "adamw_gm_w_s")
    (own_in, p_in), (own_conv, p_conv) = _exchange_wait(sent["in"], res["gm_w_s"][1], "grads_in_wait")
    res["w_in"] = tuple(r.T for r in _adamw_reduce(p_in, own_in, me, w_in_sh, m_in_sh, v_in_sh, shard_in, "adamw_w_in"))
    res["conv_w"] = _adamw_small(p_conv, own_conv, me, conv_w[0], m_conv_w[0], v_conv_w[0], None, "adamw_conv_w")
    ((_, p_rows),) = _exchange_wait(sent_rows, res["w_in"][1], "grads_rows_wait")
    rows = _adamw_small(p_rows, None, me, _pack_rows([w[k] for k in _SMALL_ROW_PARAMS]),
                        _pack_rows([m[k] for k in _SMALL_ROW_PARAMS]), _pack_rows([v[k] for k in _SMALL_ROW_PARAMS]),
                        None, "adamw_rows")
    for i, k in enumerate(_SMALL_ROW_PARAMS):
        size = int(np.prod(w[k].shape))
        res[k] = tuple(r[i, :size].reshape(w[k].shape) for r in rows)
    for k in ("w_in", "w_out", "w_up", "w_down", "conv_w", "gm_w_s"):
        res[k] = tuple(r.reshape(w[k].shape) for r in res[k])

    outs = [loss, grad_x.reshape(x.shape)]
    for part in range(4):
        outs.extend(res[k][part] for k in _WEIGHTS)
    return tuple(outs)
```

```python
import functools

import jax
import jax.numpy as jnp
import numpy as np
from jax import lax
from jax.experimental import pallas as pl
from jax.experimental.pallas import tpu as pltpu

F32 = jnp.float32
BF16 = jnp.bfloat16

D_MODEL = 1024
GM_WIDTH = 512
SSM_WIDTH = 512
CONV_CH = 1024
N_HEADS = 8
HEAD_DIM = 64
N_STATE = 128
CHUNK = 128
D_FF = 4096
IN_COLS = 2568
IN_PAD = 2688
N_DEV = 8
EPS = 1e-6
ADAM_LR, ADAM_B1, ADAM_B2, ADAM_EPS, ADAM_WD, ADAM_STEP = 0.001, 0.9, 0.999, 1e-08, 0.01, 10
VMEM_LIMIT_BYTES = 56 * 1024 * 1024
SMALL_ROWS = 16

_NT = (((1,), (1,)), ((), ()))
_TN = (((0,), (0,)), ((), ()))


def _params(*sem):
    return pltpu.CompilerParams(dimension_semantics=sem or None, vmem_limit_bytes=VMEM_LIMIT_BYTES)


def _dot(a, b, dims=None):
    if dims is None:
        return jnp.dot(a, b, preferred_element_type=F32)
    return lax.dot_general(a, b, dims, preferred_element_type=F32)


def _split_terms(x, terms):
    out, rem = [], x
    for i in range(terms):
        hi = rem.astype(BF16)
        out.append(hi)
        if i + 1 < terms:
            rem = rem - hi.astype(F32)
    return out


def _split_dot(x, m, terms):
    acc = None
    for hi in _split_terms(x, terms):
        part = _dot(hi, m)
        acc = part if acc is None else acc + part
    return acc


def _split_dot_left(m, x, terms):
    acc = None
    for hi in _split_terms(x, terms):
        part = _dot(m, hi)
        acc = part if acc is None else acc + part
    return acc


def _gelu_and_grad(x):
    c = 0.7978845608028654
    inner = c * (x + 0.044715 * x * x * x)
    t = jnp.tanh(inner)
    g = 0.5 * x * (1.0 + t)
    dg = 0.5 * (1.0 + t) + 0.5 * x * (1.0 - t * t) * c * (1.0 + 3.0 * 0.044715 * x * x)
    return g, dg


def _softplus(x):
    return jnp.maximum(x, 0.0) + jnp.log(1.0 + jnp.exp(-jnp.abs(x)))


def _rsum(x):
    return jnp.sum(x, axis=0, keepdims=True)


def _acc_rows(ref, part, first):
    val = jnp.broadcast_to(part, ref.shape)

    @pl.when(first)
    def _():
        ref[...] = val

    @pl.when(jnp.logical_not(first))
    def _():
        ref[...] += val


def _rms_bwd(n, g, dout):
    r = lax.rsqrt(jnp.mean(n * n, axis=-1, keepdims=True) + EPS)
    nh = n * r
    dg = dout * g
    dn = r * (dg - nh * jnp.mean(dg * nh, axis=-1, keepdims=True))
    return dn, _rsum(dout * nh)


def _const_mats():
    avg = np.kron(np.eye(4), np.full((HEAD_DIM, HEAD_DIM), 1.0 / HEAD_DIM))
    expand = np.zeros((CHUNK, SSM_WIDTH), np.float32)
    for h in range(N_HEADS):
        expand[h, h * HEAD_DIM:(h + 1) * HEAD_DIM] = 1.0
    tril = np.tril(np.ones((CHUNK, CHUNK), np.float32))
    as_bf16 = lambda a: jnp.asarray(a, dtype=BF16)
    return as_bf16(avg), as_bf16(expand), as_bf16(expand.T), as_bf16(tril), as_bf16(tril.T)


def _full(shape):
    nd = len(shape)
    return pl.BlockSpec(shape, lambda *_: (0,) * nd)


_HBM = pl.BlockSpec(memory_space=pltpu.HBM)
_SEM = pl.BlockSpec(memory_space=pltpu.SEMAPHORE)
_ALL_PEERS = tuple((k, 0) for k in range(1, N_DEV))
_SAME_CORE_PEERS = ((2, 0), (4, 0), (6, 0))
_SIBLING_FORWARD = ((1, 0), (1, 2), (1, 4), (1, 6))


def _flip(j, k):
    for bit in (4, 2, 1):
        if k & bit:
            j = j + bit - 2 * (j & bit)
    return j


def _copies(src, land, send_sems, recv_sems, hops):
    x, y, c = lax.axis_index("x"), lax.axis_index("y"), lax.axis_index("c")
    me = 4 * x + 2 * y + c
    out = []
    for t in range(len(src)):
        for i, (k, b) in enumerate(hops):
            pos = (1 - x if k & 4 else x, 1 - y if k & 2 else y, 1 - c if k & 1 else c)
            peer = _flip(me, k)
            sem = t * len(hops) + i
            mk = functools.partial(pltpu.make_async_remote_copy, send_sem=send_sems.at[sem], recv_sem=recv_sems.at[sem],
                                   device_id=pos, device_id_type=pl.DeviceIdType.MESH)
            if land[t] is None and src[t].shape[0] != N_DEV:
                width = src[t].shape[1] // N_DEV
                slab = lambda j: src[t].at[:, pl.ds(pl.multiple_of(j * width, 128), width)]
                mine = functools.partial(mk, src_ref=slab(_flip(me, b)), dst_ref=slab(_flip(me, b)))
                theirs = functools.partial(mk, src_ref=slab(_flip(peer, b)), dst_ref=slab(_flip(peer, b)))
            elif land[t] is None:
                mine = functools.partial(mk, src_ref=src[t].at[_flip(me, b)], dst_ref=src[t].at[_flip(me, b)])
                theirs = functools.partial(mk, src_ref=src[t].at[_flip(peer, b)], dst_ref=src[t].at[_flip(peer, b)])
            else:
                assert b == 0
                mine = functools.partial(mk, src_ref=src[t].at[peer], dst_ref=land[t].at[me])
                theirs = functools.partial(mk, src_ref=src[t].at[peer], dst_ref=land[t].at[peer])
            out.append((mine, theirs))
    return out


def _exchange_start(srcs, inplace, peers, name, dep=None):
    n = len(srcs)
    lands = [None if ip else pltpu.with_memory_space_constraint(lax.empty(s.shape, s.dtype), pltpu.HBM)
             for s, ip in zip(srcs, inplace)]
    real_lands = [l for l in lands if l is not None]
    n_l = len(real_lands)
    deps = [] if dep is None else [dep]

    def body(*refs):
        src = refs[:n]
        land_refs = list(refs[n:n + n_l])
        send_sems, recv_sems = refs[n + n_l + len(deps)], refs[n + n_l + len(deps) + 1]
        token = refs[-1]
        land = [None if ip else land_refs.pop(0) for ip in inplace]
        for mine, _ in _copies(src, land, send_sems, recv_sems, peers):
            mine().start()
        token[...] = jnp.zeros_like(token)

    sem_t = pltpu.SemaphoreType.DMA((n * len(peers),))
    outs = pl.pallas_call(
        body, name=name,
        out_shape=(sem_t, sem_t) + tuple(pltpu.HBM(a.shape, a.dtype) for a in list(srcs) + real_lands)
        + (jax.ShapeDtypeStruct((8, 128), F32),),
        in_specs=[_HBM] * (n + n_l) + [pl.BlockSpec(memory_space=pl.ANY)] * len(deps),
        out_specs=(_SEM, _SEM) + (_HBM,) * (n + n_l) + (pl.BlockSpec(memory_space=pltpu.VMEM),),
        input_output_aliases={i: 2 + i for i in range(n + n_l)},
        compiler_params=pltpu.CompilerParams(has_side_effects=pltpu.SideEffectType.DATAFLOW_SIDE_EFFECTING),
    )(*[pltpu.with_memory_space_constraint(s, pltpu.HBM) for s in srcs], *real_lands, *deps)
    handle = dict(send=outs[0], recv=outs[1], srcs=outs[2:2 + n], lands=outs[2 + n:2 + n + n_l], inplace=inplace,
                  peers=peers)
    return handle, outs[-1]


def _exchange_wait(handle, after, name):
    srcs, lands, inplace, peers = handle["srcs"], handle["lands"], handle["inplace"], handle["peers"]
    n, n_l = len(srcs), len(lands)

    def body(*refs):
        src = refs[:n]
        land_refs = list(refs[n:n + n_l])
        send_sems, recv_sems = refs[n + n_l], refs[n + n_l + 1]
        land = [None if ip else land_refs.pop(0) for ip in inplace]
        for mine, theirs in _copies(src, land, send_sems, recv_sems, peers):
            mine().wait_send()
            theirs().wait_recv()

    outs = pl.pallas_call(
        body, name=name, out_shape=tuple(pltpu.HBM(a.shape, a.dtype) for a in list(srcs) + list(lands)),
        in_specs=[_HBM] * (n + n_l) + [_SEM, _SEM, pl.BlockSpec(memory_space=pl.ANY)],
        out_specs=(_HBM,) * (n + n_l), input_output_aliases={i: i for i in range(n + n_l)},
        compiler_params=pltpu.CompilerParams(has_side_effects=pltpu.SideEffectType.DATAFLOW_SIDE_EFFECTING),
    )(*srcs, *lands, handle["send"], handle["recv"], after)
    res, land_out = [], list(outs[n:])
    for t in range(n):
        res.append((outs[t], outs[t] if inplace[t] else land_out.pop(0)))
    return res


def _cast_to_slot(w, me, rows, name, cols=False):
    r, cdim = w.shape

    def body(me_ref, w_ref, o_ref):
        if cols:
            o_ref[...] = w_ref[...].astype(BF16)
        else:
            o_ref[0] = w_ref[...].astype(BF16)

    if cols:
        out_shape = jax.ShapeDtypeStruct((r, N_DEV * cdim), BF16)
        out_spec = pl.BlockSpec((rows, cdim), lambda i, me_ref: (i, me_ref[0]))
    else:
        out_shape = jax.ShapeDtypeStruct((N_DEV, r, cdim), BF16)
        out_spec = pl.BlockSpec((1, rows, cdim), lambda i, me_ref: (me_ref[0], i, 0))
    return pl.pallas_call(
        body, name=name, out_shape=out_shape,
        grid_spec=pltpu.PrefetchScalarGridSpec(
            num_scalar_prefetch=1, grid=(r // rows,), in_specs=[pl.BlockSpec((rows, cdim), lambda i, me_ref: (i, 0))],
            out_specs=out_spec),
        compiler_params=_params("parallel"))(me, w)


def _adamw_math(w, g, m, v):
    m = ADAM_B1 * m + (1.0 - ADAM_B1) * g
    v = ADAM_B2 * v + (1.0 - ADAM_B2) * (g * g)
    m_hat = m / (1.0 - ADAM_B1 ** ADAM_STEP)
    v_hat = v / (1.0 - ADAM_B2 ** ADAM_STEP)
    delta = -ADAM_LR * (m_hat / (jnp.sqrt(v_hat) + ADAM_EPS) + ADAM_WD * w)
    return delta, m, v


def _sum_parts(me, p_ref, own):
    g = None
    for j in range(N_DEV):
        term = (p_ref[j] if own is None else jnp.where(me == j, own, p_ref[j])).astype(F32)
        g = term if g is None else g + term
    return g


def _adamw_reduce(parts, own, me, w, m, v, rows, name):
    r, cdim = w.shape

    def body(me_ref, p_ref, own_ref, w_ref, m_ref, v_ref, g_out, d_out, m_out, v_out):
        g = _sum_parts(me_ref[0], p_ref, own_ref[0])
        d, mn, vn = _adamw_math(w_ref[...], g, m_ref[...], v_ref[...])
        g_out[...] = g
        d_out[...] = d
        m_out[...] = mn
        v_out[...] = vn

    blk = pl.BlockSpec((rows, cdim), lambda i, me_ref: (i, 0))
    sds = jax.ShapeDtypeStruct(w.shape, F32)
    return pl.pallas_call(
        body, name=name, out_shape=(sds,) * 4,
        grid_spec=pltpu.PrefetchScalarGridSpec(
            num_scalar_prefetch=1, grid=(r // rows,),
            in_specs=[pl.BlockSpec((N_DEV, rows, cdim), lambda i, me_ref: (0, i, 0)),
                      pl.BlockSpec((1, rows, cdim), lambda i, me_ref: (me_ref[0], i, 0)), blk, blk, blk],
            out_specs=(blk,) * 4),
        compiler_params=_params("parallel"))(me, parts, own, w, m, v)


def _adamw_small(parts, own, me, w, m, v, mask, name):
    def body(me_ref, *refs):
        refs = list(refs)
        p_ref = refs.pop(0)
        own_ref = None if own is None else refs.pop(0)
        w_ref, m_ref, v_ref = refs[:3]
        k_ref = None if mask is None else refs[3]
        g_out, d_out, m_out, v_out = refs[-4:]
        g = _sum_parts(me_ref[0], p_ref, None if own is None else own_ref[me_ref[0]])
        if mask is not None:
            g = g * k_ref[...]
        d, mn, vn = _adamw_math(w_ref[...], g, m_ref[...], v_ref[...])
        g_out[...] = g
        d_out[...] = d
        m_out[...] = mn
        v_out[...] = vn

    def whole(shape):
        nd = len(shape)
        return pl.BlockSpec(shape, lambda i, me_ref: (0,) * nd)

    sds = jax.ShapeDtypeStruct(w.shape, F32)
    ins = [parts] + ([] if own is None else [own]) + [w, m, v] + ([] if mask is None else [mask])
    return pl.pallas_call(
        body, name=name, out_shape=(sds,) * 4,
        grid_spec=pltpu.PrefetchScalarGridSpec(
            num_scalar_prefetch=1, grid=(1,), in_specs=[whole(a.shape) for a in ins],
            out_specs=(whole(w.shape),) * 4),
        compiler_params=_params("arbitrary"))(me, *ins)


_IN_SPLITS = ((0, 512), (512, 1024), (1024, 1536), (1536, 2560), (2560, IN_PAD))


def _in_proj(x, g1, w_in, tm):
    t_tok = x.shape[0]

    def body(x_ref, g_ref, w_ref, h_ref, *outs):
        xv = x_ref[...]
        r = lax.rsqrt(jnp.mean(xv * xv, axis=-1, keepdims=True) + EPS)
        h = (xv * r * g_ref[...]).astype(BF16)
        h_ref[...] = h
        for (a, b), o_ref in zip(_IN_SPLITS, outs):
            o_ref[...] = _dot(h, w_ref[a:b, :], _NT)

    row = lambda n: pl.BlockSpec((tm, n), lambda i: (i, 0))
    widths = [b - a for a, b in _IN_SPLITS]
    return pl.pallas_call(
        body, name="in_proj", grid=(t_tok // tm,),
        out_shape=(jax.ShapeDtypeStruct((t_tok, D_MODEL), BF16),) + tuple(
            jax.ShapeDtypeStruct((t_tok, n), F32) for n in widths),
        in_specs=[row(D_MODEL), _full((1, D_MODEL)), _full((IN_PAD, D_MODEL))],
        out_specs=(row(D_MODEL),) + tuple(row(n) for n in widths),
        compiler_params=_params("parallel"))(x, g1, w_in)


def _lane_masks():
    lane = lax.broadcasted_iota(jnp.int32, (1, 2 * HEAD_DIM), 1)
    left = (lane < HEAD_DIM).astype(F32)
    return left, 1.0 - left


def _stack_pair(v, m_l, m_r):
    return jnp.concatenate([v * m_l, v * m_r], axis=0).astype(BF16)


def _head_mean(x, avg):
    n = avg.shape[0]
    return jnp.concatenate([_split_dot(x[:, n * i:n * (i + 1)], avg, 2) for i in range(x.shape[1] // n)], axis=1)


def _gmlp_common(u, v, lnw, lnb, avg, wcat_ref, bias, m_l, m_r):
    ug, dug = _gelu_and_grad(u)
    vg, dvg = _gelu_and_grad(v)
    mu = _head_mean(vg, avg)
    vc = vg - mu
    var = _head_mean(vc * vc, avg)
    rstd = lax.rsqrt(var + EPS)
    vhat = vc * rstd
    vn = vhat * lnw + lnb
    rows = []
    for r in range(u.shape[0] // CHUNK):
        cols = []
        for j in range(N_HEADS // 2):
            pair = vn[CHUNK * r:CHUNK * (r + 1), 128 * j:128 * (j + 1)]
            cols.append(_dot(wcat_ref[j], _stack_pair(pair, m_l, m_r)))
        rows.append(jnp.concatenate(cols, axis=1) + bias)
    mixed = jnp.concatenate(rows, axis=0)
    return ug, dug, dvg, rstd, vhat, vn, mixed


_GMLP_ROWS = 4 * CHUNK


def _gmlp_fwd(u, v, lnw, lnb, wcat, bias, avg):
    t_tok = u.shape[0]
    tm = min(_GMLP_ROWS, t_tok)

    def body(u_ref, v_ref, lnw_ref, lnb_ref, wcat_ref, bias_ref, avg_ref, o_ref):
        m_l, m_r = _lane_masks()
        ug, _, _, _, _, _, mixed = _gmlp_common(
            u_ref[...], v_ref[...], lnw_ref[...], lnb_ref[...], avg_ref[...], wcat_ref, bias_ref[...], m_l, m_r)
        o_ref[...] = (ug * mixed).astype(BF16)

    row = pl.BlockSpec((tm, GM_WIDTH), lambda i: (i, 0))
    return pl.pallas_call(
        body, name="gmlp_fwd", grid=(t_tok // tm,), out_shape=jax.ShapeDtypeStruct((t_tok, GM_WIDTH), BF16),
        in_specs=[row, row, _full((1, GM_WIDTH)), _full((1, GM_WIDTH)), _full(wcat.shape), _full(bias.shape),
                  _full(avg.shape)],
        out_specs=row, compiler_params=_params("parallel"))(u, v, lnw, lnb, wcat, bias, avg)


def _shift_rows(x, edge, j, down):
    groups, cols = x.shape[0] // 8, x.shape[1]
    amount = j if down else 8 - j
    rot = pltpu.roll(x.reshape(groups, 8, cols), amount, axis=1)
    edge_rot = pltpu.roll(edge, amount, axis=0)[None]
    sub = lax.broadcasted_iota(jnp.int32, (1, 8, 1), 1)
    if down:
        out = jnp.where(sub < j, jnp.concatenate([edge_rot, rot[:-1]], axis=0), rot)
    else:
        out = jnp.where(sub < 8 - j, rot, jnp.concatenate([rot[1:], edge_rot], axis=0))
    return out.reshape(x.shape)


def _ssd_common(xbc, tail, dtr, cw_ref, cb, dtb, alog, expand, tril):
    q = CHUNK
    taps = [_shift_rows(xbc, tail, 3 - k, True) for k in range(3)] + [xbc]
    pre = cb + cw_ref[0:1, :] * taps[0] + cw_ref[1:2, :] * taps[1] + cw_ref[2:3, :] * taps[2] + cw_ref[3:4, :] * taps[3]
    sg = jax.nn.sigmoid(pre)
    act = pre * sg
    lane = lax.broadcasted_iota(jnp.int32, (1, CHUNK), 1)
    a_row = jnp.where(lane < N_HEADS, -jnp.exp(alog), 0.0)
    dtp = dtr + dtb
    dt = _softplus(dtp)
    a_cs = _split_dot_left(tril, dt * a_row, 3)
    a_cs_t = a_cs.T
    dt_exp = _split_dot(dt, expand, 3)
    a_exp = _split_dot(a_cs, expand, 3)
    a_end = a_exp[q - 1:q, :]
    li = lax.broadcasted_iota(jnp.int32, (q, q), 0)
    si = lax.broadcasted_iota(jnp.int32, (q, q), 1)
    causal = si <= li
    decay = []
    for h in range(N_HEADS):
        seg = a_cs[:, h:h + 1] - a_cs_t[h:h + 1, :]
        decay.append(jnp.where(causal, jnp.exp(jnp.minimum(seg, 0.0)), 0.0))
    return dict(taps=taps, pre=pre, sg=sg, act=act, a_row=a_row, dtp=dtp, dt=dt, dt_exp=dt_exp, a_exp=a_exp,
                e=jnp.exp(a_exp), w_end=jnp.exp(a_end - a_exp), cd=jnp.exp(a_end), decay=decay)


def _ssd_specs(t_tok, seq, reverse):
    nc = seq // CHUNK

    def chunk(b, c):
        return b * nc + (nc - 1 - c if reverse else c)

    def row(n):
        return pl.BlockSpec((CHUNK, n), lambda b, c: (chunk(b, c), 0))

    tail = pl.BlockSpec((8, CONV_CH), lambda b, c: (jnp.maximum(chunk(b, c) * (CHUNK // 8) - 1, 0), 0))
    return nc, chunk, row, tail


def _ssd_fwd(z, xbc, dtr, cw, cb, dtb, alog, dskip_exp, nw, expand, tril, seq):
    t_tok = z.shape[0]
    nc, chunk, row, tail = _ssd_specs(t_tok, seq, False)

    def body(z_ref, xbc_ref, tail_ref, dtr_ref, cw_ref, cb_ref, dtb_ref, alog_ref, dsk_ref, nw_ref, exp_ref,
             tril_ref, o_ref, y_ref, st_ref, state_ref):
        c = pl.program_id(1)

        @pl.when(c == 0)
        def _():
            state_ref[...] = jnp.zeros_like(state_ref)

        m_l, m_r = _lane_masks()
        f = _ssd_common(xbc_ref[...], jnp.where(c == 0, 0.0, tail_ref[...]), dtr_ref[...], cw_ref, cb_ref[...],
                        dtb_ref[...], alog_ref[...], exp_ref[...], tril_ref[...])
        act = f["act"]
        xs = act[:, :SSM_WIDTH]
        xdt = xs * f["dt_exp"]
        xw = xdt * f["w_end"]
        state = state_ref[...]
        st_ref[0] = state
        ydiag, yoff, snew = [], [], []
        for g in range(2):
            bg = act[:, 512 + 128 * g:640 + 128 * g].astype(BF16)
            cg = act[:, 768 + 128 * g:896 + 128 * g].astype(BF16)
            cb_mat = _dot(cg, bg, _NT)
            for pr in range(2):
                h0 = 4 * g + 2 * pr
                gcat = jnp.concatenate(
                    [(cb_mat * f["decay"][h0]).astype(BF16), (cb_mat * f["decay"][h0 + 1]).astype(BF16)], axis=1)
                ydiag.append(_dot(gcat, _stack_pair(xdt[:, 64 * h0:64 * h0 + 128], m_l, m_r)))
            yoff.append(_dot(cg, state[:, 256 * g:256 * (g + 1)].astype(BF16)))
            snew.append(_dot(bg, xw[:, 256 * g:256 * (g + 1)].astype(BF16), _TN))
        y = jnp.concatenate(ydiag, axis=1) + f["e"] * jnp.concatenate(yoff, axis=1) + dsk_ref[...] * xs
        state_ref[...] = state * f["cd"] + jnp.concatenate(snew, axis=1)
        y_ref[...] = y
        zv = z_ref[...]
        yg = y * (zv * jax.nn.sigmoid(zv))
        outs = []
        for g in range(2):
            ygg = yg[:, 256 * g:256 * (g + 1)]
            outs.append(ygg * lax.rsqrt(jnp.mean(ygg * ygg, axis=-1, keepdims=True) + EPS))
        o_ref[...] = (jnp.concatenate(outs, axis=1) * nw_ref[...]).astype(BF16)

    consts = [cw, cb, dtb, alog, dskip_exp, nw, expand, tril]
    return pl.pallas_call(
        body, name="ssd_fwd", grid=(t_tok // seq, nc),
        out_shape=(jax.ShapeDtypeStruct((t_tok, SSM_WIDTH), BF16), jax.ShapeDtypeStruct((t_tok, SSM_WIDTH), F32),
                   jax.ShapeDtypeStruct((t_tok // CHUNK, N_STATE, SSM_WIDTH), F32)),
        in_specs=[row(SSM_WIDTH), row(CONV_CH), tail, row(CHUNK)] + [_full(a.shape) for a in consts],
        out_specs=(row(SSM_WIDTH), row(SSM_WIDTH),
                   pl.BlockSpec((1, N_STATE, SSM_WIDTH), lambda b, c: (chunk(b, c), 0, 0))),
        scratch_shapes=[pltpu.VMEM((N_STATE, SSM_WIDTH), F32)],
        compiler_params=_params("arbitrary", "arbitrary"))(z, xbc, xbc, dtr, *consts)


def _out_proj(mix_a, mix_b, w_out, x, g2, g3, tm, dep=None):
    t_tok = x.shape[0]
    deps = [] if dep is None else [dep]

    def body(a_ref, b_ref, w_ref, x_ref, g2_ref, g3_ref, *rest):
        o_ref, x2_ref, h3_ref, mix_ref = rest[-4:]
        o = _dot(a_ref[...], w_ref[0:GM_WIDTH, :]) + _dot(b_ref[...], w_ref[GM_WIDTH:, :])
        o_ref[...] = o
        mix_ref[:, 0:GM_WIDTH] = a_ref[...]
        mix_ref[:, GM_WIDTH:] = b_ref[...]
        r2 = lax.rsqrt(jnp.mean(o * o, axis=-1, keepdims=True) + EPS)
        x2 = x_ref[...] + o * r2 * g2_ref[...]
        x2_ref[...] = x2
        r3 = lax.rsqrt(jnp.mean(x2 * x2, axis=-1, keepdims=True) + EPS)
        h3_ref[...] = (x2 * r3 * g3_ref[...]).astype(BF16)

    row = lambda n: pl.BlockSpec((tm, n), lambda i: (i, 0))
    sd = lambda dt: jax.ShapeDtypeStruct((t_tok, D_MODEL), dt)
    return pl.pallas_call(
        body, name="out_proj", grid=(t_tok // tm,), out_shape=(sd(F32), sd(F32), sd(BF16), sd(BF16)),
        in_specs=[row(GM_WIDTH), row(SSM_WIDTH), _full((D_MODEL, D_MODEL)), row(D_MODEL), _full((1, D_MODEL)),
                  _full((1, D_MODEL))] + [pl.BlockSpec(memory_space=pl.ANY)] * len(deps),
        out_specs=(row(D_MODEL),) * 4, compiler_params=_params("parallel"))(mix_a, mix_b, w_out, x, g2, g3, *deps)


def _mlp_fwd(h3, w_up, w_down, x2, target, g4, tm, tf):
    t_tok = x2.shape[0]
    nf = D_FF // tf

    def body(h_ref, wu_ref, wd_ref, x2_ref, t_ref, g4_ref, ra_ref, dd_ref, dy_ref, dg4_ref, loss_ref, acc_ref):
        i, j = pl.program_id(0), pl.program_id(1)
        ra = jnp.maximum(_dot(h_ref[...], wu_ref[...]), 0.0).astype(BF16)
        ra_ref[...] = ra
        part = _dot(ra * ra, wd_ref[...])

        @pl.when(j == 0)
        def _():
            acc_ref[...] = part

        @pl.when(j > 0)
        def _():
            acc_ref[...] += part

        @pl.when(j == nf - 1)
        def _():
            dvec = acc_ref[...]
            r4 = lax.rsqrt(jnp.mean(dvec * dvec, axis=-1, keepdims=True) + EPS)
            dn = dvec * r4
            g4 = g4_ref[...]
            err = x2_ref[...] + dn * g4 - t_ref[...]
            dy = err * (1.0 / D_MODEL)
            dy_ref[...] = dy
            dg = dy * g4
            dd_ref[...] = (r4 * (dg - dn * jnp.mean(dg * dn, axis=-1, keepdims=True))).astype(BF16)
            _acc_rows(dg4_ref, _rsum(dy * dn), i == 0)
            tile_loss = 0.5 * jnp.sum(jnp.sum(err * err, axis=-1, keepdims=True), axis=0, keepdims=True) / D_MODEL
            _acc_rows(loss_ref, jnp.broadcast_to(tile_loss, (1, 128)), i == 0)

    row = pl.BlockSpec((tm, D_MODEL), lambda i, j: (i, 0))
    return pl.pallas_call(
        body, name="mlp_fwd", grid=(t_tok // tm, nf),
        out_shape=(jax.ShapeDtypeStruct((t_tok, D_FF), BF16), jax.ShapeDtypeStruct((t_tok, D_MODEL), BF16),
                   jax.ShapeDtypeStruct((t_tok, D_MODEL), F32), jax.ShapeDtypeStruct((8, D_MODEL), F32),
                   jax.ShapeDtypeStruct((8, 128), F32)),
        in_specs=[row, pl.BlockSpec((D_MODEL, tf), lambda i, j: (0, j)),
                  pl.BlockSpec((tf, D_MODEL), lambda i, j: (j, 0)), row, row, _full((1, D_MODEL))],
        out_specs=(pl.BlockSpec((tm, tf), lambda i, j: (i, j)), row, row, _full((8, D_MODEL)), _full((8, 128))),
        scratch_shapes=[pltpu.VMEM((tm, D_MODEL), F32)],
        compiler_params=_params("arbitrary", "arbitrary"))(h3, w_up, w_down, x2, target, g4)


def _mlp_bwd(dd, w_down, ra, w_up, x2, dy, o, g3, g2, tm, tf):
    t_tok = x2.shape[0]
    nf = D_FF // tf

    def body(dd_ref, wd_ref, ra_ref, wu_ref, x2_ref, dy_ref, o_ref, g3_ref, g2_ref, da_ref, dx2_ref, do_ref, dg3_ref,
             dg2_ref, acc_ref):
        i, j = pl.program_id(0), pl.program_id(1)
        df = _dot(dd_ref[...], wd_ref[...], _NT)
        da = (df * (2.0 * ra_ref[...].astype(F32))).astype(BF16)
        da_ref[...] = da
        part = _dot(da, wu_ref[...], _NT)

        @pl.when(j == 0)
        def _():
            acc_ref[...] = part

        @pl.when(j > 0)
        def _():
            acc_ref[...] += part

        @pl.when(j == nf - 1)
        def _():
            dn3, dg3 = _rms_bwd(x2_ref[...], g3_ref[...], acc_ref[...])
            dx2 = dy_ref[...] + dn3
            dx2_ref[...] = dx2
            do, dg2 = _rms_bwd(o_ref[...], g2_ref[...], dx2)
            do_ref[...] = do.astype(BF16)
            _acc_rows(dg3_ref, dg3, i == 0)
            _acc_rows(dg2_ref, dg2, i == 0)

    row = pl.BlockSpec((tm, D_MODEL), lambda i, j: (i, 0))
    vec = _full((1, D_MODEL))
    acc = _full((8, D_MODEL))
    sd = lambda dt: jax.ShapeDtypeStruct((t_tok, D_MODEL), dt)
    return pl.pallas_call(
        body, name="mlp_bwd", grid=(t_tok // tm, nf),
        out_shape=(jax.ShapeDtypeStruct((t_tok, D_FF), BF16), sd(F32), sd(BF16),
                   jax.ShapeDtypeStruct((8, D_MODEL), F32), jax.ShapeDtypeStruct((8, D_MODEL), F32)),
        in_specs=[row, pl.BlockSpec((tf, D_MODEL), lambda i, j: (j, 0)), pl.BlockSpec((tm, tf), lambda i, j: (i, j)),
                  pl.BlockSpec((D_MODEL, tf), lambda i, j: (0, j)), row, row, row, vec, vec],
        out_specs=(pl.BlockSpec((tm, tf), lambda i, j: (i, j)), row, row, acc, acc),
        scratch_shapes=[pltpu.VMEM((tm, D_MODEL), F32)],
        compiler_params=_params("arbitrary", "arbitrary"))(dd, w_down, ra, w_up, x2, dy, o, g3, g2)


def _wgrad(a, b, out_blocks, bm, bn, bk, square_a, name, dep=None):
    t_tok, m = a.shape
    n = b.shape[1]
    nk = t_tok // bk

    def body(a_ref, b_ref, *rest):
        o_ref, acc_ref = rest[-2:]
        k = pl.program_id(2)
        av = a_ref[...]
        if square_a:
            av = av * av
        part = _dot(av, b_ref[...], _TN)

        def emit(res):
            if out_blocks is None:
                o_ref[...] = res.astype(BF16)
            else:
                o_ref[0] = res.astype(BF16)

        if nk == 1:
            emit(part)
            return

        @pl.when(k == 0)
        def _():
            acc_ref[...] = part

        @pl.when(k > 0)
        def _():
            acc_ref[...] += part

        @pl.when(k == nk - 1)
        def _():
            emit(acc_ref[...])

    if out_blocks is None:
        out_shape = jax.ShapeDtypeStruct((m, n), BF16)
        out_spec = pl.BlockSpec((bm, bn), lambda i, j, k: (i, j))
    else:
        assert n // out_blocks == bn
        out_shape = jax.ShapeDtypeStruct((out_blocks, m, bn), BF16)
        out_spec = pl.BlockSpec((1, bm, bn), lambda i, j, k: (j, i, 0))
    deps = [] if dep is None else [dep]
    return pl.pallas_call(
        body, name=name, grid=(m // bm, n // bn, nk), out_shape=out_shape,
        in_specs=[pl.BlockSpec((bk, bm), lambda i, j, k: (k, i)), pl.BlockSpec((bk, bn), lambda i, j, k: (k, j))]
        + [pl.BlockSpec(memory_space=pl.ANY)] * len(deps),
        out_specs=out_spec, scratch_shapes=[pltpu.VMEM((bm, bn) if nk > 1 else (8, 128), F32)],
        compiler_params=_params("parallel", "parallel", "arbitrary"))(a, b, *deps)


def _wgrad_in(h1, pieces, bn, bk, dep=None):
    t_tok = h1.shape[0]
    nk = t_tok // bk
    widths = [b - a for a, b in _IN_SPLITS]

    def body(h_ref, *rest):
        piece_refs = rest[:len(widths)]
        o_ref, acc_ref = rest[-2:]
        k = pl.program_id(1)
        hv = h_ref[...]
        for (a, b), r in zip(_IN_SPLITS, piece_refs):
            part = _dot(r[...], hv, _TN)
            if nk == 1:
                o_ref[a:b, :] = part.astype(BF16)
                continue

            @pl.when(k == 0)
            def _():
                acc_ref[a:b, :] = part

            @pl.when(k > 0)
            def _():
                acc_ref[a:b, :] += part

        if nk > 1:
            @pl.when(k == nk - 1)
            def _():
                o_ref[...] = acc_ref[...].astype(BF16)

    deps = [] if dep is None else [dep]
    return pl.pallas_call(
        body, name="wgrad_in", grid=(D_MODEL // bn, nk), out_shape=jax.ShapeDtypeStruct((IN_PAD, D_MODEL), BF16),
        in_specs=[pl.BlockSpec((bk, bn), lambda j, k: (k, j))] + [pl.BlockSpec((bk, n), lambda j, k: (k, 0)) for n in widths]
        + [pl.BlockSpec(memory_space=pl.ANY)] * len(deps),
        out_specs=pl.BlockSpec((IN_PAD, bn), lambda j, k: (0, j)),
        scratch_shapes=[pltpu.VMEM((IN_PAD, bn) if nk > 1 else (8, 128), F32)],
        compiler_params=_params("parallel", "arbitrary"))(h1, *pieces, *deps)


def _dmix(do, w_out, tm, dep=None):
    t_tok = do.shape[0]

    def body(d_ref, w_ref, *rest):
        rest[-1][...] = _dot(d_ref[...], w_ref[...], _NT)

    row = pl.BlockSpec((tm, D_MODEL), lambda i: (i, 0))
    deps = [] if dep is None else [dep]
    return pl.pallas_call(
        body, name="dmix", grid=(t_tok // tm,), out_shape=jax.ShapeDtypeStruct((t_tok, D_MODEL), F32),
        in_specs=[row, _full((D_MODEL, D_MODEL))] + [pl.BlockSpec(memory_space=pl.ANY)] * len(deps), out_specs=row,
        compiler_params=_params("parallel"))(do, w_out, *deps)


def _gmlp_bwd(dmix, u, v, lnw, lnb, wcat, wtcat, bias, avg, expand_t):
    t_tok = u.shape[0]
    tm = min(_GMLP_ROWS, t_tok)

    def body(dm_ref, u_ref, v_ref, lnw_ref, lnb_ref, wcat_ref, wtcat_ref, bias_ref, avg_ref, expt_ref, du_ref, dv_ref,
             dw_ref, db_ref, dlnw_ref, dlnb_ref):
        i = pl.program_id(0)
        m_l, m_r = _lane_masks()
        avg = avg_ref[...]
        lnw = lnw_ref[...]
        ug, dug, dvg, rstd, vhat, vn, mixed = _gmlp_common(
            u_ref[...], v_ref[...], lnw, lnb_ref[...], avg, wcat_ref, bias_ref[...], m_l, m_r)
        dya = dm_ref[...]
        du_ref[...] = (dya * mixed * dug).astype(BF16)
        dmixed = dya * ug
        dvn_rows, dws, dbt = [], [None] * N_HEADS, None
        for r in range(tm // CHUNK):
            dvn_cols = []
            for j in range(N_HEADS // 2):
                dmp = dmixed[CHUNK * r:CHUNK * (r + 1), 128 * j:128 * (j + 1)]
                dvn_cols.append(_dot(wtcat_ref[j], _stack_pair(dmp, m_l, m_r)))
                vnp = vn[CHUNK * r:CHUNK * (r + 1), 128 * j:128 * (j + 1)].astype(BF16)
                for i_h, mask in enumerate((m_l, m_r)):
                    part = _dot((dmp * mask).astype(BF16), vnp, _NT)
                    dws[2 * j + i_h] = part if r == 0 else dws[2 * j + i_h] + part
            dvn_rows.append(jnp.concatenate(dvn_cols, axis=1))
            part = _split_dot(dmixed[CHUNK * r:CHUNK * (r + 1), :], expt_ref[...], 2)
            dbt = part if r == 0 else dbt + part
        dvn = jnp.concatenate(dvn_rows, axis=0)
        dvh = dvn * lnw
        dvgel = rstd * (dvh - _head_mean(dvh, avg) - vhat * _head_mean(dvh * vhat, avg))
        dv_ref[...] = (dvgel * dvg).astype(BF16)
        first = i == 0

        @pl.when(first)
        def _():
            for h in range(N_HEADS):
                dw_ref[h] = dws[h]
            db_ref[...] = dbt

        @pl.when(jnp.logical_not(first))
        def _():
            for h in range(N_HEADS):
                dw_ref[h] += dws[h]
            db_ref[...] += dbt

        _acc_rows(dlnw_ref, _rsum(dvn * vhat), first)
        _acc_rows(dlnb_ref, _rsum(dvn), first)

    row = pl.BlockSpec((tm, GM_WIDTH), lambda i: (i, 0))
    consts = [lnw, lnb, wcat, wtcat, bias, avg, expand_t]
    return pl.pallas_call(
        body, name="gmlp_bwd", grid=(t_tok // tm,),
        out_shape=(jax.ShapeDtypeStruct((t_tok, GM_WIDTH), BF16), jax.ShapeDtypeStruct((t_tok, GM_WIDTH), BF16),
                   jax.ShapeDtypeStruct((N_HEADS, CHUNK, CHUNK), F32), jax.ShapeDtypeStruct((CHUNK, CHUNK), F32),
                   jax.ShapeDtypeStruct((8, GM_WIDTH), F32), jax.ShapeDtypeStruct((8, GM_WIDTH), F32)),
        in_specs=[row, row, row] + [_full(a.shape) for a in consts],
        out_specs=(row, row, _full((N_HEADS, CHUNK, CHUNK)), _full((CHUNK, CHUNK)), _full((8, GM_WIDTH)),
                   _full((8, GM_WIDTH))),
        compiler_params=_params("arbitrary"))(dmix, u, v, *consts)


def _ssd_bwd(dmix, z, xbc, dtr, y, states, cw, cb, dtb, alog, dskip_exp, nw, expand, expand_t, tril, triu, seq,
             dep=None):
    t_tok = z.shape[0]
    nc, chunk, row, tail = _ssd_specs(t_tok, seq, True)
    q = CHUNK

    def body(dm_ref, z_ref, xbc_ref, tail_ref, dtr_ref, y_ref, st_ref, cw_ref, cb_ref, dtb_ref, alog_ref, dsk_ref,
             nw_ref, exp_ref, expt_ref, tril_ref, triu_ref, dz_ref, dxbc_ref, ddt_ref, dcw_ref, dcb_ref, ddtb_ref,
             dalog_ref, dd_ref, dnw_ref, dhead_ref, dstate_ref):
        b, c = pl.program_id(0), pl.program_id(1)
        first = jnp.logical_and(b == 0, c == 0)

        @pl.when(c == 0)
        def _():
            dstate_ref[...] = jnp.zeros_like(dstate_ref)
            dhead_ref[...] = jnp.zeros_like(dhead_ref)

        m_l, m_r = _lane_masks()
        expt = expt_ref[...]
        f = _ssd_common(xbc_ref[...], jnp.where(c == nc - 1, 0.0, tail_ref[...]), dtr_ref[...], cw_ref, cb_ref[...],
                        dtb_ref[...], alog_ref[...], exp_ref[...], tril_ref[...])
        act, pre, sg = f["act"], f["pre"], f["sg"]
        xs = act[:, :SSM_WIDTH]
        xdt = xs * f["dt_exp"]
        xw = xdt * f["w_end"]
        state = st_ref[0]
        dstate = dstate_ref[...]
        zv, yv, dout, nw = z_ref[...], y_ref[...], dm_ref[...], nw_ref[...]
        sz = jax.nn.sigmoid(zv)
        sl = zv * sz
        yg = yv * sl
        tv = dout * nw
        dyg_parts, ygh_parts = [], []
        for g in range(2):
            ygg = yg[:, 256 * g:256 * (g + 1)]
            rr = lax.rsqrt(jnp.mean(ygg * ygg, axis=-1, keepdims=True) + EPS)
            ygh = ygg * rr
            tg = tv[:, 256 * g:256 * (g + 1)]
            dyg_parts.append(rr * (tg - ygh * jnp.mean(tg * ygh, axis=-1, keepdims=True)))
            ygh_parts.append(ygh)
        dyg = jnp.concatenate(dyg_parts, axis=1)
        dnw = _rsum(dout * jnp.concatenate(ygh_parts, axis=1))
        dy = dyg * sl
        dz_ref[...] = (dyg * yv * (sz * (1.0 + zv * (1.0 - sz)))).astype(BF16)
        ddsk = _rsum(dy * xs)
        dye = dy * f["e"]
        lane = lax.broadcasted_iota(jnp.int32, (q, q), 1)
        sub = lax.broadcasted_iota(jnp.int32, (q, q), 0)
        rs_mat = jnp.zeros((q, q), F32)
        cs_mat = jnp.zeros((q, q), F32)
        dxdt_cols, yoff, dst_in, dxw, d_b, d_c = [], [], [], [], [], []
        for g in range(2):
            bg = act[:, 512 + 128 * g:640 + 128 * g].astype(BF16)
            cg = act[:, 768 + 128 * g:896 + 128 * g].astype(BF16)
            cb_mat = _dot(cg, bg, _NT)
            stg = state[:, 256 * g:256 * (g + 1)].astype(BF16)
            dyeg = dye[:, 256 * g:256 * (g + 1)].astype(BF16)
            yoff.append(_dot(cg, stg))
            dcg = _dot(dyeg, stg, _NT)
            dst_in.append(_dot(cg, dyeg, _TN))
            dcb = jnp.zeros((q, q), F32)
            for pr in range(2):
                h0 = 4 * g + 2 * pr
                gf = [cb_mat * f["decay"][h0], cb_mat * f["decay"][h0 + 1]]
                gcat = jnp.concatenate([gf[0].astype(BF16), gf[1].astype(BF16)], axis=1)
                xst = _stack_pair(xdt[:, 64 * h0:64 * h0 + 128], m_l, m_r)
                dyp = dy[:, 64 * h0:64 * h0 + 128].astype(BF16)
                dgcat = _dot(dyp, xst, _NT)
                dxst = _dot(gcat, dyp, _TN)
                dxdt_cols.append(dxst[:q] * m_l + dxst[q:] * m_r)
                for i in range(2):
                    h = h0 + i
                    dg = dgcat[:, q * i:q * (i + 1)]
                    mm = dg * gf[i]
                    rs_mat = rs_mat + jnp.where(lane == h, jnp.sum(mm, axis=1, keepdims=True), 0.0)
                    cs_mat = cs_mat + jnp.where(sub == h, jnp.sum(mm, axis=0, keepdims=True), 0.0)
                    dcb = dcb + dg * f["decay"][h]
            dcb16 = dcb.astype(BF16)
            dstg = dstate[:, 256 * g:256 * (g + 1)].astype(BF16)
            d_c.append(dcg + _dot(dcb16, bg))
            dxw.append(_dot(bg, dstg))
            d_b.append(_dot(dcb16, cg, _TN) + _dot(xw[:, 256 * g:256 * (g + 1)].astype(BF16), dstg, _NT))
        dxw = jnp.concatenate(dxw, axis=1)
        dxdt = jnp.concatenate(dxdt_cols, axis=1) + dxw * f["w_end"]
        qv = dxw * xw
        end_row = _rsum(qv) + _rsum(dstate * state) * f["cd"]
        x2 = dye * jnp.concatenate(yoff, axis=1) - qv
        row_i = lax.broadcasted_iota(jnp.int32, (q, 1), 0)
        x2 = x2 + jnp.where(row_i == q - 1, end_row, 0.0)
        da_cs = _split_dot(x2, expt, 3) + rs_mat - cs_mat.T
        ddt = _split_dot(dxdt * xs, expt, 3)
        dxs = dsk_ref[...] * dy + dxdt * f["dt_exp"]
        dda = _split_dot_left(triu_ref[...], da_cs, 3)
        ddt = ddt + dda * f["a_row"]
        dalog = _rsum(dda * f["dt"]) * f["a_row"]
        draw = ddt * jax.nn.sigmoid(f["dtp"])
        ddt_ref[...] = draw.astype(BF16)
        dact = jnp.concatenate([dxs] + d_b + d_c, axis=1)
        dpre = dact * (sg * (1.0 + pre * (1.0 - sg)))
        dhead = dhead_ref[...]
        dxbc = cw_ref[3:4, :] * dpre
        for k in range(3):
            dxbc = dxbc + cw_ref[k:k + 1, :] * _shift_rows(dpre, dhead, 3 - k, False)
        dxbc_ref[...] = dxbc.astype(BF16)
        dhead_ref[...] = dpre[0:8, :]
        dstate_ref[...] = dstate * f["cd"] + jnp.concatenate(dst_in, axis=1)
        row8 = lax.broadcasted_iota(jnp.int32, (8, 1), 0)
        dcw = jnp.zeros((8, CONV_CH), F32)
        for k in range(4):
            dcw = dcw + jnp.where(row8 == k, _rsum(dpre * f["taps"][k]), 0.0)

        @pl.when(first)
        def _():
            dcw_ref[...] = dcw

        @pl.when(jnp.logical_not(first))
        def _():
            dcw_ref[...] += dcw

        _acc_rows(dcb_ref, _rsum(dpre), first)
        _acc_rows(ddtb_ref, _rsum(draw), first)
        _acc_rows(dalog_ref, dalog, first)
        _acc_rows(dd_ref, ddsk, first)
        _acc_rows(dnw_ref, dnw, first)

    consts = [cw, cb, dtb, alog, dskip_exp, nw, expand, expand_t, tril, triu]
    deps = [] if dep is None else [dep]
    n_in = 7 + len(consts)

    def body_skipping_dep(*refs):
        body(*refs[:n_in], *refs[n_in + len(deps):])

    acc = lambda n: jax.ShapeDtypeStruct((8, n), F32)
    return pl.pallas_call(
        body_skipping_dep, name="ssd_bwd", grid=(t_tok // seq, nc),
        out_shape=(jax.ShapeDtypeStruct((t_tok, SSM_WIDTH), BF16), jax.ShapeDtypeStruct((t_tok, CONV_CH), BF16),
                   jax.ShapeDtypeStruct((t_tok, CHUNK), BF16), acc(CONV_CH), acc(CONV_CH), acc(CHUNK), acc(CHUNK),
                   acc(SSM_WIDTH), acc(SSM_WIDTH)),
        in_specs=[pl.BlockSpec((CHUNK, SSM_WIDTH), lambda b, c: (chunk(b, c), 1)), row(SSM_WIDTH), row(CONV_CH), tail,
                  row(CHUNK), row(SSM_WIDTH), pl.BlockSpec((1, N_STATE, SSM_WIDTH), lambda b, c: (chunk(b, c), 0, 0))]
        + [_full(a.shape) for a in consts] + [pl.BlockSpec(memory_space=pl.ANY)] * len(deps),
        out_specs=(row(SSM_WIDTH), row(CONV_CH), row(CHUNK), _full((8, CONV_CH)), _full((8, CONV_CH)),
                   _full((8, CHUNK)), _full((8, CHUNK)), _full((8, SSM_WIDTH)), _full((8, SSM_WIDTH))),
        scratch_shapes=[pltpu.VMEM((8, CONV_CH), F32), pltpu.VMEM((N_STATE, SSM_WIDTH), F32)],
        compiler_params=_params("arbitrary", "arbitrary"))(dmix, z, xbc, xbc, dtr, y, states, *consts, *deps)


def _in_bwd(du, dv, dz, dxbc, ddt, w_in, x, dx2, g1, tm, dep=None):
    t_tok = x.shape[0]

    def body(du_ref, dv_ref, dz_ref, dxbc_ref, ddt_ref, w_ref, x_ref, dx2_ref, g_ref, *rest):
        gx_ref, dg_ref = rest[-2:]
        i = pl.program_id(0)
        dh = None
        for (a, b), ref in zip(_IN_SPLITS, (du_ref, dv_ref, dz_ref, dxbc_ref, ddt_ref)):
            part = _dot(ref[...], w_ref[a:b, :])
            dh = part if dh is None else dh + part
        dn, dg = _rms_bwd(x_ref[...], g_ref[...], dh)
        gx_ref[...] = dx2_ref[...] + dn
        _acc_rows(dg_ref, dg, i == 0)

    row = lambda n: pl.BlockSpec((tm, n), lambda i: (i, 0))
    widths = [b - a for a, b in _IN_SPLITS]
    deps = [] if dep is None else [dep]
    return pl.pallas_call(
        body, name="in_bwd", grid=(t_tok // tm,),
        out_shape=(jax.ShapeDtypeStruct((t_tok, D_MODEL), F32), jax.ShapeDtypeStruct((8, D_MODEL), F32)),
        in_specs=[row(n) for n in widths] + [_full((IN_PAD, D_MODEL)), row(D_MODEL), row(D_MODEL), _full((1, D_MODEL))]
        + [pl.BlockSpec(memory_space=pl.ANY)] * len(deps),
        out_specs=(row(D_MODEL), _full((8, D_MODEL))),
        compiler_params=_params("arbitrary"))(du, dv, dz, dxbc, ddt, w_in, x, dx2, g1, *deps)


def _pad_lanes(a, n):
    return jnp.pad(a, ((0, 0), (0, n - a.shape[1])))


def _local_step(x, target, seq, w_in_t, conv_w, small, hooks):
    t_tok = x.shape[0]
    tm = min(512, t_tok)
    avg, expand, expand_t, tril, triu = _const_mats()
    g1, g2, g3, g4 = (small[k].reshape(1, D_MODEL) for k in
                      ("norm_mix_pre", "norm_mix_post", "norm_ffn_pre", "norm_ffn_post"))
    lnw = small["gm_ln_w"].reshape(1, GM_WIDTH)
    lnb = small["gm_ln_b"].reshape(1, GM_WIDTH)
    causal = jnp.tril(jnp.ones((CHUNK, CHUNK), F32))
    wm = small["gm_w_s"] * causal
    pair = lambda w: w.reshape(4, 2, CHUNK, CHUNK).transpose(0, 2, 1, 3).reshape(4, CHUNK, 2 * CHUNK).astype(BF16)
    wcat = pair(wm)
    wtcat = pair(jnp.swapaxes(wm, 1, 2))
    bias = jnp.repeat(small["gm_b_s"].T, HEAD_DIM, axis=1)
    cb = small["conv_b"].reshape(1, CONV_CH)
    dtb = _pad_lanes(small["dt_bias"].reshape(1, N_HEADS), CHUNK)
    alog = _pad_lanes(small["a_log"].reshape(1, N_HEADS), CHUNK)
    dskip_exp = jnp.repeat(small["d_skip"].reshape(1, N_HEADS), HEAD_DIM, axis=1)
    nw = small["ssm_norm_w"].reshape(1, SSM_WIDTH)

    h1, u, v, z, xbc, dtr = _in_proj(x, g1, w_in_t, tm)
    mix_a = _gmlp_fwd(u, v, lnw, lnb, wcat, bias, avg)
    mix_b, y_pre, states = _ssd_fwd(z, xbc, dtr, conv_w, cb, dtb, alog, dskip_exp, nw, expand, tril, seq)
    w_out, dep = hooks["mixers_done"](mix_b)
    o, x2, h3, mix = _out_proj(mix_a, mix_b, w_out, x, g2, g3, tm, dep)
    w_up, w_down = hooks["mlp_weights"](h3)
    tf = 2048
    ra, dd, dy, dg4, loss = _mlp_fwd(h3, w_up, w_down, x2, target, g4, tm, tf)

    da, dx2, do, dg3, dg2 = _mlp_bwd(dd, w_down, ra, w_up, x2, dy, o, g3, g2, tm, tf)
    bk = min(2048, t_tok)
    g_w_down = _wgrad(ra, dd, None, 512, 512, t_tok, True, "wgrad_down")
    g_w_up = _wgrad(h3, da, N_DEV, 512, D_FF // N_DEV, t_tok, False, "wgrad_up")
    dep = hooks["mlp_grads"](g_w_down, g_w_up)
    dmix = _dmix(do, w_out, tm, dep)
    g_w_out = _wgrad(mix, do, None, 512, 512, t_tok, False, "wgrad_out", dep)
    du, dv, dws, dbt, dlnw, dlnb = _gmlp_bwd(dmix, u, v, lnw, lnb, wcat, wtcat, bias, avg, expand_t)
    dep = hooks["gmlp_grads"](g_w_out, dws)
    dz, dxbc, ddt, dcw, dcb, ddtb, dalog, ddsk, dnw = _ssd_bwd(
        dmix, z, xbc, dtr, y_pre, states, conv_w, cb, dtb, alog, dskip_exp, nw, expand, expand_t, tril, triu, seq, dep)
    g_w_in = _wgrad_in(h1, (du, dv, dz, dxbc, ddt), 512, bk, dep)
    dep = hooks["in_grads"](g_w_in, dcw[0:4])
    grad_x, dg1 = _in_bwd(du, dv, dz, dxbc, ddt, w_in_t, x, dx2, g1, tm, dep)

    grads = dict(
        w_in=g_w_in, w_out=g_w_out, w_up=g_w_up, w_down=g_w_down, conv_w=dcw[0:4],
        norm_mix_pre=dg1[0:1], norm_mix_post=dg2[0:1], norm_ffn_pre=dg3[0:1], norm_ffn_post=dg4[0:1],
        gm_ln_w=dlnw[0:1], gm_ln_b=dlnb[0:1], gm_w_s=dws, gm_b_s=dbt.T[0:N_HEADS], conv_b=dcb[0:1],
        dt_bias=ddtb[0:1, 0:N_HEADS], a_log=dalog[0:1, 0:N_HEADS],
        d_skip=ddsk[0:1].reshape(N_HEADS, HEAD_DIM).sum(axis=1).reshape(1, N_HEADS), ssm_norm_w=dnw[0:1])
    return loss[0, 0], grad_x, grads


_SMALL_ROW_PARAMS = ("norm_mix_pre", "norm_mix_post", "norm_ffn_pre", "norm_ffn_post", "gm_ln_w", "gm_ln_b", "gm_b_s",
                     "conv_b", "dt_bias", "a_log", "d_skip", "ssm_norm_w")
_WEIGHTS = ("norm_mix_pre", "w_in", "gm_ln_w", "gm_ln_b", "gm_w_s", "gm_b_s", "conv_w", "conv_b", "dt_bias", "a_log",
            "d_skip", "ssm_norm_w", "w_out", "norm_mix_post", "norm_ffn_pre", "w_up", "w_down", "norm_ffn_post")


def _pack_rows(tensors):
    rows = [_pad_lanes(t.reshape(1, -1), D_MODEL) for t in tensors]
    rows.append(jnp.zeros((SMALL_ROWS - len(rows), D_MODEL), F32))
    return jnp.concatenate(rows, axis=0)


def kernel(x, norm_mix_pre, w_in, gm_ln_w, gm_ln_b, gm_w_s, gm_b_s, conv_w, conv_b, dt_bias, a_log, d_skip, ssm_norm_w, w_out, norm_mix_post, norm_ffn_pre, w_up, w_down, norm_ffn_post, loss_target, m_norm_mix_pre, m_w_in, m_gm_ln_w, m_gm_ln_b, m_gm_w_s, m_gm_b_s, m_conv_w, m_conv_b, m_dt_bias, m_a_log, m_d_skip, m_ssm_norm_w, m_w_out, m_norm_mix_post, m_norm_ffn_pre, m_w_up, m_w_down, m_norm_ffn_post, v_norm_mix_pre, v_w_in, v_gm_ln_w, v_gm_ln_b, v_gm_w_s, v_gm_b_s, v_conv_w, v_conv_b, v_dt_bias, v_a_log, v_d_skip, v_ssm_norm_w, v_w_out, v_norm_mix_post, v_norm_ffn_pre, v_w_up, v_w_down, v_norm_ffn_post):
    w = dict(norm_mix_pre=norm_mix_pre, w_in=w_in, gm_ln_w=gm_ln_w, gm_ln_b=gm_ln_b, gm_w_s=gm_w_s, gm_b_s=gm_b_s, conv_w=conv_w, conv_b=conv_b, dt_bias=dt_bias, a_log=a_log, d_skip=d_skip, ssm_norm_w=ssm_norm_w, w_out=w_out, norm_mix_post=norm_mix_post, norm_ffn_pre=norm_ffn_pre, w_up=w_up, w_down=w_down, norm_ffn_post=norm_ffn_post)
    m = dict(norm_mix_pre=m_norm_mix_pre, w_in=m_w_in, gm_ln_w=m_gm_ln_w, gm_ln_b=m_gm_ln_b, gm_w_s=m_gm_w_s, gm_b_s=m_gm_b_s, conv_w=m_conv_w, conv_b=m_conv_b, dt_bias=m_dt_bias, a_log=m_a_log, d_skip=m_d_skip, ssm_norm_w=m_ssm_norm_w, w_out=m_w_out, norm_mix_post=m_norm_mix_post, norm_ffn_pre=m_norm_ffn_pre, w_up=m_w_up, w_down=m_w_down, norm_ffn_post=m_norm_ffn_post)
    v = dict(norm_mix_pre=v_norm_mix_pre, w_in=v_w_in, gm_ln_w=v_gm_ln_w, gm_ln_b=v_gm_ln_b, gm_w_s=v_gm_w_s, gm_b_s=v_gm_b_s, conv_w=v_conv_w, conv_b=v_conv_b, dt_bias=v_dt_bias, a_log=v_a_log, d_skip=v_d_skip, ssm_norm_w=v_ssm_norm_w, w_out=v_w_out, norm_mix_post=v_norm_mix_post, norm_ffn_pre=v_norm_ffn_pre, w_up=v_w_up, w_down=v_w_down, norm_ffn_post=v_norm_ffn_post)
    n_batch, seq, _ = x.shape
    shard_in = IN_COLS // N_DEV

    me = (4 * lax.axis_index("x") + 2 * lax.axis_index("y") + lax.axis_index("c")).astype(jnp.int32).reshape(1)

    def in_slot(own):
        return lax.dynamic_update_slice(lax.empty((N_DEV,) + own.shape, own.dtype), own[None],
                                        (me[0],) + (0,) * own.ndim)

    w_in_sh, m_in_sh, v_in_sh = w_in[0].T, m_w_in[0].T, v_w_in[0].T
    first = [_cast_to_slot(w_in_sh, me, shard_in, "cast_w_in"), in_slot(conv_w[0]),
             _cast_to_slot(w_out[0], me, 128, "cast_w_out")]
    ici_1, _ = _exchange_start(first, [True] * 3, _SAME_CORE_PEERS, "gather_mix_ici_start")
    first = [buf for buf, _ in _exchange_wait(ici_1, me, "gather_mix_ici_wait")]
    d2d_1, tok_d2d_1 = _exchange_start(first, [True] * 3, _SIBLING_FORWARD, "gather_mix_d2d_start")
    second = [_cast_to_slot(w_up[0], me, 256, "cast_w_up", cols=True), _cast_to_slot(w_down[0], me, 256, "cast_w_down")]
    ici_2, tok_ici_2 = _exchange_start(second, [True] * 2, _SAME_CORE_PEERS, "gather_mlp_ici_start", dep=tok_d2d_1)
    (_, ag_in), (_, ag_conv), (_, ag_out) = _exchange_wait(d2d_1, tok_ici_2, "gather_mix_d2d_wait")
    w_in_t = jnp.pad(ag_in.reshape(IN_COLS, D_MODEL), ((0, IN_PAD - IN_COLS), (0, 0)))
    conv_w_f = ag_conv.transpose(1, 0, 2).reshape(4, CONV_CH)
    w_out_f = ag_out.reshape(D_MODEL, D_MODEL)
    gathering = {}

    def mixers_done(after):
        bufs = [buf for buf, _ in _exchange_wait(ici_2, after, "gather_mlp_ici_wait")]
        gathering["mlp"], tok = _exchange_start(bufs, [True] * 2, _SIBLING_FORWARD, "gather_mlp_d2d_start")
        return w_out_f, tok

    def mlp_weights(after):
        (_, ag_up), (_, ag_down) = _exchange_wait(gathering["mlp"], after, "gather_mlp_d2d_wait")
        return ag_up, ag_down.reshape(D_FF, D_MODEL)

    sent = {}

    def mlp_grads(g_w_down, g_w_up):
        sent["mlp"], tok = _exchange_start(
            [g_w_down.reshape(N_DEV, D_FF // N_DEV, D_MODEL), g_w_up], [False, False], _ALL_PEERS, "grads_mlp_start")
        return tok

    def gmlp_grads(g_w_out, g_w_s):
        sent["gmlp"], tok = _exchange_start(
            [g_w_out.reshape(N_DEV, D_MODEL // N_DEV, D_MODEL), in_slot(g_w_s.astype(BF16))], [False, True], _ALL_PEERS,
            "grads_gmlp_start")
        return tok

    def in_grads(g_w_in_t, g_conv_w):
        g_in_blk = g_w_in_t[:IN_COLS].reshape(N_DEV, shard_in, D_MODEL)
        g_conv_blk = g_conv_w.reshape(4, N_DEV, CONV_CH // N_DEV).transpose(1, 0, 2)
        sent["in"], tok = _exchange_start([g_in_blk, g_conv_blk], [False, False], _ALL_PEERS, "grads_in_start")
        return tok

    small = {k: w[k][0] for k in _SMALL_ROW_PARAMS + ("gm_w_s",)}
    loss_part, grad_x, g = _local_step(
        x.reshape(n_batch * seq, D_MODEL), loss_target.reshape(n_batch * seq, D_MODEL), seq, w_in_t, conv_w_f, small,
        dict(mixers_done=mixers_done, mlp_weights=mlp_weights, mlp_grads=mlp_grads, gmlp_grads=gmlp_grads,
             in_grads=in_grads))

    loss_row = jnp.broadcast_to(loss_part, (1, D_MODEL))
    sent_rows, tok_rows = _exchange_start(
        [in_slot(_pack_rows([g[k] for k in _SMALL_ROW_PARAMS] + [loss_row]))], [True], _ALL_PEERS, "grads_rows_start")
    (own_down, p_down), (own_up, p_up) = _exchange_wait(sent["mlp"], tok_rows, "grads_mlp_wait")
    res = {}
    res["w_up"] = _adamw_reduce(p_up, own_up, me, w_up[0], m_w_up[0], v_w_up[0], 256, "adamw_w_up")
    res["w_down"] = _adamw_reduce(p_down, own_down, me, w_down[0], m_w_down[0], v_w_down[0], 128, "adamw_w_down")
    (own_out, p_out), (_, p_ws) = _exchange_wait(sent["gmlp"], res["w_down"][1], "grads_gmlp_wait")
    res["w_out"] = _adamw_reduce(p_out, own_out, me, w_out[0], m_w_out[0], v_w_out[0], 128, "adamw_w_out")
    causal = jnp.tril(jnp.ones((1, CHUNK, CHUNK), F32))
    res["gm_w_s"] = _adamw_small(p_ws, None, me, gm_w_s[0], m_gm_w_s[0], v_gm_w_s[0], causal, "adamw_gm_w_s")
    (own_in, p_in), (own_conv, p_conv) = _exchange_wait(sent["in"], res["gm_w_s"][1], "grads_in_wait")
    res["w_in"] = tuple(r.T for r in _adamw_reduce(p_in, own_in, me, w_in_sh, m_in_sh, v_in_sh, shard_in, "adamw_w_in"))
    res["conv_w"] = _adamw_small(p_conv, own_conv, me, conv_w[0], m_conv_w[0], v_conv_w[0], None, "adamw_conv_w")
    ((_, p_rows),) = _exchange_wait(sent_rows, res["w_in"][1], "grads_rows_wait")
    rows = _adamw_small(p_rows, None, me, _pack_rows([w[k] for k in _SMALL_ROW_PARAMS]),
                        _pack_rows([m[k] for k in _SMALL_ROW_PARAMS]), _pack_rows([v[k] for k in _SMALL_ROW_PARAMS]),
                        None, "adamw_rows")
    loss = rows[0][len(_SMALL_ROW_PARAMS), 0]
    for i, k in enumerate(_SMALL_ROW_PARAMS):
        size = int(np.prod(w[k].shape))
        res[k] = tuple(r[i, :size].reshape(w[k].shape) for r in rows)
    for k in ("w_in", "w_out", "w_up", "w_down", "conv_w", "gm_w_s"):
        res[k] = tuple(r.reshape(w[k].shape) for r in res[k])

    outs = [loss, grad_x.reshape(x.shape)]
    for part in range(4):
        outs.extend(res[k][part] for k in _WEIGHTS)
    return tuple(outs)
```

```python
import functools

import jax
import jax.numpy as jnp
import numpy as np
from jax import lax
from jax.experimental import pallas as pl
from jax.experimental.pallas import tpu as pltpu

F32 = jnp.float32
BF16 = jnp.bfloat16

D_MODEL = 1024
GM_WIDTH = 512
SSM_WIDTH = 512
CONV_CH = 1024
N_HEADS = 8
HEAD_DIM = 64
N_STATE = 128
CHUNK = 128
D_FF = 4096
IN_COLS = 2568
IN_PAD = 2688
N_DEV = 8
EPS = 1e-6
ADAM_LR, ADAM_B1, ADAM_B2, ADAM_EPS, ADAM_WD, ADAM_STEP = 0.001, 0.9, 0.999, 1e-08, 0.01, 10
VMEM_LIMIT_BYTES = 56 * 1024 * 1024
SMALL_ROWS = 16

_NT = (((1,), (1,)), ((), ()))
_TN = (((0,), (0,)), ((), ()))


def _params(*sem):
    return pltpu.CompilerParams(dimension_semantics=sem or None, vmem_limit_bytes=VMEM_LIMIT_BYTES)


def _dot(a, b, dims=None):
    if dims is None:
        return jnp.dot(a, b, preferred_element_type=F32)
    return lax.dot_general(a, b, dims, preferred_element_type=F32)


def _split_terms(x, terms):
    out, rem = [], x
    for i in range(terms):
        hi = rem.astype(BF16)
        out.append(hi)
        if i + 1 < terms:
            rem = rem - hi.astype(F32)
    return out


def _split_dot(x, m, terms):
    acc = None
    for hi in _split_terms(x, terms):
        part = _dot(hi, m)
        acc = part if acc is None else acc + part
    return acc


def _split_dot_left(m, x, terms):
    acc = None
    for hi in _split_terms(x, terms):
        part = _dot(m, hi)
        acc = part if acc is None else acc + part
    return acc


def _gelu_and_grad(x):
    c = 0.7978845608028654
    inner = c * (x + 0.044715 * x * x * x)
    t = jnp.tanh(inner)
    g = 0.5 * x * (1.0 + t)
    dg = 0.5 * (1.0 + t) + 0.5 * x * (1.0 - t * t) * c * (1.0 + 3.0 * 0.044715 * x * x)
    return g, dg


def _softplus(x):
    return jnp.maximum(x, 0.0) + jnp.log(1.0 + jnp.exp(-jnp.abs(x)))


def _rsum(x):
    return jnp.sum(x, axis=0, keepdims=True)


def _acc_rows(ref, part, first):
    val = jnp.broadcast_to(part, ref.shape)

    @pl.when(first)
    def _():
        ref[...] = val

    @pl.when(jnp.logical_not(first))
    def _():
        ref[...] += val


def _rms_bwd(n, g, dout):
    r = lax.rsqrt(jnp.mean(n * n, axis=-1, keepdims=True) + EPS)
    nh = n * r
    dg = dout * g
    dn = r * (dg - nh * jnp.mean(dg * nh, axis=-1, keepdims=True))
    return dn, _rsum(dout * nh)


def _const_mats():
    avg = np.kron(np.eye(4), np.full((HEAD_DIM, HEAD_DIM), 1.0 / HEAD_DIM))
    expand = np.zeros((CHUNK, SSM_WIDTH), np.float32)
    for h in range(N_HEADS):
        expand[h, h * HEAD_DIM:(h + 1) * HEAD_DIM] = 1.0
    tril = np.tril(np.ones((CHUNK, CHUNK), np.float32))
    as_bf16 = lambda a: jnp.asarray(a, dtype=BF16)
    return as_bf16(avg), as_bf16(expand), as_bf16(expand.T), as_bf16(tril), as_bf16(tril.T)


def _full(shape):
    nd = len(shape)
    return pl.BlockSpec(shape, lambda *_: (0,) * nd)


_HBM = pl.BlockSpec(memory_space=pltpu.HBM)
_SEM = pl.BlockSpec(memory_space=pltpu.SEMAPHORE)
_ALL_PEERS = tuple((k, 0) for k in range(1, N_DEV))
_SAME_CORE_PEERS = ((2, 0), (4, 0), (6, 0))
_SIBLING_FORWARD = ((1, 0), (1, 2), (1, 4), (1, 6))


def _flip(j, k):
    for bit in (4, 2, 1):
        if k & bit:
            j = j + bit - 2 * (j & bit)
    return j


def _copies(src, land, send_sems, recv_sems, hops):
    x, y, c = lax.axis_index("x"), lax.axis_index("y"), lax.axis_index("c")
    me = 4 * x + 2 * y + c
    out = []
    for t in range(len(src)):
        for i, (k, b) in enumerate(hops):
            pos = (1 - x if k & 4 else x, 1 - y if k & 2 else y, 1 - c if k & 1 else c)
            peer = _flip(me, k)
            sem = t * len(hops) + i
            mk = functools.partial(pltpu.make_async_remote_copy, send_sem=send_sems.at[sem], recv_sem=recv_sems.at[sem],
                                   device_id=pos, device_id_type=pl.DeviceIdType.MESH)
            if land[t] is None and src[t].shape[0] != N_DEV:
                width = src[t].shape[1] // N_DEV
                slab = lambda j: src[t].at[:, pl.ds(pl.multiple_of(j * width, 128), width)]
                mine = functools.partial(mk, src_ref=slab(_flip(me, b)), dst_ref=slab(_flip(me, b)))
                theirs = functools.partial(mk, src_ref=slab(_flip(peer, b)), dst_ref=slab(_flip(peer, b)))
            elif land[t] is None:
                mine = functools.partial(mk, src_ref=src[t].at[_flip(me, b)], dst_ref=src[t].at[_flip(me, b)])
                theirs = functools.partial(mk, src_ref=src[t].at[_flip(peer, b)], dst_ref=src[t].at[_flip(peer, b)])
            else:
                assert b == 0
                mine = functools.partial(mk, src_ref=src[t].at[peer], dst_ref=land[t].at[me])
                theirs = functools.partial(mk, src_ref=src[t].at[peer], dst_ref=land[t].at[peer])
            out.append((mine, theirs))
    return out


def _exchange_start(srcs, inplace, peers, name, dep=None):
    n = len(srcs)
    lands = [None if ip else pltpu.with_memory_space_constraint(lax.empty(s.shape, s.dtype), pltpu.HBM)
             for s, ip in zip(srcs, inplace)]
    real_lands = [l for l in lands if l is not None]
    n_l = len(real_lands)
    deps = [] if dep is None else [dep]

    def body(*refs):
        src = refs[:n]
        land_refs = list(refs[n:n + n_l])
        send_sems, recv_sems = refs[n + n_l + len(deps)], refs[n + n_l + len(deps) + 1]
        token = refs[-1]
        land = [None if ip else land_refs.pop(0) for ip in inplace]
        for mine, _ in _copies(src, land, send_sems, recv_sems, peers):
            mine().start()
        token[...] = jnp.zeros_like(token)

    sem_t = pltpu.SemaphoreType.DMA((n * len(peers),))
    outs = pl.pallas_call(
        body, name=name,
        out_shape=(sem_t, sem_t) + tuple(pltpu.HBM(a.shape, a.dtype) for a in list(srcs) + real_lands)
        + (jax.ShapeDtypeStruct((8, 128), F32),),
        in_specs=[_HBM] * (n + n_l) + [pl.BlockSpec(memory_space=pl.ANY)] * len(deps),
        out_specs=(_SEM, _SEM) + (_HBM,) * (n + n_l) + (pl.BlockSpec(memory_space=pltpu.VMEM),),
        input_output_aliases={i: 2 + i for i in range(n + n_l)},
        compiler_params=pltpu.CompilerParams(has_side_effects=pltpu.SideEffectType.DATAFLOW_SIDE_EFFECTING),
    )(*[pltpu.with_memory_space_constraint(s, pltpu.HBM) for s in srcs], *real_lands, *deps)
    handle = dict(send=outs[0], recv=outs[1], srcs=outs[2:2 + n], lands=outs[2 + n:2 + n + n_l], inplace=inplace,
                  peers=peers)
    return handle, outs[-1]


def _exchange_wait(handle, after, name):
    srcs, lands, inplace, peers = handle["srcs"], handle["lands"], handle["inplace"], handle["peers"]
    n, n_l = len(srcs), len(lands)

    def body(*refs):
        src = refs[:n]
        land_refs = list(refs[n:n + n_l])
        send_sems, recv_sems = refs[n + n_l], refs[n + n_l + 1]
        land = [None if ip else land_refs.pop(0) for ip in inplace]
        for mine, theirs in _copies(src, land, send_sems, recv_sems, peers):
            mine().wait_send()
            theirs().wait_recv()

    outs = pl.pallas_call(
        body, name=name, out_shape=tuple(pltpu.HBM(a.shape, a.dtype) for a in list(srcs) + list(lands)),
        in_specs=[_HBM] * (n + n_l) + [_SEM, _SEM, pl.BlockSpec(memory_space=pl.ANY)],
        out_specs=(_HBM,) * (n + n_l), input_output_aliases={i: i for i in range(n + n_l)},
        compiler_params=pltpu.CompilerParams(has_side_effects=pltpu.SideEffectType.DATAFLOW_SIDE_EFFECTING),
    )(*srcs, *lands, handle["send"], handle["recv"], after)
    res, land_out = [], list(outs[n:])
    for t in range(n):
        res.append((outs[t], outs[t] if inplace[t] else land_out.pop(0)))
    return res


def _cast_to_slot(w, me, rows, name, cols=False):
    r, cdim = w.shape

    def body(me_ref, w_ref, o_ref):
        if cols:
            o_ref[...] = w_ref[...].astype(BF16)
        else:
            o_ref[0] = w_ref[...].astype(BF16)

    if cols:
        out_shape = jax.ShapeDtypeStruct((r, N_DEV * cdim), BF16)
        out_spec = pl.BlockSpec((rows, cdim), lambda i, me_ref: (i, me_ref[0]))
    else:
        out_shape = jax.ShapeDtypeStruct((N_DEV, r, cdim), BF16)
        out_spec = pl.BlockSpec((1, rows, cdim), lambda i, me_ref: (me_ref[0], i, 0))
    return pl.pallas_call(
        body, name=name, out_shape=out_shape,
        grid_spec=pltpu.PrefetchScalarGridSpec(
            num_scalar_prefetch=1, grid=(r // rows,), in_specs=[pl.BlockSpec((rows, cdim), lambda i, me_ref: (i, 0))],
            out_specs=out_spec),
        compiler_params=_params("parallel"))(me, w)


def _adamw_math(w, g, m, v):
    m = ADAM_B1 * m + (1.0 - ADAM_B1) * g
    v = ADAM_B2 * v + (1.0 - ADAM_B2) * (g * g)
    m_hat = m / (1.0 - ADAM_B1 ** ADAM_STEP)
    v_hat = v / (1.0 - ADAM_B2 ** ADAM_STEP)
    delta = -ADAM_LR * (m_hat / (jnp.sqrt(v_hat) + ADAM_EPS) + ADAM_WD * w)
    return delta, m, v


def _sum_parts(me, p_ref, own):
    g = None
    for j in range(N_DEV):
        term = (p_ref[j] if own is None else jnp.where(me == j, own, p_ref[j])).astype(F32)
        g = term if g is None else g + term
    return g


def _adamw_reduce(parts, own, me, w, m, v, rows, name):
    r, cdim = w.shape

    def body(me_ref, p_ref, own_ref, w_ref, m_ref, v_ref, g_out, d_out, m_out, v_out):
        g = _sum_parts(me_ref[0], p_ref, own_ref[0])
        d, mn, vn = _adamw_math(w_ref[...], g, m_ref[...], v_ref[...])
        g_out[...] = g
        d_out[...] = d
        m_out[...] = mn
        v_out[...] = vn

    blk = pl.BlockSpec((rows, cdim), lambda i, me_ref: (i, 0))
    sds = jax.ShapeDtypeStruct(w.shape, F32)
    return pl.pallas_call(
        body, name=name, out_shape=(sds,) * 4,
        grid_spec=pltpu.PrefetchScalarGridSpec(
            num_scalar_prefetch=1, grid=(r // rows,),
            in_specs=[pl.BlockSpec((N_DEV, rows, cdim), lambda i, me_ref: (0, i, 0)),
                      pl.BlockSpec((1, rows, cdim), lambda i, me_ref: (me_ref[0], i, 0)), blk, blk, blk],
            out_specs=(blk,) * 4),
        compiler_params=_params("parallel"))(me, parts, own, w, m, v)


def _adamw_small(parts, own, me, w, m, v, mask, name):
    def body(me_ref, *refs):
        refs = list(refs)
        p_ref = refs.pop(0)
        own_ref = None if own is None else refs.pop(0)
        w_ref, m_ref, v_ref = refs[:3]
        k_ref = None if mask is None else refs[3]
        g_out, d_out, m_out, v_out = refs[-4:]
        g = _sum_parts(me_ref[0], p_ref, None if own is None else own_ref[me_ref[0]])
        if mask is not None:
            g = g * k_ref[...]
        d, mn, vn = _adamw_math(w_ref[...], g, m_ref[...], v_ref[...])
        g_out[...] = g
        d_out[...] = d
        m_out[...] = mn
        v_out[...] = vn

    def whole(shape):
        nd = len(shape)
        return pl.BlockSpec(shape, lambda i, me_ref: (0,) * nd)

    sds = jax.ShapeDtypeStruct(w.shape, F32)
    ins = [parts] + ([] if own is None else [own]) + [w, m, v] + ([] if mask is None else [mask])
    return pl.pallas_call(
        body, name=name, out_shape=(sds,) * 4,
        grid_spec=pltpu.PrefetchScalarGridSpec(
            num_scalar_prefetch=1, grid=(1,), in_specs=[whole(a.shape) for a in ins],
            out_specs=(whole(w.shape),) * 4),
        compiler_params=_params("arbitrary"))(me, *ins)


_IN_SPLITS = ((0, 512), (512, 1024), (1024, 1536), (1536, 2560), (2560, IN_PAD))


def _in_proj(x, g1, w_in, tm):
    t_tok = x.shape[0]

    def body(x_ref, g_ref, w_ref, h_ref, *outs):
        xv = x_ref[...]
        r = lax.rsqrt(jnp.mean(xv * xv, axis=-1, keepdims=True) + EPS)
        h = (xv * r * g_ref[...]).astype(BF16)
        h_ref[...] = h
        for (a, b), o_ref in zip(_IN_SPLITS, outs):
            o_ref[...] = _dot(h, w_ref[a:b, :], _NT)

    row = lambda n: pl.BlockSpec((tm, n), lambda i: (i, 0))
    widths = [b - a for a, b in _IN_SPLITS]
    return pl.pallas_call(
        body, name="in_proj", grid=(t_tok // tm,),
        out_shape=(jax.ShapeDtypeStruct((t_tok, D_MODEL), BF16),) + tuple(
            jax.ShapeDtypeStruct((t_tok, n), F32) for n in widths),
        in_specs=[row(D_MODEL), _full((1, D_MODEL)), _full((IN_PAD, D_MODEL))],
        out_specs=(row(D_MODEL),) + tuple(row(n) for n in widths),
        compiler_params=_params("parallel"))(x, g1, w_in)


def _lane_masks():
    lane = lax.broadcasted_iota(jnp.int32, (1, 2 * HEAD_DIM), 1)
    left = (lane < HEAD_DIM).astype(F32)
    return left, 1.0 - left


def _stack_pair(v, m_l, m_r):
    return jnp.concatenate([v * m_l, v * m_r], axis=0).astype(BF16)


def _head_mean(x, avg):
    n = avg.shape[0]
    return jnp.concatenate([_split_dot(x[:, n * i:n * (i + 1)], avg, 2) for i in range(x.shape[1] // n)], axis=1)


def _gmlp_common(u, v, lnw, lnb, avg, wcat_ref, bias, m_l, m_r):
    ug, dug = _gelu_and_grad(u)
    vg, dvg = _gelu_and_grad(v)
    mu = _head_mean(vg, avg)
    vc = vg - mu
    var = _head_mean(vc * vc, avg)
    rstd = lax.rsqrt(var + EPS)
    vhat = vc * rstd
    vn = vhat * lnw + lnb
    rows = []
    for r in range(u.shape[0] // CHUNK):
        cols = []
        for j in range(N_HEADS // 2):
            pair = vn[CHUNK * r:CHUNK * (r + 1), 128 * j:128 * (j + 1)]
            cols.append(_dot(wcat_ref[j], _stack_pair(pair, m_l, m_r)))
        rows.append(jnp.concatenate(cols, axis=1) + bias)
    mixed = jnp.concatenate(rows, axis=0)
    return ug, dug, dvg, rstd, vhat, vn, mixed


_GMLP_ROWS = 4 * CHUNK


def _gmlp_fwd(u, v, lnw, lnb, wcat, bias, avg):
    t_tok = u.shape[0]
    tm = min(_GMLP_ROWS, t_tok)

    def body(u_ref, v_ref, lnw_ref, lnb_ref, wcat_ref, bias_ref, avg_ref, o_ref):
        m_l, m_r = _lane_masks()
        ug, _, _, _, _, _, mixed = _gmlp_common(
            u_ref[...], v_ref[...], lnw_ref[...], lnb_ref[...], avg_ref[...], wcat_ref, bias_ref[...], m_l, m_r)
        o_ref[...] = (ug * mixed).astype(BF16)

    row = pl.BlockSpec((tm, GM_WIDTH), lambda i: (i, 0))
    return pl.pallas_call(
        body, name="gmlp_fwd", grid=(t_tok // tm,), out_shape=jax.ShapeDtypeStruct((t_tok, GM_WIDTH), BF16),
        in_specs=[row, row, _full((1, GM_WIDTH)), _full((1, GM_WIDTH)), _full(wcat.shape), _full(bias.shape),
                  _full(avg.shape)],
        out_specs=row, compiler_params=_params("parallel"))(u, v, lnw, lnb, wcat, bias, avg)


def _shift_rows(x, edge, j, down):
    groups, cols = x.shape[0] // 8, x.shape[1]
    amount = j if down else 8 - j
    rot = pltpu.roll(x.reshape(groups, 8, cols), amount, axis=1)
    edge_rot = pltpu.roll(edge, amount, axis=0)[None]
    sub = lax.broadcasted_iota(jnp.int32, (1, 8, 1), 1)
    if down:
        out = jnp.where(sub < j, jnp.concatenate([edge_rot, rot[:-1]], axis=0), rot)
    else:
        out = jnp.where(sub < 8 - j, rot, jnp.concatenate([rot[1:], edge_rot], axis=0))
    return out.reshape(x.shape)


def _ssd_common(xbc, tail, dtr, cw_ref, cb, dtb, alog, expand, tril):
    q = CHUNK
    taps = [_shift_rows(xbc, tail, 3 - k, True) for k in range(3)] + [xbc]
    pre = cb + cw_ref[0:1, :] * taps[0] + cw_ref[1:2, :] * taps[1] + cw_ref[2:3, :] * taps[2] + cw_ref[3:4, :] * taps[3]
    sg = jax.nn.sigmoid(pre)
    act = pre * sg
    lane = lax.broadcasted_iota(jnp.int32, (1, CHUNK), 1)
    a_row = jnp.where(lane < N_HEADS, -jnp.exp(alog), 0.0)
    dtp = dtr + dtb
    dt = _softplus(dtp)
    a_cs = _split_dot_left(tril, dt * a_row, 3)
    a_cs_t = a_cs.T
    dt_exp = _split_dot(dt, expand, 3)
    a_exp = _split_dot(a_cs, expand, 3)
    a_end = a_exp[q - 1:q, :]
    li = lax.broadcasted_iota(jnp.int32, (q, q), 0)
    si = lax.broadcasted_iota(jnp.int32, (q, q), 1)
    causal = si <= li
    decay = []
    for h in range(N_HEADS):
        seg = a_cs[:, h:h + 1] - a_cs_t[h:h + 1, :]
        decay.append(jnp.where(causal, jnp.exp(jnp.minimum(seg, 0.0)), 0.0))
    return dict(taps=taps, pre=pre, sg=sg, act=act, a_row=a_row, dtp=dtp, dt=dt, dt_exp=dt_exp, a_exp=a_exp,
                e=jnp.exp(a_exp), w_end=jnp.exp(a_end - a_exp), cd=jnp.exp(a_end), decay=decay)


def _ssd_specs(t_tok, seq, reverse):
    nc = seq // CHUNK

    def chunk(b, c):
        return b * nc + (nc - 1 - c if reverse else c)

    def row(n):
        return pl.BlockSpec((CHUNK, n), lambda b, c: (chunk(b, c), 0))

    tail = pl.BlockSpec((8, CONV_CH), lambda b, c: (jnp.maximum(chunk(b, c) * (CHUNK // 8) - 1, 0), 0))
    return nc, chunk, row, tail


def _ssd_fwd(z, xbc, dtr, cw, cb, dtb, alog, dskip_exp, nw, expand, tril, seq):
    t_tok = z.shape[0]
    nc, chunk, row, tail = _ssd_specs(t_tok, seq, False)

    def body(z_ref, xbc_ref, tail_ref, dtr_ref, cw_ref, cb_ref, dtb_ref, alog_ref, dsk_ref, nw_ref, exp_ref,
             tril_ref, o_ref, y_ref, st_ref, state_ref):
        c = pl.program_id(1)

        @pl.when(c == 0)
        def _():
            state_ref[...] = jnp.zeros_like(state_ref)

        m_l, m_r = _lane_masks()
        f = _ssd_common(xbc_ref[...], jnp.where(c == 0, 0.0, tail_ref[...]), dtr_ref[...], cw_ref, cb_ref[...],
                        dtb_ref[...], alog_ref[...], exp_ref[...], tril_ref[...])
        act = f["act"]
        xs = act[:, :SSM_WIDTH]
        xdt = xs * f["dt_exp"]
        xw = xdt * f["w_end"]
        state = state_ref[...]
        st_ref[0] = state
        ydiag, yoff, snew = [], [], []
        for g in range(2):
            bg = act[:, 512 + 128 * g:640 + 128 * g].astype(BF16)
            cg = act[:, 768 + 128 * g:896 + 128 * g].astype(BF16)
            cb_mat = _dot(cg, bg, _NT)
            for pr in range(2):
                h0 = 4 * g + 2 * pr
                gcat = jnp.concatenate(
                    [(cb_mat * f["decay"][h0]).astype(BF16), (cb_mat * f["decay"][h0 + 1]).astype(BF16)], axis=1)
                ydiag.append(_dot(gcat, _stack_pair(xdt[:, 64 * h0:64 * h0 + 128], m_l, m_r)))
            yoff.append(_dot(cg, state[:, 256 * g:256 * (g + 1)].astype(BF16)))
            snew.append(_dot(bg, xw[:, 256 * g:256 * (g + 1)].astype(BF16), _TN))
        y = jnp.concatenate(ydiag, axis=1) + f["e"] * jnp.concatenate(yoff, axis=1) + dsk_ref[...] * xs
        state_ref[...] = state * f["cd"] + jnp.concatenate(snew, axis=1)
        y_ref[...] = y
        zv = z_ref[...]
        yg = y * (zv * jax.nn.sigmoid(zv))
        outs = []
        for g in range(2):
            ygg = yg[:, 256 * g:256 * (g + 1)]
            outs.append(ygg * lax.rsqrt(jnp.mean(ygg * ygg, axis=-1, keepdims=True) + EPS))
        o_ref[...] = (jnp.concatenate(outs, axis=1) * nw_ref[...]).astype(BF16)

    consts = [cw, cb, dtb, alog, dskip_exp, nw, expand, tril]
    return pl.pallas_call(
        body, name="ssd_fwd", grid=(t_tok // seq, nc),
        out_shape=(jax.ShapeDtypeStruct((t_tok, SSM_WIDTH), BF16), jax.ShapeDtypeStruct((t_tok, SSM_WIDTH), F32),
                   jax.ShapeDtypeStruct((t_tok // CHUNK, N_STATE, SSM_WIDTH), F32)),
        in_specs=[row(SSM_WIDTH), row(CONV_CH), tail, row(CHUNK)] + [_full(a.shape) for a in consts],
        out_specs=(row(SSM_WIDTH), row(SSM_WIDTH),
                   pl.BlockSpec((1, N_STATE, SSM_WIDTH), lambda b, c: (chunk(b, c), 0, 0))),
        scratch_shapes=[pltpu.VMEM((N_STATE, SSM_WIDTH), F32)],
        compiler_params=_params("arbitrary", "arbitrary"))(z, xbc, xbc, dtr, *consts)


def _out_proj(mix_a, mix_b, w_out, x, g2, g3, tm, dep=None):
    t_tok = x.shape[0]
    deps = [] if dep is None else [dep]

    def body(a_ref, b_ref, w_ref, x_ref, g2_ref, g3_ref, *rest):
        o_ref, x2_ref, h3_ref, mix_ref = rest[-4:]
        o = _dot(a_ref[...], w_ref[0:GM_WIDTH, :]) + _dot(b_ref[...], w_ref[GM_WIDTH:, :])
        o_ref[...] = o
        mix_ref[:, 0:GM_WIDTH] = a_ref[...]
        mix_ref[:, GM_WIDTH:] = b_ref[...]
        r2 = lax.rsqrt(jnp.mean(o * o, axis=-1, keepdims=True) + EPS)
        x2 = x_ref[...] + o * r2 * g2_ref[...]
        x2_ref[...] = x2
        r3 = lax.rsqrt(jnp.mean(x2 * x2, axis=-1, keepdims=True) + EPS)
        h3_ref[...] = (x2 * r3 * g3_ref[...]).astype(BF16)

    row = lambda n: pl.BlockSpec((tm, n), lambda i: (i, 0))
    sd = lambda dt: jax.ShapeDtypeStruct((t_tok, D_MODEL), dt)
    return pl.pallas_call(
        body, name="out_proj", grid=(t_tok // tm,), out_shape=(sd(F32), sd(F32), sd(BF16), sd(BF16)),
        in_specs=[row(GM_WIDTH), row(SSM_WIDTH), _full((D_MODEL, D_MODEL)), row(D_MODEL), _full((1, D_MODEL)),
                  _full((1, D_MODEL))] + [pl.BlockSpec(memory_space=pl.ANY)] * len(deps),
        out_specs=(row(D_MODEL),) * 4, compiler_params=_params("parallel"))(mix_a, mix_b, w_out, x, g2, g3, *deps)


def _mlp_fwd(h3, w_up, w_down, x2, target, g4, tm, tf):
    t_tok = x2.shape[0]
    nf = D_FF // tf

    def body(h_ref, wu_ref, wd_ref, x2_ref, t_ref, g4_ref, ra_ref, dd_ref, dy_ref, dg4_ref, loss_ref, acc_ref):
        i, j = pl.program_id(0), pl.program_id(1)
        ra = jnp.maximum(_dot(h_ref[...], wu_ref[...]), 0.0).astype(BF16)
        ra_ref[...] = ra
        part = _dot(ra * ra, wd_ref[...])

        @pl.when(j == 0)
        def _():
            acc_ref[...] = part

        @pl.when(j > 0)
        def _():
            acc_ref[...] += part

        @pl.when(j == nf - 1)
        def _():
            dvec = acc_ref[...]
            r4 = lax.rsqrt(jnp.mean(dvec * dvec, axis=-1, keepdims=True) + EPS)
            dn = dvec * r4
            g4 = g4_ref[...]
            err = x2_ref[...] + dn * g4 - t_ref[...]
            dy = err * (1.0 / D_MODEL)
            dy_ref[...] = dy
            dg = dy * g4
            dd_ref[...] = (r4 * (dg - dn * jnp.mean(dg * dn, axis=-1, keepdims=True))).astype(BF16)
            _acc_rows(dg4_ref, _rsum(dy * dn), i == 0)
            tile_loss = 0.5 * jnp.sum(jnp.sum(err * err, axis=-1, keepdims=True), axis=0, keepdims=True) / D_MODEL
            _acc_rows(loss_ref, jnp.broadcast_to(tile_loss, (1, 128)), i == 0)

    row = pl.BlockSpec((tm, D_MODEL), lambda i, j: (i, 0))
    return pl.pallas_call(
        body, name="mlp_fwd", grid=(t_tok // tm, nf),
        out_shape=(jax.ShapeDtypeStruct((t_tok, D_FF), BF16), jax.ShapeDtypeStruct((t_tok, D_MODEL), BF16),
                   jax.ShapeDtypeStruct((t_tok, D_MODEL), F32), jax.ShapeDtypeStruct((1, D_MODEL), F32),
                   jax.ShapeDtypeStruct((1, 128), F32)),
        in_specs=[row, pl.BlockSpec((D_MODEL, tf), lambda i, j: (0, j)),
                  pl.BlockSpec((tf, D_MODEL), lambda i, j: (j, 0)), row, row, _full((1, D_MODEL))],
        out_specs=(pl.BlockSpec((tm, tf), lambda i, j: (i, j)), row, row, _full((1, D_MODEL)), _full((1, 128))),
        scratch_shapes=[pltpu.VMEM((tm, D_MODEL), F32)],
        compiler_params=_params("arbitrary", "arbitrary"))(h3, w_up, w_down, x2, target, g4)


def _mlp_bwd(dd, w_down, ra, w_up, x2, dy, o, g3, g2, tm, tf):
    t_tok = x2.shape[0]
    nf = D_FF // tf

    def body(dd_ref, wd_ref, ra_ref, wu_ref, x2_ref, dy_ref, o_ref, g3_ref, g2_ref, da_ref, dx2_ref, do_ref, dg3_ref,
             dg2_ref, acc_ref):
        i, j = pl.program_id(0), pl.program_id(1)
        df = _dot(dd_ref[...], wd_ref[...], _NT)
        da = (df * (2.0 * ra_ref[...].astype(F32))).astype(BF16)
        da_ref[...] = da
        part = _dot(da, wu_ref[...], _NT)

        @pl.when(j == 0)
        def _():
            acc_ref[...] = part

        @pl.when(j > 0)
        def _():
            acc_ref[...] += part

        @pl.when(j == nf - 1)
        def _():
            dn3, dg3 = _rms_bwd(x2_ref[...], g3_ref[...], acc_ref[...])
            dx2 = dy_ref[...] + dn3
            dx2_ref[...] = dx2
            do, dg2 = _rms_bwd(o_ref[...], g2_ref[...], dx2)
            do_ref[...] = do.astype(BF16)
            _acc_rows(dg3_ref, dg3, i == 0)
            _acc_rows(dg2_ref, dg2, i == 0)

    row = pl.BlockSpec((tm, D_MODEL), lambda i, j: (i, 0))
    vec = _full((1, D_MODEL))
    acc = _full((1, D_MODEL))
    sd = lambda dt: jax.ShapeDtypeStruct((t_tok, D_MODEL), dt)
    return pl.pallas_call(
        body, name="mlp_bwd", grid=(t_tok // tm, nf),
        out_shape=(jax.ShapeDtypeStruct((t_tok, D_FF), BF16), sd(F32), sd(BF16),
                   jax.ShapeDtypeStruct((1, D_MODEL), F32), jax.ShapeDtypeStruct((1, D_MODEL), F32)),
        in_specs=[row, pl.BlockSpec((tf, D_MODEL), lambda i, j: (j, 0)), pl.BlockSpec((tm, tf), lambda i, j: (i, j)),
                  pl.BlockSpec((D_MODEL, tf), lambda i, j: (0, j)), row, row, row, vec, vec],
        out_specs=(pl.BlockSpec((tm, tf), lambda i, j: (i, j)), row, row, acc, acc),
        scratch_shapes=[pltpu.VMEM((tm, D_MODEL), F32)],
        compiler_params=_params("arbitrary", "arbitrary"))(dd, w_down, ra, w_up, x2, dy, o, g3, g2)


def _wgrad(a, b, out_blocks, bm, bn, bk, square_a, name, dep=None):
    t_tok, m = a.shape
    n = b.shape[1]
    nk = t_tok // bk

    def body(a_ref, b_ref, *rest):
        o_ref, acc_ref = rest[-2:]
        k = pl.program_id(2)
        av = a_ref[...]
        if square_a:
            av = av * av
        part = _dot(av, b_ref[...], _TN)

        def emit(res):
            if out_blocks is None:
                o_ref[...] = res.astype(BF16)
            else:
                o_ref[0] = res.astype(BF16)

        if nk == 1:
            emit(part)
            return

        @pl.when(k == 0)
        def _():
            acc_ref[...] = part

        @pl.when(k > 0)
        def _():
            acc_ref[...] += part

        @pl.when(k == nk - 1)
        def _():
            emit(acc_ref[...])

    if out_blocks is None:
        out_shape = jax.ShapeDtypeStruct((m, n), BF16)
        out_spec = pl.BlockSpec((bm, bn), lambda i, j, k: (i, j))
    else:
        assert n // out_blocks == bn
        out_shape = jax.ShapeDtypeStruct((out_blocks, m, bn), BF16)
        out_spec = pl.BlockSpec((1, bm, bn), lambda i, j, k: (j, i, 0))
    deps = [] if dep is None else [dep]
    return pl.pallas_call(
        body, name=name, grid=(m // bm, n // bn, nk), out_shape=out_shape,
        in_specs=[pl.BlockSpec((bk, bm), lambda i, j, k: (k, i)), pl.BlockSpec((bk, bn), lambda i, j, k: (k, j))]
        + [pl.BlockSpec(memory_space=pl.ANY)] * len(deps),
        out_specs=out_spec, scratch_shapes=[pltpu.VMEM((bm, bn) if nk > 1 else (8, 128), F32)],
        compiler_params=_params("parallel", "parallel", "arbitrary"))(a, b, *deps)


def _wgrad_in(h1, pieces, bn, bk, dep=None):
    t_tok = h1.shape[0]
    nk = t_tok // bk
    widths = [b - a for a, b in _IN_SPLITS]

    def body(h_ref, *rest):
        piece_refs = rest[:len(widths)]
        o_ref, acc_ref = rest[-2:]
        k = pl.program_id(1)
        hv = h_ref[...]
        for (a, b), r in zip(_IN_SPLITS, piece_refs):
            part = _dot(r[...], hv, _TN)
            if nk == 1:
                o_ref[a:b, :] = part.astype(BF16)
                continue

            @pl.when(k == 0)
            def _():
                acc_ref[a:b, :] = part

            @pl.when(k > 0)
            def _():
                acc_ref[a:b, :] += part

        if nk > 1:
            @pl.when(k == nk - 1)
            def _():
                o_ref[...] = acc_ref[...].astype(BF16)

    deps = [] if dep is None else [dep]
    return pl.pallas_call(
        body, name="wgrad_in", grid=(D_MODEL // bn, nk), out_shape=jax.ShapeDtypeStruct((IN_PAD, D_MODEL), BF16),
        in_specs=[pl.BlockSpec((bk, bn), lambda j, k: (k, j))] + [pl.BlockSpec((bk, n), lambda j, k: (k, 0)) for n in widths]
        + [pl.BlockSpec(memory_space=pl.ANY)] * len(deps),
        out_specs=pl.BlockSpec((IN_PAD, bn), lambda j, k: (0, j)),
        scratch_shapes=[pltpu.VMEM((IN_PAD, bn) if nk > 1 else (8, 128), F32)],
        compiler_params=_params("parallel", "arbitrary"))(h1, *pieces, *deps)


def _dmix(do, w_out, tm, dep=None):
    t_tok = do.shape[0]

    def body(d_ref, w_ref, *rest):
        rest[-1][...] = _dot(d_ref[...], w_ref[...], _NT)

    row = pl.BlockSpec((tm, D_MODEL), lambda i: (i, 0))
    deps = [] if dep is None else [dep]
    return pl.pallas_call(
        body, name="dmix", grid=(t_tok // tm,), out_shape=jax.ShapeDtypeStruct((t_tok, D_MODEL), F32),
        in_specs=[row, _full((D_MODEL, D_MODEL))] + [pl.BlockSpec(memory_space=pl.ANY)] * len(deps), out_specs=row,
        compiler_params=_params("parallel"))(do, w_out, *deps)


def _gmlp_bwd(dmix, u, v, lnw, lnb, wcat, wtcat, bias, avg, expand_t):
    t_tok = u.shape[0]
    tm = min(_GMLP_ROWS, t_tok)

    def body(dm_ref, u_ref, v_ref, lnw_ref, lnb_ref, wcat_ref, wtcat_ref, bias_ref, avg_ref, expt_ref, du_ref, dv_ref,
             dw_ref, db_ref, dlnw_ref, dlnb_ref):
        i = pl.program_id(0)
        m_l, m_r = _lane_masks()
        avg = avg_ref[...]
        lnw = lnw_ref[...]
        ug, dug, dvg, rstd, vhat, vn, mixed = _gmlp_common(
            u_ref[...], v_ref[...], lnw, lnb_ref[...], avg, wcat_ref, bias_ref[...], m_l, m_r)
        dya = dm_ref[...]
        du_ref[...] = (dya * mixed * dug).astype(BF16)
        dmixed = dya * ug
        dvn_rows, dws, dbt = [], [None] * N_HEADS, None
        for r in range(tm // CHUNK):
            dvn_cols = []
            for j in range(N_HEADS // 2):
                dmp = dmixed[CHUNK * r:CHUNK * (r + 1), 128 * j:128 * (j + 1)]
                dvn_cols.append(_dot(wtcat_ref[j], _stack_pair(dmp, m_l, m_r)))
                vnp = vn[CHUNK * r:CHUNK * (r + 1), 128 * j:128 * (j + 1)].astype(BF16)
                for i_h, mask in enumerate((m_l, m_r)):
                    part = _dot((dmp * mask).astype(BF16), vnp, _NT)
                    dws[2 * j + i_h] = part if r == 0 else dws[2 * j + i_h] + part
            dvn_rows.append(jnp.concatenate(dvn_cols, axis=1))
            part = _split_dot(dmixed[CHUNK * r:CHUNK * (r + 1), :], expt_ref[...], 2)
            dbt = part if r == 0 else dbt + part
        dvn = jnp.concatenate(dvn_rows, axis=0)
        dvh = dvn * lnw
        dvgel = rstd * (dvh - _head_mean(dvh, avg) - vhat * _head_mean(dvh * vhat, avg))
        dv_ref[...] = (dvgel * dvg).astype(BF16)
        first = i == 0

        @pl.when(first)
        def _():
            for h in range(N_HEADS):
                dw_ref[h] = dws[h]
            db_ref[...] = dbt

        @pl.when(jnp.logical_not(first))
        def _():
            for h in range(N_HEADS):
                dw_ref[h] += dws[h]
            db_ref[...] += dbt

        _acc_rows(dlnw_ref, _rsum(dvn * vhat), first)
        _acc_rows(dlnb_ref, _rsum(dvn), first)

    row = pl.BlockSpec((tm, GM_WIDTH), lambda i: (i, 0))
    consts = [lnw, lnb, wcat, wtcat, bias, avg, expand_t]
    return pl.pallas_call(
        body, name="gmlp_bwd", grid=(t_tok // tm,),
        out_shape=(jax.ShapeDtypeStruct((t_tok, GM_WIDTH), BF16), jax.ShapeDtypeStruct((t_tok, GM_WIDTH), BF16),
                   jax.ShapeDtypeStruct((N_HEADS, CHUNK, CHUNK), F32), jax.ShapeDtypeStruct((CHUNK, CHUNK), F32),
                   jax.ShapeDtypeStruct((1, GM_WIDTH), F32), jax.ShapeDtypeStruct((1, GM_WIDTH), F32)),
        in_specs=[row, row, row] + [_full(a.shape) for a in consts],
        out_specs=(row, row, _full((N_HEADS, CHUNK, CHUNK)), _full((CHUNK, CHUNK)), _full((1, GM_WIDTH)),
                   _full((1, GM_WIDTH))),
        compiler_params=_params("arbitrary"))(dmix, u, v, *consts)


def _ssd_bwd(dmix, z, xbc, dtr, y, states, cw, cb, dtb, alog, dskip_exp, nw, expand, expand_t, tril, triu, seq,
             dep=None):
    t_tok = z.shape[0]
    nc, chunk, row, tail = _ssd_specs(t_tok, seq, True)
    q = CHUNK

    def body(dm_ref, z_ref, xbc_ref, tail_ref, dtr_ref, y_ref, st_ref, cw_ref, cb_ref, dtb_ref, alog_ref, dsk_ref,
             nw_ref, exp_ref, expt_ref, tril_ref, triu_ref, dz_ref, dxbc_ref, ddt_ref, dcw_ref, dcb_ref, ddtb_ref,
             dalog_ref, dd_ref, dnw_ref, dhead_ref, dstate_ref):
        b, c = pl.program_id(0), pl.program_id(1)
        first = jnp.logical_and(b == 0, c == 0)

        @pl.when(c == 0)
        def _():
            dstate_ref[...] = jnp.zeros_like(dstate_ref)
            dhead_ref[...] = jnp.zeros_like(dhead_ref)

        m_l, m_r = _lane_masks()
        expt = expt_ref[...]
        f = _ssd_common(xbc_ref[...], jnp.where(c == nc - 1, 0.0, tail_ref[...]), dtr_ref[...], cw_ref, cb_ref[...],
                        dtb_ref[...], alog_ref[...], exp_ref[...], tril_ref[...])
        act, pre, sg = f["act"], f["pre"], f["sg"]
        xs = act[:, :SSM_WIDTH]
        xdt = xs * f["dt_exp"]
        xw = xdt * f["w_end"]
        state = st_ref[0]
        dstate = dstate_ref[...]
        zv, yv, dout, nw = z_ref[...], y_ref[...], dm_ref[...], nw_ref[...]
        sz = jax.nn.sigmoid(zv)
        sl = zv * sz
        yg = yv * sl
        tv = dout * nw
        dyg_parts, ygh_parts = [], []
        for g in range(2):
            ygg = yg[:, 256 * g:256 * (g + 1)]
            rr = lax.rsqrt(jnp.mean(ygg * ygg, axis=-1, keepdims=True) + EPS)
            ygh = ygg * rr
            tg = tv[:, 256 * g:256 * (g + 1)]
            dyg_parts.append(rr * (tg - ygh * jnp.mean(tg * ygh, axis=-1, keepdims=True)))
            ygh_parts.append(ygh)
        dyg = jnp.concatenate(dyg_parts, axis=1)
        dnw = _rsum(dout * jnp.concatenate(ygh_parts, axis=1))
        dy = dyg * sl
        dz_ref[...] = (dyg * yv * (sz * (1.0 + zv * (1.0 - sz)))).astype(BF16)
        ddsk = _rsum(dy * xs)
        dye = dy * f["e"]
        lane = lax.broadcasted_iota(jnp.int32, (q, q), 1)
        sub = lax.broadcasted_iota(jnp.int32, (q, q), 0)
        rs_mat = jnp.zeros((q, q), F32)
        cs_mat = jnp.zeros((q, q), F32)
        dxdt_cols, yoff, dst_in, dxw, d_b, d_c = [], [], [], [], [], []
        for g in range(2):
            bg = act[:, 512 + 128 * g:640 + 128 * g].astype(BF16)
            cg = act[:, 768 + 128 * g:896 + 128 * g].astype(BF16)
            cb_mat = _dot(cg, bg, _NT)
            stg = state[:, 256 * g:256 * (g + 1)].astype(BF16)
            dyeg = dye[:, 256 * g:256 * (g + 1)].astype(BF16)
            yoff.append(_dot(cg, stg))
            dcg = _dot(dyeg, stg, _NT)
            dst_in.append(_dot(cg, dyeg, _TN))
            dcb = jnp.zeros((q, q), F32)
            for pr in range(2):
                h0 = 4 * g + 2 * pr
                gf = [cb_mat * f["decay"][h0], cb_mat * f["decay"][h0 + 1]]
                gcat = jnp.concatenate([gf[0].astype(BF16), gf[1].astype(BF16)], axis=1)
                xst = _stack_pair(xdt[:, 64 * h0:64 * h0 + 128], m_l, m_r)
                dyp = dy[:, 64 * h0:64 * h0 + 128].astype(BF16)
                dgcat = _dot(dyp, xst, _NT)
                dxst = _dot(gcat, dyp, _TN)
                dxdt_cols.append(dxst[:q] * m_l + dxst[q:] * m_r)
                for i in range(2):
                    h = h0 + i
                    dg = dgcat[:, q * i:q * (i + 1)]
                    mm = dg * gf[i]
                    rs_mat = rs_mat + jnp.where(lane == h, jnp.sum(mm, axis=1, keepdims=True), 0.0)
                    cs_mat = cs_mat + jnp.where(sub == h, jnp.sum(mm, axis=0, keepdims=True), 0.0)
                    dcb = dcb + dg * f["decay"][h]
            dcb16 = dcb.astype(BF16)
            dstg = dstate[:, 256 * g:256 * (g + 1)].astype(BF16)
            d_c.append(dcg + _dot(dcb16, bg))
            dxw.append(_dot(bg, dstg))
            d_b.append(_dot(dcb16, cg, _TN) + _dot(xw[:, 256 * g:256 * (g + 1)].astype(BF16), dstg, _NT))
        dxw = jnp.concatenate(dxw, axis=1)
        dxdt = jnp.concatenate(dxdt_cols, axis=1) + dxw * f["w_end"]
        qv = dxw * xw
        end_row = _rsum(qv) + _rsum(dstate * state) * f["cd"]
        x2 = dye * jnp.concatenate(yoff, axis=1) - qv
        row_i = lax.broadcasted_iota(jnp.int32, (q, 1), 0)
        x2 = x2 + jnp.where(row_i == q - 1, end_row, 0.0)
        da_cs = _split_dot(x2, expt, 3) + rs_mat - cs_mat.T
        ddt = _split_dot(dxdt * xs, expt, 3)
        dxs = dsk_ref[...] * dy + dxdt * f["dt_exp"]
        dda = _split_dot_left(triu_ref[...], da_cs, 3)
        ddt = ddt + dda * f["a_row"]
        dalog = _rsum(dda * f["dt"]) * f["a_row"]
        draw = ddt * jax.nn.sigmoid(f["dtp"])
        ddt_ref[...] = draw.astype(BF16)
        dact = jnp.concatenate([dxs] + d_b + d_c, axis=1)
        dpre = dact * (sg * (1.0 + pre * (1.0 - sg)))
        dhead = dhead_ref[...]
        dxbc = cw_ref[3:4, :] * dpre
        for k in range(3):
            dxbc = dxbc + cw_ref[k:k + 1, :] * _shift_rows(dpre, dhead, 3 - k, False)
        dxbc_ref[...] = dxbc.astype(BF16)
        dhead_ref[...] = dpre[0:8, :]
        dstate_ref[...] = dstate * f["cd"] + jnp.concatenate(dst_in, axis=1)
        row8 = lax.broadcasted_iota(jnp.int32, (8, 1), 0)
        dcw = jnp.zeros((8, CONV_CH), F32)
        for k in range(4):
            dcw = dcw + jnp.where(row8 == k, _rsum(dpre * f["taps"][k]), 0.0)

        @pl.when(first)
        def _():
            dcw_ref[...] = dcw

        @pl.when(jnp.logical_not(first))
        def _():
            dcw_ref[...] += dcw

        _acc_rows(dcb_ref, _rsum(dpre), first)
        _acc_rows(ddtb_ref, _rsum(draw), first)
        _acc_rows(dalog_ref, dalog, first)
        _acc_rows(dd_ref, _split_dot(ddsk, expt, 3), first)
        _acc_rows(dnw_ref, dnw, first)

    consts = [cw, cb, dtb, alog, dskip_exp, nw, expand, expand_t, tril, triu]
    deps = [] if dep is None else [dep]
    n_in = 7 + len(consts)

    def body_skipping_dep(*refs):
        body(*refs[:n_in], *refs[n_in + len(deps):])

    acc = lambda n: jax.ShapeDtypeStruct((1, n), F32)
    return pl.pallas_call(
        body_skipping_dep, name="ssd_bwd", grid=(t_tok // seq, nc),
        out_shape=(jax.ShapeDtypeStruct((t_tok, SSM_WIDTH), BF16), jax.ShapeDtypeStruct((t_tok, CONV_CH), BF16),
                   jax.ShapeDtypeStruct((t_tok, CHUNK), BF16), jax.ShapeDtypeStruct((8, CONV_CH), F32), acc(CONV_CH),
                   acc(CHUNK), acc(CHUNK), acc(CHUNK), acc(SSM_WIDTH)),
        in_specs=[pl.BlockSpec((CHUNK, SSM_WIDTH), lambda b, c: (chunk(b, c), 1)), row(SSM_WIDTH), row(CONV_CH), tail,
                  row(CHUNK), row(SSM_WIDTH), pl.BlockSpec((1, N_STATE, SSM_WIDTH), lambda b, c: (chunk(b, c), 0, 0))]
        + [_full(a.shape) for a in consts] + [pl.BlockSpec(memory_space=pl.ANY)] * len(deps),
        out_specs=(row(SSM_WIDTH), row(CONV_CH), row(CHUNK), _full((8, CONV_CH)), _full((1, CONV_CH)),
                   _full((1, CHUNK)), _full((1, CHUNK)), _full((1, CHUNK)), _full((1, SSM_WIDTH))),
        scratch_shapes=[pltpu.VMEM((8, CONV_CH), F32), pltpu.VMEM((N_STATE, SSM_WIDTH), F32)],
        compiler_params=_params("arbitrary", "arbitrary"))(dmix, z, xbc, xbc, dtr, y, states, *consts, *deps)


def _in_bwd(du, dv, dz, dxbc, ddt, w_in, x, dx2, g1, tm, dep=None):
    t_tok = x.shape[0]

    def body(du_ref, dv_ref, dz_ref, dxbc_ref, ddt_ref, w_ref, x_ref, dx2_ref, g_ref, *rest):
        gx_ref, dg_ref = rest[-2:]
        i = pl.program_id(0)
        dh = None
        for (a, b), ref in zip(_IN_SPLITS, (du_ref, dv_ref, dz_ref, dxbc_ref, ddt_ref)):
            part = _dot(ref[...], w_ref[a:b, :])
            dh = part if dh is None else dh + part
        dn, dg = _rms_bwd(x_ref[...], g_ref[...], dh)
        gx_ref[...] = dx2_ref[...] + dn
        _acc_rows(dg_ref, dg, i == 0)

    row = lambda n: pl.BlockSpec((tm, n), lambda i: (i, 0))
    widths = [b - a for a, b in _IN_SPLITS]
    deps = [] if dep is None else [dep]
    return pl.pallas_call(
        body, name="in_bwd", grid=(t_tok // tm,),
        out_shape=(jax.ShapeDtypeStruct((t_tok, D_MODEL), F32), jax.ShapeDtypeStruct((1, D_MODEL), F32)),
        in_specs=[row(n) for n in widths] + [_full((IN_PAD, D_MODEL)), row(D_MODEL), row(D_MODEL), _full((1, D_MODEL))]
        + [pl.BlockSpec(memory_space=pl.ANY)] * len(deps),
        out_specs=(row(D_MODEL), _full((1, D_MODEL))),
        compiler_params=_params("arbitrary"))(du, dv, dz, dxbc, ddt, w_in, x, dx2, g1, *deps)


def _pad_lanes(a, n):
    return jnp.pad(a, ((0, 0), (0, n - a.shape[1])))


def _local_step(x, target, seq, w_in_t, conv_w, small, hooks):
    t_tok = x.shape[0]
    tm = min(512, t_tok)
    avg, expand, expand_t, tril, triu = _const_mats()
    g1, g2, g3, g4 = (small[k].reshape(1, D_MODEL) for k in
                      ("norm_mix_pre", "norm_mix_post", "norm_ffn_pre", "norm_ffn_post"))
    lnw = small["gm_ln_w"].reshape(1, GM_WIDTH)
    lnb = small["gm_ln_b"].reshape(1, GM_WIDTH)
    causal = jnp.tril(jnp.ones((CHUNK, CHUNK), F32))
    wm = small["gm_w_s"] * causal
    pair = lambda w: w.reshape(4, 2, CHUNK, CHUNK).transpose(0, 2, 1, 3).reshape(4, CHUNK, 2 * CHUNK).astype(BF16)
    wcat = pair(wm)
    wtcat = pair(jnp.swapaxes(wm, 1, 2))
    bias = jnp.repeat(small["gm_b_s"].T, HEAD_DIM, axis=1)
    cb = small["conv_b"].reshape(1, CONV_CH)
    dtb = _pad_lanes(small["dt_bias"].reshape(1, N_HEADS), CHUNK)
    alog = _pad_lanes(small["a_log"].reshape(1, N_HEADS), CHUNK)
    dskip_exp = jnp.repeat(small["d_skip"].reshape(1, N_HEADS), HEAD_DIM, axis=1)
    nw = small["ssm_norm_w"].reshape(1, SSM_WIDTH)

    h1, u, v, z, xbc, dtr = _in_proj(x, g1, w_in_t, tm)
    mix_a = _gmlp_fwd(u, v, lnw, lnb, wcat, bias, avg)
    mix_b, y_pre, states = _ssd_fwd(z, xbc, dtr, conv_w, cb, dtb, alog, dskip_exp, nw, expand, tril, seq)
    w_out, dep = hooks["mixers_done"](mix_b)
    o, x2, h3, mix = _out_proj(mix_a, mix_b, w_out, x, g2, g3, tm, dep)
    w_up, w_down = hooks["mlp_weights"](h3)
    tf = 2048
    ra, dd, dy, dg4, loss = _mlp_fwd(h3, w_up, w_down, x2, target, g4, tm, tf)

    da, dx2, do, dg3, dg2 = _mlp_bwd(dd, w_down, ra, w_up, x2, dy, o, g3, g2, tm, tf)
    bk = min(2048, t_tok)
    g_w_down = _wgrad(ra, dd, None, 512, 512, t_tok, True, "wgrad_down")
    g_w_up = _wgrad(h3, da, N_DEV, 512, D_FF // N_DEV, t_tok, False, "wgrad_up")
    dep = hooks["mlp_grads"](g_w_down, g_w_up)
    dmix = _dmix(do, w_out, tm, dep)
    g_w_out = _wgrad(mix, do, None, 512, 512, t_tok, False, "wgrad_out", dep)
    du, dv, dws, dbt, dlnw, dlnb = _gmlp_bwd(dmix, u, v, lnw, lnb, wcat, wtcat, bias, avg, expand_t)
    dep = hooks["gmlp_grads"](g_w_out, dws)
    dz, dxbc, ddt, dcw, dcb, ddtb, dalog, ddsk, dnw = _ssd_bwd(
        dmix, z, xbc, dtr, y_pre, states, conv_w, cb, dtb, alog, dskip_exp, nw, expand, expand_t, tril, triu, seq, dep)
    g_w_in = _wgrad_in(h1, (du, dv, dz, dxbc, ddt), 512, bk, dep)
    dep = hooks["in_grads"](g_w_in, dcw[0:4])
    grad_x, dg1 = _in_bwd(du, dv, dz, dxbc, ddt, w_in_t, x, dx2, g1, tm, dep)

    grads = dict(
        w_in=g_w_in, w_out=g_w_out, w_up=g_w_up, w_down=g_w_down, conv_w=dcw[0:4],
        norm_mix_pre=dg1, norm_mix_post=dg2, norm_ffn_pre=dg3, norm_ffn_post=dg4, gm_ln_w=dlnw, gm_ln_b=dlnb,
        gm_w_s=dws, gm_b_s=dbt, conv_b=dcb, dt_bias=ddtb, a_log=dalog, d_skip=ddsk, ssm_norm_w=dnw)
    return loss[0, 0], grad_x, grads


_WEIGHTS = ("norm_mix_pre", "w_in", "gm_ln_w", "gm_ln_b", "gm_w_s", "gm_b_s", "conv_w", "conv_b", "dt_bias", "a_log",
            "d_skip", "ssm_norm_w", "w_out", "norm_mix_post", "norm_ffn_pre", "w_up", "w_down", "norm_ffn_post")
_SLAB_ROWS = (("norm_mix_pre", 1024), ("norm_mix_post", 1024), ("norm_ffn_pre", 1024), ("norm_ffn_post", 1024),
              ("conv_b", 1024), ("ssm_norm_w", 512), ("gm_ln_w", 512), ("gm_ln_b", 512), ("dt_bias", 8), ("a_log", 8),
              ("d_skip", 8))
_SLAB_LOSS_ROW = len(_SLAB_ROWS)
_SLAB_BS_ROW = 16
_SLAB_HEIGHT = 24
_SMALL_PARAMS = tuple(name for name, _ in _SLAB_ROWS) + ("gm_b_s",)
_LN_PARAMS = ("gm_ln_w", "gm_ln_b")


def _pack_slab(g, loss_part):
    rows = [_pad_lanes(g[name], D_MODEL) for name, _ in _SLAB_ROWS]
    rows.append(jnp.broadcast_to(loss_part, (1, D_MODEL)))
    rows.append(jnp.zeros((_SLAB_BS_ROW - len(rows), D_MODEL), F32))
    rows.append(_pad_lanes(g["gm_b_s"].T[0:N_HEADS], D_MODEL))
    return jnp.concatenate(rows, axis=0)


def _adamw_slab(parts, w, m, v):
    names = _SMALL_PARAMS
    shapes = [w[k].shape for k in names]
    unfold = np.zeros((GM_WIDTH, HEAD_DIM), np.float32)
    for h in range(N_HEADS):
        unfold[h * HEAD_DIM:(h + 1) * HEAD_DIM, :] = np.eye(HEAD_DIM)
    unfold = jnp.asarray(unfold, dtype=BF16)
    n = len(names)

    def body(p_ref, unfold_ref, *refs):
        w_refs, m_refs, v_refs = refs[:n], refs[n:2 * n], refs[2 * n:3 * n]
        outs = refs[3 * n:]
        g_all = p_ref[0]
        for j in range(1, N_DEV):
            g_all = g_all + p_ref[j]
        lane = lax.broadcasted_iota(jnp.int32, (N_HEADS, GM_WIDTH), 1)
        head = lax.broadcasted_iota(jnp.int32, (N_HEADS, GM_WIDTH), 0)
        own_lanes = jnp.logical_and(lane >= head * HEAD_DIM, lane < (head + 1) * HEAD_DIM)
        for i, name in enumerate(names):
            if name == "gm_b_s":
                g = g_all[_SLAB_BS_ROW:_SLAB_BS_ROW + N_HEADS, 0:CHUNK]
            else:
                row = [r for r, (k, _) in enumerate(_SLAB_ROWS) if k == name][0]
                g = g_all[row:row + 1, 0:dict(_SLAB_ROWS)[name]]
                if name in _LN_PARAMS:
                    g = _split_dot(jnp.where(own_lanes, g, 0.0), unfold_ref[...], 3)
            d, mn, vn = _adamw_math(w_refs[i][...], g, m_refs[i][...], v_refs[i][...])
            for o_ref, val in zip(outs[4 * i:4 * i + 4], (g, d, mn, vn)):
                o_ref[...] = val
        outs[-1][...] = g_all[_SLAB_LOSS_ROW:_SLAB_LOSS_ROW + 1, 0:128]

    ins = [parts, unfold] + [d[k] for d in (w, m, v) for k in names]
    out_shape = tuple(jax.ShapeDtypeStruct(s, F32) for s in shapes for _ in range(4)) + (
        jax.ShapeDtypeStruct((1, 128), F32),)
    outs = pl.pallas_call(
        body, name="adamw_small", out_shape=out_shape, grid=(1,), in_specs=[_full(a.shape) for a in ins],
        out_specs=tuple(_full(s.shape) for s in out_shape), compiler_params=_params("arbitrary"))(*ins)
    return {k: tuple(outs[4 * i:4 * i + 4]) for i, k in enumerate(names)}, outs[-1][0, 0]


def kernel(x, norm_mix_pre, w_in, gm_ln_w, gm_ln_b, gm_w_s, gm_b_s, conv_w, conv_b, dt_bias, a_log, d_skip, ssm_norm_w, w_out, norm_mix_post, norm_ffn_pre, w_up, w_down, norm_ffn_post, loss_target, m_norm_mix_pre, m_w_in, m_gm_ln_w, m_gm_ln_b, m_gm_w_s, m_gm_b_s, m_conv_w, m_conv_b, m_dt_bias, m_a_log, m_d_skip, m_ssm_norm_w, m_w_out, m_norm_mix_post, m_norm_ffn_pre, m_w_up, m_w_down, m_norm_ffn_post, v_norm_mix_pre, v_w_in, v_gm_ln_w, v_gm_ln_b, v_gm_w_s, v_gm_b_s, v_conv_w, v_conv_b, v_dt_bias, v_a_log, v_d_skip, v_ssm_norm_w, v_w_out, v_norm_mix_post, v_norm_ffn_pre, v_w_up, v_w_down, v_norm_ffn_post):
    w = dict(norm_mix_pre=norm_mix_pre, w_in=w_in, gm_ln_w=gm_ln_w, gm_ln_b=gm_ln_b, gm_w_s=gm_w_s, gm_b_s=gm_b_s, conv_w=conv_w, conv_b=conv_b, dt_bias=dt_bias, a_log=a_log, d_skip=d_skip, ssm_norm_w=ssm_norm_w, w_out=w_out, norm_mix_post=norm_mix_post, norm_ffn_pre=norm_ffn_pre, w_up=w_up, w_down=w_down, norm_ffn_post=norm_ffn_post)
    m = dict(norm_mix_pre=m_norm_mix_pre, w_in=m_w_in, gm_ln_w=m_gm_ln_w, gm_ln_b=m_gm_ln_b, gm_w_s=m_gm_w_s, gm_b_s=m_gm_b_s, conv_w=m_conv_w, conv_b=m_conv_b, dt_bias=m_dt_bias, a_log=m_a_log, d_skip=m_d_skip, ssm_norm_w=m_ssm_norm_w, w_out=m_w_out, norm_mix_post=m_norm_mix_post, norm_ffn_pre=m_norm_ffn_pre, w_up=m_w_up, w_down=m_w_down, norm_ffn_post=m_norm_ffn_post)
    v = dict(norm_mix_pre=v_norm_mix_pre, w_in=v_w_in, gm_ln_w=v_gm_ln_w, gm_ln_b=v_gm_ln_b, gm_w_s=v_gm_w_s, gm_b_s=v_gm_b_s, conv_w=v_conv_w, conv_b=v_conv_b, dt_bias=v_dt_bias, a_log=v_a_log, d_skip=v_d_skip, ssm_norm_w=v_ssm_norm_w, w_out=v_w_out, norm_mix_post=v_norm_mix_post, norm_ffn_pre=v_norm_ffn_pre, w_up=v_w_up, w_down=v_w_down, norm_ffn_post=v_norm_ffn_post)
    n_batch, seq, _ = x.shape
    shard_in = IN_COLS // N_DEV

    me = (4 * lax.axis_index("x") + 2 * lax.axis_index("y") + lax.axis_index("c")).astype(jnp.int32).reshape(1)

    def in_slot(own):
        return lax.dynamic_update_slice(lax.empty((N_DEV,) + own.shape, own.dtype), own[None],
                                        (me[0],) + (0,) * own.ndim)

    w_in_sh, m_in_sh, v_in_sh = w_in[0].T, m_w_in[0].T, v_w_in[0].T
    first = [_cast_to_slot(w_in_sh, me, shard_in, "cast_w_in"), in_slot(conv_w[0]),
             _cast_to_slot(w_out[0], me, 128, "cast_w_out")]
    ici_1, _ = _exchange_start(first, [True] * 3, _SAME_CORE_PEERS, "gather_mix_ici_start")
    first = [buf for buf, _ in _exchange_wait(ici_1, me, "gather_mix_ici_wait")]
    d2d_1, tok_d2d_1 = _exchange_start(first, [True] * 3, _SIBLING_FORWARD, "gather_mix_d2d_start")
    second = [_cast_to_slot(w_up[0], me, 256, "cast_w_up", cols=True), _cast_to_slot(w_down[0], me, 256, "cast_w_down")]
    ici_2, tok_ici_2 = _exchange_start(second, [True] * 2, _SAME_CORE_PEERS, "gather_mlp_ici_start", dep=tok_d2d_1)
    (_, ag_in), (_, ag_conv), (_, ag_out) = _exchange_wait(d2d_1, tok_ici_2, "gather_mix_d2d_wait")
    w_in_t = jnp.pad(ag_in.reshape(IN_COLS, D_MODEL), ((0, IN_PAD - IN_COLS), (0, 0)))
    conv_w_f = ag_conv.transpose(1, 0, 2).reshape(4, CONV_CH)
    w_out_f = ag_out.reshape(D_MODEL, D_MODEL)
    gathering = {}

    def mixers_done(after):
        bufs = [buf for buf, _ in _exchange_wait(ici_2, after, "gather_mlp_ici_wait")]
        gathering["mlp"], tok = _exchange_start(bufs, [True] * 2, _SIBLING_FORWARD, "gather_mlp_d2d_start")
        return w_out_f, tok

    def mlp_weights(after):
        (_, ag_up), (_, ag_down) = _exchange_wait(gathering["mlp"], after, "gather_mlp_d2d_wait")
        return ag_up, ag_down.reshape(D_FF, D_MODEL)

    sent = {}

    def mlp_grads(g_w_down, g_w_up):
        sent["mlp"], tok = _exchange_start(
            [g_w_down.reshape(N_DEV, D_FF // N_DEV, D_MODEL), g_w_up], [False, False], _ALL_PEERS, "grads_mlp_start")
        return tok

    def gmlp_grads(g_w_out, g_w_s):
        sent["gmlp"], tok = _exchange_start(
            [g_w_out.reshape(N_DEV, D_MODEL // N_DEV, D_MODEL), in_slot(g_w_s.astype(BF16))], [False, True], _ALL_PEERS,
            "grads_gmlp_start")
        return tok

    def in_grads(g_w_in_t, g_conv_w):
        g_in_blk = g_w_in_t[:IN_COLS].reshape(N_DEV, shard_in, D_MODEL)
        g_conv_blk = g_conv_w.reshape(4, N_DEV, CONV_CH // N_DEV).transpose(1, 0, 2)
        sent["in"], tok = _exchange_start([g_in_blk, g_conv_blk], [False, False], _ALL_PEERS, "grads_in_start")
        return tok

    small = {k: w[k][0] for k in _SMALL_PARAMS + ("gm_w_s",)}
    loss_part, grad_x, g = _local_step(
        x.reshape(n_batch * seq, D_MODEL), loss_target.reshape(n_batch * seq, D_MODEL), seq, w_in_t, conv_w_f, small,
        dict(mixers_done=mixers_done, mlp_weights=mlp_weights, mlp_grads=mlp_grads, gmlp_grads=gmlp_grads,
             in_grads=in_grads))

    sent_rows, tok_rows = _exchange_start([in_slot(_pack_slab(g, loss_part))], [True], _ALL_PEERS, "grads_rows_start")
    (own_down, p_down), (own_up, p_up) = _exchange_wait(sent["mlp"], tok_rows, "grads_mlp_wait")
    res = {}
    res["w_up"] = _adamw_reduce(p_up, own_up, me, w_up[0], m_w_up[0], v_w_up[0], 256, "adamw_w_up")
    res["w_down"] = _adamw_reduce(p_down, own_down, me, w_down[0], m_w_down[0], v_w_down[0], 128, "adamw_w_down")
    (own_out, p_out), (_, p_ws) = _exchange_wait(sent["gmlp"], res["w_down"][1], "grads_gmlp_wait")
    res["w_out"] = _adamw_reduce(p_out, own_out, me, w_out[0], m_w_out[0], v_w_out[0], 128, "adamw_w_out")
    causal = jnp.tril(jnp.ones((1, CHUNK, CHUNK), F32))
    res["gm_w_s"] = _adamw_small(p_ws, None, me, gm_w_s[0], m_gm_w_s[0], v_gm_w_s[0], causal, "adamw_gm_w_s")
    (own_in, p_in), (own_conv, p_conv) = _exchange_wait(sent["in"], res["gm_w_s"][1], "grads_in_wait")
    res["w_in"] = tuple(r.T for r in _adamw_reduce(p_in, own_in, me, w_in_sh, m_in_sh, v_in_sh, shard_in, "adamw_w_in"))
    res["conv_w"] = _adamw_small(p_conv, own_conv, me, conv_w[0], m_conv_w[0], v_conv_w[0], None, "adamw_conv_w")
    ((_, p_rows),) = _exchange_wait(sent_rows, res["w_in"][1], "grads_rows_wait")
    flat = lambda t: t[0] if t.ndim == 3 else t
    small_res, loss = _adamw_slab(p_rows, *({k: flat(d[k]) for k in _SMALL_PARAMS} for d in (w, m, v)))
    res.update(small_res)
    res = {k: tuple(r.reshape(w[k].shape) for r in res[k]) for k in _WEIGHTS}

    outs = [loss, grad_x.reshape(x.shape)]
    for part in range(4):
        outs.extend(res[k][part] for k in _WEIGHTS)
    return tuple(outs)
```

```python
import functools

import jax
import jax.numpy as jnp
import numpy as np
from jax import lax
from jax.experimental import pallas as pl
from jax.experimental.pallas import tpu as pltpu

F32 = jnp.float32
BF16 = jnp.bfloat16

D_MODEL = 1024
GM_WIDTH = 512
SSM_WIDTH = 512
CONV_CH = 1024
N_HEADS = 8
HEAD_DIM = 64
N_STATE = 128
CHUNK = 128
D_FF = 4096
IN_COLS = 2568
IN_PAD = 2688
N_DEV = 8
EPS = 1e-6
ADAM_LR, ADAM_B1, ADAM_B2, ADAM_EPS, ADAM_WD, ADAM_STEP = 0.001, 0.9, 0.999, 1e-08, 0.01, 10
VMEM_LIMIT_BYTES = 56 * 1024 * 1024
SMALL_ROWS = 16

_NT = (((1,), (1,)), ((), ()))
_TN = (((0,), (0,)), ((), ()))


def _params(*sem):
    return pltpu.CompilerParams(dimension_semantics=sem or None, vmem_limit_bytes=VMEM_LIMIT_BYTES)


def _dot(a, b, dims=None):
    if dims is None:
        return jnp.dot(a, b, preferred_element_type=F32)
    return lax.dot_general(a, b, dims, preferred_element_type=F32)


def _split_terms(x, terms):
    out, rem = [], x
    for i in range(terms):
        hi = rem.astype(BF16)
        out.append(hi)
        if i + 1 < terms:
            rem = rem - hi.astype(F32)
    return out


def _split_dot(x, m, terms):
    acc = None
    for hi in _split_terms(x, terms):
        part = _dot(hi, m)
        acc = part if acc is None else acc + part
    return acc


def _split_dot_left(m, x, terms):
    acc = None
    for hi in _split_terms(x, terms):
        part = _dot(m, hi)
        acc = part if acc is None else acc + part
    return acc


def _gelu_and_grad(x):
    c = 0.7978845608028654
    inner = c * (x + 0.044715 * x * x * x)
    t = jnp.tanh(inner)
    g = 0.5 * x * (1.0 + t)
    dg = 0.5 * (1.0 + t) + 0.5 * x * (1.0 - t * t) * c * (1.0 + 3.0 * 0.044715 * x * x)
    return g, dg


def _softplus(x):
    return jnp.maximum(x, 0.0) + jnp.log(1.0 + jnp.exp(-jnp.abs(x)))


def _rsum(x):
    return jnp.sum(x, axis=0, keepdims=True)


def _acc_rows(ref, part, first):
    val = jnp.broadcast_to(part, ref.shape)

    @pl.when(first)
    def _():
        ref[...] = val

    @pl.when(jnp.logical_not(first))
    def _():
        ref[...] += val


def _rms_bwd(n, g, dout):
    r = lax.rsqrt(jnp.mean(n * n, axis=-1, keepdims=True) + EPS)
    nh = n * r
    dg = dout * g
    dn = r * (dg - nh * jnp.mean(dg * nh, axis=-1, keepdims=True))
    return dn, _rsum(dout * nh)


def _const_mats():
    avg = np.kron(np.eye(4), np.full((HEAD_DIM, HEAD_DIM), 1.0 / HEAD_DIM))
    expand = np.zeros((CHUNK, SSM_WIDTH), np.float32)
    for h in range(N_HEADS):
        expand[h, h * HEAD_DIM:(h + 1) * HEAD_DIM] = 1.0
    tril = np.tril(np.ones((CHUNK, CHUNK), np.float32))
    as_bf16 = lambda a: jnp.asarray(a, dtype=BF16)
    return as_bf16(avg), as_bf16(expand), as_bf16(expand.T), as_bf16(tril), as_bf16(tril.T)


def _full(shape):
    nd = len(shape)
    return pl.BlockSpec(shape, lambda *_: (0,) * nd)


_HBM = pl.BlockSpec(memory_space=pltpu.HBM)
_SEM = pl.BlockSpec(memory_space=pltpu.SEMAPHORE)
_ALL_PEERS = tuple((k, 0) for k in range(1, N_DEV))
_SAME_CORE_PEERS = ((2, 0), (4, 0), (6, 0))
_SIBLING_FORWARD = ((1, 0), (1, 2), (1, 4), (1, 6))


def _flip(j, k):
    for bit in (4, 2, 1):
        if k & bit:
            j = j + bit - 2 * (j & bit)
    return j


def _copies(src, land, send_sems, recv_sems, hops):
    x, y, c = lax.axis_index("x"), lax.axis_index("y"), lax.axis_index("c")
    me = 4 * x + 2 * y + c
    out = []
    for t in range(len(src)):
        for i, (k, b) in enumerate(hops):
            pos = (1 - x if k & 4 else x, 1 - y if k & 2 else y, 1 - c if k & 1 else c)
            peer = _flip(me, k)
            sem = t * len(hops) + i
            mk = functools.partial(pltpu.make_async_remote_copy, send_sem=send_sems.at[sem], recv_sem=recv_sems.at[sem],
                                   device_id=pos, device_id_type=pl.DeviceIdType.MESH)
            if land[t] is None and src[t].shape[0] != N_DEV:
                width = src[t].shape[1] // N_DEV
                slab = lambda j: src[t].at[:, pl.ds(pl.multiple_of(j * width, 128), width)]
                mine = functools.partial(mk, src_ref=slab(_flip(me, b)), dst_ref=slab(_flip(me, b)))
                theirs = functools.partial(mk, src_ref=slab(_flip(peer, b)), dst_ref=slab(_flip(peer, b)))
            elif land[t] is None:
                mine = functools.partial(mk, src_ref=src[t].at[_flip(me, b)], dst_ref=src[t].at[_flip(me, b)])
                theirs = functools.partial(mk, src_ref=src[t].at[_flip(peer, b)], dst_ref=src[t].at[_flip(peer, b)])
            else:
                assert b == 0
                mine = functools.partial(mk, src_ref=src[t].at[peer], dst_ref=land[t].at[me])
                theirs = functools.partial(mk, src_ref=src[t].at[peer], dst_ref=land[t].at[peer])
            out.append((mine, theirs))
    return out


def _exchange_start(srcs, inplace, peers, name, dep=None):
    n = len(srcs)
    lands = [None if ip else pltpu.with_memory_space_constraint(lax.empty(s.shape, s.dtype), pltpu.HBM)
             for s, ip in zip(srcs, inplace)]
    real_lands = [l for l in lands if l is not None]
    n_l = len(real_lands)
    deps = [] if dep is None else [dep]

    def body(*refs):
        src = refs[:n]
        land_refs = list(refs[n:n + n_l])
        send_sems, recv_sems = refs[n + n_l + len(deps)], refs[n + n_l + len(deps) + 1]
        token = refs[-1]
        land = [None if ip else land_refs.pop(0) for ip in inplace]
        for mine, _ in _copies(src, land, send_sems, recv_sems, peers):
            mine().start()
        token[...] = jnp.zeros_like(token)

    sem_t = pltpu.SemaphoreType.DMA((n * len(peers),))
    outs = pl.pallas_call(
        body, name=name,
        out_shape=(sem_t, sem_t) + tuple(pltpu.HBM(a.shape, a.dtype) for a in list(srcs) + real_lands)
        + (jax.ShapeDtypeStruct((8, 128), F32),),
        in_specs=[_HBM] * (n + n_l) + [pl.BlockSpec(memory_space=pl.ANY)] * len(deps),
        out_specs=(_SEM, _SEM) + (_HBM,) * (n + n_l) + (pl.BlockSpec(memory_space=pltpu.VMEM),),
        input_output_aliases={i: 2 + i for i in range(n + n_l)},
        compiler_params=pltpu.CompilerParams(has_side_effects=pltpu.SideEffectType.DATAFLOW_SIDE_EFFECTING),
    )(*[pltpu.with_memory_space_constraint(s, pltpu.HBM) for s in srcs], *real_lands, *deps)
    handle = dict(send=outs[0], recv=outs[1], srcs=outs[2:2 + n], lands=outs[2 + n:2 + n + n_l], inplace=inplace,
                  peers=peers)
    return handle, outs[-1]


def _exchange_wait(handle, after, name):
    srcs, lands, inplace, peers = handle["srcs"], handle["lands"], handle["inplace"], handle["peers"]
    n, n_l = len(srcs), len(lands)

    def body(*refs):
        src = refs[:n]
        land_refs = list(refs[n:n + n_l])
        send_sems, recv_sems = refs[n + n_l], refs[n + n_l + 1]
        land = [None if ip else land_refs.pop(0) for ip in inplace]
        for mine, theirs in _copies(src, land, send_sems, recv_sems, peers):
            mine().wait_send()
            theirs().wait_recv()

    outs = pl.pallas_call(
        body, name=name, out_shape=tuple(pltpu.HBM(a.shape, a.dtype) for a in list(srcs) + list(lands)),
        in_specs=[_HBM] * (n + n_l) + [_SEM, _SEM, pl.BlockSpec(memory_space=pl.ANY)],
        out_specs=(_HBM,) * (n + n_l), input_output_aliases={i: i for i in range(n + n_l)},
        compiler_params=pltpu.CompilerParams(has_side_effects=pltpu.SideEffectType.DATAFLOW_SIDE_EFFECTING),
    )(*srcs, *lands, handle["send"], handle["recv"], after)
    res, land_out = [], list(outs[n:])
    for t in range(n):
        res.append((outs[t], outs[t] if inplace[t] else land_out.pop(0)))
    return res


def _cast_to_slot(w, me, rows, name, cols=False):
    r, cdim = w.shape

    def body(me_ref, w_ref, o_ref):
        if cols:
            o_ref[...] = w_ref[...].astype(BF16)
        else:
            o_ref[0] = w_ref[...].astype(BF16)

    if cols:
        out_shape = jax.ShapeDtypeStruct((r, N_DEV * cdim), BF16)
        out_spec = pl.BlockSpec((rows, cdim), lambda i, me_ref: (i, me_ref[0]))
    else:
        out_shape = jax.ShapeDtypeStruct((N_DEV, r, cdim), BF16)
        out_spec = pl.BlockSpec((1, rows, cdim), lambda i, me_ref: (me_ref[0], i, 0))
    return pl.pallas_call(
        body, name=name, out_shape=out_shape,
        grid_spec=pltpu.PrefetchScalarGridSpec(
            num_scalar_prefetch=1, grid=(r // rows,), in_specs=[pl.BlockSpec((rows, cdim), lambda i, me_ref: (i, 0))],
            out_specs=out_spec),
        compiler_params=_params("parallel"))(me, w)


def _adamw_math(w, g, m, v):
    m = ADAM_B1 * m + (1.0 - ADAM_B1) * g
    v = ADAM_B2 * v + (1.0 - ADAM_B2) * (g * g)
    m_hat = m / (1.0 - ADAM_B1 ** ADAM_STEP)
    v_hat = v / (1.0 - ADAM_B2 ** ADAM_STEP)
    delta = -ADAM_LR * (m_hat / (jnp.sqrt(v_hat) + ADAM_EPS) + ADAM_WD * w)
    return delta, m, v


def _sum_parts(me, p_ref, own):
    g = None
    for j in range(N_DEV):
        term = (p_ref[j] if own is None else jnp.where(me == j, own, p_ref[j])).astype(F32)
        g = term if g is None else g + term
    return g


def _adamw_reduce(parts, own, me, w, m, v, rows, name):
    r, cdim = w.shape

    def body(me_ref, p_ref, own_ref, w_ref, m_ref, v_ref, g_out, d_out, m_out, v_out):
        g = _sum_parts(me_ref[0], p_ref, own_ref[0])
        d, mn, vn = _adamw_math(w_ref[...], g, m_ref[...], v_ref[...])
        g_out[...] = g
        d_out[...] = d
        m_out[...] = mn
        v_out[...] = vn

    blk = pl.BlockSpec((rows, cdim), lambda i, me_ref: (i, 0))
    sds = jax.ShapeDtypeStruct(w.shape, F32)
    return pl.pallas_call(
        body, name=name, out_shape=(sds,) * 4,
        grid_spec=pltpu.PrefetchScalarGridSpec(
            num_scalar_prefetch=1, grid=(r // rows,),
            in_specs=[pl.BlockSpec((N_DEV, rows, cdim), lambda i, me_ref: (0, i, 0)),
                      pl.BlockSpec((1, rows, cdim), lambda i, me_ref: (me_ref[0], i, 0)), blk, blk, blk],
            out_specs=(blk,) * 4),
        compiler_params=_params("parallel"))(me, parts, own, w, m, v)


def _adamw_small(parts, own, me, w, m, v, mask, name):
    def body(me_ref, *refs):
        refs = list(refs)
        p_ref = refs.pop(0)
        own_ref = None if own is None else refs.pop(0)
        w_ref, m_ref, v_ref = refs[:3]
        k_ref = None if mask is None else refs[3]
        g_out, d_out, m_out, v_out = refs[-4:]
        g = _sum_parts(me_ref[0], p_ref, None if own is None else own_ref[me_ref[0]])
        if mask is not None:
            g = g * k_ref[...]
        d, mn, vn = _adamw_math(w_ref[...], g, m_ref[...], v_ref[...])
        g_out[...] = g
        d_out[...] = d
        m_out[...] = mn
        v_out[...] = vn

    def whole(shape):
        nd = len(shape)
        return pl.BlockSpec(shape, lambda i, me_ref: (0,) * nd)

    sds = jax.ShapeDtypeStruct(w.shape, F32)
    ins = [parts] + ([] if own is None else [own]) + [w, m, v] + ([] if mask is None else [mask])
    return pl.pallas_call(
        body, name=name, out_shape=(sds,) * 4,
        grid_spec=pltpu.PrefetchScalarGridSpec(
            num_scalar_prefetch=1, grid=(1,), in_specs=[whole(a.shape) for a in ins],
            out_specs=(whole(w.shape),) * 4),
        compiler_params=_params("arbitrary"))(me, *ins)


_IN_SPLITS = ((0, 512), (512, 1024), (1024, 1536), (1536, 2560), (2560, IN_PAD))


def _in_proj(x, g1, w_in, tm):
    t_tok = x.shape[0]

    def body(x_ref, g_ref, w_ref, h_ref, *outs):
        xv = x_ref[...]
        r = lax.rsqrt(jnp.mean(xv * xv, axis=-1, keepdims=True) + EPS)
        h = (xv * r * g_ref[...]).astype(BF16)
        h_ref[...] = h
        for (a, b), o_ref in zip(_IN_SPLITS, outs):
            o_ref[...] = _dot(h, w_ref[a:b, :], _NT).astype(o_ref.dtype)

    row = lambda n: pl.BlockSpec((tm, n), lambda i: (i, 0))
    widths = [b - a for a, b in _IN_SPLITS]
    dtypes = (BF16, BF16, BF16, F32, F32)
    return pl.pallas_call(
        body, name="in_proj", grid=(t_tok // tm,),
        out_shape=(jax.ShapeDtypeStruct((t_tok, D_MODEL), BF16),) + tuple(
            jax.ShapeDtypeStruct((t_tok, n), dt) for n, dt in zip(widths, dtypes)),
        in_specs=[row(D_MODEL), _full((1, D_MODEL)), _full((IN_PAD, D_MODEL))],
        out_specs=(row(D_MODEL),) + tuple(row(n) for n in widths),
        compiler_params=_params("parallel"))(x, g1, w_in)


def _lane_masks():
    lane = lax.broadcasted_iota(jnp.int32, (1, 2 * HEAD_DIM), 1)
    left = (lane < HEAD_DIM).astype(F32)
    return left, 1.0 - left


def _stack_pair(v, m_l, m_r):
    return jnp.concatenate([v * m_l, v * m_r], axis=0).astype(BF16)


def _head_mean(x, avg):
    n = avg.shape[0]
    return jnp.concatenate([_split_dot(x[:, n * i:n * (i + 1)], avg, 2) for i in range(x.shape[1] // n)], axis=1)


def _gmlp_common(u, v, lnw, lnb, avg, wcat_ref, bias, m_l, m_r):
    ug, dug = _gelu_and_grad(u)
    vg, dvg = _gelu_and_grad(v)
    mu = _head_mean(vg, avg)
    vc = vg - mu
    var = _head_mean(vc * vc, avg)
    rstd = lax.rsqrt(var + EPS)
    vhat = vc * rstd
    vn = vhat * lnw + lnb
    rows = []
    for r in range(u.shape[0] // CHUNK):
        cols = []
        for j in range(N_HEADS // 2):
            pair = vn[CHUNK * r:CHUNK * (r + 1), 128 * j:128 * (j + 1)]
            cols.append(_dot(wcat_ref[j], _stack_pair(pair, m_l, m_r)))
        rows.append(jnp.concatenate(cols, axis=1) + bias)
    mixed = jnp.concatenate(rows, axis=0)
    return ug, dug, dvg, rstd, vhat, vn, mixed


_GMLP_ROWS = 4 * CHUNK


def _gmlp_fwd(u, v, lnw, lnb, wcat, bias, avg):
    t_tok = u.shape[0]
    tm = min(_GMLP_ROWS, t_tok)

    def body(u_ref, v_ref, lnw_ref, lnb_ref, wcat_ref, bias_ref, avg_ref, o_ref):
        m_l, m_r = _lane_masks()
        ug, _, _, _, _, _, mixed = _gmlp_common(
            u_ref[...].astype(F32), v_ref[...].astype(F32), lnw_ref[...], lnb_ref[...], avg_ref[...], wcat_ref,
            bias_ref[...], m_l, m_r)
        o_ref[...] = (ug * mixed).astype(BF16)

    row = pl.BlockSpec((tm, GM_WIDTH), lambda i: (i, 0))
    return pl.pallas_call(
        body, name="gmlp_fwd", grid=(t_tok // tm,), out_shape=jax.ShapeDtypeStruct((t_tok, GM_WIDTH), BF16),
        in_specs=[row, row, _full((1, GM_WIDTH)), _full((1, GM_WIDTH)), _full(wcat.shape), _full(bias.shape),
                  _full(avg.shape)],
        out_specs=row, compiler_params=_params("parallel"))(u, v, lnw, lnb, wcat, bias, avg)


def _shift_rows(x, edge, j, down):
    groups, cols = x.shape[0] // 8, x.shape[1]
    amount = j if down else 8 - j
    rot = pltpu.roll(x.reshape(groups, 8, cols), amount, axis=1)
    edge_rot = pltpu.roll(edge, amount, axis=0)[None]
    sub = lax.broadcasted_iota(jnp.int32, (1, 8, 1), 1)
    if down:
        out = jnp.where(sub < j, jnp.concatenate([edge_rot, rot[:-1]], axis=0), rot)
    else:
        out = jnp.where(sub < 8 - j, rot, jnp.concatenate([rot[1:], edge_rot], axis=0))
    return out.reshape(x.shape)


def _ssd_common(xbc, tail, dtr, cw_ref, cb, dtb, alog, expand, tril):
    q = CHUNK
    taps = [_shift_rows(xbc, tail, 3 - k, True) for k in range(3)] + [xbc]
    pre = cb + cw_ref[0:1, :] * taps[0] + cw_ref[1:2, :] * taps[1] + cw_ref[2:3, :] * taps[2] + cw_ref[3:4, :] * taps[3]
    sg = jax.nn.sigmoid(pre)
    act = pre * sg
    lane = lax.broadcasted_iota(jnp.int32, (1, CHUNK), 1)
    a_row = jnp.where(lane < N_HEADS, -jnp.exp(alog), 0.0)
    dtp = dtr + dtb
    dt = _softplus(dtp)
    a_cs = _split_dot_left(tril, dt * a_row, 3)
    a_cs_t = a_cs.T
    dt_exp = _split_dot(dt, expand, 3)
    a_exp = _split_dot(a_cs, expand, 3)
    a_end = a_exp[q - 1:q, :]
    li = lax.broadcasted_iota(jnp.int32, (q, q), 0)
    si = lax.broadcasted_iota(jnp.int32, (q, q), 1)
    causal = si <= li
    decay = []
    for h in range(N_HEADS):
        seg = a_cs[:, h:h + 1] - a_cs_t[h:h + 1, :]
        decay.append(jnp.where(causal, jnp.exp(jnp.minimum(seg, 0.0)), 0.0))
    return dict(taps=taps, pre=pre, sg=sg, act=act, a_row=a_row, dtp=dtp, dt=dt, dt_exp=dt_exp, a_exp=a_exp,
                e=jnp.exp(a_exp), w_end=jnp.exp(a_end - a_exp), cd=jnp.exp(a_end), decay=decay)


def _ssd_specs(t_tok, seq, reverse):
    nc = seq // CHUNK

    def chunk(b, c):
        return b * nc + (nc - 1 - c if reverse else c)

    def row(n):
        return pl.BlockSpec((CHUNK, n), lambda b, c: (chunk(b, c), 0))

    tail = pl.BlockSpec((8, CONV_CH), lambda b, c: (jnp.maximum(chunk(b, c) * (CHUNK // 8) - 1, 0), 0))
    return nc, chunk, row, tail


def _ssd_fwd(z, xbc, dtr, cw, cb, dtb, alog, dskip_exp, nw, expand, tril, seq):
    t_tok = z.shape[0]
    nc, chunk, row, tail = _ssd_specs(t_tok, seq, False)

    def body(z_ref, xbc_ref, tail_ref, dtr_ref, cw_ref, cb_ref, dtb_ref, alog_ref, dsk_ref, nw_ref, exp_ref,
             tril_ref, o_ref, y_ref, st_ref, state_ref):
        c = pl.program_id(1)

        @pl.when(c == 0)
        def _():
            state_ref[...] = jnp.zeros_like(state_ref)

        m_l, m_r = _lane_masks()
        f = _ssd_common(xbc_ref[...], jnp.where(c == 0, 0.0, tail_ref[...]), dtr_ref[...], cw_ref, cb_ref[...],
                        dtb_ref[...], alog_ref[...], exp_ref[...], tril_ref[...])
        act = f["act"]
        xs = act[:, :SSM_WIDTH]
        xdt = xs * f["dt_exp"]
        xw = xdt * f["w_end"]
        state = state_ref[...]
        st_ref[0] = state
        ydiag, yoff, snew = [], [], []
        for g in range(2):
            bg = act[:, 512 + 128 * g:640 + 128 * g].astype(BF16)
            cg = act[:, 768 + 128 * g:896 + 128 * g].astype(BF16)
            cb_mat = _dot(cg, bg, _NT)
            for pr in range(2):
                h0 = 4 * g + 2 * pr
                gcat = jnp.concatenate(
                    [(cb_mat * f["decay"][h0]).astype(BF16), (cb_mat * f["decay"][h0 + 1]).astype(BF16)], axis=1)
                ydiag.append(_dot(gcat, _stack_pair(xdt[:, 64 * h0:64 * h0 + 128], m_l, m_r)))
            yoff.append(_dot(cg, state[:, 256 * g:256 * (g + 1)].astype(BF16)))
            snew.append(_dot(bg, xw[:, 256 * g:256 * (g + 1)].astype(BF16), _TN))
        y = jnp.concatenate(ydiag, axis=1) + f["e"] * jnp.concatenate(yoff, axis=1) + dsk_ref[...] * xs
        state_ref[...] = state * f["cd"] + jnp.concatenate(snew, axis=1)
        y_ref[...] = y
        zv = z_ref[...].astype(F32)
        yg = y * (zv * jax.nn.sigmoid(zv))
        outs = []
        for g in range(2):
            ygg = yg[:, 256 * g:256 * (g + 1)]
            outs.append(ygg * lax.rsqrt(jnp.mean(ygg * ygg, axis=-1, keepdims=True) + EPS))
        o_ref[...] = (jnp.concatenate(outs, axis=1) * nw_ref[...]).astype(BF16)

    consts = [cw, cb, dtb, alog, dskip_exp, nw, expand, tril]
    return pl.pallas_call(
        body, name="ssd_fwd", grid=(t_tok // seq, nc),
        out_shape=(jax.ShapeDtypeStruct((t_tok, SSM_WIDTH), BF16), jax.ShapeDtypeStruct((t_tok, SSM_WIDTH), F32),
                   jax.ShapeDtypeStruct((t_tok // CHUNK, N_STATE, SSM_WIDTH), F32)),
        in_specs=[row(SSM_WIDTH), row(CONV_CH), tail, row(CHUNK)] + [_full(a.shape) for a in consts],
        out_specs=(row(SSM_WIDTH), row(SSM_WIDTH),
                   pl.BlockSpec((1, N_STATE, SSM_WIDTH), lambda b, c: (chunk(b, c), 0, 0))),
        scratch_shapes=[pltpu.VMEM((N_STATE, SSM_WIDTH), F32)],
        compiler_params=_params("arbitrary", "arbitrary"))(z, xbc, xbc, dtr, *consts)


def _out_proj(mix_a, mix_b, w_out, x, g2, g3, tm, dep=None):
    t_tok = x.shape[0]
    deps = [] if dep is None else [dep]

    def body(a_ref, b_ref, w_ref, x_ref, g2_ref, g3_ref, *rest):
        o_ref, x2_ref, h3_ref, mix_ref = rest[-4:]
        o = _dot(a_ref[...], w_ref[0:GM_WIDTH, :]) + _dot(b_ref[...], w_ref[GM_WIDTH:, :])
        o_ref[...] = o
        mix_ref[:, 0:GM_WIDTH] = a_ref[...]
        mix_ref[:, GM_WIDTH:] = b_ref[...]
        r2 = lax.rsqrt(jnp.mean(o * o, axis=-1, keepdims=True) + EPS)
        x2 = x_ref[...] + o * r2 * g2_ref[...]
        x2_ref[...] = x2
        r3 = lax.rsqrt(jnp.mean(x2 * x2, axis=-1, keepdims=True) + EPS)
        h3_ref[...] = (x2 * r3 * g3_ref[...]).astype(BF16)

    row = lambda n: pl.BlockSpec((tm, n), lambda i: (i, 0))
    sd = lambda dt: jax.ShapeDtypeStruct((t_tok, D_MODEL), dt)
    return pl.pallas_call(
        body, name="out_proj", grid=(t_tok // tm,), out_shape=(sd(F32), sd(F32), sd(BF16), sd(BF16)),
        in_specs=[row(GM_WIDTH), row(SSM_WIDTH), _full((D_MODEL, D_MODEL)), row(D_MODEL), _full((1, D_MODEL)),
                  _full((1, D_MODEL))] + [pl.BlockSpec(memory_space=pl.ANY)] * len(deps),
        out_specs=(row(D_MODEL),) * 4, compiler_params=_params("parallel"))(mix_a, mix_b, w_out, x, g2, g3, *deps)


def _mlp_fwd(h3, w_up, w_down, x2, target, g4, tm, tf):
    t_tok = x2.shape[0]
    nf = D_FF // tf

    def body(h_ref, wu_ref, wd_ref, x2_ref, t_ref, g4_ref, ra_ref, dd_ref, dy_ref, dg4_ref, loss_ref, acc_ref):
        i, j = pl.program_id(0), pl.program_id(1)
        ra = jnp.maximum(_dot(h_ref[...], wu_ref[...]), 0.0).astype(BF16)
        ra_ref[...] = ra
        part = _dot(ra * ra, wd_ref[...])

        @pl.when(j == 0)
        def _():
            acc_ref[...] = part

        @pl.when(j > 0)
        def _():
            acc_ref[...] += part

        @pl.when(j == nf - 1)
        def _():
            dvec = acc_ref[...]
            r4 = lax.rsqrt(jnp.mean(dvec * dvec, axis=-1, keepdims=True) + EPS)
            dn = dvec * r4
            g4 = g4_ref[...]
            err = x2_ref[...] + dn * g4 - t_ref[...]
            dy = err * (1.0 / D_MODEL)
            dy_ref[...] = dy
            dg = dy * g4
            dd_ref[...] = (r4 * (dg - dn * jnp.mean(dg * dn, axis=-1, keepdims=True))).astype(BF16)
            _acc_rows(dg4_ref, _rsum(dy * dn), i == 0)
            tile_loss = 0.5 * jnp.sum(jnp.sum(err * err, axis=-1, keepdims=True), axis=0, keepdims=True) / D_MODEL
            _acc_rows(loss_ref, jnp.broadcast_to(tile_loss, (1, 128)), i == 0)

    row = pl.BlockSpec((tm, D_MODEL), lambda i, j: (i, 0))
    return pl.pallas_call(
        body, name="mlp_fwd", grid=(t_tok // tm, nf),
        out_shape=(jax.ShapeDtypeStruct((t_tok, D_FF), BF16), jax.ShapeDtypeStruct((t_tok, D_MODEL), BF16),
                   jax.ShapeDtypeStruct((t_tok, D_MODEL), F32), jax.ShapeDtypeStruct((1, D_MODEL), F32),
                   jax.ShapeDtypeStruct((1, 128), F32)),
        in_specs=[row, pl.BlockSpec((D_MODEL, tf), lambda i, j: (0, j)),
                  pl.BlockSpec((tf, D_MODEL), lambda i, j: (j, 0)), row, row, _full((1, D_MODEL))],
        out_specs=(pl.BlockSpec((tm, tf), lambda i, j: (i, j)), row, row, _full((1, D_MODEL)), _full((1, 128))),
        scratch_shapes=[pltpu.VMEM((tm, D_MODEL), F32)],
        compiler_params=_params("arbitrary", "arbitrary"))(h3, w_up, w_down, x2, target, g4)


def _mlp_bwd(dd, w_down, ra, w_up, x2, dy, o, g3, g2, tm, tf):
    t_tok = x2.shape[0]
    nf = D_FF // tf

    def body(dd_ref, wd_ref, ra_ref, wu_ref, x2_ref, dy_ref, o_ref, g3_ref, g2_ref, da_ref, dx2_ref, do_ref, dg3_ref,
             dg2_ref, acc_ref):
        i, j = pl.program_id(0), pl.program_id(1)
        df = _dot(dd_ref[...], wd_ref[...], _NT)
        da = (df * (2.0 * ra_ref[...].astype(F32))).astype(BF16)
        da_ref[...] = da
        part = _dot(da, wu_ref[...], _NT)

        @pl.when(j == 0)
        def _():
            acc_ref[...] = part

        @pl.when(j > 0)
        def _():
            acc_ref[...] += part

        @pl.when(j == nf - 1)
        def _():
            dn3, dg3 = _rms_bwd(x2_ref[...], g3_ref[...], acc_ref[...])
            dx2 = dy_ref[...] + dn3
            dx2_ref[...] = dx2
            do, dg2 = _rms_bwd(o_ref[...], g2_ref[...], dx2)
            do_ref[...] = do.astype(BF16)
            _acc_rows(dg3_ref, dg3, i == 0)
            _acc_rows(dg2_ref, dg2, i == 0)

    row = pl.BlockSpec((tm, D_MODEL), lambda i, j: (i, 0))
    vec = _full((1, D_MODEL))
    acc = _full((1, D_MODEL))
    sd = lambda dt: jax.ShapeDtypeStruct((t_tok, D_MODEL), dt)
    return pl.pallas_call(
        body, name="mlp_bwd", grid=(t_tok // tm, nf),
        out_shape=(jax.ShapeDtypeStruct((t_tok, D_FF), BF16), sd(F32), sd(BF16),
                   jax.ShapeDtypeStruct((1, D_MODEL), F32), jax.ShapeDtypeStruct((1, D_MODEL), F32)),
        in_specs=[row, pl.BlockSpec((tf, D_MODEL), lambda i, j: (j, 0)), pl.BlockSpec((tm, tf), lambda i, j: (i, j)),
                  pl.BlockSpec((D_MODEL, tf), lambda i, j: (0, j)), row, row, row, vec, vec],
        out_specs=(pl.BlockSpec((tm, tf), lambda i, j: (i, j)), row, row, acc, acc),
        scratch_shapes=[pltpu.VMEM((tm, D_MODEL), F32)],
        compiler_params=_params("arbitrary", "arbitrary"))(dd, w_down, ra, w_up, x2, dy, o, g3, g2)


def _wgrad(a, b, out_blocks, bm, bn, bk, square_a, name, dep=None):
    t_tok, m = a.shape
    n = b.shape[1]
    nk = t_tok // bk

    def body(a_ref, b_ref, *rest):
        o_ref, acc_ref = rest[-2:]
        k = pl.program_id(2)
        av = a_ref[...]
        if square_a:
            av = av * av
        part = _dot(av, b_ref[...], _TN)

        def emit(res):
            if out_blocks is None:
                o_ref[...] = res.astype(BF16)
            else:
                o_ref[0] = res.astype(BF16)

        if nk == 1:
            emit(part)
            return

        @pl.when(k == 0)
        def _():
            acc_ref[...] = part

        @pl.when(k > 0)
        def _():
            acc_ref[...] += part

        @pl.when(k == nk - 1)
        def _():
            emit(acc_ref[...])

    if out_blocks is None:
        out_shape = jax.ShapeDtypeStruct((m, n), BF16)
        out_spec = pl.BlockSpec((bm, bn), lambda i, j, k: (i, j))
    else:
        assert n // out_blocks == bn
        out_shape = jax.ShapeDtypeStruct((out_blocks, m, bn), BF16)
        out_spec = pl.BlockSpec((1, bm, bn), lambda i, j, k: (j, i, 0))
    deps = [] if dep is None else [dep]
    return pl.pallas_call(
        body, name=name, grid=(m // bm, n // bn, nk), out_shape=out_shape,
        in_specs=[pl.BlockSpec((bk, bm), lambda i, j, k: (k, i)), pl.BlockSpec((bk, bn), lambda i, j, k: (k, j))]
        + [pl.BlockSpec(memory_space=pl.ANY)] * len(deps),
        out_specs=out_spec, scratch_shapes=[pltpu.VMEM((bm, bn) if nk > 1 else (8, 128), F32)],
        compiler_params=_params("parallel", "parallel", "arbitrary"))(a, b, *deps)


def _wgrad_in(h1, pieces, bn, bk, dep=None):
    t_tok = h1.shape[0]
    nk = t_tok // bk
    widths = [b - a for a, b in _IN_SPLITS]

    def body(h_ref, *rest):
        piece_refs = rest[:len(widths)]
        o_ref, acc_ref = rest[-2:]
        k = pl.program_id(1)
        hv = h_ref[...]
        for (a, b), r in zip(_IN_SPLITS, piece_refs):
            part = _dot(r[...], hv, _TN)
            if nk == 1:
                o_ref[a:b, :] = part.astype(BF16)
                continue

            @pl.when(k == 0)
            def _():
                acc_ref[a:b, :] = part

            @pl.when(k > 0)
            def _():
                acc_ref[a:b, :] += part

        if nk > 1:
            @pl.when(k == nk - 1)
            def _():
                o_ref[...] = acc_ref[...].astype(BF16)

    deps = [] if dep is None else [dep]
    return pl.pallas_call(
        body, name="wgrad_in", grid=(D_MODEL // bn, nk), out_shape=jax.ShapeDtypeStruct((IN_PAD, D_MODEL), BF16),
        in_specs=[pl.BlockSpec((bk, bn), lambda j, k: (k, j))] + [pl.BlockSpec((bk, n), lambda j, k: (k, 0)) for n in widths]
        + [pl.BlockSpec(memory_space=pl.ANY)] * len(deps),
        out_specs=pl.BlockSpec((IN_PAD, bn), lambda j, k: (0, j)),
        scratch_shapes=[pltpu.VMEM((IN_PAD, bn) if nk > 1 else (8, 128), F32)],
        compiler_params=_params("parallel", "arbitrary"))(h1, *pieces, *deps)


def _dmix(do, w_out, tm, dep=None):
    t_tok = do.shape[0]

    def body(d_ref, w_ref, *rest):
        rest[-1][...] = _dot(d_ref[...], w_ref[...], _NT)

    row = pl.BlockSpec((tm, D_MODEL), lambda i: (i, 0))
    deps = [] if dep is None else [dep]
    return pl.pallas_call(
        body, name="dmix", grid=(t_tok // tm,), out_shape=jax.ShapeDtypeStruct((t_tok, D_MODEL), F32),
        in_specs=[row, _full((D_MODEL, D_MODEL))] + [pl.BlockSpec(memory_space=pl.ANY)] * len(deps), out_specs=row,
        compiler_params=_params("parallel"))(do, w_out, *deps)


def _gmlp_bwd(dmix, u, v, lnw, lnb, wcat, wtcat, bias, avg, expand_t):
    t_tok = u.shape[0]
    tm = min(_GMLP_ROWS, t_tok)

    def body(dm_ref, u_ref, v_ref, lnw_ref, lnb_ref, wcat_ref, wtcat_ref, bias_ref, avg_ref, expt_ref, du_ref, dv_ref,
             dw_ref, db_ref, dlnw_ref, dlnb_ref):
        i = pl.program_id(0)
        m_l, m_r = _lane_masks()
        avg = avg_ref[...]
        lnw = lnw_ref[...]
        ug, dug, dvg, rstd, vhat, vn, mixed = _gmlp_common(
            u_ref[...].astype(F32), v_ref[...].astype(F32), lnw, lnb_ref[...], avg, wcat_ref, bias_ref[...], m_l, m_r)
        dya = dm_ref[...]
        du_ref[...] = (dya * mixed * dug).astype(BF16)
        dmixed = dya * ug
        dvn_rows, dws, dbt = [], [None] * N_HEADS, None
        for r in range(tm // CHUNK):
            dvn_cols = []
            for j in range(N_HEADS // 2):
                dmp = dmixed[CHUNK * r:CHUNK * (r + 1), 128 * j:128 * (j + 1)]
                dvn_cols.append(_dot(wtcat_ref[j], _stack_pair(dmp, m_l, m_r)))
                vnp = vn[CHUNK * r:CHUNK * (r + 1), 128 * j:128 * (j + 1)].astype(BF16)
                for i_h, mask in enumerate((m_l, m_r)):
                    part = _dot((dmp * mask).astype(BF16), vnp, _NT)
                    dws[2 * j + i_h] = part if r == 0 else dws[2 * j + i_h] + part
            dvn_rows.append(jnp.concatenate(dvn_cols, axis=1))
            part = _split_dot(dmixed[CHUNK * r:CHUNK * (r + 1), :], expt_ref[...], 2)
            dbt = part if r == 0 else dbt + part
        dvn = jnp.concatenate(dvn_rows, axis=0)
        dvh = dvn * lnw
        dvgel = rstd * (dvh - _head_mean(dvh, avg) - vhat * _head_mean(dvh * vhat, avg))
        dv_ref[...] = (dvgel * dvg).astype(BF16)
        first = i == 0

        @pl.when(first)
        def _():
            for h in range(N_HEADS):
                dw_ref[h] = dws[h]
            db_ref[...] = dbt

        @pl.when(jnp.logical_not(first))
        def _():
            for h in range(N_HEADS):
                dw_ref[h] += dws[h]
            db_ref[...] += dbt

        _acc_rows(dlnw_ref, _rsum(dvn * vhat), first)
        _acc_rows(dlnb_ref, _rsum(dvn), first)

    row = pl.BlockSpec((tm, GM_WIDTH), lambda i: (i, 0))
    consts = [lnw, lnb, wcat, wtcat, bias, avg, expand_t]
    return pl.pallas_call(
        body, name="gmlp_bwd", grid=(t_tok // tm,),
        out_shape=(jax.ShapeDtypeStruct((t_tok, GM_WIDTH), BF16), jax.ShapeDtypeStruct((t_tok, GM_WIDTH), BF16),
                   jax.ShapeDtypeStruct((N_HEADS, CHUNK, CHUNK), F32), jax.ShapeDtypeStruct((CHUNK, CHUNK), F32),
                   jax.ShapeDtypeStruct((1, GM_WIDTH), F32), jax.ShapeDtypeStruct((1, GM_WIDTH), F32)),
        in_specs=[row, row, row] + [_full(a.shape) for a in consts],
        out_specs=(row, row, _full((N_HEADS, CHUNK, CHUNK)), _full((CHUNK, CHUNK)), _full((1, GM_WIDTH)),
                   _full((1, GM_WIDTH))),
        compiler_params=_params("arbitrary"))(dmix, u, v, *consts)


def _ssd_bwd(dmix, z, xbc, dtr, y, states, cw, cb, dtb, alog, dskip_exp, nw, expand, expand_t, tril, triu, seq,
             dep=None):
    t_tok = z.shape[0]
    nc, chunk, row, tail = _ssd_specs(t_tok, seq, True)
    q = CHUNK

    def body(dm_ref, z_ref, xbc_ref, tail_ref, dtr_ref, y_ref, st_ref, cw_ref, cb_ref, dtb_ref, alog_ref, dsk_ref,
             nw_ref, exp_ref, expt_ref, tril_ref, triu_ref, dz_ref, dxbc_ref, ddt_ref, dcw_ref, dcb_ref, ddtb_ref,
             dalog_ref, dd_ref, dnw_ref, dhead_ref, dstate_ref):
        b, c = pl.program_id(0), pl.program_id(1)
        first = jnp.logical_and(b == 0, c == 0)

        @pl.when(c == 0)
        def _():
            dstate_ref[...] = jnp.zeros_like(dstate_ref)
            dhead_ref[...] = jnp.zeros_like(dhead_ref)

        m_l, m_r = _lane_masks()
        expt = expt_ref[...]
        f = _ssd_common(xbc_ref[...], jnp.where(c == nc - 1, 0.0, tail_ref[...]), dtr_ref[...], cw_ref, cb_ref[...],
                        dtb_ref[...], alog_ref[...], exp_ref[...], tril_ref[...])
        act, pre, sg = f["act"], f["pre"], f["sg"]
        xs = act[:, :SSM_WIDTH]
        xdt = xs * f["dt_exp"]
        xw = xdt * f["w_end"]
        state = st_ref[0]
        dstate = dstate_ref[...]
        zv, yv, dout, nw = z_ref[...].astype(F32), y_ref[...], dm_ref[...], nw_ref[...]
        sz = jax.nn.sigmoid(zv)
        sl = zv * sz
        yg = yv * sl
        tv = dout * nw
        dyg_parts, ygh_parts = [], []
        for g in range(2):
            ygg = yg[:, 256 * g:256 * (g + 1)]
            rr = lax.rsqrt(jnp.mean(ygg * ygg, axis=-1, keepdims=True) + EPS)
            ygh = ygg * rr
            tg = tv[:, 256 * g:256 * (g + 1)]
            dyg_parts.append(rr * (tg - ygh * jnp.mean(tg * ygh, axis=-1, keepdims=True)))
            ygh_parts.append(ygh)
        dyg = jnp.concatenate(dyg_parts, axis=1)
        dnw = _rsum(dout * jnp.concatenate(ygh_parts, axis=1))
        dy = dyg * sl
        dz_ref[...] = (dyg * yv * (sz * (1.0 + zv * (1.0 - sz)))).astype(BF16)
        ddsk = _rsum(dy * xs)
        dye = dy * f["e"]
        lane = lax.broadcasted_iota(jnp.int32, (q, q), 1)
        sub = lax.broadcasted_iota(jnp.int32, (q, q), 0)
        rs_mat = jnp.zeros((q, q), F32)
        cs_mat = jnp.zeros((q, q), F32)
        dxdt_cols, yoff, dst_in, dxw, d_b, d_c = [], [], [], [], [], []
        for g in range(2):
            bg = act[:, 512 + 128 * g:640 + 128 * g].astype(BF16)
            cg = act[:, 768 + 128 * g:896 + 128 * g].astype(BF16)
            cb_mat = _dot(cg, bg, _NT)
            stg = state[:, 256 * g:256 * (g + 1)].astype(BF16)
            dyeg = dye[:, 256 * g:256 * (g + 1)].astype(BF16)
            yoff.append(_dot(cg, stg))
            dcg = _dot(dyeg, stg, _NT)
            dst_in.append(_dot(cg, dyeg, _TN))
            dcb = jnp.zeros((q, q), F32)
            for pr in range(2):
                h0 = 4 * g + 2 * pr
                gf = [cb_mat * f["decay"][h0], cb_mat * f["decay"][h0 + 1]]
                gcat = jnp.concatenate([gf[0].astype(BF16), gf[1].astype(BF16)], axis=1)
                xst = _stack_pair(xdt[:, 64 * h0:64 * h0 + 128], m_l, m_r)
                dyp = dy[:, 64 * h0:64 * h0 + 128].astype(BF16)
                dgcat = _dot(dyp, xst, _NT)
                dxst = _dot(gcat, dyp, _TN)
                dxdt_cols.append(dxst[:q] * m_l + dxst[q:] * m_r)
                for i in range(2):
                    h = h0 + i
                    dg = dgcat[:, q * i:q * (i + 1)]
                    mm = dg * gf[i]
                    rs_mat = rs_mat + jnp.where(lane == h, jnp.sum(mm, axis=1, keepdims=True), 0.0)
                    cs_mat = cs_mat + jnp.where(sub == h, jnp.sum(mm, axis=0, keepdims=True), 0.0)
                    dcb = dcb + dg * f["decay"][h]
            dcb16 = dcb.astype(BF16)
            dstg = dstate[:, 256 * g:256 * (g + 1)].astype(BF16)
            d_c.append(dcg + _dot(dcb16, bg))
            dxw.append(_dot(bg, dstg))
            d_b.append(_dot(dcb16, cg, _TN) + _dot(xw[:, 256 * g:256 * (g + 1)].astype(BF16), dstg, _NT))
        dxw = jnp.concatenate(dxw, axis=1)
        dxdt = jnp.concatenate(dxdt_cols, axis=1) + dxw * f["w_end"]
        qv = dxw * xw
        end_row = _rsum(qv) + _rsum(dstate * state) * f["cd"]
        x2 = dye * jnp.concatenate(yoff, axis=1) - qv
        row_i = lax.broadcasted_iota(jnp.int32, (q, 1), 0)
        x2 = x2 + jnp.where(row_i == q - 1, end_row, 0.0)
        da_cs = _split_dot(x2, expt, 3) + rs_mat - cs_mat.T
        ddt = _split_dot(dxdt * xs, expt, 3)
        dxs = dsk_ref[...] * dy + dxdt * f["dt_exp"]
        dda = _split_dot_left(triu_ref[...], da_cs, 3)
        ddt = ddt + dda * f["a_row"]
        dalog = _rsum(dda * f["dt"]) * f["a_row"]
        draw = ddt * jax.nn.sigmoid(f["dtp"])
        ddt_ref[...] = draw.astype(BF16)
        dact = jnp.concatenate([dxs] + d_b + d_c, axis=1)
        dpre = dact * (sg * (1.0 + pre * (1.0 - sg)))
        dhead = dhead_ref[...]
        dxbc = cw_ref[3:4, :] * dpre
        for k in range(3):
            dxbc = dxbc + cw_ref[k:k + 1, :] * _shift_rows(dpre, dhead, 3 - k, False)
        dxbc_ref[...] = dxbc.astype(BF16)
        dhead_ref[...] = dpre[0:8, :]
        dstate_ref[...] = dstate * f["cd"] + jnp.concatenate(dst_in, axis=1)
        row8 = lax.broadcasted_iota(jnp.int32, (8, 1), 0)
        dcw = jnp.zeros((8, CONV_CH), F32)
        for k in range(4):
            dcw = dcw + jnp.where(row8 == k, _rsum(dpre * f["taps"][k]), 0.0)

        @pl.when(first)
        def _():
            dcw_ref[...] = dcw

        @pl.when(jnp.logical_not(first))
        def _():
            dcw_ref[...] += dcw

        _acc_rows(dcb_ref, _rsum(dpre), first)
        _acc_rows(ddtb_ref, _rsum(draw), first)
        _acc_rows(dalog_ref, dalog, first)
        _acc_rows(dd_ref, _split_dot(ddsk, expt, 3), first)
        _acc_rows(dnw_ref, dnw, first)

    consts = [cw, cb, dtb, alog, dskip_exp, nw, expand, expand_t, tril, triu]
    deps = [] if dep is None else [dep]
    n_in = 7 + len(consts)

    def body_skipping_dep(*refs):
        body(*refs[:n_in], *refs[n_in + len(deps):])

    acc = lambda n: jax.ShapeDtypeStruct((1, n), F32)
    return pl.pallas_call(
        body_skipping_dep, name="ssd_bwd", grid=(t_tok // seq, nc),
        out_shape=(jax.ShapeDtypeStruct((t_tok, SSM_WIDTH), BF16), jax.ShapeDtypeStruct((t_tok, CONV_CH), BF16),
                   jax.ShapeDtypeStruct((t_tok, CHUNK), BF16), jax.ShapeDtypeStruct((8, CONV_CH), F32), acc(CONV_CH),
                   acc(CHUNK), acc(CHUNK), acc(CHUNK), acc(SSM_WIDTH)),
        in_specs=[pl.BlockSpec((CHUNK, SSM_WIDTH), lambda b, c: (chunk(b, c), 1)), row(SSM_WIDTH), row(CONV_CH), tail,
                  row(CHUNK), row(SSM_WIDTH), pl.BlockSpec((1, N_STATE, SSM_WIDTH), lambda b, c: (chunk(b, c), 0, 0))]
        + [_full(a.shape) for a in consts] + [pl.BlockSpec(memory_space=pl.ANY)] * len(deps),
        out_specs=(row(SSM_WIDTH), row(CONV_CH), row(CHUNK), _full((8, CONV_CH)), _full((1, CONV_CH)),
                   _full((1, CHUNK)), _full((1, CHUNK)), _full((1, CHUNK)), _full((1, SSM_WIDTH))),
        scratch_shapes=[pltpu.VMEM((8, CONV_CH), F32), pltpu.VMEM((N_STATE, SSM_WIDTH), F32)],
        compiler_params=_params("arbitrary", "arbitrary"))(dmix, z, xbc, xbc, dtr, y, states, *consts, *deps)


def _in_bwd(du, dv, dz, dxbc, ddt, w_in, x, dx2, g1, tm, dep=None):
    t_tok = x.shape[0]

    def body(du_ref, dv_ref, dz_ref, dxbc_ref, ddt_ref, w_ref, x_ref, dx2_ref, g_ref, *rest):
        gx_ref, dg_ref = rest[-2:]
        i = pl.program_id(0)
        dh = None
        for (a, b), ref in zip(_IN_SPLITS, (du_ref, dv_ref, dz_ref, dxbc_ref, ddt_ref)):
            part = _dot(ref[...], w_ref[a:b, :])
            dh = part if dh is None else dh + part
        dn, dg = _rms_bwd(x_ref[...], g_ref[...], dh)
        gx_ref[...] = dx2_ref[...] + dn
        _acc_rows(dg_ref, dg, i == 0)

    row = lambda n: pl.BlockSpec((tm, n), lambda i: (i, 0))
    widths = [b - a for a, b in _IN_SPLITS]
    deps = [] if dep is None else [dep]
    return pl.pallas_call(
        body, name="in_bwd", grid=(t_tok // tm,),
        out_shape=(jax.ShapeDtypeStruct((t_tok, D_MODEL), F32), jax.ShapeDtypeStruct((1, D_MODEL), F32)),
        in_specs=[row(n) for n in widths] + [_full((IN_PAD, D_MODEL)), row(D_MODEL), row(D_MODEL), _full((1, D_MODEL))]
        + [pl.BlockSpec(memory_space=pl.ANY)] * len(deps),
        out_specs=(row(D_MODEL), _full((1, D_MODEL))),
        compiler_params=_params("arbitrary"))(du, dv, dz, dxbc, ddt, w_in, x, dx2, g1, *deps)


def _pad_lanes(a, n):
    return jnp.pad(a, ((0, 0), (0, n - a.shape[1])))


def _local_step(x, target, seq, w_in_t, conv_w, small, hooks):
    t_tok = x.shape[0]
    tm = min(512, t_tok)
    avg, expand, expand_t, tril, triu = _const_mats()
    g1, g2, g3, g4 = (small[k].reshape(1, D_MODEL) for k in
                      ("norm_mix_pre", "norm_mix_post", "norm_ffn_pre", "norm_ffn_post"))
    lnw = small["gm_ln_w"].reshape(1, GM_WIDTH)
    lnb = small["gm_ln_b"].reshape(1, GM_WIDTH)
    causal = jnp.tril(jnp.ones((CHUNK, CHUNK), F32))
    wm = small["gm_w_s"] * causal
    pair = lambda w: w.reshape(4, 2, CHUNK, CHUNK).transpose(0, 2, 1, 3).reshape(4, CHUNK, 2 * CHUNK).astype(BF16)
    wcat = pair(wm)
    wtcat = pair(jnp.swapaxes(wm, 1, 2))
    bias = jnp.repeat(small["gm_b_s"].T, HEAD_DIM, axis=1)
    cb = small["conv_b"].reshape(1, CONV_CH)
    dtb = _pad_lanes(small["dt_bias"].reshape(1, N_HEADS), CHUNK)
    alog = _pad_lanes(small["a_log"].reshape(1, N_HEADS), CHUNK)
    dskip_exp = jnp.repeat(small["d_skip"].reshape(1, N_HEADS), HEAD_DIM, axis=1)
    nw = small["ssm_norm_w"].reshape(1, SSM_WIDTH)

    h1, u, v, z, xbc, dtr = _in_proj(x, g1, w_in_t, tm)
    mix_a = _gmlp_fwd(u, v, lnw, lnb, wcat, bias, avg)
    mix_b, y_pre, states = _ssd_fwd(z, xbc, dtr, conv_w, cb, dtb, alog, dskip_exp, nw, expand, tril, seq)
    w_out, dep = hooks["mixers_done"](mix_b)
    o, x2, h3, mix = _out_proj(mix_a, mix_b, w_out, x, g2, g3, tm, dep)
    w_up, w_down = hooks["mlp_weights"](h3)
    tf = 2048
    ra, dd, dy, dg4, loss = _mlp_fwd(h3, w_up, w_down, x2, target, g4, tm, tf)

    da, dx2, do, dg3, dg2 = _mlp_bwd(dd, w_down, ra, w_up, x2, dy, o, g3, g2, tm, tf)
    bk = min(2048, t_tok)
    g_w_down = _wgrad(ra, dd, None, 512, D_MODEL, t_tok, True, "wgrad_down")
    g_w_up = _wgrad(h3, da, N_DEV, D_MODEL, D_FF // N_DEV, t_tok, False, "wgrad_up")
    dep = hooks["mlp_grads"](g_w_down, g_w_up)
    dmix = _dmix(do, w_out, tm, dep)
    g_w_out = _wgrad(mix, do, None, D_MODEL, 512, t_tok, False, "wgrad_out", dep)
    du, dv, dws, dbt, dlnw, dlnb = _gmlp_bwd(dmix, u, v, lnw, lnb, wcat, wtcat, bias, avg, expand_t)
    dep = hooks["gmlp_grads"](g_w_out, dws)
    dz, dxbc, ddt, dcw, dcb, ddtb, dalog, ddsk, dnw = _ssd_bwd(
        dmix, z, xbc, dtr, y_pre, states, conv_w, cb, dtb, alog, dskip_exp, nw, expand, expand_t, tril, triu, seq, dep)
    g_w_in = _wgrad_in(h1, (du, dv, dz, dxbc, ddt), 512, bk, dep)
    dep = hooks["in_grads"](g_w_in, dcw[0:4])
    grad_x, dg1 = _in_bwd(du, dv, dz, dxbc, ddt, w_in_t, x, dx2, g1, tm, dep)

    grads = dict(
        w_in=g_w_in, w_out=g_w_out, w_up=g_w_up, w_down=g_w_down, conv_w=dcw[0:4],
        norm_mix_pre=dg1, norm_mix_post=dg2, norm_ffn_pre=dg3, norm_ffn_post=dg4, gm_ln_w=dlnw, gm_ln_b=dlnb,
        gm_w_s=dws, gm_b_s=dbt, conv_b=dcb, dt_bias=ddtb, a_log=dalog, d_skip=ddsk, ssm_norm_w=dnw)
    return loss[0, 0], grad_x, grads


_WEIGHTS = ("norm_mix_pre", "w_in", "gm_ln_w", "gm_ln_b", "gm_w_s", "gm_b_s", "conv_w", "conv_b", "dt_bias", "a_log",
            "d_skip", "ssm_norm_w", "w_out", "norm_mix_post", "norm_ffn_pre", "w_up", "w_down", "norm_ffn_post")
_SLAB_ROWS = (("norm_mix_pre", 1024), ("norm_mix_post", 1024), ("norm_ffn_pre", 1024), ("norm_ffn_post", 1024),
              ("conv_b", 1024), ("ssm_norm_w", 512), ("gm_ln_w", 512), ("gm_ln_b", 512), ("dt_bias", 8), ("a_log", 8),
              ("d_skip", 8))
_SLAB_LOSS_ROW = len(_SLAB_ROWS)
_SLAB_BS_ROW = 16
_SLAB_HEIGHT = 24
_SMALL_PARAMS = tuple(name for name, _ in _SLAB_ROWS) + ("gm_b_s",)
_LN_PARAMS = ("gm_ln_w", "gm_ln_b")


def _pack_slab(g, loss_part):
    rows = [_pad_lanes(g[name], D_MODEL) for name, _ in _SLAB_ROWS]
    rows.append(jnp.broadcast_to(loss_part, (1, D_MODEL)))
    rows.append(jnp.zeros((_SLAB_BS_ROW - len(rows), D_MODEL), F32))
    rows.append(_pad_lanes(g["gm_b_s"].T[0:N_HEADS], D_MODEL))
    return jnp.concatenate(rows, axis=0)


def _adamw_slab(parts, w, m, v):
    names = _SMALL_PARAMS
    shapes = [w[k].shape for k in names]
    unfold = np.zeros((GM_WIDTH, HEAD_DIM), np.float32)
    for h in range(N_HEADS):
        unfold[h * HEAD_DIM:(h + 1) * HEAD_DIM, :] = np.eye(HEAD_DIM)
    unfold = jnp.asarray(unfold, dtype=BF16)
    n = len(names)

    def body(p_ref, unfold_ref, *refs):
        w_refs, m_refs, v_refs = refs[:n], refs[n:2 * n], refs[2 * n:3 * n]
        outs = refs[3 * n:]
        g_all = p_ref[0]
        for j in range(1, N_DEV):
            g_all = g_all + p_ref[j]
        lane = lax.broadcasted_iota(jnp.int32, (N_HEADS, GM_WIDTH), 1)
        head = lax.broadcasted_iota(jnp.int32, (N_HEADS, GM_WIDTH), 0)
        own_lanes = jnp.logical_and(lane >= head * HEAD_DIM, lane < (head + 1) * HEAD_DIM)
        for i, name in enumerate(names):
            if name == "gm_b_s":
                g = g_all[_SLAB_BS_ROW:_SLAB_BS_ROW + N_HEADS, 0:CHUNK]
            else:
                row = [r for r, (k, _) in enumerate(_SLAB_ROWS) if k == name][0]
                g = g_all[row:row + 1, 0:dict(_SLAB_ROWS)[name]]
                if name in _LN_PARAMS:
                    g = _split_dot(jnp.where(own_lanes, g, 0.0), unfold_ref[...], 3)
            d, mn, vn = _adamw_math(w_refs[i][...], g, m_refs[i][...], v_refs[i][...])
            for o_ref, val in zip(outs[4 * i:4 * i + 4], (g, d, mn, vn)):
                o_ref[...] = val
        outs[-1][...] = g_all[_SLAB_LOSS_ROW:_SLAB_LOSS_ROW + 1, 0:128]

    ins = [parts, unfold] + [d[k] for d in (w, m, v) for k in names]
    out_shape = tuple(jax.ShapeDtypeStruct(s, F32) for s in shapes for _ in range(4)) + (
        jax.ShapeDtypeStruct((1, 128), F32),)
    outs = pl.pallas_call(
        body, name="adamw_small", out_shape=out_shape, grid=(1,), in_specs=[_full(a.shape) for a in ins],
        out_specs=tuple(_full(s.shape) for s in out_shape), compiler_params=_params("arbitrary"))(*ins)
    return {k: tuple(outs[4 * i:4 * i + 4]) for i, k in enumerate(names)}, outs[-1][0, 0]


def kernel(x, norm_mix_pre, w_in, gm_ln_w, gm_ln_b, gm_w_s, gm_b_s, conv_w, conv_b, dt_bias, a_log, d_skip, ssm_norm_w, w_out, norm_mix_post, norm_ffn_pre, w_up, w_down, norm_ffn_post, loss_target, m_norm_mix_pre, m_w_in, m_gm_ln_w, m_gm_ln_b, m_gm_w_s, m_gm_b_s, m_conv_w, m_conv_b, m_dt_bias, m_a_log, m_d_skip, m_ssm_norm_w, m_w_out, m_norm_mix_post, m_norm_ffn_pre, m_w_up, m_w_down, m_norm_ffn_post, v_norm_mix_pre, v_w_in, v_gm_ln_w, v_gm_ln_b, v_gm_w_s, v_gm_b_s, v_conv_w, v_conv_b, v_dt_bias, v_a_log, v_d_skip, v_ssm_norm_w, v_w_out, v_norm_mix_post, v_norm_ffn_pre, v_w_up, v_w_down, v_norm_ffn_post):
    w = dict(norm_mix_pre=norm_mix_pre, w_in=w_in, gm_ln_w=gm_ln_w, gm_ln_b=gm_ln_b, gm_w_s=gm_w_s, gm_b_s=gm_b_s, conv_w=conv_w, conv_b=conv_b, dt_bias=dt_bias, a_log=a_log, d_skip=d_skip, ssm_norm_w=ssm_norm_w, w_out=w_out, norm_mix_post=norm_mix_post, norm_ffn_pre=norm_ffn_pre, w_up=w_up, w_down=w_down, norm_ffn_post=norm_ffn_post)
    m = dict(norm_mix_pre=m_norm_mix_pre, w_in=m_w_in, gm_ln_w=m_gm_ln_w, gm_ln_b=m_gm_ln_b, gm_w_s=m_gm_w_s, gm_b_s=m_gm_b_s, conv_w=m_conv_w, conv_b=m_conv_b, dt_bias=m_dt_bias, a_log=m_a_log, d_skip=m_d_skip, ssm_norm_w=m_ssm_norm_w, w_out=m_w_out, norm_mix_post=m_norm_mix_post, norm_ffn_pre=m_norm_ffn_pre, w_up=m_w_up, w_down=m_w_down, norm_ffn_post=m_norm_ffn_post)
    v = dict(norm_mix_pre=v_norm_mix_pre, w_in=v_w_in, gm_ln_w=v_gm_ln_w, gm_ln_b=v_gm_ln_b, gm_w_s=v_gm_w_s, gm_b_s=v_gm_b_s, conv_w=v_conv_w, conv_b=v_conv_b, dt_bias=v_dt_bias, a_log=v_a_log, d_skip=v_d_skip, ssm_norm_w=v_ssm_norm_w, w_out=v_w_out, norm_mix_post=v_norm_mix_post, norm_ffn_pre=v_norm_ffn_pre, w_up=v_w_up, w_down=v_w_down, norm_ffn_post=v_norm_ffn_post)
    n_batch, seq, _ = x.shape
    shard_in = IN_COLS // N_DEV

    me = (4 * lax.axis_index("x") + 2 * lax.axis_index("y") + lax.axis_index("c")).astype(jnp.int32).reshape(1)

    def in_slot(own):
        return lax.dynamic_update_slice(lax.empty((N_DEV,) + own.shape, own.dtype), own[None],
                                        (me[0],) + (0,) * own.ndim)

    w_in_sh, m_in_sh, v_in_sh = w_in[0].T, m_w_in[0].T, v_w_in[0].T
    first = [_cast_to_slot(w_in_sh, me, shard_in, "cast_w_in"), in_slot(conv_w[0]),
             _cast_to_slot(w_out[0], me, 128, "cast_w_out")]
    ici_1, _ = _exchange_start(first, [True] * 3, _SAME_CORE_PEERS, "gather_mix_ici_start")
    first = [buf for buf, _ in _exchange_wait(ici_1, me, "gather_mix_ici_wait")]
    d2d_1, tok_d2d_1 = _exchange_start(first, [True] * 3, _SIBLING_FORWARD, "gather_mix_d2d_start")
    second = [_cast_to_slot(w_up[0], me, 256, "cast_w_up", cols=True), _cast_to_slot(w_down[0], me, 256, "cast_w_down")]
    ici_2, tok_ici_2 = _exchange_start(second, [True] * 2, _SAME_CORE_PEERS, "gather_mlp_ici_start", dep=tok_d2d_1)
    (_, ag_in), (_, ag_conv), (_, ag_out) = _exchange_wait(d2d_1, tok_ici_2, "gather_mix_d2d_wait")
    w_in_t = jnp.pad(ag_in.reshape(IN_COLS, D_MODEL), ((0, IN_PAD - IN_COLS), (0, 0)))
    conv_w_f = ag_conv.transpose(1, 0, 2).reshape(4, CONV_CH)
    w_out_f = ag_out.reshape(D_MODEL, D_MODEL)
    gathering = {}

    def mixers_done(after):
        bufs = [buf for buf, _ in _exchange_wait(ici_2, after, "gather_mlp_ici_wait")]
        gathering["mlp"], tok = _exchange_start(bufs, [True] * 2, _SIBLING_FORWARD, "gather_mlp_d2d_start")
        return w_out_f, tok

    def mlp_weights(after):
        (_, ag_up), (_, ag_down) = _exchange_wait(gathering["mlp"], after, "gather_mlp_d2d_wait")
        return ag_up, ag_down.reshape(D_FF, D_MODEL)

    sent = {}

    def mlp_grads(g_w_down, g_w_up):
        sent["mlp"], tok = _exchange_start(
            [g_w_down.reshape(N_DEV, D_FF // N_DEV, D_MODEL), g_w_up], [False, False], _ALL_PEERS, "grads_mlp_start")
        return tok

    def gmlp_grads(g_w_out, g_w_s):
        sent["gmlp"], tok = _exchange_start(
            [g_w_out.reshape(N_DEV, D_MODEL // N_DEV, D_MODEL), in_slot(g_w_s.astype(BF16))], [False, True], _ALL_PEERS,
            "grads_gmlp_start")
        return tok

    def in_grads(g_w_in_t, g_conv_w):
        g_in_blk = g_w_in_t[:IN_COLS].reshape(N_DEV, shard_in, D_MODEL)
        g_conv_blk = g_conv_w.reshape(4, N_DEV, CONV_CH // N_DEV).transpose(1, 0, 2)
        sent["in"], tok = _exchange_start([g_in_blk, g_conv_blk], [False, False], _ALL_PEERS, "grads_in_start")
        return tok

    small = {k: w[k][0] for k in _SMALL_PARAMS + ("gm_w_s",)}
    loss_part, grad_x, g = _local_step(
        x.reshape(n_batch * seq, D_MODEL), loss_target.reshape(n_batch * seq, D_MODEL), seq, w_in_t, conv_w_f, small,
        dict(mixers_done=mixers_done, mlp_weights=mlp_weights, mlp_grads=mlp_grads, gmlp_grads=gmlp_grads,
             in_grads=in_grads))

    sent_rows, tok_rows = _exchange_start([in_slot(_pack_slab(g, loss_part))], [True], _ALL_PEERS, "grads_rows_start")
    (own_down, p_down), (own_up, p_up) = _exchange_wait(sent["mlp"], tok_rows, "grads_mlp_wait")
    res = {}
    res["w_up"] = _adamw_reduce(p_up, own_up, me, w_up[0], m_w_up[0], v_w_up[0], 64, "adamw_w_up")
    res["w_down"] = _adamw_reduce(p_down, own_down, me, w_down[0], m_w_down[0], v_w_down[0], 32, "adamw_w_down")
    (own_out, p_out), (_, p_ws) = _exchange_wait(sent["gmlp"], res["w_down"][1], "grads_gmlp_wait")
    res["w_out"] = _adamw_reduce(p_out, own_out, me, w_out[0], m_w_out[0], v_w_out[0], 32, "adamw_w_out")
    causal = jnp.tril(jnp.ones((1, CHUNK, CHUNK), F32))
    res["gm_w_s"] = _adamw_small(p_ws, None, me, gm_w_s[0], m_gm_w_s[0], v_gm_w_s[0], causal, "adamw_gm_w_s")
    (own_in, p_in), (own_conv, p_conv) = _exchange_wait(sent["in"], res["gm_w_s"][1], "grads_in_wait")
    res["w_in"] = tuple(r.T for r in _adamw_reduce(p_in, own_in, me, w_in_sh, m_in_sh, v_in_sh, shard_in, "adamw_w_in"))
    res["conv_w"] = _adamw_small(p_conv, own_conv, me, conv_w[0], m_conv_w[0], v_conv_w[0], None, "adamw_conv_w")
    ((_, p_rows),) = _exchange_wait(sent_rows, res["w_in"][1], "grads_rows_wait")
    flat = lambda t: t[0] if t.ndim == 3 else t
    small_res, loss = _adamw_slab(p_rows, *({k: flat(d[k]) for k in _SMALL_PARAMS} for d in (w, m, v)))
    res.update(small_res)
    res = {k: tuple(r.reshape(w[k].shape) for r in res[k]) for k in _WEIGHTS}

    outs = [loss, grad_x.reshape(x.shape)]
    for part in range(4):
        outs.extend(res[k][part] for k in _WEIGHTS)
    return tuple(outs)
```

```python
import functools

import jax
import jax.numpy as jnp
import numpy as np
from jax import lax
from jax.experimental import pallas as pl
from jax.experimental.pallas import tpu as pltpu

F32 = jnp.float32
BF16 = jnp.bfloat16

D_MODEL = 1024
GM_WIDTH = 512
SSM_WIDTH = 512
CONV_CH = 1024
N_HEADS = 8
HEAD_DIM = 64
N_STATE = 128
CHUNK = 128
D_FF = 4096
IN_COLS = 2568
IN_PAD = 2688
N_DEV = 8
EPS = 1e-6
ADAM_LR, ADAM_B1, ADAM_B2, ADAM_EPS, ADAM_WD, ADAM_STEP = 0.001, 0.9, 0.999, 1e-08, 0.01, 10
VMEM_LIMIT_BYTES = 56 * 1024 * 1024
SMALL_ROWS = 16

_NT = (((1,), (1,)), ((), ()))
_TN = (((0,), (0,)), ((), ()))


def _params(*sem):
    return pltpu.CompilerParams(dimension_semantics=sem or None, vmem_limit_bytes=VMEM_LIMIT_BYTES)


def _dot(a, b, dims=None):
    if dims is None:
        return jnp.dot(a, b, preferred_element_type=F32)
    return lax.dot_general(a, b, dims, preferred_element_type=F32)


def _split_terms(x, terms):
    out, rem = [], x
    for i in range(terms):
        hi = rem.astype(BF16)
        out.append(hi)
        if i + 1 < terms:
            rem = rem - hi.astype(F32)
    return out


def _split_dot(x, m, terms):
    acc = None
    for hi in _split_terms(x, terms):
        part = _dot(hi, m)
        acc = part if acc is None else acc + part
    return acc


def _split_dot_left(m, x, terms):
    acc = None
    for hi in _split_terms(x, terms):
        part = _dot(m, hi)
        acc = part if acc is None else acc + part
    return acc


def _gelu_and_grad(x):
    c = 0.7978845608028654
    inner = c * (x + 0.044715 * x * x * x)
    t = jnp.tanh(inner)
    g = 0.5 * x * (1.0 + t)
    dg = 0.5 * (1.0 + t) + 0.5 * x * (1.0 - t * t) * c * (1.0 + 3.0 * 0.044715 * x * x)
    return g, dg


def _softplus(x):
    return jnp.maximum(x, 0.0) + jnp.log(1.0 + jnp.exp(-jnp.abs(x)))


def _rsum(x):
    return jnp.sum(x, axis=0, keepdims=True)


def _acc_rows(ref, part, first):
    val = jnp.broadcast_to(part, ref.shape)

    @pl.when(first)
    def _():
        ref[...] = val

    @pl.when(jnp.logical_not(first))
    def _():
        ref[...] += val


def _rms_bwd(n, g, dout):
    r = lax.rsqrt(jnp.mean(n * n, axis=-1, keepdims=True) + EPS)
    nh = n * r
    dg = dout * g
    dn = r * (dg - nh * jnp.mean(dg * nh, axis=-1, keepdims=True))
    return dn, _rsum(dout * nh)


def _const_mats():
    avg = np.kron(np.eye(4), np.full((HEAD_DIM, HEAD_DIM), 1.0 / HEAD_DIM))
    expand = np.zeros((CHUNK, SSM_WIDTH), np.float32)
    for h in range(N_HEADS):
        expand[h, h * HEAD_DIM:(h + 1) * HEAD_DIM] = 1.0
    tril = np.tril(np.ones((CHUNK, CHUNK), np.float32))
    as_bf16 = lambda a: jnp.asarray(a, dtype=BF16)
    return as_bf16(avg), as_bf16(expand), as_bf16(expand.T), as_bf16(tril), as_bf16(tril.T)


def _full(shape):
    nd = len(shape)
    return pl.BlockSpec(shape, lambda *_: (0,) * nd)


_HBM = pl.BlockSpec(memory_space=pltpu.HBM)
_SEM = pl.BlockSpec(memory_space=pltpu.SEMAPHORE)
_ALL_PEERS = tuple((k, 0) for k in range(1, N_DEV))
_SAME_CORE_PEERS = ((2, 0), (4, 0), (6, 0))
_SIBLING_FORWARD = ((1, 0), (1, 2), (1, 4), (1, 6))


def _flip(j, k):
    for bit in (4, 2, 1):
        if k & bit:
            j = j + bit - 2 * (j & bit)
    return j


def _copies(src, land, send_sems, recv_sems, hops):
    x, y, c = lax.axis_index("x"), lax.axis_index("y"), lax.axis_index("c")
    me = 4 * x + 2 * y + c
    out = []
    for t in range(len(src)):
        for i, (k, b) in enumerate(hops):
            pos = (1 - x if k & 4 else x, 1 - y if k & 2 else y, 1 - c if k & 1 else c)
            peer = _flip(me, k)
            sem = t * len(hops) + i
            mk = functools.partial(pltpu.make_async_remote_copy, send_sem=send_sems.at[sem], recv_sem=recv_sems.at[sem],
                                   device_id=pos, device_id_type=pl.DeviceIdType.MESH)
            if land[t] is None and src[t].shape[0] != N_DEV:
                width = src[t].shape[1] // N_DEV
                slab = lambda j: src[t].at[:, pl.ds(pl.multiple_of(j * width, 128), width)]
                mine = functools.partial(mk, src_ref=slab(_flip(me, b)), dst_ref=slab(_flip(me, b)))
                theirs = functools.partial(mk, src_ref=slab(_flip(peer, b)), dst_ref=slab(_flip(peer, b)))
            elif land[t] is None:
                mine = functools.partial(mk, src_ref=src[t].at[_flip(me, b)], dst_ref=src[t].at[_flip(me, b)])
                theirs = functools.partial(mk, src_ref=src[t].at[_flip(peer, b)], dst_ref=src[t].at[_flip(peer, b)])
            else:
                assert b == 0
                mine = functools.partial(mk, src_ref=src[t].at[peer], dst_ref=land[t].at[me])
                theirs = functools.partial(mk, src_ref=src[t].at[peer], dst_ref=land[t].at[peer])
            out.append((mine, theirs))
    return out


def _exchange_start(srcs, inplace, peers, name, dep=None):
    n = len(srcs)
    lands = [None if ip else pltpu.with_memory_space_constraint(lax.empty(s.shape, s.dtype), pltpu.HBM)
             for s, ip in zip(srcs, inplace)]
    real_lands = [l for l in lands if l is not None]
    n_l = len(real_lands)
    deps = [] if dep is None else [dep]

    def body(*refs):
        src = refs[:n]
        land_refs = list(refs[n:n + n_l])
        send_sems, recv_sems = refs[n + n_l + len(deps)], refs[n + n_l + len(deps) + 1]
        token = refs[-1]
        land = [None if ip else land_refs.pop(0) for ip in inplace]
        for mine, _ in _copies(src, land, send_sems, recv_sems, peers):
            mine().start()
        token[...] = jnp.zeros_like(token)

    sem_t = pltpu.SemaphoreType.DMA((n * len(peers),))
    outs = pl.pallas_call(
        body, name=name,
        out_shape=(sem_t, sem_t) + tuple(pltpu.HBM(a.shape, a.dtype) for a in list(srcs) + real_lands)
        + (jax.ShapeDtypeStruct((8, 128), F32),),
        in_specs=[_HBM] * (n + n_l) + [pl.BlockSpec(memory_space=pl.ANY)] * len(deps),
        out_specs=(_SEM, _SEM) + (_HBM,) * (n + n_l) + (pl.BlockSpec(memory_space=pltpu.VMEM),),
        input_output_aliases={i: 2 + i for i in range(n + n_l)},
        compiler_params=pltpu.CompilerParams(has_side_effects=pltpu.SideEffectType.DATAFLOW_SIDE_EFFECTING),
    )(*[pltpu.with_memory_space_constraint(s, pltpu.HBM) for s in srcs], *real_lands, *deps)
    handle = dict(send=outs[0], recv=outs[1], srcs=outs[2:2 + n], lands=outs[2 + n:2 + n + n_l], inplace=inplace,
                  peers=peers)
    return handle, outs[-1]


def _exchange_wait(handle, after, name):
    srcs, lands, inplace, peers = handle["srcs"], handle["lands"], handle["inplace"], handle["peers"]
    n, n_l = len(srcs), len(lands)

    def body(*refs):
        src = refs[:n]
        land_refs = list(refs[n:n + n_l])
        send_sems, recv_sems = refs[n + n_l], refs[n + n_l + 1]
        land = [None if ip else land_refs.pop(0) for ip in inplace]
        for mine, theirs in _copies(src, land, send_sems, recv_sems, peers):
            mine().wait_send()
            theirs().wait_recv()

    outs = pl.pallas_call(
        body, name=name, out_shape=tuple(pltpu.HBM(a.shape, a.dtype) for a in list(srcs) + list(lands)),
        in_specs=[_HBM] * (n + n_l) + [_SEM, _SEM, pl.BlockSpec(memory_space=pl.ANY)],
        out_specs=(_HBM,) * (n + n_l), input_output_aliases={i: i for i in range(n + n_l)},
        compiler_params=pltpu.CompilerParams(has_side_effects=pltpu.SideEffectType.DATAFLOW_SIDE_EFFECTING),
    )(*srcs, *lands, handle["send"], handle["recv"], after)
    res, land_out = [], list(outs[n:])
    for t in range(n):
        res.append((outs[t], outs[t] if inplace[t] else land_out.pop(0)))
    return res


def _cast_to_slot(w, me, rows, name, cols=False):
    r, cdim = w.shape

    def body(me_ref, w_ref, o_ref):
        if cols:
            o_ref[...] = w_ref[...].astype(BF16)
        else:
            o_ref[0] = w_ref[...].astype(BF16)

    if cols:
        out_shape = jax.ShapeDtypeStruct((r, N_DEV * cdim), BF16)
        out_spec = pl.BlockSpec((rows, cdim), lambda i, me_ref: (i, me_ref[0]))
    else:
        out_shape = jax.ShapeDtypeStruct((N_DEV, r, cdim), BF16)
        out_spec = pl.BlockSpec((1, rows, cdim), lambda i, me_ref: (me_ref[0], i, 0))
    return pl.pallas_call(
        body, name=name, out_shape=out_shape,
        grid_spec=pltpu.PrefetchScalarGridSpec(
            num_scalar_prefetch=1, grid=(r // rows,), in_specs=[pl.BlockSpec((rows, cdim), lambda i, me_ref: (i, 0))],
            out_specs=out_spec),
        compiler_params=_params("parallel"))(me, w)


def _adamw_math(w, g, m, v):
    m = ADAM_B1 * m + (1.0 - ADAM_B1) * g
    v = ADAM_B2 * v + (1.0 - ADAM_B2) * (g * g)
    m_hat = m / (1.0 - ADAM_B1 ** ADAM_STEP)
    v_hat = v / (1.0 - ADAM_B2 ** ADAM_STEP)
    delta = -ADAM_LR * (m_hat / (jnp.sqrt(v_hat) + ADAM_EPS) + ADAM_WD * w)
    return delta, m, v


def _sum_parts(me, p_ref, own):
    g = None
    for j in range(N_DEV):
        term = (p_ref[j] if own is None else jnp.where(me == j, own, p_ref[j])).astype(F32)
        g = term if g is None else g + term
    return g


def _adamw_reduce(parts, own, me, w, m, v, rows, name):
    r, cdim = w.shape

    def body(me_ref, p_ref, own_ref, w_ref, m_ref, v_ref, g_out, d_out, m_out, v_out):
        g = _sum_parts(me_ref[0], p_ref, own_ref[0])
        d, mn, vn = _adamw_math(w_ref[...], g, m_ref[...], v_ref[...])
        g_out[...] = g
        d_out[...] = d
        m_out[...] = mn
        v_out[...] = vn

    blk = pl.BlockSpec((rows, cdim), lambda i, me_ref: (i, 0))
    sds = jax.ShapeDtypeStruct(w.shape, F32)
    return pl.pallas_call(
        body, name=name, out_shape=(sds,) * 4,
        grid_spec=pltpu.PrefetchScalarGridSpec(
            num_scalar_prefetch=1, grid=(r // rows,),
            in_specs=[pl.BlockSpec((N_DEV, rows, cdim), lambda i, me_ref: (0, i, 0)),
                      pl.BlockSpec((1, rows, cdim), lambda i, me_ref: (me_ref[0], i, 0)), blk, blk, blk],
            out_specs=(blk,) * 4),
        compiler_params=_params("parallel"))(me, parts, own, w, m, v)


def _adamw_small(parts, own, me, w, m, v, mask, name):
    def body(me_ref, *refs):
        refs = list(refs)
        p_ref = refs.pop(0)
        own_ref = None if own is None else refs.pop(0)
        w_ref, m_ref, v_ref = refs[:3]
        k_ref = None if mask is None else refs[3]
        g_out, d_out, m_out, v_out = refs[-4:]
        g = _sum_parts(me_ref[0], p_ref, None if own is None else own_ref[me_ref[0]])
        if mask is not None:
            g = g * k_ref[...]
        d, mn, vn = _adamw_math(w_ref[...], g, m_ref[...], v_ref[...])
        g_out[...] = g
        d_out[...] = d
        m_out[...] = mn
        v_out[...] = vn

    def whole(shape):
        nd = len(shape)
        return pl.BlockSpec(shape, lambda i, me_ref: (0,) * nd)

    sds = jax.ShapeDtypeStruct(w.shape, F32)
    ins = [parts] + ([] if own is None else [own]) + [w, m, v] + ([] if mask is None else [mask])
    return pl.pallas_call(
        body, name=name, out_shape=(sds,) * 4,
        grid_spec=pltpu.PrefetchScalarGridSpec(
            num_scalar_prefetch=1, grid=(1,), in_specs=[whole(a.shape) for a in ins],
            out_specs=(whole(w.shape),) * 4),
        compiler_params=_params("arbitrary"))(me, *ins)


_IN_SPLITS = ((0, 512), (512, 1024), (1024, 1536), (1536, 2560), (2560, IN_PAD))


def _in_proj(x, g1, w_in, tm):
    t_tok = x.shape[0]

    def body(x_ref, g_ref, w_ref, h_ref, *outs):
        xv = x_ref[...]
        r = lax.rsqrt(jnp.mean(xv * xv, axis=-1, keepdims=True) + EPS)
        h = (xv * r * g_ref[...]).astype(BF16)
        h_ref[...] = h
        for (a, b), o_ref in zip(_IN_SPLITS, outs):
            o_ref[...] = _dot(h, w_ref[a:b, :], _NT).astype(o_ref.dtype)

    row = lambda n: pl.BlockSpec((tm, n), lambda i: (i, 0))
    widths = [b - a for a, b in _IN_SPLITS]
    dtypes = (BF16, BF16, BF16, F32, F32)
    return pl.pallas_call(
        body, name="in_proj", grid=(t_tok // tm,),
        out_shape=(jax.ShapeDtypeStruct((t_tok, D_MODEL), BF16),) + tuple(
            jax.ShapeDtypeStruct((t_tok, n), dt) for n, dt in zip(widths, dtypes)),
        in_specs=[row(D_MODEL), _full((1, D_MODEL)), _full((IN_PAD, D_MODEL))],
        out_specs=(row(D_MODEL),) + tuple(row(n) for n in widths),
        compiler_params=_params("parallel"))(x, g1, w_in)


def _lane_masks():
    lane = lax.broadcasted_iota(jnp.int32, (1, 2 * HEAD_DIM), 1)
    left = (lane < HEAD_DIM).astype(F32)
    return left, 1.0 - left


def _stack_pair(v, m_l, m_r):
    return jnp.concatenate([v * m_l, v * m_r], axis=0).astype(BF16)


def _head_mean(x, avg):
    n = avg.shape[0]
    return jnp.concatenate([_split_dot(x[:, n * i:n * (i + 1)], avg, 2) for i in range(x.shape[1] // n)], axis=1)


def _gmlp_common(u, v, lnw, lnb, avg, wcat_ref, bias, m_l, m_r):
    ug, dug = _gelu_and_grad(u)
    vg, dvg = _gelu_and_grad(v)
    mu = _head_mean(vg, avg)
    vc = vg - mu
    var = _head_mean(vc * vc, avg)
    rstd = lax.rsqrt(var + EPS)
    vhat = vc * rstd
    vn = vhat * lnw + lnb
    rows = []
    for r in range(u.shape[0] // CHUNK):
        cols = []
        for j in range(N_HEADS // 2):
            pair = vn[CHUNK * r:CHUNK * (r + 1), 128 * j:128 * (j + 1)]
            cols.append(_dot(wcat_ref[j], _stack_pair(pair, m_l, m_r)))
        rows.append(jnp.concatenate(cols, axis=1) + bias)
    mixed = jnp.concatenate(rows, axis=0)
    return ug, dug, dvg, rstd, vhat, vn, mixed


_GMLP_ROWS = 4 * CHUNK


def _gmlp_fwd(u, v, lnw, lnb, wcat, bias, avg):
    t_tok = u.shape[0]
    tm = min(_GMLP_ROWS, t_tok)

    def body(u_ref, v_ref, lnw_ref, lnb_ref, wcat_ref, bias_ref, avg_ref, o_ref):
        m_l, m_r = _lane_masks()
        ug, _, _, _, _, _, mixed = _gmlp_common(
            u_ref[...].astype(F32), v_ref[...].astype(F32), lnw_ref[...], lnb_ref[...], avg_ref[...], wcat_ref,
            bias_ref[...], m_l, m_r)
        o_ref[...] = (ug * mixed).astype(BF16)

    row = pl.BlockSpec((tm, GM_WIDTH), lambda i: (i, 0))
    return pl.pallas_call(
        body, name="gmlp_fwd", grid=(t_tok // tm,), out_shape=jax.ShapeDtypeStruct((t_tok, GM_WIDTH), BF16),
        in_specs=[row, row, _full((1, GM_WIDTH)), _full((1, GM_WIDTH)), _full(wcat.shape), _full(bias.shape),
                  _full(avg.shape)],
        out_specs=row, compiler_params=_params("parallel"))(u, v, lnw, lnb, wcat, bias, avg)


def _shift_rows(x, edge, j, down):
    groups, cols = x.shape[0] // 8, x.shape[1]
    amount = j if down else 8 - j
    rot = pltpu.roll(x.reshape(groups, 8, cols), amount, axis=1)
    edge_rot = pltpu.roll(edge, amount, axis=0)[None]
    sub = lax.broadcasted_iota(jnp.int32, (1, 8, 1), 1)
    if down:
        out = jnp.where(sub < j, jnp.concatenate([edge_rot, rot[:-1]], axis=0), rot)
    else:
        out = jnp.where(sub < 8 - j, rot, jnp.concatenate([rot[1:], edge_rot], axis=0))
    return out.reshape(x.shape)


def _ssd_common(xbc, tail, dtr, cw_ref, cb, dtb, alog, expand, tril):
    q = CHUNK
    taps = [_shift_rows(xbc, tail, 3 - k, True) for k in range(3)] + [xbc]
    pre = cb + cw_ref[0:1, :] * taps[0] + cw_ref[1:2, :] * taps[1] + cw_ref[2:3, :] * taps[2] + cw_ref[3:4, :] * taps[3]
    sg = jax.nn.sigmoid(pre)
    act = pre * sg
    lane = lax.broadcasted_iota(jnp.int32, (1, CHUNK), 1)
    a_row = jnp.where(lane < N_HEADS, -jnp.exp(alog), 0.0)
    dtp = dtr + dtb
    dt = _softplus(dtp)
    a_cs = _split_dot_left(tril, dt * a_row, 3)
    a_cs_t = a_cs.T
    dt_exp = _split_dot(dt, expand, 3)
    a_exp = _split_dot(a_cs, expand, 3)
    a_end = a_exp[q - 1:q, :]
    li = lax.broadcasted_iota(jnp.int32, (q, q), 0)
    si = lax.broadcasted_iota(jnp.int32, (q, q), 1)
    causal = si <= li
    decay = []
    for h in range(N_HEADS):
        seg = a_cs[:, h:h + 1] - a_cs_t[h:h + 1, :]
        decay.append(jnp.where(causal, jnp.exp(jnp.minimum(seg, 0.0)), 0.0))
    return dict(taps=taps, pre=pre, sg=sg, act=act, a_row=a_row, dtp=dtp, dt=dt, dt_exp=dt_exp, a_exp=a_exp,
                e=jnp.exp(a_exp), w_end=jnp.exp(a_end - a_exp), cd=jnp.exp(a_end), decay=decay)


def _ssd_specs(t_tok, seq, reverse):
    nc = seq // CHUNK

    def chunk(b, c):
        return b * nc + (nc - 1 - c if reverse else c)

    def row(n):
        return pl.BlockSpec((CHUNK, n), lambda b, c: (chunk(b, c), 0))

    tail = pl.BlockSpec((8, CONV_CH), lambda b, c: (jnp.maximum(chunk(b, c) * (CHUNK // 8) - 1, 0), 0))
    return nc, chunk, row, tail


def _ssd_fwd(z, xbc, dtr, cw, cb, dtb, alog, dskip_exp, nw, expand, tril, seq):
    t_tok = z.shape[0]
    nc, chunk, row, tail = _ssd_specs(t_tok, seq, False)

    def body(z_ref, xbc_ref, tail_ref, dtr_ref, cw_ref, cb_ref, dtb_ref, alog_ref, dsk_ref, nw_ref, exp_ref,
             tril_ref, o_ref, y_ref, st_ref, state_ref):
        c = pl.program_id(1)

        @pl.when(c == 0)
        def _():
            state_ref[...] = jnp.zeros_like(state_ref)

        m_l, m_r = _lane_masks()
        f = _ssd_common(xbc_ref[...], jnp.where(c == 0, 0.0, tail_ref[...]), dtr_ref[...], cw_ref, cb_ref[...],
                        dtb_ref[...], alog_ref[...], exp_ref[...], tril_ref[...])
        act = f["act"]
        xs = act[:, :SSM_WIDTH]
        xdt = xs * f["dt_exp"]
        xw = xdt * f["w_end"]
        state = state_ref[...]
        st_ref[0] = state
        ydiag, yoff, snew = [], [], []
        for g in range(2):
            bg = act[:, 512 + 128 * g:640 + 128 * g].astype(BF16)
            cg = act[:, 768 + 128 * g:896 + 128 * g].astype(BF16)
            cb_mat = _dot(cg, bg, _NT)
            for pr in range(2):
                h0 = 4 * g + 2 * pr
                gcat = jnp.concatenate(
                    [(cb_mat * f["decay"][h0]).astype(BF16), (cb_mat * f["decay"][h0 + 1]).astype(BF16)], axis=1)
                ydiag.append(_dot(gcat, _stack_pair(xdt[:, 64 * h0:64 * h0 + 128], m_l, m_r)))
            yoff.append(_dot(cg, state[:, 256 * g:256 * (g + 1)].astype(BF16)))
            snew.append(_dot(bg, xw[:, 256 * g:256 * (g + 1)].astype(BF16), _TN))
        y = jnp.concatenate(ydiag, axis=1) + f["e"] * jnp.concatenate(yoff, axis=1) + dsk_ref[...] * xs
        state_ref[...] = state * f["cd"] + jnp.concatenate(snew, axis=1)
        y_ref[...] = y
        zv = z_ref[...].astype(F32)
        yg = y * (zv * jax.nn.sigmoid(zv))
        outs = []
        for g in range(2):
            ygg = yg[:, 256 * g:256 * (g + 1)]
            outs.append(ygg * lax.rsqrt(jnp.mean(ygg * ygg, axis=-1, keepdims=True) + EPS))
        o_ref[...] = (jnp.concatenate(outs, axis=1) * nw_ref[...]).astype(BF16)

    consts = [cw, cb, dtb, alog, dskip_exp, nw, expand, tril]
    return pl.pallas_call(
        body, name="ssd_fwd", grid=(t_tok // seq, nc),
        out_shape=(jax.ShapeDtypeStruct((t_tok, SSM_WIDTH), BF16), jax.ShapeDtypeStruct((t_tok, SSM_WIDTH), F32),
                   jax.ShapeDtypeStruct((t_tok // CHUNK, N_STATE, SSM_WIDTH), F32)),
        in_specs=[row(SSM_WIDTH), row(CONV_CH), tail, row(CHUNK)] + [_full(a.shape) for a in consts],
        out_specs=(row(SSM_WIDTH), row(SSM_WIDTH),
                   pl.BlockSpec((1, N_STATE, SSM_WIDTH), lambda b, c: (chunk(b, c), 0, 0))),
        scratch_shapes=[pltpu.VMEM((N_STATE, SSM_WIDTH), F32)],
        compiler_params=_params("arbitrary", "arbitrary"))(z, xbc, xbc, dtr, *consts)


def _out_proj(mix_a, mix_b, w_out, x, g2, g3, tm, dep=None):
    t_tok = x.shape[0]
    deps = [] if dep is None else [dep]

    def body(a_ref, b_ref, w_ref, x_ref, g2_ref, g3_ref, *rest):
        o_ref, x2_ref, h3_ref, mix_ref = rest[-4:]
        o = _dot(a_ref[...], w_ref[0:GM_WIDTH, :]) + _dot(b_ref[...], w_ref[GM_WIDTH:, :])
        o_ref[...] = o
        mix_ref[:, 0:GM_WIDTH] = a_ref[...]
        mix_ref[:, GM_WIDTH:] = b_ref[...]
        r2 = lax.rsqrt(jnp.mean(o * o, axis=-1, keepdims=True) + EPS)
        x2 = x_ref[...] + o * r2 * g2_ref[...]
        x2_ref[...] = x2
        r3 = lax.rsqrt(jnp.mean(x2 * x2, axis=-1, keepdims=True) + EPS)
        h3_ref[...] = (x2 * r3 * g3_ref[...]).astype(BF16)

    row = lambda n: pl.BlockSpec((tm, n), lambda i: (i, 0))
    sd = lambda dt: jax.ShapeDtypeStruct((t_tok, D_MODEL), dt)
    return pl.pallas_call(
        body, name="out_proj", grid=(t_tok // tm,), out_shape=(sd(F32), sd(F32), sd(BF16), sd(BF16)),
        in_specs=[row(GM_WIDTH), row(SSM_WIDTH), _full((D_MODEL, D_MODEL)), row(D_MODEL), _full((1, D_MODEL)),
                  _full((1, D_MODEL))] + [pl.BlockSpec(memory_space=pl.ANY)] * len(deps),
        out_specs=(row(D_MODEL),) * 4, compiler_params=_params("parallel"))(mix_a, mix_b, w_out, x, g2, g3, *deps)


def _mlp_fwd(h3, w_up, w_down, x2, target, g4, tm, tf):
    t_tok = x2.shape[0]

    def up_body(h_ref, wu_ref, ra_ref):
        ra_ref[...] = jnp.maximum(_dot(h_ref[...], wu_ref[...]), 0.0).astype(BF16)

    ra = pl.pallas_call(
        up_body, name="mlp_up", grid=(D_FF // tf, t_tok // tm), out_shape=jax.ShapeDtypeStruct((t_tok, D_FF), BF16),
        in_specs=[pl.BlockSpec((tm, D_MODEL), lambda j, i: (i, 0)), pl.BlockSpec((D_MODEL, tf), lambda j, i: (0, j))],
        out_specs=pl.BlockSpec((tm, tf), lambda j, i: (i, j)), compiler_params=_params("parallel", "parallel"))(h3, w_up)

    def down_body(ra_ref, wd_ref, x2_ref, t_ref, g4_ref, dd_ref, dy_ref, dg4_ref, loss_ref):
        i = pl.program_id(0)
        rav = ra_ref[...]
        dvec = _dot(rav * rav, wd_ref[...])
        r4 = lax.rsqrt(jnp.mean(dvec * dvec, axis=-1, keepdims=True) + EPS)
        dn = dvec * r4
        g4 = g4_ref[...]
        err = x2_ref[...] + dn * g4 - t_ref[...]
        dy = err * (1.0 / D_MODEL)
        dy_ref[...] = dy
        dg = dy * g4
        dd_ref[...] = (r4 * (dg - dn * jnp.mean(dg * dn, axis=-1, keepdims=True))).astype(BF16)
        _acc_rows(dg4_ref, _rsum(dy * dn), i == 0)
        tile_loss = 0.5 * jnp.sum(jnp.sum(err * err, axis=-1, keepdims=True), axis=0, keepdims=True) / D_MODEL
        _acc_rows(loss_ref, jnp.broadcast_to(tile_loss, (1, 128)), i == 0)

    row = pl.BlockSpec((tm, D_MODEL), lambda i: (i, 0))
    dd, dy, dg4, loss = pl.pallas_call(
        down_body, name="mlp_down", grid=(t_tok // tm,),
        out_shape=(jax.ShapeDtypeStruct((t_tok, D_MODEL), BF16), jax.ShapeDtypeStruct((t_tok, D_MODEL), F32),
                   jax.ShapeDtypeStruct((1, D_MODEL), F32), jax.ShapeDtypeStruct((1, 128), F32)),
        in_specs=[pl.BlockSpec((tm, D_FF), lambda i: (i, 0)), _full((D_FF, D_MODEL)), row, row, _full((1, D_MODEL))],
        out_specs=(row, row, _full((1, D_MODEL)), _full((1, 128))),
        compiler_params=_params("arbitrary"))(ra, w_down, x2, target, g4)
    return ra, dd, dy, dg4, loss


def _mlp_bwd(dd, w_down, ra, w_up, x2, dy, o, g3, g2, tm, tf):
    t_tok = x2.shape[0]

    def hidden_body(dd_ref, wd_ref, ra_ref, da_ref):
        df = _dot(dd_ref[...], wd_ref[...], _NT)
        da_ref[...] = (df * (2.0 * ra_ref[...].astype(F32))).astype(BF16)

    da = pl.pallas_call(
        hidden_body, name="mlp_bwd_hidden", grid=(D_FF // tf, t_tok // tm),
        out_shape=jax.ShapeDtypeStruct((t_tok, D_FF), BF16),
        in_specs=[pl.BlockSpec((tm, D_MODEL), lambda j, i: (i, 0)), pl.BlockSpec((tf, D_MODEL), lambda j, i: (j, 0)),
                  pl.BlockSpec((tm, tf), lambda j, i: (i, j))],
        out_specs=pl.BlockSpec((tm, tf), lambda j, i: (i, j)),
        compiler_params=_params("parallel", "parallel"))(dd, w_down, ra)

    def in_body(da_ref, wu_ref, x2_ref, dy_ref, o_ref, g3_ref, g2_ref, dx2_ref, do_ref, dg3_ref, dg2_ref):
        i = pl.program_id(0)
        dh3 = _dot(da_ref[...], wu_ref[...], _NT)
        dn3, dg3 = _rms_bwd(x2_ref[...], g3_ref[...], dh3)
        dx2 = dy_ref[...] + dn3
        dx2_ref[...] = dx2
        do, dg2 = _rms_bwd(o_ref[...], g2_ref[...], dx2)
        do_ref[...] = do.astype(BF16)
        _acc_rows(dg3_ref, dg3, i == 0)
        _acc_rows(dg2_ref, dg2, i == 0)

    row = pl.BlockSpec((tm, D_MODEL), lambda i: (i, 0))
    vec = _full((1, D_MODEL))
    sd = lambda dt: jax.ShapeDtypeStruct((t_tok, D_MODEL), dt)
    dx2, do, dg3, dg2 = pl.pallas_call(
        in_body, name="mlp_bwd_in", grid=(t_tok // tm,),
        out_shape=(sd(F32), sd(BF16), jax.ShapeDtypeStruct((1, D_MODEL), F32), jax.ShapeDtypeStruct((1, D_MODEL), F32)),
        in_specs=[pl.BlockSpec((tm, D_FF), lambda i: (i, 0)), _full((D_MODEL, D_FF)), row, row, row, vec, vec],
        out_specs=(row, row, vec, vec), compiler_params=_params("arbitrary"))(da, w_up, x2, dy, o, g3, g2)
    return da, dx2, do, dg3, dg2


def _wgrad(a, b, out_blocks, bm, bn, bk, square_a, name, dep=None):
    t_tok, m = a.shape
    n = b.shape[1]
    nk = t_tok // bk

    def body(a_ref, b_ref, *rest):
        o_ref, acc_ref = rest[-2:]
        k = pl.program_id(2)
        av = a_ref[...]
        if square_a:
            av = av * av
        part = _dot(av, b_ref[...], _TN)

        def emit(res):
            if out_blocks is None:
                o_ref[...] = res.astype(BF16)
            else:
                o_ref[0] = res.astype(BF16)

        if nk == 1:
            emit(part)
            return

        @pl.when(k == 0)
        def _():
            acc_ref[...] = part

        @pl.when(k > 0)
        def _():
            acc_ref[...] += part

        @pl.when(k == nk - 1)
        def _():
            emit(acc_ref[...])

    if out_blocks is None:
        out_shape = jax.ShapeDtypeStruct((m, n), BF16)
        out_spec = pl.BlockSpec((bm, bn), lambda i, j, k: (i, j))
    else:
        assert n // out_blocks == bn
        out_shape = jax.ShapeDtypeStruct((out_blocks, m, bn), BF16)
        out_spec = pl.BlockSpec((1, bm, bn), lambda i, j, k: (j, i, 0))
    deps = [] if dep is None else [dep]
    return pl.pallas_call(
        body, name=name, grid=(m // bm, n // bn, nk), out_shape=out_shape,
        in_specs=[pl.BlockSpec((bk, bm), lambda i, j, k: (k, i)), pl.BlockSpec((bk, bn), lambda i, j, k: (k, j))]
        + [pl.BlockSpec(memory_space=pl.ANY)] * len(deps),
        out_specs=out_spec, scratch_shapes=[pltpu.VMEM((bm, bn) if nk > 1 else (8, 128), F32)],
        compiler_params=_params("parallel", "parallel", "arbitrary"))(a, b, *deps)


def _wgrad_in(h1, pieces, bn, bk, dep=None):
    t_tok = h1.shape[0]
    nk = t_tok // bk
    widths = [b - a for a, b in _IN_SPLITS]

    def body(h_ref, *rest):
        piece_refs = rest[:len(widths)]
        o_ref, acc_ref = rest[-2:]
        k = pl.program_id(1)
        hv = h_ref[...]
        for (a, b), r in zip(_IN_SPLITS, piece_refs):
            part = _dot(r[...], hv, _TN)
            if nk == 1:
                o_ref[a:b, :] = part.astype(BF16)
                continue

            @pl.when(k == 0)
            def _():
                acc_ref[a:b, :] = part

            @pl.when(k > 0)
            def _():
                acc_ref[a:b, :] += part

        if nk > 1:
            @pl.when(k == nk - 1)
            def _():
                o_ref[...] = acc_ref[...].astype(BF16)

    deps = [] if dep is None else [dep]
    return pl.pallas_call(
        body, name="wgrad_in", grid=(D_MODEL // bn, nk), out_shape=jax.ShapeDtypeStruct((IN_PAD, D_MODEL), BF16),
        in_specs=[pl.BlockSpec((bk, bn), lambda j, k: (k, j))] + [pl.BlockSpec((bk, n), lambda j, k: (k, 0)) for n in widths]
        + [pl.BlockSpec(memory_space=pl.ANY)] * len(deps),
        out_specs=pl.BlockSpec((IN_PAD, bn), lambda j, k: (0, j)),
        scratch_shapes=[pltpu.VMEM((IN_PAD, bn) if nk > 1 else (8, 128), F32)],
        compiler_params=_params("parallel", "arbitrary"))(h1, *pieces, *deps)


def _dmix(do, w_out, tm, dep=None):
    t_tok = do.shape[0]

    def body(d_ref, w_ref, *rest):
        rest[-1][...] = _dot(d_ref[...], w_ref[...], _NT)

    row = pl.BlockSpec((tm, D_MODEL), lambda i: (i, 0))
    deps = [] if dep is None else [dep]
    return pl.pallas_call(
        body, name="dmix", grid=(t_tok // tm,), out_shape=jax.ShapeDtypeStruct((t_tok, D_MODEL), F32),
        in_specs=[row, _full((D_MODEL, D_MODEL))] + [pl.BlockSpec(memory_space=pl.ANY)] * len(deps), out_specs=row,
        compiler_params=_params("parallel"))(do, w_out, *deps)


def _gmlp_bwd(dmix, u, v, lnw, lnb, wcat, wtcat, bias, avg, expand_t):
    t_tok = u.shape[0]
    tm = min(_GMLP_ROWS, t_tok)

    def body(dm_ref, u_ref, v_ref, lnw_ref, lnb_ref, wcat_ref, wtcat_ref, bias_ref, avg_ref, expt_ref, du_ref, dv_ref,
             dw_ref, db_ref, dlnw_ref, dlnb_ref):
        i = pl.program_id(0)
        m_l, m_r = _lane_masks()
        avg = avg_ref[...]
        lnw = lnw_ref[...]
        ug, dug, dvg, rstd, vhat, vn, mixed = _gmlp_common(
            u_ref[...].astype(F32), v_ref[...].astype(F32), lnw, lnb_ref[...], avg, wcat_ref, bias_ref[...], m_l, m_r)
        dya = dm_ref[...]
        du_ref[...] = (dya * mixed * dug).astype(BF16)
        dmixed = dya * ug
        dvn_rows, dws, dbt = [], [None] * N_HEADS, None
        for r in range(tm // CHUNK):
            dvn_cols = []
            for j in range(N_HEADS // 2):
                dmp = dmixed[CHUNK * r:CHUNK * (r + 1), 128 * j:128 * (j + 1)]
                dvn_cols.append(_dot(wtcat_ref[j], _stack_pair(dmp, m_l, m_r)))
                vnp = vn[CHUNK * r:CHUNK * (r + 1), 128 * j:128 * (j + 1)].astype(BF16)
                for i_h, mask in enumerate((m_l, m_r)):
                    part = _dot((dmp * mask).astype(BF16), vnp, _NT)
                    dws[2 * j + i_h] = part if r == 0 else dws[2 * j + i_h] + part
            dvn_rows.append(jnp.concatenate(dvn_cols, axis=1))
            part = _split_dot(dmixed[CHUNK * r:CHUNK * (r + 1), :], expt_ref[...], 2)
            dbt = part if r == 0 else dbt + part
        dvn = jnp.concatenate(dvn_rows, axis=0)
        dvh = dvn * lnw
        dvgel = rstd * (dvh - _head_mean(dvh, avg) - vhat * _head_mean(dvh * vhat, avg))
        dv_ref[...] = (dvgel * dvg).astype(BF16)
        first = i == 0

        @pl.when(first)
        def _():
            for h in range(N_HEADS):
                dw_ref[h] = dws[h]
            db_ref[...] = dbt

        @pl.when(jnp.logical_not(first))
        def _():
            for h in range(N_HEADS):
                dw_ref[h] += dws[h]
            db_ref[...] += dbt

        _acc_rows(dlnw_ref, _rsum(dvn * vhat), first)
        _acc_rows(dlnb_ref, _rsum(dvn), first)

    row = pl.BlockSpec((tm, GM_WIDTH), lambda i: (i, 0))
    consts = [lnw, lnb, wcat, wtcat, bias, avg, expand_t]
    return pl.pallas_call(
        body, name="gmlp_bwd", grid=(t_tok // tm,),
        out_shape=(jax.ShapeDtypeStruct((t_tok, GM_WIDTH), BF16), jax.ShapeDtypeStruct((t_tok, GM_WIDTH), BF16),
                   jax.ShapeDtypeStruct((N_HEADS, CHUNK, CHUNK), F32), jax.ShapeDtypeStruct((CHUNK, CHUNK), F32),
                   jax.ShapeDtypeStruct((1, GM_WIDTH), F32), jax.ShapeDtypeStruct((1, GM_WIDTH), F32)),
        in_specs=[row, row, row] + [_full(a.shape) for a in consts],
        out_specs=(row, row, _full((N_HEADS, CHUNK, CHUNK)), _full((CHUNK, CHUNK)), _full((1, GM_WIDTH)),
                   _full((1, GM_WIDTH))),
        compiler_params=_params("arbitrary"))(dmix, u, v, *consts)


def _ssd_bwd(dmix, z, xbc, dtr, y, states, cw, cb, dtb, alog, dskip_exp, nw, expand, expand_t, tril, triu, seq,
             dep=None):
    t_tok = z.shape[0]
    nc, chunk, row, tail = _ssd_specs(t_tok, seq, True)
    q = CHUNK

    def body(dm_ref, z_ref, xbc_ref, tail_ref, dtr_ref, y_ref, st_ref, cw_ref, cb_ref, dtb_ref, alog_ref, dsk_ref,
             nw_ref, exp_ref, expt_ref, tril_ref, triu_ref, dz_ref, dxbc_ref, ddt_ref, dcw_ref, dcb_ref, ddtb_ref,
             dalog_ref, dd_ref, dnw_ref, dhead_ref, dstate_ref):
        b, c = pl.program_id(0), pl.program_id(1)
        first = jnp.logical_and(b == 0, c == 0)

        @pl.when(c == 0)
        def _():
            dstate_ref[...] = jnp.zeros_like(dstate_ref)
            dhead_ref[...] = jnp.zeros_like(dhead_ref)

        m_l, m_r = _lane_masks()
        expt = expt_ref[...]
        f = _ssd_common(xbc_ref[...], jnp.where(c == nc - 1, 0.0, tail_ref[...]), dtr_ref[...], cw_ref, cb_ref[...],
                        dtb_ref[...], alog_ref[...], exp_ref[...], tril_ref[...])
        act, pre, sg = f["act"], f["pre"], f["sg"]
        xs = act[:, :SSM_WIDTH]
        xdt = xs * f["dt_exp"]
        xw = xdt * f["w_end"]
        state = st_ref[0]
        dstate = dstate_ref[...]
        zv, yv, dout, nw = z_ref[...].astype(F32), y_ref[...], dm_ref[...], nw_ref[...]
        sz = jax.nn.sigmoid(zv)
        sl = zv * sz
        yg = yv * sl
        tv = dout * nw
        dyg_parts, ygh_parts = [], []
        for g in range(2):
            ygg = yg[:, 256 * g:256 * (g + 1)]
            rr = lax.rsqrt(jnp.mean(ygg * ygg, axis=-1, keepdims=True) + EPS)
            ygh = ygg * rr
            tg = tv[:, 256 * g:256 * (g + 1)]
            dyg_parts.append(rr * (tg - ygh * jnp.mean(tg * ygh, axis=-1, keepdims=True)))
            ygh_parts.append(ygh)
        dyg = jnp.concatenate(dyg_parts, axis=1)
        dnw = _rsum(dout * jnp.concatenate(ygh_parts, axis=1))
        dy = dyg * sl
        dz_ref[...] = (dyg * yv * (sz * (1.0 + zv * (1.0 - sz)))).astype(BF16)
        ddsk = _rsum(dy * xs)
        dye = dy * f["e"]
        lane = lax.broadcasted_iota(jnp.int32, (q, q), 1)
        sub = lax.broadcasted_iota(jnp.int32, (q, q), 0)
        rs_mat = jnp.zeros((q, q), F32)
        cs_mat = jnp.zeros((q, q), F32)
        dxdt_cols, yoff, dst_in, dxw, d_b, d_c = [], [], [], [], [], []
        for g in range(2):
            bg = act[:, 512 + 128 * g:640 + 128 * g].astype(BF16)
            cg = act[:, 768 + 128 * g:896 + 128 * g].astype(BF16)
            cb_mat = _dot(cg, bg, _NT)
            stg = state[:, 256 * g:256 * (g + 1)].astype(BF16)
            dyeg = dye[:, 256 * g:256 * (g + 1)].astype(BF16)
            yoff.append(_dot(cg, stg))
            dcg = _dot(dyeg, stg, _NT)
            dst_in.append(_dot(cg, dyeg, _TN))
            dcb = jnp.zeros((q, q), F32)
            for pr in range(2):
                h0 = 4 * g + 2 * pr
                gf = [cb_mat * f["decay"][h0], cb_mat * f["decay"][h0 + 1]]
                gcat = jnp.concatenate([gf[0].astype(BF16), gf[1].astype(BF16)], axis=1)
                xst = _stack_pair(xdt[:, 64 * h0:64 * h0 + 128], m_l, m_r)
                dyp = dy[:, 64 * h0:64 * h0 + 128].astype(BF16)
                dgcat = _dot(dyp, xst, _NT)
                dxst = _dot(gcat, dyp, _TN)
                dxdt_cols.append(dxst[:q] * m_l + dxst[q:] * m_r)
                for i in range(2):
                    h = h0 + i
                    dg = dgcat[:, q * i:q * (i + 1)]
                    mm = dg * gf[i]
                    rs_mat = rs_mat + jnp.where(lane == h, jnp.sum(mm, axis=1, keepdims=True), 0.0)
                    cs_mat = cs_mat + jnp.where(sub == h, jnp.sum(mm, axis=0, keepdims=True), 0.0)
                    dcb = dcb + dg * f["decay"][h]
            dcb16 = dcb.astype(BF16)
            dstg = dstate[:, 256 * g:256 * (g + 1)].astype(BF16)
            d_c.append(dcg + _dot(dcb16, bg))
            dxw.append(_dot(bg, dstg))
            d_b.append(_dot(dcb16, cg, _TN) + _dot(xw[:, 256 * g:256 * (g + 1)].astype(BF16), dstg, _NT))
        dxw = jnp.concatenate(dxw, axis=1)
        dxdt = jnp.concatenate(dxdt_cols, axis=1) + dxw * f["w_end"]
        qv = dxw * xw
        end_row = _rsum(qv) + _rsum(dstate * state) * f["cd"]
        x2 = dye * jnp.concatenate(yoff, axis=1) - qv
        row_i = lax.broadcasted_iota(jnp.int32, (q, 1), 0)
        x2 = x2 + jnp.where(row_i == q - 1, end_row, 0.0)
        da_cs = _split_dot(x2, expt, 3) + rs_mat - cs_mat.T
        ddt = _split_dot(dxdt * xs, expt, 3)
        dxs = dsk_ref[...] * dy + dxdt * f["dt_exp"]
        dda = _split_dot_left(triu_ref[...], da_cs, 3)
        ddt = ddt + dda * f["a_row"]
        dalog = _rsum(dda * f["dt"]) * f["a_row"]
        draw = ddt * jax.nn.sigmoid(f["dtp"])
        ddt_ref[...] = draw.astype(BF16)
        dact = jnp.concatenate([dxs] + d_b + d_c, axis=1)
        dpre = dact * (sg * (1.0 + pre * (1.0 - sg)))
        dhead = dhead_ref[...]
        dxbc = cw_ref[3:4, :] * dpre
        for k in range(3):
            dxbc = dxbc + cw_ref[k:k + 1, :] * _shift_rows(dpre, dhead, 3 - k, False)
        dxbc_ref[...] = dxbc.astype(BF16)
        dhead_ref[...] = dpre[0:8, :]
        dstate_ref[...] = dstate * f["cd"] + jnp.concatenate(dst_in, axis=1)
        row8 = lax.broadcasted_iota(jnp.int32, (8, 1), 0)
        dcw = jnp.zeros((8, CONV_CH), F32)
        for k in range(4):
            dcw = dcw + jnp.where(row8 == k, _rsum(dpre * f["taps"][k]), 0.0)

        @pl.when(first)
        def _():
            dcw_ref[...] = dcw

        @pl.when(jnp.logical_not(first))
        def _():
            dcw_ref[...] += dcw

        _acc_rows(dcb_ref, _rsum(dpre), first)
        _acc_rows(ddtb_ref, _rsum(draw), first)
        _acc_rows(dalog_ref, dalog, first)
        _acc_rows(dd_ref, _split_dot(ddsk, expt, 3), first)
        _acc_rows(dnw_ref, dnw, first)

    consts = [cw, cb, dtb, alog, dskip_exp, nw, expand, expand_t, tril, triu]
    deps = [] if dep is None else [dep]
    n_in = 7 + len(consts)

    def body_skipping_dep(*refs):
        body(*refs[:n_in], *refs[n_in + len(deps):])

    acc = lambda n: jax.ShapeDtypeStruct((1, n), F32)
    return pl.pallas_call(
        body_skipping_dep, name="ssd_bwd", grid=(t_tok // seq, nc),
        out_shape=(jax.ShapeDtypeStruct((t_tok, SSM_WIDTH), BF16), jax.ShapeDtypeStruct((t_tok, CONV_CH), BF16),
                   jax.ShapeDtypeStruct((t_tok, CHUNK), BF16), jax.ShapeDtypeStruct((8, CONV_CH), F32), acc(CONV_CH),
                   acc(CHUNK), acc(CHUNK), acc(CHUNK), acc(SSM_WIDTH)),
        in_specs=[pl.BlockSpec((CHUNK, SSM_WIDTH), lambda b, c: (chunk(b, c), 1)), row(SSM_WIDTH), row(CONV_CH), tail,
                  row(CHUNK), row(SSM_WIDTH), pl.BlockSpec((1, N_STATE, SSM_WIDTH), lambda b, c: (chunk(b, c), 0, 0))]
        + [_full(a.shape) for a in consts] + [pl.BlockSpec(memory_space=pl.ANY)] * len(deps),
        out_specs=(row(SSM_WIDTH), row(CONV_CH), row(CHUNK), _full((8, CONV_CH)), _full((1, CONV_CH)),
                   _full((1, CHUNK)), _full((1, CHUNK)), _full((1, CHUNK)), _full((1, SSM_WIDTH))),
        scratch_shapes=[pltpu.VMEM((8, CONV_CH), F32), pltpu.VMEM((N_STATE, SSM_WIDTH), F32)],
        compiler_params=_params("arbitrary", "arbitrary"))(dmix, z, xbc, xbc, dtr, y, states, *consts, *deps)


def _in_bwd(du, dv, dz, dxbc, ddt, w_in, x, dx2, g1, tm, dep=None):
    t_tok = x.shape[0]

    def body(du_ref, dv_ref, dz_ref, dxbc_ref, ddt_ref, w_ref, x_ref, dx2_ref, g_ref, *rest):
        gx_ref, dg_ref = rest[-2:]
        i = pl.program_id(0)
        dh = None
        for (a, b), ref in zip(_IN_SPLITS, (du_ref, dv_ref, dz_ref, dxbc_ref, ddt_ref)):
            part = _dot(ref[...], w_ref[a:b, :])
            dh = part if dh is None else dh + part
        dn, dg = _rms_bwd(x_ref[...], g_ref[...], dh)
        gx_ref[...] = dx2_ref[...] + dn
        _acc_rows(dg_ref, dg, i == 0)

    row = lambda n: pl.BlockSpec((tm, n), lambda i: (i, 0))
    widths = [b - a for a, b in _IN_SPLITS]
    deps = [] if dep is None else [dep]
    return pl.pallas_call(
        body, name="in_bwd", grid=(t_tok // tm,),
        out_shape=(jax.ShapeDtypeStruct((t_tok, D_MODEL), F32), jax.ShapeDtypeStruct((1, D_MODEL), F32)),
        in_specs=[row(n) for n in widths] + [_full((IN_PAD, D_MODEL)), row(D_MODEL), row(D_MODEL), _full((1, D_MODEL))]
        + [pl.BlockSpec(memory_space=pl.ANY)] * len(deps),
        out_specs=(row(D_MODEL), _full((1, D_MODEL))),
        compiler_params=_params("arbitrary"))(du, dv, dz, dxbc, ddt, w_in, x, dx2, g1, *deps)


def _pad_lanes(a, n):
    return jnp.pad(a, ((0, 0), (0, n - a.shape[1])))


def _local_step(x, target, seq, w_in_t, conv_w, small, hooks):
    t_tok = x.shape[0]
    tm = min(512, t_tok)
    avg, expand, expand_t, tril, triu = _const_mats()
    g1, g2, g3, g4 = (small[k].reshape(1, D_MODEL) for k in
                      ("norm_mix_pre", "norm_mix_post", "norm_ffn_pre", "norm_ffn_post"))
    lnw = small["gm_ln_w"].reshape(1, GM_WIDTH)
    lnb = small["gm_ln_b"].reshape(1, GM_WIDTH)
    causal = jnp.tril(jnp.ones((CHUNK, CHUNK), F32))
    wm = small["gm_w_s"] * causal
    pair = lambda w: w.reshape(4, 2, CHUNK, CHUNK).transpose(0, 2, 1, 3).reshape(4, CHUNK, 2 * CHUNK).astype(BF16)
    wcat = pair(wm)
    wtcat = pair(jnp.swapaxes(wm, 1, 2))
    bias = jnp.repeat(small["gm_b_s"].T, HEAD_DIM, axis=1)
    cb = small["conv_b"].reshape(1, CONV_CH)
    dtb = _pad_lanes(small["dt_bias"].reshape(1, N_HEADS), CHUNK)
    alog = _pad_lanes(small["a_log"].reshape(1, N_HEADS), CHUNK)
    dskip_exp = jnp.repeat(small["d_skip"].reshape(1, N_HEADS), HEAD_DIM, axis=1)
    nw = small["ssm_norm_w"].reshape(1, SSM_WIDTH)

    h1, u, v, z, xbc, dtr = _in_proj(x, g1, w_in_t, tm)
    mix_a = _gmlp_fwd(u, v, lnw, lnb, wcat, bias, avg)
    mix_b, y_pre, states = _ssd_fwd(z, xbc, dtr, conv_w, cb, dtb, alog, dskip_exp, nw, expand, tril, seq)
    w_out, dep = hooks["mixers_done"](mix_b)
    o, x2, h3, mix = _out_proj(mix_a, mix_b, w_out, x, g2, g3, tm, dep)
    w_up, w_down = hooks["mlp_weights"](h3)
    tf = 2048
    ra, dd, dy, dg4, loss = _mlp_fwd(h3, w_up, w_down, x2, target, g4, tm, tf)

    da, dx2, do, dg3, dg2 = _mlp_bwd(dd, w_down, ra, w_up, x2, dy, o, g3, g2, tm, tf)
    bk = min(2048, t_tok)
    g_w_down = _wgrad(ra, dd, None, 512, D_MODEL, t_tok, True, "wgrad_down")
    g_w_up = _wgrad(h3, da, N_DEV, D_MODEL, D_FF // N_DEV, t_tok, False, "wgrad_up")
    dep = hooks["mlp_grads"](g_w_down, g_w_up)
    dmix = _dmix(do, w_out, tm, dep)
    g_w_out = _wgrad(mix, do, None, D_MODEL, 512, t_tok, False, "wgrad_out", dep)
    du, dv, dws, dbt, dlnw, dlnb = _gmlp_bwd(dmix, u, v, lnw, lnb, wcat, wtcat, bias, avg, expand_t)
    dep = hooks["gmlp_grads"](g_w_out, dws)
    dz, dxbc, ddt, dcw, dcb, ddtb, dalog, ddsk, dnw = _ssd_bwd(
        dmix, z, xbc, dtr, y_pre, states, conv_w, cb, dtb, alog, dskip_exp, nw, expand, expand_t, tril, triu, seq, dep)
    g_w_in = _wgrad_in(h1, (du, dv, dz, dxbc, ddt), 512, bk, dep)
    dep = hooks["in_grads"](g_w_in, dcw[0:4])
    grad_x, dg1 = _in_bwd(du, dv, dz, dxbc, ddt, w_in_t, x, dx2, g1, tm, dep)

    grads = dict(
        w_in=g_w_in, w_out=g_w_out, w_up=g_w_up, w_down=g_w_down, conv_w=dcw[0:4],
        norm_mix_pre=dg1, norm_mix_post=dg2, norm_ffn_pre=dg3, norm_ffn_post=dg4, gm_ln_w=dlnw, gm_ln_b=dlnb,
        gm_w_s=dws, gm_b_s=dbt, conv_b=dcb, dt_bias=ddtb, a_log=dalog, d_skip=ddsk, ssm_norm_w=dnw)
    return loss[0, 0], grad_x, grads


_WEIGHTS = ("norm_mix_pre", "w_in", "gm_ln_w", "gm_ln_b", "gm_w_s", "gm_b_s", "conv_w", "conv_b", "dt_bias", "a_log",
            "d_skip", "ssm_norm_w", "w_out", "norm_mix_post", "norm_ffn_pre", "w_up", "w_down", "norm_ffn_post")
_SLAB_ROWS = (("norm_mix_pre", 1024), ("norm_mix_post", 1024), ("norm_ffn_pre", 1024), ("norm_ffn_post", 1024),
              ("conv_b", 1024), ("ssm_norm_w", 512), ("gm_ln_w", 512), ("gm_ln_b", 512), ("dt_bias", 8), ("a_log", 8),
              ("d_skip", 8))
_SLAB_LOSS_ROW = len(_SLAB_ROWS)
_SLAB_BS_ROW = 16
_SLAB_HEIGHT = 24
_SMALL_PARAMS = tuple(name for name, _ in _SLAB_ROWS) + ("gm_b_s",)
_LN_PARAMS = ("gm_ln_w", "gm_ln_b")


def _pack_slab(g, loss_part):
    rows = [_pad_lanes(g[name], D_MODEL) for name, _ in _SLAB_ROWS]
    rows.append(jnp.broadcast_to(loss_part, (1, D_MODEL)))
    rows.append(jnp.zeros((_SLAB_BS_ROW - len(rows), D_MODEL), F32))
    rows.append(_pad_lanes(g["gm_b_s"].T[0:N_HEADS], D_MODEL))
    return jnp.concatenate(rows, axis=0)


def _adamw_slab(parts, w, m, v):
    names = _SMALL_PARAMS
    shapes = [w[k].shape for k in names]
    unfold = np.zeros((GM_WIDTH, HEAD_DIM), np.float32)
    for h in range(N_HEADS):
        unfold[h * HEAD_DIM:(h + 1) * HEAD_DIM, :] = np.eye(HEAD_DIM)
    unfold = jnp.asarray(unfold, dtype=BF16)
    n = len(names)

    def body(p_ref, unfold_ref, *refs):
        w_refs, m_refs, v_refs = refs[:n], refs[n:2 * n], refs[2 * n:3 * n]
        outs = refs[3 * n:]
        g_all = p_ref[0]
        for j in range(1, N_DEV):
            g_all = g_all + p_ref[j]
        lane = lax.broadcasted_iota(jnp.int32, (N_HEADS, GM_WIDTH), 1)
        head = lax.broadcasted_iota(jnp.int32, (N_HEADS, GM_WIDTH), 0)
        own_lanes = jnp.logical_and(lane >= head * HEAD_DIM, lane < (head + 1) * HEAD_DIM)
        for i, name in enumerate(names):
            if name == "gm_b_s":
                g = g_all[_SLAB_BS_ROW:_SLAB_BS_ROW + N_HEADS, 0:CHUNK]
            else:
                row = [r for r, (k, _) in enumerate(_SLAB_ROWS) if k == name][0]
                g = g_all[row:row + 1, 0:dict(_SLAB_ROWS)[name]]
                if name in _LN_PARAMS:
                    g = _split_dot(jnp.where(own_lanes, g, 0.0), unfold_ref[...], 3)
            d, mn, vn = _adamw_math(w_refs[i][...], g, m_refs[i][...], v_refs[i][...])
            for o_ref, val in zip(outs[4 * i:4 * i + 4], (g, d, mn, vn)):
                o_ref[...] = val
        outs[-1][...] = g_all[_SLAB_LOSS_ROW:_SLAB_LOSS_ROW + 1, 0:128]

    ins = [parts, unfold] + [d[k] for d in (w, m, v) for k in names]
    out_shape = tuple(jax.ShapeDtypeStruct(s, F32) for s in shapes for _ in range(4)) + (
        jax.ShapeDtypeStruct((1, 128), F32),)
    outs = pl.pallas_call(
        body, name="adamw_small", out_shape=out_shape, grid=(1,), in_specs=[_full(a.shape) for a in ins],
        out_specs=tuple(_full(s.shape) for s in out_shape), compiler_params=_params("arbitrary"))(*ins)
    return {k: tuple(outs[4 * i:4 * i + 4]) for i, k in enumerate(names)}, outs[-1][0, 0]


def kernel(x, norm_mix_pre, w_in, gm_ln_w, gm_ln_b, gm_w_s, gm_b_s, conv_w, conv_b, dt_bias, a_log, d_skip, ssm_norm_w, w_out, norm_mix_post, norm_ffn_pre, w_up, w_down, norm_ffn_post, loss_target, m_norm_mix_pre, m_w_in, m_gm_ln_w, m_gm_ln_b, m_gm_w_s, m_gm_b_s, m_conv_w, m_conv_b, m_dt_bias, m_a_log, m_d_skip, m_ssm_norm_w, m_w_out, m_norm_mix_post, m_norm_ffn_pre, m_w_up, m_w_down, m_norm_ffn_post, v_norm_mix_pre, v_w_in, v_gm_ln_w, v_gm_ln_b, v_gm_w_s, v_gm_b_s, v_conv_w, v_conv_b, v_dt_bias, v_a_log, v_d_skip, v_ssm_norm_w, v_w_out, v_norm_mix_post, v_norm_ffn_pre, v_w_up, v_w_down, v_norm_ffn_post):
    w = dict(norm_mix_pre=norm_mix_pre, w_in=w_in, gm_ln_w=gm_ln_w, gm_ln_b=gm_ln_b, gm_w_s=gm_w_s, gm_b_s=gm_b_s, conv_w=conv_w, conv_b=conv_b, dt_bias=dt_bias, a_log=a_log, d_skip=d_skip, ssm_norm_w=ssm_norm_w, w_out=w_out, norm_mix_post=norm_mix_post, norm_ffn_pre=norm_ffn_pre, w_up=w_up, w_down=w_down, norm_ffn_post=norm_ffn_post)
    m = dict(norm_mix_pre=m_norm_mix_pre, w_in=m_w_in, gm_ln_w=m_gm_ln_w, gm_ln_b=m_gm_ln_b, gm_w_s=m_gm_w_s, gm_b_s=m_gm_b_s, conv_w=m_conv_w, conv_b=m_conv_b, dt_bias=m_dt_bias, a_log=m_a_log, d_skip=m_d_skip, ssm_norm_w=m_ssm_norm_w, w_out=m_w_out, norm_mix_post=m_norm_mix_post, norm_ffn_pre=m_norm_ffn_pre, w_up=m_w_up, w_down=m_w_down, norm_ffn_post=m_norm_ffn_post)
    v = dict(norm_mix_pre=v_norm_mix_pre, w_in=v_w_in, gm_ln_w=v_gm_ln_w, gm_ln_b=v_gm_ln_b, gm_w_s=v_gm_w_s, gm_b_s=v_gm_b_s, conv_w=v_conv_w, conv_b=v_conv_b, dt_bias=v_dt_bias, a_log=v_a_log, d_skip=v_d_skip, ssm_norm_w=v_ssm_norm_w, w_out=v_w_out, norm_mix_post=v_norm_mix_post, norm_ffn_pre=v_norm_ffn_pre, w_up=v_w_up, w_down=v_w_down, norm_ffn_post=v_norm_ffn_post)
    n_batch, seq, _ = x.shape
    shard_in = IN_COLS // N_DEV

    me = (4 * lax.axis_index("x") + 2 * lax.axis_index("y") + lax.axis_index("c")).astype(jnp.int32).reshape(1)

    def in_slot(own):
        return lax.dynamic_update_slice(lax.empty((N_DEV,) + own.shape, own.dtype), own[None],
                                        (me[0],) + (0,) * own.ndim)

    w_in_sh, m_in_sh, v_in_sh = w_in[0].T, m_w_in[0].T, v_w_in[0].T
    first = [_cast_to_slot(w_in_sh, me, shard_in, "cast_w_in"), in_slot(conv_w[0]),
             _cast_to_slot(w_out[0], me, 128, "cast_w_out")]
    ici_1, _ = _exchange_start(first, [True] * 3, _SAME_CORE_PEERS, "gather_mix_ici_start")
    first = [buf for buf, _ in _exchange_wait(ici_1, me, "gather_mix_ici_wait")]
    d2d_1, tok_d2d_1 = _exchange_start(first, [True] * 3, _SIBLING_FORWARD, "gather_mix_d2d_start")
    second = [_cast_to_slot(w_up[0], me, 256, "cast_w_up", cols=True), _cast_to_slot(w_down[0], me, 256, "cast_w_down")]
    ici_2, tok_ici_2 = _exchange_start(second, [True] * 2, _SAME_CORE_PEERS, "gather_mlp_ici_start", dep=tok_d2d_1)
    (_, ag_in), (_, ag_conv), (_, ag_out) = _exchange_wait(d2d_1, tok_ici_2, "gather_mix_d2d_wait")
    w_in_t = jnp.pad(ag_in.reshape(IN_COLS, D_MODEL), ((0, IN_PAD - IN_COLS), (0, 0)))
    conv_w_f = ag_conv.transpose(1, 0, 2).reshape(4, CONV_CH)
    w_out_f = ag_out.reshape(D_MODEL, D_MODEL)
    gathering = {}

    def mixers_done(after):
        bufs = [buf for buf, _ in _exchange_wait(ici_2, after, "gather_mlp_ici_wait")]
        gathering["mlp"], tok = _exchange_start(bufs, [True] * 2, _SIBLING_FORWARD, "gather_mlp_d2d_start")
        return w_out_f, tok

    def mlp_weights(after):
        (_, ag_up), (_, ag_down) = _exchange_wait(gathering["mlp"], after, "gather_mlp_d2d_wait")
        return ag_up, ag_down.reshape(D_FF, D_MODEL)

    sent = {}

    def mlp_grads(g_w_down, g_w_up):
        sent["mlp"], tok = _exchange_start(
            [g_w_down.reshape(N_DEV, D_FF // N_DEV, D_MODEL), g_w_up], [False, False], _ALL_PEERS, "grads_mlp_start")
        return tok

    def gmlp_grads(g_w_out, g_w_s):
        sent["gmlp"], tok = _exchange_start(
            [g_w_out.reshape(N_DEV, D_MODEL // N_DEV, D_MODEL), in_slot(g_w_s.astype(BF16))], [False, True], _ALL_PEERS,
            "grads_gmlp_start")
        return tok

    def in_grads(g_w_in_t, g_conv_w):
        g_in_blk = g_w_in_t[:IN_COLS].reshape(N_DEV, shard_in, D_MODEL)
        g_conv_blk = g_conv_w.reshape(4, N_DEV, CONV_CH // N_DEV).transpose(1, 0, 2)
        sent["in"], tok = _exchange_start([g_in_blk, g_conv_blk], [False, False], _ALL_PEERS, "grads_in_start")
        return tok

    small = {k: w[k][0] for k in _SMALL_PARAMS + ("gm_w_s",)}
    loss_part, grad_x, g = _local_step(
        x.reshape(n_batch * seq, D_MODEL), loss_target.reshape(n_batch * seq, D_MODEL), seq, w_in_t, conv_w_f, small,
        dict(mixers_done=mixers_done, mlp_weights=mlp_weights, mlp_grads=mlp_grads, gmlp_grads=gmlp_grads,
             in_grads=in_grads))

    sent_rows, tok_rows = _exchange_start([in_slot(_pack_slab(g, loss_part))], [True], _ALL_PEERS, "grads_rows_start")
    (own_down, p_down), (own_up, p_up) = _exchange_wait(sent["mlp"], tok_rows, "grads_mlp_wait")
    res = {}
    res["w_up"] = _adamw_reduce(p_up, own_up, me, w_up[0], m_w_up[0], v_w_up[0], 256, "adamw_w_up")
    res["w_down"] = _adamw_reduce(p_down, own_down, me, w_down[0], m_w_down[0], v_w_down[0], 128, "adamw_w_down")
    (own_out, p_out), (_, p_ws) = _exchange_wait(sent["gmlp"], res["w_down"][1], "grads_gmlp_wait")
    res["w_out"] = _adamw_reduce(p_out, own_out, me, w_out[0], m_w_out[0], v_w_out[0], 128, "adamw_w_out")
    causal = jnp.tril(jnp.ones((1, CHUNK, CHUNK), F32))
    res["gm_w_s"] = _adamw_small(p_ws, None, me, gm_w_s[0], m_gm_w_s[0], v_gm_w_s[0], causal, "adamw_gm_w_s")
    (own_in, p_in), (own_conv, p_conv) = _exchange_wait(sent["in"], res["gm_w_s"][1], "grads_in_wait")
    res["w_in"] = tuple(r.T for r in _adamw_reduce(p_in, own_in, me, w_in_sh, m_in_sh, v_in_sh, shard_in, "adamw_w_in"))
    res["conv_w"] = _adamw_small(p_conv, own_conv, me, conv_w[0], m_conv_w[0], v_conv_w[0], None, "adamw_conv_w")
    ((_, p_rows),) = _exchange_wait(sent_rows, res["w_in"][1], "grads_rows_wait")
    flat = lambda t: t[0] if t.ndim == 3 else t
    small_res, loss = _adamw_slab(p_rows, *({k: flat(d[k]) for k in _SMALL_PARAMS} for d in (w, m, v)))
    res.update(small_res)
    res = {k: tuple(r.reshape(w[k].shape) for r in res[k]) for k in _WEIGHTS}

    outs = [loss, grad_x.reshape(x.shape)]
    for part in range(4):
        outs.extend(res[k][part] for k in _WEIGHTS)
    return tuple(outs)
```

```python
import functools

import jax
import jax.numpy as jnp
import numpy as np
from jax import lax
from jax.experimental import pallas as pl
from jax.experimental.pallas import tpu as pltpu

F32 = jnp.float32
BF16 = jnp.bfloat16

D_MODEL = 1024
GM_WIDTH = 512
SSM_WIDTH = 512
CONV_CH = 1024
N_HEADS = 8
HEAD_DIM = 64
N_STATE = 128
CHUNK = 128
D_FF = 4096
IN_COLS = 2568
IN_PAD = 2688
N_DEV = 8
EPS = 1e-6
ADAM_LR, ADAM_B1, ADAM_B2, ADAM_EPS, ADAM_WD, ADAM_STEP = 0.001, 0.9, 0.999, 1e-08, 0.01, 10
VMEM_LIMIT_BYTES = 56 * 1024 * 1024
SMALL_ROWS = 16

_NT = (((1,), (1,)), ((), ()))
_TN = (((0,), (0,)), ((), ()))


def _params(*sem):
    return pltpu.CompilerParams(dimension_semantics=sem or None, vmem_limit_bytes=VMEM_LIMIT_BYTES)


def _dot(a, b, dims=None):
    if dims is None:
        return jnp.dot(a, b, preferred_element_type=F32)
    return lax.dot_general(a, b, dims, preferred_element_type=F32)


def _split_terms(x, terms):
    out, rem = [], x
    for i in range(terms):
        hi = rem.astype(BF16)
        out.append(hi)
        if i + 1 < terms:
            rem = rem - hi.astype(F32)
    return out


def _split_dot(x, m, terms):
    acc = None
    for hi in _split_terms(x, terms):
        part = _dot(hi, m)
        acc = part if acc is None else acc + part
    return acc


def _split_dot_left(m, x, terms):
    acc = None
    for hi in _split_terms(x, terms):
        part = _dot(m, hi)
        acc = part if acc is None else acc + part
    return acc


def _gelu_and_grad(x):
    c = 0.7978845608028654
    inner = c * (x + 0.044715 * x * x * x)
    t = jnp.tanh(inner)
    g = 0.5 * x * (1.0 + t)
    dg = 0.5 * (1.0 + t) + 0.5 * x * (1.0 - t * t) * c * (1.0 + 3.0 * 0.044715 * x * x)
    return g, dg


def _softplus(x):
    return jnp.maximum(x, 0.0) + jnp.log(1.0 + jnp.exp(-jnp.abs(x)))


def _rsum(x):
    return jnp.sum(x, axis=0, keepdims=True)


def _acc_rows(ref, part, first):
    val = jnp.broadcast_to(part, ref.shape)

    @pl.when(first)
    def _():
        ref[...] = val

    @pl.when(jnp.logical_not(first))
    def _():
        ref[...] += val


def _rms_bwd(n, g, dout):
    r = lax.rsqrt(jnp.mean(n * n, axis=-1, keepdims=True) + EPS)
    nh = n * r
    dg = dout * g
    dn = r * (dg - nh * jnp.mean(dg * nh, axis=-1, keepdims=True))
    return dn, _rsum(dout * nh)


def _const_mats():
    avg = np.kron(np.eye(4), np.full((HEAD_DIM, HEAD_DIM), 1.0 / HEAD_DIM))
    expand = np.zeros((CHUNK, SSM_WIDTH), np.float32)
    for h in range(N_HEADS):
        expand[h, h * HEAD_DIM:(h + 1) * HEAD_DIM] = 1.0
    tril = np.tril(np.ones((CHUNK, CHUNK), np.float32))
    as_bf16 = lambda a: jnp.asarray(a, dtype=BF16)
    return as_bf16(avg), as_bf16(expand), as_bf16(expand.T), as_bf16(tril), as_bf16(tril.T)


def _full(shape):
    nd = len(shape)
    return pl.BlockSpec(shape, lambda *_: (0,) * nd)


_HBM = pl.BlockSpec(memory_space=pltpu.HBM)
_SEM = pl.BlockSpec(memory_space=pltpu.SEMAPHORE)
_ALL_PEERS = tuple((k, 0) for k in range(1, N_DEV))
_SAME_CORE_PEERS = ((2, 0), (4, 0), (6, 0))
_SIBLING_FORWARD = ((1, 0), (1, 2), (1, 4), (1, 6))


def _flip(j, k):
    for bit in (4, 2, 1):
        if k & bit:
            j = j + bit - 2 * (j & bit)
    return j


def _copies(src, land, send_sems, recv_sems, hops):
    x, y, c = lax.axis_index("x"), lax.axis_index("y"), lax.axis_index("c")
    me = 4 * x + 2 * y + c
    out = []
    for t in range(len(src)):
        for i, (k, b) in enumerate(hops):
            pos = (1 - x if k & 4 else x, 1 - y if k & 2 else y, 1 - c if k & 1 else c)
            peer = _flip(me, k)
            sem = t * len(hops) + i
            mk = functools.partial(pltpu.make_async_remote_copy, send_sem=send_sems.at[sem], recv_sem=recv_sems.at[sem],
                                   device_id=pos, device_id_type=pl.DeviceIdType.MESH)
            if land[t] is None and src[t].shape[0] != N_DEV:
                width = src[t].shape[1] // N_DEV
                slab = lambda j: src[t].at[:, pl.ds(pl.multiple_of(j * width, 128), width)]
                mine = functools.partial(mk, src_ref=slab(_flip(me, b)), dst_ref=slab(_flip(me, b)))
                theirs = functools.partial(mk, src_ref=slab(_flip(peer, b)), dst_ref=slab(_flip(peer, b)))
            elif land[t] is None:
                mine = functools.partial(mk, src_ref=src[t].at[_flip(me, b)], dst_ref=src[t].at[_flip(me, b)])
                theirs = functools.partial(mk, src_ref=src[t].at[_flip(peer, b)], dst_ref=src[t].at[_flip(peer, b)])
            else:
                assert b == 0
                mine = functools.partial(mk, src_ref=src[t].at[peer], dst_ref=land[t].at[me])
                theirs = functools.partial(mk, src_ref=src[t].at[peer], dst_ref=land[t].at[peer])
            out.append((mine, theirs))
    return out


def _exchange_start(srcs, inplace, peers, name, dep=None):
    n = len(srcs)
    lands = [None if ip else pltpu.with_memory_space_constraint(lax.empty(s.shape, s.dtype), pltpu.HBM)
             for s, ip in zip(srcs, inplace)]
    real_lands = [l for l in lands if l is not None]
    n_l = len(real_lands)
    deps = [] if dep is None else [dep]

    def body(*refs):
        src = refs[:n]
        land_refs = list(refs[n:n + n_l])
        send_sems, recv_sems = refs[n + n_l + len(deps)], refs[n + n_l + len(deps) + 1]
        token = refs[-1]
        land = [None if ip else land_refs.pop(0) for ip in inplace]
        for mine, _ in _copies(src, land, send_sems, recv_sems, peers):
            mine().start()
        token[...] = jnp.zeros_like(token)

    sem_t = pltpu.SemaphoreType.DMA((n * len(peers),))
    outs = pl.pallas_call(
        body, name=name,
        out_shape=(sem_t, sem_t) + tuple(pltpu.HBM(a.shape, a.dtype) for a in list(srcs) + real_lands)
        + (jax.ShapeDtypeStruct((8, 128), F32),),
        in_specs=[_HBM] * (n + n_l) + [pl.BlockSpec(memory_space=pl.ANY)] * len(deps),
        out_specs=(_SEM, _SEM) + (_HBM,) * (n + n_l) + (pl.BlockSpec(memory_space=pltpu.VMEM),),
        input_output_aliases={i: 2 + i for i in range(n + n_l)},
        compiler_params=pltpu.CompilerParams(has_side_effects=pltpu.SideEffectType.DATAFLOW_SIDE_EFFECTING),
    )(*[pltpu.with_memory_space_constraint(s, pltpu.HBM) for s in srcs], *real_lands, *deps)
    handle = dict(send=outs[0], recv=outs[1], srcs=outs[2:2 + n], lands=outs[2 + n:2 + n + n_l], inplace=inplace,
                  peers=peers)
    return handle, outs[-1]


def _exchange_wait(handle, after, name):
    srcs, lands, inplace, peers = handle["srcs"], handle["lands"], handle["inplace"], handle["peers"]
    n, n_l = len(srcs), len(lands)

    def body(*refs):
        src = refs[:n]
        land_refs = list(refs[n:n + n_l])
        send_sems, recv_sems = refs[n + n_l], refs[n + n_l + 1]
        land = [None if ip else land_refs.pop(0) for ip in inplace]
        for mine, theirs in _copies(src, land, send_sems, recv_sems, peers):
            mine().wait_send()
            theirs().wait_recv()

    outs = pl.pallas_call(
        body, name=name, out_shape=tuple(pltpu.HBM(a.shape, a.dtype) for a in list(srcs) + list(lands)),
        in_specs=[_HBM] * (n + n_l) + [_SEM, _SEM, pl.BlockSpec(memory_space=pl.ANY)],
        out_specs=(_HBM,) * (n + n_l), input_output_aliases={i: i for i in range(n + n_l)},
        compiler_params=pltpu.CompilerParams(has_side_effects=pltpu.SideEffectType.DATAFLOW_SIDE_EFFECTING),
    )(*srcs, *lands, handle["send"], handle["recv"], after)
    res, land_out = [], list(outs[n:])
    for t in range(n):
        res.append((outs[t], outs[t] if inplace[t] else land_out.pop(0)))
    return res


def _cast_to_slot(w, me, rows, name, cols=False, dep=None):
    r, cdim = w.shape
    deps = [] if dep is None else [dep]

    def body(me_ref, w_ref, *rest):
        o_ref = rest[-1]
        if cols:
            o_ref[...] = w_ref[...].astype(BF16)
        else:
            o_ref[0] = w_ref[...].astype(BF16)

    if cols:
        out_shape = jax.ShapeDtypeStruct((r, N_DEV * cdim), BF16)
        out_spec = pl.BlockSpec((rows, cdim), lambda i, me_ref: (i, me_ref[0]))
    else:
        out_shape = jax.ShapeDtypeStruct((N_DEV, r, cdim), BF16)
        out_spec = pl.BlockSpec((1, rows, cdim), lambda i, me_ref: (me_ref[0], i, 0))
    return pl.pallas_call(
        body, name=name, out_shape=out_shape,
        grid_spec=pltpu.PrefetchScalarGridSpec(
            num_scalar_prefetch=1, grid=(r // rows,),
            in_specs=[pl.BlockSpec((rows, cdim), lambda i, me_ref: (i, 0))]
            + [pl.BlockSpec(memory_space=pl.ANY)] * len(deps), out_specs=out_spec),
        compiler_params=_params("parallel"))(me, w, *deps)


def _adamw_math(w, g, m, v):
    m = ADAM_B1 * m + (1.0 - ADAM_B1) * g
    v = ADAM_B2 * v + (1.0 - ADAM_B2) * (g * g)
    m_hat = m / (1.0 - ADAM_B1 ** ADAM_STEP)
    v_hat = v / (1.0 - ADAM_B2 ** ADAM_STEP)
    delta = -ADAM_LR * (m_hat / (jnp.sqrt(v_hat) + ADAM_EPS) + ADAM_WD * w)
    return delta, m, v


def _sum_parts(me, p_ref, own):
    g = None
    for j in range(N_DEV):
        term = (p_ref[j] if own is None else jnp.where(me == j, own, p_ref[j])).astype(F32)
        g = term if g is None else g + term
    return g


def _adamw_reduce(parts, own, me, w, m, v, rows, name):
    r, cdim = w.shape

    def body(me_ref, p_ref, own_ref, w_ref, m_ref, v_ref, g_out, d_out, m_out, v_out):
        g = _sum_parts(me_ref[0], p_ref, own_ref[0])
        d, mn, vn = _adamw_math(w_ref[...], g, m_ref[...], v_ref[...])
        g_out[...] = g
        d_out[...] = d
        m_out[...] = mn
        v_out[...] = vn

    blk = pl.BlockSpec((rows, cdim), lambda i, me_ref: (i, 0))
    sds = jax.ShapeDtypeStruct(w.shape, F32)
    return pl.pallas_call(
        body, name=name, out_shape=(sds,) * 4,
        grid_spec=pltpu.PrefetchScalarGridSpec(
            num_scalar_prefetch=1, grid=(r // rows,),
            in_specs=[pl.BlockSpec((N_DEV, rows, cdim), lambda i, me_ref: (0, i, 0)),
                      pl.BlockSpec((1, rows, cdim), lambda i, me_ref: (me_ref[0], i, 0)), blk, blk, blk],
            out_specs=(blk,) * 4),
        compiler_params=_params("parallel"))(me, parts, own, w, m, v)


def _adamw_small(parts, own, me, w, m, v, mask, name):
    def body(me_ref, *refs):
        refs = list(refs)
        p_ref = refs.pop(0)
        own_ref = None if own is None else refs.pop(0)
        w_ref, m_ref, v_ref = refs[:3]
        k_ref = None if mask is None else refs[3]
        g_out, d_out, m_out, v_out = refs[-4:]
        g = _sum_parts(me_ref[0], p_ref, None if own is None else own_ref[me_ref[0]])
        if mask is not None:
            g = g * k_ref[...]
        d, mn, vn = _adamw_math(w_ref[...], g, m_ref[...], v_ref[...])
        g_out[...] = g
        d_out[...] = d
        m_out[...] = mn
        v_out[...] = vn

    def whole(shape):
        nd = len(shape)
        return pl.BlockSpec(shape, lambda i, me_ref: (0,) * nd)

    sds = jax.ShapeDtypeStruct(w.shape, F32)
    ins = [parts] + ([] if own is None else [own]) + [w, m, v] + ([] if mask is None else [mask])
    return pl.pallas_call(
        body, name=name, out_shape=(sds,) * 4,
        grid_spec=pltpu.PrefetchScalarGridSpec(
            num_scalar_prefetch=1, grid=(1,), in_specs=[whole(a.shape) for a in ins],
            out_specs=(whole(w.shape),) * 4),
        compiler_params=_params("arbitrary"))(me, *ins)


_IN_SPLITS = ((0, 512), (512, 1024), (1024, 1536), (1536, 2560), (2560, IN_PAD))


def _prenorm(x, g1, tm, dep=None):
    t_tok = x.shape[0]
    deps = [] if dep is None else [dep]

    def body(x_ref, g_ref, *rest):
        xv = x_ref[...]
        r = lax.rsqrt(jnp.mean(xv * xv, axis=-1, keepdims=True) + EPS)
        rest[-1][...] = (xv * r * g_ref[...]).astype(BF16)

    row = pl.BlockSpec((tm, D_MODEL), lambda i: (i, 0))
    return pl.pallas_call(
        body, name="prenorm", grid=(t_tok // tm,), out_shape=jax.ShapeDtypeStruct((t_tok, D_MODEL), BF16),
        in_specs=[row, _full((1, D_MODEL))] + [pl.BlockSpec(memory_space=pl.ANY)] * len(deps), out_specs=row,
        compiler_params=_params("parallel"))(x, g1, *deps)


def _in_proj(h1, w_in, tm):
    t_tok = h1.shape[0]

    def body(h_ref, w_ref, *outs):
        h = h_ref[...]
        for (a, b), o_ref in zip(_IN_SPLITS, outs):
            o_ref[...] = _dot(h, w_ref[a:b, :], _NT).astype(o_ref.dtype)

    row = lambda n: pl.BlockSpec((tm, n), lambda i: (i, 0))
    widths = [b - a for a, b in _IN_SPLITS]
    dtypes = (BF16, BF16, BF16, F32, F32)
    return pl.pallas_call(
        body, name="in_proj", grid=(t_tok // tm,),
        out_shape=tuple(jax.ShapeDtypeStruct((t_tok, n), dt) for n, dt in zip(widths, dtypes)),
        in_specs=[row(D_MODEL), _full((IN_PAD, D_MODEL))], out_specs=tuple(row(n) for n in widths),
        compiler_params=_params("parallel"))(h1, w_in)


def _lane_masks():
    lane = lax.broadcasted_iota(jnp.int32, (1, 2 * HEAD_DIM), 1)
    left = (lane < HEAD_DIM).astype(F32)
    return left, 1.0 - left


def _stack_pair(v, m_l, m_r):
    return jnp.concatenate([v * m_l, v * m_r], axis=0).astype(BF16)


def _head_mean(x, avg):
    n = avg.shape[0]
    return jnp.concatenate([_split_dot(x[:, n * i:n * (i + 1)], avg, 2) for i in range(x.shape[1] // n)], axis=1)


def _gmlp_common(u, v, lnw, lnb, avg, wcat_ref, bias, m_l, m_r):
    ug, dug = _gelu_and_grad(u)
    vg, dvg = _gelu_and_grad(v)
    mu = _head_mean(vg, avg)
    vc = vg - mu
    var = _head_mean(vc * vc, avg)
    rstd = lax.rsqrt(var + EPS)
    vhat = vc * rstd
    vn = vhat * lnw + lnb
    rows = []
    for r in range(u.shape[0] // CHUNK):
        cols = []
        for j in range(N_HEADS // 2):
            pair = vn[CHUNK * r:CHUNK * (r + 1), 128 * j:128 * (j + 1)]
            cols.append(_dot(wcat_ref[j], _stack_pair(pair, m_l, m_r)))
        rows.append(jnp.concatenate(cols, axis=1) + bias)
    mixed = jnp.concatenate(rows, axis=0)
    return ug, dug, dvg, rstd, vhat, vn, mixed


_GMLP_ROWS = 4 * CHUNK


def _gmlp_fwd(u, v, lnw, lnb, wcat, bias, avg):
    t_tok = u.shape[0]
    tm = min(_GMLP_ROWS, t_tok)

    def body(u_ref, v_ref, lnw_ref, lnb_ref, wcat_ref, bias_ref, avg_ref, o_ref):
        m_l, m_r = _lane_masks()
        ug, _, _, _, _, _, mixed = _gmlp_common(
            u_ref[...].astype(F32), v_ref[...].astype(F32), lnw_ref[...], lnb_ref[...], avg_ref[...], wcat_ref,
            bias_ref[...], m_l, m_r)
        o_ref[...] = (ug * mixed).astype(BF16)

    row = pl.BlockSpec((tm, GM_WIDTH), lambda i: (i, 0))
    return pl.pallas_call(
        body, name="gmlp_fwd", grid=(t_tok // tm,), out_shape=jax.ShapeDtypeStruct((t_tok, GM_WIDTH), BF16),
        in_specs=[row, row, _full((1, GM_WIDTH)), _full((1, GM_WIDTH)), _full(wcat.shape), _full(bias.shape),
                  _full(avg.shape)],
        out_specs=row, compiler_params=_params("parallel"))(u, v, lnw, lnb, wcat, bias, avg)


def _shift_rows(x, edge, j, down):
    groups, cols = x.shape[0] // 8, x.shape[1]
    amount = j if down else 8 - j
    rot = pltpu.roll(x.reshape(groups, 8, cols), amount, axis=1)
    edge_rot = pltpu.roll(edge, amount, axis=0)[None]
    sub = lax.broadcasted_iota(jnp.int32, (1, 8, 1), 1)
    if down:
        out = jnp.where(sub < j, jnp.concatenate([edge_rot, rot[:-1]], axis=0), rot)
    else:
        out = jnp.where(sub < 8 - j, rot, jnp.concatenate([rot[1:], edge_rot], axis=0))
    return out.reshape(x.shape)


def _conv_pre(xbc, tail, cw_ref, cb):
    taps = [_shift_rows(xbc, tail, 3 - k, True) for k in range(3)] + [xbc]
    return cb + cw_ref[0:1, :] * taps[0] + cw_ref[1:2, :] * taps[1] + cw_ref[2:3, :] * taps[2] + cw_ref[3:4, :] * taps[3]


def _ssd_common(pre, dtr, dtb, alog, expand, tril):
    q = CHUNK
    sg = jax.nn.sigmoid(pre)
    act = pre * sg
    lane = lax.broadcasted_iota(jnp.int32, (1, CHUNK), 1)
    a_row = jnp.where(lane < N_HEADS, -jnp.exp(alog), 0.0)
    dtp = dtr + dtb
    dt = _softplus(dtp)
    a_cs = _split_dot_left(tril, dt * a_row, 3)
    a_cs_t = a_cs.T
    dt_exp = _split_dot(dt, expand, 3)
    a_exp = _split_dot(a_cs, expand, 3)
    a_end = a_exp[q - 1:q, :]
    li = lax.broadcasted_iota(jnp.int32, (q, q), 0)
    si = lax.broadcasted_iota(jnp.int32, (q, q), 1)
    causal = si <= li
    decay = []
    for h in range(N_HEADS):
        seg = a_cs[:, h:h + 1] - a_cs_t[h:h + 1, :]
        decay.append(jnp.where(causal, jnp.exp(jnp.minimum(seg, 0.0)), 0.0))
    return dict(pre=pre, sg=sg, act=act, a_row=a_row, dtp=dtp, dt=dt, dt_exp=dt_exp, a_exp=a_exp,
                e=jnp.exp(a_exp), w_end=jnp.exp(a_end - a_exp), cd=jnp.exp(a_end), decay=decay)


def _ssd_specs(t_tok, seq, reverse):
    nc = seq // CHUNK

    def chunk(b, c):
        return b * nc + (nc - 1 - c if reverse else c)

    def row(n):
        return pl.BlockSpec((CHUNK, n), lambda b, c: (chunk(b, c), 0))

    tail = pl.BlockSpec((8, CONV_CH), lambda b, c: (jnp.maximum(chunk(b, c) * (CHUNK // 8) - 1, 0), 0))
    return nc, chunk, row, tail


def _ssd_fwd(z, xbc, dtr, cw, cb, dtb, alog, dskip_exp, nw, expand, tril, seq):
    t_tok = z.shape[0]
    nc, chunk, row, tail = _ssd_specs(t_tok, seq, False)

    def body(z_ref, xbc_ref, tail_ref, dtr_ref, cw_ref, cb_ref, dtb_ref, alog_ref, dsk_ref, nw_ref, exp_ref,
             tril_ref, o_ref, y_ref, st_ref, pre_ref, state_ref):
        c = pl.program_id(1)

        @pl.when(c == 0)
        def _():
            state_ref[...] = jnp.zeros_like(state_ref)

        m_l, m_r = _lane_masks()
        pre = _conv_pre(xbc_ref[...], jnp.where(c == 0, 0.0, tail_ref[...]), cw_ref, cb_ref[...])
        pre_ref[...] = pre
        f = _ssd_common(pre, dtr_ref[...], dtb_ref[...], alog_ref[...], exp_ref[...], tril_ref[...])
        act = f["act"]
        xs = act[:, :SSM_WIDTH]
        xdt = xs * f["dt_exp"]
        xw = xdt * f["w_end"]
        state = state_ref[...]
        st_ref[0] = state
        ydiag, yoff, snew = [], [], []
        for g in range(2):
            bg = act[:, 512 + 128 * g:640 + 128 * g].astype(BF16)
            cg = act[:, 768 + 128 * g:896 + 128 * g].astype(BF16)
            cb_mat = _dot(cg, bg, _NT)
            for pr in range(2):
                h0 = 4 * g + 2 * pr
                gcat = jnp.concatenate(
                    [(cb_mat * f["decay"][h0]).astype(BF16), (cb_mat * f["decay"][h0 + 1]).astype(BF16)], axis=1)
                ydiag.append(_dot(gcat, _stack_pair(xdt[:, 64 * h0:64 * h0 + 128], m_l, m_r)))
            yoff.append(_dot(cg, state[:, 256 * g:256 * (g + 1)].astype(BF16)))
            snew.append(_dot(bg, xw[:, 256 * g:256 * (g + 1)].astype(BF16), _TN))
        y = jnp.concatenate(ydiag, axis=1) + f["e"] * jnp.concatenate(yoff, axis=1) + dsk_ref[...] * xs
        state_ref[...] = state * f["cd"] + jnp.concatenate(snew, axis=1)
        y_ref[...] = y
        zv = z_ref[...].astype(F32)
        yg = y * (zv * jax.nn.sigmoid(zv))
        outs = []
        for g in range(2):
            ygg = yg[:, 256 * g:256 * (g + 1)]
            outs.append(ygg * lax.rsqrt(jnp.mean(ygg * ygg, axis=-1, keepdims=True) + EPS))
        o_ref[...] = (jnp.concatenate(outs, axis=1) * nw_ref[...]).astype(BF16)

    consts = [cw, cb, dtb, alog, dskip_exp, nw, expand, tril]
    return pl.pallas_call(
        body, name="ssd_fwd", grid=(t_tok // seq, nc),
        out_shape=(jax.ShapeDtypeStruct((t_tok, SSM_WIDTH), BF16), jax.ShapeDtypeStruct((t_tok, SSM_WIDTH), F32),
                   jax.ShapeDtypeStruct((t_tok // CHUNK, N_STATE, SSM_WIDTH), F32),
                   jax.ShapeDtypeStruct((t_tok, CONV_CH), F32)),
        in_specs=[row(SSM_WIDTH), row(CONV_CH), tail, row(CHUNK)] + [_full(a.shape) for a in consts],
        out_specs=(row(SSM_WIDTH), row(SSM_WIDTH),
                   pl.BlockSpec((1, N_STATE, SSM_WIDTH), lambda b, c: (chunk(b, c), 0, 0)), row(CONV_CH)),
        scratch_shapes=[pltpu.VMEM((N_STATE, SSM_WIDTH), F32)],
        compiler_params=_params("arbitrary", "arbitrary"))(z, xbc, xbc, dtr, *consts)


def _out_proj(mix_a, mix_b, w_out, x, g2, g3, tm, dep=None):
    t_tok = x.shape[0]
    deps = [] if dep is None else [dep]

    def body(a_ref, b_ref, w_ref, x_ref, g2_ref, g3_ref, *rest):
        o_ref, x2_ref, h3_ref, mix_ref = rest[-4:]
        o = _dot(a_ref[...], w_ref[0:GM_WIDTH, :]) + _dot(b_ref[...], w_ref[GM_WIDTH:, :])
        o_ref[...] = o
        mix_ref[:, 0:GM_WIDTH] = a_ref[...]
        mix_ref[:, GM_WIDTH:] = b_ref[...]
        r2 = lax.rsqrt(jnp.mean(o * o, axis=-1, keepdims=True) + EPS)
        x2 = x_ref[...] + o * r2 * g2_ref[...]
        x2_ref[...] = x2
        r3 = lax.rsqrt(jnp.mean(x2 * x2, axis=-1, keepdims=True) + EPS)
        h3_ref[...] = (x2 * r3 * g3_ref[...]).astype(BF16)

    row = lambda n: pl.BlockSpec((tm, n), lambda i: (i, 0))
    sd = lambda dt: jax.ShapeDtypeStruct((t_tok, D_MODEL), dt)
    return pl.pallas_call(
        body, name="out_proj", grid=(t_tok // tm,), out_shape=(sd(F32), sd(F32), sd(BF16), sd(BF16)),
        in_specs=[row(GM_WIDTH), row(SSM_WIDTH), _full((D_MODEL, D_MODEL)), row(D_MODEL), _full((1, D_MODEL)),
                  _full((1, D_MODEL))] + [pl.BlockSpec(memory_space=pl.ANY)] * len(deps),
        out_specs=(row(D_MODEL),) * 4, compiler_params=_params("parallel"))(mix_a, mix_b, w_out, x, g2, g3, *deps)


def _mlp_fwd(h3, w_up, w_down, x2, target, g4, tm, tf):
    t_tok = x2.shape[0]

    def up_body(h_ref, wu_ref, ra_ref):
        ra_ref[...] = jnp.maximum(_dot(h_ref[...], wu_ref[...]), 0.0).astype(BF16)

    ra = pl.pallas_call(
        up_body, name="mlp_up", grid=(D_FF // tf, t_tok // tm), out_shape=jax.ShapeDtypeStruct((t_tok, D_FF), BF16),
        in_specs=[pl.BlockSpec((tm, D_MODEL), lambda j, i: (i, 0)), pl.BlockSpec((D_MODEL, tf), lambda j, i: (0, j))],
        out_specs=pl.BlockSpec((tm, tf), lambda j, i: (i, j)), compiler_params=_params("parallel", "parallel"))(h3, w_up)

    def down_body(ra_ref, wd_ref, x2_ref, t_ref, g4_ref, dd_ref, dy_ref, dg4_ref, loss_ref):
        i = pl.program_id(0)
        rav = ra_ref[...]
        dvec = _dot(rav * rav, wd_ref[...])
        r4 = lax.rsqrt(jnp.mean(dvec * dvec, axis=-1, keepdims=True) + EPS)
        dn = dvec * r4
        g4 = g4_ref[...]
        err = x2_ref[...] + dn * g4 - t_ref[...]
        dy = err * (1.0 / D_MODEL)
        dy_ref[...] = dy
        dg = dy * g4
        dd_ref[...] = (r4 * (dg - dn * jnp.mean(dg * dn, axis=-1, keepdims=True))).astype(BF16)
        _acc_rows(dg4_ref, _rsum(dy * dn), i == 0)
        tile_loss = 0.5 * jnp.sum(jnp.sum(err * err, axis=-1, keepdims=True), axis=0, keepdims=True) / D_MODEL
        _acc_rows(loss_ref, jnp.broadcast_to(tile_loss, (1, 128)), i == 0)

    row = pl.BlockSpec((tm, D_MODEL), lambda i: (i, 0))
    dd, dy, dg4, loss = pl.pallas_call(
        down_body, name="mlp_down", grid=(t_tok // tm,),
        out_shape=(jax.ShapeDtypeStruct((t_tok, D_MODEL), BF16), jax.ShapeDtypeStruct((t_tok, D_MODEL), F32),
                   jax.ShapeDtypeStruct((1, D_MODEL), F32), jax.ShapeDtypeStruct((1, 128), F32)),
        in_specs=[pl.BlockSpec((tm, D_FF), lambda i: (i, 0)), _full((D_FF, D_MODEL)), row, row, _full((1, D_MODEL))],
        out_specs=(row, row, _full((1, D_MODEL)), _full((1, 128))),
        compiler_params=_params("arbitrary"))(ra, w_down, x2, target, g4)
    return ra, dd, dy, dg4, loss


def _mlp_bwd(dd, w_down, ra, w_up, x2, dy, o, g3, g2, tm, tf):
    t_tok = x2.shape[0]

    def hidden_body(dd_ref, wd_ref, ra_ref, da_ref):
        df = _dot(dd_ref[...], wd_ref[...], _NT)
        da_ref[...] = (df * (2.0 * ra_ref[...].astype(F32))).astype(BF16)

    da = pl.pallas_call(
        hidden_body, name="mlp_bwd_hidden", grid=(D_FF // tf, t_tok // tm),
        out_shape=jax.ShapeDtypeStruct((t_tok, D_FF), BF16),
        in_specs=[pl.BlockSpec((tm, D_MODEL), lambda j, i: (i, 0)), pl.BlockSpec((tf, D_MODEL), lambda j, i: (j, 0)),
                  pl.BlockSpec((tm, tf), lambda j, i: (i, j))],
        out_specs=pl.BlockSpec((tm, tf), lambda j, i: (i, j)),
        compiler_params=_params("parallel", "parallel"))(dd, w_down, ra)

    def in_body(da_ref, wu_ref, x2_ref, dy_ref, o_ref, g3_ref, g2_ref, dx2_ref, do_ref, dg3_ref, dg2_ref):
        i = pl.program_id(0)
        dh3 = _dot(da_ref[...], wu_ref[...], _NT)
        dn3, dg3 = _rms_bwd(x2_ref[...], g3_ref[...], dh3)
        dx2 = dy_ref[...] + dn3
        dx2_ref[...] = dx2
        do, dg2 = _rms_bwd(o_ref[...], g2_ref[...], dx2)
        do_ref[...] = do.astype(BF16)
        _acc_rows(dg3_ref, dg3, i == 0)
        _acc_rows(dg2_ref, dg2, i == 0)

    row = pl.BlockSpec((tm, D_MODEL), lambda i: (i, 0))
    vec = _full((1, D_MODEL))
    sd = lambda dt: jax.ShapeDtypeStruct((t_tok, D_MODEL), dt)
    dx2, do, dg3, dg2 = pl.pallas_call(
        in_body, name="mlp_bwd_in", grid=(t_tok // tm,),
        out_shape=(sd(F32), sd(BF16), jax.ShapeDtypeStruct((1, D_MODEL), F32), jax.ShapeDtypeStruct((1, D_MODEL), F32)),
        in_specs=[pl.BlockSpec((tm, D_FF), lambda i: (i, 0)), _full((D_MODEL, D_FF)), row, row, row, vec, vec],
        out_specs=(row, row, vec, vec), compiler_params=_params("arbitrary"))(da, w_up, x2, dy, o, g3, g2)
    return da, dx2, do, dg3, dg2


def _wgrad(a, b, out_blocks, bm, bn, bk, square_a, name, dep=None):
    t_tok, m = a.shape
    n = b.shape[1]
    nk = t_tok // bk

    def body(a_ref, b_ref, *rest):
        o_ref, acc_ref = rest[-2:]
        k = pl.program_id(2)
        av = a_ref[...]
        if square_a:
            av = av * av
        part = _dot(av, b_ref[...], _TN)

        def emit(res):
            if out_blocks is None:
                o_ref[...] = res.astype(BF16)
            else:
                o_ref[0] = res.astype(BF16)

        if nk == 1:
            emit(part)
            return

        @pl.when(k == 0)
        def _():
            acc_ref[...] = part

        @pl.when(k > 0)
        def _():
            acc_ref[...] += part

        @pl.when(k == nk - 1)
        def _():
            emit(acc_ref[...])

    if out_blocks is None:
        out_shape = jax.ShapeDtypeStruct((m, n), BF16)
        out_spec = pl.BlockSpec((bm, bn), lambda i, j, k: (i, j))
    else:
        assert n // out_blocks == bn
        out_shape = jax.ShapeDtypeStruct((out_blocks, m, bn), BF16)
        out_spec = pl.BlockSpec((1, bm, bn), lambda i, j, k: (j, i, 0))
    deps = [] if dep is None else [dep]
    return pl.pallas_call(
        body, name=name, grid=(m // bm, n // bn, nk), out_shape=out_shape,
        in_specs=[pl.BlockSpec((bk, bm), lambda i, j, k: (k, i)), pl.BlockSpec((bk, bn), lambda i, j, k: (k, j))]
        + [pl.BlockSpec(memory_space=pl.ANY)] * len(deps),
        out_specs=out_spec, scratch_shapes=[pltpu.VMEM((bm, bn) if nk > 1 else (8, 128), F32)],
        compiler_params=_params("parallel", "parallel", "arbitrary"))(a, b, *deps)


def _wgrad_in(h1, pieces, bn, bk, dep=None):
    t_tok = h1.shape[0]
    nk = t_tok // bk
    widths = [b - a for a, b in _IN_SPLITS]

    def body(h_ref, *rest):
        piece_refs = rest[:len(widths)]
        o_ref, acc_ref = rest[-2:]
        k = pl.program_id(1)
        hv = h_ref[...]
        for (a, b), r in zip(_IN_SPLITS, piece_refs):
            part = _dot(r[...], hv, _TN)
            if nk == 1:
                o_ref[a:b, :] = part.astype(BF16)
                continue

            @pl.when(k == 0)
            def _():
                acc_ref[a:b, :] = part

            @pl.when(k > 0)
            def _():
                acc_ref[a:b, :] += part

        if nk > 1:
            @pl.when(k == nk - 1)
            def _():
                o_ref[...] = acc_ref[...].astype(BF16)

    deps = [] if dep is None else [dep]
    return pl.pallas_call(
        body, name="wgrad_in", grid=(D_MODEL // bn, nk), out_shape=jax.ShapeDtypeStruct((IN_PAD, D_MODEL), BF16),
        in_specs=[pl.BlockSpec((bk, bn), lambda j, k: (k, j))] + [pl.BlockSpec((bk, n), lambda j, k: (k, 0)) for n in widths]
        + [pl.BlockSpec(memory_space=pl.ANY)] * len(deps),
        out_specs=pl.BlockSpec((IN_PAD, bn), lambda j, k: (0, j)),
        scratch_shapes=[pltpu.VMEM((IN_PAD, bn) if nk > 1 else (8, 128), F32)],
        compiler_params=_params("parallel", "arbitrary"))(h1, *pieces, *deps)


def _dmix(do, w_out, tm, dep=None):
    t_tok = do.shape[0]

    def body(d_ref, w_ref, *rest):
        rest[-1][...] = _dot(d_ref[...], w_ref[...], _NT)

    row = pl.BlockSpec((tm, D_MODEL), lambda i: (i, 0))
    deps = [] if dep is None else [dep]
    return pl.pallas_call(
        body, name="dmix", grid=(t_tok // tm,), out_shape=jax.ShapeDtypeStruct((t_tok, D_MODEL), F32),
        in_specs=[row, _full((D_MODEL, D_MODEL))] + [pl.BlockSpec(memory_space=pl.ANY)] * len(deps), out_specs=row,
        compiler_params=_params("parallel"))(do, w_out, *deps)


def _gmlp_bwd(dmix, u, v, lnw, lnb, wcat, wtcat, bias, avg, expand_t):
    t_tok = u.shape[0]
    tm = min(_GMLP_ROWS, t_tok)

    def body(dm_ref, u_ref, v_ref, lnw_ref, lnb_ref, wcat_ref, wtcat_ref, bias_ref, avg_ref, expt_ref, du_ref, dv_ref,
             dw_ref, db_ref, dlnw_ref, dlnb_ref):
        i = pl.program_id(0)
        m_l, m_r = _lane_masks()
        avg = avg_ref[...]
        lnw = lnw_ref[...]
        ug, dug, dvg, rstd, vhat, vn, mixed = _gmlp_common(
            u_ref[...].astype(F32), v_ref[...].astype(F32), lnw, lnb_ref[...], avg, wcat_ref, bias_ref[...], m_l, m_r)
        dya = dm_ref[...]
        du_ref[...] = (dya * mixed * dug).astype(BF16)
        dmixed = dya * ug
        dvn_rows, dws, dbt = [], [None] * N_HEADS, None
        for r in range(tm // CHUNK):
            dvn_cols = []
            for j in range(N_HEADS // 2):
                dmp = dmixed[CHUNK * r:CHUNK * (r + 1), 128 * j:128 * (j + 1)]
                dvn_cols.append(_dot(wtcat_ref[j], _stack_pair(dmp, m_l, m_r)))
                vnp = vn[CHUNK * r:CHUNK * (r + 1), 128 * j:128 * (j + 1)].astype(BF16)
                for i_h, mask in enumerate((m_l, m_r)):
                    part = _dot((dmp * mask).astype(BF16), vnp, _NT)
                    dws[2 * j + i_h] = part if r == 0 else dws[2 * j + i_h] + part
            dvn_rows.append(jnp.concatenate(dvn_cols, axis=1))
            part = _split_dot(dmixed[CHUNK * r:CHUNK * (r + 1), :], expt_ref[...], 2)
            dbt = part if r == 0 else dbt + part
        dvn = jnp.concatenate(dvn_rows, axis=0)
        dvh = dvn * lnw
        dvgel = rstd * (dvh - _head_mean(dvh, avg) - vhat * _head_mean(dvh * vhat, avg))
        dv_ref[...] = (dvgel * dvg).astype(BF16)
        first = i == 0

        @pl.when(first)
        def _():
            for h in range(N_HEADS):
                dw_ref[h] = dws[h]
            db_ref[...] = dbt

        @pl.when(jnp.logical_not(first))
        def _():
            for h in range(N_HEADS):
                dw_ref[h] += dws[h]
            db_ref[...] += dbt

        _acc_rows(dlnw_ref, _rsum(dvn * vhat), first)
        _acc_rows(dlnb_ref, _rsum(dvn), first)

    row = pl.BlockSpec((tm, GM_WIDTH), lambda i: (i, 0))
    consts = [lnw, lnb, wcat, wtcat, bias, avg, expand_t]
    return pl.pallas_call(
        body, name="gmlp_bwd", grid=(t_tok // tm,),
        out_shape=(jax.ShapeDtypeStruct((t_tok, GM_WIDTH), BF16), jax.ShapeDtypeStruct((t_tok, GM_WIDTH), BF16),
                   jax.ShapeDtypeStruct((N_HEADS, CHUNK, CHUNK), F32), jax.ShapeDtypeStruct((CHUNK, CHUNK), F32),
                   jax.ShapeDtypeStruct((1, GM_WIDTH), F32), jax.ShapeDtypeStruct((1, GM_WIDTH), F32)),
        in_specs=[row, row, row] + [_full(a.shape) for a in consts],
        out_specs=(row, row, _full((N_HEADS, CHUNK, CHUNK)), _full((CHUNK, CHUNK)), _full((1, GM_WIDTH)),
                   _full((1, GM_WIDTH))),
        compiler_params=_params("arbitrary"))(dmix, u, v, *consts)


def _ssd_bwd(dmix, z, xbc, pre, dtr, y, states, cw, cb, dtb, alog, dskip_exp, nw, expand, expand_t, tril, triu, seq,
             dep=None):
    t_tok = z.shape[0]
    nc, chunk, row, _ = _ssd_specs(t_tok, seq, True)
    q = CHUNK

    def body(dm_ref, z_ref, xbc_ref, pre_ref, dtr_ref, y_ref, st_ref, cw_ref, cb_ref, dtb_ref, alog_ref, dsk_ref,
             nw_ref, exp_ref, expt_ref, tril_ref, triu_ref, dz_ref, dxbc_ref, ddt_ref, dcw_ref, dcb_ref, ddtb_ref,
             dalog_ref, dd_ref, dnw_ref, dhead_ref, dstate_ref):
        b, c = pl.program_id(0), pl.program_id(1)
        first = jnp.logical_and(b == 0, c == 0)

        @pl.when(c == 0)
        def _():
            dstate_ref[...] = jnp.zeros_like(dstate_ref)
            dhead_ref[...] = jnp.zeros_like(dhead_ref)

        m_l, m_r = _lane_masks()
        expt = expt_ref[...]
        f = _ssd_common(pre_ref[...], dtr_ref[...], dtb_ref[...], alog_ref[...], exp_ref[...], tril_ref[...])
        act, pre, sg = f["act"], f["pre"], f["sg"]
        xs = act[:, :SSM_WIDTH]
        xdt = xs * f["dt_exp"]
        xw = xdt * f["w_end"]
        state = st_ref[0]
        dstate = dstate_ref[...]
        zv, yv, dout, nw = z_ref[...].astype(F32), y_ref[...], dm_ref[...], nw_ref[...]
        sz = jax.nn.sigmoid(zv)
        sl = zv * sz
        yg = yv * sl
        tv = dout * nw
        dyg_parts, ygh_parts = [], []
        for g in range(2):
            ygg = yg[:, 256 * g:256 * (g + 1)]
            rr = lax.rsqrt(jnp.mean(ygg * ygg, axis=-1, keepdims=True) + EPS)
            ygh = ygg * rr
            tg = tv[:, 256 * g:256 * (g + 1)]
            dyg_parts.append(rr * (tg - ygh * jnp.mean(tg * ygh, axis=-1, keepdims=True)))
            ygh_parts.append(ygh)
        dyg = jnp.concatenate(dyg_parts, axis=1)
        dnw = _rsum(dout * jnp.concatenate(ygh_parts, axis=1))
        dy = dyg * sl
        dz_ref[...] = (dyg * yv * (sz * (1.0 + zv * (1.0 - sz)))).astype(BF16)
        ddsk = _rsum(dy * xs)
        dye = dy * f["e"]
        lane = lax.broadcasted_iota(jnp.int32, (q, q), 1)
        sub = lax.broadcasted_iota(jnp.int32, (q, q), 0)
        rs_mat = jnp.zeros((q, q), F32)
        cs_mat = jnp.zeros((q, q), F32)
        dxdt_cols, yoff, dst_in, dxw, d_b, d_c = [], [], [], [], [], []
        for g in range(2):
            bg = act[:, 512 + 128 * g:640 + 128 * g].astype(BF16)
            cg = act[:, 768 + 128 * g:896 + 128 * g].astype(BF16)
            cb_mat = _dot(cg, bg, _NT)
            stg = state[:, 256 * g:256 * (g + 1)].astype(BF16)
            dyeg = dye[:, 256 * g:256 * (g + 1)].astype(BF16)
            yoff.append(_dot(cg, stg))
            dcg = _dot(dyeg, stg, _NT)
            dst_in.append(_dot(cg, dyeg, _TN))
            dcb = jnp.zeros((q, q), F32)
            for pr in range(2):
                h0 = 4 * g + 2 * pr
                gf = [cb_mat * f["decay"][h0], cb_mat * f["decay"][h0 + 1]]
                gcat = jnp.concatenate([gf[0].astype(BF16), gf[1].astype(BF16)], axis=1)
                xst = _stack_pair(xdt[:, 64 * h0:64 * h0 + 128], m_l, m_r)
                dyp = dy[:, 64 * h0:64 * h0 + 128].astype(BF16)
                dgcat = _dot(dyp, xst, _NT)
                dxst = _dot(gcat, dyp, _TN)
                dxdt_cols.append(dxst[:q] * m_l + dxst[q:] * m_r)
                for i in range(2):
                    h = h0 + i
                    dg = dgcat[:, q * i:q * (i + 1)]
                    mm = dg * gf[i]
                    rs_mat = rs_mat + jnp.where(lane == h, jnp.sum(mm, axis=1, keepdims=True), 0.0)
                    cs_mat = cs_mat + jnp.where(sub == h, jnp.sum(mm, axis=0, keepdims=True), 0.0)
                    dcb = dcb + dg * f["decay"][h]
            dcb16 = dcb.astype(BF16)
            dstg = dstate[:, 256 * g:256 * (g + 1)].astype(BF16)
            d_c.append(dcg + _dot(dcb16, bg))
            dxw.append(_dot(bg, dstg))
            d_b.append(_dot(dcb16, cg, _TN) + _dot(xw[:, 256 * g:256 * (g + 1)].astype(BF16), dstg, _NT))
        dxw = jnp.concatenate(dxw, axis=1)
        dxdt = jnp.concatenate(dxdt_cols, axis=1) + dxw * f["w_end"]
        qv = dxw * xw
        end_row = _rsum(qv) + _rsum(dstate * state) * f["cd"]
        x2 = dye * jnp.concatenate(yoff, axis=1) - qv
        row_i = lax.broadcasted_iota(jnp.int32, (q, 1), 0)
        x2 = x2 + jnp.where(row_i == q - 1, end_row, 0.0)
        da_cs = _split_dot(x2, expt, 3) + rs_mat - cs_mat.T
        ddt = _split_dot(dxdt * xs, expt, 3)
        dxs = dsk_ref[...] * dy + dxdt * f["dt_exp"]
        dda = _split_dot_left(triu_ref[...], da_cs, 3)
        ddt = ddt + dda * f["a_row"]
        dalog = _rsum(dda * f["dt"]) * f["a_row"]
        draw = ddt * jax.nn.sigmoid(f["dtp"])
        ddt_ref[...] = draw.astype(BF16)
        dact = jnp.concatenate([dxs] + d_b + d_c, axis=1)
        dpre = dact * (sg * (1.0 + pre * (1.0 - sg)))
        dhead = dhead_ref[...]
        xv = xbc_ref[...]
        shifted = [_shift_rows(dpre, dhead, 3 - k, False) for k in range(3)] + [dpre]
        dxbc = cw_ref[3:4, :] * dpre
        for k in range(3):
            dxbc = dxbc + cw_ref[k:k + 1, :] * shifted[k]
        dxbc_ref[...] = dxbc.astype(BF16)
        dhead_ref[...] = dpre[0:8, :]
        dstate_ref[...] = dstate * f["cd"] + jnp.concatenate(dst_in, axis=1)
        row8 = lax.broadcasted_iota(jnp.int32, (8, 1), 0)
        dcw = jnp.zeros((8, CONV_CH), F32)
        for k in range(4):
            dcw = dcw + jnp.where(row8 == k, _rsum(shifted[k] * xv), 0.0)

        @pl.when(first)
        def _():
            dcw_ref[...] = dcw

        @pl.when(jnp.logical_not(first))
        def _():
            dcw_ref[...] += dcw

        _acc_rows(dcb_ref, _rsum(dpre), first)
        _acc_rows(ddtb_ref, _rsum(draw), first)
        _acc_rows(dalog_ref, dalog, first)
        _acc_rows(dd_ref, _split_dot(ddsk, expt, 3), first)
        _acc_rows(dnw_ref, dnw, first)

    consts = [cw, cb, dtb, alog, dskip_exp, nw, expand, expand_t, tril, triu]
    deps = [] if dep is None else [dep]
    n_in = 7 + len(consts)

    def body_skipping_dep(*refs):
        body(*refs[:n_in], *refs[n_in + len(deps):])

    acc = lambda n: jax.ShapeDtypeStruct((1, n), F32)
    return pl.pallas_call(
        body_skipping_dep, name="ssd_bwd", grid=(t_tok // seq, nc),
        out_shape=(jax.ShapeDtypeStruct((t_tok, SSM_WIDTH), BF16), jax.ShapeDtypeStruct((t_tok, CONV_CH), BF16),
                   jax.ShapeDtypeStruct((t_tok, CHUNK), BF16), jax.ShapeDtypeStruct((8, CONV_CH), F32), acc(CONV_CH),
                   acc(CHUNK), acc(CHUNK), acc(CHUNK), acc(SSM_WIDTH)),
        in_specs=[pl.BlockSpec((CHUNK, SSM_WIDTH), lambda b, c: (chunk(b, c), 1)), row(SSM_WIDTH), row(CONV_CH),
                  row(CONV_CH), row(CHUNK), row(SSM_WIDTH),
                  pl.BlockSpec((1, N_STATE, SSM_WIDTH), lambda b, c: (chunk(b, c), 0, 0))]
        + [_full(a.shape) for a in consts] + [pl.BlockSpec(memory_space=pl.ANY)] * len(deps),
        out_specs=(row(SSM_WIDTH), row(CONV_CH), row(CHUNK), _full((8, CONV_CH)), _full((1, CONV_CH)),
                   _full((1, CHUNK)), _full((1, CHUNK)), _full((1, CHUNK)), _full((1, SSM_WIDTH))),
        scratch_shapes=[pltpu.VMEM((8, CONV_CH), F32), pltpu.VMEM((N_STATE, SSM_WIDTH), F32)],
        compiler_params=_params("arbitrary", "arbitrary"))(dmix, z, xbc, pre, dtr, y, states, *consts, *deps)


def _in_bwd(du, dv, dz, dxbc, ddt, w_in, x, dx2, g1, tm, dep=None):
    t_tok = x.shape[0]

    def body(du_ref, dv_ref, dz_ref, dxbc_ref, ddt_ref, w_ref, x_ref, dx2_ref, g_ref, *rest):
        gx_ref, dg_ref = rest[-2:]
        i = pl.program_id(0)
        dh = None
        for (a, b), ref in zip(_IN_SPLITS, (du_ref, dv_ref, dz_ref, dxbc_ref, ddt_ref)):
            part = _dot(ref[...], w_ref[a:b, :])
            dh = part if dh is None else dh + part
        dn, dg = _rms_bwd(x_ref[...], g_ref[...], dh)
        gx_ref[...] = dx2_ref[...] + dn
        _acc_rows(dg_ref, dg, i == 0)

    row = lambda n: pl.BlockSpec((tm, n), lambda i: (i, 0))
    widths = [b - a for a, b in _IN_SPLITS]
    deps = [] if dep is None else [dep]
    return pl.pallas_call(
        body, name="in_bwd", grid=(t_tok // tm,),
        out_shape=(jax.ShapeDtypeStruct((t_tok, D_MODEL), F32), jax.ShapeDtypeStruct((1, D_MODEL), F32)),
        in_specs=[row(n) for n in widths] + [_full((IN_PAD, D_MODEL)), row(D_MODEL), row(D_MODEL), _full((1, D_MODEL))]
        + [pl.BlockSpec(memory_space=pl.ANY)] * len(deps),
        out_specs=(row(D_MODEL), _full((1, D_MODEL))),
        compiler_params=_params("arbitrary"))(du, dv, dz, dxbc, ddt, w_in, x, dx2, g1, *deps)


def _pad_lanes(a, n):
    return jnp.pad(a, ((0, 0), (0, n - a.shape[1])))


def _local_step(x, target, seq, small, hooks, first_dep=None):
    t_tok = x.shape[0]
    tm = min(512, t_tok)
    avg, expand, expand_t, tril, triu = _const_mats()
    g1, g2, g3, g4 = (small[k].reshape(1, D_MODEL) for k in
                      ("norm_mix_pre", "norm_mix_post", "norm_ffn_pre", "norm_ffn_post"))
    tie = (lambda a: a) if first_dep is None else (lambda a: a + first_dep[0, 0])
    lnw = tie(small["gm_ln_w"]).reshape(1, GM_WIDTH)
    lnb = tie(small["gm_ln_b"]).reshape(1, GM_WIDTH)
    causal = jnp.tril(jnp.ones((CHUNK, CHUNK), F32))
    wm = tie(small["gm_w_s"]) * causal
    pair = lambda w: w.reshape(4, 2, CHUNK, CHUNK).transpose(0, 2, 1, 3).reshape(4, CHUNK, 2 * CHUNK).astype(BF16)
    wcat = pair(wm)
    wtcat = pair(jnp.swapaxes(wm, 1, 2))
    bias = jnp.repeat(tie(small["gm_b_s"]).T, HEAD_DIM, axis=1)
    cb = small["conv_b"].reshape(1, CONV_CH)
    dtb = _pad_lanes(tie(small["dt_bias"]).reshape(1, N_HEADS), CHUNK)
    alog = _pad_lanes(tie(small["a_log"]).reshape(1, N_HEADS), CHUNK)
    dskip_exp = jnp.repeat(tie(small["d_skip"]).reshape(1, N_HEADS), HEAD_DIM, axis=1)
    nw = small["ssm_norm_w"].reshape(1, SSM_WIDTH)

    h1 = _prenorm(x, g1, tm, hooks.get("prenorm_after", first_dep))
    w_in_t, conv_w = hooks["mixer_weights"](h1)
    u, v, z, xbc, dtr = _in_proj(h1, w_in_t, tm)
    mix_a = _gmlp_fwd(u, v, lnw, lnb, wcat, bias, avg)
    mix_b, y_pre, states, pre = _ssd_fwd(z, xbc, dtr, conv_w, cb, dtb, alog, dskip_exp, nw, expand, tril, seq)
    w_out, dep = hooks["mixers_done"](mix_b)
    o, x2, h3, mix = _out_proj(mix_a, mix_b, w_out, x, g2, g3, tm, dep)
    w_up, w_down = hooks["mlp_weights"](h3)
    tf = 2048
    ra, dd, dy, dg4, loss = _mlp_fwd(h3, w_up, w_down, x2, target, g4, tm, tf)

    da, dx2, do, dg3, dg2 = _mlp_bwd(dd, w_down, ra, w_up, x2, dy, o, g3, g2, tm, tf)
    bk = min(2048, t_tok)
    g_w_down = _wgrad(ra, dd, None, 512, D_MODEL, t_tok, True, "wgrad_down")
    g_w_up = _wgrad(h3, da, N_DEV, D_MODEL, D_FF // N_DEV, t_tok, False, "wgrad_up")
    dep = hooks["mlp_grads"](g_w_down, g_w_up)
    dmix = _dmix(do, w_out, tm, dep)
    g_w_out = _wgrad(mix, do, None, D_MODEL, 512, t_tok, False, "wgrad_out", dep)
    du, dv, dws, dbt, dlnw, dlnb = _gmlp_bwd(dmix, u, v, lnw, lnb, wcat, wtcat, bias, avg, expand_t)
    dep = hooks["gmlp_grads"](g_w_out, dws)
    dz, dxbc, ddt, dcw, dcb, ddtb, dalog, ddsk, dnw = _ssd_bwd(
        dmix, z, xbc, pre, dtr, y_pre, states, conv_w, cb, dtb, alog, dskip_exp, nw, expand, expand_t, tril, triu, seq,
        dep)
    g_w_in = _wgrad_in(h1, (du, dv, dz, dxbc, ddt), 512, bk, dep)
    dep = hooks["in_grads"](g_w_in, dcw[0:4])
    grad_x, dg1 = _in_bwd(du, dv, dz, dxbc, ddt, w_in_t, x, dx2, g1, tm, dep)

    grads = dict(
        w_in=g_w_in, w_out=g_w_out, w_up=g_w_up, w_down=g_w_down, conv_w=dcw[0:4],
        norm_mix_pre=dg1, norm_mix_post=dg2, norm_ffn_pre=dg3, norm_ffn_post=dg4, gm_ln_w=dlnw, gm_ln_b=dlnb,
        gm_w_s=dws, gm_b_s=dbt, conv_b=dcb, dt_bias=ddtb, a_log=dalog, d_skip=ddsk, ssm_norm_w=dnw)
    return loss[0, 0], grad_x, grads


_WEIGHTS = ("norm_mix_pre", "w_in", "gm_ln_w", "gm_ln_b", "gm_w_s", "gm_b_s", "conv_w", "conv_b", "dt_bias", "a_log",
            "d_skip", "ssm_norm_w", "w_out", "norm_mix_post", "norm_ffn_pre", "w_up", "w_down", "norm_ffn_post")
_SLAB_ROWS = (("norm_mix_pre", 1024), ("norm_mix_post", 1024), ("norm_ffn_pre", 1024), ("norm_ffn_post", 1024),
              ("conv_b", 1024), ("ssm_norm_w", 512), ("gm_ln_w", 512), ("gm_ln_b", 512), ("dt_bias", 8), ("a_log", 8),
              ("d_skip", 8))
_SLAB_LOSS_ROW = len(_SLAB_ROWS)
_SLAB_BS_ROW = 16
_SLAB_HEIGHT = 24
_SMALL_PARAMS = tuple(name for name, _ in _SLAB_ROWS) + ("gm_b_s",)
_LN_PARAMS = ("gm_ln_w", "gm_ln_b")


def _pack_slab(g, loss_part):
    rows = [_pad_lanes(g[name], D_MODEL) for name, _ in _SLAB_ROWS]
    rows.append(jnp.broadcast_to(loss_part, (1, D_MODEL)))
    rows.append(jnp.zeros((_SLAB_BS_ROW - len(rows), D_MODEL), F32))
    rows.append(_pad_lanes(g["gm_b_s"].T[0:N_HEADS], D_MODEL))
    return jnp.concatenate(rows, axis=0)


def _adamw_slab(parts, w, m, v):
    names = _SMALL_PARAMS
    shapes = [w[k].shape for k in names]
    unfold = np.zeros((GM_WIDTH, HEAD_DIM), np.float32)
    for h in range(N_HEADS):
        unfold[h * HEAD_DIM:(h + 1) * HEAD_DIM, :] = np.eye(HEAD_DIM)
    unfold = jnp.asarray(unfold, dtype=BF16)
    n = len(names)

    def body(p_ref, unfold_ref, *refs):
        w_refs, m_refs, v_refs = refs[:n], refs[n:2 * n], refs[2 * n:3 * n]
        outs = refs[3 * n:]
        g_all = p_ref[0]
        for j in range(1, N_DEV):
            g_all = g_all + p_ref[j]
        lane = lax.broadcasted_iota(jnp.int32, (N_HEADS, GM_WIDTH), 1)
        head = lax.broadcasted_iota(jnp.int32, (N_HEADS, GM_WIDTH), 0)
        own_lanes = jnp.logical_and(lane >= head * HEAD_DIM, lane < (head + 1) * HEAD_DIM)
        for i, name in enumerate(names):
            if name == "gm_b_s":
                g = g_all[_SLAB_BS_ROW:_SLAB_BS_ROW + N_HEADS, 0:CHUNK]
            else:
                row = [r for r, (k, _) in enumerate(_SLAB_ROWS) if k == name][0]
                g = g_all[row:row + 1, 0:dict(_SLAB_ROWS)[name]]
                if name in _LN_PARAMS:
                    g = _split_dot(jnp.where(own_lanes, g, 0.0), unfold_ref[...], 3)
            d, mn, vn = _adamw_math(w_refs[i][...], g, m_refs[i][...], v_refs[i][...])
            for o_ref, val in zip(outs[4 * i:4 * i + 4], (g, d, mn, vn)):
                o_ref[...] = val
        outs[-1][...] = g_all[_SLAB_LOSS_ROW:_SLAB_LOSS_ROW + 1, 0:128]

    ins = [parts, unfold] + [d[k] for d in (w, m, v) for k in names]
    out_shape = tuple(jax.ShapeDtypeStruct(s, F32) for s in shapes for _ in range(4)) + (
        jax.ShapeDtypeStruct((1, 128), F32),)
    outs = pl.pallas_call(
        body, name="adamw_small", out_shape=out_shape, grid=(1,), in_specs=[_full(a.shape) for a in ins],
        out_specs=tuple(_full(s.shape) for s in out_shape), compiler_params=_params("arbitrary"))(*ins)
    return {k: tuple(outs[4 * i:4 * i + 4]) for i, k in enumerate(names)}, outs[-1][0, 0]


def kernel(x, norm_mix_pre, w_in, gm_ln_w, gm_ln_b, gm_w_s, gm_b_s, conv_w, conv_b, dt_bias, a_log, d_skip, ssm_norm_w, w_out, norm_mix_post, norm_ffn_pre, w_up, w_down, norm_ffn_post, loss_target, m_norm_mix_pre, m_w_in, m_gm_ln_w, m_gm_ln_b, m_gm_w_s, m_gm_b_s, m_conv_w, m_conv_b, m_dt_bias, m_a_log, m_d_skip, m_ssm_norm_w, m_w_out, m_norm_mix_post, m_norm_ffn_pre, m_w_up, m_w_down, m_norm_ffn_post, v_norm_mix_pre, v_w_in, v_gm_ln_w, v_gm_ln_b, v_gm_w_s, v_gm_b_s, v_conv_w, v_conv_b, v_dt_bias, v_a_log, v_d_skip, v_ssm_norm_w, v_w_out, v_norm_mix_post, v_norm_ffn_pre, v_w_up, v_w_down, v_norm_ffn_post):
    w = dict(norm_mix_pre=norm_mix_pre, w_in=w_in, gm_ln_w=gm_ln_w, gm_ln_b=gm_ln_b, gm_w_s=gm_w_s, gm_b_s=gm_b_s, conv_w=conv_w, conv_b=conv_b, dt_bias=dt_bias, a_log=a_log, d_skip=d_skip, ssm_norm_w=ssm_norm_w, w_out=w_out, norm_mix_post=norm_mix_post, norm_ffn_pre=norm_ffn_pre, w_up=w_up, w_down=w_down, norm_ffn_post=norm_ffn_post)
    m = dict(norm_mix_pre=m_norm_mix_pre, w_in=m_w_in, gm_ln_w=m_gm_ln_w, gm_ln_b=m_gm_ln_b, gm_w_s=m_gm_w_s, gm_b_s=m_gm_b_s, conv_w=m_conv_w, conv_b=m_conv_b, dt_bias=m_dt_bias, a_log=m_a_log, d_skip=m_d_skip, ssm_norm_w=m_ssm_norm_w, w_out=m_w_out, norm_mix_post=m_norm_mix_post, norm_ffn_pre=m_norm_ffn_pre, w_up=m_w_up, w_down=m_w_down, norm_ffn_post=m_norm_ffn_post)
    v = dict(norm_mix_pre=v_norm_mix_pre, w_in=v_w_in, gm_ln_w=v_gm_ln_w, gm_ln_b=v_gm_ln_b, gm_w_s=v_gm_w_s, gm_b_s=v_gm_b_s, conv_w=v_conv_w, conv_b=v_conv_b, dt_bias=v_dt_bias, a_log=v_a_log, d_skip=v_d_skip, ssm_norm_w=v_ssm_norm_w, w_out=v_w_out, norm_mix_post=v_norm_mix_post, norm_ffn_pre=v_norm_ffn_pre, w_up=v_w_up, w_down=v_w_down, norm_ffn_post=v_norm_ffn_post)
    n_batch, seq, _ = x.shape
    shard_in = IN_COLS // N_DEV

    me = (4 * lax.axis_index("x") + 2 * lax.axis_index("y") + lax.axis_index("c")).astype(jnp.int32).reshape(1)

    def in_slot(own):
        return lax.dynamic_update_slice(lax.empty((N_DEV,) + own.shape, own.dtype), own[None],
                                        (me[0],) + (0,) * own.ndim)

    w_in_sh, m_in_sh, v_in_sh = w_in[0].T, m_w_in[0].T, v_w_in[0].T
    first = [_cast_to_slot(w_in_sh, me, shard_in, "cast_w_in"), in_slot(conv_w[0]),
             _cast_to_slot(w_out[0], me, 128, "cast_w_out")]
    ici_1, tok_ici_1 = _exchange_start(first, [True] * 3, _SAME_CORE_PEERS, "gather_mix_ici_start")
    cast_up = _cast_to_slot(w_up[0], me, 1024, "cast_w_up", cols=True, dep=tok_ici_1)
    second = [cast_up, _cast_to_slot(w_down[0], me, 512, "cast_w_down", dep=cast_up)]
    gathering = {}

    def mixer_weights(after):
        bufs = [buf for buf, _ in _exchange_wait(ici_1, after, "gather_mix_ici_wait")]
        d2d_1, tok_d2d_1 = _exchange_start(bufs, [True] * 3, _SIBLING_FORWARD, "gather_mix_d2d_start")
        gathering["mlp_ici"], tok_ici_2 = _exchange_start(
            second, [True] * 2, _SAME_CORE_PEERS, "gather_mlp_ici_start", dep=tok_d2d_1)
        (_, ag_in), (_, ag_conv), (_, ag_out) = _exchange_wait(d2d_1, tok_ici_2, "gather_mix_d2d_wait")
        gathering["w_out"] = ag_out.reshape(D_MODEL, D_MODEL)
        w_in_t = jnp.pad(ag_in.reshape(IN_COLS, D_MODEL), ((0, IN_PAD - IN_COLS), (0, 0)))
        return w_in_t, ag_conv.transpose(1, 0, 2).reshape(4, CONV_CH)

    def mixers_done(after):
        bufs = [buf for buf, _ in _exchange_wait(gathering["mlp_ici"], after, "gather_mlp_ici_wait")]
        gathering["mlp"], tok = _exchange_start(bufs, [True] * 2, _SIBLING_FORWARD, "gather_mlp_d2d_start")
        return gathering["w_out"], tok

    def mlp_weights(after):
        (_, ag_up), (_, ag_down) = _exchange_wait(gathering["mlp"], after, "gather_mlp_d2d_wait")
        return ag_up, ag_down.reshape(D_FF, D_MODEL)

    sent = {}

    def mlp_grads(g_w_down, g_w_up):
        sent["mlp"], tok = _exchange_start(
            [g_w_down.reshape(N_DEV, D_FF // N_DEV, D_MODEL), g_w_up], [False, False], _ALL_PEERS, "grads_mlp_start")
        return tok

    def gmlp_grads(g_w_out, g_w_s):
        sent["gmlp"], tok = _exchange_start(
            [g_w_out.reshape(N_DEV, D_MODEL // N_DEV, D_MODEL), in_slot(g_w_s.astype(BF16))], [False, True], _ALL_PEERS,
            "grads_gmlp_start")
        return tok

    def in_grads(g_w_in_t, g_conv_w):
        g_in_blk = g_w_in_t[:IN_COLS].reshape(N_DEV, shard_in, D_MODEL)
        g_conv_blk = g_conv_w.reshape(4, N_DEV, CONV_CH // N_DEV).transpose(1, 0, 2)
        sent["in"], tok = _exchange_start([g_in_blk, g_conv_blk], [False, False], _ALL_PEERS, "grads_in_start")
        return tok

    small = {k: w[k][0] for k in _SMALL_PARAMS + ("gm_w_s",)}
    loss_part, grad_x, g = _local_step(
        x.reshape(n_batch * seq, D_MODEL), loss_target.reshape(n_batch * seq, D_MODEL), seq, small,
        dict(mixer_weights=mixer_weights, mixers_done=mixers_done, mlp_weights=mlp_weights, mlp_grads=mlp_grads,
             gmlp_grads=gmlp_grads, in_grads=in_grads, prenorm_after=second[1]), first_dep=tok_ici_1)

    sent_rows, tok_rows = _exchange_start([in_slot(_pack_slab(g, loss_part))], [True], _ALL_PEERS, "grads_rows_start")
    (own_down, p_down), (own_up, p_up) = _exchange_wait(sent["mlp"], tok_rows, "grads_mlp_wait")
    res = {}
    res["w_up"] = _adamw_reduce(p_up, own_up, me, w_up[0], m_w_up[0], v_w_up[0], 256, "adamw_w_up")
    res["w_down"] = _adamw_reduce(p_down, own_down, me, w_down[0], m_w_down[0], v_w_down[0], 128, "adamw_w_down")
    (own_out, p_out), (_, p_ws) = _exchange_wait(sent["gmlp"], res["w_down"][1], "grads_gmlp_wait")
    res["w_out"] = _adamw_reduce(p_out, own_out, me, w_out[0], m_w_out[0], v_w_out[0], 128, "adamw_w_out")
    causal = jnp.tril(jnp.ones((1, CHUNK, CHUNK), F32))
    res["gm_w_s"] = _adamw_small(p_ws, None, me, gm_w_s[0], m_gm_w_s[0], v_gm_w_s[0], causal, "adamw_gm_w_s")
    (own_in, p_in), (own_conv, p_conv) = _exchange_wait(sent["in"], res["gm_w_s"][1], "grads_in_wait")
    res["w_in"] = tuple(r.T for r in _adamw_reduce(p_in, own_in, me, w_in_sh, m_in_sh, v_in_sh, shard_in, "adamw_w_in"))
    res["conv_w"] = _adamw_small(p_conv, own_conv, me, conv_w[0], m_conv_w[0], v_conv_w[0], None, "adamw_conv_w")
    ((_, p_rows),) = _exchange_wait(sent_rows, res["w_in"][1], "grads_rows_wait")
    flat = lambda t: t[0] if t.ndim == 3 else t
    small_res, loss = _adamw_slab(p_rows, *({k: flat(d[k]) for k in _SMALL_PARAMS} for d in (w, m, v)))
    res.update(small_res)
    res = {k: tuple(r.reshape(w[k].shape) for r in res[k]) for k in _WEIGHTS}

    outs = [loss, grad_x.reshape(x.shape)]
    for part in range(4):
        outs.extend(res[k][part] for k in _WEIGHTS)
    return tuple(outs)
```

```python
import functools

import jax
import jax.numpy as jnp
import numpy as np
from jax import lax
from jax.experimental import pallas as pl
from jax.experimental.pallas import tpu as pltpu

F32 = jnp.float32
BF16 = jnp.bfloat16

D_MODEL = 1024
GM_WIDTH = 512
SSM_WIDTH = 512
CONV_CH = 1024
N_HEADS = 8
HEAD_DIM = 64
N_STATE = 128
CHUNK = 128
D_FF = 4096
IN_COLS = 2568
IN_PAD = 2688
N_DEV = 8
EPS = 1e-6
ADAM_LR, ADAM_B1, ADAM_B2, ADAM_EPS, ADAM_WD, ADAM_STEP = 0.001, 0.9, 0.999, 1e-08, 0.01, 10
VMEM_LIMIT_BYTES = 56 * 1024 * 1024
SMALL_ROWS = 16

_NT = (((1,), (1,)), ((), ()))
_TN = (((0,), (0,)), ((), ()))


def _params(*sem):
    return pltpu.CompilerParams(dimension_semantics=sem or None, vmem_limit_bytes=VMEM_LIMIT_BYTES)


def _dot(a, b, dims=None):
    if dims is None:
        return jnp.dot(a, b, preferred_element_type=F32)
    return lax.dot_general(a, b, dims, preferred_element_type=F32)


def _split_terms(x, terms):
    out, rem = [], x
    for i in range(terms):
        hi = rem.astype(BF16)
        out.append(hi)
        if i + 1 < terms:
            rem = rem - hi.astype(F32)
    return out


def _split_dot(x, m, terms):
    acc = None
    for hi in _split_terms(x, terms):
        part = _dot(hi, m)
        acc = part if acc is None else acc + part
    return acc


def _split_dot_left(m, x, terms):
    acc = None
    for hi in _split_terms(x, terms):
        part = _dot(m, hi)
        acc = part if acc is None else acc + part
    return acc


def _gelu_and_grad(x):
    c = 0.7978845608028654
    inner = c * (x + 0.044715 * x * x * x)
    t = jnp.tanh(inner)
    g = 0.5 * x * (1.0 + t)
    dg = 0.5 * (1.0 + t) + 0.5 * x * (1.0 - t * t) * c * (1.0 + 3.0 * 0.044715 * x * x)
    return g, dg


def _softplus(x):
    return jnp.maximum(x, 0.0) + jnp.log(1.0 + jnp.exp(-jnp.abs(x)))


def _rsum(x):
    return jnp.sum(x, axis=0, keepdims=True)


def _acc_rows(ref, part, first):
    val = jnp.broadcast_to(part, ref.shape)

    @pl.when(first)
    def _():
        ref[...] = val

    @pl.when(jnp.logical_not(first))
    def _():
        ref[...] += val


def _rms_bwd(n, g, dout):
    r = lax.rsqrt(jnp.mean(n * n, axis=-1, keepdims=True) + EPS)
    nh = n * r
    dg = dout * g
    dn = r * (dg - nh * jnp.mean(dg * nh, axis=-1, keepdims=True))
    return dn, _rsum(dout * nh)


def _const_mats():
    avg = np.kron(np.eye(4), np.full((HEAD_DIM, HEAD_DIM), 1.0 / HEAD_DIM))
    expand = np.zeros((CHUNK, SSM_WIDTH), np.float32)
    for h in range(N_HEADS):
        expand[h, h * HEAD_DIM:(h + 1) * HEAD_DIM] = 1.0
    tril = np.tril(np.ones((CHUNK, CHUNK), np.float32))
    as_bf16 = lambda a: jnp.asarray(a, dtype=BF16)
    return as_bf16(avg), as_bf16(expand), as_bf16(expand.T), as_bf16(tril), as_bf16(tril.T)


def _full(shape):
    nd = len(shape)
    return pl.BlockSpec(shape, lambda *_: (0,) * nd)


_HBM = pl.BlockSpec(memory_space=pltpu.HBM)
_SEM = pl.BlockSpec(memory_space=pltpu.SEMAPHORE)
_ALL_PEERS = tuple((k, 0) for k in range(1, N_DEV))
_SAME_CORE_PEERS = ((2, 0), (4, 0), (6, 0))
_SIBLING_FORWARD = ((1, 0), (1, 2), (1, 4), (1, 6))


def _flip(j, k):
    for bit in (4, 2, 1):
        if k & bit:
            j = j + bit - 2 * (j & bit)
    return j


def _copies(src, land, send_sems, recv_sems, hops):
    x, y, c = lax.axis_index("x"), lax.axis_index("y"), lax.axis_index("c")
    me = 4 * x + 2 * y + c
    out = []
    for t in range(len(src)):
        for i, (k, b) in enumerate(hops):
            pos = (1 - x if k & 4 else x, 1 - y if k & 2 else y, 1 - c if k & 1 else c)
            peer = _flip(me, k)
            sem = t * len(hops) + i
            mk = functools.partial(pltpu.make_async_remote_copy, send_sem=send_sems.at[sem], recv_sem=recv_sems.at[sem],
                                   device_id=pos, device_id_type=pl.DeviceIdType.MESH)
            if land[t] is None and src[t].shape[0] != N_DEV:
                width = src[t].shape[1] // N_DEV
                slab = lambda j: src[t].at[:, pl.ds(pl.multiple_of(j * width, 128), width)]
                mine = functools.partial(mk, src_ref=slab(_flip(me, b)), dst_ref=slab(_flip(me, b)))
                theirs = functools.partial(mk, src_ref=slab(_flip(peer, b)), dst_ref=slab(_flip(peer, b)))
            elif land[t] is None:
                mine = functools.partial(mk, src_ref=src[t].at[_flip(me, b)], dst_ref=src[t].at[_flip(me, b)])
                theirs = functools.partial(mk, src_ref=src[t].at[_flip(peer, b)], dst_ref=src[t].at[_flip(peer, b)])
            else:
                assert b == 0
                mine = functools.partial(mk, src_ref=src[t].at[peer], dst_ref=land[t].at[me])
                theirs = functools.partial(mk, src_ref=src[t].at[peer], dst_ref=land[t].at[peer])
            out.append((mine, theirs))
    return out


def _exchange_start(srcs, inplace, peers, name, dep=None):
    n = len(srcs)
    lands = [None if ip else pltpu.with_memory_space_constraint(lax.empty(s.shape, s.dtype), pltpu.HBM)
             for s, ip in zip(srcs, inplace)]
    real_lands = [l for l in lands if l is not None]
    n_l = len(real_lands)
    deps = [] if dep is None else [dep]

    def body(*refs):
        src = refs[:n]
        land_refs = list(refs[n:n + n_l])
        send_sems, recv_sems = refs[n + n_l + len(deps)], refs[n + n_l + len(deps) + 1]
        token = refs[-1]
        land = [None if ip else land_refs.pop(0) for ip in inplace]
        for mine, _ in _copies(src, land, send_sems, recv_sems, peers):
            mine().start()
        token[...] = jnp.zeros_like(token)

    sem_t = pltpu.SemaphoreType.DMA((n * len(peers),))
    outs = pl.pallas_call(
        body, name=name,
        out_shape=(sem_t, sem_t) + tuple(pltpu.HBM(a.shape, a.dtype) for a in list(srcs) + real_lands)
        + (jax.ShapeDtypeStruct((8, 128), F32),),
        in_specs=[_HBM] * (n + n_l) + [pl.BlockSpec(memory_space=pl.ANY)] * len(deps),
        out_specs=(_SEM, _SEM) + (_HBM,) * (n + n_l) + (pl.BlockSpec(memory_space=pltpu.VMEM),),
        input_output_aliases={i: 2 + i for i in range(n + n_l)},
        compiler_params=pltpu.CompilerParams(has_side_effects=pltpu.SideEffectType.DATAFLOW_SIDE_EFFECTING),
    )(*[pltpu.with_memory_space_constraint(s, pltpu.HBM) for s in srcs], *real_lands, *deps)
    handle = dict(send=outs[0], recv=outs[1], srcs=outs[2:2 + n], lands=outs[2 + n:2 + n + n_l], inplace=inplace,
                  peers=peers)
    return handle, outs[-1]


def _exchange_wait(handle, after, name):
    srcs, lands, inplace, peers = handle["srcs"], handle["lands"], handle["inplace"], handle["peers"]
    n, n_l = len(srcs), len(lands)

    def body(*refs):
        src = refs[:n]
        land_refs = list(refs[n:n + n_l])
        send_sems, recv_sems = refs[n + n_l], refs[n + n_l + 1]
        land = [None if ip else land_refs.pop(0) for ip in inplace]
        for mine, theirs in _copies(src, land, send_sems, recv_sems, peers):
            mine().wait_send()
            theirs().wait_recv()

    outs = pl.pallas_call(
        body, name=name, out_shape=tuple(pltpu.HBM(a.shape, a.dtype) for a in list(srcs) + list(lands)),
        in_specs=[_HBM] * (n + n_l) + [_SEM, _SEM, pl.BlockSpec(memory_space=pl.ANY)],
        out_specs=(_HBM,) * (n + n_l), input_output_aliases={i: i for i in range(n + n_l)},
        compiler_params=pltpu.CompilerParams(has_side_effects=pltpu.SideEffectType.DATAFLOW_SIDE_EFFECTING),
    )(*srcs, *lands, handle["send"], handle["recv"], after)
    res, land_out = [], list(outs[n:])
    for t in range(n):
        res.append((outs[t], outs[t] if inplace[t] else land_out.pop(0)))
    return res


def _cast_to_slot(w, me, rows, name, cols=False, dep=None):
    r, cdim = w.shape
    deps = [] if dep is None else [dep]

    def body(me_ref, w_ref, *rest):
        o_ref = rest[-1]
        if cols:
            o_ref[...] = w_ref[...].astype(BF16)
        else:
            o_ref[0] = w_ref[...].astype(BF16)

    if cols:
        out_shape = jax.ShapeDtypeStruct((r, N_DEV * cdim), BF16)
        out_spec = pl.BlockSpec((rows, cdim), lambda i, me_ref: (i, me_ref[0]))
    else:
        out_shape = jax.ShapeDtypeStruct((N_DEV, r, cdim), BF16)
        out_spec = pl.BlockSpec((1, rows, cdim), lambda i, me_ref: (me_ref[0], i, 0))
    return pl.pallas_call(
        body, name=name, out_shape=out_shape,
        grid_spec=pltpu.PrefetchScalarGridSpec(
            num_scalar_prefetch=1, grid=(r // rows,),
            in_specs=[pl.BlockSpec((rows, cdim), lambda i, me_ref: (i, 0))]
            + [pl.BlockSpec(memory_space=pl.ANY)] * len(deps), out_specs=out_spec),
        compiler_params=_params("parallel"))(me, w, *deps)


def _adamw_math(w, g, m, v):
    m = ADAM_B1 * m + (1.0 - ADAM_B1) * g
    v = ADAM_B2 * v + (1.0 - ADAM_B2) * (g * g)
    m_hat = m / (1.0 - ADAM_B1 ** ADAM_STEP)
    v_hat = v / (1.0 - ADAM_B2 ** ADAM_STEP)
    delta = -ADAM_LR * (m_hat / (jnp.sqrt(v_hat) + ADAM_EPS) + ADAM_WD * w)
    return delta, m, v


def _sum_parts(me, p_ref, own):
    g = None
    for j in range(N_DEV):
        term = (p_ref[j] if own is None else jnp.where(me == j, own, p_ref[j])).astype(F32)
        g = term if g is None else g + term
    return g


def _adamw_reduce(parts, own, me, w, m, v, rows, name):
    r, cdim = w.shape

    def body(me_ref, p_ref, own_ref, w_ref, m_ref, v_ref, g_out, d_out, m_out, v_out):
        g = _sum_parts(me_ref[0], p_ref, own_ref[0])
        d, mn, vn = _adamw_math(w_ref[...], g, m_ref[...], v_ref[...])
        g_out[...] = g
        d_out[...] = d
        m_out[...] = mn
        v_out[...] = vn

    blk = pl.BlockSpec((rows, cdim), lambda i, me_ref: (i, 0))
    sds = jax.ShapeDtypeStruct(w.shape, F32)
    return pl.pallas_call(
        body, name=name, out_shape=(sds,) * 4,
        grid_spec=pltpu.PrefetchScalarGridSpec(
            num_scalar_prefetch=1, grid=(r // rows,),
            in_specs=[pl.BlockSpec((N_DEV, rows, cdim), lambda i, me_ref: (0, i, 0)),
                      pl.BlockSpec((1, rows, cdim), lambda i, me_ref: (me_ref[0], i, 0)), blk, blk, blk],
            out_specs=(blk,) * 4),
        compiler_params=_params("parallel"))(me, parts, own, w, m, v)


def _adamw_small(parts, own, me, w, m, v, mask, name):
    def body(me_ref, *refs):
        refs = list(refs)
        p_ref = refs.pop(0)
        own_ref = None if own is None else refs.pop(0)
        w_ref, m_ref, v_ref = refs[:3]
        k_ref = None if mask is None else refs[3]
        g_out, d_out, m_out, v_out = refs[-4:]
        g = _sum_parts(me_ref[0], p_ref, None if own is None else own_ref[me_ref[0]])
        if mask is not None:
            g = g * k_ref[...]
        d, mn, vn = _adamw_math(w_ref[...], g, m_ref[...], v_ref[...])
        g_out[...] = g
        d_out[...] = d
        m_out[...] = mn
        v_out[...] = vn

    def whole(shape):
        nd = len(shape)
        return pl.BlockSpec(shape, lambda i, me_ref: (0,) * nd)

    sds = jax.ShapeDtypeStruct(w.shape, F32)
    ins = [parts] + ([] if own is None else [own]) + [w, m, v] + ([] if mask is None else [mask])
    return pl.pallas_call(
        body, name=name, out_shape=(sds,) * 4,
        grid_spec=pltpu.PrefetchScalarGridSpec(
            num_scalar_prefetch=1, grid=(1,), in_specs=[whole(a.shape) for a in ins],
            out_specs=(whole(w.shape),) * 4),
        compiler_params=_params("arbitrary"))(me, *ins)


_IN_SPLITS = ((0, 512), (512, 1024), (1024, 1536), (1536, 2560), (2560, IN_PAD))


def _prenorm(x, g1, tm, dep=None):
    t_tok = x.shape[0]
    deps = [] if dep is None else [dep]

    def body(x_ref, g_ref, *rest):
        xv = x_ref[...]
        r = lax.rsqrt(jnp.mean(xv * xv, axis=-1, keepdims=True) + EPS)
        rest[-1][...] = (xv * r * g_ref[...]).astype(BF16)

    row = pl.BlockSpec((tm, D_MODEL), lambda i: (i, 0))
    return pl.pallas_call(
        body, name="prenorm", grid=(t_tok // tm,), out_shape=jax.ShapeDtypeStruct((t_tok, D_MODEL), BF16),
        in_specs=[row, _full((1, D_MODEL))] + [pl.BlockSpec(memory_space=pl.ANY)] * len(deps), out_specs=row,
        compiler_params=_params("parallel"))(x, g1, *deps)


def _in_proj(h1, w_in, tm):
    t_tok = h1.shape[0]

    def body(h_ref, w_ref, *outs):
        h = h_ref[...]
        for (a, b), o_ref in zip(_IN_SPLITS, outs):
            o_ref[...] = _dot(h, w_ref[a:b, :], _NT).astype(o_ref.dtype)

    row = lambda n: pl.BlockSpec((tm, n), lambda i: (i, 0))
    widths = [b - a for a, b in _IN_SPLITS]
    dtypes = (BF16, BF16, BF16, F32, F32)
    return pl.pallas_call(
        body, name="in_proj", grid=(t_tok // tm,),
        out_shape=tuple(jax.ShapeDtypeStruct((t_tok, n), dt) for n, dt in zip(widths, dtypes)),
        in_specs=[row(D_MODEL), _full((IN_PAD, D_MODEL))], out_specs=tuple(row(n) for n in widths),
        compiler_params=_params("parallel"))(h1, w_in)


def _lane_masks():
    lane = lax.broadcasted_iota(jnp.int32, (1, 2 * HEAD_DIM), 1)
    left = (lane < HEAD_DIM).astype(F32)
    return left, 1.0 - left


def _stack_pair(v, m_l, m_r):
    return jnp.concatenate([v * m_l, v * m_r], axis=0).astype(BF16)


def _head_mean(x, avg):
    n = avg.shape[0]
    return jnp.concatenate([_split_dot(x[:, n * i:n * (i + 1)], avg, 2) for i in range(x.shape[1] // n)], axis=1)


def _gmlp_common(u, v, lnw, lnb, avg, wcat_ref, bias, m_l, m_r):
    ug, dug = _gelu_and_grad(u)
    vg, dvg = _gelu_and_grad(v)
    mu = _head_mean(vg, avg)
    vc = vg - mu
    var = _head_mean(vc * vc, avg)
    rstd = lax.rsqrt(var + EPS)
    vhat = vc * rstd
    vn = vhat * lnw + lnb
    rows = []
    for r in range(u.shape[0] // CHUNK):
        cols = []
        for j in range(N_HEADS // 2):
            pair = vn[CHUNK * r:CHUNK * (r + 1), 128 * j:128 * (j + 1)]
            cols.append(_dot(wcat_ref[j], _stack_pair(pair, m_l, m_r)))
        rows.append(jnp.concatenate(cols, axis=1) + bias)
    mixed = jnp.concatenate(rows, axis=0)
    return ug, dug, dvg, rstd, vhat, vn, mixed


_GMLP_ROWS = 4 * CHUNK


def _gmlp_fwd(u, v, lnw, lnb, wcat, bias, avg):
    t_tok = u.shape[0]
    tm = min(_GMLP_ROWS, t_tok)

    def body(u_ref, v_ref, lnw_ref, lnb_ref, wcat_ref, bias_ref, avg_ref, o_ref):
        m_l, m_r = _lane_masks()
        ug, _, _, _, _, _, mixed = _gmlp_common(
            u_ref[...].astype(F32), v_ref[...].astype(F32), lnw_ref[...], lnb_ref[...], avg_ref[...], wcat_ref,
            bias_ref[...], m_l, m_r)
        o_ref[...] = (ug * mixed).astype(BF16)

    row = pl.BlockSpec((tm, GM_WIDTH), lambda i: (i, 0))
    return pl.pallas_call(
        body, name="gmlp_fwd", grid=(t_tok // tm,), out_shape=jax.ShapeDtypeStruct((t_tok, GM_WIDTH), BF16),
        in_specs=[row, row, _full((1, GM_WIDTH)), _full((1, GM_WIDTH)), _full(wcat.shape), _full(bias.shape),
                  _full(avg.shape)],
        out_specs=row, compiler_params=_params("parallel"))(u, v, lnw, lnb, wcat, bias, avg)


def _shift_rows(x, edge, j, down):
    groups, cols = x.shape[0] // 8, x.shape[1]
    amount = j if down else 8 - j
    rot = pltpu.roll(x.reshape(groups, 8, cols), amount, axis=1)
    edge_rot = pltpu.roll(edge, amount, axis=0)[None]
    sub = lax.broadcasted_iota(jnp.int32, (1, 8, 1), 1)
    if down:
        out = jnp.where(sub < j, jnp.concatenate([edge_rot, rot[:-1]], axis=0), rot)
    else:
        out = jnp.where(sub < 8 - j, rot, jnp.concatenate([rot[1:], edge_rot], axis=0))
    return out.reshape(x.shape)


def _conv_pre(xbc, tail, cw_ref, cb):
    taps = [_shift_rows(xbc, tail, 3 - k, True) for k in range(3)] + [xbc]
    return cb + cw_ref[0:1, :] * taps[0] + cw_ref[1:2, :] * taps[1] + cw_ref[2:3, :] * taps[2] + cw_ref[3:4, :] * taps[3]


def _ssd_common(pre, dtr, dtb, alog, expand, tril):
    q = CHUNK
    sg = jax.nn.sigmoid(pre)
    act = pre * sg
    lane = lax.broadcasted_iota(jnp.int32, (1, CHUNK), 1)
    a_row = jnp.where(lane < N_HEADS, -jnp.exp(alog), 0.0)
    dtp = dtr + dtb
    dt = _softplus(dtp)
    a_cs = _split_dot_left(tril, dt * a_row, 3)
    a_cs_t = a_cs.T
    dt_exp = _split_dot(dt, expand, 3)
    a_exp = _split_dot(a_cs, expand, 3)
    a_end = a_exp[q - 1:q, :]
    li = lax.broadcasted_iota(jnp.int32, (q, q), 0)
    si = lax.broadcasted_iota(jnp.int32, (q, q), 1)
    causal = si <= li
    decay = []
    for h in range(N_HEADS):
        seg = a_cs[:, h:h + 1] - a_cs_t[h:h + 1, :]
        decay.append(jnp.where(causal, jnp.exp(jnp.minimum(seg, 0.0)), 0.0))
    return dict(pre=pre, sg=sg, act=act, a_row=a_row, dtp=dtp, dt=dt, dt_exp=dt_exp, a_exp=a_exp,
                e=jnp.exp(a_exp), w_end=jnp.exp(a_end - a_exp), cd=jnp.exp(a_end), decay=decay)


def _ssd_specs(t_tok, seq, reverse):
    nb, nc = t_tok // seq, seq // CHUNK

    def chunk(c):
        return nc - 1 - c if reverse else c

    def row(n, col=0):
        return pl.BlockSpec((nb, CHUNK, n), lambda c: (0, chunk(c), col))

    tail = pl.BlockSpec((nb, 8, CONV_CH), lambda c: (0, jnp.maximum(chunk(c) * (CHUNK // 8) - 1, 0), 0))
    states = pl.BlockSpec((nb, 1, N_STATE, SSM_WIDTH), lambda c: (0, chunk(c), 0, 0))
    fold = lambda a: a.reshape(nb, seq, a.shape[-1])
    unfold = lambda a: a.reshape(t_tok, a.shape[-1])
    return nb, nc, row, tail, states, fold, unfold


def _ssd_fwd(z, xbc, dtr, cw, cb, dtb, alog, dskip_exp, nw, expand, tril, seq):
    t_tok = z.shape[0]
    nb, nc, row, tail, states_spec, fold, unfold = _ssd_specs(t_tok, seq, False)

    def body(z_ref, xbc_ref, tail_ref, dtr_ref, cw_ref, cb_ref, dtb_ref, alog_ref, dsk_ref, nw_ref, exp_ref,
             tril_ref, o_ref, y_ref, st_ref, pre_ref, state_ref):
        c = pl.program_id(0)

        @pl.when(c == 0)
        def _():
            state_ref[...] = jnp.zeros_like(state_ref)

        m_l, m_r = _lane_masks()
        for s in range(nb):
            pre = _conv_pre(xbc_ref[s], jnp.where(c == 0, 0.0, tail_ref[s]), cw_ref, cb_ref[...])
            pre_ref[s] = pre
            f = _ssd_common(pre, dtr_ref[s], dtb_ref[...], alog_ref[...], exp_ref[...], tril_ref[...])
            act = f["act"]
            xs = act[:, :SSM_WIDTH]
            xdt = xs * f["dt_exp"]
            xw = xdt * f["w_end"]
            state = state_ref[s]
            st_ref[s, 0] = state
            ydiag, yoff, snew = [], [], []
            for g in range(2):
                bg = act[:, 512 + 128 * g:640 + 128 * g].astype(BF16)
                cg = act[:, 768 + 128 * g:896 + 128 * g].astype(BF16)
                cb_mat = _dot(cg, bg, _NT)
                for pr in range(2):
                    h0 = 4 * g + 2 * pr
                    gcat = jnp.concatenate(
                        [(cb_mat * f["decay"][h0]).astype(BF16), (cb_mat * f["decay"][h0 + 1]).astype(BF16)], axis=1)
                    ydiag.append(_dot(gcat, _stack_pair(xdt[:, 64 * h0:64 * h0 + 128], m_l, m_r)))
                yoff.append(_dot(cg, state[:, 256 * g:256 * (g + 1)].astype(BF16)))
                snew.append(_dot(bg, xw[:, 256 * g:256 * (g + 1)].astype(BF16), _TN))
            y = jnp.concatenate(ydiag, axis=1) + f["e"] * jnp.concatenate(yoff, axis=1) + dsk_ref[...] * xs
            state_ref[s] = state * f["cd"] + jnp.concatenate(snew, axis=1)
            y_ref[s] = y
            zv = z_ref[s].astype(F32)
            yg = y * (zv * jax.nn.sigmoid(zv))
            outs = []
            for g in range(2):
                ygg = yg[:, 256 * g:256 * (g + 1)]
                outs.append(ygg * lax.rsqrt(jnp.mean(ygg * ygg, axis=-1, keepdims=True) + EPS))
            o_ref[s] = (jnp.concatenate(outs, axis=1) * nw_ref[...]).astype(BF16)

    consts = [cw, cb, dtb, alog, dskip_exp, nw, expand, tril]
    sd = lambda n, dt: jax.ShapeDtypeStruct((nb, seq, n), dt)
    o, y, states, pre = pl.pallas_call(
        body, name="ssd_fwd", grid=(nc,),
        out_shape=(sd(SSM_WIDTH, BF16), sd(SSM_WIDTH, F32), jax.ShapeDtypeStruct((nb, nc, N_STATE, SSM_WIDTH), F32),
                   sd(CONV_CH, F32)),
        in_specs=[row(SSM_WIDTH), row(CONV_CH), tail, row(CHUNK)] + [_full(a.shape) for a in consts],
        out_specs=(row(SSM_WIDTH), row(SSM_WIDTH), states_spec, row(CONV_CH)),
        scratch_shapes=[pltpu.VMEM((nb, N_STATE, SSM_WIDTH), F32)],
        compiler_params=_params("arbitrary"))(fold(z), fold(xbc), fold(xbc), fold(dtr), *consts)
    return unfold(o), unfold(y), states, unfold(pre)


def _out_proj(mix_a, mix_b, w_out, x, g2, g3, tm, dep=None):
    t_tok = x.shape[0]
    deps = [] if dep is None else [dep]

    def body(a_ref, b_ref, w_ref, x_ref, g2_ref, g3_ref, *rest):
        o_ref, x2_ref, h3_ref, mix_ref = rest[-4:]
        o = _dot(a_ref[...], w_ref[0:GM_WIDTH, :]) + _dot(b_ref[...], w_ref[GM_WIDTH:, :])
        o_ref[...] = o
        mix_ref[:, 0:GM_WIDTH] = a_ref[...]
        mix_ref[:, GM_WIDTH:] = b_ref[...]
        r2 = lax.rsqrt(jnp.mean(o * o, axis=-1, keepdims=True) + EPS)
        x2 = x_ref[...] + o * r2 * g2_ref[...]
        x2_ref[...] = x2
        r3 = lax.rsqrt(jnp.mean(x2 * x2, axis=-1, keepdims=True) + EPS)
        h3_ref[...] = (x2 * r3 * g3_ref[...]).astype(BF16)

    row = lambda n: pl.BlockSpec((tm, n), lambda i: (i, 0))
    sd = lambda dt: jax.ShapeDtypeStruct((t_tok, D_MODEL), dt)
    return pl.pallas_call(
        body, name="out_proj", grid=(t_tok // tm,), out_shape=(sd(F32), sd(F32), sd(BF16), sd(BF16)),
        in_specs=[row(GM_WIDTH), row(SSM_WIDTH), _full((D_MODEL, D_MODEL)), row(D_MODEL), _full((1, D_MODEL)),
                  _full((1, D_MODEL))] + [pl.BlockSpec(memory_space=pl.ANY)] * len(deps),
        out_specs=(row(D_MODEL),) * 4, compiler_params=_params("parallel"))(mix_a, mix_b, w_out, x, g2, g3, *deps)


def _mlp_fwd(h3, w_up, w_down, x2, target, g4, tm, tf):
    t_tok = x2.shape[0]

    def up_body(h_ref, wu_ref, ra_ref):
        ra_ref[...] = jnp.maximum(_dot(h_ref[...], wu_ref[...]), 0.0).astype(BF16)

    ra = pl.pallas_call(
        up_body, name="mlp_up", grid=(D_FF // tf, t_tok // tm), out_shape=jax.ShapeDtypeStruct((t_tok, D_FF), BF16),
        in_specs=[pl.BlockSpec((tm, D_MODEL), lambda j, i: (i, 0)), pl.BlockSpec((D_MODEL, tf), lambda j, i: (0, j))],
        out_specs=pl.BlockSpec((tm, tf), lambda j, i: (i, j)), compiler_params=_params("parallel", "parallel"))(h3, w_up)

    def down_body(ra_ref, wd_ref, x2_ref, t_ref, g4_ref, dd_ref, dy_ref, dg4_ref, loss_ref):
        i = pl.program_id(0)
        rav = ra_ref[...]
        dvec = _dot(rav * rav, wd_ref[...])
        r4 = lax.rsqrt(jnp.mean(dvec * dvec, axis=-1, keepdims=True) + EPS)
        dn = dvec * r4
        g4 = g4_ref[...]
        err = x2_ref[...] + dn * g4 - t_ref[...]
        dy = err * (1.0 / D_MODEL)
        dy_ref[...] = dy
        dg = dy * g4
        dd_ref[...] = (r4 * (dg - dn * jnp.mean(dg * dn, axis=-1, keepdims=True))).astype(BF16)
        _acc_rows(dg4_ref, _rsum(dy * dn), i == 0)
        tile_loss = 0.5 * jnp.sum(jnp.sum(err * err, axis=-1, keepdims=True), axis=0, keepdims=True) / D_MODEL
        _acc_rows(loss_ref, jnp.broadcast_to(tile_loss, (1, 128)), i == 0)

    row = pl.BlockSpec((tm, D_MODEL), lambda i: (i, 0))
    dd, dy, dg4, loss = pl.pallas_call(
        down_body, name="mlp_down", grid=(t_tok // tm,),
        out_shape=(jax.ShapeDtypeStruct((t_tok, D_MODEL), BF16), jax.ShapeDtypeStruct((t_tok, D_MODEL), F32),
                   jax.ShapeDtypeStruct((1, D_MODEL), F32), jax.ShapeDtypeStruct((1, 128), F32)),
        in_specs=[pl.BlockSpec((tm, D_FF), lambda i: (i, 0)), _full((D_FF, D_MODEL)), row, row, _full((1, D_MODEL))],
        out_specs=(row, row, _full((1, D_MODEL)), _full((1, 128))),
        compiler_params=_params("arbitrary"))(ra, w_down, x2, target, g4)
    return ra, dd, dy, dg4, loss


def _mlp_bwd(dd, w_down, ra, w_up, x2, dy, o, g3, g2, tm, tf):
    t_tok = x2.shape[0]

    def hidden_body(dd_ref, wd_ref, ra_ref, da_ref):
        df = _dot(dd_ref[...], wd_ref[...], _NT)
        da_ref[...] = (df * (2.0 * ra_ref[...].astype(F32))).astype(BF16)

    da = pl.pallas_call(
        hidden_body, name="mlp_bwd_hidden", grid=(D_FF // tf, t_tok // tm),
        out_shape=jax.ShapeDtypeStruct((t_tok, D_FF), BF16),
        in_specs=[pl.BlockSpec((tm, D_MODEL), lambda j, i: (i, 0)), pl.BlockSpec((tf, D_MODEL), lambda j, i: (j, 0)),
                  pl.BlockSpec((tm, tf), lambda j, i: (i, j))],
        out_specs=pl.BlockSpec((tm, tf), lambda j, i: (i, j)),
        compiler_params=_params("parallel", "parallel"))(dd, w_down, ra)

    def in_body(da_ref, wu_ref, x2_ref, dy_ref, o_ref, g3_ref, g2_ref, dx2_ref, do_ref, dg3_ref, dg2_ref):
        i = pl.program_id(0)
        dh3 = _dot(da_ref[...], wu_ref[...], _NT)
        dn3, dg3 = _rms_bwd(x2_ref[...], g3_ref[...], dh3)
        dx2 = dy_ref[...] + dn3
        dx2_ref[...] = dx2
        do, dg2 = _rms_bwd(o_ref[...], g2_ref[...], dx2)
        do_ref[...] = do.astype(BF16)
        _acc_rows(dg3_ref, dg3, i == 0)
        _acc_rows(dg2_ref, dg2, i == 0)

    row = pl.BlockSpec((tm, D_MODEL), lambda i: (i, 0))
    vec = _full((1, D_MODEL))
    sd = lambda dt: jax.ShapeDtypeStruct((t_tok, D_MODEL), dt)
    dx2, do, dg3, dg2 = pl.pallas_call(
        in_body, name="mlp_bwd_in", grid=(t_tok // tm,),
        out_shape=(sd(F32), sd(BF16), jax.ShapeDtypeStruct((1, D_MODEL), F32), jax.ShapeDtypeStruct((1, D_MODEL), F32)),
        in_specs=[pl.BlockSpec((tm, D_FF), lambda i: (i, 0)), _full((D_MODEL, D_FF)), row, row, row, vec, vec],
        out_specs=(row, row, vec, vec), compiler_params=_params("arbitrary"))(da, w_up, x2, dy, o, g3, g2)
    return da, dx2, do, dg3, dg2


def _wgrad(a, b, out_blocks, bm, bn, bk, square_a, name, dep=None):
    t_tok, m = a.shape
    n = b.shape[1]
    nk = t_tok // bk

    def body(a_ref, b_ref, *rest):
        o_ref, acc_ref = rest[-2:]
        k = pl.program_id(2)
        av = a_ref[...]
        if square_a:
            av = av * av
        part = _dot(av, b_ref[...], _TN)

        def emit(res):
            if out_blocks is None:
                o_ref[...] = res.astype(BF16)
            else:
                o_ref[0] = res.astype(BF16)

        if nk == 1:
            emit(part)
            return

        @pl.when(k == 0)
        def _():
            acc_ref[...] = part

        @pl.when(k > 0)
        def _():
            acc_ref[...] += part

        @pl.when(k == nk - 1)
        def _():
            emit(acc_ref[...])

    if out_blocks is None:
        out_shape = jax.ShapeDtypeStruct((m, n), BF16)
        out_spec = pl.BlockSpec((bm, bn), lambda i, j, k: (i, j))
    else:
        assert n // out_blocks == bn
        out_shape = jax.ShapeDtypeStruct((out_blocks, m, bn), BF16)
        out_spec = pl.BlockSpec((1, bm, bn), lambda i, j, k: (j, i, 0))
    deps = [] if dep is None else [dep]
    return pl.pallas_call(
        body, name=name, grid=(m // bm, n // bn, nk), out_shape=out_shape,
        in_specs=[pl.BlockSpec((bk, bm), lambda i, j, k: (k, i)), pl.BlockSpec((bk, bn), lambda i, j, k: (k, j))]
        + [pl.BlockSpec(memory_space=pl.ANY)] * len(deps),
        out_specs=out_spec, scratch_shapes=[pltpu.VMEM((bm, bn) if nk > 1 else (8, 128), F32)],
        compiler_params=_params("parallel", "parallel", "arbitrary"))(a, b, *deps)


def _wgrad_in(h1, pieces, bn, bk, dep=None):
    t_tok = h1.shape[0]
    nk = t_tok // bk
    widths = [b - a for a, b in _IN_SPLITS]

    def body(h_ref, *rest):
        piece_refs = rest[:len(widths)]
        o_ref, acc_ref = rest[-2:]
        k = pl.program_id(1)
        hv = h_ref[...]
        for (a, b), r in zip(_IN_SPLITS, piece_refs):
            part = _dot(r[...], hv, _TN)
            if nk == 1:
                o_ref[a:b, :] = part.astype(BF16)
                continue

            @pl.when(k == 0)
            def _():
                acc_ref[a:b, :] = part

            @pl.when(k > 0)
            def _():
                acc_ref[a:b, :] += part

        if nk > 1:
            @pl.when(k == nk - 1)
            def _():
                o_ref[...] = acc_ref[...].astype(BF16)

    deps = [] if dep is None else [dep]
    return pl.pallas_call(
        body, name="wgrad_in", grid=(D_MODEL // bn, nk), out_shape=jax.ShapeDtypeStruct((IN_PAD, D_MODEL), BF16),
        in_specs=[pl.BlockSpec((bk, bn), lambda j, k: (k, j))] + [pl.BlockSpec((bk, n), lambda j, k: (k, 0)) for n in widths]
        + [pl.BlockSpec(memory_space=pl.ANY)] * len(deps),
        out_specs=pl.BlockSpec((IN_PAD, bn), lambda j, k: (0, j)),
        scratch_shapes=[pltpu.VMEM((IN_PAD, bn) if nk > 1 else (8, 128), F32)],
        compiler_params=_params("parallel", "arbitrary"))(h1, *pieces, *deps)


def _dmix(do, w_out, tm, dep=None):
    t_tok = do.shape[0]

    def body(d_ref, w_ref, *rest):
        rest[-1][...] = _dot(d_ref[...], w_ref[...], _NT)

    row = pl.BlockSpec((tm, D_MODEL), lambda i: (i, 0))
    deps = [] if dep is None else [dep]
    return pl.pallas_call(
        body, name="dmix", grid=(t_tok // tm,), out_shape=jax.ShapeDtypeStruct((t_tok, D_MODEL), F32),
        in_specs=[row, _full((D_MODEL, D_MODEL))] + [pl.BlockSpec(memory_space=pl.ANY)] * len(deps), out_specs=row,
        compiler_params=_params("parallel"))(do, w_out, *deps)


def _gmlp_bwd(dmix, u, v, lnw, lnb, wcat, wtcat, bias, avg, expand_t):
    t_tok = u.shape[0]
    tm = min(_GMLP_ROWS, t_tok)

    def body(dm_ref, u_ref, v_ref, lnw_ref, lnb_ref, wcat_ref, wtcat_ref, bias_ref, avg_ref, expt_ref, du_ref, dv_ref,
             dw_ref, db_ref, dlnw_ref, dlnb_ref):
        i = pl.program_id(0)
        m_l, m_r = _lane_masks()
        avg = avg_ref[...]
        lnw = lnw_ref[...]
        ug, dug, dvg, rstd, vhat, vn, mixed = _gmlp_common(
            u_ref[...].astype(F32), v_ref[...].astype(F32), lnw, lnb_ref[...], avg, wcat_ref, bias_ref[...], m_l, m_r)
        dya = dm_ref[...]
        du_ref[...] = (dya * mixed * dug).astype(BF16)
        dmixed = dya * ug
        dvn_rows, dws, dbt = [], [None] * N_HEADS, None
        for r in range(tm // CHUNK):
            dvn_cols = []
            for j in range(N_HEADS // 2):
                dmp = dmixed[CHUNK * r:CHUNK * (r + 1), 128 * j:128 * (j + 1)]
                dvn_cols.append(_dot(wtcat_ref[j], _stack_pair(dmp, m_l, m_r)))
                vnp = vn[CHUNK * r:CHUNK * (r + 1), 128 * j:128 * (j + 1)].astype(BF16)
                for i_h, mask in enumerate((m_l, m_r)):
                    part = _dot((dmp * mask).astype(BF16), vnp, _NT)
                    dws[2 * j + i_h] = part if r == 0 else dws[2 * j + i_h] + part
            dvn_rows.append(jnp.concatenate(dvn_cols, axis=1))
            part = _split_dot(dmixed[CHUNK * r:CHUNK * (r + 1), :], expt_ref[...], 2)
            dbt = part if r == 0 else dbt + part
        dvn = jnp.concatenate(dvn_rows, axis=0)
        dvh = dvn * lnw
        dvgel = rstd * (dvh - _head_mean(dvh, avg) - vhat * _head_mean(dvh * vhat, avg))
        dv_ref[...] = (dvgel * dvg).astype(BF16)
        first = i == 0

        @pl.when(first)
        def _():
            for h in range(N_HEADS):
                dw_ref[h] = dws[h]
            db_ref[...] = dbt

        @pl.when(jnp.logical_not(first))
        def _():
            for h in range(N_HEADS):
                dw_ref[h] += dws[h]
            db_ref[...] += dbt

        _acc_rows(dlnw_ref, _rsum(dvn * vhat), first)
        _acc_rows(dlnb_ref, _rsum(dvn), first)

    row = pl.BlockSpec((tm, GM_WIDTH), lambda i: (i, 0))
    consts = [lnw, lnb, wcat, wtcat, bias, avg, expand_t]
    return pl.pallas_call(
        body, name="gmlp_bwd", grid=(t_tok // tm,),
        out_shape=(jax.ShapeDtypeStruct((t_tok, GM_WIDTH), BF16), jax.ShapeDtypeStruct((t_tok, GM_WIDTH), BF16),
                   jax.ShapeDtypeStruct((N_HEADS, CHUNK, CHUNK), F32), jax.ShapeDtypeStruct((CHUNK, CHUNK), F32),
                   jax.ShapeDtypeStruct((1, GM_WIDTH), F32), jax.ShapeDtypeStruct((1, GM_WIDTH), F32)),
        in_specs=[row, row, row] + [_full(a.shape) for a in consts],
        out_specs=(row, row, _full((N_HEADS, CHUNK, CHUNK)), _full((CHUNK, CHUNK)), _full((1, GM_WIDTH)),
                   _full((1, GM_WIDTH))),
        compiler_params=_params("arbitrary"))(dmix, u, v, *consts)


def _ssd_bwd(dmix, z, xbc, pre, dtr, y, states, cw, cb, dtb, alog, dskip_exp, nw, expand, expand_t, tril, triu, seq,
             dep=None):
    t_tok = z.shape[0]
    nb, nc, row, _, states_spec, fold, unfold = _ssd_specs(t_tok, seq, True)
    q = CHUNK

    def one_sequence(s, dm_ref, z_ref, xbc_ref, pre_ref, dtr_ref, y_ref, st_ref, cw_ref, dtb_ref, alog_ref, dsk_ref,
                     nw_ref, exp_ref, expt_ref, tril_ref, triu_ref, dz_ref, dxbc_ref, ddt_ref, dhead_ref, dstate_ref):
        m_l, m_r = _lane_masks()
        expt = expt_ref[...]
        f = _ssd_common(pre_ref[s], dtr_ref[s], dtb_ref[...], alog_ref[...], exp_ref[...], tril_ref[...])
        act, pre, sg = f["act"], f["pre"], f["sg"]
        xs = act[:, :SSM_WIDTH]
        xdt = xs * f["dt_exp"]
        xw = xdt * f["w_end"]
        state = st_ref[s, 0]
        dstate = dstate_ref[s]
        zv, yv, dout, nw = z_ref[s].astype(F32), y_ref[s], dm_ref[s], nw_ref[...]
        sz = jax.nn.sigmoid(zv)
        sl = zv * sz
        yg = yv * sl
        tv = dout * nw
        dyg_parts, ygh_parts = [], []
        for g in range(2):
            ygg = yg[:, 256 * g:256 * (g + 1)]
            rr = lax.rsqrt(jnp.mean(ygg * ygg, axis=-1, keepdims=True) + EPS)
            ygh = ygg * rr
            tg = tv[:, 256 * g:256 * (g + 1)]
            dyg_parts.append(rr * (tg - ygh * jnp.mean(tg * ygh, axis=-1, keepdims=True)))
            ygh_parts.append(ygh)
        dyg = jnp.concatenate(dyg_parts, axis=1)
        dnw = _rsum(dout * jnp.concatenate(ygh_parts, axis=1))
        dy = dyg * sl
        dz_ref[s] = (dyg * yv * (sz * (1.0 + zv * (1.0 - sz)))).astype(BF16)
        ddsk = _rsum(dy * xs)
        dye = dy * f["e"]
        lane = lax.broadcasted_iota(jnp.int32, (q, q), 1)
        sub = lax.broadcasted_iota(jnp.int32, (q, q), 0)
        rs_mat = jnp.zeros((q, q), F32)
        cs_mat = jnp.zeros((q, q), F32)
        dxdt_cols, yoff, dst_in, dxw, d_b, d_c = [], [], [], [], [], []
        for g in range(2):
            bg = act[:, 512 + 128 * g:640 + 128 * g].astype(BF16)
            cg = act[:, 768 + 128 * g:896 + 128 * g].astype(BF16)
            cb_mat = _dot(cg, bg, _NT)
            stg = state[:, 256 * g:256 * (g + 1)].astype(BF16)
            dyeg = dye[:, 256 * g:256 * (g + 1)].astype(BF16)
            yoff.append(_dot(cg, stg))
            dcg = _dot(dyeg, stg, _NT)
            dst_in.append(_dot(cg, dyeg, _TN))
            dcb = jnp.zeros((q, q), F32)
            for pr in range(2):
                h0 = 4 * g + 2 * pr
                gf = [cb_mat * f["decay"][h0], cb_mat * f["decay"][h0 + 1]]
                gcat = jnp.concatenate([gf[0].astype(BF16), gf[1].astype(BF16)], axis=1)
                xst = _stack_pair(xdt[:, 64 * h0:64 * h0 + 128], m_l, m_r)
                dyp = dy[:, 64 * h0:64 * h0 + 128].astype(BF16)
                dgcat = _dot(dyp, xst, _NT)
                dxst = _dot(gcat, dyp, _TN)
                dxdt_cols.append(dxst[:q] * m_l + dxst[q:] * m_r)
                for i in range(2):
                    h = h0 + i
                    dg = dgcat[:, q * i:q * (i + 1)]
                    mm = dg * gf[i]
                    rs_mat = rs_mat + jnp.where(lane == h, jnp.sum(mm, axis=1, keepdims=True), 0.0)
                    cs_mat = cs_mat + jnp.where(sub == h, jnp.sum(mm, axis=0, keepdims=True), 0.0)
                    dcb = dcb + dg * f["decay"][h]
            dcb16 = dcb.astype(BF16)
            dstg = dstate[:, 256 * g:256 * (g + 1)].astype(BF16)
            d_c.append(dcg + _dot(dcb16, bg))
            dxw.append(_dot(bg, dstg))
            d_b.append(_dot(dcb16, cg, _TN) + _dot(xw[:, 256 * g:256 * (g + 1)].astype(BF16), dstg, _NT))
        dxw = jnp.concatenate(dxw, axis=1)
        dxdt = jnp.concatenate(dxdt_cols, axis=1) + dxw * f["w_end"]
        qv = dxw * xw
        end_row = _rsum(qv) + _rsum(dstate * state) * f["cd"]
        x2 = dye * jnp.concatenate(yoff, axis=1) - qv
        row_i = lax.broadcasted_iota(jnp.int32, (q, 1), 0)
        x2 = x2 + jnp.where(row_i == q - 1, end_row, 0.0)
        da_cs = _split_dot(x2, expt, 3) + rs_mat - cs_mat.T
        ddt = _split_dot(dxdt * xs, expt, 3)
        dxs = dsk_ref[...] * dy + dxdt * f["dt_exp"]
        dda = _split_dot_left(triu_ref[...], da_cs, 3)
        ddt = ddt + dda * f["a_row"]
        dalog = _rsum(dda * f["dt"]) * f["a_row"]
        draw = ddt * jax.nn.sigmoid(f["dtp"])
        ddt_ref[s] = draw.astype(BF16)
        dact = jnp.concatenate([dxs] + d_b + d_c, axis=1)
        dpre = dact * (sg * (1.0 + pre * (1.0 - sg)))
        dhead = dhead_ref[s]
        xv = xbc_ref[s]
        shifted = [_shift_rows(dpre, dhead, 3 - k, False) for k in range(3)] + [dpre]
        dxbc = cw_ref[3:4, :] * dpre
        for k in range(3):
            dxbc = dxbc + cw_ref[k:k + 1, :] * shifted[k]
        dxbc_ref[s] = dxbc.astype(BF16)
        dhead_ref[s] = dpre[0:8, :]
        dstate_ref[s] = dstate * f["cd"] + jnp.concatenate(dst_in, axis=1)
        row8 = lax.broadcasted_iota(jnp.int32, (8, 1), 0)
        dcw = jnp.zeros((8, CONV_CH), F32)
        for k in range(4):
            dcw = dcw + jnp.where(row8 == k, _rsum(shifted[k] * xv), 0.0)
        return dcw, _rsum(dpre), _rsum(draw), dalog, _split_dot(ddsk, expt, 3), dnw

    def body(dm_ref, z_ref, xbc_ref, pre_ref, dtr_ref, y_ref, st_ref, cw_ref, cb_ref, dtb_ref, alog_ref, dsk_ref,
             nw_ref, exp_ref, expt_ref, tril_ref, triu_ref, dz_ref, dxbc_ref, ddt_ref, dcw_ref, dcb_ref, ddtb_ref,
             dalog_ref, dd_ref, dnw_ref, dhead_ref, dstate_ref):
        c = pl.program_id(0)
        first = c == 0

        @pl.when(first)
        def _():
            dstate_ref[...] = jnp.zeros_like(dstate_ref)
            dhead_ref[...] = jnp.zeros_like(dhead_ref)

        total = None
        for s in range(nb):
            parts = one_sequence(s, dm_ref, z_ref, xbc_ref, pre_ref, dtr_ref, y_ref, st_ref, cw_ref, dtb_ref, alog_ref,
                                 dsk_ref, nw_ref, exp_ref, expt_ref, tril_ref, triu_ref, dz_ref, dxbc_ref, ddt_ref,
                                 dhead_ref, dstate_ref)
            total = parts if total is None else tuple(a + b for a, b in zip(total, parts))
        dcw = total[0]

        @pl.when(first)
        def _():
            dcw_ref[...] = dcw

        @pl.when(jnp.logical_not(first))
        def _():
            dcw_ref[...] += dcw

        for ref, part in zip((dcb_ref, ddtb_ref, dalog_ref, dd_ref, dnw_ref), total[1:]):
            _acc_rows(ref, part, first)

    consts = [cw, cb, dtb, alog, dskip_exp, nw, expand, expand_t, tril, triu]
    deps = [] if dep is None else [dep]
    n_in = 7 + len(consts)

    def body_skipping_dep(*refs):
        body(*refs[:n_in], *refs[n_in + len(deps):])

    acc = lambda n: jax.ShapeDtypeStruct((1, n), F32)
    sd = lambda n: jax.ShapeDtypeStruct((nb, seq, n), BF16)
    dz, dxbc, ddt, *small_grads = pl.pallas_call(
        body_skipping_dep, name="ssd_bwd", grid=(nc,),
        out_shape=(sd(SSM_WIDTH), sd(CONV_CH), sd(CHUNK), jax.ShapeDtypeStruct((8, CONV_CH), F32), acc(CONV_CH),
                   acc(CHUNK), acc(CHUNK), acc(CHUNK), acc(SSM_WIDTH)),
        in_specs=[row(SSM_WIDTH, col=1), row(SSM_WIDTH), row(CONV_CH), row(CONV_CH), row(CHUNK), row(SSM_WIDTH),
                  states_spec]
        + [_full(a.shape) for a in consts] + [pl.BlockSpec(memory_space=pl.ANY)] * len(deps),
        out_specs=(row(SSM_WIDTH), row(CONV_CH), row(CHUNK), _full((8, CONV_CH)), _full((1, CONV_CH)),
                   _full((1, CHUNK)), _full((1, CHUNK)), _full((1, CHUNK)), _full((1, SSM_WIDTH))),
        scratch_shapes=[pltpu.VMEM((nb, 8, CONV_CH), F32), pltpu.VMEM((nb, N_STATE, SSM_WIDTH), F32)],
        compiler_params=_params("arbitrary"))(
            fold(dmix), fold(z), fold(xbc), fold(pre), fold(dtr), fold(y), states, *consts, *deps)
    return (unfold(dz), unfold(dxbc), unfold(ddt), *small_grads)


def _in_bwd(du, dv, dz, dxbc, ddt, w_in, x, dx2, g1, tm, dep=None):
    t_tok = x.shape[0]

    def body(du_ref, dv_ref, dz_ref, dxbc_ref, ddt_ref, w_ref, x_ref, dx2_ref, g_ref, *rest):
        gx_ref, dg_ref = rest[-2:]
        i = pl.program_id(0)
        dh = None
        for (a, b), ref in zip(_IN_SPLITS, (du_ref, dv_ref, dz_ref, dxbc_ref, ddt_ref)):
            part = _dot(ref[...], w_ref[a:b, :])
            dh = part if dh is None else dh + part
        dn, dg = _rms_bwd(x_ref[...], g_ref[...], dh)
        gx_ref[...] = dx2_ref[...] + dn
        _acc_rows(dg_ref, dg, i == 0)

    row = lambda n: pl.BlockSpec((tm, n), lambda i: (i, 0))
    widths = [b - a for a, b in _IN_SPLITS]
    deps = [] if dep is None else [dep]
    return pl.pallas_call(
        body, name="in_bwd", grid=(t_tok // tm,),
        out_shape=(jax.ShapeDtypeStruct((t_tok, D_MODEL), F32), jax.ShapeDtypeStruct((1, D_MODEL), F32)),
        in_specs=[row(n) for n in widths] + [_full((IN_PAD, D_MODEL)), row(D_MODEL), row(D_MODEL), _full((1, D_MODEL))]
        + [pl.BlockSpec(memory_space=pl.ANY)] * len(deps),
        out_specs=(row(D_MODEL), _full((1, D_MODEL))),
        compiler_params=_params("arbitrary"))(du, dv, dz, dxbc, ddt, w_in, x, dx2, g1, *deps)


def _pad_lanes(a, n):
    return jnp.pad(a, ((0, 0), (0, n - a.shape[1])))


def _local_step(x, target, seq, small, hooks, first_dep=None):
    t_tok = x.shape[0]
    tm = min(512, t_tok)
    avg, expand, expand_t, tril, triu = _const_mats()
    g1, g2, g3, g4 = (small[k].reshape(1, D_MODEL) for k in
                      ("norm_mix_pre", "norm_mix_post", "norm_ffn_pre", "norm_ffn_post"))
    tie = (lambda a: a) if first_dep is None else (lambda a: a + first_dep[0, 0])
    lnw = tie(small["gm_ln_w"]).reshape(1, GM_WIDTH)
    lnb = tie(small["gm_ln_b"]).reshape(1, GM_WIDTH)
    causal = jnp.tril(jnp.ones((CHUNK, CHUNK), F32))
    wm = tie(small["gm_w_s"]) * causal
    pair = lambda w: w.reshape(4, 2, CHUNK, CHUNK).transpose(0, 2, 1, 3).reshape(4, CHUNK, 2 * CHUNK).astype(BF16)
    wcat = pair(wm)
    wtcat = pair(jnp.swapaxes(wm, 1, 2))
    bias = jnp.repeat(tie(small["gm_b_s"]).T, HEAD_DIM, axis=1)
    cb = small["conv_b"].reshape(1, CONV_CH)
    dtb = _pad_lanes(tie(small["dt_bias"]).reshape(1, N_HEADS), CHUNK)
    alog = _pad_lanes(tie(small["a_log"]).reshape(1, N_HEADS), CHUNK)
    dskip_exp = jnp.repeat(tie(small["d_skip"]).reshape(1, N_HEADS), HEAD_DIM, axis=1)
    nw = small["ssm_norm_w"].reshape(1, SSM_WIDTH)

    h1 = _prenorm(x, g1, tm, hooks.get("prenorm_after", first_dep))
    w_in_t, conv_w = hooks["mixer_weights"](h1)
    u, v, z, xbc, dtr = _in_proj(h1, w_in_t, tm)
    mix_a = _gmlp_fwd(u, v, lnw, lnb, wcat, bias, avg)
    mix_b, y_pre, states, pre = _ssd_fwd(z, xbc, dtr, conv_w, cb, dtb, alog, dskip_exp, nw, expand, tril, seq)
    w_out, dep = hooks["mixers_done"](mix_b)
    o, x2, h3, mix = _out_proj(mix_a, mix_b, w_out, x, g2, g3, tm, dep)
    w_up, w_down = hooks["mlp_weights"](h3)
    tf = 2048
    ra, dd, dy, dg4, loss = _mlp_fwd(h3, w_up, w_down, x2, target, g4, tm, tf)

    da, dx2, do, dg3, dg2 = _mlp_bwd(dd, w_down, ra, w_up, x2, dy, o, g3, g2, tm, tf)
    bk = min(2048, t_tok)
    g_w_down = _wgrad(ra, dd, None, 512, D_MODEL, t_tok, True, "wgrad_down")
    g_w_up = _wgrad(h3, da, N_DEV, D_MODEL, D_FF // N_DEV, t_tok, False, "wgrad_up")
    dep = hooks["mlp_grads"](g_w_down, g_w_up)
    dmix = _dmix(do, w_out, tm, dep)
    g_w_out = _wgrad(mix, do, None, D_MODEL, 512, t_tok, False, "wgrad_out", dep)
    du, dv, dws, dbt, dlnw, dlnb = _gmlp_bwd(dmix, u, v, lnw, lnb, wcat, wtcat, bias, avg, expand_t)
    dep = hooks["gmlp_grads"](g_w_out, dws)
    dz, dxbc, ddt, dcw, dcb, ddtb, dalog, ddsk, dnw = _ssd_bwd(
        dmix, z, xbc, pre, dtr, y_pre, states, conv_w, cb, dtb, alog, dskip_exp, nw, expand, expand_t, tril, triu, seq,
        dep)
    g_w_in = _wgrad_in(h1, (du, dv, dz, dxbc, ddt), 512, bk, dep)
    dep = hooks["in_grads"](g_w_in, dcw[0:4])
    grad_x, dg1 = _in_bwd(du, dv, dz, dxbc, ddt, w_in_t, x, dx2, g1, tm, dep)

    grads = dict(
        w_in=g_w_in, w_out=g_w_out, w_up=g_w_up, w_down=g_w_down, conv_w=dcw[0:4],
        norm_mix_pre=dg1, norm_mix_post=dg2, norm_ffn_pre=dg3, norm_ffn_post=dg4, gm_ln_w=dlnw, gm_ln_b=dlnb,
        gm_w_s=dws, gm_b_s=dbt, conv_b=dcb, dt_bias=ddtb, a_log=dalog, d_skip=ddsk, ssm_norm_w=dnw)
    return loss[0, 0], grad_x, grads


_WEIGHTS = ("norm_mix_pre", "w_in", "gm_ln_w", "gm_ln_b", "gm_w_s", "gm_b_s", "conv_w", "conv_b", "dt_bias", "a_log",
            "d_skip", "ssm_norm_w", "w_out", "norm_mix_post", "norm_ffn_pre", "w_up", "w_down", "norm_ffn_post")
_SLAB_ROWS = (("norm_mix_pre", 1024), ("norm_mix_post", 1024), ("norm_ffn_pre", 1024), ("norm_ffn_post", 1024),
              ("conv_b", 1024), ("ssm_norm_w", 512), ("gm_ln_w", 512), ("gm_ln_b", 512), ("dt_bias", 8), ("a_log", 8),
              ("d_skip", 8))
_SLAB_LOSS_ROW = len(_SLAB_ROWS)
_SLAB_BS_ROW = 16
_SLAB_HEIGHT = 24
_SMALL_PARAMS = tuple(name for name, _ in _SLAB_ROWS) + ("gm_b_s",)
_LN_PARAMS = ("gm_ln_w", "gm_ln_b")


def _pack_slab(g, loss_part):
    rows = [_pad_lanes(g[name], D_MODEL) for name, _ in _SLAB_ROWS]
    rows.append(jnp.broadcast_to(loss_part, (1, D_MODEL)))
    rows.append(jnp.zeros((_SLAB_BS_ROW - len(rows), D_MODEL), F32))
    rows.append(_pad_lanes(g["gm_b_s"].T[0:N_HEADS], D_MODEL))
    return jnp.concatenate(rows, axis=0)


def _adamw_slab(parts, w, m, v):
    names = _SMALL_PARAMS
    shapes = [w[k].shape for k in names]
    unfold = np.zeros((GM_WIDTH, HEAD_DIM), np.float32)
    for h in range(N_HEADS):
        unfold[h * HEAD_DIM:(h + 1) * HEAD_DIM, :] = np.eye(HEAD_DIM)
    unfold = jnp.asarray(unfold, dtype=BF16)
    n = len(names)

    def body(p_ref, unfold_ref, *refs):
        w_refs, m_refs, v_refs = refs[:n], refs[n:2 * n], refs[2 * n:3 * n]
        outs = refs[3 * n:]
        g_all = p_ref[0]
        for j in range(1, N_DEV):
            g_all = g_all + p_ref[j]
        lane = lax.broadcasted_iota(jnp.int32, (N_HEADS, GM_WIDTH), 1)
        head = lax.broadcasted_iota(jnp.int32, (N_HEADS, GM_WIDTH), 0)
        own_lanes = jnp.logical_and(lane >= head * HEAD_DIM, lane < (head + 1) * HEAD_DIM)
        for i, name in enumerate(names):
            if name == "gm_b_s":
                g = g_all[_SLAB_BS_ROW:_SLAB_BS_ROW + N_HEADS, 0:CHUNK]
            else:
                row = [r for r, (k, _) in enumerate(_SLAB_ROWS) if k == name][0]
                g = g_all[row:row + 1, 0:dict(_SLAB_ROWS)[name]]
                if name in _LN_PARAMS:
                    g = _split_dot(jnp.where(own_lanes, g, 0.0), unfold_ref[...], 3)
            d, mn, vn = _adamw_math(w_refs[i][...], g, m_refs[i][...], v_refs[i][...])
            for o_ref, val in zip(outs[4 * i:4 * i + 4], (g, d, mn, vn)):
                o_ref[...] = val
        outs[-1][...] = g_all[_SLAB_LOSS_ROW:_SLAB_LOSS_ROW + 1, 0:128]

    ins = [parts, unfold] + [d[k] for d in (w, m, v) for k in names]
    out_shape = tuple(jax.ShapeDtypeStruct(s, F32) for s in shapes for _ in range(4)) + (
        jax.ShapeDtypeStruct((1, 128), F32),)
    outs = pl.pallas_call(
        body, name="adamw_small", out_shape=out_shape, grid=(1,), in_specs=[_full(a.shape) for a in ins],
        out_specs=tuple(_full(s.shape) for s in out_shape), compiler_params=_params("arbitrary"))(*ins)
    return {k: tuple(outs[4 * i:4 * i + 4]) for i, k in enumerate(names)}, outs[-1][0, 0]


def kernel(x, norm_mix_pre, w_in, gm_ln_w, gm_ln_b, gm_w_s, gm_b_s, conv_w, conv_b, dt_bias, a_log, d_skip, ssm_norm_w, w_out, norm_mix_post, norm_ffn_pre, w_up, w_down, norm_ffn_post, loss_target, m_norm_mix_pre, m_w_in, m_gm_ln_w, m_gm_ln_b, m_gm_w_s, m_gm_b_s, m_conv_w, m_conv_b, m_dt_bias, m_a_log, m_d_skip, m_ssm_norm_w, m_w_out, m_norm_mix_post, m_norm_ffn_pre, m_w_up, m_w_down, m_norm_ffn_post, v_norm_mix_pre, v_w_in, v_gm_ln_w, v_gm_ln_b, v_gm_w_s, v_gm_b_s, v_conv_w, v_conv_b, v_dt_bias, v_a_log, v_d_skip, v_ssm_norm_w, v_w_out, v_norm_mix_post, v_norm_ffn_pre, v_w_up, v_w_down, v_norm_ffn_post):
    w = dict(norm_mix_pre=norm_mix_pre, w_in=w_in, gm_ln_w=gm_ln_w, gm_ln_b=gm_ln_b, gm_w_s=gm_w_s, gm_b_s=gm_b_s, conv_w=conv_w, conv_b=conv_b, dt_bias=dt_bias, a_log=a_log, d_skip=d_skip, ssm_norm_w=ssm_norm_w, w_out=w_out, norm_mix_post=norm_mix_post, norm_ffn_pre=norm_ffn_pre, w_up=w_up, w_down=w_down, norm_ffn_post=norm_ffn_post)
    m = dict(norm_mix_pre=m_norm_mix_pre, w_in=m_w_in, gm_ln_w=m_gm_ln_w, gm_ln_b=m_gm_ln_b, gm_w_s=m_gm_w_s, gm_b_s=m_gm_b_s, conv_w=m_conv_w, conv_b=m_conv_b, dt_bias=m_dt_bias, a_log=m_a_log, d_skip=m_d_skip, ssm_norm_w=m_ssm_norm_w, w_out=m_w_out, norm_mix_post=m_norm_mix_post, norm_ffn_pre=m_norm_ffn_pre, w_up=m_w_up, w_down=m_w_down, norm_ffn_post=m_norm_ffn_post)
    v = dict(norm_mix_pre=v_norm_mix_pre, w_in=v_w_in, gm_ln_w=v_gm_ln_w, gm_ln_b=v_gm_ln_b, gm_w_s=v_gm_w_s, gm_b_s=v_gm_b_s, conv_w=v_conv_w, conv_b=v_conv_b, dt_bias=v_dt_bias, a_log=v_a_log, d_skip=v_d_skip, ssm_norm_w=v_ssm_norm_w, w_out=v_w_out, norm_mix_post=v_norm_mix_post, norm_ffn_pre=v_norm_ffn_pre, w_up=v_w_up, w_down=v_w_down, norm_ffn_post=v_norm_ffn_post)
    n_batch, seq, _ = x.shape
    shard_in = IN_COLS // N_DEV

    me = (4 * lax.axis_index("x") + 2 * lax.axis_index("y") + lax.axis_index("c")).astype(jnp.int32).reshape(1)

    def in_slot(own):
        return lax.dynamic_update_slice(lax.empty((N_DEV,) + own.shape, own.dtype), own[None],
                                        (me[0],) + (0,) * own.ndim)

    w_in_sh, m_in_sh, v_in_sh = w_in[0].T, m_w_in[0].T, v_w_in[0].T
    first = [_cast_to_slot(w_in_sh, me, shard_in, "cast_w_in"), in_slot(conv_w[0]),
             _cast_to_slot(w_out[0], me, 128, "cast_w_out")]
    ici_1, tok_ici_1 = _exchange_start(first, [True] * 3, _SAME_CORE_PEERS, "gather_mix_ici_start")
    cast_up = _cast_to_slot(w_up[0], me, 1024, "cast_w_up", cols=True, dep=tok_ici_1)
    second = [cast_up, _cast_to_slot(w_down[0], me, 512, "cast_w_down", dep=cast_up)]
    gathering = {}

    def mixer_weights(after):
        bufs = [buf for buf, _ in _exchange_wait(ici_1, after, "gather_mix_ici_wait")]
        d2d_1, tok_d2d_1 = _exchange_start(bufs, [True] * 3, _SIBLING_FORWARD, "gather_mix_d2d_start")
        gathering["mlp_ici"], tok_ici_2 = _exchange_start(
            second, [True] * 2, _SAME_CORE_PEERS, "gather_mlp_ici_start", dep=tok_d2d_1)
        (_, ag_in), (_, ag_conv), (_, ag_out) = _exchange_wait(d2d_1, tok_ici_2, "gather_mix_d2d_wait")
        gathering["w_out"] = ag_out.reshape(D_MODEL, D_MODEL)
        w_in_t = jnp.pad(ag_in.reshape(IN_COLS, D_MODEL), ((0, IN_PAD - IN_COLS), (0, 0)))
        return w_in_t, ag_conv.transpose(1, 0, 2).reshape(4, CONV_CH)

    def mixers_done(after):
        bufs = [buf for buf, _ in _exchange_wait(gathering["mlp_ici"], after, "gather_mlp_ici_wait")]
        gathering["mlp"], tok = _exchange_start(bufs, [True] * 2, _SIBLING_FORWARD, "gather_mlp_d2d_start")
        return gathering["w_out"], tok

    def mlp_weights(after):
        (_, ag_up), (_, ag_down) = _exchange_wait(gathering["mlp"], after, "gather_mlp_d2d_wait")
        return ag_up, ag_down.reshape(D_FF, D_MODEL)

    sent = {}

    def mlp_grads(g_w_down, g_w_up):
        sent["mlp"], tok = _exchange_start(
            [g_w_down.reshape(N_DEV, D_FF // N_DEV, D_MODEL), g_w_up], [False, False], _ALL_PEERS, "grads_mlp_start")
        return tok

    def gmlp_grads(g_w_out, g_w_s):
        sent["gmlp"], tok = _exchange_start(
            [g_w_out.reshape(N_DEV, D_MODEL // N_DEV, D_MODEL), in_slot(g_w_s.astype(BF16))], [False, True], _ALL_PEERS,
            "grads_gmlp_start")
        return tok

    def in_grads(g_w_in_t, g_conv_w):
        g_in_blk = g_w_in_t[:IN_COLS].reshape(N_DEV, shard_in, D_MODEL)
        g_conv_blk = g_conv_w.reshape(4, N_DEV, CONV_CH // N_DEV).transpose(1, 0, 2)
        sent["in"], tok = _exchange_start([g_in_blk, g_conv_blk], [False, False], _ALL_PEERS, "grads_in_start")
        return tok

    small = {k: w[k][0] for k in _SMALL_PARAMS + ("gm_w_s",)}
    loss_part, grad_x, g = _local_step(
        x.reshape(n_batch * seq, D_MODEL), loss_target.reshape(n_batch * seq, D_MODEL), seq, small,
        dict(mixer_weights=mixer_weights, mixers_done=mixers_done, mlp_weights=mlp_weights, mlp_grads=mlp_grads,
             gmlp_grads=gmlp_grads, in_grads=in_grads, prenorm_after=second[1]), first_dep=tok_ici_1)

    sent_rows, tok_rows = _exchange_start([in_slot(_pack_slab(g, loss_part))], [True], _ALL_PEERS, "grads_rows_start")
    (own_down, p_down), (own_up, p_up) = _exchange_wait(sent["mlp"], tok_rows, "grads_mlp_wait")
    res = {}
    res["w_up"] = _adamw_reduce(p_up, own_up, me, w_up[0], m_w_up[0], v_w_up[0], 256, "adamw_w_up")
    res["w_down"] = _adamw_reduce(p_down, own_down, me, w_down[0], m_w_down[0], v_w_down[0], 128, "adamw_w_down")
    (own_out, p_out), (_, p_ws) = _exchange_wait(sent["gmlp"], res["w_down"][1], "grads_gmlp_wait")
    res["w_out"] = _adamw_reduce(p_out, own_out, me, w_out[0], m_w_out[0], v_w_out[0], 128, "adamw_w_out")
    causal = jnp.tril(jnp.ones((1, CHUNK, CHUNK), F32))
    res["gm_w_s"] = _adamw_small(p_ws, None, me, gm_w_s[0], m_gm_w_s[0], v_gm_w_s[0], causal, "adamw_gm_w_s")
    (own_in, p_in), (own_conv, p_conv) = _exchange_wait(sent["in"], res["gm_w_s"][1], "grads_in_wait")
    res["w_in"] = tuple(r.T for r in _adamw_reduce(p_in, own_in, me, w_in_sh, m_in_sh, v_in_sh, shard_in, "adamw_w_in"))
    res["conv_w"] = _adamw_small(p_conv, own_conv, me, conv_w[0], m_conv_w[0], v_conv_w[0], None, "adamw_conv_w")
    ((_, p_rows),) = _exchange_wait(sent_rows, res["w_in"][1], "grads_rows_wait")
    flat = lambda t: t[0] if t.ndim == 3 else t
    small_res, loss = _adamw_slab(p_rows, *({k: flat(d[k]) for k in _SMALL_PARAMS} for d in (w, m, v)))
    res.update(small_res)
    res = {k: tuple(r.reshape(w[k].shape) for r in res[k]) for k in _WEIGHTS}

    outs = [loss, grad_x.reshape(x.shape)]
    for part in range(4):
        outs.extend(res[k][part] for k in _WEIGHTS)
    return tuple(outs)
```

```python
import functools

import jax
import jax.numpy as jnp
import numpy as np
from jax import lax
from jax.experimental import pallas as pl
from jax.experimental.pallas import tpu as pltpu

F32 = jnp.float32
BF16 = jnp.bfloat16

D_MODEL = 1024
GM_WIDTH = 512
SSM_WIDTH = 512
CONV_CH = 1024
N_HEADS = 8
HEAD_DIM = 64
N_STATE = 128
CHUNK = 128
D_FF = 4096
IN_COLS = 2568
IN_PAD = 2688
N_DEV = 8
EPS = 1e-6
ADAM_LR, ADAM_B1, ADAM_B2, ADAM_EPS, ADAM_WD, ADAM_STEP = 0.001, 0.9, 0.999, 1e-08, 0.01, 10
VMEM_LIMIT_BYTES = 56 * 1024 * 1024

_NT = (((1,), (1,)), ((), ()))
_TN = (((0,), (0,)), ((), ()))


def _params(*sem):
    return pltpu.CompilerParams(dimension_semantics=sem or None, vmem_limit_bytes=VMEM_LIMIT_BYTES)


def _dot(a, b, dims=None):
    if dims is None:
        return jnp.dot(a, b, preferred_element_type=F32)
    return lax.dot_general(a, b, dims, preferred_element_type=F32)


def _split_terms(x, terms):
    out, rem = [], x
    for i in range(terms):
        hi = rem.astype(BF16)
        out.append(hi)
        if i + 1 < terms:
            rem = rem - hi.astype(F32)
    return out


def _split_dot(x, m, terms):
    acc = None
    for hi in _split_terms(x, terms):
        part = _dot(hi, m)
        acc = part if acc is None else acc + part
    return acc


def _split_dot_left(m, x, terms):
    acc = None
    for hi in _split_terms(x, terms):
        part = _dot(m, hi)
        acc = part if acc is None else acc + part
    return acc


def _gelu_and_grad(x):
    c = 0.7978845608028654
    inner = c * (x + 0.044715 * x * x * x)
    t = jnp.tanh(inner)
    g = 0.5 * x * (1.0 + t)
    dg = 0.5 * (1.0 + t) + 0.5 * x * (1.0 - t * t) * c * (1.0 + 3.0 * 0.044715 * x * x)
    return g, dg


def _softplus(x):
    return jnp.maximum(x, 0.0) + jnp.log(1.0 + jnp.exp(-jnp.abs(x)))


def _rsum(x):
    return jnp.sum(x, axis=0, keepdims=True)


def _acc_rows(ref, part, first):
    val = jnp.broadcast_to(part, ref.shape)

    @pl.when(first)
    def _():
        ref[...] = val

    @pl.when(jnp.logical_not(first))
    def _():
        ref[...] += val


def _rms_bwd(n, g, dout):
    r = lax.rsqrt(jnp.mean(n * n, axis=-1, keepdims=True) + EPS)
    nh = n * r
    dg = dout * g
    dn = r * (dg - nh * jnp.mean(dg * nh, axis=-1, keepdims=True))
    return dn, _rsum(dout * nh)


def _const_mats():
    avg = np.kron(np.eye(4), np.full((HEAD_DIM, HEAD_DIM), 1.0 / HEAD_DIM))
    expand = np.zeros((CHUNK, SSM_WIDTH), np.float32)
    for h in range(N_HEADS):
        expand[h, h * HEAD_DIM:(h + 1) * HEAD_DIM] = 1.0
    tril = np.tril(np.ones((CHUNK, CHUNK), np.float32))
    as_bf16 = lambda a: jnp.asarray(a, dtype=BF16)
    return as_bf16(avg), as_bf16(expand), as_bf16(expand.T), as_bf16(tril), as_bf16(tril.T)


def _full(shape):
    nd = len(shape)
    return pl.BlockSpec(shape, lambda *_: (0,) * nd)


_HBM = pl.BlockSpec(memory_space=pltpu.HBM)
_SEM = pl.BlockSpec(memory_space=pltpu.SEMAPHORE)
_ALL_PEERS = tuple((k, 0) for k in range(1, N_DEV))
_SAME_CORE_PEERS = ((2, 0), (4, 0), (6, 0))
_SIBLING_FORWARD = ((1, 0), (1, 2), (1, 4), (1, 6))


def _flip(j, k):
    for bit in (4, 2, 1):
        if k & bit:
            j = j + bit - 2 * (j & bit)
    return j


def _copies(src, land, send_sems, recv_sems, hops):
    x, y, c = lax.axis_index("x"), lax.axis_index("y"), lax.axis_index("c")
    me = 4 * x + 2 * y + c
    out = []
    for t in range(len(src)):
        for i, (k, b) in enumerate(hops):
            pos = (1 - x if k & 4 else x, 1 - y if k & 2 else y, 1 - c if k & 1 else c)
            peer = _flip(me, k)
            sem = t * len(hops) + i
            mk = functools.partial(pltpu.make_async_remote_copy, send_sem=send_sems.at[sem], recv_sem=recv_sems.at[sem],
                                   device_id=pos, device_id_type=pl.DeviceIdType.MESH)
            if land[t] is None and src[t].shape[0] != N_DEV:
                width = src[t].shape[1] // N_DEV
                slab = lambda j: src[t].at[:, pl.ds(pl.multiple_of(j * width, 128), width)]
                mine = functools.partial(mk, src_ref=slab(_flip(me, b)), dst_ref=slab(_flip(me, b)))
                theirs = functools.partial(mk, src_ref=slab(_flip(peer, b)), dst_ref=slab(_flip(peer, b)))
            elif land[t] is None:
                mine = functools.partial(mk, src_ref=src[t].at[_flip(me, b)], dst_ref=src[t].at[_flip(me, b)])
                theirs = functools.partial(mk, src_ref=src[t].at[_flip(peer, b)], dst_ref=src[t].at[_flip(peer, b)])
            else:
                assert b == 0
                mine = functools.partial(mk, src_ref=src[t].at[peer], dst_ref=land[t].at[me])
                theirs = functools.partial(mk, src_ref=src[t].at[peer], dst_ref=land[t].at[peer])
            out.append((mine, theirs))
    return out


def _exchange_start(srcs, inplace, peers, name, dep=None):
    n = len(srcs)
    lands = [None if ip else pltpu.with_memory_space_constraint(lax.empty(s.shape, s.dtype), pltpu.HBM)
             for s, ip in zip(srcs, inplace)]
    real_lands = [l for l in lands if l is not None]
    n_l = len(real_lands)
    deps = [] if dep is None else [dep]

    def body(*refs):
        src = refs[:n]
        land_refs = list(refs[n:n + n_l])
        send_sems, recv_sems = refs[n + n_l + len(deps)], refs[n + n_l + len(deps) + 1]
        token = refs[-1]
        land = [None if ip else land_refs.pop(0) for ip in inplace]
        for mine, _ in _copies(src, land, send_sems, recv_sems, peers):
            mine().start()
        token[...] = jnp.zeros_like(token)

    sem_t = pltpu.SemaphoreType.DMA((n * len(peers),))
    outs = pl.pallas_call(
        body, name=name,
        out_shape=(sem_t, sem_t) + tuple(pltpu.HBM(a.shape, a.dtype) for a in list(srcs) + real_lands)
        + (jax.ShapeDtypeStruct((8, 128), F32),),
        in_specs=[_HBM] * (n + n_l) + [pl.BlockSpec(memory_space=pl.ANY)] * len(deps),
        out_specs=(_SEM, _SEM) + (_HBM,) * (n + n_l) + (pl.BlockSpec(memory_space=pltpu.VMEM),),
        input_output_aliases={i: 2 + i for i in range(n + n_l)},
        compiler_params=pltpu.CompilerParams(has_side_effects=pltpu.SideEffectType.DATAFLOW_SIDE_EFFECTING),
    )(*[pltpu.with_memory_space_constraint(s, pltpu.HBM) for s in srcs], *real_lands, *deps)
    handle = dict(send=outs[0], recv=outs[1], srcs=outs[2:2 + n], lands=outs[2 + n:2 + n + n_l], inplace=inplace,
                  peers=peers)
    return handle, outs[-1]


def _exchange_wait(handle, after, name):
    srcs, lands, inplace, peers = handle["srcs"], handle["lands"], handle["inplace"], handle["peers"]
    n, n_l = len(srcs), len(lands)

    def body(*refs):
        src = refs[:n]
        land_refs = list(refs[n:n + n_l])
        send_sems, recv_sems = refs[n + n_l], refs[n + n_l + 1]
        land = [None if ip else land_refs.pop(0) for ip in inplace]
        for mine, theirs in _copies(src, land, send_sems, recv_sems, peers):
            mine().wait_send()
            theirs().wait_recv()

    outs = pl.pallas_call(
        body, name=name, out_shape=tuple(pltpu.HBM(a.shape, a.dtype) for a in list(srcs) + list(lands)),
        in_specs=[_HBM] * (n + n_l) + [_SEM, _SEM, pl.BlockSpec(memory_space=pl.ANY)],
        out_specs=(_HBM,) * (n + n_l), input_output_aliases={i: i for i in range(n + n_l)},
        compiler_params=pltpu.CompilerParams(has_side_effects=pltpu.SideEffectType.DATAFLOW_SIDE_EFFECTING),
    )(*srcs, *lands, handle["send"], handle["recv"], after)
    res, land_out = [], list(outs[n:])
    for t in range(n):
        res.append((outs[t], outs[t] if inplace[t] else land_out.pop(0)))
    return res


def _cast_to_slot(w, me, rows, name, cols=False, dep=None):
    r, cdim = w.shape
    deps = [] if dep is None else [dep]

    def body(me_ref, w_ref, *rest):
        o_ref = rest[-1]
        if cols:
            o_ref[...] = w_ref[...].astype(BF16)
        else:
            o_ref[0] = w_ref[...].astype(BF16)

    if cols:
        out_shape = jax.ShapeDtypeStruct((r, N_DEV * cdim), BF16)
        out_spec = pl.BlockSpec((rows, cdim), lambda i, me_ref: (i, me_ref[0]))
    else:
        out_shape = jax.ShapeDtypeStruct((N_DEV, r, cdim), BF16)
        out_spec = pl.BlockSpec((1, rows, cdim), lambda i, me_ref: (me_ref[0], i, 0))
    return pl.pallas_call(
        body, name=name, out_shape=out_shape,
        grid_spec=pltpu.PrefetchScalarGridSpec(
            num_scalar_prefetch=1, grid=(r // rows,),
            in_specs=[pl.BlockSpec((rows, cdim), lambda i, me_ref: (i, 0))]
            + [pl.BlockSpec(memory_space=pl.ANY)] * len(deps), out_specs=out_spec),
        compiler_params=_params("parallel"))(me, w, *deps)


def _adamw_math(w, g, m, v):
    m = ADAM_B1 * m + (1.0 - ADAM_B1) * g
    v = ADAM_B2 * v + (1.0 - ADAM_B2) * (g * g)
    m_hat = m / (1.0 - ADAM_B1 ** ADAM_STEP)
    v_hat = v / (1.0 - ADAM_B2 ** ADAM_STEP)
    delta = -ADAM_LR * (m_hat / (jnp.sqrt(v_hat) + ADAM_EPS) + ADAM_WD * w)
    return delta, m, v


def _sum_parts(me, p_ref, own):
    g = None
    for j in range(N_DEV):
        term = (p_ref[j] if own is None else jnp.where(me == j, own, p_ref[j])).astype(F32)
        g = term if g is None else g + term
    return g


def _adamw_reduce(parts, own, me, w, m, v, rows, name):
    r, cdim = w.shape

    def body(me_ref, p_ref, own_ref, w_ref, m_ref, v_ref, g_out, d_out, m_out, v_out):
        g = _sum_parts(me_ref[0], p_ref, own_ref[0])
        d, mn, vn = _adamw_math(w_ref[...], g, m_ref[...], v_ref[...])
        g_out[...] = g
        d_out[...] = d
        m_out[...] = mn
        v_out[...] = vn

    blk = pl.BlockSpec((rows, cdim), lambda i, me_ref: (i, 0))
    sds = jax.ShapeDtypeStruct(w.shape, F32)
    return pl.pallas_call(
        body, name=name, out_shape=(sds,) * 4,
        grid_spec=pltpu.PrefetchScalarGridSpec(
            num_scalar_prefetch=1, grid=(r // rows,),
            in_specs=[pl.BlockSpec((N_DEV, rows, cdim), lambda i, me_ref: (0, i, 0)),
                      pl.BlockSpec((1, rows, cdim), lambda i, me_ref: (me_ref[0], i, 0)), blk, blk, blk],
            out_specs=(blk,) * 4),
        compiler_params=_params("parallel"))(me, parts, own, w, m, v)


def _adamw_small(parts, own, me, w, m, v, mask, name):
    def body(me_ref, *refs):
        refs = list(refs)
        p_ref = refs.pop(0)
        own_ref = None if own is None else refs.pop(0)
        w_ref, m_ref, v_ref = refs[:3]
        k_ref = None if mask is None else refs[3]
        g_out, d_out, m_out, v_out = refs[-4:]
        g = _sum_parts(me_ref[0], p_ref, None if own is None else own_ref[me_ref[0]])
        if mask is not None:
            g = g * k_ref[...]
        d, mn, vn = _adamw_math(w_ref[...], g, m_ref[...], v_ref[...])
        g_out[...] = g
        d_out[...] = d
        m_out[...] = mn
        v_out[...] = vn

    def whole(shape):
        nd = len(shape)
        return pl.BlockSpec(shape, lambda i, me_ref: (0,) * nd)

    sds = jax.ShapeDtypeStruct(w.shape, F32)
    ins = [parts] + ([] if own is None else [own]) + [w, m, v] + ([] if mask is None else [mask])
    return pl.pallas_call(
        body, name=name, out_shape=(sds,) * 4,
        grid_spec=pltpu.PrefetchScalarGridSpec(
            num_scalar_prefetch=1, grid=(1,), in_specs=[whole(a.shape) for a in ins],
            out_specs=(whole(w.shape),) * 4),
        compiler_params=_params("arbitrary"))(me, *ins)


_IN_SPLITS = ((0, 512), (512, 1024), (1024, 1536), (1536, 2560), (2560, IN_PAD))


def _prenorm(x, g1, tm, dep=None):
    t_tok = x.shape[0]
    deps = [] if dep is None else [dep]

    def body(x_ref, g_ref, *rest):
        xv = x_ref[...]
        r = lax.rsqrt(jnp.mean(xv * xv, axis=-1, keepdims=True) + EPS)
        rest[-1][...] = (xv * r * g_ref[...]).astype(BF16)

    row = pl.BlockSpec((tm, D_MODEL), lambda i: (i, 0))
    return pl.pallas_call(
        body, name="prenorm", grid=(t_tok // tm,), out_shape=jax.ShapeDtypeStruct((t_tok, D_MODEL), BF16),
        in_specs=[row, _full((1, D_MODEL))] + [pl.BlockSpec(memory_space=pl.ANY)] * len(deps), out_specs=row,
        compiler_params=_params("parallel"))(x, g1, *deps)


def _in_proj(h1, w_in, tm):
    t_tok = h1.shape[0]

    def body(h_ref, w_ref, *outs):
        h = h_ref[...]
        for (a, b), o_ref in zip(_IN_SPLITS, outs):
            o_ref[...] = _dot(h, w_ref[a:b, :], _NT).astype(o_ref.dtype)

    row = lambda n: pl.BlockSpec((tm, n), lambda i: (i, 0))
    widths = [b - a for a, b in _IN_SPLITS]
    dtypes = (BF16, BF16, BF16, F32, F32)
    return pl.pallas_call(
        body, name="in_proj", grid=(t_tok // tm,),
        out_shape=tuple(jax.ShapeDtypeStruct((t_tok, n), dt) for n, dt in zip(widths, dtypes)),
        in_specs=[row(D_MODEL), _full((IN_PAD, D_MODEL))], out_specs=tuple(row(n) for n in widths),
        compiler_params=_params("parallel"))(h1, w_in)


def _lane_masks():
    lane = lax.broadcasted_iota(jnp.int32, (1, 2 * HEAD_DIM), 1)
    left = (lane < HEAD_DIM).astype(F32)
    return left, 1.0 - left


def _stack_pair(v, m_l, m_r):
    return jnp.concatenate([v * m_l, v * m_r], axis=0).astype(BF16)


def _head_mean(x, avg):
    n = avg.shape[0]
    return jnp.concatenate([_split_dot(x[:, n * i:n * (i + 1)], avg, 2) for i in range(x.shape[1] // n)], axis=1)


def _gmlp_common(u, v, lnw, lnb, avg, wcat_ref, bias, m_l, m_r):
    ug, dug = _gelu_and_grad(u)
    vg, dvg = _gelu_and_grad(v)
    mu = _head_mean(vg, avg)
    vc = vg - mu
    var = _head_mean(vc * vc, avg)
    rstd = lax.rsqrt(var + EPS)
    vhat = vc * rstd
    vn = vhat * lnw + lnb
    rows = []
    for r in range(u.shape[0] // CHUNK):
        cols = []
        for j in range(N_HEADS // 2):
            pair = vn[CHUNK * r:CHUNK * (r + 1), 128 * j:128 * (j + 1)]
            cols.append(_dot(wcat_ref[j], _stack_pair(pair, m_l, m_r)))
        rows.append(jnp.concatenate(cols, axis=1) + bias)
    mixed = jnp.concatenate(rows, axis=0)
    return ug, dug, dvg, rstd, vhat, vn, mixed


_GMLP_ROWS = 4 * CHUNK


def _gmlp_fwd(u, v, lnw, lnb, wcat, bias, avg):
    t_tok = u.shape[0]
    tm = min(_GMLP_ROWS, t_tok)

    def body(u_ref, v_ref, lnw_ref, lnb_ref, wcat_ref, bias_ref, avg_ref, o_ref):
        m_l, m_r = _lane_masks()
        ug, _, _, _, _, _, mixed = _gmlp_common(
            u_ref[...].astype(F32), v_ref[...].astype(F32), lnw_ref[...], lnb_ref[...], avg_ref[...], wcat_ref,
            bias_ref[...], m_l, m_r)
        o_ref[...] = (ug * mixed).astype(BF16)

    row = pl.BlockSpec((tm, GM_WIDTH), lambda i: (i, 0))
    return pl.pallas_call(
        body, name="gmlp_fwd", grid=(t_tok // tm,), out_shape=jax.ShapeDtypeStruct((t_tok, GM_WIDTH), BF16),
        in_specs=[row, row, _full((1, GM_WIDTH)), _full((1, GM_WIDTH)), _full(wcat.shape), _full(bias.shape),
                  _full(avg.shape)],
        out_specs=row, compiler_params=_params("parallel"))(u, v, lnw, lnb, wcat, bias, avg)


def _shift_rows(x, edge, j, down):
    groups, cols = x.shape[0] // 8, x.shape[1]
    amount = j if down else 8 - j
    rot = pltpu.roll(x.reshape(groups, 8, cols), amount, axis=1)
    edge_rot = pltpu.roll(edge, amount, axis=0)[None]
    sub = lax.broadcasted_iota(jnp.int32, (1, 8, 1), 1)
    if down:
        out = jnp.where(sub < j, jnp.concatenate([edge_rot, rot[:-1]], axis=0), rot)
    else:
        out = jnp.where(sub < 8 - j, rot, jnp.concatenate([rot[1:], edge_rot], axis=0))
    return out.reshape(x.shape)


def _conv_pre(xbc, tail, cw_ref, cb):
    taps = [_shift_rows(xbc, tail, 3 - k, True) for k in range(3)] + [xbc]
    return cb + cw_ref[0:1, :] * taps[0] + cw_ref[1:2, :] * taps[1] + cw_ref[2:3, :] * taps[2] + cw_ref[3:4, :] * taps[3]


def _ssd_common(pre, dtr, dtb, alog, expand, tril):
    q = CHUNK
    sg = jax.nn.sigmoid(pre)
    act = pre * sg
    lane = lax.broadcasted_iota(jnp.int32, (1, CHUNK), 1)
    a_row = jnp.where(lane < N_HEADS, -jnp.exp(alog), 0.0)
    dtp = dtr + dtb
    dt = _softplus(dtp)
    a_cs = _split_dot_left(tril, dt * a_row, 3)
    a_cs_t = a_cs.T
    dt_exp = _split_dot(dt, expand, 3)
    a_exp = _split_dot(a_cs, expand, 3)
    a_end = a_exp[q - 1:q, :]
    li = lax.broadcasted_iota(jnp.int32, (q, q), 0)
    si = lax.broadcasted_iota(jnp.int32, (q, q), 1)
    causal = si <= li
    decay = []
    for h in range(N_HEADS):
        seg = a_cs[:, h:h + 1] - a_cs_t[h:h + 1, :]
        decay.append(jnp.where(causal, jnp.exp(jnp.minimum(seg, 0.0)), 0.0))
    return dict(pre=pre, sg=sg, act=act, a_row=a_row, dtp=dtp, dt=dt, dt_exp=dt_exp, a_exp=a_exp,
                e=jnp.exp(a_exp), w_end=jnp.exp(a_end - a_exp), cd=jnp.exp(a_end), decay=decay)


def _ssd_specs(t_tok, seq, reverse):
    nb, nc = t_tok // seq, seq // CHUNK

    def chunk(c):
        return nc - 1 - c if reverse else c

    def row(n, col=0):
        return pl.BlockSpec((nb, CHUNK, n), lambda c: (0, chunk(c), col))

    tail = pl.BlockSpec((nb, 8, CONV_CH), lambda c: (0, jnp.maximum(chunk(c) * (CHUNK // 8) - 1, 0), 0))
    states = pl.BlockSpec((nb, 1, N_STATE, SSM_WIDTH), lambda c: (0, chunk(c), 0, 0))
    fold = lambda a: a.reshape(nb, seq, a.shape[-1])
    unfold = lambda a: a.reshape(t_tok, a.shape[-1])
    return nb, nc, row, tail, states, fold, unfold


def _ssd_fwd(z, xbc, dtr, cw, cb, dtb, alog, dskip_exp, nw, expand, tril, seq):
    t_tok = z.shape[0]
    nb, nc, row, tail, states_spec, fold, unfold = _ssd_specs(t_tok, seq, False)

    def body(z_ref, xbc_ref, tail_ref, dtr_ref, cw_ref, cb_ref, dtb_ref, alog_ref, dsk_ref, nw_ref, exp_ref,
             tril_ref, o_ref, y_ref, st_ref, pre_ref, state_ref):
        c = pl.program_id(0)

        @pl.when(c == 0)
        def _():
            state_ref[...] = jnp.zeros_like(state_ref)

        m_l, m_r = _lane_masks()
        for s in range(nb):
            pre = _conv_pre(xbc_ref[s], jnp.where(c == 0, 0.0, tail_ref[s]), cw_ref, cb_ref[...])
            pre_ref[s] = pre
            f = _ssd_common(pre, dtr_ref[s], dtb_ref[...], alog_ref[...], exp_ref[...], tril_ref[...])
            act = f["act"]
            xs = act[:, :SSM_WIDTH]
            xdt = xs * f["dt_exp"]
            xw = xdt * f["w_end"]
            state = state_ref[s]
            st_ref[s, 0] = state
            ydiag, yoff, snew = [], [], []
            for g in range(2):
                bg = act[:, 512 + 128 * g:640 + 128 * g].astype(BF16)
                cg = act[:, 768 + 128 * g:896 + 128 * g].astype(BF16)
                cb_mat = _dot(cg, bg, _NT)
                for pr in range(2):
                    h0 = 4 * g + 2 * pr
                    gcat = jnp.concatenate(
                        [(cb_mat * f["decay"][h0]).astype(BF16), (cb_mat * f["decay"][h0 + 1]).astype(BF16)], axis=1)
                    ydiag.append(_dot(gcat, _stack_pair(xdt[:, 64 * h0:64 * h0 + 128], m_l, m_r)))
                yoff.append(_dot(cg, state[:, 256 * g:256 * (g + 1)].astype(BF16)))
                snew.append(_dot(bg, xw[:, 256 * g:256 * (g + 1)].astype(BF16), _TN))
            y = jnp.concatenate(ydiag, axis=1) + f["e"] * jnp.concatenate(yoff, axis=1) + dsk_ref[...] * xs
            state_ref[s] = state * f["cd"] + jnp.concatenate(snew, axis=1)
            y_ref[s] = y
            zv = z_ref[s].astype(F32)
            yg = y * (zv * jax.nn.sigmoid(zv))
            outs = []
            for g in range(2):
                ygg = yg[:, 256 * g:256 * (g + 1)]
                outs.append(ygg * lax.rsqrt(jnp.mean(ygg * ygg, axis=-1, keepdims=True) + EPS))
            o_ref[s] = (jnp.concatenate(outs, axis=1) * nw_ref[...]).astype(BF16)

    consts = [cw, cb, dtb, alog, dskip_exp, nw, expand, tril]
    sd = lambda n, dt: jax.ShapeDtypeStruct((nb, seq, n), dt)
    o, y, states, pre = pl.pallas_call(
        body, name="ssd_fwd", grid=(nc,),
        out_shape=(sd(SSM_WIDTH, BF16), sd(SSM_WIDTH, F32), jax.ShapeDtypeStruct((nb, nc, N_STATE, SSM_WIDTH), F32),
                   sd(CONV_CH, F32)),
        in_specs=[row(SSM_WIDTH), row(CONV_CH), tail, row(CHUNK)] + [_full(a.shape) for a in consts],
        out_specs=(row(SSM_WIDTH), row(SSM_WIDTH), states_spec, row(CONV_CH)),
        scratch_shapes=[pltpu.VMEM((nb, N_STATE, SSM_WIDTH), F32)],
        compiler_params=_params("arbitrary"))(fold(z), fold(xbc), fold(xbc), fold(dtr), *consts)
    return unfold(o), unfold(y), states, unfold(pre)


def _out_proj(mix_a, mix_b, w_out, x, g2, g3, tm, dep=None):
    t_tok = x.shape[0]
    deps = [] if dep is None else [dep]

    def body(a_ref, b_ref, w_ref, x_ref, g2_ref, g3_ref, *rest):
        o_ref, x2_ref, h3_ref = rest[-3:]
        o = _dot(a_ref[...], w_ref[0:GM_WIDTH, :]) + _dot(b_ref[...], w_ref[GM_WIDTH:, :])
        o_ref[...] = o
        r2 = lax.rsqrt(jnp.mean(o * o, axis=-1, keepdims=True) + EPS)
        x2 = x_ref[...] + o * r2 * g2_ref[...]
        x2_ref[...] = x2
        r3 = lax.rsqrt(jnp.mean(x2 * x2, axis=-1, keepdims=True) + EPS)
        h3_ref[...] = (x2 * r3 * g3_ref[...]).astype(BF16)

    row = lambda n: pl.BlockSpec((tm, n), lambda i: (i, 0))
    sd = lambda dt: jax.ShapeDtypeStruct((t_tok, D_MODEL), dt)
    return pl.pallas_call(
        body, name="out_proj", grid=(t_tok // tm,), out_shape=(sd(F32), sd(F32), sd(BF16)),
        in_specs=[row(GM_WIDTH), row(SSM_WIDTH), _full((D_MODEL, D_MODEL)), row(D_MODEL), _full((1, D_MODEL)),
                  _full((1, D_MODEL))] + [pl.BlockSpec(memory_space=pl.ANY)] * len(deps),
        out_specs=(row(D_MODEL),) * 3, compiler_params=_params("parallel"))(mix_a, mix_b, w_out, x, g2, g3, *deps)


def _mlp_fwd(h3, w_up, w_down, x2, target, g4, tm, tf):
    t_tok = x2.shape[0]

    def up_body(h_ref, wu_ref, ra_ref):
        ra_ref[...] = jnp.maximum(_dot(h_ref[...], wu_ref[...]), 0.0).astype(BF16)

    ra = pl.pallas_call(
        up_body, name="mlp_up", grid=(D_FF // tf, t_tok // tm), out_shape=jax.ShapeDtypeStruct((t_tok, D_FF), BF16),
        in_specs=[pl.BlockSpec((tm, D_MODEL), lambda j, i: (i, 0)), pl.BlockSpec((D_MODEL, tf), lambda j, i: (0, j))],
        out_specs=pl.BlockSpec((tm, tf), lambda j, i: (i, j)), compiler_params=_params("parallel", "parallel"))(h3, w_up)

    def down_body(ra_ref, wd_ref, x2_ref, t_ref, g4_ref, dd_ref, dy_ref, dg4_ref, loss_ref):
        i = pl.program_id(0)
        rav = ra_ref[...]
        dvec = _dot(rav * rav, wd_ref[...])
        r4 = lax.rsqrt(jnp.mean(dvec * dvec, axis=-1, keepdims=True) + EPS)
        dn = dvec * r4
        g4 = g4_ref[...]
        err = x2_ref[...] + dn * g4 - t_ref[...]
        dy = err * (1.0 / D_MODEL)
        dy_ref[...] = dy
        dg = dy * g4
        dd_ref[...] = (r4 * (dg - dn * jnp.mean(dg * dn, axis=-1, keepdims=True))).astype(BF16)
        _acc_rows(dg4_ref, _rsum(dy * dn), i == 0)
        tile_loss = 0.5 * jnp.sum(jnp.sum(err * err, axis=-1, keepdims=True), axis=0, keepdims=True) / D_MODEL
        _acc_rows(loss_ref, jnp.broadcast_to(tile_loss, (1, 128)), i == 0)

    row = pl.BlockSpec((tm, D_MODEL), lambda i: (i, 0))
    dd, dy, dg4, loss = pl.pallas_call(
        down_body, name="mlp_down", grid=(t_tok // tm,),
        out_shape=(jax.ShapeDtypeStruct((t_tok, D_MODEL), BF16), jax.ShapeDtypeStruct((t_tok, D_MODEL), F32),
                   jax.ShapeDtypeStruct((1, D_MODEL), F32), jax.ShapeDtypeStruct((1, 128), F32)),
        in_specs=[pl.BlockSpec((tm, D_FF), lambda i: (i, 0)), _full((D_FF, D_MODEL)), row, row, _full((1, D_MODEL))],
        out_specs=(row, row, _full((1, D_MODEL)), _full((1, 128))),
        compiler_params=_params("arbitrary"))(ra, w_down, x2, target, g4)
    return ra, dd, dy, dg4, loss


def _mlp_bwd(dd, w_down, ra, w_up, x2, dy, o, g3, g2, tm, tf):
    t_tok = x2.shape[0]

    def hidden_body(dd_ref, wd_ref, ra_ref, da_ref):
        df = _dot(dd_ref[...], wd_ref[...], _NT)
        da_ref[...] = (df * (2.0 * ra_ref[...].astype(F32))).astype(BF16)

    da = pl.pallas_call(
        hidden_body, name="mlp_bwd_hidden", grid=(D_FF // tf, t_tok // tm),
        out_shape=jax.ShapeDtypeStruct((t_tok, D_FF), BF16),
        in_specs=[pl.BlockSpec((tm, D_MODEL), lambda j, i: (i, 0)), pl.BlockSpec((tf, D_MODEL), lambda j, i: (j, 0)),
                  pl.BlockSpec((tm, tf), lambda j, i: (i, j))],
        out_specs=pl.BlockSpec((tm, tf), lambda j, i: (i, j)),
        compiler_params=_params("parallel", "parallel"))(dd, w_down, ra)

    def in_body(da_ref, wu_ref, x2_ref, dy_ref, o_ref, g3_ref, g2_ref, dx2_ref, do_ref, dg3_ref, dg2_ref):
        i = pl.program_id(0)
        dh3 = _dot(da_ref[...], wu_ref[...], _NT)
        dn3, dg3 = _rms_bwd(x2_ref[...], g3_ref[...], dh3)
        dx2 = dy_ref[...] + dn3
        dx2_ref[...] = dx2
        do, dg2 = _rms_bwd(o_ref[...], g2_ref[...], dx2)
        do_ref[...] = do.astype(BF16)
        _acc_rows(dg3_ref, dg3, i == 0)
        _acc_rows(dg2_ref, dg2, i == 0)

    row = pl.BlockSpec((tm, D_MODEL), lambda i: (i, 0))
    vec = _full((1, D_MODEL))
    sd = lambda dt: jax.ShapeDtypeStruct((t_tok, D_MODEL), dt)
    dx2, do, dg3, dg2 = pl.pallas_call(
        in_body, name="mlp_bwd_in", grid=(t_tok // tm,),
        out_shape=(sd(F32), sd(BF16), jax.ShapeDtypeStruct((1, D_MODEL), F32), jax.ShapeDtypeStruct((1, D_MODEL), F32)),
        in_specs=[pl.BlockSpec((tm, D_FF), lambda i: (i, 0)), _full((D_MODEL, D_FF)), row, row, row, vec, vec],
        out_specs=(row, row, vec, vec), compiler_params=_params("arbitrary"))(da, w_up, x2, dy, o, g3, g2)
    return da, dx2, do, dg3, dg2


def _wgrad(a, b, out_blocks, bm, bn, bk, square_a, name, dep=None):
    t_tok, m = a.shape
    n = b.shape[1]
    nk = t_tok // bk

    def body(a_ref, b_ref, *rest):
        o_ref, acc_ref = rest[-2:]
        k = pl.program_id(2)
        av = a_ref[...]
        if square_a:
            av = av * av
        part = _dot(av, b_ref[...], _TN)

        def emit(res):
            if out_blocks is None:
                o_ref[...] = res.astype(BF16)
            else:
                o_ref[0] = res.astype(BF16)

        if nk == 1:
            emit(part)
            return

        @pl.when(k == 0)
        def _():
            acc_ref[...] = part

        @pl.when(k > 0)
        def _():
            acc_ref[...] += part

        @pl.when(k == nk - 1)
        def _():
            emit(acc_ref[...])

    if out_blocks is None:
        out_shape = jax.ShapeDtypeStruct((m, n), BF16)
        out_spec = pl.BlockSpec((bm, bn), lambda i, j, k: (i, j))
    else:
        assert n // out_blocks == bn
        out_shape = jax.ShapeDtypeStruct((out_blocks, m, bn), BF16)
        out_spec = pl.BlockSpec((1, bm, bn), lambda i, j, k: (j, i, 0))
    deps = [] if dep is None else [dep]
    return pl.pallas_call(
        body, name=name, grid=(m // bm, n // bn, nk), out_shape=out_shape,
        in_specs=[pl.BlockSpec((bk, bm), lambda i, j, k: (k, i)), pl.BlockSpec((bk, bn), lambda i, j, k: (k, j))]
        + [pl.BlockSpec(memory_space=pl.ANY)] * len(deps),
        out_specs=out_spec, scratch_shapes=[pltpu.VMEM((bm, bn) if nk > 1 else (8, 128), F32)],
        compiler_params=_params("parallel", "parallel", "arbitrary"))(a, b, *deps)


def _wgrad_in(h1, pieces, bn, name, dep=None):
    t_tok = h1.shape[0]
    widths = [p.shape[1] for p in pieces]
    starts = [sum(widths[:i]) for i in range(len(widths))]

    def body(h_ref, *rest):
        piece_refs = rest[:len(widths)]
        o_ref = rest[-1]
        hv = h_ref[...]
        for a, n, r in zip(starts, widths, piece_refs):
            o_ref[a:a + n, :] = _dot(r[...], hv, _TN).astype(BF16)

    deps = [] if dep is None else [dep]
    return pl.pallas_call(
        body, name=name, grid=(D_MODEL // bn,), out_shape=jax.ShapeDtypeStruct((sum(widths), D_MODEL), BF16),
        in_specs=[pl.BlockSpec((t_tok, bn), lambda j: (0, j))] + [pl.BlockSpec((t_tok, n), lambda j: (0, 0)) for n in widths]
        + [pl.BlockSpec(memory_space=pl.ANY)] * len(deps),
        out_specs=pl.BlockSpec((sum(widths), bn), lambda j: (0, j)),
        compiler_params=_params("parallel"))(h1, *pieces, *deps)


def _dmix(do, w_out, tm, dep=None):
    t_tok = do.shape[0]

    def body(d_ref, w_ref, *rest):
        rest[-1][...] = _dot(d_ref[...], w_ref[...], _NT)

    row = pl.BlockSpec((tm, D_MODEL), lambda i: (i, 0))
    deps = [] if dep is None else [dep]
    return pl.pallas_call(
        body, name="dmix", grid=(t_tok // tm,), out_shape=jax.ShapeDtypeStruct((t_tok, D_MODEL), F32),
        in_specs=[row, _full((D_MODEL, D_MODEL))] + [pl.BlockSpec(memory_space=pl.ANY)] * len(deps), out_specs=row,
        compiler_params=_params("parallel"))(do, w_out, *deps)


def _gmlp_bwd(dmix, u, v, lnw, lnb, wcat, wtcat, bias, avg, expand_t):
    t_tok = u.shape[0]
    tm = min(_GMLP_ROWS, t_tok)

    def body(dm_ref, u_ref, v_ref, lnw_ref, lnb_ref, wcat_ref, wtcat_ref, bias_ref, avg_ref, expt_ref, du_ref, dv_ref,
             dw_ref, db_ref, dlnw_ref, dlnb_ref):
        i = pl.program_id(0)
        m_l, m_r = _lane_masks()
        avg = avg_ref[...]
        lnw = lnw_ref[...]
        ug, dug, dvg, rstd, vhat, vn, mixed = _gmlp_common(
            u_ref[...].astype(F32), v_ref[...].astype(F32), lnw, lnb_ref[...], avg, wcat_ref, bias_ref[...], m_l, m_r)
        dya = dm_ref[...]
        du_ref[...] = (dya * mixed * dug).astype(BF16)
        dmixed = dya * ug
        dvn_rows, dws, dbt = [], [None] * N_HEADS, None
        for r in range(tm // CHUNK):
            dvn_cols = []
            for j in range(N_HEADS // 2):
                dmp = dmixed[CHUNK * r:CHUNK * (r + 1), 128 * j:128 * (j + 1)]
                dvn_cols.append(_dot(wtcat_ref[j], _stack_pair(dmp, m_l, m_r)))
                vnp = vn[CHUNK * r:CHUNK * (r + 1), 128 * j:128 * (j + 1)].astype(BF16)
                for i_h, mask in enumerate((m_l, m_r)):
                    part = _dot((dmp * mask).astype(BF16), vnp, _NT)
                    dws[2 * j + i_h] = part if r == 0 else dws[2 * j + i_h] + part
            dvn_rows.append(jnp.concatenate(dvn_cols, axis=1))
            part = _split_dot(dmixed[CHUNK * r:CHUNK * (r + 1), :], expt_ref[...], 2)
            dbt = part if r == 0 else dbt + part
        dvn = jnp.concatenate(dvn_rows, axis=0)
        dvh = dvn * lnw
        dvgel = rstd * (dvh - _head_mean(dvh, avg) - vhat * _head_mean(dvh * vhat, avg))
        dv_ref[...] = (dvgel * dvg).astype(BF16)
        first = i == 0

        @pl.when(first)
        def _():
            for h in range(N_HEADS):
                dw_ref[h] = dws[h]
            db_ref[...] = dbt

        @pl.when(jnp.logical_not(first))
        def _():
            for h in range(N_HEADS):
                dw_ref[h] += dws[h]
            db_ref[...] += dbt

        _acc_rows(dlnw_ref, _rsum(dvn * vhat), first)
        _acc_rows(dlnb_ref, _rsum(dvn), first)

    row = pl.BlockSpec((tm, GM_WIDTH), lambda i: (i, 0))
    consts = [lnw, lnb, wcat, wtcat, bias, avg, expand_t]
    return pl.pallas_call(
        body, name="gmlp_bwd", grid=(t_tok // tm,),
        out_shape=(jax.ShapeDtypeStruct((t_tok, GM_WIDTH), BF16), jax.ShapeDtypeStruct((t_tok, GM_WIDTH), BF16),
                   jax.ShapeDtypeStruct((N_HEADS, CHUNK, CHUNK), F32), jax.ShapeDtypeStruct((CHUNK, CHUNK), F32),
                   jax.ShapeDtypeStruct((1, GM_WIDTH), F32), jax.ShapeDtypeStruct((1, GM_WIDTH), F32)),
        in_specs=[row, row, row] + [_full(a.shape) for a in consts],
        out_specs=(row, row, _full((N_HEADS, CHUNK, CHUNK)), _full((CHUNK, CHUNK)), _full((1, GM_WIDTH)),
                   _full((1, GM_WIDTH))),
        compiler_params=_params("arbitrary"))(dmix, u, v, *consts)


def _ssd_bwd(dmix, z, xbc, pre, dtr, y, states, cw, cb, dtb, alog, dskip_exp, nw, expand, expand_t, tril, triu, seq,
             dep=None):
    t_tok = z.shape[0]
    nb, nc, row, _, states_spec, fold, unfold = _ssd_specs(t_tok, seq, True)
    q = CHUNK

    def one_sequence(s, dm_ref, z_ref, xbc_ref, pre_ref, dtr_ref, y_ref, st_ref, cw_ref, dtb_ref, alog_ref, dsk_ref,
                     nw_ref, exp_ref, expt_ref, tril_ref, triu_ref, dz_ref, dxbc_ref, ddt_ref, dhead_ref, dstate_ref):
        m_l, m_r = _lane_masks()
        expt = expt_ref[...]
        f = _ssd_common(pre_ref[s], dtr_ref[s], dtb_ref[...], alog_ref[...], exp_ref[...], tril_ref[...])
        act, pre, sg = f["act"], f["pre"], f["sg"]
        xs = act[:, :SSM_WIDTH]
        xdt = xs * f["dt_exp"]
        xw = xdt * f["w_end"]
        state = st_ref[s, 0]
        dstate = dstate_ref[s]
        zv, yv, dout, nw = z_ref[s].astype(F32), y_ref[s], dm_ref[s], nw_ref[...]
        sz = jax.nn.sigmoid(zv)
        sl = zv * sz
        yg = yv * sl
        tv = dout * nw
        dyg_parts, ygh_parts = [], []
        for g in range(2):
            ygg = yg[:, 256 * g:256 * (g + 1)]
            rr = lax.rsqrt(jnp.mean(ygg * ygg, axis=-1, keepdims=True) + EPS)
            ygh = ygg * rr
            tg = tv[:, 256 * g:256 * (g + 1)]
            dyg_parts.append(rr * (tg - ygh * jnp.mean(tg * ygh, axis=-1, keepdims=True)))
            ygh_parts.append(ygh)
        dyg = jnp.concatenate(dyg_parts, axis=1)
        dnw = _rsum(dout * jnp.concatenate(ygh_parts, axis=1))
        dy = dyg * sl
        dz_ref[s] = (dyg * yv * (sz * (1.0 + zv * (1.0 - sz)))).astype(BF16)
        ddsk = _rsum(dy * xs)
        dye = dy * f["e"]
        lane = lax.broadcasted_iota(jnp.int32, (q, q), 1)
        sub = lax.broadcasted_iota(jnp.int32, (q, q), 0)
        rs_mat = jnp.zeros((q, q), F32)
        cs_mat = jnp.zeros((q, q), F32)
        dxdt_cols, yoff, dst_in, dxw, d_b, d_c = [], [], [], [], [], []
        for g in range(2):
            bg = act[:, 512 + 128 * g:640 + 128 * g].astype(BF16)
            cg = act[:, 768 + 128 * g:896 + 128 * g].astype(BF16)
            cb_mat = _dot(cg, bg, _NT)
            stg = state[:, 256 * g:256 * (g + 1)].astype(BF16)
            dyeg = dye[:, 256 * g:256 * (g + 1)].astype(BF16)
            yoff.append(_dot(cg, stg))
            dcg = _dot(dyeg, stg, _NT)
            dst_in.append(_dot(cg, dyeg, _TN))
            dcb = jnp.zeros((q, q), F32)
            for pr in range(2):
                h0 = 4 * g + 2 * pr
                gf = [cb_mat * f["decay"][h0], cb_mat * f["decay"][h0 + 1]]
                gcat = jnp.concatenate([gf[0].astype(BF16), gf[1].astype(BF16)], axis=1)
                xst = _stack_pair(xdt[:, 64 * h0:64 * h0 + 128], m_l, m_r)
                dyp = dy[:, 64 * h0:64 * h0 + 128].astype(BF16)
                dgcat = _dot(dyp, xst, _NT)
                dxst = _dot(gcat, dyp, _TN)
                dxdt_cols.append(dxst[:q] * m_l + dxst[q:] * m_r)
                for i in range(2):
                    h = h0 + i
                    dg = dgcat[:, q * i:q * (i + 1)]
                    mm = dg * gf[i]
                    rs_mat = rs_mat + jnp.where(lane == h, jnp.sum(mm, axis=1, keepdims=True), 0.0)
                    cs_mat = cs_mat + jnp.where(sub == h, jnp.sum(mm, axis=0, keepdims=True), 0.0)
                    dcb = dcb + dg * f["decay"][h]
            dcb16 = dcb.astype(BF16)
            dstg = dstate[:, 256 * g:256 * (g + 1)].astype(BF16)
            d_c.append(dcg + _dot(dcb16, bg))
            dxw.append(_dot(bg, dstg))
            d_b.append(_dot(dcb16, cg, _TN) + _dot(xw[:, 256 * g:256 * (g + 1)].astype(BF16), dstg, _NT))
        dxw = jnp.concatenate(dxw, axis=1)
        dxdt = jnp.concatenate(dxdt_cols, axis=1) + dxw * f["w_end"]
        qv = dxw * xw
        end_row = _rsum(qv) + _rsum(dstate * state) * f["cd"]
        x2 = dye * jnp.concatenate(yoff, axis=1) - qv
        row_i = lax.broadcasted_iota(jnp.int32, (q, 1), 0)
        x2 = x2 + jnp.where(row_i == q - 1, end_row, 0.0)
        da_cs = _split_dot(x2, expt, 3) + rs_mat - cs_mat.T
        ddt = _split_dot(dxdt * xs, expt, 3)
        dxs = dsk_ref[...] * dy + dxdt * f["dt_exp"]
        dda = _split_dot_left(triu_ref[...], da_cs, 3)
        ddt = ddt + dda * f["a_row"]
        dalog = _rsum(dda * f["dt"]) * f["a_row"]
        draw = ddt * jax.nn.sigmoid(f["dtp"])
        ddt_ref[s] = draw.astype(BF16)
        dact = jnp.concatenate([dxs] + d_b + d_c, axis=1)
        dpre = dact * (sg * (1.0 + pre * (1.0 - sg)))
        dhead = dhead_ref[s]
        xv = xbc_ref[s]
        shifted = [_shift_rows(dpre, dhead, 3 - k, False) for k in range(3)] + [dpre]
        dxbc = cw_ref[3:4, :] * dpre
        for k in range(3):
            dxbc = dxbc + cw_ref[k:k + 1, :] * shifted[k]
        dxbc_ref[s] = dxbc.astype(BF16)
        dhead_ref[s] = dpre[0:8, :]
        dstate_ref[s] = dstate * f["cd"] + jnp.concatenate(dst_in, axis=1)
        row8 = lax.broadcasted_iota(jnp.int32, (8, 1), 0)
        dcw = jnp.zeros((8, CONV_CH), F32)
        for k in range(4):
            dcw = dcw + jnp.where(row8 == k, _rsum(shifted[k] * xv), 0.0)
        return dcw, _rsum(dpre), _rsum(draw), dalog, _split_dot(ddsk, expt, 3), dnw

    def body(dm_ref, z_ref, xbc_ref, pre_ref, dtr_ref, y_ref, st_ref, cw_ref, cb_ref, dtb_ref, alog_ref, dsk_ref,
             nw_ref, exp_ref, expt_ref, tril_ref, triu_ref, dz_ref, dxbc_ref, ddt_ref, dcw_ref, dcb_ref, ddtb_ref,
             dalog_ref, dd_ref, dnw_ref, dhead_ref, dstate_ref):
        c = pl.program_id(0)
        first = c == 0

        @pl.when(first)
        def _():
            dstate_ref[...] = jnp.zeros_like(dstate_ref)
            dhead_ref[...] = jnp.zeros_like(dhead_ref)

        total = None
        for s in range(nb):
            parts = one_sequence(s, dm_ref, z_ref, xbc_ref, pre_ref, dtr_ref, y_ref, st_ref, cw_ref, dtb_ref, alog_ref,
                                 dsk_ref, nw_ref, exp_ref, expt_ref, tril_ref, triu_ref, dz_ref, dxbc_ref, ddt_ref,
                                 dhead_ref, dstate_ref)
            total = parts if total is None else tuple(a + b for a, b in zip(total, parts))
        dcw = total[0]

        @pl.when(first)
        def _():
            dcw_ref[...] = dcw

        @pl.when(jnp.logical_not(first))
        def _():
            dcw_ref[...] += dcw

        for ref, part in zip((dcb_ref, ddtb_ref, dalog_ref, dd_ref, dnw_ref), total[1:]):
            _acc_rows(ref, part, first)

    consts = [cw, cb, dtb, alog, dskip_exp, nw, expand, expand_t, tril, triu]
    deps = [] if dep is None else [dep]
    n_in = 7 + len(consts)

    def body_skipping_dep(*refs):
        body(*refs[:n_in], *refs[n_in + len(deps):])

    acc = lambda n: jax.ShapeDtypeStruct((1, n), F32)
    sd = lambda n: jax.ShapeDtypeStruct((nb, seq, n), BF16)
    dz, dxbc, ddt, *small_grads = pl.pallas_call(
        body_skipping_dep, name="ssd_bwd", grid=(nc,),
        out_shape=(sd(SSM_WIDTH), sd(CONV_CH), sd(CHUNK), jax.ShapeDtypeStruct((8, CONV_CH), F32), acc(CONV_CH),
                   acc(CHUNK), acc(CHUNK), acc(CHUNK), acc(SSM_WIDTH)),
        in_specs=[row(SSM_WIDTH, col=1), row(SSM_WIDTH), row(CONV_CH), row(CONV_CH), row(CHUNK), row(SSM_WIDTH),
                  states_spec]
        + [_full(a.shape) for a in consts] + [pl.BlockSpec(memory_space=pl.ANY)] * len(deps),
        out_specs=(row(SSM_WIDTH), row(CONV_CH), row(CHUNK), _full((8, CONV_CH)), _full((1, CONV_CH)),
                   _full((1, CHUNK)), _full((1, CHUNK)), _full((1, CHUNK)), _full((1, SSM_WIDTH))),
        scratch_shapes=[pltpu.VMEM((nb, 8, CONV_CH), F32), pltpu.VMEM((nb, N_STATE, SSM_WIDTH), F32)],
        compiler_params=_params("arbitrary"))(
            fold(dmix), fold(z), fold(xbc), fold(pre), fold(dtr), fold(y), states, *consts, *deps)
    return (unfold(dz), unfold(dxbc), unfold(ddt), *small_grads)


def _in_bwd(du, dv, dz, dxbc, ddt, w_in, x, dx2, g1, tm, dep=None):
    t_tok = x.shape[0]

    def body(du_ref, dv_ref, dz_ref, dxbc_ref, ddt_ref, w_ref, x_ref, dx2_ref, g_ref, *rest):
        gx_ref, dg_ref = rest[-2:]
        i = pl.program_id(0)
        dh = None
        for (a, b), ref in zip(_IN_SPLITS, (du_ref, dv_ref, dz_ref, dxbc_ref, ddt_ref)):
            part = _dot(ref[...], w_ref[a:b, :])
            dh = part if dh is None else dh + part
        dn, dg = _rms_bwd(x_ref[...], g_ref[...], dh)
        gx_ref[...] = dx2_ref[...] + dn
        _acc_rows(dg_ref, dg, i == 0)

    row = lambda n: pl.BlockSpec((tm, n), lambda i: (i, 0))
    widths = [b - a for a, b in _IN_SPLITS]
    deps = [] if dep is None else [dep]
    return pl.pallas_call(
        body, name="in_bwd", grid=(t_tok // tm,),
        out_shape=(jax.ShapeDtypeStruct((t_tok, D_MODEL), F32), jax.ShapeDtypeStruct((1, D_MODEL), F32)),
        in_specs=[row(n) for n in widths] + [_full((IN_PAD, D_MODEL)), row(D_MODEL), row(D_MODEL), _full((1, D_MODEL))]
        + [pl.BlockSpec(memory_space=pl.ANY)] * len(deps),
        out_specs=(row(D_MODEL), _full((1, D_MODEL))),
        compiler_params=_params("arbitrary"))(du, dv, dz, dxbc, ddt, w_in, x, dx2, g1, *deps)


def _pad_lanes(a, n):
    return jnp.pad(a, ((0, 0), (0, n - a.shape[1])))


def _local_step(x, target, seq, small, hooks, first_dep=None):
    t_tok = x.shape[0]
    tm = min(512, t_tok)
    avg, expand, expand_t, tril, triu = _const_mats()
    g1, g2, g3, g4 = (small[k].reshape(1, D_MODEL) for k in
                      ("norm_mix_pre", "norm_mix_post", "norm_ffn_pre", "norm_ffn_post"))
    tie = (lambda a: a) if first_dep is None else (lambda a: a + first_dep[0, 0])
    lnw = tie(small["gm_ln_w"]).reshape(1, GM_WIDTH)
    lnb = tie(small["gm_ln_b"]).reshape(1, GM_WIDTH)
    causal = jnp.tril(jnp.ones((CHUNK, CHUNK), F32))
    wm = tie(small["gm_w_s"]) * causal
    pair = lambda w: w.reshape(4, 2, CHUNK, CHUNK).transpose(0, 2, 1, 3).reshape(4, CHUNK, 2 * CHUNK).astype(BF16)
    wcat = pair(wm)
    wtcat = pair(jnp.swapaxes(wm, 1, 2))
    bias = jnp.repeat(tie(small["gm_b_s"]).T, HEAD_DIM, axis=1)
    cb = small["conv_b"].reshape(1, CONV_CH)
    dtb = _pad_lanes(tie(small["dt_bias"]).reshape(1, N_HEADS), CHUNK)
    alog = _pad_lanes(tie(small["a_log"]).reshape(1, N_HEADS), CHUNK)
    dskip_exp = jnp.repeat(tie(small["d_skip"]).reshape(1, N_HEADS), HEAD_DIM, axis=1)
    nw = small["ssm_norm_w"].reshape(1, SSM_WIDTH)

    h1 = _prenorm(x, g1, tm, hooks.get("prenorm_after", first_dep))
    w_in_t, conv_w = hooks["mixer_weights"](h1)
    u, v, z, xbc, dtr = _in_proj(h1, w_in_t, tm)
    mix_a = _gmlp_fwd(u, v, lnw, lnb, wcat, bias, avg)
    mix_b, y_pre, states, pre = _ssd_fwd(z, xbc, dtr, conv_w, cb, dtb, alog, dskip_exp, nw, expand, tril, seq)
    w_out, dep = hooks["mixers_done"](mix_b)
    o, x2, h3 = _out_proj(mix_a, mix_b, w_out, x, g2, g3, tm, dep)
    w_up, w_down = hooks["mlp_weights"](h3)
    tf = 2048
    ra, dd, dy, dg4, loss = _mlp_fwd(h3, w_up, w_down, x2, target, g4, tm, tf)

    da, dx2, do, dg3, dg2 = _mlp_bwd(dd, w_down, ra, w_up, x2, dy, o, g3, g2, tm, tf)
    g_w_down = _wgrad(ra, dd, None, 512, D_MODEL, t_tok, True, "wgrad_down")
    g_w_up = _wgrad(h3, da, N_DEV, D_MODEL, D_FF // N_DEV, t_tok, False, "wgrad_up")
    dep = hooks["mlp_grads"](g_w_down, g_w_up)
    dmix = _dmix(do, w_out, tm, dep)
    g_w_out = _wgrad_in(do, (mix_a, mix_b), 512, "wgrad_out", dep)
    du, dv, dws, dbt, dlnw, dlnb = _gmlp_bwd(dmix, u, v, lnw, lnb, wcat, wtcat, bias, avg, expand_t)
    dep = hooks["gmlp_grads"](g_w_out, dws)
    dz, dxbc, ddt, dcw, dcb, ddtb, dalog, ddsk, dnw = _ssd_bwd(
        dmix, z, xbc, pre, dtr, y_pre, states, conv_w, cb, dtb, alog, dskip_exp, nw, expand, expand_t, tril, triu, seq,
        dep)
    g_w_in = jnp.concatenate([_wgrad_in(h1, (du, dv, dz), 512, "wgrad_in_a", dep),
                              _wgrad_in(h1, (dxbc, ddt), 512, "wgrad_in_b", dep)], axis=0)
    dep = hooks["in_grads"](g_w_in, dcw[0:4])
    grad_x, dg1 = _in_bwd(du, dv, dz, dxbc, ddt, w_in_t, x, dx2, g1, tm, dep)

    grads = dict(
        w_in=g_w_in, w_out=g_w_out, w_up=g_w_up, w_down=g_w_down, conv_w=dcw[0:4],
        norm_mix_pre=dg1, norm_mix_post=dg2, norm_ffn_pre=dg3, norm_ffn_post=dg4, gm_ln_w=dlnw, gm_ln_b=dlnb,
        gm_w_s=dws, gm_b_s=dbt, conv_b=dcb, dt_bias=ddtb, a_log=dalog, d_skip=ddsk, ssm_norm_w=dnw)
    return loss[0, 0], grad_x, grads


_WEIGHTS = ("norm_mix_pre", "w_in", "gm_ln_w", "gm_ln_b", "gm_w_s", "gm_b_s", "conv_w", "conv_b", "dt_bias", "a_log",
            "d_skip", "ssm_norm_w", "w_out", "norm_mix_post", "norm_ffn_pre", "w_up", "w_down", "norm_ffn_post")
_SLAB_ROWS = (("norm_mix_pre", 1024), ("norm_mix_post", 1024), ("norm_ffn_pre", 1024), ("norm_ffn_post", 1024),
              ("conv_b", 1024), ("ssm_norm_w", 512), ("gm_ln_w", 512), ("gm_ln_b", 512), ("dt_bias", 8), ("a_log", 8),
              ("d_skip", 8))
_SLAB_LOSS_ROW = len(_SLAB_ROWS)
_SLAB_BS_ROW = 16
_SLAB_HEIGHT = 24
_SMALL_PARAMS = tuple(name for name, _ in _SLAB_ROWS) + ("gm_b_s",)
_LN_PARAMS = ("gm_ln_w", "gm_ln_b")


def _pack_slab(g, loss_part):
    rows = [_pad_lanes(g[name], D_MODEL) for name, _ in _SLAB_ROWS]
    rows.append(jnp.broadcast_to(loss_part, (1, D_MODEL)))
    rows.append(jnp.zeros((_SLAB_BS_ROW - len(rows), D_MODEL), F32))
    rows.append(_pad_lanes(g["gm_b_s"].T[0:N_HEADS], D_MODEL))
    return jnp.concatenate(rows, axis=0)


def _adamw_slab(parts, w, m, v):
    names = _SMALL_PARAMS
    shapes = [w[k].shape for k in names]
    unfold = np.zeros((GM_WIDTH, HEAD_DIM), np.float32)
    for h in range(N_HEADS):
        unfold[h * HEAD_DIM:(h + 1) * HEAD_DIM, :] = np.eye(HEAD_DIM)
    unfold = jnp.asarray(unfold, dtype=BF16)
    n = len(names)

    def body(p_ref, unfold_ref, *refs):
        w_refs, m_refs, v_refs = refs[:n], refs[n:2 * n], refs[2 * n:3 * n]
        outs = refs[3 * n:]
        g_all = p_ref[0]
        for j in range(1, N_DEV):
            g_all = g_all + p_ref[j]
        lane = lax.broadcasted_iota(jnp.int32, (N_HEADS, GM_WIDTH), 1)
        head = lax.broadcasted_iota(jnp.int32, (N_HEADS, GM_WIDTH), 0)
        own_lanes = jnp.logical_and(lane >= head * HEAD_DIM, lane < (head + 1) * HEAD_DIM)
        for i, name in enumerate(names):
            if name == "gm_b_s":
                g = g_all[_SLAB_BS_ROW:_SLAB_BS_ROW + N_HEADS, 0:CHUNK]
            else:
                row = [r for r, (k, _) in enumerate(_SLAB_ROWS) if k == name][0]
                g = g_all[row:row + 1, 0:dict(_SLAB_ROWS)[name]]
                if name in _LN_PARAMS:
                    g = _split_dot(jnp.where(own_lanes, g, 0.0), unfold_ref[...], 3)
            d, mn, vn = _adamw_math(w_refs[i][...], g, m_refs[i][...], v_refs[i][...])
            for o_ref, val in zip(outs[4 * i:4 * i + 4], (g, d, mn, vn)):
                o_ref[...] = val
        outs[-1][...] = g_all[_SLAB_LOSS_ROW:_SLAB_LOSS_ROW + 1, 0:128]

    ins = [parts, unfold] + [d[k] for d in (w, m, v) for k in names]
    out_shape = tuple(jax.ShapeDtypeStruct(s, F32) for s in shapes for _ in range(4)) + (
        jax.ShapeDtypeStruct((1, 128), F32),)
    outs = pl.pallas_call(
        body, name="adamw_small", out_shape=out_shape, grid=(1,), in_specs=[_full(a.shape) for a in ins],
        out_specs=tuple(_full(s.shape) for s in out_shape), compiler_params=_params("arbitrary"))(*ins)
    return {k: tuple(outs[4 * i:4 * i + 4]) for i, k in enumerate(names)}, outs[-1][0, 0]


def kernel(x, norm_mix_pre, w_in, gm_ln_w, gm_ln_b, gm_w_s, gm_b_s, conv_w, conv_b, dt_bias, a_log, d_skip, ssm_norm_w, w_out, norm_mix_post, norm_ffn_pre, w_up, w_down, norm_ffn_post, loss_target, m_norm_mix_pre, m_w_in, m_gm_ln_w, m_gm_ln_b, m_gm_w_s, m_gm_b_s, m_conv_w, m_conv_b, m_dt_bias, m_a_log, m_d_skip, m_ssm_norm_w, m_w_out, m_norm_mix_post, m_norm_ffn_pre, m_w_up, m_w_down, m_norm_ffn_post, v_norm_mix_pre, v_w_in, v_gm_ln_w, v_gm_ln_b, v_gm_w_s, v_gm_b_s, v_conv_w, v_conv_b, v_dt_bias, v_a_log, v_d_skip, v_ssm_norm_w, v_w_out, v_norm_mix_post, v_norm_ffn_pre, v_w_up, v_w_down, v_norm_ffn_post):
    w = dict(norm_mix_pre=norm_mix_pre, w_in=w_in, gm_ln_w=gm_ln_w, gm_ln_b=gm_ln_b, gm_w_s=gm_w_s, gm_b_s=gm_b_s, conv_w=conv_w, conv_b=conv_b, dt_bias=dt_bias, a_log=a_log, d_skip=d_skip, ssm_norm_w=ssm_norm_w, w_out=w_out, norm_mix_post=norm_mix_post, norm_ffn_pre=norm_ffn_pre, w_up=w_up, w_down=w_down, norm_ffn_post=norm_ffn_post)
    m = dict(norm_mix_pre=m_norm_mix_pre, w_in=m_w_in, gm_ln_w=m_gm_ln_w, gm_ln_b=m_gm_ln_b, gm_w_s=m_gm_w_s, gm_b_s=m_gm_b_s, conv_w=m_conv_w, conv_b=m_conv_b, dt_bias=m_dt_bias, a_log=m_a_log, d_skip=m_d_skip, ssm_norm_w=m_ssm_norm_w, w_out=m_w_out, norm_mix_post=m_norm_mix_post, norm_ffn_pre=m_norm_ffn_pre, w_up=m_w_up, w_down=m_w_down, norm_ffn_post=m_norm_ffn_post)
    v = dict(norm_mix_pre=v_norm_mix_pre, w_in=v_w_in, gm_ln_w=v_gm_ln_w, gm_ln_b=v_gm_ln_b, gm_w_s=v_gm_w_s, gm_b_s=v_gm_b_s, conv_w=v_conv_w, conv_b=v_conv_b, dt_bias=v_dt_bias, a_log=v_a_log, d_skip=v_d_skip, ssm_norm_w=v_ssm_norm_w, w_out=v_w_out, norm_mix_post=v_norm_mix_post, norm_ffn_pre=v_norm_ffn_pre, w_up=v_w_up, w_down=v_w_down, norm_ffn_post=v_norm_ffn_post)
    n_batch, seq, _ = x.shape
    shard_in = IN_COLS // N_DEV

    me = (4 * lax.axis_index("x") + 2 * lax.axis_index("y") + lax.axis_index("c")).astype(jnp.int32).reshape(1)

    def in_slot(own):
        return lax.dynamic_update_slice(lax.empty((N_DEV,) + own.shape, own.dtype), own[None],
                                        (me[0],) + (0,) * own.ndim)

    w_in_sh, m_in_sh, v_in_sh = w_in[0].T, m_w_in[0].T, v_w_in[0].T
    first = [_cast_to_slot(w_in_sh, me, shard_in, "cast_w_in"), in_slot(conv_w[0]),
             _cast_to_slot(w_out[0], me, 128, "cast_w_out")]
    ici_1, tok_ici_1 = _exchange_start(first, [True] * 3, _SAME_CORE_PEERS, "gather_mix_ici_start")
    cast_up = _cast_to_slot(w_up[0], me, 1024, "cast_w_up", cols=True, dep=tok_ici_1)
    second = [cast_up, _cast_to_slot(w_down[0], me, 512, "cast_w_down", dep=cast_up)]
    gathering = {}

    def mixer_weights(after):
        bufs = [buf for buf, _ in _exchange_wait(ici_1, after, "gather_mix_ici_wait")]
        d2d_1, tok_d2d_1 = _exchange_start(bufs, [True] * 3, _SIBLING_FORWARD, "gather_mix_d2d_start")
        gathering["mlp_ici"], tok_ici_2 = _exchange_start(
            second, [True] * 2, _SAME_CORE_PEERS, "gather_mlp_ici_start", dep=tok_d2d_1)
        (_, ag_in), (_, ag_conv), (_, ag_out) = _exchange_wait(d2d_1, tok_ici_2, "gather_mix_d2d_wait")
        gathering["w_out"] = ag_out.reshape(D_MODEL, D_MODEL)
        w_in_t = jnp.pad(ag_in.reshape(IN_COLS, D_MODEL), ((0, IN_PAD - IN_COLS), (0, 0)))
        return w_in_t, ag_conv.transpose(1, 0, 2).reshape(4, CONV_CH)

    def mixers_done(after):
        bufs = [buf for buf, _ in _exchange_wait(gathering["mlp_ici"], after, "gather_mlp_ici_wait")]
        gathering["mlp"], tok = _exchange_start(bufs, [True] * 2, _SIBLING_FORWARD, "gather_mlp_d2d_start")
        return gathering["w_out"], tok

    def mlp_weights(after):
        (_, ag_up), (_, ag_down) = _exchange_wait(gathering["mlp"], after, "gather_mlp_d2d_wait")
        return ag_up, ag_down.reshape(D_FF, D_MODEL)

    sent = {}

    def mlp_grads(g_w_down, g_w_up):
        sent["mlp"], tok = _exchange_start(
            [g_w_down.reshape(N_DEV, D_FF // N_DEV, D_MODEL), g_w_up], [False, False], _ALL_PEERS, "grads_mlp_start")
        return tok

    def gmlp_grads(g_w_out, g_w_s):
        sent["gmlp"], tok = _exchange_start(
            [g_w_out.reshape(N_DEV, D_MODEL // N_DEV, D_MODEL), in_slot(g_w_s.astype(BF16))], [False, True], _ALL_PEERS,
            "grads_gmlp_start")
        return tok

    def in_grads(g_w_in_t, g_conv_w):
        g_in_blk = g_w_in_t[:IN_COLS].reshape(N_DEV, shard_in, D_MODEL)
        g_conv_blk = g_conv_w.reshape(4, N_DEV, CONV_CH // N_DEV).transpose(1, 0, 2)
        sent["in"], tok = _exchange_start([g_in_blk, g_conv_blk], [False, False], _ALL_PEERS, "grads_in_start")
        return tok

    small = {k: w[k][0] for k in _SMALL_PARAMS + ("gm_w_s",)}
    loss_part, grad_x, g = _local_step(
        x.reshape(n_batch * seq, D_MODEL), loss_target.reshape(n_batch * seq, D_MODEL), seq, small,
        dict(mixer_weights=mixer_weights, mixers_done=mixers_done, mlp_weights=mlp_weights, mlp_grads=mlp_grads,
             gmlp_grads=gmlp_grads, in_grads=in_grads, prenorm_after=second[1]), first_dep=tok_ici_1)

    sent_rows, tok_rows = _exchange_start([in_slot(_pack_slab(g, loss_part))], [True], _ALL_PEERS, "grads_rows_start")
    (own_down, p_down), (own_up, p_up) = _exchange_wait(sent["mlp"], tok_rows, "grads_mlp_wait")
    res = {}
    res["w_up"] = _adamw_reduce(p_up, own_up, me, w_up[0], m_w_up[0], v_w_up[0], 256, "adamw_w_up")
    res["w_down"] = _adamw_reduce(p_down, own_down, me, w_down[0], m_w_down[0], v_w_down[0], 128, "adamw_w_down")
    (own_out, p_out), (_, p_ws) = _exchange_wait(sent["gmlp"], res["w_down"][1], "grads_gmlp_wait")
    res["w_out"] = _adamw_reduce(p_out, own_out, me, w_out[0], m_w_out[0], v_w_out[0], 128, "adamw_w_out")
    causal = jnp.tril(jnp.ones((1, CHUNK, CHUNK), F32))
    res["gm_w_s"] = _adamw_small(p_ws, None, me, gm_w_s[0], m_gm_w_s[0], v_gm_w_s[0], causal, "adamw_gm_w_s")
    (own_in, p_in), (own_conv, p_conv) = _exchange_wait(sent["in"], res["gm_w_s"][1], "grads_in_wait")
    res["w_in"] = tuple(r.T for r in _adamw_reduce(p_in, own_in, me, w_in_sh, m_in_sh, v_in_sh, shard_in, "adamw_w_in"))
    res["conv_w"] = _adamw_small(p_conv, own_conv, me, conv_w[0], m_conv_w[0], v_conv_w[0], None, "adamw_conv_w")
    ((_, p_rows),) = _exchange_wait(sent_rows, res["w_in"][1], "grads_rows_wait")
    flat = lambda t: t[0] if t.ndim == 3 else t
    small_res, loss = _adamw_slab(p_rows, *({k: flat(d[k]) for k in _SMALL_PARAMS} for d in (w, m, v)))
    res.update(small_res)
    res = {k: tuple(r.reshape(w[k].shape) for r in res[k]) for k in _WEIGHTS}

    outs = [loss, grad_x.reshape(x.shape)]
    for part in range(4):
        outs.extend(res[k][part] for k in _WEIGHTS)
    return tuple(outs)
```

```python
import functools

import jax
import jax.numpy as jnp
import numpy as np
from jax import lax
from jax.experimental import pallas as pl
from jax.experimental.pallas import tpu as pltpu

F32 = jnp.float32
BF16 = jnp.bfloat16

D_MODEL = 1024
GM_WIDTH = 512
SSM_WIDTH = 512
CONV_CH = 1024
N_HEADS = 8
HEAD_DIM = 64
N_STATE = 128
CHUNK = 128
D_FF = 4096
IN_COLS = 2568
IN_PAD = 2688
N_DEV = 8
EPS = 1e-6
ADAM_LR, ADAM_B1, ADAM_B2, ADAM_EPS, ADAM_WD, ADAM_STEP = 0.001, 0.9, 0.999, 1e-08, 0.01, 10
VMEM_LIMIT_BYTES = 56 * 1024 * 1024

_NT = (((1,), (1,)), ((), ()))
_TN = (((0,), (0,)), ((), ()))


def _params(*sem):
    return pltpu.CompilerParams(dimension_semantics=sem or None, vmem_limit_bytes=VMEM_LIMIT_BYTES)


def _dot(a, b, dims=None):
    if dims is None:
        return jnp.dot(a, b, preferred_element_type=F32)
    return lax.dot_general(a, b, dims, preferred_element_type=F32)


def _split_terms(x, terms):
    out, rem = [], x
    for i in range(terms):
        hi = rem.astype(BF16)
        out.append(hi)
        if i + 1 < terms:
            rem = rem - hi.astype(F32)
    return out


def _split_dot(x, m, terms):
    acc = None
    for hi in _split_terms(x, terms):
        part = _dot(hi, m)
        acc = part if acc is None else acc + part
    return acc


def _split_dot_left(m, x, terms):
    acc = None
    for hi in _split_terms(x, terms):
        part = _dot(m, hi)
        acc = part if acc is None else acc + part
    return acc


def _gelu_and_grad(x):
    c = 0.7978845608028654
    inner = c * (x + 0.044715 * x * x * x)
    t = jnp.tanh(inner)
    g = 0.5 * x * (1.0 + t)
    dg = 0.5 * (1.0 + t) + 0.5 * x * (1.0 - t * t) * c * (1.0 + 3.0 * 0.044715 * x * x)
    return g, dg


def _softplus(x):
    return jnp.maximum(x, 0.0) + jnp.log(1.0 + jnp.exp(-jnp.abs(x)))


def _rsum(x):
    return jnp.sum(x, axis=0, keepdims=True)


def _acc_rows(ref, part, first):
    val = jnp.broadcast_to(part, ref.shape)

    @pl.when(first)
    def _():
        ref[...] = val

    @pl.when(jnp.logical_not(first))
    def _():
        ref[...] += val


def _rms_bwd(n, g, dout):
    r = lax.rsqrt(jnp.mean(n * n, axis=-1, keepdims=True) + EPS)
    nh = n * r
    dg = dout * g
    dn = r * (dg - nh * jnp.mean(dg * nh, axis=-1, keepdims=True))
    return dn, _rsum(dout * nh)


def _const_mats():
    avg = np.kron(np.eye(4), np.full((HEAD_DIM, HEAD_DIM), 1.0 / HEAD_DIM))
    expand = np.zeros((CHUNK, SSM_WIDTH), np.float32)
    for h in range(N_HEADS):
        expand[h, h * HEAD_DIM:(h + 1) * HEAD_DIM] = 1.0
    tril = np.tril(np.ones((CHUNK, CHUNK), np.float32))
    as_bf16 = lambda a: jnp.asarray(a, dtype=BF16)
    return as_bf16(avg), as_bf16(expand), as_bf16(expand.T), as_bf16(tril), as_bf16(tril.T)


def _full(shape):
    nd = len(shape)
    return pl.BlockSpec(shape, lambda *_: (0,) * nd)


_HBM = pl.BlockSpec(memory_space=pltpu.HBM)
_SEM = pl.BlockSpec(memory_space=pltpu.SEMAPHORE)
_ALL_PEERS = tuple((k, 0) for k in range(1, N_DEV))
_SAME_CORE_PEERS = ((2, 0), (4, 0), (6, 0))
_SIBLING_FORWARD = ((1, 0), (1, 2), (1, 4), (1, 6))


def _flip(j, k):
    for bit in (4, 2, 1):
        if k & bit:
            j = j + bit - 2 * (j & bit)
    return j


def _copies(src, land, send_sems, recv_sems, hops):
    x, y, c = lax.axis_index("x"), lax.axis_index("y"), lax.axis_index("c")
    me = 4 * x + 2 * y + c
    out = []
    for t in range(len(src)):
        for i, (k, b) in enumerate(hops):
            pos = (1 - x if k & 4 else x, 1 - y if k & 2 else y, 1 - c if k & 1 else c)
            peer = _flip(me, k)
            sem = t * len(hops) + i
            mk = functools.partial(pltpu.make_async_remote_copy, send_sem=send_sems.at[sem], recv_sem=recv_sems.at[sem],
                                   device_id=pos, device_id_type=pl.DeviceIdType.MESH)
            if land[t] is None and src[t].shape[0] != N_DEV:
                width = src[t].shape[1] // N_DEV
                slab = lambda j: src[t].at[:, pl.ds(pl.multiple_of(j * width, 128), width)]
                mine = functools.partial(mk, src_ref=slab(_flip(me, b)), dst_ref=slab(_flip(me, b)))
                theirs = functools.partial(mk, src_ref=slab(_flip(peer, b)), dst_ref=slab(_flip(peer, b)))
            elif land[t] is None:
                mine = functools.partial(mk, src_ref=src[t].at[_flip(me, b)], dst_ref=src[t].at[_flip(me, b)])
                theirs = functools.partial(mk, src_ref=src[t].at[_flip(peer, b)], dst_ref=src[t].at[_flip(peer, b)])
            else:
                assert b == 0
                mine = functools.partial(mk, src_ref=src[t].at[peer], dst_ref=land[t].at[me])
                theirs = functools.partial(mk, src_ref=src[t].at[peer], dst_ref=land[t].at[peer])
            out.append((mine, theirs))
    return out


def _exchange_start(srcs, inplace, peers, name, dep=None):
    n = len(srcs)
    lands = [None if ip else pltpu.with_memory_space_constraint(lax.empty(s.shape, s.dtype), pltpu.HBM)
             for s, ip in zip(srcs, inplace)]
    real_lands = [l for l in lands if l is not None]
    n_l = len(real_lands)
    deps = [] if dep is None else [dep]

    def body(*refs):
        src = refs[:n]
        land_refs = list(refs[n:n + n_l])
        send_sems, recv_sems = refs[n + n_l + len(deps)], refs[n + n_l + len(deps) + 1]
        token = refs[-1]
        land = [None if ip else land_refs.pop(0) for ip in inplace]
        for mine, _ in _copies(src, land, send_sems, recv_sems, peers):
            mine().start()
        token[...] = jnp.zeros_like(token)

    sem_t = pltpu.SemaphoreType.DMA((n * len(peers),))
    outs = pl.pallas_call(
        body, name=name,
        out_shape=(sem_t, sem_t) + tuple(pltpu.HBM(a.shape, a.dtype) for a in list(srcs) + real_lands)
        + (jax.ShapeDtypeStruct((8, 128), F32),),
        in_specs=[_HBM] * (n + n_l) + [pl.BlockSpec(memory_space=pl.ANY)] * len(deps),
        out_specs=(_SEM, _SEM) + (_HBM,) * (n + n_l) + (pl.BlockSpec(memory_space=pltpu.VMEM),),
        input_output_aliases={i: 2 + i for i in range(n + n_l)},
        compiler_params=pltpu.CompilerParams(has_side_effects=pltpu.SideEffectType.DATAFLOW_SIDE_EFFECTING),
    )(*[pltpu.with_memory_space_constraint(s, pltpu.HBM) for s in srcs], *real_lands, *deps)
    handle = dict(send=outs[0], recv=outs[1], srcs=outs[2:2 + n], lands=outs[2 + n:2 + n + n_l], inplace=inplace,
                  peers=peers)
    return handle, outs[-1]


def _exchange_wait(handle, after, name):
    srcs, lands, inplace, peers = handle["srcs"], handle["lands"], handle["inplace"], handle["peers"]
    n, n_l = len(srcs), len(lands)

    def body(*refs):
        src = refs[:n]
        land_refs = list(refs[n:n + n_l])
        send_sems, recv_sems = refs[n + n_l], refs[n + n_l + 1]
        land = [None if ip else land_refs.pop(0) for ip in inplace]
        for mine, theirs in _copies(src, land, send_sems, recv_sems, peers):
            mine().wait_send()
            theirs().wait_recv()

    outs = pl.pallas_call(
        body, name=name, out_shape=tuple(pltpu.HBM(a.shape, a.dtype) for a in list(srcs) + list(lands)),
        in_specs=[_HBM] * (n + n_l) + [_SEM, _SEM, pl.BlockSpec(memory_space=pl.ANY)],
        out_specs=(_HBM,) * (n + n_l), input_output_aliases={i: i for i in range(n + n_l)},
        compiler_params=pltpu.CompilerParams(has_side_effects=pltpu.SideEffectType.DATAFLOW_SIDE_EFFECTING),
    )(*srcs, *lands, handle["send"], handle["recv"], after)
    res, land_out = [], list(outs[n:])
    for t in range(n):
        res.append((outs[t], outs[t] if inplace[t] else land_out.pop(0)))
    return res


def _cast_to_slot(w, me, rows, name, cols=False, dep=None):
    r, cdim = w.shape
    deps = [] if dep is None else [dep]

    def body(me_ref, w_ref, *rest):
        o_ref = rest[-1]
        if cols:
            o_ref[...] = w_ref[...].astype(BF16)
        else:
            o_ref[0] = w_ref[...].astype(BF16)

    if cols:
        out_shape = jax.ShapeDtypeStruct((r, N_DEV * cdim), BF16)
        out_spec = pl.BlockSpec((rows, cdim), lambda i, me_ref: (i, me_ref[0]))
    else:
        out_shape = jax.ShapeDtypeStruct((N_DEV, r, cdim), BF16)
        out_spec = pl.BlockSpec((1, rows, cdim), lambda i, me_ref: (me_ref[0], i, 0))
    return pl.pallas_call(
        body, name=name, out_shape=out_shape,
        grid_spec=pltpu.PrefetchScalarGridSpec(
            num_scalar_prefetch=1, grid=(r // rows,),
            in_specs=[pl.BlockSpec((rows, cdim), lambda i, me_ref: (i, 0))]
            + [pl.BlockSpec(memory_space=pl.ANY)] * len(deps), out_specs=out_spec),
        compiler_params=_params("parallel"))(me, w, *deps)


def _adamw_math(w, g, m, v):
    m = ADAM_B1 * m + (1.0 - ADAM_B1) * g
    v = ADAM_B2 * v + (1.0 - ADAM_B2) * (g * g)
    m_hat = m / (1.0 - ADAM_B1 ** ADAM_STEP)
    v_hat = v / (1.0 - ADAM_B2 ** ADAM_STEP)
    delta = -ADAM_LR * (m_hat / (jnp.sqrt(v_hat) + ADAM_EPS) + ADAM_WD * w)
    return delta, m, v


def _sum_parts(me, p_ref, own):
    g = None
    for j in range(N_DEV):
        term = (p_ref[j] if own is None else jnp.where(me == j, own, p_ref[j])).astype(F32)
        g = term if g is None else g + term
    return g


def _adamw_reduce(parts, own, me, w, m, v, rows, name):
    r, cdim = w.shape

    def body(me_ref, p_ref, own_ref, w_ref, m_ref, v_ref, g_out, d_out, m_out, v_out):
        g = _sum_parts(me_ref[0], p_ref, own_ref[0])
        d, mn, vn = _adamw_math(w_ref[...], g, m_ref[...], v_ref[...])
        g_out[...] = g
        d_out[...] = d
        m_out[...] = mn
        v_out[...] = vn

    blk = pl.BlockSpec((rows, cdim), lambda i, me_ref: (i, 0))
    sds = jax.ShapeDtypeStruct(w.shape, F32)
    return pl.pallas_call(
        body, name=name, out_shape=(sds,) * 4,
        grid_spec=pltpu.PrefetchScalarGridSpec(
            num_scalar_prefetch=1, grid=(r // rows,),
            in_specs=[pl.BlockSpec((N_DEV, rows, cdim), lambda i, me_ref: (0, i, 0)),
                      pl.BlockSpec((1, rows, cdim), lambda i, me_ref: (me_ref[0], i, 0)), blk, blk, blk],
            out_specs=(blk,) * 4),
        compiler_params=_params("parallel"))(me, parts, own, w, m, v)


def _adamw_small(parts, own, me, w, m, v, mask, name):
    def body(me_ref, *refs):
        refs = list(refs)
        p_ref = refs.pop(0)
        own_ref = None if own is None else refs.pop(0)
        w_ref, m_ref, v_ref = refs[:3]
        k_ref = None if mask is None else refs[3]
        g_out, d_out, m_out, v_out = refs[-4:]
        g = _sum_parts(me_ref[0], p_ref, None if own is None else own_ref[me_ref[0]])
        if mask is not None:
            g = g * k_ref[...]
        d, mn, vn = _adamw_math(w_ref[...], g, m_ref[...], v_ref[...])
        g_out[...] = g
        d_out[...] = d
        m_out[...] = mn
        v_out[...] = vn

    def whole(shape):
        nd = len(shape)
        return pl.BlockSpec(shape, lambda i, me_ref: (0,) * nd)

    sds = jax.ShapeDtypeStruct(w.shape, F32)
    ins = [parts] + ([] if own is None else [own]) + [w, m, v] + ([] if mask is None else [mask])
    return pl.pallas_call(
        body, name=name, out_shape=(sds,) * 4,
        grid_spec=pltpu.PrefetchScalarGridSpec(
            num_scalar_prefetch=1, grid=(1,), in_specs=[whole(a.shape) for a in ins],
            out_specs=(whole(w.shape),) * 4),
        compiler_params=_params("arbitrary"))(me, *ins)


_IN_SPLITS = ((0, 512), (512, 1024), (1024, 1536), (1536, 2560), (2560, IN_PAD))


def _prenorm(x, g1, tm, dep=None):
    t_tok = x.shape[0]
    deps = [] if dep is None else [dep]

    def body(x_ref, g_ref, *rest):
        xv = x_ref[...]
        r = lax.rsqrt(jnp.mean(xv * xv, axis=-1, keepdims=True) + EPS)
        rest[-1][...] = (xv * r * g_ref[...]).astype(BF16)

    row = pl.BlockSpec((tm, D_MODEL), lambda i: (i, 0))
    return pl.pallas_call(
        body, name="prenorm", grid=(t_tok // tm,), out_shape=jax.ShapeDtypeStruct((t_tok, D_MODEL), BF16),
        in_specs=[row, _full((1, D_MODEL))] + [pl.BlockSpec(memory_space=pl.ANY)] * len(deps), out_specs=row,
        compiler_params=_params("parallel"))(x, g1, *deps)


def _in_proj(h1, w_in, tm):
    t_tok = h1.shape[0]

    def body(h_ref, w_ref, *outs):
        h = h_ref[...]
        for (a, b), o_ref in zip(_IN_SPLITS, outs):
            o_ref[...] = _dot(h, w_ref[a:b, :], _NT).astype(o_ref.dtype)

    row = lambda n: pl.BlockSpec((tm, n), lambda i: (i, 0))
    widths = [b - a for a, b in _IN_SPLITS]
    dtypes = (BF16, BF16, BF16, F32, F32)
    return pl.pallas_call(
        body, name="in_proj", grid=(t_tok // tm,),
        out_shape=tuple(jax.ShapeDtypeStruct((t_tok, n), dt) for n, dt in zip(widths, dtypes)),
        in_specs=[row(D_MODEL), _full((IN_PAD, D_MODEL))], out_specs=tuple(row(n) for n in widths),
        compiler_params=_params("parallel"))(h1, w_in)


def _lane_masks():
    lane = lax.broadcasted_iota(jnp.int32, (1, 2 * HEAD_DIM), 1)
    left = (lane < HEAD_DIM).astype(F32)
    return left, 1.0 - left


def _stack_pair(v, m_l, m_r):
    return jnp.concatenate([v * m_l, v * m_r], axis=0).astype(BF16)


def _head_mean(x, avg):
    n = avg.shape[0]
    return jnp.concatenate([_split_dot(x[:, n * i:n * (i + 1)], avg, 2) for i in range(x.shape[1] // n)], axis=1)


def _gmlp_common(u, v, lnw, lnb, avg, wcat_ref, bias, m_l, m_r):
    ug, dug = _gelu_and_grad(u)
    vg, dvg = _gelu_and_grad(v)
    mu = _head_mean(vg, avg)
    vc = vg - mu
    var = _head_mean(vc * vc, avg)
    rstd = lax.rsqrt(var + EPS)
    vhat = vc * rstd
    vn = vhat * lnw + lnb
    rows = []
    for r in range(u.shape[0] // CHUNK):
        cols = []
        for j in range(N_HEADS // 2):
            pair = vn[CHUNK * r:CHUNK * (r + 1), 128 * j:128 * (j + 1)]
            cols.append(_dot(wcat_ref[j], _stack_pair(pair, m_l, m_r)))
        rows.append(jnp.concatenate(cols, axis=1) + bias)
    mixed = jnp.concatenate(rows, axis=0)
    return ug, dug, dvg, rstd, vhat, vn, mixed


_GMLP_ROWS = 4 * CHUNK


def _gmlp_fwd(u, v, lnw, lnb, wcat, bias, avg):
    t_tok = u.shape[0]
    tm = min(_GMLP_ROWS, t_tok)

    def body(u_ref, v_ref, lnw_ref, lnb_ref, wcat_ref, bias_ref, avg_ref, o_ref):
        m_l, m_r = _lane_masks()
        ug, _, _, _, _, _, mixed = _gmlp_common(
            u_ref[...].astype(F32), v_ref[...].astype(F32), lnw_ref[...], lnb_ref[...], avg_ref[...], wcat_ref,
            bias_ref[...], m_l, m_r)
        o_ref[...] = (ug * mixed).astype(BF16)

    row = pl.BlockSpec((tm, GM_WIDTH), lambda i: (i, 0))
    return pl.pallas_call(
        body, name="gmlp_fwd", grid=(t_tok // tm,), out_shape=jax.ShapeDtypeStruct((t_tok, GM_WIDTH), BF16),
        in_specs=[row, row, _full((1, GM_WIDTH)), _full((1, GM_WIDTH)), _full(wcat.shape), _full(bias.shape),
                  _full(avg.shape)],
        out_specs=row, compiler_params=_params("parallel"))(u, v, lnw, lnb, wcat, bias, avg)


def _shift_rows(x, edge, j, down):
    groups, cols = x.shape[0] // 8, x.shape[1]
    amount = j if down else 8 - j
    rot = pltpu.roll(x.reshape(groups, 8, cols), amount, axis=1)
    edge_rot = pltpu.roll(edge, amount, axis=0)[None]
    sub = lax.broadcasted_iota(jnp.int32, (1, 8, 1), 1)
    if down:
        out = jnp.where(sub < j, jnp.concatenate([edge_rot, rot[:-1]], axis=0), rot)
    else:
        out = jnp.where(sub < 8 - j, rot, jnp.concatenate([rot[1:], edge_rot], axis=0))
    return out.reshape(x.shape)


def _conv_pre(xbc, tail, cw_ref, cb):
    taps = [_shift_rows(xbc, tail, 3 - k, True) for k in range(3)] + [xbc]
    return cb + cw_ref[0:1, :] * taps[0] + cw_ref[1:2, :] * taps[1] + cw_ref[2:3, :] * taps[2] + cw_ref[3:4, :] * taps[3]


def _ssd_common(pre, dtr, dtb, alog, expand, tril):
    q = CHUNK
    sg = jax.nn.sigmoid(pre)
    act = pre * sg
    lane = lax.broadcasted_iota(jnp.int32, (1, CHUNK), 1)
    a_row = jnp.where(lane < N_HEADS, -jnp.exp(alog), 0.0)
    dtp = dtr + dtb
    dt = _softplus(dtp)
    a_cs = _split_dot_left(tril, dt * a_row, 3)
    a_cs_t = a_cs.T
    dt_exp = _split_dot(dt, expand, 3)
    a_exp = _split_dot(a_cs, expand, 3)
    a_end = a_exp[q - 1:q, :]
    li = lax.broadcasted_iota(jnp.int32, (q, q), 0)
    si = lax.broadcasted_iota(jnp.int32, (q, q), 1)
    causal = si <= li
    decay = []
    for h in range(N_HEADS):
        seg = a_cs[:, h:h + 1] - a_cs_t[h:h + 1, :]
        decay.append(jnp.where(causal, jnp.exp(jnp.minimum(seg, 0.0)), 0.0))
    return dict(pre=pre, sg=sg, act=act, a_row=a_row, dtp=dtp, dt=dt, dt_exp=dt_exp, a_exp=a_exp,
                e=jnp.exp(a_exp), w_end=jnp.exp(a_end - a_exp), cd=jnp.exp(a_end), decay=decay)


def _ssd_specs(t_tok, seq, reverse):
    nb, nc = t_tok // seq, seq // CHUNK

    def chunk(c):
        return nc - 1 - c if reverse else c

    def row(n, col=0):
        return pl.BlockSpec((nb, CHUNK, n), lambda c: (0, chunk(c), col))

    tail = pl.BlockSpec((nb, 8, CONV_CH), lambda c: (0, jnp.maximum(chunk(c) * (CHUNK // 8) - 1, 0), 0))
    states = pl.BlockSpec((nb, 1, N_STATE, SSM_WIDTH), lambda c: (0, chunk(c), 0, 0))
    fold = lambda a: a.reshape(nb, seq, a.shape[-1])
    unfold = lambda a: a.reshape(t_tok, a.shape[-1])
    return nb, nc, row, tail, states, fold, unfold


def _ssd_fwd(z, xbc, dtr, cw, cb, dtb, alog, dskip_exp, nw, expand, tril, seq):
    t_tok = z.shape[0]
    nb, nc, row, tail, states_spec, fold, unfold = _ssd_specs(t_tok, seq, False)

    def body(z_ref, xbc_ref, tail_ref, dtr_ref, cw_ref, cb_ref, dtb_ref, alog_ref, dsk_ref, nw_ref, exp_ref,
             tril_ref, o_ref, y_ref, st_ref, pre_ref, state_ref):
        c = pl.program_id(0)

        @pl.when(c == 0)
        def _():
            state_ref[...] = jnp.zeros_like(state_ref)

        m_l, m_r = _lane_masks()
        for s in range(nb):
            pre = _conv_pre(xbc_ref[s], jnp.where(c == 0, 0.0, tail_ref[s]), cw_ref, cb_ref[...])
            pre_ref[s] = pre
            f = _ssd_common(pre, dtr_ref[s], dtb_ref[...], alog_ref[...], exp_ref[...], tril_ref[...])
            act = f["act"]
            xs = act[:, :SSM_WIDTH]
            xdt = xs * f["dt_exp"]
            xw = xdt * f["w_end"]
            state = state_ref[s]
            st_ref[s, 0] = state
            ydiag, yoff, snew = [], [], []
            for g in range(2):
                bg = act[:, 512 + 128 * g:640 + 128 * g].astype(BF16)
                cg = act[:, 768 + 128 * g:896 + 128 * g].astype(BF16)
                cb_mat = _dot(cg, bg, _NT)
                for pr in range(2):
                    h0 = 4 * g + 2 * pr
                    gcat = jnp.concatenate(
                        [(cb_mat * f["decay"][h0]).astype(BF16), (cb_mat * f["decay"][h0 + 1]).astype(BF16)], axis=1)
                    ydiag.append(_dot(gcat, _stack_pair(xdt[:, 64 * h0:64 * h0 + 128], m_l, m_r)))
                yoff.append(_dot(cg, state[:, 256 * g:256 * (g + 1)].astype(BF16)))
                snew.append(_dot(bg, xw[:, 256 * g:256 * (g + 1)].astype(BF16), _TN))
            y = jnp.concatenate(ydiag, axis=1) + f["e"] * jnp.concatenate(yoff, axis=1) + dsk_ref[...] * xs
            state_ref[s] = state * f["cd"] + jnp.concatenate(snew, axis=1)
            y_ref[s] = y
            zv = z_ref[s].astype(F32)
            yg = y * (zv * jax.nn.sigmoid(zv))
            outs = []
            for g in range(2):
                ygg = yg[:, 256 * g:256 * (g + 1)]
                outs.append(ygg * lax.rsqrt(jnp.mean(ygg * ygg, axis=-1, keepdims=True) + EPS))
            o_ref[s] = (jnp.concatenate(outs, axis=1) * nw_ref[...]).astype(BF16)

    consts = [cw, cb, dtb, alog, dskip_exp, nw, expand, tril]
    sd = lambda n, dt: jax.ShapeDtypeStruct((nb, seq, n), dt)
    o, y, states, pre = pl.pallas_call(
        body, name="ssd_fwd", grid=(nc,),
        out_shape=(sd(SSM_WIDTH, BF16), sd(SSM_WIDTH, F32), jax.ShapeDtypeStruct((nb, nc, N_STATE, SSM_WIDTH), F32),
                   sd(CONV_CH, F32)),
        in_specs=[row(SSM_WIDTH), row(CONV_CH), tail, row(CHUNK)] + [_full(a.shape) for a in consts],
        out_specs=(row(SSM_WIDTH), row(SSM_WIDTH), states_spec, row(CONV_CH)),
        scratch_shapes=[pltpu.VMEM((nb, N_STATE, SSM_WIDTH), F32)],
        compiler_params=_params("arbitrary"))(fold(z), fold(xbc), fold(xbc), fold(dtr), *consts)
    return unfold(o), unfold(y), states, unfold(pre)


def _out_proj(mix_a, mix_b, w_out, x, g2, g3, tm, dep=None):
    t_tok = x.shape[0]
    deps = [] if dep is None else [dep]

    def body(a_ref, b_ref, w_ref, x_ref, g2_ref, g3_ref, *rest):
        o_ref, x2_ref, h3_ref = rest[-3:]
        o = _dot(a_ref[...], w_ref[0:GM_WIDTH, :]) + _dot(b_ref[...], w_ref[GM_WIDTH:, :])
        o_ref[...] = o
        r2 = lax.rsqrt(jnp.mean(o * o, axis=-1, keepdims=True) + EPS)
        x2 = x_ref[...] + o * r2 * g2_ref[...]
        x2_ref[...] = x2
        r3 = lax.rsqrt(jnp.mean(x2 * x2, axis=-1, keepdims=True) + EPS)
        h3_ref[...] = (x2 * r3 * g3_ref[...]).astype(BF16)

    row = lambda n: pl.BlockSpec((tm, n), lambda i: (i, 0))
    sd = lambda dt: jax.ShapeDtypeStruct((t_tok, D_MODEL), dt)
    return pl.pallas_call(
        body, name="out_proj", grid=(t_tok // tm,), out_shape=(sd(F32), sd(F32), sd(BF16)),
        in_specs=[row(GM_WIDTH), row(SSM_WIDTH), _full((D_MODEL, D_MODEL)), row(D_MODEL), _full((1, D_MODEL)),
                  _full((1, D_MODEL))] + [pl.BlockSpec(memory_space=pl.ANY)] * len(deps),
        out_specs=(row(D_MODEL),) * 3, compiler_params=_params("parallel"))(mix_a, mix_b, w_out, x, g2, g3, *deps)


def _mlp_fwd(h3, w_up, w_down, x2, target, g4, tm, tf):
    t_tok = x2.shape[0]

    def up_body(h_ref, wu_ref, ra_ref):
        ra_ref[...] = jnp.maximum(_dot(h_ref[...], wu_ref[...]), 0.0).astype(BF16)

    tu = min(2 * tm, t_tok)
    ra = pl.pallas_call(
        up_body, name="mlp_up", grid=(D_FF // tf, t_tok // tu), out_shape=jax.ShapeDtypeStruct((t_tok, D_FF), BF16),
        in_specs=[pl.BlockSpec((tu, D_MODEL), lambda j, i: (i, 0)), pl.BlockSpec((D_MODEL, tf), lambda j, i: (0, j))],
        out_specs=pl.BlockSpec((tu, tf), lambda j, i: (i, j)), compiler_params=_params("parallel", "parallel"))(h3, w_up)

    def down_body(ra_ref, wd_ref, x2_ref, t_ref, g4_ref, dd_ref, dy_ref, dg4_ref, loss_ref):
        i = pl.program_id(0)
        rav = ra_ref[...]
        dvec = _dot(rav * rav, wd_ref[...])
        r4 = lax.rsqrt(jnp.mean(dvec * dvec, axis=-1, keepdims=True) + EPS)
        dn = dvec * r4
        g4 = g4_ref[...]
        err = x2_ref[...] + dn * g4 - t_ref[...]
        dy = err * (1.0 / D_MODEL)
        dy_ref[...] = dy
        dg = dy * g4
        dd_ref[...] = (r4 * (dg - dn * jnp.mean(dg * dn, axis=-1, keepdims=True))).astype(BF16)
        _acc_rows(dg4_ref, _rsum(dy * dn), i == 0)
        tile_loss = 0.5 * jnp.sum(jnp.sum(err * err, axis=-1, keepdims=True), axis=0, keepdims=True) / D_MODEL
        _acc_rows(loss_ref, jnp.broadcast_to(tile_loss, (1, 128)), i == 0)

    row = pl.BlockSpec((tm, D_MODEL), lambda i: (i, 0))
    dd, dy, dg4, loss = pl.pallas_call(
        down_body, name="mlp_down", grid=(t_tok // tm,),
        out_shape=(jax.ShapeDtypeStruct((t_tok, D_MODEL), BF16), jax.ShapeDtypeStruct((t_tok, D_MODEL), F32),
                   jax.ShapeDtypeStruct((1, D_MODEL), F32), jax.ShapeDtypeStruct((1, 128), F32)),
        in_specs=[pl.BlockSpec((tm, D_FF), lambda i: (i, 0)), _full((D_FF, D_MODEL)), row, row, _full((1, D_MODEL))],
        out_specs=(row, row, _full((1, D_MODEL)), _full((1, 128))),
        compiler_params=_params("arbitrary"))(ra, w_down, x2, target, g4)
    return ra, dd, dy, dg4, loss


def _mlp_bwd(dd, w_down, ra, w_up, x2, dy, o, g3, g2, tm, tf):
    t_tok = x2.shape[0]

    def hidden_body(dd_ref, wd_ref, ra_ref, da_ref):
        df = _dot(dd_ref[...], wd_ref[...], _NT)
        da_ref[...] = (df * (2.0 * ra_ref[...].astype(F32))).astype(BF16)

    tu = min(2 * tm, t_tok)
    da = pl.pallas_call(
        hidden_body, name="mlp_bwd_hidden", grid=(D_FF // tf, t_tok // tu),
        out_shape=jax.ShapeDtypeStruct((t_tok, D_FF), BF16),
        in_specs=[pl.BlockSpec((tu, D_MODEL), lambda j, i: (i, 0)), pl.BlockSpec((tf, D_MODEL), lambda j, i: (j, 0)),
                  pl.BlockSpec((tu, tf), lambda j, i: (i, j))],
        out_specs=pl.BlockSpec((tu, tf), lambda j, i: (i, j)),
        compiler_params=_params("parallel", "parallel"))(dd, w_down, ra)

    def in_body(da_ref, wu_ref, x2_ref, dy_ref, o_ref, g3_ref, g2_ref, dx2_ref, do_ref, dg3_ref, dg2_ref):
        i = pl.program_id(0)
        dh3 = _dot(da_ref[...], wu_ref[...], _NT)
        dn3, dg3 = _rms_bwd(x2_ref[...], g3_ref[...], dh3)
        dx2 = dy_ref[...] + dn3
        dx2_ref[...] = dx2
        do, dg2 = _rms_bwd(o_ref[...], g2_ref[...], dx2)
        do_ref[...] = do.astype(BF16)
        _acc_rows(dg3_ref, dg3, i == 0)
        _acc_rows(dg2_ref, dg2, i == 0)

    row = pl.BlockSpec((tm, D_MODEL), lambda i: (i, 0))
    vec = _full((1, D_MODEL))
    sd = lambda dt: jax.ShapeDtypeStruct((t_tok, D_MODEL), dt)
    dx2, do, dg3, dg2 = pl.pallas_call(
        in_body, name="mlp_bwd_in", grid=(t_tok // tm,),
        out_shape=(sd(F32), sd(BF16), jax.ShapeDtypeStruct((1, D_MODEL), F32), jax.ShapeDtypeStruct((1, D_MODEL), F32)),
        in_specs=[pl.BlockSpec((tm, D_FF), lambda i: (i, 0)), _full((D_MODEL, D_FF)), row, row, row, vec, vec],
        out_specs=(row, row, vec, vec), compiler_params=_params("arbitrary"))(da, w_up, x2, dy, o, g3, g2)
    return da, dx2, do, dg3, dg2


def _wgrad(a, b, out_blocks, bm, bn, bk, square_a, name, dep=None):
    t_tok, m = a.shape
    n = b.shape[1]
    nk = t_tok // bk

    def body(a_ref, b_ref, *rest):
        o_ref, acc_ref = rest[-2:]
        k = pl.program_id(2)
        av = a_ref[...]
        if square_a:
            av = av * av
        part = _dot(av, b_ref[...], _TN)

        def emit(res):
            if out_blocks is None:
                o_ref[...] = res.astype(BF16)
            else:
                o_ref[0] = res.astype(BF16)

        if nk == 1:
            emit(part)
            return

        @pl.when(k == 0)
        def _():
            acc_ref[...] = part

        @pl.when(k > 0)
        def _():
            acc_ref[...] += part

        @pl.when(k == nk - 1)
        def _():
            emit(acc_ref[...])

    if out_blocks is None:
        out_shape = jax.ShapeDtypeStruct((m, n), BF16)
        out_spec = pl.BlockSpec((bm, bn), lambda i, j, k: (i, j))
    else:
        assert n // out_blocks == bn
        out_shape = jax.ShapeDtypeStruct((out_blocks, m, bn), BF16)
        out_spec = pl.BlockSpec((1, bm, bn), lambda i, j, k: (j, i, 0))
    deps = [] if dep is None else [dep]
    return pl.pallas_call(
        body, name=name, grid=(m // bm, n // bn, nk), out_shape=out_shape,
        in_specs=[pl.BlockSpec((bk, bm), lambda i, j, k: (k, i)), pl.BlockSpec((bk, bn), lambda i, j, k: (k, j))]
        + [pl.BlockSpec(memory_space=pl.ANY)] * len(deps),
        out_specs=out_spec, scratch_shapes=[pltpu.VMEM((bm, bn) if nk > 1 else (8, 128), F32)],
        compiler_params=_params("parallel", "parallel", "arbitrary"))(a, b, *deps)


def _wgrad_in(h1, pieces, bn, name, dep=None):
    t_tok = h1.shape[0]
    widths = [p.shape[1] for p in pieces]
    starts = [sum(widths[:i]) for i in range(len(widths))]

    def body(h_ref, *rest):
        piece_refs = rest[:len(widths)]
        o_ref = rest[-1]
        hv = h_ref[...]
        for a, n, r in zip(starts, widths, piece_refs):
            o_ref[a:a + n, :] = _dot(r[...], hv, _TN).astype(BF16)

    deps = [] if dep is None else [dep]
    return pl.pallas_call(
        body, name=name, grid=(D_MODEL // bn,), out_shape=jax.ShapeDtypeStruct((sum(widths), D_MODEL), BF16),
        in_specs=[pl.BlockSpec((t_tok, bn), lambda j: (0, j))] + [pl.BlockSpec((t_tok, n), lambda j: (0, 0)) for n in widths]
        + [pl.BlockSpec(memory_space=pl.ANY)] * len(deps),
        out_specs=pl.BlockSpec((sum(widths), bn), lambda j: (0, j)),
        compiler_params=_params("parallel"))(h1, *pieces, *deps)


def _dmix(do, w_out, tm, dep=None):
    t_tok = do.shape[0]

    def body(d_ref, w_ref, *rest):
        rest[-1][...] = _dot(d_ref[...], w_ref[...], _NT).astype(BF16)

    row = pl.BlockSpec((tm, D_MODEL), lambda i: (i, 0))
    deps = [] if dep is None else [dep]
    return pl.pallas_call(
        body, name="dmix", grid=(t_tok // tm,), out_shape=jax.ShapeDtypeStruct((t_tok, D_MODEL), BF16),
        in_specs=[row, _full((D_MODEL, D_MODEL))] + [pl.BlockSpec(memory_space=pl.ANY)] * len(deps), out_specs=row,
        compiler_params=_params("parallel"))(do, w_out, *deps)


def _gmlp_bwd(dmix, u, v, lnw, lnb, wcat, wtcat, bias, avg, expand_t):
    t_tok = u.shape[0]
    tm = min(_GMLP_ROWS, t_tok)

    def body(dm_ref, u_ref, v_ref, lnw_ref, lnb_ref, wcat_ref, wtcat_ref, bias_ref, avg_ref, expt_ref, du_ref, dv_ref,
             dw_ref, db_ref, dlnw_ref, dlnb_ref):
        i = pl.program_id(0)
        m_l, m_r = _lane_masks()
        avg = avg_ref[...]
        lnw = lnw_ref[...]
        ug, dug, dvg, rstd, vhat, vn, mixed = _gmlp_common(
            u_ref[...].astype(F32), v_ref[...].astype(F32), lnw, lnb_ref[...], avg, wcat_ref, bias_ref[...], m_l, m_r)
        dya = dm_ref[...].astype(F32)
        du_ref[...] = (dya * mixed * dug).astype(BF16)
        dmixed = dya * ug
        dvn_rows, dws, dbt = [], [None] * N_HEADS, None
        for r in range(tm // CHUNK):
            dvn_cols = []
            for j in range(N_HEADS // 2):
                dmp = dmixed[CHUNK * r:CHUNK * (r + 1), 128 * j:128 * (j + 1)]
                dvn_cols.append(_dot(wtcat_ref[j], _stack_pair(dmp, m_l, m_r)))
                vnp = vn[CHUNK * r:CHUNK * (r + 1), 128 * j:128 * (j + 1)].astype(BF16)
                for i_h, mask in enumerate((m_l, m_r)):
                    part = _dot((dmp * mask).astype(BF16), vnp, _NT)
                    dws[2 * j + i_h] = part if r == 0 else dws[2 * j + i_h] + part
            dvn_rows.append(jnp.concatenate(dvn_cols, axis=1))
            part = _split_dot(dmixed[CHUNK * r:CHUNK * (r + 1), :], expt_ref[...], 2)
            dbt = part if r == 0 else dbt + part
        dvn = jnp.concatenate(dvn_rows, axis=0)
        dvh = dvn * lnw
        dvgel = rstd * (dvh - _head_mean(dvh, avg) - vhat * _head_mean(dvh * vhat, avg))
        dv_ref[...] = (dvgel * dvg).astype(BF16)
        first = i == 0

        @pl.when(first)
        def _():
            for h in range(N_HEADS):
                dw_ref[h] = dws[h]
            db_ref[...] = dbt

        @pl.when(jnp.logical_not(first))
        def _():
            for h in range(N_HEADS):
                dw_ref[h] += dws[h]
            db_ref[...] += dbt

        _acc_rows(dlnw_ref, _rsum(dvn * vhat), first)
        _acc_rows(dlnb_ref, _rsum(dvn), first)

    row = pl.BlockSpec((tm, GM_WIDTH), lambda i: (i, 0))
    consts = [lnw, lnb, wcat, wtcat, bias, avg, expand_t]
    return pl.pallas_call(
        body, name="gmlp_bwd", grid=(t_tok // tm,),
        out_shape=(jax.ShapeDtypeStruct((t_tok, GM_WIDTH), BF16), jax.ShapeDtypeStruct((t_tok, GM_WIDTH), BF16),
                   jax.ShapeDtypeStruct((N_HEADS, CHUNK, CHUNK), F32), jax.ShapeDtypeStruct((CHUNK, CHUNK), F32),
                   jax.ShapeDtypeStruct((1, GM_WIDTH), F32), jax.ShapeDtypeStruct((1, GM_WIDTH), F32)),
        in_specs=[row, row, row] + [_full(a.shape) for a in consts],
        out_specs=(row, row, _full((N_HEADS, CHUNK, CHUNK)), _full((CHUNK, CHUNK)), _full((1, GM_WIDTH)),
                   _full((1, GM_WIDTH))),
        compiler_params=_params("arbitrary"))(dmix, u, v, *consts)


def _ssd_bwd(dmix, z, xbc, pre, dtr, y, states, cw, cb, dtb, alog, dskip_exp, nw, expand, expand_t, tril, triu, seq,
             dep=None):
    t_tok = z.shape[0]
    nb, nc, row, _, states_spec, fold, unfold = _ssd_specs(t_tok, seq, True)
    q = CHUNK

    def one_sequence(s, dm_ref, z_ref, xbc_ref, pre_ref, dtr_ref, y_ref, st_ref, cw_ref, dtb_ref, alog_ref, dsk_ref,
                     nw_ref, exp_ref, expt_ref, tril_ref, triu_ref, dz_ref, dxbc_ref, ddt_ref, dhead_ref, dstate_ref):
        m_l, m_r = _lane_masks()
        expt = expt_ref[...]
        f = _ssd_common(pre_ref[s], dtr_ref[s], dtb_ref[...], alog_ref[...], exp_ref[...], tril_ref[...])
        act, pre, sg = f["act"], f["pre"], f["sg"]
        xs = act[:, :SSM_WIDTH]
        xdt = xs * f["dt_exp"]
        xw = xdt * f["w_end"]
        state = st_ref[s, 0]
        dstate = dstate_ref[s]
        zv, yv, dout, nw = z_ref[s].astype(F32), y_ref[s], dm_ref[s].astype(F32), nw_ref[...]
        sz = jax.nn.sigmoid(zv)
        sl = zv * sz
        yg = yv * sl
        tv = dout * nw
        dyg_parts, ygh_parts = [], []
        for g in range(2):
            ygg = yg[:, 256 * g:256 * (g + 1)]
            rr = lax.rsqrt(jnp.mean(ygg * ygg, axis=-1, keepdims=True) + EPS)
            ygh = ygg * rr
            tg = tv[:, 256 * g:256 * (g + 1)]
            dyg_parts.append(rr * (tg - ygh * jnp.mean(tg * ygh, axis=-1, keepdims=True)))
            ygh_parts.append(ygh)
        dyg = jnp.concatenate(dyg_parts, axis=1)
        dnw = _rsum(dout * jnp.concatenate(ygh_parts, axis=1))
        dy = dyg * sl
        dz_ref[s] = (dyg * yv * (sz * (1.0 + zv * (1.0 - sz)))).astype(BF16)
        ddsk = _rsum(dy * xs)
        dye = dy * f["e"]
        lane = lax.broadcasted_iota(jnp.int32, (q, q), 1)
        sub = lax.broadcasted_iota(jnp.int32, (q, q), 0)
        rs_mat = jnp.zeros((q, q), F32)
        cs_mat = jnp.zeros((q, q), F32)
        dxdt_cols, yoff, dst_in, dxw, d_b, d_c = [], [], [], [], [], []
        for g in range(2):
            bg = act[:, 512 + 128 * g:640 + 128 * g].astype(BF16)
            cg = act[:, 768 + 128 * g:896 + 128 * g].astype(BF16)
            cb_mat = _dot(cg, bg, _NT)
            stg = state[:, 256 * g:256 * (g + 1)].astype(BF16)
            dyeg = dye[:, 256 * g:256 * (g + 1)].astype(BF16)
            yoff.append(_dot(cg, stg))
            dcg = _dot(dyeg, stg, _NT)
            dst_in.append(_dot(cg, dyeg, _TN))
            dcb = jnp.zeros((q, q), F32)
            for pr in range(2):
                h0 = 4 * g + 2 * pr
                gf = [cb_mat * f["decay"][h0], cb_mat * f["decay"][h0 + 1]]
                gcat = jnp.concatenate([gf[0].astype(BF16), gf[1].astype(BF16)], axis=1)
                xst = _stack_pair(xdt[:, 64 * h0:64 * h0 + 128], m_l, m_r)
                dyp = dy[:, 64 * h0:64 * h0 + 128].astype(BF16)
                dgcat = _dot(dyp, xst, _NT)
                dxst = _dot(gcat, dyp, _TN)
                dxdt_cols.append(dxst[:q] * m_l + dxst[q:] * m_r)
                for i in range(2):
                    h = h0 + i
                    dg = dgcat[:, q * i:q * (i + 1)]
                    mm = dg * gf[i]
                    rs_mat = rs_mat + jnp.where(lane == h, jnp.sum(mm, axis=1, keepdims=True), 0.0)
                    cs_mat = cs_mat + jnp.where(sub == h, jnp.sum(mm, axis=0, keepdims=True), 0.0)
                    dcb = dcb + dg * f["decay"][h]
            dcb16 = dcb.astype(BF16)
            dstg = dstate[:, 256 * g:256 * (g + 1)].astype(BF16)
            d_c.append(dcg + _dot(dcb16, bg))
            dxw.append(_dot(bg, dstg))
            d_b.append(_dot(dcb16, cg, _TN) + _dot(xw[:, 256 * g:256 * (g + 1)].astype(BF16), dstg, _NT))
        dxw = jnp.concatenate(dxw, axis=1)
        dxdt = jnp.concatenate(dxdt_cols, axis=1) + dxw * f["w_end"]
        qv = dxw * xw
        end_row = _rsum(qv) + _rsum(dstate * state) * f["cd"]
        x2 = dye * jnp.concatenate(yoff, axis=1) - qv
        row_i = lax.broadcasted_iota(jnp.int32, (q, 1), 0)
        x2 = x2 + jnp.where(row_i == q - 1, end_row, 0.0)
        da_cs = _split_dot(x2, expt, 2) + rs_mat - cs_mat.T
        ddt = _split_dot(dxdt * xs, expt, 2)
        dxs = dsk_ref[...] * dy + dxdt * f["dt_exp"]
        dda = _split_dot_left(triu_ref[...], da_cs, 3)
        ddt = ddt + dda * f["a_row"]
        dalog = _rsum(dda * f["dt"]) * f["a_row"]
        draw = ddt * jax.nn.sigmoid(f["dtp"])
        ddt_ref[s] = draw.astype(BF16)
        dact = jnp.concatenate([dxs] + d_b + d_c, axis=1)
        dpre = dact * (sg * (1.0 + pre * (1.0 - sg)))
        dhead = dhead_ref[s]
        xv = xbc_ref[s]
        shifted = [_shift_rows(dpre, dhead, 3 - k, False) for k in range(3)] + [dpre]
        dxbc = cw_ref[3:4, :] * dpre
        for k in range(3):
            dxbc = dxbc + cw_ref[k:k + 1, :] * shifted[k]
        dxbc_ref[s] = dxbc.astype(BF16)
        dhead_ref[s] = dpre[0:8, :]
        dstate_ref[s] = dstate * f["cd"] + jnp.concatenate(dst_in, axis=1)
        row8 = lax.broadcasted_iota(jnp.int32, (8, 1), 0)
        dcw = jnp.zeros((8, CONV_CH), F32)
        for k in range(4):
            dcw = dcw + jnp.where(row8 == k, _rsum(shifted[k] * xv), 0.0)
        return dcw, _rsum(dpre), _rsum(draw), dalog, _split_dot(ddsk, expt, 3), dnw

    def body(dm_ref, z_ref, xbc_ref, pre_ref, dtr_ref, y_ref, st_ref, cw_ref, cb_ref, dtb_ref, alog_ref, dsk_ref,
             nw_ref, exp_ref, expt_ref, tril_ref, triu_ref, dz_ref, dxbc_ref, ddt_ref, dcw_ref, dcb_ref, ddtb_ref,
             dalog_ref, dd_ref, dnw_ref, dhead_ref, dstate_ref):
        c = pl.program_id(0)
        first = c == 0

        @pl.when(first)
        def _():
            dstate_ref[...] = jnp.zeros_like(dstate_ref)
            dhead_ref[...] = jnp.zeros_like(dhead_ref)

        total = None
        for s in range(nb):
            parts = one_sequence(s, dm_ref, z_ref, xbc_ref, pre_ref, dtr_ref, y_ref, st_ref, cw_ref, dtb_ref, alog_ref,
                                 dsk_ref, nw_ref, exp_ref, expt_ref, tril_ref, triu_ref, dz_ref, dxbc_ref, ddt_ref,
                                 dhead_ref, dstate_ref)
            total = parts if total is None else tuple(a + b for a, b in zip(total, parts))
        dcw = total[0]

        @pl.when(first)
        def _():
            dcw_ref[...] = dcw

        @pl.when(jnp.logical_not(first))
        def _():
            dcw_ref[...] += dcw

        for ref, part in zip((dcb_ref, ddtb_ref, dalog_ref, dd_ref, dnw_ref), total[1:]):
            _acc_rows(ref, part, first)

    consts = [cw, cb, dtb, alog, dskip_exp, nw, expand, expand_t, tril, triu]
    deps = [] if dep is None else [dep]
    n_in = 7 + len(consts)

    def body_skipping_dep(*refs):
        body(*refs[:n_in], *refs[n_in + len(deps):])

    acc = lambda n: jax.ShapeDtypeStruct((1, n), F32)
    sd = lambda n: jax.ShapeDtypeStruct((nb, seq, n), BF16)
    dz, dxbc, ddt, *small_grads = pl.pallas_call(
        body_skipping_dep, name="ssd_bwd", grid=(nc,),
        out_shape=(sd(SSM_WIDTH), sd(CONV_CH), sd(CHUNK), jax.ShapeDtypeStruct((8, CONV_CH), F32), acc(CONV_CH),
                   acc(CHUNK), acc(CHUNK), acc(CHUNK), acc(SSM_WIDTH)),
        in_specs=[row(SSM_WIDTH, col=1), row(SSM_WIDTH), row(CONV_CH), row(CONV_CH), row(CHUNK), row(SSM_WIDTH),
                  states_spec]
        + [_full(a.shape) for a in consts] + [pl.BlockSpec(memory_space=pl.ANY)] * len(deps),
        out_specs=(row(SSM_WIDTH), row(CONV_CH), row(CHUNK), _full((8, CONV_CH)), _full((1, CONV_CH)),
                   _full((1, CHUNK)), _full((1, CHUNK)), _full((1, CHUNK)), _full((1, SSM_WIDTH))),
        scratch_shapes=[pltpu.VMEM((nb, 8, CONV_CH), F32), pltpu.VMEM((nb, N_STATE, SSM_WIDTH), F32)],
        compiler_params=_params("arbitrary"))(
            fold(dmix), fold(z), fold(xbc), fold(pre), fold(dtr), fold(y), states, *consts, *deps)
    return (unfold(dz), unfold(dxbc), unfold(ddt), *small_grads)


def _in_bwd(du, dv, dz, dxbc, ddt, w_in, x, dx2, g1, tm, dep=None):
    t_tok = x.shape[0]

    def body(du_ref, dv_ref, dz_ref, dxbc_ref, ddt_ref, w_ref, x_ref, dx2_ref, g_ref, *rest):
        gx_ref, dg_ref = rest[-2:]
        i = pl.program_id(0)
        dh = None
        for (a, b), ref in zip(_IN_SPLITS, (du_ref, dv_ref, dz_ref, dxbc_ref, ddt_ref)):
            part = _dot(ref[...], w_ref[a:b, :])
            dh = part if dh is None else dh + part
        dn, dg = _rms_bwd(x_ref[...], g_ref[...], dh)
        gx_ref[...] = dx2_ref[...] + dn
        _acc_rows(dg_ref, dg, i == 0)

    row = lambda n: pl.BlockSpec((tm, n), lambda i: (i, 0))
    widths = [b - a for a, b in _IN_SPLITS]
    deps = [] if dep is None else [dep]
    return pl.pallas_call(
        body, name="in_bwd", grid=(t_tok // tm,),
        out_shape=(jax.ShapeDtypeStruct((t_tok, D_MODEL), F32), jax.ShapeDtypeStruct((1, D_MODEL), F32)),
        in_specs=[row(n) for n in widths] + [_full((IN_PAD, D_MODEL)), row(D_MODEL), row(D_MODEL), _full((1, D_MODEL))]
        + [pl.BlockSpec(memory_space=pl.ANY)] * len(deps),
        out_specs=(row(D_MODEL), _full((1, D_MODEL))),
        compiler_params=_params("arbitrary"))(du, dv, dz, dxbc, ddt, w_in, x, dx2, g1, *deps)


def _pad_lanes(a, n):
    return jnp.pad(a, ((0, 0), (0, n - a.shape[1])))


def _local_step(x, target, seq, small, hooks, first_dep=None):
    t_tok = x.shape[0]
    tm = min(512, t_tok)
    avg, expand, expand_t, tril, triu = _const_mats()
    g1, g2, g3, g4 = (small[k].reshape(1, D_MODEL) for k in
                      ("norm_mix_pre", "norm_mix_post", "norm_ffn_pre", "norm_ffn_post"))
    tie = (lambda a: a) if first_dep is None else (lambda a: a + first_dep[0, 0])
    lnw = tie(small["gm_ln_w"]).reshape(1, GM_WIDTH)
    lnb = tie(small["gm_ln_b"]).reshape(1, GM_WIDTH)
    causal = jnp.tril(jnp.ones((CHUNK, CHUNK), F32))
    wm = tie(small["gm_w_s"]) * causal
    pair = lambda w: w.reshape(4, 2, CHUNK, CHUNK).transpose(0, 2, 1, 3).reshape(4, CHUNK, 2 * CHUNK).astype(BF16)
    wcat = pair(wm)
    wtcat = pair(jnp.swapaxes(wm, 1, 2))
    bias = jnp.repeat(tie(small["gm_b_s"]).T, HEAD_DIM, axis=1)
    cb = small["conv_b"].reshape(1, CONV_CH)
    dtb = _pad_lanes(tie(small["dt_bias"]).reshape(1, N_HEADS), CHUNK)
    alog = _pad_lanes(tie(small["a_log"]).reshape(1, N_HEADS), CHUNK)
    dskip_exp = jnp.repeat(tie(small["d_skip"]).reshape(1, N_HEADS), HEAD_DIM, axis=1)
    nw = small["ssm_norm_w"].reshape(1, SSM_WIDTH)

    h1 = _prenorm(x, g1, tm, hooks.get("prenorm_after", first_dep))
    w_in_t, conv_w = hooks["mixer_weights"](h1)
    u, v, z, xbc, dtr = _in_proj(h1, w_in_t, tm)
    mix_a = _gmlp_fwd(u, v, lnw, lnb, wcat, bias, avg)
    mix_b, y_pre, states, pre = _ssd_fwd(z, xbc, dtr, conv_w, cb, dtb, alog, dskip_exp, nw, expand, tril, seq)
    w_out, dep = hooks["mixers_done"](mix_b)
    o, x2, h3 = _out_proj(mix_a, mix_b, w_out, x, g2, g3, tm, dep)
    w_up, w_down = hooks["mlp_weights"](h3)
    tf = 2048
    ra, dd, dy, dg4, loss = _mlp_fwd(h3, w_up, w_down, x2, target, g4, tm, tf)

    da, dx2, do, dg3, dg2 = _mlp_bwd(dd, w_down, ra, w_up, x2, dy, o, g3, g2, tm, tf)
    g_w_down = _wgrad(ra, dd, None, 512, D_MODEL, t_tok, True, "wgrad_down")
    g_w_up = _wgrad(h3, da, N_DEV, D_MODEL, D_FF // N_DEV, t_tok, False, "wgrad_up")
    dep = hooks["mlp_grads"](g_w_down, g_w_up)
    dmix = _dmix(do, w_out, tm, dep)
    g_w_out = _wgrad_in(do, (mix_a, mix_b), 512, "wgrad_out", dep)
    du, dv, dws, dbt, dlnw, dlnb = _gmlp_bwd(dmix, u, v, lnw, lnb, wcat, wtcat, bias, avg, expand_t)
    dep = hooks["gmlp_grads"](g_w_out, dws)
    dz, dxbc, ddt, dcw, dcb, ddtb, dalog, ddsk, dnw = _ssd_bwd(
        dmix, z, xbc, pre, dtr, y_pre, states, conv_w, cb, dtb, alog, dskip_exp, nw, expand, expand_t, tril, triu, seq,
        dep)
    g_w_in = jnp.concatenate([_wgrad_in(h1, (du, dv, dz), 512, "wgrad_in_a", dep),
                              _wgrad_in(h1, (dxbc, ddt), 512, "wgrad_in_b", dep)], axis=0)
    dep = hooks["in_grads"](g_w_in, dcw[0:4])
    grad_x, dg1 = _in_bwd(du, dv, dz, dxbc, ddt, w_in_t, x, dx2, g1, tm, dep)

    grads = dict(
        w_in=g_w_in, w_out=g_w_out, w_up=g_w_up, w_down=g_w_down, conv_w=dcw[0:4],
        norm_mix_pre=dg1, norm_mix_post=dg2, norm_ffn_pre=dg3, norm_ffn_post=dg4, gm_ln_w=dlnw, gm_ln_b=dlnb,
        gm_w_s=dws, gm_b_s=dbt, conv_b=dcb, dt_bias=ddtb, a_log=dalog, d_skip=ddsk, ssm_norm_w=dnw)
    return loss[0, 0], grad_x, grads


_WEIGHTS = ("norm_mix_pre", "w_in", "gm_ln_w", "gm_ln_b", "gm_w_s", "gm_b_s", "conv_w", "conv_b", "dt_bias", "a_log",
            "d_skip", "ssm_norm_w", "w_out", "norm_mix_post", "norm_ffn_pre", "w_up", "w_down", "norm_ffn_post")
_SLAB_ROWS = (("norm_mix_pre", 1024), ("norm_mix_post", 1024), ("norm_ffn_pre", 1024), ("norm_ffn_post", 1024),
              ("conv_b", 1024), ("ssm_norm_w", 512), ("gm_ln_w", 512), ("gm_ln_b", 512), ("dt_bias", 8), ("a_log", 8),
              ("d_skip", 8))
_SLAB_LOSS_ROW = len(_SLAB_ROWS)
_SLAB_BS_ROW = 16
_SLAB_HEIGHT = 24
_SMALL_PARAMS = tuple(name for name, _ in _SLAB_ROWS) + ("gm_b_s",)
_LN_PARAMS = ("gm_ln_w", "gm_ln_b")


def _pack_slab(g, loss_part):
    rows = [_pad_lanes(g[name], D_MODEL) for name, _ in _SLAB_ROWS]
    rows.append(jnp.broadcast_to(loss_part, (1, D_MODEL)))
    rows.append(jnp.zeros((_SLAB_BS_ROW - len(rows), D_MODEL), F32))
    rows.append(_pad_lanes(g["gm_b_s"].T[0:N_HEADS], D_MODEL))
    return jnp.concatenate(rows, axis=0)


def _adamw_slab(parts, w, m, v):
    names = _SMALL_PARAMS
    shapes = [w[k].shape for k in names]
    unfold = np.zeros((GM_WIDTH, HEAD_DIM), np.float32)
    for h in range(N_HEADS):
        unfold[h * HEAD_DIM:(h + 1) * HEAD_DIM, :] = np.eye(HEAD_DIM)
    unfold = jnp.asarray(unfold, dtype=BF16)
    n = len(names)

    def body(p_ref, unfold_ref, *refs):
        w_refs, m_refs, v_refs = refs[:n], refs[n:2 * n], refs[2 * n:3 * n]
        outs = refs[3 * n:]
        g_all = p_ref[0]
        for j in range(1, N_DEV):
            g_all = g_all + p_ref[j]
        lane = lax.broadcasted_iota(jnp.int32, (N_HEADS, GM_WIDTH), 1)
        head = lax.broadcasted_iota(jnp.int32, (N_HEADS, GM_WIDTH), 0)
        own_lanes = jnp.logical_and(lane >= head * HEAD_DIM, lane < (head + 1) * HEAD_DIM)
        for i, name in enumerate(names):
            if name == "gm_b_s":
                g = g_all[_SLAB_BS_ROW:_SLAB_BS_ROW + N_HEADS, 0:CHUNK]
            else:
                row = [r for r, (k, _) in enumerate(_SLAB_ROWS) if k == name][0]
                g = g_all[row:row + 1, 0:dict(_SLAB_ROWS)[name]]
                if name in _LN_PARAMS:
                    g = _split_dot(jnp.where(own_lanes, g, 0.0), unfold_ref[...], 3)
            d, mn, vn = _adamw_math(w_refs[i][...], g, m_refs[i][...], v_refs[i][...])
            for o_ref, val in zip(outs[4 * i:4 * i + 4], (g, d, mn, vn)):
                o_ref[...] = val
        outs[-1][...] = g_all[_SLAB_LOSS_ROW:_SLAB_LOSS_ROW + 1, 0:128]

    ins = [parts, unfold] + [d[k] for d in (w, m, v) for k in names]
    out_shape = tuple(jax.ShapeDtypeStruct(s, F32) for s in shapes for _ in range(4)) + (
        jax.ShapeDtypeStruct((1, 128), F32),)
    outs = pl.pallas_call(
        body, name="adamw_small", out_shape=out_shape, grid=(1,), in_specs=[_full(a.shape) for a in ins],
        out_specs=tuple(_full(s.shape) for s in out_shape), compiler_params=_params("arbitrary"))(*ins)
    return {k: tuple(outs[4 * i:4 * i + 4]) for i, k in enumerate(names)}, outs[-1][0, 0]


def kernel(x, norm_mix_pre, w_in, gm_ln_w, gm_ln_b, gm_w_s, gm_b_s, conv_w, conv_b, dt_bias, a_log, d_skip, ssm_norm_w, w_out, norm_mix_post, norm_ffn_pre, w_up, w_down, norm_ffn_post, loss_target, m_norm_mix_pre, m_w_in, m_gm_ln_w, m_gm_ln_b, m_gm_w_s, m_gm_b_s, m_conv_w, m_conv_b, m_dt_bias, m_a_log, m_d_skip, m_ssm_norm_w, m_w_out, m_norm_mix_post, m_norm_ffn_pre, m_w_up, m_w_down, m_norm_ffn_post, v_norm_mix_pre, v_w_in, v_gm_ln_w, v_gm_ln_b, v_gm_w_s, v_gm_b_s, v_conv_w, v_conv_b, v_dt_bias, v_a_log, v_d_skip, v_ssm_norm_w, v_w_out, v_norm_mix_post, v_norm_ffn_pre, v_w_up, v_w_down, v_norm_ffn_post):
    w = dict(norm_mix_pre=norm_mix_pre, w_in=w_in, gm_ln_w=gm_ln_w, gm_ln_b=gm_ln_b, gm_w_s=gm_w_s, gm_b_s=gm_b_s, conv_w=conv_w, conv_b=conv_b, dt_bias=dt_bias, a_log=a_log, d_skip=d_skip, ssm_norm_w=ssm_norm_w, w_out=w_out, norm_mix_post=norm_mix_post, norm_ffn_pre=norm_ffn_pre, w_up=w_up, w_down=w_down, norm_ffn_post=norm_ffn_post)
    m = dict(norm_mix_pre=m_norm_mix_pre, w_in=m_w_in, gm_ln_w=m_gm_ln_w, gm_ln_b=m_gm_ln_b, gm_w_s=m_gm_w_s, gm_b_s=m_gm_b_s, conv_w=m_conv_w, conv_b=m_conv_b, dt_bias=m_dt_bias, a_log=m_a_log, d_skip=m_d_skip, ssm_norm_w=m_ssm_norm_w, w_out=m_w_out, norm_mix_post=m_norm_mix_post, norm_ffn_pre=m_norm_ffn_pre, w_up=m_w_up, w_down=m_w_down, norm_ffn_post=m_norm_ffn_post)
    v = dict(norm_mix_pre=v_norm_mix_pre, w_in=v_w_in, gm_ln_w=v_gm_ln_w, gm_ln_b=v_gm_ln_b, gm_w_s=v_gm_w_s, gm_b_s=v_gm_b_s, conv_w=v_conv_w, conv_b=v_conv_b, dt_bias=v_dt_bias, a_log=v_a_log, d_skip=v_d_skip, ssm_norm_w=v_ssm_norm_w, w_out=v_w_out, norm_mix_post=v_norm_mix_post, norm_ffn_pre=v_norm_ffn_pre, w_up=v_w_up, w_down=v_w_down, norm_ffn_post=v_norm_ffn_post)
    n_batch, seq, _ = x.shape
    shard_in = IN_COLS // N_DEV

    me = (4 * lax.axis_index("x") + 2 * lax.axis_index("y") + lax.axis_index("c")).astype(jnp.int32).reshape(1)

    def in_slot(own):
        return lax.dynamic_update_slice(lax.empty((N_DEV,) + own.shape, own.dtype), own[None],
                                        (me[0],) + (0,) * own.ndim)

    w_in_sh, m_in_sh, v_in_sh = w_in[0].T, m_w_in[0].T, v_w_in[0].T
    first = [_cast_to_slot(w_in_sh, me, shard_in, "cast_w_in"), in_slot(conv_w[0]),
             _cast_to_slot(w_out[0], me, 128, "cast_w_out")]
    ici_1, tok_ici_1 = _exchange_start(first, [True] * 3, _SAME_CORE_PEERS, "gather_mix_ici_start")
    cast_up = _cast_to_slot(w_up[0], me, 1024, "cast_w_up", cols=True, dep=tok_ici_1)
    second = [cast_up, _cast_to_slot(w_down[0], me, 512, "cast_w_down", dep=cast_up)]
    gathering = {}

    def mixer_weights(after):
        bufs = [buf for buf, _ in _exchange_wait(ici_1, after, "gather_mix_ici_wait")]
        d2d_1, tok_d2d_1 = _exchange_start(bufs, [True] * 3, _SIBLING_FORWARD, "gather_mix_d2d_start")
        gathering["mlp_ici"], tok_ici_2 = _exchange_start(
            second, [True] * 2, _SAME_CORE_PEERS, "gather_mlp_ici_start", dep=tok_d2d_1)
        (_, ag_in), (_, ag_conv), (_, ag_out) = _exchange_wait(d2d_1, tok_ici_2, "gather_mix_d2d_wait")
        gathering["w_out"] = ag_out.reshape(D_MODEL, D_MODEL)
        w_in_t = jnp.pad(ag_in.reshape(IN_COLS, D_MODEL), ((0, IN_PAD - IN_COLS), (0, 0)))
        return w_in_t, ag_conv.transpose(1, 0, 2).reshape(4, CONV_CH)

    def mixers_done(after):
        bufs = [buf for buf, _ in _exchange_wait(gathering["mlp_ici"], after, "gather_mlp_ici_wait")]
        gathering["mlp"], tok = _exchange_start(bufs, [True] * 2, _SIBLING_FORWARD, "gather_mlp_d2d_start")
        return gathering["w_out"], tok

    def mlp_weights(after):
        (_, ag_up), (_, ag_down) = _exchange_wait(gathering["mlp"], after, "gather_mlp_d2d_wait")
        return ag_up, ag_down.reshape(D_FF, D_MODEL)

    sent = {}

    def mlp_grads(g_w_down, g_w_up):
        sent["mlp"], tok = _exchange_start(
            [g_w_down.reshape(N_DEV, D_FF // N_DEV, D_MODEL), g_w_up], [False, False], _ALL_PEERS, "grads_mlp_start")
        return tok

    def gmlp_grads(g_w_out, g_w_s):
        sent["gmlp"], tok = _exchange_start(
            [g_w_out.reshape(N_DEV, D_MODEL // N_DEV, D_MODEL), in_slot(g_w_s.astype(BF16))], [False, True], _ALL_PEERS,
            "grads_gmlp_start")
        return tok

    def in_grads(g_w_in_t, g_conv_w):
        g_in_blk = g_w_in_t[:IN_COLS].reshape(N_DEV, shard_in, D_MODEL)
        g_conv_blk = g_conv_w.reshape(4, N_DEV, CONV_CH // N_DEV).transpose(1, 0, 2)
        sent["in"], tok = _exchange_start([g_in_blk, g_conv_blk], [False, False], _ALL_PEERS, "grads_in_start")
        return tok

    small = {k: w[k][0] for k in _SMALL_PARAMS + ("gm_w_s",)}
    loss_part, grad_x, g = _local_step(
        x.reshape(n_batch * seq, D_MODEL), loss_target.reshape(n_batch * seq, D_MODEL), seq, small,
        dict(mixer_weights=mixer_weights, mixers_done=mixers_done, mlp_weights=mlp_weights, mlp_grads=mlp_grads,
             gmlp_grads=gmlp_grads, in_grads=in_grads, prenorm_after=second[1]), first_dep=tok_ici_1)

    sent_rows, tok_rows = _exchange_start([in_slot(_pack_slab(g, loss_part))], [True], _ALL_PEERS, "grads_rows_start")
    (own_down, p_down), (own_up, p_up) = _exchange_wait(sent["mlp"], tok_rows, "grads_mlp_wait")
    res = {}
    res["w_up"] = _adamw_reduce(p_up, own_up, me, w_up[0], m_w_up[0], v_w_up[0], 256, "adamw_w_up")
    res["w_down"] = _adamw_reduce(p_down, own_down, me, w_down[0], m_w_down[0], v_w_down[0], 128, "adamw_w_down")
    (own_out, p_out), (_, p_ws) = _exchange_wait(sent["gmlp"], res["w_down"][1], "grads_gmlp_wait")
    res["w_out"] = _adamw_reduce(p_out, own_out, me, w_out[0], m_w_out[0], v_w_out[0], 128, "adamw_w_out")
    causal = jnp.tril(jnp.ones((1, CHUNK, CHUNK), F32))
    res["gm_w_s"] = _adamw_small(p_ws, None, me, gm_w_s[0], m_gm_w_s[0], v_gm_w_s[0], causal, "adamw_gm_w_s")
    (own_in, p_in), (own_conv, p_conv) = _exchange_wait(sent["in"], res["gm_w_s"][1], "grads_in_wait")
    res["w_in"] = tuple(r.T for r in _adamw_reduce(p_in, own_in, me, w_in_sh, m_in_sh, v_in_sh, shard_in, "adamw_w_in"))
    res["conv_w"] = _adamw_small(p_conv, own_conv, me, conv_w[0], m_conv_w[0], v_conv_w[0], None, "adamw_conv_w")
    ((_, p_rows),) = _exchange_wait(sent_rows, res["w_in"][1], "grads_rows_wait")
    flat = lambda t: t[0] if t.ndim == 3 else t
    small_res, loss = _adamw_slab(p_rows, *({k: flat(d[k]) for k in _SMALL_PARAMS} for d in (w, m, v)))
    res.update(small_res)
    res = {k: tuple(r.reshape(w[k].shape) for r in res[k]) for k in _WEIGHTS}

    outs = [loss, grad_x.reshape(x.shape)]
    for part in range(4):
        outs.extend(res[k][part] for k in _WEIGHTS)
    return tuple(outs)
```

```python
import functools

import jax
import jax.numpy as jnp
import numpy as np
from jax import lax
from jax.experimental import pallas as pl
from jax.experimental.pallas import tpu as pltpu

F32 = jnp.float32
BF16 = jnp.bfloat16

D_MODEL = 1024
GM_WIDTH = 512
SSM_WIDTH = 512
CONV_CH = 1024
N_HEADS = 8
HEAD_DIM = 64
N_STATE = 128
CHUNK = 128
D_FF = 4096
IN_COLS = 2568
IN_PAD = 2688
N_DEV = 8
EPS = 1e-6
ADAM_LR, ADAM_B1, ADAM_B2, ADAM_EPS, ADAM_WD, ADAM_STEP = 0.001, 0.9, 0.999, 1e-08, 0.01, 10
VMEM_LIMIT_BYTES = 56 * 1024 * 1024
TOKEN_TILE = 512
FF_TILE = 2048
WGRAD_TILE = 512

_NT = (((1,), (1,)), ((), ()))
_TN = (((0,), (0,)), ((), ()))


def _params(*sem):
    return pltpu.CompilerParams(dimension_semantics=sem or None, vmem_limit_bytes=VMEM_LIMIT_BYTES)


def _dot(a, b, dims=None):
    if dims is None:
        return jnp.dot(a, b, preferred_element_type=F32)
    return lax.dot_general(a, b, dims, preferred_element_type=F32)


def _split_terms(x, terms):
    out, rem = [], x
    for i in range(terms):
        hi = rem.astype(BF16)
        out.append(hi)
        if i + 1 < terms:
            rem = rem - hi.astype(F32)
    return out


def _split_dot(x, m, terms):
    acc = None
    for hi in _split_terms(x, terms):
        part = _dot(hi, m)
        acc = part if acc is None else acc + part
    return acc


def _split_dot_left(m, x, terms):
    acc = None
    for hi in _split_terms(x, terms):
        part = _dot(m, hi)
        acc = part if acc is None else acc + part
    return acc


def _gelu_and_grad(x):
    c = 0.7978845608028654
    inner = c * (x + 0.044715 * x * x * x)
    t = jnp.tanh(inner)
    g = 0.5 * x * (1.0 + t)
    dg = 0.5 * (1.0 + t) + 0.5 * x * (1.0 - t * t) * c * (1.0 + 3.0 * 0.044715 * x * x)
    return g, dg


def _softplus(x):
    return jnp.maximum(x, 0.0) + jnp.log(1.0 + jnp.exp(-jnp.abs(x)))


def _rsum(x):
    return jnp.sum(x, axis=0, keepdims=True)


def _acc_rows(ref, part, first):
    val = jnp.broadcast_to(part, ref.shape)

    @pl.when(first)
    def _():
        ref[...] = val

    @pl.when(jnp.logical_not(first))
    def _():
        ref[...] += val


def _rms_bwd(n, g, dout):
    r = lax.rsqrt(jnp.mean(n * n, axis=-1, keepdims=True) + EPS)
    nh = n * r
    dg = dout * g
    dn = r * (dg - nh * jnp.mean(dg * nh, axis=-1, keepdims=True))
    return dn, _rsum(dout * nh)


def _const_mats():
    avg = np.kron(np.eye(4), np.full((HEAD_DIM, HEAD_DIM), 1.0 / HEAD_DIM))
    expand = np.zeros((CHUNK, SSM_WIDTH), np.float32)
    for h in range(N_HEADS):
        expand[h, h * HEAD_DIM:(h + 1) * HEAD_DIM] = 1.0
    tril = np.tril(np.ones((CHUNK, CHUNK), np.float32))
    as_bf16 = lambda a: jnp.asarray(a, dtype=BF16)
    return as_bf16(avg), as_bf16(expand), as_bf16(expand.T), as_bf16(tril), as_bf16(tril.T)


def _full(shape):
    nd = len(shape)
    return pl.BlockSpec(shape, lambda *_: (0,) * nd)


_HBM = pl.BlockSpec(memory_space=pltpu.HBM)
_SEM = pl.BlockSpec(memory_space=pltpu.SEMAPHORE)
_ALL_PEERS = tuple((k, 0) for k in range(1, N_DEV))
_SAME_CORE_PEERS = ((2, 0), (4, 0), (6, 0))
_SIBLING_FORWARD = ((1, 0), (1, 2), (1, 4), (1, 6))


def _flip(j, k):
    for bit in (4, 2, 1):
        if k & bit:
            j = j + bit - 2 * (j & bit)
    return j


def _copies(src, land, send_sems, recv_sems, hops):
    x, y, c = lax.axis_index("x"), lax.axis_index("y"), lax.axis_index("c")
    me = 4 * x + 2 * y + c
    out = []
    for t in range(len(src)):
        for i, (k, b) in enumerate(hops):
            pos = (1 - x if k & 4 else x, 1 - y if k & 2 else y, 1 - c if k & 1 else c)
            peer = _flip(me, k)
            sem = t * len(hops) + i
            mk = functools.partial(pltpu.make_async_remote_copy, send_sem=send_sems.at[sem], recv_sem=recv_sems.at[sem],
                                   device_id=pos, device_id_type=pl.DeviceIdType.MESH)
            if land[t] is None and src[t].shape[0] != N_DEV:
                width = src[t].shape[1] // N_DEV
                slab = lambda j: src[t].at[:, pl.ds(pl.multiple_of(j * width, 128), width)]
                mine = functools.partial(mk, src_ref=slab(_flip(me, b)), dst_ref=slab(_flip(me, b)))
                theirs = functools.partial(mk, src_ref=slab(_flip(peer, b)), dst_ref=slab(_flip(peer, b)))
            elif land[t] is None:
                mine = functools.partial(mk, src_ref=src[t].at[_flip(me, b)], dst_ref=src[t].at[_flip(me, b)])
                theirs = functools.partial(mk, src_ref=src[t].at[_flip(peer, b)], dst_ref=src[t].at[_flip(peer, b)])
            else:
                assert b == 0
                mine = functools.partial(mk, src_ref=src[t].at[peer], dst_ref=land[t].at[me])
                theirs = functools.partial(mk, src_ref=src[t].at[peer], dst_ref=land[t].at[peer])
            out.append((mine, theirs))
    return out


def _exchange_start(srcs, inplace, peers, name, dep=None):
    n = len(srcs)
    lands = [None if ip else pltpu.with_memory_space_constraint(lax.empty(s.shape, s.dtype), pltpu.HBM)
             for s, ip in zip(srcs, inplace)]
    real_lands = [l for l in lands if l is not None]
    n_l = len(real_lands)
    deps = [] if dep is None else [dep]

    def body(*refs):
        src = refs[:n]
        land_refs = list(refs[n:n + n_l])
        send_sems, recv_sems = refs[n + n_l + len(deps)], refs[n + n_l + len(deps) + 1]
        token = refs[-1]
        land = [None if ip else land_refs.pop(0) for ip in inplace]
        for mine, _ in _copies(src, land, send_sems, recv_sems, peers):
            mine().start()
        token[...] = jnp.zeros_like(token)

    sem_t = pltpu.SemaphoreType.DMA((n * len(peers),))
    outs = pl.pallas_call(
        body, name=name,
        out_shape=(sem_t, sem_t) + tuple(pltpu.HBM(a.shape, a.dtype) for a in list(srcs) + real_lands)
        + (jax.ShapeDtypeStruct((8, 128), F32),),
        in_specs=[_HBM] * (n + n_l) + [pl.BlockSpec(memory_space=pl.ANY)] * len(deps),
        out_specs=(_SEM, _SEM) + (_HBM,) * (n + n_l) + (pl.BlockSpec(memory_space=pltpu.VMEM),),
        input_output_aliases={i: 2 + i for i in range(n + n_l)},
        compiler_params=pltpu.CompilerParams(has_side_effects=pltpu.SideEffectType.DATAFLOW_SIDE_EFFECTING),
    )(*[pltpu.with_memory_space_constraint(s, pltpu.HBM) for s in srcs], *real_lands, *deps)
    handle = dict(send=outs[0], recv=outs[1], srcs=outs[2:2 + n], lands=outs[2 + n:2 + n + n_l], inplace=inplace,
                  peers=peers)
    return handle, outs[-1]


def _exchange_wait(handle, after, name):
    srcs, lands, inplace, peers = handle["srcs"], handle["lands"], handle["inplace"], handle["peers"]
    n, n_l = len(srcs), len(lands)

    def body(*refs):
        src = refs[:n]
        land_refs = list(refs[n:n + n_l])
        send_sems, recv_sems = refs[n + n_l], refs[n + n_l + 1]
        land = [None if ip else land_refs.pop(0) for ip in inplace]
        for mine, theirs in _copies(src, land, send_sems, recv_sems, peers):
            mine().wait_send()
            theirs().wait_recv()

    outs = pl.pallas_call(
        body, name=name, out_shape=tuple(pltpu.HBM(a.shape, a.dtype) for a in list(srcs) + list(lands)),
        in_specs=[_HBM] * (n + n_l) + [_SEM, _SEM, pl.BlockSpec(memory_space=pl.ANY)],
        out_specs=(_HBM,) * (n + n_l), input_output_aliases={i: i for i in range(n + n_l)},
        compiler_params=pltpu.CompilerParams(has_side_effects=pltpu.SideEffectType.DATAFLOW_SIDE_EFFECTING),
    )(*srcs, *lands, handle["send"], handle["recv"], after)
    res, land_out = [], list(outs[n:])
    for t in range(n):
        res.append((outs[t], outs[t] if inplace[t] else land_out.pop(0)))
    return res


def _cast_to_slot(w, me, rows, name, cols=False, dep=None):
    r, cdim = w.shape
    deps = [] if dep is None else [dep]

    def body(me_ref, w_ref, *rest):
        o_ref = rest[-1]
        if cols:
            o_ref[...] = w_ref[...].astype(BF16)
        else:
            o_ref[0] = w_ref[...].astype(BF16)

    if cols:
        out_shape = jax.ShapeDtypeStruct((r, N_DEV * cdim), BF16)
        out_spec = pl.BlockSpec((rows, cdim), lambda i, me_ref: (i, me_ref[0]))
    else:
        out_shape = jax.ShapeDtypeStruct((N_DEV, r, cdim), BF16)
        out_spec = pl.BlockSpec((1, rows, cdim), lambda i, me_ref: (me_ref[0], i, 0))
    return pl.pallas_call(
        body, name=name, out_shape=out_shape,
        grid_spec=pltpu.PrefetchScalarGridSpec(
            num_scalar_prefetch=1, grid=(r // rows,),
            in_specs=[pl.BlockSpec((rows, cdim), lambda i, me_ref: (i, 0))]
            + [pl.BlockSpec(memory_space=pl.ANY)] * len(deps), out_specs=out_spec),
        compiler_params=_params("parallel"))(me, w, *deps)


def _adamw_math(w, g, m, v):
    m = ADAM_B1 * m + (1.0 - ADAM_B1) * g
    v = ADAM_B2 * v + (1.0 - ADAM_B2) * (g * g)
    m_hat = m / (1.0 - ADAM_B1 ** ADAM_STEP)
    v_hat = v / (1.0 - ADAM_B2 ** ADAM_STEP)
    delta = -ADAM_LR * (m_hat / (jnp.sqrt(v_hat) + ADAM_EPS) + ADAM_WD * w)
    return delta, m, v


def _sum_parts(me, p_ref, own):
    g = None
    for j in range(N_DEV):
        term = (p_ref[j] if own is None else jnp.where(me == j, own, p_ref[j])).astype(F32)
        g = term if g is None else g + term
    return g


def _adamw_reduce(parts, own, me, w, m, v, rows, name):
    r, cdim = w.shape

    def body(me_ref, p_ref, own_ref, w_ref, m_ref, v_ref, g_out, d_out, m_out, v_out):
        g = _sum_parts(me_ref[0], p_ref, own_ref[0])
        d, mn, vn = _adamw_math(w_ref[...], g, m_ref[...], v_ref[...])
        g_out[...] = g
        d_out[...] = d
        m_out[...] = mn
        v_out[...] = vn

    blk = pl.BlockSpec((rows, cdim), lambda i, me_ref: (i, 0))
    sds = jax.ShapeDtypeStruct(w.shape, F32)
    return pl.pallas_call(
        body, name=name, out_shape=(sds,) * 4,
        grid_spec=pltpu.PrefetchScalarGridSpec(
            num_scalar_prefetch=1, grid=(r // rows,),
            in_specs=[pl.BlockSpec((N_DEV, rows, cdim), lambda i, me_ref: (0, i, 0)),
                      pl.BlockSpec((1, rows, cdim), lambda i, me_ref: (me_ref[0], i, 0)), blk, blk, blk],
            out_specs=(blk,) * 4),
        compiler_params=_params("parallel"))(me, parts, own, w, m, v)


def _adamw_small(parts, own, me, w, m, v, mask, name):
    def body(me_ref, *refs):
        refs = list(refs)
        p_ref = refs.pop(0)
        own_ref = None if own is None else refs.pop(0)
        w_ref, m_ref, v_ref = refs[:3]
        k_ref = None if mask is None else refs[3]
        g_out, d_out, m_out, v_out = refs[-4:]
        g = _sum_parts(me_ref[0], p_ref, None if own is None else own_ref[me_ref[0]])
        if mask is not None:
            g = g * k_ref[...]
        d, mn, vn = _adamw_math(w_ref[...], g, m_ref[...], v_ref[...])
        g_out[...] = g
        d_out[...] = d
        m_out[...] = mn
        v_out[...] = vn

    def whole(shape):
        nd = len(shape)
        return pl.BlockSpec(shape, lambda i, me_ref: (0,) * nd)

    sds = jax.ShapeDtypeStruct(w.shape, F32)
    ins = [parts] + ([] if own is None else [own]) + [w, m, v] + ([] if mask is None else [mask])
    return pl.pallas_call(
        body, name=name, out_shape=(sds,) * 4,
        grid_spec=pltpu.PrefetchScalarGridSpec(
            num_scalar_prefetch=1, grid=(1,), in_specs=[whole(a.shape) for a in ins],
            out_specs=(whole(w.shape),) * 4),
        compiler_params=_params("arbitrary"))(me, *ins)


_IN_SPLITS = ((0, 512), (512, 1024), (1024, 1536), (1536, 2560), (2560, IN_PAD))


def _prenorm(x, g1, tm, dep=None):
    t_tok = x.shape[0]
    deps = [] if dep is None else [dep]

    def body(x_ref, g_ref, *rest):
        xv = x_ref[...]
        r = lax.rsqrt(jnp.mean(xv * xv, axis=-1, keepdims=True) + EPS)
        rest[-1][...] = (xv * r * g_ref[...]).astype(BF16)

    row = pl.BlockSpec((tm, D_MODEL), lambda i: (i, 0))
    return pl.pallas_call(
        body, name="prenorm", grid=(t_tok // tm,), out_shape=jax.ShapeDtypeStruct((t_tok, D_MODEL), BF16),
        in_specs=[row, _full((1, D_MODEL))] + [pl.BlockSpec(memory_space=pl.ANY)] * len(deps), out_specs=row,
        compiler_params=_params("parallel"))(x, g1, *deps)


def _in_proj(h1, w_in, tm):
    t_tok = h1.shape[0]

    def body(h_ref, w_ref, *outs):
        h = h_ref[...]
        for (a, b), o_ref in zip(_IN_SPLITS, outs):
            o_ref[...] = _dot(h, w_ref[a:b, :], _NT).astype(o_ref.dtype)

    row = lambda n: pl.BlockSpec((tm, n), lambda i: (i, 0))
    widths = [b - a for a, b in _IN_SPLITS]
    dtypes = (BF16, BF16, BF16, F32, F32)
    return pl.pallas_call(
        body, name="in_proj", grid=(t_tok // tm,),
        out_shape=tuple(jax.ShapeDtypeStruct((t_tok, n), dt) for n, dt in zip(widths, dtypes)),
        in_specs=[row(D_MODEL), _full((IN_PAD, D_MODEL))], out_specs=tuple(row(n) for n in widths),
        compiler_params=_params("parallel"))(h1, w_in)


def _lane_masks():
    lane = lax.broadcasted_iota(jnp.int32, (1, 2 * HEAD_DIM), 1)
    left = (lane < HEAD_DIM).astype(F32)
    return left, 1.0 - left


def _stack_pair(v, m_l, m_r):
    return jnp.concatenate([v * m_l, v * m_r], axis=0).astype(BF16)


def _head_mean(x, avg):
    n = avg.shape[0]
    return jnp.concatenate([_split_dot(x[:, n * i:n * (i + 1)], avg, 2) for i in range(x.shape[1] // n)], axis=1)


def _gmlp_common(u, v, lnw, lnb, avg, wcat_ref, bias, m_l, m_r):
    ug, dug = _gelu_and_grad(u)
    vg, dvg = _gelu_and_grad(v)
    mu = _head_mean(vg, avg)
    vc = vg - mu
    var = _head_mean(vc * vc, avg)
    rstd = lax.rsqrt(var + EPS)
    vhat = vc * rstd
    vn = vhat * lnw + lnb
    rows = []
    for r in range(u.shape[0] // CHUNK):
        cols = []
        for j in range(N_HEADS // 2):
            pair = vn[CHUNK * r:CHUNK * (r + 1), 128 * j:128 * (j + 1)]
            cols.append(_dot(wcat_ref[j], _stack_pair(pair, m_l, m_r)))
        rows.append(jnp.concatenate(cols, axis=1) + bias)
    mixed = jnp.concatenate(rows, axis=0)
    return ug, dug, dvg, rstd, vhat, vn, mixed


_GMLP_ROWS = 4 * CHUNK


def _gmlp_fwd(u, v, lnw, lnb, wcat, bias, avg):
    t_tok = u.shape[0]
    tm = min(_GMLP_ROWS, t_tok)

    def body(u_ref, v_ref, lnw_ref, lnb_ref, wcat_ref, bias_ref, avg_ref, o_ref):
        m_l, m_r = _lane_masks()
        ug, _, _, _, _, _, mixed = _gmlp_common(
            u_ref[...].astype(F32), v_ref[...].astype(F32), lnw_ref[...], lnb_ref[...], avg_ref[...], wcat_ref,
            bias_ref[...], m_l, m_r)
        o_ref[...] = (ug * mixed).astype(BF16)

    row = pl.BlockSpec((tm, GM_WIDTH), lambda i: (i, 0))
    return pl.pallas_call(
        body, name="gmlp_fwd", grid=(t_tok // tm,), out_shape=jax.ShapeDtypeStruct((t_tok, GM_WIDTH), BF16),
        in_specs=[row, row, _full((1, GM_WIDTH)), _full((1, GM_WIDTH)), _full(wcat.shape), _full(bias.shape),
                  _full(avg.shape)],
        out_specs=row, compiler_params=_params("parallel"))(u, v, lnw, lnb, wcat, bias, avg)


def _shift_rows(x, edge, j, down):
    groups, cols = x.shape[0] // 8, x.shape[1]
    amount = j if down else 8 - j
    rot = pltpu.roll(x.reshape(groups, 8, cols), amount, axis=1)
    edge_rot = pltpu.roll(edge, amount, axis=0)[None]
    sub = lax.broadcasted_iota(jnp.int32, (1, 8, 1), 1)
    if down:
        out = jnp.where(sub < j, jnp.concatenate([edge_rot, rot[:-1]], axis=0), rot)
    else:
        out = jnp.where(sub < 8 - j, rot, jnp.concatenate([rot[1:], edge_rot], axis=0))
    return out.reshape(x.shape)


def _conv_pre(xbc, tail, cw_ref, cb):
    taps = [_shift_rows(xbc, tail, 3 - k, True) for k in range(3)] + [xbc]
    return cb + cw_ref[0:1, :] * taps[0] + cw_ref[1:2, :] * taps[1] + cw_ref[2:3, :] * taps[2] + cw_ref[3:4, :] * taps[3]


def _ssd_common(pre, dtr, dtb, alog, expand, tril):
    q = CHUNK
    sg = jax.nn.sigmoid(pre)
    act = pre * sg
    lane = lax.broadcasted_iota(jnp.int32, (1, CHUNK), 1)
    a_row = jnp.where(lane < N_HEADS, -jnp.exp(alog), 0.0)
    dtp = dtr + dtb
    dt = _softplus(dtp)
    a_cs = _split_dot_left(tril, dt * a_row, 3)
    a_cs_t = a_cs.T
    dt_exp = _split_dot(dt, expand, 3)
    a_exp = _split_dot(a_cs, expand, 3)
    a_end = a_exp[q - 1:q, :]
    li = lax.broadcasted_iota(jnp.int32, (q, q), 0)
    si = lax.broadcasted_iota(jnp.int32, (q, q), 1)
    causal = si <= li
    decay = []
    for h in range(N_HEADS):
        seg = a_cs[:, h:h + 1] - a_cs_t[h:h + 1, :]
        decay.append(jnp.where(causal, jnp.exp(jnp.minimum(seg, 0.0)), 0.0))
    return dict(pre=pre, sg=sg, act=act, a_row=a_row, dtp=dtp, dt=dt, dt_exp=dt_exp, a_exp=a_exp,
                e=jnp.exp(a_exp), w_end=jnp.exp(a_end - a_exp), cd=jnp.exp(a_end), decay=decay)


def _ssd_specs(t_tok, seq, reverse):
    nb, nc = t_tok // seq, seq // CHUNK

    def chunk(c):
        return nc - 1 - c if reverse else c

    def row(n, col=0):
        return pl.BlockSpec((nb, CHUNK, n), lambda c: (0, chunk(c), col))

    tail = pl.BlockSpec((nb, 8, CONV_CH), lambda c: (0, jnp.maximum(chunk(c) * (CHUNK // 8) - 1, 0), 0))
    states = pl.BlockSpec((nb, 1, N_STATE, SSM_WIDTH), lambda c: (0, chunk(c), 0, 0))
    fold = lambda a: a.reshape(nb, seq, a.shape[-1])
    unfold = lambda a: a.reshape(t_tok, a.shape[-1])
    return nb, nc, row, tail, states, fold, unfold


def _ssd_fwd(z, xbc, dtr, cw, cb, dtb, alog, dskip_exp, nw, expand, tril, seq):
    t_tok = z.shape[0]
    nb, nc, row, tail, states_spec, fold, unfold = _ssd_specs(t_tok, seq, False)

    def body(z_ref, xbc_ref, tail_ref, dtr_ref, cw_ref, cb_ref, dtb_ref, alog_ref, dsk_ref, nw_ref, exp_ref,
             tril_ref, o_ref, y_ref, st_ref, pre_ref, state_ref):
        c = pl.program_id(0)

        @pl.when(c == 0)
        def _():
            state_ref[...] = jnp.zeros_like(state_ref)

        m_l, m_r = _lane_masks()
        for s in range(nb):
            pre = _conv_pre(xbc_ref[s], jnp.where(c == 0, 0.0, tail_ref[s]), cw_ref, cb_ref[...])
            pre_ref[s] = pre
            f = _ssd_common(pre, dtr_ref[s], dtb_ref[...], alog_ref[...], exp_ref[...], tril_ref[...])
            act = f["act"]
            xs = act[:, :SSM_WIDTH]
            xdt = xs * f["dt_exp"]
            xw = xdt * f["w_end"]
            state = state_ref[s]
            st_ref[s, 0] = state
            ydiag, yoff, snew = [], [], []
            for g in range(2):
                bg = act[:, 512 + 128 * g:640 + 128 * g].astype(BF16)
                cg = act[:, 768 + 128 * g:896 + 128 * g].astype(BF16)
                cb_mat = _dot(cg, bg, _NT)
                for pr in range(2):
                    h0 = 4 * g + 2 * pr
                    gcat = jnp.concatenate(
                        [(cb_mat * f["decay"][h0]).astype(BF16), (cb_mat * f["decay"][h0 + 1]).astype(BF16)], axis=1)
                    ydiag.append(_dot(gcat, _stack_pair(xdt[:, 64 * h0:64 * h0 + 128], m_l, m_r)))
                yoff.append(_dot(cg, state[:, 256 * g:256 * (g + 1)].astype(BF16)))
                snew.append(_dot(bg, xw[:, 256 * g:256 * (g + 1)].astype(BF16), _TN))
            y = jnp.concatenate(ydiag, axis=1) + f["e"] * jnp.concatenate(yoff, axis=1) + dsk_ref[...] * xs
            state_ref[s] = state * f["cd"] + jnp.concatenate(snew, axis=1)
            y_ref[s] = y
            zv = z_ref[s].astype(F32)
            yg = y * (zv * jax.nn.sigmoid(zv))
            outs = []
            for g in range(2):
                ygg = yg[:, 256 * g:256 * (g + 1)]
                outs.append(ygg * lax.rsqrt(jnp.mean(ygg * ygg, axis=-1, keepdims=True) + EPS))
            o_ref[s] = (jnp.concatenate(outs, axis=1) * nw_ref[...]).astype(BF16)

    consts = [cw, cb, dtb, alog, dskip_exp, nw, expand, tril]
    sd = lambda n, dt: jax.ShapeDtypeStruct((nb, seq, n), dt)
    o, y, states, pre = pl.pallas_call(
        body, name="ssd_fwd", grid=(nc,),
        out_shape=(sd(SSM_WIDTH, BF16), sd(SSM_WIDTH, F32), jax.ShapeDtypeStruct((nb, nc, N_STATE, SSM_WIDTH), F32),
                   sd(CONV_CH, F32)),
        in_specs=[row(SSM_WIDTH), row(CONV_CH), tail, row(CHUNK)] + [_full(a.shape) for a in consts],
        out_specs=(row(SSM_WIDTH), row(SSM_WIDTH), states_spec, row(CONV_CH)),
        scratch_shapes=[pltpu.VMEM((nb, N_STATE, SSM_WIDTH), F32)],
        compiler_params=_params("arbitrary"))(fold(z), fold(xbc), fold(xbc), fold(dtr), *consts)
    return unfold(o), unfold(y), states, unfold(pre)


def _out_proj(mix_a, mix_b, w_out, x, g2, g3, tm, dep=None):
    t_tok = x.shape[0]
    deps = [] if dep is None else [dep]

    def body(a_ref, b_ref, w_ref, x_ref, g2_ref, g3_ref, *rest):
        o_ref, x2_ref, h3_ref = rest[-3:]
        o = _dot(a_ref[...], w_ref[0:GM_WIDTH, :]) + _dot(b_ref[...], w_ref[GM_WIDTH:, :])
        o_ref[...] = o
        r2 = lax.rsqrt(jnp.mean(o * o, axis=-1, keepdims=True) + EPS)
        x2 = x_ref[...] + o * r2 * g2_ref[...]
        x2_ref[...] = x2
        r3 = lax.rsqrt(jnp.mean(x2 * x2, axis=-1, keepdims=True) + EPS)
        h3_ref[...] = (x2 * r3 * g3_ref[...]).astype(BF16)

    row = lambda n: pl.BlockSpec((tm, n), lambda i: (i, 0))
    sd = lambda dt: jax.ShapeDtypeStruct((t_tok, D_MODEL), dt)
    return pl.pallas_call(
        body, name="out_proj", grid=(t_tok // tm,), out_shape=(sd(F32), sd(F32), sd(BF16)),
        in_specs=[row(GM_WIDTH), row(SSM_WIDTH), _full((D_MODEL, D_MODEL)), row(D_MODEL), _full((1, D_MODEL)),
                  _full((1, D_MODEL))] + [pl.BlockSpec(memory_space=pl.ANY)] * len(deps),
        out_specs=(row(D_MODEL),) * 3, compiler_params=_params("parallel"))(mix_a, mix_b, w_out, x, g2, g3, *deps)


def _mlp_fwd(h3, w_up, w_down, x2, target, g4, tm, tf):
    t_tok = x2.shape[0]

    def up_body(h_ref, wu_ref, ra_ref):
        ra_ref[...] = jnp.maximum(_dot(h_ref[...], wu_ref[...]), 0.0).astype(BF16)

    tu = min(2 * tm, t_tok)
    ra = pl.pallas_call(
        up_body, name="mlp_up", grid=(D_FF // tf, t_tok // tu), out_shape=jax.ShapeDtypeStruct((t_tok, D_FF), BF16),
        in_specs=[pl.BlockSpec((tu, D_MODEL), lambda j, i: (i, 0)), pl.BlockSpec((D_MODEL, tf), lambda j, i: (0, j))],
        out_specs=pl.BlockSpec((tu, tf), lambda j, i: (i, j)), compiler_params=_params("parallel", "parallel"))(h3, w_up)

    def down_body(ra_ref, wd_ref, x2_ref, t_ref, g4_ref, dd_ref, dy_ref, dg4_ref, loss_ref):
        i = pl.program_id(0)
        rav = ra_ref[...]
        dvec = _dot(rav * rav, wd_ref[...])
        r4 = lax.rsqrt(jnp.mean(dvec * dvec, axis=-1, keepdims=True) + EPS)
        dn = dvec * r4
        g4 = g4_ref[...]
        err = x2_ref[...] + dn * g4 - t_ref[...]
        dy = err * (1.0 / D_MODEL)
        dy_ref[...] = dy
        dg = dy * g4
        dd_ref[...] = (r4 * (dg - dn * jnp.mean(dg * dn, axis=-1, keepdims=True))).astype(BF16)
        _acc_rows(dg4_ref, _rsum(dy * dn), i == 0)
        tile_loss = 0.5 * jnp.sum(jnp.sum(err * err, axis=-1, keepdims=True), axis=0, keepdims=True) / D_MODEL
        _acc_rows(loss_ref, jnp.broadcast_to(tile_loss, (1, 128)), i == 0)

    row = pl.BlockSpec((tm, D_MODEL), lambda i: (i, 0))
    dd, dy, dg4, loss = pl.pallas_call(
        down_body, name="mlp_down", grid=(t_tok // tm,),
        out_shape=(jax.ShapeDtypeStruct((t_tok, D_MODEL), BF16), jax.ShapeDtypeStruct((t_tok, D_MODEL), F32),
                   jax.ShapeDtypeStruct((1, D_MODEL), F32), jax.ShapeDtypeStruct((1, 128), F32)),
        in_specs=[pl.BlockSpec((tm, D_FF), lambda i: (i, 0)), _full((D_FF, D_MODEL)), row, row, _full((1, D_MODEL))],
        out_specs=(row, row, _full((1, D_MODEL)), _full((1, 128))),
        compiler_params=_params("arbitrary"))(ra, w_down, x2, target, g4)
    return ra, dd, dy, dg4, loss


def _mlp_bwd(dd, w_down, ra, w_up, x2, dy, o, g3, g2, tm, tf):
    t_tok = x2.shape[0]

    def hidden_body(dd_ref, wd_ref, ra_ref, da_ref):
        df = _dot(dd_ref[...], wd_ref[...], _NT)
        da_ref[...] = (df * (2.0 * ra_ref[...].astype(F32))).astype(BF16)

    tu = min(2 * tm, t_tok)
    da = pl.pallas_call(
        hidden_body, name="mlp_bwd_hidden", grid=(D_FF // tf, t_tok // tu),
        out_shape=jax.ShapeDtypeStruct((t_tok, D_FF), BF16),
        in_specs=[pl.BlockSpec((tu, D_MODEL), lambda j, i: (i, 0)), pl.BlockSpec((tf, D_MODEL), lambda j, i: (j, 0)),
                  pl.BlockSpec((tu, tf), lambda j, i: (i, j))],
        out_specs=pl.BlockSpec((tu, tf), lambda j, i: (i, j)),
        compiler_params=_params("parallel", "parallel"))(dd, w_down, ra)

    def in_body(da_ref, wu_ref, x2_ref, dy_ref, o_ref, g3_ref, g2_ref, dx2_ref, do_ref, dg3_ref, dg2_ref):
        i = pl.program_id(0)
        dh3 = _dot(da_ref[...], wu_ref[...], _NT)
        dn3, dg3 = _rms_bwd(x2_ref[...], g3_ref[...], dh3)
        dx2 = dy_ref[...] + dn3
        dx2_ref[...] = dx2
        do, dg2 = _rms_bwd(o_ref[...], g2_ref[...], dx2)
        do_ref[...] = do.astype(BF16)
        _acc_rows(dg3_ref, dg3, i == 0)
        _acc_rows(dg2_ref, dg2, i == 0)

    row = pl.BlockSpec((tm, D_MODEL), lambda i: (i, 0))
    vec = _full((1, D_MODEL))
    sd = lambda dt: jax.ShapeDtypeStruct((t_tok, D_MODEL), dt)
    dx2, do, dg3, dg2 = pl.pallas_call(
        in_body, name="mlp_bwd_in", grid=(t_tok // tm,),
        out_shape=(sd(F32), sd(BF16), jax.ShapeDtypeStruct((1, D_MODEL), F32), jax.ShapeDtypeStruct((1, D_MODEL), F32)),
        in_specs=[pl.BlockSpec((tm, D_FF), lambda i: (i, 0)), _full((D_MODEL, D_FF)), row, row, row, vec, vec],
        out_specs=(row, row, vec, vec), compiler_params=_params("arbitrary"))(da, w_up, x2, dy, o, g3, g2)
    return da, dx2, do, dg3, dg2


def _wgrad(a, b, out_blocks, bm, bn, bk, square_a, name, dep=None):
    t_tok, m = a.shape
    n = b.shape[1]
    nk = t_tok // bk

    def body(a_ref, b_ref, *rest):
        o_ref, acc_ref = rest[-2:]
        k = pl.program_id(2)
        av = a_ref[...]
        if square_a:
            av = av * av
        part = _dot(av, b_ref[...], _TN)

        def emit(res):
            if out_blocks is None:
                o_ref[...] = res.astype(BF16)
            else:
                o_ref[0] = res.astype(BF16)

        if nk == 1:
            emit(part)
            return

        @pl.when(k == 0)
        def _():
            acc_ref[...] = part

        @pl.when(k > 0)
        def _():
            acc_ref[...] += part

        @pl.when(k == nk - 1)
        def _():
            emit(acc_ref[...])

    if out_blocks is None:
        out_shape = jax.ShapeDtypeStruct((m, n), BF16)
        out_spec = pl.BlockSpec((bm, bn), lambda i, j, k: (i, j))
    else:
        assert n // out_blocks == bn
        out_shape = jax.ShapeDtypeStruct((out_blocks, m, bn), BF16)
        out_spec = pl.BlockSpec((1, bm, bn), lambda i, j, k: (j, i, 0))
    deps = [] if dep is None else [dep]
    return pl.pallas_call(
        body, name=name, grid=(m // bm, n // bn, nk), out_shape=out_shape,
        in_specs=[pl.BlockSpec((bk, bm), lambda i, j, k: (k, i)), pl.BlockSpec((bk, bn), lambda i, j, k: (k, j))]
        + [pl.BlockSpec(memory_space=pl.ANY)] * len(deps),
        out_specs=out_spec, scratch_shapes=[pltpu.VMEM((bm, bn) if nk > 1 else (8, 128), F32)],
        compiler_params=_params("parallel", "parallel", "arbitrary"))(a, b, *deps)


def _wgrad_in(h1, pieces, bn, name, dep=None):
    t_tok = h1.shape[0]
    widths = [p.shape[1] for p in pieces]
    starts = [sum(widths[:i]) for i in range(len(widths))]

    def body(h_ref, *rest):
        piece_refs = rest[:len(widths)]
        o_ref = rest[-1]
        hv = h_ref[...]
        for a, n, r in zip(starts, widths, piece_refs):
            o_ref[a:a + n, :] = _dot(r[...], hv, _TN).astype(BF16)

    deps = [] if dep is None else [dep]
    return pl.pallas_call(
        body, name=name, grid=(D_MODEL // bn,), out_shape=jax.ShapeDtypeStruct((sum(widths), D_MODEL), BF16),
        in_specs=[pl.BlockSpec((t_tok, bn), lambda j: (0, j))] + [pl.BlockSpec((t_tok, n), lambda j: (0, 0)) for n in widths]
        + [pl.BlockSpec(memory_space=pl.ANY)] * len(deps),
        out_specs=pl.BlockSpec((sum(widths), bn), lambda j: (0, j)),
        compiler_params=_params("parallel"))(h1, *pieces, *deps)


def _dmix(do, w_out, tm, dep=None):
    t_tok = do.shape[0]

    def body(d_ref, w_ref, *rest):
        rest[-1][...] = _dot(d_ref[...], w_ref[...], _NT).astype(BF16)

    row = pl.BlockSpec((tm, D_MODEL), lambda i: (i, 0))
    deps = [] if dep is None else [dep]
    return pl.pallas_call(
        body, name="dmix", grid=(t_tok // tm,), out_shape=jax.ShapeDtypeStruct((t_tok, D_MODEL), BF16),
        in_specs=[row, _full((D_MODEL, D_MODEL))] + [pl.BlockSpec(memory_space=pl.ANY)] * len(deps), out_specs=row,
        compiler_params=_params("parallel"))(do, w_out, *deps)


def _gmlp_bwd(dmix, u, v, lnw, lnb, wcat, wtcat, bias, avg, expand_t):
    t_tok = u.shape[0]
    tm = min(_GMLP_ROWS, t_tok)

    def body(dm_ref, u_ref, v_ref, lnw_ref, lnb_ref, wcat_ref, wtcat_ref, bias_ref, avg_ref, expt_ref, du_ref, dv_ref,
             dw_ref, db_ref, dlnw_ref, dlnb_ref):
        i = pl.program_id(0)
        m_l, m_r = _lane_masks()
        avg = avg_ref[...]
        lnw = lnw_ref[...]
        ug, dug, dvg, rstd, vhat, vn, mixed = _gmlp_common(
            u_ref[...].astype(F32), v_ref[...].astype(F32), lnw, lnb_ref[...], avg, wcat_ref, bias_ref[...], m_l, m_r)
        dya = dm_ref[...].astype(F32)
        du_ref[...] = (dya * mixed * dug).astype(BF16)
        dmixed = dya * ug
        dvn_rows, dws, dbt = [], [None] * N_HEADS, None
        for r in range(tm // CHUNK):
            dvn_cols = []
            for j in range(N_HEADS // 2):
                dmp = dmixed[CHUNK * r:CHUNK * (r + 1), 128 * j:128 * (j + 1)]
                dvn_cols.append(_dot(wtcat_ref[j], _stack_pair(dmp, m_l, m_r)))
                vnp = vn[CHUNK * r:CHUNK * (r + 1), 128 * j:128 * (j + 1)].astype(BF16)
                for i_h, mask in enumerate((m_l, m_r)):
                    part = _dot((dmp * mask).astype(BF16), vnp, _NT)
                    dws[2 * j + i_h] = part if r == 0 else dws[2 * j + i_h] + part
            dvn_rows.append(jnp.concatenate(dvn_cols, axis=1))
            part = _split_dot(dmixed[CHUNK * r:CHUNK * (r + 1), :], expt_ref[...], 2)
            dbt = part if r == 0 else dbt + part
        dvn = jnp.concatenate(dvn_rows, axis=0)
        dvh = dvn * lnw
        dvgel = rstd * (dvh - _head_mean(dvh, avg) - vhat * _head_mean(dvh * vhat, avg))
        dv_ref[...] = (dvgel * dvg).astype(BF16)
        first = i == 0

        @pl.when(first)
        def _():
            for h in range(N_HEADS):
                dw_ref[h] = dws[h]
            db_ref[...] = dbt

        @pl.when(jnp.logical_not(first))
        def _():
            for h in range(N_HEADS):
                dw_ref[h] += dws[h]
            db_ref[...] += dbt

        _acc_rows(dlnw_ref, _rsum(dvn * vhat), first)
        _acc_rows(dlnb_ref, _rsum(dvn), first)

    row = pl.BlockSpec((tm, GM_WIDTH), lambda i: (i, 0))
    consts = [lnw, lnb, wcat, wtcat, bias, avg, expand_t]
    return pl.pallas_call(
        body, name="gmlp_bwd", grid=(t_tok // tm,),
        out_shape=(jax.ShapeDtypeStruct((t_tok, GM_WIDTH), BF16), jax.ShapeDtypeStruct((t_tok, GM_WIDTH), BF16),
                   jax.ShapeDtypeStruct((N_HEADS, CHUNK, CHUNK), F32), jax.ShapeDtypeStruct((CHUNK, CHUNK), F32),
                   jax.ShapeDtypeStruct((1, GM_WIDTH), F32), jax.ShapeDtypeStruct((1, GM_WIDTH), F32)),
        in_specs=[row, row, row] + [_full(a.shape) for a in consts],
        out_specs=(row, row, _full((N_HEADS, CHUNK, CHUNK)), _full((CHUNK, CHUNK)), _full((1, GM_WIDTH)),
                   _full((1, GM_WIDTH))),
        compiler_params=_params("arbitrary"))(dmix, u, v, *consts)


def _ssd_bwd(dmix, z, xbc, pre, dtr, y, states, cw, cb, dtb, alog, dskip_exp, nw, expand, expand_t, tril, triu, seq,
             dep=None):
    t_tok = z.shape[0]
    nb, nc, row, _, states_spec, fold, unfold = _ssd_specs(t_tok, seq, True)
    q = CHUNK

    def one_sequence(s, dm_ref, z_ref, xbc_ref, pre_ref, dtr_ref, y_ref, st_ref, cw_ref, dtb_ref, alog_ref, dsk_ref,
                     nw_ref, exp_ref, expt_ref, tril_ref, triu_ref, dz_ref, dxbc_ref, ddt_ref, dhead_ref, dstate_ref):
        m_l, m_r = _lane_masks()
        expt = expt_ref[...]
        f = _ssd_common(pre_ref[s], dtr_ref[s], dtb_ref[...], alog_ref[...], exp_ref[...], tril_ref[...])
        act, pre, sg = f["act"], f["pre"], f["sg"]
        xs = act[:, :SSM_WIDTH]
        xdt = xs * f["dt_exp"]
        xw = xdt * f["w_end"]
        state = st_ref[s, 0]
        dstate = dstate_ref[s]
        zv, yv, dout, nw = z_ref[s].astype(F32), y_ref[s], dm_ref[s].astype(F32), nw_ref[...]
        sz = jax.nn.sigmoid(zv)
        sl = zv * sz
        yg = yv * sl
        tv = dout * nw
        dyg_parts, ygh_parts = [], []
        for g in range(2):
            ygg = yg[:, 256 * g:256 * (g + 1)]
            rr = lax.rsqrt(jnp.mean(ygg * ygg, axis=-1, keepdims=True) + EPS)
            ygh = ygg * rr
            tg = tv[:, 256 * g:256 * (g + 1)]
            dyg_parts.append(rr * (tg - ygh * jnp.mean(tg * ygh, axis=-1, keepdims=True)))
            ygh_parts.append(ygh)
        dyg = jnp.concatenate(dyg_parts, axis=1)
        dnw = _rsum(dout * jnp.concatenate(ygh_parts, axis=1))
        dy = dyg * sl
        dz_ref[s] = (dyg * yv * (sz * (1.0 + zv * (1.0 - sz)))).astype(BF16)
        ddsk = _rsum(dy * xs)
        dye = dy * f["e"]
        lane = lax.broadcasted_iota(jnp.int32, (q, q), 1)
        sub = lax.broadcasted_iota(jnp.int32, (q, q), 0)
        rs_mat = jnp.zeros((q, q), F32)
        cs_mat = jnp.zeros((q, q), F32)
        dxdt_cols, yoff, dst_in, dxw, d_b, d_c = [], [], [], [], [], []
        for g in range(2):
            bg = act[:, 512 + 128 * g:640 + 128 * g].astype(BF16)
            cg = act[:, 768 + 128 * g:896 + 128 * g].astype(BF16)
            cb_mat = _dot(cg, bg, _NT)
            stg = state[:, 256 * g:256 * (g + 1)].astype(BF16)
            dyeg = dye[:, 256 * g:256 * (g + 1)].astype(BF16)
            yoff.append(_dot(cg, stg))
            dcg = _dot(dyeg, stg, _NT)
            dst_in.append(_dot(cg, dyeg, _TN))
            dcb = jnp.zeros((q, q), F32)
            for pr in range(2):
                h0 = 4 * g + 2 * pr
                gf = [cb_mat * f["decay"][h0], cb_mat * f["decay"][h0 + 1]]
                gcat = jnp.concatenate([gf[0].astype(BF16), gf[1].astype(BF16)], axis=1)
                xst = _stack_pair(xdt[:, 64 * h0:64 * h0 + 128], m_l, m_r)
                dyp = dy[:, 64 * h0:64 * h0 + 128].astype(BF16)
                dgcat = _dot(dyp, xst, _NT)
                dxst = _dot(gcat, dyp, _TN)
                dxdt_cols.append(dxst[:q] * m_l + dxst[q:] * m_r)
                for i in range(2):
                    h = h0 + i
                    dg = dgcat[:, q * i:q * (i + 1)]
                    mm = dg * gf[i]
                    rs_mat = rs_mat + jnp.where(lane == h, jnp.sum(mm, axis=1, keepdims=True), 0.0)
                    cs_mat = cs_mat + jnp.where(sub == h, jnp.sum(mm, axis=0, keepdims=True), 0.0)
                    dcb = dcb + dg * f["decay"][h]
            dcb16 = dcb.astype(BF16)
            dstg = dstate[:, 256 * g:256 * (g + 1)].astype(BF16)
            d_c.append(dcg + _dot(dcb16, bg))
            dxw.append(_dot(bg, dstg))
            d_b.append(_dot(dcb16, cg, _TN) + _dot(xw[:, 256 * g:256 * (g + 1)].astype(BF16), dstg, _NT))
        dxw = jnp.concatenate(dxw, axis=1)
        dxdt = jnp.concatenate(dxdt_cols, axis=1) + dxw * f["w_end"]
        qv = dxw * xw
        end_row = _rsum(qv) + _rsum(dstate * state) * f["cd"]
        x2 = dye * jnp.concatenate(yoff, axis=1) - qv
        row_i = lax.broadcasted_iota(jnp.int32, (q, 1), 0)
        x2 = x2 + jnp.where(row_i == q - 1, end_row, 0.0)
        da_cs = _split_dot(x2, expt, 2) + rs_mat - cs_mat.T
        ddt = _split_dot(dxdt * xs, expt, 2)
        dxs = dsk_ref[...] * dy + dxdt * f["dt_exp"]
        dda = _split_dot_left(triu_ref[...], da_cs, 3)
        ddt = ddt + dda * f["a_row"]
        dalog = _rsum(dda * f["dt"]) * f["a_row"]
        draw = ddt * jax.nn.sigmoid(f["dtp"])
        ddt_ref[s] = draw.astype(BF16)
        dact = jnp.concatenate([dxs] + d_b + d_c, axis=1)
        dpre = dact * (sg * (1.0 + pre * (1.0 - sg)))
        dhead = dhead_ref[s]
        xv = xbc_ref[s]
        shifted = [_shift_rows(dpre, dhead, 3 - k, False) for k in range(3)] + [dpre]
        dxbc = cw_ref[3:4, :] * dpre
        for k in range(3):
            dxbc = dxbc + cw_ref[k:k + 1, :] * shifted[k]
        dxbc_ref[s] = dxbc.astype(BF16)
        dhead_ref[s] = dpre[0:8, :]
        dstate_ref[s] = dstate * f["cd"] + jnp.concatenate(dst_in, axis=1)
        row8 = lax.broadcasted_iota(jnp.int32, (8, 1), 0)
        dcw = jnp.zeros((8, CONV_CH), F32)
        for k in range(4):
            dcw = dcw + jnp.where(row8 == k, _rsum(shifted[k] * xv), 0.0)
        return dcw, _rsum(dpre), _rsum(draw), dalog, _split_dot(ddsk, expt, 3), dnw

    def body(dm_ref, z_ref, xbc_ref, pre_ref, dtr_ref, y_ref, st_ref, cw_ref, cb_ref, dtb_ref, alog_ref, dsk_ref,
             nw_ref, exp_ref, expt_ref, tril_ref, triu_ref, dz_ref, dxbc_ref, ddt_ref, dcw_ref, dcb_ref, ddtb_ref,
             dalog_ref, dd_ref, dnw_ref, dhead_ref, dstate_ref):
        c = pl.program_id(0)
        first = c == 0

        @pl.when(first)
        def _():
            dstate_ref[...] = jnp.zeros_like(dstate_ref)
            dhead_ref[...] = jnp.zeros_like(dhead_ref)

        total = None
        for s in range(nb):
            parts = one_sequence(s, dm_ref, z_ref, xbc_ref, pre_ref, dtr_ref, y_ref, st_ref, cw_ref, dtb_ref, alog_ref,
                                 dsk_ref, nw_ref, exp_ref, expt_ref, tril_ref, triu_ref, dz_ref, dxbc_ref, ddt_ref,
                                 dhead_ref, dstate_ref)
            total = parts if total is None else tuple(a + b for a, b in zip(total, parts))
        dcw = total[0]

        @pl.when(first)
        def _():
            dcw_ref[...] = dcw

        @pl.when(jnp.logical_not(first))
        def _():
            dcw_ref[...] += dcw

        for ref, part in zip((dcb_ref, ddtb_ref, dalog_ref, dd_ref, dnw_ref), total[1:]):
            _acc_rows(ref, part, first)

    consts = [cw, cb, dtb, alog, dskip_exp, nw, expand, expand_t, tril, triu]
    deps = [] if dep is None else [dep]
    n_in = 7 + len(consts)

    def body_skipping_dep(*refs):
        body(*refs[:n_in], *refs[n_in + len(deps):])

    acc = lambda n: jax.ShapeDtypeStruct((1, n), F32)
    sd = lambda n: jax.ShapeDtypeStruct((nb, seq, n), BF16)
    dz, dxbc, ddt, *small_grads = pl.pallas_call(
        body_skipping_dep, name="ssd_bwd", grid=(nc,),
        out_shape=(sd(SSM_WIDTH), sd(CONV_CH), sd(CHUNK), jax.ShapeDtypeStruct((8, CONV_CH), F32), acc(CONV_CH),
                   acc(CHUNK), acc(CHUNK), acc(CHUNK), acc(SSM_WIDTH)),
        in_specs=[row(SSM_WIDTH, col=1), row(SSM_WIDTH), row(CONV_CH), row(CONV_CH), row(CHUNK), row(SSM_WIDTH),
                  states_spec]
        + [_full(a.shape) for a in consts] + [pl.BlockSpec(memory_space=pl.ANY)] * len(deps),
        out_specs=(row(SSM_WIDTH), row(CONV_CH), row(CHUNK), _full((8, CONV_CH)), _full((1, CONV_CH)),
                   _full((1, CHUNK)), _full((1, CHUNK)), _full((1, CHUNK)), _full((1, SSM_WIDTH))),
        scratch_shapes=[pltpu.VMEM((nb, 8, CONV_CH), F32), pltpu.VMEM((nb, N_STATE, SSM_WIDTH), F32)],
        compiler_params=_params("arbitrary"))(
            fold(dmix), fold(z), fold(xbc), fold(pre), fold(dtr), fold(y), states, *consts, *deps)
    return (unfold(dz), unfold(dxbc), unfold(ddt), *small_grads)


def _in_bwd(du, dv, dz, dxbc, ddt, w_in, x, dx2, g1, tm, me, riders=(), dep=None):
    t_tok = x.shape[0]
    steps = t_tok // tm

    def body(me_ref, du_ref, dv_ref, dz_ref, dxbc_ref, ddt_ref, w_ref, x_ref, dx2_ref, g_ref, *rest):
        rider_in = rest[:5 * len(riders)]
        outs = rest[len(rest) - 2 - 4 * len(riders):]
        gx_ref, dg_ref = outs[:2]
        i = pl.program_id(0)
        dh = None
        for (a, b), ref in zip(_IN_SPLITS, (du_ref, dv_ref, dz_ref, dxbc_ref, ddt_ref)):
            part = _dot(ref[...], w_ref[a:b, :])
            dh = part if dh is None else dh + part
        dn, dg = _rms_bwd(x_ref[...], g_ref[...], dh)
        gx_ref[...] = dx2_ref[...] + dn
        _acc_rows(dg_ref, dg, i == 0)
        for r in range(len(riders)):
            p_ref, own_ref, w_ref_r, m_ref_r, v_ref_r = rider_in[5 * r:5 * r + 5]
            g = _sum_parts(me_ref[0], p_ref, own_ref[0])
            d, mn, vn = _adamw_math(w_ref_r[...], g, m_ref_r[...], v_ref_r[...])
            for o_ref, val in zip(outs[2 + 4 * r:6 + 4 * r], (g, d, mn, vn)):
                o_ref[...] = val

    row = lambda n: pl.BlockSpec((tm, n), lambda i, me_ref: (i, 0))
    whole = lambda shape: pl.BlockSpec(shape, lambda i, me_ref: (0,) * len(shape))
    widths = [b - a for a, b in _IN_SPLITS]
    deps = [] if dep is None else [dep]
    rider_args, rider_specs, rider_out_shapes, rider_out_specs = [], [], [], []
    for rd in riders:
        rows, cols = rd["w"].shape[0] // steps, rd["w"].shape[1]
        blk = pl.BlockSpec((rows, cols), lambda i, me_ref: (i, 0))
        rider_args += [rd["parts"], rd["own"], rd["w"], rd["m"], rd["v"]]
        rider_specs += [pl.BlockSpec((N_DEV, rows, cols), lambda i, me_ref: (0, i, 0)),
                        pl.BlockSpec((1, rows, cols), lambda i, me_ref: (me_ref[0], i, 0)), blk, blk, blk]
        rider_out_shapes += [jax.ShapeDtypeStruct(rd["w"].shape, F32)] * 4
        rider_out_specs += [blk] * 4
    outs = pl.pallas_call(
        body, name="in_bwd",
        out_shape=(jax.ShapeDtypeStruct((t_tok, D_MODEL), F32), jax.ShapeDtypeStruct((1, D_MODEL), F32),
                   *rider_out_shapes),
        grid_spec=pltpu.PrefetchScalarGridSpec(
            num_scalar_prefetch=1, grid=(steps,),
            in_specs=[row(n) for n in widths] + [whole((IN_PAD, D_MODEL)), row(D_MODEL), row(D_MODEL),
                                                 whole((1, D_MODEL))] + rider_specs
            + [pl.BlockSpec(memory_space=pl.ANY)] * len(deps),
            out_specs=(row(D_MODEL), whole((1, D_MODEL)), *rider_out_specs)),
        compiler_params=_params("arbitrary"))(me, du, dv, dz, dxbc, ddt, w_in, x, dx2, g1, *rider_args, *deps)
    return outs[0], outs[1], [tuple(outs[2 + 4 * r:6 + 4 * r]) for r in range(len(riders))]


def _pad_lanes(a, n):
    return jnp.pad(a, ((0, 0), (0, n - a.shape[1])))


def _local_step(x, target, seq, small, hooks, first_dep=None):
    t_tok = x.shape[0]
    tm = min(TOKEN_TILE, t_tok)
    avg, expand, expand_t, tril, triu = _const_mats()
    g1, g2, g3, g4 = (small[k].reshape(1, D_MODEL) for k in
                      ("norm_mix_pre", "norm_mix_post", "norm_ffn_pre", "norm_ffn_post"))
    tie = (lambda a: a) if first_dep is None else (lambda a: a + first_dep[0, 0])
    lnw = tie(small["gm_ln_w"]).reshape(1, GM_WIDTH)
    lnb = tie(small["gm_ln_b"]).reshape(1, GM_WIDTH)
    causal = jnp.tril(jnp.ones((CHUNK, CHUNK), F32))
    wm = tie(small["gm_w_s"]) * causal
    pair = lambda w: w.reshape(4, 2, CHUNK, CHUNK).transpose(0, 2, 1, 3).reshape(4, CHUNK, 2 * CHUNK).astype(BF16)
    wcat = pair(wm)
    wtcat = pair(jnp.swapaxes(wm, 1, 2))
    bias = jnp.repeat(tie(small["gm_b_s"]).T, HEAD_DIM, axis=1)
    cb = small["conv_b"].reshape(1, CONV_CH)
    dtb = _pad_lanes(tie(small["dt_bias"]).reshape(1, N_HEADS), CHUNK)
    alog = _pad_lanes(tie(small["a_log"]).reshape(1, N_HEADS), CHUNK)
    dskip_exp = jnp.repeat(tie(small["d_skip"]).reshape(1, N_HEADS), HEAD_DIM, axis=1)
    nw = small["ssm_norm_w"].reshape(1, SSM_WIDTH)

    h1 = _prenorm(x, g1, tm, hooks.get("prenorm_after", first_dep))
    w_in_t, conv_w = hooks["mixer_weights"](h1)
    u, v, z, xbc, dtr = _in_proj(h1, w_in_t, tm)
    mix_a = _gmlp_fwd(u, v, lnw, lnb, wcat, bias, avg)
    mix_b, y_pre, states, pre = _ssd_fwd(z, xbc, dtr, conv_w, cb, dtb, alog, dskip_exp, nw, expand, tril, seq)
    w_out, dep = hooks["mixers_done"](mix_b)
    o, x2, h3 = _out_proj(mix_a, mix_b, w_out, x, g2, g3, tm, dep)
    w_up, w_down = hooks["mlp_weights"](h3)
    tf = FF_TILE
    ra, dd, dy, dg4, loss = _mlp_fwd(h3, w_up, w_down, x2, target, g4, tm, tf)

    da, dx2, do, dg3, dg2 = _mlp_bwd(dd, w_down, ra, w_up, x2, dy, o, g3, g2, tm, tf)
    g_w_down = _wgrad(ra, dd, None, WGRAD_TILE, D_MODEL, t_tok, True, "wgrad_down")
    g_w_up = _wgrad(h3, da, N_DEV, D_MODEL, D_FF // N_DEV, t_tok, False, "wgrad_up")
    dep = hooks["mlp_grads"](g_w_down, g_w_up)
    dmix = _dmix(do, w_out, tm, dep)
    g_w_out = _wgrad_in(do, (mix_a, mix_b), WGRAD_TILE, "wgrad_out", dep)
    du, dv, dws, dbt, dlnw, dlnb = _gmlp_bwd(dmix, u, v, lnw, lnb, wcat, wtcat, bias, avg, expand_t)
    dep = hooks["gmlp_grads"](g_w_out, dws)
    dz, dxbc, ddt, dcw, dcb, ddtb, dalog, ddsk, dnw = _ssd_bwd(
        dmix, z, xbc, pre, dtr, y_pre, states, conv_w, cb, dtb, alog, dskip_exp, nw, expand, expand_t, tril, triu, seq,
        dep)
    g_w_in = jnp.concatenate([_wgrad_in(h1, (du, dv, dz), WGRAD_TILE, "wgrad_in_a", dep),
                              _wgrad_in(h1, (dxbc, ddt), WGRAD_TILE, "wgrad_in_b", dep)], axis=0)
    dep = hooks["in_grads"](g_w_in, dcw[0:4])
    riders = hooks["arrived_updates"](dep) if "arrived_updates" in hooks else []
    me = hooks.get("me", jnp.zeros((1,), jnp.int32))
    grad_x, dg1, updates = _in_bwd(du, dv, dz, dxbc, ddt, w_in_t, x, dx2, g1, tm, me, riders, dep)

    grads = dict(
        updates=updates,
        w_in=g_w_in, w_out=g_w_out, w_up=g_w_up, w_down=g_w_down, conv_w=dcw[0:4],
        norm_mix_pre=dg1, norm_mix_post=dg2, norm_ffn_pre=dg3, norm_ffn_post=dg4, gm_ln_w=dlnw, gm_ln_b=dlnb,
        gm_w_s=dws, gm_b_s=dbt, conv_b=dcb, dt_bias=ddtb, a_log=dalog, d_skip=ddsk, ssm_norm_w=dnw)
    return loss[0, 0], grad_x, grads


_WEIGHTS = ("norm_mix_pre", "w_in", "gm_ln_w", "gm_ln_b", "gm_w_s", "gm_b_s", "conv_w", "conv_b", "dt_bias", "a_log",
            "d_skip", "ssm_norm_w", "w_out", "norm_mix_post", "norm_ffn_pre", "w_up", "w_down", "norm_ffn_post")
_SLAB_ROWS = (("norm_mix_pre", 1024), ("norm_mix_post", 1024), ("norm_ffn_pre", 1024), ("norm_ffn_post", 1024),
              ("conv_b", 1024), ("ssm_norm_w", 512), ("gm_ln_w", 512), ("gm_ln_b", 512), ("dt_bias", 8), ("a_log", 8),
              ("d_skip", 8))
_SLAB_LOSS_ROW = len(_SLAB_ROWS)
_SLAB_BS_ROW = 16
_SMALL_PARAMS = tuple(name for name, _ in _SLAB_ROWS) + ("gm_b_s",)
_LN_PARAMS = ("gm_ln_w", "gm_ln_b")


def _pack_slab(g, loss_part):
    rows = [_pad_lanes(g[name], D_MODEL) for name, _ in _SLAB_ROWS]
    rows.append(jnp.broadcast_to(loss_part, (1, D_MODEL)))
    rows.append(jnp.zeros((_SLAB_BS_ROW - len(rows), D_MODEL), F32))
    rows.append(_pad_lanes(g["gm_b_s"].T[0:N_HEADS], D_MODEL))
    return jnp.concatenate(rows, axis=0)


def _adamw_slab(parts, w, m, v):
    names = _SMALL_PARAMS
    shapes = [w[k].shape for k in names]
    unfold = np.zeros((GM_WIDTH, HEAD_DIM), np.float32)
    for h in range(N_HEADS):
        unfold[h * HEAD_DIM:(h + 1) * HEAD_DIM, :] = np.eye(HEAD_DIM)
    unfold = jnp.asarray(unfold, dtype=BF16)
    n = len(names)

    def body(p_ref, unfold_ref, *refs):
        w_refs, m_refs, v_refs = refs[:n], refs[n:2 * n], refs[2 * n:3 * n]
        outs = refs[3 * n:]
        g_all = p_ref[0]
        for j in range(1, N_DEV):
            g_all = g_all + p_ref[j]
        lane = lax.broadcasted_iota(jnp.int32, (N_HEADS, GM_WIDTH), 1)
        head = lax.broadcasted_iota(jnp.int32, (N_HEADS, GM_WIDTH), 0)
        own_lanes = jnp.logical_and(lane >= head * HEAD_DIM, lane < (head + 1) * HEAD_DIM)
        for i, name in enumerate(names):
            if name == "gm_b_s":
                g = g_all[_SLAB_BS_ROW:_SLAB_BS_ROW + N_HEADS, 0:CHUNK]
            else:
                row = [r for r, (k, _) in enumerate(_SLAB_ROWS) if k == name][0]
                g = g_all[row:row + 1, 0:dict(_SLAB_ROWS)[name]]
                if name in _LN_PARAMS:
                    g = _split_dot(jnp.where(own_lanes, g, 0.0), unfold_ref[...], 3)
            d, mn, vn = _adamw_math(w_refs[i][...], g, m_refs[i][...], v_refs[i][...])
            for o_ref, val in zip(outs[4 * i:4 * i + 4], (g, d, mn, vn)):
                o_ref[...] = val
        outs[-1][...] = g_all[_SLAB_LOSS_ROW:_SLAB_LOSS_ROW + 1, 0:128]

    ins = [parts, unfold] + [d[k] for d in (w, m, v) for k in names]
    out_shape = tuple(jax.ShapeDtypeStruct(s, F32) for s in shapes for _ in range(4)) + (
        jax.ShapeDtypeStruct((1, 128), F32),)
    outs = pl.pallas_call(
        body, name="adamw_small", out_shape=out_shape, grid=(1,), in_specs=[_full(a.shape) for a in ins],
        out_specs=tuple(_full(s.shape) for s in out_shape), compiler_params=_params("arbitrary"))(*ins)
    return {k: tuple(outs[4 * i:4 * i + 4]) for i, k in enumerate(names)}, outs[-1][0, 0]


def kernel(x, norm_mix_pre, w_in, gm_ln_w, gm_ln_b, gm_w_s, gm_b_s, conv_w, conv_b, dt_bias, a_log, d_skip, ssm_norm_w, w_out, norm_mix_post, norm_ffn_pre, w_up, w_down, norm_ffn_post, loss_target, m_norm_mix_pre, m_w_in, m_gm_ln_w, m_gm_ln_b, m_gm_w_s, m_gm_b_s, m_conv_w, m_conv_b, m_dt_bias, m_a_log, m_d_skip, m_ssm_norm_w, m_w_out, m_norm_mix_post, m_norm_ffn_pre, m_w_up, m_w_down, m_norm_ffn_post, v_norm_mix_pre, v_w_in, v_gm_ln_w, v_gm_ln_b, v_gm_w_s, v_gm_b_s, v_conv_w, v_conv_b, v_dt_bias, v_a_log, v_d_skip, v_ssm_norm_w, v_w_out, v_norm_mix_post, v_norm_ffn_pre, v_w_up, v_w_down, v_norm_ffn_post):
    w = dict(norm_mix_pre=norm_mix_pre, w_in=w_in, gm_ln_w=gm_ln_w, gm_ln_b=gm_ln_b, gm_w_s=gm_w_s, gm_b_s=gm_b_s, conv_w=conv_w, conv_b=conv_b, dt_bias=dt_bias, a_log=a_log, d_skip=d_skip, ssm_norm_w=ssm_norm_w, w_out=w_out, norm_mix_post=norm_mix_post, norm_ffn_pre=norm_ffn_pre, w_up=w_up, w_down=w_down, norm_ffn_post=norm_ffn_post)
    m = dict(norm_mix_pre=m_norm_mix_pre, w_in=m_w_in, gm_ln_w=m_gm_ln_w, gm_ln_b=m_gm_ln_b, gm_w_s=m_gm_w_s, gm_b_s=m_gm_b_s, conv_w=m_conv_w, conv_b=m_conv_b, dt_bias=m_dt_bias, a_log=m_a_log, d_skip=m_d_skip, ssm_norm_w=m_ssm_norm_w, w_out=m_w_out, norm_mix_post=m_norm_mix_post, norm_ffn_pre=m_norm_ffn_pre, w_up=m_w_up, w_down=m_w_down, norm_ffn_post=m_norm_ffn_post)
    v = dict(norm_mix_pre=v_norm_mix_pre, w_in=v_w_in, gm_ln_w=v_gm_ln_w, gm_ln_b=v_gm_ln_b, gm_w_s=v_gm_w_s, gm_b_s=v_gm_b_s, conv_w=v_conv_w, conv_b=v_conv_b, dt_bias=v_dt_bias, a_log=v_a_log, d_skip=v_d_skip, ssm_norm_w=v_ssm_norm_w, w_out=v_w_out, norm_mix_post=v_norm_mix_post, norm_ffn_pre=v_norm_ffn_pre, w_up=v_w_up, w_down=v_w_down, norm_ffn_post=v_norm_ffn_post)
    n_batch, seq, _ = x.shape
    shard_in = IN_COLS // N_DEV

    me = (4 * lax.axis_index("x") + 2 * lax.axis_index("y") + lax.axis_index("c")).astype(jnp.int32).reshape(1)

    def in_slot(own):
        return lax.dynamic_update_slice(lax.empty((N_DEV,) + own.shape, own.dtype), own[None],
                                        (me[0],) + (0,) * own.ndim)

    w_in_sh, m_in_sh, v_in_sh = w_in[0].T, m_w_in[0].T, v_w_in[0].T
    first = [_cast_to_slot(w_in_sh, me, shard_in, "cast_w_in"), in_slot(conv_w[0]),
             _cast_to_slot(w_out[0], me, 128, "cast_w_out")]
    ici_1, tok_ici_1 = _exchange_start(first, [True] * 3, _SAME_CORE_PEERS, "gather_mix_ici_start")
    cast_up = _cast_to_slot(w_up[0], me, 1024, "cast_w_up", cols=True, dep=tok_ici_1)
    second = [cast_up, _cast_to_slot(w_down[0], me, 512, "cast_w_down", dep=cast_up)]
    gathering = {}

    def mixer_weights(after):
        bufs = [buf for buf, _ in _exchange_wait(ici_1, after, "gather_mix_ici_wait")]
        d2d_1, tok_d2d_1 = _exchange_start(bufs, [True] * 3, _SIBLING_FORWARD, "gather_mix_d2d_start")
        gathering["mlp_ici"], tok_ici_2 = _exchange_start(
            second, [True] * 2, _SAME_CORE_PEERS, "gather_mlp_ici_start", dep=tok_d2d_1)
        (_, ag_in), (_, ag_conv), (_, ag_out) = _exchange_wait(d2d_1, tok_ici_2, "gather_mix_d2d_wait")
        gathering["w_out"] = ag_out.reshape(D_MODEL, D_MODEL)
        w_in_t = jnp.pad(ag_in.reshape(IN_COLS, D_MODEL), ((0, IN_PAD - IN_COLS), (0, 0)))
        return w_in_t, ag_conv.transpose(1, 0, 2).reshape(4, CONV_CH)

    def mixers_done(after):
        bufs = [buf for buf, _ in _exchange_wait(gathering["mlp_ici"], after, "gather_mlp_ici_wait")]
        gathering["mlp"], tok = _exchange_start(bufs, [True] * 2, _SIBLING_FORWARD, "gather_mlp_d2d_start")
        return gathering["w_out"], tok

    def mlp_weights(after):
        (_, ag_up), (_, ag_down) = _exchange_wait(gathering["mlp"], after, "gather_mlp_d2d_wait")
        return ag_up, ag_down.reshape(D_FF, D_MODEL)

    sent = {}

    def mlp_grads(g_w_down, g_w_up):
        sent["mlp"], tok = _exchange_start(
            [g_w_down.reshape(N_DEV, D_FF // N_DEV, D_MODEL), g_w_up], [False, False], _ALL_PEERS, "grads_mlp_start")
        return tok

    def gmlp_grads(g_w_out, g_w_s):
        sent["gmlp"], tok = _exchange_start(
            [g_w_out.reshape(N_DEV, D_MODEL // N_DEV, D_MODEL), in_slot(g_w_s.astype(BF16))], [False, True], _ALL_PEERS,
            "grads_gmlp_start")
        return tok

    def in_grads(g_w_in_t, g_conv_w):
        g_in_blk = g_w_in_t[:IN_COLS].reshape(N_DEV, shard_in, D_MODEL)
        g_conv_blk = g_conv_w.reshape(4, N_DEV, CONV_CH // N_DEV).transpose(1, 0, 2)
        sent["in"], tok = _exchange_start([g_in_blk, g_conv_blk], [False, False], _ALL_PEERS, "grads_in_start")
        return tok

    def arrived_updates(after):
        (own_down, p_down), (own_up, p_up) = _exchange_wait(sent["mlp"], after, "grads_mlp_wait")
        return [dict(parts=p_up, own=own_up, w=w_up[0], m=m_w_up[0], v=v_w_up[0]),
                dict(parts=p_down, own=own_down, w=w_down[0], m=m_w_down[0], v=v_w_down[0])]

    small = {k: w[k][0] for k in _SMALL_PARAMS + ("gm_w_s",)}
    loss_part, grad_x, g = _local_step(
        x.reshape(n_batch * seq, D_MODEL), loss_target.reshape(n_batch * seq, D_MODEL), seq, small,
        dict(mixer_weights=mixer_weights, mixers_done=mixers_done, mlp_weights=mlp_weights, mlp_grads=mlp_grads,
             gmlp_grads=gmlp_grads, in_grads=in_grads, arrived_updates=arrived_updates, me=me,
             prenorm_after=second[1]), first_dep=tok_ici_1)

    sent_rows, tok_rows = _exchange_start([in_slot(_pack_slab(g, loss_part))], [True], _ALL_PEERS, "grads_rows_start")
    res = dict(w_up=g["updates"][0], w_down=g["updates"][1])
    (own_out, p_out), (_, p_ws) = _exchange_wait(sent["gmlp"], tok_rows, "grads_gmlp_wait")
    res["w_out"] = _adamw_reduce(p_out, own_out, me, w_out[0], m_w_out[0], v_w_out[0], 128, "adamw_w_out")
    causal = jnp.tril(jnp.ones((1, CHUNK, CHUNK), F32))
    res["gm_w_s"] = _adamw_small(p_ws, None, me, gm_w_s[0], m_gm_w_s[0], v_gm_w_s[0], causal, "adamw_gm_w_s")
    (own_in, p_in), (own_conv, p_conv) = _exchange_wait(sent["in"], res["gm_w_s"][1], "grads_in_wait")
    res["w_in"] = tuple(r.T for r in _adamw_reduce(p_in, own_in, me, w_in_sh, m_in_sh, v_in_sh, shard_in, "adamw_w_in"))
    res["conv_w"] = _adamw_small(p_conv, own_conv, me, conv_w[0], m_conv_w[0], v_conv_w[0], None, "adamw_conv_w")
    ((_, p_rows),) = _exchange_wait(sent_rows, res["w_in"][1], "grads_rows_wait")
    flat = lambda t: t[0] if t.ndim == 3 else t
    small_res, loss = _adamw_slab(p_rows, *({k: flat(d[k]) for k in _SMALL_PARAMS} for d in (w, m, v)))
    res.update(small_res)
    res = {k: tuple(r.reshape(w[k].shape) for r in res[k]) for k in _WEIGHTS}

    outs = [loss, grad_x.reshape(x.shape)]
    for part in range(4):
        outs.extend(res[k][part] for k in _WEIGHTS)
    return tuple(outs)
```

```python
import functools

import jax
import jax.numpy as jnp
import numpy as np
from jax import lax
from jax.experimental import pallas as pl
from jax.experimental.pallas import tpu as pltpu

F32 = jnp.float32
BF16 = jnp.bfloat16

D_MODEL = 1024
GM_WIDTH = 512
SSM_WIDTH = 512
CONV_CH = 1024
N_HEADS = 8
HEAD_DIM = 64
N_STATE = 128
CHUNK = 128
D_FF = 4096
IN_COLS = 2568
IN_PAD = 2688
N_DEV = 8
EPS = 1e-6
ADAM_LR, ADAM_B1, ADAM_B2, ADAM_EPS, ADAM_WD, ADAM_STEP = 0.001, 0.9, 0.999, 1e-08, 0.01, 10
VMEM_LIMIT_BYTES = 56 * 1024 * 1024
TOKEN_TILE = 512
FF_TILE = 2048
WGRAD_TILE = 512

_NT = (((1,), (1,)), ((), ()))
_TN = (((0,), (0,)), ((), ()))


def _params(*sem):
    return pltpu.CompilerParams(dimension_semantics=sem or None, vmem_limit_bytes=VMEM_LIMIT_BYTES)


def _dot(a, b, dims=None):
    if dims is None:
        return jnp.dot(a, b, preferred_element_type=F32)
    return lax.dot_general(a, b, dims, preferred_element_type=F32)


def _split_terms(x, terms):
    out, rem = [], x
    for i in range(terms):
        hi = rem.astype(BF16)
        out.append(hi)
        if i + 1 < terms:
            rem = rem - hi.astype(F32)
    return out


def _split_dot(x, m, terms):
    acc = None
    for hi in _split_terms(x, terms):
        part = _dot(hi, m)
        acc = part if acc is None else acc + part
    return acc


def _split_dot_left(m, x, terms):
    acc = None
    for hi in _split_terms(x, terms):
        part = _dot(m, hi)
        acc = part if acc is None else acc + part
    return acc


def _gelu_and_grad(x):
    c = 0.7978845608028654
    inner = c * (x + 0.044715 * x * x * x)
    t = jnp.tanh(inner)
    g = 0.5 * x * (1.0 + t)
    dg = 0.5 * (1.0 + t) + 0.5 * x * (1.0 - t * t) * c * (1.0 + 3.0 * 0.044715 * x * x)
    return g, dg


def _softplus(x):
    return jnp.maximum(x, 0.0) + jnp.log(1.0 + jnp.exp(-jnp.abs(x)))


def _rsum(x):
    return jnp.sum(x, axis=0, keepdims=True)


def _acc_rows(ref, part, first):
    val = jnp.broadcast_to(part, ref.shape)

    @pl.when(first)
    def _():
        ref[...] = val

    @pl.when(jnp.logical_not(first))
    def _():
        ref[...] += val


def _rms_bwd(n, g, dout):
    r = lax.rsqrt(jnp.mean(n * n, axis=-1, keepdims=True) + EPS)
    nh = n * r
    dg = dout * g
    dn = r * (dg - nh * jnp.mean(dg * nh, axis=-1, keepdims=True))
    return dn, _rsum(dout * nh)


def _const_mats():
    avg = np.kron(np.eye(4), np.full((HEAD_DIM, HEAD_DIM), 1.0 / HEAD_DIM))
    expand = np.zeros((CHUNK, SSM_WIDTH), np.float32)
    for h in range(N_HEADS):
        expand[h, h * HEAD_DIM:(h + 1) * HEAD_DIM] = 1.0
    tril = np.tril(np.ones((CHUNK, CHUNK), np.float32))
    as_bf16 = lambda a: jnp.asarray(a, dtype=BF16)
    return as_bf16(avg), as_bf16(expand), as_bf16(expand.T), as_bf16(tril), as_bf16(tril.T)


def _full(shape):
    nd = len(shape)
    return pl.BlockSpec(shape, lambda *_: (0,) * nd)


_HBM = pl.BlockSpec(memory_space=pltpu.HBM)
_SEM = pl.BlockSpec(memory_space=pltpu.SEMAPHORE)
_ALL_PEERS = tuple((k, 0) for k in range(1, N_DEV))
_SAME_CORE_PEERS = ((2, 0), (4, 0), (6, 0))
_SIBLING_FORWARD = ((1, 0), (1, 2), (1, 4), (1, 6))


def _flip(j, k):
    for bit in (4, 2, 1):
        if k & bit:
            j = j + bit - 2 * (j & bit)
    return j


def _copies(src, land, send_sems, recv_sems, hops):
    x, y, c = lax.axis_index("x"), lax.axis_index("y"), lax.axis_index("c")
    me = 4 * x + 2 * y + c
    out = []
    for t in range(len(src)):
        for i, (k, b) in enumerate(hops):
            pos = (1 - x if k & 4 else x, 1 - y if k & 2 else y, 1 - c if k & 1 else c)
            peer = _flip(me, k)
            sem = t * len(hops) + i
            mk = functools.partial(pltpu.make_async_remote_copy, send_sem=send_sems.at[sem], recv_sem=recv_sems.at[sem],
                                   device_id=pos, device_id_type=pl.DeviceIdType.MESH)
            if land[t] is None and src[t].shape[0] != N_DEV:
                width = src[t].shape[1] // N_DEV
                slab = lambda j: src[t].at[:, pl.ds(pl.multiple_of(j * width, 128), width)]
                mine = functools.partial(mk, src_ref=slab(_flip(me, b)), dst_ref=slab(_flip(me, b)))
                theirs = functools.partial(mk, src_ref=slab(_flip(peer, b)), dst_ref=slab(_flip(peer, b)))
            elif land[t] is None:
                mine = functools.partial(mk, src_ref=src[t].at[_flip(me, b)], dst_ref=src[t].at[_flip(me, b)])
                theirs = functools.partial(mk, src_ref=src[t].at[_flip(peer, b)], dst_ref=src[t].at[_flip(peer, b)])
            else:
                assert b == 0
                mine = functools.partial(mk, src_ref=src[t].at[peer], dst_ref=land[t].at[me])
                theirs = functools.partial(mk, src_ref=src[t].at[peer], dst_ref=land[t].at[peer])
            out.append((mine, theirs))
    return out


def _exchange_start(srcs, inplace, peers, name, dep=None):
    n = len(srcs)
    lands = [None if ip else pltpu.with_memory_space_constraint(lax.empty(s.shape, s.dtype), pltpu.HBM)
             for s, ip in zip(srcs, inplace)]
    real_lands = [l for l in lands if l is not None]
    n_l = len(real_lands)
    deps = [] if dep is None else [dep]

    def body(*refs):
        src = refs[:n]
        land_refs = list(refs[n:n + n_l])
        send_sems, recv_sems = refs[n + n_l + len(deps)], refs[n + n_l + len(deps) + 1]
        token = refs[-1]
        land = [None if ip else land_refs.pop(0) for ip in inplace]
        for mine, _ in _copies(src, land, send_sems, recv_sems, peers):
            mine().start()
        token[...] = jnp.zeros_like(token)

    sem_t = pltpu.SemaphoreType.DMA((n * len(peers),))
    outs = pl.pallas_call(
        body, name=name,
        out_shape=(sem_t, sem_t) + tuple(pltpu.HBM(a.shape, a.dtype) for a in list(srcs) + real_lands)
        + (jax.ShapeDtypeStruct((8, 128), F32),),
        in_specs=[_HBM] * (n + n_l) + [pl.BlockSpec(memory_space=pl.ANY)] * len(deps),
        out_specs=(_SEM, _SEM) + (_HBM,) * (n + n_l) + (pl.BlockSpec(memory_space=pltpu.VMEM),),
        input_output_aliases={i: 2 + i for i in range(n + n_l)},
        compiler_params=pltpu.CompilerParams(has_side_effects=pltpu.SideEffectType.DATAFLOW_SIDE_EFFECTING),
    )(*[pltpu.with_memory_space_constraint(s, pltpu.HBM) for s in srcs], *real_lands, *deps)
    handle = dict(send=outs[0], recv=outs[1], srcs=outs[2:2 + n], lands=outs[2 + n:2 + n + n_l], inplace=inplace,
                  peers=peers)
    return handle, outs[-1]


def _exchange_wait(handle, after, name):
    srcs, lands, inplace, peers = handle["srcs"], handle["lands"], handle["inplace"], handle["peers"]
    n, n_l = len(srcs), len(lands)

    def body(*refs):
        src = refs[:n]
        land_refs = list(refs[n:n + n_l])
        send_sems, recv_sems = refs[n + n_l], refs[n + n_l + 1]
        land = [None if ip else land_refs.pop(0) for ip in inplace]
        for mine, theirs in _copies(src, land, send_sems, recv_sems, peers):
            mine().wait_send()
            theirs().wait_recv()

    outs = pl.pallas_call(
        body, name=name, out_shape=tuple(pltpu.HBM(a.shape, a.dtype) for a in list(srcs) + list(lands)),
        in_specs=[_HBM] * (n + n_l) + [_SEM, _SEM, pl.BlockSpec(memory_space=pl.ANY)],
        out_specs=(_HBM,) * (n + n_l), input_output_aliases={i: i for i in range(n + n_l)},
        compiler_params=pltpu.CompilerParams(has_side_effects=pltpu.SideEffectType.DATAFLOW_SIDE_EFFECTING),
    )(*srcs, *lands, handle["send"], handle["recv"], after)
    res, land_out = [], list(outs[n:])
    for t in range(n):
        res.append((outs[t], outs[t] if inplace[t] else land_out.pop(0)))
    return res


def _cast_to_slot(w, me, rows, name, cols=False, dep=None):
    r, cdim = w.shape
    deps = [] if dep is None else [dep]

    def body(me_ref, w_ref, *rest):
        o_ref = rest[-1]
        if cols:
            o_ref[...] = w_ref[...].astype(BF16)
        else:
            o_ref[0] = w_ref[...].astype(BF16)

    if cols:
        out_shape = jax.ShapeDtypeStruct((r, N_DEV * cdim), BF16)
        out_spec = pl.BlockSpec((rows, cdim), lambda i, me_ref: (i, me_ref[0]))
    else:
        out_shape = jax.ShapeDtypeStruct((N_DEV, r, cdim), BF16)
        out_spec = pl.BlockSpec((1, rows, cdim), lambda i, me_ref: (me_ref[0], i, 0))
    return pl.pallas_call(
        body, name=name, out_shape=out_shape,
        grid_spec=pltpu.PrefetchScalarGridSpec(
            num_scalar_prefetch=1, grid=(r // rows,),
            in_specs=[pl.BlockSpec((rows, cdim), lambda i, me_ref: (i, 0))]
            + [pl.BlockSpec(memory_space=pl.ANY)] * len(deps), out_specs=out_spec),
        compiler_params=_params("parallel"))(me, w, *deps)


def _adamw_math(w, g, m, v):
    m = ADAM_B1 * m + (1.0 - ADAM_B1) * g
    v = ADAM_B2 * v + (1.0 - ADAM_B2) * (g * g)
    m_hat = m / (1.0 - ADAM_B1 ** ADAM_STEP)
    v_hat = v / (1.0 - ADAM_B2 ** ADAM_STEP)
    delta = -ADAM_LR * (m_hat / (jnp.sqrt(v_hat) + ADAM_EPS) + ADAM_WD * w)
    return delta, m, v


def _sum_parts(me, p_ref, own):
    g = None
    for j in range(N_DEV):
        term = (p_ref[j] if own is None else jnp.where(me == j, own, p_ref[j])).astype(F32)
        g = term if g is None else g + term
    return g


def _adamw_reduce(parts, own, me, w, m, v, rows, name):
    r, cdim = w.shape

    def body(me_ref, p_ref, own_ref, w_ref, m_ref, v_ref, g_out, d_out, m_out, v_out):
        g = _sum_parts(me_ref[0], p_ref, own_ref[0])
        d, mn, vn = _adamw_math(w_ref[...], g, m_ref[...], v_ref[...])
        g_out[...] = g
        d_out[...] = d
        m_out[...] = mn
        v_out[...] = vn

    blk = pl.BlockSpec((rows, cdim), lambda i, me_ref: (i, 0))
    sds = jax.ShapeDtypeStruct(w.shape, F32)
    return pl.pallas_call(
        body, name=name, out_shape=(sds,) * 4,
        grid_spec=pltpu.PrefetchScalarGridSpec(
            num_scalar_prefetch=1, grid=(r // rows,),
            in_specs=[pl.BlockSpec((N_DEV, rows, cdim), lambda i, me_ref: (0, i, 0)),
                      pl.BlockSpec((1, rows, cdim), lambda i, me_ref: (me_ref[0], i, 0)), blk, blk, blk],
            out_specs=(blk,) * 4),
        compiler_params=_params("parallel"))(me, parts, own, w, m, v)


def _adamw_small(parts, own, me, w, m, v, mask, name):
    def body(me_ref, *refs):
        refs = list(refs)
        p_ref = refs.pop(0)
        own_ref = None if own is None else refs.pop(0)
        w_ref, m_ref, v_ref = refs[:3]
        k_ref = None if mask is None else refs[3]
        g_out, d_out, m_out, v_out = refs[-4:]
        g = _sum_parts(me_ref[0], p_ref, None if own is None else own_ref[me_ref[0]])
        if mask is not None:
            g = g * k_ref[...]
        d, mn, vn = _adamw_math(w_ref[...], g, m_ref[...], v_ref[...])
        g_out[...] = g
        d_out[...] = d
        m_out[...] = mn
        v_out[...] = vn

    def whole(shape):
        nd = len(shape)
        return pl.BlockSpec(shape, lambda i, me_ref: (0,) * nd)

    sds = jax.ShapeDtypeStruct(w.shape, F32)
    ins = [parts] + ([] if own is None else [own]) + [w, m, v] + ([] if mask is None else [mask])
    return pl.pallas_call(
        body, name=name, out_shape=(sds,) * 4,
        grid_spec=pltpu.PrefetchScalarGridSpec(
            num_scalar_prefetch=1, grid=(1,), in_specs=[whole(a.shape) for a in ins],
            out_specs=(whole(w.shape),) * 4),
        compiler_params=_params("arbitrary"))(me, *ins)


_IN_SPLITS = ((0, 512), (512, 1024), (1024, 1536), (1536, 2560), (2560, IN_PAD))


def _prenorm(x, g1, tm, dep=None):
    t_tok = x.shape[0]
    deps = [] if dep is None else [dep]

    def body(x_ref, g_ref, *rest):
        xv = x_ref[...]
        r = lax.rsqrt(jnp.mean(xv * xv, axis=-1, keepdims=True) + EPS)
        rest[-1][...] = (xv * r * g_ref[...]).astype(BF16)

    row = pl.BlockSpec((tm, D_MODEL), lambda i: (i, 0))
    return pl.pallas_call(
        body, name="prenorm", grid=(t_tok // tm,), out_shape=jax.ShapeDtypeStruct((t_tok, D_MODEL), BF16),
        in_specs=[row, _full((1, D_MODEL))] + [pl.BlockSpec(memory_space=pl.ANY)] * len(deps), out_specs=row,
        compiler_params=_params("parallel"))(x, g1, *deps)


def _in_proj(h1, w_in, tm):
    t_tok = h1.shape[0]

    def body(h_ref, w_ref, *outs):
        h = h_ref[...]
        for (a, b), o_ref in zip(_IN_SPLITS, outs):
            o_ref[...] = _dot(h, w_ref[a:b, :], _NT).astype(o_ref.dtype)

    row = lambda n: pl.BlockSpec((tm, n), lambda i: (i, 0))
    widths = [b - a for a, b in _IN_SPLITS]
    dtypes = (BF16, BF16, BF16, F32, F32)
    return pl.pallas_call(
        body, name="in_proj", grid=(t_tok // tm,),
        out_shape=tuple(jax.ShapeDtypeStruct((t_tok, n), dt) for n, dt in zip(widths, dtypes)),
        in_specs=[row(D_MODEL), _full((IN_PAD, D_MODEL))], out_specs=tuple(row(n) for n in widths),
        compiler_params=_params("parallel"))(h1, w_in)


def _lane_masks():
    lane = lax.broadcasted_iota(jnp.int32, (1, 2 * HEAD_DIM), 1)
    left = (lane < HEAD_DIM).astype(F32)
    return left, 1.0 - left


def _stack_pair(v, m_l, m_r):
    return jnp.concatenate([v * m_l, v * m_r], axis=0).astype(BF16)


def _head_mean(x, avg):
    n = avg.shape[0]
    return jnp.concatenate([_split_dot(x[:, n * i:n * (i + 1)], avg, 2) for i in range(x.shape[1] // n)], axis=1)


def _gmlp_common(u, v, lnw, lnb, avg, wcat_ref, bias, m_l, m_r):
    ug, dug = _gelu_and_grad(u)
    vg, dvg = _gelu_and_grad(v)
    mu = _head_mean(vg, avg)
    vc = vg - mu
    var = _head_mean(vc * vc, avg)
    rstd = lax.rsqrt(var + EPS)
    vhat = vc * rstd
    vn = vhat * lnw + lnb
    rows = []
    for r in range(u.shape[0] // CHUNK):
        cols = []
        for j in range(N_HEADS // 2):
            pair = vn[CHUNK * r:CHUNK * (r + 1), 128 * j:128 * (j + 1)]
            cols.append(_dot(wcat_ref[j], _stack_pair(pair, m_l, m_r)))
        rows.append(jnp.concatenate(cols, axis=1) + bias)
    mixed = jnp.concatenate(rows, axis=0)
    return ug, dug, dvg, rstd, vhat, vn, mixed


_GMLP_ROWS = 4 * CHUNK


def _gmlp_fwd(u, v, lnw, lnb, wcat, bias, avg):
    t_tok = u.shape[0]
    tm = min(_GMLP_ROWS, t_tok)

    def body(u_ref, v_ref, lnw_ref, lnb_ref, wcat_ref, bias_ref, avg_ref, o_ref):
        m_l, m_r = _lane_masks()
        ug, _, _, _, _, _, mixed = _gmlp_common(
            u_ref[...].astype(F32), v_ref[...].astype(F32), lnw_ref[...], lnb_ref[...], avg_ref[...], wcat_ref,
            bias_ref[...], m_l, m_r)
        o_ref[...] = (ug * mixed).astype(BF16)

    row = pl.BlockSpec((tm, GM_WIDTH), lambda i: (i, 0))
    return pl.pallas_call(
        body, name="gmlp_fwd", grid=(t_tok // tm,), out_shape=jax.ShapeDtypeStruct((t_tok, GM_WIDTH), BF16),
        in_specs=[row, row, _full((1, GM_WIDTH)), _full((1, GM_WIDTH)), _full(wcat.shape), _full(bias.shape),
                  _full(avg.shape)],
        out_specs=row, compiler_params=_params("parallel"))(u, v, lnw, lnb, wcat, bias, avg)


def _shift_rows(x, edge, j, down):
    groups, cols = x.shape[0] // 8, x.shape[1]
    amount = j if down else 8 - j
    rot = pltpu.roll(x.reshape(groups, 8, cols), amount, axis=1)
    edge_rot = pltpu.roll(edge, amount, axis=0)[None]
    sub = lax.broadcasted_iota(jnp.int32, (1, 8, 1), 1)
    if down:
        out = jnp.where(sub < j, jnp.concatenate([edge_rot, rot[:-1]], axis=0), rot)
    else:
        out = jnp.where(sub < 8 - j, rot, jnp.concatenate([rot[1:], edge_rot], axis=0))
    return out.reshape(x.shape)


def _conv_pre(xbc, tail, cw_ref, cb):
    taps = [_shift_rows(xbc, tail, 3 - k, True) for k in range(3)] + [xbc]
    return cb + cw_ref[0:1, :] * taps[0] + cw_ref[1:2, :] * taps[1] + cw_ref[2:3, :] * taps[2] + cw_ref[3:4, :] * taps[3]


def _ssd_common(pre, dtr, dtb, alog, expand, tril):
    q = CHUNK
    sg = jax.nn.sigmoid(pre)
    act = pre * sg
    lane = lax.broadcasted_iota(jnp.int32, (1, CHUNK), 1)
    a_row = jnp.where(lane < N_HEADS, -jnp.exp(alog), 0.0)
    dtp = dtr + dtb
    dt = _softplus(dtp)
    a_cs = _split_dot_left(tril, dt * a_row, 3)
    a_cs_t = a_cs.T
    dt_exp = _split_dot(dt, expand, 3)
    a_exp = _split_dot(a_cs, expand, 3)
    a_end = a_exp[q - 1:q, :]
    li = lax.broadcasted_iota(jnp.int32, (q, q), 0)
    si = lax.broadcasted_iota(jnp.int32, (q, q), 1)
    causal = si <= li
    decay = []
    for h in range(N_HEADS):
        seg = a_cs[:, h:h + 1] - a_cs_t[h:h + 1, :]
        decay.append(jnp.where(causal, jnp.exp(jnp.minimum(seg, 0.0)), 0.0))
    return dict(pre=pre, sg=sg, act=act, a_row=a_row, dtp=dtp, dt=dt, dt_exp=dt_exp, a_exp=a_exp,
                e=jnp.exp(a_exp), w_end=jnp.exp(a_end - a_exp), cd=jnp.exp(a_end), decay=decay)


def _ssd_specs(t_tok, seq, reverse):
    nb, nc = t_tok // seq, seq // CHUNK

    def chunk(c):
        return nc - 1 - c if reverse else c

    def row(n, col=0):
        return pl.BlockSpec((nb, CHUNK, n), lambda c: (0, chunk(c), col))

    tail = pl.BlockSpec((nb, 8, CONV_CH), lambda c: (0, jnp.maximum(chunk(c) * (CHUNK // 8) - 1, 0), 0))
    states = pl.BlockSpec((nb, 1, N_STATE, SSM_WIDTH), lambda c: (0, chunk(c), 0, 0))
    fold = lambda a: a.reshape(nb, seq, a.shape[-1])
    unfold = lambda a: a.reshape(t_tok, a.shape[-1])
    return nb, nc, row, tail, states, fold, unfold


def _ssd_fwd(z, xbc, dtr, cw, cb, dtb, alog, dskip_exp, nw, expand, tril, seq):
    t_tok = z.shape[0]
    nb, nc, row, tail, states_spec, fold, unfold = _ssd_specs(t_tok, seq, False)

    def body(z_ref, xbc_ref, tail_ref, dtr_ref, cw_ref, cb_ref, dtb_ref, alog_ref, dsk_ref, nw_ref, exp_ref,
             tril_ref, o_ref, y_ref, st_ref, pre_ref, state_ref):
        c = pl.program_id(0)

        @pl.when(c == 0)
        def _():
            state_ref[...] = jnp.zeros_like(state_ref)

        m_l, m_r = _lane_masks()
        for s in range(nb):
            pre = _conv_pre(xbc_ref[s], jnp.where(c == 0, 0.0, tail_ref[s]), cw_ref, cb_ref[...])
            pre_ref[s] = pre
            f = _ssd_common(pre, dtr_ref[s], dtb_ref[...], alog_ref[...], exp_ref[...], tril_ref[...])
            act = f["act"]
            xs = act[:, :SSM_WIDTH]
            xdt = xs * f["dt_exp"]
            xw = xdt * f["w_end"]
            state = state_ref[s]
            st_ref[s, 0] = state
            ydiag, yoff, snew = [], [], []
            for g in range(2):
                bg = act[:, 512 + 128 * g:640 + 128 * g].astype(BF16)
                cg = act[:, 768 + 128 * g:896 + 128 * g].astype(BF16)
                cb_mat = _dot(cg, bg, _NT)
                for pr in range(2):
                    h0 = 4 * g + 2 * pr
                    gcat = jnp.concatenate(
                        [(cb_mat * f["decay"][h0]).astype(BF16), (cb_mat * f["decay"][h0 + 1]).astype(BF16)], axis=1)
                    ydiag.append(_dot(gcat, _stack_pair(xdt[:, 64 * h0:64 * h0 + 128], m_l, m_r)))
                yoff.append(_dot(cg, state[:, 256 * g:256 * (g + 1)].astype(BF16)))
                snew.append(_dot(bg, xw[:, 256 * g:256 * (g + 1)].astype(BF16), _TN))
            y = jnp.concatenate(ydiag, axis=1) + f["e"] * jnp.concatenate(yoff, axis=1) + dsk_ref[...] * xs
            state_ref[s] = state * f["cd"] + jnp.concatenate(snew, axis=1)
            y_ref[s] = y
            zv = z_ref[s].astype(F32)
            yg = y * (zv * jax.nn.sigmoid(zv))
            outs = []
            for g in range(2):
                ygg = yg[:, 256 * g:256 * (g + 1)]
                outs.append(ygg * lax.rsqrt(jnp.mean(ygg * ygg, axis=-1, keepdims=True) + EPS))
            o_ref[s] = (jnp.concatenate(outs, axis=1) * nw_ref[...]).astype(BF16)

    consts = [cw, cb, dtb, alog, dskip_exp, nw, expand, tril]
    sd = lambda n, dt: jax.ShapeDtypeStruct((nb, seq, n), dt)
    o, y, states, pre = pl.pallas_call(
        body, name="ssd_fwd", grid=(nc,),
        out_shape=(sd(SSM_WIDTH, BF16), sd(SSM_WIDTH, F32), jax.ShapeDtypeStruct((nb, nc, N_STATE, SSM_WIDTH), F32),
                   sd(CONV_CH, F32)),
        in_specs=[row(SSM_WIDTH), row(CONV_CH), tail, row(CHUNK)] + [_full(a.shape) for a in consts],
        out_specs=(row(SSM_WIDTH), row(SSM_WIDTH), states_spec, row(CONV_CH)),
        scratch_shapes=[pltpu.VMEM((nb, N_STATE, SSM_WIDTH), F32)],
        compiler_params=_params("arbitrary"))(fold(z), fold(xbc), fold(xbc), fold(dtr), *consts)
    return unfold(o), unfold(y), states, unfold(pre)


def _out_proj(mix_a, mix_b, w_out, x, g2, g3, tm, dep=None):
    t_tok = x.shape[0]
    deps = [] if dep is None else [dep]

    def body(a_ref, b_ref, w_ref, x_ref, g2_ref, g3_ref, *rest):
        o_ref, x2_ref, h3_ref = rest[-3:]
        o = _dot(a_ref[...], w_ref[0:GM_WIDTH, :]) + _dot(b_ref[...], w_ref[GM_WIDTH:, :])
        o_ref[...] = o
        r2 = lax.rsqrt(jnp.mean(o * o, axis=-1, keepdims=True) + EPS)
        x2 = x_ref[...] + o * r2 * g2_ref[...]
        x2_ref[...] = x2
        r3 = lax.rsqrt(jnp.mean(x2 * x2, axis=-1, keepdims=True) + EPS)
        h3_ref[...] = (x2 * r3 * g3_ref[...]).astype(BF16)

    row = lambda n: pl.BlockSpec((tm, n), lambda i: (i, 0))
    sd = lambda dt: jax.ShapeDtypeStruct((t_tok, D_MODEL), dt)
    return pl.pallas_call(
        body, name="out_proj", grid=(t_tok // tm,), out_shape=(sd(F32), sd(F32), sd(BF16)),
        in_specs=[row(GM_WIDTH), row(SSM_WIDTH), _full((D_MODEL, D_MODEL)), row(D_MODEL), _full((1, D_MODEL)),
                  _full((1, D_MODEL))] + [pl.BlockSpec(memory_space=pl.ANY)] * len(deps),
        out_specs=(row(D_MODEL),) * 3, compiler_params=_params("parallel"))(mix_a, mix_b, w_out, x, g2, g3, *deps)


def _mlp_fwd(h3, w_up, w_down, x2, target, g4, tm, tf):
    t_tok = x2.shape[0]

    def up_body(h_ref, wu_ref, ra_ref):
        ra_ref[...] = jnp.maximum(_dot(h_ref[...], wu_ref[...]), 0.0).astype(BF16)

    tu = min(2 * tm, t_tok)
    ra = pl.pallas_call(
        up_body, name="mlp_up", grid=(D_FF // tf, t_tok // tu), out_shape=jax.ShapeDtypeStruct((t_tok, D_FF), BF16),
        in_specs=[pl.BlockSpec((tu, D_MODEL), lambda j, i: (i, 0)), pl.BlockSpec((D_MODEL, tf), lambda j, i: (0, j))],
        out_specs=pl.BlockSpec((tu, tf), lambda j, i: (i, j)), compiler_params=_params("parallel", "parallel"))(h3, w_up)

    def down_body(ra_ref, wd_ref, x2_ref, t_ref, g4_ref, dd_ref, dy_ref, dg4_ref, loss_ref):
        i = pl.program_id(0)
        rav = ra_ref[...]
        dvec = _dot(rav * rav, wd_ref[...])
        r4 = lax.rsqrt(jnp.mean(dvec * dvec, axis=-1, keepdims=True) + EPS)
        dn = dvec * r4
        g4 = g4_ref[...]
        err = x2_ref[...] + dn * g4 - t_ref[...]
        dy = err * (1.0 / D_MODEL)
        dy_ref[...] = dy
        dg = dy * g4
        dd_ref[...] = (r4 * (dg - dn * jnp.mean(dg * dn, axis=-1, keepdims=True))).astype(BF16)
        _acc_rows(dg4_ref, _rsum(dy * dn), i == 0)
        tile_loss = 0.5 * jnp.sum(jnp.sum(err * err, axis=-1, keepdims=True), axis=0, keepdims=True) / D_MODEL
        _acc_rows(loss_ref, jnp.broadcast_to(tile_loss, (1, 128)), i == 0)

    row = pl.BlockSpec((tm, D_MODEL), lambda i: (i, 0))
    dd, dy, dg4, loss = pl.pallas_call(
        down_body, name="mlp_down", grid=(t_tok // tm,),
        out_shape=(jax.ShapeDtypeStruct((t_tok, D_MODEL), BF16), jax.ShapeDtypeStruct((t_tok, D_MODEL), F32),
                   jax.ShapeDtypeStruct((1, D_MODEL), F32), jax.ShapeDtypeStruct((1, 128), F32)),
        in_specs=[pl.BlockSpec((tm, D_FF), lambda i: (i, 0)), _full((D_FF, D_MODEL)), row, row, _full((1, D_MODEL))],
        out_specs=(row, row, _full((1, D_MODEL)), _full((1, 128))),
        compiler_params=_params("arbitrary"))(ra, w_down, x2, target, g4)
    return ra, dd, dy, dg4, loss


def _mlp_bwd(dd, w_down, ra, w_up, x2, dy, o, g3, g2, tm, tf):
    t_tok = x2.shape[0]

    def hidden_body(dd_ref, wd_ref, ra_ref, da_ref):
        df = _dot(dd_ref[...], wd_ref[...], _NT)
        da_ref[...] = (df * (2.0 * ra_ref[...].astype(F32))).astype(BF16)

    tu = min(2 * tm, t_tok)
    da = pl.pallas_call(
        hidden_body, name="mlp_bwd_hidden", grid=(D_FF // tf, t_tok // tu),
        out_shape=jax.ShapeDtypeStruct((t_tok, D_FF), BF16),
        in_specs=[pl.BlockSpec((tu, D_MODEL), lambda j, i: (i, 0)), pl.BlockSpec((tf, D_MODEL), lambda j, i: (j, 0)),
                  pl.BlockSpec((tu, tf), lambda j, i: (i, j))],
        out_specs=pl.BlockSpec((tu, tf), lambda j, i: (i, j)),
        compiler_params=_params("parallel", "parallel"))(dd, w_down, ra)

    def in_body(da_ref, wu_ref, x2_ref, dy_ref, o_ref, g3_ref, g2_ref, dx2_ref, do_ref, dg3_ref, dg2_ref):
        i = pl.program_id(0)
        dh3 = _dot(da_ref[...], wu_ref[...], _NT)
        dn3, dg3 = _rms_bwd(x2_ref[...], g3_ref[...], dh3)
        dx2 = dy_ref[...] + dn3
        dx2_ref[...] = dx2
        do, dg2 = _rms_bwd(o_ref[...], g2_ref[...], dx2)
        do_ref[...] = do.astype(BF16)
        _acc_rows(dg3_ref, dg3, i == 0)
        _acc_rows(dg2_ref, dg2, i == 0)

    row = pl.BlockSpec((tm, D_MODEL), lambda i: (i, 0))
    vec = _full((1, D_MODEL))
    sd = lambda dt: jax.ShapeDtypeStruct((t_tok, D_MODEL), dt)
    dx2, do, dg3, dg2 = pl.pallas_call(
        in_body, name="mlp_bwd_in", grid=(t_tok // tm,),
        out_shape=(sd(F32), sd(BF16), jax.ShapeDtypeStruct((1, D_MODEL), F32), jax.ShapeDtypeStruct((1, D_MODEL), F32)),
        in_specs=[pl.BlockSpec((tm, D_FF), lambda i: (i, 0)), _full((D_MODEL, D_FF)), row, row, row, vec, vec],
        out_specs=(row, row, vec, vec), compiler_params=_params("arbitrary"))(da, w_up, x2, dy, o, g3, g2)
    return da, dx2, do, dg3, dg2


def _wgrad(a, b, out_blocks, bm, bn, bk, square_a, name, dep=None):
    t_tok, m = a.shape
    n = b.shape[1]
    nk = t_tok // bk

    def body(a_ref, b_ref, *rest):
        o_ref, acc_ref = rest[-2:]
        k = pl.program_id(2)
        av = a_ref[...]
        if square_a:
            av = av * av
        part = _dot(av, b_ref[...], _TN)

        def emit(res):
            if out_blocks is None:
                o_ref[...] = res.astype(BF16)
            else:
                o_ref[0] = res.astype(BF16)

        if nk == 1:
            emit(part)
            return

        @pl.when(k == 0)
        def _():
            acc_ref[...] = part

        @pl.when(k > 0)
        def _():
            acc_ref[...] += part

        @pl.when(k == nk - 1)
        def _():
            emit(acc_ref[...])

    if out_blocks is None:
        out_shape = jax.ShapeDtypeStruct((m, n), BF16)
        out_spec = pl.BlockSpec((bm, bn), lambda i, j, k: (i, j))
    else:
        assert n // out_blocks == bn
        out_shape = jax.ShapeDtypeStruct((out_blocks, m, bn), BF16)
        out_spec = pl.BlockSpec((1, bm, bn), lambda i, j, k: (j, i, 0))
    deps = [] if dep is None else [dep]
    return pl.pallas_call(
        body, name=name, grid=(m // bm, n // bn, nk), out_shape=out_shape,
        in_specs=[pl.BlockSpec((bk, bm), lambda i, j, k: (k, i)), pl.BlockSpec((bk, bn), lambda i, j, k: (k, j))]
        + [pl.BlockSpec(memory_space=pl.ANY)] * len(deps),
        out_specs=out_spec, scratch_shapes=[pltpu.VMEM((bm, bn) if nk > 1 else (8, 128), F32)],
        compiler_params=_params("parallel", "parallel", "arbitrary"))(a, b, *deps)


def _wgrad_in(h1, pieces, bn, name, dep=None):
    t_tok = h1.shape[0]
    widths = [p.shape[1] for p in pieces]
    starts = [sum(widths[:i]) for i in range(len(widths))]

    def body(h_ref, *rest):
        piece_refs = rest[:len(widths)]
        o_ref = rest[-1]
        hv = h_ref[...]
        for a, n, r in zip(starts, widths, piece_refs):
            o_ref[a:a + n, :] = _dot(r[...], hv, _TN).astype(BF16)

    deps = [] if dep is None else [dep]
    return pl.pallas_call(
        body, name=name, grid=(D_MODEL // bn,), out_shape=jax.ShapeDtypeStruct((sum(widths), D_MODEL), BF16),
        in_specs=[pl.BlockSpec((t_tok, bn), lambda j: (0, j))] + [pl.BlockSpec((t_tok, n), lambda j: (0, 0)) for n in widths]
        + [pl.BlockSpec(memory_space=pl.ANY)] * len(deps),
        out_specs=pl.BlockSpec((sum(widths), bn), lambda j: (0, j)),
        compiler_params=_params("parallel"))(h1, *pieces, *deps)


def _dmix(do, w_out, tm, dep=None):
    t_tok = do.shape[0]

    def body(d_ref, w_ref, *rest):
        rest[-1][...] = _dot(d_ref[...], w_ref[...], _NT).astype(BF16)

    row = pl.BlockSpec((tm, D_MODEL), lambda i: (i, 0))
    deps = [] if dep is None else [dep]
    return pl.pallas_call(
        body, name="dmix", grid=(t_tok // tm,), out_shape=jax.ShapeDtypeStruct((t_tok, D_MODEL), BF16),
        in_specs=[row, _full((D_MODEL, D_MODEL))] + [pl.BlockSpec(memory_space=pl.ANY)] * len(deps), out_specs=row,
        compiler_params=_params("parallel"))(do, w_out, *deps)


def _gmlp_bwd(dmix, u, v, lnw, lnb, wcat, wtcat, bias, avg, expand_t):
    t_tok = u.shape[0]
    tm = min(_GMLP_ROWS, t_tok)

    def body(dm_ref, u_ref, v_ref, lnw_ref, lnb_ref, wcat_ref, wtcat_ref, bias_ref, avg_ref, expt_ref, du_ref, dv_ref,
             dw_ref, db_ref, dlnw_ref, dlnb_ref):
        i = pl.program_id(0)
        m_l, m_r = _lane_masks()
        avg = avg_ref[...]
        lnw = lnw_ref[...]
        ug, dug, dvg, rstd, vhat, vn, mixed = _gmlp_common(
            u_ref[...].astype(F32), v_ref[...].astype(F32), lnw, lnb_ref[...], avg, wcat_ref, bias_ref[...], m_l, m_r)
        dya = dm_ref[...].astype(F32)
        du_ref[...] = (dya * mixed * dug).astype(BF16)
        dmixed = dya * ug
        dvn_rows, dws, dbt = [], [None] * N_HEADS, None
        for r in range(tm // CHUNK):
            dvn_cols = []
            for j in range(N_HEADS // 2):
                dmp = dmixed[CHUNK * r:CHUNK * (r + 1), 128 * j:128 * (j + 1)]
                dvn_cols.append(_dot(wtcat_ref[j], _stack_pair(dmp, m_l, m_r)))
                vnp = vn[CHUNK * r:CHUNK * (r + 1), 128 * j:128 * (j + 1)].astype(BF16)
                for i_h, mask in enumerate((m_l, m_r)):
                    part = _dot((dmp * mask).astype(BF16), vnp, _NT)
                    dws[2 * j + i_h] = part if r == 0 else dws[2 * j + i_h] + part
            dvn_rows.append(jnp.concatenate(dvn_cols, axis=1))
            part = _split_dot(dmixed[CHUNK * r:CHUNK * (r + 1), :], expt_ref[...], 2)
            dbt = part if r == 0 else dbt + part
        dvn = jnp.concatenate(dvn_rows, axis=0)
        dvh = dvn * lnw
        dvgel = rstd * (dvh - _head_mean(dvh, avg) - vhat * _head_mean(dvh * vhat, avg))
        dv_ref[...] = (dvgel * dvg).astype(BF16)
        first = i == 0

        @pl.when(first)
        def _():
            for h in range(N_HEADS):
                dw_ref[h] = dws[h]
            db_ref[...] = dbt

        @pl.when(jnp.logical_not(first))
        def _():
            for h in range(N_HEADS):
                dw_ref[h] += dws[h]
            db_ref[...] += dbt

        _acc_rows(dlnw_ref, _rsum(dvn * vhat), first)
        _acc_rows(dlnb_ref, _rsum(dvn), first)

    row = pl.BlockSpec((tm, GM_WIDTH), lambda i: (i, 0))
    consts = [lnw, lnb, wcat, wtcat, bias, avg, expand_t]
    return pl.pallas_call(
        body, name="gmlp_bwd", grid=(t_tok // tm,),
        out_shape=(jax.ShapeDtypeStruct((t_tok, GM_WIDTH), BF16), jax.ShapeDtypeStruct((t_tok, GM_WIDTH), BF16),
                   jax.ShapeDtypeStruct((N_HEADS, CHUNK, CHUNK), F32), jax.ShapeDtypeStruct((CHUNK, CHUNK), F32),
                   jax.ShapeDtypeStruct((1, GM_WIDTH), F32), jax.ShapeDtypeStruct((1, GM_WIDTH), F32)),
        in_specs=[row, row, row] + [_full(a.shape) for a in consts],
        out_specs=(row, row, _full((N_HEADS, CHUNK, CHUNK)), _full((CHUNK, CHUNK)), _full((1, GM_WIDTH)),
                   _full((1, GM_WIDTH))),
        compiler_params=_params("arbitrary"))(dmix, u, v, *consts)


def _ssd_bwd(dmix, z, xbc, pre, dtr, y, states, cw, cb, dtb, alog, dskip_exp, nw, expand, expand_t, tril, triu, seq,
             dep=None):
    t_tok = z.shape[0]
    nb, nc, row, _, states_spec, fold, unfold = _ssd_specs(t_tok, seq, True)
    q = CHUNK

    def one_sequence(s, dm_ref, z_ref, xbc_ref, pre_ref, dtr_ref, y_ref, st_ref, cw_ref, dtb_ref, alog_ref, dsk_ref,
                     nw_ref, exp_ref, expt_ref, tril_ref, triu_ref, dz_ref, dxbc_ref, ddt_ref, dhead_ref, dstate_ref):
        m_l, m_r = _lane_masks()
        expt = expt_ref[...]
        f = _ssd_common(pre_ref[s], dtr_ref[s], dtb_ref[...], alog_ref[...], exp_ref[...], tril_ref[...])
        act, pre, sg = f["act"], f["pre"], f["sg"]
        xs = act[:, :SSM_WIDTH]
        xdt = xs * f["dt_exp"]
        xw = xdt * f["w_end"]
        state = st_ref[s, 0]
        dstate = dstate_ref[s]
        zv, yv, dout, nw = z_ref[s].astype(F32), y_ref[s], dm_ref[s].astype(F32), nw_ref[...]
        sz = jax.nn.sigmoid(zv)
        sl = zv * sz
        yg = yv * sl
        tv = dout * nw
        dyg_parts, ygh_parts = [], []
        for g in range(2):
            ygg = yg[:, 256 * g:256 * (g + 1)]
            rr = lax.rsqrt(jnp.mean(ygg * ygg, axis=-1, keepdims=True) + EPS)
            ygh = ygg * rr
            tg = tv[:, 256 * g:256 * (g + 1)]
            dyg_parts.append(rr * (tg - ygh * jnp.mean(tg * ygh, axis=-1, keepdims=True)))
            ygh_parts.append(ygh)
        dyg = jnp.concatenate(dyg_parts, axis=1)
        dnw = _rsum(dout * jnp.concatenate(ygh_parts, axis=1))
        dy = dyg * sl
        dz_ref[s] = (dyg * yv * (sz * (1.0 + zv * (1.0 - sz)))).astype(BF16)
        ddsk = _rsum(dy * xs)
        dye = dy * f["e"]
        lane = lax.broadcasted_iota(jnp.int32, (q, q), 1)
        sub = lax.broadcasted_iota(jnp.int32, (q, q), 0)
        rs_mat = jnp.zeros((q, q), F32)
        cs_mat = jnp.zeros((q, q), F32)
        dxdt_cols, yoff, dst_in, dxw, d_b, d_c = [], [], [], [], [], []
        for g in range(2):
            bg = act[:, 512 + 128 * g:640 + 128 * g].astype(BF16)
            cg = act[:, 768 + 128 * g:896 + 128 * g].astype(BF16)
            cb_mat = _dot(cg, bg, _NT)
            stg = state[:, 256 * g:256 * (g + 1)].astype(BF16)
            dyeg = dye[:, 256 * g:256 * (g + 1)].astype(BF16)
            yoff.append(_dot(cg, stg))
            dcg = _dot(dyeg, stg, _NT)
            dst_in.append(_dot(cg, dyeg, _TN))
            dcb = jnp.zeros((q, q), F32)
            for pr in range(2):
                h0 = 4 * g + 2 * pr
                gf = [cb_mat * f["decay"][h0], cb_mat * f["decay"][h0 + 1]]
                gcat = jnp.concatenate([gf[0].astype(BF16), gf[1].astype(BF16)], axis=1)
                xst = _stack_pair(xdt[:, 64 * h0:64 * h0 + 128], m_l, m_r)
                dyp = dy[:, 64 * h0:64 * h0 + 128].astype(BF16)
                dgcat = _dot(dyp, xst, _NT)
                dxst = _dot(gcat, dyp, _TN)
                dxdt_cols.append(dxst[:q] * m_l + dxst[q:] * m_r)
                for i in range(2):
                    h = h0 + i
                    dg = dgcat[:, q * i:q * (i + 1)]
                    mm = dg * gf[i]
                    rs_mat = rs_mat + jnp.where(lane == h, jnp.sum(mm, axis=1, keepdims=True), 0.0)
                    cs_mat = cs_mat + jnp.where(sub == h, jnp.sum(mm, axis=0, keepdims=True), 0.0)
                    dcb = dcb + dg * f["decay"][h]
            dcb16 = dcb.astype(BF16)
            dstg = dstate[:, 256 * g:256 * (g + 1)].astype(BF16)
            d_c.append(dcg + _dot(dcb16, bg))
            dxw.append(_dot(bg, dstg))
            d_b.append(_dot(dcb16, cg, _TN) + _dot(xw[:, 256 * g:256 * (g + 1)].astype(BF16), dstg, _NT))
        dxw = jnp.concatenate(dxw, axis=1)
        dxdt = jnp.concatenate(dxdt_cols, axis=1) + dxw * f["w_end"]
        qv = dxw * xw
        end_row = _rsum(qv) + _rsum(dstate * state) * f["cd"]
        x2 = dye * jnp.concatenate(yoff, axis=1) - qv
        row_i = lax.broadcasted_iota(jnp.int32, (q, 1), 0)
        x2 = x2 + jnp.where(row_i == q - 1, end_row, 0.0)
        da_cs = _split_dot(x2, expt, 2) + rs_mat - cs_mat.T
        ddt = _split_dot(dxdt * xs, expt, 2)
        dxs = dsk_ref[...] * dy + dxdt * f["dt_exp"]
        dda = _split_dot_left(triu_ref[...], da_cs, 3)
        ddt = ddt + dda * f["a_row"]
        dalog = _rsum(dda * f["dt"]) * f["a_row"]
        draw = ddt * jax.nn.sigmoid(f["dtp"])
        ddt_ref[s] = draw.astype(BF16)
        dact = jnp.concatenate([dxs] + d_b + d_c, axis=1)
        dpre = dact * (sg * (1.0 + pre * (1.0 - sg)))
        dhead = dhead_ref[s]
        xv = xbc_ref[s]
        shifted = [_shift_rows(dpre, dhead, 3 - k, False) for k in range(3)] + [dpre]
        dxbc = cw_ref[3:4, :] * dpre
        for k in range(3):
            dxbc = dxbc + cw_ref[k:k + 1, :] * shifted[k]
        dxbc_ref[s] = dxbc.astype(BF16)
        dhead_ref[s] = dpre[0:8, :]
        dstate_ref[s] = dstate * f["cd"] + jnp.concatenate(dst_in, axis=1)
        row8 = lax.broadcasted_iota(jnp.int32, (8, 1), 0)
        dcw = jnp.zeros((8, CONV_CH), F32)
        for k in range(4):
            dcw = dcw + jnp.where(row8 == k, _rsum(shifted[k] * xv), 0.0)
        return dcw, _rsum(dpre), _rsum(draw), dalog, _split_dot(ddsk, expt, 3), dnw

    def body(dm_ref, z_ref, xbc_ref, pre_ref, dtr_ref, y_ref, st_ref, cw_ref, cb_ref, dtb_ref, alog_ref, dsk_ref,
             nw_ref, exp_ref, expt_ref, tril_ref, triu_ref, dz_ref, dxbc_ref, ddt_ref, dcw_ref, dcb_ref, ddtb_ref,
             dalog_ref, dd_ref, dnw_ref, dhead_ref, dstate_ref):
        c = pl.program_id(0)
        first = c == 0

        @pl.when(first)
        def _():
            dstate_ref[...] = jnp.zeros_like(dstate_ref)
            dhead_ref[...] = jnp.zeros_like(dhead_ref)

        total = None
        for s in range(nb):
            parts = one_sequence(s, dm_ref, z_ref, xbc_ref, pre_ref, dtr_ref, y_ref, st_ref, cw_ref, dtb_ref, alog_ref,
                                 dsk_ref, nw_ref, exp_ref, expt_ref, tril_ref, triu_ref, dz_ref, dxbc_ref, ddt_ref,
                                 dhead_ref, dstate_ref)
            total = parts if total is None else tuple(a + b for a, b in zip(total, parts))
        dcw = total[0]

        @pl.when(first)
        def _():
            dcw_ref[...] = dcw

        @pl.when(jnp.logical_not(first))
        def _():
            dcw_ref[...] += dcw

        for ref, part in zip((dcb_ref, ddtb_ref, dalog_ref, dd_ref, dnw_ref), total[1:]):
            _acc_rows(ref, part, first)

    consts = [cw, cb, dtb, alog, dskip_exp, nw, expand, expand_t, tril, triu]
    deps = [] if dep is None else [dep]
    n_in = 7 + len(consts)

    def body_skipping_dep(*refs):
        body(*refs[:n_in], *refs[n_in + len(deps):])

    acc = lambda n: jax.ShapeDtypeStruct((1, n), F32)
    sd = lambda n: jax.ShapeDtypeStruct((nb, seq, n), BF16)
    dz, dxbc, ddt, *small_grads = pl.pallas_call(
        body_skipping_dep, name="ssd_bwd", grid=(nc,),
        out_shape=(sd(SSM_WIDTH), sd(CONV_CH), sd(CHUNK), jax.ShapeDtypeStruct((8, CONV_CH), F32), acc(CONV_CH),
                   acc(CHUNK), acc(CHUNK), acc(CHUNK), acc(SSM_WIDTH)),
        in_specs=[row(SSM_WIDTH, col=1), row(SSM_WIDTH), row(CONV_CH), row(CONV_CH), row(CHUNK), row(SSM_WIDTH),
                  states_spec]
        + [_full(a.shape) for a in consts] + [pl.BlockSpec(memory_space=pl.ANY)] * len(deps),
        out_specs=(row(SSM_WIDTH), row(CONV_CH), row(CHUNK), _full((8, CONV_CH)), _full((1, CONV_CH)),
                   _full((1, CHUNK)), _full((1, CHUNK)), _full((1, CHUNK)), _full((1, SSM_WIDTH))),
        scratch_shapes=[pltpu.VMEM((nb, 8, CONV_CH), F32), pltpu.VMEM((nb, N_STATE, SSM_WIDTH), F32)],
        compiler_params=_params("arbitrary"))(
            fold(dmix), fold(z), fold(xbc), fold(pre), fold(dtr), fold(y), states, *consts, *deps)
    return (unfold(dz), unfold(dxbc), unfold(ddt), *small_grads)


def _in_bwd(du, dv, dz, dxbc, ddt, w_in, x, dx2, g1, tm, me, riders=(), dep=None):
    t_tok = x.shape[0]
    steps = t_tok // tm

    n_in = [5 + ("mask" in rd) for rd in riders]
    first_in = [sum(n_in[:r]) for r in range(len(riders))]

    def body(me_ref, du_ref, dv_ref, dz_ref, dxbc_ref, ddt_ref, w_ref, x_ref, dx2_ref, g_ref, *rest):
        outs = rest[len(rest) - 2 - 4 * len(riders):]
        gx_ref, dg_ref = outs[:2]
        i = pl.program_id(0)
        dh = None
        for (a, b), ref in zip(_IN_SPLITS, (du_ref, dv_ref, dz_ref, dxbc_ref, ddt_ref)):
            part = _dot(ref[...], w_ref[a:b, :])
            dh = part if dh is None else dh + part
        dn, dg = _rms_bwd(x_ref[...], g_ref[...], dh)
        gx_ref[...] = dx2_ref[...] + dn
        _acc_rows(dg_ref, dg, i == 0)
        for r in range(len(riders)):
            p_ref, own_ref, w_ref_r, m_ref_r, v_ref_r = rest[first_in[r]:first_in[r] + 5]
            g = _sum_parts(me_ref[0], p_ref, own_ref[0])
            if n_in[r] == 6:
                g = g * rest[first_in[r] + 5][...]
            d, mn, vn = _adamw_math(w_ref_r[...], g, m_ref_r[...], v_ref_r[...])
            for o_ref, val in zip(outs[2 + 4 * r:6 + 4 * r], (g, d, mn, vn)):
                o_ref[...] = val

    row = lambda n: pl.BlockSpec((tm, n), lambda i, me_ref: (i, 0))
    whole = lambda shape: pl.BlockSpec(shape, lambda i, me_ref: (0,) * len(shape))
    widths = [b - a for a, b in _IN_SPLITS]
    deps = [] if dep is None else [dep]
    rider_args, rider_specs, rider_out_shapes, rider_out_specs = [], [], [], []
    for rd in riders:
        rows, cols = rd["w"].shape[0] // steps, rd["w"].shape[1]
        blk = pl.BlockSpec((rows, cols), lambda i, me_ref: (i, 0))
        rider_args += [rd["parts"], rd["own"], rd["w"], rd["m"], rd["v"]]
        rider_specs += [pl.BlockSpec((N_DEV, rows, cols), lambda i, me_ref: (0, i, 0)),
                        pl.BlockSpec((1, rows, cols), lambda i, me_ref: (me_ref[0], i, 0)), blk, blk, blk]
        if "mask" in rd:
            rider_args.append(rd["mask"])
            rider_specs.append(whole((rows, cols)))
        rider_out_shapes += [jax.ShapeDtypeStruct(rd["w"].shape, F32)] * 4
        rider_out_specs += [blk] * 4
    outs = pl.pallas_call(
        body, name="in_bwd",
        out_shape=(jax.ShapeDtypeStruct((t_tok, D_MODEL), F32), jax.ShapeDtypeStruct((1, D_MODEL), F32),
                   *rider_out_shapes),
        grid_spec=pltpu.PrefetchScalarGridSpec(
            num_scalar_prefetch=1, grid=(steps,),
            in_specs=[row(n) for n in widths] + [whole((IN_PAD, D_MODEL)), row(D_MODEL), row(D_MODEL),
                                                 whole((1, D_MODEL))] + rider_specs
            + [pl.BlockSpec(memory_space=pl.ANY)] * len(deps),
            out_specs=(row(D_MODEL), whole((1, D_MODEL)), *rider_out_specs)),
        compiler_params=_params("arbitrary"))(me, du, dv, dz, dxbc, ddt, w_in, x, dx2, g1, *rider_args, *deps)
    return outs[0], outs[1], [tuple(outs[2 + 4 * r:6 + 4 * r]) for r in range(len(riders))]


def _pad_lanes(a, n):
    return jnp.pad(a, ((0, 0), (0, n - a.shape[1])))


def _local_step(x, target, seq, small, hooks, first_dep=None):
    t_tok = x.shape[0]
    tm = min(TOKEN_TILE, t_tok)
    avg, expand, expand_t, tril, triu = _const_mats()
    g1, g2, g3, g4 = (small[k].reshape(1, D_MODEL) for k in
                      ("norm_mix_pre", "norm_mix_post", "norm_ffn_pre", "norm_ffn_post"))
    tie = (lambda a: a) if first_dep is None else (lambda a: a + first_dep[0, 0])
    lnw = tie(small["gm_ln_w"]).reshape(1, GM_WIDTH)
    lnb = tie(small["gm_ln_b"]).reshape(1, GM_WIDTH)
    causal = jnp.tril(jnp.ones((CHUNK, CHUNK), F32))
    wm = tie(small["gm_w_s"]) * causal
    pair = lambda w: w.reshape(4, 2, CHUNK, CHUNK).transpose(0, 2, 1, 3).reshape(4, CHUNK, 2 * CHUNK).astype(BF16)
    wcat = pair(wm)
    wtcat = pair(jnp.swapaxes(wm, 1, 2))
    bias = jnp.repeat(tie(small["gm_b_s"]).T, HEAD_DIM, axis=1)
    cb = small["conv_b"].reshape(1, CONV_CH)
    dtb = _pad_lanes(tie(small["dt_bias"]).reshape(1, N_HEADS), CHUNK)
    alog = _pad_lanes(tie(small["a_log"]).reshape(1, N_HEADS), CHUNK)
    dskip_exp = jnp.repeat(tie(small["d_skip"]).reshape(1, N_HEADS), HEAD_DIM, axis=1)
    nw = small["ssm_norm_w"].reshape(1, SSM_WIDTH)

    h1 = _prenorm(x, g1, tm, hooks.get("prenorm_after", first_dep))
    w_in_t, conv_w = hooks["mixer_weights"](h1)
    u, v, z, xbc, dtr = _in_proj(h1, w_in_t, tm)
    mix_a = _gmlp_fwd(u, v, lnw, lnb, wcat, bias, avg)
    mix_b, y_pre, states, pre = _ssd_fwd(z, xbc, dtr, conv_w, cb, dtb, alog, dskip_exp, nw, expand, tril, seq)
    w_out, dep = hooks["mixers_done"](mix_b)
    o, x2, h3 = _out_proj(mix_a, mix_b, w_out, x, g2, g3, tm, dep)
    w_up, w_down = hooks["mlp_weights"](h3)
    tf = FF_TILE
    ra, dd, dy, dg4, loss = _mlp_fwd(h3, w_up, w_down, x2, target, g4, tm, tf)

    da, dx2, do, dg3, dg2 = _mlp_bwd(dd, w_down, ra, w_up, x2, dy, o, g3, g2, tm, tf)
    g_w_down = _wgrad(ra, dd, None, WGRAD_TILE, D_MODEL, t_tok, True, "wgrad_down")
    g_w_up = _wgrad(h3, da, N_DEV, D_MODEL, D_FF // N_DEV, t_tok, False, "wgrad_up")
    dep = hooks["mlp_grads"](g_w_down, g_w_up)
    dmix = _dmix(do, w_out, tm, dep)
    g_w_out = _wgrad_in(do, (mix_a, mix_b), WGRAD_TILE, "wgrad_out", dep)
    du, dv, dws, dbt, dlnw, dlnb = _gmlp_bwd(dmix, u, v, lnw, lnb, wcat, wtcat, bias, avg, expand_t)
    dep = hooks["gmlp_grads"](g_w_out, dws)
    dz, dxbc, ddt, dcw, dcb, ddtb, dalog, ddsk, dnw = _ssd_bwd(
        dmix, z, xbc, pre, dtr, y_pre, states, conv_w, cb, dtb, alog, dskip_exp, nw, expand, expand_t, tril, triu, seq,
        dep)
    g_w_in = jnp.concatenate([_wgrad_in(h1, (du, dv, dz), WGRAD_TILE, "wgrad_in_a", dep),
                              _wgrad_in(h1, (dxbc, ddt), WGRAD_TILE, "wgrad_in_b", dep)], axis=0)
    dep = hooks["in_grads"](g_w_in, dcw[0:4])
    riders = hooks["arrived_updates"](dep) if "arrived_updates" in hooks else []
    me = hooks.get("me", jnp.zeros((1,), jnp.int32))
    grad_x, dg1, updates = _in_bwd(du, dv, dz, dxbc, ddt, w_in_t, x, dx2, g1, tm, me, riders, dep)

    grads = dict(
        updates=updates,
        w_in=g_w_in, w_out=g_w_out, w_up=g_w_up, w_down=g_w_down, conv_w=dcw[0:4],
        norm_mix_pre=dg1, norm_mix_post=dg2, norm_ffn_pre=dg3, norm_ffn_post=dg4, gm_ln_w=dlnw, gm_ln_b=dlnb,
        gm_w_s=dws, gm_b_s=dbt, conv_b=dcb, dt_bias=ddtb, a_log=dalog, d_skip=ddsk, ssm_norm_w=dnw)
    return loss[0, 0], grad_x, grads


_WEIGHTS = ("norm_mix_pre", "w_in", "gm_ln_w", "gm_ln_b", "gm_w_s", "gm_b_s", "conv_w", "conv_b", "dt_bias", "a_log",
            "d_skip", "ssm_norm_w", "w_out", "norm_mix_post", "norm_ffn_pre", "w_up", "w_down", "norm_ffn_post")
_SLAB_ROWS = (("norm_mix_pre", 1024), ("norm_mix_post", 1024), ("norm_ffn_pre", 1024), ("norm_ffn_post", 1024),
              ("conv_b", 1024), ("ssm_norm_w", 512), ("gm_ln_w", 512), ("gm_ln_b", 512), ("dt_bias", 8), ("a_log", 8),
              ("d_skip", 8))
_SLAB_LOSS_ROW = len(_SLAB_ROWS)
_SLAB_BS_ROW = 16
_SMALL_PARAMS = tuple(name for name, _ in _SLAB_ROWS) + ("gm_b_s",)
_LN_PARAMS = ("gm_ln_w", "gm_ln_b")


def _pack_slab(g, loss_part):
    rows = [_pad_lanes(g[name], D_MODEL) for name, _ in _SLAB_ROWS]
    rows.append(jnp.broadcast_to(loss_part, (1, D_MODEL)))
    rows.append(jnp.zeros((_SLAB_BS_ROW - len(rows), D_MODEL), F32))
    rows.append(_pad_lanes(g["gm_b_s"].T[0:N_HEADS], D_MODEL))
    return jnp.concatenate(rows, axis=0)


def _adamw_slab(parts, w, m, v):
    names = _SMALL_PARAMS
    shapes = [w[k].shape for k in names]
    unfold = np.zeros((GM_WIDTH, HEAD_DIM), np.float32)
    for h in range(N_HEADS):
        unfold[h * HEAD_DIM:(h + 1) * HEAD_DIM, :] = np.eye(HEAD_DIM)
    unfold = jnp.asarray(unfold, dtype=BF16)
    n = len(names)

    def body(p_ref, unfold_ref, *refs):
        w_refs, m_refs, v_refs = refs[:n], refs[n:2 * n], refs[2 * n:3 * n]
        outs = refs[3 * n:]
        g_all = p_ref[0]
        for j in range(1, N_DEV):
            g_all = g_all + p_ref[j]
        lane = lax.broadcasted_iota(jnp.int32, (N_HEADS, GM_WIDTH), 1)
        head = lax.broadcasted_iota(jnp.int32, (N_HEADS, GM_WIDTH), 0)
        own_lanes = jnp.logical_and(lane >= head * HEAD_DIM, lane < (head + 1) * HEAD_DIM)
        for i, name in enumerate(names):
            if name == "gm_b_s":
                g = g_all[_SLAB_BS_ROW:_SLAB_BS_ROW + N_HEADS, 0:CHUNK]
            else:
                row = [r for r, (k, _) in enumerate(_SLAB_ROWS) if k == name][0]
                g = g_all[row:row + 1, 0:dict(_SLAB_ROWS)[name]]
                if name in _LN_PARAMS:
                    g = _split_dot(jnp.where(own_lanes, g, 0.0), unfold_ref[...], 3)
            d, mn, vn = _adamw_math(w_refs[i][...], g, m_refs[i][...], v_refs[i][...])
            for o_ref, val in zip(outs[4 * i:4 * i + 4], (g, d, mn, vn)):
                o_ref[...] = val
        outs[-1][...] = g_all[_SLAB_LOSS_ROW:_SLAB_LOSS_ROW + 1, 0:128]

    ins = [parts, unfold] + [d[k] for d in (w, m, v) for k in names]
    out_shape = tuple(jax.ShapeDtypeStruct(s, F32) for s in shapes for _ in range(4)) + (
        jax.ShapeDtypeStruct((1, 128), F32),)
    outs = pl.pallas_call(
        body, name="adamw_small", out_shape=out_shape, grid=(1,), in_specs=[_full(a.shape) for a in ins],
        out_specs=tuple(_full(s.shape) for s in out_shape), compiler_params=_params("arbitrary"))(*ins)
    return {k: tuple(outs[4 * i:4 * i + 4]) for i, k in enumerate(names)}, outs[-1][0, 0]


def kernel(x, norm_mix_pre, w_in, gm_ln_w, gm_ln_b, gm_w_s, gm_b_s, conv_w, conv_b, dt_bias, a_log, d_skip, ssm_norm_w, w_out, norm_mix_post, norm_ffn_pre, w_up, w_down, norm_ffn_post, loss_target, m_norm_mix_pre, m_w_in, m_gm_ln_w, m_gm_ln_b, m_gm_w_s, m_gm_b_s, m_conv_w, m_conv_b, m_dt_bias, m_a_log, m_d_skip, m_ssm_norm_w, m_w_out, m_norm_mix_post, m_norm_ffn_pre, m_w_up, m_w_down, m_norm_ffn_post, v_norm_mix_pre, v_w_in, v_gm_ln_w, v_gm_ln_b, v_gm_w_s, v_gm_b_s, v_conv_w, v_conv_b, v_dt_bias, v_a_log, v_d_skip, v_ssm_norm_w, v_w_out, v_norm_mix_post, v_norm_ffn_pre, v_w_up, v_w_down, v_norm_ffn_post):
    w = dict(norm_mix_pre=norm_mix_pre, w_in=w_in, gm_ln_w=gm_ln_w, gm_ln_b=gm_ln_b, gm_w_s=gm_w_s, gm_b_s=gm_b_s, conv_w=conv_w, conv_b=conv_b, dt_bias=dt_bias, a_log=a_log, d_skip=d_skip, ssm_norm_w=ssm_norm_w, w_out=w_out, norm_mix_post=norm_mix_post, norm_ffn_pre=norm_ffn_pre, w_up=w_up, w_down=w_down, norm_ffn_post=norm_ffn_post)
    m = dict(norm_mix_pre=m_norm_mix_pre, w_in=m_w_in, gm_ln_w=m_gm_ln_w, gm_ln_b=m_gm_ln_b, gm_w_s=m_gm_w_s, gm_b_s=m_gm_b_s, conv_w=m_conv_w, conv_b=m_conv_b, dt_bias=m_dt_bias, a_log=m_a_log, d_skip=m_d_skip, ssm_norm_w=m_ssm_norm_w, w_out=m_w_out, norm_mix_post=m_norm_mix_post, norm_ffn_pre=m_norm_ffn_pre, w_up=m_w_up, w_down=m_w_down, norm_ffn_post=m_norm_ffn_post)
    v = dict(norm_mix_pre=v_norm_mix_pre, w_in=v_w_in, gm_ln_w=v_gm_ln_w, gm_ln_b=v_gm_ln_b, gm_w_s=v_gm_w_s, gm_b_s=v_gm_b_s, conv_w=v_conv_w, conv_b=v_conv_b, dt_bias=v_dt_bias, a_log=v_a_log, d_skip=v_d_skip, ssm_norm_w=v_ssm_norm_w, w_out=v_w_out, norm_mix_post=v_norm_mix_post, norm_ffn_pre=v_norm_ffn_pre, w_up=v_w_up, w_down=v_w_down, norm_ffn_post=v_norm_ffn_post)
    n_batch, seq, _ = x.shape
    shard_in = IN_COLS // N_DEV

    me = (4 * lax.axis_index("x") + 2 * lax.axis_index("y") + lax.axis_index("c")).astype(jnp.int32).reshape(1)

    def in_slot(own):
        return lax.dynamic_update_slice(lax.empty((N_DEV,) + own.shape, own.dtype), own[None],
                                        (me[0],) + (0,) * own.ndim)

    w_in_sh, m_in_sh, v_in_sh = w_in[0].T, m_w_in[0].T, v_w_in[0].T
    first = [_cast_to_slot(w_in_sh, me, shard_in, "cast_w_in"), in_slot(conv_w[0]),
             _cast_to_slot(w_out[0], me, 128, "cast_w_out")]
    ici_1, tok_ici_1 = _exchange_start(first, [True] * 3, _SAME_CORE_PEERS, "gather_mix_ici_start")
    cast_up = _cast_to_slot(w_up[0], me, 1024, "cast_w_up", cols=True, dep=tok_ici_1)
    second = [cast_up, _cast_to_slot(w_down[0], me, 512, "cast_w_down", dep=cast_up)]
    gathering = {}

    def mixer_weights(after):
        bufs = [buf for buf, _ in _exchange_wait(ici_1, after, "gather_mix_ici_wait")]
        d2d_1, tok_d2d_1 = _exchange_start(bufs, [True] * 3, _SIBLING_FORWARD, "gather_mix_d2d_start")
        gathering["mlp_ici"], tok_ici_2 = _exchange_start(
            second, [True] * 2, _SAME_CORE_PEERS, "gather_mlp_ici_start", dep=tok_d2d_1)
        (_, ag_in), (_, ag_conv), (_, ag_out) = _exchange_wait(d2d_1, tok_ici_2, "gather_mix_d2d_wait")
        gathering["w_out"] = ag_out.reshape(D_MODEL, D_MODEL)
        w_in_t = jnp.pad(ag_in.reshape(IN_COLS, D_MODEL), ((0, IN_PAD - IN_COLS), (0, 0)))
        return w_in_t, ag_conv.transpose(1, 0, 2).reshape(4, CONV_CH)

    def mixers_done(after):
        bufs = [buf for buf, _ in _exchange_wait(gathering["mlp_ici"], after, "gather_mlp_ici_wait")]
        gathering["mlp"], tok = _exchange_start(bufs, [True] * 2, _SIBLING_FORWARD, "gather_mlp_d2d_start")
        return gathering["w_out"], tok

    def mlp_weights(after):
        (_, ag_up), (_, ag_down) = _exchange_wait(gathering["mlp"], after, "gather_mlp_d2d_wait")
        return ag_up, ag_down.reshape(D_FF, D_MODEL)

    sent = {}

    def mlp_grads(g_w_down, g_w_up):
        sent["mlp"], tok = _exchange_start(
            [g_w_down.reshape(N_DEV, D_FF // N_DEV, D_MODEL), g_w_up], [False, False], _ALL_PEERS, "grads_mlp_start")
        return tok

    def gmlp_grads(g_w_out, g_w_s):
        sent["gmlp"], tok = _exchange_start(
            [g_w_out.reshape(N_DEV, D_MODEL // N_DEV, D_MODEL), in_slot(g_w_s.astype(BF16))], [False, True], _ALL_PEERS,
            "grads_gmlp_start")
        return tok

    def in_grads(g_w_in_t, g_conv_w):
        g_in_blk = g_w_in_t[:IN_COLS].reshape(N_DEV, shard_in, D_MODEL)
        g_conv_blk = g_conv_w.reshape(4, N_DEV, CONV_CH // N_DEV).transpose(1, 0, 2)
        sent["in"], tok = _exchange_start([g_in_blk, g_conv_blk], [False, False], _ALL_PEERS, "grads_in_start")
        return tok

    def arrived_updates(after):
        (own_down, p_down), (own_up, p_up) = _exchange_wait(sent["mlp"], after, "grads_mlp_wait")
        (own_out, p_out), (_, p_ws) = _exchange_wait(sent["gmlp"], own_up, "grads_gmlp_wait")
        rows = lambda t: t.reshape(t.shape[:-3] + (N_HEADS * CHUNK, CHUNK))
        return [dict(parts=p_up, own=own_up, w=w_up[0], m=m_w_up[0], v=v_w_up[0]),
                dict(parts=p_down, own=own_down, w=w_down[0], m=m_w_down[0], v=v_w_down[0]),
                dict(parts=p_out, own=own_out, w=w_out[0], m=m_w_out[0], v=v_w_out[0]),
                dict(parts=rows(p_ws), own=rows(p_ws), w=rows(gm_w_s[0]), m=rows(m_gm_w_s[0]), v=rows(v_gm_w_s[0]),
                     mask=jnp.tril(jnp.ones((CHUNK, CHUNK), F32)))]

    small = {k: w[k][0] for k in _SMALL_PARAMS + ("gm_w_s",)}
    loss_part, grad_x, g = _local_step(
        x.reshape(n_batch * seq, D_MODEL), loss_target.reshape(n_batch * seq, D_MODEL), seq, small,
        dict(mixer_weights=mixer_weights, mixers_done=mixers_done, mlp_weights=mlp_weights, mlp_grads=mlp_grads,
             gmlp_grads=gmlp_grads, in_grads=in_grads, arrived_updates=arrived_updates, me=me,
             prenorm_after=second[1]), first_dep=tok_ici_1)

    sent_rows, tok_rows = _exchange_start([in_slot(_pack_slab(g, loss_part))], [True], _ALL_PEERS, "grads_rows_start")
    res = dict(zip(("w_up", "w_down", "w_out", "gm_w_s"), g["updates"]))
    (own_in, p_in), (own_conv, p_conv) = _exchange_wait(sent["in"], tok_rows, "grads_in_wait")
    res["w_in"] = tuple(r.T for r in _adamw_reduce(p_in, own_in, me, w_in_sh, m_in_sh, v_in_sh, shard_in, "adamw_w_in"))
    res["conv_w"] = _adamw_small(p_conv, own_conv, me, conv_w[0], m_conv_w[0], v_conv_w[0], None, "adamw_conv_w")
    ((_, p_rows),) = _exchange_wait(sent_rows, res["w_in"][1], "grads_rows_wait")
    flat = lambda t: t[0] if t.ndim == 3 else t
    small_res, loss = _adamw_slab(p_rows, *({k: flat(d[k]) for k in _SMALL_PARAMS} for d in (w, m, v)))
    res.update(small_res)
    res = {k: tuple(r.reshape(w[k].shape) for r in res[k]) for k in _WEIGHTS}

    outs = [loss, grad_x.reshape(x.shape)]
    for part in range(4):
        outs.extend(res[k][part] for k in _WEIGHTS)
    return tuple(outs)
```

```python
import functools

import jax
import jax.numpy as jnp
import numpy as np
from jax import lax
from jax.experimental import pallas as pl
from jax.experimental.pallas import tpu as pltpu

F32 = jnp.float32
BF16 = jnp.bfloat16

D_MODEL = 1024
GM_WIDTH = 512
SSM_WIDTH = 512
CONV_CH = 1024
N_HEADS = 8
HEAD_DIM = 64
N_STATE = 128
CHUNK = 128
D_FF = 4096
IN_COLS = 2568
IN_PAD = 2688
N_DEV = 8
EPS = 1e-6
ADAM_LR, ADAM_B1, ADAM_B2, ADAM_EPS, ADAM_WD, ADAM_STEP = 0.001, 0.9, 0.999, 1e-08, 0.01, 10
VMEM_LIMIT_BYTES = 56 * 1024 * 1024
TOKEN_TILE = 512
FF_TILE = 2048
WGRAD_TILE = 512

_NT = (((1,), (1,)), ((), ()))
_TN = (((0,), (0,)), ((), ()))


def _params(*sem):
    return pltpu.CompilerParams(dimension_semantics=sem or None, vmem_limit_bytes=VMEM_LIMIT_BYTES)


def _dot(a, b, dims=None):
    if dims is None:
        return jnp.dot(a, b, preferred_element_type=F32)
    return lax.dot_general(a, b, dims, preferred_element_type=F32)


def _split_terms(x, terms):
    out, rem = [], x
    for i in range(terms):
        hi = rem.astype(BF16)
        out.append(hi)
        if i + 1 < terms:
            rem = rem - hi.astype(F32)
    return out


def _split_dot(x, m, terms):
    acc = None
    for hi in _split_terms(x, terms):
        part = _dot(hi, m)
        acc = part if acc is None else acc + part
    return acc


def _split_dot_left(m, x, terms):
    acc = None
    for hi in _split_terms(x, terms):
        part = _dot(m, hi)
        acc = part if acc is None else acc + part
    return acc


def _gelu_and_grad(x):
    c = 0.7978845608028654
    inner = c * (x + 0.044715 * x * x * x)
    t = jnp.tanh(inner)
    g = 0.5 * x * (1.0 + t)
    dg = 0.5 * (1.0 + t) + 0.5 * x * (1.0 - t * t) * c * (1.0 + 3.0 * 0.044715 * x * x)
    return g, dg


def _softplus(x):
    return jnp.maximum(x, 0.0) + jnp.log(1.0 + jnp.exp(-jnp.abs(x)))


def _rsum(x):
    return jnp.sum(x, axis=0, keepdims=True)


def _acc_rows(ref, part, first):
    val = jnp.broadcast_to(part, ref.shape)

    @pl.when(first)
    def _():
        ref[...] = val

    @pl.when(jnp.logical_not(first))
    def _():
        ref[...] += val


def _rms_bwd(n, g, dout):
    r = lax.rsqrt(jnp.mean(n * n, axis=-1, keepdims=True) + EPS)
    nh = n * r
    dg = dout * g
    dn = r * (dg - nh * jnp.mean(dg * nh, axis=-1, keepdims=True))
    return dn, _rsum(dout * nh)


def _const_mats():
    avg = np.kron(np.eye(4), np.full((HEAD_DIM, HEAD_DIM), 1.0 / HEAD_DIM))
    expand = np.zeros((CHUNK, SSM_WIDTH), np.float32)
    for h in range(N_HEADS):
        expand[h, h * HEAD_DIM:(h + 1) * HEAD_DIM] = 1.0
    tril = np.tril(np.ones((CHUNK, CHUNK), np.float32))
    as_bf16 = lambda a: jnp.asarray(a, dtype=BF16)
    return as_bf16(avg), as_bf16(expand), as_bf16(expand.T), as_bf16(tril), as_bf16(tril.T)


def _full(shape):
    nd = len(shape)
    return pl.BlockSpec(shape, lambda *_: (0,) * nd)


_HBM = pl.BlockSpec(memory_space=pltpu.HBM)
_SEM = pl.BlockSpec(memory_space=pltpu.SEMAPHORE)
_ALL_PEERS = tuple((k, 0) for k in range(1, N_DEV))
_SAME_CORE_PEERS = ((2, 0), (4, 0), (6, 0))
_SIBLING_FORWARD = ((1, 0), (1, 2), (1, 4), (1, 6))


def _flip(j, k):
    for bit in (4, 2, 1):
        if k & bit:
            j = j + bit - 2 * (j & bit)
    return j


def _copies(src, land, send_sems, recv_sems, hops):
    x, y, c = lax.axis_index("x"), lax.axis_index("y"), lax.axis_index("c")
    me = 4 * x + 2 * y + c
    out = []
    for t in range(len(src)):
        for i, (k, b) in enumerate(hops):
            pos = (1 - x if k & 4 else x, 1 - y if k & 2 else y, 1 - c if k & 1 else c)
            peer = _flip(me, k)
            sem = t * len(hops) + i
            mk = functools.partial(pltpu.make_async_remote_copy, send_sem=send_sems.at[sem], recv_sem=recv_sems.at[sem],
                                   device_id=pos, device_id_type=pl.DeviceIdType.MESH)
            if land[t] is None and src[t].shape[0] != N_DEV:
                width = src[t].shape[1] // N_DEV
                slab = lambda j: src[t].at[:, pl.ds(pl.multiple_of(j * width, 128), width)]
                mine = functools.partial(mk, src_ref=slab(_flip(me, b)), dst_ref=slab(_flip(me, b)))
                theirs = functools.partial(mk, src_ref=slab(_flip(peer, b)), dst_ref=slab(_flip(peer, b)))
            elif land[t] is None:
                mine = functools.partial(mk, src_ref=src[t].at[_flip(me, b)], dst_ref=src[t].at[_flip(me, b)])
                theirs = functools.partial(mk, src_ref=src[t].at[_flip(peer, b)], dst_ref=src[t].at[_flip(peer, b)])
            else:
                assert b == 0
                mine = functools.partial(mk, src_ref=src[t].at[peer], dst_ref=land[t].at[me])
                theirs = functools.partial(mk, src_ref=src[t].at[peer], dst_ref=land[t].at[peer])
            out.append((mine, theirs))
    return out


def _exchange_start(srcs, inplace, peers, name, dep=None):
    n = len(srcs)
    lands = [None if ip else pltpu.with_memory_space_constraint(lax.empty(s.shape, s.dtype), pltpu.HBM)
             for s, ip in zip(srcs, inplace)]
    real_lands = [l for l in lands if l is not None]
    n_l = len(real_lands)
    deps = [] if dep is None else [dep]

    def body(*refs):
        src = refs[:n]
        land_refs = list(refs[n:n + n_l])
        send_sems, recv_sems = refs[n + n_l + len(deps)], refs[n + n_l + len(deps) + 1]
        token = refs[-1]
        land = [None if ip else land_refs.pop(0) for ip in inplace]
        for mine, _ in _copies(src, land, send_sems, recv_sems, peers):
            mine().start()
        token[...] = jnp.zeros_like(token)

    sem_t = pltpu.SemaphoreType.DMA((n * len(peers),))
    outs = pl.pallas_call(
        body, name=name,
        out_shape=(sem_t, sem_t) + tuple(pltpu.HBM(a.shape, a.dtype) for a in list(srcs) + real_lands)
        + (jax.ShapeDtypeStruct((8, 128), F32),),
        in_specs=[_HBM] * (n + n_l) + [pl.BlockSpec(memory_space=pl.ANY)] * len(deps),
        out_specs=(_SEM, _SEM) + (_HBM,) * (n + n_l) + (pl.BlockSpec(memory_space=pltpu.VMEM),),
        input_output_aliases={i: 2 + i for i in range(n + n_l)},
        compiler_params=pltpu.CompilerParams(has_side_effects=pltpu.SideEffectType.DATAFLOW_SIDE_EFFECTING),
    )(*[pltpu.with_memory_space_constraint(s, pltpu.HBM) for s in srcs], *real_lands, *deps)
    handle = dict(send=outs[0], recv=outs[1], srcs=outs[2:2 + n], lands=outs[2 + n:2 + n + n_l], inplace=inplace,
                  peers=peers)
    return handle, outs[-1]


def _exchange_wait(handle, after, name):
    srcs, lands, inplace, peers = handle["srcs"], handle["lands"], handle["inplace"], handle["peers"]
    n, n_l = len(srcs), len(lands)

    def body(*refs):
        src = refs[:n]
        land_refs = list(refs[n:n + n_l])
        send_sems, recv_sems = refs[n + n_l], refs[n + n_l + 1]
        land = [None if ip else land_refs.pop(0) for ip in inplace]
        for mine, theirs in _copies(src, land, send_sems, recv_sems, peers):
            mine().wait_send()
            theirs().wait_recv()

    outs = pl.pallas_call(
        body, name=name, out_shape=tuple(pltpu.HBM(a.shape, a.dtype) for a in list(srcs) + list(lands)),
        in_specs=[_HBM] * (n + n_l) + [_SEM, _SEM, pl.BlockSpec(memory_space=pl.ANY)],
        out_specs=(_HBM,) * (n + n_l), input_output_aliases={i: i for i in range(n + n_l)},
        compiler_params=pltpu.CompilerParams(has_side_effects=pltpu.SideEffectType.DATAFLOW_SIDE_EFFECTING),
    )(*srcs, *lands, handle["send"], handle["recv"], after)
    res, land_out = [], list(outs[n:])
    for t in range(n):
        res.append((outs[t], outs[t] if inplace[t] else land_out.pop(0)))
    return res


def _cast_to_slot(w, me, rows, name, cols=False, dep=None):
    r, cdim = w.shape
    deps = [] if dep is None else [dep]

    def body(me_ref, w_ref, *rest):
        o_ref = rest[-1]
        if cols:
            o_ref[...] = w_ref[...].astype(BF16)
        else:
            o_ref[0] = w_ref[...].astype(BF16)

    if cols:
        out_shape = jax.ShapeDtypeStruct((r, N_DEV * cdim), BF16)
        out_spec = pl.BlockSpec((rows, cdim), lambda i, me_ref: (i, me_ref[0]))
    else:
        out_shape = jax.ShapeDtypeStruct((N_DEV, r, cdim), BF16)
        out_spec = pl.BlockSpec((1, rows, cdim), lambda i, me_ref: (me_ref[0], i, 0))
    return pl.pallas_call(
        body, name=name, out_shape=out_shape,
        grid_spec=pltpu.PrefetchScalarGridSpec(
            num_scalar_prefetch=1, grid=(r // rows,),
            in_specs=[pl.BlockSpec((rows, cdim), lambda i, me_ref: (i, 0))]
            + [pl.BlockSpec(memory_space=pl.ANY)] * len(deps), out_specs=out_spec),
        compiler_params=_params("parallel"))(me, w, *deps)


def _adamw_math(w, g, m, v):
    m = ADAM_B1 * m + (1.0 - ADAM_B1) * g
    v = ADAM_B2 * v + (1.0 - ADAM_B2) * (g * g)
    m_hat = m / (1.0 - ADAM_B1 ** ADAM_STEP)
    v_hat = v / (1.0 - ADAM_B2 ** ADAM_STEP)
    delta = -ADAM_LR * (m_hat / (jnp.sqrt(v_hat) + ADAM_EPS) + ADAM_WD * w)
    return delta, m, v


def _sum_parts(me, p_ref, own):
    g = None
    for j in range(N_DEV):
        term = (p_ref[j] if own is None else jnp.where(me == j, own, p_ref[j])).astype(F32)
        g = term if g is None else g + term
    return g


def _adamw_reduce(parts, own, me, w, m, v, rows, name):
    r, cdim = w.shape

    def body(me_ref, p_ref, own_ref, w_ref, m_ref, v_ref, g_out, d_out, m_out, v_out):
        g = _sum_parts(me_ref[0], p_ref, own_ref[0])
        d, mn, vn = _adamw_math(w_ref[...], g, m_ref[...], v_ref[...])
        g_out[...] = g
        d_out[...] = d
        m_out[...] = mn
        v_out[...] = vn

    blk = pl.BlockSpec((rows, cdim), lambda i, me_ref: (i, 0))
    sds = jax.ShapeDtypeStruct(w.shape, F32)
    return pl.pallas_call(
        body, name=name, out_shape=(sds,) * 4,
        grid_spec=pltpu.PrefetchScalarGridSpec(
            num_scalar_prefetch=1, grid=(r // rows,),
            in_specs=[pl.BlockSpec((N_DEV, rows, cdim), lambda i, me_ref: (0, i, 0)),
                      pl.BlockSpec((1, rows, cdim), lambda i, me_ref: (me_ref[0], i, 0)), blk, blk, blk],
            out_specs=(blk,) * 4),
        compiler_params=_params("parallel"))(me, parts, own, w, m, v)


def _adamw_small(parts, own, me, w, m, v, mask, name):
    def body(me_ref, *refs):
        refs = list(refs)
        p_ref = refs.pop(0)
        own_ref = None if own is None else refs.pop(0)
        w_ref, m_ref, v_ref = refs[:3]
        k_ref = None if mask is None else refs[3]
        g_out, d_out, m_out, v_out = refs[-4:]
        g = _sum_parts(me_ref[0], p_ref, None if own is None else own_ref[me_ref[0]])
        if mask is not None:
            g = g * k_ref[...]
        d, mn, vn = _adamw_math(w_ref[...], g, m_ref[...], v_ref[...])
        g_out[...] = g
        d_out[...] = d
        m_out[...] = mn
        v_out[...] = vn

    def whole(shape):
        nd = len(shape)
        return pl.BlockSpec(shape, lambda i, me_ref: (0,) * nd)

    sds = jax.ShapeDtypeStruct(w.shape, F32)
    ins = [parts] + ([] if own is None else [own]) + [w, m, v] + ([] if mask is None else [mask])
    return pl.pallas_call(
        body, name=name, out_shape=(sds,) * 4,
        grid_spec=pltpu.PrefetchScalarGridSpec(
            num_scalar_prefetch=1, grid=(1,), in_specs=[whole(a.shape) for a in ins],
            out_specs=(whole(w.shape),) * 4),
        compiler_params=_params("arbitrary"))(me, *ins)


_IN_SPLITS = ((0, 512), (512, 1024), (1024, 1536), (1536, 2560), (2560, IN_PAD))


def _prenorm(x, g1, tm, dep=None):
    t_tok = x.shape[0]
    deps = [] if dep is None else [dep]

    def body(x_ref, g_ref, *rest):
        xv = x_ref[...]
        r = lax.rsqrt(jnp.mean(xv * xv, axis=-1, keepdims=True) + EPS)
        rest[-1][...] = (xv * r * g_ref[...]).astype(BF16)

    row = pl.BlockSpec((tm, D_MODEL), lambda i: (i, 0))
    return pl.pallas_call(
        body, name="prenorm", grid=(t_tok // tm,), out_shape=jax.ShapeDtypeStruct((t_tok, D_MODEL), BF16),
        in_specs=[row, _full((1, D_MODEL))] + [pl.BlockSpec(memory_space=pl.ANY)] * len(deps), out_specs=row,
        compiler_params=_params("parallel"))(x, g1, *deps)


def _in_proj(h1, w_in, tm):
    t_tok = h1.shape[0]

    def body(h_ref, w_ref, *outs):
        h = h_ref[...]
        for (a, b), o_ref in zip(_IN_SPLITS, outs):
            o_ref[...] = _dot(h, w_ref[a:b, :], _NT).astype(o_ref.dtype)

    row = lambda n: pl.BlockSpec((tm, n), lambda i: (i, 0))
    widths = [b - a for a, b in _IN_SPLITS]
    dtypes = (BF16, BF16, BF16, F32, F32)
    return pl.pallas_call(
        body, name="in_proj", grid=(t_tok // tm,),
        out_shape=tuple(jax.ShapeDtypeStruct((t_tok, n), dt) for n, dt in zip(widths, dtypes)),
        in_specs=[row(D_MODEL), _full((IN_PAD, D_MODEL))], out_specs=tuple(row(n) for n in widths),
        compiler_params=_params("parallel"))(h1, w_in)


def _lane_masks():
    lane = lax.broadcasted_iota(jnp.int32, (1, 2 * HEAD_DIM), 1)
    left = (lane < HEAD_DIM).astype(F32)
    return left, 1.0 - left


def _stack_pair(v, m_l, m_r):
    return jnp.concatenate([v * m_l, v * m_r], axis=0).astype(BF16)


def _head_mean(x, avg):
    n = avg.shape[0]
    return jnp.concatenate([_split_dot(x[:, n * i:n * (i + 1)], avg, 2) for i in range(x.shape[1] // n)], axis=1)


def _gmlp_common(u, v, lnw, lnb, avg, wcat_ref, bias, m_l, m_r):
    ug, dug = _gelu_and_grad(u)
    vg, dvg = _gelu_and_grad(v)
    mu = _head_mean(vg, avg)
    vc = vg - mu
    var = _head_mean(vc * vc, avg)
    rstd = lax.rsqrt(var + EPS)
    vhat = vc * rstd
    vn = vhat * lnw + lnb
    rows = []
    for r in range(u.shape[0] // CHUNK):
        cols = []
        for j in range(N_HEADS // 2):
            pair = vn[CHUNK * r:CHUNK * (r + 1), 128 * j:128 * (j + 1)]
            cols.append(_dot(wcat_ref[j], _stack_pair(pair, m_l, m_r)))
        rows.append(jnp.concatenate(cols, axis=1) + bias)
    mixed = jnp.concatenate(rows, axis=0)
    return ug, dug, dvg, rstd, vhat, vn, mixed


_GMLP_ROWS = 4 * CHUNK


def _gmlp_fwd(u, v, lnw, lnb, wcat, bias, avg):
    t_tok = u.shape[0]
    tm = min(_GMLP_ROWS, t_tok)

    def body(u_ref, v_ref, lnw_ref, lnb_ref, wcat_ref, bias_ref, avg_ref, o_ref):
        m_l, m_r = _lane_masks()
        ug, _, _, _, _, _, mixed = _gmlp_common(
            u_ref[...].astype(F32), v_ref[...].astype(F32), lnw_ref[...], lnb_ref[...], avg_ref[...], wcat_ref,
            bias_ref[...], m_l, m_r)
        o_ref[...] = (ug * mixed).astype(BF16)

    row = pl.BlockSpec((tm, GM_WIDTH), lambda i: (i, 0))
    return pl.pallas_call(
        body, name="gmlp_fwd", grid=(t_tok // tm,), out_shape=jax.ShapeDtypeStruct((t_tok, GM_WIDTH), BF16),
        in_specs=[row, row, _full((1, GM_WIDTH)), _full((1, GM_WIDTH)), _full(wcat.shape), _full(bias.shape),
                  _full(avg.shape)],
        out_specs=row, compiler_params=_params("parallel"))(u, v, lnw, lnb, wcat, bias, avg)


def _shift_rows(x, edge, j, down):
    groups, cols = x.shape[0] // 8, x.shape[1]
    amount = j if down else 8 - j
    rot = pltpu.roll(x.reshape(groups, 8, cols), amount, axis=1)
    edge_rot = pltpu.roll(edge, amount, axis=0)[None]
    sub = lax.broadcasted_iota(jnp.int32, (1, 8, 1), 1)
    if down:
        out = jnp.where(sub < j, jnp.concatenate([edge_rot, rot[:-1]], axis=0), rot)
    else:
        out = jnp.where(sub < 8 - j, rot, jnp.concatenate([rot[1:], edge_rot], axis=0))
    return out.reshape(x.shape)


def _conv_pre(xbc, tail, cw_ref, cb):
    taps = [_shift_rows(xbc, tail, 3 - k, True) for k in range(3)] + [xbc]
    return cb + cw_ref[0:1, :] * taps[0] + cw_ref[1:2, :] * taps[1] + cw_ref[2:3, :] * taps[2] + cw_ref[3:4, :] * taps[3]


def _ssd_common(pre, dtr, dtb, alog, expand, tril):
    q = CHUNK
    sg = jax.nn.sigmoid(pre)
    act = pre * sg
    lane = lax.broadcasted_iota(jnp.int32, (1, CHUNK), 1)
    a_row = jnp.where(lane < N_HEADS, -jnp.exp(alog), 0.0)
    dtp = dtr + dtb
    dt = _softplus(dtp)
    a_cs = _split_dot_left(tril, dt * a_row, 3)
    a_cs_t = a_cs.T
    dt_exp = _split_dot(dt, expand, 3)
    a_exp = _split_dot(a_cs, expand, 3)
    a_end = a_exp[q - 1:q, :]
    li = lax.broadcasted_iota(jnp.int32, (q, q), 0)
    si = lax.broadcasted_iota(jnp.int32, (q, q), 1)
    causal = si <= li
    decay = []
    for h in range(N_HEADS):
        seg = a_cs[:, h:h + 1] - a_cs_t[h:h + 1, :]
        decay.append(jnp.where(causal, jnp.exp(jnp.minimum(seg, 0.0)), 0.0))
    return dict(pre=pre, sg=sg, act=act, a_row=a_row, dtp=dtp, dt=dt, dt_exp=dt_exp, a_exp=a_exp,
                e=jnp.exp(a_exp), w_end=jnp.exp(a_end - a_exp), cd=jnp.exp(a_end), decay=decay)


def _ssd_specs(t_tok, seq, reverse):
    nb, nc = t_tok // seq, seq // CHUNK

    def chunk(c):
        return nc - 1 - c if reverse else c

    def row(n, col=0):
        return pl.BlockSpec((nb, CHUNK, n), lambda c: (0, chunk(c), col))

    tail = pl.BlockSpec((nb, 8, CONV_CH), lambda c: (0, jnp.maximum(chunk(c) * (CHUNK // 8) - 1, 0), 0))
    states = pl.BlockSpec((nb, 1, N_STATE, SSM_WIDTH), lambda c: (0, chunk(c), 0, 0))
    fold = lambda a: a.reshape(nb, seq, a.shape[-1])
    unfold = lambda a: a.reshape(t_tok, a.shape[-1])
    return nb, nc, row, tail, states, fold, unfold


def _ssd_fwd(z, xbc, dtr, cw, cb, dtb, alog, dskip_exp, nw, expand, tril, seq):
    t_tok = z.shape[0]
    nb, nc, row, tail, states_spec, fold, unfold = _ssd_specs(t_tok, seq, False)

    def body(z_ref, xbc_ref, tail_ref, dtr_ref, cw_ref, cb_ref, dtb_ref, alog_ref, dsk_ref, nw_ref, exp_ref,
             tril_ref, o_ref, y_ref, st_ref, pre_ref, state_ref):
        c = pl.program_id(0)

        @pl.when(c == 0)
        def _():
            state_ref[...] = jnp.zeros_like(state_ref)

        m_l, m_r = _lane_masks()
        for s in range(nb):
            pre = _conv_pre(xbc_ref[s], jnp.where(c == 0, 0.0, tail_ref[s]), cw_ref, cb_ref[...])
            pre_ref[s] = pre
            f = _ssd_common(pre, dtr_ref[s], dtb_ref[...], alog_ref[...], exp_ref[...], tril_ref[...])
            act = f["act"]
            xs = act[:, :SSM_WIDTH]
            xdt = xs * f["dt_exp"]
            xw = xdt * f["w_end"]
            state = state_ref[s]
            st_ref[s, 0] = state
            ydiag, yoff, snew = [], [], []
            for g in range(2):
                bg = act[:, 512 + 128 * g:640 + 128 * g].astype(BF16)
                cg = act[:, 768 + 128 * g:896 + 128 * g].astype(BF16)
                cb_mat = _dot(cg, bg, _NT)
                for pr in range(2):
                    h0 = 4 * g + 2 * pr
                    gcat = jnp.concatenate(
                        [(cb_mat * f["decay"][h0]).astype(BF16), (cb_mat * f["decay"][h0 + 1]).astype(BF16)], axis=1)
                    ydiag.append(_dot(gcat, _stack_pair(xdt[:, 64 * h0:64 * h0 + 128], m_l, m_r)))
                yoff.append(_dot(cg, state[:, 256 * g:256 * (g + 1)].astype(BF16)))
                snew.append(_dot(bg, xw[:, 256 * g:256 * (g + 1)].astype(BF16), _TN))
            y = jnp.concatenate(ydiag, axis=1) + f["e"] * jnp.concatenate(yoff, axis=1) + dsk_ref[...] * xs
            state_ref[s] = state * f["cd"] + jnp.concatenate(snew, axis=1)
            y_ref[s] = y
            zv = z_ref[s].astype(F32)
            yg = y * (zv * jax.nn.sigmoid(zv))
            outs = []
            for g in range(2):
                ygg = yg[:, 256 * g:256 * (g + 1)]
                outs.append(ygg * lax.rsqrt(jnp.mean(ygg * ygg, axis=-1, keepdims=True) + EPS))
            o_ref[s] = (jnp.concatenate(outs, axis=1) * nw_ref[...]).astype(BF16)

    consts = [cw, cb, dtb, alog, dskip_exp, nw, expand, tril]
    sd = lambda n, dt: jax.ShapeDtypeStruct((nb, seq, n), dt)
    o, y, states, pre = pl.pallas_call(
        body, name="ssd_fwd", grid=(nc,),
        out_shape=(sd(SSM_WIDTH, BF16), sd(SSM_WIDTH, F32), jax.ShapeDtypeStruct((nb, nc, N_STATE, SSM_WIDTH), F32),
                   sd(CONV_CH, F32)),
        in_specs=[row(SSM_WIDTH), row(CONV_CH), tail, row(CHUNK)] + [_full(a.shape) for a in consts],
        out_specs=(row(SSM_WIDTH), row(SSM_WIDTH), states_spec, row(CONV_CH)),
        scratch_shapes=[pltpu.VMEM((nb, N_STATE, SSM_WIDTH), F32)],
        compiler_params=_params("arbitrary"))(fold(z), fold(xbc), fold(xbc), fold(dtr), *consts)
    return unfold(o), unfold(y), states, unfold(pre)


def _out_proj(mix_a, mix_b, w_out, x, g2, g3, tm, dep=None):
    t_tok = x.shape[0]
    deps = [] if dep is None else [dep]

    def body(a_ref, b_ref, w_ref, x_ref, g2_ref, g3_ref, *rest):
        o_ref, x2_ref, h3_ref = rest[-3:]
        o = _dot(a_ref[...], w_ref[0:GM_WIDTH, :]) + _dot(b_ref[...], w_ref[GM_WIDTH:, :])
        o_ref[...] = o
        r2 = lax.rsqrt(jnp.mean(o * o, axis=-1, keepdims=True) + EPS)
        x2 = x_ref[...] + o * r2 * g2_ref[...]
        x2_ref[...] = x2
        r3 = lax.rsqrt(jnp.mean(x2 * x2, axis=-1, keepdims=True) + EPS)
        h3_ref[...] = (x2 * r3 * g3_ref[...]).astype(BF16)

    row = lambda n: pl.BlockSpec((tm, n), lambda i: (i, 0))
    sd = lambda dt: jax.ShapeDtypeStruct((t_tok, D_MODEL), dt)
    return pl.pallas_call(
        body, name="out_proj", grid=(t_tok // tm,), out_shape=(sd(F32), sd(F32), sd(BF16)),
        in_specs=[row(GM_WIDTH), row(SSM_WIDTH), _full((D_MODEL, D_MODEL)), row(D_MODEL), _full((1, D_MODEL)),
                  _full((1, D_MODEL))] + [pl.BlockSpec(memory_space=pl.ANY)] * len(deps),
        out_specs=(row(D_MODEL),) * 3, compiler_params=_params("parallel"))(mix_a, mix_b, w_out, x, g2, g3, *deps)


def _mlp_fwd(h3, w_up, w_down, x2, target, g4, tm, tf):
    t_tok = x2.shape[0]

    def up_body(h_ref, wu_ref, ra_ref):
        ra_ref[...] = jnp.maximum(_dot(h_ref[...], wu_ref[...]), 0.0).astype(BF16)

    tu = min(2 * tm, t_tok)
    ra = pl.pallas_call(
        up_body, name="mlp_up", grid=(D_FF // tf, t_tok // tu), out_shape=jax.ShapeDtypeStruct((t_tok, D_FF), BF16),
        in_specs=[pl.BlockSpec((tu, D_MODEL), lambda j, i: (i, 0)), pl.BlockSpec((D_MODEL, tf), lambda j, i: (0, j))],
        out_specs=pl.BlockSpec((tu, tf), lambda j, i: (i, j)), compiler_params=_params("parallel", "parallel"))(h3, w_up)

    def down_body(ra_ref, wd_ref, x2_ref, t_ref, g4_ref, dd_ref, dy_ref, dg4_ref, loss_ref):
        i = pl.program_id(0)
        rav = ra_ref[...]
        dvec = _dot(rav * rav, wd_ref[...])
        r4 = lax.rsqrt(jnp.mean(dvec * dvec, axis=-1, keepdims=True) + EPS)
        dn = dvec * r4
        g4 = g4_ref[...]
        err = x2_ref[...] + dn * g4 - t_ref[...]
        dy = err * (1.0 / D_MODEL)
        dy_ref[...] = dy
        dg = dy * g4
        dd_ref[...] = (r4 * (dg - dn * jnp.mean(dg * dn, axis=-1, keepdims=True))).astype(BF16)
        _acc_rows(dg4_ref, _rsum(dy * dn), i == 0)
        tile_loss = 0.5 * jnp.sum(jnp.sum(err * err, axis=-1, keepdims=True), axis=0, keepdims=True) / D_MODEL
        _acc_rows(loss_ref, jnp.broadcast_to(tile_loss, (1, 128)), i == 0)

    row = pl.BlockSpec((tm, D_MODEL), lambda i: (i, 0))
    dd, dy, dg4, loss = pl.pallas_call(
        down_body, name="mlp_down", grid=(t_tok // tm,),
        out_shape=(jax.ShapeDtypeStruct((t_tok, D_MODEL), BF16), jax.ShapeDtypeStruct((t_tok, D_MODEL), F32),
                   jax.ShapeDtypeStruct((1, D_MODEL), F32), jax.ShapeDtypeStruct((1, 128), F32)),
        in_specs=[pl.BlockSpec((tm, D_FF), lambda i: (i, 0)), _full((D_FF, D_MODEL)), row, row, _full((1, D_MODEL))],
        out_specs=(row, row, _full((1, D_MODEL)), _full((1, 128))),
        compiler_params=_params("arbitrary"))(ra, w_down, x2, target, g4)
    return ra, dd, dy, dg4, loss


def _mlp_bwd(dd, w_down, ra, w_up, x2, dy, o, g3, g2, tm, tf):
    t_tok = x2.shape[0]

    def hidden_body(dd_ref, wd_ref, ra_ref, da_ref):
        df = _dot(dd_ref[...], wd_ref[...], _NT)
        da_ref[...] = (df * (2.0 * ra_ref[...].astype(F32))).astype(BF16)

    tu = min(2 * tm, t_tok)
    da = pl.pallas_call(
        hidden_body, name="mlp_bwd_hidden", grid=(D_FF // tf, t_tok // tu),
        out_shape=jax.ShapeDtypeStruct((t_tok, D_FF), BF16),
        in_specs=[pl.BlockSpec((tu, D_MODEL), lambda j, i: (i, 0)), pl.BlockSpec((tf, D_MODEL), lambda j, i: (j, 0)),
                  pl.BlockSpec((tu, tf), lambda j, i: (i, j))],
        out_specs=pl.BlockSpec((tu, tf), lambda j, i: (i, j)),
        compiler_params=_params("parallel", "parallel"))(dd, w_down, ra)

    def in_body(da_ref, wu_ref, x2_ref, dy_ref, o_ref, g3_ref, g2_ref, dx2_ref, do_ref, dg3_ref, dg2_ref):
        i = pl.program_id(0)
        dh3 = _dot(da_ref[...], wu_ref[...], _NT)
        dn3, dg3 = _rms_bwd(x2_ref[...], g3_ref[...], dh3)
        dx2 = dy_ref[...] + dn3
        dx2_ref[...] = dx2
        do, dg2 = _rms_bwd(o_ref[...], g2_ref[...], dx2)
        do_ref[...] = do.astype(BF16)
        _acc_rows(dg3_ref, dg3, i == 0)
        _acc_rows(dg2_ref, dg2, i == 0)

    row = pl.BlockSpec((tm, D_MODEL), lambda i: (i, 0))
    vec = _full((1, D_MODEL))
    sd = lambda dt: jax.ShapeDtypeStruct((t_tok, D_MODEL), dt)
    dx2, do, dg3, dg2 = pl.pallas_call(
        in_body, name="mlp_bwd_in", grid=(t_tok // tm,),
        out_shape=(sd(F32), sd(BF16), jax.ShapeDtypeStruct((1, D_MODEL), F32), jax.ShapeDtypeStruct((1, D_MODEL), F32)),
        in_specs=[pl.BlockSpec((tm, D_FF), lambda i: (i, 0)), _full((D_MODEL, D_FF)), row, row, row, vec, vec],
        out_specs=(row, row, vec, vec), compiler_params=_params("arbitrary"))(da, w_up, x2, dy, o, g3, g2)
    return da, dx2, do, dg3, dg2


def _wgrad(a, b, out_blocks, bm, bn, bk, square_a, name, dep=None):
    t_tok, m = a.shape
    n = b.shape[1]
    nk = t_tok // bk

    def body(a_ref, b_ref, *rest):
        o_ref, acc_ref = rest[-2:]
        k = pl.program_id(2)
        av = a_ref[...]
        if square_a:
            av = av * av
        part = _dot(av, b_ref[...], _TN)

        def emit(res):
            if out_blocks is None:
                o_ref[...] = res.astype(BF16)
            else:
                o_ref[0] = res.astype(BF16)

        if nk == 1:
            emit(part)
            return

        @pl.when(k == 0)
        def _():
            acc_ref[...] = part

        @pl.when(k > 0)
        def _():
            acc_ref[...] += part

        @pl.when(k == nk - 1)
        def _():
            emit(acc_ref[...])

    if out_blocks is None:
        out_shape = jax.ShapeDtypeStruct((m, n), BF16)
        out_spec = pl.BlockSpec((bm, bn), lambda i, j, k: (i, j))
    else:
        assert n // out_blocks == bn
        out_shape = jax.ShapeDtypeStruct((out_blocks, m, bn), BF16)
        out_spec = pl.BlockSpec((1, bm, bn), lambda i, j, k: (j, i, 0))
    deps = [] if dep is None else [dep]
    return pl.pallas_call(
        body, name=name, grid=(m // bm, n // bn, nk), out_shape=out_shape,
        in_specs=[pl.BlockSpec((bk, bm), lambda i, j, k: (k, i)), pl.BlockSpec((bk, bn), lambda i, j, k: (k, j))]
        + [pl.BlockSpec(memory_space=pl.ANY)] * len(deps),
        out_specs=out_spec, scratch_shapes=[pltpu.VMEM((bm, bn) if nk > 1 else (8, 128), F32)],
        compiler_params=_params("parallel", "parallel", "arbitrary"))(a, b, *deps)


def _wgrad_in(h1, pieces, bn, name, dep=None):
    t_tok = h1.shape[0]
    widths = [p.shape[1] for p in pieces]
    starts = [sum(widths[:i]) for i in range(len(widths))]

    def body(h_ref, *rest):
        piece_refs = rest[:len(widths)]
        o_ref = rest[-1]
        hv = h_ref[...]
        for a, n, r in zip(starts, widths, piece_refs):
            o_ref[a:a + n, :] = _dot(r[...], hv, _TN).astype(BF16)

    deps = [] if dep is None else [dep]
    return pl.pallas_call(
        body, name=name, grid=(D_MODEL // bn,), out_shape=jax.ShapeDtypeStruct((sum(widths), D_MODEL), BF16),
        in_specs=[pl.BlockSpec((t_tok, bn), lambda j: (0, j))] + [pl.BlockSpec((t_tok, n), lambda j: (0, 0)) for n in widths]
        + [pl.BlockSpec(memory_space=pl.ANY)] * len(deps),
        out_specs=pl.BlockSpec((sum(widths), bn), lambda j: (0, j)),
        compiler_params=_params("parallel"))(h1, *pieces, *deps)


def _dmix(do, w_out, tm, dep=None):
    t_tok = do.shape[0]

    def body(d_ref, w_ref, *rest):
        rest[-1][...] = _dot(d_ref[...], w_ref[...], _NT).astype(BF16)

    row = pl.BlockSpec((tm, D_MODEL), lambda i: (i, 0))
    deps = [] if dep is None else [dep]
    return pl.pallas_call(
        body, name="dmix", grid=(t_tok // tm,), out_shape=jax.ShapeDtypeStruct((t_tok, D_MODEL), BF16),
        in_specs=[row, _full((D_MODEL, D_MODEL))] + [pl.BlockSpec(memory_space=pl.ANY)] * len(deps), out_specs=row,
        compiler_params=_params("parallel"))(do, w_out, *deps)


def _gmlp_bwd(dmix, u, v, lnw, lnb, wcat, wtcat, bias, avg, expand_t):
    t_tok = u.shape[0]
    tm = min(_GMLP_ROWS, t_tok)

    def body(dm_ref, u_ref, v_ref, lnw_ref, lnb_ref, wcat_ref, wtcat_ref, bias_ref, avg_ref, expt_ref, du_ref, dv_ref,
             dw_ref, db_ref, dlnw_ref, dlnb_ref):
        i = pl.program_id(0)
        m_l, m_r = _lane_masks()
        avg = avg_ref[...]
        lnw = lnw_ref[...]
        ug, dug, dvg, rstd, vhat, vn, mixed = _gmlp_common(
            u_ref[...].astype(F32), v_ref[...].astype(F32), lnw, lnb_ref[...], avg, wcat_ref, bias_ref[...], m_l, m_r)
        dya = dm_ref[...].astype(F32)
        du_ref[...] = (dya * mixed * dug).astype(BF16)
        dmixed = dya * ug
        dvn_rows, dws, dbt = [], [None] * N_HEADS, None
        for r in range(tm // CHUNK):
            dvn_cols = []
            for j in range(N_HEADS // 2):
                dmp = dmixed[CHUNK * r:CHUNK * (r + 1), 128 * j:128 * (j + 1)]
                dvn_cols.append(_dot(wtcat_ref[j], _stack_pair(dmp, m_l, m_r)))
                vnp = vn[CHUNK * r:CHUNK * (r + 1), 128 * j:128 * (j + 1)].astype(BF16)
                for i_h, mask in enumerate((m_l, m_r)):
                    part = _dot((dmp * mask).astype(BF16), vnp, _NT)
                    dws[2 * j + i_h] = part if r == 0 else dws[2 * j + i_h] + part
            dvn_rows.append(jnp.concatenate(dvn_cols, axis=1))
            part = _split_dot(dmixed[CHUNK * r:CHUNK * (r + 1), :], expt_ref[...], 2)
            dbt = part if r == 0 else dbt + part
        dvn = jnp.concatenate(dvn_rows, axis=0)
        dvh = dvn * lnw
        dvgel = rstd * (dvh - _head_mean(dvh, avg) - vhat * _head_mean(dvh * vhat, avg))
        dv_ref[...] = (dvgel * dvg).astype(BF16)
        first = i == 0

        @pl.when(first)
        def _():
            for h in range(N_HEADS):
                dw_ref[h] = dws[h]
            db_ref[...] = dbt

        @pl.when(jnp.logical_not(first))
        def _():
            for h in range(N_HEADS):
                dw_ref[h] += dws[h]
            db_ref[...] += dbt

        _acc_rows(dlnw_ref, _rsum(dvn * vhat), first)
        _acc_rows(dlnb_ref, _rsum(dvn), first)

    row = pl.BlockSpec((tm, GM_WIDTH), lambda i: (i, 0))
    consts = [lnw, lnb, wcat, wtcat, bias, avg, expand_t]
    return pl.pallas_call(
        body, name="gmlp_bwd", grid=(t_tok // tm,),
        out_shape=(jax.ShapeDtypeStruct((t_tok, GM_WIDTH), BF16), jax.ShapeDtypeStruct((t_tok, GM_WIDTH), BF16),
                   jax.ShapeDtypeStruct((N_HEADS, CHUNK, CHUNK), F32), jax.ShapeDtypeStruct((CHUNK, CHUNK), F32),
                   jax.ShapeDtypeStruct((1, GM_WIDTH), F32), jax.ShapeDtypeStruct((1, GM_WIDTH), F32)),
        in_specs=[row, row, row] + [_full(a.shape) for a in consts],
        out_specs=(row, row, _full((N_HEADS, CHUNK, CHUNK)), _full((CHUNK, CHUNK)), _full((1, GM_WIDTH)),
                   _full((1, GM_WIDTH))),
        compiler_params=_params("arbitrary"))(dmix, u, v, *consts)


def _ssd_bwd(dmix, z, xbc, pre, dtr, y, states, cw, cb, dtb, alog, dskip_exp, nw, expand, expand_t, tril, triu, seq,
             dep=None):
    t_tok = z.shape[0]
    nb, nc, row, _, states_spec, fold, unfold = _ssd_specs(t_tok, seq, True)
    q = CHUNK

    def one_sequence(s, dm_ref, z_ref, xbc_ref, pre_ref, dtr_ref, y_ref, st_ref, cw_ref, dtb_ref, alog_ref, dsk_ref,
                     nw_ref, exp_ref, expt_ref, tril_ref, triu_ref, dz_ref, dxbc_ref, ddt_ref, dhead_ref, dstate_ref):
        m_l, m_r = _lane_masks()
        expt = expt_ref[...]
        f = _ssd_common(pre_ref[s], dtr_ref[s], dtb_ref[...], alog_ref[...], exp_ref[...], tril_ref[...])
        act, pre, sg = f["act"], f["pre"], f["sg"]
        xs = act[:, :SSM_WIDTH]
        xdt = xs * f["dt_exp"]
        xw = xdt * f["w_end"]
        state = st_ref[s, 0]
        dstate = dstate_ref[s]
        zv, yv, dout, nw = z_ref[s].astype(F32), y_ref[s], dm_ref[s].astype(F32), nw_ref[...]
        sz = jax.nn.sigmoid(zv)
        sl = zv * sz
        yg = yv * sl
        tv = dout * nw
        dyg_parts, ygh_parts = [], []
        for g in range(2):
            ygg = yg[:, 256 * g:256 * (g + 1)]
            rr = lax.rsqrt(jnp.mean(ygg * ygg, axis=-1, keepdims=True) + EPS)
            ygh = ygg * rr
            tg = tv[:, 256 * g:256 * (g + 1)]
            dyg_parts.append(rr * (tg - ygh * jnp.mean(tg * ygh, axis=-1, keepdims=True)))
            ygh_parts.append(ygh)
        dyg = jnp.concatenate(dyg_parts, axis=1)
        dnw = _rsum(dout * jnp.concatenate(ygh_parts, axis=1))
        dy = dyg * sl
        dz_ref[s] = (dyg * yv * (sz * (1.0 + zv * (1.0 - sz)))).astype(BF16)
        ddsk = _rsum(dy * xs)
        dye = dy * f["e"]
        lane = lax.broadcasted_iota(jnp.int32, (q, q), 1)
        sub = lax.broadcasted_iota(jnp.int32, (q, q), 0)
        rs_mat = jnp.zeros((q, q), F32)
        cs_mat = jnp.zeros((q, q), F32)
        dxdt_cols, yoff, dst_in, dxw, d_b, d_c = [], [], [], [], [], []
        for g in range(2):
            bg = act[:, 512 + 128 * g:640 + 128 * g].astype(BF16)
            cg = act[:, 768 + 128 * g:896 + 128 * g].astype(BF16)
            cb_mat = _dot(cg, bg, _NT)
            stg = state[:, 256 * g:256 * (g + 1)].astype(BF16)
            dyeg = dye[:, 256 * g:256 * (g + 1)].astype(BF16)
            yoff.append(_dot(cg, stg))
            dcg = _dot(dyeg, stg, _NT)
            dst_in.append(_dot(cg, dyeg, _TN))
            dcb = jnp.zeros((q, q), F32)
            for pr in range(2):
                h0 = 4 * g + 2 * pr
                gf = [cb_mat * f["decay"][h0], cb_mat * f["decay"][h0 + 1]]
                gcat = jnp.concatenate([gf[0].astype(BF16), gf[1].astype(BF16)], axis=1)
                xst = _stack_pair(xdt[:, 64 * h0:64 * h0 + 128], m_l, m_r)
                dyp = dy[:, 64 * h0:64 * h0 + 128].astype(BF16)
                dgcat = _dot(dyp, xst, _NT)
                dxst = _dot(gcat, dyp, _TN)
                dxdt_cols.append(dxst[:q] * m_l + dxst[q:] * m_r)
                for i in range(2):
                    h = h0 + i
                    dg = dgcat[:, q * i:q * (i + 1)]
                    mm = dg * gf[i]
                    rs_mat = rs_mat + jnp.where(lane == h, jnp.sum(mm, axis=1, keepdims=True), 0.0)
                    cs_mat = cs_mat + jnp.where(sub == h, jnp.sum(mm, axis=0, keepdims=True), 0.0)
                    dcb = dcb + dg * f["decay"][h]
            dcb16 = dcb.astype(BF16)
            dstg = dstate[:, 256 * g:256 * (g + 1)].astype(BF16)
            d_c.append(dcg + _dot(dcb16, bg))
            dxw.append(_dot(bg, dstg))
            d_b.append(_dot(dcb16, cg, _TN) + _dot(xw[:, 256 * g:256 * (g + 1)].astype(BF16), dstg, _NT))
        dxw = jnp.concatenate(dxw, axis=1)
        dxdt = jnp.concatenate(dxdt_cols, axis=1) + dxw * f["w_end"]
        qv = dxw * xw
        end_row = _rsum(qv) + _rsum(dstate * state) * f["cd"]
        x2 = dye * jnp.concatenate(yoff, axis=1) - qv
        row_i = lax.broadcasted_iota(jnp.int32, (q, 1), 0)
        x2 = x2 + jnp.where(row_i == q - 1, end_row, 0.0)
        da_cs = _split_dot(x2, expt, 2) + rs_mat - cs_mat.T
        ddt = _split_dot(dxdt * xs, expt, 2)
        dxs = dsk_ref[...] * dy + dxdt * f["dt_exp"]
        dda = _split_dot_left(triu_ref[...], da_cs, 3)
        ddt = ddt + dda * f["a_row"]
        dalog = _rsum(dda * f["dt"]) * f["a_row"]
        draw = ddt * jax.nn.sigmoid(f["dtp"])
        ddt_ref[s] = draw.astype(BF16)
        dact = jnp.concatenate([dxs] + d_b + d_c, axis=1)
        dpre = dact * (sg * (1.0 + pre * (1.0 - sg)))
        dhead = dhead_ref[s]
        xv = xbc_ref[s]
        shifted = [_shift_rows(dpre, dhead, 3 - k, False) for k in range(3)] + [dpre]
        dxbc = cw_ref[3:4, :] * dpre
        for k in range(3):
            dxbc = dxbc + cw_ref[k:k + 1, :] * shifted[k]
        dxbc_ref[s] = dxbc.astype(BF16)
        dhead_ref[s] = dpre[0:8, :]
        dstate_ref[s] = dstate * f["cd"] + jnp.concatenate(dst_in, axis=1)
        row8 = lax.broadcasted_iota(jnp.int32, (8, 1), 0)
        dcw = jnp.zeros((8, CONV_CH), F32)
        for k in range(4):
            dcw = dcw + jnp.where(row8 == k, _rsum(shifted[k] * xv), 0.0)
        return dcw, _rsum(dpre), _rsum(draw), dalog, _split_dot(ddsk, expt, 3), dnw

    def body(dm_ref, z_ref, xbc_ref, pre_ref, dtr_ref, y_ref, st_ref, cw_ref, cb_ref, dtb_ref, alog_ref, dsk_ref,
             nw_ref, exp_ref, expt_ref, tril_ref, triu_ref, dz_ref, dxbc_ref, ddt_ref, dcw_ref, dcb_ref, ddtb_ref,
             dalog_ref, dd_ref, dnw_ref, dhead_ref, dstate_ref):
        c = pl.program_id(0)
        first = c == 0

        @pl.when(first)
        def _():
            dstate_ref[...] = jnp.zeros_like(dstate_ref)
            dhead_ref[...] = jnp.zeros_like(dhead_ref)

        total = None
        for s in range(nb):
            parts = one_sequence(s, dm_ref, z_ref, xbc_ref, pre_ref, dtr_ref, y_ref, st_ref, cw_ref, dtb_ref, alog_ref,
                                 dsk_ref, nw_ref, exp_ref, expt_ref, tril_ref, triu_ref, dz_ref, dxbc_ref, ddt_ref,
                                 dhead_ref, dstate_ref)
            total = parts if total is None else tuple(a + b for a, b in zip(total, parts))
        dcw = total[0]

        @pl.when(first)
        def _():
            dcw_ref[...] = dcw

        @pl.when(jnp.logical_not(first))
        def _():
            dcw_ref[...] += dcw

        for ref, part in zip((dcb_ref, ddtb_ref, dalog_ref, dd_ref, dnw_ref), total[1:]):
            _acc_rows(ref, part, first)

    consts = [cw, cb, dtb, alog, dskip_exp, nw, expand, expand_t, tril, triu]
    deps = [] if dep is None else [dep]
    n_in = 7 + len(consts)

    def body_skipping_dep(*refs):
        body(*refs[:n_in], *refs[n_in + len(deps):])

    acc = lambda n: jax.ShapeDtypeStruct((1, n), F32)
    sd = lambda n: jax.ShapeDtypeStruct((nb, seq, n), BF16)
    dz, dxbc, ddt, *small_grads = pl.pallas_call(
        body_skipping_dep, name="ssd_bwd", grid=(nc,),
        out_shape=(sd(SSM_WIDTH), sd(CONV_CH), sd(CHUNK), jax.ShapeDtypeStruct((8, CONV_CH), F32), acc(CONV_CH),
                   acc(CHUNK), acc(CHUNK), acc(CHUNK), acc(SSM_WIDTH)),
        in_specs=[row(SSM_WIDTH, col=1), row(SSM_WIDTH), row(CONV_CH), row(CONV_CH), row(CHUNK), row(SSM_WIDTH),
                  states_spec]
        + [_full(a.shape) for a in consts] + [pl.BlockSpec(memory_space=pl.ANY)] * len(deps),
        out_specs=(row(SSM_WIDTH), row(CONV_CH), row(CHUNK), _full((8, CONV_CH)), _full((1, CONV_CH)),
                   _full((1, CHUNK)), _full((1, CHUNK)), _full((1, CHUNK)), _full((1, SSM_WIDTH))),
        scratch_shapes=[pltpu.VMEM((nb, 8, CONV_CH), F32), pltpu.VMEM((nb, N_STATE, SSM_WIDTH), F32)],
        compiler_params=_params("arbitrary"))(
            fold(dmix), fold(z), fold(xbc), fold(pre), fold(dtr), fold(y), states, *consts, *deps)
    return (unfold(dz), unfold(dxbc), unfold(ddt), *small_grads)


def _in_bwd(du, dv, dz, dxbc, ddt, w_in, x, dx2, g1, tm, me, riders=(), dep=None):
    t_tok = x.shape[0]
    steps = t_tok // tm

    n_in = [5 + ("mask" in rd) for rd in riders]
    first_in = [sum(n_in[:r]) for r in range(len(riders))]

    def body(me_ref, du_ref, dv_ref, dz_ref, dxbc_ref, ddt_ref, w_ref, x_ref, dx2_ref, g_ref, *rest):
        outs = rest[len(rest) - 2 - 4 * len(riders):]
        gx_ref, dg_ref = outs[:2]
        i = pl.program_id(0)
        dh = None
        for (a, b), ref in zip(_IN_SPLITS, (du_ref, dv_ref, dz_ref, dxbc_ref, ddt_ref)):
            part = _dot(ref[...], w_ref[a:b, :])
            dh = part if dh is None else dh + part
        dn, dg = _rms_bwd(x_ref[...], g_ref[...], dh)
        gx_ref[...] = dx2_ref[...] + dn
        _acc_rows(dg_ref, dg, i == 0)
        for r in range(len(riders)):
            p_ref, own_ref, w_ref_r, m_ref_r, v_ref_r = rest[first_in[r]:first_in[r] + 5]
            g = _sum_parts(me_ref[0], p_ref, own_ref[0])
            if n_in[r] == 6:
                g = g * rest[first_in[r] + 5][...]
            d, mn, vn = _adamw_math(w_ref_r[...], g, m_ref_r[...], v_ref_r[...])
            for o_ref, val in zip(outs[2 + 4 * r:6 + 4 * r], (g, d, mn, vn)):
                o_ref[...] = val

    row = lambda n: pl.BlockSpec((tm, n), lambda i, me_ref: (i, 0))
    whole = lambda shape: pl.BlockSpec(shape, lambda i, me_ref: (0,) * len(shape))
    widths = [b - a for a, b in _IN_SPLITS]
    deps = [] if dep is None else [dep]
    rider_args, rider_specs, rider_out_shapes, rider_out_specs = [], [], [], []
    for rd in riders:
        rows, cols = rd["w"].shape[0] // steps, rd["w"].shape[1]
        blk = pl.BlockSpec((rows, cols), lambda i, me_ref: (i, 0))
        rider_args += [rd["parts"], rd["own"], rd["w"], rd["m"], rd["v"]]
        rider_specs += [pl.BlockSpec((N_DEV, rows, cols), lambda i, me_ref: (0, i, 0)),
                        pl.BlockSpec((1, rows, cols), lambda i, me_ref: (me_ref[0], i, 0)), blk, blk, blk]
        if "mask" in rd:
            rider_args.append(rd["mask"])
            rider_specs.append(whole((rows, cols)))
        rider_out_shapes += [jax.ShapeDtypeStruct(rd["w"].shape, F32)] * 4
        rider_out_specs += [blk] * 4
    outs = pl.pallas_call(
        body, name="in_bwd",
        out_shape=(jax.ShapeDtypeStruct((t_tok, D_MODEL), F32), jax.ShapeDtypeStruct((1, D_MODEL), F32),
                   *rider_out_shapes),
        grid_spec=pltpu.PrefetchScalarGridSpec(
            num_scalar_prefetch=1, grid=(steps,),
            in_specs=[row(n) for n in widths] + [whole((IN_PAD, D_MODEL)), row(D_MODEL), row(D_MODEL),
                                                 whole((1, D_MODEL))] + rider_specs
            + [pl.BlockSpec(memory_space=pl.ANY)] * len(deps),
            out_specs=(row(D_MODEL), whole((1, D_MODEL)), *rider_out_specs)),
        compiler_params=_params("arbitrary"))(me, du, dv, dz, dxbc, ddt, w_in, x, dx2, g1, *rider_args, *deps)
    return outs[0], outs[1], [tuple(outs[2 + 4 * r:6 + 4 * r]) for r in range(len(riders))]


def _pad_lanes(a, n):
    return jnp.pad(a, ((0, 0), (0, n - a.shape[1])))


def _local_step(x, target, seq, small, hooks, first_dep=None):
    t_tok = x.shape[0]
    tm = min(TOKEN_TILE, t_tok)
    avg, expand, expand_t, tril, triu = _const_mats()
    g1, g2, g3, g4 = (small[k].reshape(1, D_MODEL) for k in
                      ("norm_mix_pre", "norm_mix_post", "norm_ffn_pre", "norm_ffn_post"))
    tie = (lambda a: a) if first_dep is None else (lambda a: a + first_dep[0, 0])
    lnw = tie(small["gm_ln_w"]).reshape(1, GM_WIDTH)
    lnb = tie(small["gm_ln_b"]).reshape(1, GM_WIDTH)
    causal = jnp.tril(jnp.ones((CHUNK, CHUNK), F32))
    wm = tie(small["gm_w_s"]) * causal
    pair = lambda w: w.reshape(4, 2, CHUNK, CHUNK).transpose(0, 2, 1, 3).reshape(4, CHUNK, 2 * CHUNK).astype(BF16)
    wcat = pair(wm)
    wtcat = pair(jnp.swapaxes(wm, 1, 2))
    bias = jnp.repeat(tie(small["gm_b_s"]).T, HEAD_DIM, axis=1)
    cb = small["conv_b"].reshape(1, CONV_CH)
    dtb = _pad_lanes(tie(small["dt_bias"]).reshape(1, N_HEADS), CHUNK)
    alog = _pad_lanes(tie(small["a_log"]).reshape(1, N_HEADS), CHUNK)
    dskip_exp = jnp.repeat(tie(small["d_skip"]).reshape(1, N_HEADS), HEAD_DIM, axis=1)
    nw = small["ssm_norm_w"].reshape(1, SSM_WIDTH)

    h1 = _prenorm(x, g1, tm, hooks.get("prenorm_after", first_dep))
    w_in_t, conv_w = hooks["mixer_weights"](h1)
    tall = min(2 * tm, t_tok)
    u, v, z, xbc, dtr = _in_proj(h1, w_in_t, tall)
    mix_a = _gmlp_fwd(u, v, lnw, lnb, wcat, bias, avg)
    mix_b, y_pre, states, pre = _ssd_fwd(z, xbc, dtr, conv_w, cb, dtb, alog, dskip_exp, nw, expand, tril, seq)
    w_out, dep = hooks["mixers_done"](mix_b)
    o, x2, h3 = _out_proj(mix_a, mix_b, w_out, x, g2, g3, tall, dep)
    w_up, w_down = hooks["mlp_weights"](h3)
    tf = FF_TILE
    ra, dd, dy, dg4, loss = _mlp_fwd(h3, w_up, w_down, x2, target, g4, tm, tf)

    da, dx2, do, dg3, dg2 = _mlp_bwd(dd, w_down, ra, w_up, x2, dy, o, g3, g2, tm, tf)
    g_w_down = _wgrad(ra, dd, None, WGRAD_TILE, D_MODEL, t_tok, True, "wgrad_down")
    g_w_up = _wgrad(h3, da, N_DEV, D_MODEL, D_FF // N_DEV, t_tok, False, "wgrad_up")
    dep = hooks["mlp_grads"](g_w_down, g_w_up)
    dmix = _dmix(do, w_out, tall, dep)
    g_w_out = _wgrad_in(do, (mix_a, mix_b), WGRAD_TILE, "wgrad_out", dep)
    du, dv, dws, dbt, dlnw, dlnb = _gmlp_bwd(dmix, u, v, lnw, lnb, wcat, wtcat, bias, avg, expand_t)
    dep = hooks["gmlp_grads"](g_w_out, dws)
    dz, dxbc, ddt, dcw, dcb, ddtb, dalog, ddsk, dnw = _ssd_bwd(
        dmix, z, xbc, pre, dtr, y_pre, states, conv_w, cb, dtb, alog, dskip_exp, nw, expand, expand_t, tril, triu, seq,
        dep)
    g_w_in = jnp.concatenate([_wgrad_in(h1, (du, dv, dz), WGRAD_TILE // 2, "wgrad_in_a", dep),
                              _wgrad_in(h1, (dxbc, ddt), WGRAD_TILE // 2, "wgrad_in_b", dep)], axis=0)
    dep = hooks["in_grads"](g_w_in, dcw[0:4])
    riders = hooks["arrived_updates"](dep) if "arrived_updates" in hooks else []
    me = hooks.get("me", jnp.zeros((1,), jnp.int32))
    grad_x, dg1, updates = _in_bwd(du, dv, dz, dxbc, ddt, w_in_t, x, dx2, g1, tm, me, riders, dep)

    grads = dict(
        updates=updates,
        w_in=g_w_in, w_out=g_w_out, w_up=g_w_up, w_down=g_w_down, conv_w=dcw[0:4],
        norm_mix_pre=dg1, norm_mix_post=dg2, norm_ffn_pre=dg3, norm_ffn_post=dg4, gm_ln_w=dlnw, gm_ln_b=dlnb,
        gm_w_s=dws, gm_b_s=dbt, conv_b=dcb, dt_bias=ddtb, a_log=dalog, d_skip=ddsk, ssm_norm_w=dnw)
    return loss[0, 0], grad_x, grads


_WEIGHTS = ("norm_mix_pre", "w_in", "gm_ln_w", "gm_ln_b", "gm_w_s", "gm_b_s", "conv_w", "conv_b", "dt_bias", "a_log",
            "d_skip", "ssm_norm_w", "w_out", "norm_mix_post", "norm_ffn_pre", "w_up", "w_down", "norm_ffn_post")
_SLAB_ROWS = (("norm_mix_pre", 1024), ("norm_mix_post", 1024), ("norm_ffn_pre", 1024), ("norm_ffn_post", 1024),
              ("conv_b", 1024), ("ssm_norm_w", 512), ("gm_ln_w", 512), ("gm_ln_b", 512), ("dt_bias", 8), ("a_log", 8),
              ("d_skip", 8))
_SLAB_LOSS_ROW = len(_SLAB_ROWS)
_SLAB_BS_ROW = 16
_SMALL_PARAMS = tuple(name for name, _ in _SLAB_ROWS) + ("gm_b_s",)
_LN_PARAMS = ("gm_ln_w", "gm_ln_b")


def _pack_slab(g, loss_part):
    rows = [_pad_lanes(g[name], D_MODEL) for name, _ in _SLAB_ROWS]
    rows.append(jnp.broadcast_to(loss_part, (1, D_MODEL)))
    rows.append(jnp.zeros((_SLAB_BS_ROW - len(rows), D_MODEL), F32))
    rows.append(_pad_lanes(g["gm_b_s"].T[0:N_HEADS], D_MODEL))
    return jnp.concatenate(rows, axis=0)


def _adamw_slab(parts, w, m, v):
    names = _SMALL_PARAMS
    shapes = [w[k].shape for k in names]
    unfold = np.zeros((GM_WIDTH, HEAD_DIM), np.float32)
    for h in range(N_HEADS):
        unfold[h * HEAD_DIM:(h + 1) * HEAD_DIM, :] = np.eye(HEAD_DIM)
    unfold = jnp.asarray(unfold, dtype=BF16)
    n = len(names)

    def body(p_ref, unfold_ref, *refs):
        w_refs, m_refs, v_refs = refs[:n], refs[n:2 * n], refs[2 * n:3 * n]
        outs = refs[3 * n:]
        g_all = p_ref[0]
        for j in range(1, N_DEV):
            g_all = g_all + p_ref[j]
        lane = lax.broadcasted_iota(jnp.int32, (N_HEADS, GM_WIDTH), 1)
        head = lax.broadcasted_iota(jnp.int32, (N_HEADS, GM_WIDTH), 0)
        own_lanes = jnp.logical_and(lane >= head * HEAD_DIM, lane < (head + 1) * HEAD_DIM)
        for i, name in enumerate(names):
            if name == "gm_b_s":
                g = g_all[_SLAB_BS_ROW:_SLAB_BS_ROW + N_HEADS, 0:CHUNK]
            else:
                row = [r for r, (k, _) in enumerate(_SLAB_ROWS) if k == name][0]
                g = g_all[row:row + 1, 0:dict(_SLAB_ROWS)[name]]
                if name in _LN_PARAMS:
                    g = _split_dot(jnp.where(own_lanes, g, 0.0), unfold_ref[...], 3)
            d, mn, vn = _adamw_math(w_refs[i][...], g, m_refs[i][...], v_refs[i][...])
            for o_ref, val in zip(outs[4 * i:4 * i + 4], (g, d, mn, vn)):
                o_ref[...] = val
        outs[-1][...] = g_all[_SLAB_LOSS_ROW:_SLAB_LOSS_ROW + 1, 0:128]

    ins = [parts, unfold] + [d[k] for d in (w, m, v) for k in names]
    out_shape = tuple(jax.ShapeDtypeStruct(s, F32) for s in shapes for _ in range(4)) + (
        jax.ShapeDtypeStruct((1, 128), F32),)
    outs = pl.pallas_call(
        body, name="adamw_small", out_shape=out_shape, grid=(1,), in_specs=[_full(a.shape) for a in ins],
        out_specs=tuple(_full(s.shape) for s in out_shape), compiler_params=_params("arbitrary"))(*ins)
    return {k: tuple(outs[4 * i:4 * i + 4]) for i, k in enumerate(names)}, outs[-1][0, 0]


def kernel(x, norm_mix_pre, w_in, gm_ln_w, gm_ln_b, gm_w_s, gm_b_s, conv_w, conv_b, dt_bias, a_log, d_skip, ssm_norm_w, w_out, norm_mix_post, norm_ffn_pre, w_up, w_down, norm_ffn_post, loss_target, m_norm_mix_pre, m_w_in, m_gm_ln_w, m_gm_ln_b, m_gm_w_s, m_gm_b_s, m_conv_w, m_conv_b, m_dt_bias, m_a_log, m_d_skip, m_ssm_norm_w, m_w_out, m_norm_mix_post, m_norm_ffn_pre, m_w_up, m_w_down, m_norm_ffn_post, v_norm_mix_pre, v_w_in, v_gm_ln_w, v_gm_ln_b, v_gm_w_s, v_gm_b_s, v_conv_w, v_conv_b, v_dt_bias, v_a_log, v_d_skip, v_ssm_norm_w, v_w_out, v_norm_mix_post, v_norm_ffn_pre, v_w_up, v_w_down, v_norm_ffn_post):
    w = dict(norm_mix_pre=norm_mix_pre, w_in=w_in, gm_ln_w=gm_ln_w, gm_ln_b=gm_ln_b, gm_w_s=gm_w_s, gm_b_s=gm_b_s, conv_w=conv_w, conv_b=conv_b, dt_bias=dt_bias, a_log=a_log, d_skip=d_skip, ssm_norm_w=ssm_norm_w, w_out=w_out, norm_mix_post=norm_mix_post, norm_ffn_pre=norm_ffn_pre, w_up=w_up, w_down=w_down, norm_ffn_post=norm_ffn_post)
    m = dict(norm_mix_pre=m_norm_mix_pre, w_in=m_w_in, gm_ln_w=m_gm_ln_w, gm_ln_b=m_gm_ln_b, gm_w_s=m_gm_w_s, gm_b_s=m_gm_b_s, conv_w=m_conv_w, conv_b=m_conv_b, dt_bias=m_dt_bias, a_log=m_a_log, d_skip=m_d_skip, ssm_norm_w=m_ssm_norm_w, w_out=m_w_out, norm_mix_post=m_norm_mix_post, norm_ffn_pre=m_norm_ffn_pre, w_up=m_w_up, w_down=m_w_down, norm_ffn_post=m_norm_ffn_post)
    v = dict(norm_mix_pre=v_norm_mix_pre, w_in=v_w_in, gm_ln_w=v_gm_ln_w, gm_ln_b=v_gm_ln_b, gm_w_s=v_gm_w_s, gm_b_s=v_gm_b_s, conv_w=v_conv_w, conv_b=v_conv_b, dt_bias=v_dt_bias, a_log=v_a_log, d_skip=v_d_skip, ssm_norm_w=v_ssm_norm_w, w_out=v_w_out, norm_mix_post=v_norm_mix_post, norm_ffn_pre=v_norm_ffn_pre, w_up=v_w_up, w_down=v_w_down, norm_ffn_post=v_norm_ffn_post)
    n_batch, seq, _ = x.shape
    shard_in = IN_COLS // N_DEV

    me = (4 * lax.axis_index("x") + 2 * lax.axis_index("y") + lax.axis_index("c")).astype(jnp.int32).reshape(1)

    def in_slot(own):
        return lax.dynamic_update_slice(lax.empty((N_DEV,) + own.shape, own.dtype), own[None],
                                        (me[0],) + (0,) * own.ndim)

    w_in_sh, m_in_sh, v_in_sh = w_in[0].T, m_w_in[0].T, v_w_in[0].T
    first = [_cast_to_slot(w_in_sh, me, shard_in, "cast_w_in"), in_slot(conv_w[0]),
             _cast_to_slot(w_out[0], me, 128, "cast_w_out")]
    ici_1, tok_ici_1 = _exchange_start(first, [True] * 3, _SAME_CORE_PEERS, "gather_mix_ici_start")
    cast_up = _cast_to_slot(w_up[0], me, 1024, "cast_w_up", cols=True, dep=tok_ici_1)
    second = [cast_up, _cast_to_slot(w_down[0], me, 512, "cast_w_down", dep=cast_up)]
    gathering = {}

    def mixer_weights(after):
        bufs = [buf for buf, _ in _exchange_wait(ici_1, after, "gather_mix_ici_wait")]
        d2d_1, tok_d2d_1 = _exchange_start(bufs, [True] * 3, _SIBLING_FORWARD, "gather_mix_d2d_start")
        gathering["mlp_ici"], tok_ici_2 = _exchange_start(
            second, [True] * 2, _SAME_CORE_PEERS, "gather_mlp_ici_start", dep=tok_d2d_1)
        (_, ag_in), (_, ag_conv), (_, ag_out) = _exchange_wait(d2d_1, tok_ici_2, "gather_mix_d2d_wait")
        gathering["w_out"] = ag_out.reshape(D_MODEL, D_MODEL)
        w_in_t = jnp.pad(ag_in.reshape(IN_COLS, D_MODEL), ((0, IN_PAD - IN_COLS), (0, 0)))
        return w_in_t, ag_conv.transpose(1, 0, 2).reshape(4, CONV_CH)

    def mixers_done(after):
        bufs = [buf for buf, _ in _exchange_wait(gathering["mlp_ici"], after, "gather_mlp_ici_wait")]
        gathering["mlp"], tok = _exchange_start(bufs, [True] * 2, _SIBLING_FORWARD, "gather_mlp_d2d_start")
        return gathering["w_out"], tok

    def mlp_weights(after):
        (_, ag_up), (_, ag_down) = _exchange_wait(gathering["mlp"], after, "gather_mlp_d2d_wait")
        return ag_up, ag_down.reshape(D_FF, D_MODEL)

    sent = {}

    def mlp_grads(g_w_down, g_w_up):
        sent["mlp"], tok = _exchange_start(
            [g_w_down.reshape(N_DEV, D_FF // N_DEV, D_MODEL), g_w_up], [False, False], _ALL_PEERS, "grads_mlp_start")
        return tok

    def gmlp_grads(g_w_out, g_w_s):
        sent["gmlp"], tok = _exchange_start(
            [g_w_out.reshape(N_DEV, D_MODEL // N_DEV, D_MODEL), in_slot(g_w_s.astype(BF16))], [False, True], _ALL_PEERS,
            "grads_gmlp_start")
        return tok

    def in_grads(g_w_in_t, g_conv_w):
        g_in_blk = g_w_in_t[:IN_COLS].reshape(N_DEV, shard_in, D_MODEL)
        g_conv_blk = g_conv_w.reshape(4, N_DEV, CONV_CH // N_DEV).transpose(1, 0, 2)
        sent["in"], tok = _exchange_start([g_in_blk, g_conv_blk], [False, False], _ALL_PEERS, "grads_in_start")
        return tok

    def arrived_updates(after):
        (own_down, p_down), (own_up, p_up) = _exchange_wait(sent["mlp"], after, "grads_mlp_wait")
        (own_out, p_out), (_, p_ws) = _exchange_wait(sent["gmlp"], own_up, "grads_gmlp_wait")
        rows = lambda t: t.reshape(t.shape[:-3] + (N_HEADS * CHUNK, CHUNK))
        return [dict(parts=p_up, own=own_up, w=w_up[0], m=m_w_up[0], v=v_w_up[0]),
                dict(parts=p_down, own=own_down, w=w_down[0], m=m_w_down[0], v=v_w_down[0]),
                dict(parts=p_out, own=own_out, w=w_out[0], m=m_w_out[0], v=v_w_out[0]),
                dict(parts=rows(p_ws), own=rows(p_ws), w=rows(gm_w_s[0]), m=rows(m_gm_w_s[0]), v=rows(v_gm_w_s[0]),
                     mask=jnp.tril(jnp.ones((CHUNK, CHUNK), F32)))]

    small = {k: w[k][0] for k in _SMALL_PARAMS + ("gm_w_s",)}
    loss_part, grad_x, g = _local_step(
        x.reshape(n_batch * seq, D_MODEL), loss_target.reshape(n_batch * seq, D_MODEL), seq, small,
        dict(mixer_weights=mixer_weights, mixers_done=mixers_done, mlp_weights=mlp_weights, mlp_grads=mlp_grads,
             gmlp_grads=gmlp_grads, in_grads=in_grads, arrived_updates=arrived_updates, me=me,
             prenorm_after=second[1]), first_dep=tok_ici_1)

    sent_rows, tok_rows = _exchange_start([in_slot(_pack_slab(g, loss_part))], [True], _ALL_PEERS, "grads_rows_start")
    res = dict(zip(("w_up", "w_down", "w_out", "gm_w_s"), g["updates"]))
    (own_in, p_in), (own_conv, p_conv) = _exchange_wait(sent["in"], tok_rows, "grads_in_wait")
    res["w_in"] = tuple(r.T for r in _adamw_reduce(p_in, own_in, me, w_in_sh, m_in_sh, v_in_sh, shard_in, "adamw_w_in"))
    res["conv_w"] = _adamw_small(p_conv, own_conv, me, conv_w[0], m_conv_w[0], v_conv_w[0], None, "adamw_conv_w")
    ((_, p_rows),) = _exchange_wait(sent_rows, res["w_in"][1], "grads_rows_wait")
    flat = lambda t: t[0] if t.ndim == 3 else t
    small_res, loss = _adamw_slab(p_rows, *({k: flat(d[k]) for k in _SMALL_PARAMS} for d in (w, m, v)))
    res.update(small_res)
    res = {k: tuple(r.reshape(w[k].shape) for r in res[k]) for k in _WEIGHTS}

    outs = [loss, grad_x.reshape(x.shape)]
    for part in range(4):
        outs.extend(res[k][part] for k in _WEIGHTS)
    return tuple(outs)
```

```python
import functools

import jax
import jax.numpy as jnp
import numpy as np
from jax import lax
from jax.experimental import pallas as pl
from jax.experimental.pallas import tpu as pltpu

F32 = jnp.float32
BF16 = jnp.bfloat16

D_MODEL = 1024
GM_WIDTH = 512
SSM_WIDTH = 512
CONV_CH = 1024
N_HEADS = 8
HEAD_DIM = 64
N_STATE = 128
CHUNK = 128
D_FF = 4096
IN_COLS = 2568
IN_PAD = 2688
N_DEV = 8
EPS = 1e-6
ADAM_LR, ADAM_B1, ADAM_B2, ADAM_EPS, ADAM_WD, ADAM_STEP = 0.001, 0.9, 0.999, 1e-08, 0.01, 10
VMEM_LIMIT_BYTES = 56 * 1024 * 1024
TOKEN_TILE = 512
FF_TILE = 2048
WGRAD_TILE = 512

_NT = (((1,), (1,)), ((), ()))
_TN = (((0,), (0,)), ((), ()))


def _params(*sem):
    return pltpu.CompilerParams(dimension_semantics=sem or None, vmem_limit_bytes=VMEM_LIMIT_BYTES)


def _dot(a, b, dims=None):
    if dims is None:
        return jnp.dot(a, b, preferred_element_type=F32)
    return lax.dot_general(a, b, dims, preferred_element_type=F32)


def _split_terms(x, terms):
    out, rem = [], x
    for i in range(terms):
        hi = rem.astype(BF16)
        out.append(hi)
        if i + 1 < terms:
            rem = rem - hi.astype(F32)
    return out


def _split_dot(x, m, terms):
    acc = None
    for hi in _split_terms(x, terms):
        part = _dot(hi, m)
        acc = part if acc is None else acc + part
    return acc


def _split_dot_left(m, x, terms):
    acc = None
    for hi in _split_terms(x, terms):
        part = _dot(m, hi)
        acc = part if acc is None else acc + part
    return acc


def _gelu_and_grad(x):
    c = 0.7978845608028654
    inner = c * (x + 0.044715 * x * x * x)
    t = jnp.tanh(inner)
    g = 0.5 * x * (1.0 + t)
    dg = 0.5 * (1.0 + t) + 0.5 * x * (1.0 - t * t) * c * (1.0 + 3.0 * 0.044715 * x * x)
    return g, dg


def _softplus(x):
    return jnp.maximum(x, 0.0) + jnp.log(1.0 + jnp.exp(-jnp.abs(x)))


def _rsum(x):
    return jnp.sum(x, axis=0, keepdims=True)


def _acc_rows(ref, part, first):
    val = jnp.broadcast_to(part, ref.shape)

    @pl.when(first)
    def _():
        ref[...] = val

    @pl.when(jnp.logical_not(first))
    def _():
        ref[...] += val


def _rms_bwd(n, g, dout):
    r = lax.rsqrt(jnp.mean(n * n, axis=-1, keepdims=True) + EPS)
    nh = n * r
    dg = dout * g
    dn = r * (dg - nh * jnp.mean(dg * nh, axis=-1, keepdims=True))
    return dn, _rsum(dout * nh)


def _const_mats():
    avg = np.kron(np.eye(4), np.full((HEAD_DIM, HEAD_DIM), 1.0 / HEAD_DIM))
    expand = np.zeros((CHUNK, SSM_WIDTH), np.float32)
    for h in range(N_HEADS):
        expand[h, h * HEAD_DIM:(h + 1) * HEAD_DIM] = 1.0
    tril = np.tril(np.ones((CHUNK, CHUNK), np.float32))
    as_bf16 = lambda a: jnp.asarray(a, dtype=BF16)
    return as_bf16(avg), as_bf16(expand), as_bf16(expand.T), as_bf16(tril), as_bf16(tril.T)


def _full(shape):
    nd = len(shape)
    return pl.BlockSpec(shape, lambda *_: (0,) * nd)


_HBM = pl.BlockSpec(memory_space=pltpu.HBM)
_SEM = pl.BlockSpec(memory_space=pltpu.SEMAPHORE)
_ALL_PEERS = tuple((k, 0) for k in range(1, N_DEV))
_SAME_CORE_PEERS = ((2, 0), (4, 0), (6, 0))
_SIBLING_FORWARD = ((1, 0), (1, 2), (1, 4), (1, 6))


def _flip(j, k):
    for bit in (4, 2, 1):
        if k & bit:
            j = j + bit - 2 * (j & bit)
    return j


def _copies(src, land, send_sems, recv_sems, hops):
    x, y, c = lax.axis_index("x"), lax.axis_index("y"), lax.axis_index("c")
    me = 4 * x + 2 * y + c
    out = []
    for t in range(len(src)):
        for i, (k, b) in enumerate(hops):
            pos = (1 - x if k & 4 else x, 1 - y if k & 2 else y, 1 - c if k & 1 else c)
            peer = _flip(me, k)
            sem = t * len(hops) + i
            mk = functools.partial(pltpu.make_async_remote_copy, send_sem=send_sems.at[sem], recv_sem=recv_sems.at[sem],
                                   device_id=pos, device_id_type=pl.DeviceIdType.MESH)
            if land[t] is None and src[t].shape[0] != N_DEV:
                width = src[t].shape[1] // N_DEV
                slab = lambda j: src[t].at[:, pl.ds(pl.multiple_of(j * width, 128), width)]
                mine = functools.partial(mk, src_ref=slab(_flip(me, b)), dst_ref=slab(_flip(me, b)))
                theirs = functools.partial(mk, src_ref=slab(_flip(peer, b)), dst_ref=slab(_flip(peer, b)))
            elif land[t] is None:
                mine = functools.partial(mk, src_ref=src[t].at[_flip(me, b)], dst_ref=src[t].at[_flip(me, b)])
                theirs = functools.partial(mk, src_ref=src[t].at[_flip(peer, b)], dst_ref=src[t].at[_flip(peer, b)])
            else:
                assert b == 0
                mine = functools.partial(mk, src_ref=src[t].at[peer], dst_ref=land[t].at[me])
                theirs = functools.partial(mk, src_ref=src[t].at[peer], dst_ref=land[t].at[peer])
            out.append((mine, theirs))
    return out


def _exchange_start(srcs, inplace, peers, name, dep=None):
    n = len(srcs)
    lands = [None if ip else pltpu.with_memory_space_constraint(lax.empty(s.shape, s.dtype), pltpu.HBM)
             for s, ip in zip(srcs, inplace)]
    real_lands = [l for l in lands if l is not None]
    n_l = len(real_lands)
    deps = [] if dep is None else [dep]

    def body(*refs):
        src = refs[:n]
        land_refs = list(refs[n:n + n_l])
        send_sems, recv_sems = refs[n + n_l + len(deps)], refs[n + n_l + len(deps) + 1]
        token = refs[-1]
        land = [None if ip else land_refs.pop(0) for ip in inplace]
        for mine, _ in _copies(src, land, send_sems, recv_sems, peers):
            mine().start()
        token[...] = jnp.zeros_like(token)

    sem_t = pltpu.SemaphoreType.DMA((n * len(peers),))
    outs = pl.pallas_call(
        body, name=name,
        out_shape=(sem_t, sem_t) + tuple(pltpu.HBM(a.shape, a.dtype) for a in list(srcs) + real_lands)
        + (jax.ShapeDtypeStruct((8, 128), F32),),
        in_specs=[_HBM] * (n + n_l) + [pl.BlockSpec(memory_space=pl.ANY)] * len(deps),
        out_specs=(_SEM, _SEM) + (_HBM,) * (n + n_l) + (pl.BlockSpec(memory_space=pltpu.VMEM),),
        input_output_aliases={i: 2 + i for i in range(n + n_l)},
        compiler_params=pltpu.CompilerParams(has_side_effects=pltpu.SideEffectType.DATAFLOW_SIDE_EFFECTING),
    )(*[pltpu.with_memory_space_constraint(s, pltpu.HBM) for s in srcs], *real_lands, *deps)
    handle = dict(send=outs[0], recv=outs[1], srcs=outs[2:2 + n], lands=outs[2 + n:2 + n + n_l], inplace=inplace,
                  peers=peers)
    return handle, outs[-1]


def _exchange_wait(handle, after, name):
    srcs, lands, inplace, peers = handle["srcs"], handle["lands"], handle["inplace"], handle["peers"]
    n, n_l = len(srcs), len(lands)

    def body(*refs):
        src = refs[:n]
        land_refs = list(refs[n:n + n_l])
        send_sems, recv_sems = refs[n + n_l], refs[n + n_l + 1]
        land = [None if ip else land_refs.pop(0) for ip in inplace]
        for mine, theirs in _copies(src, land, send_sems, recv_sems, peers):
            mine().wait_send()
            theirs().wait_recv()

    outs = pl.pallas_call(
        body, name=name, out_shape=tuple(pltpu.HBM(a.shape, a.dtype) for a in list(srcs) + list(lands)),
        in_specs=[_HBM] * (n + n_l) + [_SEM, _SEM, pl.BlockSpec(memory_space=pl.ANY)],
        out_specs=(_HBM,) * (n + n_l), input_output_aliases={i: i for i in range(n + n_l)},
        compiler_params=pltpu.CompilerParams(has_side_effects=pltpu.SideEffectType.DATAFLOW_SIDE_EFFECTING),
    )(*srcs, *lands, handle["send"], handle["recv"], after)
    res, land_out = [], list(outs[n:])
    for t in range(n):
        res.append((outs[t], outs[t] if inplace[t] else land_out.pop(0)))
    return res


def _cast_to_slot(w, me, rows, name, cols=False, dep=None):
    r, cdim = w.shape
    deps = [] if dep is None else [dep]

    def body(me_ref, w_ref, *rest):
        o_ref = rest[-1]
        if cols:
            o_ref[...] = w_ref[...].astype(BF16)
        else:
            o_ref[0] = w_ref[...].astype(BF16)

    if cols:
        out_shape = jax.ShapeDtypeStruct((r, N_DEV * cdim), BF16)
        out_spec = pl.BlockSpec((rows, cdim), lambda i, me_ref: (i, me_ref[0]))
    else:
        out_shape = jax.ShapeDtypeStruct((N_DEV, r, cdim), BF16)
        out_spec = pl.BlockSpec((1, rows, cdim), lambda i, me_ref: (me_ref[0], i, 0))
    return pl.pallas_call(
        body, name=name, out_shape=out_shape,
        grid_spec=pltpu.PrefetchScalarGridSpec(
            num_scalar_prefetch=1, grid=(r // rows,),
            in_specs=[pl.BlockSpec((rows, cdim), lambda i, me_ref: (i, 0))]
            + [pl.BlockSpec(memory_space=pl.ANY)] * len(deps), out_specs=out_spec),
        compiler_params=_params("parallel"))(me, w, *deps)


def _adamw_math(w, g, m, v):
    m = ADAM_B1 * m + (1.0 - ADAM_B1) * g
    v = ADAM_B2 * v + (1.0 - ADAM_B2) * (g * g)
    m_hat = m / (1.0 - ADAM_B1 ** ADAM_STEP)
    v_hat = v / (1.0 - ADAM_B2 ** ADAM_STEP)
    delta = -ADAM_LR * (m_hat / (jnp.sqrt(v_hat) + ADAM_EPS) + ADAM_WD * w)
    return delta, m, v


def _sum_parts(me, p_ref, own):
    g = None
    for j in range(N_DEV):
        term = (p_ref[j] if own is None else jnp.where(me == j, own, p_ref[j])).astype(F32)
        g = term if g is None else g + term
    return g


def _adamw_reduce(parts, own, me, w, m, v, rows, name):
    r, cdim = w.shape

    def body(me_ref, p_ref, own_ref, w_ref, m_ref, v_ref, g_out, d_out, m_out, v_out):
        g = _sum_parts(me_ref[0], p_ref, own_ref[0])
        d, mn, vn = _adamw_math(w_ref[...], g, m_ref[...], v_ref[...])
        g_out[...] = g
        d_out[...] = d
        m_out[...] = mn
        v_out[...] = vn

    blk = pl.BlockSpec((rows, cdim), lambda i, me_ref: (i, 0))
    sds = jax.ShapeDtypeStruct(w.shape, F32)
    return pl.pallas_call(
        body, name=name, out_shape=(sds,) * 4,
        grid_spec=pltpu.PrefetchScalarGridSpec(
            num_scalar_prefetch=1, grid=(r // rows,),
            in_specs=[pl.BlockSpec((N_DEV, rows, cdim), lambda i, me_ref: (0, i, 0)),
                      pl.BlockSpec((1, rows, cdim), lambda i, me_ref: (me_ref[0], i, 0)), blk, blk, blk],
            out_specs=(blk,) * 4),
        compiler_params=_params("parallel"))(me, parts, own, w, m, v)


_IN_SPLITS = ((0, 512), (512, 1024), (1024, 1536), (1536, 2560), (2560, IN_PAD))


def _prenorm(x, g1, tm, dep=None):
    t_tok = x.shape[0]
    deps = [] if dep is None else [dep]

    def body(x_ref, g_ref, *rest):
        xv = x_ref[...]
        r = lax.rsqrt(jnp.mean(xv * xv, axis=-1, keepdims=True) + EPS)
        rest[-1][...] = (xv * r * g_ref[...]).astype(BF16)

    row = pl.BlockSpec((tm, D_MODEL), lambda i: (i, 0))
    return pl.pallas_call(
        body, name="prenorm", grid=(t_tok // tm,), out_shape=jax.ShapeDtypeStruct((t_tok, D_MODEL), BF16),
        in_specs=[row, _full((1, D_MODEL))] + [pl.BlockSpec(memory_space=pl.ANY)] * len(deps), out_specs=row,
        compiler_params=_params("parallel"))(x, g1, *deps)


def _in_proj(h1, w_in, tm):
    t_tok = h1.shape[0]

    def body(h_ref, w_ref, *outs):
        h = h_ref[...]
        for (a, b), o_ref in zip(_IN_SPLITS, outs):
            o_ref[...] = _dot(h, w_ref[a:b, :], _NT).astype(o_ref.dtype)

    row = lambda n: pl.BlockSpec((tm, n), lambda i: (i, 0))
    widths = [b - a for a, b in _IN_SPLITS]
    dtypes = (BF16, BF16, BF16, F32, F32)
    return pl.pallas_call(
        body, name="in_proj", grid=(t_tok // tm,),
        out_shape=tuple(jax.ShapeDtypeStruct((t_tok, n), dt) for n, dt in zip(widths, dtypes)),
        in_specs=[row(D_MODEL), _full((IN_PAD, D_MODEL))], out_specs=tuple(row(n) for n in widths),
        compiler_params=_params("parallel"))(h1, w_in)


def _lane_masks():
    lane = lax.broadcasted_iota(jnp.int32, (1, 2 * HEAD_DIM), 1)
    left = (lane < HEAD_DIM).astype(F32)
    return left, 1.0 - left


def _stack_pair(v, m_l, m_r):
    return jnp.concatenate([v * m_l, v * m_r], axis=0).astype(BF16)


def _head_mean(x, avg):
    n = avg.shape[0]
    return jnp.concatenate([_split_dot(x[:, n * i:n * (i + 1)], avg, 2) for i in range(x.shape[1] // n)], axis=1)


def _gmlp_common(u, v, lnw, lnb, avg, wcat_ref, bias, m_l, m_r):
    ug, dug = _gelu_and_grad(u)
    vg, dvg = _gelu_and_grad(v)
    mu = _head_mean(vg, avg)
    vc = vg - mu
    var = _head_mean(vc * vc, avg)
    rstd = lax.rsqrt(var + EPS)
    vhat = vc * rstd
    vn = vhat * lnw + lnb
    rows = []
    for r in range(u.shape[0] // CHUNK):
        cols = []
        for j in range(N_HEADS // 2):
            pair = vn[CHUNK * r:CHUNK * (r + 1), 128 * j:128 * (j + 1)]
            cols.append(_dot(wcat_ref[j], _stack_pair(pair, m_l, m_r)))
        rows.append(jnp.concatenate(cols, axis=1) + bias)
    mixed = jnp.concatenate(rows, axis=0)
    return ug, dug, dvg, rstd, vhat, vn, mixed


_GMLP_ROWS = 4 * CHUNK


def _gmlp_fwd(u, v, lnw, lnb, wcat, bias, avg):
    t_tok = u.shape[0]
    tm = min(_GMLP_ROWS, t_tok)

    def body(u_ref, v_ref, lnw_ref, lnb_ref, wcat_ref, bias_ref, avg_ref, o_ref):
        m_l, m_r = _lane_masks()
        ug, _, _, _, _, _, mixed = _gmlp_common(
            u_ref[...].astype(F32), v_ref[...].astype(F32), lnw_ref[...], lnb_ref[...], avg_ref[...], wcat_ref,
            bias_ref[...], m_l, m_r)
        o_ref[...] = (ug * mixed).astype(BF16)

    row = pl.BlockSpec((tm, GM_WIDTH), lambda i: (i, 0))
    return pl.pallas_call(
        body, name="gmlp_fwd", grid=(t_tok // tm,), out_shape=jax.ShapeDtypeStruct((t_tok, GM_WIDTH), BF16),
        in_specs=[row, row, _full((1, GM_WIDTH)), _full((1, GM_WIDTH)), _full(wcat.shape), _full(bias.shape),
                  _full(avg.shape)],
        out_specs=row, compiler_params=_params("parallel"))(u, v, lnw, lnb, wcat, bias, avg)


def _shift_rows(x, edge, j, down):
    groups, cols = x.shape[0] // 8, x.shape[1]
    amount = j if down else 8 - j
    rot = pltpu.roll(x.reshape(groups, 8, cols), amount, axis=1)
    edge_rot = pltpu.roll(edge, amount, axis=0)[None]
    sub = lax.broadcasted_iota(jnp.int32, (1, 8, 1), 1)
    if down:
        out = jnp.where(sub < j, jnp.concatenate([edge_rot, rot[:-1]], axis=0), rot)
    else:
        out = jnp.where(sub < 8 - j, rot, jnp.concatenate([rot[1:], edge_rot], axis=0))
    return out.reshape(x.shape)


def _conv_pre(xbc, tail, cw_ref, cb):
    taps = [_shift_rows(xbc, tail, 3 - k, True) for k in range(3)] + [xbc]
    return cb + cw_ref[0:1, :] * taps[0] + cw_ref[1:2, :] * taps[1] + cw_ref[2:3, :] * taps[2] + cw_ref[3:4, :] * taps[3]


def _ssd_common(pre, dtr, dtb, alog, expand, tril):
    q = CHUNK
    sg = jax.nn.sigmoid(pre)
    act = pre * sg
    lane = lax.broadcasted_iota(jnp.int32, (1, CHUNK), 1)
    a_row = jnp.where(lane < N_HEADS, -jnp.exp(alog), 0.0)
    dtp = dtr + dtb
    dt = _softplus(dtp)
    a_cs = _split_dot_left(tril, dt * a_row, 3)
    a_cs_t = a_cs.T
    dt_exp = _split_dot(dt, expand, 3)
    a_exp = _split_dot(a_cs, expand, 3)
    a_end = a_exp[q - 1:q, :]
    li = lax.broadcasted_iota(jnp.int32, (q, q), 0)
    si = lax.broadcasted_iota(jnp.int32, (q, q), 1)
    causal = si <= li
    decay = []
    for h in range(N_HEADS):
        seg = a_cs[:, h:h + 1] - a_cs_t[h:h + 1, :]
        decay.append(jnp.where(causal, jnp.exp(jnp.minimum(seg, 0.0)), 0.0))
    return dict(pre=pre, sg=sg, act=act, a_row=a_row, dtp=dtp, dt=dt, dt_exp=dt_exp, a_exp=a_exp,
                e=jnp.exp(a_exp), w_end=jnp.exp(a_end - a_exp), cd=jnp.exp(a_end), decay=decay)


def _ssd_specs(t_tok, seq, reverse):
    nb, nc = t_tok // seq, seq // CHUNK

    def chunk(c):
        return nc - 1 - c if reverse else c

    def row(n, col=0):
        return pl.BlockSpec((nb, CHUNK, n), lambda c: (0, chunk(c), col))

    tail = pl.BlockSpec((nb, 8, CONV_CH), lambda c: (0, jnp.maximum(chunk(c) * (CHUNK // 8) - 1, 0), 0))
    states = pl.BlockSpec((nb, 1, N_STATE, SSM_WIDTH), lambda c: (0, chunk(c), 0, 0))
    fold = lambda a: a.reshape(nb, seq, a.shape[-1])
    unfold = lambda a: a.reshape(t_tok, a.shape[-1])
    return nb, nc, row, tail, states, fold, unfold


def _ssd_fwd(z, xbc, dtr, cw, cb, dtb, alog, dskip_exp, nw, expand, tril, seq):
    t_tok = z.shape[0]
    nb, nc, row, tail, states_spec, fold, unfold = _ssd_specs(t_tok, seq, False)

    def body(z_ref, xbc_ref, tail_ref, dtr_ref, cw_ref, cb_ref, dtb_ref, alog_ref, dsk_ref, nw_ref, exp_ref,
             tril_ref, o_ref, y_ref, st_ref, pre_ref, state_ref):
        c = pl.program_id(0)

        @pl.when(c == 0)
        def _():
            state_ref[...] = jnp.zeros_like(state_ref)

        m_l, m_r = _lane_masks()
        for s in range(nb):
            pre = _conv_pre(xbc_ref[s], jnp.where(c == 0, 0.0, tail_ref[s]), cw_ref, cb_ref[...])
            pre_ref[s] = pre
            f = _ssd_common(pre, dtr_ref[s], dtb_ref[...], alog_ref[...], exp_ref[...], tril_ref[...])
            act = f["act"]
            xs = act[:, :SSM_WIDTH]
            xdt = xs * f["dt_exp"]
            xw = xdt * f["w_end"]
            state = state_ref[s]
            st_ref[s, 0] = state
            ydiag, yoff, snew = [], [], []
            for g in range(2):
                bg = act[:, 512 + 128 * g:640 + 128 * g].astype(BF16)
                cg = act[:, 768 + 128 * g:896 + 128 * g].astype(BF16)
                cb_mat = _dot(cg, bg, _NT)
                for pr in range(2):
                    h0 = 4 * g + 2 * pr
                    gcat = jnp.concatenate(
                        [(cb_mat * f["decay"][h0]).astype(BF16), (cb_mat * f["decay"][h0 + 1]).astype(BF16)], axis=1)
                    ydiag.append(_dot(gcat, _stack_pair(xdt[:, 64 * h0:64 * h0 + 128], m_l, m_r)))
                yoff.append(_dot(cg, state[:, 256 * g:256 * (g + 1)].astype(BF16)))
                snew.append(_dot(bg, xw[:, 256 * g:256 * (g + 1)].astype(BF16), _TN))
            y = jnp.concatenate(ydiag, axis=1) + f["e"] * jnp.concatenate(yoff, axis=1) + dsk_ref[...] * xs
            state_ref[s] = state * f["cd"] + jnp.concatenate(snew, axis=1)
            y_ref[s] = y
            zv = z_ref[s].astype(F32)
            yg = y * (zv * jax.nn.sigmoid(zv))
            outs = []
            for g in range(2):
                ygg = yg[:, 256 * g:256 * (g + 1)]
                outs.append(ygg * lax.rsqrt(jnp.mean(ygg * ygg, axis=-1, keepdims=True) + EPS))
            o_ref[s] = (jnp.concatenate(outs, axis=1) * nw_ref[...]).astype(BF16)

    consts = [cw, cb, dtb, alog, dskip_exp, nw, expand, tril]
    sd = lambda n, dt: jax.ShapeDtypeStruct((nb, seq, n), dt)
    o, y, states, pre = pl.pallas_call(
        body, name="ssd_fwd", grid=(nc,),
        out_shape=(sd(SSM_WIDTH, BF16), sd(SSM_WIDTH, F32), jax.ShapeDtypeStruct((nb, nc, N_STATE, SSM_WIDTH), F32),
                   sd(CONV_CH, F32)),
        in_specs=[row(SSM_WIDTH), row(CONV_CH), tail, row(CHUNK)] + [_full(a.shape) for a in consts],
        out_specs=(row(SSM_WIDTH), row(SSM_WIDTH), states_spec, row(CONV_CH)),
        scratch_shapes=[pltpu.VMEM((nb, N_STATE, SSM_WIDTH), F32)],
        compiler_params=_params("arbitrary"))(fold(z), fold(xbc), fold(xbc), fold(dtr), *consts)
    return unfold(o), unfold(y), states, unfold(pre)


def _out_proj(mix_a, mix_b, w_out, x, g2, g3, tm, dep=None):
    t_tok = x.shape[0]
    deps = [] if dep is None else [dep]

    def body(a_ref, b_ref, w_ref, x_ref, g2_ref, g3_ref, *rest):
        o_ref, x2_ref, h3_ref = rest[-3:]
        o = _dot(a_ref[...], w_ref[0:GM_WIDTH, :]) + _dot(b_ref[...], w_ref[GM_WIDTH:, :])
        o_ref[...] = o
        r2 = lax.rsqrt(jnp.mean(o * o, axis=-1, keepdims=True) + EPS)
        x2 = x_ref[...] + o * r2 * g2_ref[...]
        x2_ref[...] = x2
        r3 = lax.rsqrt(jnp.mean(x2 * x2, axis=-1, keepdims=True) + EPS)
        h3_ref[...] = (x2 * r3 * g3_ref[...]).astype(BF16)

    row = lambda n: pl.BlockSpec((tm, n), lambda i: (i, 0))
    sd = lambda dt: jax.ShapeDtypeStruct((t_tok, D_MODEL), dt)
    return pl.pallas_call(
        body, name="out_proj", grid=(t_tok // tm,), out_shape=(sd(F32), sd(F32), sd(BF16)),
        in_specs=[row(GM_WIDTH), row(SSM_WIDTH), _full((D_MODEL, D_MODEL)), row(D_MODEL), _full((1, D_MODEL)),
                  _full((1, D_MODEL))] + [pl.BlockSpec(memory_space=pl.ANY)] * len(deps),
        out_specs=(row(D_MODEL),) * 3, compiler_params=_params("parallel"))(mix_a, mix_b, w_out, x, g2, g3, *deps)


def _mlp_fwd(h3, w_up, w_down, x2, target, g4, tm, tf):
    t_tok = x2.shape[0]

    def up_body(h_ref, wu_ref, ra_ref):
        ra_ref[...] = jnp.maximum(_dot(h_ref[...], wu_ref[...]), 0.0).astype(BF16)

    tu = min(2 * tm, t_tok)
    ra = pl.pallas_call(
        up_body, name="mlp_up", grid=(D_FF // tf, t_tok // tu), out_shape=jax.ShapeDtypeStruct((t_tok, D_FF), BF16),
        in_specs=[pl.BlockSpec((tu, D_MODEL), lambda j, i: (i, 0)), pl.BlockSpec((D_MODEL, tf), lambda j, i: (0, j))],
        out_specs=pl.BlockSpec((tu, tf), lambda j, i: (i, j)), compiler_params=_params("parallel", "parallel"))(h3, w_up)

    def down_body(ra_ref, wd_ref, x2_ref, t_ref, g4_ref, dd_ref, dy_ref, dg4_ref, loss_ref):
        i = pl.program_id(0)
        rav = ra_ref[...]
        dvec = _dot(rav * rav, wd_ref[...])
        r4 = lax.rsqrt(jnp.mean(dvec * dvec, axis=-1, keepdims=True) + EPS)
        dn = dvec * r4
        g4 = g4_ref[...]
        err = x2_ref[...] + dn * g4 - t_ref[...]
        dy = err * (1.0 / D_MODEL)
        dy_ref[...] = dy
        dg = dy * g4
        dd_ref[...] = (r4 * (dg - dn * jnp.mean(dg * dn, axis=-1, keepdims=True))).astype(BF16)
        _acc_rows(dg4_ref, _rsum(dy * dn), i == 0)
        tile_loss = 0.5 * jnp.sum(jnp.sum(err * err, axis=-1, keepdims=True), axis=0, keepdims=True) / D_MODEL
        _acc_rows(loss_ref, jnp.broadcast_to(tile_loss, (1, 128)), i == 0)

    row = pl.BlockSpec((tm, D_MODEL), lambda i: (i, 0))
    dd, dy, dg4, loss = pl.pallas_call(
        down_body, name="mlp_down", grid=(t_tok // tm,),
        out_shape=(jax.ShapeDtypeStruct((t_tok, D_MODEL), BF16), jax.ShapeDtypeStruct((t_tok, D_MODEL), F32),
                   jax.ShapeDtypeStruct((1, D_MODEL), F32), jax.ShapeDtypeStruct((1, 128), F32)),
        in_specs=[pl.BlockSpec((tm, D_FF), lambda i: (i, 0)), _full((D_FF, D_MODEL)), row, row, _full((1, D_MODEL))],
        out_specs=(row, row, _full((1, D_MODEL)), _full((1, 128))),
        compiler_params=_params("arbitrary"))(ra, w_down, x2, target, g4)
    return ra, dd, dy, dg4, loss


def _mlp_bwd(dd, w_down, ra, w_up, x2, dy, o, g3, g2, tm, tf):
    t_tok = x2.shape[0]

    def hidden_body(dd_ref, wd_ref, ra_ref, da_ref):
        df = _dot(dd_ref[...], wd_ref[...], _NT)
        da_ref[...] = (df * (2.0 * ra_ref[...].astype(F32))).astype(BF16)

    tu = min(2 * tm, t_tok)
    da = pl.pallas_call(
        hidden_body, name="mlp_bwd_hidden", grid=(D_FF // tf, t_tok // tu),
        out_shape=jax.ShapeDtypeStruct((t_tok, D_FF), BF16),
        in_specs=[pl.BlockSpec((tu, D_MODEL), lambda j, i: (i, 0)), pl.BlockSpec((tf, D_MODEL), lambda j, i: (j, 0)),
                  pl.BlockSpec((tu, tf), lambda j, i: (i, j))],
        out_specs=pl.BlockSpec((tu, tf), lambda j, i: (i, j)),
        compiler_params=_params("parallel", "parallel"))(dd, w_down, ra)

    def in_body(da_ref, wu_ref, x2_ref, dy_ref, o_ref, g3_ref, g2_ref, dx2_ref, do_ref, dg3_ref, dg2_ref):
        i = pl.program_id(0)
        dh3 = _dot(da_ref[...], wu_ref[...], _NT)
        dn3, dg3 = _rms_bwd(x2_ref[...], g3_ref[...], dh3)
        dx2 = dy_ref[...] + dn3
        dx2_ref[...] = dx2
        do, dg2 = _rms_bwd(o_ref[...], g2_ref[...], dx2)
        do_ref[...] = do.astype(BF16)
        _acc_rows(dg3_ref, dg3, i == 0)
        _acc_rows(dg2_ref, dg2, i == 0)

    row = pl.BlockSpec((tm, D_MODEL), lambda i: (i, 0))
    vec = _full((1, D_MODEL))
    sd = lambda dt: jax.ShapeDtypeStruct((t_tok, D_MODEL), dt)
    dx2, do, dg3, dg2 = pl.pallas_call(
        in_body, name="mlp_bwd_in", grid=(t_tok // tm,),
        out_shape=(sd(F32), sd(BF16), jax.ShapeDtypeStruct((1, D_MODEL), F32), jax.ShapeDtypeStruct((1, D_MODEL), F32)),
        in_specs=[pl.BlockSpec((tm, D_FF), lambda i: (i, 0)), _full((D_MODEL, D_FF)), row, row, row, vec, vec],
        out_specs=(row, row, vec, vec), compiler_params=_params("arbitrary"))(da, w_up, x2, dy, o, g3, g2)
    return da, dx2, do, dg3, dg2


def _wgrad(a, b, out_blocks, bm, bn, bk, square_a, name, dep=None):
    t_tok, m = a.shape
    n = b.shape[1]
    nk = t_tok // bk

    def body(a_ref, b_ref, *rest):
        o_ref, acc_ref = rest[-2:]
        k = pl.program_id(2)
        av = a_ref[...]
        if square_a:
            av = av * av
        part = _dot(av, b_ref[...], _TN)

        def emit(res):
            if out_blocks is None:
                o_ref[...] = res.astype(BF16)
            else:
                o_ref[0] = res.astype(BF16)

        if nk == 1:
            emit(part)
            return

        @pl.when(k == 0)
        def _():
            acc_ref[...] = part

        @pl.when(k > 0)
        def _():
            acc_ref[...] += part

        @pl.when(k == nk - 1)
        def _():
            emit(acc_ref[...])

    if out_blocks is None:
        out_shape = jax.ShapeDtypeStruct((m, n), BF16)
        out_spec = pl.BlockSpec((bm, bn), lambda i, j, k: (i, j))
    else:
        assert n // out_blocks == bn
        out_shape = jax.ShapeDtypeStruct((out_blocks, m, bn), BF16)
        out_spec = pl.BlockSpec((1, bm, bn), lambda i, j, k: (j, i, 0))
    deps = [] if dep is None else [dep]
    return pl.pallas_call(
        body, name=name, grid=(m // bm, n // bn, nk), out_shape=out_shape,
        in_specs=[pl.BlockSpec((bk, bm), lambda i, j, k: (k, i)), pl.BlockSpec((bk, bn), lambda i, j, k: (k, j))]
        + [pl.BlockSpec(memory_space=pl.ANY)] * len(deps),
        out_specs=out_spec, scratch_shapes=[pltpu.VMEM((bm, bn) if nk > 1 else (8, 128), F32)],
        compiler_params=_params("parallel", "parallel", "arbitrary"))(a, b, *deps)


def _wgrad_in(h1, pieces, bn, name, dep=None):
    t_tok = h1.shape[0]
    widths = [p.shape[1] for p in pieces]
    starts = [sum(widths[:i]) for i in range(len(widths))]

    def body(h_ref, *rest):
        piece_refs = rest[:len(widths)]
        o_ref = rest[-1]
        hv = h_ref[...]
        for a, n, r in zip(starts, widths, piece_refs):
            o_ref[a:a + n, :] = _dot(r[...], hv, _TN).astype(BF16)

    deps = [] if dep is None else [dep]
    return pl.pallas_call(
        body, name=name, grid=(D_MODEL // bn,), out_shape=jax.ShapeDtypeStruct((sum(widths), D_MODEL), BF16),
        in_specs=[pl.BlockSpec((t_tok, bn), lambda j: (0, j))] + [pl.BlockSpec((t_tok, n), lambda j: (0, 0)) for n in widths]
        + [pl.BlockSpec(memory_space=pl.ANY)] * len(deps),
        out_specs=pl.BlockSpec((sum(widths), bn), lambda j: (0, j)),
        compiler_params=_params("parallel"))(h1, *pieces, *deps)


def _dmix(do, w_out, tm, dep=None):
    t_tok = do.shape[0]

    def body(d_ref, w_ref, *rest):
        rest[-1][...] = _dot(d_ref[...], w_ref[...], _NT).astype(BF16)

    row = pl.BlockSpec((tm, D_MODEL), lambda i: (i, 0))
    deps = [] if dep is None else [dep]
    return pl.pallas_call(
        body, name="dmix", grid=(t_tok // tm,), out_shape=jax.ShapeDtypeStruct((t_tok, D_MODEL), BF16),
        in_specs=[row, _full((D_MODEL, D_MODEL))] + [pl.BlockSpec(memory_space=pl.ANY)] * len(deps), out_specs=row,
        compiler_params=_params("parallel"))(do, w_out, *deps)


def _gmlp_bwd(dmix, u, v, lnw, lnb, wcat, wtcat, bias, avg, expand_t):
    t_tok = u.shape[0]
    tm = min(_GMLP_ROWS, t_tok)

    def body(dm_ref, u_ref, v_ref, lnw_ref, lnb_ref, wcat_ref, wtcat_ref, bias_ref, avg_ref, expt_ref, du_ref, dv_ref,
             dw_ref, db_ref, dlnw_ref, dlnb_ref):
        i = pl.program_id(0)
        m_l, m_r = _lane_masks()
        avg = avg_ref[...]
        lnw = lnw_ref[...]
        ug, dug, dvg, rstd, vhat, vn, mixed = _gmlp_common(
            u_ref[...].astype(F32), v_ref[...].astype(F32), lnw, lnb_ref[...], avg, wcat_ref, bias_ref[...], m_l, m_r)
        dya = dm_ref[...].astype(F32)
        du_ref[...] = (dya * mixed * dug).astype(BF16)
        dmixed = dya * ug
        dvn_rows, dws, dbt = [], [None] * N_HEADS, None
        for r in range(tm // CHUNK):
            dvn_cols = []
            for j in range(N_HEADS // 2):
                dmp = dmixed[CHUNK * r:CHUNK * (r + 1), 128 * j:128 * (j + 1)]
                dvn_cols.append(_dot(wtcat_ref[j], _stack_pair(dmp, m_l, m_r)))
                vnp = vn[CHUNK * r:CHUNK * (r + 1), 128 * j:128 * (j + 1)].astype(BF16)
                for i_h, mask in enumerate((m_l, m_r)):
                    part = _dot((dmp * mask).astype(BF16), vnp, _NT)
                    dws[2 * j + i_h] = part if r == 0 else dws[2 * j + i_h] + part
            dvn_rows.append(jnp.concatenate(dvn_cols, axis=1))
            part = _split_dot(dmixed[CHUNK * r:CHUNK * (r + 1), :], expt_ref[...], 2)
            dbt = part if r == 0 else dbt + part
        dvn = jnp.concatenate(dvn_rows, axis=0)
        dvh = dvn * lnw
        dvgel = rstd * (dvh - _head_mean(dvh, avg) - vhat * _head_mean(dvh * vhat, avg))
        dv_ref[...] = (dvgel * dvg).astype(BF16)
        first = i == 0

        @pl.when(first)
        def _():
            for h in range(N_HEADS):
                dw_ref[h] = dws[h]
            db_ref[...] = dbt

        @pl.when(jnp.logical_not(first))
        def _():
            for h in range(N_HEADS):
                dw_ref[h] += dws[h]
            db_ref[...] += dbt

        _acc_rows(dlnw_ref, _rsum(dvn * vhat), first)
        _acc_rows(dlnb_ref, _rsum(dvn), first)

    row = pl.BlockSpec((tm, GM_WIDTH), lambda i: (i, 0))
    consts = [lnw, lnb, wcat, wtcat, bias, avg, expand_t]
    return pl.pallas_call(
        body, name="gmlp_bwd", grid=(t_tok // tm,),
        out_shape=(jax.ShapeDtypeStruct((t_tok, GM_WIDTH), BF16), jax.ShapeDtypeStruct((t_tok, GM_WIDTH), BF16),
                   jax.ShapeDtypeStruct((N_HEADS, CHUNK, CHUNK), F32), jax.ShapeDtypeStruct((CHUNK, CHUNK), F32),
                   jax.ShapeDtypeStruct((1, GM_WIDTH), F32), jax.ShapeDtypeStruct((1, GM_WIDTH), F32)),
        in_specs=[row, row, row] + [_full(a.shape) for a in consts],
        out_specs=(row, row, _full((N_HEADS, CHUNK, CHUNK)), _full((CHUNK, CHUNK)), _full((1, GM_WIDTH)),
                   _full((1, GM_WIDTH))),
        compiler_params=_params("arbitrary"))(dmix, u, v, *consts)


def _ssd_bwd(dmix, z, xbc, pre, dtr, y, states, cw, cb, dtb, alog, dskip_exp, nw, expand, expand_t, tril, triu, seq,
             dep=None):
    t_tok = z.shape[0]
    nb, nc, row, _, states_spec, fold, unfold = _ssd_specs(t_tok, seq, True)
    q = CHUNK

    def one_sequence(s, dm_ref, z_ref, xbc_ref, pre_ref, dtr_ref, y_ref, st_ref, cw_ref, dtb_ref, alog_ref, dsk_ref,
                     nw_ref, exp_ref, expt_ref, tril_ref, triu_ref, dz_ref, dxbc_ref, ddt_ref, dhead_ref, dstate_ref):
        m_l, m_r = _lane_masks()
        expt = expt_ref[...]
        f = _ssd_common(pre_ref[s], dtr_ref[s], dtb_ref[...], alog_ref[...], exp_ref[...], tril_ref[...])
        act, pre, sg = f["act"], f["pre"], f["sg"]
        xs = act[:, :SSM_WIDTH]
        xdt = xs * f["dt_exp"]
        xw = xdt * f["w_end"]
        state = st_ref[s, 0]
        dstate = dstate_ref[s]
        zv, yv, dout, nw = z_ref[s].astype(F32), y_ref[s], dm_ref[s].astype(F32), nw_ref[...]
        sz = jax.nn.sigmoid(zv)
        sl = zv * sz
        yg = yv * sl
        tv = dout * nw
        dyg_parts, ygh_parts = [], []
        for g in range(2):
            ygg = yg[:, 256 * g:256 * (g + 1)]
            rr = lax.rsqrt(jnp.mean(ygg * ygg, axis=-1, keepdims=True) + EPS)
            ygh = ygg * rr
            tg = tv[:, 256 * g:256 * (g + 1)]
            dyg_parts.append(rr * (tg - ygh * jnp.mean(tg * ygh, axis=-1, keepdims=True)))
            ygh_parts.append(ygh)
        dyg = jnp.concatenate(dyg_parts, axis=1)
        dnw = _rsum(dout * jnp.concatenate(ygh_parts, axis=1))
        dy = dyg * sl
        dz_ref[s] = (dyg * yv * (sz * (1.0 + zv * (1.0 - sz)))).astype(BF16)
        ddsk = _rsum(dy * xs)
        dye = dy * f["e"]
        lane = lax.broadcasted_iota(jnp.int32, (q, q), 1)
        sub = lax.broadcasted_iota(jnp.int32, (q, q), 0)
        rs_mat = jnp.zeros((q, q), F32)
        cs_mat = jnp.zeros((q, q), F32)
        dxdt_cols, yoff, dst_in, dxw, d_b, d_c = [], [], [], [], [], []
        for g in range(2):
            bg = act[:, 512 + 128 * g:640 + 128 * g].astype(BF16)
            cg = act[:, 768 + 128 * g:896 + 128 * g].astype(BF16)
            cb_mat = _dot(cg, bg, _NT)
            stg = state[:, 256 * g:256 * (g + 1)].astype(BF16)
            dyeg = dye[:, 256 * g:256 * (g + 1)].astype(BF16)
            yoff.append(_dot(cg, stg))
            dcg = _dot(dyeg, stg, _NT)
            dst_in.append(_dot(cg, dyeg, _TN))
            dcb = jnp.zeros((q, q), F32)
            for pr in range(2):
                h0 = 4 * g + 2 * pr
                gf = [cb_mat * f["decay"][h0], cb_mat * f["decay"][h0 + 1]]
                gcat = jnp.concatenate([gf[0].astype(BF16), gf[1].astype(BF16)], axis=1)
                xst = _stack_pair(xdt[:, 64 * h0:64 * h0 + 128], m_l, m_r)
                dyp = dy[:, 64 * h0:64 * h0 + 128].astype(BF16)
                dgcat = _dot(dyp, xst, _NT)
                dxst = _dot(gcat, dyp, _TN)
                dxdt_cols.append(dxst[:q] * m_l + dxst[q:] * m_r)
                for i in range(2):
                    h = h0 + i
                    dg = dgcat[:, q * i:q * (i + 1)]
                    mm = dg * gf[i]
                    rs_mat = rs_mat + jnp.where(lane == h, jnp.sum(mm, axis=1, keepdims=True), 0.0)
                    cs_mat = cs_mat + jnp.where(sub == h, jnp.sum(mm, axis=0, keepdims=True), 0.0)
                    dcb = dcb + dg * f["decay"][h]
            dcb16 = dcb.astype(BF16)
            dstg = dstate[:, 256 * g:256 * (g + 1)].astype(BF16)
            d_c.append(dcg + _dot(dcb16, bg))
            dxw.append(_dot(bg, dstg))
            d_b.append(_dot(dcb16, cg, _TN) + _dot(xw[:, 256 * g:256 * (g + 1)].astype(BF16), dstg, _NT))
        dxw = jnp.concatenate(dxw, axis=1)
        dxdt = jnp.concatenate(dxdt_cols, axis=1) + dxw * f["w_end"]
        qv = dxw * xw
        end_row = _rsum(qv) + _rsum(dstate * state) * f["cd"]
        x2 = dye * jnp.concatenate(yoff, axis=1) - qv
        row_i = lax.broadcasted_iota(jnp.int32, (q, 1), 0)
        x2 = x2 + jnp.where(row_i == q - 1, end_row, 0.0)
        da_cs = _split_dot(x2, expt, 2) + rs_mat - cs_mat.T
        ddt = _split_dot(dxdt * xs, expt, 2)
        dxs = dsk_ref[...] * dy + dxdt * f["dt_exp"]
        dda = _split_dot_left(triu_ref[...], da_cs, 3)
        ddt = ddt + dda * f["a_row"]
        dalog = _rsum(dda * f["dt"]) * f["a_row"]
        draw = ddt * jax.nn.sigmoid(f["dtp"])
        ddt_ref[s] = draw.astype(BF16)
        dact = jnp.concatenate([dxs] + d_b + d_c, axis=1)
        dpre = dact * (sg * (1.0 + pre * (1.0 - sg)))
        dhead = dhead_ref[s]
        xv = xbc_ref[s]
        shifted = [_shift_rows(dpre, dhead, 3 - k, False) for k in range(3)] + [dpre]
        dxbc = cw_ref[3:4, :] * dpre
        for k in range(3):
            dxbc = dxbc + cw_ref[k:k + 1, :] * shifted[k]
        dxbc_ref[s] = dxbc.astype(BF16)
        dhead_ref[s] = dpre[0:8, :]
        dstate_ref[s] = dstate * f["cd"] + jnp.concatenate(dst_in, axis=1)
        row8 = lax.broadcasted_iota(jnp.int32, (8, 1), 0)
        dcw = jnp.zeros((8, CONV_CH), F32)
        for k in range(4):
            dcw = dcw + jnp.where(row8 == k, _rsum(shifted[k] * xv), 0.0)
        return dcw, _rsum(dpre), _rsum(draw), dalog, _split_dot(ddsk, expt, 3), dnw

    def body(dm_ref, z_ref, xbc_ref, pre_ref, dtr_ref, y_ref, st_ref, cw_ref, cb_ref, dtb_ref, alog_ref, dsk_ref,
             nw_ref, exp_ref, expt_ref, tril_ref, triu_ref, dz_ref, dxbc_ref, ddt_ref, dcw_ref, dcb_ref, ddtb_ref,
             dalog_ref, dd_ref, dnw_ref, dhead_ref, dstate_ref):
        c = pl.program_id(0)
        first = c == 0

        @pl.when(first)
        def _():
            dstate_ref[...] = jnp.zeros_like(dstate_ref)
            dhead_ref[...] = jnp.zeros_like(dhead_ref)

        total = None
        for s in range(nb):
            parts = one_sequence(s, dm_ref, z_ref, xbc_ref, pre_ref, dtr_ref, y_ref, st_ref, cw_ref, dtb_ref, alog_ref,
                                 dsk_ref, nw_ref, exp_ref, expt_ref, tril_ref, triu_ref, dz_ref, dxbc_ref, ddt_ref,
                                 dhead_ref, dstate_ref)
            total = parts if total is None else tuple(a + b for a, b in zip(total, parts))
        dcw = total[0]

        @pl.when(first)
        def _():
            dcw_ref[...] = dcw

        @pl.when(jnp.logical_not(first))
        def _():
            dcw_ref[...] += dcw

        for ref, part in zip((dcb_ref, ddtb_ref, dalog_ref, dd_ref, dnw_ref), total[1:]):
            _acc_rows(ref, part, first)

    consts = [cw, cb, dtb, alog, dskip_exp, nw, expand, expand_t, tril, triu]
    deps = [] if dep is None else [dep]
    n_in = 7 + len(consts)

    def body_skipping_dep(*refs):
        body(*refs[:n_in], *refs[n_in + len(deps):])

    acc = lambda n: jax.ShapeDtypeStruct((1, n), F32)
    sd = lambda n: jax.ShapeDtypeStruct((nb, seq, n), BF16)
    dz, dxbc, ddt, *small_grads = pl.pallas_call(
        body_skipping_dep, name="ssd_bwd", grid=(nc,),
        out_shape=(sd(SSM_WIDTH), sd(CONV_CH), sd(CHUNK), jax.ShapeDtypeStruct((8, CONV_CH), F32), acc(CONV_CH),
                   acc(CHUNK), acc(CHUNK), acc(CHUNK), acc(SSM_WIDTH)),
        in_specs=[row(SSM_WIDTH, col=1), row(SSM_WIDTH), row(CONV_CH), row(CONV_CH), row(CHUNK), row(SSM_WIDTH),
                  states_spec]
        + [_full(a.shape) for a in consts] + [pl.BlockSpec(memory_space=pl.ANY)] * len(deps),
        out_specs=(row(SSM_WIDTH), row(CONV_CH), row(CHUNK), _full((8, CONV_CH)), _full((1, CONV_CH)),
                   _full((1, CHUNK)), _full((1, CHUNK)), _full((1, CHUNK)), _full((1, SSM_WIDTH))),
        scratch_shapes=[pltpu.VMEM((nb, 8, CONV_CH), F32), pltpu.VMEM((nb, N_STATE, SSM_WIDTH), F32)],
        compiler_params=_params("arbitrary"))(
            fold(dmix), fold(z), fold(xbc), fold(pre), fold(dtr), fold(y), states, *consts, *deps)
    return (unfold(dz), unfold(dxbc), unfold(ddt), *small_grads)


def _in_bwd(du, dv, dz, dxbc, ddt, w_in, x, dx2, g1, tm, me, riders=(), dep=None):
    t_tok = x.shape[0]
    steps = t_tok // tm

    n_in = [5 + ("mask" in rd) for rd in riders]
    first_in = [sum(n_in[:r]) for r in range(len(riders))]

    def body(me_ref, du_ref, dv_ref, dz_ref, dxbc_ref, ddt_ref, w_ref, x_ref, dx2_ref, g_ref, *rest):
        outs = rest[len(rest) - 2 - 4 * len(riders):]
        gx_ref, dg_ref = outs[:2]
        i = pl.program_id(0)
        dh = None
        for (a, b), ref in zip(_IN_SPLITS, (du_ref, dv_ref, dz_ref, dxbc_ref, ddt_ref)):
            part = _dot(ref[...], w_ref[a:b, :])
            dh = part if dh is None else dh + part
        dn, dg = _rms_bwd(x_ref[...], g_ref[...], dh)
        gx_ref[...] = dx2_ref[...] + dn
        _acc_rows(dg_ref, dg, i == 0)
        for r in range(len(riders)):
            p_ref, own_ref, w_ref_r, m_ref_r, v_ref_r = rest[first_in[r]:first_in[r] + 5]
            g = _sum_parts(me_ref[0], p_ref, own_ref[0])
            if n_in[r] == 6:
                g = g * rest[first_in[r] + 5][...]
            d, mn, vn = _adamw_math(w_ref_r[...], g, m_ref_r[...], v_ref_r[...])
            for o_ref, val in zip(outs[2 + 4 * r:6 + 4 * r], (g, d, mn, vn)):
                o_ref[...] = val

    row = lambda n: pl.BlockSpec((tm, n), lambda i, me_ref: (i, 0))
    whole = lambda shape: pl.BlockSpec(shape, lambda i, me_ref: (0,) * len(shape))
    widths = [b - a for a, b in _IN_SPLITS]
    deps = [] if dep is None else [dep]
    rider_args, rider_specs, rider_out_shapes, rider_out_specs = [], [], [], []
    for rd in riders:
        rows, cols = rd["w"].shape[0] // steps, rd["w"].shape[1]
        blk = pl.BlockSpec((rows, cols), lambda i, me_ref: (i, 0))
        rider_args += [rd["parts"], rd["own"], rd["w"], rd["m"], rd["v"]]
        rider_specs += [pl.BlockSpec((N_DEV, rows, cols), lambda i, me_ref: (0, i, 0)),
                        pl.BlockSpec((1, rows, cols), lambda i, me_ref: (me_ref[0], i, 0)), blk, blk, blk]
        if "mask" in rd:
            rider_args.append(rd["mask"])
            rider_specs.append(whole((rows, cols)))
        rider_out_shapes += [jax.ShapeDtypeStruct(rd["w"].shape, F32)] * 4
        rider_out_specs += [blk] * 4
    outs = pl.pallas_call(
        body, name="in_bwd",
        out_shape=(jax.ShapeDtypeStruct((t_tok, D_MODEL), F32), jax.ShapeDtypeStruct((1, D_MODEL), F32),
                   *rider_out_shapes),
        grid_spec=pltpu.PrefetchScalarGridSpec(
            num_scalar_prefetch=1, grid=(steps,),
            in_specs=[row(n) for n in widths] + [whole((IN_PAD, D_MODEL)), row(D_MODEL), row(D_MODEL),
                                                 whole((1, D_MODEL))] + rider_specs
            + [pl.BlockSpec(memory_space=pl.ANY)] * len(deps),
            out_specs=(row(D_MODEL), whole((1, D_MODEL)), *rider_out_specs)),
        compiler_params=_params("arbitrary"))(me, du, dv, dz, dxbc, ddt, w_in, x, dx2, g1, *rider_args, *deps)
    return outs[0], outs[1], [tuple(outs[2 + 4 * r:6 + 4 * r]) for r in range(len(riders))]


def _pad_lanes(a, n):
    return jnp.pad(a, ((0, 0), (0, n - a.shape[1])))


def _local_step(x, target, seq, small, hooks, first_dep=None):
    t_tok = x.shape[0]
    tm = min(TOKEN_TILE, t_tok)
    avg, expand, expand_t, tril, triu = _const_mats()
    g1, g2, g3, g4 = (small[k].reshape(1, D_MODEL) for k in
                      ("norm_mix_pre", "norm_mix_post", "norm_ffn_pre", "norm_ffn_post"))
    tie = (lambda a: a) if first_dep is None else (lambda a: a + first_dep[0, 0])
    lnw = tie(small["gm_ln_w"]).reshape(1, GM_WIDTH)
    lnb = tie(small["gm_ln_b"]).reshape(1, GM_WIDTH)
    causal = jnp.tril(jnp.ones((CHUNK, CHUNK), F32))
    wm = tie(small["gm_w_s"]) * causal
    pair = lambda w: w.reshape(4, 2, CHUNK, CHUNK).transpose(0, 2, 1, 3).reshape(4, CHUNK, 2 * CHUNK).astype(BF16)
    wcat = pair(wm)
    wtcat = pair(jnp.swapaxes(wm, 1, 2))
    bias = jnp.repeat(tie(small["gm_b_s"]).T, HEAD_DIM, axis=1)
    cb = small["conv_b"].reshape(1, CONV_CH)
    dtb = _pad_lanes(tie(small["dt_bias"]).reshape(1, N_HEADS), CHUNK)
    alog = _pad_lanes(tie(small["a_log"]).reshape(1, N_HEADS), CHUNK)
    dskip_exp = jnp.repeat(tie(small["d_skip"]).reshape(1, N_HEADS), HEAD_DIM, axis=1)
    nw = small["ssm_norm_w"].reshape(1, SSM_WIDTH)

    h1 = _prenorm(x, g1, tm, hooks.get("prenorm_after", first_dep))
    w_in_t, conv_w = hooks["mixer_weights"](h1)
    tall = min(2 * tm, t_tok)
    u, v, z, xbc, dtr = _in_proj(h1, w_in_t, tall)
    mix_a = _gmlp_fwd(u, v, lnw, lnb, wcat, bias, avg)
    mix_b, y_pre, states, pre = _ssd_fwd(z, xbc, dtr, conv_w, cb, dtb, alog, dskip_exp, nw, expand, tril, seq)
    w_out, dep = hooks["mixers_done"](mix_b)
    o, x2, h3 = _out_proj(mix_a, mix_b, w_out, x, g2, g3, tall, dep)
    w_up, w_down = hooks["mlp_weights"](h3)
    tf = FF_TILE
    ra, dd, dy, dg4, loss = _mlp_fwd(h3, w_up, w_down, x2, target, g4, tm, tf)

    da, dx2, do, dg3, dg2 = _mlp_bwd(dd, w_down, ra, w_up, x2, dy, o, g3, g2, tm, tf)
    g_w_down = _wgrad(ra, dd, None, WGRAD_TILE, D_MODEL, t_tok, True, "wgrad_down")
    g_w_up = _wgrad(h3, da, N_DEV, D_MODEL, D_FF // N_DEV, t_tok, False, "wgrad_up")
    dep = hooks["mlp_grads"](g_w_down, g_w_up)
    dmix = _dmix(do, w_out, tall, dep)
    g_w_out = _wgrad_in(do, (mix_a, mix_b), WGRAD_TILE, "wgrad_out", dep)
    du, dv, dws, dbt, dlnw, dlnb = _gmlp_bwd(dmix, u, v, lnw, lnb, wcat, wtcat, bias, avg, expand_t)
    dep = hooks["gmlp_grads"](g_w_out, dws)
    dz, dxbc, ddt, dcw, dcb, ddtb, dalog, ddsk, dnw = _ssd_bwd(
        dmix, z, xbc, pre, dtr, y_pre, states, conv_w, cb, dtb, alog, dskip_exp, nw, expand, expand_t, tril, triu, seq,
        dep)
    g_w_in = jnp.concatenate([_wgrad_in(h1, (du, dv, dz), WGRAD_TILE // 2, "wgrad_in_a", dep),
                              _wgrad_in(h1, (dxbc, ddt), WGRAD_TILE // 2, "wgrad_in_b", dep)], axis=0)
    dep = hooks["in_grads"](g_w_in, dcw[0:4])
    riders = hooks["arrived_updates"](dep) if "arrived_updates" in hooks else []
    me = hooks.get("me", jnp.zeros((1,), jnp.int32))
    grad_x, dg1, updates = _in_bwd(du, dv, dz, dxbc, ddt, w_in_t, x, dx2, g1, tm, me, riders, dep)

    grads = dict(
        updates=updates,
        w_in=g_w_in, w_out=g_w_out, w_up=g_w_up, w_down=g_w_down, conv_w=dcw[0:4],
        norm_mix_pre=dg1, norm_mix_post=dg2, norm_ffn_pre=dg3, norm_ffn_post=dg4, gm_ln_w=dlnw, gm_ln_b=dlnb,
        gm_w_s=dws, gm_b_s=dbt, conv_b=dcb, dt_bias=ddtb, a_log=dalog, d_skip=ddsk, ssm_norm_w=dnw)
    return loss[0, 0], grad_x, grads


_WEIGHTS = ("norm_mix_pre", "w_in", "gm_ln_w", "gm_ln_b", "gm_w_s", "gm_b_s", "conv_w", "conv_b", "dt_bias", "a_log",
            "d_skip", "ssm_norm_w", "w_out", "norm_mix_post", "norm_ffn_pre", "w_up", "w_down", "norm_ffn_post")
_SLAB_ROWS = (("norm_mix_pre", 1024), ("norm_mix_post", 1024), ("norm_ffn_pre", 1024), ("norm_ffn_post", 1024),
              ("conv_b", 1024), ("ssm_norm_w", 512), ("gm_ln_w", 512), ("gm_ln_b", 512), ("dt_bias", 8), ("a_log", 8),
              ("d_skip", 8))
_SLAB_LOSS_ROW = len(_SLAB_ROWS)
_SLAB_BS_ROW = 16
_SMALL_PARAMS = tuple(name for name, _ in _SLAB_ROWS) + ("gm_b_s",)
_LN_PARAMS = ("gm_ln_w", "gm_ln_b")


_SLAB_CONV_ROW = _SLAB_LOSS_ROW + 1


def _pack_slab(g, loss_part):
    rows = [_pad_lanes(g[name], D_MODEL) for name, _ in _SLAB_ROWS]
    rows.append(jnp.broadcast_to(loss_part, (1, D_MODEL)))
    rows.append(g["conv_w"])
    assert sum(r.shape[0] for r in rows) == _SLAB_BS_ROW
    rows.append(_pad_lanes(g["gm_b_s"].T[0:N_HEADS], D_MODEL))
    return jnp.concatenate(rows, axis=0)


def _adamw_slab(parts, me, w, m, v):
    names = _SMALL_PARAMS + ("conv_w",)
    shapes = [w[k].shape for k in names]
    unfold = np.zeros((GM_WIDTH, HEAD_DIM), np.float32)
    for h in range(N_HEADS):
        unfold[h * HEAD_DIM:(h + 1) * HEAD_DIM, :] = np.eye(HEAD_DIM)
    unfold = jnp.asarray(unfold, dtype=BF16)
    n = len(names)
    shard = CONV_CH // N_DEV

    def body(me_ref, p_ref, unfold_ref, *refs):
        w_refs, m_refs, v_refs = refs[:n], refs[n:2 * n], refs[2 * n:3 * n]
        outs = refs[3 * n:]
        g_all = p_ref[0]
        for j in range(1, N_DEV):
            g_all = g_all + p_ref[j]
        lane = lax.broadcasted_iota(jnp.int32, (N_HEADS, GM_WIDTH), 1)
        head = lax.broadcasted_iota(jnp.int32, (N_HEADS, GM_WIDTH), 0)
        own_lanes = jnp.logical_and(lane >= head * HEAD_DIM, lane < (head + 1) * HEAD_DIM)
        mine = pl.ds(pl.multiple_of(me_ref[0] * shard, shard), shard)
        for i, name in enumerate(names):
            if name == "gm_b_s":
                g = g_all[_SLAB_BS_ROW:_SLAB_BS_ROW + N_HEADS, 0:CHUNK]
            elif name == "conv_w":
                g = p_ref[0, _SLAB_CONV_ROW:_SLAB_CONV_ROW + 4, mine]
                for j in range(1, N_DEV):
                    g = g + p_ref[j, _SLAB_CONV_ROW:_SLAB_CONV_ROW + 4, mine]
            else:
                row = [r for r, (k, _) in enumerate(_SLAB_ROWS) if k == name][0]
                g = g_all[row:row + 1, 0:dict(_SLAB_ROWS)[name]]
                if name in _LN_PARAMS:
                    g = _split_dot(jnp.where(own_lanes, g, 0.0), unfold_ref[...], 3)
            d, mn, vn = _adamw_math(w_refs[i][...], g, m_refs[i][...], v_refs[i][...])
            for o_ref, val in zip(outs[4 * i:4 * i + 4], (g, d, mn, vn)):
                o_ref[...] = val
        outs[-1][...] = g_all[_SLAB_LOSS_ROW:_SLAB_LOSS_ROW + 1, 0:128]

    def whole(shape):
        nd = len(shape)
        return pl.BlockSpec(shape, lambda i, me_ref: (0,) * nd)

    ins = [parts, unfold] + [d[k] for d in (w, m, v) for k in names]
    out_shape = tuple(jax.ShapeDtypeStruct(s, F32) for s in shapes for _ in range(4)) + (
        jax.ShapeDtypeStruct((1, 128), F32),)
    outs = pl.pallas_call(
        body, name="adamw_small", out_shape=out_shape,
        grid_spec=pltpu.PrefetchScalarGridSpec(
            num_scalar_prefetch=1, grid=(1,), in_specs=[whole(a.shape) for a in ins],
            out_specs=tuple(whole(s.shape) for s in out_shape)),
        compiler_params=_params("arbitrary"))(me, *ins)
    return {k: tuple(outs[4 * i:4 * i + 4]) for i, k in enumerate(names)}, outs[-1][0, 0]


def kernel(x, norm_mix_pre, w_in, gm_ln_w, gm_ln_b, gm_w_s, gm_b_s, conv_w, conv_b, dt_bias, a_log, d_skip, ssm_norm_w, w_out, norm_mix_post, norm_ffn_pre, w_up, w_down, norm_ffn_post, loss_target, m_norm_mix_pre, m_w_in, m_gm_ln_w, m_gm_ln_b, m_gm_w_s, m_gm_b_s, m_conv_w, m_conv_b, m_dt_bias, m_a_log, m_d_skip, m_ssm_norm_w, m_w_out, m_norm_mix_post, m_norm_ffn_pre, m_w_up, m_w_down, m_norm_ffn_post, v_norm_mix_pre, v_w_in, v_gm_ln_w, v_gm_ln_b, v_gm_w_s, v_gm_b_s, v_conv_w, v_conv_b, v_dt_bias, v_a_log, v_d_skip, v_ssm_norm_w, v_w_out, v_norm_mix_post, v_norm_ffn_pre, v_w_up, v_w_down, v_norm_ffn_post):
    w = dict(norm_mix_pre=norm_mix_pre, w_in=w_in, gm_ln_w=gm_ln_w, gm_ln_b=gm_ln_b, gm_w_s=gm_w_s, gm_b_s=gm_b_s, conv_w=conv_w, conv_b=conv_b, dt_bias=dt_bias, a_log=a_log, d_skip=d_skip, ssm_norm_w=ssm_norm_w, w_out=w_out, norm_mix_post=norm_mix_post, norm_ffn_pre=norm_ffn_pre, w_up=w_up, w_down=w_down, norm_ffn_post=norm_ffn_post)
    m = dict(norm_mix_pre=m_norm_mix_pre, w_in=m_w_in, gm_ln_w=m_gm_ln_w, gm_ln_b=m_gm_ln_b, gm_w_s=m_gm_w_s, gm_b_s=m_gm_b_s, conv_w=m_conv_w, conv_b=m_conv_b, dt_bias=m_dt_bias, a_log=m_a_log, d_skip=m_d_skip, ssm_norm_w=m_ssm_norm_w, w_out=m_w_out, norm_mix_post=m_norm_mix_post, norm_ffn_pre=m_norm_ffn_pre, w_up=m_w_up, w_down=m_w_down, norm_ffn_post=m_norm_ffn_post)
    v = dict(norm_mix_pre=v_norm_mix_pre, w_in=v_w_in, gm_ln_w=v_gm_ln_w, gm_ln_b=v_gm_ln_b, gm_w_s=v_gm_w_s, gm_b_s=v_gm_b_s, conv_w=v_conv_w, conv_b=v_conv_b, dt_bias=v_dt_bias, a_log=v_a_log, d_skip=v_d_skip, ssm_norm_w=v_ssm_norm_w, w_out=v_w_out, norm_mix_post=v_norm_mix_post, norm_ffn_pre=v_norm_ffn_pre, w_up=v_w_up, w_down=v_w_down, norm_ffn_post=v_norm_ffn_post)
    n_batch, seq, _ = x.shape
    shard_in = IN_COLS // N_DEV

    me = (4 * lax.axis_index("x") + 2 * lax.axis_index("y") + lax.axis_index("c")).astype(jnp.int32).reshape(1)

    def in_slot(own):
        return lax.dynamic_update_slice(lax.empty((N_DEV,) + own.shape, own.dtype), own[None],
                                        (me[0],) + (0,) * own.ndim)

    w_in_sh, m_in_sh, v_in_sh = w_in[0].T, m_w_in[0].T, v_w_in[0].T
    first = [_cast_to_slot(w_in_sh, me, shard_in, "cast_w_in"), in_slot(conv_w[0]),
             _cast_to_slot(w_out[0], me, 128, "cast_w_out")]
    ici_1, tok_ici_1 = _exchange_start(first, [True] * 3, _SAME_CORE_PEERS, "gather_mix_ici_start")
    cast_up = _cast_to_slot(w_up[0], me, 1024, "cast_w_up", cols=True, dep=tok_ici_1)
    second = [cast_up, _cast_to_slot(w_down[0], me, 512, "cast_w_down", dep=cast_up)]
    gathering = {}

    def mixer_weights(after):
        bufs = [buf for buf, _ in _exchange_wait(ici_1, after, "gather_mix_ici_wait")]
        d2d_1, tok_d2d_1 = _exchange_start(bufs, [True] * 3, _SIBLING_FORWARD, "gather_mix_d2d_start")
        gathering["mlp_ici"], tok_ici_2 = _exchange_start(
            second, [True] * 2, _SAME_CORE_PEERS, "gather_mlp_ici_start", dep=tok_d2d_1)
        (_, ag_in), (_, ag_conv), (_, ag_out) = _exchange_wait(d2d_1, tok_ici_2, "gather_mix_d2d_wait")
        gathering["w_out"] = ag_out.reshape(D_MODEL, D_MODEL)
        w_in_t = jnp.pad(ag_in.reshape(IN_COLS, D_MODEL), ((0, IN_PAD - IN_COLS), (0, 0)))
        return w_in_t, ag_conv.transpose(1, 0, 2).reshape(4, CONV_CH)

    def mixers_done(after):
        bufs = [buf for buf, _ in _exchange_wait(gathering["mlp_ici"], after, "gather_mlp_ici_wait")]
        gathering["mlp"], tok = _exchange_start(bufs, [True] * 2, _SIBLING_FORWARD, "gather_mlp_d2d_start")
        return gathering["w_out"], tok

    def mlp_weights(after):
        (_, ag_up), (_, ag_down) = _exchange_wait(gathering["mlp"], after, "gather_mlp_d2d_wait")
        return ag_up, ag_down.reshape(D_FF, D_MODEL)

    sent = {}

    def mlp_grads(g_w_down, g_w_up):
        sent["mlp"], tok = _exchange_start(
            [g_w_down.reshape(N_DEV, D_FF // N_DEV, D_MODEL), g_w_up], [False, False], _ALL_PEERS, "grads_mlp_start")
        return tok

    def gmlp_grads(g_w_out, g_w_s):
        sent["gmlp"], tok = _exchange_start(
            [g_w_out.reshape(N_DEV, D_MODEL // N_DEV, D_MODEL), in_slot(g_w_s.astype(BF16))], [False, True], _ALL_PEERS,
            "grads_gmlp_start")
        return tok

    def in_grads(g_w_in_t, g_conv_w):
        g_in_blk = g_w_in_t[:IN_COLS].reshape(N_DEV, shard_in, D_MODEL)
        sent["in"], tok = _exchange_start([g_in_blk], [False], _ALL_PEERS, "grads_in_start")
        return tok

    def arrived_updates(after):
        (own_down, p_down), (own_up, p_up) = _exchange_wait(sent["mlp"], after, "grads_mlp_wait")
        (own_out, p_out), (_, p_ws) = _exchange_wait(sent["gmlp"], own_up, "grads_gmlp_wait")
        rows = lambda t: t.reshape(t.shape[:-3] + (N_HEADS * CHUNK, CHUNK))
        return [dict(parts=p_up, own=own_up, w=w_up[0], m=m_w_up[0], v=v_w_up[0]),
                dict(parts=p_down, own=own_down, w=w_down[0], m=m_w_down[0], v=v_w_down[0]),
                dict(parts=p_out, own=own_out, w=w_out[0], m=m_w_out[0], v=v_w_out[0]),
                dict(parts=rows(p_ws), own=rows(p_ws), w=rows(gm_w_s[0]), m=rows(m_gm_w_s[0]), v=rows(v_gm_w_s[0]),
                     mask=jnp.tril(jnp.ones((CHUNK, CHUNK), F32)))]

    small = {k: w[k][0] for k in _SMALL_PARAMS + ("gm_w_s",)}
    loss_part, grad_x, g = _local_step(
        x.reshape(n_batch * seq, D_MODEL), loss_target.reshape(n_batch * seq, D_MODEL), seq, small,
        dict(mixer_weights=mixer_weights, mixers_done=mixers_done, mlp_weights=mlp_weights, mlp_grads=mlp_grads,
             gmlp_grads=gmlp_grads, in_grads=in_grads, arrived_updates=arrived_updates, me=me,
             prenorm_after=second[1]), first_dep=tok_ici_1)

    sent_rows, tok_rows = _exchange_start([in_slot(_pack_slab(g, loss_part))], [True], _ALL_PEERS, "grads_rows_start")
    res = dict(zip(("w_up", "w_down", "w_out", "gm_w_s"), g["updates"]))
    ((own_in, p_in),) = _exchange_wait(sent["in"], tok_rows, "grads_in_wait")
    res["w_in"] = tuple(r.T for r in _adamw_reduce(p_in, own_in, me, w_in_sh, m_in_sh, v_in_sh, shard_in, "adamw_w_in"))
    ((_, p_rows),) = _exchange_wait(sent_rows, res["w_in"][1], "grads_rows_wait")
    flat = lambda t: t[0] if t.ndim == 3 else t
    small_res, loss = _adamw_slab(
        p_rows, me, *({k: flat(d[k]) for k in _SMALL_PARAMS + ("conv_w",)} for d in (w, m, v)))
    res.update(small_res)
    res = {k: tuple(r.reshape(w[k].shape) for r in res[k]) for k in _WEIGHTS}

    outs = [loss, grad_x.reshape(x.shape)]
    for part in range(4):
        outs.extend(res[k][part] for k in _WEIGHTS)
    return tuple(outs)
```

```python
import functools

import jax
import jax.numpy as jnp
import numpy as np
from jax import lax
from jax.experimental import pallas as pl
from jax.experimental.pallas import tpu as pltpu

F32 = jnp.float32
BF16 = jnp.bfloat16

D_MODEL = 1024
GM_WIDTH = 512
SSM_WIDTH = 512
CONV_CH = 1024
N_HEADS = 8
HEAD_DIM = 64
N_STATE = 128
CHUNK = 128
D_FF = 4096
IN_COLS = 2568
IN_PAD = 2688
N_DEV = 8
EPS = 1e-6
ADAM_LR, ADAM_B1, ADAM_B2, ADAM_EPS, ADAM_WD, ADAM_STEP = 0.001, 0.9, 0.999, 1e-08, 0.01, 10
VMEM_LIMIT_BYTES = 56 * 1024 * 1024
TOKEN_TILE = 512
FF_TILE = 2048
WGRAD_TILE = 512

_NT = (((1,), (1,)), ((), ()))
_TN = (((0,), (0,)), ((), ()))


def _params(*sem):
    return pltpu.CompilerParams(dimension_semantics=sem or None, vmem_limit_bytes=VMEM_LIMIT_BYTES)


def _dot(a, b, dims=None):
    if dims is None:
        return jnp.dot(a, b, preferred_element_type=F32)
    return lax.dot_general(a, b, dims, preferred_element_type=F32)


def _split_terms(x, terms):
    out, rem = [], x
    for i in range(terms):
        hi = rem.astype(BF16)
        out.append(hi)
        if i + 1 < terms:
            rem = rem - hi.astype(F32)
    return out


def _split_dot(x, m, terms):
    acc = None
    for hi in _split_terms(x, terms):
        part = _dot(hi, m)
        acc = part if acc is None else acc + part
    return acc


def _split_dot_left(m, x, terms):
    acc = None
    for hi in _split_terms(x, terms):
        part = _dot(m, hi)
        acc = part if acc is None else acc + part
    return acc


def _gelu_and_grad(x):
    c = 0.7978845608028654
    inner = c * (x + 0.044715 * x * x * x)
    t = jnp.tanh(inner)
    g = 0.5 * x * (1.0 + t)
    dg = 0.5 * (1.0 + t) + 0.5 * x * (1.0 - t * t) * c * (1.0 + 3.0 * 0.044715 * x * x)
    return g, dg


def _softplus(x):
    return jnp.maximum(x, 0.0) + jnp.log(1.0 + jnp.exp(-jnp.abs(x)))


def _rsum(x):
    return jnp.sum(x, axis=0, keepdims=True)


def _acc_rows(ref, part, first):
    val = jnp.broadcast_to(part, ref.shape)

    @pl.when(first)
    def _():
        ref[...] = val

    @pl.when(jnp.logical_not(first))
    def _():
        ref[...] += val


def _rms_bwd(n, g, dout):
    r = lax.rsqrt(jnp.mean(n * n, axis=-1, keepdims=True) + EPS)
    nh = n * r
    dg = dout * g
    dn = r * (dg - nh * jnp.mean(dg * nh, axis=-1, keepdims=True))
    return dn, _rsum(dout * nh)


def _const_mats():
    avg = np.kron(np.eye(4), np.full((HEAD_DIM, HEAD_DIM), 1.0 / HEAD_DIM))
    expand = np.zeros((CHUNK, SSM_WIDTH), np.float32)
    for h in range(N_HEADS):
        expand[h, h * HEAD_DIM:(h + 1) * HEAD_DIM] = 1.0
    tril = np.tril(np.ones((CHUNK, CHUNK), np.float32))
    as_bf16 = lambda a: jnp.asarray(a, dtype=BF16)
    return as_bf16(avg), as_bf16(expand), as_bf16(expand.T), as_bf16(tril), as_bf16(tril.T)


def _full(shape):
    nd = len(shape)
    return pl.BlockSpec(shape, lambda *_: (0,) * nd)


_HBM = pl.BlockSpec(memory_space=pltpu.HBM)
_SEM = pl.BlockSpec(memory_space=pltpu.SEMAPHORE)
_ALL_PEERS = tuple((k, 0) for k in range(1, N_DEV))
_SAME_CORE_PEERS = ((2, 0), (4, 0), (6, 0))
_SIBLING_FORWARD = ((1, 0), (1, 2), (1, 4), (1, 6))


def _flip(j, k):
    for bit in (4, 2, 1):
        if k & bit:
            j = j + bit - 2 * (j & bit)
    return j


def _copies(src, land, send_sems, recv_sems, hops):
    x, y, c = lax.axis_index("x"), lax.axis_index("y"), lax.axis_index("c")
    me = 4 * x + 2 * y + c
    out = []
    for t in range(len(src)):
        for i, (k, b) in enumerate(hops):
            pos = (1 - x if k & 4 else x, 1 - y if k & 2 else y, 1 - c if k & 1 else c)
            peer = _flip(me, k)
            sem = t * len(hops) + i
            mk = functools.partial(pltpu.make_async_remote_copy, send_sem=send_sems.at[sem], recv_sem=recv_sems.at[sem],
                                   device_id=pos, device_id_type=pl.DeviceIdType.MESH)
            if land[t] is None and src[t].shape[0] != N_DEV:
                width = src[t].shape[1] // N_DEV
                slab = lambda j: src[t].at[:, pl.ds(pl.multiple_of(j * width, 128), width)]
                mine = functools.partial(mk, src_ref=slab(_flip(me, b)), dst_ref=slab(_flip(me, b)))
                theirs = functools.partial(mk, src_ref=slab(_flip(peer, b)), dst_ref=slab(_flip(peer, b)))
            elif land[t] is None:
                mine = functools.partial(mk, src_ref=src[t].at[_flip(me, b)], dst_ref=src[t].at[_flip(me, b)])
                theirs = functools.partial(mk, src_ref=src[t].at[_flip(peer, b)], dst_ref=src[t].at[_flip(peer, b)])
            else:
                assert b == 0
                mine = functools.partial(mk, src_ref=src[t].at[peer], dst_ref=land[t].at[me])
                theirs = functools.partial(mk, src_ref=src[t].at[peer], dst_ref=land[t].at[peer])
            out.append((mine, theirs))
    return out


def _exchange_start(srcs, inplace, peers, name, dep=None):
    n = len(srcs)
    lands = [None if ip else pltpu.with_memory_space_constraint(lax.empty(s.shape, s.dtype), pltpu.HBM)
             for s, ip in zip(srcs, inplace)]
    real_lands = [l for l in lands if l is not None]
    n_l = len(real_lands)
    deps = [] if dep is None else [dep]

    def body(*refs):
        src = refs[:n]
        land_refs = list(refs[n:n + n_l])
        send_sems, recv_sems = refs[n + n_l + len(deps)], refs[n + n_l + len(deps) + 1]
        token = refs[-1]
        land = [None if ip else land_refs.pop(0) for ip in inplace]
        for mine, _ in _copies(src, land, send_sems, recv_sems, peers):
            mine().start()
        token[...] = jnp.zeros_like(token)

    sem_t = pltpu.SemaphoreType.DMA((n * len(peers),))
    outs = pl.pallas_call(
        body, name=name,
        out_shape=(sem_t, sem_t) + tuple(pltpu.HBM(a.shape, a.dtype) for a in list(srcs) + real_lands)
        + (jax.ShapeDtypeStruct((8, 128), F32),),
        in_specs=[_HBM] * (n + n_l) + [pl.BlockSpec(memory_space=pl.ANY)] * len(deps),
        out_specs=(_SEM, _SEM) + (_HBM,) * (n + n_l) + (pl.BlockSpec(memory_space=pltpu.VMEM),),
        input_output_aliases={i: 2 + i for i in range(n + n_l)},
        compiler_params=pltpu.CompilerParams(has_side_effects=pltpu.SideEffectType.DATAFLOW_SIDE_EFFECTING),
    )(*[pltpu.with_memory_space_constraint(s, pltpu.HBM) for s in srcs], *real_lands, *deps)
    handle = dict(send=outs[0], recv=outs[1], srcs=outs[2:2 + n], lands=outs[2 + n:2 + n + n_l], inplace=inplace,
                  peers=peers)
    return handle, outs[-1]


def _exchange_wait(handle, after, name):
    srcs, lands, inplace, peers = handle["srcs"], handle["lands"], handle["inplace"], handle["peers"]
    n, n_l = len(srcs), len(lands)

    def body(*refs):
        src = refs[:n]
        land_refs = list(refs[n:n + n_l])
        send_sems, recv_sems = refs[n + n_l], refs[n + n_l + 1]
        land = [None if ip else land_refs.pop(0) for ip in inplace]
        for mine, theirs in _copies(src, land, send_sems, recv_sems, peers):
            mine().wait_send()
            theirs().wait_recv()

    outs = pl.pallas_call(
        body, name=name, out_shape=tuple(pltpu.HBM(a.shape, a.dtype) for a in list(srcs) + list(lands)),
        in_specs=[_HBM] * (n + n_l) + [_SEM, _SEM, pl.BlockSpec(memory_space=pl.ANY)],
        out_specs=(_HBM,) * (n + n_l), input_output_aliases={i: i for i in range(n + n_l)},
        compiler_params=pltpu.CompilerParams(has_side_effects=pltpu.SideEffectType.DATAFLOW_SIDE_EFFECTING),
    )(*srcs, *lands, handle["send"], handle["recv"], after)
    res, land_out = [], list(outs[n:])
    for t in range(n):
        res.append((outs[t], outs[t] if inplace[t] else land_out.pop(0)))
    return res


def _cast_to_slot(w, me, rows, name, cols=False, dep=None):
    r, cdim = w.shape
    deps = [] if dep is None else [dep]

    def body(me_ref, w_ref, *rest):
        o_ref = rest[-1]
        if cols:
            o_ref[...] = w_ref[...].astype(BF16)
        else:
            o_ref[0] = w_ref[...].astype(BF16)

    if cols:
        out_shape = jax.ShapeDtypeStruct((r, N_DEV * cdim), BF16)
        out_spec = pl.BlockSpec((rows, cdim), lambda i, me_ref: (i, me_ref[0]))
    else:
        out_shape = jax.ShapeDtypeStruct((N_DEV, r, cdim), BF16)
        out_spec = pl.BlockSpec((1, rows, cdim), lambda i, me_ref: (me_ref[0], i, 0))
    return pl.pallas_call(
        body, name=name, out_shape=out_shape,
        grid_spec=pltpu.PrefetchScalarGridSpec(
            num_scalar_prefetch=1, grid=(r // rows,),
            in_specs=[pl.BlockSpec((rows, cdim), lambda i, me_ref: (i, 0))]
            + [pl.BlockSpec(memory_space=pl.ANY)] * len(deps), out_specs=out_spec),
        compiler_params=_params("parallel"))(me, w, *deps)


def _adamw_math(w, g, m, v):
    m = ADAM_B1 * m + (1.0 - ADAM_B1) * g
    v = ADAM_B2 * v + (1.0 - ADAM_B2) * (g * g)
    m_hat = m / (1.0 - ADAM_B1 ** ADAM_STEP)
    v_hat = v / (1.0 - ADAM_B2 ** ADAM_STEP)
    delta = -ADAM_LR * (m_hat / (jnp.sqrt(v_hat) + ADAM_EPS) + ADAM_WD * w)
    return delta, m, v


def _sum_parts(me, p_ref, own):
    g = None
    for j in range(N_DEV):
        term = (p_ref[j] if own is None else jnp.where(me == j, own, p_ref[j])).astype(F32)
        g = term if g is None else g + term
    return g


def _adamw_reduce(parts, own, me, w, m, v, rows, name):
    r, cdim = w.shape

    def body(me_ref, p_ref, own_ref, w_ref, m_ref, v_ref, g_out, d_out, m_out, v_out):
        g = _sum_parts(me_ref[0], p_ref, own_ref[0])
        d, mn, vn = _adamw_math(w_ref[...], g, m_ref[...], v_ref[...])
        g_out[...] = g
        d_out[...] = d
        m_out[...] = mn
        v_out[...] = vn

    blk = pl.BlockSpec((rows, cdim), lambda i, me_ref: (i, 0))
    sds = jax.ShapeDtypeStruct(w.shape, F32)
    return pl.pallas_call(
        body, name=name, out_shape=(sds,) * 4,
        grid_spec=pltpu.PrefetchScalarGridSpec(
            num_scalar_prefetch=1, grid=(r // rows,),
            in_specs=[pl.BlockSpec((N_DEV, rows, cdim), lambda i, me_ref: (0, i, 0)),
                      pl.BlockSpec((1, rows, cdim), lambda i, me_ref: (me_ref[0], i, 0)), blk, blk, blk],
            out_specs=(blk,) * 4),
        compiler_params=_params("parallel"))(me, parts, own, w, m, v)


_IN_SPLITS = ((0, 512), (512, 1024), (1024, 1536), (1536, 2560), (2560, IN_PAD))


def _prenorm(x, g1, tm, dep=None):
    t_tok = x.shape[0]
    deps = [] if dep is None else [dep]

    def body(x_ref, g_ref, *rest):
        h_ref, nh_ref, r_ref = rest[-3:]
        xv = x_ref[...]
        r = lax.rsqrt(jnp.mean(xv * xv, axis=-1, keepdims=True) + EPS)
        nh = xv * r
        h_ref[...] = (nh * g_ref[...]).astype(BF16)
        nh_ref[...] = nh.astype(BF16)
        r_ref[...] = r

    row = pl.BlockSpec((tm, D_MODEL), lambda i: (i, 0))
    return pl.pallas_call(
        body, name="prenorm", grid=(t_tok // tm,),
        out_shape=(jax.ShapeDtypeStruct((t_tok, D_MODEL), BF16), jax.ShapeDtypeStruct((t_tok, D_MODEL), BF16),
                   jax.ShapeDtypeStruct((t_tok, 1), F32)),
        in_specs=[row, _full((1, D_MODEL))] + [pl.BlockSpec(memory_space=pl.ANY)] * len(deps),
        out_specs=(row, row, pl.BlockSpec((tm, 1), lambda i: (i, 0))),
        compiler_params=_params("parallel"))(x, g1, *deps)


def _in_proj(h1, w_in, tm):
    t_tok = h1.shape[0]

    def body(h_ref, w_ref, *outs):
        h = h_ref[...]
        for (a, b), o_ref in zip(_IN_SPLITS, outs):
            o_ref[...] = _dot(h, w_ref[a:b, :], _NT).astype(o_ref.dtype)

    row = lambda n: pl.BlockSpec((tm, n), lambda i: (i, 0))
    widths = [b - a for a, b in _IN_SPLITS]
    dtypes = (BF16, BF16, BF16, F32, F32)
    return pl.pallas_call(
        body, name="in_proj", grid=(t_tok // tm,),
        out_shape=tuple(jax.ShapeDtypeStruct((t_tok, n), dt) for n, dt in zip(widths, dtypes)),
        in_specs=[row(D_MODEL), _full((IN_PAD, D_MODEL))], out_specs=tuple(row(n) for n in widths),
        compiler_params=_params("parallel"))(h1, w_in)


def _lane_masks():
    lane = lax.broadcasted_iota(jnp.int32, (1, 2 * HEAD_DIM), 1)
    left = (lane < HEAD_DIM).astype(F32)
    return left, 1.0 - left


def _stack_pair(v, m_l, m_r):
    return jnp.concatenate([v * m_l, v * m_r], axis=0).astype(BF16)


def _head_mean(x, avg):
    n = avg.shape[0]
    return jnp.concatenate([_split_dot(x[:, n * i:n * (i + 1)], avg, 2) for i in range(x.shape[1] // n)], axis=1)


def _gmlp_common(u, v, lnw, lnb, avg, wcat_ref, bias, m_l, m_r):
    ug, dug = _gelu_and_grad(u)
    vg, dvg = _gelu_and_grad(v)
    mu = _head_mean(vg, avg)
    vc = vg - mu
    var = _head_mean(vc * vc, avg)
    rstd = lax.rsqrt(var + EPS)
    vhat = vc * rstd
    vn = vhat * lnw + lnb
    rows = []
    for r in range(u.shape[0] // CHUNK):
        cols = []
        for j in range(N_HEADS // 2):
            pair = vn[CHUNK * r:CHUNK * (r + 1), 128 * j:128 * (j + 1)]
            cols.append(_dot(wcat_ref[j], _stack_pair(pair, m_l, m_r)))
        rows.append(jnp.concatenate(cols, axis=1) + bias)
    mixed = jnp.concatenate(rows, axis=0)
    return ug, dug, dvg, rstd, vhat, vn, mixed


_GMLP_ROWS = 4 * CHUNK


def _gmlp_fwd(u, v, lnw, lnb, wcat, bias, avg):
    t_tok = u.shape[0]
    tm = min(_GMLP_ROWS, t_tok)

    def body(u_ref, v_ref, lnw_ref, lnb_ref, wcat_ref, bias_ref, avg_ref, o_ref):
        m_l, m_r = _lane_masks()
        ug, _, _, _, _, _, mixed = _gmlp_common(
            u_ref[...].astype(F32), v_ref[...].astype(F32), lnw_ref[...], lnb_ref[...], avg_ref[...], wcat_ref,
            bias_ref[...], m_l, m_r)
        o_ref[...] = (ug * mixed).astype(BF16)

    row = pl.BlockSpec((tm, GM_WIDTH), lambda i: (i, 0))
    return pl.pallas_call(
        body, name="gmlp_fwd", grid=(t_tok // tm,), out_shape=jax.ShapeDtypeStruct((t_tok, GM_WIDTH), BF16),
        in_specs=[row, row, _full((1, GM_WIDTH)), _full((1, GM_WIDTH)), _full(wcat.shape), _full(bias.shape),
                  _full(avg.shape)],
        out_specs=row, compiler_params=_params("parallel"))(u, v, lnw, lnb, wcat, bias, avg)


def _shift_rows(x, edge, j, down):
    groups, cols = x.shape[0] // 8, x.shape[1]
    amount = j if down else 8 - j
    rot = pltpu.roll(x.reshape(groups, 8, cols), amount, axis=1)
    edge_rot = pltpu.roll(edge, amount, axis=0)[None]
    sub = lax.broadcasted_iota(jnp.int32, (1, 8, 1), 1)
    if down:
        out = jnp.where(sub < j, jnp.concatenate([edge_rot, rot[:-1]], axis=0), rot)
    else:
        out = jnp.where(sub < 8 - j, rot, jnp.concatenate([rot[1:], edge_rot], axis=0))
    return out.reshape(x.shape)


def _conv_pre(xbc, tail, cw_ref, cb):
    taps = [_shift_rows(xbc, tail, 3 - k, True) for k in range(3)] + [xbc]
    return cb + cw_ref[0:1, :] * taps[0] + cw_ref[1:2, :] * taps[1] + cw_ref[2:3, :] * taps[2] + cw_ref[3:4, :] * taps[3]


def _ssd_common(pre, dtr, dtb, alog, expand, tril):
    q = CHUNK
    sg = jax.nn.sigmoid(pre)
    act = pre * sg
    lane = lax.broadcasted_iota(jnp.int32, (1, CHUNK), 1)
    a_row = jnp.where(lane < N_HEADS, -jnp.exp(alog), 0.0)
    dtp = dtr + dtb
    dt = _softplus(dtp)
    a_cs = _split_dot_left(tril, dt * a_row, 3)
    a_cs_t = a_cs.T
    dt_exp = _split_dot(dt, expand, 3)
    a_exp = _split_dot(a_cs, expand, 3)
    a_end = a_exp[q - 1:q, :]
    li = lax.broadcasted_iota(jnp.int32, (q, q), 0)
    si = lax.broadcasted_iota(jnp.int32, (q, q), 1)
    causal = si <= li
    decay = []
    for h in range(N_HEADS):
        seg = a_cs[:, h:h + 1] - a_cs_t[h:h + 1, :]
        decay.append(jnp.where(causal, jnp.exp(jnp.minimum(seg, 0.0)), 0.0))
    return dict(pre=pre, sg=sg, act=act, a_row=a_row, dtp=dtp, dt=dt, dt_exp=dt_exp, a_exp=a_exp,
                e=jnp.exp(a_exp), w_end=jnp.exp(a_end - a_exp), cd=jnp.exp(a_end), decay=decay)


def _ssd_specs(t_tok, seq, reverse):
    nb, nc = t_tok // seq, seq // CHUNK

    def chunk(c):
        return nc - 1 - c if reverse else c

    def row(n, col=0):
        return pl.BlockSpec((nb, CHUNK, n), lambda c: (0, chunk(c), col))

    tail = pl.BlockSpec((nb, 8, CONV_CH), lambda c: (0, jnp.maximum(chunk(c) * (CHUNK // 8) - 1, 0), 0))
    states = pl.BlockSpec((nb, 1, N_STATE, SSM_WIDTH), lambda c: (0, chunk(c), 0, 0))
    fold = lambda a: a.reshape(nb, seq, a.shape[-1])
    unfold = lambda a: a.reshape(t_tok, a.shape[-1])
    return nb, nc, row, tail, states, fold, unfold


def _ssd_fwd(z, xbc, dtr, cw, cb, dtb, alog, dskip_exp, nw, expand, tril, seq):
    t_tok = z.shape[0]
    nb, nc, row, tail, states_spec, fold, unfold = _ssd_specs(t_tok, seq, False)

    def body(z_ref, xbc_ref, tail_ref, dtr_ref, cw_ref, cb_ref, dtb_ref, alog_ref, dsk_ref, nw_ref, exp_ref,
             tril_ref, o_ref, y_ref, st_ref, pre_ref, state_ref):
        c = pl.program_id(0)

        @pl.when(c == 0)
        def _():
            state_ref[...] = jnp.zeros_like(state_ref)

        m_l, m_r = _lane_masks()
        for s in range(nb):
            pre = _conv_pre(xbc_ref[s], jnp.where(c == 0, 0.0, tail_ref[s]), cw_ref, cb_ref[...])
            pre_ref[s] = pre
            f = _ssd_common(pre, dtr_ref[s], dtb_ref[...], alog_ref[...], exp_ref[...], tril_ref[...])
            act = f["act"]
            xs = act[:, :SSM_WIDTH]
            xdt = xs * f["dt_exp"]
            xw = xdt * f["w_end"]
            state = state_ref[s]
            st_ref[s, 0] = state
            ydiag, yoff, snew = [], [], []
            for g in range(2):
                bg = act[:, 512 + 128 * g:640 + 128 * g].astype(BF16)
                cg = act[:, 768 + 128 * g:896 + 128 * g].astype(BF16)
                cb_mat = _dot(cg, bg, _NT)
                for pr in range(2):
                    h0 = 4 * g + 2 * pr
                    gcat = jnp.concatenate(
                        [(cb_mat * f["decay"][h0]).astype(BF16), (cb_mat * f["decay"][h0 + 1]).astype(BF16)], axis=1)
                    ydiag.append(_dot(gcat, _stack_pair(xdt[:, 64 * h0:64 * h0 + 128], m_l, m_r)))
                yoff.append(_dot(cg, state[:, 256 * g:256 * (g + 1)].astype(BF16)))
                snew.append(_dot(bg, xw[:, 256 * g:256 * (g + 1)].astype(BF16), _TN))
            y = jnp.concatenate(ydiag, axis=1) + f["e"] * jnp.concatenate(yoff, axis=1) + dsk_ref[...] * xs
            state_ref[s] = state * f["cd"] + jnp.concatenate(snew, axis=1)
            y_ref[s] = y
            zv = z_ref[s].astype(F32)
            yg = y * (zv * jax.nn.sigmoid(zv))
            outs = []
            for g in range(2):
                ygg = yg[:, 256 * g:256 * (g + 1)]
                outs.append(ygg * lax.rsqrt(jnp.mean(ygg * ygg, axis=-1, keepdims=True) + EPS))
            o_ref[s] = (jnp.concatenate(outs, axis=1) * nw_ref[...]).astype(BF16)

    consts = [cw, cb, dtb, alog, dskip_exp, nw, expand, tril]
    sd = lambda n, dt: jax.ShapeDtypeStruct((nb, seq, n), dt)
    o, y, states, pre = pl.pallas_call(
        body, name="ssd_fwd", grid=(nc,),
        out_shape=(sd(SSM_WIDTH, BF16), sd(SSM_WIDTH, F32), jax.ShapeDtypeStruct((nb, nc, N_STATE, SSM_WIDTH), F32),
                   sd(CONV_CH, F32)),
        in_specs=[row(SSM_WIDTH), row(CONV_CH), tail, row(CHUNK)] + [_full(a.shape) for a in consts],
        out_specs=(row(SSM_WIDTH), row(SSM_WIDTH), states_spec, row(CONV_CH)),
        scratch_shapes=[pltpu.VMEM((nb, N_STATE, SSM_WIDTH), F32)],
        compiler_params=_params("arbitrary"))(fold(z), fold(xbc), fold(xbc), fold(dtr), *consts)
    return unfold(o), unfold(y), states, unfold(pre)


def _out_proj(mix_a, mix_b, w_out, x, g2, g3, tm, dep=None):
    t_tok = x.shape[0]
    deps = [] if dep is None else [dep]

    def body(a_ref, b_ref, w_ref, x_ref, g2_ref, g3_ref, *rest):
        o_ref, x2_ref, h3_ref = rest[-3:]
        o = _dot(a_ref[...], w_ref[0:GM_WIDTH, :]) + _dot(b_ref[...], w_ref[GM_WIDTH:, :])
        o_ref[...] = o
        r2 = lax.rsqrt(jnp.mean(o * o, axis=-1, keepdims=True) + EPS)
        x2 = x_ref[...] + o * r2 * g2_ref[...]
        x2_ref[...] = x2
        r3 = lax.rsqrt(jnp.mean(x2 * x2, axis=-1, keepdims=True) + EPS)
        h3_ref[...] = (x2 * r3 * g3_ref[...]).astype(BF16)

    row = lambda n: pl.BlockSpec((tm, n), lambda i: (i, 0))
    sd = lambda dt: jax.ShapeDtypeStruct((t_tok, D_MODEL), dt)
    return pl.pallas_call(
        body, name="out_proj", grid=(t_tok // tm,), out_shape=(sd(F32), sd(F32), sd(BF16)),
        in_specs=[row(GM_WIDTH), row(SSM_WIDTH), _full((D_MODEL, D_MODEL)), row(D_MODEL), _full((1, D_MODEL)),
                  _full((1, D_MODEL))] + [pl.BlockSpec(memory_space=pl.ANY)] * len(deps),
        out_specs=(row(D_MODEL),) * 3, compiler_params=_params("parallel"))(mix_a, mix_b, w_out, x, g2, g3, *deps)


def _mlp_fwd(h3, w_up, w_down, x2, target, g4, tm, tf):
    t_tok = x2.shape[0]

    def up_body(h_ref, wu_ref, ra_ref):
        ra_ref[...] = jnp.maximum(_dot(h_ref[...], wu_ref[...]), 0.0).astype(BF16)

    tu = min(2 * tm, t_tok)
    ra = pl.pallas_call(
        up_body, name="mlp_up", grid=(D_FF // tf, t_tok // tu), out_shape=jax.ShapeDtypeStruct((t_tok, D_FF), BF16),
        in_specs=[pl.BlockSpec((tu, D_MODEL), lambda j, i: (i, 0)), pl.BlockSpec((D_MODEL, tf), lambda j, i: (0, j))],
        out_specs=pl.BlockSpec((tu, tf), lambda j, i: (i, j)), compiler_params=_params("parallel", "parallel"))(h3, w_up)

    def down_body(ra_ref, wd_ref, x2_ref, t_ref, g4_ref, dd_ref, dy_ref, dg4_ref, loss_ref):
        i = pl.program_id(0)
        rav = ra_ref[...]
        dvec = _dot(rav * rav, wd_ref[...])
        r4 = lax.rsqrt(jnp.mean(dvec * dvec, axis=-1, keepdims=True) + EPS)
        dn = dvec * r4
        g4 = g4_ref[...]
        err = x2_ref[...] + dn * g4 - t_ref[...]
        dy = err * (1.0 / D_MODEL)
        dy_ref[...] = dy
        dg = dy * g4
        dd_ref[...] = (r4 * (dg - dn * jnp.mean(dg * dn, axis=-1, keepdims=True))).astype(BF16)
        _acc_rows(dg4_ref, _rsum(dy * dn), i == 0)
        tile_loss = 0.5 * jnp.sum(jnp.sum(err * err, axis=-1, keepdims=True), axis=0, keepdims=True) / D_MODEL
        _acc_rows(loss_ref, jnp.broadcast_to(tile_loss, (1, 128)), i == 0)

    row = pl.BlockSpec((tm, D_MODEL), lambda i: (i, 0))
    dd, dy, dg4, loss = pl.pallas_call(
        down_body, name="mlp_down", grid=(t_tok // tm,),
        out_shape=(jax.ShapeDtypeStruct((t_tok, D_MODEL), BF16), jax.ShapeDtypeStruct((t_tok, D_MODEL), F32),
                   jax.ShapeDtypeStruct((1, D_MODEL), F32), jax.ShapeDtypeStruct((1, 128), F32)),
        in_specs=[pl.BlockSpec((tm, D_FF), lambda i: (i, 0)), _full((D_FF, D_MODEL)), row, row, _full((1, D_MODEL))],
        out_specs=(row, row, _full((1, D_MODEL)), _full((1, 128))),
        compiler_params=_params("arbitrary"))(ra, w_down, x2, target, g4)
    return ra, dd, dy, dg4, loss


def _mlp_bwd(dd, w_down, ra, w_up, x2, dy, o, g3, g2, tm, tf):
    t_tok = x2.shape[0]

    def hidden_body(dd_ref, wd_ref, ra_ref, da_ref):
        df = _dot(dd_ref[...], wd_ref[...], _NT)
        da_ref[...] = (df * (2.0 * ra_ref[...].astype(F32))).astype(BF16)

    tu = min(2 * tm, t_tok)
    da = pl.pallas_call(
        hidden_body, name="mlp_bwd_hidden", grid=(D_FF // tf, t_tok // tu),
        out_shape=jax.ShapeDtypeStruct((t_tok, D_FF), BF16),
        in_specs=[pl.BlockSpec((tu, D_MODEL), lambda j, i: (i, 0)), pl.BlockSpec((tf, D_MODEL), lambda j, i: (j, 0)),
                  pl.BlockSpec((tu, tf), lambda j, i: (i, j))],
        out_specs=pl.BlockSpec((tu, tf), lambda j, i: (i, j)),
        compiler_params=_params("parallel", "parallel"))(dd, w_down, ra)

    def in_body(da_ref, wu_ref, x2_ref, dy_ref, o_ref, g3_ref, g2_ref, dx2_ref, do_ref, dg3_ref, dg2_ref):
        i = pl.program_id(0)
        dh3 = _dot(da_ref[...], wu_ref[...], _NT)
        dn3, dg3 = _rms_bwd(x2_ref[...], g3_ref[...], dh3)
        dx2 = dy_ref[...] + dn3
        dx2_ref[...] = dx2
        do, dg2 = _rms_bwd(o_ref[...], g2_ref[...], dx2)
        do_ref[...] = do.astype(BF16)
        _acc_rows(dg3_ref, dg3, i == 0)
        _acc_rows(dg2_ref, dg2, i == 0)

    row = pl.BlockSpec((tm, D_MODEL), lambda i: (i, 0))
    vec = _full((1, D_MODEL))
    sd = lambda dt: jax.ShapeDtypeStruct((t_tok, D_MODEL), dt)
    dx2, do, dg3, dg2 = pl.pallas_call(
        in_body, name="mlp_bwd_in", grid=(t_tok // tm,),
        out_shape=(sd(F32), sd(BF16), jax.ShapeDtypeStruct((1, D_MODEL), F32), jax.ShapeDtypeStruct((1, D_MODEL), F32)),
        in_specs=[pl.BlockSpec((tm, D_FF), lambda i: (i, 0)), _full((D_MODEL, D_FF)), row, row, row, vec, vec],
        out_specs=(row, row, vec, vec), compiler_params=_params("arbitrary"))(da, w_up, x2, dy, o, g3, g2)
    return da, dx2, do, dg3, dg2


def _wgrad(a, b, out_blocks, bm, bn, bk, square_a, name, dep=None):
    t_tok, m = a.shape
    n = b.shape[1]
    nk = t_tok // bk

    def body(a_ref, b_ref, *rest):
        o_ref, acc_ref = rest[-2:]
        k = pl.program_id(2)
        av = a_ref[...]
        if square_a:
            av = av * av
        part = _dot(av, b_ref[...], _TN)

        def emit(res):
            if out_blocks is None:
                o_ref[...] = res.astype(BF16)
            else:
                o_ref[0] = res.astype(BF16)

        if nk == 1:
            emit(part)
            return

        @pl.when(k == 0)
        def _():
            acc_ref[...] = part

        @pl.when(k > 0)
        def _():
            acc_ref[...] += part

        @pl.when(k == nk - 1)
        def _():
            emit(acc_ref[...])

    if out_blocks is None:
        out_shape = jax.ShapeDtypeStruct((m, n), BF16)
        out_spec = pl.BlockSpec((bm, bn), lambda i, j, k: (i, j))
    else:
        assert n // out_blocks == bn
        out_shape = jax.ShapeDtypeStruct((out_blocks, m, bn), BF16)
        out_spec = pl.BlockSpec((1, bm, bn), lambda i, j, k: (j, i, 0))
    deps = [] if dep is None else [dep]
    return pl.pallas_call(
        body, name=name, grid=(m // bm, n // bn, nk), out_shape=out_shape,
        in_specs=[pl.BlockSpec((bk, bm), lambda i, j, k: (k, i)), pl.BlockSpec((bk, bn), lambda i, j, k: (k, j))]
        + [pl.BlockSpec(memory_space=pl.ANY)] * len(deps),
        out_specs=out_spec, scratch_shapes=[pltpu.VMEM((bm, bn) if nk > 1 else (8, 128), F32)],
        compiler_params=_params("parallel", "parallel", "arbitrary"))(a, b, *deps)


def _wgrad_in(h1, pieces, bn, name, dep=None):
    t_tok = h1.shape[0]
    widths = [p.shape[1] for p in pieces]
    starts = [sum(widths[:i]) for i in range(len(widths))]

    def body(h_ref, *rest):
        piece_refs = rest[:len(widths)]
        o_ref = rest[-1]
        hv = h_ref[...]
        for a, n, r in zip(starts, widths, piece_refs):
            o_ref[a:a + n, :] = _dot(r[...], hv, _TN).astype(BF16)

    deps = [] if dep is None else [dep]
    return pl.pallas_call(
        body, name=name, grid=(D_MODEL // bn,), out_shape=jax.ShapeDtypeStruct((sum(widths), D_MODEL), BF16),
        in_specs=[pl.BlockSpec((t_tok, bn), lambda j: (0, j))] + [pl.BlockSpec((t_tok, n), lambda j: (0, 0)) for n in widths]
        + [pl.BlockSpec(memory_space=pl.ANY)] * len(deps),
        out_specs=pl.BlockSpec((sum(widths), bn), lambda j: (0, j)),
        compiler_params=_params("parallel"))(h1, *pieces, *deps)


def _dmix(do, w_out, tm, dep=None):
    t_tok = do.shape[0]

    def body(d_ref, w_ref, *rest):
        rest[-1][...] = _dot(d_ref[...], w_ref[...], _NT).astype(BF16)

    row = pl.BlockSpec((tm, D_MODEL), lambda i: (i, 0))
    deps = [] if dep is None else [dep]
    return pl.pallas_call(
        body, name="dmix", grid=(t_tok // tm,), out_shape=jax.ShapeDtypeStruct((t_tok, D_MODEL), BF16),
        in_specs=[row, _full((D_MODEL, D_MODEL))] + [pl.BlockSpec(memory_space=pl.ANY)] * len(deps), out_specs=row,
        compiler_params=_params("parallel"))(do, w_out, *deps)


def _gmlp_bwd(dmix, u, v, lnw, lnb, wcat, wtcat, bias, avg, expand_t):
    t_tok = u.shape[0]
    tm = min(_GMLP_ROWS, t_tok)

    def body(dm_ref, u_ref, v_ref, lnw_ref, lnb_ref, wcat_ref, wtcat_ref, bias_ref, avg_ref, expt_ref, du_ref, dv_ref,
             dw_ref, db_ref, dlnw_ref, dlnb_ref):
        i = pl.program_id(0)
        m_l, m_r = _lane_masks()
        avg = avg_ref[...]
        lnw = lnw_ref[...]
        ug, dug, dvg, rstd, vhat, vn, mixed = _gmlp_common(
            u_ref[...].astype(F32), v_ref[...].astype(F32), lnw, lnb_ref[...], avg, wcat_ref, bias_ref[...], m_l, m_r)
        dya = dm_ref[...].astype(F32)
        du_ref[...] = (dya * mixed * dug).astype(BF16)
        dmixed = dya * ug
        dvn_rows, dws, dbt = [], [None] * N_HEADS, None
        for r in range(tm // CHUNK):
            dvn_cols = []
            for j in range(N_HEADS // 2):
                dmp = dmixed[CHUNK * r:CHUNK * (r + 1), 128 * j:128 * (j + 1)]
                dvn_cols.append(_dot(wtcat_ref[j], _stack_pair(dmp, m_l, m_r)))
                vnp = vn[CHUNK * r:CHUNK * (r + 1), 128 * j:128 * (j + 1)].astype(BF16)
                for i_h, mask in enumerate((m_l, m_r)):
                    part = _dot((dmp * mask).astype(BF16), vnp, _NT)
                    dws[2 * j + i_h] = part if r == 0 else dws[2 * j + i_h] + part
            dvn_rows.append(jnp.concatenate(dvn_cols, axis=1))
            part = _split_dot(dmixed[CHUNK * r:CHUNK * (r + 1), :], expt_ref[...], 2)
            dbt = part if r == 0 else dbt + part
        dvn = jnp.concatenate(dvn_rows, axis=0)
        dvh = dvn * lnw
        dvgel = rstd * (dvh - _head_mean(dvh, avg) - vhat * _head_mean(dvh * vhat, avg))
        dv_ref[...] = (dvgel * dvg).astype(BF16)
        first = i == 0

        @pl.when(first)
        def _():
            for h in range(N_HEADS):
                dw_ref[h] = dws[h]
            db_ref[...] = dbt

        @pl.when(jnp.logical_not(first))
        def _():
            for h in range(N_HEADS):
                dw_ref[h] += dws[h]
            db_ref[...] += dbt

        _acc_rows(dlnw_ref, _rsum(dvn * vhat), first)
        _acc_rows(dlnb_ref, _rsum(dvn), first)

    row = pl.BlockSpec((tm, GM_WIDTH), lambda i: (i, 0))
    consts = [lnw, lnb, wcat, wtcat, bias, avg, expand_t]
    return pl.pallas_call(
        body, name="gmlp_bwd", grid=(t_tok // tm,),
        out_shape=(jax.ShapeDtypeStruct((t_tok, GM_WIDTH), BF16), jax.ShapeDtypeStruct((t_tok, GM_WIDTH), BF16),
                   jax.ShapeDtypeStruct((N_HEADS, CHUNK, CHUNK), F32), jax.ShapeDtypeStruct((CHUNK, CHUNK), F32),
                   jax.ShapeDtypeStruct((1, GM_WIDTH), F32), jax.ShapeDtypeStruct((1, GM_WIDTH), F32)),
        in_specs=[row, row, row] + [_full(a.shape) for a in consts],
        out_specs=(row, row, _full((N_HEADS, CHUNK, CHUNK)), _full((CHUNK, CHUNK)), _full((1, GM_WIDTH)),
                   _full((1, GM_WIDTH))),
        compiler_params=_params("arbitrary"))(dmix, u, v, *consts)


def _ssd_bwd(dmix, z, xbc, pre, dtr, y, states, cw, cb, dtb, alog, dskip_exp, nw, expand, expand_t, tril, triu, seq,
             dep=None):
    t_tok = z.shape[0]
    nb, nc, row, _, states_spec, fold, unfold = _ssd_specs(t_tok, seq, True)
    q = CHUNK

    def one_sequence(s, dm_ref, z_ref, xbc_ref, pre_ref, dtr_ref, y_ref, st_ref, cw_ref, dtb_ref, alog_ref, dsk_ref,
                     nw_ref, exp_ref, expt_ref, tril_ref, triu_ref, dz_ref, dxbc_ref, ddt_ref, dhead_ref, dstate_ref):
        m_l, m_r = _lane_masks()
        expt = expt_ref[...]
        f = _ssd_common(pre_ref[s], dtr_ref[s], dtb_ref[...], alog_ref[...], exp_ref[...], tril_ref[...])
        act, pre, sg = f["act"], f["pre"], f["sg"]
        xs = act[:, :SSM_WIDTH]
        xdt = xs * f["dt_exp"]
        xw = xdt * f["w_end"]
        state = st_ref[s, 0]
        dstate = dstate_ref[s]
        zv, yv, dout, nw = z_ref[s].astype(F32), y_ref[s], dm_ref[s].astype(F32), nw_ref[...]
        sz = jax.nn.sigmoid(zv)
        sl = zv * sz
        yg = yv * sl
        tv = dout * nw
        dyg_parts, ygh_parts = [], []
        for g in range(2):
            ygg = yg[:, 256 * g:256 * (g + 1)]
            rr = lax.rsqrt(jnp.mean(ygg * ygg, axis=-1, keepdims=True) + EPS)
            ygh = ygg * rr
            tg = tv[:, 256 * g:256 * (g + 1)]
            dyg_parts.append(rr * (tg - ygh * jnp.mean(tg * ygh, axis=-1, keepdims=True)))
            ygh_parts.append(ygh)
        dyg = jnp.concatenate(dyg_parts, axis=1)
        dnw = _rsum(dout * jnp.concatenate(ygh_parts, axis=1))
        dy = dyg * sl
        dz_ref[s] = (dyg * yv * (sz * (1.0 + zv * (1.0 - sz)))).astype(BF16)
        ddsk = _rsum(dy * xs)
        dye = dy * f["e"]
        lane = lax.broadcasted_iota(jnp.int32, (q, q), 1)
        sub = lax.broadcasted_iota(jnp.int32, (q, q), 0)
        rs_mat = jnp.zeros((q, q), F32)
        cs_mat = jnp.zeros((q, q), F32)
        dxdt_cols, yoff, dst_in, dxw, d_b, d_c = [], [], [], [], [], []
        for g in range(2):
            bg = act[:, 512 + 128 * g:640 + 128 * g].astype(BF16)
            cg = act[:, 768 + 128 * g:896 + 128 * g].astype(BF16)
            cb_mat = _dot(cg, bg, _NT)
            stg = state[:, 256 * g:256 * (g + 1)].astype(BF16)
            dyeg = dye[:, 256 * g:256 * (g + 1)].astype(BF16)
            yoff.append(_dot(cg, stg))
            dcg = _dot(dyeg, stg, _NT)
            dst_in.append(_dot(cg, dyeg, _TN))
            dcb = jnp.zeros((q, q), F32)
            for pr in range(2):
                h0 = 4 * g + 2 * pr
                gf = [cb_mat * f["decay"][h0], cb_mat * f["decay"][h0 + 1]]
                gcat = jnp.concatenate([gf[0].astype(BF16), gf[1].astype(BF16)], axis=1)
                xst = _stack_pair(xdt[:, 64 * h0:64 * h0 + 128], m_l, m_r)
                dyp = dy[:, 64 * h0:64 * h0 + 128].astype(BF16)
                dgcat = _dot(dyp, xst, _NT)
                dxst = _dot(gcat, dyp, _TN)
                dxdt_cols.append(dxst[:q] * m_l + dxst[q:] * m_r)
                for i in range(2):
                    h = h0 + i
                    dg = dgcat[:, q * i:q * (i + 1)]
                    mm = dg * gf[i]
                    rs_mat = rs_mat + jnp.where(lane == h, jnp.sum(mm, axis=1, keepdims=True), 0.0)
                    cs_mat = cs_mat + jnp.where(sub == h, jnp.sum(mm, axis=0, keepdims=True), 0.0)
                    dcb = dcb + dg * f["decay"][h]
            dcb16 = dcb.astype(BF16)
            dstg = dstate[:, 256 * g:256 * (g + 1)].astype(BF16)
            d_c.append(dcg + _dot(dcb16, bg))
            dxw.append(_dot(bg, dstg))
            d_b.append(_dot(dcb16, cg, _TN) + _dot(xw[:, 256 * g:256 * (g + 1)].astype(BF16), dstg, _NT))
        dxw = jnp.concatenate(dxw, axis=1)
        dxdt = jnp.concatenate(dxdt_cols, axis=1) + dxw * f["w_end"]
        qv = dxw * xw
        end_row = _rsum(qv) + _rsum(dstate * state) * f["cd"]
        x2 = dye * jnp.concatenate(yoff, axis=1) - qv
        row_i = lax.broadcasted_iota(jnp.int32, (q, 1), 0)
        x2 = x2 + jnp.where(row_i == q - 1, end_row, 0.0)
        da_cs = _split_dot(x2, expt, 2) + rs_mat - cs_mat.T
        ddt = _split_dot(dxdt * xs, expt, 2)
        dxs = dsk_ref[...] * dy + dxdt * f["dt_exp"]
        dda = _split_dot_left(triu_ref[...], da_cs, 3)
        ddt = ddt + dda * f["a_row"]
        dalog = _rsum(dda * f["dt"]) * f["a_row"]
        draw = ddt * jax.nn.sigmoid(f["dtp"])
        ddt_ref[s] = draw.astype(BF16)
        dact = jnp.concatenate([dxs] + d_b + d_c, axis=1)
        dpre = dact * (sg * (1.0 + pre * (1.0 - sg)))
        dhead = dhead_ref[s]
        xv = xbc_ref[s]
        shifted = [_shift_rows(dpre, dhead, 3 - k, False) for k in range(3)] + [dpre]
        dxbc = cw_ref[3:4, :] * dpre
        for k in range(3):
            dxbc = dxbc + cw_ref[k:k + 1, :] * shifted[k]
        dxbc_ref[s] = dxbc.astype(BF16)
        dhead_ref[s] = dpre[0:8, :]
        dstate_ref[s] = dstate * f["cd"] + jnp.concatenate(dst_in, axis=1)
        row8 = lax.broadcasted_iota(jnp.int32, (8, 1), 0)
        dcw = jnp.zeros((8, CONV_CH), F32)
        for k in range(4):
            dcw = dcw + jnp.where(row8 == k, _rsum(shifted[k] * xv), 0.0)
        return dcw, _rsum(dpre), _rsum(draw), dalog, _split_dot(ddsk, expt, 3), dnw

    def body(dm_ref, z_ref, xbc_ref, pre_ref, dtr_ref, y_ref, st_ref, cw_ref, cb_ref, dtb_ref, alog_ref, dsk_ref,
             nw_ref, exp_ref, expt_ref, tril_ref, triu_ref, dz_ref, dxbc_ref, ddt_ref, dcw_ref, dcb_ref, ddtb_ref,
             dalog_ref, dd_ref, dnw_ref, dhead_ref, dstate_ref):
        c = pl.program_id(0)
        first = c == 0

        @pl.when(first)
        def _():
            dstate_ref[...] = jnp.zeros_like(dstate_ref)
            dhead_ref[...] = jnp.zeros_like(dhead_ref)

        total = None
        for s in range(nb):
            parts = one_sequence(s, dm_ref, z_ref, xbc_ref, pre_ref, dtr_ref, y_ref, st_ref, cw_ref, dtb_ref, alog_ref,
                                 dsk_ref, nw_ref, exp_ref, expt_ref, tril_ref, triu_ref, dz_ref, dxbc_ref, ddt_ref,
                                 dhead_ref, dstate_ref)
            total = parts if total is None else tuple(a + b for a, b in zip(total, parts))
        dcw = total[0]

        @pl.when(first)
        def _():
            dcw_ref[...] = dcw

        @pl.when(jnp.logical_not(first))
        def _():
            dcw_ref[...] += dcw

        for ref, part in zip((dcb_ref, ddtb_ref, dalog_ref, dd_ref, dnw_ref), total[1:]):
            _acc_rows(ref, part, first)

    consts = [cw, cb, dtb, alog, dskip_exp, nw, expand, expand_t, tril, triu]
    deps = [] if dep is None else [dep]
    n_in = 7 + len(consts)

    def body_skipping_dep(*refs):
        body(*refs[:n_in], *refs[n_in + len(deps):])

    acc = lambda n: jax.ShapeDtypeStruct((1, n), F32)
    sd = lambda n: jax.ShapeDtypeStruct((nb, seq, n), BF16)
    dz, dxbc, ddt, *small_grads = pl.pallas_call(
        body_skipping_dep, name="ssd_bwd", grid=(nc,),
        out_shape=(sd(SSM_WIDTH), sd(CONV_CH), sd(CHUNK), jax.ShapeDtypeStruct((8, CONV_CH), F32), acc(CONV_CH),
                   acc(CHUNK), acc(CHUNK), acc(CHUNK), acc(SSM_WIDTH)),
        in_specs=[row(SSM_WIDTH, col=1), row(SSM_WIDTH), row(CONV_CH), row(CONV_CH), row(CHUNK), row(SSM_WIDTH),
                  states_spec]
        + [_full(a.shape) for a in consts] + [pl.BlockSpec(memory_space=pl.ANY)] * len(deps),
        out_specs=(row(SSM_WIDTH), row(CONV_CH), row(CHUNK), _full((8, CONV_CH)), _full((1, CONV_CH)),
                   _full((1, CHUNK)), _full((1, CHUNK)), _full((1, CHUNK)), _full((1, SSM_WIDTH))),
        scratch_shapes=[pltpu.VMEM((nb, 8, CONV_CH), F32), pltpu.VMEM((nb, N_STATE, SSM_WIDTH), F32)],
        compiler_params=_params("arbitrary"))(
            fold(dmix), fold(z), fold(xbc), fold(pre), fold(dtr), fold(y), states, *consts, *deps)
    return (unfold(dz), unfold(dxbc), unfold(ddt), *small_grads)


def _in_bwd(du, dv, dz, dxbc, ddt, w_in, nh, r, dx2, g1, tm, me, riders=(), dep=None):
    t_tok = nh.shape[0]
    steps = t_tok // tm

    n_in = [5 + ("mask" in rd) for rd in riders]
    first_in = [sum(n_in[:r]) for r in range(len(riders))]

    def body(me_ref, du_ref, dv_ref, dz_ref, dxbc_ref, ddt_ref, w_ref, nh_ref, r_ref, dx2_ref, g_ref, *rest):
        outs = rest[len(rest) - 2 - 4 * len(riders):]
        gx_ref, dg_ref = outs[:2]
        i = pl.program_id(0)
        dh = None
        for (a, b), ref in zip(_IN_SPLITS, (du_ref, dv_ref, dz_ref, dxbc_ref, ddt_ref)):
            part = _dot(ref[...], w_ref[a:b, :])
            dh = part if dh is None else dh + part
        nhv = nh_ref[...].astype(F32)
        dgv = dh * g_ref[...]
        gx_ref[...] = dx2_ref[...] + r_ref[...] * (dgv - nhv * jnp.mean(dgv * nhv, axis=-1, keepdims=True))
        _acc_rows(dg_ref, _rsum(dh * nhv), i == 0)
        for r in range(len(riders)):
            p_ref, own_ref, w_ref_r, m_ref_r, v_ref_r = rest[first_in[r]:first_in[r] + 5]
            g = _sum_parts(me_ref[0], p_ref, own_ref[0])
            if n_in[r] == 6:
                g = g * rest[first_in[r] + 5][...]
            d, mn, vn = _adamw_math(w_ref_r[...], g, m_ref_r[...], v_ref_r[...])
            for o_ref, val in zip(outs[2 + 4 * r:6 + 4 * r], (g, d, mn, vn)):
                o_ref[...] = val

    row = lambda n: pl.BlockSpec((tm, n), lambda i, me_ref: (i, 0))
    whole = lambda shape: pl.BlockSpec(shape, lambda i, me_ref: (0,) * len(shape))
    widths = [b - a for a, b in _IN_SPLITS]
    deps = [] if dep is None else [dep]
    rider_args, rider_specs, rider_out_shapes, rider_out_specs = [], [], [], []
    for rd in riders:
        rows, cols = rd["w"].shape[0] // steps, rd["w"].shape[1]
        blk = pl.BlockSpec((rows, cols), lambda i, me_ref: (i, 0))
        rider_args += [rd["parts"], rd["own"], rd["w"], rd["m"], rd["v"]]
        rider_specs += [pl.BlockSpec((N_DEV, rows, cols), lambda i, me_ref: (0, i, 0)),
                        pl.BlockSpec((1, rows, cols), lambda i, me_ref: (me_ref[0], i, 0)), blk, blk, blk]
        if "mask" in rd:
            rider_args.append(rd["mask"])
            rider_specs.append(whole((rows, cols)))
        rider_out_shapes += [jax.ShapeDtypeStruct(rd["w"].shape, F32)] * 4
        rider_out_specs += [blk] * 4
    outs = pl.pallas_call(
        body, name="in_bwd",
        out_shape=(jax.ShapeDtypeStruct((t_tok, D_MODEL), F32), jax.ShapeDtypeStruct((1, D_MODEL), F32),
                   *rider_out_shapes),
        grid_spec=pltpu.PrefetchScalarGridSpec(
            num_scalar_prefetch=1, grid=(steps,),
            in_specs=[row(n) for n in widths] + [whole((IN_PAD, D_MODEL)), row(D_MODEL), row(1), row(D_MODEL),
                                                 whole((1, D_MODEL))] + rider_specs
            + [pl.BlockSpec(memory_space=pl.ANY)] * len(deps),
            out_specs=(row(D_MODEL), whole((1, D_MODEL)), *rider_out_specs)),
        compiler_params=_params("arbitrary"))(me, du, dv, dz, dxbc, ddt, w_in, nh, r, dx2, g1, *rider_args, *deps)
    return outs[0], outs[1], [tuple(outs[2 + 4 * r:6 + 4 * r]) for r in range(len(riders))]


def _pad_lanes(a, n):
    return jnp.pad(a, ((0, 0), (0, n - a.shape[1])))


def _local_step(x, target, seq, small, hooks, first_dep=None):
    t_tok = x.shape[0]
    tm = min(TOKEN_TILE, t_tok)
    avg, expand, expand_t, tril, triu = _const_mats()
    g1, g2, g3, g4 = (small[k].reshape(1, D_MODEL) for k in
                      ("norm_mix_pre", "norm_mix_post", "norm_ffn_pre", "norm_ffn_post"))
    tie = (lambda a: a) if first_dep is None else (lambda a: a + first_dep[0, 0])
    lnw = tie(small["gm_ln_w"]).reshape(1, GM_WIDTH)
    lnb = tie(small["gm_ln_b"]).reshape(1, GM_WIDTH)
    causal = jnp.tril(jnp.ones((CHUNK, CHUNK), F32))
    wm = tie(small["gm_w_s"]) * causal
    pair = lambda w: w.reshape(4, 2, CHUNK, CHUNK).transpose(0, 2, 1, 3).reshape(4, CHUNK, 2 * CHUNK).astype(BF16)
    wcat = pair(wm)
    wtcat = pair(jnp.swapaxes(wm, 1, 2))
    bias = jnp.repeat(tie(small["gm_b_s"]).T, HEAD_DIM, axis=1)
    cb = small["conv_b"].reshape(1, CONV_CH)
    dtb = _pad_lanes(tie(small["dt_bias"]).reshape(1, N_HEADS), CHUNK)
    alog = _pad_lanes(tie(small["a_log"]).reshape(1, N_HEADS), CHUNK)
    dskip_exp = jnp.repeat(tie(small["d_skip"]).reshape(1, N_HEADS), HEAD_DIM, axis=1)
    nw = small["ssm_norm_w"].reshape(1, SSM_WIDTH)

    h1, nh1, r1 = _prenorm(x, g1, tm, hooks.get("prenorm_after", first_dep))
    w_in_t, conv_w = hooks["mixer_weights"](h1)
    tall = min(2 * tm, t_tok)
    u, v, z, xbc, dtr = _in_proj(h1, w_in_t, tall)
    mix_a = _gmlp_fwd(u, v, lnw, lnb, wcat, bias, avg)
    mix_b, y_pre, states, pre = _ssd_fwd(z, xbc, dtr, conv_w, cb, dtb, alog, dskip_exp, nw, expand, tril, seq)
    w_out, dep = hooks["mixers_done"](mix_b)
    o, x2, h3 = _out_proj(mix_a, mix_b, w_out, x, g2, g3, tall, dep)
    w_up, w_down = hooks["mlp_weights"](h3)
    tf = FF_TILE
    ra, dd, dy, dg4, loss = _mlp_fwd(h3, w_up, w_down, x2, target, g4, tm, tf)

    da, dx2, do, dg3, dg2 = _mlp_bwd(dd, w_down, ra, w_up, x2, dy, o, g3, g2, tm, tf)
    g_w_down = _wgrad(ra, dd, None, WGRAD_TILE, D_MODEL, t_tok, True, "wgrad_down")
    g_w_up = _wgrad(h3, da, N_DEV, D_MODEL, D_FF // N_DEV, t_tok, False, "wgrad_up")
    dep = hooks["mlp_grads"](g_w_down, g_w_up)
    dmix = _dmix(do, w_out, tall, dep)
    g_w_out = _wgrad_in(do, (mix_a, mix_b), WGRAD_TILE, "wgrad_out", dep)
    du, dv, dws, dbt, dlnw, dlnb = _gmlp_bwd(dmix, u, v, lnw, lnb, wcat, wtcat, bias, avg, expand_t)
    dep = hooks["gmlp_grads"](g_w_out, dws)
    dz, dxbc, ddt, dcw, dcb, ddtb, dalog, ddsk, dnw = _ssd_bwd(
        dmix, z, xbc, pre, dtr, y_pre, states, conv_w, cb, dtb, alog, dskip_exp, nw, expand, expand_t, tril, triu, seq,
        dep)
    g_w_in = jnp.concatenate([_wgrad_in(h1, (du, dv, dz), WGRAD_TILE // 2, "wgrad_in_a", dep),
                              _wgrad_in(h1, (dxbc, ddt), WGRAD_TILE // 2, "wgrad_in_b", dep)], axis=0)
    dep = hooks["in_grads"](g_w_in, dcw[0:4])
    riders = hooks["arrived_updates"](dep) if "arrived_updates" in hooks else []
    me = hooks.get("me", jnp.zeros((1,), jnp.int32))
    grad_x, dg1, updates = _in_bwd(du, dv, dz, dxbc, ddt, w_in_t, nh1, r1, dx2, g1, tm, me, riders, dep)

    grads = dict(
        updates=updates,
        w_in=g_w_in, w_out=g_w_out, w_up=g_w_up, w_down=g_w_down, conv_w=dcw[0:4],
        norm_mix_pre=dg1, norm_mix_post=dg2, norm_ffn_pre=dg3, norm_ffn_post=dg4, gm_ln_w=dlnw, gm_ln_b=dlnb,
        gm_w_s=dws, gm_b_s=dbt, conv_b=dcb, dt_bias=ddtb, a_log=dalog, d_skip=ddsk, ssm_norm_w=dnw)
    return loss[0, 0], grad_x, grads


_WEIGHTS = ("norm_mix_pre", "w_in", "gm_ln_w", "gm_ln_b", "gm_w_s", "gm_b_s", "conv_w", "conv_b", "dt_bias", "a_log",
            "d_skip", "ssm_norm_w", "w_out", "norm_mix_post", "norm_ffn_pre", "w_up", "w_down", "norm_ffn_post")
_SLAB_ROWS = (("norm_mix_pre", 1024), ("norm_mix_post", 1024), ("norm_ffn_pre", 1024), ("norm_ffn_post", 1024),
              ("conv_b", 1024), ("ssm_norm_w", 512), ("gm_ln_w", 512), ("gm_ln_b", 512), ("dt_bias", 8), ("a_log", 8),
              ("d_skip", 8))
_SLAB_LOSS_ROW = len(_SLAB_ROWS)
_SLAB_BS_ROW = 16
_SMALL_PARAMS = tuple(name for name, _ in _SLAB_ROWS) + ("gm_b_s",)
_LN_PARAMS = ("gm_ln_w", "gm_ln_b")


_SLAB_CONV_ROW = _SLAB_LOSS_ROW + 1


def _pack_slab(g, loss_part):
    rows = [_pad_lanes(g[name], D_MODEL) for name, _ in _SLAB_ROWS]
    rows.append(jnp.broadcast_to(loss_part, (1, D_MODEL)))
    rows.append(g["conv_w"])
    assert sum(r.shape[0] for r in rows) == _SLAB_BS_ROW
    rows.append(_pad_lanes(g["gm_b_s"].T[0:N_HEADS], D_MODEL))
    return jnp.concatenate(rows, axis=0)


def _adamw_slab(parts, me, w, m, v):
    names = _SMALL_PARAMS + ("conv_w",)
    shapes = [w[k].shape for k in names]
    unfold = np.zeros((GM_WIDTH, HEAD_DIM), np.float32)
    for h in range(N_HEADS):
        unfold[h * HEAD_DIM:(h + 1) * HEAD_DIM, :] = np.eye(HEAD_DIM)
    unfold = jnp.asarray(unfold, dtype=BF16)
    n = len(names)
    shard = CONV_CH // N_DEV

    def body(me_ref, p_ref, unfold_ref, *refs):
        w_refs, m_refs, v_refs = refs[:n], refs[n:2 * n], refs[2 * n:3 * n]
        outs = refs[3 * n:]
        g_all = p_ref[0]
        for j in range(1, N_DEV):
            g_all = g_all + p_ref[j]
        lane = lax.broadcasted_iota(jnp.int32, (N_HEADS, GM_WIDTH), 1)
        head = lax.broadcasted_iota(jnp.int32, (N_HEADS, GM_WIDTH), 0)
        own_lanes = jnp.logical_and(lane >= head * HEAD_DIM, lane < (head + 1) * HEAD_DIM)
        mine = pl.ds(pl.multiple_of(me_ref[0] * shard, shard), shard)
        for i, name in enumerate(names):
            if name == "gm_b_s":
                g = g_all[_SLAB_BS_ROW:_SLAB_BS_ROW + N_HEADS, 0:CHUNK]
            elif name == "conv_w":
                g = p_ref[0, _SLAB_CONV_ROW:_SLAB_CONV_ROW + 4, mine]
                for j in range(1, N_DEV):
                    g = g + p_ref[j, _SLAB_CONV_ROW:_SLAB_CONV_ROW + 4, mine]
            else:
                row = [r for r, (k, _) in enumerate(_SLAB_ROWS) if k == name][0]
                g = g_all[row:row + 1, 0:dict(_SLAB_ROWS)[name]]
                if name in _LN_PARAMS:
                    g = _split_dot(jnp.where(own_lanes, g, 0.0), unfold_ref[...], 3)
            d, mn, vn = _adamw_math(w_refs[i][...], g, m_refs[i][...], v_refs[i][...])
            for o_ref, val in zip(outs[4 * i:4 * i + 4], (g, d, mn, vn)):
                o_ref[...] = val
        outs[-1][...] = g_all[_SLAB_LOSS_ROW:_SLAB_LOSS_ROW + 1, 0:128]

    def whole(shape):
        nd = len(shape)
        return pl.BlockSpec(shape, lambda i, me_ref: (0,) * nd)

    ins = [parts, unfold] + [d[k] for d in (w, m, v) for k in names]
    out_shape = tuple(jax.ShapeDtypeStruct(s, F32) for s in shapes for _ in range(4)) + (
        jax.ShapeDtypeStruct((1, 128), F32),)
    outs = pl.pallas_call(
        body, name="adamw_small", out_shape=out_shape,
        grid_spec=pltpu.PrefetchScalarGridSpec(
            num_scalar_prefetch=1, grid=(1,), in_specs=[whole(a.shape) for a in ins],
            out_specs=tuple(whole(s.shape) for s in out_shape)),
        compiler_params=_params("arbitrary"))(me, *ins)
    return {k: tuple(outs[4 * i:4 * i + 4]) for i, k in enumerate(names)}, outs[-1][0, 0]


def kernel(x, norm_mix_pre, w_in, gm_ln_w, gm_ln_b, gm_w_s, gm_b_s, conv_w, conv_b, dt_bias, a_log, d_skip, ssm_norm_w, w_out, norm_mix_post, norm_ffn_pre, w_up, w_down, norm_ffn_post, loss_target, m_norm_mix_pre, m_w_in, m_gm_ln_w, m_gm_ln_b, m_gm_w_s, m_gm_b_s, m_conv_w, m_conv_b, m_dt_bias, m_a_log, m_d_skip, m_ssm_norm_w, m_w_out, m_norm_mix_post, m_norm_ffn_pre, m_w_up, m_w_down, m_norm_ffn_post, v_norm_mix_pre, v_w_in, v_gm_ln_w, v_gm_ln_b, v_gm_w_s, v_gm_b_s, v_conv_w, v_conv_b, v_dt_bias, v_a_log, v_d_skip, v_ssm_norm_w, v_w_out, v_norm_mix_post, v_norm_ffn_pre, v_w_up, v_w_down, v_norm_ffn_post):
    w = dict(norm_mix_pre=norm_mix_pre, w_in=w_in, gm_ln_w=gm_ln_w, gm_ln_b=gm_ln_b, gm_w_s=gm_w_s, gm_b_s=gm_b_s, conv_w=conv_w, conv_b=conv_b, dt_bias=dt_bias, a_log=a_log, d_skip=d_skip, ssm_norm_w=ssm_norm_w, w_out=w_out, norm_mix_post=norm_mix_post, norm_ffn_pre=norm_ffn_pre, w_up=w_up, w_down=w_down, norm_ffn_post=norm_ffn_post)
    m = dict(norm_mix_pre=m_norm_mix_pre, w_in=m_w_in, gm_ln_w=m_gm_ln_w, gm_ln_b=m_gm_ln_b, gm_w_s=m_gm_w_s, gm_b_s=m_gm_b_s, conv_w=m_conv_w, conv_b=m_conv_b, dt_bias=m_dt_bias, a_log=m_a_log, d_skip=m_d_skip, ssm_norm_w=m_ssm_norm_w, w_out=m_w_out, norm_mix_post=m_norm_mix_post, norm_ffn_pre=m_norm_ffn_pre, w_up=m_w_up, w_down=m_w_down, norm_ffn_post=m_norm_ffn_post)
    v = dict(norm_mix_pre=v_norm_mix_pre, w_in=v_w_in, gm_ln_w=v_gm_ln_w, gm_ln_b=v_gm_ln_b, gm_w_s=v_gm_w_s, gm_b_s=v_gm_b_s, conv_w=v_conv_w, conv_b=v_conv_b, dt_bias=v_dt_bias, a_log=v_a_log, d_skip=v_d_skip, ssm_norm_w=v_ssm_norm_w, w_out=v_w_out, norm_mix_post=v_norm_mix_post, norm_ffn_pre=v_norm_ffn_pre, w_up=v_w_up, w_down=v_w_down, norm_ffn_post=v_norm_ffn_post)
    n_batch, seq, _ = x.shape
    shard_in = IN_COLS // N_DEV

    me = (4 * lax.axis_index("x") + 2 * lax.axis_index("y") + lax.axis_index("c")).astype(jnp.int32).reshape(1)

    def in_slot(own):
        return lax.dynamic_update_slice(lax.empty((N_DEV,) + own.shape, own.dtype), own[None],
                                        (me[0],) + (0,) * own.ndim)

    w_in_sh, m_in_sh, v_in_sh = w_in[0].T, m_w_in[0].T, v_w_in[0].T
    first = [_cast_to_slot(w_in_sh, me, shard_in, "cast_w_in"), in_slot(conv_w[0]),
             _cast_to_slot(w_out[0], me, 128, "cast_w_out")]
    ici_1, tok_ici_1 = _exchange_start(first, [True] * 3, _SAME_CORE_PEERS, "gather_mix_ici_start")
    cast_up = _cast_to_slot(w_up[0], me, 1024, "cast_w_up", cols=True, dep=tok_ici_1)
    second = [cast_up, _cast_to_slot(w_down[0], me, 512, "cast_w_down", dep=cast_up)]
    gathering = {}

    def mixer_weights(after):
        bufs = [buf for buf, _ in _exchange_wait(ici_1, after, "gather_mix_ici_wait")]
        d2d_1, tok_d2d_1 = _exchange_start(bufs, [True] * 3, _SIBLING_FORWARD, "gather_mix_d2d_start")
        gathering["mlp_ici"], tok_ici_2 = _exchange_start(
            second, [True] * 2, _SAME_CORE_PEERS, "gather_mlp_ici_start", dep=tok_d2d_1)
        (_, ag_in), (_, ag_conv), (_, ag_out) = _exchange_wait(d2d_1, tok_ici_2, "gather_mix_d2d_wait")
        gathering["w_out"] = ag_out.reshape(D_MODEL, D_MODEL)
        w_in_t = jnp.pad(ag_in.reshape(IN_COLS, D_MODEL), ((0, IN_PAD - IN_COLS), (0, 0)))
        return w_in_t, ag_conv.transpose(1, 0, 2).reshape(4, CONV_CH)

    def mixers_done(after):
        bufs = [buf for buf, _ in _exchange_wait(gathering["mlp_ici"], after, "gather_mlp_ici_wait")]
        gathering["mlp"], tok = _exchange_start(bufs, [True] * 2, _SIBLING_FORWARD, "gather_mlp_d2d_start")
        return gathering["w_out"], tok

    def mlp_weights(after):
        (_, ag_up), (_, ag_down) = _exchange_wait(gathering["mlp"], after, "gather_mlp_d2d_wait")
        return ag_up, ag_down.reshape(D_FF, D_MODEL)

    sent = {}

    def mlp_grads(g_w_down, g_w_up):
        sent["mlp"], tok = _exchange_start(
            [g_w_down.reshape(N_DEV, D_FF // N_DEV, D_MODEL), g_w_up], [False, False], _ALL_PEERS, "grads_mlp_start")
        return tok

    def gmlp_grads(g_w_out, g_w_s):
        sent["gmlp"], tok = _exchange_start(
            [g_w_out.reshape(N_DEV, D_MODEL // N_DEV, D_MODEL), in_slot(g_w_s.astype(BF16))], [False, True], _ALL_PEERS,
            "grads_gmlp_start")
        return tok

    def in_grads(g_w_in_t, g_conv_w):
        g_in_blk = g_w_in_t[:IN_COLS].reshape(N_DEV, shard_in, D_MODEL)
        sent["in"], tok = _exchange_start([g_in_blk], [False], _ALL_PEERS, "grads_in_start")
        return tok

    def arrived_updates(after):
        (own_down, p_down), (own_up, p_up) = _exchange_wait(sent["mlp"], after, "grads_mlp_wait")
        (own_out, p_out), (_, p_ws) = _exchange_wait(sent["gmlp"], own_up, "grads_gmlp_wait")
        rows = lambda t: t.reshape(t.shape[:-3] + (N_HEADS * CHUNK, CHUNK))
        return [dict(parts=p_up, own=own_up, w=w_up[0], m=m_w_up[0], v=v_w_up[0]),
                dict(parts=p_down, own=own_down, w=w_down[0], m=m_w_down[0], v=v_w_down[0]),
                dict(parts=p_out, own=own_out, w=w_out[0], m=m_w_out[0], v=v_w_out[0]),
                dict(parts=rows(p_ws), own=rows(p_ws), w=rows(gm_w_s[0]), m=rows(m_gm_w_s[0]), v=rows(v_gm_w_s[0]),
                     mask=jnp.tril(jnp.ones((CHUNK, CHUNK), F32)))]

    small = {k: w[k][0] for k in _SMALL_PARAMS + ("gm_w_s",)}
    loss_part, grad_x, g = _local_step(
        x.reshape(n_batch * seq, D_MODEL), loss_target.reshape(n_batch * seq, D_MODEL), seq, small,
        dict(mixer_weights=mixer_weights, mixers_done=mixers_done, mlp_weights=mlp_weights, mlp_grads=mlp_grads,
             gmlp_grads=gmlp_grads, in_grads=in_grads, arrived_updates=arrived_updates, me=me,
             prenorm_after=second[1]), first_dep=tok_ici_1)

    sent_rows, tok_rows = _exchange_start([in_slot(_pack_slab(g, loss_part))], [True], _ALL_PEERS, "grads_rows_start")
    res = dict(zip(("w_up", "w_down", "w_out", "gm_w_s"), g["updates"]))
    ((own_in, p_in),) = _exchange_wait(sent["in"], tok_rows, "grads_in_wait")
    res["w_in"] = tuple(r.T for r in _adamw_reduce(p_in, own_in, me, w_in_sh, m_in_sh, v_in_sh, shard_in, "adamw_w_in"))
    ((_, p_rows),) = _exchange_wait(sent_rows, res["w_in"][1], "grads_rows_wait")
    flat = lambda t: t[0] if t.ndim == 3 else t
    small_res, loss = _adamw_slab(
        p_rows, me, *({k: flat(d[k]) for k in _SMALL_PARAMS + ("conv_w",)} for d in (w, m, v)))
    res.update(small_res)
    res = {k: tuple(r.reshape(w[k].shape) for r in res[k]) for k in _WEIGHTS}

    outs = [loss, grad_x.reshape(x.shape)]
    for part in range(4):
        outs.extend(res[k][part] for k in _WEIGHTS)
    return tuple(outs)
```

```python
import functools

import jax
import jax.numpy as jnp
import numpy as np
from jax import lax
from jax.experimental import pallas as pl
from jax.experimental.pallas import tpu as pltpu

F32 = jnp.float32
BF16 = jnp.bfloat16

D_MODEL = 1024
GM_WIDTH = 512
SSM_WIDTH = 512
CONV_CH = 1024
N_HEADS = 8
HEAD_DIM = 64
N_STATE = 128
CHUNK = 128
D_FF = 4096
IN_COLS = 2568
IN_PAD = 2688
N_DEV = 8
EPS = 1e-6
ADAM_LR, ADAM_B1, ADAM_B2, ADAM_EPS, ADAM_WD, ADAM_STEP = 0.001, 0.9, 0.999, 1e-08, 0.01, 10
VMEM_LIMIT_BYTES = 56 * 1024 * 1024
TOKEN_TILE = 512
FF_TILE = 2048
WGRAD_TILE = 512

_NT = (((1,), (1,)), ((), ()))
_TN = (((0,), (0,)), ((), ()))


def _params(*sem):
    return pltpu.CompilerParams(dimension_semantics=sem or None, vmem_limit_bytes=VMEM_LIMIT_BYTES)


def _dot(a, b, dims=None):
    if dims is None:
        return jnp.dot(a, b, preferred_element_type=F32)
    return lax.dot_general(a, b, dims, preferred_element_type=F32)


def _split_terms(x, terms):
    out, rem = [], x
    for i in range(terms):
        hi = rem.astype(BF16)
        out.append(hi)
        if i + 1 < terms:
            rem = rem - hi.astype(F32)
    return out


def _split_dot(x, m, terms):
    acc = None
    for hi in _split_terms(x, terms):
        part = _dot(hi, m)
        acc = part if acc is None else acc + part
    return acc


def _split_dot_left(m, x, terms):
    acc = None
    for hi in _split_terms(x, terms):
        part = _dot(m, hi)
        acc = part if acc is None else acc + part
    return acc


def _gelu_and_grad(x):
    c = 0.7978845608028654
    inner = c * (x + 0.044715 * x * x * x)
    t = jnp.tanh(inner)
    g = 0.5 * x * (1.0 + t)
    dg = 0.5 * (1.0 + t) + 0.5 * x * (1.0 - t * t) * c * (1.0 + 3.0 * 0.044715 * x * x)
    return g, dg


def _softplus(x):
    return jnp.maximum(x, 0.0) + jnp.log(1.0 + jnp.exp(-jnp.abs(x)))


def _rsum(x):
    return jnp.sum(x, axis=0, keepdims=True)


def _acc_rows(ref, part, first):
    val = jnp.broadcast_to(part, ref.shape)

    @pl.when(first)
    def _():
        ref[...] = val

    @pl.when(jnp.logical_not(first))
    def _():
        ref[...] += val


def _rms_bwd(n, g, dout):
    r = lax.rsqrt(jnp.mean(n * n, axis=-1, keepdims=True) + EPS)
    nh = n * r
    dg = dout * g
    dn = r * (dg - nh * jnp.mean(dg * nh, axis=-1, keepdims=True))
    return dn, _rsum(dout * nh)


def _const_mats():
    avg = np.kron(np.eye(4), np.full((HEAD_DIM, HEAD_DIM), 1.0 / HEAD_DIM))
    expand = np.zeros((CHUNK, SSM_WIDTH), np.float32)
    for h in range(N_HEADS):
        expand[h, h * HEAD_DIM:(h + 1) * HEAD_DIM] = 1.0
    tril = np.tril(np.ones((CHUNK, CHUNK), np.float32))
    as_bf16 = lambda a: jnp.asarray(a, dtype=BF16)
    return as_bf16(avg), as_bf16(expand), as_bf16(expand.T), as_bf16(tril), as_bf16(tril.T)


def _full(shape):
    nd = len(shape)
    return pl.BlockSpec(shape, lambda *_: (0,) * nd)


_HBM = pl.BlockSpec(memory_space=pltpu.HBM)
_SEM = pl.BlockSpec(memory_space=pltpu.SEMAPHORE)
_ALL_PEERS = tuple((k, 0) for k in range(1, N_DEV))
_SAME_CORE_PEERS = ((2, 0), (4, 0), (6, 0))
_SIBLING_FORWARD = ((1, 0), (1, 2), (1, 4), (1, 6))


def _flip(j, k):
    for bit in (4, 2, 1):
        if k & bit:
            j = j + bit - 2 * (j & bit)
    return j


def _copies(src, land, send_sems, recv_sems, hops):
    x, y, c = lax.axis_index("x"), lax.axis_index("y"), lax.axis_index("c")
    me = 4 * x + 2 * y + c
    out = []
    for t in range(len(src)):
        for i, (k, b) in enumerate(hops):
            pos = (1 - x if k & 4 else x, 1 - y if k & 2 else y, 1 - c if k & 1 else c)
            peer = _flip(me, k)
            sem = t * len(hops) + i
            mk = functools.partial(pltpu.make_async_remote_copy, send_sem=send_sems.at[sem], recv_sem=recv_sems.at[sem],
                                   device_id=pos, device_id_type=pl.DeviceIdType.MESH)
            if land[t] is None and src[t].shape[0] != N_DEV:
                width = src[t].shape[1] // N_DEV
                slab = lambda j: src[t].at[:, pl.ds(pl.multiple_of(j * width, 128), width)]
                mine = functools.partial(mk, src_ref=slab(_flip(me, b)), dst_ref=slab(_flip(me, b)))
                theirs = functools.partial(mk, src_ref=slab(_flip(peer, b)), dst_ref=slab(_flip(peer, b)))
            elif land[t] is None:
                mine = functools.partial(mk, src_ref=src[t].at[_flip(me, b)], dst_ref=src[t].at[_flip(me, b)])
                theirs = functools.partial(mk, src_ref=src[t].at[_flip(peer, b)], dst_ref=src[t].at[_flip(peer, b)])
            else:
                assert b == 0
                mine = functools.partial(mk, src_ref=src[t].at[peer], dst_ref=land[t].at[me])
                theirs = functools.partial(mk, src_ref=src[t].at[peer], dst_ref=land[t].at[peer])
            out.append((mine, theirs))
    return out


def _exchange_start(srcs, inplace, peers, name, dep=None):
    n = len(srcs)
    lands = [None if ip else pltpu.with_memory_space_constraint(lax.empty(s.shape, s.dtype), pltpu.HBM)
             for s, ip in zip(srcs, inplace)]
    real_lands = [l for l in lands if l is not None]
    n_l = len(real_lands)
    deps = [] if dep is None else [dep]

    def body(*refs):
        src = refs[:n]
        land_refs = list(refs[n:n + n_l])
        send_sems, recv_sems = refs[n + n_l + len(deps)], refs[n + n_l + len(deps) + 1]
        token = refs[-1]
        land = [None if ip else land_refs.pop(0) for ip in inplace]
        for mine, _ in _copies(src, land, send_sems, recv_sems, peers):
            mine().start()
        token[...] = jnp.zeros_like(token)

    sem_t = pltpu.SemaphoreType.DMA((n * len(peers),))
    outs = pl.pallas_call(
        body, name=name,
        out_shape=(sem_t, sem_t) + tuple(pltpu.HBM(a.shape, a.dtype) for a in list(srcs) + real_lands)
        + (jax.ShapeDtypeStruct((8, 128), F32),),
        in_specs=[_HBM] * (n + n_l) + [pl.BlockSpec(memory_space=pl.ANY)] * len(deps),
        out_specs=(_SEM, _SEM) + (_HBM,) * (n + n_l) + (pl.BlockSpec(memory_space=pltpu.VMEM),),
        input_output_aliases={i: 2 + i for i in range(n + n_l)},
        compiler_params=pltpu.CompilerParams(has_side_effects=pltpu.SideEffectType.DATAFLOW_SIDE_EFFECTING),
    )(*[pltpu.with_memory_space_constraint(s, pltpu.HBM) for s in srcs], *real_lands, *deps)
    handle = dict(send=outs[0], recv=outs[1], srcs=outs[2:2 + n], lands=outs[2 + n:2 + n + n_l], inplace=inplace,
                  peers=peers)
    return handle, outs[-1]


def _exchange_wait(handle, after, name):
    srcs, lands, inplace, peers = handle["srcs"], handle["lands"], handle["inplace"], handle["peers"]
    n, n_l = len(srcs), len(lands)

    def body(*refs):
        src = refs[:n]
        land_refs = list(refs[n:n + n_l])
        send_sems, recv_sems = refs[n + n_l], refs[n + n_l + 1]
        land = [None if ip else land_refs.pop(0) for ip in inplace]
        for mine, theirs in _copies(src, land, send_sems, recv_sems, peers):
            mine().wait_send()
            theirs().wait_recv()

    outs = pl.pallas_call(
        body, name=name, out_shape=tuple(pltpu.HBM(a.shape, a.dtype) for a in list(srcs) + list(lands)),
        in_specs=[_HBM] * (n + n_l) + [_SEM, _SEM, pl.BlockSpec(memory_space=pl.ANY)],
        out_specs=(_HBM,) * (n + n_l), input_output_aliases={i: i for i in range(n + n_l)},
        compiler_params=pltpu.CompilerParams(has_side_effects=pltpu.SideEffectType.DATAFLOW_SIDE_EFFECTING),
    )(*srcs, *lands, handle["send"], handle["recv"], after)
    res, land_out = [], list(outs[n:])
    for t in range(n):
        res.append((outs[t], outs[t] if inplace[t] else land_out.pop(0)))
    return res


def _cast_to_slot(w, me, rows, name, cols=False, dep=None):
    r, cdim = w.shape
    deps = [] if dep is None else [dep]

    def body(me_ref, w_ref, *rest):
        o_ref = rest[-1]
        if cols:
            o_ref[...] = w_ref[...].astype(BF16)
        else:
            o_ref[0] = w_ref[...].astype(BF16)

    if cols:
        out_shape = jax.ShapeDtypeStruct((r, N_DEV * cdim), BF16)
        out_spec = pl.BlockSpec((rows, cdim), lambda i, me_ref: (i, me_ref[0]))
    else:
        out_shape = jax.ShapeDtypeStruct((N_DEV, r, cdim), BF16)
        out_spec = pl.BlockSpec((1, rows, cdim), lambda i, me_ref: (me_ref[0], i, 0))
    return pl.pallas_call(
        body, name=name, out_shape=out_shape,
        grid_spec=pltpu.PrefetchScalarGridSpec(
            num_scalar_prefetch=1, grid=(r // rows,),
            in_specs=[pl.BlockSpec((rows, cdim), lambda i, me_ref: (i, 0))]
            + [pl.BlockSpec(memory_space=pl.ANY)] * len(deps), out_specs=out_spec),
        compiler_params=_params("parallel"))(me, w, *deps)


def _adamw_math(w, g, m, v):
    m = ADAM_B1 * m + (1.0 - ADAM_B1) * g
    v = ADAM_B2 * v + (1.0 - ADAM_B2) * (g * g)
    m_hat = m / (1.0 - ADAM_B1 ** ADAM_STEP)
    v_hat = v / (1.0 - ADAM_B2 ** ADAM_STEP)
    delta = -ADAM_LR * (m_hat / (jnp.sqrt(v_hat) + ADAM_EPS) + ADAM_WD * w)
    return delta, m, v


def _sum_parts(me, p_ref, own):
    g = None
    for j in range(N_DEV):
        term = (p_ref[j] if own is None else jnp.where(me == j, own, p_ref[j])).astype(F32)
        g = term if g is None else g + term
    return g


def _adamw_reduce(parts, own, me, w, m, v, rows, name):
    r, cdim = w.shape

    def body(me_ref, p_ref, own_ref, w_ref, m_ref, v_ref, g_out, d_out, m_out, v_out):
        g = _sum_parts(me_ref[0], p_ref, own_ref[0])
        d, mn, vn = _adamw_math(w_ref[...], g, m_ref[...], v_ref[...])
        g_out[...] = g
        d_out[...] = d
        m_out[...] = mn
        v_out[...] = vn

    blk = pl.BlockSpec((rows, cdim), lambda i, me_ref: (i, 0))
    sds = jax.ShapeDtypeStruct(w.shape, F32)
    return pl.pallas_call(
        body, name=name, out_shape=(sds,) * 4,
        grid_spec=pltpu.PrefetchScalarGridSpec(
            num_scalar_prefetch=1, grid=(r // rows,),
            in_specs=[pl.BlockSpec((N_DEV, rows, cdim), lambda i, me_ref: (0, i, 0)),
                      pl.BlockSpec((1, rows, cdim), lambda i, me_ref: (me_ref[0], i, 0)), blk, blk, blk],
            out_specs=(blk,) * 4),
        compiler_params=_params("parallel"))(me, parts, own, w, m, v)


_IN_SPLITS = ((0, 512), (512, 1024), (1024, 1536), (1536, 2560), (2560, IN_PAD))


def _prenorm(x, g1, tm, dep=None):
    t_tok = x.shape[0]
    deps = [] if dep is None else [dep]

    def body(x_ref, g_ref, *rest):
        xv = x_ref[...]
        r = lax.rsqrt(jnp.mean(xv * xv, axis=-1, keepdims=True) + EPS)
        rest[-1][...] = (xv * r * g_ref[...]).astype(BF16)

    row = pl.BlockSpec((tm, D_MODEL), lambda i: (i, 0))
    return pl.pallas_call(
        body, name="prenorm", grid=(t_tok // tm,), out_shape=jax.ShapeDtypeStruct((t_tok, D_MODEL), BF16),
        in_specs=[row, _full((1, D_MODEL))] + [pl.BlockSpec(memory_space=pl.ANY)] * len(deps), out_specs=row,
        compiler_params=_params("parallel"))(x, g1, *deps)


def _in_proj(h1, w_in, tm):
    t_tok = h1.shape[0]

    def body(h_ref, w_ref, *outs):
        h = h_ref[...]
        for (a, b), o_ref in zip(_IN_SPLITS, outs):
            o_ref[...] = _dot(h, w_ref[a:b, :], _NT).astype(o_ref.dtype)

    row = lambda n: pl.BlockSpec((tm, n), lambda i: (i, 0))
    widths = [b - a for a, b in _IN_SPLITS]
    dtypes = (BF16, BF16, BF16, F32, F32)
    return pl.pallas_call(
        body, name="in_proj", grid=(t_tok // tm,),
        out_shape=tuple(jax.ShapeDtypeStruct((t_tok, n), dt) for n, dt in zip(widths, dtypes)),
        in_specs=[row(D_MODEL), _full((IN_PAD, D_MODEL))], out_specs=tuple(row(n) for n in widths),
        compiler_params=_params("parallel"))(h1, w_in)


def _lane_masks():
    lane = lax.broadcasted_iota(jnp.int32, (1, 2 * HEAD_DIM), 1)
    left = (lane < HEAD_DIM).astype(F32)
    return left, 1.0 - left


def _stack_pair(v, m_l, m_r):
    return jnp.concatenate([v * m_l, v * m_r], axis=0).astype(BF16)


def _head_mean(x, avg):
    n = avg.shape[0]
    return jnp.concatenate([_split_dot(x[:, n * i:n * (i + 1)], avg, 2) for i in range(x.shape[1] // n)], axis=1)


def _gmlp_common(u, v, lnw, lnb, avg, wcat_ref, bias, m_l, m_r):
    ug, dug = _gelu_and_grad(u)
    vg, dvg = _gelu_and_grad(v)
    mu = _head_mean(vg, avg)
    vc = vg - mu
    var = _head_mean(vc * vc, avg)
    rstd = lax.rsqrt(var + EPS)
    vhat = vc * rstd
    vn = vhat * lnw + lnb
    rows = []
    for r in range(u.shape[0] // CHUNK):
        cols = []
        for j in range(N_HEADS // 2):
            pair = vn[CHUNK * r:CHUNK * (r + 1), 128 * j:128 * (j + 1)]
            cols.append(_dot(wcat_ref[j], _stack_pair(pair, m_l, m_r)))
        rows.append(jnp.concatenate(cols, axis=1) + bias)
    mixed = jnp.concatenate(rows, axis=0)
    return ug, dug, dvg, rstd, vhat, vn, mixed


_GMLP_ROWS = 4 * CHUNK


def _gmlp_fwd(u, v, lnw, lnb, wcat, bias, avg):
    t_tok = u.shape[0]
    tm = min(_GMLP_ROWS, t_tok)

    def body(u_ref, v_ref, lnw_ref, lnb_ref, wcat_ref, bias_ref, avg_ref, o_ref):
        m_l, m_r = _lane_masks()
        ug, _, _, _, _, _, mixed = _gmlp_common(
            u_ref[...].astype(F32), v_ref[...].astype(F32), lnw_ref[...], lnb_ref[...], avg_ref[...], wcat_ref,
            bias_ref[...], m_l, m_r)
        o_ref[...] = (ug * mixed).astype(BF16)

    row = pl.BlockSpec((tm, GM_WIDTH), lambda i: (i, 0))
    return pl.pallas_call(
        body, name="gmlp_fwd", grid=(t_tok // tm,), out_shape=jax.ShapeDtypeStruct((t_tok, GM_WIDTH), BF16),
        in_specs=[row, row, _full((1, GM_WIDTH)), _full((1, GM_WIDTH)), _full(wcat.shape), _full(bias.shape),
                  _full(avg.shape)],
        out_specs=row, compiler_params=_params("parallel"))(u, v, lnw, lnb, wcat, bias, avg)


def _shift_rows(x, edge, j, down):
    groups, cols = x.shape[0] // 8, x.shape[1]
    amount = j if down else 8 - j
    rot = pltpu.roll(x.reshape(groups, 8, cols), amount, axis=1)
    edge_rot = pltpu.roll(edge, amount, axis=0)[None]
    sub = lax.broadcasted_iota(jnp.int32, (1, 8, 1), 1)
    if down:
        out = jnp.where(sub < j, jnp.concatenate([edge_rot, rot[:-1]], axis=0), rot)
    else:
        out = jnp.where(sub < 8 - j, rot, jnp.concatenate([rot[1:], edge_rot], axis=0))
    return out.reshape(x.shape)


def _conv_pre(xbc, tail, cw_ref, cb):
    taps = [_shift_rows(xbc, tail, 3 - k, True) for k in range(3)] + [xbc]
    return cb + cw_ref[0:1, :] * taps[0] + cw_ref[1:2, :] * taps[1] + cw_ref[2:3, :] * taps[2] + cw_ref[3:4, :] * taps[3]


def _ssd_common(pre, dtr, dtb, alog, expand, tril):
    q = CHUNK
    sg = jax.nn.sigmoid(pre)
    act = pre * sg
    lane = lax.broadcasted_iota(jnp.int32, (1, CHUNK), 1)
    a_row = jnp.where(lane < N_HEADS, -jnp.exp(alog), 0.0)
    dtp = dtr + dtb
    dt = _softplus(dtp)
    a_cs = _split_dot_left(tril, dt * a_row, 3)
    a_cs_t = a_cs.T
    dt_exp = _split_dot(dt, expand, 3)
    a_exp = _split_dot(a_cs, expand, 3)
    a_end = a_exp[q - 1:q, :]
    li = lax.broadcasted_iota(jnp.int32, (q, q), 0)
    si = lax.broadcasted_iota(jnp.int32, (q, q), 1)
    causal = si <= li
    decay = []
    for h in range(N_HEADS):
        seg = a_cs[:, h:h + 1] - a_cs_t[h:h + 1, :]
        decay.append(jnp.where(causal, jnp.exp(jnp.minimum(seg, 0.0)), 0.0))
    return dict(pre=pre, sg=sg, act=act, a_row=a_row, dtp=dtp, dt=dt, dt_exp=dt_exp, a_exp=a_exp,
                e=jnp.exp(a_exp), w_end=jnp.exp(a_end - a_exp), cd=jnp.exp(a_end), decay=decay)


def _ssd_specs(t_tok, seq, reverse):
    nb, nc = t_tok // seq, seq // CHUNK

    def chunk(c):
        return nc - 1 - c if reverse else c

    def row(n, col=0):
        return pl.BlockSpec((nb, CHUNK, n), lambda c: (0, chunk(c), col))

    tail = pl.BlockSpec((nb, 8, CONV_CH), lambda c: (0, jnp.maximum(chunk(c) * (CHUNK // 8) - 1, 0), 0))
    states = pl.BlockSpec((nb, 1, N_STATE, SSM_WIDTH), lambda c: (0, chunk(c), 0, 0))
    fold = lambda a: a.reshape(nb, seq, a.shape[-1])
    unfold = lambda a: a.reshape(t_tok, a.shape[-1])
    return nb, nc, row, tail, states, fold, unfold


def _ssd_fwd(z, xbc, dtr, cw, cb, dtb, alog, dskip_exp, nw, expand, tril, seq):
    t_tok = z.shape[0]
    nb, nc, row, tail, states_spec, fold, unfold = _ssd_specs(t_tok, seq, False)

    def body(z_ref, xbc_ref, tail_ref, dtr_ref, cw_ref, cb_ref, dtb_ref, alog_ref, dsk_ref, nw_ref, exp_ref,
             tril_ref, o_ref, y_ref, st_ref, pre_ref, state_ref):
        c = pl.program_id(0)

        @pl.when(c == 0)
        def _():
            state_ref[...] = jnp.zeros_like(state_ref)

        m_l, m_r = _lane_masks()
        for s in range(nb):
            pre = _conv_pre(xbc_ref[s], jnp.where(c == 0, 0.0, tail_ref[s]), cw_ref, cb_ref[...])
            pre_ref[s] = pre
            f = _ssd_common(pre, dtr_ref[s], dtb_ref[...], alog_ref[...], exp_ref[...], tril_ref[...])
            act = f["act"]
            xs = act[:, :SSM_WIDTH]
            xdt = xs * f["dt_exp"]
            xw = xdt * f["w_end"]
            state = state_ref[s]
            st_ref[s, 0] = state
            ydiag, yoff, snew = [], [], []
            for g in range(2):
                bg = act[:, 512 + 128 * g:640 + 128 * g].astype(BF16)
                cg = act[:, 768 + 128 * g:896 + 128 * g].astype(BF16)
                cb_mat = _dot(cg, bg, _NT)
                for pr in range(2):
                    h0 = 4 * g + 2 * pr
                    gcat = jnp.concatenate(
                        [(cb_mat * f["decay"][h0]).astype(BF16), (cb_mat * f["decay"][h0 + 1]).astype(BF16)], axis=1)
                    ydiag.append(_dot(gcat, _stack_pair(xdt[:, 64 * h0:64 * h0 + 128], m_l, m_r)))
                yoff.append(_dot(cg, state[:, 256 * g:256 * (g + 1)].astype(BF16)))
                snew.append(_dot(bg, xw[:, 256 * g:256 * (g + 1)].astype(BF16), _TN))
            y = jnp.concatenate(ydiag, axis=1) + f["e"] * jnp.concatenate(yoff, axis=1) + dsk_ref[...] * xs
            state_ref[s] = state * f["cd"] + jnp.concatenate(snew, axis=1)
            y_ref[s] = y
            zv = z_ref[s].astype(F32)
            yg = y * (zv * jax.nn.sigmoid(zv))
            outs = []
            for g in range(2):
                ygg = yg[:, 256 * g:256 * (g + 1)]
                outs.append(ygg * lax.rsqrt(jnp.mean(ygg * ygg, axis=-1, keepdims=True) + EPS))
            o_ref[s] = (jnp.concatenate(outs, axis=1) * nw_ref[...]).astype(BF16)

    consts = [cw, cb, dtb, alog, dskip_exp, nw, expand, tril]
    sd = lambda n, dt: jax.ShapeDtypeStruct((nb, seq, n), dt)
    o, y, states, pre = pl.pallas_call(
        body, name="ssd_fwd", grid=(nc,),
        out_shape=(sd(SSM_WIDTH, BF16), sd(SSM_WIDTH, F32), jax.ShapeDtypeStruct((nb, nc, N_STATE, SSM_WIDTH), F32),
                   sd(CONV_CH, F32)),
        in_specs=[row(SSM_WIDTH), row(CONV_CH), tail, row(CHUNK)] + [_full(a.shape) for a in consts],
        out_specs=(row(SSM_WIDTH), row(SSM_WIDTH), states_spec, row(CONV_CH)),
        scratch_shapes=[pltpu.VMEM((nb, N_STATE, SSM_WIDTH), F32)],
        compiler_params=_params("arbitrary"))(fold(z), fold(xbc), fold(xbc), fold(dtr), *consts)
    return unfold(o), unfold(y), states, unfold(pre)


def _out_proj(mix_a, mix_b, w_out, x, g2, g3, tm, dep=None):
    t_tok = x.shape[0]
    deps = [] if dep is None else [dep]

    def body(a_ref, b_ref, w_ref, x_ref, g2_ref, g3_ref, *rest):
        o_ref, x2_ref, h3_ref = rest[-3:]
        o = _dot(a_ref[...], w_ref[0:GM_WIDTH, :]) + _dot(b_ref[...], w_ref[GM_WIDTH:, :])
        o_ref[...] = o
        r2 = lax.rsqrt(jnp.mean(o * o, axis=-1, keepdims=True) + EPS)
        x2 = x_ref[...] + o * r2 * g2_ref[...]
        x2_ref[...] = x2
        r3 = lax.rsqrt(jnp.mean(x2 * x2, axis=-1, keepdims=True) + EPS)
        h3_ref[...] = (x2 * r3 * g3_ref[...]).astype(BF16)

    row = lambda n: pl.BlockSpec((tm, n), lambda i: (i, 0))
    sd = lambda dt: jax.ShapeDtypeStruct((t_tok, D_MODEL), dt)
    return pl.pallas_call(
        body, name="out_proj", grid=(t_tok // tm,), out_shape=(sd(F32), sd(F32), sd(BF16)),
        in_specs=[row(GM_WIDTH), row(SSM_WIDTH), _full((D_MODEL, D_MODEL)), row(D_MODEL), _full((1, D_MODEL)),
                  _full((1, D_MODEL))] + [pl.BlockSpec(memory_space=pl.ANY)] * len(deps),
        out_specs=(row(D_MODEL),) * 3, compiler_params=_params("parallel"))(mix_a, mix_b, w_out, x, g2, g3, *deps)


def _mlp_fwd(h3, w_up, w_down, x2, target, g4, tm, tf):
    t_tok = x2.shape[0]

    def up_body(h_ref, wu_ref, ra_ref):
        ra_ref[...] = jnp.maximum(_dot(h_ref[...], wu_ref[...]), 0.0).astype(BF16)

    tu = min(2 * tm, t_tok)
    ra = pl.pallas_call(
        up_body, name="mlp_up", grid=(D_FF // tf, t_tok // tu), out_shape=jax.ShapeDtypeStruct((t_tok, D_FF), BF16),
        in_specs=[pl.BlockSpec((tu, D_MODEL), lambda j, i: (i, 0)), pl.BlockSpec((D_MODEL, tf), lambda j, i: (0, j))],
        out_specs=pl.BlockSpec((tu, tf), lambda j, i: (i, j)), compiler_params=_params("parallel", "parallel"))(h3, w_up)

    def down_body(ra_ref, wd_ref, x2_ref, t_ref, g4_ref, dd_ref, dy_ref, dg4_ref, loss_ref):
        i = pl.program_id(0)
        rav = ra_ref[...]
        dvec = _dot(rav * rav, wd_ref[...])
        r4 = lax.rsqrt(jnp.mean(dvec * dvec, axis=-1, keepdims=True) + EPS)
        dn = dvec * r4
        g4 = g4_ref[...]
        err = x2_ref[...] + dn * g4 - t_ref[...]
        dy = err * (1.0 / D_MODEL)
        dy_ref[...] = dy
        dg = dy * g4
        dd_ref[...] = (r4 * (dg - dn * jnp.mean(dg * dn, axis=-1, keepdims=True))).astype(BF16)
        _acc_rows(dg4_ref, _rsum(dy * dn), i == 0)
        tile_loss = 0.5 * jnp.sum(jnp.sum(err * err, axis=-1, keepdims=True), axis=0, keepdims=True) / D_MODEL
        _acc_rows(loss_ref, jnp.broadcast_to(tile_loss, (1, 128)), i == 0)

    row = pl.BlockSpec((tm, D_MODEL), lambda i: (i, 0))
    dd, dy, dg4, loss = pl.pallas_call(
        down_body, name="mlp_down", grid=(t_tok // tm,),
        out_shape=(jax.ShapeDtypeStruct((t_tok, D_MODEL), BF16), jax.ShapeDtypeStruct((t_tok, D_MODEL), F32),
                   jax.ShapeDtypeStruct((1, D_MODEL), F32), jax.ShapeDtypeStruct((1, 128), F32)),
        in_specs=[pl.BlockSpec((tm, D_FF), lambda i: (i, 0)), _full((D_FF, D_MODEL)), row, row, _full((1, D_MODEL))],
        out_specs=(row, row, _full((1, D_MODEL)), _full((1, 128))),
        compiler_params=_params("arbitrary"))(ra, w_down, x2, target, g4)
    return ra, dd, dy, dg4, loss


def _mlp_bwd(dd, w_down, ra, w_up, x2, dy, o, g3, g2, tm, tf):
    t_tok = x2.shape[0]

    def hidden_body(dd_ref, wd_ref, ra_ref, da_ref):
        df = _dot(dd_ref[...], wd_ref[...], _NT)
        da_ref[...] = (df * (2.0 * ra_ref[...].astype(F32))).astype(BF16)

    tu = min(2 * tm, t_tok)
    da = pl.pallas_call(
        hidden_body, name="mlp_bwd_hidden", grid=(D_FF // tf, t_tok // tu),
        out_shape=jax.ShapeDtypeStruct((t_tok, D_FF), BF16),
        in_specs=[pl.BlockSpec((tu, D_MODEL), lambda j, i: (i, 0)), pl.BlockSpec((tf, D_MODEL), lambda j, i: (j, 0)),
                  pl.BlockSpec((tu, tf), lambda j, i: (i, j))],
        out_specs=pl.BlockSpec((tu, tf), lambda j, i: (i, j)),
        compiler_params=_params("parallel", "parallel"))(dd, w_down, ra)

    def in_body(da_ref, wu_ref, x2_ref, dy_ref, o_ref, g3_ref, g2_ref, dx2_ref, do_ref, dg3_ref, dg2_ref):
        i = pl.program_id(0)
        dh3 = _dot(da_ref[...], wu_ref[...], _NT)
        dn3, dg3 = _rms_bwd(x2_ref[...], g3_ref[...], dh3)
        dx2 = dy_ref[...] + dn3
        dx2_ref[...] = dx2
        do, dg2 = _rms_bwd(o_ref[...], g2_ref[...], dx2)
        do_ref[...] = do.astype(BF16)
        _acc_rows(dg3_ref, dg3, i == 0)
        _acc_rows(dg2_ref, dg2, i == 0)

    row = pl.BlockSpec((tm, D_MODEL), lambda i: (i, 0))
    vec = _full((1, D_MODEL))
    sd = lambda dt: jax.ShapeDtypeStruct((t_tok, D_MODEL), dt)
    dx2, do, dg3, dg2 = pl.pallas_call(
        in_body, name="mlp_bwd_in", grid=(t_tok // tm,),
        out_shape=(sd(F32), sd(BF16), jax.ShapeDtypeStruct((1, D_MODEL), F32), jax.ShapeDtypeStruct((1, D_MODEL), F32)),
        in_specs=[pl.BlockSpec((tm, D_FF), lambda i: (i, 0)), _full((D_MODEL, D_FF)), row, row, row, vec, vec],
        out_specs=(row, row, vec, vec), compiler_params=_params("arbitrary"))(da, w_up, x2, dy, o, g3, g2)
    return da, dx2, do, dg3, dg2


def _wgrad(a, b, out_blocks, bm, bn, bk, square_a, name, dep=None):
    t_tok, m = a.shape
    n = b.shape[1]
    nk = t_tok // bk

    def body(a_ref, b_ref, *rest):
        o_ref, acc_ref = rest[-2:]
        k = pl.program_id(2)
        av = a_ref[...]
        if square_a:
            av = av * av
        part = _dot(av, b_ref[...], _TN)

        def emit(res):
            if out_blocks is None:
                o_ref[...] = res.astype(BF16)
            else:
                o_ref[0] = res.astype(BF16)

        if nk == 1:
            emit(part)
            return

        @pl.when(k == 0)
        def _():
            acc_ref[...] = part

        @pl.when(k > 0)
        def _():
            acc_ref[...] += part

        @pl.when(k == nk - 1)
        def _():
            emit(acc_ref[...])

    if out_blocks is None:
        out_shape = jax.ShapeDtypeStruct((m, n), BF16)
        out_spec = pl.BlockSpec((bm, bn), lambda i, j, k: (i, j))
    else:
        assert n // out_blocks == bn
        out_shape = jax.ShapeDtypeStruct((out_blocks, m, bn), BF16)
        out_spec = pl.BlockSpec((1, bm, bn), lambda i, j, k: (j, i, 0))
    deps = [] if dep is None else [dep]
    return pl.pallas_call(
        body, name=name, grid=(m // bm, n // bn, nk), out_shape=out_shape,
        in_specs=[pl.BlockSpec((bk, bm), lambda i, j, k: (k, i)), pl.BlockSpec((bk, bn), lambda i, j, k: (k, j))]
        + [pl.BlockSpec(memory_space=pl.ANY)] * len(deps),
        out_specs=out_spec, scratch_shapes=[pltpu.VMEM((bm, bn) if nk > 1 else (8, 128), F32)],
        compiler_params=_params("parallel", "parallel", "arbitrary"))(a, b, *deps)


def _wgrad_in_chunked(h1, pieces, bn, bk, dep=None):
    t_tok = h1.shape[0]
    nk = t_tok // bk
    widths = [b - a for a, b in _IN_SPLITS]

    def body(h_ref, *rest):
        piece_refs = rest[:len(widths)]
        o_ref, acc_ref = rest[-2:]
        k = pl.program_id(1)
        hv = h_ref[...]
        for (a, b), r in zip(_IN_SPLITS, piece_refs):
            part = _dot(r[...], hv, _TN)

            @pl.when(k == 0)
            def _():
                acc_ref[a:b, :] = part

            @pl.when(k > 0)
            def _():
                acc_ref[a:b, :] += part

        @pl.when(k == nk - 1)
        def _():
            o_ref[...] = acc_ref[...].astype(BF16)

    deps = [] if dep is None else [dep]
    return pl.pallas_call(
        body, name="wgrad_in", grid=(D_MODEL // bn, nk), out_shape=jax.ShapeDtypeStruct((IN_PAD, D_MODEL), BF16),
        in_specs=[pl.BlockSpec((bk, bn), lambda j, k: (k, j))] + [pl.BlockSpec((bk, n), lambda j, k: (k, 0)) for n in widths]
        + [pl.BlockSpec(memory_space=pl.ANY)] * len(deps),
        out_specs=pl.BlockSpec((IN_PAD, bn), lambda j, k: (0, j)), scratch_shapes=[pltpu.VMEM((IN_PAD, bn), F32)],
        compiler_params=_params("parallel", "arbitrary"))(h1, *pieces, *deps)


def _wgrad_pieces(h1, pieces, bn, name, dep=None):
    t_tok = h1.shape[0]
    widths = [p.shape[1] for p in pieces]
    starts = [sum(widths[:i]) for i in range(len(widths))]

    def body(h_ref, *rest):
        piece_refs = rest[:len(widths)]
        o_ref = rest[-1]
        hv = h_ref[...]
        for a, n, r in zip(starts, widths, piece_refs):
            o_ref[a:a + n, :] = _dot(r[...], hv, _TN).astype(BF16)

    deps = [] if dep is None else [dep]
    return pl.pallas_call(
        body, name=name, grid=(D_MODEL // bn,), out_shape=jax.ShapeDtypeStruct((sum(widths), D_MODEL), BF16),
        in_specs=[pl.BlockSpec((t_tok, bn), lambda j: (0, j))] + [pl.BlockSpec((t_tok, n), lambda j: (0, 0)) for n in widths]
        + [pl.BlockSpec(memory_space=pl.ANY)] * len(deps),
        out_specs=pl.BlockSpec((sum(widths), bn), lambda j: (0, j)),
        compiler_params=_params("parallel"))(h1, *pieces, *deps)


def _dmix(do, w_out, tm, dep=None):
    t_tok = do.shape[0]

    def body(d_ref, w_ref, *rest):
        rest[-1][...] = _dot(d_ref[...], w_ref[...], _NT).astype(BF16)

    row = pl.BlockSpec((tm, D_MODEL), lambda i: (i, 0))
    deps = [] if dep is None else [dep]
    return pl.pallas_call(
        body, name="dmix", grid=(t_tok // tm,), out_shape=jax.ShapeDtypeStruct((t_tok, D_MODEL), BF16),
        in_specs=[row, _full((D_MODEL, D_MODEL))] + [pl.BlockSpec(memory_space=pl.ANY)] * len(deps), out_specs=row,
        compiler_params=_params("parallel"))(do, w_out, *deps)


def _gmlp_bwd(dmix, u, v, lnw, lnb, wcat, wtcat, bias, avg, expand_t):
    t_tok = u.shape[0]
    tm = min(_GMLP_ROWS, t_tok)

    def body(dm_ref, u_ref, v_ref, lnw_ref, lnb_ref, wcat_ref, wtcat_ref, bias_ref, avg_ref, expt_ref, du_ref, dv_ref,
             dw_ref, db_ref, dlnw_ref, dlnb_ref):
        i = pl.program_id(0)
        m_l, m_r = _lane_masks()
        avg = avg_ref[...]
        lnw = lnw_ref[...]
        ug, dug, dvg, rstd, vhat, vn, mixed = _gmlp_common(
            u_ref[...].astype(F32), v_ref[...].astype(F32), lnw, lnb_ref[...], avg, wcat_ref, bias_ref[...], m_l, m_r)
        dya = dm_ref[...].astype(F32)
        du_ref[...] = (dya * mixed * dug).astype(BF16)
        dmixed = dya * ug
        dvn_rows, dws, dbt = [], [None] * N_HEADS, None
        for r in range(tm // CHUNK):
            dvn_cols = []
            for j in range(N_HEADS // 2):
                dmp = dmixed[CHUNK * r:CHUNK * (r + 1), 128 * j:128 * (j + 1)]
                dvn_cols.append(_dot(wtcat_ref[j], _stack_pair(dmp, m_l, m_r)))
                vnp = vn[CHUNK * r:CHUNK * (r + 1), 128 * j:128 * (j + 1)].astype(BF16)
                for i_h, mask in enumerate((m_l, m_r)):
                    part = _dot((dmp * mask).astype(BF16), vnp, _NT)
                    dws[2 * j + i_h] = part if r == 0 else dws[2 * j + i_h] + part
            dvn_rows.append(jnp.concatenate(dvn_cols, axis=1))
            part = _split_dot(dmixed[CHUNK * r:CHUNK * (r + 1), :], expt_ref[...], 2)
            dbt = part if r == 0 else dbt + part
        dvn = jnp.concatenate(dvn_rows, axis=0)
        dvh = dvn * lnw
        dvgel = rstd * (dvh - _head_mean(dvh, avg) - vhat * _head_mean(dvh * vhat, avg))
        dv_ref[...] = (dvgel * dvg).astype(BF16)
        first = i == 0

        @pl.when(first)
        def _():
            for h in range(N_HEADS):
                dw_ref[h] = dws[h]
            db_ref[...] = dbt

        @pl.when(jnp.logical_not(first))
        def _():
            for h in range(N_HEADS):
                dw_ref[h] += dws[h]
            db_ref[...] += dbt

        _acc_rows(dlnw_ref, _rsum(dvn * vhat), first)
        _acc_rows(dlnb_ref, _rsum(dvn), first)

    row = pl.BlockSpec((tm, GM_WIDTH), lambda i: (i, 0))
    consts = [lnw, lnb, wcat, wtcat, bias, avg, expand_t]
    return pl.pallas_call(
        body, name="gmlp_bwd", grid=(t_tok // tm,),
        out_shape=(jax.ShapeDtypeStruct((t_tok, GM_WIDTH), BF16), jax.ShapeDtypeStruct((t_tok, GM_WIDTH), BF16),
                   jax.ShapeDtypeStruct((N_HEADS, CHUNK, CHUNK), F32), jax.ShapeDtypeStruct((CHUNK, CHUNK), F32),
                   jax.ShapeDtypeStruct((1, GM_WIDTH), F32), jax.ShapeDtypeStruct((1, GM_WIDTH), F32)),
        in_specs=[row, row, row] + [_full(a.shape) for a in consts],
        out_specs=(row, row, _full((N_HEADS, CHUNK, CHUNK)), _full((CHUNK, CHUNK)), _full((1, GM_WIDTH)),
                   _full((1, GM_WIDTH))),
        compiler_params=_params("arbitrary"))(dmix, u, v, *consts)


def _ssd_bwd(dmix, z, xbc, pre, dtr, y, states, cw, cb, dtb, alog, dskip_exp, nw, expand, expand_t, tril, triu, seq,
             dep=None):
    t_tok = z.shape[0]
    nb, nc, row, _, states_spec, fold, unfold = _ssd_specs(t_tok, seq, True)
    q = CHUNK

    def one_sequence(s, dm_ref, z_ref, xbc_ref, pre_ref, dtr_ref, y_ref, st_ref, cw_ref, dtb_ref, alog_ref, dsk_ref,
                     nw_ref, exp_ref, expt_ref, tril_ref, triu_ref, dz_ref, dxbc_ref, ddt_ref, dhead_ref, dstate_ref):
        m_l, m_r = _lane_masks()
        expt = expt_ref[...]
        f = _ssd_common(pre_ref[s], dtr_ref[s], dtb_ref[...], alog_ref[...], exp_ref[...], tril_ref[...])
        act, pre, sg = f["act"], f["pre"], f["sg"]
        xs = act[:, :SSM_WIDTH]
        xdt = xs * f["dt_exp"]
        xw = xdt * f["w_end"]
        state = st_ref[s, 0]
        dstate = dstate_ref[s]
        zv, yv, dout, nw = z_ref[s].astype(F32), y_ref[s], dm_ref[s].astype(F32), nw_ref[...]
        sz = jax.nn.sigmoid(zv)
        sl = zv * sz
        yg = yv * sl
        tv = dout * nw
        dyg_parts, ygh_parts = [], []
        for g in range(2):
            ygg = yg[:, 256 * g:256 * (g + 1)]
            rr = lax.rsqrt(jnp.mean(ygg * ygg, axis=-1, keepdims=True) + EPS)
            ygh = ygg * rr
            tg = tv[:, 256 * g:256 * (g + 1)]
            dyg_parts.append(rr * (tg - ygh * jnp.mean(tg * ygh, axis=-1, keepdims=True)))
            ygh_parts.append(ygh)
        dyg = jnp.concatenate(dyg_parts, axis=1)
        dnw = _rsum(dout * jnp.concatenate(ygh_parts, axis=1))
        dy = dyg * sl
        dz_ref[s] = (dyg * yv * (sz * (1.0 + zv * (1.0 - sz)))).astype(BF16)
        ddsk = _rsum(dy * xs)
        dye = dy * f["e"]
        lane = lax.broadcasted_iota(jnp.int32, (q, q), 1)
        sub = lax.broadcasted_iota(jnp.int32, (q, q), 0)
        rs_mat = jnp.zeros((q, q), F32)
        cs_mat = jnp.zeros((q, q), F32)
        dxdt_cols, yoff, dst_in, dxw, d_b, d_c = [], [], [], [], [], []
        for g in range(2):
            bg = act[:, 512 + 128 * g:640 + 128 * g].astype(BF16)
            cg = act[:, 768 + 128 * g:896 + 128 * g].astype(BF16)
            cb_mat = _dot(cg, bg, _NT)
            stg = state[:, 256 * g:256 * (g + 1)].astype(BF16)
            dyeg = dye[:, 256 * g:256 * (g + 1)].astype(BF16)
            yoff.append(_dot(cg, stg))
            dcg = _dot(dyeg, stg, _NT)
            dst_in.append(_dot(cg, dyeg, _TN))
            dcb = jnp.zeros((q, q), F32)
            for pr in range(2):
                h0 = 4 * g + 2 * pr
                gf = [cb_mat * f["decay"][h0], cb_mat * f["decay"][h0 + 1]]
                gcat = jnp.concatenate([gf[0].astype(BF16), gf[1].astype(BF16)], axis=1)
                xst = _stack_pair(xdt[:, 64 * h0:64 * h0 + 128], m_l, m_r)
                dyp = dy[:, 64 * h0:64 * h0 + 128].astype(BF16)
                dgcat = _dot(dyp, xst, _NT)
                dxst = _dot(gcat, dyp, _TN)
                dxdt_cols.append(dxst[:q] * m_l + dxst[q:] * m_r)
                for i in range(2):
                    h = h0 + i
                    dg = dgcat[:, q * i:q * (i + 1)]
                    mm = dg * gf[i]
                    rs_mat = rs_mat + jnp.where(lane == h, jnp.sum(mm, axis=1, keepdims=True), 0.0)
                    cs_mat = cs_mat + jnp.where(sub == h, jnp.sum(mm, axis=0, keepdims=True), 0.0)
                    dcb = dcb + dg * f["decay"][h]
            dcb16 = dcb.astype(BF16)
            dstg = dstate[:, 256 * g:256 * (g + 1)].astype(BF16)
            d_c.append(dcg + _dot(dcb16, bg))
            dxw.append(_dot(bg, dstg))
            d_b.append(_dot(dcb16, cg, _TN) + _dot(xw[:, 256 * g:256 * (g + 1)].astype(BF16), dstg, _NT))
        dxw = jnp.concatenate(dxw, axis=1)
        dxdt = jnp.concatenate(dxdt_cols, axis=1) + dxw * f["w_end"]
        qv = dxw * xw
        end_row = _rsum(qv) + _rsum(dstate * state) * f["cd"]
        x2 = dye * jnp.concatenate(yoff, axis=1) - qv
        row_i = lax.broadcasted_iota(jnp.int32, (q, 1), 0)
        x2 = x2 + jnp.where(row_i == q - 1, end_row, 0.0)
        da_cs = _split_dot(x2, expt, 2) + rs_mat - cs_mat.T
        ddt = _split_dot(dxdt * xs, expt, 2)
        dxs = dsk_ref[...] * dy + dxdt * f["dt_exp"]
        dda = _split_dot_left(triu_ref[...], da_cs, 3)
        ddt = ddt + dda * f["a_row"]
        dalog = _rsum(dda * f["dt"]) * f["a_row"]
        draw = ddt * jax.nn.sigmoid(f["dtp"])
        ddt_ref[s] = draw.astype(BF16)
        dact = jnp.concatenate([dxs] + d_b + d_c, axis=1)
        dpre = dact * (sg * (1.0 + pre * (1.0 - sg)))
        dhead = dhead_ref[s]
        xv = xbc_ref[s]
        shifted = [_shift_rows(dpre, dhead, 3 - k, False) for k in range(3)] + [dpre]
        dxbc = cw_ref[3:4, :] * dpre
        for k in range(3):
            dxbc = dxbc + cw_ref[k:k + 1, :] * shifted[k]
        dxbc_ref[s] = dxbc.astype(BF16)
        dhead_ref[s] = dpre[0:8, :]
        dstate_ref[s] = dstate * f["cd"] + jnp.concatenate(dst_in, axis=1)
        row8 = lax.broadcasted_iota(jnp.int32, (8, 1), 0)
        dcw = jnp.zeros((8, CONV_CH), F32)
        for k in range(4):
            dcw = dcw + jnp.where(row8 == k, _rsum(shifted[k] * xv), 0.0)
        return dcw, _rsum(dpre), _rsum(draw), dalog, _split_dot(ddsk, expt, 3), dnw

    def body(dm_ref, z_ref, xbc_ref, pre_ref, dtr_ref, y_ref, st_ref, cw_ref, cb_ref, dtb_ref, alog_ref, dsk_ref,
             nw_ref, exp_ref, expt_ref, tril_ref, triu_ref, dz_ref, dxbc_ref, ddt_ref, dcw_ref, dcb_ref, ddtb_ref,
             dalog_ref, dd_ref, dnw_ref, dhead_ref, dstate_ref):
        c = pl.program_id(0)
        first = c == 0

        @pl.when(first)
        def _():
            dstate_ref[...] = jnp.zeros_like(dstate_ref)
            dhead_ref[...] = jnp.zeros_like(dhead_ref)

        total = None
        for s in range(nb):
            parts = one_sequence(s, dm_ref, z_ref, xbc_ref, pre_ref, dtr_ref, y_ref, st_ref, cw_ref, dtb_ref, alog_ref,
                                 dsk_ref, nw_ref, exp_ref, expt_ref, tril_ref, triu_ref, dz_ref, dxbc_ref, ddt_ref,
                                 dhead_ref, dstate_ref)
            total = parts if total is None else tuple(a + b for a, b in zip(total, parts))
        dcw = total[0]

        @pl.when(first)
        def _():
            dcw_ref[...] = dcw

        @pl.when(jnp.logical_not(first))
        def _():
            dcw_ref[...] += dcw

        for ref, part in zip((dcb_ref, ddtb_ref, dalog_ref, dd_ref, dnw_ref), total[1:]):
            _acc_rows(ref, part, first)

    consts = [cw, cb, dtb, alog, dskip_exp, nw, expand, expand_t, tril, triu]
    deps = [] if dep is None else [dep]
    n_in = 7 + len(consts)

    def body_skipping_dep(*refs):
        body(*refs[:n_in], *refs[n_in + len(deps):])

    acc = lambda n: jax.ShapeDtypeStruct((1, n), F32)
    sd = lambda n: jax.ShapeDtypeStruct((nb, seq, n), BF16)
    dz, dxbc, ddt, *small_grads = pl.pallas_call(
        body_skipping_dep, name="ssd_bwd", grid=(nc,),
        out_shape=(sd(SSM_WIDTH), sd(CONV_CH), sd(CHUNK), jax.ShapeDtypeStruct((8, CONV_CH), F32), acc(CONV_CH),
                   acc(CHUNK), acc(CHUNK), acc(CHUNK), acc(SSM_WIDTH)),
        in_specs=[row(SSM_WIDTH, col=1), row(SSM_WIDTH), row(CONV_CH), row(CONV_CH), row(CHUNK), row(SSM_WIDTH),
                  states_spec]
        + [_full(a.shape) for a in consts] + [pl.BlockSpec(memory_space=pl.ANY)] * len(deps),
        out_specs=(row(SSM_WIDTH), row(CONV_CH), row(CHUNK), _full((8, CONV_CH)), _full((1, CONV_CH)),
                   _full((1, CHUNK)), _full((1, CHUNK)), _full((1, CHUNK)), _full((1, SSM_WIDTH))),
        scratch_shapes=[pltpu.VMEM((nb, 8, CONV_CH), F32), pltpu.VMEM((nb, N_STATE, SSM_WIDTH), F32)],
        compiler_params=_params("arbitrary"))(
            fold(dmix), fold(z), fold(xbc), fold(pre), fold(dtr), fold(y), states, *consts, *deps)
    return (unfold(dz), unfold(dxbc), unfold(ddt), *small_grads)


def _in_bwd(du, dv, dz, dxbc, ddt, w_in, x, dx2, g1, tm, me, riders=(), dep=None):
    t_tok = x.shape[0]
    steps = t_tok // tm

    n_in = [5 + ("mask" in rd) for rd in riders]
    first_in = [sum(n_in[:r]) for r in range(len(riders))]

    def body(me_ref, du_ref, dv_ref, dz_ref, dxbc_ref, ddt_ref, w_ref, x_ref, dx2_ref, g_ref, *rest):
        outs = rest[len(rest) - 2 - 4 * len(riders):]
        gx_ref, dg_ref = outs[:2]
        i = pl.program_id(0)
        dh = None
        for (a, b), ref in zip(_IN_SPLITS, (du_ref, dv_ref, dz_ref, dxbc_ref, ddt_ref)):
            part = _dot(ref[...], w_ref[a:b, :])
            dh = part if dh is None else dh + part
        dn, dg = _rms_bwd(x_ref[...], g_ref[...], dh)
        gx_ref[...] = dx2_ref[...] + dn
        _acc_rows(dg_ref, dg, i == 0)
        for r in range(len(riders)):
            p_ref, own_ref, w_ref_r, m_ref_r, v_ref_r = rest[first_in[r]:first_in[r] + 5]
            g = _sum_parts(me_ref[0], p_ref, own_ref[0])
            if n_in[r] == 6:
                g = g * rest[first_in[r] + 5][...]
            d, mn, vn = _adamw_math(w_ref_r[...], g, m_ref_r[...], v_ref_r[...])
            for o_ref, val in zip(outs[2 + 4 * r:6 + 4 * r], (g, d, mn, vn)):
                o_ref[...] = val

    row = lambda n: pl.BlockSpec((tm, n), lambda i, me_ref: (i, 0))
    whole = lambda shape: pl.BlockSpec(shape, lambda i, me_ref: (0,) * len(shape))
    widths = [b - a for a, b in _IN_SPLITS]
    deps = [] if dep is None else [dep]
    rider_args, rider_specs, rider_out_shapes, rider_out_specs = [], [], [], []
    for rd in riders:
        rows, cols = rd["w"].shape[0] // steps, rd["w"].shape[1]
        blk = pl.BlockSpec((rows, cols), lambda i, me_ref: (i, 0))
        rider_args += [rd["parts"], rd["own"], rd["w"], rd["m"], rd["v"]]
        rider_specs += [pl.BlockSpec((N_DEV, rows, cols), lambda i, me_ref: (0, i, 0)),
                        pl.BlockSpec((1, rows, cols), lambda i, me_ref: (me_ref[0], i, 0)), blk, blk, blk]
        if "mask" in rd:
            rider_args.append(rd["mask"])
            rider_specs.append(whole((rows, cols)))
        rider_out_shapes += [jax.ShapeDtypeStruct(rd["w"].shape, F32)] * 4
        rider_out_specs += [blk] * 4
    outs = pl.pallas_call(
        body, name="in_bwd",
        out_shape=(jax.ShapeDtypeStruct((t_tok, D_MODEL), F32), jax.ShapeDtypeStruct((1, D_MODEL), F32),
                   *rider_out_shapes),
        grid_spec=pltpu.PrefetchScalarGridSpec(
            num_scalar_prefetch=1, grid=(steps,),
            in_specs=[row(n) for n in widths] + [whole((IN_PAD, D_MODEL)), row(D_MODEL), row(D_MODEL),
                                                 whole((1, D_MODEL))] + rider_specs
            + [pl.BlockSpec(memory_space=pl.ANY)] * len(deps),
            out_specs=(row(D_MODEL), whole((1, D_MODEL)), *rider_out_specs)),
        compiler_params=_params("arbitrary"))(me, du, dv, dz, dxbc, ddt, w_in, x, dx2, g1, *rider_args, *deps)
    return outs[0], outs[1], [tuple(outs[2 + 4 * r:6 + 4 * r]) for r in range(len(riders))]


def _pad_lanes(a, n):
    return jnp.pad(a, ((0, 0), (0, n - a.shape[1])))


def _local_step(x, target, seq, small, hooks, first_dep=None):
    t_tok = x.shape[0]
    tm = min(TOKEN_TILE, t_tok)
    avg, expand, expand_t, tril, triu = _const_mats()
    g1, g2, g3, g4 = (small[k].reshape(1, D_MODEL) for k in
                      ("norm_mix_pre", "norm_mix_post", "norm_ffn_pre", "norm_ffn_post"))
    tie = (lambda a: a) if first_dep is None else (lambda a: a + first_dep[0, 0])
    lnw = tie(small["gm_ln_w"]).reshape(1, GM_WIDTH)
    lnb = tie(small["gm_ln_b"]).reshape(1, GM_WIDTH)
    causal = jnp.tril(jnp.ones((CHUNK, CHUNK), F32))
    wm = tie(small["gm_w_s"]) * causal
    pair = lambda w: w.reshape(4, 2, CHUNK, CHUNK).transpose(0, 2, 1, 3).reshape(4, CHUNK, 2 * CHUNK).astype(BF16)
    wcat = pair(wm)
    wtcat = pair(jnp.swapaxes(wm, 1, 2))
    bias = jnp.repeat(tie(small["gm_b_s"]).T, HEAD_DIM, axis=1)
    cb = small["conv_b"].reshape(1, CONV_CH)
    dtb = _pad_lanes(tie(small["dt_bias"]).reshape(1, N_HEADS), CHUNK)
    alog = _pad_lanes(tie(small["a_log"]).reshape(1, N_HEADS), CHUNK)
    dskip_exp = jnp.repeat(tie(small["d_skip"]).reshape(1, N_HEADS), HEAD_DIM, axis=1)
    nw = small["ssm_norm_w"].reshape(1, SSM_WIDTH)

    h1 = _prenorm(x, g1, tm, hooks.get("prenorm_after", first_dep))
    w_in_t, conv_w = hooks["mixer_weights"](h1)
    tall = min(2 * tm, t_tok)
    u, v, z, xbc, dtr = _in_proj(h1, w_in_t, tall)
    mix_a = _gmlp_fwd(u, v, lnw, lnb, wcat, bias, avg)
    mix_b, y_pre, states, pre = _ssd_fwd(z, xbc, dtr, conv_w, cb, dtb, alog, dskip_exp, nw, expand, tril, seq)
    w_out, dep = hooks["mixers_done"](mix_b)
    o, x2, h3 = _out_proj(mix_a, mix_b, w_out, x, g2, g3, tall, dep)
    w_up, w_down = hooks["mlp_weights"](h3)
    tf = FF_TILE
    ra, dd, dy, dg4, loss = _mlp_fwd(h3, w_up, w_down, x2, target, g4, tm, tf)

    da, dx2, do, dg3, dg2 = _mlp_bwd(dd, w_down, ra, w_up, x2, dy, o, g3, g2, tm, tf)
    g_w_down = _wgrad(ra, dd, None, WGRAD_TILE, D_MODEL, t_tok, True, "wgrad_down")
    g_w_up = _wgrad(h3, da, N_DEV, D_MODEL, D_FF // N_DEV, t_tok, False, "wgrad_up")
    dep = hooks["mlp_grads"](g_w_down, g_w_up)
    dmix = _dmix(do, w_out, tall, dep)
    g_w_out = _wgrad_pieces(do, (mix_a, mix_b), WGRAD_TILE, "wgrad_out", dep)
    du, dv, dws, dbt, dlnw, dlnb = _gmlp_bwd(dmix, u, v, lnw, lnb, wcat, wtcat, bias, avg, expand_t)
    dep = hooks["gmlp_grads"](g_w_out, dws)
    dz, dxbc, ddt, dcw, dcb, ddtb, dalog, ddsk, dnw = _ssd_bwd(
        dmix, z, xbc, pre, dtr, y_pre, states, conv_w, cb, dtb, alog, dskip_exp, nw, expand, expand_t, tril, triu, seq,
        dep)
    g_w_in = _wgrad_in_chunked(h1, (du, dv, dz, dxbc, ddt), WGRAD_TILE, t_tok // 2, dep)
    dep = hooks["in_grads"](g_w_in, dcw[0:4])
    riders = hooks["arrived_updates"](dep) if "arrived_updates" in hooks else []
    me = hooks.get("me", jnp.zeros((1,), jnp.int32))
    grad_x, dg1, updates = _in_bwd(du, dv, dz, dxbc, ddt, w_in_t, x, dx2, g1, tm, me, riders, dep)

    grads = dict(
        updates=updates,
        w_in=g_w_in, w_out=g_w_out, w_up=g_w_up, w_down=g_w_down, conv_w=dcw[0:4],
        norm_mix_pre=dg1, norm_mix_post=dg2, norm_ffn_pre=dg3, norm_ffn_post=dg4, gm_ln_w=dlnw, gm_ln_b=dlnb,
        gm_w_s=dws, gm_b_s=dbt, conv_b=dcb, dt_bias=ddtb, a_log=dalog, d_skip=ddsk, ssm_norm_w=dnw)
    return loss[0, 0], grad_x, grads


_WEIGHTS = ("norm_mix_pre", "w_in", "gm_ln_w", "gm_ln_b", "gm_w_s", "gm_b_s", "conv_w", "conv_b", "dt_bias", "a_log",
            "d_skip", "ssm_norm_w", "w_out", "norm_mix_post", "norm_ffn_pre", "w_up", "w_down", "norm_ffn_post")
_SLAB_ROWS = (("norm_mix_pre", 1024), ("norm_mix_post", 1024), ("norm_ffn_pre", 1024), ("norm_ffn_post", 1024),
              ("conv_b", 1024), ("ssm_norm_w", 512), ("gm_ln_w", 512), ("gm_ln_b", 512), ("dt_bias", 8), ("a_log", 8),
              ("d_skip", 8))
_SLAB_LOSS_ROW = len(_SLAB_ROWS)
_SLAB_BS_ROW = 16
_SMALL_PARAMS = tuple(name for name, _ in _SLAB_ROWS) + ("gm_b_s",)
_LN_PARAMS = ("gm_ln_w", "gm_ln_b")


_SLAB_CONV_ROW = _SLAB_LOSS_ROW + 1


def _pack_slab(g, loss_part):
    rows = [_pad_lanes(g[name], D_MODEL) for name, _ in _SLAB_ROWS]
    rows.append(jnp.broadcast_to(loss_part, (1, D_MODEL)))
    rows.append(g["conv_w"])
    assert sum(r.shape[0] for r in rows) == _SLAB_BS_ROW
    rows.append(_pad_lanes(g["gm_b_s"].T[0:N_HEADS], D_MODEL))
    return jnp.concatenate(rows, axis=0)


def _adamw_slab(parts, me, w, m, v):
    names = _SMALL_PARAMS + ("conv_w",)
    shapes = [w[k].shape for k in names]
    unfold = np.zeros((GM_WIDTH, HEAD_DIM), np.float32)
    for h in range(N_HEADS):
        unfold[h * HEAD_DIM:(h + 1) * HEAD_DIM, :] = np.eye(HEAD_DIM)
    unfold = jnp.asarray(unfold, dtype=BF16)
    n = len(names)
    shard = CONV_CH // N_DEV

    def body(me_ref, p_ref, unfold_ref, *refs):
        w_refs, m_refs, v_refs = refs[:n], refs[n:2 * n], refs[2 * n:3 * n]
        outs = refs[3 * n:]
        g_all = p_ref[0]
        for j in range(1, N_DEV):
            g_all = g_all + p_ref[j]
        lane = lax.broadcasted_iota(jnp.int32, (N_HEADS, GM_WIDTH), 1)
        head = lax.broadcasted_iota(jnp.int32, (N_HEADS, GM_WIDTH), 0)
        own_lanes = jnp.logical_and(lane >= head * HEAD_DIM, lane < (head + 1) * HEAD_DIM)
        mine = pl.ds(pl.multiple_of(me_ref[0] * shard, shard), shard)
        for i, name in enumerate(names):
            if name == "gm_b_s":
                g = g_all[_SLAB_BS_ROW:_SLAB_BS_ROW + N_HEADS, 0:CHUNK]
            elif name == "conv_w":
                g = p_ref[0, _SLAB_CONV_ROW:_SLAB_CONV_ROW + 4, mine]
                for j in range(1, N_DEV):
                    g = g + p_ref[j, _SLAB_CONV_ROW:_SLAB_CONV_ROW + 4, mine]
            else:
                row = [r for r, (k, _) in enumerate(_SLAB_ROWS) if k == name][0]
                g = g_all[row:row + 1, 0:dict(_SLAB_ROWS)[name]]
                if name in _LN_PARAMS:
                    g = _split_dot(jnp.where(own_lanes, g, 0.0), unfold_ref[...], 3)
            d, mn, vn = _adamw_math(w_refs[i][...], g, m_refs[i][...], v_refs[i][...])
            for o_ref, val in zip(outs[4 * i:4 * i + 4], (g, d, mn, vn)):
                o_ref[...] = val
        outs[-1][...] = g_all[_SLAB_LOSS_ROW:_SLAB_LOSS_ROW + 1, 0:128]

    def whole(shape):
        nd = len(shape)
        return pl.BlockSpec(shape, lambda i, me_ref: (0,) * nd)

    ins = [parts, unfold] + [d[k] for d in (w, m, v) for k in names]
    out_shape = tuple(jax.ShapeDtypeStruct(s, F32) for s in shapes for _ in range(4)) + (
        jax.ShapeDtypeStruct((1, 128), F32),)
    outs = pl.pallas_call(
        body, name="adamw_small", out_shape=out_shape,
        grid_spec=pltpu.PrefetchScalarGridSpec(
            num_scalar_prefetch=1, grid=(1,), in_specs=[whole(a.shape) for a in ins],
            out_specs=tuple(whole(s.shape) for s in out_shape)),
        compiler_params=_params("arbitrary"))(me, *ins)
    return {k: tuple(outs[4 * i:4 * i + 4]) for i, k in enumerate(names)}, outs[-1][0, 0]


def kernel(x, norm_mix_pre, w_in, gm_ln_w, gm_ln_b, gm_w_s, gm_b_s, conv_w, conv_b, dt_bias, a_log, d_skip, ssm_norm_w, w_out, norm_mix_post, norm_ffn_pre, w_up, w_down, norm_ffn_post, loss_target, m_norm_mix_pre, m_w_in, m_gm_ln_w, m_gm_ln_b, m_gm_w_s, m_gm_b_s, m_conv_w, m_conv_b, m_dt_bias, m_a_log, m_d_skip, m_ssm_norm_w, m_w_out, m_norm_mix_post, m_norm_ffn_pre, m_w_up, m_w_down, m_norm_ffn_post, v_norm_mix_pre, v_w_in, v_gm_ln_w, v_gm_ln_b, v_gm_w_s, v_gm_b_s, v_conv_w, v_conv_b, v_dt_bias, v_a_log, v_d_skip, v_ssm_norm_w, v_w_out, v_norm_mix_post, v_norm_ffn_pre, v_w_up, v_w_down, v_norm_ffn_post):
    w = dict(norm_mix_pre=norm_mix_pre, w_in=w_in, gm_ln_w=gm_ln_w, gm_ln_b=gm_ln_b, gm_w_s=gm_w_s, gm_b_s=gm_b_s, conv_w=conv_w, conv_b=conv_b, dt_bias=dt_bias, a_log=a_log, d_skip=d_skip, ssm_norm_w=ssm_norm_w, w_out=w_out, norm_mix_post=norm_mix_post, norm_ffn_pre=norm_ffn_pre, w_up=w_up, w_down=w_down, norm_ffn_post=norm_ffn_post)
    m = dict(norm_mix_pre=m_norm_mix_pre, w_in=m_w_in, gm_ln_w=m_gm_ln_w, gm_ln_b=m_gm_ln_b, gm_w_s=m_gm_w_s, gm_b_s=m_gm_b_s, conv_w=m_conv_w, conv_b=m_conv_b, dt_bias=m_dt_bias, a_log=m_a_log, d_skip=m_d_skip, ssm_norm_w=m_ssm_norm_w, w_out=m_w_out, norm_mix_post=m_norm_mix_post, norm_ffn_pre=m_norm_ffn_pre, w_up=m_w_up, w_down=m_w_down, norm_ffn_post=m_norm_ffn_post)
    v = dict(norm_mix_pre=v_norm_mix_pre, w_in=v_w_in, gm_ln_w=v_gm_ln_w, gm_ln_b=v_gm_ln_b, gm_w_s=v_gm_w_s, gm_b_s=v_gm_b_s, conv_w=v_conv_w, conv_b=v_conv_b, dt_bias=v_dt_bias, a_log=v_a_log, d_skip=v_d_skip, ssm_norm_w=v_ssm_norm_w, w_out=v_w_out, norm_mix_post=v_norm_mix_post, norm_ffn_pre=v_norm_ffn_pre, w_up=v_w_up, w_down=v_w_down, norm_ffn_post=v_norm_ffn_post)
    n_batch, seq, _ = x.shape
    shard_in = IN_COLS // N_DEV

    me = (4 * lax.axis_index("x") + 2 * lax.axis_index("y") + lax.axis_index("c")).astype(jnp.int32).reshape(1)

    def in_slot(own):
        return lax.dynamic_update_slice(lax.empty((N_DEV,) + own.shape, own.dtype), own[None],
                                        (me[0],) + (0,) * own.ndim)

    w_in_sh, m_in_sh, v_in_sh = w_in[0].T, m_w_in[0].T, v_w_in[0].T
    first = [_cast_to_slot(w_in_sh, me, shard_in, "cast_w_in"), in_slot(conv_w[0]),
             _cast_to_slot(w_out[0], me, 128, "cast_w_out")]
    ici_1, tok_ici_1 = _exchange_start(first, [True] * 3, _SAME_CORE_PEERS, "gather_mix_ici_start")
    cast_up = _cast_to_slot(w_up[0], me, 1024, "cast_w_up", cols=True, dep=tok_ici_1)
    second = [cast_up, _cast_to_slot(w_down[0], me, 512, "cast_w_down", dep=cast_up)]
    gathering = {}

    def mixer_weights(after):
        bufs = [buf for buf, _ in _exchange_wait(ici_1, after, "gather_mix_ici_wait")]
        d2d_1, tok_d2d_1 = _exchange_start(bufs, [True] * 3, _SIBLING_FORWARD, "gather_mix_d2d_start")
        gathering["mlp_ici"], tok_ici_2 = _exchange_start(
            second, [True] * 2, _SAME_CORE_PEERS, "gather_mlp_ici_start", dep=tok_d2d_1)
        (_, ag_in), (_, ag_conv), (_, ag_out) = _exchange_wait(d2d_1, tok_ici_2, "gather_mix_d2d_wait")
        gathering["w_out"] = ag_out.reshape(D_MODEL, D_MODEL)
        w_in_t = jnp.pad(ag_in.reshape(IN_COLS, D_MODEL), ((0, IN_PAD - IN_COLS), (0, 0)))
        return w_in_t, ag_conv.transpose(1, 0, 2).reshape(4, CONV_CH)

    def mixers_done(after):
        bufs = [buf for buf, _ in _exchange_wait(gathering["mlp_ici"], after, "gather_mlp_ici_wait")]
        gathering["mlp"], tok = _exchange_start(bufs, [True] * 2, _SIBLING_FORWARD, "gather_mlp_d2d_start")
        return gathering["w_out"], tok

    def mlp_weights(after):
        (_, ag_up), (_, ag_down) = _exchange_wait(gathering["mlp"], after, "gather_mlp_d2d_wait")
        return ag_up, ag_down.reshape(D_FF, D_MODEL)

    sent = {}

    def mlp_grads(g_w_down, g_w_up):
        sent["mlp"], tok = _exchange_start(
            [g_w_down.reshape(N_DEV, D_FF // N_DEV, D_MODEL), g_w_up], [False, False], _ALL_PEERS, "grads_mlp_start")
        return tok

    def gmlp_grads(g_w_out, g_w_s):
        sent["gmlp"], tok = _exchange_start(
            [g_w_out.reshape(N_DEV, D_MODEL // N_DEV, D_MODEL), in_slot(g_w_s.astype(BF16))], [False, True], _ALL_PEERS,
            "grads_gmlp_start")
        return tok

    def in_grads(g_w_in_t, g_conv_w):
        g_in_blk = g_w_in_t[:IN_COLS].reshape(N_DEV, shard_in, D_MODEL)
        sent["in"], tok = _exchange_start([g_in_blk], [False], _ALL_PEERS, "grads_in_start")
        return tok

    def arrived_updates(after):
        (own_down, p_down), (own_up, p_up) = _exchange_wait(sent["mlp"], after, "grads_mlp_wait")
        (own_out, p_out), (_, p_ws) = _exchange_wait(sent["gmlp"], own_up, "grads_gmlp_wait")
        rows = lambda t: t.reshape(t.shape[:-3] + (N_HEADS * CHUNK, CHUNK))
        return [dict(parts=p_up, own=own_up, w=w_up[0], m=m_w_up[0], v=v_w_up[0]),
                dict(parts=p_down, own=own_down, w=w_down[0], m=m_w_down[0], v=v_w_down[0]),
                dict(parts=p_out, own=own_out, w=w_out[0], m=m_w_out[0], v=v_w_out[0]),
                dict(parts=rows(p_ws), own=rows(p_ws), w=rows(gm_w_s[0]), m=rows(m_gm_w_s[0]), v=rows(v_gm_w_s[0]),
                     mask=jnp.tril(jnp.ones((CHUNK, CHUNK), F32)))]

    small = {k: w[k][0] for k in _SMALL_PARAMS + ("gm_w_s",)}
    loss_part, grad_x, g = _local_step(
        x.reshape(n_batch * seq, D_MODEL), loss_target.reshape(n_batch * seq, D_MODEL), seq, small,
        dict(mixer_weights=mixer_weights, mixers_done=mixers_done, mlp_weights=mlp_weights, mlp_grads=mlp_grads,
             gmlp_grads=gmlp_grads, in_grads=in_grads, arrived_updates=arrived_updates, me=me,
             prenorm_after=second[1]), first_dep=tok_ici_1)

    sent_rows, tok_rows = _exchange_start([in_slot(_pack_slab(g, loss_part))], [True], _ALL_PEERS, "grads_rows_start")
    res = dict(zip(("w_up", "w_down", "w_out", "gm_w_s"), g["updates"]))
    ((own_in, p_in),) = _exchange_wait(sent["in"], tok_rows, "grads_in_wait")
    res["w_in"] = tuple(r.T for r in _adamw_reduce(p_in, own_in, me, w_in_sh, m_in_sh, v_in_sh, shard_in, "adamw_w_in"))
    ((_, p_rows),) = _exchange_wait(sent_rows, res["w_in"][1], "grads_rows_wait")
    flat = lambda t: t[0] if t.ndim == 3 else t
    small_res, loss = _adamw_slab(
        p_rows, me, *({k: flat(d[k]) for k in _SMALL_PARAMS + ("conv_w",)} for d in (w, m, v)))
    res.update(small_res)
    res = {k: tuple(r.reshape(w[k].shape) for r in res[k]) for k in _WEIGHTS}

    outs = [loss, grad_x.reshape(x.shape)]
    for part in range(4):
        outs.extend(res[k][part] for k in _WEIGHTS)
    return tuple(outs)
```

```python
import functools

import jax
import jax.numpy as jnp
import numpy as np
from jax import lax
from jax.experimental import pallas as pl
from jax.experimental.pallas import tpu as pltpu

F32 = jnp.float32
BF16 = jnp.bfloat16

D_MODEL = 1024
GM_WIDTH = 512
SSM_WIDTH = 512
CONV_CH = 1024
N_HEADS = 8
HEAD_DIM = 64
N_STATE = 128
CHUNK = 128
D_FF = 4096
IN_COLS = 2568
IN_PAD = 2688
N_DEV = 8
EPS = 1e-6
ADAM_LR, ADAM_B1, ADAM_B2, ADAM_EPS, ADAM_WD, ADAM_STEP = 0.001, 0.9, 0.999, 1e-08, 0.01, 10
VMEM_LIMIT_BYTES = 56 * 1024 * 1024
TOKEN_TILE = 512
FF_TILE = 2048
WGRAD_TILE = 512

_NT = (((1,), (1,)), ((), ()))
_TN = (((0,), (0,)), ((), ()))


def _params(*sem):
    return pltpu.CompilerParams(dimension_semantics=sem or None, vmem_limit_bytes=VMEM_LIMIT_BYTES)


def _dot(a, b, dims=None):
    if dims is None:
        return jnp.dot(a, b, preferred_element_type=F32)
    return lax.dot_general(a, b, dims, preferred_element_type=F32)


def _split_terms(x, terms):
    out, rem = [], x
    for i in range(terms):
        hi = rem.astype(BF16)
        out.append(hi)
        if i + 1 < terms:
            rem = rem - hi.astype(F32)
    return out


def _split_dot(x, m, terms):
    acc = None
    for hi in _split_terms(x, terms):
        part = _dot(hi, m)
        acc = part if acc is None else acc + part
    return acc


def _split_dot_left(m, x, terms):
    acc = None
    for hi in _split_terms(x, terms):
        part = _dot(m, hi)
        acc = part if acc is None else acc + part
    return acc


def _gelu_and_grad(x):
    c = 0.7978845608028654
    inner = c * (x + 0.044715 * x * x * x)
    t = jnp.tanh(inner)
    g = 0.5 * x * (1.0 + t)
    dg = 0.5 * (1.0 + t) + 0.5 * x * (1.0 - t * t) * c * (1.0 + 3.0 * 0.044715 * x * x)
    return g, dg


def _softplus(x):
    return jnp.maximum(x, 0.0) + jnp.log(1.0 + jnp.exp(-jnp.abs(x)))


def _rsum(x):
    return jnp.sum(x, axis=0, keepdims=True)


def _acc_rows(ref, part, first):
    val = jnp.broadcast_to(part, ref.shape)

    @pl.when(first)
    def _():
        ref[...] = val

    @pl.when(jnp.logical_not(first))
    def _():
        ref[...] += val


def _rms_bwd(n, g, dout):
    r = lax.rsqrt(jnp.mean(n * n, axis=-1, keepdims=True) + EPS)
    nh = n * r
    dg = dout * g
    dn = r * (dg - nh * jnp.mean(dg * nh, axis=-1, keepdims=True))
    return dn, _rsum(dout * nh)


def _const_mats():
    avg = np.kron(np.eye(4), np.full((HEAD_DIM, HEAD_DIM), 1.0 / HEAD_DIM))
    expand = np.zeros((CHUNK, SSM_WIDTH), np.float32)
    for h in range(N_HEADS):
        expand[h, h * HEAD_DIM:(h + 1) * HEAD_DIM] = 1.0
    tril = np.tril(np.ones((CHUNK, CHUNK), np.float32))
    as_bf16 = lambda a: jnp.asarray(a, dtype=BF16)
    return as_bf16(avg), as_bf16(expand), as_bf16(expand.T), as_bf16(tril), as_bf16(tril.T)


def _full(shape):
    nd = len(shape)
    return pl.BlockSpec(shape, lambda *_: (0,) * nd)


_HBM = pl.BlockSpec(memory_space=pltpu.HBM)
_SEM = pl.BlockSpec(memory_space=pltpu.SEMAPHORE)
_ALL_PEERS = tuple((k, 0) for k in range(1, N_DEV))
_SAME_CORE_PEERS = ((2, 0), (4, 0), (6, 0))
_SIBLING_FORWARD = ((1, 0), (1, 2), (1, 4), (1, 6))


def _flip(j, k):
    for bit in (4, 2, 1):
        if k & bit:
            j = j + bit - 2 * (j & bit)
    return j


def _copies(src, land, send_sems, recv_sems, hops, slots=None):
    x, y, c = lax.axis_index("x"), lax.axis_index("y"), lax.axis_index("c")
    me = 4 * x + 2 * y + c
    slots = range(len(src)) if slots is None else slots
    out = []
    for t in range(len(src)):
        for i, (k, b) in enumerate(hops):
            pos = (1 - x if k & 4 else x, 1 - y if k & 2 else y, 1 - c if k & 1 else c)
            peer = _flip(me, k)
            sem = slots[t] * len(hops) + i
            mk = functools.partial(pltpu.make_async_remote_copy, send_sem=send_sems.at[sem], recv_sem=recv_sems.at[sem],
                                   device_id=pos, device_id_type=pl.DeviceIdType.MESH)
            if land[t] is None and src[t].shape[0] != N_DEV:
                width = src[t].shape[1] // N_DEV
                slab = lambda j: src[t].at[:, pl.ds(pl.multiple_of(j * width, 128), width)]
                mine = functools.partial(mk, src_ref=slab(_flip(me, b)), dst_ref=slab(_flip(me, b)))
                theirs = functools.partial(mk, src_ref=slab(_flip(peer, b)), dst_ref=slab(_flip(peer, b)))
            elif land[t] is None:
                mine = functools.partial(mk, src_ref=src[t].at[_flip(me, b)], dst_ref=src[t].at[_flip(me, b)])
                theirs = functools.partial(mk, src_ref=src[t].at[_flip(peer, b)], dst_ref=src[t].at[_flip(peer, b)])
            else:
                assert b == 0
                mine = functools.partial(mk, src_ref=src[t].at[peer], dst_ref=land[t].at[me])
                theirs = functools.partial(mk, src_ref=src[t].at[peer], dst_ref=land[t].at[peer])
            out.append((mine, theirs))
    return out


def _exchange_start(srcs, inplace, peers, name, dep=None):
    n = len(srcs)
    lands = [None if ip else pltpu.with_memory_space_constraint(lax.empty(s.shape, s.dtype), pltpu.HBM)
             for s, ip in zip(srcs, inplace)]
    real_lands = [l for l in lands if l is not None]
    n_l = len(real_lands)
    deps = [] if dep is None else [dep]

    def body(*refs):
        src = refs[:n]
        land_refs = list(refs[n:n + n_l])
        send_sems, recv_sems = refs[n + n_l + len(deps)], refs[n + n_l + len(deps) + 1]
        token = refs[-1]
        land = [None if ip else land_refs.pop(0) for ip in inplace]
        for mine, _ in _copies(src, land, send_sems, recv_sems, peers):
            mine().start()
        token[...] = jnp.zeros_like(token)

    sem_t = pltpu.SemaphoreType.DMA((n * len(peers),))
    outs = pl.pallas_call(
        body, name=name,
        out_shape=(sem_t, sem_t) + tuple(pltpu.HBM(a.shape, a.dtype) for a in list(srcs) + real_lands)
        + (jax.ShapeDtypeStruct((8, 128), F32),),
        in_specs=[_HBM] * (n + n_l) + [pl.BlockSpec(memory_space=pl.ANY)] * len(deps),
        out_specs=(_SEM, _SEM) + (_HBM,) * (n + n_l) + (pl.BlockSpec(memory_space=pltpu.VMEM),),
        input_output_aliases={i: 2 + i for i in range(n + n_l)},
        compiler_params=pltpu.CompilerParams(has_side_effects=pltpu.SideEffectType.DATAFLOW_SIDE_EFFECTING),
    )(*[pltpu.with_memory_space_constraint(s, pltpu.HBM) for s in srcs], *real_lands, *deps)
    handle = dict(send=outs[0], recv=outs[1], srcs=outs[2:2 + n], lands=outs[2 + n:2 + n + n_l], inplace=inplace,
                  peers=peers)
    return handle, outs[-1]


def _exchange_wait(handle, after, name, only=None):
    srcs, lands, inplace, peers = handle["srcs"], handle["lands"], handle["inplace"], handle["peers"]
    slots = None
    if only is not None:
        assert all(inplace)
        slots, srcs, inplace = list(only), [srcs[t] for t in only], [True] * len(only)
    n, n_l = len(srcs), len(lands)

    def body(*refs):
        src = refs[:n]
        land_refs = list(refs[n:n + n_l])
        send_sems, recv_sems = refs[n + n_l], refs[n + n_l + 1]
        land = [None if ip else land_refs.pop(0) for ip in inplace]
        for mine, theirs in _copies(src, land, send_sems, recv_sems, peers, slots):
            mine().wait_send()
            theirs().wait_recv()

    outs = pl.pallas_call(
        body, name=name, out_shape=tuple(pltpu.HBM(a.shape, a.dtype) for a in list(srcs) + list(lands)),
        in_specs=[_HBM] * (n + n_l) + [_SEM, _SEM, pl.BlockSpec(memory_space=pl.ANY)],
        out_specs=(_HBM,) * (n + n_l), input_output_aliases={i: i for i in range(n + n_l)},
        compiler_params=pltpu.CompilerParams(has_side_effects=pltpu.SideEffectType.DATAFLOW_SIDE_EFFECTING),
    )(*srcs, *lands, handle["send"], handle["recv"], after)
    res, land_out = [], list(outs[n:])
    for t in range(n):
        res.append((outs[t], outs[t] if inplace[t] else land_out.pop(0)))
    return res


def _cast_to_slot(w, me, rows, name, cols=False, dep=None):
    r, cdim = w.shape
    deps = [] if dep is None else [dep]

    def body(me_ref, w_ref, *rest):
        o_ref = rest[-1]
        if cols:
            o_ref[...] = w_ref[...].astype(BF16)
        else:
            o_ref[0] = w_ref[...].astype(BF16)

    if cols:
        out_shape = jax.ShapeDtypeStruct((r, N_DEV * cdim), BF16)
        out_spec = pl.BlockSpec((rows, cdim), lambda i, me_ref: (i, me_ref[0]))
    else:
        out_shape = jax.ShapeDtypeStruct((N_DEV, r, cdim), BF16)
        out_spec = pl.BlockSpec((1, rows, cdim), lambda i, me_ref: (me_ref[0], i, 0))
    return pl.pallas_call(
        body, name=name, out_shape=out_shape,
        grid_spec=pltpu.PrefetchScalarGridSpec(
            num_scalar_prefetch=1, grid=(r // rows,),
            in_specs=[pl.BlockSpec((rows, cdim), lambda i, me_ref: (i, 0))]
            + [pl.BlockSpec(memory_space=pl.ANY)] * len(deps), out_specs=out_spec),
        compiler_params=_params("parallel"))(me, w, *deps)


def _adamw_math(w, g, m, v):
    m = ADAM_B1 * m + (1.0 - ADAM_B1) * g
    v = ADAM_B2 * v + (1.0 - ADAM_B2) * (g * g)
    m_hat = m / (1.0 - ADAM_B1 ** ADAM_STEP)
    v_hat = v / (1.0 - ADAM_B2 ** ADAM_STEP)
    delta = -ADAM_LR * (m_hat / (jnp.sqrt(v_hat) + ADAM_EPS) + ADAM_WD * w)
    return delta, m, v


def _sum_parts(me, p_ref, own):
    g = None
    for j in range(N_DEV):
        term = (p_ref[j] if own is None else jnp.where(me == j, own, p_ref[j])).astype(F32)
        g = term if g is None else g + term
    return g


def _adamw_reduce(parts, own, me, w, m, v, rows, name):
    r, cdim = w.shape

    def body(me_ref, p_ref, own_ref, w_ref, m_ref, v_ref, g_out, d_out, m_out, v_out):
        g = _sum_parts(me_ref[0], p_ref, own_ref[0])
        d, mn, vn = _adamw_math(w_ref[...], g, m_ref[...], v_ref[...])
        g_out[...] = g
        d_out[...] = d
        m_out[...] = mn
        v_out[...] = vn

    blk = pl.BlockSpec((rows, cdim), lambda i, me_ref: (i, 0))
    sds = jax.ShapeDtypeStruct(w.shape, F32)
    return pl.pallas_call(
        body, name=name, out_shape=(sds,) * 4,
        grid_spec=pltpu.PrefetchScalarGridSpec(
            num_scalar_prefetch=1, grid=(r // rows,),
            in_specs=[pl.BlockSpec((N_DEV, rows, cdim), lambda i, me_ref: (0, i, 0)),
                      pl.BlockSpec((1, rows, cdim), lambda i, me_ref: (me_ref[0], i, 0)), blk, blk, blk],
            out_specs=(blk,) * 4),
        compiler_params=_params("parallel"))(me, parts, own, w, m, v)


_IN_SPLITS = ((0, 512), (512, 1024), (1024, 1536), (1536, 2560), (2560, IN_PAD))


def _prenorm(x, g1, tm, dep=None):
    t_tok = x.shape[0]
    deps = [] if dep is None else [dep]

    def body(x_ref, g_ref, *rest):
        xv = x_ref[...]
        r = lax.rsqrt(jnp.mean(xv * xv, axis=-1, keepdims=True) + EPS)
        rest[-1][...] = (xv * r * g_ref[...]).astype(BF16)

    row = pl.BlockSpec((tm, D_MODEL), lambda i: (i, 0))
    return pl.pallas_call(
        body, name="prenorm", grid=(t_tok // tm,), out_shape=jax.ShapeDtypeStruct((t_tok, D_MODEL), BF16),
        in_specs=[row, _full((1, D_MODEL))] + [pl.BlockSpec(memory_space=pl.ANY)] * len(deps), out_specs=row,
        compiler_params=_params("parallel"))(x, g1, *deps)


def _in_proj(h1, w_in, tm):
    t_tok = h1.shape[0]

    def body(h_ref, w_ref, *outs):
        h = h_ref[...]
        for (a, b), o_ref in zip(_IN_SPLITS, outs):
            o_ref[...] = _dot(h, w_ref[a:b, :], _NT).astype(o_ref.dtype)

    row = lambda n: pl.BlockSpec((tm, n), lambda i: (i, 0))
    widths = [b - a for a, b in _IN_SPLITS]
    dtypes = (BF16, BF16, BF16, F32, F32)
    return pl.pallas_call(
        body, name="in_proj", grid=(t_tok // tm,),
        out_shape=tuple(jax.ShapeDtypeStruct((t_tok, n), dt) for n, dt in zip(widths, dtypes)),
        in_specs=[row(D_MODEL), _full((IN_PAD, D_MODEL))], out_specs=tuple(row(n) for n in widths),
        compiler_params=_params("parallel"))(h1, w_in)


def _lane_masks():
    lane = lax.broadcasted_iota(jnp.int32, (1, 2 * HEAD_DIM), 1)
    left = (lane < HEAD_DIM).astype(F32)
    return left, 1.0 - left


def _stack_pair(v, m_l, m_r):
    return jnp.concatenate([v * m_l, v * m_r], axis=0).astype(BF16)


def _head_mean(x, avg):
    n = avg.shape[0]
    return jnp.concatenate([_split_dot(x[:, n * i:n * (i + 1)], avg, 2) for i in range(x.shape[1] // n)], axis=1)


def _gmlp_common(u, v, lnw, lnb, avg, wcat_ref, bias, m_l, m_r):
    ug, dug = _gelu_and_grad(u)
    vg, dvg = _gelu_and_grad(v)
    mu = _head_mean(vg, avg)
    vc = vg - mu
    var = _head_mean(vc * vc, avg)
    rstd = lax.rsqrt(var + EPS)
    vhat = vc * rstd
    vn = vhat * lnw + lnb
    rows = []
    for r in range(u.shape[0] // CHUNK):
        cols = []
        for j in range(N_HEADS // 2):
            pair = vn[CHUNK * r:CHUNK * (r + 1), 128 * j:128 * (j + 1)]
            cols.append(_dot(wcat_ref[j], _stack_pair(pair, m_l, m_r)))
        rows.append(jnp.concatenate(cols, axis=1) + bias)
    mixed = jnp.concatenate(rows, axis=0)
    return ug, dug, dvg, rstd, vhat, vn, mixed


_GMLP_ROWS = 4 * CHUNK


def _gmlp_fwd(u, v, lnw, lnb, wcat, bias, avg):
    t_tok = u.shape[0]
    tm = min(_GMLP_ROWS, t_tok)

    def body(u_ref, v_ref, lnw_ref, lnb_ref, wcat_ref, bias_ref, avg_ref, o_ref):
        m_l, m_r = _lane_masks()
        ug, _, _, _, _, _, mixed = _gmlp_common(
            u_ref[...].astype(F32), v_ref[...].astype(F32), lnw_ref[...], lnb_ref[...], avg_ref[...], wcat_ref,
            bias_ref[...], m_l, m_r)
        o_ref[...] = (ug * mixed).astype(BF16)

    row = pl.BlockSpec((tm, GM_WIDTH), lambda i: (i, 0))
    return pl.pallas_call(
        body, name="gmlp_fwd", grid=(t_tok // tm,), out_shape=jax.ShapeDtypeStruct((t_tok, GM_WIDTH), BF16),
        in_specs=[row, row, _full((1, GM_WIDTH)), _full((1, GM_WIDTH)), _full(wcat.shape), _full(bias.shape),
                  _full(avg.shape)],
        out_specs=row, compiler_params=_params("parallel"))(u, v, lnw, lnb, wcat, bias, avg)


def _shift_rows(x, edge, j, down):
    groups, cols = x.shape[0] // 8, x.shape[1]
    amount = j if down else 8 - j
    rot = pltpu.roll(x.reshape(groups, 8, cols), amount, axis=1)
    edge_rot = pltpu.roll(edge, amount, axis=0)[None]
    sub = lax.broadcasted_iota(jnp.int32, (1, 8, 1), 1)
    if down:
        out = jnp.where(sub < j, jnp.concatenate([edge_rot, rot[:-1]], axis=0), rot)
    else:
        out = jnp.where(sub < 8 - j, rot, jnp.concatenate([rot[1:], edge_rot], axis=0))
    return out.reshape(x.shape)


def _conv_pre(xbc, tail, cw_ref, cb):
    taps = [_shift_rows(xbc, tail, 3 - k, True) for k in range(3)] + [xbc]
    return cb + cw_ref[0:1, :] * taps[0] + cw_ref[1:2, :] * taps[1] + cw_ref[2:3, :] * taps[2] + cw_ref[3:4, :] * taps[3]


def _ssd_common(pre, dtr, dtb, alog, expand, tril):
    q = CHUNK
    sg = jax.nn.sigmoid(pre)
    act = pre * sg
    lane = lax.broadcasted_iota(jnp.int32, (1, CHUNK), 1)
    a_row = jnp.where(lane < N_HEADS, -jnp.exp(alog), 0.0)
    dtp = dtr + dtb
    dt = _softplus(dtp)
    a_cs = _split_dot_left(tril, dt * a_row, 3)
    a_cs_t = a_cs.T
    dt_exp = _split_dot(dt, expand, 3)
    a_exp = _split_dot(a_cs, expand, 3)
    a_end = a_exp[q - 1:q, :]
    li = lax.broadcasted_iota(jnp.int32, (q, q), 0)
    si = lax.broadcasted_iota(jnp.int32, (q, q), 1)
    causal = si <= li
    decay = []
    for h in range(N_HEADS):
        seg = a_cs[:, h:h + 1] - a_cs_t[h:h + 1, :]
        decay.append(jnp.where(causal, jnp.exp(jnp.minimum(seg, 0.0)), 0.0))
    return dict(pre=pre, sg=sg, act=act, a_row=a_row, dtp=dtp, dt=dt, dt_exp=dt_exp, a_exp=a_exp,
                e=jnp.exp(a_exp), w_end=jnp.exp(a_end - a_exp), cd=jnp.exp(a_end), decay=decay)


def _ssd_specs(t_tok, seq, reverse):
    nb, nc = t_tok // seq, seq // CHUNK

    def chunk(c):
        return nc - 1 - c if reverse else c

    def row(n, col=0):
        return pl.BlockSpec((nb, CHUNK, n), lambda c: (0, chunk(c), col))

    tail = pl.BlockSpec((nb, 8, CONV_CH), lambda c: (0, jnp.maximum(chunk(c) * (CHUNK // 8) - 1, 0), 0))
    states = pl.BlockSpec((nb, 1, N_STATE, SSM_WIDTH), lambda c: (0, chunk(c), 0, 0))
    fold = lambda a: a.reshape(nb, seq, a.shape[-1])
    unfold = lambda a: a.reshape(t_tok, a.shape[-1])
    return nb, nc, row, tail, states, fold, unfold


def _ssd_fwd(z, xbc, dtr, cw, cb, dtb, alog, dskip_exp, nw, expand, tril, seq, dep=None):
    t_tok = z.shape[0]
    nb, nc, row, tail, states_spec, fold, unfold = _ssd_specs(t_tok, seq, False)

    def body(z_ref, xbc_ref, tail_ref, dtr_ref, cw_ref, cb_ref, dtb_ref, alog_ref, dsk_ref, nw_ref, exp_ref,
             tril_ref, o_ref, y_ref, st_ref, pre_ref, state_ref):
        c = pl.program_id(0)

        @pl.when(c == 0)
        def _():
            state_ref[...] = jnp.zeros_like(state_ref)

        m_l, m_r = _lane_masks()
        for s in range(nb):
            pre = _conv_pre(xbc_ref[s], jnp.where(c == 0, 0.0, tail_ref[s]), cw_ref, cb_ref[...])
            pre_ref[s] = pre
            f = _ssd_common(pre, dtr_ref[s], dtb_ref[...], alog_ref[...], exp_ref[...], tril_ref[...])
            act = f["act"]
            xs = act[:, :SSM_WIDTH]
            xdt = xs * f["dt_exp"]
            xw = xdt * f["w_end"]
            state = state_ref[s]
            st_ref[s, 0] = state
            ydiag, yoff, snew = [], [], []
            for g in range(2):
                bg = act[:, 512 + 128 * g:640 + 128 * g].astype(BF16)
                cg = act[:, 768 + 128 * g:896 + 128 * g].astype(BF16)
                cb_mat = _dot(cg, bg, _NT)
                for pr in range(2):
                    h0 = 4 * g + 2 * pr
                    gcat = jnp.concatenate(
                        [(cb_mat * f["decay"][h0]).astype(BF16), (cb_mat * f["decay"][h0 + 1]).astype(BF16)], axis=1)
                    ydiag.append(_dot(gcat, _stack_pair(xdt[:, 64 * h0:64 * h0 + 128], m_l, m_r)))
                yoff.append(_dot(cg, state[:, 256 * g:256 * (g + 1)].astype(BF16)))
                snew.append(_dot(bg, xw[:, 256 * g:256 * (g + 1)].astype(BF16), _TN))
            y = jnp.concatenate(ydiag, axis=1) + f["e"] * jnp.concatenate(yoff, axis=1) + dsk_ref[...] * xs
            state_ref[s] = state * f["cd"] + jnp.concatenate(snew, axis=1)
            y_ref[s] = y
            zv = z_ref[s].astype(F32)
            yg = y * (zv * jax.nn.sigmoid(zv))
            outs = []
            for g in range(2):
                ygg = yg[:, 256 * g:256 * (g + 1)]
                outs.append(ygg * lax.rsqrt(jnp.mean(ygg * ygg, axis=-1, keepdims=True) + EPS))
            o_ref[s] = (jnp.concatenate(outs, axis=1) * nw_ref[...]).astype(BF16)

    consts = [cw, cb, dtb, alog, dskip_exp, nw, expand, tril]
    deps = [] if dep is None else [dep]
    n_in = 4 + len(consts)

    def body_skipping_dep(*refs):
        body(*refs[:n_in], *refs[n_in + len(deps):])

    sd = lambda n, dt: jax.ShapeDtypeStruct((nb, seq, n), dt)
    o, y, states, pre = pl.pallas_call(
        body_skipping_dep, name="ssd_fwd", grid=(nc,),
        out_shape=(sd(SSM_WIDTH, BF16), sd(SSM_WIDTH, F32), jax.ShapeDtypeStruct((nb, nc, N_STATE, SSM_WIDTH), F32),
                   sd(CONV_CH, F32)),
        in_specs=[row(SSM_WIDTH), row(CONV_CH), tail, row(CHUNK)] + [_full(a.shape) for a in consts]
        + [pl.BlockSpec(memory_space=pl.ANY)] * len(deps),
        out_specs=(row(SSM_WIDTH), row(SSM_WIDTH), states_spec, row(CONV_CH)),
        scratch_shapes=[pltpu.VMEM((nb, N_STATE, SSM_WIDTH), F32)],
        compiler_params=_params("arbitrary"))(fold(z), fold(xbc), fold(xbc), fold(dtr), *consts, *deps)
    return unfold(o), unfold(y), states, unfold(pre)


def _out_proj(mix_a, mix_b, w_out, x, g2, g3, tm, dep=None):
    t_tok = x.shape[0]
    deps = [] if dep is None else [dep]

    def body(a_ref, b_ref, w_ref, x_ref, g2_ref, g3_ref, *rest):
        o_ref, x2_ref, h3_ref = rest[-3:]
        o = _dot(a_ref[...], w_ref[0:GM_WIDTH, :]) + _dot(b_ref[...], w_ref[GM_WIDTH:, :])
        o_ref[...] = o
        r2 = lax.rsqrt(jnp.mean(o * o, axis=-1, keepdims=True) + EPS)
        x2 = x_ref[...] + o * r2 * g2_ref[...]
        x2_ref[...] = x2
        r3 = lax.rsqrt(jnp.mean(x2 * x2, axis=-1, keepdims=True) + EPS)
        h3_ref[...] = (x2 * r3 * g3_ref[...]).astype(BF16)

    row = lambda n: pl.BlockSpec((tm, n), lambda i: (i, 0))
    sd = lambda dt: jax.ShapeDtypeStruct((t_tok, D_MODEL), dt)
    return pl.pallas_call(
        body, name="out_proj", grid=(t_tok // tm,), out_shape=(sd(F32), sd(F32), sd(BF16)),
        in_specs=[row(GM_WIDTH), row(SSM_WIDTH), _full((D_MODEL, D_MODEL)), row(D_MODEL), _full((1, D_MODEL)),
                  _full((1, D_MODEL))] + [pl.BlockSpec(memory_space=pl.ANY)] * len(deps),
        out_specs=(row(D_MODEL),) * 3, compiler_params=_params("parallel"))(mix_a, mix_b, w_out, x, g2, g3, *deps)


def _mlp_fwd(h3, w_up, w_down, x2, target, g4, tm, tf):
    t_tok = x2.shape[0]

    def up_body(h_ref, wu_ref, ra_ref):
        ra_ref[...] = jnp.maximum(_dot(h_ref[...], wu_ref[...]), 0.0).astype(BF16)

    tu = min(2 * tm, t_tok)
    ra = pl.pallas_call(
        up_body, name="mlp_up", grid=(D_FF // tf, t_tok // tu), out_shape=jax.ShapeDtypeStruct((t_tok, D_FF), BF16),
        in_specs=[pl.BlockSpec((tu, D_MODEL), lambda j, i: (i, 0)), pl.BlockSpec((D_MODEL, tf), lambda j, i: (0, j))],
        out_specs=pl.BlockSpec((tu, tf), lambda j, i: (i, j)), compiler_params=_params("parallel", "parallel"))(h3, w_up)

    def down_body(ra_ref, wd_ref, x2_ref, t_ref, g4_ref, dd_ref, dy_ref, dg4_ref, loss_ref):
        i = pl.program_id(0)
        rav = ra_ref[...]
        dvec = _dot(rav * rav, wd_ref[...])
        r4 = lax.rsqrt(jnp.mean(dvec * dvec, axis=-1, keepdims=True) + EPS)
        dn = dvec * r4
        g4 = g4_ref[...]
        err = x2_ref[...] + dn * g4 - t_ref[...]
        dy = err * (1.0 / D_MODEL)
        dy_ref[...] = dy
        dg = dy * g4
        dd_ref[...] = (r4 * (dg - dn * jnp.mean(dg * dn, axis=-1, keepdims=True))).astype(BF16)
        _acc_rows(dg4_ref, _rsum(dy * dn), i == 0)
        tile_loss = 0.5 * jnp.sum(jnp.sum(err * err, axis=-1, keepdims=True), axis=0, keepdims=True) / D_MODEL
        _acc_rows(loss_ref, jnp.broadcast_to(tile_loss, (1, 128)), i == 0)

    row = pl.BlockSpec((tm, D_MODEL), lambda i: (i, 0))
    dd, dy, dg4, loss = pl.pallas_call(
        down_body, name="mlp_down", grid=(t_tok // tm,),
        out_shape=(jax.ShapeDtypeStruct((t_tok, D_MODEL), BF16), jax.ShapeDtypeStruct((t_tok, D_MODEL), F32),
                   jax.ShapeDtypeStruct((1, D_MODEL), F32), jax.ShapeDtypeStruct((1, 128), F32)),
        in_specs=[pl.BlockSpec((tm, D_FF), lambda i: (i, 0)), _full((D_FF, D_MODEL)), row, row, _full((1, D_MODEL))],
        out_specs=(row, row, _full((1, D_MODEL)), _full((1, 128))),
        compiler_params=_params("arbitrary"))(ra, w_down, x2, target, g4)
    return ra, dd, dy, dg4, loss


def _mlp_bwd(dd, w_down, ra, w_up, x2, dy, o, g3, g2, tm, tf):
    t_tok = x2.shape[0]

    def hidden_body(dd_ref, wd_ref, ra_ref, da_ref):
        df = _dot(dd_ref[...], wd_ref[...], _NT)
        da_ref[...] = (df * (2.0 * ra_ref[...].astype(F32))).astype(BF16)

    tu = min(2 * tm, t_tok)
    da = pl.pallas_call(
        hidden_body, name="mlp_bwd_hidden", grid=(D_FF // tf, t_tok // tu),
        out_shape=jax.ShapeDtypeStruct((t_tok, D_FF), BF16),
        in_specs=[pl.BlockSpec((tu, D_MODEL), lambda j, i: (i, 0)), pl.BlockSpec((tf, D_MODEL), lambda j, i: (j, 0)),
                  pl.BlockSpec((tu, tf), lambda j, i: (i, j))],
        out_specs=pl.BlockSpec((tu, tf), lambda j, i: (i, j)),
        compiler_params=_params("parallel", "parallel"))(dd, w_down, ra)

    def in_body(da_ref, wu_ref, x2_ref, dy_ref, o_ref, g3_ref, g2_ref, dx2_ref, do_ref, dg3_ref, dg2_ref):
        i = pl.program_id(0)
        dh3 = _dot(da_ref[...], wu_ref[...], _NT)
        dn3, dg3 = _rms_bwd(x2_ref[...], g3_ref[...], dh3)
        dx2 = dy_ref[...] + dn3
        dx2_ref[...] = dx2
        do, dg2 = _rms_bwd(o_ref[...], g2_ref[...], dx2)
        do_ref[...] = do.astype(BF16)
        _acc_rows(dg3_ref, dg3, i == 0)
        _acc_rows(dg2_ref, dg2, i == 0)

    row = pl.BlockSpec((tm, D_MODEL), lambda i: (i, 0))
    vec = _full((1, D_MODEL))
    sd = lambda dt: jax.ShapeDtypeStruct((t_tok, D_MODEL), dt)
    dx2, do, dg3, dg2 = pl.pallas_call(
        in_body, name="mlp_bwd_in", grid=(t_tok // tm,),
        out_shape=(sd(F32), sd(BF16), jax.ShapeDtypeStruct((1, D_MODEL), F32), jax.ShapeDtypeStruct((1, D_MODEL), F32)),
        in_specs=[pl.BlockSpec((tm, D_FF), lambda i: (i, 0)), _full((D_MODEL, D_FF)), row, row, row, vec, vec],
        out_specs=(row, row, vec, vec), compiler_params=_params("arbitrary"))(da, w_up, x2, dy, o, g3, g2)
    return da, dx2, do, dg3, dg2


def _wgrad(a, b, out_blocks, bm, bn, bk, square_a, name, dep=None):
    t_tok, m = a.shape
    n = b.shape[1]
    nk = t_tok // bk

    def body(a_ref, b_ref, *rest):
        o_ref, acc_ref = rest[-2:]
        k = pl.program_id(2)
        av = a_ref[...]
        if square_a:
            av = av * av
        part = _dot(av, b_ref[...], _TN)

        def emit(res):
            if out_blocks is None:
                o_ref[...] = res.astype(BF16)
            else:
                o_ref[0] = res.astype(BF16)

        if nk == 1:
            emit(part)
            return

        @pl.when(k == 0)
        def _():
            acc_ref[...] = part

        @pl.when(k > 0)
        def _():
            acc_ref[...] += part

        @pl.when(k == nk - 1)
        def _():
            emit(acc_ref[...])

    if out_blocks is None:
        out_shape = jax.ShapeDtypeStruct((m, n), BF16)
        out_spec = pl.BlockSpec((bm, bn), lambda i, j, k: (i, j))
    else:
        assert n // out_blocks == bn
        out_shape = jax.ShapeDtypeStruct((out_blocks, m, bn), BF16)
        out_spec = pl.BlockSpec((1, bm, bn), lambda i, j, k: (j, i, 0))
    deps = [] if dep is None else [dep]
    return pl.pallas_call(
        body, name=name, grid=(m // bm, n // bn, nk), out_shape=out_shape,
        in_specs=[pl.BlockSpec((bk, bm), lambda i, j, k: (k, i)), pl.BlockSpec((bk, bn), lambda i, j, k: (k, j))]
        + [pl.BlockSpec(memory_space=pl.ANY)] * len(deps),
        out_specs=out_spec, scratch_shapes=[pltpu.VMEM((bm, bn) if nk > 1 else (8, 128), F32)],
        compiler_params=_params("parallel", "parallel", "arbitrary"))(a, b, *deps)


def _wgrad_in_chunked(h1, pieces, bn, bk, dep=None):
    t_tok = h1.shape[0]
    nk = t_tok // bk
    widths = [b - a for a, b in _IN_SPLITS]

    def body(h_ref, *rest):
        piece_refs = rest[:len(widths)]
        o_ref, acc_ref = rest[-2:]
        k = pl.program_id(1)
        hv = h_ref[...]
        for (a, b), r in zip(_IN_SPLITS, piece_refs):
            part = _dot(r[...], hv, _TN)

            @pl.when(k == 0)
            def _():
                acc_ref[a:b, :] = part

            @pl.when(k > 0)
            def _():
                acc_ref[a:b, :] += part

        @pl.when(k == nk - 1)
        def _():
            o_ref[...] = acc_ref[...].astype(BF16)

    deps = [] if dep is None else [dep]
    return pl.pallas_call(
        body, name="wgrad_in", grid=(D_MODEL // bn, nk), out_shape=jax.ShapeDtypeStruct((IN_PAD, D_MODEL), BF16),
        in_specs=[pl.BlockSpec((bk, bn), lambda j, k: (k, j))] + [pl.BlockSpec((bk, n), lambda j, k: (k, 0)) for n in widths]
        + [pl.BlockSpec(memory_space=pl.ANY)] * len(deps),
        out_specs=pl.BlockSpec((IN_PAD, bn), lambda j, k: (0, j)), scratch_shapes=[pltpu.VMEM((IN_PAD, bn), F32)],
        compiler_params=_params("parallel", "arbitrary"))(h1, *pieces, *deps)


def _wgrad_pieces(h1, pieces, bn, name, dep=None):
    t_tok = h1.shape[0]
    widths = [p.shape[1] for p in pieces]
    starts = [sum(widths[:i]) for i in range(len(widths))]

    def body(h_ref, *rest):
        piece_refs = rest[:len(widths)]
        o_ref = rest[-1]
        hv = h_ref[...]
        for a, n, r in zip(starts, widths, piece_refs):
            o_ref[a:a + n, :] = _dot(r[...], hv, _TN).astype(BF16)

    deps = [] if dep is None else [dep]
    return pl.pallas_call(
        body, name=name, grid=(D_MODEL // bn,), out_shape=jax.ShapeDtypeStruct((sum(widths), D_MODEL), BF16),
        in_specs=[pl.BlockSpec((t_tok, bn), lambda j: (0, j))] + [pl.BlockSpec((t_tok, n), lambda j: (0, 0)) for n in widths]
        + [pl.BlockSpec(memory_space=pl.ANY)] * len(deps),
        out_specs=pl.BlockSpec((sum(widths), bn), lambda j: (0, j)),
        compiler_params=_params("parallel"))(h1, *pieces, *deps)


def _dmix(do, w_out, tm, dep=None):
    t_tok = do.shape[0]

    def body(d_ref, w_ref, *rest):
        rest[-1][...] = _dot(d_ref[...], w_ref[...], _NT).astype(BF16)

    row = pl.BlockSpec((tm, D_MODEL), lambda i: (i, 0))
    deps = [] if dep is None else [dep]
    return pl.pallas_call(
        body, name="dmix", grid=(t_tok // tm,), out_shape=jax.ShapeDtypeStruct((t_tok, D_MODEL), BF16),
        in_specs=[row, _full((D_MODEL, D_MODEL))] + [pl.BlockSpec(memory_space=pl.ANY)] * len(deps), out_specs=row,
        compiler_params=_params("parallel"))(do, w_out, *deps)


def _gmlp_bwd(dmix, u, v, lnw, lnb, wcat, wtcat, bias, avg, expand_t):
    t_tok = u.shape[0]
    tm = min(_GMLP_ROWS, t_tok)

    def body(dm_ref, u_ref, v_ref, lnw_ref, lnb_ref, wcat_ref, wtcat_ref, bias_ref, avg_ref, expt_ref, du_ref, dv_ref,
             dw_ref, db_ref, dlnw_ref, dlnb_ref):
        i = pl.program_id(0)
        m_l, m_r = _lane_masks()
        avg = avg_ref[...]
        lnw = lnw_ref[...]
        ug, dug, dvg, rstd, vhat, vn, mixed = _gmlp_common(
            u_ref[...].astype(F32), v_ref[...].astype(F32), lnw, lnb_ref[...], avg, wcat_ref, bias_ref[...], m_l, m_r)
        dya = dm_ref[...].astype(F32)
        du_ref[...] = (dya * mixed * dug).astype(BF16)
        dmixed = dya * ug
        dvn_rows, dws, dbt = [], [None] * N_HEADS, None
        for r in range(tm // CHUNK):
            dvn_cols = []
            for j in range(N_HEADS // 2):
                dmp = dmixed[CHUNK * r:CHUNK * (r + 1), 128 * j:128 * (j + 1)]
                dvn_cols.append(_dot(wtcat_ref[j], _stack_pair(dmp, m_l, m_r)))
                vnp = vn[CHUNK * r:CHUNK * (r + 1), 128 * j:128 * (j + 1)].astype(BF16)
                for i_h, mask in enumerate((m_l, m_r)):
                    part = _dot((dmp * mask).astype(BF16), vnp, _NT)
                    dws[2 * j + i_h] = part if r == 0 else dws[2 * j + i_h] + part
            dvn_rows.append(jnp.concatenate(dvn_cols, axis=1))
            part = _split_dot(dmixed[CHUNK * r:CHUNK * (r + 1), :], expt_ref[...], 2)
            dbt = part if r == 0 else dbt + part
        dvn = jnp.concatenate(dvn_rows, axis=0)
        dvh = dvn * lnw
        dvgel = rstd * (dvh - _head_mean(dvh, avg) - vhat * _head_mean(dvh * vhat, avg))
        dv_ref[...] = (dvgel * dvg).astype(BF16)
        first = i == 0

        @pl.when(first)
        def _():
            for h in range(N_HEADS):
                dw_ref[h] = dws[h]
            db_ref[...] = dbt

        @pl.when(jnp.logical_not(first))
        def _():
            for h in range(N_HEADS):
                dw_ref[h] += dws[h]
            db_ref[...] += dbt

        _acc_rows(dlnw_ref, _rsum(dvn * vhat), first)
        _acc_rows(dlnb_ref, _rsum(dvn), first)

    row = pl.BlockSpec((tm, GM_WIDTH), lambda i: (i, 0))
    consts = [lnw, lnb, wcat, wtcat, bias, avg, expand_t]
    return pl.pallas_call(
        body, name="gmlp_bwd", grid=(t_tok // tm,),
        out_shape=(jax.ShapeDtypeStruct((t_tok, GM_WIDTH), BF16), jax.ShapeDtypeStruct((t_tok, GM_WIDTH), BF16),
                   jax.ShapeDtypeStruct((N_HEADS, CHUNK, CHUNK), F32), jax.ShapeDtypeStruct((CHUNK, CHUNK), F32),
                   jax.ShapeDtypeStruct((1, GM_WIDTH), F32), jax.ShapeDtypeStruct((1, GM_WIDTH), F32)),
        in_specs=[row, row, row] + [_full(a.shape) for a in consts],
        out_specs=(row, row, _full((N_HEADS, CHUNK, CHUNK)), _full((CHUNK, CHUNK)), _full((1, GM_WIDTH)),
                   _full((1, GM_WIDTH))),
        compiler_params=_params("arbitrary"))(dmix, u, v, *consts)


def _ssd_bwd(dmix, z, xbc, pre, dtr, y, states, cw, cb, dtb, alog, dskip_exp, nw, expand, expand_t, tril, triu, seq,
             dep=None):
    t_tok = z.shape[0]
    nb, nc, row, _, states_spec, fold, unfold = _ssd_specs(t_tok, seq, True)
    q = CHUNK

    def one_sequence(s, dm_ref, z_ref, xbc_ref, pre_ref, dtr_ref, y_ref, st_ref, cw_ref, dtb_ref, alog_ref, dsk_ref,
                     nw_ref, exp_ref, expt_ref, tril_ref, triu_ref, dz_ref, dxbc_ref, ddt_ref, dhead_ref, dstate_ref):
        m_l, m_r = _lane_masks()
        expt = expt_ref[...]
        f = _ssd_common(pre_ref[s], dtr_ref[s], dtb_ref[...], alog_ref[...], exp_ref[...], tril_ref[...])
        act, pre, sg = f["act"], f["pre"], f["sg"]
        xs = act[:, :SSM_WIDTH]
        xdt = xs * f["dt_exp"]
        xw = xdt * f["w_end"]
        state = st_ref[s, 0]
        dstate = dstate_ref[s]
        zv, yv, dout, nw = z_ref[s].astype(F32), y_ref[s], dm_ref[s].astype(F32), nw_ref[...]
        sz = jax.nn.sigmoid(zv)
        sl = zv * sz
        yg = yv * sl
        tv = dout * nw
        dyg_parts, ygh_parts = [], []
        for g in range(2):
            ygg = yg[:, 256 * g:256 * (g + 1)]
            rr = lax.rsqrt(jnp.mean(ygg * ygg, axis=-1, keepdims=True) + EPS)
            ygh = ygg * rr
            tg = tv[:, 256 * g:256 * (g + 1)]
            dyg_parts.append(rr * (tg - ygh * jnp.mean(tg * ygh, axis=-1, keepdims=True)))
            ygh_parts.append(ygh)
        dyg = jnp.concatenate(dyg_parts, axis=1)
        dnw = _rsum(dout * jnp.concatenate(ygh_parts, axis=1))
        dy = dyg * sl
        dz_ref[s] = (dyg * yv * (sz * (1.0 + zv * (1.0 - sz)))).astype(BF16)
        ddsk = _rsum(dy * xs)
        dye = dy * f["e"]
        lane = lax.broadcasted_iota(jnp.int32, (q, q), 1)
        sub = lax.broadcasted_iota(jnp.int32, (q, q), 0)
        rs_mat = jnp.zeros((q, q), F32)
        cs_mat = jnp.zeros((q, q), F32)
        dxdt_cols, yoff, dst_in, dxw, d_b, d_c = [], [], [], [], [], []
        for g in range(2):
            bg = act[:, 512 + 128 * g:640 + 128 * g].astype(BF16)
            cg = act[:, 768 + 128 * g:896 + 128 * g].astype(BF16)
            cb_mat = _dot(cg, bg, _NT)
            stg = state[:, 256 * g:256 * (g + 1)].astype(BF16)
            dyeg = dye[:, 256 * g:256 * (g + 1)].astype(BF16)
            yoff.append(_dot(cg, stg))
            dcg = _dot(dyeg, stg, _NT)
            dst_in.append(_dot(cg, dyeg, _TN))
            dcb = jnp.zeros((q, q), F32)
            for pr in range(2):
                h0 = 4 * g + 2 * pr
                gf = [cb_mat * f["decay"][h0], cb_mat * f["decay"][h0 + 1]]
                gcat = jnp.concatenate([gf[0].astype(BF16), gf[1].astype(BF16)], axis=1)
                xst = _stack_pair(xdt[:, 64 * h0:64 * h0 + 128], m_l, m_r)
                dyp = dy[:, 64 * h0:64 * h0 + 128].astype(BF16)
                dgcat = _dot(dyp, xst, _NT)
                dxst = _dot(gcat, dyp, _TN)
                dxdt_cols.append(dxst[:q] * m_l + dxst[q:] * m_r)
                for i in range(2):
                    h = h0 + i
                    dg = dgcat[:, q * i:q * (i + 1)]
                    mm = dg * gf[i]
                    rs_mat = rs_mat + jnp.where(lane == h, jnp.sum(mm, axis=1, keepdims=True), 0.0)
                    cs_mat = cs_mat + jnp.where(sub == h, jnp.sum(mm, axis=0, keepdims=True), 0.0)
                    dcb = dcb + dg * f["decay"][h]
            dcb16 = dcb.astype(BF16)
            dstg = dstate[:, 256 * g:256 * (g + 1)].astype(BF16)
            d_c.append(dcg + _dot(dcb16, bg))
            dxw.append(_dot(bg, dstg))
            d_b.append(_dot(dcb16, cg, _TN) + _dot(xw[:, 256 * g:256 * (g + 1)].astype(BF16), dstg, _NT))
        dxw = jnp.concatenate(dxw, axis=1)
        dxdt = jnp.concatenate(dxdt_cols, axis=1) + dxw * f["w_end"]
        qv = dxw * xw
        end_row = _rsum(qv) + _rsum(dstate * state) * f["cd"]
        x2 = dye * jnp.concatenate(yoff, axis=1) - qv
        row_i = lax.broadcasted_iota(jnp.int32, (q, 1), 0)
        x2 = x2 + jnp.where(row_i == q - 1, end_row, 0.0)
        da_cs = _split_dot(x2, expt, 2) + rs_mat - cs_mat.T
        ddt = _split_dot(dxdt * xs, expt, 2)
        dxs = dsk_ref[...] * dy + dxdt * f["dt_exp"]
        dda = _split_dot_left(triu_ref[...], da_cs, 3)
        ddt = ddt + dda * f["a_row"]
        dalog = _rsum(dda * f["dt"]) * f["a_row"]
        draw = ddt * jax.nn.sigmoid(f["dtp"])
        ddt_ref[s] = draw.astype(BF16)
        dact = jnp.concatenate([dxs] + d_b + d_c, axis=1)
        dpre = dact * (sg * (1.0 + pre * (1.0 - sg)))
        dhead = dhead_ref[s]
        xv = xbc_ref[s]
        shifted = [_shift_rows(dpre, dhead, 3 - k, False) for k in range(3)] + [dpre]
        dxbc = cw_ref[3:4, :] * dpre
        for k in range(3):
            dxbc = dxbc + cw_ref[k:k + 1, :] * shifted[k]
        dxbc_ref[s] = dxbc.astype(BF16)
        dhead_ref[s] = dpre[0:8, :]
        dstate_ref[s] = dstate * f["cd"] + jnp.concatenate(dst_in, axis=1)
        row8 = lax.broadcasted_iota(jnp.int32, (8, 1), 0)
        dcw = jnp.zeros((8, CONV_CH), F32)
        for k in range(4):
            dcw = dcw + jnp.where(row8 == k, _rsum(shifted[k] * xv), 0.0)
        return dcw, _rsum(dpre), _rsum(draw), dalog, _split_dot(ddsk, expt, 3), dnw

    def body(dm_ref, z_ref, xbc_ref, pre_ref, dtr_ref, y_ref, st_ref, cw_ref, cb_ref, dtb_ref, alog_ref, dsk_ref,
             nw_ref, exp_ref, expt_ref, tril_ref, triu_ref, dz_ref, dxbc_ref, ddt_ref, dcw_ref, dcb_ref, ddtb_ref,
             dalog_ref, dd_ref, dnw_ref, dhead_ref, dstate_ref):
        c = pl.program_id(0)
        first = c == 0

        @pl.when(first)
        def _():
            dstate_ref[...] = jnp.zeros_like(dstate_ref)
            dhead_ref[...] = jnp.zeros_like(dhead_ref)

        total = None
        for s in range(nb):
            parts = one_sequence(s, dm_ref, z_ref, xbc_ref, pre_ref, dtr_ref, y_ref, st_ref, cw_ref, dtb_ref, alog_ref,
                                 dsk_ref, nw_ref, exp_ref, expt_ref, tril_ref, triu_ref, dz_ref, dxbc_ref, ddt_ref,
                                 dhead_ref, dstate_ref)
            total = parts if total is None else tuple(a + b for a, b in zip(total, parts))
        dcw = total[0]

        @pl.when(first)
        def _():
            dcw_ref[...] = dcw

        @pl.when(jnp.logical_not(first))
        def _():
            dcw_ref[...] += dcw

        for ref, part in zip((dcb_ref, ddtb_ref, dalog_ref, dd_ref, dnw_ref), total[1:]):
            _acc_rows(ref, part, first)

    consts = [cw, cb, dtb, alog, dskip_exp, nw, expand, expand_t, tril, triu]
    deps = [] if dep is None else [dep]
    n_in = 7 + len(consts)

    def body_skipping_dep(*refs):
        body(*refs[:n_in], *refs[n_in + len(deps):])

    acc = lambda n: jax.ShapeDtypeStruct((1, n), F32)
    sd = lambda n: jax.ShapeDtypeStruct((nb, seq, n), BF16)
    dz, dxbc, ddt, *small_grads = pl.pallas_call(
        body_skipping_dep, name="ssd_bwd", grid=(nc,),
        out_shape=(sd(SSM_WIDTH), sd(CONV_CH), sd(CHUNK), jax.ShapeDtypeStruct((8, CONV_CH), F32), acc(CONV_CH),
                   acc(CHUNK), acc(CHUNK), acc(CHUNK), acc(SSM_WIDTH)),
        in_specs=[row(SSM_WIDTH, col=1), row(SSM_WIDTH), row(CONV_CH), row(CONV_CH), row(CHUNK), row(SSM_WIDTH),
                  states_spec]
        + [_full(a.shape) for a in consts] + [pl.BlockSpec(memory_space=pl.ANY)] * len(deps),
        out_specs=(row(SSM_WIDTH), row(CONV_CH), row(CHUNK), _full((8, CONV_CH)), _full((1, CONV_CH)),
                   _full((1, CHUNK)), _full((1, CHUNK)), _full((1, CHUNK)), _full((1, SSM_WIDTH))),
        scratch_shapes=[pltpu.VMEM((nb, 8, CONV_CH), F32), pltpu.VMEM((nb, N_STATE, SSM_WIDTH), F32)],
        compiler_params=_params("arbitrary"))(
            fold(dmix), fold(z), fold(xbc), fold(pre), fold(dtr), fold(y), states, *consts, *deps)
    return (unfold(dz), unfold(dxbc), unfold(ddt), *small_grads)


def _in_bwd(du, dv, dz, dxbc, ddt, w_in, x, dx2, g1, tm, me, riders=(), dep=None):
    t_tok = x.shape[0]
    steps = t_tok // tm

    n_in = [5 + ("mask" in rd) for rd in riders]
    first_in = [sum(n_in[:r]) for r in range(len(riders))]

    def body(me_ref, du_ref, dv_ref, dz_ref, dxbc_ref, ddt_ref, w_ref, x_ref, dx2_ref, g_ref, *rest):
        outs = rest[len(rest) - 2 - 4 * len(riders):]
        gx_ref, dg_ref = outs[:2]
        i = pl.program_id(0)
        dh = None
        for (a, b), ref in zip(_IN_SPLITS, (du_ref, dv_ref, dz_ref, dxbc_ref, ddt_ref)):
            part = _dot(ref[...], w_ref[a:b, :])
            dh = part if dh is None else dh + part
        dn, dg = _rms_bwd(x_ref[...], g_ref[...], dh)
        gx_ref[...] = dx2_ref[...] + dn
        _acc_rows(dg_ref, dg, i == 0)
        for r in range(len(riders)):
            p_ref, own_ref, w_ref_r, m_ref_r, v_ref_r = rest[first_in[r]:first_in[r] + 5]
            g = _sum_parts(me_ref[0], p_ref, own_ref[0])
            if n_in[r] == 6:
                g = g * rest[first_in[r] + 5][...]
            d, mn, vn = _adamw_math(w_ref_r[...], g, m_ref_r[...], v_ref_r[...])
            for o_ref, val in zip(outs[2 + 4 * r:6 + 4 * r], (g, d, mn, vn)):
                o_ref[...] = val

    row = lambda n: pl.BlockSpec((tm, n), lambda i, me_ref: (i, 0))
    whole = lambda shape: pl.BlockSpec(shape, lambda i, me_ref: (0,) * len(shape))
    widths = [b - a for a, b in _IN_SPLITS]
    deps = [] if dep is None else [dep]
    rider_args, rider_specs, rider_out_shapes, rider_out_specs = [], [], [], []
    for rd in riders:
        rows, cols = rd["w"].shape[0] // steps, rd["w"].shape[1]
        blk = pl.BlockSpec((rows, cols), lambda i, me_ref: (i, 0))
        rider_args += [rd["parts"], rd["own"], rd["w"], rd["m"], rd["v"]]
        rider_specs += [pl.BlockSpec((N_DEV, rows, cols), lambda i, me_ref: (0, i, 0)),
                        pl.BlockSpec((1, rows, cols), lambda i, me_ref: (me_ref[0], i, 0)), blk, blk, blk]
        if "mask" in rd:
            rider_args.append(rd["mask"])
            rider_specs.append(whole((rows, cols)))
        rider_out_shapes += [jax.ShapeDtypeStruct(rd["w"].shape, F32)] * 4
        rider_out_specs += [blk] * 4
    outs = pl.pallas_call(
        body, name="in_bwd",
        out_shape=(jax.ShapeDtypeStruct((t_tok, D_MODEL), F32), jax.ShapeDtypeStruct((1, D_MODEL), F32),
                   *rider_out_shapes),
        grid_spec=pltpu.PrefetchScalarGridSpec(
            num_scalar_prefetch=1, grid=(steps,),
            in_specs=[row(n) for n in widths] + [whole((IN_PAD, D_MODEL)), row(D_MODEL), row(D_MODEL),
                                                 whole((1, D_MODEL))] + rider_specs
            + [pl.BlockSpec(memory_space=pl.ANY)] * len(deps),
            out_specs=(row(D_MODEL), whole((1, D_MODEL)), *rider_out_specs)),
        compiler_params=_params("arbitrary"))(me, du, dv, dz, dxbc, ddt, w_in, x, dx2, g1, *rider_args, *deps)
    return outs[0], outs[1], [tuple(outs[2 + 4 * r:6 + 4 * r]) for r in range(len(riders))]


def _pad_lanes(a, n):
    return jnp.pad(a, ((0, 0), (0, n - a.shape[1])))


def _local_step(x, target, seq, small, hooks, first_dep=None):
    t_tok = x.shape[0]
    tm = min(TOKEN_TILE, t_tok)
    avg, expand, expand_t, tril, triu = _const_mats()
    g1, g2, g3, g4 = (small[k].reshape(1, D_MODEL) for k in
                      ("norm_mix_pre", "norm_mix_post", "norm_ffn_pre", "norm_ffn_post"))
    tie = (lambda a: a) if first_dep is None else (lambda a: a + first_dep[0, 0])
    lnw = tie(small["gm_ln_w"]).reshape(1, GM_WIDTH)
    lnb = tie(small["gm_ln_b"]).reshape(1, GM_WIDTH)
    causal = jnp.tril(jnp.ones((CHUNK, CHUNK), F32))
    wm = tie(small["gm_w_s"]) * causal
    pair = lambda w: w.reshape(4, 2, CHUNK, CHUNK).transpose(0, 2, 1, 3).reshape(4, CHUNK, 2 * CHUNK).astype(BF16)
    wcat = pair(wm)
    wtcat = pair(jnp.swapaxes(wm, 1, 2))
    bias = jnp.repeat(tie(small["gm_b_s"]).T, HEAD_DIM, axis=1)
    cb = small["conv_b"].reshape(1, CONV_CH)
    dtb = _pad_lanes(tie(small["dt_bias"]).reshape(1, N_HEADS), CHUNK)
    alog = _pad_lanes(tie(small["a_log"]).reshape(1, N_HEADS), CHUNK)
    dskip_exp = jnp.repeat(tie(small["d_skip"]).reshape(1, N_HEADS), HEAD_DIM, axis=1)
    nw = small["ssm_norm_w"].reshape(1, SSM_WIDTH)

    h1 = _prenorm(x, g1, tm, hooks.get("prenorm_after", first_dep))
    w_in_t, conv_w = hooks["mixer_weights"](h1)
    tall = min(2 * tm, t_tok)
    u, v, z, xbc, dtr = _in_proj(h1, w_in_t, tall)
    mix_a = _gmlp_fwd(u, v, lnw, lnb, wcat, bias, avg)
    dep = hooks["gmlp_done"](mix_a) if "gmlp_done" in hooks else None
    mix_b, y_pre, states, pre = _ssd_fwd(z, xbc, dtr, conv_w, cb, dtb, alog, dskip_exp, nw, expand, tril, seq, dep)
    w_out, dep = hooks["mixers_done"](mix_b)
    o, x2, h3 = _out_proj(mix_a, mix_b, w_out, x, g2, g3, tall, dep)
    w_up, w_down = hooks["mlp_weights"](h3)
    tf = FF_TILE
    ra, dd, dy, dg4, loss = _mlp_fwd(h3, w_up, w_down, x2, target, g4, tm, tf)

    da, dx2, do, dg3, dg2 = _mlp_bwd(dd, w_down, ra, w_up, x2, dy, o, g3, g2, tm, tf)
    g_w_down = _wgrad(ra, dd, None, WGRAD_TILE, D_MODEL, t_tok, True, "wgrad_down")
    g_w_up = _wgrad(h3, da, N_DEV, D_MODEL, D_FF // N_DEV, t_tok, False, "wgrad_up")
    dep = hooks["mlp_grads"](g_w_down, g_w_up)
    dmix = _dmix(do, w_out, tall, dep)
    g_w_out = _wgrad_pieces(do, (mix_a, mix_b), WGRAD_TILE, "wgrad_out", dep)
    du, dv, dws, dbt, dlnw, dlnb = _gmlp_bwd(dmix, u, v, lnw, lnb, wcat, wtcat, bias, avg, expand_t)
    dep = hooks["gmlp_grads"](g_w_out, dws)
    dz, dxbc, ddt, dcw, dcb, ddtb, dalog, ddsk, dnw = _ssd_bwd(
        dmix, z, xbc, pre, dtr, y_pre, states, conv_w, cb, dtb, alog, dskip_exp, nw, expand, expand_t, tril, triu, seq,
        dep)
    g_w_in = _wgrad_in_chunked(h1, (du, dv, dz, dxbc, ddt), WGRAD_TILE, t_tok // 2, dep)
    dep = hooks["in_grads"](g_w_in, dcw[0:4])
    riders = hooks["arrived_updates"](dep) if "arrived_updates" in hooks else []
    me = hooks.get("me", jnp.zeros((1,), jnp.int32))
    grad_x, dg1, updates = _in_bwd(du, dv, dz, dxbc, ddt, w_in_t, x, dx2, g1, tm, me, riders, dep)

    grads = dict(
        updates=updates,
        w_in=g_w_in, w_out=g_w_out, w_up=g_w_up, w_down=g_w_down, conv_w=dcw[0:4],
        norm_mix_pre=dg1, norm_mix_post=dg2, norm_ffn_pre=dg3, norm_ffn_post=dg4, gm_ln_w=dlnw, gm_ln_b=dlnb,
        gm_w_s=dws, gm_b_s=dbt, conv_b=dcb, dt_bias=ddtb, a_log=dalog, d_skip=ddsk, ssm_norm_w=dnw)
    return loss[0, 0], grad_x, grads


_WEIGHTS = ("norm_mix_pre", "w_in", "gm_ln_w", "gm_ln_b", "gm_w_s", "gm_b_s", "conv_w", "conv_b", "dt_bias", "a_log",
            "d_skip", "ssm_norm_w", "w_out", "norm_mix_post", "norm_ffn_pre", "w_up", "w_down", "norm_ffn_post")
_SLAB_ROWS = (("norm_mix_pre", 1024), ("norm_mix_post", 1024), ("norm_ffn_pre", 1024), ("norm_ffn_post", 1024),
              ("conv_b", 1024), ("ssm_norm_w", 512), ("gm_ln_w", 512), ("gm_ln_b", 512), ("dt_bias", 8), ("a_log", 8),
              ("d_skip", 8))
_SLAB_LOSS_ROW = len(_SLAB_ROWS)
_SLAB_BS_ROW = 16
_SMALL_PARAMS = tuple(name for name, _ in _SLAB_ROWS) + ("gm_b_s",)
_LN_PARAMS = ("gm_ln_w", "gm_ln_b")


_SLAB_CONV_ROW = _SLAB_LOSS_ROW + 1


def _pack_slab(g, loss_part):
    rows = [_pad_lanes(g[name], D_MODEL) for name, _ in _SLAB_ROWS]
    rows.append(jnp.broadcast_to(loss_part, (1, D_MODEL)))
    rows.append(g["conv_w"])
    assert sum(r.shape[0] for r in rows) == _SLAB_BS_ROW
    rows.append(_pad_lanes(g["gm_b_s"].T[0:N_HEADS], D_MODEL))
    return jnp.concatenate(rows, axis=0)


def _adamw_slab(parts, me, w, m, v):
    names = _SMALL_PARAMS + ("conv_w",)
    shapes = [w[k].shape for k in names]
    unfold = np.zeros((GM_WIDTH, HEAD_DIM), np.float32)
    for h in range(N_HEADS):
        unfold[h * HEAD_DIM:(h + 1) * HEAD_DIM, :] = np.eye(HEAD_DIM)
    unfold = jnp.asarray(unfold, dtype=BF16)
    n = len(names)
    shard = CONV_CH // N_DEV

    def body(me_ref, p_ref, unfold_ref, *refs):
        w_refs, m_refs, v_refs = refs[:n], refs[n:2 * n], refs[2 * n:3 * n]
        outs = refs[3 * n:]
        g_all = p_ref[0]
        for j in range(1, N_DEV):
            g_all = g_all + p_ref[j]
        lane = lax.broadcasted_iota(jnp.int32, (N_HEADS, GM_WIDTH), 1)
        head = lax.broadcasted_iota(jnp.int32, (N_HEADS, GM_WIDTH), 0)
        own_lanes = jnp.logical_and(lane >= head * HEAD_DIM, lane < (head + 1) * HEAD_DIM)
        mine = pl.ds(pl.multiple_of(me_ref[0] * shard, shard), shard)
        for i, name in enumerate(names):
            if name == "gm_b_s":
                g = g_all[_SLAB_BS_ROW:_SLAB_BS_ROW + N_HEADS, 0:CHUNK]
            elif name == "conv_w":
                g = p_ref[0, _SLAB_CONV_ROW:_SLAB_CONV_ROW + 4, mine]
                for j in range(1, N_DEV):
                    g = g + p_ref[j, _SLAB_CONV_ROW:_SLAB_CONV_ROW + 4, mine]
            else:
                row = [r for r, (k, _) in enumerate(_SLAB_ROWS) if k == name][0]
                g = g_all[row:row + 1, 0:dict(_SLAB_ROWS)[name]]
                if name in _LN_PARAMS:
                    g = _split_dot(jnp.where(own_lanes, g, 0.0), unfold_ref[...], 3)
            d, mn, vn = _adamw_math(w_refs[i][...], g, m_refs[i][...], v_refs[i][...])
            for o_ref, val in zip(outs[4 * i:4 * i + 4], (g, d, mn, vn)):
                o_ref[...] = val
        outs[-1][...] = g_all[_SLAB_LOSS_ROW:_SLAB_LOSS_ROW + 1, 0:128]

    def whole(shape):
        nd = len(shape)
        return pl.BlockSpec(shape, lambda i, me_ref: (0,) * nd)

    ins = [parts, unfold] + [d[k] for d in (w, m, v) for k in names]
    out_shape = tuple(jax.ShapeDtypeStruct(s, F32) for s in shapes for _ in range(4)) + (
        jax.ShapeDtypeStruct((1, 128), F32),)
    outs = pl.pallas_call(
        body, name="adamw_small", out_shape=out_shape,
        grid_spec=pltpu.PrefetchScalarGridSpec(
            num_scalar_prefetch=1, grid=(1,), in_specs=[whole(a.shape) for a in ins],
            out_specs=tuple(whole(s.shape) for s in out_shape)),
        compiler_params=_params("arbitrary"))(me, *ins)
    return {k: tuple(outs[4 * i:4 * i + 4]) for i, k in enumerate(names)}, outs[-1][0, 0]


def kernel(x, norm_mix_pre, w_in, gm_ln_w, gm_ln_b, gm_w_s, gm_b_s, conv_w, conv_b, dt_bias, a_log, d_skip, ssm_norm_w, w_out, norm_mix_post, norm_ffn_pre, w_up, w_down, norm_ffn_post, loss_target, m_norm_mix_pre, m_w_in, m_gm_ln_w, m_gm_ln_b, m_gm_w_s, m_gm_b_s, m_conv_w, m_conv_b, m_dt_bias, m_a_log, m_d_skip, m_ssm_norm_w, m_w_out, m_norm_mix_post, m_norm_ffn_pre, m_w_up, m_w_down, m_norm_ffn_post, v_norm_mix_pre, v_w_in, v_gm_ln_w, v_gm_ln_b, v_gm_w_s, v_gm_b_s, v_conv_w, v_conv_b, v_dt_bias, v_a_log, v_d_skip, v_ssm_norm_w, v_w_out, v_norm_mix_post, v_norm_ffn_pre, v_w_up, v_w_down, v_norm_ffn_post):
    w = dict(norm_mix_pre=norm_mix_pre, w_in=w_in, gm_ln_w=gm_ln_w, gm_ln_b=gm_ln_b, gm_w_s=gm_w_s, gm_b_s=gm_b_s, conv_w=conv_w, conv_b=conv_b, dt_bias=dt_bias, a_log=a_log, d_skip=d_skip, ssm_norm_w=ssm_norm_w, w_out=w_out, norm_mix_post=norm_mix_post, norm_ffn_pre=norm_ffn_pre, w_up=w_up, w_down=w_down, norm_ffn_post=norm_ffn_post)
    m = dict(norm_mix_pre=m_norm_mix_pre, w_in=m_w_in, gm_ln_w=m_gm_ln_w, gm_ln_b=m_gm_ln_b, gm_w_s=m_gm_w_s, gm_b_s=m_gm_b_s, conv_w=m_conv_w, conv_b=m_conv_b, dt_bias=m_dt_bias, a_log=m_a_log, d_skip=m_d_skip, ssm_norm_w=m_ssm_norm_w, w_out=m_w_out, norm_mix_post=m_norm_mix_post, norm_ffn_pre=m_norm_ffn_pre, w_up=m_w_up, w_down=m_w_down, norm_ffn_post=m_norm_ffn_post)
    v = dict(norm_mix_pre=v_norm_mix_pre, w_in=v_w_in, gm_ln_w=v_gm_ln_w, gm_ln_b=v_gm_ln_b, gm_w_s=v_gm_w_s, gm_b_s=v_gm_b_s, conv_w=v_conv_w, conv_b=v_conv_b, dt_bias=v_dt_bias, a_log=v_a_log, d_skip=v_d_skip, ssm_norm_w=v_ssm_norm_w, w_out=v_w_out, norm_mix_post=v_norm_mix_post, norm_ffn_pre=v_norm_ffn_pre, w_up=v_w_up, w_down=v_w_down, norm_ffn_post=v_norm_ffn_post)
    n_batch, seq, _ = x.shape
    shard_in = IN_COLS // N_DEV

    me = (4 * lax.axis_index("x") + 2 * lax.axis_index("y") + lax.axis_index("c")).astype(jnp.int32).reshape(1)

    def in_slot(own):
        return lax.dynamic_update_slice(lax.empty((N_DEV,) + own.shape, own.dtype), own[None],
                                        (me[0],) + (0,) * own.ndim)

    w_in_sh, m_in_sh, v_in_sh = w_in[0].T, m_w_in[0].T, v_w_in[0].T
    first = [_cast_to_slot(w_in_sh, me, shard_in, "cast_w_in"), in_slot(conv_w[0])]
    ici_1, tok_ici_1 = _exchange_start(first, [True] * 2, _SAME_CORE_PEERS, "gather_mix_ici_start")
    cast_out = _cast_to_slot(w_out[0], me, 128, "cast_w_out", dep=tok_ici_1)
    cast_up = _cast_to_slot(w_up[0], me, 1024, "cast_w_up", cols=True, dep=cast_out)
    second = [cast_out, cast_up, _cast_to_slot(w_down[0], me, 512, "cast_w_down", dep=cast_up)]
    gathering = {}

    def mixer_weights(after):
        bufs = [buf for buf, _ in _exchange_wait(ici_1, after, "gather_mix_ici_wait")]
        d2d_1, tok_d2d_1 = _exchange_start(bufs, [True] * 2, _SIBLING_FORWARD, "gather_mix_d2d_start")
        gathering["late_ici"], tok_ici_2 = _exchange_start(
            second, [True] * 3, _SAME_CORE_PEERS, "gather_late_ici_start", dep=tok_d2d_1)
        (_, ag_in), (_, ag_conv) = _exchange_wait(d2d_1, tok_ici_2, "gather_mix_d2d_wait")
        w_in_t = jnp.pad(ag_in.reshape(IN_COLS, D_MODEL), ((0, IN_PAD - IN_COLS), (0, 0)))
        return w_in_t, ag_conv.transpose(1, 0, 2).reshape(4, CONV_CH)

    def gmlp_done(after):
        ((buf, _),) = _exchange_wait(gathering["late_ici"], after, "gather_out_ici_wait", only=(0,))
        gathering["out"], tok = _exchange_start([buf], [True], _SIBLING_FORWARD, "gather_out_d2d_start")
        return tok

    def mixers_done(after):
        bufs = [buf for buf, _ in _exchange_wait(gathering["late_ici"], after, "gather_mlp_ici_wait", only=(1, 2))]
        gathering["mlp"], tok = _exchange_start(bufs, [True] * 2, _SIBLING_FORWARD, "gather_mlp_d2d_start")
        ((_, ag_out),) = _exchange_wait(gathering["out"], tok, "gather_out_d2d_wait")
        return ag_out.reshape(D_MODEL, D_MODEL), tok

    def mlp_weights(after):
        (_, ag_up), (_, ag_down) = _exchange_wait(gathering["mlp"], after, "gather_mlp_d2d_wait")
        return ag_up, ag_down.reshape(D_FF, D_MODEL)

    sent = {}

    def mlp_grads(g_w_down, g_w_up):
        sent["mlp"], tok = _exchange_start(
            [g_w_down.reshape(N_DEV, D_FF // N_DEV, D_MODEL), g_w_up], [False, False], _ALL_PEERS, "grads_mlp_start")
        return tok

    def gmlp_grads(g_w_out, g_w_s):
        sent["gmlp"], tok = _exchange_start(
            [g_w_out.reshape(N_DEV, D_MODEL // N_DEV, D_MODEL), in_slot(g_w_s.astype(BF16))], [False, True], _ALL_PEERS,
            "grads_gmlp_start")
        return tok

    def in_grads(g_w_in_t, g_conv_w):
        g_in_blk = g_w_in_t[:IN_COLS].reshape(N_DEV, shard_in, D_MODEL)
        sent["in"], tok = _exchange_start([g_in_blk], [False], _ALL_PEERS, "grads_in_start")
        return tok

    def arrived_updates(after):
        (own_down, p_down), (own_up, p_up) = _exchange_wait(sent["mlp"], after, "grads_mlp_wait")
        (own_out, p_out), (_, p_ws) = _exchange_wait(sent["gmlp"], own_up, "grads_gmlp_wait")
        rows = lambda t: t.reshape(t.shape[:-3] + (N_HEADS * CHUNK, CHUNK))
        return [dict(parts=p_up, own=own_up, w=w_up[0], m=m_w_up[0], v=v_w_up[0]),
                dict(parts=p_down, own=own_down, w=w_down[0], m=m_w_down[0], v=v_w_down[0]),
                dict(parts=p_out, own=own_out, w=w_out[0], m=m_w_out[0], v=v_w_out[0]),
                dict(parts=rows(p_ws), own=rows(p_ws), w=rows(gm_w_s[0]), m=rows(m_gm_w_s[0]), v=rows(v_gm_w_s[0]),
                     mask=jnp.tril(jnp.ones((CHUNK, CHUNK), F32)))]

    small = {k: w[k][0] for k in _SMALL_PARAMS + ("gm_w_s",)}
    loss_part, grad_x, g = _local_step(
        x.reshape(n_batch * seq, D_MODEL), loss_target.reshape(n_batch * seq, D_MODEL), seq, small,
        dict(mixer_weights=mixer_weights, gmlp_done=gmlp_done, mixers_done=mixers_done, mlp_weights=mlp_weights,
             mlp_grads=mlp_grads, gmlp_grads=gmlp_grads, in_grads=in_grads, arrived_updates=arrived_updates, me=me,
             prenorm_after=second[2]), first_dep=tok_ici_1)

    sent_rows, tok_rows = _exchange_start([in_slot(_pack_slab(g, loss_part))], [True], _ALL_PEERS, "grads_rows_start")
    res = dict(zip(("w_up", "w_down", "w_out", "gm_w_s"), g["updates"]))
    ((own_in, p_in),) = _exchange_wait(sent["in"], tok_rows, "grads_in_wait")
    res["w_in"] = tuple(r.T for r in _adamw_reduce(p_in, own_in, me, w_in_sh, m_in_sh, v_in_sh, shard_in, "adamw_w_in"))
    ((_, p_rows),) = _exchange_wait(sent_rows, res["w_in"][1], "grads_rows_wait")
    flat = lambda t: t[0] if t.ndim == 3 else t
    small_res, loss = _adamw_slab(
        p_rows, me, *({k: flat(d[k]) for k in _SMALL_PARAMS + ("conv_w",)} for d in (w, m, v)))
    res.update(small_res)
    res = {k: tuple(r.reshape(w[k].shape) for r in res[k]) for k in _WEIGHTS}

    outs = [loss, grad_x.reshape(x.shape)]
    for part in range(4):
        outs.extend(res[k][part] for k in _WEIGHTS)
    return tuple(outs)
```

```python
import functools

import jax
import jax.numpy as jnp
import numpy as np
from jax import lax
from jax.experimental import pallas as pl
from jax.experimental.pallas import tpu as pltpu

F32 = jnp.float32
BF16 = jnp.bfloat16

D_MODEL = 1024
GM_WIDTH = 512
SSM_WIDTH = 512
CONV_CH = 1024
N_HEADS = 8
HEAD_DIM = 64
N_STATE = 128
CHUNK = 128
D_FF = 4096
IN_COLS = 2568
IN_PAD = 2688
N_DEV = 8
EPS = 1e-6
ADAM_LR, ADAM_B1, ADAM_B2, ADAM_EPS, ADAM_WD, ADAM_STEP = 0.001, 0.9, 0.999, 1e-08, 0.01, 10
VMEM_LIMIT_BYTES = 56 * 1024 * 1024
TOKEN_TILE = 512
FF_TILE = 2048
WGRAD_TILE = 512
_W_IN_UPDATE_STEPS = 3

_NT = (((1,), (1,)), ((), ()))
_TN = (((0,), (0,)), ((), ()))


def _params(*sem):
    return pltpu.CompilerParams(dimension_semantics=sem or None, vmem_limit_bytes=VMEM_LIMIT_BYTES)


def _dot(a, b, dims=None):
    if dims is None:
        return jnp.dot(a, b, preferred_element_type=F32)
    return lax.dot_general(a, b, dims, preferred_element_type=F32)


def _split_terms(x, terms):
    out, rem = [], x
    for i in range(terms):
        hi = rem.astype(BF16)
        out.append(hi)
        if i + 1 < terms:
            rem = rem - hi.astype(F32)
    return out


def _split_dot(x, m, terms):
    acc = None
    for hi in _split_terms(x, terms):
        part = _dot(hi, m)
        acc = part if acc is None else acc + part
    return acc


def _split_dot_left(m, x, terms):
    acc = None
    for hi in _split_terms(x, terms):
        part = _dot(m, hi)
        acc = part if acc is None else acc + part
    return acc


def _gelu_and_grad(x):
    c = 0.7978845608028654
    inner = c * (x + 0.044715 * x * x * x)
    t = jnp.tanh(inner)
    g = 0.5 * x * (1.0 + t)
    dg = 0.5 * (1.0 + t) + 0.5 * x * (1.0 - t * t) * c * (1.0 + 3.0 * 0.044715 * x * x)
    return g, dg


def _softplus(x):
    return jnp.maximum(x, 0.0) + jnp.log(1.0 + jnp.exp(-jnp.abs(x)))


def _rsum(x):
    return jnp.sum(x, axis=0, keepdims=True)


def _acc_rows(ref, part, first):
    val = jnp.broadcast_to(part, ref.shape)

    @pl.when(first)
    def _():
        ref[...] = val

    @pl.when(jnp.logical_not(first))
    def _():
        ref[...] += val


def _rms_bwd(n, g, dout):
    r = lax.rsqrt(jnp.mean(n * n, axis=-1, keepdims=True) + EPS)
    nh = n * r
    dg = dout * g
    dn = r * (dg - nh * jnp.mean(dg * nh, axis=-1, keepdims=True))
    return dn, _rsum(dout * nh)


def _const_mats():
    avg = np.kron(np.eye(4), np.full((HEAD_DIM, HEAD_DIM), 1.0 / HEAD_DIM))
    expand = np.zeros((CHUNK, SSM_WIDTH), np.float32)
    for h in range(N_HEADS):
        expand[h, h * HEAD_DIM:(h + 1) * HEAD_DIM] = 1.0
    tril = np.tril(np.ones((CHUNK, CHUNK), np.float32))
    as_bf16 = lambda a: jnp.asarray(a, dtype=BF16)
    return as_bf16(avg), as_bf16(expand), as_bf16(expand.T), as_bf16(tril), as_bf16(tril.T)


def _full(shape):
    nd = len(shape)
    return pl.BlockSpec(shape, lambda *_: (0,) * nd)


_HBM = pl.BlockSpec(memory_space=pltpu.HBM)
_SEM = pl.BlockSpec(memory_space=pltpu.SEMAPHORE)
_ALL_PEERS = tuple((k, 0) for k in range(1, N_DEV))
_SAME_CORE_PEERS = ((2, 0), (4, 0), (6, 0))
_SIBLING_FORWARD = ((1, 0), (1, 2), (1, 4), (1, 6))


def _flip(j, k):
    for bit in (4, 2, 1):
        if k & bit:
            j = j + bit - 2 * (j & bit)
    return j


def _copies(src, land, send_sems, recv_sems, hops, slots=None):
    x, y, c = lax.axis_index("x"), lax.axis_index("y"), lax.axis_index("c")
    me = 4 * x + 2 * y + c
    slots = range(len(src)) if slots is None else slots
    out = []
    for t in range(len(src)):
        for i, (k, b) in enumerate(hops):
            pos = (1 - x if k & 4 else x, 1 - y if k & 2 else y, 1 - c if k & 1 else c)
            peer = _flip(me, k)
            sem = slots[t] * len(hops) + i
            mk = functools.partial(pltpu.make_async_remote_copy, send_sem=send_sems.at[sem], recv_sem=recv_sems.at[sem],
                                   device_id=pos, device_id_type=pl.DeviceIdType.MESH)
            if land[t] is None and src[t].shape[0] != N_DEV:
                width = src[t].shape[1] // N_DEV
                slab = lambda j: src[t].at[:, pl.ds(pl.multiple_of(j * width, 128), width)]
                mine = functools.partial(mk, src_ref=slab(_flip(me, b)), dst_ref=slab(_flip(me, b)))
                theirs = functools.partial(mk, src_ref=slab(_flip(peer, b)), dst_ref=slab(_flip(peer, b)))
            elif land[t] is None:
                mine = functools.partial(mk, src_ref=src[t].at[_flip(me, b)], dst_ref=src[t].at[_flip(me, b)])
                theirs = functools.partial(mk, src_ref=src[t].at[_flip(peer, b)], dst_ref=src[t].at[_flip(peer, b)])
            else:
                assert b == 0
                mine = functools.partial(mk, src_ref=src[t].at[peer], dst_ref=land[t].at[me])
                theirs = functools.partial(mk, src_ref=src[t].at[peer], dst_ref=land[t].at[peer])
            out.append((mine, theirs))
    return out


def _exchange_start(srcs, inplace, peers, name, dep=None):
    n = len(srcs)
    lands = [None if ip else pltpu.with_memory_space_constraint(lax.empty(s.shape, s.dtype), pltpu.HBM)
             for s, ip in zip(srcs, inplace)]
    real_lands = [l for l in lands if l is not None]
    n_l = len(real_lands)
    deps = [] if dep is None else [dep]

    def body(*refs):
        src = refs[:n]
        land_refs = list(refs[n:n + n_l])
        send_sems, recv_sems = refs[n + n_l + len(deps)], refs[n + n_l + len(deps) + 1]
        token = refs[-1]
        land = [None if ip else land_refs.pop(0) for ip in inplace]
        for mine, _ in _copies(src, land, send_sems, recv_sems, peers):
            mine().start()
        token[...] = jnp.zeros_like(token)

    sem_t = pltpu.SemaphoreType.DMA((n * len(peers),))
    outs = pl.pallas_call(
        body, name=name,
        out_shape=(sem_t, sem_t) + tuple(pltpu.HBM(a.shape, a.dtype) for a in list(srcs) + real_lands)
        + (jax.ShapeDtypeStruct((8, 128), F32),),
        in_specs=[_HBM] * (n + n_l) + [pl.BlockSpec(memory_space=pl.ANY)] * len(deps),
        out_specs=(_SEM, _SEM) + (_HBM,) * (n + n_l) + (pl.BlockSpec(memory_space=pltpu.VMEM),),
        input_output_aliases={i: 2 + i for i in range(n + n_l)},
        compiler_params=pltpu.CompilerParams(has_side_effects=pltpu.SideEffectType.DATAFLOW_SIDE_EFFECTING),
    )(*[pltpu.with_memory_space_constraint(s, pltpu.HBM) for s in srcs], *real_lands, *deps)
    handle = dict(send=outs[0], recv=outs[1], srcs=outs[2:2 + n], lands=outs[2 + n:2 + n + n_l], inplace=inplace,
                  peers=peers)
    return handle, outs[-1]


def _exchange_wait(handle, after, name, only=None):
    srcs, lands, inplace, peers = handle["srcs"], handle["lands"], handle["inplace"], handle["peers"]
    slots = None
    if only is not None:
        assert all(inplace)
        slots, srcs, inplace = list(only), [srcs[t] for t in only], [True] * len(only)
    n, n_l = len(srcs), len(lands)

    def body(*refs):
        src = refs[:n]
        land_refs = list(refs[n:n + n_l])
        send_sems, recv_sems = refs[n + n_l], refs[n + n_l + 1]
        land = [None if ip else land_refs.pop(0) for ip in inplace]
        for mine, theirs in _copies(src, land, send_sems, recv_sems, peers, slots):
            mine().wait_send()
            theirs().wait_recv()

    outs = pl.pallas_call(
        body, name=name, out_shape=tuple(pltpu.HBM(a.shape, a.dtype) for a in list(srcs) + list(lands)),
        in_specs=[_HBM] * (n + n_l) + [_SEM, _SEM, pl.BlockSpec(memory_space=pl.ANY)],
        out_specs=(_HBM,) * (n + n_l), input_output_aliases={i: i for i in range(n + n_l)},
        compiler_params=pltpu.CompilerParams(has_side_effects=pltpu.SideEffectType.DATAFLOW_SIDE_EFFECTING),
    )(*srcs, *lands, handle["send"], handle["recv"], after)
    res, land_out = [], list(outs[n:])
    for t in range(n):
        res.append((outs[t], outs[t] if inplace[t] else land_out.pop(0)))
    return res


def _cast_to_slot(w, me, rows, name, cols=False, dep=None):
    r, cdim = w.shape
    deps = [] if dep is None else [dep]

    def body(me_ref, w_ref, *rest):
        o_ref = rest[-1]
        if cols:
            o_ref[...] = w_ref[...].astype(BF16)
        else:
            o_ref[0] = w_ref[...].astype(BF16)

    if cols:
        out_shape = jax.ShapeDtypeStruct((r, N_DEV * cdim), BF16)
        out_spec = pl.BlockSpec((rows, cdim), lambda i, me_ref: (i, me_ref[0]))
    else:
        out_shape = jax.ShapeDtypeStruct((N_DEV, r, cdim), BF16)
        out_spec = pl.BlockSpec((1, rows, cdim), lambda i, me_ref: (me_ref[0], i, 0))
    return pl.pallas_call(
        body, name=name, out_shape=out_shape,
        grid_spec=pltpu.PrefetchScalarGridSpec(
            num_scalar_prefetch=1, grid=(r // rows,),
            in_specs=[pl.BlockSpec((rows, cdim), lambda i, me_ref: (i, 0))]
            + [pl.BlockSpec(memory_space=pl.ANY)] * len(deps), out_specs=out_spec),
        compiler_params=_params("parallel"))(me, w, *deps)


def _adamw_math(w, g, m, v):
    m = ADAM_B1 * m + (1.0 - ADAM_B1) * g
    v = ADAM_B2 * v + (1.0 - ADAM_B2) * (g * g)
    m_hat = m / (1.0 - ADAM_B1 ** ADAM_STEP)
    v_hat = v / (1.0 - ADAM_B2 ** ADAM_STEP)
    delta = -ADAM_LR * (m_hat / (jnp.sqrt(v_hat) + ADAM_EPS) + ADAM_WD * w)
    return delta, m, v


def _sum_parts(me, p_ref, own):
    g = None
    for j in range(N_DEV):
        term = (p_ref[j] if own is None else jnp.where(me == j, own, p_ref[j])).astype(F32)
        g = term if g is None else g + term
    return g


def _adamw_reduce(parts, own, me, w, m, v, rows, name):
    r, cdim = w.shape

    def body(me_ref, p_ref, own_ref, w_ref, m_ref, v_ref, g_out, d_out, m_out, v_out):
        g = _sum_parts(me_ref[0], p_ref, own_ref[0])
        d, mn, vn = _adamw_math(w_ref[...], g, m_ref[...], v_ref[...])
        g_out[...] = g
        d_out[...] = d
        m_out[...] = mn
        v_out[...] = vn

    blk = pl.BlockSpec((rows, cdim), lambda i, me_ref: (i, 0))
    sds = jax.ShapeDtypeStruct(w.shape, F32)
    return pl.pallas_call(
        body, name=name, out_shape=(sds,) * 4,
        grid_spec=pltpu.PrefetchScalarGridSpec(
            num_scalar_prefetch=1, grid=(r // rows,),
            in_specs=[pl.BlockSpec((N_DEV, rows, cdim), lambda i, me_ref: (0, i, 0)),
                      pl.BlockSpec((1, rows, cdim), lambda i, me_ref: (me_ref[0], i, 0)), blk, blk, blk],
            out_specs=(blk,) * 4),
        compiler_params=_params("parallel"))(me, parts, own, w, m, v)


def _sum_grads(parts, own, me, rows, name):
    _, r, cdim = parts.shape

    def body(me_ref, p_ref, own_ref, g_out):
        g_out[...] = _sum_parts(me_ref[0], p_ref, own_ref[0])

    return pl.pallas_call(
        body, name=name, out_shape=jax.ShapeDtypeStruct((r, cdim), F32),
        grid_spec=pltpu.PrefetchScalarGridSpec(
            num_scalar_prefetch=1, grid=(r // rows,),
            in_specs=[pl.BlockSpec((N_DEV, rows, cdim), lambda i, me_ref: (0, i, 0)),
                      pl.BlockSpec((1, rows, cdim), lambda i, me_ref: (me_ref[0], i, 0))],
            out_specs=pl.BlockSpec((rows, cdim), lambda i, me_ref: (i, 0))),
        compiler_params=_params("parallel"))(me, parts, own)


def _adamw_rows(g, w, m, v, rows, name):
    r, _, cdim = w.shape

    def body(g_ref, w_ref, m_ref, v_ref, g_out, d_out, m_out, v_out):
        d, mn, vn = _adamw_math(w_ref[...], g_ref[...], m_ref[...], v_ref[...])
        g_out[...] = g_ref[...]
        d_out[...] = d
        m_out[...] = mn
        v_out[...] = vn

    blk = pl.BlockSpec((rows, 1, cdim), lambda i: (i, 0, 0))
    return pl.pallas_call(
        body, name=name, out_shape=(jax.ShapeDtypeStruct(w.shape, F32),) * 4, grid=(r // rows,),
        in_specs=[blk] * 4, out_specs=(blk,) * 4, compiler_params=_params("parallel"))(g, w, m, v)


_IN_SPLITS = ((0, 512), (512, 1024), (1024, 1536), (1536, 2560), (2560, IN_PAD))


def _prenorm(x, g1, tm, dep=None):
    t_tok = x.shape[0]
    deps = [] if dep is None else [dep]

    def body(x_ref, g_ref, *rest):
        xv = x_ref[...]
        r = lax.rsqrt(jnp.mean(xv * xv, axis=-1, keepdims=True) + EPS)
        rest[-1][...] = (xv * r * g_ref[...]).astype(BF16)

    row = pl.BlockSpec((tm, D_MODEL), lambda i: (i, 0))
    return pl.pallas_call(
        body, name="prenorm", grid=(t_tok // tm,), out_shape=jax.ShapeDtypeStruct((t_tok, D_MODEL), BF16),
        in_specs=[row, _full((1, D_MODEL))] + [pl.BlockSpec(memory_space=pl.ANY)] * len(deps), out_specs=row,
        compiler_params=_params("parallel"))(x, g1, *deps)


def _in_proj(h1, w_in, tm):
    t_tok = h1.shape[0]

    def body(h_ref, w_ref, *outs):
        h = h_ref[...]
        for (a, b), o_ref in zip(_IN_SPLITS, outs):
            o_ref[...] = _dot(h, w_ref[a:b, :], _NT).astype(o_ref.dtype)

    row = lambda n: pl.BlockSpec((tm, n), lambda i: (i, 0))
    widths = [b - a for a, b in _IN_SPLITS]
    dtypes = (BF16, BF16, BF16, F32, F32)
    return pl.pallas_call(
        body, name="in_proj", grid=(t_tok // tm,),
        out_shape=tuple(jax.ShapeDtypeStruct((t_tok, n), dt) for n, dt in zip(widths, dtypes)),
        in_specs=[row(D_MODEL), _full((IN_PAD, D_MODEL))], out_specs=tuple(row(n) for n in widths),
        compiler_params=_params("parallel"))(h1, w_in)


def _lane_masks():
    lane = lax.broadcasted_iota(jnp.int32, (1, 2 * HEAD_DIM), 1)
    left = (lane < HEAD_DIM).astype(F32)
    return left, 1.0 - left


def _stack_pair(v, m_l, m_r):
    return jnp.concatenate([v * m_l, v * m_r], axis=0).astype(BF16)


def _head_mean(x, avg):
    n = avg.shape[0]
    return jnp.concatenate([_split_dot(x[:, n * i:n * (i + 1)], avg, 2) for i in range(x.shape[1] // n)], axis=1)


def _gmlp_common(u, v, lnw, lnb, avg, wcat_ref, bias, m_l, m_r):
    ug, dug = _gelu_and_grad(u)
    vg, dvg = _gelu_and_grad(v)
    mu = _head_mean(vg, avg)
    vc = vg - mu
    var = _head_mean(vc * vc, avg)
    rstd = lax.rsqrt(var + EPS)
    vhat = vc * rstd
    vn = vhat * lnw + lnb
    rows = []
    for r in range(u.shape[0] // CHUNK):
        cols = []
        for j in range(N_HEADS // 2):
            pair = vn[CHUNK * r:CHUNK * (r + 1), 128 * j:128 * (j + 1)]
            cols.append(_dot(wcat_ref[j], _stack_pair(pair, m_l, m_r)))
        rows.append(jnp.concatenate(cols, axis=1) + bias)
    mixed = jnp.concatenate(rows, axis=0)
    return ug, dug, dvg, rstd, vhat, vn, mixed


_GMLP_ROWS = 4 * CHUNK


def _gmlp_fwd(u, v, lnw, lnb, wcat, bias, avg):
    t_tok = u.shape[0]
    tm = min(_GMLP_ROWS, t_tok)

    def body(u_ref, v_ref, lnw_ref, lnb_ref, wcat_ref, bias_ref, avg_ref, o_ref):
        m_l, m_r = _lane_masks()
        ug, _, _, _, _, _, mixed = _gmlp_common(
            u_ref[...].astype(F32), v_ref[...].astype(F32), lnw_ref[...], lnb_ref[...], avg_ref[...], wcat_ref,
            bias_ref[...], m_l, m_r)
        o_ref[...] = (ug * mixed).astype(BF16)

    row = pl.BlockSpec((tm, GM_WIDTH), lambda i: (i, 0))
    return pl.pallas_call(
        body, name="gmlp_fwd", grid=(t_tok // tm,), out_shape=jax.ShapeDtypeStruct((t_tok, GM_WIDTH), BF16),
        in_specs=[row, row, _full((1, GM_WIDTH)), _full((1, GM_WIDTH)), _full(wcat.shape), _full(bias.shape),
                  _full(avg.shape)],
        out_specs=row, compiler_params=_params("parallel"))(u, v, lnw, lnb, wcat, bias, avg)


def _shift_rows(x, edge, j, down):
    groups, cols = x.shape[0] // 8, x.shape[1]
    amount = j if down else 8 - j
    rot = pltpu.roll(x.reshape(groups, 8, cols), amount, axis=1)
    edge_rot = pltpu.roll(edge, amount, axis=0)[None]
    sub = lax.broadcasted_iota(jnp.int32, (1, 8, 1), 1)
    if down:
        out = jnp.where(sub < j, jnp.concatenate([edge_rot, rot[:-1]], axis=0), rot)
    else:
        out = jnp.where(sub < 8 - j, rot, jnp.concatenate([rot[1:], edge_rot], axis=0))
    return out.reshape(x.shape)


def _conv_pre(xbc, tail, cw_ref, cb):
    taps = [_shift_rows(xbc, tail, 3 - k, True) for k in range(3)] + [xbc]
    return cb + cw_ref[0:1, :] * taps[0] + cw_ref[1:2, :] * taps[1] + cw_ref[2:3, :] * taps[2] + cw_ref[3:4, :] * taps[3]


def _ssd_common(pre, dtr, dtb, alog, expand, tril):
    q = CHUNK
    sg = jax.nn.sigmoid(pre)
    act = pre * sg
    lane = lax.broadcasted_iota(jnp.int32, (1, CHUNK), 1)
    a_row = jnp.where(lane < N_HEADS, -jnp.exp(alog), 0.0)
    dtp = dtr + dtb
    dt = _softplus(dtp)
    a_cs = _split_dot_left(tril, dt * a_row, 3)
    a_cs_t = a_cs.T
    dt_exp = _split_dot(dt, expand, 3)
    a_exp = _split_dot(a_cs, expand, 3)
    a_end = a_exp[q - 1:q, :]
    li = lax.broadcasted_iota(jnp.int32, (q, q), 0)
    si = lax.broadcasted_iota(jnp.int32, (q, q), 1)
    causal = si <= li
    decay = []
    for h in range(N_HEADS):
        seg = a_cs[:, h:h + 1] - a_cs_t[h:h + 1, :]
        decay.append(jnp.where(causal, jnp.exp(jnp.minimum(seg, 0.0)), 0.0))
    return dict(pre=pre, sg=sg, act=act, a_row=a_row, dtp=dtp, dt=dt, dt_exp=dt_exp, a_exp=a_exp,
                e=jnp.exp(a_exp), w_end=jnp.exp(a_end - a_exp), cd=jnp.exp(a_end), decay=decay)


def _ssd_specs(t_tok, seq, reverse):
    nb, nc = t_tok // seq, seq // CHUNK

    def chunk(c):
        return nc - 1 - c if reverse else c

    def row(n, col=0):
        return pl.BlockSpec((nb, CHUNK, n), lambda c: (0, chunk(c), col))

    tail = pl.BlockSpec((nb, 8, CONV_CH), lambda c: (0, jnp.maximum(chunk(c) * (CHUNK // 8) - 1, 0), 0))
    states = pl.BlockSpec((nb, 1, N_STATE, SSM_WIDTH), lambda c: (0, chunk(c), 0, 0))
    fold = lambda a: a.reshape(nb, seq, a.shape[-1])
    unfold = lambda a: a.reshape(t_tok, a.shape[-1])
    return nb, nc, row, tail, states, fold, unfold


def _ssd_fwd(z, xbc, dtr, cw, cb, dtb, alog, dskip_exp, nw, expand, tril, seq, dep=None):
    t_tok = z.shape[0]
    nb, nc, row, tail, states_spec, fold, unfold = _ssd_specs(t_tok, seq, False)

    def body(z_ref, xbc_ref, tail_ref, dtr_ref, cw_ref, cb_ref, dtb_ref, alog_ref, dsk_ref, nw_ref, exp_ref,
             tril_ref, o_ref, y_ref, st_ref, pre_ref, state_ref):
        c = pl.program_id(0)

        @pl.when(c == 0)
        def _():
            state_ref[...] = jnp.zeros_like(state_ref)

        m_l, m_r = _lane_masks()
        for s in range(nb):
            pre = _conv_pre(xbc_ref[s], jnp.where(c == 0, 0.0, tail_ref[s]), cw_ref, cb_ref[...])
            pre_ref[s] = pre
            f = _ssd_common(pre, dtr_ref[s], dtb_ref[...], alog_ref[...], exp_ref[...], tril_ref[...])
            act = f["act"]
            xs = act[:, :SSM_WIDTH]
            xdt = xs * f["dt_exp"]
            xw = xdt * f["w_end"]
            state = state_ref[s]
            st_ref[s, 0] = state
            ydiag, yoff, snew = [], [], []
            for g in range(2):
                bg = act[:, 512 + 128 * g:640 + 128 * g].astype(BF16)
                cg = act[:, 768 + 128 * g:896 + 128 * g].astype(BF16)
                cb_mat = _dot(cg, bg, _NT)
                for pr in range(2):
                    h0 = 4 * g + 2 * pr
                    gcat = jnp.concatenate(
                        [(cb_mat * f["decay"][h0]).astype(BF16), (cb_mat * f["decay"][h0 + 1]).astype(BF16)], axis=1)
                    ydiag.append(_dot(gcat, _stack_pair(xdt[:, 64 * h0:64 * h0 + 128], m_l, m_r)))
                yoff.append(_dot(cg, state[:, 256 * g:256 * (g + 1)].astype(BF16)))
                snew.append(_dot(bg, xw[:, 256 * g:256 * (g + 1)].astype(BF16), _TN))
            y = jnp.concatenate(ydiag, axis=1) + f["e"] * jnp.concatenate(yoff, axis=1) + dsk_ref[...] * xs
            state_ref[s] = state * f["cd"] + jnp.concatenate(snew, axis=1)
            y_ref[s] = y
            zv = z_ref[s].astype(F32)
            yg = y * (zv * jax.nn.sigmoid(zv))
            outs = []
            for g in range(2):
                ygg = yg[:, 256 * g:256 * (g + 1)]
                outs.append(ygg * lax.rsqrt(jnp.mean(ygg * ygg, axis=-1, keepdims=True) + EPS))
            o_ref[s] = (jnp.concatenate(outs, axis=1) * nw_ref[...]).astype(BF16)

    consts = [cw, cb, dtb, alog, dskip_exp, nw, expand, tril]
    deps = [] if dep is None else [dep]
    n_in = 4 + len(consts)

    def body_skipping_dep(*refs):
        body(*refs[:n_in], *refs[n_in + len(deps):])

    sd = lambda n, dt: jax.ShapeDtypeStruct((nb, seq, n), dt)
    o, y, states, pre = pl.pallas_call(
        body_skipping_dep, name="ssd_fwd", grid=(nc,),
        out_shape=(sd(SSM_WIDTH, BF16), sd(SSM_WIDTH, F32), jax.ShapeDtypeStruct((nb, nc, N_STATE, SSM_WIDTH), F32),
                   sd(CONV_CH, F32)),
        in_specs=[row(SSM_WIDTH), row(CONV_CH), tail, row(CHUNK)] + [_full(a.shape) for a in consts]
        + [pl.BlockSpec(memory_space=pl.ANY)] * len(deps),
        out_specs=(row(SSM_WIDTH), row(SSM_WIDTH), states_spec, row(CONV_CH)),
        scratch_shapes=[pltpu.VMEM((nb, N_STATE, SSM_WIDTH), F32)],
        compiler_params=_params("arbitrary"))(fold(z), fold(xbc), fold(xbc), fold(dtr), *consts, *deps)
    return unfold(o), unfold(y), states, unfold(pre)


def _out_proj(mix_a, mix_b, w_out, x, g2, g3, tm, dep=None):
    t_tok = x.shape[0]
    deps = [] if dep is None else [dep]

    def body(a_ref, b_ref, w_ref, x_ref, g2_ref, g3_ref, *rest):
        o_ref, x2_ref, h3_ref = rest[-3:]
        o = _dot(a_ref[...], w_ref[0:GM_WIDTH, :]) + _dot(b_ref[...], w_ref[GM_WIDTH:, :])
        o_ref[...] = o
        r2 = lax.rsqrt(jnp.mean(o * o, axis=-1, keepdims=True) + EPS)
        x2 = x_ref[...] + o * r2 * g2_ref[...]
        x2_ref[...] = x2
        r3 = lax.rsqrt(jnp.mean(x2 * x2, axis=-1, keepdims=True) + EPS)
        h3_ref[...] = (x2 * r3 * g3_ref[...]).astype(BF16)

    row = lambda n: pl.BlockSpec((tm, n), lambda i: (i, 0))
    sd = lambda dt: jax.ShapeDtypeStruct((t_tok, D_MODEL), dt)
    return pl.pallas_call(
        body, name="out_proj", grid=(t_tok // tm,), out_shape=(sd(F32), sd(F32), sd(BF16)),
        in_specs=[row(GM_WIDTH), row(SSM_WIDTH), _full((D_MODEL, D_MODEL)), row(D_MODEL), _full((1, D_MODEL)),
                  _full((1, D_MODEL))] + [pl.BlockSpec(memory_space=pl.ANY)] * len(deps),
        out_specs=(row(D_MODEL),) * 3, compiler_params=_params("parallel"))(mix_a, mix_b, w_out, x, g2, g3, *deps)


def _mlp_fwd(h3, w_up, w_down, x2, target, g4, tm, tf):
    t_tok = x2.shape[0]

    def up_body(h_ref, wu_ref, ra_ref):
        ra_ref[...] = jnp.maximum(_dot(h_ref[...], wu_ref[...]), 0.0).astype(BF16)

    tu = min(2 * tm, t_tok)
    ra = pl.pallas_call(
        up_body, name="mlp_up", grid=(D_FF // tf, t_tok // tu), out_shape=jax.ShapeDtypeStruct((t_tok, D_FF), BF16),
        in_specs=[pl.BlockSpec((tu, D_MODEL), lambda j, i: (i, 0)), pl.BlockSpec((D_MODEL, tf), lambda j, i: (0, j))],
        out_specs=pl.BlockSpec((tu, tf), lambda j, i: (i, j)), compiler_params=_params("parallel", "parallel"))(h3, w_up)

    def down_body(ra_ref, wd_ref, x2_ref, t_ref, g4_ref, dd_ref, dy_ref, dg4_ref, loss_ref):
        i = pl.program_id(0)
        rav = ra_ref[...]
        dvec = _dot(rav * rav, wd_ref[...])
        r4 = lax.rsqrt(jnp.mean(dvec * dvec, axis=-1, keepdims=True) + EPS)
        dn = dvec * r4
        g4 = g4_ref[...]
        err = x2_ref[...] + dn * g4 - t_ref[...]
        dy = err * (1.0 / D_MODEL)
        dy_ref[...] = dy
        dg = dy * g4
        dd_ref[...] = (r4 * (dg - dn * jnp.mean(dg * dn, axis=-1, keepdims=True))).astype(BF16)
        _acc_rows(dg4_ref, _rsum(dy * dn), i == 0)
        tile_loss = 0.5 * jnp.sum(jnp.sum(err * err, axis=-1, keepdims=True), axis=0, keepdims=True) / D_MODEL
        _acc_rows(loss_ref, jnp.broadcast_to(tile_loss, (1, 128)), i == 0)

    row = pl.BlockSpec((tm, D_MODEL), lambda i: (i, 0))
    dd, dy, dg4, loss = pl.pallas_call(
        down_body, name="mlp_down", grid=(t_tok // tm,),
        out_shape=(jax.ShapeDtypeStruct((t_tok, D_MODEL), BF16), jax.ShapeDtypeStruct((t_tok, D_MODEL), F32),
                   jax.ShapeDtypeStruct((1, D_MODEL), F32), jax.ShapeDtypeStruct((1, 128), F32)),
        in_specs=[pl.BlockSpec((tm, D_FF), lambda i: (i, 0)), _full((D_FF, D_MODEL)), row, row, _full((1, D_MODEL))],
        out_specs=(row, row, _full((1, D_MODEL)), _full((1, 128))),
        compiler_params=_params("arbitrary"))(ra, w_down, x2, target, g4)
    return ra, dd, dy, dg4, loss


def _mlp_bwd(dd, w_down, ra, w_up, x2, dy, o, g3, g2, tm, tf):
    t_tok = x2.shape[0]

    def hidden_body(dd_ref, wd_ref, ra_ref, da_ref):
        df = _dot(dd_ref[...], wd_ref[...], _NT)
        da_ref[...] = (df * (2.0 * ra_ref[...].astype(F32))).astype(BF16)

    tu = min(2 * tm, t_tok)
    da = pl.pallas_call(
        hidden_body, name="mlp_bwd_hidden", grid=(D_FF // tf, t_tok // tu),
        out_shape=jax.ShapeDtypeStruct((t_tok, D_FF), BF16),
        in_specs=[pl.BlockSpec((tu, D_MODEL), lambda j, i: (i, 0)), pl.BlockSpec((tf, D_MODEL), lambda j, i: (j, 0)),
                  pl.BlockSpec((tu, tf), lambda j, i: (i, j))],
        out_specs=pl.BlockSpec((tu, tf), lambda j, i: (i, j)),
        compiler_params=_params("parallel", "parallel"))(dd, w_down, ra)

    def in_body(da_ref, wu_ref, x2_ref, dy_ref, o_ref, g3_ref, g2_ref, dx2_ref, do_ref, dg3_ref, dg2_ref):
        i = pl.program_id(0)
        dh3 = _dot(da_ref[...], wu_ref[...], _NT)
        dn3, dg3 = _rms_bwd(x2_ref[...], g3_ref[...], dh3)
        dx2 = dy_ref[...] + dn3
        dx2_ref[...] = dx2
        do, dg2 = _rms_bwd(o_ref[...], g2_ref[...], dx2)
        do_ref[...] = do.astype(BF16)
        _acc_rows(dg3_ref, dg3, i == 0)
        _acc_rows(dg2_ref, dg2, i == 0)

    row = pl.BlockSpec((tm, D_MODEL), lambda i: (i, 0))
    vec = _full((1, D_MODEL))
    sd = lambda dt: jax.ShapeDtypeStruct((t_tok, D_MODEL), dt)
    dx2, do, dg3, dg2 = pl.pallas_call(
        in_body, name="mlp_bwd_in", grid=(t_tok // tm,),
        out_shape=(sd(F32), sd(BF16), jax.ShapeDtypeStruct((1, D_MODEL), F32), jax.ShapeDtypeStruct((1, D_MODEL), F32)),
        in_specs=[pl.BlockSpec((tm, D_FF), lambda i: (i, 0)), _full((D_MODEL, D_FF)), row, row, row, vec, vec],
        out_specs=(row, row, vec, vec), compiler_params=_params("arbitrary"))(da, w_up, x2, dy, o, g3, g2)
    return da, dx2, do, dg3, dg2


def _wgrad(a, b, out_blocks, bm, bn, bk, square_a, name, dep=None):
    t_tok, m = a.shape
    n = b.shape[1]
    nk = t_tok // bk

    def body(a_ref, b_ref, *rest):
        o_ref, acc_ref = rest[-2:]
        k = pl.program_id(2)
        av = a_ref[...]
        if square_a:
            av = av * av
        part = _dot(av, b_ref[...], _TN)

        def emit(res):
            if out_blocks is None:
                o_ref[...] = res.astype(BF16)
            else:
                o_ref[0] = res.astype(BF16)

        if nk == 1:
            emit(part)
            return

        @pl.when(k == 0)
        def _():
            acc_ref[...] = part

        @pl.when(k > 0)
        def _():
            acc_ref[...] += part

        @pl.when(k == nk - 1)
        def _():
            emit(acc_ref[...])

    if out_blocks is None:
        out_shape = jax.ShapeDtypeStruct((m, n), BF16)
        out_spec = pl.BlockSpec((bm, bn), lambda i, j, k: (i, j))
    else:
        assert n // out_blocks == bn
        out_shape = jax.ShapeDtypeStruct((out_blocks, m, bn), BF16)
        out_spec = pl.BlockSpec((1, bm, bn), lambda i, j, k: (j, i, 0))
    deps = [] if dep is None else [dep]
    return pl.pallas_call(
        body, name=name, grid=(m // bm, n // bn, nk), out_shape=out_shape,
        in_specs=[pl.BlockSpec((bk, bm), lambda i, j, k: (k, i)), pl.BlockSpec((bk, bn), lambda i, j, k: (k, j))]
        + [pl.BlockSpec(memory_space=pl.ANY)] * len(deps),
        out_specs=out_spec, scratch_shapes=[pltpu.VMEM((bm, bn) if nk > 1 else (8, 128), F32)],
        compiler_params=_params("parallel", "parallel", "arbitrary"))(a, b, *deps)


def _wgrad_in_chunked(h1, pieces, bn, bk, dep=None):
    t_tok = h1.shape[0]
    nk = t_tok // bk
    widths = [b - a for a, b in _IN_SPLITS]

    def body(h_ref, *rest):
        piece_refs = rest[:len(widths)]
        o_ref, acc_ref = rest[-2:]
        k = pl.program_id(1)
        hv = h_ref[...]
        for (a, b), r in zip(_IN_SPLITS, piece_refs):
            part = _dot(r[...], hv, _TN)

            @pl.when(k == 0)
            def _():
                acc_ref[a:b, :] = part

            @pl.when(k > 0)
            def _():
                acc_ref[a:b, :] += part

        @pl.when(k == nk - 1)
        def _():
            o_ref[...] = acc_ref[...].astype(BF16)

    deps = [] if dep is None else [dep]
    return pl.pallas_call(
        body, name="wgrad_in", grid=(D_MODEL // bn, nk), out_shape=jax.ShapeDtypeStruct((IN_PAD, D_MODEL), BF16),
        in_specs=[pl.BlockSpec((bk, bn), lambda j, k: (k, j))] + [pl.BlockSpec((bk, n), lambda j, k: (k, 0)) for n in widths]
        + [pl.BlockSpec(memory_space=pl.ANY)] * len(deps),
        out_specs=pl.BlockSpec((IN_PAD, bn), lambda j, k: (0, j)), scratch_shapes=[pltpu.VMEM((IN_PAD, bn), F32)],
        compiler_params=_params("parallel", "arbitrary"))(h1, *pieces, *deps)


def _wgrad_pieces(h1, pieces, bn, name, dep=None):
    t_tok = h1.shape[0]
    widths = [p.shape[1] for p in pieces]
    starts = [sum(widths[:i]) for i in range(len(widths))]

    def body(h_ref, *rest):
        piece_refs = rest[:len(widths)]
        o_ref = rest[-1]
        hv = h_ref[...]
        for a, n, r in zip(starts, widths, piece_refs):
            o_ref[a:a + n, :] = _dot(r[...], hv, _TN).astype(BF16)

    deps = [] if dep is None else [dep]
    return pl.pallas_call(
        body, name=name, grid=(D_MODEL // bn,), out_shape=jax.ShapeDtypeStruct((sum(widths), D_MODEL), BF16),
        in_specs=[pl.BlockSpec((t_tok, bn), lambda j: (0, j))] + [pl.BlockSpec((t_tok, n), lambda j: (0, 0)) for n in widths]
        + [pl.BlockSpec(memory_space=pl.ANY)] * len(deps),
        out_specs=pl.BlockSpec((sum(widths), bn), lambda j: (0, j)),
        compiler_params=_params("parallel"))(h1, *pieces, *deps)


def _dmix(do, w_out, tm, dep=None):
    t_tok = do.shape[0]

    def body(d_ref, w_ref, *rest):
        rest[-1][...] = _dot(d_ref[...], w_ref[...], _NT).astype(BF16)

    row = pl.BlockSpec((tm, D_MODEL), lambda i: (i, 0))
    deps = [] if dep is None else [dep]
    return pl.pallas_call(
        body, name="dmix", grid=(t_tok // tm,), out_shape=jax.ShapeDtypeStruct((t_tok, D_MODEL), BF16),
        in_specs=[row, _full((D_MODEL, D_MODEL))] + [pl.BlockSpec(memory_space=pl.ANY)] * len(deps), out_specs=row,
        compiler_params=_params("parallel"))(do, w_out, *deps)


def _gmlp_bwd(dmix, u, v, lnw, lnb, wcat, wtcat, bias, avg, expand_t):
    t_tok = u.shape[0]
    tm = min(_GMLP_ROWS, t_tok)

    def body(dm_ref, u_ref, v_ref, lnw_ref, lnb_ref, wcat_ref, wtcat_ref, bias_ref, avg_ref, expt_ref, du_ref, dv_ref,
             dw_ref, db_ref, dlnw_ref, dlnb_ref):
        i = pl.program_id(0)
        m_l, m_r = _lane_masks()
        avg = avg_ref[...]
        lnw = lnw_ref[...]
        ug, dug, dvg, rstd, vhat, vn, mixed = _gmlp_common(
            u_ref[...].astype(F32), v_ref[...].astype(F32), lnw, lnb_ref[...], avg, wcat_ref, bias_ref[...], m_l, m_r)
        dya = dm_ref[...].astype(F32)
        du_ref[...] = (dya * mixed * dug).astype(BF16)
        dmixed = dya * ug
        dvn_rows, dws, dbt = [], [None] * N_HEADS, None
        for r in range(tm // CHUNK):
            dvn_cols = []
            for j in range(N_HEADS // 2):
                dmp = dmixed[CHUNK * r:CHUNK * (r + 1), 128 * j:128 * (j + 1)]
                dvn_cols.append(_dot(wtcat_ref[j], _stack_pair(dmp, m_l, m_r)))
                vnp = vn[CHUNK * r:CHUNK * (r + 1), 128 * j:128 * (j + 1)].astype(BF16)
                for i_h, mask in enumerate((m_l, m_r)):
                    part = _dot((dmp * mask).astype(BF16), vnp, _NT)
                    dws[2 * j + i_h] = part if r == 0 else dws[2 * j + i_h] + part
            dvn_rows.append(jnp.concatenate(dvn_cols, axis=1))
            part = _split_dot(dmixed[CHUNK * r:CHUNK * (r + 1), :], expt_ref[...], 2)
            dbt = part if r == 0 else dbt + part
        dvn = jnp.concatenate(dvn_rows, axis=0)
        dvh = dvn * lnw
        dvgel = rstd * (dvh - _head_mean(dvh, avg) - vhat * _head_mean(dvh * vhat, avg))
        dv_ref[...] = (dvgel * dvg).astype(BF16)
        first = i == 0

        @pl.when(first)
        def _():
            for h in range(N_HEADS):
                dw_ref[h] = dws[h]
            db_ref[...] = dbt

        @pl.when(jnp.logical_not(first))
        def _():
            for h in range(N_HEADS):
                dw_ref[h] += dws[h]
            db_ref[...] += dbt

        _acc_rows(dlnw_ref, _rsum(dvn * vhat), first)
        _acc_rows(dlnb_ref, _rsum(dvn), first)

    row = pl.BlockSpec((tm, GM_WIDTH), lambda i: (i, 0))
    consts = [lnw, lnb, wcat, wtcat, bias, avg, expand_t]
    return pl.pallas_call(
        body, name="gmlp_bwd", grid=(t_tok // tm,),
        out_shape=(jax.ShapeDtypeStruct((t_tok, GM_WIDTH), BF16), jax.ShapeDtypeStruct((t_tok, GM_WIDTH), BF16),
                   jax.ShapeDtypeStruct((N_HEADS, CHUNK, CHUNK), F32), jax.ShapeDtypeStruct((CHUNK, CHUNK), F32),
                   jax.ShapeDtypeStruct((1, GM_WIDTH), F32), jax.ShapeDtypeStruct((1, GM_WIDTH), F32)),
        in_specs=[row, row, row] + [_full(a.shape) for a in consts],
        out_specs=(row, row, _full((N_HEADS, CHUNK, CHUNK)), _full((CHUNK, CHUNK)), _full((1, GM_WIDTH)),
                   _full((1, GM_WIDTH))),
        compiler_params=_params("arbitrary"))(dmix, u, v, *consts)


def _ssd_bwd(dmix, z, xbc, pre, dtr, y, states, cw, cb, dtb, alog, dskip_exp, nw, expand, expand_t, tril, triu, seq,
             dep=None):
    t_tok = z.shape[0]
    nb, nc, row, _, states_spec, fold, unfold = _ssd_specs(t_tok, seq, True)
    q = CHUNK

    def one_sequence(s, dm_ref, z_ref, xbc_ref, pre_ref, dtr_ref, y_ref, st_ref, cw_ref, dtb_ref, alog_ref, dsk_ref,
                     nw_ref, exp_ref, expt_ref, tril_ref, triu_ref, dz_ref, dxbc_ref, ddt_ref, dhead_ref, dstate_ref):
        m_l, m_r = _lane_masks()
        expt = expt_ref[...]
        f = _ssd_common(pre_ref[s], dtr_ref[s], dtb_ref[...], alog_ref[...], exp_ref[...], tril_ref[...])
        act, pre, sg = f["act"], f["pre"], f["sg"]
        xs = act[:, :SSM_WIDTH]
        xdt = xs * f["dt_exp"]
        xw = xdt * f["w_end"]
        state = st_ref[s, 0]
        dstate = dstate_ref[s]
        zv, yv, dout, nw = z_ref[s].astype(F32), y_ref[s], dm_ref[s].astype(F32), nw_ref[...]
        sz = jax.nn.sigmoid(zv)
        sl = zv * sz
        yg = yv * sl
        tv = dout * nw
        dyg_parts, ygh_parts = [], []
        for g in range(2):
            ygg = yg[:, 256 * g:256 * (g + 1)]
            rr = lax.rsqrt(jnp.mean(ygg * ygg, axis=-1, keepdims=True) + EPS)
            ygh = ygg * rr
            tg = tv[:, 256 * g:256 * (g + 1)]
            dyg_parts.append(rr * (tg - ygh * jnp.mean(tg * ygh, axis=-1, keepdims=True)))
            ygh_parts.append(ygh)
        dyg = jnp.concatenate(dyg_parts, axis=1)
        dnw = _rsum(dout * jnp.concatenate(ygh_parts, axis=1))
        dy = dyg * sl
        dz_ref[s] = (dyg * yv * (sz * (1.0 + zv * (1.0 - sz)))).astype(BF16)
        ddsk = _rsum(dy * xs)
        dye = dy * f["e"]
        lane = lax.broadcasted_iota(jnp.int32, (q, q), 1)
        sub = lax.broadcasted_iota(jnp.int32, (q, q), 0)
        rs_mat = jnp.zeros((q, q), F32)
        cs_mat = jnp.zeros((q, q), F32)
        dxdt_cols, yoff, dst_in, dxw, d_b, d_c = [], [], [], [], [], []
        for g in range(2):
            bg = act[:, 512 + 128 * g:640 + 128 * g].astype(BF16)
            cg = act[:, 768 + 128 * g:896 + 128 * g].astype(BF16)
            cb_mat = _dot(cg, bg, _NT)
            stg = state[:, 256 * g:256 * (g + 1)].astype(BF16)
            dyeg = dye[:, 256 * g:256 * (g + 1)].astype(BF16)
            yoff.append(_dot(cg, stg))
            dcg = _dot(dyeg, stg, _NT)
            dst_in.append(_dot(cg, dyeg, _TN))
            dcb = jnp.zeros((q, q), F32)
            for pr in range(2):
                h0 = 4 * g + 2 * pr
                gf = [cb_mat * f["decay"][h0], cb_mat * f["decay"][h0 + 1]]
                gcat = jnp.concatenate([gf[0].astype(BF16), gf[1].astype(BF16)], axis=1)
                xst = _stack_pair(xdt[:, 64 * h0:64 * h0 + 128], m_l, m_r)
                dyp = dy[:, 64 * h0:64 * h0 + 128].astype(BF16)
                dgcat = _dot(dyp, xst, _NT)
                dxst = _dot(gcat, dyp, _TN)
                dxdt_cols.append(dxst[:q] * m_l + dxst[q:] * m_r)
                for i in range(2):
                    h = h0 + i
                    dg = dgcat[:, q * i:q * (i + 1)]
                    mm = dg * gf[i]
                    rs_mat = rs_mat + jnp.where(lane == h, jnp.sum(mm, axis=1, keepdims=True), 0.0)
                    cs_mat = cs_mat + jnp.where(sub == h, jnp.sum(mm, axis=0, keepdims=True), 0.0)
                    dcb = dcb + dg * f["decay"][h]
            dcb16 = dcb.astype(BF16)
            dstg = dstate[:, 256 * g:256 * (g + 1)].astype(BF16)
            d_c.append(dcg + _dot(dcb16, bg))
            dxw.append(_dot(bg, dstg))
            d_b.append(_dot(dcb16, cg, _TN) + _dot(xw[:, 256 * g:256 * (g + 1)].astype(BF16), dstg, _NT))
        dxw = jnp.concatenate(dxw, axis=1)
        dxdt = jnp.concatenate(dxdt_cols, axis=1) + dxw * f["w_end"]
        qv = dxw * xw
        end_row = _rsum(qv) + _rsum(dstate * state) * f["cd"]
        x2 = dye * jnp.concatenate(yoff, axis=1) - qv
        row_i = lax.broadcasted_iota(jnp.int32, (q, 1), 0)
        x2 = x2 + jnp.where(row_i == q - 1, end_row, 0.0)
        da_cs = _split_dot(x2, expt, 2) + rs_mat - cs_mat.T
        ddt = _split_dot(dxdt * xs, expt, 2)
        dxs = dsk_ref[...] * dy + dxdt * f["dt_exp"]
        dda = _split_dot_left(triu_ref[...], da_cs, 3)
        ddt = ddt + dda * f["a_row"]
        dalog = _rsum(dda * f["dt"]) * f["a_row"]
        draw = ddt * jax.nn.sigmoid(f["dtp"])
        ddt_ref[s] = draw.astype(BF16)
        dact = jnp.concatenate([dxs] + d_b + d_c, axis=1)
        dpre = dact * (sg * (1.0 + pre * (1.0 - sg)))
        dhead = dhead_ref[s]
        xv = xbc_ref[s]
        shifted = [_shift_rows(dpre, dhead, 3 - k, False) for k in range(3)] + [dpre]
        dxbc = cw_ref[3:4, :] * dpre
        for k in range(3):
            dxbc = dxbc + cw_ref[k:k + 1, :] * shifted[k]
        dxbc_ref[s] = dxbc.astype(BF16)
        dhead_ref[s] = dpre[0:8, :]
        dstate_ref[s] = dstate * f["cd"] + jnp.concatenate(dst_in, axis=1)
        row8 = lax.broadcasted_iota(jnp.int32, (8, 1), 0)
        dcw = jnp.zeros((8, CONV_CH), F32)
        for k in range(4):
            dcw = dcw + jnp.where(row8 == k, _rsum(shifted[k] * xv), 0.0)
        return dcw, _rsum(dpre), _rsum(draw), dalog, _split_dot(ddsk, expt, 3), dnw

    def body(dm_ref, z_ref, xbc_ref, pre_ref, dtr_ref, y_ref, st_ref, cw_ref, cb_ref, dtb_ref, alog_ref, dsk_ref,
             nw_ref, exp_ref, expt_ref, tril_ref, triu_ref, dz_ref, dxbc_ref, ddt_ref, dcw_ref, dcb_ref, ddtb_ref,
             dalog_ref, dd_ref, dnw_ref, dhead_ref, dstate_ref):
        c = pl.program_id(0)
        first = c == 0

        @pl.when(first)
        def _():
            dstate_ref[...] = jnp.zeros_like(dstate_ref)
            dhead_ref[...] = jnp.zeros_like(dhead_ref)

        total = None
        for s in range(nb):
            parts = one_sequence(s, dm_ref, z_ref, xbc_ref, pre_ref, dtr_ref, y_ref, st_ref, cw_ref, dtb_ref, alog_ref,
                                 dsk_ref, nw_ref, exp_ref, expt_ref, tril_ref, triu_ref, dz_ref, dxbc_ref, ddt_ref,
                                 dhead_ref, dstate_ref)
            total = parts if total is None else tuple(a + b for a, b in zip(total, parts))
        dcw = total[0]

        @pl.when(first)
        def _():
            dcw_ref[...] = dcw

        @pl.when(jnp.logical_not(first))
        def _():
            dcw_ref[...] += dcw

        for ref, part in zip((dcb_ref, ddtb_ref, dalog_ref, dd_ref, dnw_ref), total[1:]):
            _acc_rows(ref, part, first)

    consts = [cw, cb, dtb, alog, dskip_exp, nw, expand, expand_t, tril, triu]
    deps = [] if dep is None else [dep]
    n_in = 7 + len(consts)

    def body_skipping_dep(*refs):
        body(*refs[:n_in], *refs[n_in + len(deps):])

    acc = lambda n: jax.ShapeDtypeStruct((1, n), F32)
    sd = lambda n: jax.ShapeDtypeStruct((nb, seq, n), BF16)
    dz, dxbc, ddt, *small_grads = pl.pallas_call(
        body_skipping_dep, name="ssd_bwd", grid=(nc,),
        out_shape=(sd(SSM_WIDTH), sd(CONV_CH), sd(CHUNK), jax.ShapeDtypeStruct((8, CONV_CH), F32), acc(CONV_CH),
                   acc(CHUNK), acc(CHUNK), acc(CHUNK), acc(SSM_WIDTH)),
        in_specs=[row(SSM_WIDTH, col=1), row(SSM_WIDTH), row(CONV_CH), row(CONV_CH), row(CHUNK), row(SSM_WIDTH),
                  states_spec]
        + [_full(a.shape) for a in consts] + [pl.BlockSpec(memory_space=pl.ANY)] * len(deps),
        out_specs=(row(SSM_WIDTH), row(CONV_CH), row(CHUNK), _full((8, CONV_CH)), _full((1, CONV_CH)),
                   _full((1, CHUNK)), _full((1, CHUNK)), _full((1, CHUNK)), _full((1, SSM_WIDTH))),
        scratch_shapes=[pltpu.VMEM((nb, 8, CONV_CH), F32), pltpu.VMEM((nb, N_STATE, SSM_WIDTH), F32)],
        compiler_params=_params("arbitrary"))(
            fold(dmix), fold(z), fold(xbc), fold(pre), fold(dtr), fold(y), states, *consts, *deps)
    return (unfold(dz), unfold(dxbc), unfold(ddt), *small_grads)


def _in_bwd(du, dv, dz, dxbc, ddt, w_in, x, dx2, g1, tm, me, riders=(), dep=None):
    t_tok = x.shape[0]
    steps = t_tok // tm

    n_in = [5 + ("mask" in rd) for rd in riders]
    first_in = [sum(n_in[:r]) for r in range(len(riders))]

    def body(me_ref, du_ref, dv_ref, dz_ref, dxbc_ref, ddt_ref, w_ref, x_ref, dx2_ref, g_ref, *rest):
        outs = rest[len(rest) - 2 - 4 * len(riders):]
        gx_ref, dg_ref = outs[:2]
        i = pl.program_id(0)
        dh = None
        for (a, b), ref in zip(_IN_SPLITS, (du_ref, dv_ref, dz_ref, dxbc_ref, ddt_ref)):
            part = _dot(ref[...], w_ref[a:b, :])
            dh = part if dh is None else dh + part
        dn, dg = _rms_bwd(x_ref[...], g_ref[...], dh)
        gx_ref[...] = dx2_ref[...] + dn
        _acc_rows(dg_ref, dg, i == 0)
        for r in range(len(riders)):
            p_ref, own_ref, w_ref_r, m_ref_r, v_ref_r = rest[first_in[r]:first_in[r] + 5]
            g = _sum_parts(me_ref[0], p_ref, own_ref[0])
            if n_in[r] == 6:
                g = g * rest[first_in[r] + 5][...]
            d, mn, vn = _adamw_math(w_ref_r[...], g, m_ref_r[...], v_ref_r[...])
            for o_ref, val in zip(outs[2 + 4 * r:6 + 4 * r], (g, d, mn, vn)):
                o_ref[...] = val

    row = lambda n: pl.BlockSpec((tm, n), lambda i, me_ref: (i, 0))
    whole = lambda shape: pl.BlockSpec(shape, lambda i, me_ref: (0,) * len(shape))
    widths = [b - a for a, b in _IN_SPLITS]
    deps = [] if dep is None else [dep]
    rider_args, rider_specs, rider_out_shapes, rider_out_specs = [], [], [], []
    for rd in riders:
        rows, cols = rd["w"].shape[0] // steps, rd["w"].shape[1]
        blk = pl.BlockSpec((rows, cols), lambda i, me_ref: (i, 0))
        rider_args += [rd["parts"], rd["own"], rd["w"], rd["m"], rd["v"]]
        rider_specs += [pl.BlockSpec((N_DEV, rows, cols), lambda i, me_ref: (0, i, 0)),
                        pl.BlockSpec((1, rows, cols), lambda i, me_ref: (me_ref[0], i, 0)), blk, blk, blk]
        if "mask" in rd:
            rider_args.append(rd["mask"])
            rider_specs.append(whole((rows, cols)))
        rider_out_shapes += [jax.ShapeDtypeStruct(rd["w"].shape, F32)] * 4
        rider_out_specs += [blk] * 4
    outs = pl.pallas_call(
        body, name="in_bwd",
        out_shape=(jax.ShapeDtypeStruct((t_tok, D_MODEL), F32), jax.ShapeDtypeStruct((1, D_MODEL), F32),
                   *rider_out_shapes),
        grid_spec=pltpu.PrefetchScalarGridSpec(
            num_scalar_prefetch=1, grid=(steps,),
            in_specs=[row(n) for n in widths] + [whole((IN_PAD, D_MODEL)), row(D_MODEL), row(D_MODEL),
                                                 whole((1, D_MODEL))] + rider_specs
            + [pl.BlockSpec(memory_space=pl.ANY)] * len(deps),
            out_specs=(row(D_MODEL), whole((1, D_MODEL)), *rider_out_specs)),
        compiler_params=_params("arbitrary"))(me, du, dv, dz, dxbc, ddt, w_in, x, dx2, g1, *rider_args, *deps)
    return outs[0], outs[1], [tuple(outs[2 + 4 * r:6 + 4 * r]) for r in range(len(riders))]


def _pad_lanes(a, n):
    return jnp.pad(a, ((0, 0), (0, n - a.shape[1])))


def _local_step(x, target, seq, small, hooks, first_dep=None):
    t_tok = x.shape[0]
    tm = min(TOKEN_TILE, t_tok)
    avg, expand, expand_t, tril, triu = _const_mats()
    g1, g2, g3, g4 = (small[k].reshape(1, D_MODEL) for k in
                      ("norm_mix_pre", "norm_mix_post", "norm_ffn_pre", "norm_ffn_post"))
    tie = (lambda a: a) if first_dep is None else (lambda a: a + first_dep[0, 0])
    lnw = tie(small["gm_ln_w"]).reshape(1, GM_WIDTH)
    lnb = tie(small["gm_ln_b"]).reshape(1, GM_WIDTH)
    causal = jnp.tril(jnp.ones((CHUNK, CHUNK), F32))
    wm = tie(small["gm_w_s"]) * causal
    pair = lambda w: w.reshape(4, 2, CHUNK, CHUNK).transpose(0, 2, 1, 3).reshape(4, CHUNK, 2 * CHUNK).astype(BF16)
    wcat = pair(wm)
    wtcat = pair(jnp.swapaxes(wm, 1, 2))
    bias = jnp.repeat(tie(small["gm_b_s"]).T, HEAD_DIM, axis=1)
    cb = small["conv_b"].reshape(1, CONV_CH)
    dtb = _pad_lanes(tie(small["dt_bias"]).reshape(1, N_HEADS), CHUNK)
    alog = _pad_lanes(tie(small["a_log"]).reshape(1, N_HEADS), CHUNK)
    dskip_exp = jnp.repeat(tie(small["d_skip"]).reshape(1, N_HEADS), HEAD_DIM, axis=1)
    nw = small["ssm_norm_w"].reshape(1, SSM_WIDTH)

    h1 = _prenorm(x, g1, tm, hooks.get("prenorm_after", first_dep))
    w_in_t, conv_w = hooks["mixer_weights"](h1)
    tall = min(2 * tm, t_tok)
    u, v, z, xbc, dtr = _in_proj(h1, w_in_t, tall)
    mix_a = _gmlp_fwd(u, v, lnw, lnb, wcat, bias, avg)
    dep = hooks["gmlp_done"](mix_a) if "gmlp_done" in hooks else None
    mix_b, y_pre, states, pre = _ssd_fwd(z, xbc, dtr, conv_w, cb, dtb, alog, dskip_exp, nw, expand, tril, seq, dep)
    w_out, dep = hooks["mixers_done"](mix_b)
    o, x2, h3 = _out_proj(mix_a, mix_b, w_out, x, g2, g3, tall, dep)
    w_up, w_down = hooks["mlp_weights"](h3)
    tf = FF_TILE
    ra, dd, dy, dg4, loss = _mlp_fwd(h3, w_up, w_down, x2, target, g4, tm, tf)

    da, dx2, do, dg3, dg2 = _mlp_bwd(dd, w_down, ra, w_up, x2, dy, o, g3, g2, tm, tf)
    g_w_down = _wgrad(ra, dd, None, WGRAD_TILE, D_MODEL, t_tok, True, "wgrad_down")
    g_w_up = _wgrad(h3, da, N_DEV, D_MODEL, D_FF // N_DEV, t_tok, False, "wgrad_up")
    dep = hooks["mlp_grads"](g_w_down, g_w_up)
    dmix = _dmix(do, w_out, tall, dep)
    g_w_out = _wgrad_pieces(do, (mix_a, mix_b), WGRAD_TILE, "wgrad_out", dep)
    du, dv, dws, dbt, dlnw, dlnb = _gmlp_bwd(dmix, u, v, lnw, lnb, wcat, wtcat, bias, avg, expand_t)
    dep = hooks["gmlp_grads"](g_w_out, dws)
    dz, dxbc, ddt, dcw, dcb, ddtb, dalog, ddsk, dnw = _ssd_bwd(
        dmix, z, xbc, pre, dtr, y_pre, states, conv_w, cb, dtb, alog, dskip_exp, nw, expand, expand_t, tril, triu, seq,
        dep)
    g_w_in = _wgrad_in_chunked(h1, (du, dv, dz, dxbc, ddt), WGRAD_TILE, t_tok // 2, dep)
    dep = hooks["in_grads"](g_w_in, dcw[0:4])
    riders = hooks["arrived_updates"](dep) if "arrived_updates" in hooks else []
    me = hooks.get("me", jnp.zeros((1,), jnp.int32))
    grad_x, dg1, updates = _in_bwd(du, dv, dz, dxbc, ddt, w_in_t, x, dx2, g1, tm, me, riders, dep)

    grads = dict(
        updates=updates,
        w_in=g_w_in, w_out=g_w_out, w_up=g_w_up, w_down=g_w_down, conv_w=dcw[0:4],
        norm_mix_pre=dg1, norm_mix_post=dg2, norm_ffn_pre=dg3, norm_ffn_post=dg4, gm_ln_w=dlnw, gm_ln_b=dlnb,
        gm_w_s=dws, gm_b_s=dbt, conv_b=dcb, dt_bias=ddtb, a_log=dalog, d_skip=ddsk, ssm_norm_w=dnw)
    return loss[0, 0], grad_x, grads


_WEIGHTS = ("norm_mix_pre", "w_in", "gm_ln_w", "gm_ln_b", "gm_w_s", "gm_b_s", "conv_w", "conv_b", "dt_bias", "a_log",
            "d_skip", "ssm_norm_w", "w_out", "norm_mix_post", "norm_ffn_pre", "w_up", "w_down", "norm_ffn_post")
_SLAB_ROWS = (("norm_mix_pre", 1024), ("norm_mix_post", 1024), ("norm_ffn_pre", 1024), ("norm_ffn_post", 1024),
              ("conv_b", 1024), ("ssm_norm_w", 512), ("gm_ln_w", 512), ("gm_ln_b", 512), ("dt_bias", 8), ("a_log", 8),
              ("d_skip", 8))
_SLAB_LOSS_ROW = len(_SLAB_ROWS)
_SLAB_BS_ROW = 16
_SMALL_PARAMS = tuple(name for name, _ in _SLAB_ROWS) + ("gm_b_s",)
_LN_PARAMS = ("gm_ln_w", "gm_ln_b")


_SLAB_CONV_ROW = _SLAB_LOSS_ROW + 1


def _pack_slab(g, loss_part):
    rows = [_pad_lanes(g[name], D_MODEL) for name, _ in _SLAB_ROWS]
    rows.append(jnp.broadcast_to(loss_part, (1, D_MODEL)))
    rows.append(g["conv_w"])
    assert sum(r.shape[0] for r in rows) == _SLAB_BS_ROW
    rows.append(_pad_lanes(g["gm_b_s"].T[0:N_HEADS], D_MODEL))
    return jnp.concatenate(rows, axis=0)


def _adamw_slab(parts, me, w, m, v):
    names = _SMALL_PARAMS + ("conv_w",)
    shapes = [w[k].shape for k in names]
    unfold = np.zeros((GM_WIDTH, HEAD_DIM), np.float32)
    for h in range(N_HEADS):
        unfold[h * HEAD_DIM:(h + 1) * HEAD_DIM, :] = np.eye(HEAD_DIM)
    unfold = jnp.asarray(unfold, dtype=BF16)
    n = len(names)
    shard = CONV_CH // N_DEV

    def body(me_ref, p_ref, unfold_ref, *refs):
        w_refs, m_refs, v_refs = refs[:n], refs[n:2 * n], refs[2 * n:3 * n]
        outs = refs[3 * n:]
        g_all = p_ref[0]
        for j in range(1, N_DEV):
            g_all = g_all + p_ref[j]
        lane = lax.broadcasted_iota(jnp.int32, (N_HEADS, GM_WIDTH), 1)
        head = lax.broadcasted_iota(jnp.int32, (N_HEADS, GM_WIDTH), 0)
        own_lanes = jnp.logical_and(lane >= head * HEAD_DIM, lane < (head + 1) * HEAD_DIM)
        mine = pl.ds(pl.multiple_of(me_ref[0] * shard, shard), shard)
        for i, name in enumerate(names):
            if name == "gm_b_s":
                g = g_all[_SLAB_BS_ROW:_SLAB_BS_ROW + N_HEADS, 0:CHUNK]
            elif name == "conv_w":
                g = p_ref[0, _SLAB_CONV_ROW:_SLAB_CONV_ROW + 4, mine]
                for j in range(1, N_DEV):
                    g = g + p_ref[j, _SLAB_CONV_ROW:_SLAB_CONV_ROW + 4, mine]
            else:
                row = [r for r, (k, _) in enumerate(_SLAB_ROWS) if k == name][0]
                g = g_all[row:row + 1, 0:dict(_SLAB_ROWS)[name]]
                if name in _LN_PARAMS:
                    g = _split_dot(jnp.where(own_lanes, g, 0.0), unfold_ref[...], 3)
            d, mn, vn = _adamw_math(w_refs[i][...], g, m_refs[i][...], v_refs[i][...])
            for o_ref, val in zip(outs[4 * i:4 * i + 4], (g, d, mn, vn)):
                o_ref[...] = val
        outs[-1][...] = g_all[_SLAB_LOSS_ROW:_SLAB_LOSS_ROW + 1, 0:128]

    def whole(shape):
        nd = len(shape)
        return pl.BlockSpec(shape, lambda i, me_ref: (0,) * nd)

    ins = [parts, unfold] + [d[k] for d in (w, m, v) for k in names]
    out_shape = tuple(jax.ShapeDtypeStruct(s, F32) for s in shapes for _ in range(4)) + (
        jax.ShapeDtypeStruct((1, 128), F32),)
    outs = pl.pallas_call(
        body, name="adamw_small", out_shape=out_shape,
        grid_spec=pltpu.PrefetchScalarGridSpec(
            num_scalar_prefetch=1, grid=(1,), in_specs=[whole(a.shape) for a in ins],
            out_specs=tuple(whole(s.shape) for s in out_shape)),
        compiler_params=_params("arbitrary"))(me, *ins)
    return {k: tuple(outs[4 * i:4 * i + 4]) for i, k in enumerate(names)}, outs[-1][0, 0]


def kernel(x, norm_mix_pre, w_in, gm_ln_w, gm_ln_b, gm_w_s, gm_b_s, conv_w, conv_b, dt_bias, a_log, d_skip, ssm_norm_w, w_out, norm_mix_post, norm_ffn_pre, w_up, w_down, norm_ffn_post, loss_target, m_norm_mix_pre, m_w_in, m_gm_ln_w, m_gm_ln_b, m_gm_w_s, m_gm_b_s, m_conv_w, m_conv_b, m_dt_bias, m_a_log, m_d_skip, m_ssm_norm_w, m_w_out, m_norm_mix_post, m_norm_ffn_pre, m_w_up, m_w_down, m_norm_ffn_post, v_norm_mix_pre, v_w_in, v_gm_ln_w, v_gm_ln_b, v_gm_w_s, v_gm_b_s, v_conv_w, v_conv_b, v_dt_bias, v_a_log, v_d_skip, v_ssm_norm_w, v_w_out, v_norm_mix_post, v_norm_ffn_pre, v_w_up, v_w_down, v_norm_ffn_post):
    w = dict(norm_mix_pre=norm_mix_pre, w_in=w_in, gm_ln_w=gm_ln_w, gm_ln_b=gm_ln_b, gm_w_s=gm_w_s, gm_b_s=gm_b_s, conv_w=conv_w, conv_b=conv_b, dt_bias=dt_bias, a_log=a_log, d_skip=d_skip, ssm_norm_w=ssm_norm_w, w_out=w_out, norm_mix_post=norm_mix_post, norm_ffn_pre=norm_ffn_pre, w_up=w_up, w_down=w_down, norm_ffn_post=norm_ffn_post)
    m = dict(norm_mix_pre=m_norm_mix_pre, w_in=m_w_in, gm_ln_w=m_gm_ln_w, gm_ln_b=m_gm_ln_b, gm_w_s=m_gm_w_s, gm_b_s=m_gm_b_s, conv_w=m_conv_w, conv_b=m_conv_b, dt_bias=m_dt_bias, a_log=m_a_log, d_skip=m_d_skip, ssm_norm_w=m_ssm_norm_w, w_out=m_w_out, norm_mix_post=m_norm_mix_post, norm_ffn_pre=m_norm_ffn_pre, w_up=m_w_up, w_down=m_w_down, norm_ffn_post=m_norm_ffn_post)
    v = dict(norm_mix_pre=v_norm_mix_pre, w_in=v_w_in, gm_ln_w=v_gm_ln_w, gm_ln_b=v_gm_ln_b, gm_w_s=v_gm_w_s, gm_b_s=v_gm_b_s, conv_w=v_conv_w, conv_b=v_conv_b, dt_bias=v_dt_bias, a_log=v_a_log, d_skip=v_d_skip, ssm_norm_w=v_ssm_norm_w, w_out=v_w_out, norm_mix_post=v_norm_mix_post, norm_ffn_pre=v_norm_ffn_pre, w_up=v_w_up, w_down=v_w_down, norm_ffn_post=v_norm_ffn_post)
    n_batch, seq, _ = x.shape
    shard_in = IN_COLS // N_DEV

    me = (4 * lax.axis_index("x") + 2 * lax.axis_index("y") + lax.axis_index("c")).astype(jnp.int32).reshape(1)

    def in_slot(own):
        return lax.dynamic_update_slice(lax.empty((N_DEV,) + own.shape, own.dtype), own[None],
                                        (me[0],) + (0,) * own.ndim)

    w_in_sh = w_in[0].T
    first = [_cast_to_slot(w_in_sh, me, shard_in, "cast_w_in"), in_slot(conv_w[0])]
    ici_1, tok_ici_1 = _exchange_start(first, [True] * 2, _SAME_CORE_PEERS, "gather_mix_ici_start")
    cast_out = _cast_to_slot(w_out[0], me, 128, "cast_w_out", dep=tok_ici_1)
    cast_up = _cast_to_slot(w_up[0], me, 1024, "cast_w_up", cols=True, dep=cast_out)
    second = [cast_out, cast_up, _cast_to_slot(w_down[0], me, 512, "cast_w_down", dep=cast_up)]
    gathering = {}

    def mixer_weights(after):
        bufs = [buf for buf, _ in _exchange_wait(ici_1, after, "gather_mix_ici_wait")]
        d2d_1, tok_d2d_1 = _exchange_start(bufs, [True] * 2, _SIBLING_FORWARD, "gather_mix_d2d_start")
        gathering["late_ici"], tok_ici_2 = _exchange_start(
            second, [True] * 3, _SAME_CORE_PEERS, "gather_late_ici_start", dep=tok_d2d_1)
        (_, ag_in), (_, ag_conv) = _exchange_wait(d2d_1, tok_ici_2, "gather_mix_d2d_wait")
        w_in_t = jnp.pad(ag_in.reshape(IN_COLS, D_MODEL), ((0, IN_PAD - IN_COLS), (0, 0)))
        return w_in_t, ag_conv.transpose(1, 0, 2).reshape(4, CONV_CH)

    def gmlp_done(after):
        ((buf, _),) = _exchange_wait(gathering["late_ici"], after, "gather_out_ici_wait", only=(0,))
        gathering["out"], tok = _exchange_start([buf], [True], _SIBLING_FORWARD, "gather_out_d2d_start")
        return tok

    def mixers_done(after):
        bufs = [buf for buf, _ in _exchange_wait(gathering["late_ici"], after, "gather_mlp_ici_wait", only=(1, 2))]
        gathering["mlp"], tok = _exchange_start(bufs, [True] * 2, _SIBLING_FORWARD, "gather_mlp_d2d_start")
        ((_, ag_out),) = _exchange_wait(gathering["out"], tok, "gather_out_d2d_wait")
        return ag_out.reshape(D_MODEL, D_MODEL), tok

    def mlp_weights(after):
        (_, ag_up), (_, ag_down) = _exchange_wait(gathering["mlp"], after, "gather_mlp_d2d_wait")
        return ag_up, ag_down.reshape(D_FF, D_MODEL)

    sent = {}

    def mlp_grads(g_w_down, g_w_up):
        sent["mlp"], tok = _exchange_start(
            [g_w_down.reshape(N_DEV, D_FF // N_DEV, D_MODEL), g_w_up], [False, False], _ALL_PEERS, "grads_mlp_start")
        return tok

    def gmlp_grads(g_w_out, g_w_s):
        sent["gmlp"], tok = _exchange_start(
            [g_w_out.reshape(N_DEV, D_MODEL // N_DEV, D_MODEL), in_slot(g_w_s.astype(BF16))], [False, True], _ALL_PEERS,
            "grads_gmlp_start")
        return tok

    def in_grads(g_w_in_t, g_conv_w):
        g_in_blk = g_w_in_t[:IN_COLS].reshape(N_DEV, shard_in, D_MODEL)
        sent["in"], tok = _exchange_start([g_in_blk], [False], _ALL_PEERS, "grads_in_start")
        return tok

    def arrived_updates(after):
        (own_down, p_down), (own_up, p_up) = _exchange_wait(sent["mlp"], after, "grads_mlp_wait")
        (own_out, p_out), (_, p_ws) = _exchange_wait(sent["gmlp"], own_up, "grads_gmlp_wait")
        rows = lambda t: t.reshape(t.shape[:-3] + (N_HEADS * CHUNK, CHUNK))
        return [dict(parts=p_up, own=own_up, w=w_up[0], m=m_w_up[0], v=v_w_up[0]),
                dict(parts=p_down, own=own_down, w=w_down[0], m=m_w_down[0], v=v_w_down[0]),
                dict(parts=p_out, own=own_out, w=w_out[0], m=m_w_out[0], v=v_w_out[0]),
                dict(parts=rows(p_ws), own=rows(p_ws), w=rows(gm_w_s[0]), m=rows(m_gm_w_s[0]), v=rows(v_gm_w_s[0]),
                     mask=jnp.tril(jnp.ones((CHUNK, CHUNK), F32)))]

    small = {k: w[k][0] for k in _SMALL_PARAMS + ("gm_w_s",)}
    loss_part, grad_x, g = _local_step(
        x.reshape(n_batch * seq, D_MODEL), loss_target.reshape(n_batch * seq, D_MODEL), seq, small,
        dict(mixer_weights=mixer_weights, gmlp_done=gmlp_done, mixers_done=mixers_done, mlp_weights=mlp_weights,
             mlp_grads=mlp_grads, gmlp_grads=gmlp_grads, in_grads=in_grads, arrived_updates=arrived_updates, me=me,
             prenorm_after=second[2]), first_dep=tok_ici_1)

    sent_rows, tok_rows = _exchange_start([in_slot(_pack_slab(g, loss_part))], [True], _ALL_PEERS, "grads_rows_start")
    res = dict(zip(("w_up", "w_down", "w_out", "gm_w_s"), g["updates"]))
    ((own_in, p_in),) = _exchange_wait(sent["in"], tok_rows, "grads_in_wait")
    lying = lambda t: jnp.transpose(t, (2, 0, 1))
    g_in = _sum_grads(p_in, own_in, me, shard_in, "sum_w_in_grads").reshape(shard_in, 1, D_MODEL)
    upd_in = _adamw_rows(g_in, lying(w_in), lying(m_w_in), lying(v_w_in), shard_in // _W_IN_UPDATE_STEPS, "adamw_w_in")
    res["w_in"] = tuple(jnp.transpose(t, (1, 2, 0)) for t in upd_in)
    ((_, p_rows),) = _exchange_wait(sent_rows, upd_in[1], "grads_rows_wait")
    flat = lambda t: t[0] if t.ndim == 3 else t
    small_res, loss = _adamw_slab(
        p_rows, me, *({k: flat(d[k]) for k in _SMALL_PARAMS + ("conv_w",)} for d in (w, m, v)))
    res.update(small_res)
    res = {k: tuple(r.reshape(w[k].shape) for r in res[k]) for k in _WEIGHTS}

    outs = [loss, grad_x.reshape(x.shape)]
    for part in range(4):
        outs.extend(res[k][part] for k in _WEIGHTS)
    return tuple(outs)
```

```python
import functools

import jax
import jax.numpy as jnp
import numpy as np
from jax import lax
from jax.experimental import pallas as pl
from jax.experimental.pallas import tpu as pltpu

F32 = jnp.float32
BF16 = jnp.bfloat16

D_MODEL = 1024
GM_WIDTH = 512
SSM_WIDTH = 512
CONV_CH = 1024
N_HEADS = 8
HEAD_DIM = 64
N_STATE = 128
CHUNK = 128
D_FF = 4096
IN_COLS = 2568
IN_PAD = 2688
N_DEV = 8
EPS = 1e-6
ADAM_LR, ADAM_B1, ADAM_B2, ADAM_EPS, ADAM_WD, ADAM_STEP = 0.001, 0.9, 0.999, 1e-08, 0.01, 10
VMEM_LIMIT_BYTES = 56 * 1024 * 1024
TOKEN_TILE = 512
FF_TILE = 2048
WGRAD_TILE = 512
_NT = (((1,), (1,)), ((), ()))
_TN = (((0,), (0,)), ((), ()))


def _params(*sem):
    return pltpu.CompilerParams(dimension_semantics=sem or None, vmem_limit_bytes=VMEM_LIMIT_BYTES)


def _dot(a, b, dims=None):
    if dims is None:
        return jnp.dot(a, b, preferred_element_type=F32)
    return lax.dot_general(a, b, dims, preferred_element_type=F32)


def _split_terms(x, terms):
    out, rem = [], x
    for i in range(terms):
        hi = rem.astype(BF16)
        out.append(hi)
        if i + 1 < terms:
            rem = rem - hi.astype(F32)
    return out


def _split_dot(x, m, terms):
    acc = None
    for hi in _split_terms(x, terms):
        part = _dot(hi, m)
        acc = part if acc is None else acc + part
    return acc


def _split_dot_left(m, x, terms):
    acc = None
    for hi in _split_terms(x, terms):
        part = _dot(m, hi)
        acc = part if acc is None else acc + part
    return acc


def _gelu_and_grad(x):
    c = 0.7978845608028654
    inner = c * (x + 0.044715 * x * x * x)
    t = jnp.tanh(inner)
    g = 0.5 * x * (1.0 + t)
    dg = 0.5 * (1.0 + t) + 0.5 * x * (1.0 - t * t) * c * (1.0 + 3.0 * 0.044715 * x * x)
    return g, dg


def _softplus(x):
    return jnp.maximum(x, 0.0) + jnp.log(1.0 + jnp.exp(-jnp.abs(x)))


def _rsum(x):
    return jnp.sum(x, axis=0, keepdims=True)


def _acc_rows(ref, part, first):
    val = jnp.broadcast_to(part, ref.shape)

    @pl.when(first)
    def _():
        ref[...] = val

    @pl.when(jnp.logical_not(first))
    def _():
        ref[...] += val


def _rms_bwd(n, g, dout):
    r = lax.rsqrt(jnp.mean(n * n, axis=-1, keepdims=True) + EPS)
    nh = n * r
    dg = dout * g
    dn = r * (dg - nh * jnp.mean(dg * nh, axis=-1, keepdims=True))
    return dn, _rsum(dout * nh)


def _const_mats():
    avg = np.kron(np.eye(4), np.full((HEAD_DIM, HEAD_DIM), 1.0 / HEAD_DIM))
    expand = np.zeros((CHUNK, SSM_WIDTH), np.float32)
    for h in range(N_HEADS):
        expand[h, h * HEAD_DIM:(h + 1) * HEAD_DIM] = 1.0
    tril = np.tril(np.ones((CHUNK, CHUNK), np.float32))
    as_bf16 = lambda a: jnp.asarray(a, dtype=BF16)
    return as_bf16(avg), as_bf16(expand), as_bf16(expand.T), as_bf16(tril), as_bf16(tril.T)


def _full(shape):
    nd = len(shape)
    return pl.BlockSpec(shape, lambda *_: (0,) * nd)


_HBM = pl.BlockSpec(memory_space=pltpu.HBM)
_SEM = pl.BlockSpec(memory_space=pltpu.SEMAPHORE)
_ALL_PEERS = tuple((k, 0) for k in range(1, N_DEV))
_SAME_CORE_PEERS = ((2, 0), (4, 0), (6, 0))
_SIBLING_FORWARD = ((1, 0), (1, 2), (1, 4), (1, 6))


def _flip(j, k):
    for bit in (4, 2, 1):
        if k & bit:
            j = j + bit - 2 * (j & bit)
    return j


def _copies(src, land, send_sems, recv_sems, hops, slots=None):
    x, y, c = lax.axis_index("x"), lax.axis_index("y"), lax.axis_index("c")
    me = 4 * x + 2 * y + c
    slots = range(len(src)) if slots is None else slots
    out = []
    for t in range(len(src)):
        for i, (k, b) in enumerate(hops):
            pos = (1 - x if k & 4 else x, 1 - y if k & 2 else y, 1 - c if k & 1 else c)
            peer = _flip(me, k)
            sem = slots[t] * len(hops) + i
            mk = functools.partial(pltpu.make_async_remote_copy, send_sem=send_sems.at[sem], recv_sem=recv_sems.at[sem],
                                   device_id=pos, device_id_type=pl.DeviceIdType.MESH)
            if land[t] is None and src[t].shape[0] != N_DEV:
                width = src[t].shape[1] // N_DEV
                slab = lambda j: src[t].at[:, pl.ds(pl.multiple_of(j * width, 128), width)]
                mine = functools.partial(mk, src_ref=slab(_flip(me, b)), dst_ref=slab(_flip(me, b)))
                theirs = functools.partial(mk, src_ref=slab(_flip(peer, b)), dst_ref=slab(_flip(peer, b)))
            elif land[t] is None:
                mine = functools.partial(mk, src_ref=src[t].at[_flip(me, b)], dst_ref=src[t].at[_flip(me, b)])
                theirs = functools.partial(mk, src_ref=src[t].at[_flip(peer, b)], dst_ref=src[t].at[_flip(peer, b)])
            else:
                assert b == 0
                mine = functools.partial(mk, src_ref=src[t].at[peer], dst_ref=land[t].at[me])
                theirs = functools.partial(mk, src_ref=src[t].at[peer], dst_ref=land[t].at[peer])
            out.append((mine, theirs))
    return out


def _exchange_start(srcs, inplace, peers, name, dep=None):
    n = len(srcs)
    lands = [None if ip else pltpu.with_memory_space_constraint(lax.empty(s.shape, s.dtype), pltpu.HBM)
             for s, ip in zip(srcs, inplace)]
    real_lands = [l for l in lands if l is not None]
    n_l = len(real_lands)
    deps = [] if dep is None else [dep]

    def body(*refs):
        src = refs[:n]
        land_refs = list(refs[n:n + n_l])
        send_sems, recv_sems = refs[n + n_l + len(deps)], refs[n + n_l + len(deps) + 1]
        token = refs[-1]
        land = [None if ip else land_refs.pop(0) for ip in inplace]
        for mine, _ in _copies(src, land, send_sems, recv_sems, peers):
            mine().start()
        token[...] = jnp.zeros_like(token)

    sem_t = pltpu.SemaphoreType.DMA((n * len(peers),))
    outs = pl.pallas_call(
        body, name=name,
        out_shape=(sem_t, sem_t) + tuple(pltpu.HBM(a.shape, a.dtype) for a in list(srcs) + real_lands)
        + (jax.ShapeDtypeStruct((8, 128), F32),),
        in_specs=[_HBM] * (n + n_l) + [pl.BlockSpec(memory_space=pl.ANY)] * len(deps),
        out_specs=(_SEM, _SEM) + (_HBM,) * (n + n_l) + (pl.BlockSpec(memory_space=pltpu.VMEM),),
        input_output_aliases={i: 2 + i for i in range(n + n_l)},
        compiler_params=pltpu.CompilerParams(has_side_effects=pltpu.SideEffectType.DATAFLOW_SIDE_EFFECTING),
    )(*[pltpu.with_memory_space_constraint(s, pltpu.HBM) for s in srcs], *real_lands, *deps)
    handle = dict(send=outs[0], recv=outs[1], srcs=outs[2:2 + n], lands=outs[2 + n:2 + n + n_l], inplace=inplace,
                  peers=peers)
    return handle, outs[-1]


def _exchange_wait(handle, after, name, only=None):
    srcs, lands, inplace, peers = handle["srcs"], handle["lands"], handle["inplace"], handle["peers"]
    slots = None
    if only is not None:
        assert all(inplace)
        slots, srcs, inplace = list(only), [srcs[t] for t in only], [True] * len(only)
    n, n_l = len(srcs), len(lands)

    def body(*refs):
        src = refs[:n]
        land_refs = list(refs[n:n + n_l])
        send_sems, recv_sems = refs[n + n_l], refs[n + n_l + 1]
        land = [None if ip else land_refs.pop(0) for ip in inplace]
        for mine, theirs in _copies(src, land, send_sems, recv_sems, peers, slots):
            mine().wait_send()
            theirs().wait_recv()

    outs = pl.pallas_call(
        body, name=name, out_shape=tuple(pltpu.HBM(a.shape, a.dtype) for a in list(srcs) + list(lands)),
        in_specs=[_HBM] * (n + n_l) + [_SEM, _SEM, pl.BlockSpec(memory_space=pl.ANY)],
        out_specs=(_HBM,) * (n + n_l), input_output_aliases={i: i for i in range(n + n_l)},
        compiler_params=pltpu.CompilerParams(has_side_effects=pltpu.SideEffectType.DATAFLOW_SIDE_EFFECTING),
    )(*srcs, *lands, handle["send"], handle["recv"], after)
    res, land_out = [], list(outs[n:])
    for t in range(n):
        res.append((outs[t], outs[t] if inplace[t] else land_out.pop(0)))
    return res


def _cast_to_slot(w, me, rows, name, cols=False, dep=None):
    r, cdim = w.shape
    deps = [] if dep is None else [dep]

    def body(me_ref, w_ref, *rest):
        o_ref = rest[-1]
        if cols:
            o_ref[...] = w_ref[...].astype(BF16)
        else:
            o_ref[0] = w_ref[...].astype(BF16)

    if cols:
        out_shape = jax.ShapeDtypeStruct((r, N_DEV * cdim), BF16)
        out_spec = pl.BlockSpec((rows, cdim), lambda i, me_ref: (i, me_ref[0]))
    else:
        out_shape = jax.ShapeDtypeStruct((N_DEV, r, cdim), BF16)
        out_spec = pl.BlockSpec((1, rows, cdim), lambda i, me_ref: (me_ref[0], i, 0))
    return pl.pallas_call(
        body, name=name, out_shape=out_shape,
        grid_spec=pltpu.PrefetchScalarGridSpec(
            num_scalar_prefetch=1, grid=(r // rows,),
            in_specs=[pl.BlockSpec((rows, cdim), lambda i, me_ref: (i, 0))]
            + [pl.BlockSpec(memory_space=pl.ANY)] * len(deps), out_specs=out_spec),
        compiler_params=_params("parallel"))(me, w, *deps)


def _adamw_math(w, g, m, v):
    m = ADAM_B1 * m + (1.0 - ADAM_B1) * g
    v = ADAM_B2 * v + (1.0 - ADAM_B2) * (g * g)
    m_hat = m / (1.0 - ADAM_B1 ** ADAM_STEP)
    v_hat = v / (1.0 - ADAM_B2 ** ADAM_STEP)
    delta = -ADAM_LR * (m_hat / (jnp.sqrt(v_hat) + ADAM_EPS) + ADAM_WD * w)
    return delta, m, v


def _sum_parts(me, p_ref, own):
    g = None
    for j in range(N_DEV):
        term = (p_ref[j] if own is None else jnp.where(me == j, own, p_ref[j])).astype(F32)
        g = term if g is None else g + term
    return g


def _adamw_reduce(parts, own, me, w, m, v, name):
    r, _, cdim = w.shape

    def body(me_ref, p_ref, own_ref, w_ref, m_ref, v_ref, g_out, d_out, m_out, v_out):
        g = _sum_parts(me_ref[0], p_ref, own_ref[0]).reshape(r, 1, cdim)
        d, mn, vn = _adamw_math(w_ref[...], g, m_ref[...], v_ref[...])
        g_out[...] = g
        d_out[...] = d
        m_out[...] = mn
        v_out[...] = vn

    blk = pl.BlockSpec((r, 1, cdim), lambda i, me_ref: (0, 0, 0))
    return pl.pallas_call(
        body, name=name, out_shape=(jax.ShapeDtypeStruct(w.shape, F32),) * 4,
        grid_spec=pltpu.PrefetchScalarGridSpec(
            num_scalar_prefetch=1, grid=(1,),
            in_specs=[pl.BlockSpec((N_DEV, r, cdim), lambda i, me_ref: (0, 0, 0)),
                      pl.BlockSpec((1, r, cdim), lambda i, me_ref: (me_ref[0], 0, 0)), blk, blk, blk],
            out_specs=(blk,) * 4),
        compiler_params=_params("arbitrary"))(me, parts, own, w, m, v)


_IN_SPLITS = ((0, 512), (512, 1024), (1024, 1536), (1536, 2560), (2560, IN_PAD))


def _prenorm(x, g1, tm, dep=None):
    t_tok = x.shape[0]
    deps = [] if dep is None else [dep]

    def body(x_ref, g_ref, *rest):
        xv = x_ref[...]
        r = lax.rsqrt(jnp.mean(xv * xv, axis=-1, keepdims=True) + EPS)
        rest[-1][...] = (xv * r * g_ref[...]).astype(BF16)

    row = pl.BlockSpec((tm, D_MODEL), lambda i: (i, 0))
    return pl.pallas_call(
        body, name="prenorm", grid=(t_tok // tm,), out_shape=jax.ShapeDtypeStruct((t_tok, D_MODEL), BF16),
        in_specs=[row, _full((1, D_MODEL))] + [pl.BlockSpec(memory_space=pl.ANY)] * len(deps), out_specs=row,
        compiler_params=_params("parallel"))(x, g1, *deps)


def _in_proj(h1, w_in, tm):
    t_tok = h1.shape[0]

    def body(h_ref, w_ref, *outs):
        h = h_ref[...]
        for (a, b), o_ref in zip(_IN_SPLITS, outs):
            o_ref[...] = _dot(h, w_ref[a:b, :], _NT).astype(o_ref.dtype)

    row = lambda n: pl.BlockSpec((tm, n), lambda i: (i, 0))
    widths = [b - a for a, b in _IN_SPLITS]
    dtypes = (BF16, BF16, BF16, F32, F32)
    return pl.pallas_call(
        body, name="in_proj", grid=(t_tok // tm,),
        out_shape=tuple(jax.ShapeDtypeStruct((t_tok, n), dt) for n, dt in zip(widths, dtypes)),
        in_specs=[row(D_MODEL), _full((IN_PAD, D_MODEL))], out_specs=tuple(row(n) for n in widths),
        compiler_params=_params("parallel"))(h1, w_in)


def _lane_masks():
    lane = lax.broadcasted_iota(jnp.int32, (1, 2 * HEAD_DIM), 1)
    left = (lane < HEAD_DIM).astype(F32)
    return left, 1.0 - left


def _stack_pair(v, m_l, m_r):
    return jnp.concatenate([v * m_l, v * m_r], axis=0).astype(BF16)


def _head_mean(x, avg):
    n = avg.shape[0]
    return jnp.concatenate([_split_dot(x[:, n * i:n * (i + 1)], avg, 2) for i in range(x.shape[1] // n)], axis=1)


def _gmlp_common(u, v, lnw, lnb, avg, wcat_ref, bias, m_l, m_r):
    ug, dug = _gelu_and_grad(u)
    vg, dvg = _gelu_and_grad(v)
    mu = _head_mean(vg, avg)
    vc = vg - mu
    var = _head_mean(vc * vc, avg)
    rstd = lax.rsqrt(var + EPS)
    vhat = vc * rstd
    vn = vhat * lnw + lnb
    rows = []
    for r in range(u.shape[0] // CHUNK):
        cols = []
        for j in range(N_HEADS // 2):
            pair = vn[CHUNK * r:CHUNK * (r + 1), 128 * j:128 * (j + 1)]
            cols.append(_dot(wcat_ref[j], _stack_pair(pair, m_l, m_r)))
        rows.append(jnp.concatenate(cols, axis=1) + bias)
    mixed = jnp.concatenate(rows, axis=0)
    return ug, dug, dvg, rstd, vhat, vn, mixed


_GMLP_ROWS = 4 * CHUNK


def _gmlp_fwd(u, v, lnw, lnb, wcat, bias, avg):
    t_tok = u.shape[0]
    tm = min(_GMLP_ROWS, t_tok)

    def body(u_ref, v_ref, lnw_ref, lnb_ref, wcat_ref, bias_ref, avg_ref, o_ref):
        m_l, m_r = _lane_masks()
        ug, _, _, _, _, _, mixed = _gmlp_common(
            u_ref[...].astype(F32), v_ref[...].astype(F32), lnw_ref[...], lnb_ref[...], avg_ref[...], wcat_ref,
            bias_ref[...], m_l, m_r)
        o_ref[...] = (ug * mixed).astype(BF16)

    row = pl.BlockSpec((tm, GM_WIDTH), lambda i: (i, 0))
    return pl.pallas_call(
        body, name="gmlp_fwd", grid=(t_tok // tm,), out_shape=jax.ShapeDtypeStruct((t_tok, GM_WIDTH), BF16),
        in_specs=[row, row, _full((1, GM_WIDTH)), _full((1, GM_WIDTH)), _full(wcat.shape), _full(bias.shape),
                  _full(avg.shape)],
        out_specs=row, compiler_params=_params("parallel"))(u, v, lnw, lnb, wcat, bias, avg)


def _shift_rows(x, edge, j, down):
    groups, cols = x.shape[0] // 8, x.shape[1]
    amount = j if down else 8 - j
    rot = pltpu.roll(x.reshape(groups, 8, cols), amount, axis=1)
    edge_rot = pltpu.roll(edge, amount, axis=0)[None]
    sub = lax.broadcasted_iota(jnp.int32, (1, 8, 1), 1)
    if down:
        out = jnp.where(sub < j, jnp.concatenate([edge_rot, rot[:-1]], axis=0), rot)
    else:
        out = jnp.where(sub < 8 - j, rot, jnp.concatenate([rot[1:], edge_rot], axis=0))
    return out.reshape(x.shape)


def _conv_pre(xbc, tail, cw_ref, cb):
    taps = [_shift_rows(xbc, tail, 3 - k, True) for k in range(3)] + [xbc]
    return cb + cw_ref[0:1, :] * taps[0] + cw_ref[1:2, :] * taps[1] + cw_ref[2:3, :] * taps[2] + cw_ref[3:4, :] * taps[3]


def _ssd_common(pre, dtr, dtb, alog, expand, tril):
    q = CHUNK
    sg = jax.nn.sigmoid(pre)
    act = pre * sg
    lane = lax.broadcasted_iota(jnp.int32, (1, CHUNK), 1)
    a_row = jnp.where(lane < N_HEADS, -jnp.exp(alog), 0.0)
    dtp = dtr + dtb
    dt = _softplus(dtp)
    a_cs = _split_dot_left(tril, dt * a_row, 3)
    a_cs_t = a_cs.T
    dt_exp = _split_dot(dt, expand, 3)
    a_exp = _split_dot(a_cs, expand, 3)
    a_end = a_exp[q - 1:q, :]
    li = lax.broadcasted_iota(jnp.int32, (q, q), 0)
    si = lax.broadcasted_iota(jnp.int32, (q, q), 1)
    causal = si <= li
    decay = []
    for h in range(N_HEADS):
        seg = a_cs[:, h:h + 1] - a_cs_t[h:h + 1, :]
        decay.append(jnp.where(causal, jnp.exp(jnp.minimum(seg, 0.0)), 0.0))
    return dict(pre=pre, sg=sg, act=act, a_row=a_row, dtp=dtp, dt=dt, dt_exp=dt_exp, a_exp=a_exp,
                e=jnp.exp(a_exp), w_end=jnp.exp(a_end - a_exp), cd=jnp.exp(a_end), decay=decay)


def _ssd_specs(t_tok, seq, reverse):
    nb, nc = t_tok // seq, seq // CHUNK

    def chunk(c):
        return nc - 1 - c if reverse else c

    def row(n, col=0):
        return pl.BlockSpec((nb, CHUNK, n), lambda c: (0, chunk(c), col))

    tail = pl.BlockSpec((nb, 8, CONV_CH), lambda c: (0, jnp.maximum(chunk(c) * (CHUNK // 8) - 1, 0), 0))
    states = pl.BlockSpec((nb, 1, N_STATE, SSM_WIDTH), lambda c: (0, chunk(c), 0, 0))
    fold = lambda a: a.reshape(nb, seq, a.shape[-1])
    unfold = lambda a: a.reshape(t_tok, a.shape[-1])
    return nb, nc, row, tail, states, fold, unfold


def _ssd_fwd(z, xbc, dtr, cw, cb, dtb, alog, dskip_exp, nw, expand, tril, seq, dep=None):
    t_tok = z.shape[0]
    nb, nc, row, tail, states_spec, fold, unfold = _ssd_specs(t_tok, seq, False)

    def body(z_ref, xbc_ref, tail_ref, dtr_ref, cw_ref, cb_ref, dtb_ref, alog_ref, dsk_ref, nw_ref, exp_ref,
             tril_ref, o_ref, y_ref, st_ref, pre_ref, state_ref):
        c = pl.program_id(0)

        @pl.when(c == 0)
        def _():
            state_ref[...] = jnp.zeros_like(state_ref)

        m_l, m_r = _lane_masks()
        for s in range(nb):
            pre = _conv_pre(xbc_ref[s], jnp.where(c == 0, 0.0, tail_ref[s]), cw_ref, cb_ref[...])
            pre_ref[s] = pre
            f = _ssd_common(pre, dtr_ref[s], dtb_ref[...], alog_ref[...], exp_ref[...], tril_ref[...])
            act = f["act"]
            xs = act[:, :SSM_WIDTH]
            xdt = xs * f["dt_exp"]
            xw = xdt * f["w_end"]
            state = state_ref[s]
            st_ref[s, 0] = state
            ydiag, yoff, snew = [], [], []
            for g in range(2):
                bg = act[:, 512 + 128 * g:640 + 128 * g].astype(BF16)
                cg = act[:, 768 + 128 * g:896 + 128 * g].astype(BF16)
                cb_mat = _dot(cg, bg, _NT)
                for pr in range(2):
                    h0 = 4 * g + 2 * pr
                    gcat = jnp.concatenate(
                        [(cb_mat * f["decay"][h0]).astype(BF16), (cb_mat * f["decay"][h0 + 1]).astype(BF16)], axis=1)
                    ydiag.append(_dot(gcat, _stack_pair(xdt[:, 64 * h0:64 * h0 + 128], m_l, m_r)))
                yoff.append(_dot(cg, state[:, 256 * g:256 * (g + 1)].astype(BF16)))
                snew.append(_dot(bg, xw[:, 256 * g:256 * (g + 1)].astype(BF16), _TN))
            y = jnp.concatenate(ydiag, axis=1) + f["e"] * jnp.concatenate(yoff, axis=1) + dsk_ref[...] * xs
            state_ref[s] = state * f["cd"] + jnp.concatenate(snew, axis=1)
            y_ref[s] = y
            zv = z_ref[s].astype(F32)
            yg = y * (zv * jax.nn.sigmoid(zv))
            outs = []
            for g in range(2):
                ygg = yg[:, 256 * g:256 * (g + 1)]
                outs.append(ygg * lax.rsqrt(jnp.mean(ygg * ygg, axis=-1, keepdims=True) + EPS))
            o_ref[s] = (jnp.concatenate(outs, axis=1) * nw_ref[...]).astype(BF16)

    consts = [cw, cb, dtb, alog, dskip_exp, nw, expand, tril]
    deps = [] if dep is None else [dep]
    n_in = 4 + len(consts)

    def body_skipping_dep(*refs):
        body(*refs[:n_in], *refs[n_in + len(deps):])

    sd = lambda n, dt: jax.ShapeDtypeStruct((nb, seq, n), dt)
    o, y, states, pre = pl.pallas_call(
        body_skipping_dep, name="ssd_fwd", grid=(nc,),
        out_shape=(sd(SSM_WIDTH, BF16), sd(SSM_WIDTH, F32), jax.ShapeDtypeStruct((nb, nc, N_STATE, SSM_WIDTH), F32),
                   sd(CONV_CH, F32)),
        in_specs=[row(SSM_WIDTH), row(CONV_CH), tail, row(CHUNK)] + [_full(a.shape) for a in consts]
        + [pl.BlockSpec(memory_space=pl.ANY)] * len(deps),
        out_specs=(row(SSM_WIDTH), row(SSM_WIDTH), states_spec, row(CONV_CH)),
        scratch_shapes=[pltpu.VMEM((nb, N_STATE, SSM_WIDTH), F32)],
        compiler_params=_params("arbitrary"))(fold(z), fold(xbc), fold(xbc), fold(dtr), *consts, *deps)
    return unfold(o), unfold(y), states, unfold(pre)


def _out_proj(mix_a, mix_b, w_out, x, g2, g3, tm, dep=None):
    t_tok = x.shape[0]
    deps = [] if dep is None else [dep]

    def body(a_ref, b_ref, w_ref, x_ref, g2_ref, g3_ref, *rest):
        o_ref, x2_ref, h3_ref = rest[-3:]
        o = _dot(a_ref[...], w_ref[0:GM_WIDTH, :]) + _dot(b_ref[...], w_ref[GM_WIDTH:, :])
        o_ref[...] = o
        r2 = lax.rsqrt(jnp.mean(o * o, axis=-1, keepdims=True) + EPS)
        x2 = x_ref[...] + o * r2 * g2_ref[...]
        x2_ref[...] = x2
        r3 = lax.rsqrt(jnp.mean(x2 * x2, axis=-1, keepdims=True) + EPS)
        h3_ref[...] = (x2 * r3 * g3_ref[...]).astype(BF16)

    row = lambda n: pl.BlockSpec((tm, n), lambda i: (i, 0))
    sd = lambda dt: jax.ShapeDtypeStruct((t_tok, D_MODEL), dt)
    return pl.pallas_call(
        body, name="out_proj", grid=(t_tok // tm,), out_shape=(sd(F32), sd(F32), sd(BF16)),
        in_specs=[row(GM_WIDTH), row(SSM_WIDTH), _full((D_MODEL, D_MODEL)), row(D_MODEL), _full((1, D_MODEL)),
                  _full((1, D_MODEL))] + [pl.BlockSpec(memory_space=pl.ANY)] * len(deps),
        out_specs=(row(D_MODEL),) * 3, compiler_params=_params("parallel"))(mix_a, mix_b, w_out, x, g2, g3, *deps)


def _mlp_fwd(h3, w_up, w_down, x2, target, g4, tm, tf):
    t_tok = x2.shape[0]

    def up_body(h_ref, wu_ref, ra_ref):
        ra_ref[...] = jnp.maximum(_dot(h_ref[...], wu_ref[...]), 0.0).astype(BF16)

    tu = min(2 * tm, t_tok)
    ra = pl.pallas_call(
        up_body, name="mlp_up", grid=(D_FF // tf, t_tok // tu), out_shape=jax.ShapeDtypeStruct((t_tok, D_FF), BF16),
        in_specs=[pl.BlockSpec((tu, D_MODEL), lambda j, i: (i, 0)), pl.BlockSpec((D_MODEL, tf), lambda j, i: (0, j))],
        out_specs=pl.BlockSpec((tu, tf), lambda j, i: (i, j)), compiler_params=_params("parallel", "parallel"))(h3, w_up)

    def down_body(ra_ref, wd_ref, x2_ref, t_ref, g4_ref, dd_ref, dy_ref, dg4_ref, loss_ref):
        i = pl.program_id(0)
        rav = ra_ref[...]
        dvec = _dot(rav * rav, wd_ref[...])
        r4 = lax.rsqrt(jnp.mean(dvec * dvec, axis=-1, keepdims=True) + EPS)
        dn = dvec * r4
        g4 = g4_ref[...]
        err = x2_ref[...] + dn * g4 - t_ref[...]
        dy = err * (1.0 / D_MODEL)
        dy_ref[...] = dy
        dg = dy * g4
        dd_ref[...] = (r4 * (dg - dn * jnp.mean(dg * dn, axis=-1, keepdims=True))).astype(BF16)
        _acc_rows(dg4_ref, _rsum(dy * dn), i == 0)
        tile_loss = 0.5 * jnp.sum(jnp.sum(err * err, axis=-1, keepdims=True), axis=0, keepdims=True) / D_MODEL
        _acc_rows(loss_ref, jnp.broadcast_to(tile_loss, (1, 128)), i == 0)

    row = pl.BlockSpec((tm, D_MODEL), lambda i: (i, 0))
    dd, dy, dg4, loss = pl.pallas_call(
        down_body, name="mlp_down", grid=(t_tok // tm,),
        out_shape=(jax.ShapeDtypeStruct((t_tok, D_MODEL), BF16), jax.ShapeDtypeStruct((t_tok, D_MODEL), F32),
                   jax.ShapeDtypeStruct((1, D_MODEL), F32), jax.ShapeDtypeStruct((1, 128), F32)),
        in_specs=[pl.BlockSpec((tm, D_FF), lambda i: (i, 0)), _full((D_FF, D_MODEL)), row, row, _full((1, D_MODEL))],
        out_specs=(row, row, _full((1, D_MODEL)), _full((1, 128))),
        compiler_params=_params("arbitrary"))(ra, w_down, x2, target, g4)
    return ra, dd, dy, dg4, loss


def _mlp_bwd(dd, w_down, ra, w_up, x2, dy, o, g3, g2, tm, tf):
    t_tok = x2.shape[0]

    def hidden_body(dd_ref, wd_ref, ra_ref, da_ref):
        df = _dot(dd_ref[...], wd_ref[...], _NT)
        da_ref[...] = (df * (2.0 * ra_ref[...].astype(F32))).astype(BF16)

    tu = min(2 * tm, t_tok)
    da = pl.pallas_call(
        hidden_body, name="mlp_bwd_hidden", grid=(D_FF // tf, t_tok // tu),
        out_shape=jax.ShapeDtypeStruct((t_tok, D_FF), BF16),
        in_specs=[pl.BlockSpec((tu, D_MODEL), lambda j, i: (i, 0)), pl.BlockSpec((tf, D_MODEL), lambda j, i: (j, 0)),
                  pl.BlockSpec((tu, tf), lambda j, i: (i, j))],
        out_specs=pl.BlockSpec((tu, tf), lambda j, i: (i, j)),
        compiler_params=_params("parallel", "parallel"))(dd, w_down, ra)

    def in_body(da_ref, wu_ref, x2_ref, dy_ref, o_ref, g3_ref, g2_ref, dx2_ref, do_ref, dg3_ref, dg2_ref):
        i = pl.program_id(0)
        dh3 = _dot(da_ref[...], wu_ref[...], _NT)
        dn3, dg3 = _rms_bwd(x2_ref[...], g3_ref[...], dh3)
        dx2 = dy_ref[...] + dn3
        dx2_ref[...] = dx2
        do, dg2 = _rms_bwd(o_ref[...], g2_ref[...], dx2)
        do_ref[...] = do.astype(BF16)
        _acc_rows(dg3_ref, dg3, i == 0)
        _acc_rows(dg2_ref, dg2, i == 0)

    row = pl.BlockSpec((tm, D_MODEL), lambda i: (i, 0))
    vec = _full((1, D_MODEL))
    sd = lambda dt: jax.ShapeDtypeStruct((t_tok, D_MODEL), dt)
    dx2, do, dg3, dg2 = pl.pallas_call(
        in_body, name="mlp_bwd_in", grid=(t_tok // tm,),
        out_shape=(sd(F32), sd(BF16), jax.ShapeDtypeStruct((1, D_MODEL), F32), jax.ShapeDtypeStruct((1, D_MODEL), F32)),
        in_specs=[pl.BlockSpec((tm, D_FF), lambda i: (i, 0)), _full((D_MODEL, D_FF)), row, row, row, vec, vec],
        out_specs=(row, row, vec, vec), compiler_params=_params("arbitrary"))(da, w_up, x2, dy, o, g3, g2)
    return da, dx2, do, dg3, dg2


def _wgrad(a, b, out_blocks, bm, bn, bk, square_a, name, dep=None):
    t_tok, m = a.shape
    n = b.shape[1]
    nk = t_tok // bk

    def body(a_ref, b_ref, *rest):
        o_ref, acc_ref = rest[-2:]
        k = pl.program_id(2)
        av = a_ref[...]
        if square_a:
            av = av * av
        part = _dot(av, b_ref[...], _TN)

        def emit(res):
            if out_blocks is None:
                o_ref[...] = res.astype(BF16)
            else:
                o_ref[0] = res.astype(BF16)

        if nk == 1:
            emit(part)
            return

        @pl.when(k == 0)
        def _():
            acc_ref[...] = part

        @pl.when(k > 0)
        def _():
            acc_ref[...] += part

        @pl.when(k == nk - 1)
        def _():
            emit(acc_ref[...])

    if out_blocks is None:
        out_shape = jax.ShapeDtypeStruct((m, n), BF16)
        out_spec = pl.BlockSpec((bm, bn), lambda i, j, k: (i, j))
    else:
        assert n // out_blocks == bn
        out_shape = jax.ShapeDtypeStruct((out_blocks, m, bn), BF16)
        out_spec = pl.BlockSpec((1, bm, bn), lambda i, j, k: (j, i, 0))
    deps = [] if dep is None else [dep]
    return pl.pallas_call(
        body, name=name, grid=(m // bm, n // bn, nk), out_shape=out_shape,
        in_specs=[pl.BlockSpec((bk, bm), lambda i, j, k: (k, i)), pl.BlockSpec((bk, bn), lambda i, j, k: (k, j))]
        + [pl.BlockSpec(memory_space=pl.ANY)] * len(deps),
        out_specs=out_spec, scratch_shapes=[pltpu.VMEM((bm, bn) if nk > 1 else (8, 128), F32)],
        compiler_params=_params("parallel", "parallel", "arbitrary"))(a, b, *deps)


def _wgrad_in_chunked(h1, pieces, bn, bk, dep=None):
    t_tok = h1.shape[0]
    nk = t_tok // bk
    widths = [b - a for a, b in _IN_SPLITS]

    def body(h_ref, *rest):
        piece_refs = rest[:len(widths)]
        o_ref, acc_ref = rest[-2:]
        k = pl.program_id(1)
        hv = h_ref[...]
        for (a, b), r in zip(_IN_SPLITS, piece_refs):
            part = _dot(r[...], hv, _TN)

            @pl.when(k == 0)
            def _():
                acc_ref[a:b, :] = part

            @pl.when(k > 0)
            def _():
                acc_ref[a:b, :] += part

        @pl.when(k == nk - 1)
        def _():
            o_ref[...] = acc_ref[...].astype(BF16)

    deps = [] if dep is None else [dep]
    return pl.pallas_call(
        body, name="wgrad_in", grid=(D_MODEL // bn, nk), out_shape=jax.ShapeDtypeStruct((IN_PAD, D_MODEL), BF16),
        in_specs=[pl.BlockSpec((bk, bn), lambda j, k: (k, j))] + [pl.BlockSpec((bk, n), lambda j, k: (k, 0)) for n in widths]
        + [pl.BlockSpec(memory_space=pl.ANY)] * len(deps),
        out_specs=pl.BlockSpec((IN_PAD, bn), lambda j, k: (0, j)), scratch_shapes=[pltpu.VMEM((IN_PAD, bn), F32)],
        compiler_params=_params("parallel", "arbitrary"))(h1, *pieces, *deps)


def _wgrad_pieces(h1, pieces, bn, name, dep=None):
    t_tok = h1.shape[0]
    widths = [p.shape[1] for p in pieces]
    starts = [sum(widths[:i]) for i in range(len(widths))]

    def body(h_ref, *rest):
        piece_refs = rest[:len(widths)]
        o_ref = rest[-1]
        hv = h_ref[...]
        for a, n, r in zip(starts, widths, piece_refs):
            o_ref[a:a + n, :] = _dot(r[...], hv, _TN).astype(BF16)

    deps = [] if dep is None else [dep]
    return pl.pallas_call(
        body, name=name, grid=(D_MODEL // bn,), out_shape=jax.ShapeDtypeStruct((sum(widths), D_MODEL), BF16),
        in_specs=[pl.BlockSpec((t_tok, bn), lambda j: (0, j))] + [pl.BlockSpec((t_tok, n), lambda j: (0, 0)) for n in widths]
        + [pl.BlockSpec(memory_space=pl.ANY)] * len(deps),
        out_specs=pl.BlockSpec((sum(widths), bn), lambda j: (0, j)),
        compiler_params=_params("parallel"))(h1, *pieces, *deps)


def _dmix(do, w_out, tm, dep=None):
    t_tok = do.shape[0]

    def body(d_ref, w_ref, *rest):
        rest[-1][...] = _dot(d_ref[...], w_ref[...], _NT).astype(BF16)

    row = pl.BlockSpec((tm, D_MODEL), lambda i: (i, 0))
    deps = [] if dep is None else [dep]
    return pl.pallas_call(
        body, name="dmix", grid=(t_tok // tm,), out_shape=jax.ShapeDtypeStruct((t_tok, D_MODEL), BF16),
        in_specs=[row, _full((D_MODEL, D_MODEL))] + [pl.BlockSpec(memory_space=pl.ANY)] * len(deps), out_specs=row,
        compiler_params=_params("parallel"))(do, w_out, *deps)


def _gmlp_bwd(dmix, u, v, lnw, lnb, wcat, wtcat, bias, avg, expand_t):
    t_tok = u.shape[0]
    tm = min(_GMLP_ROWS, t_tok)

    def body(dm_ref, u_ref, v_ref, lnw_ref, lnb_ref, wcat_ref, wtcat_ref, bias_ref, avg_ref, expt_ref, du_ref, dv_ref,
             dw_ref, db_ref, dlnw_ref, dlnb_ref):
        i = pl.program_id(0)
        m_l, m_r = _lane_masks()
        avg = avg_ref[...]
        lnw = lnw_ref[...]
        ug, dug, dvg, rstd, vhat, vn, mixed = _gmlp_common(
            u_ref[...].astype(F32), v_ref[...].astype(F32), lnw, lnb_ref[...], avg, wcat_ref, bias_ref[...], m_l, m_r)
        dya = dm_ref[...].astype(F32)
        du_ref[...] = (dya * mixed * dug).astype(BF16)
        dmixed = dya * ug
        dvn_rows, dws, dbt = [], [None] * N_HEADS, None
        for r in range(tm // CHUNK):
            dvn_cols = []
            for j in range(N_HEADS // 2):
                dmp = dmixed[CHUNK * r:CHUNK * (r + 1), 128 * j:128 * (j + 1)]
                dvn_cols.append(_dot(wtcat_ref[j], _stack_pair(dmp, m_l, m_r)))
                vnp = vn[CHUNK * r:CHUNK * (r + 1), 128 * j:128 * (j + 1)].astype(BF16)
                for i_h, mask in enumerate((m_l, m_r)):
                    part = _dot((dmp * mask).astype(BF16), vnp, _NT)
                    dws[2 * j + i_h] = part if r == 0 else dws[2 * j + i_h] + part
            dvn_rows.append(jnp.concatenate(dvn_cols, axis=1))
            part = _split_dot(dmixed[CHUNK * r:CHUNK * (r + 1), :], expt_ref[...], 2)
            dbt = part if r == 0 else dbt + part
        dvn = jnp.concatenate(dvn_rows, axis=0)
        dvh = dvn * lnw
        dvgel = rstd * (dvh - _head_mean(dvh, avg) - vhat * _head_mean(dvh * vhat, avg))
        dv_ref[...] = (dvgel * dvg).astype(BF16)
        first = i == 0

        @pl.when(first)
        def _():
            for h in range(N_HEADS):
                dw_ref[h] = dws[h]
            db_ref[...] = dbt

        @pl.when(jnp.logical_not(first))
        def _():
            for h in range(N_HEADS):
                dw_ref[h] += dws[h]
            db_ref[...] += dbt

        _acc_rows(dlnw_ref, _rsum(dvn * vhat), first)
        _acc_rows(dlnb_ref, _rsum(dvn), first)

    row = pl.BlockSpec((tm, GM_WIDTH), lambda i: (i, 0))
    consts = [lnw, lnb, wcat, wtcat, bias, avg, expand_t]
    return pl.pallas_call(
        body, name="gmlp_bwd", grid=(t_tok // tm,),
        out_shape=(jax.ShapeDtypeStruct((t_tok, GM_WIDTH), BF16), jax.ShapeDtypeStruct((t_tok, GM_WIDTH), BF16),
                   jax.ShapeDtypeStruct((N_HEADS, CHUNK, CHUNK), F32), jax.ShapeDtypeStruct((CHUNK, CHUNK), F32),
                   jax.ShapeDtypeStruct((1, GM_WIDTH), F32), jax.ShapeDtypeStruct((1, GM_WIDTH), F32)),
        in_specs=[row, row, row] + [_full(a.shape) for a in consts],
        out_specs=(row, row, _full((N_HEADS, CHUNK, CHUNK)), _full((CHUNK, CHUNK)), _full((1, GM_WIDTH)),
                   _full((1, GM_WIDTH))),
        compiler_params=_params("arbitrary"))(dmix, u, v, *consts)


def _ssd_bwd(dmix, z, xbc, pre, dtr, y, states, cw, cb, dtb, alog, dskip_exp, nw, expand, expand_t, tril, triu, seq,
             dep=None):
    t_tok = z.shape[0]
    nb, nc, row, _, states_spec, fold, unfold = _ssd_specs(t_tok, seq, True)
    q = CHUNK

    def one_sequence(s, dm_ref, z_ref, xbc_ref, pre_ref, dtr_ref, y_ref, st_ref, cw_ref, dtb_ref, alog_ref, dsk_ref,
                     nw_ref, exp_ref, expt_ref, tril_ref, triu_ref, dz_ref, dxbc_ref, ddt_ref, dhead_ref, dstate_ref):
        m_l, m_r = _lane_masks()
        expt = expt_ref[...]
        f = _ssd_common(pre_ref[s], dtr_ref[s], dtb_ref[...], alog_ref[...], exp_ref[...], tril_ref[...])
        act, pre, sg = f["act"], f["pre"], f["sg"]
        xs = act[:, :SSM_WIDTH]
        xdt = xs * f["dt_exp"]
        xw = xdt * f["w_end"]
        state = st_ref[s, 0]
        dstate = dstate_ref[s]
        zv, yv, dout, nw = z_ref[s].astype(F32), y_ref[s], dm_ref[s].astype(F32), nw_ref[...]
        sz = jax.nn.sigmoid(zv)
        sl = zv * sz
        yg = yv * sl
        tv = dout * nw
        dyg_parts, ygh_parts = [], []
        for g in range(2):
            ygg = yg[:, 256 * g:256 * (g + 1)]
            rr = lax.rsqrt(jnp.mean(ygg * ygg, axis=-1, keepdims=True) + EPS)
            ygh = ygg * rr
            tg = tv[:, 256 * g:256 * (g + 1)]
            dyg_parts.append(rr * (tg - ygh * jnp.mean(tg * ygh, axis=-1, keepdims=True)))
            ygh_parts.append(ygh)
        dyg = jnp.concatenate(dyg_parts, axis=1)
        dnw = _rsum(dout * jnp.concatenate(ygh_parts, axis=1))
        dy = dyg * sl
        dz_ref[s] = (dyg * yv * (sz * (1.0 + zv * (1.0 - sz)))).astype(BF16)
        ddsk = _rsum(dy * xs)
        dye = dy * f["e"]
        lane = lax.broadcasted_iota(jnp.int32, (q, q), 1)
        sub = lax.broadcasted_iota(jnp.int32, (q, q), 0)
        rs_mat = jnp.zeros((q, q), F32)
        cs_mat = jnp.zeros((q, q), F32)
        dxdt_cols, yoff, dst_in, dxw, d_b, d_c = [], [], [], [], [], []
        for g in range(2):
            bg = act[:, 512 + 128 * g:640 + 128 * g].astype(BF16)
            cg = act[:, 768 + 128 * g:896 + 128 * g].astype(BF16)
            cb_mat = _dot(cg, bg, _NT)
            stg = state[:, 256 * g:256 * (g + 1)].astype(BF16)
            dyeg = dye[:, 256 * g:256 * (g + 1)].astype(BF16)
            yoff.append(_dot(cg, stg))
            dcg = _dot(dyeg, stg, _NT)
            dst_in.append(_dot(cg, dyeg, _TN))
            dcb = jnp.zeros((q, q), F32)
            for pr in range(2):
                h0 = 4 * g + 2 * pr
                gf = [cb_mat * f["decay"][h0], cb_mat * f["decay"][h0 + 1]]
                gcat = jnp.concatenate([gf[0].astype(BF16), gf[1].astype(BF16)], axis=1)
                xst = _stack_pair(xdt[:, 64 * h0:64 * h0 + 128], m_l, m_r)
                dyp = dy[:, 64 * h0:64 * h0 + 128].astype(BF16)
                dgcat = _dot(dyp, xst, _NT)
                dxst = _dot(gcat, dyp, _TN)
                dxdt_cols.append(dxst[:q] * m_l + dxst[q:] * m_r)
                for i in range(2):
                    h = h0 + i
                    dg = dgcat[:, q * i:q * (i + 1)]
                    mm = dg * gf[i]
                    rs_mat = rs_mat + jnp.where(lane == h, jnp.sum(mm, axis=1, keepdims=True), 0.0)
                    cs_mat = cs_mat + jnp.where(sub == h, jnp.sum(mm, axis=0, keepdims=True), 0.0)
                    dcb = dcb + dg * f["decay"][h]
            dcb16 = dcb.astype(BF16)
            dstg = dstate[:, 256 * g:256 * (g + 1)].astype(BF16)
            d_c.append(dcg + _dot(dcb16, bg))
            dxw.append(_dot(bg, dstg))
            d_b.append(_dot(dcb16, cg, _TN) + _dot(xw[:, 256 * g:256 * (g + 1)].astype(BF16), dstg, _NT))
        dxw = jnp.concatenate(dxw, axis=1)
        dxdt = jnp.concatenate(dxdt_cols, axis=1) + dxw * f["w_end"]
        qv = dxw * xw
        end_row = _rsum(qv) + _rsum(dstate * state) * f["cd"]
        x2 = dye * jnp.concatenate(yoff, axis=1) - qv
        row_i = lax.broadcasted_iota(jnp.int32, (q, 1), 0)
        x2 = x2 + jnp.where(row_i == q - 1, end_row, 0.0)
        da_cs = _split_dot(x2, expt, 2) + rs_mat - cs_mat.T
        ddt = _split_dot(dxdt * xs, expt, 2)
        dxs = dsk_ref[...] * dy + dxdt * f["dt_exp"]
        dda = _split_dot_left(triu_ref[...], da_cs, 3)
        ddt = ddt + dda * f["a_row"]
        dalog = _rsum(dda * f["dt"]) * f["a_row"]
        draw = ddt * jax.nn.sigmoid(f["dtp"])
        ddt_ref[s] = draw.astype(BF16)
        dact = jnp.concatenate([dxs] + d_b + d_c, axis=1)
        dpre = dact * (sg * (1.0 + pre * (1.0 - sg)))
        dhead = dhead_ref[s]
        xv = xbc_ref[s]
        shifted = [_shift_rows(dpre, dhead, 3 - k, False) for k in range(3)] + [dpre]
        dxbc = cw_ref[3:4, :] * dpre
        for k in range(3):
            dxbc = dxbc + cw_ref[k:k + 1, :] * shifted[k]
        dxbc_ref[s] = dxbc.astype(BF16)
        dhead_ref[s] = dpre[0:8, :]
        dstate_ref[s] = dstate * f["cd"] + jnp.concatenate(dst_in, axis=1)
        row8 = lax.broadcasted_iota(jnp.int32, (8, 1), 0)
        dcw = jnp.zeros((8, CONV_CH), F32)
        for k in range(4):
            dcw = dcw + jnp.where(row8 == k, _rsum(shifted[k] * xv), 0.0)
        return dcw, _rsum(dpre), _rsum(draw), dalog, _split_dot(ddsk, expt, 3), dnw

    def body(dm_ref, z_ref, xbc_ref, pre_ref, dtr_ref, y_ref, st_ref, cw_ref, cb_ref, dtb_ref, alog_ref, dsk_ref,
             nw_ref, exp_ref, expt_ref, tril_ref, triu_ref, dz_ref, dxbc_ref, ddt_ref, dcw_ref, dcb_ref, ddtb_ref,
             dalog_ref, dd_ref, dnw_ref, dhead_ref, dstate_ref):
        c = pl.program_id(0)
        first = c == 0

        @pl.when(first)
        def _():
            dstate_ref[...] = jnp.zeros_like(dstate_ref)
            dhead_ref[...] = jnp.zeros_like(dhead_ref)

        total = None
        for s in range(nb):
            parts = one_sequence(s, dm_ref, z_ref, xbc_ref, pre_ref, dtr_ref, y_ref, st_ref, cw_ref, dtb_ref, alog_ref,
                                 dsk_ref, nw_ref, exp_ref, expt_ref, tril_ref, triu_ref, dz_ref, dxbc_ref, ddt_ref,
                                 dhead_ref, dstate_ref)
            total = parts if total is None else tuple(a + b for a, b in zip(total, parts))
        dcw = total[0]

        @pl.when(first)
        def _():
            dcw_ref[...] = dcw

        @pl.when(jnp.logical_not(first))
        def _():
            dcw_ref[...] += dcw

        for ref, part in zip((dcb_ref, ddtb_ref, dalog_ref, dd_ref, dnw_ref), total[1:]):
            _acc_rows(ref, part, first)

    consts = [cw, cb, dtb, alog, dskip_exp, nw, expand, expand_t, tril, triu]
    deps = [] if dep is None else [dep]
    n_in = 7 + len(consts)

    def body_skipping_dep(*refs):
        body(*refs[:n_in], *refs[n_in + len(deps):])

    acc = lambda n: jax.ShapeDtypeStruct((1, n), F32)
    sd = lambda n: jax.ShapeDtypeStruct((nb, seq, n), BF16)
    dz, dxbc, ddt, *small_grads = pl.pallas_call(
        body_skipping_dep, name="ssd_bwd", grid=(nc,),
        out_shape=(sd(SSM_WIDTH), sd(CONV_CH), sd(CHUNK), jax.ShapeDtypeStruct((8, CONV_CH), F32), acc(CONV_CH),
                   acc(CHUNK), acc(CHUNK), acc(CHUNK), acc(SSM_WIDTH)),
        in_specs=[row(SSM_WIDTH, col=1), row(SSM_WIDTH), row(CONV_CH), row(CONV_CH), row(CHUNK), row(SSM_WIDTH),
                  states_spec]
        + [_full(a.shape) for a in consts] + [pl.BlockSpec(memory_space=pl.ANY)] * len(deps),
        out_specs=(row(SSM_WIDTH), row(CONV_CH), row(CHUNK), _full((8, CONV_CH)), _full((1, CONV_CH)),
                   _full((1, CHUNK)), _full((1, CHUNK)), _full((1, CHUNK)), _full((1, SSM_WIDTH))),
        scratch_shapes=[pltpu.VMEM((nb, 8, CONV_CH), F32), pltpu.VMEM((nb, N_STATE, SSM_WIDTH), F32)],
        compiler_params=_params("arbitrary"))(
            fold(dmix), fold(z), fold(xbc), fold(pre), fold(dtr), fold(y), states, *consts, *deps)
    return (unfold(dz), unfold(dxbc), unfold(ddt), *small_grads)


def _in_bwd(du, dv, dz, dxbc, ddt, w_in, x, dx2, g1, tm, me, riders=(), dep=None):
    t_tok = x.shape[0]
    steps = t_tok // tm

    n_in = [5 + ("mask" in rd) for rd in riders]
    first_in = [sum(n_in[:r]) for r in range(len(riders))]

    def body(me_ref, du_ref, dv_ref, dz_ref, dxbc_ref, ddt_ref, w_ref, x_ref, dx2_ref, g_ref, *rest):
        outs = rest[len(rest) - 2 - 4 * len(riders):]
        gx_ref, dg_ref = outs[:2]
        i = pl.program_id(0)
        dh = None
        for (a, b), ref in zip(_IN_SPLITS, (du_ref, dv_ref, dz_ref, dxbc_ref, ddt_ref)):
            part = _dot(ref[...], w_ref[a:b, :])
            dh = part if dh is None else dh + part
        dn, dg = _rms_bwd(x_ref[...], g_ref[...], dh)
        gx_ref[...] = dx2_ref[...] + dn
        _acc_rows(dg_ref, dg, i == 0)
        for r in range(len(riders)):
            p_ref, own_ref, w_ref_r, m_ref_r, v_ref_r = rest[first_in[r]:first_in[r] + 5]
            g = _sum_parts(me_ref[0], p_ref, own_ref[0])
            if n_in[r] == 6:
                g = g * rest[first_in[r] + 5][...]
            d, mn, vn = _adamw_math(w_ref_r[...], g, m_ref_r[...], v_ref_r[...])
            for o_ref, val in zip(outs[2 + 4 * r:6 + 4 * r], (g, d, mn, vn)):
                o_ref[...] = val

    row = lambda n: pl.BlockSpec((tm, n), lambda i, me_ref: (i, 0))
    whole = lambda shape: pl.BlockSpec(shape, lambda i, me_ref: (0,) * len(shape))
    widths = [b - a for a, b in _IN_SPLITS]
    deps = [] if dep is None else [dep]
    rider_args, rider_specs, rider_out_shapes, rider_out_specs = [], [], [], []
    for rd in riders:
        rows, cols = rd["w"].shape[0] // steps, rd["w"].shape[1]
        blk = pl.BlockSpec((rows, cols), lambda i, me_ref: (i, 0))
        rider_args += [rd["parts"], rd["own"], rd["w"], rd["m"], rd["v"]]
        rider_specs += [pl.BlockSpec((N_DEV, rows, cols), lambda i, me_ref: (0, i, 0)),
                        pl.BlockSpec((1, rows, cols), lambda i, me_ref: (me_ref[0], i, 0)), blk, blk, blk]
        if "mask" in rd:
            rider_args.append(rd["mask"])
            rider_specs.append(whole((rows, cols)))
        rider_out_shapes += [jax.ShapeDtypeStruct(rd["w"].shape, F32)] * 4
        rider_out_specs += [blk] * 4
    outs = pl.pallas_call(
        body, name="in_bwd",
        out_shape=(jax.ShapeDtypeStruct((t_tok, D_MODEL), F32), jax.ShapeDtypeStruct((1, D_MODEL), F32),
                   *rider_out_shapes),
        grid_spec=pltpu.PrefetchScalarGridSpec(
            num_scalar_prefetch=1, grid=(steps,),
            in_specs=[row(n) for n in widths] + [whole((IN_PAD, D_MODEL)), row(D_MODEL), row(D_MODEL),
                                                 whole((1, D_MODEL))] + rider_specs
            + [pl.BlockSpec(memory_space=pl.ANY)] * len(deps),
            out_specs=(row(D_MODEL), whole((1, D_MODEL)), *rider_out_specs)),
        compiler_params=_params("arbitrary"))(me, du, dv, dz, dxbc, ddt, w_in, x, dx2, g1, *rider_args, *deps)
    return outs[0], outs[1], [tuple(outs[2 + 4 * r:6 + 4 * r]) for r in range(len(riders))]


def _pad_lanes(a, n):
    return jnp.pad(a, ((0, 0), (0, n - a.shape[1])))


def _local_step(x, target, seq, small, hooks, first_dep=None):
    t_tok = x.shape[0]
    tm = min(TOKEN_TILE, t_tok)
    avg, expand, expand_t, tril, triu = _const_mats()
    g1, g2, g3, g4 = (small[k].reshape(1, D_MODEL) for k in
                      ("norm_mix_pre", "norm_mix_post", "norm_ffn_pre", "norm_ffn_post"))
    tie = (lambda a: a) if first_dep is None else (lambda a: a + first_dep[0, 0])
    lnw = tie(small["gm_ln_w"]).reshape(1, GM_WIDTH)
    lnb = tie(small["gm_ln_b"]).reshape(1, GM_WIDTH)
    causal = jnp.tril(jnp.ones((CHUNK, CHUNK), F32))
    wm = tie(small["gm_w_s"]) * causal
    pair = lambda w: w.reshape(4, 2, CHUNK, CHUNK).transpose(0, 2, 1, 3).reshape(4, CHUNK, 2 * CHUNK).astype(BF16)
    wcat = pair(wm)
    wtcat = pair(jnp.swapaxes(wm, 1, 2))
    bias = jnp.repeat(tie(small["gm_b_s"]).T, HEAD_DIM, axis=1)
    cb = small["conv_b"].reshape(1, CONV_CH)
    dtb = _pad_lanes(tie(small["dt_bias"]).reshape(1, N_HEADS), CHUNK)
    alog = _pad_lanes(tie(small["a_log"]).reshape(1, N_HEADS), CHUNK)
    dskip_exp = jnp.repeat(tie(small["d_skip"]).reshape(1, N_HEADS), HEAD_DIM, axis=1)
    nw = small["ssm_norm_w"].reshape(1, SSM_WIDTH)

    h1 = _prenorm(x, g1, tm, hooks.get("prenorm_after", first_dep))
    w_in_t, conv_w = hooks["mixer_weights"](h1)
    tall = min(2 * tm, t_tok)
    u, v, z, xbc, dtr = _in_proj(h1, w_in_t, tall)
    mix_a = _gmlp_fwd(u, v, lnw, lnb, wcat, bias, avg)
    dep = hooks["gmlp_done"](mix_a) if "gmlp_done" in hooks else None
    mix_b, y_pre, states, pre = _ssd_fwd(z, xbc, dtr, conv_w, cb, dtb, alog, dskip_exp, nw, expand, tril, seq, dep)
    w_out, dep = hooks["mixers_done"](mix_b)
    o, x2, h3 = _out_proj(mix_a, mix_b, w_out, x, g2, g3, tall, dep)
    w_up, w_down = hooks["mlp_weights"](h3)
    tf = FF_TILE
    ra, dd, dy, dg4, loss = _mlp_fwd(h3, w_up, w_down, x2, target, g4, tm, tf)

    da, dx2, do, dg3, dg2 = _mlp_bwd(dd, w_down, ra, w_up, x2, dy, o, g3, g2, tm, tf)
    g_w_down = _wgrad(ra, dd, None, WGRAD_TILE, D_MODEL, t_tok, True, "wgrad_down")
    g_w_up = _wgrad(h3, da, N_DEV, D_MODEL, D_FF // N_DEV, t_tok, False, "wgrad_up")
    dep = hooks["mlp_grads"](g_w_down, g_w_up)
    dmix = _dmix(do, w_out, tall, dep)
    g_w_out = _wgrad_pieces(do, (mix_a, mix_b), WGRAD_TILE, "wgrad_out", dep)
    du, dv, dws, dbt, dlnw, dlnb = _gmlp_bwd(dmix, u, v, lnw, lnb, wcat, wtcat, bias, avg, expand_t)
    dep = hooks["gmlp_grads"](g_w_out, dws)
    dz, dxbc, ddt, dcw, dcb, ddtb, dalog, ddsk, dnw = _ssd_bwd(
        dmix, z, xbc, pre, dtr, y_pre, states, conv_w, cb, dtb, alog, dskip_exp, nw, expand, expand_t, tril, triu, seq,
        dep)
    g_w_in = _wgrad_in_chunked(h1, (du, dv, dz, dxbc, ddt), WGRAD_TILE, t_tok // 2, dep)
    dep = hooks["in_grads"](g_w_in, dcw[0:4])
    riders = hooks["arrived_updates"](dep) if "arrived_updates" in hooks else []
    me = hooks.get("me", jnp.zeros((1,), jnp.int32))
    grad_x, dg1, updates = _in_bwd(du, dv, dz, dxbc, ddt, w_in_t, x, dx2, g1, tm, me, riders, dep)

    grads = dict(
        updates=updates,
        w_in=g_w_in, w_out=g_w_out, w_up=g_w_up, w_down=g_w_down, conv_w=dcw[0:4],
        norm_mix_pre=dg1, norm_mix_post=dg2, norm_ffn_pre=dg3, norm_ffn_post=dg4, gm_ln_w=dlnw, gm_ln_b=dlnb,
        gm_w_s=dws, gm_b_s=dbt, conv_b=dcb, dt_bias=ddtb, a_log=dalog, d_skip=ddsk, ssm_norm_w=dnw)
    return loss[0, 0], grad_x, grads


_WEIGHTS = ("norm_mix_pre", "w_in", "gm_ln_w", "gm_ln_b", "gm_w_s", "gm_b_s", "conv_w", "conv_b", "dt_bias", "a_log",
            "d_skip", "ssm_norm_w", "w_out", "norm_mix_post", "norm_ffn_pre", "w_up", "w_down", "norm_ffn_post")
_SLAB_ROWS = (("norm_mix_pre", 1024), ("norm_mix_post", 1024), ("norm_ffn_pre", 1024), ("norm_ffn_post", 1024),
              ("conv_b", 1024), ("ssm_norm_w", 512), ("gm_ln_w", 512), ("gm_ln_b", 512), ("dt_bias", 8), ("a_log", 8),
              ("d_skip", 8))
_SLAB_LOSS_ROW = len(_SLAB_ROWS)
_SLAB_BS_ROW = 16
_SMALL_PARAMS = tuple(name for name, _ in _SLAB_ROWS) + ("gm_b_s",)
_LN_PARAMS = ("gm_ln_w", "gm_ln_b")


_SLAB_CONV_ROW = _SLAB_LOSS_ROW + 1


def _pack_slab(g, loss_part):
    rows = [_pad_lanes(g[name], D_MODEL) for name, _ in _SLAB_ROWS]
    rows.append(jnp.broadcast_to(loss_part, (1, D_MODEL)))
    rows.append(g["conv_w"])
    assert sum(r.shape[0] for r in rows) == _SLAB_BS_ROW
    rows.append(_pad_lanes(g["gm_b_s"].T[0:N_HEADS], D_MODEL))
    return jnp.concatenate(rows, axis=0)


def _adamw_slab(parts, me, w, m, v):
    names = _SMALL_PARAMS + ("conv_w",)
    shapes = [w[k].shape for k in names]
    unfold = np.zeros((GM_WIDTH, HEAD_DIM), np.float32)
    for h in range(N_HEADS):
        unfold[h * HEAD_DIM:(h + 1) * HEAD_DIM, :] = np.eye(HEAD_DIM)
    unfold = jnp.asarray(unfold, dtype=BF16)
    n = len(names)
    shard = CONV_CH // N_DEV

    def body(me_ref, p_ref, unfold_ref, *refs):
        w_refs, m_refs, v_refs = refs[:n], refs[n:2 * n], refs[2 * n:3 * n]
        outs = refs[3 * n:]
        g_all = p_ref[0]
        for j in range(1, N_DEV):
            g_all = g_all + p_ref[j]
        lane = lax.broadcasted_iota(jnp.int32, (N_HEADS, GM_WIDTH), 1)
        head = lax.broadcasted_iota(jnp.int32, (N_HEADS, GM_WIDTH), 0)
        own_lanes = jnp.logical_and(lane >= head * HEAD_DIM, lane < (head + 1) * HEAD_DIM)
        mine = pl.ds(pl.multiple_of(me_ref[0] * shard, shard), shard)
        for i, name in enumerate(names):
            if name == "gm_b_s":
                g = g_all[_SLAB_BS_ROW:_SLAB_BS_ROW + N_HEADS, 0:CHUNK]
            elif name == "conv_w":
                g = p_ref[0, _SLAB_CONV_ROW:_SLAB_CONV_ROW + 4, mine]
                for j in range(1, N_DEV):
                    g = g + p_ref[j, _SLAB_CONV_ROW:_SLAB_CONV_ROW + 4, mine]
            else:
                row = [r for r, (k, _) in enumerate(_SLAB_ROWS) if k == name][0]
                g = g_all[row:row + 1, 0:dict(_SLAB_ROWS)[name]]
                if name in _LN_PARAMS:
                    g = _split_dot(jnp.where(own_lanes, g, 0.0), unfold_ref[...], 3)
            d, mn, vn = _adamw_math(w_refs[i][...], g, m_refs[i][...], v_refs[i][...])
            for o_ref, val in zip(outs[4 * i:4 * i + 4], (g, d, mn, vn)):
                o_ref[...] = val
        outs[-1][...] = g_all[_SLAB_LOSS_ROW:_SLAB_LOSS_ROW + 1, 0:128]

    def whole(shape):
        nd = len(shape)
        return pl.BlockSpec(shape, lambda i, me_ref: (0,) * nd)

    ins = [parts, unfold] + [d[k] for d in (w, m, v) for k in names]
    out_shape = tuple(jax.ShapeDtypeStruct(s, F32) for s in shapes for _ in range(4)) + (
        jax.ShapeDtypeStruct((1, 128), F32),)
    outs = pl.pallas_call(
        body, name="adamw_small", out_shape=out_shape,
        grid_spec=pltpu.PrefetchScalarGridSpec(
            num_scalar_prefetch=1, grid=(1,), in_specs=[whole(a.shape) for a in ins],
            out_specs=tuple(whole(s.shape) for s in out_shape)),
        compiler_params=_params("arbitrary"))(me, *ins)
    return {k: tuple(outs[4 * i:4 * i + 4]) for i, k in enumerate(names)}, outs[-1][0, 0]


def kernel(x, norm_mix_pre, w_in, gm_ln_w, gm_ln_b, gm_w_s, gm_b_s, conv_w, conv_b, dt_bias, a_log, d_skip, ssm_norm_w, w_out, norm_mix_post, norm_ffn_pre, w_up, w_down, norm_ffn_post, loss_target, m_norm_mix_pre, m_w_in, m_gm_ln_w, m_gm_ln_b, m_gm_w_s, m_gm_b_s, m_conv_w, m_conv_b, m_dt_bias, m_a_log, m_d_skip, m_ssm_norm_w, m_w_out, m_norm_mix_post, m_norm_ffn_pre, m_w_up, m_w_down, m_norm_ffn_post, v_norm_mix_pre, v_w_in, v_gm_ln_w, v_gm_ln_b, v_gm_w_s, v_gm_b_s, v_conv_w, v_conv_b, v_dt_bias, v_a_log, v_d_skip, v_ssm_norm_w, v_w_out, v_norm_mix_post, v_norm_ffn_pre, v_w_up, v_w_down, v_norm_ffn_post):
    w = dict(norm_mix_pre=norm_mix_pre, w_in=w_in, gm_ln_w=gm_ln_w, gm_ln_b=gm_ln_b, gm_w_s=gm_w_s, gm_b_s=gm_b_s, conv_w=conv_w, conv_b=conv_b, dt_bias=dt_bias, a_log=a_log, d_skip=d_skip, ssm_norm_w=ssm_norm_w, w_out=w_out, norm_mix_post=norm_mix_post, norm_ffn_pre=norm_ffn_pre, w_up=w_up, w_down=w_down, norm_ffn_post=norm_ffn_post)
    m = dict(norm_mix_pre=m_norm_mix_pre, w_in=m_w_in, gm_ln_w=m_gm_ln_w, gm_ln_b=m_gm_ln_b, gm_w_s=m_gm_w_s, gm_b_s=m_gm_b_s, conv_w=m_conv_w, conv_b=m_conv_b, dt_bias=m_dt_bias, a_log=m_a_log, d_skip=m_d_skip, ssm_norm_w=m_ssm_norm_w, w_out=m_w_out, norm_mix_post=m_norm_mix_post, norm_ffn_pre=m_norm_ffn_pre, w_up=m_w_up, w_down=m_w_down, norm_ffn_post=m_norm_ffn_post)
    v = dict(norm_mix_pre=v_norm_mix_pre, w_in=v_w_in, gm_ln_w=v_gm_ln_w, gm_ln_b=v_gm_ln_b, gm_w_s=v_gm_w_s, gm_b_s=v_gm_b_s, conv_w=v_conv_w, conv_b=v_conv_b, dt_bias=v_dt_bias, a_log=v_a_log, d_skip=v_d_skip, ssm_norm_w=v_ssm_norm_w, w_out=v_w_out, norm_mix_post=v_norm_mix_post, norm_ffn_pre=v_norm_ffn_pre, w_up=v_w_up, w_down=v_w_down, norm_ffn_post=v_norm_ffn_post)
    n_batch, seq, _ = x.shape
    shard_in = IN_COLS // N_DEV

    me = (4 * lax.axis_index("x") + 2 * lax.axis_index("y") + lax.axis_index("c")).astype(jnp.int32).reshape(1)

    def in_slot(own):
        return lax.dynamic_update_slice(lax.empty((N_DEV,) + own.shape, own.dtype), own[None],
                                        (me[0],) + (0,) * own.ndim)

    w_in_sh = w_in[0].T
    first = [_cast_to_slot(w_in_sh, me, shard_in, "cast_w_in"), in_slot(conv_w[0])]
    ici_1, tok_ici_1 = _exchange_start(first, [True] * 2, _SAME_CORE_PEERS, "gather_mix_ici_start")
    cast_out = _cast_to_slot(w_out[0], me, 128, "cast_w_out", dep=tok_ici_1)
    cast_up = _cast_to_slot(w_up[0], me, 1024, "cast_w_up", cols=True, dep=cast_out)
    second = [cast_out, cast_up, _cast_to_slot(w_down[0], me, 512, "cast_w_down", dep=cast_up)]
    gathering = {}

    def mixer_weights(after):
        bufs = [buf for buf, _ in _exchange_wait(ici_1, after, "gather_mix_ici_wait")]
        d2d_1, tok_d2d_1 = _exchange_start(bufs, [True] * 2, _SIBLING_FORWARD, "gather_mix_d2d_start")
        gathering["late_ici"], tok_ici_2 = _exchange_start(
            second, [True] * 3, _SAME_CORE_PEERS, "gather_late_ici_start", dep=tok_d2d_1)
        (_, ag_in), (_, ag_conv) = _exchange_wait(d2d_1, tok_ici_2, "gather_mix_d2d_wait")
        w_in_t = jnp.pad(ag_in.reshape(IN_COLS, D_MODEL), ((0, IN_PAD - IN_COLS), (0, 0)))
        return w_in_t, ag_conv.transpose(1, 0, 2).reshape(4, CONV_CH)

    def gmlp_done(after):
        ((buf, _),) = _exchange_wait(gathering["late_ici"], after, "gather_out_ici_wait", only=(0,))
        gathering["out"], tok = _exchange_start([buf], [True], _SIBLING_FORWARD, "gather_out_d2d_start")
        return tok

    def mixers_done(after):
        bufs = [buf for buf, _ in _exchange_wait(gathering["late_ici"], after, "gather_mlp_ici_wait", only=(1, 2))]
        gathering["mlp"], tok = _exchange_start(bufs, [True] * 2, _SIBLING_FORWARD, "gather_mlp_d2d_start")
        ((_, ag_out),) = _exchange_wait(gathering["out"], tok, "gather_out_d2d_wait")
        return ag_out.reshape(D_MODEL, D_MODEL), tok

    def mlp_weights(after):
        (_, ag_up), (_, ag_down) = _exchange_wait(gathering["mlp"], after, "gather_mlp_d2d_wait")
        return ag_up, ag_down.reshape(D_FF, D_MODEL)

    sent = {}

    def mlp_grads(g_w_down, g_w_up):
        sent["mlp"], tok = _exchange_start(
            [g_w_down.reshape(N_DEV, D_FF // N_DEV, D_MODEL), g_w_up], [False, False], _ALL_PEERS, "grads_mlp_start")
        return tok

    def gmlp_grads(g_w_out, g_w_s):
        sent["gmlp"], tok = _exchange_start(
            [g_w_out.reshape(N_DEV, D_MODEL // N_DEV, D_MODEL), in_slot(g_w_s.astype(BF16))], [False, True], _ALL_PEERS,
            "grads_gmlp_start")
        return tok

    def in_grads(g_w_in_t, g_conv_w):
        g_in_blk = g_w_in_t[:IN_COLS].reshape(N_DEV, shard_in, D_MODEL)
        sent["in"], tok = _exchange_start([g_in_blk], [False], _ALL_PEERS, "grads_in_start")
        return tok

    def arrived_updates(after):
        (own_down, p_down), (own_up, p_up) = _exchange_wait(sent["mlp"], after, "grads_mlp_wait")
        (own_out, p_out), (_, p_ws) = _exchange_wait(sent["gmlp"], own_up, "grads_gmlp_wait")
        rows = lambda t: t.reshape(t.shape[:-3] + (N_HEADS * CHUNK, CHUNK))
        return [dict(parts=p_up, own=own_up, w=w_up[0], m=m_w_up[0], v=v_w_up[0]),
                dict(parts=p_down, own=own_down, w=w_down[0], m=m_w_down[0], v=v_w_down[0]),
                dict(parts=p_out, own=own_out, w=w_out[0], m=m_w_out[0], v=v_w_out[0]),
                dict(parts=rows(p_ws), own=rows(p_ws), w=rows(gm_w_s[0]), m=rows(m_gm_w_s[0]), v=rows(v_gm_w_s[0]),
                     mask=jnp.tril(jnp.ones((CHUNK, CHUNK), F32)))]

    small = {k: w[k][0] for k in _SMALL_PARAMS + ("gm_w_s",)}
    loss_part, grad_x, g = _local_step(
        x.reshape(n_batch * seq, D_MODEL), loss_target.reshape(n_batch * seq, D_MODEL), seq, small,
        dict(mixer_weights=mixer_weights, gmlp_done=gmlp_done, mixers_done=mixers_done, mlp_weights=mlp_weights,
             mlp_grads=mlp_grads, gmlp_grads=gmlp_grads, in_grads=in_grads, arrived_updates=arrived_updates, me=me,
             prenorm_after=second[2]), first_dep=tok_ici_1)

    sent_rows, tok_rows = _exchange_start([in_slot(_pack_slab(g, loss_part))], [True], _ALL_PEERS, "grads_rows_start")
    res = dict(zip(("w_up", "w_down", "w_out", "gm_w_s"), g["updates"]))
    ((own_in, p_in),) = _exchange_wait(sent["in"], tok_rows, "grads_in_wait")
    lying = lambda t: jnp.transpose(t, (2, 0, 1))
    upd_in = _adamw_reduce(p_in, own_in, me, lying(w_in), lying(m_w_in), lying(v_w_in), "adamw_w_in")
    res["w_in"] = tuple(jnp.transpose(t, (1, 2, 0)) for t in upd_in)
    ((_, p_rows),) = _exchange_wait(sent_rows, upd_in[1], "grads_rows_wait")
    flat = lambda t: t[0] if t.ndim == 3 else t
    small_res, loss = _adamw_slab(
        p_rows, me, *({k: flat(d[k]) for k in _SMALL_PARAMS + ("conv_w",)} for d in (w, m, v)))
    res.update(small_res)
    res = {k: tuple(r.reshape(w[k].shape) for r in res[k]) for k in _WEIGHTS}

    outs = [loss, grad_x.reshape(x.shape)]
    for part in range(4):
        outs.extend(res[k][part] for k in _WEIGHTS)
    return tuple(outs)
```

```python
import functools

import jax
import jax.numpy as jnp
import numpy as np
from jax import lax
from jax.experimental import pallas as pl
from jax.experimental.pallas import tpu as pltpu

F32 = jnp.float32
BF16 = jnp.bfloat16

D_MODEL = 1024
GM_WIDTH = 512
SSM_WIDTH = 512
CONV_CH = 1024
N_HEADS = 8
HEAD_DIM = 64
N_STATE = 128
CHUNK = 128
D_FF = 4096
IN_COLS = 2568
IN_PAD = 2688
N_DEV = 8
EPS = 1e-6
ADAM_LR, ADAM_B1, ADAM_B2, ADAM_EPS, ADAM_WD, ADAM_STEP = 0.001, 0.9, 0.999, 1e-08, 0.01, 10
VMEM_LIMIT_BYTES = 56 * 1024 * 1024
TOKEN_TILE = 512
FF_TILE = 2048
WGRAD_TILE = 512
_NT = (((1,), (1,)), ((), ()))
_TN = (((0,), (0,)), ((), ()))


def _params(*sem):
    return pltpu.CompilerParams(dimension_semantics=sem or None, vmem_limit_bytes=VMEM_LIMIT_BYTES)


def _dot(a, b, dims=None):
    if dims is None:
        return jnp.dot(a, b, preferred_element_type=F32)
    return lax.dot_general(a, b, dims, preferred_element_type=F32)


def _split_terms(x, terms):
    out, rem = [], x
    for i in range(terms):
        hi = rem.astype(BF16)
        out.append(hi)
        if i + 1 < terms:
            rem = rem - hi.astype(F32)
    return out


def _split_dot(x, m, terms):
    acc = None
    for hi in _split_terms(x, terms):
        part = _dot(hi, m)
        acc = part if acc is None else acc + part
    return acc


def _split_dot_left(m, x, terms):
    acc = None
    for hi in _split_terms(x, terms):
        part = _dot(m, hi)
        acc = part if acc is None else acc + part
    return acc


def _gelu_and_grad(x):
    c = 0.7978845608028654
    inner = c * (x + 0.044715 * x * x * x)
    t = jnp.tanh(inner)
    g = 0.5 * x * (1.0 + t)
    dg = 0.5 * (1.0 + t) + 0.5 * x * (1.0 - t * t) * c * (1.0 + 3.0 * 0.044715 * x * x)
    return g, dg


def _softplus(x):
    return jnp.maximum(x, 0.0) + jnp.log(1.0 + jnp.exp(-jnp.abs(x)))


def _rsum(x):
    return jnp.sum(x, axis=0, keepdims=True)


def _acc_rows(ref, part, first):
    val = jnp.broadcast_to(part, ref.shape)

    @pl.when(first)
    def _():
        ref[...] = val

    @pl.when(jnp.logical_not(first))
    def _():
        ref[...] += val


def _rms_bwd(n, g, dout):
    r = lax.rsqrt(jnp.mean(n * n, axis=-1, keepdims=True) + EPS)
    nh = n * r
    dg = dout * g
    dn = r * (dg - nh * jnp.mean(dg * nh, axis=-1, keepdims=True))
    return dn, _rsum(dout * nh)


def _const_mats():
    avg = np.kron(np.eye(4), np.full((HEAD_DIM, HEAD_DIM), 1.0 / HEAD_DIM))
    expand = np.zeros((CHUNK, SSM_WIDTH), np.float32)
    for h in range(N_HEADS):
        expand[h, h * HEAD_DIM:(h + 1) * HEAD_DIM] = 1.0
    tril = np.tril(np.ones((CHUNK, CHUNK), np.float32))
    as_bf16 = lambda a: jnp.asarray(a, dtype=BF16)
    return as_bf16(avg), as_bf16(expand), as_bf16(expand.T), as_bf16(tril), as_bf16(tril.T)


def _full(shape):
    nd = len(shape)
    return pl.BlockSpec(shape, lambda *_: (0,) * nd)


_HBM = pl.BlockSpec(memory_space=pltpu.HBM)
_SEM = pl.BlockSpec(memory_space=pltpu.SEMAPHORE)
_ALL_PEERS = tuple((k, 0) for k in range(1, N_DEV))
_SAME_CORE_PEERS = ((2, 0), (4, 0), (6, 0))
_SIBLING_FORWARD = ((1, 0), (1, 2), (1, 4), (1, 6))


def _flip(j, k):
    for bit in (4, 2, 1):
        if k & bit:
            j = j + bit - 2 * (j & bit)
    return j


def _copies(src, land, send_sems, recv_sems, hops, slots=None):
    x, y, c = lax.axis_index("x"), lax.axis_index("y"), lax.axis_index("c")
    me = 4 * x + 2 * y + c
    slots = range(len(src)) if slots is None else slots
    out = []
    for t in range(len(src)):
        for i, (k, b) in enumerate(hops):
            pos = (1 - x if k & 4 else x, 1 - y if k & 2 else y, 1 - c if k & 1 else c)
            peer = _flip(me, k)
            sem = slots[t] * len(hops) + i
            mk = functools.partial(pltpu.make_async_remote_copy, send_sem=send_sems.at[sem], recv_sem=recv_sems.at[sem],
                                   device_id=pos, device_id_type=pl.DeviceIdType.MESH)
            if land[t] is None and src[t].shape[0] != N_DEV:
                width = src[t].shape[1] // N_DEV
                slab = lambda j: src[t].at[:, pl.ds(pl.multiple_of(j * width, 128), width)]
                mine = functools.partial(mk, src_ref=slab(_flip(me, b)), dst_ref=slab(_flip(me, b)))
                theirs = functools.partial(mk, src_ref=slab(_flip(peer, b)), dst_ref=slab(_flip(peer, b)))
            elif land[t] is None:
                mine = functools.partial(mk, src_ref=src[t].at[_flip(me, b)], dst_ref=src[t].at[_flip(me, b)])
                theirs = functools.partial(mk, src_ref=src[t].at[_flip(peer, b)], dst_ref=src[t].at[_flip(peer, b)])
            else:
                assert b == 0
                mine = functools.partial(mk, src_ref=src[t].at[peer], dst_ref=land[t].at[me])
                theirs = functools.partial(mk, src_ref=src[t].at[peer], dst_ref=land[t].at[peer])
            out.append((mine, theirs))
    return out


def _exchange_start(srcs, inplace, peers, name, dep=None):
    n = len(srcs)
    lands = [None if ip else pltpu.with_memory_space_constraint(lax.empty(s.shape, s.dtype), pltpu.HBM)
             for s, ip in zip(srcs, inplace)]
    real_lands = [l for l in lands if l is not None]
    n_l = len(real_lands)
    deps = [] if dep is None else [dep]

    def body(*refs):
        src = refs[:n]
        land_refs = list(refs[n:n + n_l])
        send_sems, recv_sems = refs[n + n_l + len(deps)], refs[n + n_l + len(deps) + 1]
        token = refs[-1]
        land = [None if ip else land_refs.pop(0) for ip in inplace]
        for mine, _ in _copies(src, land, send_sems, recv_sems, peers):
            mine().start()
        token[...] = jnp.zeros_like(token)

    sem_t = pltpu.SemaphoreType.DMA((n * len(peers),))
    outs = pl.pallas_call(
        body, name=name,
        out_shape=(sem_t, sem_t) + tuple(pltpu.HBM(a.shape, a.dtype) for a in list(srcs) + real_lands)
        + (jax.ShapeDtypeStruct((8, 128), F32),),
        in_specs=[_HBM] * (n + n_l) + [pl.BlockSpec(memory_space=pl.ANY)] * len(deps),
        out_specs=(_SEM, _SEM) + (_HBM,) * (n + n_l) + (pl.BlockSpec(memory_space=pltpu.VMEM),),
        input_output_aliases={i: 2 + i for i in range(n + n_l)},
        compiler_params=pltpu.CompilerParams(has_side_effects=pltpu.SideEffectType.DATAFLOW_SIDE_EFFECTING),
    )(*[pltpu.with_memory_space_constraint(s, pltpu.HBM) for s in srcs], *real_lands, *deps)
    handle = dict(send=outs[0], recv=outs[1], srcs=outs[2:2 + n], lands=outs[2 + n:2 + n + n_l], inplace=inplace,
                  peers=peers)
    return handle, outs[-1]


def _exchange_wait(handle, after, name, only=None):
    srcs, lands, inplace, peers = handle["srcs"], handle["lands"], handle["inplace"], handle["peers"]
    slots = None
    if only is not None:
        assert all(inplace)
        slots, srcs, inplace = list(only), [srcs[t] for t in only], [True] * len(only)
    n, n_l = len(srcs), len(lands)

    def body(*refs):
        src = refs[:n]
        land_refs = list(refs[n:n + n_l])
        send_sems, recv_sems = refs[n + n_l], refs[n + n_l + 1]
        land = [None if ip else land_refs.pop(0) for ip in inplace]
        for mine, theirs in _copies(src, land, send_sems, recv_sems, peers, slots):
            mine().wait_send()
            theirs().wait_recv()

    outs = pl.pallas_call(
        body, name=name, out_shape=tuple(pltpu.HBM(a.shape, a.dtype) for a in list(srcs) + list(lands)),
        in_specs=[_HBM] * (n + n_l) + [_SEM, _SEM, pl.BlockSpec(memory_space=pl.ANY)],
        out_specs=(_HBM,) * (n + n_l), input_output_aliases={i: i for i in range(n + n_l)},
        compiler_params=pltpu.CompilerParams(has_side_effects=pltpu.SideEffectType.DATAFLOW_SIDE_EFFECTING),
    )(*srcs, *lands, handle["send"], handle["recv"], after)
    res, land_out = [], list(outs[n:])
    for t in range(n):
        res.append((outs[t], outs[t] if inplace[t] else land_out.pop(0)))
    return res


def _cast_to_slot(w, me, rows, name, cols=False, dep=None):
    r, cdim = w.shape[0], w.shape[-1]
    deps = [] if dep is None else [dep]

    def body(me_ref, w_ref, *rest):
        o_ref = rest[-1]
        if cols:
            o_ref[...] = w_ref[...].astype(BF16)
        else:
            o_ref[0] = w_ref[...].reshape(rows, cdim).astype(BF16)

    if cols:
        out_shape = jax.ShapeDtypeStruct((r, N_DEV * cdim), BF16)
        out_spec = pl.BlockSpec((rows, cdim), lambda i, me_ref: (i, me_ref[0]))
    else:
        out_shape = jax.ShapeDtypeStruct((N_DEV, r, cdim), BF16)
        out_spec = pl.BlockSpec((1, rows, cdim), lambda i, me_ref: (me_ref[0], i, 0))
    return pl.pallas_call(
        body, name=name, out_shape=out_shape,
        grid_spec=pltpu.PrefetchScalarGridSpec(
            num_scalar_prefetch=1, grid=(r // rows,),
            in_specs=[pl.BlockSpec((rows, cdim), lambda i, me_ref: (i, 0)) if w.ndim == 2 else
                      pl.BlockSpec((rows, 1, cdim), lambda i, me_ref: (i, 0, 0))]
            + [pl.BlockSpec(memory_space=pl.ANY)] * len(deps), out_specs=out_spec),
        compiler_params=_params("parallel"))(me, w, *deps)


def _adamw_math(w, g, m, v):
    m = ADAM_B1 * m + (1.0 - ADAM_B1) * g
    v = ADAM_B2 * v + (1.0 - ADAM_B2) * (g * g)
    m_hat = m / (1.0 - ADAM_B1 ** ADAM_STEP)
    v_hat = v / (1.0 - ADAM_B2 ** ADAM_STEP)
    delta = -ADAM_LR * (m_hat / (jnp.sqrt(v_hat) + ADAM_EPS) + ADAM_WD * w)
    return delta, m, v


def _sum_parts(me, p_ref, own):
    g = None
    for j in range(N_DEV):
        term = (p_ref[j] if own is None else jnp.where(me == j, own, p_ref[j])).astype(F32)
        g = term if g is None else g + term
    return g


def _adamw_reduce(parts, own, me, w, m, v, name):
    r, _, cdim = w.shape

    def body(me_ref, p_ref, own_ref, w_ref, m_ref, v_ref, g_out, d_out, m_out, v_out):
        g = _sum_parts(me_ref[0], p_ref, own_ref[0]).reshape(r, 1, cdim)
        d, mn, vn = _adamw_math(w_ref[...], g, m_ref[...], v_ref[...])
        g_out[...] = g
        d_out[...] = d
        m_out[...] = mn
        v_out[...] = vn

    blk = pl.BlockSpec((r, 1, cdim), lambda i, me_ref: (0, 0, 0))
    return pl.pallas_call(
        body, name=name, out_shape=(jax.ShapeDtypeStruct(w.shape, F32),) * 4,
        grid_spec=pltpu.PrefetchScalarGridSpec(
            num_scalar_prefetch=1, grid=(1,),
            in_specs=[pl.BlockSpec((N_DEV, r, cdim), lambda i, me_ref: (0, 0, 0)),
                      pl.BlockSpec((1, r, cdim), lambda i, me_ref: (me_ref[0], 0, 0)), blk, blk, blk],
            out_specs=(blk,) * 4),
        compiler_params=_params("arbitrary"))(me, parts, own, w, m, v)


_IN_SPLITS = ((0, 512), (512, 1024), (1024, 1536), (1536, 2560), (2560, IN_PAD))


def _prenorm(x, g1, tm, dep=None):
    t_tok = x.shape[0]
    deps = [] if dep is None else [dep]

    def body(x_ref, g_ref, *rest):
        xv = x_ref[...]
        r = lax.rsqrt(jnp.mean(xv * xv, axis=-1, keepdims=True) + EPS)
        rest[-1][...] = (xv * r * g_ref[...]).astype(BF16)

    row = pl.BlockSpec((tm, D_MODEL), lambda i: (i, 0))
    return pl.pallas_call(
        body, name="prenorm", grid=(t_tok // tm,), out_shape=jax.ShapeDtypeStruct((t_tok, D_MODEL), BF16),
        in_specs=[row, _full((1, D_MODEL))] + [pl.BlockSpec(memory_space=pl.ANY)] * len(deps), out_specs=row,
        compiler_params=_params("parallel"))(x, g1, *deps)


def _in_proj(h1, w_in, tm):
    t_tok = h1.shape[0]

    def body(h_ref, w_ref, *outs):
        h = h_ref[...]
        for (a, b), o_ref in zip(_IN_SPLITS, outs):
            o_ref[...] = _dot(h, w_ref[a:b, :], _NT).astype(o_ref.dtype)

    row = lambda n: pl.BlockSpec((tm, n), lambda i: (i, 0))
    widths = [b - a for a, b in _IN_SPLITS]
    dtypes = (BF16, BF16, BF16, F32, F32)
    return pl.pallas_call(
        body, name="in_proj", grid=(t_tok // tm,),
        out_shape=tuple(jax.ShapeDtypeStruct((t_tok, n), dt) for n, dt in zip(widths, dtypes)),
        in_specs=[row(D_MODEL), _full((IN_PAD, D_MODEL))], out_specs=tuple(row(n) for n in widths),
        compiler_params=_params("parallel"))(h1, w_in)


def _lane_masks():
    lane = lax.broadcasted_iota(jnp.int32, (1, 2 * HEAD_DIM), 1)
    left = (lane < HEAD_DIM).astype(F32)
    return left, 1.0 - left


def _stack_pair(v, m_l, m_r):
    return jnp.concatenate([v * m_l, v * m_r], axis=0).astype(BF16)


def _head_mean(x, avg):
    n = avg.shape[0]
    return jnp.concatenate([_split_dot(x[:, n * i:n * (i + 1)], avg, 2) for i in range(x.shape[1] // n)], axis=1)


def _gmlp_common(u, v, lnw, lnb, avg, wcat_ref, bias, m_l, m_r):
    ug, dug = _gelu_and_grad(u)
    vg, dvg = _gelu_and_grad(v)
    mu = _head_mean(vg, avg)
    vc = vg - mu
    var = _head_mean(vc * vc, avg)
    rstd = lax.rsqrt(var + EPS)
    vhat = vc * rstd
    vn = vhat * lnw + lnb
    rows = []
    for r in range(u.shape[0] // CHUNK):
        cols = []
        for j in range(N_HEADS // 2):
            pair = vn[CHUNK * r:CHUNK * (r + 1), 128 * j:128 * (j + 1)]
            cols.append(_dot(wcat_ref[j], _stack_pair(pair, m_l, m_r)))
        rows.append(jnp.concatenate(cols, axis=1) + bias)
    mixed = jnp.concatenate(rows, axis=0)
    return ug, dug, dvg, rstd, vhat, vn, mixed


_GMLP_ROWS = 4 * CHUNK


def _gmlp_fwd(u, v, lnw, lnb, wcat, bias, avg):
    t_tok = u.shape[0]
    tm = min(_GMLP_ROWS, t_tok)

    def body(u_ref, v_ref, lnw_ref, lnb_ref, wcat_ref, bias_ref, avg_ref, o_ref):
        m_l, m_r = _lane_masks()
        ug, _, _, _, _, _, mixed = _gmlp_common(
            u_ref[...].astype(F32), v_ref[...].astype(F32), lnw_ref[...], lnb_ref[...], avg_ref[...], wcat_ref,
            bias_ref[...], m_l, m_r)
        o_ref[...] = (ug * mixed).astype(BF16)

    row = pl.BlockSpec((tm, GM_WIDTH), lambda i: (i, 0))
    return pl.pallas_call(
        body, name="gmlp_fwd", grid=(t_tok // tm,), out_shape=jax.ShapeDtypeStruct((t_tok, GM_WIDTH), BF16),
        in_specs=[row, row, _full((1, GM_WIDTH)), _full((1, GM_WIDTH)), _full(wcat.shape), _full(bias.shape),
                  _full(avg.shape)],
        out_specs=row, compiler_params=_params("parallel"))(u, v, lnw, lnb, wcat, bias, avg)


def _shift_rows(x, edge, j, down):
    groups, cols = x.shape[0] // 8, x.shape[1]
    amount = j if down else 8 - j
    rot = pltpu.roll(x.reshape(groups, 8, cols), amount, axis=1)
    edge_rot = pltpu.roll(edge, amount, axis=0)[None]
    sub = lax.broadcasted_iota(jnp.int32, (1, 8, 1), 1)
    if down:
        out = jnp.where(sub < j, jnp.concatenate([edge_rot, rot[:-1]], axis=0), rot)
    else:
        out = jnp.where(sub < 8 - j, rot, jnp.concatenate([rot[1:], edge_rot], axis=0))
    return out.reshape(x.shape)


def _conv_pre(xbc, tail, cw_ref, cb):
    taps = [_shift_rows(xbc, tail, 3 - k, True) for k in range(3)] + [xbc]
    return cb + cw_ref[0:1, :] * taps[0] + cw_ref[1:2, :] * taps[1] + cw_ref[2:3, :] * taps[2] + cw_ref[3:4, :] * taps[3]


def _ssd_common(pre, dtr, dtb, alog, expand, tril):
    q = CHUNK
    sg = jax.nn.sigmoid(pre)
    act = pre * sg
    lane = lax.broadcasted_iota(jnp.int32, (1, CHUNK), 1)
    a_row = jnp.where(lane < N_HEADS, -jnp.exp(alog), 0.0)
    dtp = dtr + dtb
    dt = _softplus(dtp)
    a_cs = _split_dot_left(tril, dt * a_row, 3)
    a_cs_t = a_cs.T
    dt_exp = _split_dot(dt, expand, 3)
    a_exp = _split_dot(a_cs, expand, 3)
    a_end = a_exp[q - 1:q, :]
    li = lax.broadcasted_iota(jnp.int32, (q, q), 0)
    si = lax.broadcasted_iota(jnp.int32, (q, q), 1)
    causal = si <= li
    decay = []
    for h in range(N_HEADS):
        seg = a_cs[:, h:h + 1] - a_cs_t[h:h + 1, :]
        decay.append(jnp.where(causal, jnp.exp(jnp.minimum(seg, 0.0)), 0.0))
    return dict(pre=pre, sg=sg, act=act, a_row=a_row, dtp=dtp, dt=dt, dt_exp=dt_exp, a_exp=a_exp,
                e=jnp.exp(a_exp), w_end=jnp.exp(a_end - a_exp), cd=jnp.exp(a_end), decay=decay)


def _ssd_specs(t_tok, seq, reverse):
    nb, nc = t_tok // seq, seq // CHUNK

    def chunk(c):
        return nc - 1 - c if reverse else c

    def row(n, col=0):
        return pl.BlockSpec((nb, CHUNK, n), lambda c: (0, chunk(c), col))

    tail = pl.BlockSpec((nb, 8, CONV_CH), lambda c: (0, jnp.maximum(chunk(c) * (CHUNK // 8) - 1, 0), 0))
    states = pl.BlockSpec((nb, 1, N_STATE, SSM_WIDTH), lambda c: (0, chunk(c), 0, 0))
    fold = lambda a: a.reshape(nb, seq, a.shape[-1])
    unfold = lambda a: a.reshape(t_tok, a.shape[-1])
    return nb, nc, row, tail, states, fold, unfold


def _ssd_fwd(z, xbc, dtr, cw, cb, dtb, alog, dskip_exp, nw, expand, tril, seq, dep=None):
    t_tok = z.shape[0]
    nb, nc, row, tail, states_spec, fold, unfold = _ssd_specs(t_tok, seq, False)

    def body(z_ref, xbc_ref, tail_ref, dtr_ref, cw_ref, cb_ref, dtb_ref, alog_ref, dsk_ref, nw_ref, exp_ref,
             tril_ref, o_ref, y_ref, st_ref, pre_ref, state_ref):
        c = pl.program_id(0)

        @pl.when(c == 0)
        def _():
            state_ref[...] = jnp.zeros_like(state_ref)

        m_l, m_r = _lane_masks()
        for s in range(nb):
            pre = _conv_pre(xbc_ref[s], jnp.where(c == 0, 0.0, tail_ref[s]), cw_ref, cb_ref[...])
            pre_ref[s] = pre
            f = _ssd_common(pre, dtr_ref[s], dtb_ref[...], alog_ref[...], exp_ref[...], tril_ref[...])
            act = f["act"]
            xs = act[:, :SSM_WIDTH]
            xdt = xs * f["dt_exp"]
            xw = xdt * f["w_end"]
            state = state_ref[s]
            st_ref[s, 0] = state
            ydiag, yoff, snew = [], [], []
            for g in range(2):
                bg = act[:, 512 + 128 * g:640 + 128 * g].astype(BF16)
                cg = act[:, 768 + 128 * g:896 + 128 * g].astype(BF16)
                cb_mat = _dot(cg, bg, _NT)
                for pr in range(2):
                    h0 = 4 * g + 2 * pr
                    gcat = jnp.concatenate(
                        [(cb_mat * f["decay"][h0]).astype(BF16), (cb_mat * f["decay"][h0 + 1]).astype(BF16)], axis=1)
                    ydiag.append(_dot(gcat, _stack_pair(xdt[:, 64 * h0:64 * h0 + 128], m_l, m_r)))
                yoff.append(_dot(cg, state[:, 256 * g:256 * (g + 1)].astype(BF16)))
                snew.append(_dot(bg, xw[:, 256 * g:256 * (g + 1)].astype(BF16), _TN))
            y = jnp.concatenate(ydiag, axis=1) + f["e"] * jnp.concatenate(yoff, axis=1) + dsk_ref[...] * xs
            state_ref[s] = state * f["cd"] + jnp.concatenate(snew, axis=1)
            y_ref[s] = y
            zv = z_ref[s].astype(F32)
            yg = y * (zv * jax.nn.sigmoid(zv))
            outs = []
            for g in range(2):
                ygg = yg[:, 256 * g:256 * (g + 1)]
                outs.append(ygg * lax.rsqrt(jnp.mean(ygg * ygg, axis=-1, keepdims=True) + EPS))
            o_ref[s] = (jnp.concatenate(outs, axis=1) * nw_ref[...]).astype(BF16)

    consts = [cw, cb, dtb, alog, dskip_exp, nw, expand, tril]
    deps = [] if dep is None else [dep]
    n_in = 4 + len(consts)

    def body_skipping_dep(*refs):
        body(*refs[:n_in], *refs[n_in + len(deps):])

    sd = lambda n, dt: jax.ShapeDtypeStruct((nb, seq, n), dt)
    o, y, states, pre = pl.pallas_call(
        body_skipping_dep, name="ssd_fwd", grid=(nc,),
        out_shape=(sd(SSM_WIDTH, BF16), sd(SSM_WIDTH, F32), jax.ShapeDtypeStruct((nb, nc, N_STATE, SSM_WIDTH), F32),
                   sd(CONV_CH, F32)),
        in_specs=[row(SSM_WIDTH), row(CONV_CH), tail, row(CHUNK)] + [_full(a.shape) for a in consts]
        + [pl.BlockSpec(memory_space=pl.ANY)] * len(deps),
        out_specs=(row(SSM_WIDTH), row(SSM_WIDTH), states_spec, row(CONV_CH)),
        scratch_shapes=[pltpu.VMEM((nb, N_STATE, SSM_WIDTH), F32)],
        compiler_params=_params("arbitrary"))(fold(z), fold(xbc), fold(xbc), fold(dtr), *consts, *deps)
    return unfold(o), unfold(y), states, unfold(pre)


def _out_proj(mix_a, mix_b, w_out, x, g2, g3, tm, dep=None):
    t_tok = x.shape[0]
    deps = [] if dep is None else [dep]

    def body(a_ref, b_ref, w_ref, x_ref, g2_ref, g3_ref, *rest):
        o_ref, x2_ref, h3_ref = rest[-3:]
        o = _dot(a_ref[...], w_ref[0:GM_WIDTH, :]) + _dot(b_ref[...], w_ref[GM_WIDTH:, :])
        o_ref[...] = o
        r2 = lax.rsqrt(jnp.mean(o * o, axis=-1, keepdims=True) + EPS)
        x2 = x_ref[...] + o * r2 * g2_ref[...]
        x2_ref[...] = x2
        r3 = lax.rsqrt(jnp.mean(x2 * x2, axis=-1, keepdims=True) + EPS)
        h3_ref[...] = (x2 * r3 * g3_ref[...]).astype(BF16)

    row = lambda n: pl.BlockSpec((tm, n), lambda i: (i, 0))
    sd = lambda dt: jax.ShapeDtypeStruct((t_tok, D_MODEL), dt)
    return pl.pallas_call(
        body, name="out_proj", grid=(t_tok // tm,), out_shape=(sd(F32), sd(F32), sd(BF16)),
        in_specs=[row(GM_WIDTH), row(SSM_WIDTH), _full((D_MODEL, D_MODEL)), row(D_MODEL), _full((1, D_MODEL)),
                  _full((1, D_MODEL))] + [pl.BlockSpec(memory_space=pl.ANY)] * len(deps),
        out_specs=(row(D_MODEL),) * 3, compiler_params=_params("parallel"))(mix_a, mix_b, w_out, x, g2, g3, *deps)


def _mlp_fwd(h3, w_up, w_down, x2, target, g4, tm, tf):
    t_tok = x2.shape[0]

    def up_body(h_ref, wu_ref, ra_ref):
        ra_ref[...] = jnp.maximum(_dot(h_ref[...], wu_ref[...]), 0.0).astype(BF16)

    tu = min(2 * tm, t_tok)
    ra = pl.pallas_call(
        up_body, name="mlp_up", grid=(D_FF // tf, t_tok // tu), out_shape=jax.ShapeDtypeStruct((t_tok, D_FF), BF16),
        in_specs=[pl.BlockSpec((tu, D_MODEL), lambda j, i: (i, 0)), pl.BlockSpec((D_MODEL, tf), lambda j, i: (0, j))],
        out_specs=pl.BlockSpec((tu, tf), lambda j, i: (i, j)), compiler_params=_params("parallel", "parallel"))(h3, w_up)

    def down_body(ra_ref, wd_ref, x2_ref, t_ref, g4_ref, dd_ref, dy_ref, dg4_ref, loss_ref):
        i = pl.program_id(0)
        rav = ra_ref[...]
        dvec = _dot(rav * rav, wd_ref[...])
        r4 = lax.rsqrt(jnp.mean(dvec * dvec, axis=-1, keepdims=True) + EPS)
        dn = dvec * r4
        g4 = g4_ref[...]
        err = x2_ref[...] + dn * g4 - t_ref[...]
        dy = err * (1.0 / D_MODEL)
        dy_ref[...] = dy
        dg = dy * g4
        dd_ref[...] = (r4 * (dg - dn * jnp.mean(dg * dn, axis=-1, keepdims=True))).astype(BF16)
        _acc_rows(dg4_ref, _rsum(dy * dn), i == 0)
        tile_loss = 0.5 * jnp.sum(jnp.sum(err * err, axis=-1, keepdims=True), axis=0, keepdims=True) / D_MODEL
        _acc_rows(loss_ref, jnp.broadcast_to(tile_loss, (1, 128)), i == 0)

    row = pl.BlockSpec((tm, D_MODEL), lambda i: (i, 0))
    dd, dy, dg4, loss = pl.pallas_call(
        down_body, name="mlp_down", grid=(t_tok // tm,),
        out_shape=(jax.ShapeDtypeStruct((t_tok, D_MODEL), BF16), jax.ShapeDtypeStruct((t_tok, D_MODEL), F32),
                   jax.ShapeDtypeStruct((1, D_MODEL), F32), jax.ShapeDtypeStruct((1, 128), F32)),
        in_specs=[pl.BlockSpec((tm, D_FF), lambda i: (i, 0)), _full((D_FF, D_MODEL)), row, row, _full((1, D_MODEL))],
        out_specs=(row, row, _full((1, D_MODEL)), _full((1, 128))),
        compiler_params=_params("arbitrary"))(ra, w_down, x2, target, g4)
    return ra, dd, dy, dg4, loss


def _mlp_bwd(dd, w_down, ra, w_up, x2, dy, o, g3, g2, tm, tf):
    t_tok = x2.shape[0]

    def hidden_body(dd_ref, wd_ref, ra_ref, da_ref):
        df = _dot(dd_ref[...], wd_ref[...], _NT)
        da_ref[...] = (df * (2.0 * ra_ref[...].astype(F32))).astype(BF16)

    tu = min(2 * tm, t_tok)
    da = pl.pallas_call(
        hidden_body, name="mlp_bwd_hidden", grid=(D_FF // tf, t_tok // tu),
        out_shape=jax.ShapeDtypeStruct((t_tok, D_FF), BF16),
        in_specs=[pl.BlockSpec((tu, D_MODEL), lambda j, i: (i, 0)), pl.BlockSpec((tf, D_MODEL), lambda j, i: (j, 0)),
                  pl.BlockSpec((tu, tf), lambda j, i: (i, j))],
        out_specs=pl.BlockSpec((tu, tf), lambda j, i: (i, j)),
        compiler_params=_params("parallel", "parallel"))(dd, w_down, ra)

    def in_body(da_ref, wu_ref, x2_ref, dy_ref, o_ref, g3_ref, g2_ref, dx2_ref, do_ref, dg3_ref, dg2_ref):
        i = pl.program_id(0)
        dh3 = _dot(da_ref[...], wu_ref[...], _NT)
        dn3, dg3 = _rms_bwd(x2_ref[...], g3_ref[...], dh3)
        dx2 = dy_ref[...] + dn3
        dx2_ref[...] = dx2
        do, dg2 = _rms_bwd(o_ref[...], g2_ref[...], dx2)
        do_ref[...] = do.astype(BF16)
        _acc_rows(dg3_ref, dg3, i == 0)
        _acc_rows(dg2_ref, dg2, i == 0)

    row = pl.BlockSpec((tm, D_MODEL), lambda i: (i, 0))
    vec = _full((1, D_MODEL))
    sd = lambda dt: jax.ShapeDtypeStruct((t_tok, D_MODEL), dt)
    dx2, do, dg3, dg2 = pl.pallas_call(
        in_body, name="mlp_bwd_in", grid=(t_tok // tm,),
        out_shape=(sd(F32), sd(BF16), jax.ShapeDtypeStruct((1, D_MODEL), F32), jax.ShapeDtypeStruct((1, D_MODEL), F32)),
        in_specs=[pl.BlockSpec((tm, D_FF), lambda i: (i, 0)), _full((D_MODEL, D_FF)), row, row, row, vec, vec],
        out_specs=(row, row, vec, vec), compiler_params=_params("arbitrary"))(da, w_up, x2, dy, o, g3, g2)
    return da, dx2, do, dg3, dg2


def _wgrad(a, b, out_blocks, bm, bn, bk, square_a, name, dep=None):
    t_tok, m = a.shape
    n = b.shape[1]
    nk = t_tok // bk

    def body(a_ref, b_ref, *rest):
        o_ref, acc_ref = rest[-2:]
        k = pl.program_id(2)
        av = a_ref[...]
        if square_a:
            av = av * av
        part = _dot(av, b_ref[...], _TN)

        def emit(res):
            if out_blocks is None:
                o_ref[...] = res.astype(BF16)
            else:
                o_ref[0] = res.astype(BF16)

        if nk == 1:
            emit(part)
            return

        @pl.when(k == 0)
        def _():
            acc_ref[...] = part

        @pl.when(k > 0)
        def _():
            acc_ref[...] += part

        @pl.when(k == nk - 1)
        def _():
            emit(acc_ref[...])

    if out_blocks is None:
        out_shape = jax.ShapeDtypeStruct((m, n), BF16)
        out_spec = pl.BlockSpec((bm, bn), lambda i, j, k: (i, j))
    else:
        assert n // out_blocks == bn
        out_shape = jax.ShapeDtypeStruct((out_blocks, m, bn), BF16)
        out_spec = pl.BlockSpec((1, bm, bn), lambda i, j, k: (j, i, 0))
    deps = [] if dep is None else [dep]
    return pl.pallas_call(
        body, name=name, grid=(m // bm, n // bn, nk), out_shape=out_shape,
        in_specs=[pl.BlockSpec((bk, bm), lambda i, j, k: (k, i)), pl.BlockSpec((bk, bn), lambda i, j, k: (k, j))]
        + [pl.BlockSpec(memory_space=pl.ANY)] * len(deps),
        out_specs=out_spec, scratch_shapes=[pltpu.VMEM((bm, bn) if nk > 1 else (8, 128), F32)],
        compiler_params=_params("parallel", "parallel", "arbitrary"))(a, b, *deps)


def _wgrad_in_chunked(h1, pieces, bn, bk, dep=None):
    t_tok = h1.shape[0]
    nk = t_tok // bk
    widths = [b - a for a, b in _IN_SPLITS]

    def body(h_ref, *rest):
        piece_refs = rest[:len(widths)]
        o_ref, acc_ref = rest[-2:]
        k = pl.program_id(1)
        hv = h_ref[...]
        for (a, b), r in zip(_IN_SPLITS, piece_refs):
            part = _dot(r[...], hv, _TN)

            @pl.when(k == 0)
            def _():
                acc_ref[a:b, :] = part

            @pl.when(k > 0)
            def _():
                acc_ref[a:b, :] += part

        @pl.when(k == nk - 1)
        def _():
            o_ref[...] = acc_ref[:IN_COLS].astype(BF16)

    deps = [] if dep is None else [dep]
    return pl.pallas_call(
        body, name="wgrad_in", grid=(D_MODEL // bn, nk), out_shape=jax.ShapeDtypeStruct((IN_COLS, D_MODEL), BF16),
        in_specs=[pl.BlockSpec((bk, bn), lambda j, k: (k, j))] + [pl.BlockSpec((bk, n), lambda j, k: (k, 0)) for n in widths]
        + [pl.BlockSpec(memory_space=pl.ANY)] * len(deps),
        out_specs=pl.BlockSpec((IN_COLS, bn), lambda j, k: (0, j)), scratch_shapes=[pltpu.VMEM((IN_PAD, bn), F32)],
        compiler_params=_params("parallel", "arbitrary"))(h1, *pieces, *deps)


def _wgrad_pieces(h1, pieces, bn, name, dep=None):
    t_tok = h1.shape[0]
    widths = [p.shape[1] for p in pieces]
    starts = [sum(widths[:i]) for i in range(len(widths))]

    def body(h_ref, *rest):
        piece_refs = rest[:len(widths)]
        o_ref = rest[-1]
        hv = h_ref[...]
        for a, n, r in zip(starts, widths, piece_refs):
            o_ref[a:a + n, :] = _dot(r[...], hv, _TN).astype(BF16)

    deps = [] if dep is None else [dep]
    return pl.pallas_call(
        body, name=name, grid=(D_MODEL // bn,), out_shape=jax.ShapeDtypeStruct((sum(widths), D_MODEL), BF16),
        in_specs=[pl.BlockSpec((t_tok, bn), lambda j: (0, j))] + [pl.BlockSpec((t_tok, n), lambda j: (0, 0)) for n in widths]
        + [pl.BlockSpec(memory_space=pl.ANY)] * len(deps),
        out_specs=pl.BlockSpec((sum(widths), bn), lambda j: (0, j)),
        compiler_params=_params("parallel"))(h1, *pieces, *deps)


def _dmix(do, w_out, tm, dep=None):
    t_tok = do.shape[0]

    def body(d_ref, w_ref, *rest):
        rest[-1][...] = _dot(d_ref[...], w_ref[...], _NT).astype(BF16)

    row = pl.BlockSpec((tm, D_MODEL), lambda i: (i, 0))
    deps = [] if dep is None else [dep]
    return pl.pallas_call(
        body, name="dmix", grid=(t_tok // tm,), out_shape=jax.ShapeDtypeStruct((t_tok, D_MODEL), BF16),
        in_specs=[row, _full((D_MODEL, D_MODEL))] + [pl.BlockSpec(memory_space=pl.ANY)] * len(deps), out_specs=row,
        compiler_params=_params("parallel"))(do, w_out, *deps)


def _gmlp_bwd(dmix, u, v, lnw, lnb, wcat, wtcat, bias, avg, expand_t):
    t_tok = u.shape[0]
    tm = min(_GMLP_ROWS, t_tok)

    def body(dm_ref, u_ref, v_ref, lnw_ref, lnb_ref, wcat_ref, wtcat_ref, bias_ref, avg_ref, expt_ref, du_ref, dv_ref,
             dw_ref, db_ref, dlnw_ref, dlnb_ref):
        i = pl.program_id(0)
        m_l, m_r = _lane_masks()
        avg = avg_ref[...]
        lnw = lnw_ref[...]
        ug, dug, dvg, rstd, vhat, vn, mixed = _gmlp_common(
            u_ref[...].astype(F32), v_ref[...].astype(F32), lnw, lnb_ref[...], avg, wcat_ref, bias_ref[...], m_l, m_r)
        dya = dm_ref[...].astype(F32)
        du_ref[...] = (dya * mixed * dug).astype(BF16)
        dmixed = dya * ug
        dvn_rows, dws, dbt = [], [None] * N_HEADS, None
        for r in range(tm // CHUNK):
            dvn_cols = []
            for j in range(N_HEADS // 2):
                dmp = dmixed[CHUNK * r:CHUNK * (r + 1), 128 * j:128 * (j + 1)]
                dvn_cols.append(_dot(wtcat_ref[j], _stack_pair(dmp, m_l, m_r)))
                vnp = vn[CHUNK * r:CHUNK * (r + 1), 128 * j:128 * (j + 1)].astype(BF16)
                for i_h, mask in enumerate((m_l, m_r)):
                    part = _dot((dmp * mask).astype(BF16), vnp, _NT)
                    dws[2 * j + i_h] = part if r == 0 else dws[2 * j + i_h] + part
            dvn_rows.append(jnp.concatenate(dvn_cols, axis=1))
            part = _split_dot(dmixed[CHUNK * r:CHUNK * (r + 1), :], expt_ref[...], 2)
            dbt = part if r == 0 else dbt + part
        dvn = jnp.concatenate(dvn_rows, axis=0)
        dvh = dvn * lnw
        dvgel = rstd * (dvh - _head_mean(dvh, avg) - vhat * _head_mean(dvh * vhat, avg))
        dv_ref[...] = (dvgel * dvg).astype(BF16)
        first = i == 0

        @pl.when(first)
        def _():
            for h in range(N_HEADS):
                dw_ref[h] = dws[h]
            db_ref[...] = dbt

        @pl.when(jnp.logical_not(first))
        def _():
            for h in range(N_HEADS):
                dw_ref[h] += dws[h]
            db_ref[...] += dbt

        _acc_rows(dlnw_ref, _rsum(dvn * vhat), first)
        _acc_rows(dlnb_ref, _rsum(dvn), first)

    row = pl.BlockSpec((tm, GM_WIDTH), lambda i: (i, 0))
    consts = [lnw, lnb, wcat, wtcat, bias, avg, expand_t]
    return pl.pallas_call(
        body, name="gmlp_bwd", grid=(t_tok // tm,),
        out_shape=(jax.ShapeDtypeStruct((t_tok, GM_WIDTH), BF16), jax.ShapeDtypeStruct((t_tok, GM_WIDTH), BF16),
                   jax.ShapeDtypeStruct((N_HEADS, CHUNK, CHUNK), F32), jax.ShapeDtypeStruct((CHUNK, CHUNK), F32),
                   jax.ShapeDtypeStruct((1, GM_WIDTH), F32), jax.ShapeDtypeStruct((1, GM_WIDTH), F32)),
        in_specs=[row, row, row] + [_full(a.shape) for a in consts],
        out_specs=(row, row, _full((N_HEADS, CHUNK, CHUNK)), _full((CHUNK, CHUNK)), _full((1, GM_WIDTH)),
                   _full((1, GM_WIDTH))),
        compiler_params=_params("arbitrary"))(dmix, u, v, *consts)


def _ssd_bwd(dmix, z, xbc, pre, dtr, y, states, cw, cb, dtb, alog, dskip_exp, nw, expand, expand_t, tril, triu, seq,
             dep=None):
    t_tok = z.shape[0]
    nb, nc, row, _, states_spec, fold, unfold = _ssd_specs(t_tok, seq, True)
    q = CHUNK

    def one_sequence(s, dm_ref, z_ref, xbc_ref, pre_ref, dtr_ref, y_ref, st_ref, cw_ref, dtb_ref, alog_ref, dsk_ref,
                     nw_ref, exp_ref, expt_ref, tril_ref, triu_ref, dz_ref, dxbc_ref, ddt_ref, dhead_ref, dstate_ref):
        m_l, m_r = _lane_masks()
        expt = expt_ref[...]
        f = _ssd_common(pre_ref[s], dtr_ref[s], dtb_ref[...], alog_ref[...], exp_ref[...], tril_ref[...])
        act, pre, sg = f["act"], f["pre"], f["sg"]
        xs = act[:, :SSM_WIDTH]
        xdt = xs * f["dt_exp"]
        xw = xdt * f["w_end"]
        state = st_ref[s, 0]
        dstate = dstate_ref[s]
        zv, yv, dout, nw = z_ref[s].astype(F32), y_ref[s], dm_ref[s].astype(F32), nw_ref[...]
        sz = jax.nn.sigmoid(zv)
        sl = zv * sz
        yg = yv * sl
        tv = dout * nw
        dyg_parts, ygh_parts = [], []
        for g in range(2):
            ygg = yg[:, 256 * g:256 * (g + 1)]
            rr = lax.rsqrt(jnp.mean(ygg * ygg, axis=-1, keepdims=True) + EPS)
            ygh = ygg * rr
            tg = tv[:, 256 * g:256 * (g + 1)]
            dyg_parts.append(rr * (tg - ygh * jnp.mean(tg * ygh, axis=-1, keepdims=True)))
            ygh_parts.append(ygh)
        dyg = jnp.concatenate(dyg_parts, axis=1)
        dnw = _rsum(dout * jnp.concatenate(ygh_parts, axis=1))
        dy = dyg * sl
        dz_ref[s] = (dyg * yv * (sz * (1.0 + zv * (1.0 - sz)))).astype(BF16)
        ddsk = _rsum(dy * xs)
        dye = dy * f["e"]
        lane = lax.broadcasted_iota(jnp.int32, (q, q), 1)
        sub = lax.broadcasted_iota(jnp.int32, (q, q), 0)
        rs_mat = jnp.zeros((q, q), F32)
        cs_mat = jnp.zeros((q, q), F32)
        dxdt_cols, yoff, dst_in, dxw, d_b, d_c = [], [], [], [], [], []
        for g in range(2):
            bg = act[:, 512 + 128 * g:640 + 128 * g].astype(BF16)
            cg = act[:, 768 + 128 * g:896 + 128 * g].astype(BF16)
            cb_mat = _dot(cg, bg, _NT)
            stg = state[:, 256 * g:256 * (g + 1)].astype(BF16)
            dyeg = dye[:, 256 * g:256 * (g + 1)].astype(BF16)
            yoff.append(_dot(cg, stg))
            dcg = _dot(dyeg, stg, _NT)
            dst_in.append(_dot(cg, dyeg, _TN))
            dcb = jnp.zeros((q, q), F32)
            for pr in range(2):
                h0 = 4 * g + 2 * pr
                gf = [cb_mat * f["decay"][h0], cb_mat * f["decay"][h0 + 1]]
                gcat = jnp.concatenate([gf[0].astype(BF16), gf[1].astype(BF16)], axis=1)
                xst = _stack_pair(xdt[:, 64 * h0:64 * h0 + 128], m_l, m_r)
                dyp = dy[:, 64 * h0:64 * h0 + 128].astype(BF16)
                dgcat = _dot(dyp, xst, _NT)
                dxst = _dot(gcat, dyp, _TN)
                dxdt_cols.append(dxst[:q] * m_l + dxst[q:] * m_r)
                for i in range(2):
                    h = h0 + i
                    dg = dgcat[:, q * i:q * (i + 1)]
                    mm = dg * gf[i]
                    rs_mat = rs_mat + jnp.where(lane == h, jnp.sum(mm, axis=1, keepdims=True), 0.0)
                    cs_mat = cs_mat + jnp.where(sub == h, jnp.sum(mm, axis=0, keepdims=True), 0.0)
                    dcb = dcb + dg * f["decay"][h]
            dcb16 = dcb.astype(BF16)
            dstg = dstate[:, 256 * g:256 * (g + 1)].astype(BF16)
            d_c.append(dcg + _dot(dcb16, bg))
            dxw.append(_dot(bg, dstg))
            d_b.append(_dot(dcb16, cg, _TN) + _dot(xw[:, 256 * g:256 * (g + 1)].astype(BF16), dstg, _NT))
        dxw = jnp.concatenate(dxw, axis=1)
        dxdt = jnp.concatenate(dxdt_cols, axis=1) + dxw * f["w_end"]
        qv = dxw * xw
        end_row = _rsum(qv) + _rsum(dstate * state) * f["cd"]
        x2 = dye * jnp.concatenate(yoff, axis=1) - qv
        row_i = lax.broadcasted_iota(jnp.int32, (q, 1), 0)
        x2 = x2 + jnp.where(row_i == q - 1, end_row, 0.0)
        da_cs = _split_dot(x2, expt, 2) + rs_mat - cs_mat.T
        ddt = _split_dot(dxdt * xs, expt, 2)
        dxs = dsk_ref[...] * dy + dxdt * f["dt_exp"]
        dda = _split_dot_left(triu_ref[...], da_cs, 3)
        ddt = ddt + dda * f["a_row"]
        dalog = _rsum(dda * f["dt"]) * f["a_row"]
        draw = ddt * jax.nn.sigmoid(f["dtp"])
        ddt_ref[s] = draw.astype(BF16)
        dact = jnp.concatenate([dxs] + d_b + d_c, axis=1)
        dpre = dact * (sg * (1.0 + pre * (1.0 - sg)))
        dhead = dhead_ref[s]
        xv = xbc_ref[s]
        shifted = [_shift_rows(dpre, dhead, 3 - k, False) for k in range(3)] + [dpre]
        dxbc = cw_ref[3:4, :] * dpre
        for k in range(3):
            dxbc = dxbc + cw_ref[k:k + 1, :] * shifted[k]
        dxbc_ref[s] = dxbc.astype(BF16)
        dhead_ref[s] = dpre[0:8, :]
        dstate_ref[s] = dstate * f["cd"] + jnp.concatenate(dst_in, axis=1)
        row8 = lax.broadcasted_iota(jnp.int32, (8, 1), 0)
        dcw = jnp.zeros((8, CONV_CH), F32)
        for k in range(4):
            dcw = dcw + jnp.where(row8 == k, _rsum(shifted[k] * xv), 0.0)
        return dcw, _rsum(dpre), _rsum(draw), dalog, _split_dot(ddsk, expt, 3), dnw

    def body(dm_ref, z_ref, xbc_ref, pre_ref, dtr_ref, y_ref, st_ref, cw_ref, cb_ref, dtb_ref, alog_ref, dsk_ref,
             nw_ref, exp_ref, expt_ref, tril_ref, triu_ref, dz_ref, dxbc_ref, ddt_ref, dcw_ref, dcb_ref, ddtb_ref,
             dalog_ref, dd_ref, dnw_ref, dhead_ref, dstate_ref):
        c = pl.program_id(0)
        first = c == 0

        @pl.when(first)
        def _():
            dstate_ref[...] = jnp.zeros_like(dstate_ref)
            dhead_ref[...] = jnp.zeros_like(dhead_ref)

        total = None
        for s in range(nb):
            parts = one_sequence(s, dm_ref, z_ref, xbc_ref, pre_ref, dtr_ref, y_ref, st_ref, cw_ref, dtb_ref, alog_ref,
                                 dsk_ref, nw_ref, exp_ref, expt_ref, tril_ref, triu_ref, dz_ref, dxbc_ref, ddt_ref,
                                 dhead_ref, dstate_ref)
            total = parts if total is None else tuple(a + b for a, b in zip(total, parts))
        dcw = total[0]

        @pl.when(first)
        def _():
            dcw_ref[...] = dcw

        @pl.when(jnp.logical_not(first))
        def _():
            dcw_ref[...] += dcw

        for ref, part in zip((dcb_ref, ddtb_ref, dalog_ref, dd_ref, dnw_ref), total[1:]):
            _acc_rows(ref, part, first)

    consts = [cw, cb, dtb, alog, dskip_exp, nw, expand, expand_t, tril, triu]
    deps = [] if dep is None else [dep]
    n_in = 7 + len(consts)

    def body_skipping_dep(*refs):
        body(*refs[:n_in], *refs[n_in + len(deps):])

    acc = lambda n: jax.ShapeDtypeStruct((1, n), F32)
    sd = lambda n: jax.ShapeDtypeStruct((nb, seq, n), BF16)
    dz, dxbc, ddt, *small_grads = pl.pallas_call(
        body_skipping_dep, name="ssd_bwd", grid=(nc,),
        out_shape=(sd(SSM_WIDTH), sd(CONV_CH), sd(CHUNK), jax.ShapeDtypeStruct((8, CONV_CH), F32), acc(CONV_CH),
                   acc(CHUNK), acc(CHUNK), acc(CHUNK), acc(SSM_WIDTH)),
        in_specs=[row(SSM_WIDTH, col=1), row(SSM_WIDTH), row(CONV_CH), row(CONV_CH), row(CHUNK), row(SSM_WIDTH),
                  states_spec]
        + [_full(a.shape) for a in consts] + [pl.BlockSpec(memory_space=pl.ANY)] * len(deps),
        out_specs=(row(SSM_WIDTH), row(CONV_CH), row(CHUNK), _full((8, CONV_CH)), _full((1, CONV_CH)),
                   _full((1, CHUNK)), _full((1, CHUNK)), _full((1, CHUNK)), _full((1, SSM_WIDTH))),
        scratch_shapes=[pltpu.VMEM((nb, 8, CONV_CH), F32), pltpu.VMEM((nb, N_STATE, SSM_WIDTH), F32)],
        compiler_params=_params("arbitrary"))(
            fold(dmix), fold(z), fold(xbc), fold(pre), fold(dtr), fold(y), states, *consts, *deps)
    return (unfold(dz), unfold(dxbc), unfold(ddt), *small_grads)


def _in_bwd(du, dv, dz, dxbc, ddt, w_in, x, dx2, g1, tm, me, riders=(), dep=None):
    t_tok = x.shape[0]
    steps = t_tok // tm

    n_in = [5 + ("mask" in rd) for rd in riders]
    first_in = [sum(n_in[:r]) for r in range(len(riders))]

    def body(me_ref, du_ref, dv_ref, dz_ref, dxbc_ref, ddt_ref, w_ref, x_ref, dx2_ref, g_ref, *rest):
        outs = rest[len(rest) - 2 - 4 * len(riders):]
        gx_ref, dg_ref = outs[:2]
        i = pl.program_id(0)
        dh = None
        for (a, b), ref in zip(_IN_SPLITS, (du_ref, dv_ref, dz_ref, dxbc_ref, ddt_ref)):
            part = _dot(ref[...], w_ref[a:b, :])
            dh = part if dh is None else dh + part
        dn, dg = _rms_bwd(x_ref[...], g_ref[...], dh)
        gx_ref[...] = dx2_ref[...] + dn
        _acc_rows(dg_ref, dg, i == 0)
        for r in range(len(riders)):
            p_ref, own_ref, w_ref_r, m_ref_r, v_ref_r = rest[first_in[r]:first_in[r] + 5]
            g = _sum_parts(me_ref[0], p_ref, own_ref[0])
            if n_in[r] == 6:
                g = g * rest[first_in[r] + 5][...]
            d, mn, vn = _adamw_math(w_ref_r[...], g, m_ref_r[...], v_ref_r[...])
            for o_ref, val in zip(outs[2 + 4 * r:6 + 4 * r], (g, d, mn, vn)):
                o_ref[...] = val

    row = lambda n: pl.BlockSpec((tm, n), lambda i, me_ref: (i, 0))
    whole = lambda shape: pl.BlockSpec(shape, lambda i, me_ref: (0,) * len(shape))
    widths = [b - a for a, b in _IN_SPLITS]
    deps = [] if dep is None else [dep]
    rider_args, rider_specs, rider_out_shapes, rider_out_specs = [], [], [], []
    for rd in riders:
        rows, cols = rd["w"].shape[0] // steps, rd["w"].shape[1]
        blk = pl.BlockSpec((rows, cols), lambda i, me_ref: (i, 0))
        rider_args += [rd["parts"], rd["own"], rd["w"], rd["m"], rd["v"]]
        rider_specs += [pl.BlockSpec((N_DEV, rows, cols), lambda i, me_ref: (0, i, 0)),
                        pl.BlockSpec((1, rows, cols), lambda i, me_ref: (me_ref[0], i, 0)), blk, blk, blk]
        if "mask" in rd:
            rider_args.append(rd["mask"])
            rider_specs.append(whole((rows, cols)))
        rider_out_shapes += [jax.ShapeDtypeStruct(rd["w"].shape, F32)] * 4
        rider_out_specs += [blk] * 4
    outs = pl.pallas_call(
        body, name="in_bwd",
        out_shape=(jax.ShapeDtypeStruct((t_tok, D_MODEL), F32), jax.ShapeDtypeStruct((1, D_MODEL), F32),
                   *rider_out_shapes),
        grid_spec=pltpu.PrefetchScalarGridSpec(
            num_scalar_prefetch=1, grid=(steps,),
            in_specs=[row(n) for n in widths] + [whole((IN_PAD, D_MODEL)), row(D_MODEL), row(D_MODEL),
                                                 whole((1, D_MODEL))] + rider_specs
            + [pl.BlockSpec(memory_space=pl.ANY)] * len(deps),
            out_specs=(row(D_MODEL), whole((1, D_MODEL)), *rider_out_specs)),
        compiler_params=_params("arbitrary"))(me, du, dv, dz, dxbc, ddt, w_in, x, dx2, g1, *rider_args, *deps)
    return outs[0], outs[1], [tuple(outs[2 + 4 * r:6 + 4 * r]) for r in range(len(riders))]


def _pad_lanes(a, n):
    return jnp.pad(a, ((0, 0), (0, n - a.shape[1])))


def _local_step(x, target, seq, small, hooks, first_dep=None):
    t_tok = x.shape[0]
    tm = min(TOKEN_TILE, t_tok)
    avg, expand, expand_t, tril, triu = _const_mats()
    g1, g2, g3, g4 = (small[k].reshape(1, D_MODEL) for k in
                      ("norm_mix_pre", "norm_mix_post", "norm_ffn_pre", "norm_ffn_post"))
    tie = (lambda a: a) if first_dep is None else (lambda a: a + first_dep[0, 0])
    lnw = tie(small["gm_ln_w"]).reshape(1, GM_WIDTH)
    lnb = tie(small["gm_ln_b"]).reshape(1, GM_WIDTH)
    causal = jnp.tril(jnp.ones((CHUNK, CHUNK), F32))
    wm = tie(small["gm_w_s"]) * causal
    pair = lambda w: w.reshape(4, 2, CHUNK, CHUNK).transpose(0, 2, 1, 3).reshape(4, CHUNK, 2 * CHUNK).astype(BF16)
    wcat = pair(wm)
    wtcat = pair(jnp.swapaxes(wm, 1, 2))
    bias = jnp.repeat(tie(small["gm_b_s"]).T, HEAD_DIM, axis=1)
    cb = small["conv_b"].reshape(1, CONV_CH)
    dtb = _pad_lanes(tie(small["dt_bias"]).reshape(1, N_HEADS), CHUNK)
    alog = _pad_lanes(tie(small["a_log"]).reshape(1, N_HEADS), CHUNK)
    dskip_exp = jnp.repeat(tie(small["d_skip"]).reshape(1, N_HEADS), HEAD_DIM, axis=1)
    nw = small["ssm_norm_w"].reshape(1, SSM_WIDTH)

    h1 = _prenorm(x, g1, tm, hooks.get("prenorm_after", first_dep))
    w_in_t, conv_w = hooks["mixer_weights"](h1)
    tall = min(2 * tm, t_tok)
    u, v, z, xbc, dtr = _in_proj(h1, w_in_t, tall)
    mix_a = _gmlp_fwd(u, v, lnw, lnb, wcat, bias, avg)
    dep = hooks["gmlp_done"](mix_a) if "gmlp_done" in hooks else None
    mix_b, y_pre, states, pre = _ssd_fwd(z, xbc, dtr, conv_w, cb, dtb, alog, dskip_exp, nw, expand, tril, seq, dep)
    w_out, dep = hooks["mixers_done"](mix_b)
    o, x2, h3 = _out_proj(mix_a, mix_b, w_out, x, g2, g3, tall, dep)
    w_up, w_down = hooks["mlp_weights"](h3)
    tf = FF_TILE
    ra, dd, dy, dg4, loss = _mlp_fwd(h3, w_up, w_down, x2, target, g4, tm, tf)

    da, dx2, do, dg3, dg2 = _mlp_bwd(dd, w_down, ra, w_up, x2, dy, o, g3, g2, tm, tf)
    g_w_down = _wgrad(ra, dd, None, WGRAD_TILE, D_MODEL, t_tok, True, "wgrad_down")
    g_w_up = _wgrad(h3, da, N_DEV, D_MODEL, D_FF // N_DEV, t_tok, False, "wgrad_up")
    dep = hooks["mlp_grads"](g_w_down, g_w_up)
    dmix = _dmix(do, w_out, tall, dep)
    g_w_out = _wgrad_pieces(do, (mix_a, mix_b), WGRAD_TILE, "wgrad_out", dep)
    du, dv, dws, dbt, dlnw, dlnb = _gmlp_bwd(dmix, u, v, lnw, lnb, wcat, wtcat, bias, avg, expand_t)
    dep = hooks["gmlp_grads"](g_w_out, dws)
    dz, dxbc, ddt, dcw, dcb, ddtb, dalog, ddsk, dnw = _ssd_bwd(
        dmix, z, xbc, pre, dtr, y_pre, states, conv_w, cb, dtb, alog, dskip_exp, nw, expand, expand_t, tril, triu, seq,
        dep)
    g_w_in = _wgrad_in_chunked(h1, (du, dv, dz, dxbc, ddt), WGRAD_TILE, t_tok // 2, dep)
    dep = hooks["in_grads"](g_w_in, dcw[0:4])
    riders = hooks["arrived_updates"](dep) if "arrived_updates" in hooks else []
    me = hooks.get("me", jnp.zeros((1,), jnp.int32))
    grad_x, dg1, updates = _in_bwd(du, dv, dz, dxbc, ddt, w_in_t, x, dx2, g1, tm, me, riders, dep)

    grads = dict(
        updates=updates,
        w_in=g_w_in, w_out=g_w_out, w_up=g_w_up, w_down=g_w_down, conv_w=dcw[0:4],
        norm_mix_pre=dg1, norm_mix_post=dg2, norm_ffn_pre=dg3, norm_ffn_post=dg4, gm_ln_w=dlnw, gm_ln_b=dlnb,
        gm_w_s=dws, gm_b_s=dbt, conv_b=dcb, dt_bias=ddtb, a_log=dalog, d_skip=ddsk, ssm_norm_w=dnw)
    return loss[0, 0], grad_x, grads


_WEIGHTS = ("norm_mix_pre", "w_in", "gm_ln_w", "gm_ln_b", "gm_w_s", "gm_b_s", "conv_w", "conv_b", "dt_bias", "a_log",
            "d_skip", "ssm_norm_w", "w_out", "norm_mix_post", "norm_ffn_pre", "w_up", "w_down", "norm_ffn_post")
_SLAB_ROWS = (("norm_mix_pre", 1024), ("norm_mix_post", 1024), ("norm_ffn_pre", 1024), ("norm_ffn_post", 1024),
              ("conv_b", 1024), ("ssm_norm_w", 512), ("gm_ln_w", 512), ("gm_ln_b", 512), ("dt_bias", 8), ("a_log", 8),
              ("d_skip", 8))
_SLAB_LOSS_ROW = len(_SLAB_ROWS)
_SLAB_BS_ROW = 16
_SMALL_PARAMS = tuple(name for name, _ in _SLAB_ROWS) + ("gm_b_s",)
_LN_PARAMS = ("gm_ln_w", "gm_ln_b")


_SLAB_CONV_ROW = _SLAB_LOSS_ROW + 1


def _pack_slab(g, loss_part):
    rows = [_pad_lanes(g[name], D_MODEL) for name, _ in _SLAB_ROWS]
    rows.append(jnp.broadcast_to(loss_part, (1, D_MODEL)))
    rows.append(g["conv_w"])
    assert sum(r.shape[0] for r in rows) == _SLAB_BS_ROW
    rows.append(_pad_lanes(g["gm_b_s"].T[0:N_HEADS], D_MODEL))
    return jnp.concatenate(rows, axis=0)


def _adamw_slab(parts, me, w, m, v):
    names = _SMALL_PARAMS + ("conv_w",)
    shapes = [w[k].shape for k in names]
    unfold = np.zeros((GM_WIDTH, HEAD_DIM), np.float32)
    for h in range(N_HEADS):
        unfold[h * HEAD_DIM:(h + 1) * HEAD_DIM, :] = np.eye(HEAD_DIM)
    unfold = jnp.asarray(unfold, dtype=BF16)
    n = len(names)
    shard = CONV_CH // N_DEV

    def body(me_ref, p_ref, unfold_ref, *refs):
        w_refs, m_refs, v_refs = refs[:n], refs[n:2 * n], refs[2 * n:3 * n]
        outs = refs[3 * n:]
        g_all = p_ref[0]
        for j in range(1, N_DEV):
            g_all = g_all + p_ref[j]
        lane = lax.broadcasted_iota(jnp.int32, (N_HEADS, GM_WIDTH), 1)
        head = lax.broadcasted_iota(jnp.int32, (N_HEADS, GM_WIDTH), 0)
        own_lanes = jnp.logical_and(lane >= head * HEAD_DIM, lane < (head + 1) * HEAD_DIM)
        mine = pl.ds(pl.multiple_of(me_ref[0] * shard, shard), shard)
        for i, name in enumerate(names):
            if name == "gm_b_s":
                g = g_all[_SLAB_BS_ROW:_SLAB_BS_ROW + N_HEADS, 0:CHUNK]
            elif name == "conv_w":
                g = p_ref[0, _SLAB_CONV_ROW:_SLAB_CONV_ROW + 4, mine]
                for j in range(1, N_DEV):
                    g = g + p_ref[j, _SLAB_CONV_ROW:_SLAB_CONV_ROW + 4, mine]
            else:
                row = [r for r, (k, _) in enumerate(_SLAB_ROWS) if k == name][0]
                g = g_all[row:row + 1, 0:dict(_SLAB_ROWS)[name]]
                if name in _LN_PARAMS:
                    g = _split_dot(jnp.where(own_lanes, g, 0.0), unfold_ref[...], 3)
            d, mn, vn = _adamw_math(w_refs[i][...], g, m_refs[i][...], v_refs[i][...])
            for o_ref, val in zip(outs[4 * i:4 * i + 4], (g, d, mn, vn)):
                o_ref[...] = val
        outs[-1][...] = g_all[_SLAB_LOSS_ROW:_SLAB_LOSS_ROW + 1, 0:128]

    def whole(shape):
        nd = len(shape)
        return pl.BlockSpec(shape, lambda i, me_ref: (0,) * nd)

    ins = [parts, unfold] + [d[k] for d in (w, m, v) for k in names]
    out_shape = tuple(jax.ShapeDtypeStruct(s, F32) for s in shapes for _ in range(4)) + (
        jax.ShapeDtypeStruct((1, 128), F32),)
    outs = pl.pallas_call(
        body, name="adamw_small", out_shape=out_shape,
        grid_spec=pltpu.PrefetchScalarGridSpec(
            num_scalar_prefetch=1, grid=(1,), in_specs=[whole(a.shape) for a in ins],
            out_specs=tuple(whole(s.shape) for s in out_shape)),
        compiler_params=_params("arbitrary"))(me, *ins)
    return {k: tuple(outs[4 * i:4 * i + 4]) for i, k in enumerate(names)}, outs[-1][0, 0]


def kernel(x, norm_mix_pre, w_in, gm_ln_w, gm_ln_b, gm_w_s, gm_b_s, conv_w, conv_b, dt_bias, a_log, d_skip, ssm_norm_w, w_out, norm_mix_post, norm_ffn_pre, w_up, w_down, norm_ffn_post, loss_target, m_norm_mix_pre, m_w_in, m_gm_ln_w, m_gm_ln_b, m_gm_w_s, m_gm_b_s, m_conv_w, m_conv_b, m_dt_bias, m_a_log, m_d_skip, m_ssm_norm_w, m_w_out, m_norm_mix_post, m_norm_ffn_pre, m_w_up, m_w_down, m_norm_ffn_post, v_norm_mix_pre, v_w_in, v_gm_ln_w, v_gm_ln_b, v_gm_w_s, v_gm_b_s, v_conv_w, v_conv_b, v_dt_bias, v_a_log, v_d_skip, v_ssm_norm_w, v_w_out, v_norm_mix_post, v_norm_ffn_pre, v_w_up, v_w_down, v_norm_ffn_post):
    w = dict(norm_mix_pre=norm_mix_pre, w_in=w_in, gm_ln_w=gm_ln_w, gm_ln_b=gm_ln_b, gm_w_s=gm_w_s, gm_b_s=gm_b_s, conv_w=conv_w, conv_b=conv_b, dt_bias=dt_bias, a_log=a_log, d_skip=d_skip, ssm_norm_w=ssm_norm_w, w_out=w_out, norm_mix_post=norm_mix_post, norm_ffn_pre=norm_ffn_pre, w_up=w_up, w_down=w_down, norm_ffn_post=norm_ffn_post)
    m = dict(norm_mix_pre=m_norm_mix_pre, w_in=m_w_in, gm_ln_w=m_gm_ln_w, gm_ln_b=m_gm_ln_b, gm_w_s=m_gm_w_s, gm_b_s=m_gm_b_s, conv_w=m_conv_w, conv_b=m_conv_b, dt_bias=m_dt_bias, a_log=m_a_log, d_skip=m_d_skip, ssm_norm_w=m_ssm_norm_w, w_out=m_w_out, norm_mix_post=m_norm_mix_post, norm_ffn_pre=m_norm_ffn_pre, w_up=m_w_up, w_down=m_w_down, norm_ffn_post=m_norm_ffn_post)
    v = dict(norm_mix_pre=v_norm_mix_pre, w_in=v_w_in, gm_ln_w=v_gm_ln_w, gm_ln_b=v_gm_ln_b, gm_w_s=v_gm_w_s, gm_b_s=v_gm_b_s, conv_w=v_conv_w, conv_b=v_conv_b, dt_bias=v_dt_bias, a_log=v_a_log, d_skip=v_d_skip, ssm_norm_w=v_ssm_norm_w, w_out=v_w_out, norm_mix_post=v_norm_mix_post, norm_ffn_pre=v_norm_ffn_pre, w_up=v_w_up, w_down=v_w_down, norm_ffn_post=v_norm_ffn_post)
    n_batch, seq, _ = x.shape
    shard_in = IN_COLS // N_DEV

    me = (4 * lax.axis_index("x") + 2 * lax.axis_index("y") + lax.axis_index("c")).astype(jnp.int32).reshape(1)

    def in_slot(own):
        return lax.dynamic_update_slice(lax.empty((N_DEV,) + own.shape, own.dtype), own[None],
                                        (me[0],) + (0,) * own.ndim)

    lying = lambda t: jnp.transpose(t, (2, 0, 1))
    first = [_cast_to_slot(lying(w_in), me, shard_in, "cast_w_in"), in_slot(conv_w[0])]
    ici_1, tok_ici_1 = _exchange_start(first, [True] * 2, _SAME_CORE_PEERS, "gather_mix_ici_start")
    cast_out = _cast_to_slot(w_out[0], me, 128, "cast_w_out", dep=tok_ici_1)
    cast_up = _cast_to_slot(w_up[0], me, 1024, "cast_w_up", cols=True, dep=cast_out)
    second = [cast_out, cast_up, _cast_to_slot(w_down[0], me, 512, "cast_w_down", dep=cast_up)]
    gathering = {}

    def mixer_weights(after):
        bufs = [buf for buf, _ in _exchange_wait(ici_1, after, "gather_mix_ici_wait")]
        d2d_1, tok_d2d_1 = _exchange_start(bufs, [True] * 2, _SIBLING_FORWARD, "gather_mix_d2d_start")
        gathering["late_ici"], tok_ici_2 = _exchange_start(
            second, [True] * 3, _SAME_CORE_PEERS, "gather_late_ici_start", dep=tok_d2d_1)
        (_, ag_in), (_, ag_conv) = _exchange_wait(d2d_1, tok_ici_2, "gather_mix_d2d_wait")
        w_in_t = jnp.pad(ag_in.reshape(IN_COLS, D_MODEL), ((0, IN_PAD - IN_COLS), (0, 0)))
        return w_in_t, ag_conv.transpose(1, 0, 2).reshape(4, CONV_CH)

    def gmlp_done(after):
        ((buf, _),) = _exchange_wait(gathering["late_ici"], after, "gather_out_ici_wait", only=(0,))
        gathering["out"], tok = _exchange_start([buf], [True], _SIBLING_FORWARD, "gather_out_d2d_start")
        return tok

    def mixers_done(after):
        bufs = [buf for buf, _ in _exchange_wait(gathering["late_ici"], after, "gather_mlp_ici_wait", only=(1, 2))]
        gathering["mlp"], tok = _exchange_start(bufs, [True] * 2, _SIBLING_FORWARD, "gather_mlp_d2d_start")
        ((_, ag_out),) = _exchange_wait(gathering["out"], tok, "gather_out_d2d_wait")
        return ag_out.reshape(D_MODEL, D_MODEL), tok

    def mlp_weights(after):
        (_, ag_up), (_, ag_down) = _exchange_wait(gathering["mlp"], after, "gather_mlp_d2d_wait")
        return ag_up, ag_down.reshape(D_FF, D_MODEL)

    sent = {}

    def mlp_grads(g_w_down, g_w_up):
        sent["mlp"], tok = _exchange_start(
            [g_w_down.reshape(N_DEV, D_FF // N_DEV, D_MODEL), g_w_up], [False, False], _ALL_PEERS, "grads_mlp_start")
        return tok

    def gmlp_grads(g_w_out, g_w_s):
        sent["gmlp"], tok = _exchange_start(
            [g_w_out.reshape(N_DEV, D_MODEL // N_DEV, D_MODEL), in_slot(g_w_s.astype(BF16))], [False, True], _ALL_PEERS,
            "grads_gmlp_start")
        return tok

    def in_grads(g_w_in_t, g_conv_w):
        g_in_blk = g_w_in_t.reshape(N_DEV, shard_in, D_MODEL)
        sent["in"], tok = _exchange_start([g_in_blk], [False], _ALL_PEERS, "grads_in_start")
        return tok

    def arrived_updates(after):
        (own_down, p_down), (own_up, p_up) = _exchange_wait(sent["mlp"], after, "grads_mlp_wait")
        (own_out, p_out), (_, p_ws) = _exchange_wait(sent["gmlp"], own_up, "grads_gmlp_wait")
        rows = lambda t: t.reshape(t.shape[:-3] + (N_HEADS * CHUNK, CHUNK))
        return [dict(parts=p_up, own=own_up, w=w_up[0], m=m_w_up[0], v=v_w_up[0]),
                dict(parts=p_down, own=own_down, w=w_down[0], m=m_w_down[0], v=v_w_down[0]),
                dict(parts=p_out, own=own_out, w=w_out[0], m=m_w_out[0], v=v_w_out[0]),
                dict(parts=rows(p_ws), own=rows(p_ws), w=rows(gm_w_s[0]), m=rows(m_gm_w_s[0]), v=rows(v_gm_w_s[0]),
                     mask=jnp.tril(jnp.ones((CHUNK, CHUNK), F32)))]

    small = {k: w[k][0] for k in _SMALL_PARAMS + ("gm_w_s",)}
    loss_part, grad_x, g = _local_step(
        x.reshape(n_batch * seq, D_MODEL), loss_target.reshape(n_batch * seq, D_MODEL), seq, small,
        dict(mixer_weights=mixer_weights, gmlp_done=gmlp_done, mixers_done=mixers_done, mlp_weights=mlp_weights,
             mlp_grads=mlp_grads, gmlp_grads=gmlp_grads, in_grads=in_grads, arrived_updates=arrived_updates, me=me,
             prenorm_after=second[2]), first_dep=tok_ici_1)

    sent_rows, tok_rows = _exchange_start([in_slot(_pack_slab(g, loss_part))], [True], _ALL_PEERS, "grads_rows_start")
    res = dict(zip(("w_up", "w_down", "w_out", "gm_w_s"), g["updates"]))
    ((own_in, p_in),) = _exchange_wait(sent["in"], tok_rows, "grads_in_wait")
    upd_in = _adamw_reduce(p_in, own_in, me, lying(w_in), lying(m_w_in), lying(v_w_in), "adamw_w_in")
    res["w_in"] = tuple(jnp.transpose(t, (1, 2, 0)) for t in upd_in)
    ((_, p_rows),) = _exchange_wait(sent_rows, upd_in[1], "grads_rows_wait")
    flat = lambda t: t[0] if t.ndim == 3 else t
    small_res, loss = _adamw_slab(
        p_rows, me, *({k: flat(d[k]) for k in _SMALL_PARAMS + ("conv_w",)} for d in (w, m, v)))
    res.update(small_res)
    res = {k: tuple(r.reshape(w[k].shape) for r in res[k]) for k in _WEIGHTS}

    outs = [loss, grad_x.reshape(x.shape)]
    for part in range(4):
        outs.extend(res[k][part] for k in _WEIGHTS)
    return tuple(outs)
```

```python
import functools

import jax
import jax.numpy as jnp
import numpy as np
from jax import lax
from jax.experimental import pallas as pl
from jax.experimental.pallas import tpu as pltpu

F32 = jnp.float32
BF16 = jnp.bfloat16

D_MODEL = 1024
GM_WIDTH = 512
SSM_WIDTH = 512
CONV_CH = 1024
N_HEADS = 8
HEAD_DIM = 64
N_STATE = 128
CHUNK = 128
D_FF = 4096
IN_COLS = 2568
IN_PAD = 2688
N_DEV = 8
EPS = 1e-6
ADAM_LR, ADAM_B1, ADAM_B2, ADAM_EPS, ADAM_WD, ADAM_STEP = 0.001, 0.9, 0.999, 1e-08, 0.01, 10
VMEM_LIMIT_BYTES = 56 * 1024 * 1024
TOKEN_TILE = 512
FF_TILE = 2048
WGRAD_TILE = 512
_NT = (((1,), (1,)), ((), ()))
_TN = (((0,), (0,)), ((), ()))


def _params(*sem):
    return pltpu.CompilerParams(dimension_semantics=sem or None, vmem_limit_bytes=VMEM_LIMIT_BYTES)


def _dot(a, b, dims=None):
    if dims is None:
        return jnp.dot(a, b, preferred_element_type=F32)
    return lax.dot_general(a, b, dims, preferred_element_type=F32)


def _split_terms(x, terms):
    out, rem = [], x
    for i in range(terms):
        hi = rem.astype(BF16)
        out.append(hi)
        if i + 1 < terms:
            rem = rem - hi.astype(F32)
    return out


def _split_dot(x, m, terms):
    acc = None
    for hi in _split_terms(x, terms):
        part = _dot(hi, m)
        acc = part if acc is None else acc + part
    return acc


def _split_dot_left(m, x, terms):
    acc = None
    for hi in _split_terms(x, terms):
        part = _dot(m, hi)
        acc = part if acc is None else acc + part
    return acc


def _gelu_and_grad(x):
    c = 0.7978845608028654
    inner = c * (x + 0.044715 * x * x * x)
    t = jnp.tanh(inner)
    g = 0.5 * x * (1.0 + t)
    dg = 0.5 * (1.0 + t) + 0.5 * x * (1.0 - t * t) * c * (1.0 + 3.0 * 0.044715 * x * x)
    return g, dg


def _softplus(x):
    return jnp.maximum(x, 0.0) + jnp.log(1.0 + jnp.exp(-jnp.abs(x)))


def _rsum(x):
    return jnp.sum(x, axis=0, keepdims=True)


def _acc_rows(ref, part, first):
    val = jnp.broadcast_to(part, ref.shape)

    @pl.when(first)
    def _():
        ref[...] = val

    @pl.when(jnp.logical_not(first))
    def _():
        ref[...] += val


def _rms_bwd(n, g, dout):
    r = lax.rsqrt(jnp.mean(n * n, axis=-1, keepdims=True) + EPS)
    nh = n * r
    dg = dout * g
    dn = r * (dg - nh * jnp.mean(dg * nh, axis=-1, keepdims=True))
    return dn, _rsum(dout * nh)


def _const_mats():
    avg = np.kron(np.eye(4), np.full((HEAD_DIM, HEAD_DIM), 1.0 / HEAD_DIM))
    expand = np.zeros((CHUNK, SSM_WIDTH), np.float32)
    for h in range(N_HEADS):
        expand[h, h * HEAD_DIM:(h + 1) * HEAD_DIM] = 1.0
    tril = np.tril(np.ones((CHUNK, CHUNK), np.float32))
    as_bf16 = lambda a: jnp.asarray(a, dtype=BF16)
    return as_bf16(avg), as_bf16(expand), as_bf16(expand.T), as_bf16(tril), as_bf16(tril.T)


def _full(shape):
    nd = len(shape)
    return pl.BlockSpec(shape, lambda *_: (0,) * nd)


_HBM = pl.BlockSpec(memory_space=pltpu.HBM)
_SEM = pl.BlockSpec(memory_space=pltpu.SEMAPHORE)
_ALL_PEERS = tuple((k, 0) for k in range(1, N_DEV))
_SAME_CORE_PEERS = ((2, 0), (4, 0), (6, 0))
_SIBLING_FORWARD = ((1, 0), (1, 2), (1, 4), (1, 6))


def _flip(j, k):
    for bit in (4, 2, 1):
        if k & bit:
            j = j + bit - 2 * (j & bit)
    return j


def _copies(src, land, send_sems, recv_sems, hops, slots=None):
    x, y, c = lax.axis_index("x"), lax.axis_index("y"), lax.axis_index("c")
    me = 4 * x + 2 * y + c
    slots = range(len(src)) if slots is None else slots
    out = []
    for t in range(len(src)):
        for i, (k, b) in enumerate(hops):
            pos = (1 - x if k & 4 else x, 1 - y if k & 2 else y, 1 - c if k & 1 else c)
            peer = _flip(me, k)
            sem = slots[t] * len(hops) + i
            mk = functools.partial(pltpu.make_async_remote_copy, send_sem=send_sems.at[sem], recv_sem=recv_sems.at[sem],
                                   device_id=pos, device_id_type=pl.DeviceIdType.MESH)
            if land[t] is None and src[t].shape[0] != N_DEV:
                width = src[t].shape[1] // N_DEV
                slab = lambda j: src[t].at[:, pl.ds(pl.multiple_of(j * width, 128), width)]
                mine = functools.partial(mk, src_ref=slab(_flip(me, b)), dst_ref=slab(_flip(me, b)))
                theirs = functools.partial(mk, src_ref=slab(_flip(peer, b)), dst_ref=slab(_flip(peer, b)))
            elif land[t] is None:
                mine = functools.partial(mk, src_ref=src[t].at[_flip(me, b)], dst_ref=src[t].at[_flip(me, b)])
                theirs = functools.partial(mk, src_ref=src[t].at[_flip(peer, b)], dst_ref=src[t].at[_flip(peer, b)])
            else:
                assert b == 0
                mine = functools.partial(mk, src_ref=src[t].at[peer], dst_ref=land[t].at[me])
                theirs = functools.partial(mk, src_ref=src[t].at[peer], dst_ref=land[t].at[peer])
            out.append((mine, theirs))
    return out


def _exchange_start(srcs, inplace, peers, name, dep=None):
    n = len(srcs)
    lands = [None if ip else pltpu.with_memory_space_constraint(lax.empty(s.shape, s.dtype), pltpu.HBM)
             for s, ip in zip(srcs, inplace)]
    real_lands = [l for l in lands if l is not None]
    n_l = len(real_lands)
    deps = [] if dep is None else [dep]

    def body(*refs):
        src = refs[:n]
        land_refs = list(refs[n:n + n_l])
        send_sems, recv_sems = refs[n + n_l + len(deps)], refs[n + n_l + len(deps) + 1]
        token = refs[-1]
        land = [None if ip else land_refs.pop(0) for ip in inplace]
        for mine, _ in _copies(src, land, send_sems, recv_sems, peers):
            mine().start()
        token[...] = jnp.zeros_like(token)

    sem_t = pltpu.SemaphoreType.DMA((n * len(peers),))
    outs = pl.pallas_call(
        body, name=name,
        out_shape=(sem_t, sem_t) + tuple(pltpu.HBM(a.shape, a.dtype) for a in list(srcs) + real_lands)
        + (jax.ShapeDtypeStruct((8, 128), F32),),
        in_specs=[_HBM] * (n + n_l) + [pl.BlockSpec(memory_space=pl.ANY)] * len(deps),
        out_specs=(_SEM, _SEM) + (_HBM,) * (n + n_l) + (pl.BlockSpec(memory_space=pltpu.VMEM),),
        input_output_aliases={i: 2 + i for i in range(n + n_l)},
        compiler_params=pltpu.CompilerParams(has_side_effects=pltpu.SideEffectType.DATAFLOW_SIDE_EFFECTING),
    )(*[pltpu.with_memory_space_constraint(s, pltpu.HBM) for s in srcs], *real_lands, *deps)
    handle = dict(send=outs[0], recv=outs[1], srcs=outs[2:2 + n], lands=outs[2 + n:2 + n + n_l], inplace=inplace,
                  peers=peers)
    return handle, outs[-1]


def _exchange_wait(handle, after, name, only=None):
    srcs, lands, inplace, peers = handle["srcs"], handle["lands"], handle["inplace"], handle["peers"]
    slots = None
    if only is not None:
        assert all(inplace)
        slots, srcs, inplace = list(only), [srcs[t] for t in only], [True] * len(only)
    n, n_l = len(srcs), len(lands)

    def body(*refs):
        src = refs[:n]
        land_refs = list(refs[n:n + n_l])
        send_sems, recv_sems = refs[n + n_l], refs[n + n_l + 1]
        land = [None if ip else land_refs.pop(0) for ip in inplace]
        for mine, theirs in _copies(src, land, send_sems, recv_sems, peers, slots):
            mine().wait_send()
            theirs().wait_recv()

    outs = pl.pallas_call(
        body, name=name, out_shape=tuple(pltpu.HBM(a.shape, a.dtype) for a in list(srcs) + list(lands)),
        in_specs=[_HBM] * (n + n_l) + [_SEM, _SEM, pl.BlockSpec(memory_space=pl.ANY)],
        out_specs=(_HBM,) * (n + n_l), input_output_aliases={i: i for i in range(n + n_l)},
        compiler_params=pltpu.CompilerParams(has_side_effects=pltpu.SideEffectType.DATAFLOW_SIDE_EFFECTING),
    )(*srcs, *lands, handle["send"], handle["recv"], after)
    res, land_out = [], list(outs[n:])
    for t in range(n):
        res.append((outs[t], outs[t] if inplace[t] else land_out.pop(0)))
    return res


def _cast_to_slot(w, me, rows, name, cols=False, dep=None):
    r, cdim = w.shape[0], w.shape[-1]
    deps = [] if dep is None else [dep]

    def body(me_ref, w_ref, *rest):
        o_ref = rest[-1]
        if cols:
            o_ref[...] = w_ref[...].astype(BF16)
        else:
            o_ref[0] = w_ref[...].reshape(rows, cdim).astype(BF16)

    if cols:
        out_shape = jax.ShapeDtypeStruct((r, N_DEV * cdim), BF16)
        out_spec = pl.BlockSpec((rows, cdim), lambda i, me_ref: (i, me_ref[0]))
    else:
        out_shape = jax.ShapeDtypeStruct((N_DEV, r, cdim), BF16)
        out_spec = pl.BlockSpec((1, rows, cdim), lambda i, me_ref: (me_ref[0], i, 0))
    return pl.pallas_call(
        body, name=name, out_shape=out_shape,
        grid_spec=pltpu.PrefetchScalarGridSpec(
            num_scalar_prefetch=1, grid=(r // rows,),
            in_specs=[pl.BlockSpec((rows, cdim), lambda i, me_ref: (i, 0)) if w.ndim == 2 else
                      pl.BlockSpec((rows, 1, cdim), lambda i, me_ref: (i, 0, 0))]
            + [pl.BlockSpec(memory_space=pl.ANY)] * len(deps), out_specs=out_spec),
        compiler_params=_params("parallel"))(me, w, *deps)


def _adamw_math(w, g, m, v):
    m = ADAM_B1 * m + (1.0 - ADAM_B1) * g
    v = ADAM_B2 * v + (1.0 - ADAM_B2) * (g * g)
    m_hat = m / (1.0 - ADAM_B1 ** ADAM_STEP)
    v_hat = v / (1.0 - ADAM_B2 ** ADAM_STEP)
    delta = -ADAM_LR * (m_hat / (jnp.sqrt(v_hat) + ADAM_EPS) + ADAM_WD * w)
    return delta, m, v


def _sum_parts(me, p_ref, own):
    g = None
    for j in range(N_DEV):
        term = (p_ref[j] if own is None else jnp.where(me == j, own, p_ref[j])).astype(F32)
        g = term if g is None else g + term
    return g


def _adamw_reduce(parts, own, me, w, m, v, name):
    r, _, cdim = w.shape

    def body(me_ref, p_ref, own_ref, w_ref, m_ref, v_ref, g_out, d_out, m_out, v_out):
        g = _sum_parts(me_ref[0], p_ref, own_ref[0]).reshape(r, 1, cdim)
        d, mn, vn = _adamw_math(w_ref[...], g, m_ref[...], v_ref[...])
        g_out[...] = g
        d_out[...] = d
        m_out[...] = mn
        v_out[...] = vn

    blk = pl.BlockSpec((r, 1, cdim), lambda i, me_ref: (0, 0, 0))
    return pl.pallas_call(
        body, name=name, out_shape=(jax.ShapeDtypeStruct(w.shape, F32),) * 4,
        grid_spec=pltpu.PrefetchScalarGridSpec(
            num_scalar_prefetch=1, grid=(1,),
            in_specs=[pl.BlockSpec((N_DEV, r, cdim), lambda i, me_ref: (0, 0, 0)),
                      pl.BlockSpec((1, r, cdim), lambda i, me_ref: (me_ref[0], 0, 0)), blk, blk, blk],
            out_specs=(blk,) * 4),
        compiler_params=_params("arbitrary"))(me, parts, own, w, m, v)


_IN_SPLITS = ((0, 512), (512, 1024), (1024, 1536), (1536, 2560), (2560, IN_PAD))


def _prenorm(x, g1, tm, dep=None):
    t_tok = x.shape[0]
    deps = [] if dep is None else [dep]

    def body(x_ref, g_ref, *rest):
        xv = x_ref[...]
        r = lax.rsqrt(jnp.mean(xv * xv, axis=-1, keepdims=True) + EPS)
        rest[-1][...] = (xv * r * g_ref[...]).astype(BF16)

    row = pl.BlockSpec((tm, D_MODEL), lambda i: (i, 0))
    return pl.pallas_call(
        body, name="prenorm", grid=(t_tok // tm,), out_shape=jax.ShapeDtypeStruct((t_tok, D_MODEL), BF16),
        in_specs=[row, _full((1, D_MODEL))] + [pl.BlockSpec(memory_space=pl.ANY)] * len(deps), out_specs=row,
        compiler_params=_params("parallel"))(x, g1, *deps)


def _in_proj(h1, w_in, tm):
    t_tok = h1.shape[0]

    def body(h_ref, w_ref, *outs):
        h = h_ref[...]
        for (a, b), o_ref in zip(_IN_SPLITS, outs):
            o_ref[...] = _dot(h, w_ref[a:b, :], _NT).astype(o_ref.dtype)

    row = lambda n: pl.BlockSpec((tm, n), lambda i: (i, 0))
    widths = [b - a for a, b in _IN_SPLITS]
    dtypes = (BF16, BF16, BF16, F32, F32)
    return pl.pallas_call(
        body, name="in_proj", grid=(t_tok // tm,),
        out_shape=tuple(jax.ShapeDtypeStruct((t_tok, n), dt) for n, dt in zip(widths, dtypes)),
        in_specs=[row(D_MODEL), _full((IN_PAD, D_MODEL))], out_specs=tuple(row(n) for n in widths),
        compiler_params=_params("parallel"))(h1, w_in)


def _lane_masks():
    lane = lax.broadcasted_iota(jnp.int32, (1, 2 * HEAD_DIM), 1)
    left = (lane < HEAD_DIM).astype(F32)
    return left, 1.0 - left


def _stack_pair(v, m_l, m_r):
    return jnp.concatenate([v * m_l, v * m_r], axis=0).astype(BF16)


def _head_mean(x, avg):
    n = avg.shape[0]
    return jnp.concatenate([_split_dot(x[:, n * i:n * (i + 1)], avg, 2) for i in range(x.shape[1] // n)], axis=1)


def _gmlp_common(u, v, lnw, lnb, avg, wcat_ref, bias, m_l, m_r):
    ug, dug = _gelu_and_grad(u)
    vg, dvg = _gelu_and_grad(v)
    mu = _head_mean(vg, avg)
    vc = vg - mu
    var = _head_mean(vc * vc, avg)
    rstd = lax.rsqrt(var + EPS)
    vhat = vc * rstd
    vn = vhat * lnw + lnb
    rows = []
    for r in range(u.shape[0] // CHUNK):
        cols = []
        for j in range(N_HEADS // 2):
            pair = vn[CHUNK * r:CHUNK * (r + 1), 128 * j:128 * (j + 1)]
            cols.append(_dot(wcat_ref[j], _stack_pair(pair, m_l, m_r)))
        rows.append(jnp.concatenate(cols, axis=1) + bias)
    mixed = jnp.concatenate(rows, axis=0)
    return ug, dug, dvg, rstd, vhat, vn, mixed


_GMLP_ROWS = 4 * CHUNK


def _gmlp_fwd(u, v, lnw, lnb, wcat, bias, avg):
    t_tok = u.shape[0]
    tm = min(_GMLP_ROWS, t_tok)

    def body(u_ref, v_ref, lnw_ref, lnb_ref, wcat_ref, bias_ref, avg_ref, o_ref):
        m_l, m_r = _lane_masks()
        ug, _, _, _, _, _, mixed = _gmlp_common(
            u_ref[...].astype(F32), v_ref[...].astype(F32), lnw_ref[...], lnb_ref[...], avg_ref[...], wcat_ref,
            bias_ref[...], m_l, m_r)
        o_ref[...] = (ug * mixed).astype(BF16)

    row = pl.BlockSpec((tm, GM_WIDTH), lambda i: (i, 0))
    return pl.pallas_call(
        body, name="gmlp_fwd", grid=(t_tok // tm,), out_shape=jax.ShapeDtypeStruct((t_tok, GM_WIDTH), BF16),
        in_specs=[row, row, _full((1, GM_WIDTH)), _full((1, GM_WIDTH)), _full(wcat.shape), _full(bias.shape),
                  _full(avg.shape)],
        out_specs=row, compiler_params=_params("parallel"))(u, v, lnw, lnb, wcat, bias, avg)


def _shift_rows(x, edge, j, down):
    groups, cols = x.shape[0] // 8, x.shape[1]
    amount = j if down else 8 - j
    rot = pltpu.roll(x.reshape(groups, 8, cols), amount, axis=1)
    edge_rot = pltpu.roll(edge, amount, axis=0)[None]
    sub = lax.broadcasted_iota(jnp.int32, (1, 8, 1), 1)
    if down:
        out = jnp.where(sub < j, jnp.concatenate([edge_rot, rot[:-1]], axis=0), rot)
    else:
        out = jnp.where(sub < 8 - j, rot, jnp.concatenate([rot[1:], edge_rot], axis=0))
    return out.reshape(x.shape)


def _conv_pre(xbc, tail, cw_ref, cb):
    taps = [_shift_rows(xbc, tail, 3 - k, True) for k in range(3)] + [xbc]
    return cb + cw_ref[0:1, :] * taps[0] + cw_ref[1:2, :] * taps[1] + cw_ref[2:3, :] * taps[2] + cw_ref[3:4, :] * taps[3]


def _ssd_common(pre, dtr, dtb, alog, expand, tril):
    q = CHUNK
    sg = jax.nn.sigmoid(pre)
    act = pre * sg
    lane = lax.broadcasted_iota(jnp.int32, (1, CHUNK), 1)
    a_row = jnp.where(lane < N_HEADS, -jnp.exp(alog), 0.0)
    dtp = dtr + dtb
    dt = _softplus(dtp)
    a_cs = _split_dot_left(tril, dt * a_row, 3)
    a_cs_t = a_cs.T
    dt_exp = _split_dot(dt, expand, 3)
    a_exp = _split_dot(a_cs, expand, 3)
    a_end = a_exp[q - 1:q, :]
    li = lax.broadcasted_iota(jnp.int32, (q, q), 0)
    si = lax.broadcasted_iota(jnp.int32, (q, q), 1)
    causal = si <= li
    decay = []
    for h in range(N_HEADS):
        seg = a_cs[:, h:h + 1] - a_cs_t[h:h + 1, :]
        decay.append(jnp.where(causal, jnp.exp(jnp.minimum(seg, 0.0)), 0.0))
    return dict(pre=pre, sg=sg, act=act, a_row=a_row, dtp=dtp, dt=dt, dt_exp=dt_exp, a_exp=a_exp,
                e=jnp.exp(a_exp), w_end=jnp.exp(a_end - a_exp), cd=jnp.exp(a_end), decay=decay)


def _ssd_specs(t_tok, seq, reverse):
    nb, nc = t_tok // seq, seq // CHUNK

    def chunk(c):
        return nc - 1 - c if reverse else c

    def row(n, col=0):
        return pl.BlockSpec((nb, CHUNK, n), lambda c: (0, chunk(c), col))

    tail = pl.BlockSpec((nb, 8, CONV_CH), lambda c: (0, jnp.maximum(chunk(c) * (CHUNK // 8) - 1, 0), 0))
    states = pl.BlockSpec((nb, 1, N_STATE, SSM_WIDTH), lambda c: (0, chunk(c), 0, 0))
    fold = lambda a: a.reshape(nb, seq, a.shape[-1])
    unfold = lambda a: a.reshape(t_tok, a.shape[-1])
    return nb, nc, row, tail, states, fold, unfold


def _ssd_fwd(z, xbc, dtr, cw, cb, dtb, alog, dskip_exp, nw, expand, tril, seq, dep=None):
    t_tok = z.shape[0]
    nb, nc, row, tail, states_spec, fold, unfold = _ssd_specs(t_tok, seq, False)

    def body(z_ref, xbc_ref, tail_ref, dtr_ref, cw_ref, cb_ref, dtb_ref, alog_ref, dsk_ref, nw_ref, exp_ref,
             tril_ref, o_ref, y_ref, st_ref, pre_ref, state_ref):
        c = pl.program_id(0)

        @pl.when(c == 0)
        def _():
            state_ref[...] = jnp.zeros_like(state_ref)

        m_l, m_r = _lane_masks()
        for s in range(nb):
            pre = _conv_pre(xbc_ref[s], jnp.where(c == 0, 0.0, tail_ref[s]), cw_ref, cb_ref[...])
            pre_ref[s] = pre
            f = _ssd_common(pre, dtr_ref[s], dtb_ref[...], alog_ref[...], exp_ref[...], tril_ref[...])
            act = f["act"]
            xs = act[:, :SSM_WIDTH]
            xdt = xs * f["dt_exp"]
            xw = xdt * f["w_end"]
            state = state_ref[s]
            st_ref[s, 0] = state
            ydiag, yoff, snew = [], [], []
            for g in range(2):
                bg = act[:, 512 + 128 * g:640 + 128 * g].astype(BF16)
                cg = act[:, 768 + 128 * g:896 + 128 * g].astype(BF16)
                cb_mat = _dot(cg, bg, _NT)
                for pr in range(2):
                    h0 = 4 * g + 2 * pr
                    gcat = jnp.concatenate(
                        [(cb_mat * f["decay"][h0]).astype(BF16), (cb_mat * f["decay"][h0 + 1]).astype(BF16)], axis=1)
                    ydiag.append(_dot(gcat, _stack_pair(xdt[:, 64 * h0:64 * h0 + 128], m_l, m_r)))
                yoff.append(_dot(cg, state[:, 256 * g:256 * (g + 1)].astype(BF16)))
                snew.append(_dot(bg, xw[:, 256 * g:256 * (g + 1)].astype(BF16), _TN))
            y = jnp.concatenate(ydiag, axis=1) + f["e"] * jnp.concatenate(yoff, axis=1) + dsk_ref[...] * xs
            state_ref[s] = state * f["cd"] + jnp.concatenate(snew, axis=1)
            y_ref[s] = y
            zv = z_ref[s].astype(F32)
            yg = y * (zv * jax.nn.sigmoid(zv))
            outs = []
            for g in range(2):
                ygg = yg[:, 256 * g:256 * (g + 1)]
                outs.append(ygg * lax.rsqrt(jnp.mean(ygg * ygg, axis=-1, keepdims=True) + EPS))
            o_ref[s] = (jnp.concatenate(outs, axis=1) * nw_ref[...]).astype(BF16)

    consts = [cw, cb, dtb, alog, dskip_exp, nw, expand, tril]
    deps = [] if dep is None else [dep]
    n_in = 4 + len(consts)

    def body_skipping_dep(*refs):
        body(*refs[:n_in], *refs[n_in + len(deps):])

    sd = lambda n, dt: jax.ShapeDtypeStruct((nb, seq, n), dt)
    o, y, states, pre = pl.pallas_call(
        body_skipping_dep, name="ssd_fwd", grid=(nc,),
        out_shape=(sd(SSM_WIDTH, BF16), sd(SSM_WIDTH, F32), jax.ShapeDtypeStruct((nb, nc, N_STATE, SSM_WIDTH), F32),
                   sd(CONV_CH, F32)),
        in_specs=[row(SSM_WIDTH), row(CONV_CH), tail, row(CHUNK)] + [_full(a.shape) for a in consts]
        + [pl.BlockSpec(memory_space=pl.ANY)] * len(deps),
        out_specs=(row(SSM_WIDTH), row(SSM_WIDTH), states_spec, row(CONV_CH)),
        scratch_shapes=[pltpu.VMEM((nb, N_STATE, SSM_WIDTH), F32)],
        compiler_params=_params("arbitrary"))(fold(z), fold(xbc), fold(xbc), fold(dtr), *consts, *deps)
    return unfold(o), unfold(y), states, unfold(pre)


def _out_proj(mix_a, mix_b, w_out, x, g2, g3, tm, dep=None):
    t_tok = x.shape[0]
    deps = [] if dep is None else [dep]

    def body(a_ref, b_ref, w_ref, x_ref, g2_ref, g3_ref, *rest):
        o_ref, x2_ref, h3_ref = rest[-3:]
        o = _dot(a_ref[...], w_ref[0:GM_WIDTH, :]) + _dot(b_ref[...], w_ref[GM_WIDTH:, :])
        o_ref[...] = o
        r2 = lax.rsqrt(jnp.mean(o * o, axis=-1, keepdims=True) + EPS)
        x2 = x_ref[...] + o * r2 * g2_ref[...]
        x2_ref[...] = x2
        r3 = lax.rsqrt(jnp.mean(x2 * x2, axis=-1, keepdims=True) + EPS)
        h3_ref[...] = (x2 * r3 * g3_ref[...]).astype(BF16)

    row = lambda n: pl.BlockSpec((tm, n), lambda i: (i, 0))
    sd = lambda dt: jax.ShapeDtypeStruct((t_tok, D_MODEL), dt)
    return pl.pallas_call(
        body, name="out_proj", grid=(t_tok // tm,), out_shape=(sd(F32), sd(F32), sd(BF16)),
        in_specs=[row(GM_WIDTH), row(SSM_WIDTH), _full((D_MODEL, D_MODEL)), row(D_MODEL), _full((1, D_MODEL)),
                  _full((1, D_MODEL))] + [pl.BlockSpec(memory_space=pl.ANY)] * len(deps),
        out_specs=(row(D_MODEL),) * 3, compiler_params=_params("parallel"))(mix_a, mix_b, w_out, x, g2, g3, *deps)


def _mlp_fwd(h3, w_up, w_down, x2, target, g4, tm, tf):
    t_tok = x2.shape[0]

    def up_body(h_ref, wu_ref, ra_ref):
        ra_ref[...] = jnp.maximum(_dot(h_ref[...], wu_ref[...]), 0.0).astype(BF16)

    tu = min(2 * tm, t_tok)
    ra = pl.pallas_call(
        up_body, name="mlp_up", grid=(D_FF // tf, t_tok // tu), out_shape=jax.ShapeDtypeStruct((t_tok, D_FF), BF16),
        in_specs=[pl.BlockSpec((tu, D_MODEL), lambda j, i: (i, 0)), pl.BlockSpec((D_MODEL, tf), lambda j, i: (0, j))],
        out_specs=pl.BlockSpec((tu, tf), lambda j, i: (i, j)), compiler_params=_params("parallel", "parallel"))(h3, w_up)

    def down_body(ra_ref, wd_ref, x2_ref, t_ref, g4_ref, dd_ref, dy_ref, dg4_ref, loss_ref):
        i = pl.program_id(0)
        rav = ra_ref[...]
        dvec = _dot(rav * rav, wd_ref[...])
        r4 = lax.rsqrt(jnp.mean(dvec * dvec, axis=-1, keepdims=True) + EPS)
        dn = dvec * r4
        g4 = g4_ref[...]
        err = x2_ref[...] + dn * g4 - t_ref[...]
        dy = err * (1.0 / D_MODEL)
        dy_ref[...] = dy
        dg = dy * g4
        dd_ref[...] = (r4 * (dg - dn * jnp.mean(dg * dn, axis=-1, keepdims=True))).astype(BF16)
        _acc_rows(dg4_ref, _rsum(dy * dn), i == 0)
        tile_loss = 0.5 * jnp.sum(jnp.sum(err * err, axis=-1, keepdims=True), axis=0, keepdims=True) / D_MODEL
        _acc_rows(loss_ref, jnp.broadcast_to(tile_loss, (1, 128)), i == 0)

    row = pl.BlockSpec((tm, D_MODEL), lambda i: (i, 0))
    dd, dy, dg4, loss = pl.pallas_call(
        down_body, name="mlp_down", grid=(t_tok // tm,),
        out_shape=(jax.ShapeDtypeStruct((t_tok, D_MODEL), BF16), jax.ShapeDtypeStruct((t_tok, D_MODEL), F32),
                   jax.ShapeDtypeStruct((1, D_MODEL), F32), jax.ShapeDtypeStruct((1, 128), F32)),
        in_specs=[pl.BlockSpec((tm, D_FF), lambda i: (i, 0)), _full((D_FF, D_MODEL)), row, row, _full((1, D_MODEL))],
        out_specs=(row, row, _full((1, D_MODEL)), _full((1, 128))),
        compiler_params=_params("arbitrary"))(ra, w_down, x2, target, g4)
    return ra, dd, dy, dg4, loss


def _mlp_bwd(dd, w_down, ra, w_up, x2, dy, o, g3, g2, tm, tf):
    t_tok = x2.shape[0]

    def hidden_body(dd_ref, wd_ref, ra_ref, da_ref):
        df = _dot(dd_ref[...], wd_ref[...], _NT)
        da_ref[...] = (df * (2.0 * ra_ref[...].astype(F32))).astype(BF16)

    tu = min(2 * tm, t_tok)
    da = pl.pallas_call(
        hidden_body, name="mlp_bwd_hidden", grid=(D_FF // tf, t_tok // tu),
        out_shape=jax.ShapeDtypeStruct((t_tok, D_FF), BF16),
        in_specs=[pl.BlockSpec((tu, D_MODEL), lambda j, i: (i, 0)), pl.BlockSpec((tf, D_MODEL), lambda j, i: (j, 0)),
                  pl.BlockSpec((tu, tf), lambda j, i: (i, j))],
        out_specs=pl.BlockSpec((tu, tf), lambda j, i: (i, j)),
        compiler_params=_params("parallel", "parallel"))(dd, w_down, ra)

    def in_body(da_ref, wu_ref, x2_ref, dy_ref, o_ref, g3_ref, g2_ref, dx2_ref, do_ref, dg3_ref, dg2_ref):
        i = pl.program_id(0)
        dh3 = _dot(da_ref[...], wu_ref[...], _NT)
        dn3, dg3 = _rms_bwd(x2_ref[...], g3_ref[...], dh3)
        dx2 = dy_ref[...] + dn3
        dx2_ref[...] = dx2
        do, dg2 = _rms_bwd(o_ref[...], g2_ref[...], dx2)
        do_ref[...] = do.astype(BF16)
        _acc_rows(dg3_ref, dg3, i == 0)
        _acc_rows(dg2_ref, dg2, i == 0)

    row = pl.BlockSpec((tm, D_MODEL), lambda i: (i, 0))
    vec = _full((1, D_MODEL))
    sd = lambda dt: jax.ShapeDtypeStruct((t_tok, D_MODEL), dt)
    dx2, do, dg3, dg2 = pl.pallas_call(
        in_body, name="mlp_bwd_in", grid=(t_tok // tm,),
        out_shape=(sd(F32), sd(BF16), jax.ShapeDtypeStruct((1, D_MODEL), F32), jax.ShapeDtypeStruct((1, D_MODEL), F32)),
        in_specs=[pl.BlockSpec((tm, D_FF), lambda i: (i, 0)), _full((D_MODEL, D_FF)), row, row, row, vec, vec],
        out_specs=(row, row, vec, vec), compiler_params=_params("arbitrary"))(da, w_up, x2, dy, o, g3, g2)
    return da, dx2, do, dg3, dg2


def _wgrad(a, b, out_blocks, bm, bn, bk, square_a, name, dep=None):
    t_tok, m = a.shape
    n = b.shape[1]
    nk = t_tok // bk

    def body(a_ref, b_ref, *rest):
        o_ref, acc_ref = rest[-2:]
        k = pl.program_id(2)
        av = a_ref[...]
        if square_a:
            av = av * av
        part = _dot(av, b_ref[...], _TN)

        def emit(res):
            if out_blocks is None:
                o_ref[...] = res.astype(BF16)
            else:
                o_ref[0] = res.astype(BF16)

        if nk == 1:
            emit(part)
            return

        @pl.when(k == 0)
        def _():
            acc_ref[...] = part

        @pl.when(k > 0)
        def _():
            acc_ref[...] += part

        @pl.when(k == nk - 1)
        def _():
            emit(acc_ref[...])

    if out_blocks is None:
        out_shape = jax.ShapeDtypeStruct((m, n), BF16)
        out_spec = pl.BlockSpec((bm, bn), lambda i, j, k: (i, j))
    else:
        assert n // out_blocks == bn
        out_shape = jax.ShapeDtypeStruct((out_blocks, m, bn), BF16)
        out_spec = pl.BlockSpec((1, bm, bn), lambda i, j, k: (j, i, 0))
    deps = [] if dep is None else [dep]
    return pl.pallas_call(
        body, name=name, grid=(m // bm, n // bn, nk), out_shape=out_shape,
        in_specs=[pl.BlockSpec((bk, bm), lambda i, j, k: (k, i)), pl.BlockSpec((bk, bn), lambda i, j, k: (k, j))]
        + [pl.BlockSpec(memory_space=pl.ANY)] * len(deps),
        out_specs=out_spec, scratch_shapes=[pltpu.VMEM((bm, bn) if nk > 1 else (8, 128), F32)],
        compiler_params=_params("parallel", "parallel", "arbitrary"))(a, b, *deps)


def _wgrad_in_chunked(h1, pieces, bn, bk, dep=None):
    t_tok = h1.shape[0]
    nk = t_tok // bk
    shard = IN_COLS // N_DEV
    widths = [b - a for a, b in _IN_SPLITS]

    def body(h_ref, *rest):
        piece_refs = rest[:len(widths)]
        o_ref, acc_ref = rest[-2:]
        k = pl.program_id(1)
        hv = h_ref[...]
        for (a, b), r in zip(_IN_SPLITS, piece_refs):
            part = _dot(r[...], hv, _TN)

            @pl.when(k == 0)
            def _():
                acc_ref[a:b, :] = part

            @pl.when(k > 0)
            def _():
                acc_ref[a:b, :] += part

        @pl.when(k == nk - 1)
        def _():
            for j in range(N_DEV):
                o_ref[j] = acc_ref[shard * j:shard * (j + 1), :].astype(BF16)

    deps = [] if dep is None else [dep]
    return pl.pallas_call(
        body, name="wgrad_in", grid=(D_MODEL // bn, nk), out_shape=jax.ShapeDtypeStruct((N_DEV, shard, D_MODEL), BF16),
        in_specs=[pl.BlockSpec((bk, bn), lambda j, k: (k, j))] + [pl.BlockSpec((bk, n), lambda j, k: (k, 0)) for n in widths]
        + [pl.BlockSpec(memory_space=pl.ANY)] * len(deps),
        out_specs=pl.BlockSpec((N_DEV, shard, bn), lambda j, k: (0, 0, j)),
        scratch_shapes=[pltpu.VMEM((IN_PAD, bn), F32)],
        compiler_params=_params("parallel", "arbitrary"))(h1, *pieces, *deps)


def _wgrad_pieces(h1, pieces, bn, name, dep=None):
    t_tok = h1.shape[0]
    widths = [p.shape[1] for p in pieces]
    starts = [sum(widths[:i]) for i in range(len(widths))]

    def body(h_ref, *rest):
        piece_refs = rest[:len(widths)]
        o_ref = rest[-1]
        hv = h_ref[...]
        for a, n, r in zip(starts, widths, piece_refs):
            o_ref[a:a + n, :] = _dot(r[...], hv, _TN).astype(BF16)

    deps = [] if dep is None else [dep]
    return pl.pallas_call(
        body, name=name, grid=(D_MODEL // bn,), out_shape=jax.ShapeDtypeStruct((sum(widths), D_MODEL), BF16),
        in_specs=[pl.BlockSpec((t_tok, bn), lambda j: (0, j))] + [pl.BlockSpec((t_tok, n), lambda j: (0, 0)) for n in widths]
        + [pl.BlockSpec(memory_space=pl.ANY)] * len(deps),
        out_specs=pl.BlockSpec((sum(widths), bn), lambda j: (0, j)),
        compiler_params=_params("parallel"))(h1, *pieces, *deps)


def _dmix(do, w_out, tm, dep=None):
    t_tok = do.shape[0]

    def body(d_ref, w_ref, *rest):
        rest[-1][...] = _dot(d_ref[...], w_ref[...], _NT).astype(BF16)

    row = pl.BlockSpec((tm, D_MODEL), lambda i: (i, 0))
    deps = [] if dep is None else [dep]
    return pl.pallas_call(
        body, name="dmix", grid=(t_tok // tm,), out_shape=jax.ShapeDtypeStruct((t_tok, D_MODEL), BF16),
        in_specs=[row, _full((D_MODEL, D_MODEL))] + [pl.BlockSpec(memory_space=pl.ANY)] * len(deps), out_specs=row,
        compiler_params=_params("parallel"))(do, w_out, *deps)


def _gmlp_bwd(dmix, u, v, lnw, lnb, wcat, wtcat, bias, avg, expand_t):
    t_tok = u.shape[0]
    tm = min(_GMLP_ROWS, t_tok)

    def body(dm_ref, u_ref, v_ref, lnw_ref, lnb_ref, wcat_ref, wtcat_ref, bias_ref, avg_ref, expt_ref, du_ref, dv_ref,
             dw_ref, db_ref, dlnw_ref, dlnb_ref):
        i = pl.program_id(0)
        m_l, m_r = _lane_masks()
        avg = avg_ref[...]
        lnw = lnw_ref[...]
        ug, dug, dvg, rstd, vhat, vn, mixed = _gmlp_common(
            u_ref[...].astype(F32), v_ref[...].astype(F32), lnw, lnb_ref[...], avg, wcat_ref, bias_ref[...], m_l, m_r)
        dya = dm_ref[...].astype(F32)
        du_ref[...] = (dya * mixed * dug).astype(BF16)
        dmixed = dya * ug
        dvn_rows, dws, dbt = [], [None] * N_HEADS, None
        for r in range(tm // CHUNK):
            dvn_cols = []
            for j in range(N_HEADS // 2):
                dmp = dmixed[CHUNK * r:CHUNK * (r + 1), 128 * j:128 * (j + 1)]
                dvn_cols.append(_dot(wtcat_ref[j], _stack_pair(dmp, m_l, m_r)))
                vnp = vn[CHUNK * r:CHUNK * (r + 1), 128 * j:128 * (j + 1)].astype(BF16)
                for i_h, mask in enumerate((m_l, m_r)):
                    part = _dot((dmp * mask).astype(BF16), vnp, _NT)
                    dws[2 * j + i_h] = part if r == 0 else dws[2 * j + i_h] + part
            dvn_rows.append(jnp.concatenate(dvn_cols, axis=1))
            part = _split_dot(dmixed[CHUNK * r:CHUNK * (r + 1), :], expt_ref[...], 2)
            dbt = part if r == 0 else dbt + part
        dvn = jnp.concatenate(dvn_rows, axis=0)
        dvh = dvn * lnw
        dvgel = rstd * (dvh - _head_mean(dvh, avg) - vhat * _head_mean(dvh * vhat, avg))
        dv_ref[...] = (dvgel * dvg).astype(BF16)
        first = i == 0

        @pl.when(first)
        def _():
            for h in range(N_HEADS):
                dw_ref[h] = dws[h]
            db_ref[...] = dbt

        @pl.when(jnp.logical_not(first))
        def _():
            for h in range(N_HEADS):
                dw_ref[h] += dws[h]
            db_ref[...] += dbt

        _acc_rows(dlnw_ref, _rsum(dvn * vhat), first)
        _acc_rows(dlnb_ref, _rsum(dvn), first)

    row = pl.BlockSpec((tm, GM_WIDTH), lambda i: (i, 0))
    consts = [lnw, lnb, wcat, wtcat, bias, avg, expand_t]
    return pl.pallas_call(
        body, name="gmlp_bwd", grid=(t_tok // tm,),
        out_shape=(jax.ShapeDtypeStruct((t_tok, GM_WIDTH), BF16), jax.ShapeDtypeStruct((t_tok, GM_WIDTH), BF16),
                   jax.ShapeDtypeStruct((N_HEADS, CHUNK, CHUNK), F32), jax.ShapeDtypeStruct((CHUNK, CHUNK), F32),
                   jax.ShapeDtypeStruct((1, GM_WIDTH), F32), jax.ShapeDtypeStruct((1, GM_WIDTH), F32)),
        in_specs=[row, row, row] + [_full(a.shape) for a in consts],
        out_specs=(row, row, _full((N_HEADS, CHUNK, CHUNK)), _full((CHUNK, CHUNK)), _full((1, GM_WIDTH)),
                   _full((1, GM_WIDTH))),
        compiler_params=_params("arbitrary"))(dmix, u, v, *consts)


def _ssd_bwd(dmix, z, xbc, pre, dtr, y, states, cw, cb, dtb, alog, dskip_exp, nw, expand, expand_t, tril, triu, seq,
             dep=None):
    t_tok = z.shape[0]
    nb, nc, row, _, states_spec, fold, unfold = _ssd_specs(t_tok, seq, True)
    q = CHUNK

    def one_sequence(s, dm_ref, z_ref, xbc_ref, pre_ref, dtr_ref, y_ref, st_ref, cw_ref, dtb_ref, alog_ref, dsk_ref,
                     nw_ref, exp_ref, expt_ref, tril_ref, triu_ref, dz_ref, dxbc_ref, ddt_ref, dhead_ref, dstate_ref):
        m_l, m_r = _lane_masks()
        expt = expt_ref[...]
        f = _ssd_common(pre_ref[s], dtr_ref[s], dtb_ref[...], alog_ref[...], exp_ref[...], tril_ref[...])
        act, pre, sg = f["act"], f["pre"], f["sg"]
        xs = act[:, :SSM_WIDTH]
        xdt = xs * f["dt_exp"]
        xw = xdt * f["w_end"]
        state = st_ref[s, 0]
        dstate = dstate_ref[s]
        zv, yv, dout, nw = z_ref[s].astype(F32), y_ref[s], dm_ref[s].astype(F32), nw_ref[...]
        sz = jax.nn.sigmoid(zv)
        sl = zv * sz
        yg = yv * sl
        tv = dout * nw
        dyg_parts, ygh_parts = [], []
        for g in range(2):
            ygg = yg[:, 256 * g:256 * (g + 1)]
            rr = lax.rsqrt(jnp.mean(ygg * ygg, axis=-1, keepdims=True) + EPS)
            ygh = ygg * rr
            tg = tv[:, 256 * g:256 * (g + 1)]
            dyg_parts.append(rr * (tg - ygh * jnp.mean(tg * ygh, axis=-1, keepdims=True)))
            ygh_parts.append(ygh)
        dyg = jnp.concatenate(dyg_parts, axis=1)
        dnw = _rsum(dout * jnp.concatenate(ygh_parts, axis=1))
        dy = dyg * sl
        dz_ref[s] = (dyg * yv * (sz * (1.0 + zv * (1.0 - sz)))).astype(BF16)
        ddsk = _rsum(dy * xs)
        dye = dy * f["e"]
        lane = lax.broadcasted_iota(jnp.int32, (q, q), 1)
        sub = lax.broadcasted_iota(jnp.int32, (q, q), 0)
        rs_mat = jnp.zeros((q, q), F32)
        cs_mat = jnp.zeros((q, q), F32)
        dxdt_cols, yoff, dst_in, dxw, d_b, d_c = [], [], [], [], [], []
        for g in range(2):
            bg = act[:, 512 + 128 * g:640 + 128 * g].astype(BF16)
            cg = act[:, 768 + 128 * g:896 + 128 * g].astype(BF16)
            cb_mat = _dot(cg, bg, _NT)
            stg = state[:, 256 * g:256 * (g + 1)].astype(BF16)
            dyeg = dye[:, 256 * g:256 * (g + 1)].astype(BF16)
            yoff.append(_dot(cg, stg))
            dcg = _dot(dyeg, stg, _NT)
            dst_in.append(_dot(cg, dyeg, _TN))
            dcb = jnp.zeros((q, q), F32)
            for pr in range(2):
                h0 = 4 * g + 2 * pr
                gf = [cb_mat * f["decay"][h0], cb_mat * f["decay"][h0 + 1]]
                gcat = jnp.concatenate([gf[0].astype(BF16), gf[1].astype(BF16)], axis=1)
                xst = _stack_pair(xdt[:, 64 * h0:64 * h0 + 128], m_l, m_r)
                dyp = dy[:, 64 * h0:64 * h0 + 128].astype(BF16)
                dgcat = _dot(dyp, xst, _NT)
                dxst = _dot(gcat, dyp, _TN)
                dxdt_cols.append(dxst[:q] * m_l + dxst[q:] * m_r)
                for i in range(2):
                    h = h0 + i
                    dg = dgcat[:, q * i:q * (i + 1)]
                    mm = dg * gf[i]
                    rs_mat = rs_mat + jnp.where(lane == h, jnp.sum(mm, axis=1, keepdims=True), 0.0)
                    cs_mat = cs_mat + jnp.where(sub == h, jnp.sum(mm, axis=0, keepdims=True), 0.0)
                    dcb = dcb + dg * f["decay"][h]
            dcb16 = dcb.astype(BF16)
            dstg = dstate[:, 256 * g:256 * (g + 1)].astype(BF16)
            d_c.append(dcg + _dot(dcb16, bg))
            dxw.append(_dot(bg, dstg))
            d_b.append(_dot(dcb16, cg, _TN) + _dot(xw[:, 256 * g:256 * (g + 1)].astype(BF16), dstg, _NT))
        dxw = jnp.concatenate(dxw, axis=1)
        dxdt = jnp.concatenate(dxdt_cols, axis=1) + dxw * f["w_end"]
        qv = dxw * xw
        end_row = _rsum(qv) + _rsum(dstate * state) * f["cd"]
        x2 = dye * jnp.concatenate(yoff, axis=1) - qv
        row_i = lax.broadcasted_iota(jnp.int32, (q, 1), 0)
        x2 = x2 + jnp.where(row_i == q - 1, end_row, 0.0)
        da_cs = _split_dot(x2, expt, 2) + rs_mat - cs_mat.T
        ddt = _split_dot(dxdt * xs, expt, 2)
        dxs = dsk_ref[...] * dy + dxdt * f["dt_exp"]
        dda = _split_dot_left(triu_ref[...], da_cs, 3)
        ddt = ddt + dda * f["a_row"]
        dalog = _rsum(dda * f["dt"]) * f["a_row"]
        draw = ddt * jax.nn.sigmoid(f["dtp"])
        ddt_ref[s] = draw.astype(BF16)
        dact = jnp.concatenate([dxs] + d_b + d_c, axis=1)
        dpre = dact * (sg * (1.0 + pre * (1.0 - sg)))
        dhead = dhead_ref[s]
        xv = xbc_ref[s]
        shifted = [_shift_rows(dpre, dhead, 3 - k, False) for k in range(3)] + [dpre]
        dxbc = cw_ref[3:4, :] * dpre
        for k in range(3):
            dxbc = dxbc + cw_ref[k:k + 1, :] * shifted[k]
        dxbc_ref[s] = dxbc.astype(BF16)
        dhead_ref[s] = dpre[0:8, :]
        dstate_ref[s] = dstate * f["cd"] + jnp.concatenate(dst_in, axis=1)
        row8 = lax.broadcasted_iota(jnp.int32, (8, 1), 0)
        dcw = jnp.zeros((8, CONV_CH), F32)
        for k in range(4):
            dcw = dcw + jnp.where(row8 == k, _rsum(shifted[k] * xv), 0.0)
        return dcw, _rsum(dpre), _rsum(draw), dalog, _split_dot(ddsk, expt, 3), dnw

    def body(dm_ref, z_ref, xbc_ref, pre_ref, dtr_ref, y_ref, st_ref, cw_ref, cb_ref, dtb_ref, alog_ref, dsk_ref,
             nw_ref, exp_ref, expt_ref, tril_ref, triu_ref, dz_ref, dxbc_ref, ddt_ref, dcw_ref, dcb_ref, ddtb_ref,
             dalog_ref, dd_ref, dnw_ref, dhead_ref, dstate_ref):
        c = pl.program_id(0)
        first = c == 0

        @pl.when(first)
        def _():
            dstate_ref[...] = jnp.zeros_like(dstate_ref)
            dhead_ref[...] = jnp.zeros_like(dhead_ref)

        total = None
        for s in range(nb):
            parts = one_sequence(s, dm_ref, z_ref, xbc_ref, pre_ref, dtr_ref, y_ref, st_ref, cw_ref, dtb_ref, alog_ref,
                                 dsk_ref, nw_ref, exp_ref, expt_ref, tril_ref, triu_ref, dz_ref, dxbc_ref, ddt_ref,
                                 dhead_ref, dstate_ref)
            total = parts if total is None else tuple(a + b for a, b in zip(total, parts))
        dcw = total[0]

        @pl.when(first)
        def _():
            dcw_ref[...] = dcw

        @pl.when(jnp.logical_not(first))
        def _():
            dcw_ref[...] += dcw

        for ref, part in zip((dcb_ref, ddtb_ref, dalog_ref, dd_ref, dnw_ref), total[1:]):
            _acc_rows(ref, part, first)

    consts = [cw, cb, dtb, alog, dskip_exp, nw, expand, expand_t, tril, triu]
    deps = [] if dep is None else [dep]
    n_in = 7 + len(consts)

    def body_skipping_dep(*refs):
        body(*refs[:n_in], *refs[n_in + len(deps):])

    acc = lambda n: jax.ShapeDtypeStruct((1, n), F32)
    sd = lambda n: jax.ShapeDtypeStruct((nb, seq, n), BF16)
    dz, dxbc, ddt, *small_grads = pl.pallas_call(
        body_skipping_dep, name="ssd_bwd", grid=(nc,),
        out_shape=(sd(SSM_WIDTH), sd(CONV_CH), sd(CHUNK), jax.ShapeDtypeStruct((8, CONV_CH), F32), acc(CONV_CH),
                   acc(CHUNK), acc(CHUNK), acc(CHUNK), acc(SSM_WIDTH)),
        in_specs=[row(SSM_WIDTH, col=1), row(SSM_WIDTH), row(CONV_CH), row(CONV_CH), row(CHUNK), row(SSM_WIDTH),
                  states_spec]
        + [_full(a.shape) for a in consts] + [pl.BlockSpec(memory_space=pl.ANY)] * len(deps),
        out_specs=(row(SSM_WIDTH), row(CONV_CH), row(CHUNK), _full((8, CONV_CH)), _full((1, CONV_CH)),
                   _full((1, CHUNK)), _full((1, CHUNK)), _full((1, CHUNK)), _full((1, SSM_WIDTH))),
        scratch_shapes=[pltpu.VMEM((nb, 8, CONV_CH), F32), pltpu.VMEM((nb, N_STATE, SSM_WIDTH), F32)],
        compiler_params=_params("arbitrary"))(
            fold(dmix), fold(z), fold(xbc), fold(pre), fold(dtr), fold(y), states, *consts, *deps)
    return (unfold(dz), unfold(dxbc), unfold(ddt), *small_grads)


def _in_bwd(du, dv, dz, dxbc, ddt, w_in, x, dx2, g1, tm, me, riders=(), dep=None):
    t_tok = x.shape[0]
    steps = t_tok // tm

    n_in = [5 + ("mask" in rd) for rd in riders]
    first_in = [sum(n_in[:r]) for r in range(len(riders))]

    def body(me_ref, du_ref, dv_ref, dz_ref, dxbc_ref, ddt_ref, w_ref, x_ref, dx2_ref, g_ref, *rest):
        outs = rest[len(rest) - 2 - 4 * len(riders):]
        gx_ref, dg_ref = outs[:2]
        i = pl.program_id(0)
        dh = None
        for (a, b), ref in zip(_IN_SPLITS, (du_ref, dv_ref, dz_ref, dxbc_ref, ddt_ref)):
            part = _dot(ref[...], w_ref[a:b, :])
            dh = part if dh is None else dh + part
        dn, dg = _rms_bwd(x_ref[...], g_ref[...], dh)
        gx_ref[...] = dx2_ref[...] + dn
        _acc_rows(dg_ref, dg, i == 0)
        for r in range(len(riders)):
            p_ref, own_ref, w_ref_r, m_ref_r, v_ref_r = rest[first_in[r]:first_in[r] + 5]
            g = _sum_parts(me_ref[0], p_ref, own_ref[0])
            if n_in[r] == 6:
                g = g * rest[first_in[r] + 5][...]
            d, mn, vn = _adamw_math(w_ref_r[...], g, m_ref_r[...], v_ref_r[...])
            for o_ref, val in zip(outs[2 + 4 * r:6 + 4 * r], (g, d, mn, vn)):
                o_ref[...] = val

    row = lambda n: pl.BlockSpec((tm, n), lambda i, me_ref: (i, 0))
    whole = lambda shape: pl.BlockSpec(shape, lambda i, me_ref: (0,) * len(shape))
    widths = [b - a for a, b in _IN_SPLITS]
    deps = [] if dep is None else [dep]
    rider_args, rider_specs, rider_out_shapes, rider_out_specs = [], [], [], []
    for rd in riders:
        rows, cols = rd["w"].shape[0] // steps, rd["w"].shape[1]
        blk = pl.BlockSpec((rows, cols), lambda i, me_ref: (i, 0))
        rider_args += [rd["parts"], rd["own"], rd["w"], rd["m"], rd["v"]]
        rider_specs += [pl.BlockSpec((N_DEV, rows, cols), lambda i, me_ref: (0, i, 0)),
                        pl.BlockSpec((1, rows, cols), lambda i, me_ref: (me_ref[0], i, 0)), blk, blk, blk]
        if "mask" in rd:
            rider_args.append(rd["mask"])
            rider_specs.append(whole((rows, cols)))
        rider_out_shapes += [jax.ShapeDtypeStruct(rd["w"].shape, F32)] * 4
        rider_out_specs += [blk] * 4
    outs = pl.pallas_call(
        body, name="in_bwd",
        out_shape=(jax.ShapeDtypeStruct((t_tok, D_MODEL), F32), jax.ShapeDtypeStruct((1, D_MODEL), F32),
                   *rider_out_shapes),
        grid_spec=pltpu.PrefetchScalarGridSpec(
            num_scalar_prefetch=1, grid=(steps,),
            in_specs=[row(n) for n in widths] + [whole((IN_PAD, D_MODEL)), row(D_MODEL), row(D_MODEL),
                                                 whole((1, D_MODEL))] + rider_specs
            + [pl.BlockSpec(memory_space=pl.ANY)] * len(deps),
            out_specs=(row(D_MODEL), whole((1, D_MODEL)), *rider_out_specs)),
        compiler_params=_params("arbitrary"))(me, du, dv, dz, dxbc, ddt, w_in, x, dx2, g1, *rider_args, *deps)
    return outs[0], outs[1], [tuple(outs[2 + 4 * r:6 + 4 * r]) for r in range(len(riders))]


def _pad_lanes(a, n):
    return jnp.pad(a, ((0, 0), (0, n - a.shape[1])))


def _local_step(x, target, seq, small, hooks, first_dep=None):
    t_tok = x.shape[0]
    tm = min(TOKEN_TILE, t_tok)
    avg, expand, expand_t, tril, triu = _const_mats()
    g1, g2, g3, g4 = (small[k].reshape(1, D_MODEL) for k in
                      ("norm_mix_pre", "norm_mix_post", "norm_ffn_pre", "norm_ffn_post"))
    tie = (lambda a: a) if first_dep is None else (lambda a: a + first_dep[0, 0])
    lnw = tie(small["gm_ln_w"]).reshape(1, GM_WIDTH)
    lnb = tie(small["gm_ln_b"]).reshape(1, GM_WIDTH)
    causal = jnp.tril(jnp.ones((CHUNK, CHUNK), F32))
    wm = tie(small["gm_w_s"]) * causal
    pair = lambda w: w.reshape(4, 2, CHUNK, CHUNK).transpose(0, 2, 1, 3).reshape(4, CHUNK, 2 * CHUNK).astype(BF16)
    wcat = pair(wm)
    wtcat = pair(jnp.swapaxes(wm, 1, 2))
    bias = jnp.repeat(tie(small["gm_b_s"]).T, HEAD_DIM, axis=1)
    cb = small["conv_b"].reshape(1, CONV_CH)
    dtb = _pad_lanes(tie(small["dt_bias"]).reshape(1, N_HEADS), CHUNK)
    alog = _pad_lanes(tie(small["a_log"]).reshape(1, N_HEADS), CHUNK)
    dskip_exp = jnp.repeat(tie(small["d_skip"]).reshape(1, N_HEADS), HEAD_DIM, axis=1)
    nw = small["ssm_norm_w"].reshape(1, SSM_WIDTH)

    h1 = _prenorm(x, g1, tm, hooks.get("prenorm_after", first_dep))
    w_in_t, conv_w = hooks["mixer_weights"](h1)
    tall = min(2 * tm, t_tok)
    u, v, z, xbc, dtr = _in_proj(h1, w_in_t, tall)
    mix_a = _gmlp_fwd(u, v, lnw, lnb, wcat, bias, avg)
    dep = hooks["gmlp_done"](mix_a) if "gmlp_done" in hooks else None
    mix_b, y_pre, states, pre = _ssd_fwd(z, xbc, dtr, conv_w, cb, dtb, alog, dskip_exp, nw, expand, tril, seq, dep)
    w_out, dep = hooks["mixers_done"](mix_b)
    o, x2, h3 = _out_proj(mix_a, mix_b, w_out, x, g2, g3, tall, dep)
    w_up, w_down = hooks["mlp_weights"](h3)
    tf = FF_TILE
    ra, dd, dy, dg4, loss = _mlp_fwd(h3, w_up, w_down, x2, target, g4, tm, tf)

    da, dx2, do, dg3, dg2 = _mlp_bwd(dd, w_down, ra, w_up, x2, dy, o, g3, g2, tm, tf)
    g_w_down = _wgrad(ra, dd, None, WGRAD_TILE, D_MODEL, t_tok, True, "wgrad_down")
    g_w_up = _wgrad(h3, da, N_DEV, D_MODEL, D_FF // N_DEV, t_tok, False, "wgrad_up")
    dep = hooks["mlp_grads"](g_w_down, g_w_up)
    dmix = _dmix(do, w_out, tall, dep)
    g_w_out = _wgrad_pieces(do, (mix_a, mix_b), WGRAD_TILE, "wgrad_out", dep)
    du, dv, dws, dbt, dlnw, dlnb = _gmlp_bwd(dmix, u, v, lnw, lnb, wcat, wtcat, bias, avg, expand_t)
    dep = hooks["gmlp_grads"](g_w_out, dws)
    dz, dxbc, ddt, dcw, dcb, ddtb, dalog, ddsk, dnw = _ssd_bwd(
        dmix, z, xbc, pre, dtr, y_pre, states, conv_w, cb, dtb, alog, dskip_exp, nw, expand, expand_t, tril, triu, seq,
        dep)
    g_w_in = _wgrad_in_chunked(h1, (du, dv, dz, dxbc, ddt), WGRAD_TILE, t_tok // 2, dep)
    dep = hooks["in_grads"](g_w_in, dcw[0:4])
    riders = hooks["arrived_updates"](dep) if "arrived_updates" in hooks else []
    me = hooks.get("me", jnp.zeros((1,), jnp.int32))
    grad_x, dg1, updates = _in_bwd(du, dv, dz, dxbc, ddt, w_in_t, x, dx2, g1, tm, me, riders, dep)

    grads = dict(
        updates=updates,
        w_in=g_w_in, w_out=g_w_out, w_up=g_w_up, w_down=g_w_down, conv_w=dcw[0:4],
        norm_mix_pre=dg1, norm_mix_post=dg2, norm_ffn_pre=dg3, norm_ffn_post=dg4, gm_ln_w=dlnw, gm_ln_b=dlnb,
        gm_w_s=dws, gm_b_s=dbt, conv_b=dcb, dt_bias=ddtb, a_log=dalog, d_skip=ddsk, ssm_norm_w=dnw)
    return loss[0, 0], grad_x, grads


_WEIGHTS = ("norm_mix_pre", "w_in", "gm_ln_w", "gm_ln_b", "gm_w_s", "gm_b_s", "conv_w", "conv_b", "dt_bias", "a_log",
            "d_skip", "ssm_norm_w", "w_out", "norm_mix_post", "norm_ffn_pre", "w_up", "w_down", "norm_ffn_post")
_SLAB_ROWS = (("norm_mix_pre", 1024), ("norm_mix_post", 1024), ("norm_ffn_pre", 1024), ("norm_ffn_post", 1024),
              ("conv_b", 1024), ("ssm_norm_w", 512), ("gm_ln_w", 512), ("gm_ln_b", 512), ("dt_bias", 8), ("a_log", 8),
              ("d_skip", 8))
_SLAB_LOSS_ROW = len(_SLAB_ROWS)
_SLAB_BS_ROW = 16
_SMALL_PARAMS = tuple(name for name, _ in _SLAB_ROWS) + ("gm_b_s",)
_LN_PARAMS = ("gm_ln_w", "gm_ln_b")


_SLAB_CONV_ROW = _SLAB_LOSS_ROW + 1


def _pack_slab(g, loss_part):
    rows = [_pad_lanes(g[name], D_MODEL) for name, _ in _SLAB_ROWS]
    rows.append(jnp.broadcast_to(loss_part, (1, D_MODEL)))
    rows.append(g["conv_w"])
    assert sum(r.shape[0] for r in rows) == _SLAB_BS_ROW
    rows.append(_pad_lanes(g["gm_b_s"].T[0:N_HEADS], D_MODEL))
    return jnp.concatenate(rows, axis=0)


def _adamw_slab(parts, me, w, m, v):
    names = _SMALL_PARAMS + ("conv_w",)
    shapes = [w[k].shape for k in names]
    unfold = np.zeros((GM_WIDTH, HEAD_DIM), np.float32)
    for h in range(N_HEADS):
        unfold[h * HEAD_DIM:(h + 1) * HEAD_DIM, :] = np.eye(HEAD_DIM)
    unfold = jnp.asarray(unfold, dtype=BF16)
    n = len(names)
    shard = CONV_CH // N_DEV

    def body(me_ref, p_ref, unfold_ref, *refs):
        w_refs, m_refs, v_refs = refs[:n], refs[n:2 * n], refs[2 * n:3 * n]
        outs = refs[3 * n:]
        g_all = p_ref[0]
        for j in range(1, N_DEV):
            g_all = g_all + p_ref[j]
        lane = lax.broadcasted_iota(jnp.int32, (N_HEADS, GM_WIDTH), 1)
        head = lax.broadcasted_iota(jnp.int32, (N_HEADS, GM_WIDTH), 0)
        own_lanes = jnp.logical_and(lane >= head * HEAD_DIM, lane < (head + 1) * HEAD_DIM)
        mine = pl.ds(pl.multiple_of(me_ref[0] * shard, shard), shard)
        for i, name in enumerate(names):
            if name == "gm_b_s":
                g = g_all[_SLAB_BS_ROW:_SLAB_BS_ROW + N_HEADS, 0:CHUNK]
            elif name == "conv_w":
                g = p_ref[0, _SLAB_CONV_ROW:_SLAB_CONV_ROW + 4, mine]
                for j in range(1, N_DEV):
                    g = g + p_ref[j, _SLAB_CONV_ROW:_SLAB_CONV_ROW + 4, mine]
            else:
                row = [r for r, (k, _) in enumerate(_SLAB_ROWS) if k == name][0]
                g = g_all[row:row + 1, 0:dict(_SLAB_ROWS)[name]]
                if name in _LN_PARAMS:
                    g = _split_dot(jnp.where(own_lanes, g, 0.0), unfold_ref[...], 3)
            d, mn, vn = _adamw_math(w_refs[i][...], g, m_refs[i][...], v_refs[i][...])
            for o_ref, val in zip(outs[4 * i:4 * i + 4], (g, d, mn, vn)):
                o_ref[...] = val
        outs[-1][...] = g_all[_SLAB_LOSS_ROW:_SLAB_LOSS_ROW + 1, 0:128]

    def whole(shape):
        nd = len(shape)
        return pl.BlockSpec(shape, lambda i, me_ref: (0,) * nd)

    ins = [parts, unfold] + [d[k] for d in (w, m, v) for k in names]
    out_shape = tuple(jax.ShapeDtypeStruct(s, F32) for s in shapes for _ in range(4)) + (
        jax.ShapeDtypeStruct((1, 128), F32),)
    outs = pl.pallas_call(
        body, name="adamw_small", out_shape=out_shape,
        grid_spec=pltpu.PrefetchScalarGridSpec(
            num_scalar_prefetch=1, grid=(1,), in_specs=[whole(a.shape) for a in ins],
            out_specs=tuple(whole(s.shape) for s in out_shape)),
        compiler_params=_params("arbitrary"))(me, *ins)
    return {k: tuple(outs[4 * i:4 * i + 4]) for i, k in enumerate(names)}, outs[-1][0, 0]


def kernel(x, norm_mix_pre, w_in, gm_ln_w, gm_ln_b, gm_w_s, gm_b_s, conv_w, conv_b, dt_bias, a_log, d_skip, ssm_norm_w, w_out, norm_mix_post, norm_ffn_pre, w_up, w_down, norm_ffn_post, loss_target, m_norm_mix_pre, m_w_in, m_gm_ln_w, m_gm_ln_b, m_gm_w_s, m_gm_b_s, m_conv_w, m_conv_b, m_dt_bias, m_a_log, m_d_skip, m_ssm_norm_w, m_w_out, m_norm_mix_post, m_norm_ffn_pre, m_w_up, m_w_down, m_norm_ffn_post, v_norm_mix_pre, v_w_in, v_gm_ln_w, v_gm_ln_b, v_gm_w_s, v_gm_b_s, v_conv_w, v_conv_b, v_dt_bias, v_a_log, v_d_skip, v_ssm_norm_w, v_w_out, v_norm_mix_post, v_norm_ffn_pre, v_w_up, v_w_down, v_norm_ffn_post):
    w = dict(norm_mix_pre=norm_mix_pre, w_in=w_in, gm_ln_w=gm_ln_w, gm_ln_b=gm_ln_b, gm_w_s=gm_w_s, gm_b_s=gm_b_s, conv_w=conv_w, conv_b=conv_b, dt_bias=dt_bias, a_log=a_log, d_skip=d_skip, ssm_norm_w=ssm_norm_w, w_out=w_out, norm_mix_post=norm_mix_post, norm_ffn_pre=norm_ffn_pre, w_up=w_up, w_down=w_down, norm_ffn_post=norm_ffn_post)
    m = dict(norm_mix_pre=m_norm_mix_pre, w_in=m_w_in, gm_ln_w=m_gm_ln_w, gm_ln_b=m_gm_ln_b, gm_w_s=m_gm_w_s, gm_b_s=m_gm_b_s, conv_w=m_conv_w, conv_b=m_conv_b, dt_bias=m_dt_bias, a_log=m_a_log, d_skip=m_d_skip, ssm_norm_w=m_ssm_norm_w, w_out=m_w_out, norm_mix_post=m_norm_mix_post, norm_ffn_pre=m_norm_ffn_pre, w_up=m_w_up, w_down=m_w_down, norm_ffn_post=m_norm_ffn_post)
    v = dict(norm_mix_pre=v_norm_mix_pre, w_in=v_w_in, gm_ln_w=v_gm_ln_w, gm_ln_b=v_gm_ln_b, gm_w_s=v_gm_w_s, gm_b_s=v_gm_b_s, conv_w=v_conv_w, conv_b=v_conv_b, dt_bias=v_dt_bias, a_log=v_a_log, d_skip=v_d_skip, ssm_norm_w=v_ssm_norm_w, w_out=v_w_out, norm_mix_post=v_norm_mix_post, norm_ffn_pre=v_norm_ffn_pre, w_up=v_w_up, w_down=v_w_down, norm_ffn_post=v_norm_ffn_post)
    n_batch, seq, _ = x.shape
    shard_in = IN_COLS // N_DEV

    me = (4 * lax.axis_index("x") + 2 * lax.axis_index("y") + lax.axis_index("c")).astype(jnp.int32).reshape(1)

    def in_slot(own):
        return lax.dynamic_update_slice(lax.empty((N_DEV,) + own.shape, own.dtype), own[None],
                                        (me[0],) + (0,) * own.ndim)

    lying = lambda t: jnp.transpose(t, (2, 0, 1))
    first = [_cast_to_slot(lying(w_in), me, shard_in, "cast_w_in"), in_slot(conv_w[0])]
    ici_1, tok_ici_1 = _exchange_start(first, [True] * 2, _SAME_CORE_PEERS, "gather_mix_ici_start")
    cast_out = _cast_to_slot(w_out[0], me, 128, "cast_w_out", dep=tok_ici_1)
    cast_up = _cast_to_slot(w_up[0], me, 1024, "cast_w_up", cols=True, dep=cast_out)
    second = [cast_out, cast_up, _cast_to_slot(w_down[0], me, 512, "cast_w_down", dep=cast_up)]
    gathering = {}

    def mixer_weights(after):
        bufs = [buf for buf, _ in _exchange_wait(ici_1, after, "gather_mix_ici_wait")]
        d2d_1, tok_d2d_1 = _exchange_start(bufs, [True] * 2, _SIBLING_FORWARD, "gather_mix_d2d_start")
        gathering["late_ici"], tok_ici_2 = _exchange_start(
            second, [True] * 3, _SAME_CORE_PEERS, "gather_late_ici_start", dep=tok_d2d_1)
        (_, ag_in), (_, ag_conv) = _exchange_wait(d2d_1, tok_ici_2, "gather_mix_d2d_wait")
        w_in_t = jnp.pad(ag_in.reshape(IN_COLS, D_MODEL), ((0, IN_PAD - IN_COLS), (0, 0)))
        return w_in_t, ag_conv.transpose(1, 0, 2).reshape(4, CONV_CH)

    def gmlp_done(after):
        ((buf, _),) = _exchange_wait(gathering["late_ici"], after, "gather_out_ici_wait", only=(0,))
        gathering["out"], tok = _exchange_start([buf], [True], _SIBLING_FORWARD, "gather_out_d2d_start")
        return tok

    def mixers_done(after):
        bufs = [buf for buf, _ in _exchange_wait(gathering["late_ici"], after, "gather_mlp_ici_wait", only=(1, 2))]
        gathering["mlp"], tok = _exchange_start(bufs, [True] * 2, _SIBLING_FORWARD, "gather_mlp_d2d_start")
        ((_, ag_out),) = _exchange_wait(gathering["out"], tok, "gather_out_d2d_wait")
        return ag_out.reshape(D_MODEL, D_MODEL), tok

    def mlp_weights(after):
        (_, ag_up), (_, ag_down) = _exchange_wait(gathering["mlp"], after, "gather_mlp_d2d_wait")
        return ag_up, ag_down.reshape(D_FF, D_MODEL)

    sent = {}

    def mlp_grads(g_w_down, g_w_up):
        sent["mlp"], tok = _exchange_start(
            [g_w_down.reshape(N_DEV, D_FF // N_DEV, D_MODEL), g_w_up], [False, False], _ALL_PEERS, "grads_mlp_start")
        return tok

    def gmlp_grads(g_w_out, g_w_s):
        sent["gmlp"], tok = _exchange_start(
            [g_w_out.reshape(N_DEV, D_MODEL // N_DEV, D_MODEL), in_slot(g_w_s.astype(BF16))], [False, True], _ALL_PEERS,
            "grads_gmlp_start")
        return tok

    def in_grads(g_w_in_t, g_conv_w):
        sent["in"], tok = _exchange_start([g_w_in_t], [False], _ALL_PEERS, "grads_in_start")
        return tok

    def arrived_updates(after):
        (own_down, p_down), (own_up, p_up) = _exchange_wait(sent["mlp"], after, "grads_mlp_wait")
        (own_out, p_out), (_, p_ws) = _exchange_wait(sent["gmlp"], own_up, "grads_gmlp_wait")
        rows = lambda t: t.reshape(t.shape[:-3] + (N_HEADS * CHUNK, CHUNK))
        return [dict(parts=p_up, own=own_up, w=w_up[0], m=m_w_up[0], v=v_w_up[0]),
                dict(parts=p_down, own=own_down, w=w_down[0], m=m_w_down[0], v=v_w_down[0]),
                dict(parts=p_out, own=own_out, w=w_out[0], m=m_w_out[0], v=v_w_out[0]),
                dict(parts=rows(p_ws), own=rows(p_ws), w=rows(gm_w_s[0]), m=rows(m_gm_w_s[0]), v=rows(v_gm_w_s[0]),
                     mask=jnp.tril(jnp.ones((CHUNK, CHUNK), F32)))]

    small = {k: w[k][0] for k in _SMALL_PARAMS + ("gm_w_s",)}
    loss_part, grad_x, g = _local_step(
        x.reshape(n_batch * seq, D_MODEL), loss_target.reshape(n_batch * seq, D_MODEL), seq, small,
        dict(mixer_weights=mixer_weights, gmlp_done=gmlp_done, mixers_done=mixers_done, mlp_weights=mlp_weights,
             mlp_grads=mlp_grads, gmlp_grads=gmlp_grads, in_grads=in_grads, arrived_updates=arrived_updates, me=me,
             prenorm_after=second[2]), first_dep=tok_ici_1)

    sent_rows, tok_rows = _exchange_start([in_slot(_pack_slab(g, loss_part))], [True], _ALL_PEERS, "grads_rows_start")
    res = dict(zip(("w_up", "w_down", "w_out", "gm_w_s"), g["updates"]))
    ((own_in, p_in),) = _exchange_wait(sent["in"], tok_rows, "grads_in_wait")
    upd_in = _adamw_reduce(p_in, own_in, me, lying(w_in), lying(m_w_in), lying(v_w_in), "adamw_w_in")
    res["w_in"] = tuple(jnp.transpose(t, (1, 2, 0)) for t in upd_in)
    ((_, p_rows),) = _exchange_wait(sent_rows, upd_in[1], "grads_rows_wait")
    flat = lambda t: t[0] if t.ndim == 3 else t
    small_res, loss = _adamw_slab(
        p_rows, me, *({k: flat(d[k]) for k in _SMALL_PARAMS + ("conv_w",)} for d in (w, m, v)))
    res.update(small_res)
    res = {k: tuple(r.reshape(w[k].shape) for r in res[k]) for k in _WEIGHTS}

    outs = [loss, grad_x.reshape(x.shape)]
    for part in range(4):
        outs.extend(res[k][part] for k in _WEIGHTS)
    return tuple(outs)
```

```python
import functools

import jax
import jax.numpy as jnp
import numpy as np
from jax import lax
from jax.experimental import pallas as pl
from jax.experimental.pallas import tpu as pltpu

F32 = jnp.float32
BF16 = jnp.bfloat16

D_MODEL = 1024
GM_WIDTH = 512
SSM_WIDTH = 512
CONV_CH = 1024
N_HEADS = 8
HEAD_DIM = 64
N_STATE = 128
CHUNK = 128
D_FF = 4096
IN_COLS = 2568
IN_PAD = 2688
N_DEV = 8
EPS = 1e-6
ADAM_LR, ADAM_B1, ADAM_B2, ADAM_EPS, ADAM_WD, ADAM_STEP = 0.001, 0.9, 0.999, 1e-08, 0.01, 10
VMEM_LIMIT_BYTES = 56 * 1024 * 1024
TOKEN_TILE = 512
FF_TILE = 2048
WGRAD_TILE = 512
STACK_TILE = 256
_NT = (((1,), (1,)), ((), ()))
_TN = (((0,), (0,)), ((), ()))


def _params(*sem):
    return pltpu.CompilerParams(dimension_semantics=sem or None, vmem_limit_bytes=VMEM_LIMIT_BYTES)


def _dot(a, b, dims=None):
    if dims is None:
        return jnp.dot(a, b, preferred_element_type=F32)
    return lax.dot_general(a, b, dims, preferred_element_type=F32)


def _split_terms(x, terms):
    out, rem = [], x
    for i in range(terms):
        hi = rem.astype(BF16)
        out.append(hi)
        if i + 1 < terms:
            rem = rem - hi.astype(F32)
    return out


def _split_dot(x, m, terms):
    acc = None
    for hi in _split_terms(x, terms):
        part = _dot(hi, m)
        acc = part if acc is None else acc + part
    return acc


def _split_dot_left(m, x, terms):
    acc = None
    for hi in _split_terms(x, terms):
        part = _dot(m, hi)
        acc = part if acc is None else acc + part
    return acc


def _gelu_and_grad(x):
    c = 0.7978845608028654
    inner = c * (x + 0.044715 * x * x * x)
    t = jnp.tanh(inner)
    g = 0.5 * x * (1.0 + t)
    dg = 0.5 * (1.0 + t) + 0.5 * x * (1.0 - t * t) * c * (1.0 + 3.0 * 0.044715 * x * x)
    return g, dg


def _softplus(x):
    return jnp.maximum(x, 0.0) + jnp.log(1.0 + jnp.exp(-jnp.abs(x)))


def _rsum(x):
    return jnp.sum(x, axis=0, keepdims=True)


def _acc_rows(ref, part, first):
    val = jnp.broadcast_to(part, ref.shape)

    @pl.when(first)
    def _():
        ref[...] = val

    @pl.when(jnp.logical_not(first))
    def _():
        ref[...] += val


def _rms_bwd(n, g, dout):
    r = lax.rsqrt(jnp.mean(n * n, axis=-1, keepdims=True) + EPS)
    nh = n * r
    dg = dout * g
    dn = r * (dg - nh * jnp.mean(dg * nh, axis=-1, keepdims=True))
    return dn, _rsum(dout * nh)


def _const_mats():
    avg = np.kron(np.eye(4), np.full((HEAD_DIM, HEAD_DIM), 1.0 / HEAD_DIM))
    expand = np.zeros((CHUNK, SSM_WIDTH), np.float32)
    for h in range(N_HEADS):
        expand[h, h * HEAD_DIM:(h + 1) * HEAD_DIM] = 1.0
    tril = np.tril(np.ones((CHUNK, CHUNK), np.float32))
    as_bf16 = lambda a: jnp.asarray(a, dtype=BF16)
    return as_bf16(avg), as_bf16(expand), as_bf16(expand.T), as_bf16(tril), as_bf16(tril.T)


def _full(shape):
    nd = len(shape)
    return pl.BlockSpec(shape, lambda *_: (0,) * nd)


_HBM = pl.BlockSpec(memory_space=pltpu.HBM)
_SEM = pl.BlockSpec(memory_space=pltpu.SEMAPHORE)
_ALL_PEERS = tuple((k, 0) for k in range(1, N_DEV))
_SAME_CORE_PEERS = ((2, 0), (4, 0), (6, 0))
_SIBLING_FORWARD = ((1, 0), (1, 2), (1, 4), (1, 6))


def _flip(j, k):
    for bit in (4, 2, 1):
        if k & bit:
            j = j + bit - 2 * (j & bit)
    return j


def _copies(src, land, send_sems, recv_sems, hops, slots=None):
    x, y, c = lax.axis_index("x"), lax.axis_index("y"), lax.axis_index("c")
    me = 4 * x + 2 * y + c
    slots = range(len(src)) if slots is None else slots
    out = []
    for t in range(len(src)):
        for i, (k, b) in enumerate(hops):
            pos = (1 - x if k & 4 else x, 1 - y if k & 2 else y, 1 - c if k & 1 else c)
            peer = _flip(me, k)
            sem = slots[t] * len(hops) + i
            mk = functools.partial(pltpu.make_async_remote_copy, send_sem=send_sems.at[sem], recv_sem=recv_sems.at[sem],
                                   device_id=pos, device_id_type=pl.DeviceIdType.MESH)
            if land[t] is None and src[t].shape[0] != N_DEV:
                width = src[t].shape[1] // N_DEV
                slab = lambda j: src[t].at[:, pl.ds(pl.multiple_of(j * width, 128), width)]
                mine = functools.partial(mk, src_ref=slab(_flip(me, b)), dst_ref=slab(_flip(me, b)))
                theirs = functools.partial(mk, src_ref=slab(_flip(peer, b)), dst_ref=slab(_flip(peer, b)))
            elif land[t] is None:
                mine = functools.partial(mk, src_ref=src[t].at[_flip(me, b)], dst_ref=src[t].at[_flip(me, b)])
                theirs = functools.partial(mk, src_ref=src[t].at[_flip(peer, b)], dst_ref=src[t].at[_flip(peer, b)])
            else:
                assert b == 0
                mine = functools.partial(mk, src_ref=src[t].at[peer], dst_ref=land[t].at[me])
                theirs = functools.partial(mk, src_ref=src[t].at[peer], dst_ref=land[t].at[peer])
            out.append((mine, theirs))
    return out


def _exchange_start(srcs, inplace, peers, name, dep=None):
    n = len(srcs)
    lands = [None if ip else pltpu.with_memory_space_constraint(lax.empty(s.shape, s.dtype), pltpu.HBM)
             for s, ip in zip(srcs, inplace)]
    real_lands = [l for l in lands if l is not None]
    n_l = len(real_lands)
    deps = [] if dep is None else [dep]

    def body(*refs):
        src = refs[:n]
        land_refs = list(refs[n:n + n_l])
        send_sems, recv_sems = refs[n + n_l + len(deps)], refs[n + n_l + len(deps) + 1]
        token = refs[-1]
        land = [None if ip else land_refs.pop(0) for ip in inplace]
        for mine, _ in _copies(src, land, send_sems, recv_sems, peers):
            mine().start()
        token[...] = jnp.zeros_like(token)

    sem_t = pltpu.SemaphoreType.DMA((n * len(peers),))
    outs = pl.pallas_call(
        body, name=name,
        out_shape=(sem_t, sem_t) + tuple(pltpu.HBM(a.shape, a.dtype) for a in list(srcs) + real_lands)
        + (jax.ShapeDtypeStruct((8, 128), F32),),
        in_specs=[_HBM] * (n + n_l) + [pl.BlockSpec(memory_space=pl.ANY)] * len(deps),
        out_specs=(_SEM, _SEM) + (_HBM,) * (n + n_l) + (pl.BlockSpec(memory_space=pltpu.VMEM),),
        input_output_aliases={i: 2 + i for i in range(n + n_l)},
        compiler_params=pltpu.CompilerParams(has_side_effects=pltpu.SideEffectType.DATAFLOW_SIDE_EFFECTING),
    )(*[pltpu.with_memory_space_constraint(s, pltpu.HBM) for s in srcs], *real_lands, *deps)
    handle = dict(send=outs[0], recv=outs[1], srcs=outs[2:2 + n], lands=outs[2 + n:2 + n + n_l], inplace=inplace,
                  peers=peers)
    return handle, outs[-1]


def _exchange_wait(handle, after, name, only=None):
    srcs, lands, inplace, peers = handle["srcs"], handle["lands"], handle["inplace"], handle["peers"]
    slots = None
    if only is not None:
        assert all(inplace)
        slots, srcs, inplace = list(only), [srcs[t] for t in only], [True] * len(only)
    n, n_l = len(srcs), len(lands)

    def body(*refs):
        src = refs[:n]
        land_refs = list(refs[n:n + n_l])
        send_sems, recv_sems = refs[n + n_l], refs[n + n_l + 1]
        land = [None if ip else land_refs.pop(0) for ip in inplace]
        for mine, theirs in _copies(src, land, send_sems, recv_sems, peers, slots):
            mine().wait_send()
            theirs().wait_recv()

    outs = pl.pallas_call(
        body, name=name, out_shape=tuple(pltpu.HBM(a.shape, a.dtype) for a in list(srcs) + list(lands)),
        in_specs=[_HBM] * (n + n_l) + [_SEM, _SEM, pl.BlockSpec(memory_space=pl.ANY)],
        out_specs=(_HBM,) * (n + n_l), input_output_aliases={i: i for i in range(n + n_l)},
        compiler_params=pltpu.CompilerParams(has_side_effects=pltpu.SideEffectType.DATAFLOW_SIDE_EFFECTING),
    )(*srcs, *lands, handle["send"], handle["recv"], after)
    res, land_out = [], list(outs[n:])
    for t in range(n):
        res.append((outs[t], outs[t] if inplace[t] else land_out.pop(0)))
    return res


def _cast_to_slot(w, me, rows, name, cols=False, dep=None):
    r, cdim = w.shape[0], w.shape[-1]
    deps = [] if dep is None else [dep]

    def body(me_ref, w_ref, *rest):
        o_ref = rest[-1]
        if cols:
            o_ref[...] = w_ref[...].astype(BF16)
        else:
            o_ref[0] = w_ref[...].reshape(rows, cdim).astype(BF16)

    if cols:
        out_shape = jax.ShapeDtypeStruct((r, N_DEV * cdim), BF16)
        out_spec = pl.BlockSpec((rows, cdim), lambda i, me_ref: (i, me_ref[0]))
    else:
        out_shape = jax.ShapeDtypeStruct((N_DEV, r, cdim), BF16)
        out_spec = pl.BlockSpec((1, rows, cdim), lambda i, me_ref: (me_ref[0], i, 0))
    return pl.pallas_call(
        body, name=name, out_shape=out_shape,
        grid_spec=pltpu.PrefetchScalarGridSpec(
            num_scalar_prefetch=1, grid=(r // rows,),
            in_specs=[pl.BlockSpec((rows, cdim), lambda i, me_ref: (i, 0)) if w.ndim == 2 else
                      pl.BlockSpec((rows, 1, cdim), lambda i, me_ref: (i, 0, 0))]
            + [pl.BlockSpec(memory_space=pl.ANY)] * len(deps), out_specs=out_spec),
        compiler_params=_params("parallel"))(me, w, *deps)


def _stack_shards(blocks, rows, bn, name):
    n, r, cdim = blocks.shape

    def body(b_ref, o_ref, acc_ref):
        acc_ref[n * r:, :] = jnp.zeros((rows - n * r, bn), F32)
        for j in range(n):
            acc_ref[r * j:r * (j + 1), :] = b_ref[j].astype(F32)
        o_ref[...] = acc_ref[...].astype(BF16)

    return pl.pallas_call(
        body, name=name, grid=(cdim // bn,), out_shape=jax.ShapeDtypeStruct((rows, cdim), BF16),
        in_specs=[pl.BlockSpec((n, r, bn), lambda i: (0, 0, i))], out_specs=pl.BlockSpec((rows, bn), lambda i: (0, i)),
        scratch_shapes=[pltpu.VMEM((rows, bn), F32)], compiler_params=_params("parallel"))(blocks)


def _adamw_math(w, g, m, v):
    m = ADAM_B1 * m + (1.0 - ADAM_B1) * g
    v = ADAM_B2 * v + (1.0 - ADAM_B2) * (g * g)
    m_hat = m / (1.0 - ADAM_B1 ** ADAM_STEP)
    v_hat = v / (1.0 - ADAM_B2 ** ADAM_STEP)
    delta = -ADAM_LR * (m_hat / (jnp.sqrt(v_hat) + ADAM_EPS) + ADAM_WD * w)
    return delta, m, v


def _sum_parts(me, p_ref, own):
    g = None
    for j in range(N_DEV):
        term = (p_ref[j] if own is None else jnp.where(me == j, own, p_ref[j])).astype(F32)
        g = term if g is None else g + term
    return g


def _adamw_reduce(parts, own, me, w, m, v, name):
    r, _, cdim = w.shape

    def body(me_ref, p_ref, own_ref, w_ref, m_ref, v_ref, g_out, d_out, m_out, v_out):
        g = _sum_parts(me_ref[0], p_ref, own_ref[0]).reshape(r, 1, cdim)
        d, mn, vn = _adamw_math(w_ref[...], g, m_ref[...], v_ref[...])
        g_out[...] = g
        d_out[...] = d
        m_out[...] = mn
        v_out[...] = vn

    blk = pl.BlockSpec((r, 1, cdim), lambda i, me_ref: (0, 0, 0))
    return pl.pallas_call(
        body, name=name, out_shape=(jax.ShapeDtypeStruct(w.shape, F32),) * 4,
        grid_spec=pltpu.PrefetchScalarGridSpec(
            num_scalar_prefetch=1, grid=(1,),
            in_specs=[pl.BlockSpec((N_DEV, r, cdim), lambda i, me_ref: (0, 0, 0)),
                      pl.BlockSpec((1, r, cdim), lambda i, me_ref: (me_ref[0], 0, 0)), blk, blk, blk],
            out_specs=(blk,) * 4),
        compiler_params=_params("arbitrary"))(me, parts, own, w, m, v)


_IN_SPLITS = ((0, 512), (512, 1024), (1024, 1536), (1536, 2560), (2560, IN_PAD))


def _prenorm(x, g1, tm, dep=None):
    t_tok = x.shape[0]
    deps = [] if dep is None else [dep]

    def body(x_ref, g_ref, *rest):
        xv = x_ref[...]
        r = lax.rsqrt(jnp.mean(xv * xv, axis=-1, keepdims=True) + EPS)
        rest[-1][...] = (xv * r * g_ref[...]).astype(BF16)

    row = pl.BlockSpec((tm, D_MODEL), lambda i: (i, 0))
    return pl.pallas_call(
        body, name="prenorm", grid=(t_tok // tm,), out_shape=jax.ShapeDtypeStruct((t_tok, D_MODEL), BF16),
        in_specs=[row, _full((1, D_MODEL))] + [pl.BlockSpec(memory_space=pl.ANY)] * len(deps), out_specs=row,
        compiler_params=_params("parallel"))(x, g1, *deps)


def _in_proj(h1, w_in, tm):
    t_tok = h1.shape[0]

    def body(h_ref, w_ref, *outs):
        h = h_ref[...]
        for (a, b), o_ref in zip(_IN_SPLITS, outs):
            o_ref[...] = _dot(h, w_ref[a:b, :], _NT).astype(o_ref.dtype)

    row = lambda n: pl.BlockSpec((tm, n), lambda i: (i, 0))
    widths = [b - a for a, b in _IN_SPLITS]
    dtypes = (BF16, BF16, BF16, F32, F32)
    return pl.pallas_call(
        body, name="in_proj", grid=(t_tok // tm,),
        out_shape=tuple(jax.ShapeDtypeStruct((t_tok, n), dt) for n, dt in zip(widths, dtypes)),
        in_specs=[row(D_MODEL), _full((IN_PAD, D_MODEL))], out_specs=tuple(row(n) for n in widths),
        compiler_params=_params("parallel"))(h1, w_in)


def _lane_masks():
    lane = lax.broadcasted_iota(jnp.int32, (1, 2 * HEAD_DIM), 1)
    left = (lane < HEAD_DIM).astype(F32)
    return left, 1.0 - left


def _stack_pair(v, m_l, m_r):
    return jnp.concatenate([v * m_l, v * m_r], axis=0).astype(BF16)


def _head_mean(x, avg):
    n = avg.shape[0]
    return jnp.concatenate([_split_dot(x[:, n * i:n * (i + 1)], avg, 2) for i in range(x.shape[1] // n)], axis=1)


def _gmlp_common(u, v, lnw, lnb, avg, wcat_ref, bias, m_l, m_r):
    ug, dug = _gelu_and_grad(u)
    vg, dvg = _gelu_and_grad(v)
    mu = _head_mean(vg, avg)
    vc = vg - mu
    var = _head_mean(vc * vc, avg)
    rstd = lax.rsqrt(var + EPS)
    vhat = vc * rstd
    vn = vhat * lnw + lnb
    rows = []
    for r in range(u.shape[0] // CHUNK):
        cols = []
        for j in range(N_HEADS // 2):
            pair = vn[CHUNK * r:CHUNK * (r + 1), 128 * j:128 * (j + 1)]
            cols.append(_dot(wcat_ref[j], _stack_pair(pair, m_l, m_r)))
        rows.append(jnp.concatenate(cols, axis=1) + bias)
    mixed = jnp.concatenate(rows, axis=0)
    return ug, dug, dvg, rstd, vhat, vn, mixed


_GMLP_ROWS = 4 * CHUNK


def _gmlp_fwd(u, v, lnw, lnb, wcat, bias, avg):
    t_tok = u.shape[0]
    tm = min(_GMLP_ROWS, t_tok)

    def body(u_ref, v_ref, lnw_ref, lnb_ref, wcat_ref, bias_ref, avg_ref, o_ref):
        m_l, m_r = _lane_masks()
        ug, _, _, _, _, _, mixed = _gmlp_common(
            u_ref[...].astype(F32), v_ref[...].astype(F32), lnw_ref[...], lnb_ref[...], avg_ref[...], wcat_ref,
            bias_ref[...], m_l, m_r)
        o_ref[...] = (ug * mixed).astype(BF16)

    row = pl.BlockSpec((tm, GM_WIDTH), lambda i: (i, 0))
    return pl.pallas_call(
        body, name="gmlp_fwd", grid=(t_tok // tm,), out_shape=jax.ShapeDtypeStruct((t_tok, GM_WIDTH), BF16),
        in_specs=[row, row, _full((1, GM_WIDTH)), _full((1, GM_WIDTH)), _full(wcat.shape), _full(bias.shape),
                  _full(avg.shape)],
        out_specs=row, compiler_params=_params("parallel"))(u, v, lnw, lnb, wcat, bias, avg)


def _shift_rows(x, edge, j, down):
    groups, cols = x.shape[0] // 8, x.shape[1]
    amount = j if down else 8 - j
    rot = pltpu.roll(x.reshape(groups, 8, cols), amount, axis=1)
    edge_rot = pltpu.roll(edge, amount, axis=0)[None]
    sub = lax.broadcasted_iota(jnp.int32, (1, 8, 1), 1)
    if down:
        out = jnp.where(sub < j, jnp.concatenate([edge_rot, rot[:-1]], axis=0), rot)
    else:
        out = jnp.where(sub < 8 - j, rot, jnp.concatenate([rot[1:], edge_rot], axis=0))
    return out.reshape(x.shape)


def _conv_pre(xbc, tail, cw_ref, cb):
    taps = [_shift_rows(xbc, tail, 3 - k, True) for k in range(3)] + [xbc]
    return cb + cw_ref[0:1, :] * taps[0] + cw_ref[1:2, :] * taps[1] + cw_ref[2:3, :] * taps[2] + cw_ref[3:4, :] * taps[3]


def _ssd_common(pre, dtr, dtb, alog, expand, tril):
    q = CHUNK
    sg = jax.nn.sigmoid(pre)
    act = pre * sg
    lane = lax.broadcasted_iota(jnp.int32, (1, CHUNK), 1)
    a_row = jnp.where(lane < N_HEADS, -jnp.exp(alog), 0.0)
    dtp = dtr + dtb
    dt = _softplus(dtp)
    a_cs = _split_dot_left(tril, dt * a_row, 3)
    a_cs_t = a_cs.T
    dt_exp = _split_dot(dt, expand, 3)
    a_exp = _split_dot(a_cs, expand, 3)
    a_end = a_exp[q - 1:q, :]
    li = lax.broadcasted_iota(jnp.int32, (q, q), 0)
    si = lax.broadcasted_iota(jnp.int32, (q, q), 1)
    causal = si <= li
    decay = []
    for h in range(N_HEADS):
        seg = a_cs[:, h:h + 1] - a_cs_t[h:h + 1, :]
        decay.append(jnp.where(causal, jnp.exp(jnp.minimum(seg, 0.0)), 0.0))
    return dict(pre=pre, sg=sg, act=act, a_row=a_row, dtp=dtp, dt=dt, dt_exp=dt_exp, a_exp=a_exp,
                e=jnp.exp(a_exp), w_end=jnp.exp(a_end - a_exp), cd=jnp.exp(a_end), decay=decay)


def _ssd_specs(t_tok, seq, reverse):
    nb, nc = t_tok // seq, seq // CHUNK

    def chunk(c):
        return nc - 1 - c if reverse else c

    def row(n, col=0):
        return pl.BlockSpec((nb, CHUNK, n), lambda c: (0, chunk(c), col))

    tail = pl.BlockSpec((nb, 8, CONV_CH), lambda c: (0, jnp.maximum(chunk(c) * (CHUNK // 8) - 1, 0), 0))
    states = pl.BlockSpec((nb, 1, N_STATE, SSM_WIDTH), lambda c: (0, chunk(c), 0, 0))
    fold = lambda a: a.reshape(nb, seq, a.shape[-1])
    unfold = lambda a: a.reshape(t_tok, a.shape[-1])
    return nb, nc, row, tail, states, fold, unfold


def _ssd_fwd(z, xbc, dtr, cw, cb, dtb, alog, dskip_exp, nw, expand, tril, seq, dep=None):
    t_tok = z.shape[0]
    nb, nc, row, tail, states_spec, fold, unfold = _ssd_specs(t_tok, seq, False)

    def body(z_ref, xbc_ref, tail_ref, dtr_ref, cw_ref, cb_ref, dtb_ref, alog_ref, dsk_ref, nw_ref, exp_ref,
             tril_ref, o_ref, y_ref, st_ref, pre_ref, state_ref):
        c = pl.program_id(0)

        @pl.when(c == 0)
        def _():
            state_ref[...] = jnp.zeros_like(state_ref)

        m_l, m_r = _lane_masks()
        for s in range(nb):
            pre = _conv_pre(xbc_ref[s], jnp.where(c == 0, 0.0, tail_ref[s]), cw_ref, cb_ref[...])
            pre_ref[s] = pre
            f = _ssd_common(pre, dtr_ref[s], dtb_ref[...], alog_ref[...], exp_ref[...], tril_ref[...])
            act = f["act"]
            xs = act[:, :SSM_WIDTH]
            xdt = xs * f["dt_exp"]
            xw = xdt * f["w_end"]
            state = state_ref[s]
            st_ref[s, 0] = state
            ydiag, yoff, snew = [], [], []
            for g in range(2):
                bg = act[:, 512 + 128 * g:640 + 128 * g].astype(BF16)
                cg = act[:, 768 + 128 * g:896 + 128 * g].astype(BF16)
                cb_mat = _dot(cg, bg, _NT)
                for pr in range(2):
                    h0 = 4 * g + 2 * pr
                    gcat = jnp.concatenate(
                        [(cb_mat * f["decay"][h0]).astype(BF16), (cb_mat * f["decay"][h0 + 1]).astype(BF16)], axis=1)
                    ydiag.append(_dot(gcat, _stack_pair(xdt[:, 64 * h0:64 * h0 + 128], m_l, m_r)))
                yoff.append(_dot(cg, state[:, 256 * g:256 * (g + 1)].astype(BF16)))
                snew.append(_dot(bg, xw[:, 256 * g:256 * (g + 1)].astype(BF16), _TN))
            y = jnp.concatenate(ydiag, axis=1) + f["e"] * jnp.concatenate(yoff, axis=1) + dsk_ref[...] * xs
            state_ref[s] = state * f["cd"] + jnp.concatenate(snew, axis=1)
            y_ref[s] = y
            zv = z_ref[s].astype(F32)
            yg = y * (zv * jax.nn.sigmoid(zv))
            outs = []
            for g in range(2):
                ygg = yg[:, 256 * g:256 * (g + 1)]
                outs.append(ygg * lax.rsqrt(jnp.mean(ygg * ygg, axis=-1, keepdims=True) + EPS))
            o_ref[s] = (jnp.concatenate(outs, axis=1) * nw_ref[...]).astype(BF16)

    consts = [cw, cb, dtb, alog, dskip_exp, nw, expand, tril]
    deps = [] if dep is None else [dep]
    n_in = 4 + len(consts)

    def body_skipping_dep(*refs):
        body(*refs[:n_in], *refs[n_in + len(deps):])

    sd = lambda n, dt: jax.ShapeDtypeStruct((nb, seq, n), dt)
    o, y, states, pre = pl.pallas_call(
        body_skipping_dep, name="ssd_fwd", grid=(nc,),
        out_shape=(sd(SSM_WIDTH, BF16), sd(SSM_WIDTH, F32), jax.ShapeDtypeStruct((nb, nc, N_STATE, SSM_WIDTH), F32),
                   sd(CONV_CH, F32)),
        in_specs=[row(SSM_WIDTH), row(CONV_CH), tail, row(CHUNK)] + [_full(a.shape) for a in consts]
        + [pl.BlockSpec(memory_space=pl.ANY)] * len(deps),
        out_specs=(row(SSM_WIDTH), row(SSM_WIDTH), states_spec, row(CONV_CH)),
        scratch_shapes=[pltpu.VMEM((nb, N_STATE, SSM_WIDTH), F32)],
        compiler_params=_params("arbitrary"))(fold(z), fold(xbc), fold(xbc), fold(dtr), *consts, *deps)
    return unfold(o), unfold(y), states, unfold(pre)


def _out_proj(mix_a, mix_b, w_out, x, g2, g3, tm, dep=None):
    t_tok = x.shape[0]
    deps = [] if dep is None else [dep]

    def body(a_ref, b_ref, w_ref, x_ref, g2_ref, g3_ref, *rest):
        o_ref, x2_ref, h3_ref = rest[-3:]
        o = _dot(a_ref[...], w_ref[0:GM_WIDTH, :]) + _dot(b_ref[...], w_ref[GM_WIDTH:, :])
        o_ref[...] = o
        r2 = lax.rsqrt(jnp.mean(o * o, axis=-1, keepdims=True) + EPS)
        x2 = x_ref[...] + o * r2 * g2_ref[...]
        x2_ref[...] = x2
        r3 = lax.rsqrt(jnp.mean(x2 * x2, axis=-1, keepdims=True) + EPS)
        h3_ref[...] = (x2 * r3 * g3_ref[...]).astype(BF16)

    row = lambda n: pl.BlockSpec((tm, n), lambda i: (i, 0))
    sd = lambda dt: jax.ShapeDtypeStruct((t_tok, D_MODEL), dt)
    return pl.pallas_call(
        body, name="out_proj", grid=(t_tok // tm,), out_shape=(sd(F32), sd(F32), sd(BF16)),
        in_specs=[row(GM_WIDTH), row(SSM_WIDTH), _full((D_MODEL, D_MODEL)), row(D_MODEL), _full((1, D_MODEL)),
                  _full((1, D_MODEL))] + [pl.BlockSpec(memory_space=pl.ANY)] * len(deps),
        out_specs=(row(D_MODEL),) * 3, compiler_params=_params("parallel"))(mix_a, mix_b, w_out, x, g2, g3, *deps)


def _mlp_fwd(h3, w_up, w_down, x2, target, g4, tm, tf):
    t_tok = x2.shape[0]

    def up_body(h_ref, wu_ref, ra_ref):
        ra_ref[...] = jnp.maximum(_dot(h_ref[...], wu_ref[...]), 0.0).astype(BF16)

    tu = min(2 * tm, t_tok)
    ra = pl.pallas_call(
        up_body, name="mlp_up", grid=(D_FF // tf, t_tok // tu), out_shape=jax.ShapeDtypeStruct((t_tok, D_FF), BF16),
        in_specs=[pl.BlockSpec((tu, D_MODEL), lambda j, i: (i, 0)), pl.BlockSpec((D_MODEL, tf), lambda j, i: (0, j))],
        out_specs=pl.BlockSpec((tu, tf), lambda j, i: (i, j)), compiler_params=_params("parallel", "parallel"))(h3, w_up)

    def down_body(ra_ref, wd_ref, x2_ref, t_ref, g4_ref, dd_ref, dy_ref, dg4_ref, loss_ref):
        i = pl.program_id(0)
        rav = ra_ref[...]
        dvec = _dot(rav * rav, wd_ref[...])
        r4 = lax.rsqrt(jnp.mean(dvec * dvec, axis=-1, keepdims=True) + EPS)
        dn = dvec * r4
        g4 = g4_ref[...]
        err = x2_ref[...] + dn * g4 - t_ref[...]
        dy = err * (1.0 / D_MODEL)
        dy_ref[...] = dy
        dg = dy * g4
        dd_ref[...] = (r4 * (dg - dn * jnp.mean(dg * dn, axis=-1, keepdims=True))).astype(BF16)
        _acc_rows(dg4_ref, _rsum(dy * dn), i == 0)
        tile_loss = 0.5 * jnp.sum(jnp.sum(err * err, axis=-1, keepdims=True), axis=0, keepdims=True) / D_MODEL
        _acc_rows(loss_ref, jnp.broadcast_to(tile_loss, (1, 128)), i == 0)

    row = pl.BlockSpec((tm, D_MODEL), lambda i: (i, 0))
    dd, dy, dg4, loss = pl.pallas_call(
        down_body, name="mlp_down", grid=(t_tok // tm,),
        out_shape=(jax.ShapeDtypeStruct((t_tok, D_MODEL), BF16), jax.ShapeDtypeStruct((t_tok, D_MODEL), F32),
                   jax.ShapeDtypeStruct((1, D_MODEL), F32), jax.ShapeDtypeStruct((1, 128), F32)),
        in_specs=[pl.BlockSpec((tm, D_FF), lambda i: (i, 0)), _full((D_FF, D_MODEL)), row, row, _full((1, D_MODEL))],
        out_specs=(row, row, _full((1, D_MODEL)), _full((1, 128))),
        compiler_params=_params("arbitrary"))(ra, w_down, x2, target, g4)
    return ra, dd, dy, dg4, loss


def _mlp_bwd(dd, w_down, ra, w_up, x2, dy, o, g3, g2, tm, tf):
    t_tok = x2.shape[0]

    def hidden_body(dd_ref, wd_ref, ra_ref, da_ref):
        df = _dot(dd_ref[...], wd_ref[...], _NT)
        da_ref[...] = (df * (2.0 * ra_ref[...].astype(F32))).astype(BF16)

    tu = min(2 * tm, t_tok)
    da = pl.pallas_call(
        hidden_body, name="mlp_bwd_hidden", grid=(D_FF // tf, t_tok // tu),
        out_shape=jax.ShapeDtypeStruct((t_tok, D_FF), BF16),
        in_specs=[pl.BlockSpec((tu, D_MODEL), lambda j, i: (i, 0)), pl.BlockSpec((tf, D_MODEL), lambda j, i: (j, 0)),
                  pl.BlockSpec((tu, tf), lambda j, i: (i, j))],
        out_specs=pl.BlockSpec((tu, tf), lambda j, i: (i, j)),
        compiler_params=_params("parallel", "parallel"))(dd, w_down, ra)

    def in_body(da_ref, wu_ref, x2_ref, dy_ref, o_ref, g3_ref, g2_ref, dx2_ref, do_ref, dg3_ref, dg2_ref):
        i = pl.program_id(0)
        dh3 = _dot(da_ref[...], wu_ref[...], _NT)
        dn3, dg3 = _rms_bwd(x2_ref[...], g3_ref[...], dh3)
        dx2 = dy_ref[...] + dn3
        dx2_ref[...] = dx2
        do, dg2 = _rms_bwd(o_ref[...], g2_ref[...], dx2)
        do_ref[...] = do.astype(BF16)
        _acc_rows(dg3_ref, dg3, i == 0)
        _acc_rows(dg2_ref, dg2, i == 0)

    row = pl.BlockSpec((tm, D_MODEL), lambda i: (i, 0))
    vec = _full((1, D_MODEL))
    sd = lambda dt: jax.ShapeDtypeStruct((t_tok, D_MODEL), dt)
    dx2, do, dg3, dg2 = pl.pallas_call(
        in_body, name="mlp_bwd_in", grid=(t_tok // tm,),
        out_shape=(sd(F32), sd(BF16), jax.ShapeDtypeStruct((1, D_MODEL), F32), jax.ShapeDtypeStruct((1, D_MODEL), F32)),
        in_specs=[pl.BlockSpec((tm, D_FF), lambda i: (i, 0)), _full((D_MODEL, D_FF)), row, row, row, vec, vec],
        out_specs=(row, row, vec, vec), compiler_params=_params("arbitrary"))(da, w_up, x2, dy, o, g3, g2)
    return da, dx2, do, dg3, dg2


def _wgrad(a, b, out_blocks, bm, bn, bk, square_a, name, dep=None):
    t_tok, m = a.shape
    n = b.shape[1]
    nk = t_tok // bk

    def body(a_ref, b_ref, *rest):
        o_ref, acc_ref = rest[-2:]
        k = pl.program_id(2)
        av = a_ref[...]
        if square_a:
            av = av * av
        part = _dot(av, b_ref[...], _TN)

        def emit(res):
            if out_blocks is None:
                o_ref[...] = res.astype(BF16)
            else:
                o_ref[0] = res.astype(BF16)

        if nk == 1:
            emit(part)
            return

        @pl.when(k == 0)
        def _():
            acc_ref[...] = part

        @pl.when(k > 0)
        def _():
            acc_ref[...] += part

        @pl.when(k == nk - 1)
        def _():
            emit(acc_ref[...])

    if out_blocks is None:
        out_shape = jax.ShapeDtypeStruct((m, n), BF16)
        out_spec = pl.BlockSpec((bm, bn), lambda i, j, k: (i, j))
    else:
        assert n // out_blocks == bn
        out_shape = jax.ShapeDtypeStruct((out_blocks, m, bn), BF16)
        out_spec = pl.BlockSpec((1, bm, bn), lambda i, j, k: (j, i, 0))
    deps = [] if dep is None else [dep]
    return pl.pallas_call(
        body, name=name, grid=(m // bm, n // bn, nk), out_shape=out_shape,
        in_specs=[pl.BlockSpec((bk, bm), lambda i, j, k: (k, i)), pl.BlockSpec((bk, bn), lambda i, j, k: (k, j))]
        + [pl.BlockSpec(memory_space=pl.ANY)] * len(deps),
        out_specs=out_spec, scratch_shapes=[pltpu.VMEM((bm, bn) if nk > 1 else (8, 128), F32)],
        compiler_params=_params("parallel", "parallel", "arbitrary"))(a, b, *deps)


def _wgrad_in_chunked(h1, pieces, bn, bk, dep=None):
    t_tok = h1.shape[0]
    nk = t_tok // bk
    shard = IN_COLS // N_DEV
    widths = [b - a for a, b in _IN_SPLITS]

    def body(h_ref, *rest):
        piece_refs = rest[:len(widths)]
        o_ref, acc_ref = rest[-2:]
        k = pl.program_id(1)
        hv = h_ref[...]
        for (a, b), r in zip(_IN_SPLITS, piece_refs):
            part = _dot(r[...], hv, _TN)

            @pl.when(k == 0)
            def _():
                acc_ref[a:b, :] = part

            @pl.when(k > 0)
            def _():
                acc_ref[a:b, :] += part

        @pl.when(k == nk - 1)
        def _():
            for j in range(N_DEV):
                o_ref[j] = acc_ref[shard * j:shard * (j + 1), :].astype(BF16)

    deps = [] if dep is None else [dep]
    return pl.pallas_call(
        body, name="wgrad_in", grid=(D_MODEL // bn, nk), out_shape=jax.ShapeDtypeStruct((N_DEV, shard, D_MODEL), BF16),
        in_specs=[pl.BlockSpec((bk, bn), lambda j, k: (k, j))] + [pl.BlockSpec((bk, n), lambda j, k: (k, 0)) for n in widths]
        + [pl.BlockSpec(memory_space=pl.ANY)] * len(deps),
        out_specs=pl.BlockSpec((N_DEV, shard, bn), lambda j, k: (0, 0, j)),
        scratch_shapes=[pltpu.VMEM((IN_PAD, bn), F32)],
        compiler_params=_params("parallel", "arbitrary"))(h1, *pieces, *deps)


def _wgrad_pieces(h1, pieces, bn, name, dep=None):
    t_tok = h1.shape[0]
    widths = [p.shape[1] for p in pieces]
    starts = [sum(widths[:i]) for i in range(len(widths))]

    def body(h_ref, *rest):
        piece_refs = rest[:len(widths)]
        o_ref = rest[-1]
        hv = h_ref[...]
        for a, n, r in zip(starts, widths, piece_refs):
            o_ref[a:a + n, :] = _dot(r[...], hv, _TN).astype(BF16)

    deps = [] if dep is None else [dep]
    return pl.pallas_call(
        body, name=name, grid=(D_MODEL // bn,), out_shape=jax.ShapeDtypeStruct((sum(widths), D_MODEL), BF16),
        in_specs=[pl.BlockSpec((t_tok, bn), lambda j: (0, j))] + [pl.BlockSpec((t_tok, n), lambda j: (0, 0)) for n in widths]
        + [pl.BlockSpec(memory_space=pl.ANY)] * len(deps),
        out_specs=pl.BlockSpec((sum(widths), bn), lambda j: (0, j)),
        compiler_params=_params("parallel"))(h1, *pieces, *deps)


def _dmix(do, w_out, tm, dep=None):
    t_tok = do.shape[0]

    def body(d_ref, w_ref, *rest):
        rest[-1][...] = _dot(d_ref[...], w_ref[...], _NT).astype(BF16)

    row = pl.BlockSpec((tm, D_MODEL), lambda i: (i, 0))
    deps = [] if dep is None else [dep]
    return pl.pallas_call(
        body, name="dmix", grid=(t_tok // tm,), out_shape=jax.ShapeDtypeStruct((t_tok, D_MODEL), BF16),
        in_specs=[row, _full((D_MODEL, D_MODEL))] + [pl.BlockSpec(memory_space=pl.ANY)] * len(deps), out_specs=row,
        compiler_params=_params("parallel"))(do, w_out, *deps)


def _gmlp_bwd(dmix, u, v, lnw, lnb, wcat, wtcat, bias, avg, expand_t):
    t_tok = u.shape[0]
    tm = min(_GMLP_ROWS, t_tok)

    def body(dm_ref, u_ref, v_ref, lnw_ref, lnb_ref, wcat_ref, wtcat_ref, bias_ref, avg_ref, expt_ref, du_ref, dv_ref,
             dw_ref, db_ref, dlnw_ref, dlnb_ref):
        i = pl.program_id(0)
        m_l, m_r = _lane_masks()
        avg = avg_ref[...]
        lnw = lnw_ref[...]
        ug, dug, dvg, rstd, vhat, vn, mixed = _gmlp_common(
            u_ref[...].astype(F32), v_ref[...].astype(F32), lnw, lnb_ref[...], avg, wcat_ref, bias_ref[...], m_l, m_r)
        dya = dm_ref[...].astype(F32)
        du_ref[...] = (dya * mixed * dug).astype(BF16)
        dmixed = dya * ug
        dvn_rows, dws, dbt = [], [None] * N_HEADS, None
        for r in range(tm // CHUNK):
            dvn_cols = []
            for j in range(N_HEADS // 2):
                dmp = dmixed[CHUNK * r:CHUNK * (r + 1), 128 * j:128 * (j + 1)]
                dvn_cols.append(_dot(wtcat_ref[j], _stack_pair(dmp, m_l, m_r)))
                vnp = vn[CHUNK * r:CHUNK * (r + 1), 128 * j:128 * (j + 1)].astype(BF16)
                for i_h, mask in enumerate((m_l, m_r)):
                    part = _dot((dmp * mask).astype(BF16), vnp, _NT)
                    dws[2 * j + i_h] = part if r == 0 else dws[2 * j + i_h] + part
            dvn_rows.append(jnp.concatenate(dvn_cols, axis=1))
            part = _split_dot(dmixed[CHUNK * r:CHUNK * (r + 1), :], expt_ref[...], 2)
            dbt = part if r == 0 else dbt + part
        dvn = jnp.concatenate(dvn_rows, axis=0)
        dvh = dvn * lnw
        dvgel = rstd * (dvh - _head_mean(dvh, avg) - vhat * _head_mean(dvh * vhat, avg))
        dv_ref[...] = (dvgel * dvg).astype(BF16)
        first = i == 0

        @pl.when(first)
        def _():
            for h in range(N_HEADS):
                dw_ref[h] = dws[h]
            db_ref[...] = dbt

        @pl.when(jnp.logical_not(first))
        def _():
            for h in range(N_HEADS):
                dw_ref[h] += dws[h]
            db_ref[...] += dbt

        _acc_rows(dlnw_ref, _rsum(dvn * vhat), first)
        _acc_rows(dlnb_ref, _rsum(dvn), first)

    row = pl.BlockSpec((tm, GM_WIDTH), lambda i: (i, 0))
    consts = [lnw, lnb, wcat, wtcat, bias, avg, expand_t]
    return pl.pallas_call(
        body, name="gmlp_bwd", grid=(t_tok // tm,),
        out_shape=(jax.ShapeDtypeStruct((t_tok, GM_WIDTH), BF16), jax.ShapeDtypeStruct((t_tok, GM_WIDTH), BF16),
                   jax.ShapeDtypeStruct((N_HEADS, CHUNK, CHUNK), F32), jax.ShapeDtypeStruct((CHUNK, CHUNK), F32),
                   jax.ShapeDtypeStruct((1, GM_WIDTH), F32), jax.ShapeDtypeStruct((1, GM_WIDTH), F32)),
        in_specs=[row, row, row] + [_full(a.shape) for a in consts],
        out_specs=(row, row, _full((N_HEADS, CHUNK, CHUNK)), _full((CHUNK, CHUNK)), _full((1, GM_WIDTH)),
                   _full((1, GM_WIDTH))),
        compiler_params=_params("arbitrary"))(dmix, u, v, *consts)


def _ssd_bwd(dmix, z, xbc, pre, dtr, y, states, cw, cb, dtb, alog, dskip_exp, nw, expand, expand_t, tril, triu, seq,
             dep=None):
    t_tok = z.shape[0]
    nb, nc, row, _, states_spec, fold, unfold = _ssd_specs(t_tok, seq, True)
    q = CHUNK

    def one_sequence(s, dm_ref, z_ref, xbc_ref, pre_ref, dtr_ref, y_ref, st_ref, cw_ref, dtb_ref, alog_ref, dsk_ref,
                     nw_ref, exp_ref, expt_ref, tril_ref, triu_ref, dz_ref, dxbc_ref, ddt_ref, dhead_ref, dstate_ref):
        m_l, m_r = _lane_masks()
        expt = expt_ref[...]
        f = _ssd_common(pre_ref[s], dtr_ref[s], dtb_ref[...], alog_ref[...], exp_ref[...], tril_ref[...])
        act, pre, sg = f["act"], f["pre"], f["sg"]
        xs = act[:, :SSM_WIDTH]
        xdt = xs * f["dt_exp"]
        xw = xdt * f["w_end"]
        state = st_ref[s, 0]
        dstate = dstate_ref[s]
        zv, yv, dout, nw = z_ref[s].astype(F32), y_ref[s], dm_ref[s].astype(F32), nw_ref[...]
        sz = jax.nn.sigmoid(zv)
        sl = zv * sz
        yg = yv * sl
        tv = dout * nw
        dyg_parts, ygh_parts = [], []
        for g in range(2):
            ygg = yg[:, 256 * g:256 * (g + 1)]
            rr = lax.rsqrt(jnp.mean(ygg * ygg, axis=-1, keepdims=True) + EPS)
            ygh = ygg * rr
            tg = tv[:, 256 * g:256 * (g + 1)]
            dyg_parts.append(rr * (tg - ygh * jnp.mean(tg * ygh, axis=-1, keepdims=True)))
            ygh_parts.append(ygh)
        dyg = jnp.concatenate(dyg_parts, axis=1)
        dnw = _rsum(dout * jnp.concatenate(ygh_parts, axis=1))
        dy = dyg * sl
        dz_ref[s] = (dyg * yv * (sz * (1.0 + zv * (1.0 - sz)))).astype(BF16)
        ddsk = _rsum(dy * xs)
        dye = dy * f["e"]
        lane = lax.broadcasted_iota(jnp.int32, (q, q), 1)
        sub = lax.broadcasted_iota(jnp.int32, (q, q), 0)
        rs_mat = jnp.zeros((q, q), F32)
        cs_mat = jnp.zeros((q, q), F32)
        dxdt_cols, yoff, dst_in, dxw, d_b, d_c = [], [], [], [], [], []
        for g in range(2):
            bg = act[:, 512 + 128 * g:640 + 128 * g].astype(BF16)
            cg = act[:, 768 + 128 * g:896 + 128 * g].astype(BF16)
            cb_mat = _dot(cg, bg, _NT)
            stg = state[:, 256 * g:256 * (g + 1)].astype(BF16)
            dyeg = dye[:, 256 * g:256 * (g + 1)].astype(BF16)
            yoff.append(_dot(cg, stg))
            dcg = _dot(dyeg, stg, _NT)
            dst_in.append(_dot(cg, dyeg, _TN))
            dcb = jnp.zeros((q, q), F32)
            for pr in range(2):
                h0 = 4 * g + 2 * pr
                gf = [cb_mat * f["decay"][h0], cb_mat * f["decay"][h0 + 1]]
                gcat = jnp.concatenate([gf[0].astype(BF16), gf[1].astype(BF16)], axis=1)
                xst = _stack_pair(xdt[:, 64 * h0:64 * h0 + 128], m_l, m_r)
                dyp = dy[:, 64 * h0:64 * h0 + 128].astype(BF16)
                dgcat = _dot(dyp, xst, _NT)
                dxst = _dot(gcat, dyp, _TN)
                dxdt_cols.append(dxst[:q] * m_l + dxst[q:] * m_r)
                for i in range(2):
                    h = h0 + i
                    dg = dgcat[:, q * i:q * (i + 1)]
                    mm = dg * gf[i]
                    rs_mat = rs_mat + jnp.where(lane == h, jnp.sum(mm, axis=1, keepdims=True), 0.0)
                    cs_mat = cs_mat + jnp.where(sub == h, jnp.sum(mm, axis=0, keepdims=True), 0.0)
                    dcb = dcb + dg * f["decay"][h]
            dcb16 = dcb.astype(BF16)
            dstg = dstate[:, 256 * g:256 * (g + 1)].astype(BF16)
            d_c.append(dcg + _dot(dcb16, bg))
            dxw.append(_dot(bg, dstg))
            d_b.append(_dot(dcb16, cg, _TN) + _dot(xw[:, 256 * g:256 * (g + 1)].astype(BF16), dstg, _NT))
        dxw = jnp.concatenate(dxw, axis=1)
        dxdt = jnp.concatenate(dxdt_cols, axis=1) + dxw * f["w_end"]
        qv = dxw * xw
        end_row = _rsum(qv) + _rsum(dstate * state) * f["cd"]
        x2 = dye * jnp.concatenate(yoff, axis=1) - qv
        row_i = lax.broadcasted_iota(jnp.int32, (q, 1), 0)
        x2 = x2 + jnp.where(row_i == q - 1, end_row, 0.0)
        da_cs = _split_dot(x2, expt, 2) + rs_mat - cs_mat.T
        ddt = _split_dot(dxdt * xs, expt, 2)
        dxs = dsk_ref[...] * dy + dxdt * f["dt_exp"]
        dda = _split_dot_left(triu_ref[...], da_cs, 3)
        ddt = ddt + dda * f["a_row"]
        dalog = _rsum(dda * f["dt"]) * f["a_row"]
        draw = ddt * jax.nn.sigmoid(f["dtp"])
        ddt_ref[s] = draw.astype(BF16)
        dact = jnp.concatenate([dxs] + d_b + d_c, axis=1)
        dpre = dact * (sg * (1.0 + pre * (1.0 - sg)))
        dhead = dhead_ref[s]
        xv = xbc_ref[s]
        shifted = [_shift_rows(dpre, dhead, 3 - k, False) for k in range(3)] + [dpre]
        dxbc = cw_ref[3:4, :] * dpre
        for k in range(3):
            dxbc = dxbc + cw_ref[k:k + 1, :] * shifted[k]
        dxbc_ref[s] = dxbc.astype(BF16)
        dhead_ref[s] = dpre[0:8, :]
        dstate_ref[s] = dstate * f["cd"] + jnp.concatenate(dst_in, axis=1)
        row8 = lax.broadcasted_iota(jnp.int32, (8, 1), 0)
        dcw = jnp.zeros((8, CONV_CH), F32)
        for k in range(4):
            dcw = dcw + jnp.where(row8 == k, _rsum(shifted[k] * xv), 0.0)
        return dcw, _rsum(dpre), _rsum(draw), dalog, _split_dot(ddsk, expt, 3), dnw

    def body(dm_ref, z_ref, xbc_ref, pre_ref, dtr_ref, y_ref, st_ref, cw_ref, cb_ref, dtb_ref, alog_ref, dsk_ref,
             nw_ref, exp_ref, expt_ref, tril_ref, triu_ref, dz_ref, dxbc_ref, ddt_ref, dcw_ref, dcb_ref, ddtb_ref,
             dalog_ref, dd_ref, dnw_ref, dhead_ref, dstate_ref):
        c = pl.program_id(0)
        first = c == 0

        @pl.when(first)
        def _():
            dstate_ref[...] = jnp.zeros_like(dstate_ref)
            dhead_ref[...] = jnp.zeros_like(dhead_ref)

        total = None
        for s in range(nb):
            parts = one_sequence(s, dm_ref, z_ref, xbc_ref, pre_ref, dtr_ref, y_ref, st_ref, cw_ref, dtb_ref, alog_ref,
                                 dsk_ref, nw_ref, exp_ref, expt_ref, tril_ref, triu_ref, dz_ref, dxbc_ref, ddt_ref,
                                 dhead_ref, dstate_ref)
            total = parts if total is None else tuple(a + b for a, b in zip(total, parts))
        dcw = total[0]

        @pl.when(first)
        def _():
            dcw_ref[...] = dcw

        @pl.when(jnp.logical_not(first))
        def _():
            dcw_ref[...] += dcw

        for ref, part in zip((dcb_ref, ddtb_ref, dalog_ref, dd_ref, dnw_ref), total[1:]):
            _acc_rows(ref, part, first)

    consts = [cw, cb, dtb, alog, dskip_exp, nw, expand, expand_t, tril, triu]
    deps = [] if dep is None else [dep]
    n_in = 7 + len(consts)

    def body_skipping_dep(*refs):
        body(*refs[:n_in], *refs[n_in + len(deps):])

    acc = lambda n: jax.ShapeDtypeStruct((1, n), F32)
    sd = lambda n: jax.ShapeDtypeStruct((nb, seq, n), BF16)
    dz, dxbc, ddt, *small_grads = pl.pallas_call(
        body_skipping_dep, name="ssd_bwd", grid=(nc,),
        out_shape=(sd(SSM_WIDTH), sd(CONV_CH), sd(CHUNK), jax.ShapeDtypeStruct((8, CONV_CH), F32), acc(CONV_CH),
                   acc(CHUNK), acc(CHUNK), acc(CHUNK), acc(SSM_WIDTH)),
        in_specs=[row(SSM_WIDTH, col=1), row(SSM_WIDTH), row(CONV_CH), row(CONV_CH), row(CHUNK), row(SSM_WIDTH),
                  states_spec]
        + [_full(a.shape) for a in consts] + [pl.BlockSpec(memory_space=pl.ANY)] * len(deps),
        out_specs=(row(SSM_WIDTH), row(CONV_CH), row(CHUNK), _full((8, CONV_CH)), _full((1, CONV_CH)),
                   _full((1, CHUNK)), _full((1, CHUNK)), _full((1, CHUNK)), _full((1, SSM_WIDTH))),
        scratch_shapes=[pltpu.VMEM((nb, 8, CONV_CH), F32), pltpu.VMEM((nb, N_STATE, SSM_WIDTH), F32)],
        compiler_params=_params("arbitrary"))(
            fold(dmix), fold(z), fold(xbc), fold(pre), fold(dtr), fold(y), states, *consts, *deps)
    return (unfold(dz), unfold(dxbc), unfold(ddt), *small_grads)


def _in_bwd(du, dv, dz, dxbc, ddt, w_in, x, dx2, g1, tm, me, riders=(), dep=None):
    t_tok = x.shape[0]
    steps = t_tok // tm

    n_in = [5 + ("mask" in rd) for rd in riders]
    first_in = [sum(n_in[:r]) for r in range(len(riders))]

    def body(me_ref, du_ref, dv_ref, dz_ref, dxbc_ref, ddt_ref, w_ref, x_ref, dx2_ref, g_ref, *rest):
        outs = rest[len(rest) - 2 - 4 * len(riders):]
        gx_ref, dg_ref = outs[:2]
        i = pl.program_id(0)
        dh = None
        for (a, b), ref in zip(_IN_SPLITS, (du_ref, dv_ref, dz_ref, dxbc_ref, ddt_ref)):
            part = _dot(ref[...], w_ref[a:b, :])
            dh = part if dh is None else dh + part
        dn, dg = _rms_bwd(x_ref[...], g_ref[...], dh)
        gx_ref[...] = dx2_ref[...] + dn
        _acc_rows(dg_ref, dg, i == 0)
        for r in range(len(riders)):
            p_ref, own_ref, w_ref_r, m_ref_r, v_ref_r = rest[first_in[r]:first_in[r] + 5]
            g = _sum_parts(me_ref[0], p_ref, own_ref[0])
            if n_in[r] == 6:
                g = g * rest[first_in[r] + 5][...]
            d, mn, vn = _adamw_math(w_ref_r[...], g, m_ref_r[...], v_ref_r[...])
            for o_ref, val in zip(outs[2 + 4 * r:6 + 4 * r], (g, d, mn, vn)):
                o_ref[...] = val

    row = lambda n: pl.BlockSpec((tm, n), lambda i, me_ref: (i, 0))
    whole = lambda shape: pl.BlockSpec(shape, lambda i, me_ref: (0,) * len(shape))
    widths = [b - a for a, b in _IN_SPLITS]
    deps = [] if dep is None else [dep]
    rider_args, rider_specs, rider_out_shapes, rider_out_specs = [], [], [], []
    for rd in riders:
        rows, cols = rd["w"].shape[0] // steps, rd["w"].shape[1]
        blk = pl.BlockSpec((rows, cols), lambda i, me_ref: (i, 0))
        rider_args += [rd["parts"], rd["own"], rd["w"], rd["m"], rd["v"]]
        rider_specs += [pl.BlockSpec((N_DEV, rows, cols), lambda i, me_ref: (0, i, 0)),
                        pl.BlockSpec((1, rows, cols), lambda i, me_ref: (me_ref[0], i, 0)), blk, blk, blk]
        if "mask" in rd:
            rider_args.append(rd["mask"])
            rider_specs.append(whole((rows, cols)))
        rider_out_shapes += [jax.ShapeDtypeStruct(rd["w"].shape, F32)] * 4
        rider_out_specs += [blk] * 4
    outs = pl.pallas_call(
        body, name="in_bwd",
        out_shape=(jax.ShapeDtypeStruct((t_tok, D_MODEL), F32), jax.ShapeDtypeStruct((1, D_MODEL), F32),
                   *rider_out_shapes),
        grid_spec=pltpu.PrefetchScalarGridSpec(
            num_scalar_prefetch=1, grid=(steps,),
            in_specs=[row(n) for n in widths] + [whole((IN_PAD, D_MODEL)), row(D_MODEL), row(D_MODEL),
                                                 whole((1, D_MODEL))] + rider_specs
            + [pl.BlockSpec(memory_space=pl.ANY)] * len(deps),
            out_specs=(row(D_MODEL), whole((1, D_MODEL)), *rider_out_specs)),
        compiler_params=_params("arbitrary"))(me, du, dv, dz, dxbc, ddt, w_in, x, dx2, g1, *rider_args, *deps)
    return outs[0], outs[1], [tuple(outs[2 + 4 * r:6 + 4 * r]) for r in range(len(riders))]


def _pad_lanes(a, n):
    return jnp.pad(a, ((0, 0), (0, n - a.shape[1])))


def _local_step(x, target, seq, small, hooks, first_dep=None):
    t_tok = x.shape[0]
    tm = min(TOKEN_TILE, t_tok)
    avg, expand, expand_t, tril, triu = _const_mats()
    g1, g2, g3, g4 = (small[k].reshape(1, D_MODEL) for k in
                      ("norm_mix_pre", "norm_mix_post", "norm_ffn_pre", "norm_ffn_post"))
    tie = (lambda a: a) if first_dep is None else (lambda a: a + first_dep[0, 0])
    lnw = tie(small["gm_ln_w"]).reshape(1, GM_WIDTH)
    lnb = tie(small["gm_ln_b"]).reshape(1, GM_WIDTH)
    causal = jnp.tril(jnp.ones((CHUNK, CHUNK), F32))
    wm = tie(small["gm_w_s"]) * causal
    pair = lambda w: w.reshape(4, 2, CHUNK, CHUNK).transpose(0, 2, 1, 3).reshape(4, CHUNK, 2 * CHUNK).astype(BF16)
    wcat = pair(wm)
    wtcat = pair(jnp.swapaxes(wm, 1, 2))
    bias = jnp.repeat(tie(small["gm_b_s"]).T, HEAD_DIM, axis=1)
    cb = small["conv_b"].reshape(1, CONV_CH)
    dtb = _pad_lanes(tie(small["dt_bias"]).reshape(1, N_HEADS), CHUNK)
    alog = _pad_lanes(tie(small["a_log"]).reshape(1, N_HEADS), CHUNK)
    dskip_exp = jnp.repeat(tie(small["d_skip"]).reshape(1, N_HEADS), HEAD_DIM, axis=1)
    nw = small["ssm_norm_w"].reshape(1, SSM_WIDTH)

    h1 = _prenorm(x, g1, tm, hooks.get("prenorm_after", first_dep))
    w_in_t, conv_w = hooks["mixer_weights"](h1)
    tall = min(2 * tm, t_tok)
    u, v, z, xbc, dtr = _in_proj(h1, w_in_t, tall)
    mix_a = _gmlp_fwd(u, v, lnw, lnb, wcat, bias, avg)
    dep = hooks["gmlp_done"](mix_a) if "gmlp_done" in hooks else None
    mix_b, y_pre, states, pre = _ssd_fwd(z, xbc, dtr, conv_w, cb, dtb, alog, dskip_exp, nw, expand, tril, seq, dep)
    w_out, dep = hooks["mixers_done"](mix_b)
    o, x2, h3 = _out_proj(mix_a, mix_b, w_out, x, g2, g3, tall, dep)
    w_up, w_down = hooks["mlp_weights"](h3)
    tf = FF_TILE
    ra, dd, dy, dg4, loss = _mlp_fwd(h3, w_up, w_down, x2, target, g4, tm, tf)

    da, dx2, do, dg3, dg2 = _mlp_bwd(dd, w_down, ra, w_up, x2, dy, o, g3, g2, tm, tf)
    g_w_down = _wgrad(ra, dd, None, WGRAD_TILE, D_MODEL, t_tok, True, "wgrad_down")
    g_w_up = _wgrad(h3, da, N_DEV, D_MODEL, D_FF // N_DEV, t_tok, False, "wgrad_up")
    dep = hooks["mlp_grads"](g_w_down, g_w_up)
    dmix = _dmix(do, w_out, tall, dep)
    g_w_out = _wgrad_pieces(do, (mix_a, mix_b), WGRAD_TILE, "wgrad_out", dep)
    du, dv, dws, dbt, dlnw, dlnb = _gmlp_bwd(dmix, u, v, lnw, lnb, wcat, wtcat, bias, avg, expand_t)
    dep = hooks["gmlp_grads"](g_w_out, dws)
    dz, dxbc, ddt, dcw, dcb, ddtb, dalog, ddsk, dnw = _ssd_bwd(
        dmix, z, xbc, pre, dtr, y_pre, states, conv_w, cb, dtb, alog, dskip_exp, nw, expand, expand_t, tril, triu, seq,
        dep)
    g_w_in = _wgrad_in_chunked(h1, (du, dv, dz, dxbc, ddt), WGRAD_TILE, t_tok // 2, dep)
    dep = hooks["in_grads"](g_w_in, dcw[0:4])
    riders = hooks["arrived_updates"](dep) if "arrived_updates" in hooks else []
    me = hooks.get("me", jnp.zeros((1,), jnp.int32))
    grad_x, dg1, updates = _in_bwd(du, dv, dz, dxbc, ddt, w_in_t, x, dx2, g1, tm, me, riders, dep)

    grads = dict(
        updates=updates,
        w_in=g_w_in, w_out=g_w_out, w_up=g_w_up, w_down=g_w_down, conv_w=dcw[0:4],
        norm_mix_pre=dg1, norm_mix_post=dg2, norm_ffn_pre=dg3, norm_ffn_post=dg4, gm_ln_w=dlnw, gm_ln_b=dlnb,
        gm_w_s=dws, gm_b_s=dbt, conv_b=dcb, dt_bias=ddtb, a_log=dalog, d_skip=ddsk, ssm_norm_w=dnw)
    return loss[0, 0], grad_x, grads


_WEIGHTS = ("norm_mix_pre", "w_in", "gm_ln_w", "gm_ln_b", "gm_w_s", "gm_b_s", "conv_w", "conv_b", "dt_bias", "a_log",
            "d_skip", "ssm_norm_w", "w_out", "norm_mix_post", "norm_ffn_pre", "w_up", "w_down", "norm_ffn_post")
_SLAB_ROWS = (("norm_mix_pre", 1024), ("norm_mix_post", 1024), ("norm_ffn_pre", 1024), ("norm_ffn_post", 1024),
              ("conv_b", 1024), ("ssm_norm_w", 512), ("gm_ln_w", 512), ("gm_ln_b", 512), ("dt_bias", 8), ("a_log", 8),
              ("d_skip", 8))
_SLAB_LOSS_ROW = len(_SLAB_ROWS)
_SLAB_BS_ROW = 16
_SMALL_PARAMS = tuple(name for name, _ in _SLAB_ROWS) + ("gm_b_s",)
_LN_PARAMS = ("gm_ln_w", "gm_ln_b")


_SLAB_CONV_ROW = _SLAB_LOSS_ROW + 1


def _pack_slab(g, loss_part):
    rows = [_pad_lanes(g[name], D_MODEL) for name, _ in _SLAB_ROWS]
    rows.append(jnp.broadcast_to(loss_part, (1, D_MODEL)))
    rows.append(g["conv_w"])
    assert sum(r.shape[0] for r in rows) == _SLAB_BS_ROW
    rows.append(_pad_lanes(g["gm_b_s"].T[0:N_HEADS], D_MODEL))
    return jnp.concatenate(rows, axis=0)


def _adamw_slab(parts, me, w, m, v):
    names = _SMALL_PARAMS + ("conv_w",)
    shapes = [w[k].shape for k in names]
    unfold = np.zeros((GM_WIDTH, HEAD_DIM), np.float32)
    for h in range(N_HEADS):
        unfold[h * HEAD_DIM:(h + 1) * HEAD_DIM, :] = np.eye(HEAD_DIM)
    unfold = jnp.asarray(unfold, dtype=BF16)
    n = len(names)
    shard = CONV_CH // N_DEV

    def body(me_ref, p_ref, unfold_ref, *refs):
        w_refs, m_refs, v_refs = refs[:n], refs[n:2 * n], refs[2 * n:3 * n]
        outs = refs[3 * n:]
        g_all = p_ref[0]
        for j in range(1, N_DEV):
            g_all = g_all + p_ref[j]
        lane = lax.broadcasted_iota(jnp.int32, (N_HEADS, GM_WIDTH), 1)
        head = lax.broadcasted_iota(jnp.int32, (N_HEADS, GM_WIDTH), 0)
        own_lanes = jnp.logical_and(lane >= head * HEAD_DIM, lane < (head + 1) * HEAD_DIM)
        mine = pl.ds(pl.multiple_of(me_ref[0] * shard, shard), shard)
        for i, name in enumerate(names):
            if name == "gm_b_s":
                g = g_all[_SLAB_BS_ROW:_SLAB_BS_ROW + N_HEADS, 0:CHUNK]
            elif name == "conv_w":
                g = p_ref[0, _SLAB_CONV_ROW:_SLAB_CONV_ROW + 4, mine]
                for j in range(1, N_DEV):
                    g = g + p_ref[j, _SLAB_CONV_ROW:_SLAB_CONV_ROW + 4, mine]
            else:
                row = [r for r, (k, _) in enumerate(_SLAB_ROWS) if k == name][0]
                g = g_all[row:row + 1, 0:dict(_SLAB_ROWS)[name]]
                if name in _LN_PARAMS:
                    g = _split_dot(jnp.where(own_lanes, g, 0.0), unfold_ref[...], 3)
            d, mn, vn = _adamw_math(w_refs[i][...], g, m_refs[i][...], v_refs[i][...])
            for o_ref, val in zip(outs[4 * i:4 * i + 4], (g, d, mn, vn)):
                o_ref[...] = val
        outs[-1][...] = g_all[_SLAB_LOSS_ROW:_SLAB_LOSS_ROW + 1, 0:128]

    def whole(shape):
        nd = len(shape)
        return pl.BlockSpec(shape, lambda i, me_ref: (0,) * nd)

    ins = [parts, unfold] + [d[k] for d in (w, m, v) for k in names]
    out_shape = tuple(jax.ShapeDtypeStruct(s, F32) for s in shapes for _ in range(4)) + (
        jax.ShapeDtypeStruct((1, 128), F32),)
    outs = pl.pallas_call(
        body, name="adamw_small", out_shape=out_shape,
        grid_spec=pltpu.PrefetchScalarGridSpec(
            num_scalar_prefetch=1, grid=(1,), in_specs=[whole(a.shape) for a in ins],
            out_specs=tuple(whole(s.shape) for s in out_shape)),
        compiler_params=_params("arbitrary"))(me, *ins)
    return {k: tuple(outs[4 * i:4 * i + 4]) for i, k in enumerate(names)}, outs[-1][0, 0]


def kernel(x, norm_mix_pre, w_in, gm_ln_w, gm_ln_b, gm_w_s, gm_b_s, conv_w, conv_b, dt_bias, a_log, d_skip, ssm_norm_w, w_out, norm_mix_post, norm_ffn_pre, w_up, w_down, norm_ffn_post, loss_target, m_norm_mix_pre, m_w_in, m_gm_ln_w, m_gm_ln_b, m_gm_w_s, m_gm_b_s, m_conv_w, m_conv_b, m_dt_bias, m_a_log, m_d_skip, m_ssm_norm_w, m_w_out, m_norm_mix_post, m_norm_ffn_pre, m_w_up, m_w_down, m_norm_ffn_post, v_norm_mix_pre, v_w_in, v_gm_ln_w, v_gm_ln_b, v_gm_w_s, v_gm_b_s, v_conv_w, v_conv_b, v_dt_bias, v_a_log, v_d_skip, v_ssm_norm_w, v_w_out, v_norm_mix_post, v_norm_ffn_pre, v_w_up, v_w_down, v_norm_ffn_post):
    w = dict(norm_mix_pre=norm_mix_pre, w_in=w_in, gm_ln_w=gm_ln_w, gm_ln_b=gm_ln_b, gm_w_s=gm_w_s, gm_b_s=gm_b_s, conv_w=conv_w, conv_b=conv_b, dt_bias=dt_bias, a_log=a_log, d_skip=d_skip, ssm_norm_w=ssm_norm_w, w_out=w_out, norm_mix_post=norm_mix_post, norm_ffn_pre=norm_ffn_pre, w_up=w_up, w_down=w_down, norm_ffn_post=norm_ffn_post)
    m = dict(norm_mix_pre=m_norm_mix_pre, w_in=m_w_in, gm_ln_w=m_gm_ln_w, gm_ln_b=m_gm_ln_b, gm_w_s=m_gm_w_s, gm_b_s=m_gm_b_s, conv_w=m_conv_w, conv_b=m_conv_b, dt_bias=m_dt_bias, a_log=m_a_log, d_skip=m_d_skip, ssm_norm_w=m_ssm_norm_w, w_out=m_w_out, norm_mix_post=m_norm_mix_post, norm_ffn_pre=m_norm_ffn_pre, w_up=m_w_up, w_down=m_w_down, norm_ffn_post=m_norm_ffn_post)
    v = dict(norm_mix_pre=v_norm_mix_pre, w_in=v_w_in, gm_ln_w=v_gm_ln_w, gm_ln_b=v_gm_ln_b, gm_w_s=v_gm_w_s, gm_b_s=v_gm_b_s, conv_w=v_conv_w, conv_b=v_conv_b, dt_bias=v_dt_bias, a_log=v_a_log, d_skip=v_d_skip, ssm_norm_w=v_ssm_norm_w, w_out=v_w_out, norm_mix_post=v_norm_mix_post, norm_ffn_pre=v_norm_ffn_pre, w_up=v_w_up, w_down=v_w_down, norm_ffn_post=v_norm_ffn_post)
    n_batch, seq, _ = x.shape
    shard_in = IN_COLS // N_DEV

    me = (4 * lax.axis_index("x") + 2 * lax.axis_index("y") + lax.axis_index("c")).astype(jnp.int32).reshape(1)

    def in_slot(own):
        return lax.dynamic_update_slice(lax.empty((N_DEV,) + own.shape, own.dtype), own[None],
                                        (me[0],) + (0,) * own.ndim)

    lying = lambda t: jnp.transpose(t, (2, 0, 1))
    first = [_cast_to_slot(lying(w_in), me, shard_in, "cast_w_in"), in_slot(conv_w[0])]
    ici_1, tok_ici_1 = _exchange_start(first, [True] * 2, _SAME_CORE_PEERS, "gather_mix_ici_start")
    cast_out = _cast_to_slot(w_out[0], me, 128, "cast_w_out", dep=tok_ici_1)
    cast_up = _cast_to_slot(w_up[0], me, 1024, "cast_w_up", cols=True, dep=cast_out)
    second = [cast_out, cast_up, _cast_to_slot(w_down[0], me, 512, "cast_w_down", dep=cast_up)]
    gathering = {}

    def mixer_weights(after):
        bufs = [buf for buf, _ in _exchange_wait(ici_1, after, "gather_mix_ici_wait")]
        d2d_1, tok_d2d_1 = _exchange_start(bufs, [True] * 2, _SIBLING_FORWARD, "gather_mix_d2d_start")
        gathering["late_ici"], tok_ici_2 = _exchange_start(
            second, [True] * 3, _SAME_CORE_PEERS, "gather_late_ici_start", dep=tok_d2d_1)
        (_, ag_in), (_, ag_conv) = _exchange_wait(d2d_1, tok_ici_2, "gather_mix_d2d_wait")
        w_in_t = _stack_shards(ag_in, IN_PAD, STACK_TILE, "stack_w_in")
        return w_in_t, ag_conv.transpose(1, 0, 2).reshape(4, CONV_CH)

    def gmlp_done(after):
        ((buf, _),) = _exchange_wait(gathering["late_ici"], after, "gather_out_ici_wait", only=(0,))
        gathering["out"], tok = _exchange_start([buf], [True], _SIBLING_FORWARD, "gather_out_d2d_start")
        return tok

    def mixers_done(after):
        bufs = [buf for buf, _ in _exchange_wait(gathering["late_ici"], after, "gather_mlp_ici_wait", only=(1, 2))]
        gathering["mlp"], tok = _exchange_start(bufs, [True] * 2, _SIBLING_FORWARD, "gather_mlp_d2d_start")
        ((_, ag_out),) = _exchange_wait(gathering["out"], tok, "gather_out_d2d_wait")
        return ag_out.reshape(D_MODEL, D_MODEL), tok

    def mlp_weights(after):
        (_, ag_up), (_, ag_down) = _exchange_wait(gathering["mlp"], after, "gather_mlp_d2d_wait")
        return ag_up, ag_down.reshape(D_FF, D_MODEL)

    sent = {}

    def mlp_grads(g_w_down, g_w_up):
        sent["mlp"], tok = _exchange_start(
            [g_w_down.reshape(N_DEV, D_FF // N_DEV, D_MODEL), g_w_up], [False, False], _ALL_PEERS, "grads_mlp_start")
        return tok

    def gmlp_grads(g_w_out, g_w_s):
        sent["gmlp"], tok = _exchange_start(
            [g_w_out.reshape(N_DEV, D_MODEL // N_DEV, D_MODEL), in_slot(g_w_s.astype(BF16))], [False, True], _ALL_PEERS,
            "grads_gmlp_start")
        return tok

    def in_grads(g_w_in_t, g_conv_w):
        sent["in"], tok = _exchange_start([g_w_in_t], [False], _ALL_PEERS, "grads_in_start")
        return tok

    def arrived_updates(after):
        (own_down, p_down), (own_up, p_up) = _exchange_wait(sent["mlp"], after, "grads_mlp_wait")
        (own_out, p_out), (_, p_ws) = _exchange_wait(sent["gmlp"], own_up, "grads_gmlp_wait")
        rows = lambda t: t.reshape(t.shape[:-3] + (N_HEADS * CHUNK, CHUNK))
        return [dict(parts=p_up, own=own_up, w=w_up[0], m=m_w_up[0], v=v_w_up[0]),
                dict(parts=p_down, own=own_down, w=w_down[0], m=m_w_down[0], v=v_w_down[0]),
                dict(parts=p_out, own=own_out, w=w_out[0], m=m_w_out[0], v=v_w_out[0]),
                dict(parts=rows(p_ws), own=rows(p_ws), w=rows(gm_w_s[0]), m=rows(m_gm_w_s[0]), v=rows(v_gm_w_s[0]),
                     mask=jnp.tril(jnp.ones((CHUNK, CHUNK), F32)))]

    small = {k: w[k][0] for k in _SMALL_PARAMS + ("gm_w_s",)}
    loss_part, grad_x, g = _local_step(
        x.reshape(n_batch * seq, D_MODEL), loss_target.reshape(n_batch * seq, D_MODEL), seq, small,
        dict(mixer_weights=mixer_weights, gmlp_done=gmlp_done, mixers_done=mixers_done, mlp_weights=mlp_weights,
             mlp_grads=mlp_grads, gmlp_grads=gmlp_grads, in_grads=in_grads, arrived_updates=arrived_updates, me=me,
             prenorm_after=second[2]), first_dep=tok_ici_1)

    sent_rows, tok_rows = _exchange_start([in_slot(_pack_slab(g, loss_part))], [True], _ALL_PEERS, "grads_rows_start")
    res = dict(zip(("w_up", "w_down", "w_out", "gm_w_s"), g["updates"]))
    ((own_in, p_in),) = _exchange_wait(sent["in"], tok_rows, "grads_in_wait")
    upd_in = _adamw_reduce(p_in, own_in, me, lying(w_in), lying(m_w_in), lying(v_w_in), "adamw_w_in")
    res["w_in"] = tuple(jnp.transpose(t, (1, 2, 0)) for t in upd_in)
    ((_, p_rows),) = _exchange_wait(sent_rows, upd_in[1], "grads_rows_wait")
    flat = lambda t: t[0] if t.ndim == 3 else t
    small_res, loss = _adamw_slab(
        p_rows, me, *({k: flat(d[k]) for k in _SMALL_PARAMS + ("conv_w",)} for d in (w, m, v)))
    res.update(small_res)
    res = {k: tuple(r.reshape(w[k].shape) for r in res[k]) for k in _WEIGHTS}

    outs = [loss, grad_x.reshape(x.shape)]
    for part in range(4):
        outs.extend(res[k][part] for k in _WEIGHTS)
    return tuple(outs)
```

```python
import functools

import jax
import jax.numpy as jnp
import numpy as np
from jax import lax
from jax.experimental import pallas as pl
from jax.experimental.pallas import tpu as pltpu

F32 = jnp.float32
BF16 = jnp.bfloat16

D_MODEL = 1024
GM_WIDTH = 512
SSM_WIDTH = 512
CONV_CH = 1024
N_HEADS = 8
HEAD_DIM = 64
N_STATE = 128
CHUNK = 128
D_FF = 4096
IN_COLS = 2568
IN_PAD = 2688
N_DEV = 8
EPS = 1e-6
ADAM_LR, ADAM_B1, ADAM_B2, ADAM_EPS, ADAM_WD, ADAM_STEP = 0.001, 0.9, 0.999, 1e-08, 0.01, 10
VMEM_LIMIT_BYTES = 56 * 1024 * 1024
TOKEN_TILE = 512
FF_TILE = 2048
WGRAD_TILE = 512
STACK_TILE = 256
_NT = (((1,), (1,)), ((), ()))
_TN = (((0,), (0,)), ((), ()))


def _params(*sem):
    return pltpu.CompilerParams(dimension_semantics=sem or None, vmem_limit_bytes=VMEM_LIMIT_BYTES)


def _dot(a, b, dims=None):
    if dims is None:
        return jnp.dot(a, b, preferred_element_type=F32)
    return lax.dot_general(a, b, dims, preferred_element_type=F32)


def _split_terms(x, terms):
    out, rem = [], x
    for i in range(terms):
        hi = rem.astype(BF16)
        out.append(hi)
        if i + 1 < terms:
            rem = rem - hi.astype(F32)
    return out


def _split_dot(x, m, terms):
    acc = None
    for hi in _split_terms(x, terms):
        part = _dot(hi, m)
        acc = part if acc is None else acc + part
    return acc


def _split_dot_left(m, x, terms):
    acc = None
    for hi in _split_terms(x, terms):
        part = _dot(m, hi)
        acc = part if acc is None else acc + part
    return acc


def _gelu_and_grad(x):
    c = 0.7978845608028654
    inner = c * (x + 0.044715 * x * x * x)
    t = jnp.tanh(inner)
    g = 0.5 * x * (1.0 + t)
    dg = 0.5 * (1.0 + t) + 0.5 * x * (1.0 - t * t) * c * (1.0 + 3.0 * 0.044715 * x * x)
    return g, dg


def _softplus(x):
    return jnp.maximum(x, 0.0) + jnp.log(1.0 + jnp.exp(-jnp.abs(x)))


def _rsum(x):
    return jnp.sum(x, axis=0, keepdims=True)


def _acc_rows(ref, part, first):
    val = jnp.broadcast_to(part, ref.shape)

    @pl.when(first)
    def _():
        ref[...] = val

    @pl.when(jnp.logical_not(first))
    def _():
        ref[...] += val


def _rms_bwd(n, g, dout):
    r = lax.rsqrt(jnp.mean(n * n, axis=-1, keepdims=True) + EPS)
    nh = n * r
    dg = dout * g
    dn = r * (dg - nh * jnp.mean(dg * nh, axis=-1, keepdims=True))
    return dn, _rsum(dout * nh)


def _const_mats():
    avg = np.kron(np.eye(4), np.full((HEAD_DIM, HEAD_DIM), 1.0 / HEAD_DIM))
    expand = np.zeros((CHUNK, SSM_WIDTH), np.float32)
    for h in range(N_HEADS):
        expand[h, h * HEAD_DIM:(h + 1) * HEAD_DIM] = 1.0
    tril = np.tril(np.ones((CHUNK, CHUNK), np.float32))
    as_bf16 = lambda a: jnp.asarray(a, dtype=BF16)
    return as_bf16(avg), as_bf16(expand), as_bf16(expand.T), as_bf16(tril), as_bf16(tril.T)


def _full(shape):
    nd = len(shape)
    return pl.BlockSpec(shape, lambda *_: (0,) * nd)


_HBM = pl.BlockSpec(memory_space=pltpu.HBM)
_SEM = pl.BlockSpec(memory_space=pltpu.SEMAPHORE)
_ALL_PEERS = tuple((k, 0) for k in range(1, N_DEV))
_SAME_CORE_PEERS = ((2, 0), (4, 0), (6, 0))
_SIBLING_FORWARD = ((1, 0), (1, 2), (1, 4), (1, 6))


def _flip(j, k):
    for bit in (4, 2, 1):
        if k & bit:
            j = j + bit - 2 * (j & bit)
    return j


def _copies(src, land, send_sems, recv_sems, hops, slots=None):
    x, y, c = lax.axis_index("x"), lax.axis_index("y"), lax.axis_index("c")
    me = 4 * x + 2 * y + c
    slots = range(len(src)) if slots is None else slots
    out = []
    for t in range(len(src)):
        for i, (k, b) in enumerate(hops):
            pos = (1 - x if k & 4 else x, 1 - y if k & 2 else y, 1 - c if k & 1 else c)
            peer = _flip(me, k)
            sem = slots[t] * len(hops) + i
            mk = functools.partial(pltpu.make_async_remote_copy, send_sem=send_sems.at[sem], recv_sem=recv_sems.at[sem],
                                   device_id=pos, device_id_type=pl.DeviceIdType.MESH)
            if land[t] is None and src[t].shape[0] != N_DEV:
                width = src[t].shape[1] // N_DEV
                slab = lambda j: src[t].at[:, pl.ds(pl.multiple_of(j * width, 128), width)]
                mine = functools.partial(mk, src_ref=slab(_flip(me, b)), dst_ref=slab(_flip(me, b)))
                theirs = functools.partial(mk, src_ref=slab(_flip(peer, b)), dst_ref=slab(_flip(peer, b)))
            elif land[t] is None:
                mine = functools.partial(mk, src_ref=src[t].at[_flip(me, b)], dst_ref=src[t].at[_flip(me, b)])
                theirs = functools.partial(mk, src_ref=src[t].at[_flip(peer, b)], dst_ref=src[t].at[_flip(peer, b)])
            else:
                assert b == 0
                mine = functools.partial(mk, src_ref=src[t].at[peer], dst_ref=land[t].at[me])
                theirs = functools.partial(mk, src_ref=src[t].at[peer], dst_ref=land[t].at[peer])
            out.append((mine, theirs))
    return out


def _exchange_start(srcs, inplace, peers, name, dep=None):
    n = len(srcs)
    lands = [None if ip else pltpu.with_memory_space_constraint(lax.empty(s.shape, s.dtype), pltpu.HBM)
             for s, ip in zip(srcs, inplace)]
    real_lands = [l for l in lands if l is not None]
    n_l = len(real_lands)
    deps = [] if dep is None else [dep]

    def body(*refs):
        src = refs[:n]
        land_refs = list(refs[n:n + n_l])
        send_sems, recv_sems = refs[n + n_l + len(deps)], refs[n + n_l + len(deps) + 1]
        token = refs[-1]
        land = [None if ip else land_refs.pop(0) for ip in inplace]
        for mine, _ in _copies(src, land, send_sems, recv_sems, peers):
            mine().start()
        token[...] = jnp.zeros_like(token)

    sem_t = pltpu.SemaphoreType.DMA((n * len(peers),))
    outs = pl.pallas_call(
        body, name=name,
        out_shape=(sem_t, sem_t) + tuple(pltpu.HBM(a.shape, a.dtype) for a in list(srcs) + real_lands)
        + (jax.ShapeDtypeStruct((8, 128), F32),),
        in_specs=[_HBM] * (n + n_l) + [pl.BlockSpec(memory_space=pl.ANY)] * len(deps),
        out_specs=(_SEM, _SEM) + (_HBM,) * (n + n_l) + (pl.BlockSpec(memory_space=pltpu.VMEM),),
        input_output_aliases={i: 2 + i for i in range(n + n_l)},
        compiler_params=pltpu.CompilerParams(has_side_effects=pltpu.SideEffectType.DATAFLOW_SIDE_EFFECTING),
    )(*[pltpu.with_memory_space_constraint(s, pltpu.HBM) for s in srcs], *real_lands, *deps)
    handle = dict(send=outs[0], recv=outs[1], srcs=outs[2:2 + n], lands=outs[2 + n:2 + n + n_l], inplace=inplace,
                  peers=peers)
    return handle, outs[-1]


def _exchange_wait(handle, after, name, only=None):
    srcs, lands, inplace, peers = handle["srcs"], handle["lands"], handle["inplace"], handle["peers"]
    slots = None
    if only is not None:
        assert all(inplace)
        slots, srcs, inplace = list(only), [srcs[t] for t in only], [True] * len(only)
    n, n_l = len(srcs), len(lands)
    after = after if isinstance(after, tuple) else (after,)

    def body(*refs):
        src = refs[:n]
        land_refs = list(refs[n:n + n_l])
        send_sems, recv_sems = refs[n + n_l], refs[n + n_l + 1]
        land = [None if ip else land_refs.pop(0) for ip in inplace]
        for mine, theirs in _copies(src, land, send_sems, recv_sems, peers, slots):
            mine().wait_send()
            theirs().wait_recv()

    outs = pl.pallas_call(
        body, name=name, out_shape=tuple(pltpu.HBM(a.shape, a.dtype) for a in list(srcs) + list(lands)),
        in_specs=[_HBM] * (n + n_l) + [_SEM, _SEM] + [pl.BlockSpec(memory_space=pl.ANY)] * len(after),
        out_specs=(_HBM,) * (n + n_l), input_output_aliases={i: i for i in range(n + n_l)},
        compiler_params=pltpu.CompilerParams(has_side_effects=pltpu.SideEffectType.DATAFLOW_SIDE_EFFECTING),
    )(*srcs, *lands, handle["send"], handle["recv"], *after)
    res, land_out = [], list(outs[n:])
    for t in range(n):
        res.append((outs[t], outs[t] if inplace[t] else land_out.pop(0)))
    return res


def _cast_to_slot(w, me, rows, name, cols=False, dep=None):
    r, cdim = w.shape[0], w.shape[-1]
    deps = [] if dep is None else [dep]

    def body(me_ref, w_ref, *rest):
        o_ref = rest[-1]
        if cols:
            o_ref[...] = w_ref[...].astype(BF16)
        else:
            o_ref[0] = w_ref[...].reshape(rows, cdim).astype(BF16)

    if cols:
        out_shape = jax.ShapeDtypeStruct((r, N_DEV * cdim), BF16)
        out_spec = pl.BlockSpec((rows, cdim), lambda i, me_ref: (i, me_ref[0]))
    else:
        out_shape = jax.ShapeDtypeStruct((N_DEV, r, cdim), BF16)
        out_spec = pl.BlockSpec((1, rows, cdim), lambda i, me_ref: (me_ref[0], i, 0))
    return pl.pallas_call(
        body, name=name, out_shape=out_shape,
        grid_spec=pltpu.PrefetchScalarGridSpec(
            num_scalar_prefetch=1, grid=(r // rows,),
            in_specs=[pl.BlockSpec((rows, cdim), lambda i, me_ref: (i, 0)) if w.ndim == 2 else
                      pl.BlockSpec((rows, 1, cdim), lambda i, me_ref: (i, 0, 0))]
            + [pl.BlockSpec(memory_space=pl.ANY)] * len(deps), out_specs=out_spec),
        compiler_params=_params("parallel"))(me, w, *deps)


def _stack_shards(blocks, rows, bn, name):
    n, r, cdim = blocks.shape

    def body(b_ref, o_ref, acc_ref):
        acc_ref[n * r:, :] = jnp.zeros((rows - n * r, bn), F32)
        for j in range(n):
            acc_ref[r * j:r * (j + 1), :] = b_ref[j].astype(F32)
        o_ref[...] = acc_ref[...].astype(BF16)

    return pl.pallas_call(
        body, name=name, grid=(cdim // bn,), out_shape=jax.ShapeDtypeStruct((rows, cdim), BF16),
        in_specs=[pl.BlockSpec((n, r, bn), lambda i: (0, 0, i))], out_specs=pl.BlockSpec((rows, bn), lambda i: (0, i)),
        scratch_shapes=[pltpu.VMEM((rows, bn), F32)], compiler_params=_params("parallel"))(blocks)


def _adamw_math(w, g, m, v):
    m = ADAM_B1 * m + (1.0 - ADAM_B1) * g
    v = ADAM_B2 * v + (1.0 - ADAM_B2) * (g * g)
    m_hat = m / (1.0 - ADAM_B1 ** ADAM_STEP)
    v_hat = v / (1.0 - ADAM_B2 ** ADAM_STEP)
    delta = -ADAM_LR * (m_hat / (jnp.sqrt(v_hat) + ADAM_EPS) + ADAM_WD * w)
    return delta, m, v


def _sum_parts(me, p_ref, own):
    g = None
    for j in range(N_DEV):
        term = (p_ref[j] if own is None else jnp.where(me == j, own, p_ref[j])).astype(F32)
        g = term if g is None else g + term
    return g


def _adamw_reduce(parts, own, me, w, m, v, name):
    r, _, cdim = w.shape

    def body(me_ref, p_ref, own_ref, w_ref, m_ref, v_ref, g_out, d_out, m_out, v_out):
        g = _sum_parts(me_ref[0], p_ref, own_ref[0]).reshape(r, 1, cdim)
        d, mn, vn = _adamw_math(w_ref[...], g, m_ref[...], v_ref[...])
        g_out[...] = g
        d_out[...] = d
        m_out[...] = mn
        v_out[...] = vn

    blk = pl.BlockSpec((r, 1, cdim), lambda i, me_ref: (0, 0, 0))
    return pl.pallas_call(
        body, name=name, out_shape=(jax.ShapeDtypeStruct(w.shape, F32),) * 4,
        grid_spec=pltpu.PrefetchScalarGridSpec(
            num_scalar_prefetch=1, grid=(1,),
            in_specs=[pl.BlockSpec((N_DEV, r, cdim), lambda i, me_ref: (0, 0, 0)),
                      pl.BlockSpec((1, r, cdim), lambda i, me_ref: (me_ref[0], 0, 0)), blk, blk, blk],
            out_specs=(blk,) * 4),
        compiler_params=_params("arbitrary"))(me, parts, own, w, m, v)


_IN_SPLITS = ((0, 512), (512, 1024), (1024, 1536), (1536, 2560), (2560, IN_PAD))


def _prenorm(x, g1, tm, dep=None):
    t_tok = x.shape[0]
    deps = [] if dep is None else [dep]

    def body(x_ref, g_ref, *rest):
        xv = x_ref[...]
        r = lax.rsqrt(jnp.mean(xv * xv, axis=-1, keepdims=True) + EPS)
        rest[-1][...] = (xv * r * g_ref[...]).astype(BF16)

    row = pl.BlockSpec((tm, D_MODEL), lambda i: (i, 0))
    return pl.pallas_call(
        body, name="prenorm", grid=(t_tok // tm,), out_shape=jax.ShapeDtypeStruct((t_tok, D_MODEL), BF16),
        in_specs=[row, _full((1, D_MODEL))] + [pl.BlockSpec(memory_space=pl.ANY)] * len(deps), out_specs=row,
        compiler_params=_params("parallel"))(x, g1, *deps)


def _in_proj(h1, w_in, tm):
    t_tok = h1.shape[0]

    def body(h_ref, w_ref, *outs):
        h = h_ref[...]
        for (a, b), o_ref in zip(_IN_SPLITS, outs):
            o_ref[...] = _dot(h, w_ref[a:b, :], _NT).astype(o_ref.dtype)

    row = lambda n: pl.BlockSpec((tm, n), lambda i: (i, 0))
    widths = [b - a for a, b in _IN_SPLITS]
    dtypes = (BF16, BF16, BF16, F32, F32)
    return pl.pallas_call(
        body, name="in_proj", grid=(t_tok // tm,),
        out_shape=tuple(jax.ShapeDtypeStruct((t_tok, n), dt) for n, dt in zip(widths, dtypes)),
        in_specs=[row(D_MODEL), _full((IN_PAD, D_MODEL))], out_specs=tuple(row(n) for n in widths),
        compiler_params=_params("parallel"))(h1, w_in)


def _lane_masks():
    lane = lax.broadcasted_iota(jnp.int32, (1, 2 * HEAD_DIM), 1)
    left = (lane < HEAD_DIM).astype(F32)
    return left, 1.0 - left


def _stack_pair(v, m_l, m_r):
    return jnp.concatenate([v * m_l, v * m_r], axis=0).astype(BF16)


def _head_mean(x, avg):
    n = avg.shape[0]
    return jnp.concatenate([_split_dot(x[:, n * i:n * (i + 1)], avg, 2) for i in range(x.shape[1] // n)], axis=1)


def _gmlp_common(u, v, lnw, lnb, avg, wcat_ref, bias, m_l, m_r):
    ug, dug = _gelu_and_grad(u)
    vg, dvg = _gelu_and_grad(v)
    mu = _head_mean(vg, avg)
    vc = vg - mu
    var = _head_mean(vc * vc, avg)
    rstd = lax.rsqrt(var + EPS)
    vhat = vc * rstd
    vn = vhat * lnw + lnb
    rows = []
    for r in range(u.shape[0] // CHUNK):
        cols = []
        for j in range(N_HEADS // 2):
            pair = vn[CHUNK * r:CHUNK * (r + 1), 128 * j:128 * (j + 1)]
            cols.append(_dot(wcat_ref[j], _stack_pair(pair, m_l, m_r)))
        rows.append(jnp.concatenate(cols, axis=1) + bias)
    mixed = jnp.concatenate(rows, axis=0)
    return ug, dug, dvg, rstd, vhat, vn, mixed


_GMLP_ROWS = 4 * CHUNK


def _gmlp_fwd(u, v, lnw, lnb, wcat, bias, avg):
    t_tok = u.shape[0]
    tm = min(_GMLP_ROWS, t_tok)

    def body(u_ref, v_ref, lnw_ref, lnb_ref, wcat_ref, bias_ref, avg_ref, o_ref):
        m_l, m_r = _lane_masks()
        ug, _, _, _, _, _, mixed = _gmlp_common(
            u_ref[...].astype(F32), v_ref[...].astype(F32), lnw_ref[...], lnb_ref[...], avg_ref[...], wcat_ref,
            bias_ref[...], m_l, m_r)
        o_ref[...] = (ug * mixed).astype(BF16)

    row = pl.BlockSpec((tm, GM_WIDTH), lambda i: (i, 0))
    return pl.pallas_call(
        body, name="gmlp_fwd", grid=(t_tok // tm,), out_shape=jax.ShapeDtypeStruct((t_tok, GM_WIDTH), BF16),
        in_specs=[row, row, _full((1, GM_WIDTH)), _full((1, GM_WIDTH)), _full(wcat.shape), _full(bias.shape),
                  _full(avg.shape)],
        out_specs=row, compiler_params=_params("parallel"))(u, v, lnw, lnb, wcat, bias, avg)


def _shift_rows(x, edge, j, down):
    groups, cols = x.shape[0] // 8, x.shape[1]
    amount = j if down else 8 - j
    rot = pltpu.roll(x.reshape(groups, 8, cols), amount, axis=1)
    edge_rot = pltpu.roll(edge, amount, axis=0)[None]
    sub = lax.broadcasted_iota(jnp.int32, (1, 8, 1), 1)
    if down:
        out = jnp.where(sub < j, jnp.concatenate([edge_rot, rot[:-1]], axis=0), rot)
    else:
        out = jnp.where(sub < 8 - j, rot, jnp.concatenate([rot[1:], edge_rot], axis=0))
    return out.reshape(x.shape)


def _conv_pre(xbc, tail, cw_ref, cb):
    taps = [_shift_rows(xbc, tail, 3 - k, True) for k in range(3)] + [xbc]
    return cb + cw_ref[0:1, :] * taps[0] + cw_ref[1:2, :] * taps[1] + cw_ref[2:3, :] * taps[2] + cw_ref[3:4, :] * taps[3]


def _ssd_common(pre, dtr, dtb, alog, expand, tril):
    q = CHUNK
    sg = jax.nn.sigmoid(pre)
    act = pre * sg
    lane = lax.broadcasted_iota(jnp.int32, (1, CHUNK), 1)
    a_row = jnp.where(lane < N_HEADS, -jnp.exp(alog), 0.0)
    dtp = dtr + dtb
    dt = _softplus(dtp)
    a_cs = _split_dot_left(tril, dt * a_row, 3)
    a_cs_t = a_cs.T
    dt_exp = _split_dot(dt, expand, 3)
    a_exp = _split_dot(a_cs, expand, 3)
    a_end = a_exp[q - 1:q, :]
    li = lax.broadcasted_iota(jnp.int32, (q, q), 0)
    si = lax.broadcasted_iota(jnp.int32, (q, q), 1)
    causal = si <= li
    decay = []
    for h in range(N_HEADS):
        seg = a_cs[:, h:h + 1] - a_cs_t[h:h + 1, :]
        decay.append(jnp.where(causal, jnp.exp(jnp.minimum(seg, 0.0)), 0.0))
    return dict(pre=pre, sg=sg, act=act, a_row=a_row, dtp=dtp, dt=dt, dt_exp=dt_exp, a_exp=a_exp,
                e=jnp.exp(a_exp), w_end=jnp.exp(a_end - a_exp), cd=jnp.exp(a_end), decay=decay)


def _ssd_specs(t_tok, seq, reverse):
    nb, nc = t_tok // seq, seq // CHUNK

    def chunk(c):
        return nc - 1 - c if reverse else c

    def row(n, col=0):
        return pl.BlockSpec((nb, CHUNK, n), lambda c: (0, chunk(c), col))

    tail = pl.BlockSpec((nb, 8, CONV_CH), lambda c: (0, jnp.maximum(chunk(c) * (CHUNK // 8) - 1, 0), 0))
    states = pl.BlockSpec((nb, 1, N_STATE, SSM_WIDTH), lambda c: (0, chunk(c), 0, 0))
    fold = lambda a: a.reshape(nb, seq, a.shape[-1])
    unfold = lambda a: a.reshape(t_tok, a.shape[-1])
    return nb, nc, row, tail, states, fold, unfold


def _ssd_fwd(z, xbc, dtr, cw, cb, dtb, alog, dskip_exp, nw, expand, tril, seq, dep=None):
    t_tok = z.shape[0]
    nb, nc, row, tail, states_spec, fold, unfold = _ssd_specs(t_tok, seq, False)

    def body(z_ref, xbc_ref, tail_ref, dtr_ref, cw_ref, cb_ref, dtb_ref, alog_ref, dsk_ref, nw_ref, exp_ref,
             tril_ref, o_ref, y_ref, st_ref, pre_ref, state_ref):
        c = pl.program_id(0)

        @pl.when(c == 0)
        def _():
            state_ref[...] = jnp.zeros_like(state_ref)

        m_l, m_r = _lane_masks()
        for s in range(nb):
            pre = _conv_pre(xbc_ref[s], jnp.where(c == 0, 0.0, tail_ref[s]), cw_ref, cb_ref[...])
            pre_ref[s] = pre
            f = _ssd_common(pre, dtr_ref[s], dtb_ref[...], alog_ref[...], exp_ref[...], tril_ref[...])
            act = f["act"]
            xs = act[:, :SSM_WIDTH]
            xdt = xs * f["dt_exp"]
            xw = xdt * f["w_end"]
            state = state_ref[s]
            st_ref[s, 0] = state
            ydiag, yoff, snew = [], [], []
            for g in range(2):
                bg = act[:, 512 + 128 * g:640 + 128 * g].astype(BF16)
                cg = act[:, 768 + 128 * g:896 + 128 * g].astype(BF16)
                cb_mat = _dot(cg, bg, _NT)
                for pr in range(2):
                    h0 = 4 * g + 2 * pr
                    gcat = jnp.concatenate(
                        [(cb_mat * f["decay"][h0]).astype(BF16), (cb_mat * f["decay"][h0 + 1]).astype(BF16)], axis=1)
                    ydiag.append(_dot(gcat, _stack_pair(xdt[:, 64 * h0:64 * h0 + 128], m_l, m_r)))
                yoff.append(_dot(cg, state[:, 256 * g:256 * (g + 1)].astype(BF16)))
                snew.append(_dot(bg, xw[:, 256 * g:256 * (g + 1)].astype(BF16), _TN))
            y = jnp.concatenate(ydiag, axis=1) + f["e"] * jnp.concatenate(yoff, axis=1) + dsk_ref[...] * xs
            state_ref[s] = state * f["cd"] + jnp.concatenate(snew, axis=1)
            y_ref[s] = y
            zv = z_ref[s].astype(F32)
            yg = y * (zv * jax.nn.sigmoid(zv))
            outs = []
            for g in range(2):
                ygg = yg[:, 256 * g:256 * (g + 1)]
                outs.append(ygg * lax.rsqrt(jnp.mean(ygg * ygg, axis=-1, keepdims=True) + EPS))
            o_ref[s] = (jnp.concatenate(outs, axis=1) * nw_ref[...]).astype(BF16)

    consts = [cw, cb, dtb, alog, dskip_exp, nw, expand, tril]
    deps = [] if dep is None else [dep]
    n_in = 4 + len(consts)

    def body_skipping_dep(*refs):
        body(*refs[:n_in], *refs[n_in + len(deps):])

    sd = lambda n, dt: jax.ShapeDtypeStruct((nb, seq, n), dt)
    o, y, states, pre = pl.pallas_call(
        body_skipping_dep, name="ssd_fwd", grid=(nc,),
        out_shape=(sd(SSM_WIDTH, BF16), sd(SSM_WIDTH, F32), jax.ShapeDtypeStruct((nb, nc, N_STATE, SSM_WIDTH), F32),
                   sd(CONV_CH, F32)),
        in_specs=[row(SSM_WIDTH), row(CONV_CH), tail, row(CHUNK)] + [_full(a.shape) for a in consts]
        + [pl.BlockSpec(memory_space=pl.ANY)] * len(deps),
        out_specs=(row(SSM_WIDTH), row(SSM_WIDTH), states_spec, row(CONV_CH)),
        scratch_shapes=[pltpu.VMEM((nb, N_STATE, SSM_WIDTH), F32)],
        compiler_params=_params("arbitrary"))(fold(z), fold(xbc), fold(xbc), fold(dtr), *consts, *deps)
    return unfold(o), unfold(y), states, unfold(pre)


def _out_proj(mix_a, mix_b, w_out, x, g2, g3, tm, dep=None):
    t_tok = x.shape[0]
    deps = [] if dep is None else [dep]

    def body(a_ref, b_ref, w_ref, x_ref, g2_ref, g3_ref, *rest):
        o_ref, x2_ref, h3_ref = rest[-3:]
        o = _dot(a_ref[...], w_ref[0:GM_WIDTH, :]) + _dot(b_ref[...], w_ref[GM_WIDTH:, :])
        o_ref[...] = o
        r2 = lax.rsqrt(jnp.mean(o * o, axis=-1, keepdims=True) + EPS)
        x2 = x_ref[...] + o * r2 * g2_ref[...]
        x2_ref[...] = x2
        r3 = lax.rsqrt(jnp.mean(x2 * x2, axis=-1, keepdims=True) + EPS)
        h3_ref[...] = (x2 * r3 * g3_ref[...]).astype(BF16)

    row = lambda n: pl.BlockSpec((tm, n), lambda i: (i, 0))
    sd = lambda dt: jax.ShapeDtypeStruct((t_tok, D_MODEL), dt)
    return pl.pallas_call(
        body, name="out_proj", grid=(t_tok // tm,), out_shape=(sd(F32), sd(F32), sd(BF16)),
        in_specs=[row(GM_WIDTH), row(SSM_WIDTH), _full((D_MODEL, D_MODEL)), row(D_MODEL), _full((1, D_MODEL)),
                  _full((1, D_MODEL))] + [pl.BlockSpec(memory_space=pl.ANY)] * len(deps),
        out_specs=(row(D_MODEL),) * 3, compiler_params=_params("parallel"))(mix_a, mix_b, w_out, x, g2, g3, *deps)


def _mlp_fwd(h3, w_up, w_down, x2, target, g4, tm, tf):
    t_tok = x2.shape[0]

    def up_body(h_ref, wu_ref, ra_ref):
        ra_ref[...] = jnp.maximum(_dot(h_ref[...], wu_ref[...]), 0.0).astype(BF16)

    tu = min(2 * tm, t_tok)
    ra = pl.pallas_call(
        up_body, name="mlp_up", grid=(D_FF // tf, t_tok // tu), out_shape=jax.ShapeDtypeStruct((t_tok, D_FF), BF16),
        in_specs=[pl.BlockSpec((tu, D_MODEL), lambda j, i: (i, 0)), pl.BlockSpec((D_MODEL, tf), lambda j, i: (0, j))],
        out_specs=pl.BlockSpec((tu, tf), lambda j, i: (i, j)), compiler_params=_params("parallel", "parallel"))(h3, w_up)

    def down_body(ra_ref, wd_ref, x2_ref, t_ref, g4_ref, dd_ref, dy_ref, dg4_ref, loss_ref):
        i = pl.program_id(0)
        rav = ra_ref[...]
        dvec = _dot(rav * rav, wd_ref[...])
        r4 = lax.rsqrt(jnp.mean(dvec * dvec, axis=-1, keepdims=True) + EPS)
        dn = dvec * r4
        g4 = g4_ref[...]
        err = x2_ref[...] + dn * g4 - t_ref[...]
        dy = err * (1.0 / D_MODEL)
        dy_ref[...] = dy
        dg = dy * g4
        dd_ref[...] = (r4 * (dg - dn * jnp.mean(dg * dn, axis=-1, keepdims=True))).astype(BF16)
        _acc_rows(dg4_ref, _rsum(dy * dn), i == 0)
        tile_loss = 0.5 * jnp.sum(jnp.sum(err * err, axis=-1, keepdims=True), axis=0, keepdims=True) / D_MODEL
        _acc_rows(loss_ref, jnp.broadcast_to(tile_loss, (1, 128)), i == 0)

    row = pl.BlockSpec((tm, D_MODEL), lambda i: (i, 0))
    dd, dy, dg4, loss = pl.pallas_call(
        down_body, name="mlp_down", grid=(t_tok // tm,),
        out_shape=(jax.ShapeDtypeStruct((t_tok, D_MODEL), BF16), jax.ShapeDtypeStruct((t_tok, D_MODEL), F32),
                   jax.ShapeDtypeStruct((1, D_MODEL), F32), jax.ShapeDtypeStruct((1, 128), F32)),
        in_specs=[pl.BlockSpec((tm, D_FF), lambda i: (i, 0)), _full((D_FF, D_MODEL)), row, row, _full((1, D_MODEL))],
        out_specs=(row, row, _full((1, D_MODEL)), _full((1, 128))),
        compiler_params=_params("arbitrary"))(ra, w_down, x2, target, g4)
    return ra, dd, dy, dg4, loss


def _mlp_bwd(dd, w_down, ra, w_up, x2, dy, o, g3, g2, tm, tf):
    t_tok = x2.shape[0]

    def hidden_body(dd_ref, wd_ref, ra_ref, da_ref):
        df = _dot(dd_ref[...], wd_ref[...], _NT)
        da_ref[...] = (df * (2.0 * ra_ref[...].astype(F32))).astype(BF16)

    tu = min(2 * tm, t_tok)
    da = pl.pallas_call(
        hidden_body, name="mlp_bwd_hidden", grid=(D_FF // tf, t_tok // tu),
        out_shape=jax.ShapeDtypeStruct((t_tok, D_FF), BF16),
        in_specs=[pl.BlockSpec((tu, D_MODEL), lambda j, i: (i, 0)), pl.BlockSpec((tf, D_MODEL), lambda j, i: (j, 0)),
                  pl.BlockSpec((tu, tf), lambda j, i: (i, j))],
        out_specs=pl.BlockSpec((tu, tf), lambda j, i: (i, j)),
        compiler_params=_params("parallel", "parallel"))(dd, w_down, ra)

    def in_body(da_ref, wu_ref, x2_ref, dy_ref, o_ref, g3_ref, g2_ref, dx2_ref, do_ref, dg3_ref, dg2_ref):
        i = pl.program_id(0)
        dh3 = _dot(da_ref[...], wu_ref[...], _NT)
        dn3, dg3 = _rms_bwd(x2_ref[...], g3_ref[...], dh3)
        dx2 = dy_ref[...] + dn3
        dx2_ref[...] = dx2
        do, dg2 = _rms_bwd(o_ref[...], g2_ref[...], dx2)
        do_ref[...] = do.astype(BF16)
        _acc_rows(dg3_ref, dg3, i == 0)
        _acc_rows(dg2_ref, dg2, i == 0)

    row = pl.BlockSpec((tm, D_MODEL), lambda i: (i, 0))
    vec = _full((1, D_MODEL))
    sd = lambda dt: jax.ShapeDtypeStruct((t_tok, D_MODEL), dt)
    dx2, do, dg3, dg2 = pl.pallas_call(
        in_body, name="mlp_bwd_in", grid=(t_tok // tm,),
        out_shape=(sd(F32), sd(BF16), jax.ShapeDtypeStruct((1, D_MODEL), F32), jax.ShapeDtypeStruct((1, D_MODEL), F32)),
        in_specs=[pl.BlockSpec((tm, D_FF), lambda i: (i, 0)), _full((D_MODEL, D_FF)), row, row, row, vec, vec],
        out_specs=(row, row, vec, vec), compiler_params=_params("arbitrary"))(da, w_up, x2, dy, o, g3, g2)
    return da, dx2, do, dg3, dg2


def _wgrad(a, b, out_blocks, bm, bn, bk, square_a, name, dep=None):
    t_tok, m = a.shape
    n = b.shape[1]
    nk = t_tok // bk

    def body(a_ref, b_ref, *rest):
        o_ref, acc_ref = rest[-2:]
        k = pl.program_id(2)
        av = a_ref[...]
        if square_a:
            av = av * av
        part = _dot(av, b_ref[...], _TN)

        def emit(res):
            if out_blocks is None:
                o_ref[...] = res.astype(BF16)
            else:
                o_ref[0] = res.astype(BF16)

        if nk == 1:
            emit(part)
            return

        @pl.when(k == 0)
        def _():
            acc_ref[...] = part

        @pl.when(k > 0)
        def _():
            acc_ref[...] += part

        @pl.when(k == nk - 1)
        def _():
            emit(acc_ref[...])

    if out_blocks is None:
        out_shape = jax.ShapeDtypeStruct((m, n), BF16)
        out_spec = pl.BlockSpec((bm, bn), lambda i, j, k: (i, j))
    else:
        assert n // out_blocks == bn
        out_shape = jax.ShapeDtypeStruct((out_blocks, m, bn), BF16)
        out_spec = pl.BlockSpec((1, bm, bn), lambda i, j, k: (j, i, 0))
    deps = [] if dep is None else [dep]
    return pl.pallas_call(
        body, name=name, grid=(m // bm, n // bn, nk), out_shape=out_shape,
        in_specs=[pl.BlockSpec((bk, bm), lambda i, j, k: (k, i)), pl.BlockSpec((bk, bn), lambda i, j, k: (k, j))]
        + [pl.BlockSpec(memory_space=pl.ANY)] * len(deps),
        out_specs=out_spec, scratch_shapes=[pltpu.VMEM((bm, bn) if nk > 1 else (8, 128), F32)],
        compiler_params=_params("parallel", "parallel", "arbitrary"))(a, b, *deps)


def _wgrad_in_chunked(h1, pieces, bn, bk, dep=None):
    t_tok = h1.shape[0]
    nk = t_tok // bk
    shard = IN_COLS // N_DEV
    widths = [b - a for a, b in _IN_SPLITS]

    def body(h_ref, *rest):
        piece_refs = rest[:len(widths)]
        o_ref, acc_ref = rest[-2:]
        k = pl.program_id(1)
        hv = h_ref[...]
        for (a, b), r in zip(_IN_SPLITS, piece_refs):
            part = _dot(r[...], hv, _TN)

            @pl.when(k == 0)
            def _():
                acc_ref[a:b, :] = part

            @pl.when(k > 0)
            def _():
                acc_ref[a:b, :] += part

        @pl.when(k == nk - 1)
        def _():
            for j in range(N_DEV):
                o_ref[j] = acc_ref[shard * j:shard * (j + 1), :].astype(BF16)

    deps = [] if dep is None else [dep]
    return pl.pallas_call(
        body, name="wgrad_in", grid=(D_MODEL // bn, nk), out_shape=jax.ShapeDtypeStruct((N_DEV, shard, D_MODEL), BF16),
        in_specs=[pl.BlockSpec((bk, bn), lambda j, k: (k, j))] + [pl.BlockSpec((bk, n), lambda j, k: (k, 0)) for n in widths]
        + [pl.BlockSpec(memory_space=pl.ANY)] * len(deps),
        out_specs=pl.BlockSpec((N_DEV, shard, bn), lambda j, k: (0, 0, j)),
        scratch_shapes=[pltpu.VMEM((IN_PAD, bn), F32)],
        compiler_params=_params("parallel", "arbitrary"))(h1, *pieces, *deps)


def _wgrad_pieces(h1, pieces, bn, name, dep=None):
    t_tok = h1.shape[0]
    widths = [p.shape[1] for p in pieces]
    starts = [sum(widths[:i]) for i in range(len(widths))]

    def body(h_ref, *rest):
        piece_refs = rest[:len(widths)]
        o_ref = rest[-1]
        hv = h_ref[...]
        for a, n, r in zip(starts, widths, piece_refs):
            o_ref[a:a + n, :] = _dot(r[...], hv, _TN).astype(BF16)

    deps = [] if dep is None else [dep]
    return pl.pallas_call(
        body, name=name, grid=(D_MODEL // bn,), out_shape=jax.ShapeDtypeStruct((sum(widths), D_MODEL), BF16),
        in_specs=[pl.BlockSpec((t_tok, bn), lambda j: (0, j))] + [pl.BlockSpec((t_tok, n), lambda j: (0, 0)) for n in widths]
        + [pl.BlockSpec(memory_space=pl.ANY)] * len(deps),
        out_specs=pl.BlockSpec((sum(widths), bn), lambda j: (0, j)),
        compiler_params=_params("parallel"))(h1, *pieces, *deps)


def _dmix(do, w_out, tm, dep=None):
    t_tok = do.shape[0]

    def body(d_ref, w_ref, *rest):
        rest[-1][...] = _dot(d_ref[...], w_ref[...], _NT).astype(BF16)

    row = pl.BlockSpec((tm, D_MODEL), lambda i: (i, 0))
    deps = [] if dep is None else [dep]
    return pl.pallas_call(
        body, name="dmix", grid=(t_tok // tm,), out_shape=jax.ShapeDtypeStruct((t_tok, D_MODEL), BF16),
        in_specs=[row, _full((D_MODEL, D_MODEL))] + [pl.BlockSpec(memory_space=pl.ANY)] * len(deps), out_specs=row,
        compiler_params=_params("parallel"))(do, w_out, *deps)


def _gmlp_bwd(dmix, u, v, lnw, lnb, wcat, wtcat, bias, avg, expand_t):
    t_tok = u.shape[0]
    tm = min(_GMLP_ROWS, t_tok)

    def body(dm_ref, u_ref, v_ref, lnw_ref, lnb_ref, wcat_ref, wtcat_ref, bias_ref, avg_ref, expt_ref, du_ref, dv_ref,
             dw_ref, db_ref, dlnw_ref, dlnb_ref):
        i = pl.program_id(0)
        m_l, m_r = _lane_masks()
        avg = avg_ref[...]
        lnw = lnw_ref[...]
        ug, dug, dvg, rstd, vhat, vn, mixed = _gmlp_common(
            u_ref[...].astype(F32), v_ref[...].astype(F32), lnw, lnb_ref[...], avg, wcat_ref, bias_ref[...], m_l, m_r)
        dya = dm_ref[...].astype(F32)
        du_ref[...] = (dya * mixed * dug).astype(BF16)
        dmixed = dya * ug
        dvn_rows, dws, dbt = [], [None] * N_HEADS, None
        for r in range(tm // CHUNK):
            dvn_cols = []
            for j in range(N_HEADS // 2):
                dmp = dmixed[CHUNK * r:CHUNK * (r + 1), 128 * j:128 * (j + 1)]
                dvn_cols.append(_dot(wtcat_ref[j], _stack_pair(dmp, m_l, m_r)))
                vnp = vn[CHUNK * r:CHUNK * (r + 1), 128 * j:128 * (j + 1)].astype(BF16)
                for i_h, mask in enumerate((m_l, m_r)):
                    part = _dot((dmp * mask).astype(BF16), vnp, _NT)
                    dws[2 * j + i_h] = part if r == 0 else dws[2 * j + i_h] + part
            dvn_rows.append(jnp.concatenate(dvn_cols, axis=1))
            part = _split_dot(dmixed[CHUNK * r:CHUNK * (r + 1), :], expt_ref[...], 2)
            dbt = part if r == 0 else dbt + part
        dvn = jnp.concatenate(dvn_rows, axis=0)
        dvh = dvn * lnw
        dvgel = rstd * (dvh - _head_mean(dvh, avg) - vhat * _head_mean(dvh * vhat, avg))
        dv_ref[...] = (dvgel * dvg).astype(BF16)
        first = i == 0

        @pl.when(first)
        def _():
            for h in range(N_HEADS):
                dw_ref[h] = dws[h]
            db_ref[...] = dbt

        @pl.when(jnp.logical_not(first))
        def _():
            for h in range(N_HEADS):
                dw_ref[h] += dws[h]
            db_ref[...] += dbt

        _acc_rows(dlnw_ref, _rsum(dvn * vhat), first)
        _acc_rows(dlnb_ref, _rsum(dvn), first)

    row = pl.BlockSpec((tm, GM_WIDTH), lambda i: (i, 0))
    consts = [lnw, lnb, wcat, wtcat, bias, avg, expand_t]
    return pl.pallas_call(
        body, name="gmlp_bwd", grid=(t_tok // tm,),
        out_shape=(jax.ShapeDtypeStruct((t_tok, GM_WIDTH), BF16), jax.ShapeDtypeStruct((t_tok, GM_WIDTH), BF16),
                   jax.ShapeDtypeStruct((N_HEADS, CHUNK, CHUNK), F32), jax.ShapeDtypeStruct((CHUNK, CHUNK), F32),
                   jax.ShapeDtypeStruct((1, GM_WIDTH), F32), jax.ShapeDtypeStruct((1, GM_WIDTH), F32)),
        in_specs=[row, row, row] + [_full(a.shape) for a in consts],
        out_specs=(row, row, _full((N_HEADS, CHUNK, CHUNK)), _full((CHUNK, CHUNK)), _full((1, GM_WIDTH)),
                   _full((1, GM_WIDTH))),
        compiler_params=_params("arbitrary"))(dmix, u, v, *consts)


def _ssd_bwd(dmix, z, xbc, pre, dtr, y, states, cw, cb, dtb, alog, dskip_exp, nw, expand, expand_t, tril, triu, seq,
             dep=None):
    t_tok = z.shape[0]
    nb, nc, row, _, states_spec, fold, unfold = _ssd_specs(t_tok, seq, True)
    q = CHUNK

    def one_sequence(s, dm_ref, z_ref, xbc_ref, pre_ref, dtr_ref, y_ref, st_ref, cw_ref, dtb_ref, alog_ref, dsk_ref,
                     nw_ref, exp_ref, expt_ref, tril_ref, triu_ref, dz_ref, dxbc_ref, ddt_ref, dhead_ref, dstate_ref):
        m_l, m_r = _lane_masks()
        expt = expt_ref[...]
        f = _ssd_common(pre_ref[s], dtr_ref[s], dtb_ref[...], alog_ref[...], exp_ref[...], tril_ref[...])
        act, pre, sg = f["act"], f["pre"], f["sg"]
        xs = act[:, :SSM_WIDTH]
        xdt = xs * f["dt_exp"]
        xw = xdt * f["w_end"]
        state = st_ref[s, 0]
        dstate = dstate_ref[s]
        zv, yv, dout, nw = z_ref[s].astype(F32), y_ref[s], dm_ref[s].astype(F32), nw_ref[...]
        sz = jax.nn.sigmoid(zv)
        sl = zv * sz
        yg = yv * sl
        tv = dout * nw
        dyg_parts, ygh_parts = [], []
        for g in range(2):
            ygg = yg[:, 256 * g:256 * (g + 1)]
            rr = lax.rsqrt(jnp.mean(ygg * ygg, axis=-1, keepdims=True) + EPS)
            ygh = ygg * rr
            tg = tv[:, 256 * g:256 * (g + 1)]
            dyg_parts.append(rr * (tg - ygh * jnp.mean(tg * ygh, axis=-1, keepdims=True)))
            ygh_parts.append(ygh)
        dyg = jnp.concatenate(dyg_parts, axis=1)
        dnw = _rsum(dout * jnp.concatenate(ygh_parts, axis=1))
        dy = dyg * sl
        dz_ref[s] = (dyg * yv * (sz * (1.0 + zv * (1.0 - sz)))).astype(BF16)
        ddsk = _rsum(dy * xs)
        dye = dy * f["e"]
        lane = lax.broadcasted_iota(jnp.int32, (q, q), 1)
        sub = lax.broadcasted_iota(jnp.int32, (q, q), 0)
        rs_mat = jnp.zeros((q, q), F32)
        cs_mat = jnp.zeros((q, q), F32)
        dxdt_cols, yoff, dst_in, dxw, d_b, d_c = [], [], [], [], [], []
        for g in range(2):
            bg = act[:, 512 + 128 * g:640 + 128 * g].astype(BF16)
            cg = act[:, 768 + 128 * g:896 + 128 * g].astype(BF16)
            cb_mat = _dot(cg, bg, _NT)
            stg = state[:, 256 * g:256 * (g + 1)].astype(BF16)
            dyeg = dye[:, 256 * g:256 * (g + 1)].astype(BF16)
            yoff.append(_dot(cg, stg))
            dcg = _dot(dyeg, stg, _NT)
            dst_in.append(_dot(cg, dyeg, _TN))
            dcb = jnp.zeros((q, q), F32)
            for pr in range(2):
                h0 = 4 * g + 2 * pr
                gf = [cb_mat * f["decay"][h0], cb_mat * f["decay"][h0 + 1]]
                gcat = jnp.concatenate([gf[0].astype(BF16), gf[1].astype(BF16)], axis=1)
                xst = _stack_pair(xdt[:, 64 * h0:64 * h0 + 128], m_l, m_r)
                dyp = dy[:, 64 * h0:64 * h0 + 128].astype(BF16)
                dgcat = _dot(dyp, xst, _NT)
                dxst = _dot(gcat, dyp, _TN)
                dxdt_cols.append(dxst[:q] * m_l + dxst[q:] * m_r)
                for i in range(2):
                    h = h0 + i
                    dg = dgcat[:, q * i:q * (i + 1)]
                    mm = dg * gf[i]
                    rs_mat = rs_mat + jnp.where(lane == h, jnp.sum(mm, axis=1, keepdims=True), 0.0)
                    cs_mat = cs_mat + jnp.where(sub == h, jnp.sum(mm, axis=0, keepdims=True), 0.0)
                    dcb = dcb + dg * f["decay"][h]
            dcb16 = dcb.astype(BF16)
            dstg = dstate[:, 256 * g:256 * (g + 1)].astype(BF16)
            d_c.append(dcg + _dot(dcb16, bg))
            dxw.append(_dot(bg, dstg))
            d_b.append(_dot(dcb16, cg, _TN) + _dot(xw[:, 256 * g:256 * (g + 1)].astype(BF16), dstg, _NT))
        dxw = jnp.concatenate(dxw, axis=1)
        dxdt = jnp.concatenate(dxdt_cols, axis=1) + dxw * f["w_end"]
        qv = dxw * xw
        end_row = _rsum(qv) + _rsum(dstate * state) * f["cd"]
        x2 = dye * jnp.concatenate(yoff, axis=1) - qv
        row_i = lax.broadcasted_iota(jnp.int32, (q, 1), 0)
        x2 = x2 + jnp.where(row_i == q - 1, end_row, 0.0)
        da_cs = _split_dot(x2, expt, 2) + rs_mat - cs_mat.T
        ddt = _split_dot(dxdt * xs, expt, 2)
        dxs = dsk_ref[...] * dy + dxdt * f["dt_exp"]
        dda = _split_dot_left(triu_ref[...], da_cs, 3)
        ddt = ddt + dda * f["a_row"]
        dalog = _rsum(dda * f["dt"]) * f["a_row"]
        draw = ddt * jax.nn.sigmoid(f["dtp"])
        ddt_ref[s] = draw.astype(BF16)
        dact = jnp.concatenate([dxs] + d_b + d_c, axis=1)
        dpre = dact * (sg * (1.0 + pre * (1.0 - sg)))
        dhead = dhead_ref[s]
        xv = xbc_ref[s]
        shifted = [_shift_rows(dpre, dhead, 3 - k, False) for k in range(3)] + [dpre]
        dxbc = cw_ref[3:4, :] * dpre
        for k in range(3):
            dxbc = dxbc + cw_ref[k:k + 1, :] * shifted[k]
        dxbc_ref[s] = dxbc.astype(BF16)
        dhead_ref[s] = dpre[0:8, :]
        dstate_ref[s] = dstate * f["cd"] + jnp.concatenate(dst_in, axis=1)
        row8 = lax.broadcasted_iota(jnp.int32, (8, 1), 0)
        dcw = jnp.zeros((8, CONV_CH), F32)
        for k in range(4):
            dcw = dcw + jnp.where(row8 == k, _rsum(shifted[k] * xv), 0.0)
        return dcw, _rsum(dpre), _rsum(draw), dalog, _split_dot(ddsk, expt, 3), dnw

    def body(dm_ref, z_ref, xbc_ref, pre_ref, dtr_ref, y_ref, st_ref, cw_ref, cb_ref, dtb_ref, alog_ref, dsk_ref,
             nw_ref, exp_ref, expt_ref, tril_ref, triu_ref, dz_ref, dxbc_ref, ddt_ref, dcw_ref, dcb_ref, ddtb_ref,
             dalog_ref, dd_ref, dnw_ref, dhead_ref, dstate_ref):
        c = pl.program_id(0)
        first = c == 0

        @pl.when(first)
        def _():
            dstate_ref[...] = jnp.zeros_like(dstate_ref)
            dhead_ref[...] = jnp.zeros_like(dhead_ref)

        total = None
        for s in range(nb):
            parts = one_sequence(s, dm_ref, z_ref, xbc_ref, pre_ref, dtr_ref, y_ref, st_ref, cw_ref, dtb_ref, alog_ref,
                                 dsk_ref, nw_ref, exp_ref, expt_ref, tril_ref, triu_ref, dz_ref, dxbc_ref, ddt_ref,
                                 dhead_ref, dstate_ref)
            total = parts if total is None else tuple(a + b for a, b in zip(total, parts))
        dcw = total[0]

        @pl.when(first)
        def _():
            dcw_ref[...] = dcw

        @pl.when(jnp.logical_not(first))
        def _():
            dcw_ref[...] += dcw

        for ref, part in zip((dcb_ref, ddtb_ref, dalog_ref, dd_ref, dnw_ref), total[1:]):
            _acc_rows(ref, part, first)

    consts = [cw, cb, dtb, alog, dskip_exp, nw, expand, expand_t, tril, triu]
    deps = [] if dep is None else [dep]
    n_in = 7 + len(consts)

    def body_skipping_dep(*refs):
        body(*refs[:n_in], *refs[n_in + len(deps):])

    acc = lambda n: jax.ShapeDtypeStruct((1, n), F32)
    sd = lambda n: jax.ShapeDtypeStruct((nb, seq, n), BF16)
    dz, dxbc, ddt, *small_grads = pl.pallas_call(
        body_skipping_dep, name="ssd_bwd", grid=(nc,),
        out_shape=(sd(SSM_WIDTH), sd(CONV_CH), sd(CHUNK), jax.ShapeDtypeStruct((8, CONV_CH), F32), acc(CONV_CH),
                   acc(CHUNK), acc(CHUNK), acc(CHUNK), acc(SSM_WIDTH)),
        in_specs=[row(SSM_WIDTH, col=1), row(SSM_WIDTH), row(CONV_CH), row(CONV_CH), row(CHUNK), row(SSM_WIDTH),
                  states_spec]
        + [_full(a.shape) for a in consts] + [pl.BlockSpec(memory_space=pl.ANY)] * len(deps),
        out_specs=(row(SSM_WIDTH), row(CONV_CH), row(CHUNK), _full((8, CONV_CH)), _full((1, CONV_CH)),
                   _full((1, CHUNK)), _full((1, CHUNK)), _full((1, CHUNK)), _full((1, SSM_WIDTH))),
        scratch_shapes=[pltpu.VMEM((nb, 8, CONV_CH), F32), pltpu.VMEM((nb, N_STATE, SSM_WIDTH), F32)],
        compiler_params=_params("arbitrary"))(
            fold(dmix), fold(z), fold(xbc), fold(pre), fold(dtr), fold(y), states, *consts, *deps)
    return (unfold(dz), unfold(dxbc), unfold(ddt), *small_grads)


def _in_bwd(du, dv, dz, dxbc, ddt, w_in, x, dx2, g1, tm, me, riders=(), dep=None):
    t_tok = x.shape[0]
    steps = t_tok // tm

    n_in = [5 + ("mask" in rd) for rd in riders]
    first_in = [sum(n_in[:r]) for r in range(len(riders))]

    def body(me_ref, du_ref, dv_ref, dz_ref, dxbc_ref, ddt_ref, w_ref, x_ref, dx2_ref, g_ref, *rest):
        outs = rest[len(rest) - 2 - 4 * len(riders):]
        gx_ref, dg_ref = outs[:2]
        i = pl.program_id(0)
        dh = None
        for (a, b), ref in zip(_IN_SPLITS, (du_ref, dv_ref, dz_ref, dxbc_ref, ddt_ref)):
            part = _dot(ref[...], w_ref[a:b, :])
            dh = part if dh is None else dh + part
        dn, dg = _rms_bwd(x_ref[...], g_ref[...], dh)
        gx_ref[...] = dx2_ref[...] + dn
        _acc_rows(dg_ref, dg, i == 0)
        for r in range(len(riders)):
            p_ref, own_ref, w_ref_r, m_ref_r, v_ref_r = rest[first_in[r]:first_in[r] + 5]
            g = _sum_parts(me_ref[0], p_ref, own_ref[0])
            if n_in[r] == 6:
                g = g * rest[first_in[r] + 5][...]
            d, mn, vn = _adamw_math(w_ref_r[...], g, m_ref_r[...], v_ref_r[...])
            for o_ref, val in zip(outs[2 + 4 * r:6 + 4 * r], (g, d, mn, vn)):
                o_ref[...] = val

    row = lambda n: pl.BlockSpec((tm, n), lambda i, me_ref: (i, 0))
    whole = lambda shape: pl.BlockSpec(shape, lambda i, me_ref: (0,) * len(shape))
    widths = [b - a for a, b in _IN_SPLITS]
    deps = [] if dep is None else [dep]
    rider_args, rider_specs, rider_out_shapes, rider_out_specs = [], [], [], []
    for rd in riders:
        rows, cols = rd["w"].shape[0] // steps, rd["w"].shape[1]
        blk = pl.BlockSpec((rows, cols), lambda i, me_ref: (i, 0))
        rider_args += [rd["parts"], rd["own"], rd["w"], rd["m"], rd["v"]]
        rider_specs += [pl.BlockSpec((N_DEV, rows, cols), lambda i, me_ref: (0, i, 0)),
                        pl.BlockSpec((1, rows, cols), lambda i, me_ref: (me_ref[0], i, 0)), blk, blk, blk]
        if "mask" in rd:
            rider_args.append(rd["mask"])
            rider_specs.append(whole((rows, cols)))
        rider_out_shapes += [jax.ShapeDtypeStruct(rd["w"].shape, F32)] * 4
        rider_out_specs += [blk] * 4
    outs = pl.pallas_call(
        body, name="in_bwd",
        out_shape=(jax.ShapeDtypeStruct((t_tok, D_MODEL), F32), jax.ShapeDtypeStruct((1, D_MODEL), F32),
                   *rider_out_shapes),
        grid_spec=pltpu.PrefetchScalarGridSpec(
            num_scalar_prefetch=1, grid=(steps,),
            in_specs=[row(n) for n in widths] + [whole((IN_PAD, D_MODEL)), row(D_MODEL), row(D_MODEL),
                                                 whole((1, D_MODEL))] + rider_specs
            + [pl.BlockSpec(memory_space=pl.ANY)] * len(deps),
            out_specs=(row(D_MODEL), whole((1, D_MODEL)), *rider_out_specs)),
        compiler_params=_params("arbitrary"))(me, du, dv, dz, dxbc, ddt, w_in, x, dx2, g1, *rider_args, *deps)
    return outs[0], outs[1], [tuple(outs[2 + 4 * r:6 + 4 * r]) for r in range(len(riders))]


def _pad_lanes(a, n):
    return jnp.pad(a, ((0, 0), (0, n - a.shape[1])))


def _local_step(x, target, seq, small, hooks, first_dep=None):
    t_tok = x.shape[0]
    tm = min(TOKEN_TILE, t_tok)
    avg, expand, expand_t, tril, triu = _const_mats()
    g1, g2, g3, g4 = (small[k].reshape(1, D_MODEL) for k in
                      ("norm_mix_pre", "norm_mix_post", "norm_ffn_pre", "norm_ffn_post"))
    tie = (lambda a: a) if first_dep is None else (lambda a: a + first_dep[0, 0])
    lnw = tie(small["gm_ln_w"]).reshape(1, GM_WIDTH)
    lnb = tie(small["gm_ln_b"]).reshape(1, GM_WIDTH)
    causal = jnp.tril(jnp.ones((CHUNK, CHUNK), F32))
    wm = tie(small["gm_w_s"]) * causal
    pair = lambda w: w.reshape(4, 2, CHUNK, CHUNK).transpose(0, 2, 1, 3).reshape(4, CHUNK, 2 * CHUNK).astype(BF16)
    wcat = pair(wm)
    wtcat = pair(jnp.swapaxes(wm, 1, 2))
    bias = jnp.repeat(tie(small["gm_b_s"]).T, HEAD_DIM, axis=1)
    cb = small["conv_b"].reshape(1, CONV_CH)
    dtb = _pad_lanes(tie(small["dt_bias"]).reshape(1, N_HEADS), CHUNK)
    alog = _pad_lanes(tie(small["a_log"]).reshape(1, N_HEADS), CHUNK)
    dskip_exp = jnp.repeat(tie(small["d_skip"]).reshape(1, N_HEADS), HEAD_DIM, axis=1)
    nw = small["ssm_norm_w"].reshape(1, SSM_WIDTH)

    h1 = _prenorm(x, g1, tm, hooks.get("prenorm_after", first_dep))
    w_in_t, conv_w = hooks["mixer_weights"]((h1, lnw, lnb, wcat, wtcat, bias, dtb, alog, dskip_exp))
    tall = min(2 * tm, t_tok)
    u, v, z, xbc, dtr = _in_proj(h1, w_in_t, tall)
    mix_a = _gmlp_fwd(u, v, lnw, lnb, wcat, bias, avg)
    dep = hooks["gmlp_done"](mix_a) if "gmlp_done" in hooks else None
    mix_b, y_pre, states, pre = _ssd_fwd(z, xbc, dtr, conv_w, cb, dtb, alog, dskip_exp, nw, expand, tril, seq, dep)
    w_out, dep = hooks["mixers_done"](mix_b)
    o, x2, h3 = _out_proj(mix_a, mix_b, w_out, x, g2, g3, tall, dep)
    w_up, w_down = hooks["mlp_weights"](h3)
    tf = FF_TILE
    ra, dd, dy, dg4, loss = _mlp_fwd(h3, w_up, w_down, x2, target, g4, tm, tf)

    da, dx2, do, dg3, dg2 = _mlp_bwd(dd, w_down, ra, w_up, x2, dy, o, g3, g2, tm, tf)
    g_w_down = _wgrad(ra, dd, None, WGRAD_TILE, D_MODEL, t_tok, True, "wgrad_down")
    g_w_up = _wgrad(h3, da, N_DEV, D_MODEL, D_FF // N_DEV, t_tok, False, "wgrad_up")
    dep = hooks["mlp_grads"](g_w_down, g_w_up)
    dmix = _dmix(do, w_out, tall, dep)
    g_w_out = _wgrad_pieces(do, (mix_a, mix_b), WGRAD_TILE, "wgrad_out", dep)
    du, dv, dws, dbt, dlnw, dlnb = _gmlp_bwd(dmix, u, v, lnw, lnb, wcat, wtcat, bias, avg, expand_t)
    dep = hooks["gmlp_grads"](g_w_out, dws)
    dz, dxbc, ddt, dcw, dcb, ddtb, dalog, ddsk, dnw = _ssd_bwd(
        dmix, z, xbc, pre, dtr, y_pre, states, conv_w, cb, dtb, alog, dskip_exp, nw, expand, expand_t, tril, triu, seq,
        dep)
    g_w_in = _wgrad_in_chunked(h1, (du, dv, dz, dxbc, ddt), WGRAD_TILE, t_tok // 2, dep)
    dep = hooks["in_grads"](g_w_in, dcw[0:4])
    riders = hooks["arrived_updates"](dep) if "arrived_updates" in hooks else []
    me = hooks.get("me", jnp.zeros((1,), jnp.int32))
    grad_x, dg1, updates = _in_bwd(du, dv, dz, dxbc, ddt, w_in_t, x, dx2, g1, tm, me, riders, dep)

    grads = dict(
        updates=updates,
        w_in=g_w_in, w_out=g_w_out, w_up=g_w_up, w_down=g_w_down, conv_w=dcw[0:4],
        norm_mix_pre=dg1, norm_mix_post=dg2, norm_ffn_pre=dg3, norm_ffn_post=dg4, gm_ln_w=dlnw, gm_ln_b=dlnb,
        gm_w_s=dws, gm_b_s=dbt, conv_b=dcb, dt_bias=ddtb, a_log=dalog, d_skip=ddsk, ssm_norm_w=dnw)
    return loss[0, 0], grad_x, grads


_WEIGHTS = ("norm_mix_pre", "w_in", "gm_ln_w", "gm_ln_b", "gm_w_s", "gm_b_s", "conv_w", "conv_b", "dt_bias", "a_log",
            "d_skip", "ssm_norm_w", "w_out", "norm_mix_post", "norm_ffn_pre", "w_up", "w_down", "norm_ffn_post")
_SLAB_ROWS = (("norm_mix_pre", 1024), ("norm_mix_post", 1024), ("norm_ffn_pre", 1024), ("norm_ffn_post", 1024),
              ("conv_b", 1024), ("ssm_norm_w", 512), ("gm_ln_w", 512), ("gm_ln_b", 512), ("dt_bias", 8), ("a_log", 8),
              ("d_skip", 8))
_SLAB_LOSS_ROW = len(_SLAB_ROWS)
_SLAB_BS_ROW = 16
_SMALL_PARAMS = tuple(name for name, _ in _SLAB_ROWS) + ("gm_b_s",)
_LN_PARAMS = ("gm_ln_w", "gm_ln_b")


_SLAB_CONV_ROW = _SLAB_LOSS_ROW + 1


def _pack_slab(g, loss_part):
    rows = [_pad_lanes(g[name], D_MODEL) for name, _ in _SLAB_ROWS]
    rows.append(jnp.broadcast_to(loss_part, (1, D_MODEL)))
    rows.append(g["conv_w"])
    assert sum(r.shape[0] for r in rows) == _SLAB_BS_ROW
    rows.append(_pad_lanes(g["gm_b_s"].T[0:N_HEADS], D_MODEL))
    return jnp.concatenate(rows, axis=0)


def _adamw_slab(parts, me, w, m, v):
    names = _SMALL_PARAMS + ("conv_w",)
    shapes = [w[k].shape for k in names]
    unfold = np.zeros((GM_WIDTH, HEAD_DIM), np.float32)
    for h in range(N_HEADS):
        unfold[h * HEAD_DIM:(h + 1) * HEAD_DIM, :] = np.eye(HEAD_DIM)
    unfold = jnp.asarray(unfold, dtype=BF16)
    n = len(names)
    shard = CONV_CH // N_DEV

    def body(me_ref, p_ref, unfold_ref, *refs):
        w_refs, m_refs, v_refs = refs[:n], refs[n:2 * n], refs[2 * n:3 * n]
        outs = refs[3 * n:]
        g_all = p_ref[0]
        for j in range(1, N_DEV):
            g_all = g_all + p_ref[j]
        lane = lax.broadcasted_iota(jnp.int32, (N_HEADS, GM_WIDTH), 1)
        head = lax.broadcasted_iota(jnp.int32, (N_HEADS, GM_WIDTH), 0)
        own_lanes = jnp.logical_and(lane >= head * HEAD_DIM, lane < (head + 1) * HEAD_DIM)
        mine = pl.ds(pl.multiple_of(me_ref[0] * shard, shard), shard)
        for i, name in enumerate(names):
            if name == "gm_b_s":
                g = g_all[_SLAB_BS_ROW:_SLAB_BS_ROW + N_HEADS, 0:CHUNK]
            elif name == "conv_w":
                g = p_ref[0, _SLAB_CONV_ROW:_SLAB_CONV_ROW + 4, mine]
                for j in range(1, N_DEV):
                    g = g + p_ref[j, _SLAB_CONV_ROW:_SLAB_CONV_ROW + 4, mine]
            else:
                row = [r for r, (k, _) in enumerate(_SLAB_ROWS) if k == name][0]
                g = g_all[row:row + 1, 0:dict(_SLAB_ROWS)[name]]
                if name in _LN_PARAMS:
                    g = _split_dot(jnp.where(own_lanes, g, 0.0), unfold_ref[...], 3)
            d, mn, vn = _adamw_math(w_refs[i][...], g, m_refs[i][...], v_refs[i][...])
            for o_ref, val in zip(outs[4 * i:4 * i + 4], (g, d, mn, vn)):
                o_ref[...] = val
        outs[-1][...] = g_all[_SLAB_LOSS_ROW:_SLAB_LOSS_ROW + 1, 0:128]

    def whole(shape):
        nd = len(shape)
        return pl.BlockSpec(shape, lambda i, me_ref: (0,) * nd)

    ins = [parts, unfold] + [d[k] for d in (w, m, v) for k in names]
    out_shape = tuple(jax.ShapeDtypeStruct(s, F32) for s in shapes for _ in range(4)) + (
        jax.ShapeDtypeStruct((1, 128), F32),)
    outs = pl.pallas_call(
        body, name="adamw_small", out_shape=out_shape,
        grid_spec=pltpu.PrefetchScalarGridSpec(
            num_scalar_prefetch=1, grid=(1,), in_specs=[whole(a.shape) for a in ins],
            out_specs=tuple(whole(s.shape) for s in out_shape)),
        compiler_params=_params("arbitrary"))(me, *ins)
    return {k: tuple(outs[4 * i:4 * i + 4]) for i, k in enumerate(names)}, outs[-1][0, 0]


def kernel(x, norm_mix_pre, w_in, gm_ln_w, gm_ln_b, gm_w_s, gm_b_s, conv_w, conv_b, dt_bias, a_log, d_skip, ssm_norm_w, w_out, norm_mix_post, norm_ffn_pre, w_up, w_down, norm_ffn_post, loss_target, m_norm_mix_pre, m_w_in, m_gm_ln_w, m_gm_ln_b, m_gm_w_s, m_gm_b_s, m_conv_w, m_conv_b, m_dt_bias, m_a_log, m_d_skip, m_ssm_norm_w, m_w_out, m_norm_mix_post, m_norm_ffn_pre, m_w_up, m_w_down, m_norm_ffn_post, v_norm_mix_pre, v_w_in, v_gm_ln_w, v_gm_ln_b, v_gm_w_s, v_gm_b_s, v_conv_w, v_conv_b, v_dt_bias, v_a_log, v_d_skip, v_ssm_norm_w, v_w_out, v_norm_mix_post, v_norm_ffn_pre, v_w_up, v_w_down, v_norm_ffn_post):
    w = dict(norm_mix_pre=norm_mix_pre, w_in=w_in, gm_ln_w=gm_ln_w, gm_ln_b=gm_ln_b, gm_w_s=gm_w_s, gm_b_s=gm_b_s, conv_w=conv_w, conv_b=conv_b, dt_bias=dt_bias, a_log=a_log, d_skip=d_skip, ssm_norm_w=ssm_norm_w, w_out=w_out, norm_mix_post=norm_mix_post, norm_ffn_pre=norm_ffn_pre, w_up=w_up, w_down=w_down, norm_ffn_post=norm_ffn_post)
    m = dict(norm_mix_pre=m_norm_mix_pre, w_in=m_w_in, gm_ln_w=m_gm_ln_w, gm_ln_b=m_gm_ln_b, gm_w_s=m_gm_w_s, gm_b_s=m_gm_b_s, conv_w=m_conv_w, conv_b=m_conv_b, dt_bias=m_dt_bias, a_log=m_a_log, d_skip=m_d_skip, ssm_norm_w=m_ssm_norm_w, w_out=m_w_out, norm_mix_post=m_norm_mix_post, norm_ffn_pre=m_norm_ffn_pre, w_up=m_w_up, w_down=m_w_down, norm_ffn_post=m_norm_ffn_post)
    v = dict(norm_mix_pre=v_norm_mix_pre, w_in=v_w_in, gm_ln_w=v_gm_ln_w, gm_ln_b=v_gm_ln_b, gm_w_s=v_gm_w_s, gm_b_s=v_gm_b_s, conv_w=v_conv_w, conv_b=v_conv_b, dt_bias=v_dt_bias, a_log=v_a_log, d_skip=v_d_skip, ssm_norm_w=v_ssm_norm_w, w_out=v_w_out, norm_mix_post=v_norm_mix_post, norm_ffn_pre=v_norm_ffn_pre, w_up=v_w_up, w_down=v_w_down, norm_ffn_post=v_norm_ffn_post)
    n_batch, seq, _ = x.shape
    shard_in = IN_COLS // N_DEV

    me = (4 * lax.axis_index("x") + 2 * lax.axis_index("y") + lax.axis_index("c")).astype(jnp.int32).reshape(1)

    def in_slot(own):
        return lax.dynamic_update_slice(lax.empty((N_DEV,) + own.shape, own.dtype), own[None],
                                        (me[0],) + (0,) * own.ndim)

    lying = lambda t: jnp.transpose(t, (2, 0, 1))
    first = [_cast_to_slot(lying(w_in), me, shard_in, "cast_w_in"), in_slot(conv_w[0])]
    ici_1, tok_ici_1 = _exchange_start(first, [True] * 2, _SAME_CORE_PEERS, "gather_mix_ici_start")
    cast_out = _cast_to_slot(w_out[0], me, 128, "cast_w_out", dep=tok_ici_1)
    cast_up = _cast_to_slot(w_up[0], me, 1024, "cast_w_up", cols=True, dep=cast_out)
    second = [cast_out, cast_up, _cast_to_slot(w_down[0], me, 512, "cast_w_down", dep=cast_up)]
    gathering = {}

    def mixer_weights(after):
        bufs = [buf for buf, _ in _exchange_wait(ici_1, after, "gather_mix_ici_wait")]
        d2d_1, tok_d2d_1 = _exchange_start(bufs, [True] * 2, _SIBLING_FORWARD, "gather_mix_d2d_start")
        gathering["late_ici"], tok_ici_2 = _exchange_start(
            second, [True] * 3, _SAME_CORE_PEERS, "gather_late_ici_start", dep=tok_d2d_1)
        (_, ag_in), (_, ag_conv) = _exchange_wait(d2d_1, tok_ici_2, "gather_mix_d2d_wait")
        w_in_t = _stack_shards(ag_in, IN_PAD, STACK_TILE, "stack_w_in")
        return w_in_t, ag_conv.transpose(1, 0, 2).reshape(4, CONV_CH)

    def gmlp_done(after):
        ((buf, _),) = _exchange_wait(gathering["late_ici"], after, "gather_out_ici_wait", only=(0,))
        gathering["out"], tok = _exchange_start([buf], [True], _SIBLING_FORWARD, "gather_out_d2d_start")
        return tok

    def mixers_done(after):
        bufs = [buf for buf, _ in _exchange_wait(gathering["late_ici"], after, "gather_mlp_ici_wait", only=(1, 2))]
        gathering["mlp"], tok = _exchange_start(bufs, [True] * 2, _SIBLING_FORWARD, "gather_mlp_d2d_start")
        ((_, ag_out),) = _exchange_wait(gathering["out"], tok, "gather_out_d2d_wait")
        return ag_out.reshape(D_MODEL, D_MODEL), tok

    def mlp_weights(after):
        (_, ag_up), (_, ag_down) = _exchange_wait(gathering["mlp"], after, "gather_mlp_d2d_wait")
        return ag_up, ag_down.reshape(D_FF, D_MODEL)

    sent = {}

    def mlp_grads(g_w_down, g_w_up):
        sent["mlp"], tok = _exchange_start(
            [g_w_down.reshape(N_DEV, D_FF // N_DEV, D_MODEL), g_w_up], [False, False], _ALL_PEERS, "grads_mlp_start")
        return tok

    def gmlp_grads(g_w_out, g_w_s):
        sent["gmlp"], tok = _exchange_start(
            [g_w_out.reshape(N_DEV, D_MODEL // N_DEV, D_MODEL), in_slot(g_w_s.astype(BF16))], [False, True], _ALL_PEERS,
            "grads_gmlp_start")
        return tok

    def in_grads(g_w_in_t, g_conv_w):
        sent["in"], tok = _exchange_start([g_w_in_t], [False], _ALL_PEERS, "grads_in_start")
        return tok

    def arrived_updates(after):
        (own_down, p_down), (own_up, p_up) = _exchange_wait(sent["mlp"], after, "grads_mlp_wait")
        (own_out, p_out), (_, p_ws) = _exchange_wait(sent["gmlp"], own_up, "grads_gmlp_wait")
        rows = lambda t: t.reshape(t.shape[:-3] + (N_HEADS * CHUNK, CHUNK))
        return [dict(parts=p_up, own=own_up, w=w_up[0], m=m_w_up[0], v=v_w_up[0]),
                dict(parts=p_down, own=own_down, w=w_down[0], m=m_w_down[0], v=v_w_down[0]),
                dict(parts=p_out, own=own_out, w=w_out[0], m=m_w_out[0], v=v_w_out[0]),
                dict(parts=rows(p_ws), own=rows(p_ws), w=rows(gm_w_s[0]), m=rows(m_gm_w_s[0]), v=rows(v_gm_w_s[0]),
                     mask=jnp.tril(jnp.ones((CHUNK, CHUNK), F32)))]

    small = {k: w[k][0] for k in _SMALL_PARAMS + ("gm_w_s",)}
    loss_part, grad_x, g = _local_step(
        x.reshape(n_batch * seq, D_MODEL), loss_target.reshape(n_batch * seq, D_MODEL), seq, small,
        dict(mixer_weights=mixer_weights, gmlp_done=gmlp_done, mixers_done=mixers_done, mlp_weights=mlp_weights,
             mlp_grads=mlp_grads, gmlp_grads=gmlp_grads, in_grads=in_grads, arrived_updates=arrived_updates, me=me,
             prenorm_after=second[2]), first_dep=tok_ici_1)

    sent_rows, tok_rows = _exchange_start([in_slot(_pack_slab(g, loss_part))], [True], _ALL_PEERS, "grads_rows_start")
    res = dict(zip(("w_up", "w_down", "w_out", "gm_w_s"), g["updates"]))
    ((own_in, p_in),) = _exchange_wait(sent["in"], tok_rows, "grads_in_wait")
    upd_in = _adamw_reduce(p_in, own_in, me, lying(w_in), lying(m_w_in), lying(v_w_in), "adamw_w_in")
    res["w_in"] = tuple(jnp.transpose(t, (1, 2, 0)) for t in upd_in)
    ((_, p_rows),) = _exchange_wait(sent_rows, upd_in[1], "grads_rows_wait")
    flat = lambda t: t[0] if t.ndim == 3 else t
    small_res, loss = _adamw_slab(
        p_rows, me, *({k: flat(d[k]) for k in _SMALL_PARAMS + ("conv_w",)} for d in (w, m, v)))
    res.update(small_res)
    res = {k: tuple(r.reshape(w[k].shape) for r in res[k]) for k in _WEIGHTS}

    outs = [loss, grad_x.reshape(x.shape)]
    for part in range(4):
        outs.extend(res[k][part] for k in _WEIGHTS)
    return tuple(outs)
```

```python
import functools

import jax
import jax.numpy as jnp
import numpy as np
from jax import lax
from jax.experimental import pallas as pl
from jax.experimental.pallas import tpu as pltpu

F32 = jnp.float32
BF16 = jnp.bfloat16

D_MODEL = 1024
GM_WIDTH = 512
SSM_WIDTH = 512
CONV_CH = 1024
N_HEADS = 8
HEAD_DIM = 64
N_STATE = 128
CHUNK = 128
D_FF = 4096
IN_COLS = 2568
IN_PAD = 2688
N_DEV = 8
EPS = 1e-6
ADAM_LR, ADAM_B1, ADAM_B2, ADAM_EPS, ADAM_WD, ADAM_STEP = 0.001, 0.9, 0.999, 1e-08, 0.01, 10
VMEM_LIMIT_BYTES = 56 * 1024 * 1024
TOKEN_TILE = 512
FF_TILE = 2048
WGRAD_TILE = 512
STACK_TILE = 256
_NT = (((1,), (1,)), ((), ()))
_TN = (((0,), (0,)), ((), ()))


def _params(*sem):
    return pltpu.CompilerParams(dimension_semantics=sem or None, vmem_limit_bytes=VMEM_LIMIT_BYTES)


def _dot(a, b, dims=None):
    if dims is None:
        return jnp.dot(a, b, preferred_element_type=F32)
    return lax.dot_general(a, b, dims, preferred_element_type=F32)


def _split_terms(x, terms):
    out, rem = [], x
    for i in range(terms):
        hi = rem.astype(BF16)
        out.append(hi)
        if i + 1 < terms:
            rem = rem - hi.astype(F32)
    return out


def _split_dot(x, m, terms):
    acc = None
    for hi in _split_terms(x, terms):
        part = _dot(hi, m)
        acc = part if acc is None else acc + part
    return acc


def _split_dot_left(m, x, terms):
    acc = None
    for hi in _split_terms(x, terms):
        part = _dot(m, hi)
        acc = part if acc is None else acc + part
    return acc


def _gelu_and_grad(x):
    c = 0.7978845608028654
    inner = c * (x + 0.044715 * x * x * x)
    t = jnp.tanh(inner)
    g = 0.5 * x * (1.0 + t)
    dg = 0.5 * (1.0 + t) + 0.5 * x * (1.0 - t * t) * c * (1.0 + 3.0 * 0.044715 * x * x)
    return g, dg


def _softplus(x):
    return jnp.maximum(x, 0.0) + jnp.log(1.0 + jnp.exp(-jnp.abs(x)))


def _rsum(x):
    return jnp.sum(x, axis=0, keepdims=True)


def _acc_rows(ref, part, first):
    val = jnp.broadcast_to(part, ref.shape)

    @pl.when(first)
    def _():
        ref[...] = val

    @pl.when(jnp.logical_not(first))
    def _():
        ref[...] += val


def _rms_bwd(n, g, dout):
    r = lax.rsqrt(jnp.mean(n * n, axis=-1, keepdims=True) + EPS)
    nh = n * r
    dg = dout * g
    dn = r * (dg - nh * jnp.mean(dg * nh, axis=-1, keepdims=True))
    return dn, _rsum(dout * nh)


def _const_mats():
    avg = np.kron(np.eye(4), np.full((HEAD_DIM, HEAD_DIM), 1.0 / HEAD_DIM))
    expand = np.zeros((CHUNK, SSM_WIDTH), np.float32)
    for h in range(N_HEADS):
        expand[h, h * HEAD_DIM:(h + 1) * HEAD_DIM] = 1.0
    tril = np.tril(np.ones((CHUNK, CHUNK), np.float32))
    as_bf16 = lambda a: jnp.asarray(a, dtype=BF16)
    return as_bf16(avg), as_bf16(expand), as_bf16(expand.T), as_bf16(tril), as_bf16(tril.T)


def _full(shape):
    nd = len(shape)
    return pl.BlockSpec(shape, lambda *_: (0,) * nd)


_HBM = pl.BlockSpec(memory_space=pltpu.HBM)
_SEM = pl.BlockSpec(memory_space=pltpu.SEMAPHORE)
_ALL_PEERS = tuple((k, 0) for k in range(1, N_DEV))
_SAME_CORE_PEERS = ((2, 0), (4, 0), (6, 0))
_SIBLING_FORWARD = ((1, 0), (1, 2), (1, 4), (1, 6))


def _flip(j, k):
    for bit in (4, 2, 1):
        if k & bit:
            j = j + bit - 2 * (j & bit)
    return j


def _copies(src, land, send_sems, recv_sems, hops, slots=None):
    x, y, c = lax.axis_index("x"), lax.axis_index("y"), lax.axis_index("c")
    me = 4 * x + 2 * y + c
    slots = range(len(src)) if slots is None else slots
    out = []
    for t in range(len(src)):
        for i, (k, b) in enumerate(hops):
            pos = (1 - x if k & 4 else x, 1 - y if k & 2 else y, 1 - c if k & 1 else c)
            peer = _flip(me, k)
            sem = slots[t] * len(hops) + i
            mk = functools.partial(pltpu.make_async_remote_copy, send_sem=send_sems.at[sem], recv_sem=recv_sems.at[sem],
                                   device_id=pos, device_id_type=pl.DeviceIdType.MESH)
            if land[t] is None and src[t].shape[0] != N_DEV:
                width = src[t].shape[1] // N_DEV
                slab = lambda j: src[t].at[:, pl.ds(pl.multiple_of(j * width, 128), width)]
                mine = functools.partial(mk, src_ref=slab(_flip(me, b)), dst_ref=slab(_flip(me, b)))
                theirs = functools.partial(mk, src_ref=slab(_flip(peer, b)), dst_ref=slab(_flip(peer, b)))
            elif land[t] is None:
                mine = functools.partial(mk, src_ref=src[t].at[_flip(me, b)], dst_ref=src[t].at[_flip(me, b)])
                theirs = functools.partial(mk, src_ref=src[t].at[_flip(peer, b)], dst_ref=src[t].at[_flip(peer, b)])
            else:
                assert b == 0
                mine = functools.partial(mk, src_ref=src[t].at[peer], dst_ref=land[t].at[me])
                theirs = functools.partial(mk, src_ref=src[t].at[peer], dst_ref=land[t].at[peer])
            out.append((mine, theirs))
    return out


def _exchange_start(srcs, inplace, peers, name, dep=None):
    n = len(srcs)
    lands = [None if ip else pltpu.with_memory_space_constraint(lax.empty(s.shape, s.dtype), pltpu.HBM)
             for s, ip in zip(srcs, inplace)]
    real_lands = [l for l in lands if l is not None]
    n_l = len(real_lands)
    deps = [] if dep is None else [dep]

    def body(*refs):
        src = refs[:n]
        land_refs = list(refs[n:n + n_l])
        send_sems, recv_sems = refs[n + n_l + len(deps)], refs[n + n_l + len(deps) + 1]
        token = refs[-1]
        land = [None if ip else land_refs.pop(0) for ip in inplace]
        for mine, _ in _copies(src, land, send_sems, recv_sems, peers):
            mine().start()
        token[...] = jnp.zeros_like(token)

    sem_t = pltpu.SemaphoreType.DMA((n * len(peers),))
    outs = pl.pallas_call(
        body, name=name,
        out_shape=(sem_t, sem_t) + tuple(pltpu.HBM(a.shape, a.dtype) for a in list(srcs) + real_lands)
        + (jax.ShapeDtypeStruct((8, 128), F32),),
        in_specs=[_HBM] * (n + n_l) + [pl.BlockSpec(memory_space=pl.ANY)] * len(deps),
        out_specs=(_SEM, _SEM) + (_HBM,) * (n + n_l) + (pl.BlockSpec(memory_space=pltpu.VMEM),),
        input_output_aliases={i: 2 + i for i in range(n + n_l)},
        compiler_params=pltpu.CompilerParams(has_side_effects=pltpu.SideEffectType.DATAFLOW_SIDE_EFFECTING),
    )(*[pltpu.with_memory_space_constraint(s, pltpu.HBM) for s in srcs], *real_lands, *deps)
    handle = dict(send=outs[0], recv=outs[1], srcs=outs[2:2 + n], lands=outs[2 + n:2 + n + n_l], inplace=inplace,
                  peers=peers)
    return handle, outs[-1]


def _exchange_wait(handle, after, name, only=None):
    srcs, lands, inplace, peers = handle["srcs"], handle["lands"], handle["inplace"], handle["peers"]
    slots = None
    if only is not None:
        assert all(inplace)
        slots, srcs, inplace = list(only), [srcs[t] for t in only], [True] * len(only)
    n, n_l = len(srcs), len(lands)
    after = after if isinstance(after, tuple) else (after,)

    def body(*refs):
        src = refs[:n]
        land_refs = list(refs[n:n + n_l])
        send_sems, recv_sems = refs[n + n_l], refs[n + n_l + 1]
        land = [None if ip else land_refs.pop(0) for ip in inplace]
        for mine, theirs in _copies(src, land, send_sems, recv_sems, peers, slots):
            mine().wait_send()
            theirs().wait_recv()

    outs = pl.pallas_call(
        body, name=name, out_shape=tuple(pltpu.HBM(a.shape, a.dtype) for a in list(srcs) + list(lands)),
        in_specs=[_HBM] * (n + n_l) + [_SEM, _SEM] + [pl.BlockSpec(memory_space=pl.ANY)] * len(after),
        out_specs=(_HBM,) * (n + n_l), input_output_aliases={i: i for i in range(n + n_l)},
        compiler_params=pltpu.CompilerParams(has_side_effects=pltpu.SideEffectType.DATAFLOW_SIDE_EFFECTING),
    )(*srcs, *lands, handle["send"], handle["recv"], *after)
    res, land_out = [], list(outs[n:])
    for t in range(n):
        res.append((outs[t], outs[t] if inplace[t] else land_out.pop(0)))
    return res


def _cast_to_slot(w, me, rows, name, cols=False, dep=None):
    r, cdim = w.shape[0], w.shape[-1]
    deps = [] if dep is None else [dep]

    def body(me_ref, w_ref, *rest):
        o_ref = rest[-1]
        if cols:
            o_ref[...] = w_ref[...].astype(BF16)
        else:
            o_ref[0] = w_ref[...].reshape(rows, cdim).astype(BF16)

    if cols:
        out_shape = jax.ShapeDtypeStruct((r, N_DEV * cdim), BF16)
        out_spec = pl.BlockSpec((rows, cdim), lambda i, me_ref: (i, me_ref[0]))
    else:
        out_shape = jax.ShapeDtypeStruct((N_DEV, r, cdim), BF16)
        out_spec = pl.BlockSpec((1, rows, cdim), lambda i, me_ref: (me_ref[0], i, 0))
    return pl.pallas_call(
        body, name=name, out_shape=out_shape,
        grid_spec=pltpu.PrefetchScalarGridSpec(
            num_scalar_prefetch=1, grid=(r // rows,),
            in_specs=[pl.BlockSpec((rows, cdim), lambda i, me_ref: (i, 0)) if w.ndim == 2 else
                      pl.BlockSpec((rows, 1, cdim), lambda i, me_ref: (i, 0, 0))]
            + [pl.BlockSpec(memory_space=pl.ANY)] * len(deps), out_specs=out_spec),
        compiler_params=_params("parallel"))(me, w, *deps)


def _stack_shards(blocks, rows, bn, name):
    n, r, cdim = blocks.shape

    def body(b_ref, o_ref, acc_ref):
        acc_ref[n * r:, :] = jnp.zeros((rows - n * r, bn), F32)
        for j in range(n):
            acc_ref[r * j:r * (j + 1), :] = b_ref[j].astype(F32)
        o_ref[...] = acc_ref[...].astype(BF16)

    return pl.pallas_call(
        body, name=name, grid=(cdim // bn,), out_shape=jax.ShapeDtypeStruct((rows, cdim), BF16),
        in_specs=[pl.BlockSpec((n, r, bn), lambda i: (0, 0, i))], out_specs=pl.BlockSpec((rows, bn), lambda i: (0, i)),
        scratch_shapes=[pltpu.VMEM((rows, bn), F32)], compiler_params=_params("parallel"))(blocks)


def _adamw_math(w, g, m, v):
    m = ADAM_B1 * m + (1.0 - ADAM_B1) * g
    v = ADAM_B2 * v + (1.0 - ADAM_B2) * (g * g)
    m_hat = m / (1.0 - ADAM_B1 ** ADAM_STEP)
    v_hat = v / (1.0 - ADAM_B2 ** ADAM_STEP)
    delta = -ADAM_LR * (m_hat / (jnp.sqrt(v_hat) + ADAM_EPS) + ADAM_WD * w)
    return delta, m, v


def _sum_parts(me, p_ref, own):
    g = None
    for j in range(N_DEV):
        term = (p_ref[j] if own is None else jnp.where(me == j, own, p_ref[j])).astype(F32)
        g = term if g is None else g + term
    return g


def _adamw_reduce(parts, own, me, w, m, v, name):
    r, _, cdim = w.shape

    def body(me_ref, p_ref, own_ref, w_ref, m_ref, v_ref, g_out, d_out, m_out, v_out):
        g = _sum_parts(me_ref[0], p_ref, own_ref[0]).reshape(r, 1, cdim)
        d, mn, vn = _adamw_math(w_ref[...], g, m_ref[...], v_ref[...])
        g_out[...] = g
        d_out[...] = d
        m_out[...] = mn
        v_out[...] = vn

    blk = pl.BlockSpec((r, 1, cdim), lambda i, me_ref: (0, 0, 0))
    return pl.pallas_call(
        body, name=name, out_shape=(jax.ShapeDtypeStruct(w.shape, F32),) * 4,
        grid_spec=pltpu.PrefetchScalarGridSpec(
            num_scalar_prefetch=1, grid=(1,),
            in_specs=[pl.BlockSpec((N_DEV, r, cdim), lambda i, me_ref: (0, 0, 0)),
                      pl.BlockSpec((1, r, cdim), lambda i, me_ref: (me_ref[0], 0, 0)), blk, blk, blk],
            out_specs=(blk,) * 4),
        compiler_params=_params("arbitrary"))(me, parts, own, w, m, v)


_IN_SPLITS = ((0, 512), (512, 1024), (1024, 1536), (1536, 2560), (2560, IN_PAD))


def _prenorm(x, g1, tm, dep=None):
    t_tok = x.shape[0]
    deps = [] if dep is None else [dep]

    def body(x_ref, g_ref, *rest):
        xv = x_ref[...]
        r = lax.rsqrt(jnp.mean(xv * xv, axis=-1, keepdims=True) + EPS)
        rest[-1][...] = (xv * r * g_ref[...]).astype(BF16)

    row = pl.BlockSpec((tm, D_MODEL), lambda i: (i, 0))
    return pl.pallas_call(
        body, name="prenorm", grid=(t_tok // tm,), out_shape=jax.ShapeDtypeStruct((t_tok, D_MODEL), BF16),
        in_specs=[row, _full((1, D_MODEL))] + [pl.BlockSpec(memory_space=pl.ANY)] * len(deps), out_specs=row,
        compiler_params=_params("parallel"))(x, g1, *deps)


def _in_proj(h1, w_in, tm):
    t_tok = h1.shape[0]

    def body(h_ref, w_ref, *outs):
        h = h_ref[...]
        for (a, b), o_ref in zip(_IN_SPLITS, outs):
            o_ref[...] = _dot(h, w_ref[a:b, :], _NT).astype(o_ref.dtype)

    row = lambda n: pl.BlockSpec((tm, n), lambda i: (i, 0))
    widths = [b - a for a, b in _IN_SPLITS]
    dtypes = (BF16, BF16, BF16, F32, F32)
    return pl.pallas_call(
        body, name="in_proj", grid=(t_tok // tm,),
        out_shape=tuple(jax.ShapeDtypeStruct((t_tok, n), dt) for n, dt in zip(widths, dtypes)),
        in_specs=[row(D_MODEL), _full((IN_PAD, D_MODEL))], out_specs=tuple(row(n) for n in widths),
        compiler_params=_params("parallel"))(h1, w_in)


def _lane_masks():
    lane = lax.broadcasted_iota(jnp.int32, (1, 2 * HEAD_DIM), 1)
    left = (lane < HEAD_DIM).astype(F32)
    return left, 1.0 - left


def _stack_pair(v, m_l, m_r):
    return jnp.concatenate([v * m_l, v * m_r], axis=0).astype(BF16)


def _head_mean(x, avg):
    n = avg.shape[0]
    return jnp.concatenate([_split_dot(x[:, n * i:n * (i + 1)], avg, 2) for i in range(x.shape[1] // n)], axis=1)


def _gmlp_common(u, v, lnw, lnb, avg, wcat_ref, bias, m_l, m_r):
    ug, dug = _gelu_and_grad(u)
    vg, dvg = _gelu_and_grad(v)
    mu = _head_mean(vg, avg)
    vc = vg - mu
    var = _head_mean(vc * vc, avg)
    rstd = lax.rsqrt(var + EPS)
    vhat = vc * rstd
    vn = vhat * lnw + lnb
    rows = []
    for r in range(u.shape[0] // CHUNK):
        cols = []
        for j in range(N_HEADS // 2):
            pair = vn[CHUNK * r:CHUNK * (r + 1), 128 * j:128 * (j + 1)]
            cols.append(_dot(wcat_ref[j], _stack_pair(pair, m_l, m_r)))
        rows.append(jnp.concatenate(cols, axis=1) + bias)
    mixed = jnp.concatenate(rows, axis=0)
    return ug, dug, dvg, rstd, vhat, vn, mixed


_GMLP_ROWS = 4 * CHUNK


def _gmlp_fwd(u, v, lnw, lnb, wcat, bias, avg):
    t_tok = u.shape[0]
    tm = min(_GMLP_ROWS, t_tok)

    def body(u_ref, v_ref, lnw_ref, lnb_ref, wcat_ref, bias_ref, avg_ref, o_ref):
        m_l, m_r = _lane_masks()
        ug, _, _, _, _, _, mixed = _gmlp_common(
            u_ref[...].astype(F32), v_ref[...].astype(F32), lnw_ref[...], lnb_ref[...], avg_ref[...], wcat_ref,
            bias_ref[...], m_l, m_r)
        o_ref[...] = (ug * mixed).astype(BF16)

    row = pl.BlockSpec((tm, GM_WIDTH), lambda i: (i, 0))
    return pl.pallas_call(
        body, name="gmlp_fwd", grid=(t_tok // tm,), out_shape=jax.ShapeDtypeStruct((t_tok, GM_WIDTH), BF16),
        in_specs=[row, row, _full((1, GM_WIDTH)), _full((1, GM_WIDTH)), _full(wcat.shape), _full(bias.shape),
                  _full(avg.shape)],
        out_specs=row, compiler_params=_params("parallel"))(u, v, lnw, lnb, wcat, bias, avg)


def _shift_rows(x, edge, j, down):
    groups, cols = x.shape[0] // 8, x.shape[1]
    amount = j if down else 8 - j
    rot = pltpu.roll(x.reshape(groups, 8, cols), amount, axis=1)
    edge_rot = pltpu.roll(edge, amount, axis=0)[None]
    sub = lax.broadcasted_iota(jnp.int32, (1, 8, 1), 1)
    if down:
        out = jnp.where(sub < j, jnp.concatenate([edge_rot, rot[:-1]], axis=0), rot)
    else:
        out = jnp.where(sub < 8 - j, rot, jnp.concatenate([rot[1:], edge_rot], axis=0))
    return out.reshape(x.shape)


def _conv_pre(xbc, tail, cw_ref, cb):
    taps = [_shift_rows(xbc, tail, 3 - k, True) for k in range(3)] + [xbc]
    return cb + cw_ref[0:1, :] * taps[0] + cw_ref[1:2, :] * taps[1] + cw_ref[2:3, :] * taps[2] + cw_ref[3:4, :] * taps[3]


def _ssd_common(pre, dtr, dtb, alog, expand, tril):
    q = CHUNK
    sg = jax.nn.sigmoid(pre)
    act = pre * sg
    lane = lax.broadcasted_iota(jnp.int32, (1, CHUNK), 1)
    a_row = jnp.where(lane < N_HEADS, -jnp.exp(alog), 0.0)
    dtp = dtr + dtb
    dt = _softplus(dtp)
    a_cs = _split_dot_left(tril, dt * a_row, 3)
    a_cs_t = a_cs.T
    dt_exp = _split_dot(dt, expand, 3)
    a_exp = _split_dot(a_cs, expand, 3)
    a_end = a_exp[q - 1:q, :]
    li = lax.broadcasted_iota(jnp.int32, (q, q), 0)
    si = lax.broadcasted_iota(jnp.int32, (q, q), 1)
    causal = si <= li
    decay = []
    for h in range(N_HEADS):
        seg = a_cs[:, h:h + 1] - a_cs_t[h:h + 1, :]
        decay.append(jnp.where(causal, jnp.exp(jnp.minimum(seg, 0.0)), 0.0))
    return dict(pre=pre, sg=sg, act=act, a_row=a_row, dtp=dtp, dt=dt, dt_exp=dt_exp, a_exp=a_exp,
                e=jnp.exp(a_exp), w_end=jnp.exp(a_end - a_exp), cd=jnp.exp(a_end), decay=decay)


def _ssd_specs(t_tok, seq, reverse):
    nb, nc = t_tok // seq, seq // CHUNK

    def chunk(c):
        return nc - 1 - c if reverse else c

    def row(n, col=0):
        return pl.BlockSpec((nb, CHUNK, n), lambda c: (0, chunk(c), col))

    tail = pl.BlockSpec((nb, 8, CONV_CH), lambda c: (0, jnp.maximum(chunk(c) * (CHUNK // 8) - 1, 0), 0))
    states = pl.BlockSpec((nb, 1, N_STATE, SSM_WIDTH), lambda c: (0, chunk(c), 0, 0))
    fold = lambda a: a.reshape(nb, seq, a.shape[-1])
    unfold = lambda a: a.reshape(t_tok, a.shape[-1])
    return nb, nc, row, tail, states, fold, unfold


def _ssd_fwd(z, xbc, dtr, cw, cb, dtb, alog, dskip_exp, nw, expand, tril, seq, dep=None):
    t_tok = z.shape[0]
    nb, nc, row, tail, states_spec, fold, unfold = _ssd_specs(t_tok, seq, False)

    def body(z_ref, xbc_ref, tail_ref, dtr_ref, cw_ref, cb_ref, dtb_ref, alog_ref, dsk_ref, nw_ref, exp_ref,
             tril_ref, o_ref, y_ref, st_ref, pre_ref, state_ref):
        c = pl.program_id(0)

        @pl.when(c == 0)
        def _():
            state_ref[...] = jnp.zeros_like(state_ref)

        m_l, m_r = _lane_masks()
        for s in range(nb):
            pre = _conv_pre(xbc_ref[s], jnp.where(c == 0, 0.0, tail_ref[s]), cw_ref, cb_ref[...])
            pre_ref[s] = pre
            f = _ssd_common(pre, dtr_ref[s], dtb_ref[...], alog_ref[...], exp_ref[...], tril_ref[...])
            act = f["act"]
            xs = act[:, :SSM_WIDTH]
            xdt = xs * f["dt_exp"]
            xw = xdt * f["w_end"]
            state = state_ref[s]
            st_ref[s, 0] = state
            ydiag, yoff, snew = [], [], []
            for g in range(2):
                bg = act[:, 512 + 128 * g:640 + 128 * g].astype(BF16)
                cg = act[:, 768 + 128 * g:896 + 128 * g].astype(BF16)
                cb_mat = _dot(cg, bg, _NT)
                for pr in range(2):
                    h0 = 4 * g + 2 * pr
                    gcat = jnp.concatenate(
                        [(cb_mat * f["decay"][h0]).astype(BF16), (cb_mat * f["decay"][h0 + 1]).astype(BF16)], axis=1)
                    ydiag.append(_dot(gcat, _stack_pair(xdt[:, 64 * h0:64 * h0 + 128], m_l, m_r)))
                yoff.append(_dot(cg, state[:, 256 * g:256 * (g + 1)].astype(BF16)))
                snew.append(_dot(bg, xw[:, 256 * g:256 * (g + 1)].astype(BF16), _TN))
            y = jnp.concatenate(ydiag, axis=1) + f["e"] * jnp.concatenate(yoff, axis=1) + dsk_ref[...] * xs
            state_ref[s] = state * f["cd"] + jnp.concatenate(snew, axis=1)
            y_ref[s] = y
            zv = z_ref[s].astype(F32)
            yg = y * (zv * jax.nn.sigmoid(zv))
            outs = []
            for g in range(2):
                ygg = yg[:, 256 * g:256 * (g + 1)]
                outs.append(ygg * lax.rsqrt(jnp.mean(ygg * ygg, axis=-1, keepdims=True) + EPS))
            o_ref[s] = (jnp.concatenate(outs, axis=1) * nw_ref[...]).astype(BF16)

    consts = [cw, cb, dtb, alog, dskip_exp, nw, expand, tril]
    deps = [] if dep is None else [dep]
    n_in = 4 + len(consts)

    def body_skipping_dep(*refs):
        body(*refs[:n_in], *refs[n_in + len(deps):])

    sd = lambda n, dt: jax.ShapeDtypeStruct((nb, seq, n), dt)
    o, y, states, pre = pl.pallas_call(
        body_skipping_dep, name="ssd_fwd", grid=(nc,),
        out_shape=(sd(SSM_WIDTH, BF16), sd(SSM_WIDTH, F32), jax.ShapeDtypeStruct((nb, nc, N_STATE, SSM_WIDTH), F32),
                   sd(CONV_CH, F32)),
        in_specs=[row(SSM_WIDTH), row(CONV_CH), tail, row(CHUNK)] + [_full(a.shape) for a in consts]
        + [pl.BlockSpec(memory_space=pl.ANY)] * len(deps),
        out_specs=(row(SSM_WIDTH), row(SSM_WIDTH), states_spec, row(CONV_CH)),
        scratch_shapes=[pltpu.VMEM((nb, N_STATE, SSM_WIDTH), F32)],
        compiler_params=_params("arbitrary"))(fold(z), fold(xbc), fold(xbc), fold(dtr), *consts, *deps)
    return unfold(o), unfold(y), states, unfold(pre)


def _out_proj(mix_a, mix_b, w_out, x, g2, g3, tm, dep=None):
    t_tok = x.shape[0]
    deps = [] if dep is None else [dep]

    def body(a_ref, b_ref, w_ref, x_ref, g2_ref, g3_ref, *rest):
        o_ref, x2_ref, h3_ref = rest[-3:]
        o = _dot(a_ref[...], w_ref[0:GM_WIDTH, :]) + _dot(b_ref[...], w_ref[GM_WIDTH:, :])
        o_ref[...] = o
        r2 = lax.rsqrt(jnp.mean(o * o, axis=-1, keepdims=True) + EPS)
        x2 = x_ref[...] + o * r2 * g2_ref[...]
        x2_ref[...] = x2
        r3 = lax.rsqrt(jnp.mean(x2 * x2, axis=-1, keepdims=True) + EPS)
        h3_ref[...] = (x2 * r3 * g3_ref[...]).astype(BF16)

    row = lambda n: pl.BlockSpec((tm, n), lambda i: (i, 0))
    sd = lambda dt: jax.ShapeDtypeStruct((t_tok, D_MODEL), dt)
    return pl.pallas_call(
        body, name="out_proj", grid=(t_tok // tm,), out_shape=(sd(F32), sd(F32), sd(BF16)),
        in_specs=[row(GM_WIDTH), row(SSM_WIDTH), _full((D_MODEL, D_MODEL)), row(D_MODEL), _full((1, D_MODEL)),
                  _full((1, D_MODEL))] + [pl.BlockSpec(memory_space=pl.ANY)] * len(deps),
        out_specs=(row(D_MODEL),) * 3, compiler_params=_params("parallel"))(mix_a, mix_b, w_out, x, g2, g3, *deps)


def _mlp_fwd(h3, w_up, w_down, x2, target, g4, tm, tf):
    t_tok = x2.shape[0]

    def up_body(h_ref, wu_ref, ra_ref):
        ra_ref[...] = jnp.maximum(_dot(h_ref[...], wu_ref[...]), 0.0).astype(BF16)

    tu = min(2 * tm, t_tok)
    ra = pl.pallas_call(
        up_body, name="mlp_up", grid=(D_FF // tf, t_tok // tu), out_shape=jax.ShapeDtypeStruct((t_tok, D_FF), BF16),
        in_specs=[pl.BlockSpec((tu, D_MODEL), lambda j, i: (i, 0)), pl.BlockSpec((D_MODEL, tf), lambda j, i: (0, j))],
        out_specs=pl.BlockSpec((tu, tf), lambda j, i: (i, j)), compiler_params=_params("parallel", "parallel"))(h3, w_up)

    def down_body(ra_ref, wd_ref, x2_ref, t_ref, g4_ref, dd_ref, dy_ref, dg4_ref, loss_ref):
        i = pl.program_id(0)
        rav = ra_ref[...]
        dvec = _dot(rav * rav, wd_ref[...])
        r4 = lax.rsqrt(jnp.mean(dvec * dvec, axis=-1, keepdims=True) + EPS)
        dn = dvec * r4
        g4 = g4_ref[...]
        err = x2_ref[...] + dn * g4 - t_ref[...]
        dy = err * (1.0 / D_MODEL)
        dy_ref[...] = dy
        dg = dy * g4
        dd_ref[...] = (r4 * (dg - dn * jnp.mean(dg * dn, axis=-1, keepdims=True))).astype(BF16)
        _acc_rows(dg4_ref, _rsum(dy * dn), i == 0)
        tile_loss = 0.5 * jnp.sum(jnp.sum(err * err, axis=-1, keepdims=True), axis=0, keepdims=True) / D_MODEL
        _acc_rows(loss_ref, jnp.broadcast_to(tile_loss, (1, 128)), i == 0)

    row = pl.BlockSpec((tm, D_MODEL), lambda i: (i, 0))
    dd, dy, dg4, loss = pl.pallas_call(
        down_body, name="mlp_down", grid=(t_tok // tm,),
        out_shape=(jax.ShapeDtypeStruct((t_tok, D_MODEL), BF16), jax.ShapeDtypeStruct((t_tok, D_MODEL), F32),
                   jax.ShapeDtypeStruct((1, D_MODEL), F32), jax.ShapeDtypeStruct((1, 128), F32)),
        in_specs=[pl.BlockSpec((tm, D_FF), lambda i: (i, 0)), _full((D_FF, D_MODEL)), row, row, _full((1, D_MODEL))],
        out_specs=(row, row, _full((1, D_MODEL)), _full((1, 128))),
        compiler_params=_params("arbitrary"))(ra, w_down, x2, target, g4)
    return ra, dd, dy, dg4, loss


def _mlp_bwd(dd, w_down, ra, w_up, x2, dy, o, g3, g2, tm, tf):
    t_tok = x2.shape[0]

    def hidden_body(dd_ref, wd_ref, ra_ref, da_ref):
        df = _dot(dd_ref[...], wd_ref[...], _NT)
        da_ref[...] = (df * (2.0 * ra_ref[...].astype(F32))).astype(BF16)

    tu = min(2 * tm, t_tok)
    da = pl.pallas_call(
        hidden_body, name="mlp_bwd_hidden", grid=(D_FF // tf, t_tok // tu),
        out_shape=jax.ShapeDtypeStruct((t_tok, D_FF), BF16),
        in_specs=[pl.BlockSpec((tu, D_MODEL), lambda j, i: (i, 0)), pl.BlockSpec((tf, D_MODEL), lambda j, i: (j, 0)),
                  pl.BlockSpec((tu, tf), lambda j, i: (i, j))],
        out_specs=pl.BlockSpec((tu, tf), lambda j, i: (i, j)),
        compiler_params=_params("parallel", "parallel"))(dd, w_down, ra)

    steps = t_tok // tm

    def in_body(da_ref, wu_ref, x2_ref, dy_ref, o_ref, g3_ref, g2_ref, dx2_ref, do_ref, dg3_ref, dg2_ref,
                prev_ref, next_ref):
        i = pl.program_id(0)

        @pl.when(i == 0)
        def _():
            prev_ref[...] = jnp.zeros_like(prev_ref)

        next_ref[...] = _dot(da_ref[...], wu_ref[...], _NT)
        dn3, dg3 = _rms_bwd(x2_ref[...], g3_ref[...], prev_ref[...])
        dx2 = dy_ref[...] + dn3
        dx2_ref[...] = dx2
        do, dg2 = _rms_bwd(o_ref[...], g2_ref[...], dx2)
        do_ref[...] = do.astype(BF16)
        _acc_rows(dg3_ref, jnp.where(i > 0, dg3, 0.0), i == 0)
        _acc_rows(dg2_ref, jnp.where(i > 0, dg2, 0.0), i == 0)
        prev_ref[...] = next_ref[...]

    row = pl.BlockSpec((tm, D_MODEL), lambda i: (jnp.maximum(i - 1, 0), 0))
    vec = _full((1, D_MODEL))
    sd = lambda dt: jax.ShapeDtypeStruct((t_tok, D_MODEL), dt)
    dx2, do, dg3, dg2 = pl.pallas_call(
        in_body, name="mlp_bwd_in", grid=(steps + 1,),
        out_shape=(sd(F32), sd(BF16), jax.ShapeDtypeStruct((1, D_MODEL), F32), jax.ShapeDtypeStruct((1, D_MODEL), F32)),
        in_specs=[pl.BlockSpec((tm, D_FF), lambda i: (jnp.minimum(i, steps - 1), 0)), _full((D_MODEL, D_FF)),
                  row, row, row, vec, vec],
        out_specs=(row, row, vec, vec), scratch_shapes=[pltpu.VMEM((tm, D_MODEL), F32)] * 2,
        compiler_params=_params("arbitrary"))(da, w_up, x2, dy, o, g3, g2)
    return da, dx2, do, dg3, dg2


def _wgrad(a, b, out_blocks, bm, bn, bk, square_a, name, dep=None):
    t_tok, m = a.shape
    n = b.shape[1]
    nk = t_tok // bk

    def body(a_ref, b_ref, *rest):
        o_ref, acc_ref = rest[-2:]
        k = pl.program_id(2)
        av = a_ref[...]
        if square_a:
            av = av * av
        part = _dot(av, b_ref[...], _TN)

        def emit(res):
            if out_blocks is None:
                o_ref[...] = res.astype(BF16)
            else:
                o_ref[0] = res.astype(BF16)

        if nk == 1:
            emit(part)
            return

        @pl.when(k == 0)
        def _():
            acc_ref[...] = part

        @pl.when(k > 0)
        def _():
            acc_ref[...] += part

        @pl.when(k == nk - 1)
        def _():
            emit(acc_ref[...])

    if out_blocks is None:
        out_shape = jax.ShapeDtypeStruct((m, n), BF16)
        out_spec = pl.BlockSpec((bm, bn), lambda i, j, k: (i, j))
    else:
        assert n // out_blocks == bn
        out_shape = jax.ShapeDtypeStruct((out_blocks, m, bn), BF16)
        out_spec = pl.BlockSpec((1, bm, bn), lambda i, j, k: (j, i, 0))
    deps = [] if dep is None else [dep]
    return pl.pallas_call(
        body, name=name, grid=(m // bm, n // bn, nk), out_shape=out_shape,
        in_specs=[pl.BlockSpec((bk, bm), lambda i, j, k: (k, i)), pl.BlockSpec((bk, bn), lambda i, j, k: (k, j))]
        + [pl.BlockSpec(memory_space=pl.ANY)] * len(deps),
        out_specs=out_spec, scratch_shapes=[pltpu.VMEM((bm, bn) if nk > 1 else (8, 128), F32)],
        compiler_params=_params("parallel", "parallel", "arbitrary"))(a, b, *deps)


def _wgrad_in_chunked(h1, pieces, bn, bk, dep=None):
    t_tok = h1.shape[0]
    nk = t_tok // bk
    shard = IN_COLS // N_DEV
    widths = [b - a for a, b in _IN_SPLITS]

    def body(h_ref, *rest):
        piece_refs = rest[:len(widths)]
        o_ref, acc_ref = rest[-2:]
        k = pl.program_id(1)
        hv = h_ref[...]
        for (a, b), r in zip(_IN_SPLITS, piece_refs):
            part = _dot(r[...], hv, _TN)

            @pl.when(k == 0)
            def _():
                acc_ref[a:b, :] = part

            @pl.when(k > 0)
            def _():
                acc_ref[a:b, :] += part

        @pl.when(k == nk - 1)
        def _():
            for j in range(N_DEV):
                o_ref[j] = acc_ref[shard * j:shard * (j + 1), :].astype(BF16)

    deps = [] if dep is None else [dep]
    return pl.pallas_call(
        body, name="wgrad_in", grid=(D_MODEL // bn, nk), out_shape=jax.ShapeDtypeStruct((N_DEV, shard, D_MODEL), BF16),
        in_specs=[pl.BlockSpec((bk, bn), lambda j, k: (k, j))] + [pl.BlockSpec((bk, n), lambda j, k: (k, 0)) for n in widths]
        + [pl.BlockSpec(memory_space=pl.ANY)] * len(deps),
        out_specs=pl.BlockSpec((N_DEV, shard, bn), lambda j, k: (0, 0, j)),
        scratch_shapes=[pltpu.VMEM((IN_PAD, bn), F32)],
        compiler_params=_params("parallel", "arbitrary"))(h1, *pieces, *deps)


def _wgrad_pieces(h1, pieces, bn, name, dep=None):
    t_tok = h1.shape[0]
    widths = [p.shape[1] for p in pieces]
    starts = [sum(widths[:i]) for i in range(len(widths))]

    def body(h_ref, *rest):
        piece_refs = rest[:len(widths)]
        o_ref = rest[-1]
        hv = h_ref[...]
        for a, n, r in zip(starts, widths, piece_refs):
            o_ref[a:a + n, :] = _dot(r[...], hv, _TN).astype(BF16)

    deps = [] if dep is None else [dep]
    return pl.pallas_call(
        body, name=name, grid=(D_MODEL // bn,), out_shape=jax.ShapeDtypeStruct((sum(widths), D_MODEL), BF16),
        in_specs=[pl.BlockSpec((t_tok, bn), lambda j: (0, j))] + [pl.BlockSpec((t_tok, n), lambda j: (0, 0)) for n in widths]
        + [pl.BlockSpec(memory_space=pl.ANY)] * len(deps),
        out_specs=pl.BlockSpec((sum(widths), bn), lambda j: (0, j)),
        compiler_params=_params("parallel"))(h1, *pieces, *deps)


def _dmix(do, w_out, tm, dep=None):
    t_tok = do.shape[0]

    def body(d_ref, w_ref, *rest):
        rest[-1][...] = _dot(d_ref[...], w_ref[...], _NT).astype(BF16)

    row = pl.BlockSpec((tm, D_MODEL), lambda i: (i, 0))
    deps = [] if dep is None else [dep]
    return pl.pallas_call(
        body, name="dmix", grid=(t_tok // tm,), out_shape=jax.ShapeDtypeStruct((t_tok, D_MODEL), BF16),
        in_specs=[row, _full((D_MODEL, D_MODEL))] + [pl.BlockSpec(memory_space=pl.ANY)] * len(deps), out_specs=row,
        compiler_params=_params("parallel"))(do, w_out, *deps)


def _gmlp_bwd(dmix, u, v, lnw, lnb, wcat, wtcat, bias, avg, expand_t):
    t_tok = u.shape[0]
    tm = min(_GMLP_ROWS, t_tok)

    def body(dm_ref, u_ref, v_ref, lnw_ref, lnb_ref, wcat_ref, wtcat_ref, bias_ref, avg_ref, expt_ref, du_ref, dv_ref,
             dw_ref, db_ref, dlnw_ref, dlnb_ref):
        i = pl.program_id(0)
        m_l, m_r = _lane_masks()
        avg = avg_ref[...]
        lnw = lnw_ref[...]
        ug, dug, dvg, rstd, vhat, vn, mixed = _gmlp_common(
            u_ref[...].astype(F32), v_ref[...].astype(F32), lnw, lnb_ref[...], avg, wcat_ref, bias_ref[...], m_l, m_r)
        dya = dm_ref[...].astype(F32)
        du_ref[...] = (dya * mixed * dug).astype(BF16)
        dmixed = dya * ug
        dvn_rows, dws, dbt = [], [None] * N_HEADS, None
        for r in range(tm // CHUNK):
            dvn_cols = []
            for j in range(N_HEADS // 2):
                dmp = dmixed[CHUNK * r:CHUNK * (r + 1), 128 * j:128 * (j + 1)]
                dvn_cols.append(_dot(wtcat_ref[j], _stack_pair(dmp, m_l, m_r)))
                vnp = vn[CHUNK * r:CHUNK * (r + 1), 128 * j:128 * (j + 1)].astype(BF16)
                for i_h, mask in enumerate((m_l, m_r)):
                    part = _dot((dmp * mask).astype(BF16), vnp, _NT)
                    dws[2 * j + i_h] = part if r == 0 else dws[2 * j + i_h] + part
            dvn_rows.append(jnp.concatenate(dvn_cols, axis=1))
            part = _split_dot(dmixed[CHUNK * r:CHUNK * (r + 1), :], expt_ref[...], 2)
            dbt = part if r == 0 else dbt + part
        dvn = jnp.concatenate(dvn_rows, axis=0)
        dvh = dvn * lnw
        dvgel = rstd * (dvh - _head_mean(dvh, avg) - vhat * _head_mean(dvh * vhat, avg))
        dv_ref[...] = (dvgel * dvg).astype(BF16)
        first = i == 0

        @pl.when(first)
        def _():
            for h in range(N_HEADS):
                dw_ref[h] = dws[h]
            db_ref[...] = dbt

        @pl.when(jnp.logical_not(first))
        def _():
            for h in range(N_HEADS):
                dw_ref[h] += dws[h]
            db_ref[...] += dbt

        _acc_rows(dlnw_ref, _rsum(dvn * vhat), first)
        _acc_rows(dlnb_ref, _rsum(dvn), first)

    row = pl.BlockSpec((tm, GM_WIDTH), lambda i: (i, 0))
    consts = [lnw, lnb, wcat, wtcat, bias, avg, expand_t]
    return pl.pallas_call(
        body, name="gmlp_bwd", grid=(t_tok // tm,),
        out_shape=(jax.ShapeDtypeStruct((t_tok, GM_WIDTH), BF16), jax.ShapeDtypeStruct((t_tok, GM_WIDTH), BF16),
                   jax.ShapeDtypeStruct((N_HEADS, CHUNK, CHUNK), F32), jax.ShapeDtypeStruct((CHUNK, CHUNK), F32),
                   jax.ShapeDtypeStruct((1, GM_WIDTH), F32), jax.ShapeDtypeStruct((1, GM_WIDTH), F32)),
        in_specs=[row, row, row] + [_full(a.shape) for a in consts],
        out_specs=(row, row, _full((N_HEADS, CHUNK, CHUNK)), _full((CHUNK, CHUNK)), _full((1, GM_WIDTH)),
                   _full((1, GM_WIDTH))),
        compiler_params=_params("arbitrary"))(dmix, u, v, *consts)


def _ssd_bwd(dmix, z, xbc, pre, dtr, y, states, cw, cb, dtb, alog, dskip_exp, nw, expand, expand_t, tril, triu, seq,
             dep=None):
    t_tok = z.shape[0]
    nb, nc, row, _, states_spec, fold, unfold = _ssd_specs(t_tok, seq, True)
    q = CHUNK

    def one_sequence(s, dm_ref, z_ref, xbc_ref, pre_ref, dtr_ref, y_ref, st_ref, cw_ref, dtb_ref, alog_ref, dsk_ref,
                     nw_ref, exp_ref, expt_ref, tril_ref, triu_ref, dz_ref, dxbc_ref, ddt_ref, dhead_ref, dstate_ref):
        m_l, m_r = _lane_masks()
        expt = expt_ref[...]
        f = _ssd_common(pre_ref[s], dtr_ref[s], dtb_ref[...], alog_ref[...], exp_ref[...], tril_ref[...])
        act, pre, sg = f["act"], f["pre"], f["sg"]
        xs = act[:, :SSM_WIDTH]
        xdt = xs * f["dt_exp"]
        xw = xdt * f["w_end"]
        state = st_ref[s, 0]
        dstate = dstate_ref[s]
        zv, yv, dout, nw = z_ref[s].astype(F32), y_ref[s], dm_ref[s].astype(F32), nw_ref[...]
        sz = jax.nn.sigmoid(zv)
        sl = zv * sz
        yg = yv * sl
        tv = dout * nw
        dyg_parts, ygh_parts = [], []
        for g in range(2):
            ygg = yg[:, 256 * g:256 * (g + 1)]
            rr = lax.rsqrt(jnp.mean(ygg * ygg, axis=-1, keepdims=True) + EPS)
            ygh = ygg * rr
            tg = tv[:, 256 * g:256 * (g + 1)]
            dyg_parts.append(rr * (tg - ygh * jnp.mean(tg * ygh, axis=-1, keepdims=True)))
            ygh_parts.append(ygh)
        dyg = jnp.concatenate(dyg_parts, axis=1)
        dnw = _rsum(dout * jnp.concatenate(ygh_parts, axis=1))
        dy = dyg * sl
        dz_ref[s] = (dyg * yv * (sz * (1.0 + zv * (1.0 - sz)))).astype(BF16)
        ddsk = _rsum(dy * xs)
        dye = dy * f["e"]
        lane = lax.broadcasted_iota(jnp.int32, (q, q), 1)
        sub = lax.broadcasted_iota(jnp.int32, (q, q), 0)
        rs_mat = jnp.zeros((q, q), F32)
        cs_mat = jnp.zeros((q, q), F32)
        dxdt_cols, yoff, dst_in, dxw, d_b, d_c = [], [], [], [], [], []
        for g in range(2):
            bg = act[:, 512 + 128 * g:640 + 128 * g].astype(BF16)
            cg = act[:, 768 + 128 * g:896 + 128 * g].astype(BF16)
            cb_mat = _dot(cg, bg, _NT)
            stg = state[:, 256 * g:256 * (g + 1)].astype(BF16)
            dyeg = dye[:, 256 * g:256 * (g + 1)].astype(BF16)
            yoff.append(_dot(cg, stg))
            dcg = _dot(dyeg, stg, _NT)
            dst_in.append(_dot(cg, dyeg, _TN))
            dcb = jnp.zeros((q, q), F32)
            for pr in range(2):
                h0 = 4 * g + 2 * pr
                gf = [cb_mat * f["decay"][h0], cb_mat * f["decay"][h0 + 1]]
                gcat = jnp.concatenate([gf[0].astype(BF16), gf[1].astype(BF16)], axis=1)
                xst = _stack_pair(xdt[:, 64 * h0:64 * h0 + 128], m_l, m_r)
                dyp = dy[:, 64 * h0:64 * h0 + 128].astype(BF16)
                dgcat = _dot(dyp, xst, _NT)
                dxst = _dot(gcat, dyp, _TN)
                dxdt_cols.append(dxst[:q] * m_l + dxst[q:] * m_r)
                for i in range(2):
                    h = h0 + i
                    dg = dgcat[:, q * i:q * (i + 1)]
                    mm = dg * gf[i]
                    rs_mat = rs_mat + jnp.where(lane == h, jnp.sum(mm, axis=1, keepdims=True), 0.0)
                    cs_mat = cs_mat + jnp.where(sub == h, jnp.sum(mm, axis=0, keepdims=True), 0.0)
                    dcb = dcb + dg * f["decay"][h]
            dcb16 = dcb.astype(BF16)
            dstg = dstate[:, 256 * g:256 * (g + 1)].astype(BF16)
            d_c.append(dcg + _dot(dcb16, bg))
            dxw.append(_dot(bg, dstg))
            d_b.append(_dot(dcb16, cg, _TN) + _dot(xw[:, 256 * g:256 * (g + 1)].astype(BF16), dstg, _NT))
        dxw = jnp.concatenate(dxw, axis=1)
        dxdt = jnp.concatenate(dxdt_cols, axis=1) + dxw * f["w_end"]
        qv = dxw * xw
        end_row = _rsum(qv) + _rsum(dstate * state) * f["cd"]
        x2 = dye * jnp.concatenate(yoff, axis=1) - qv
        row_i = lax.broadcasted_iota(jnp.int32, (q, 1), 0)
        x2 = x2 + jnp.where(row_i == q - 1, end_row, 0.0)
        da_cs = _split_dot(x2, expt, 2) + rs_mat - cs_mat.T
        ddt = _split_dot(dxdt * xs, expt, 2)
        dxs = dsk_ref[...] * dy + dxdt * f["dt_exp"]
        dda = _split_dot_left(triu_ref[...], da_cs, 3)
        ddt = ddt + dda * f["a_row"]
        dalog = _rsum(dda * f["dt"]) * f["a_row"]
        draw = ddt * jax.nn.sigmoid(f["dtp"])
        ddt_ref[s] = draw.astype(BF16)
        dact = jnp.concatenate([dxs] + d_b + d_c, axis=1)
        dpre = dact * (sg * (1.0 + pre * (1.0 - sg)))
        dhead = dhead_ref[s]
        xv = xbc_ref[s]
        shifted = [_shift_rows(dpre, dhead, 3 - k, False) for k in range(3)] + [dpre]
        dxbc = cw_ref[3:4, :] * dpre
        for k in range(3):
            dxbc = dxbc + cw_ref[k:k + 1, :] * shifted[k]
        dxbc_ref[s] = dxbc.astype(BF16)
        dhead_ref[s] = dpre[0:8, :]
        dstate_ref[s] = dstate * f["cd"] + jnp.concatenate(dst_in, axis=1)
        row8 = lax.broadcasted_iota(jnp.int32, (8, 1), 0)
        dcw = jnp.zeros((8, CONV_CH), F32)
        for k in range(4):
            dcw = dcw + jnp.where(row8 == k, _rsum(shifted[k] * xv), 0.0)
        return dcw, _rsum(dpre), _rsum(draw), dalog, _split_dot(ddsk, expt, 3), dnw

    def body(dm_ref, z_ref, xbc_ref, pre_ref, dtr_ref, y_ref, st_ref, cw_ref, cb_ref, dtb_ref, alog_ref, dsk_ref,
             nw_ref, exp_ref, expt_ref, tril_ref, triu_ref, dz_ref, dxbc_ref, ddt_ref, dcw_ref, dcb_ref, ddtb_ref,
             dalog_ref, dd_ref, dnw_ref, dhead_ref, dstate_ref):
        c = pl.program_id(0)
        first = c == 0

        @pl.when(first)
        def _():
            dstate_ref[...] = jnp.zeros_like(dstate_ref)
            dhead_ref[...] = jnp.zeros_like(dhead_ref)

        total = None
        for s in range(nb):
            parts = one_sequence(s, dm_ref, z_ref, xbc_ref, pre_ref, dtr_ref, y_ref, st_ref, cw_ref, dtb_ref, alog_ref,
                                 dsk_ref, nw_ref, exp_ref, expt_ref, tril_ref, triu_ref, dz_ref, dxbc_ref, ddt_ref,
                                 dhead_ref, dstate_ref)
            total = parts if total is None else tuple(a + b for a, b in zip(total, parts))
        dcw = total[0]

        @pl.when(first)
        def _():
            dcw_ref[...] = dcw

        @pl.when(jnp.logical_not(first))
        def _():
            dcw_ref[...] += dcw

        for ref, part in zip((dcb_ref, ddtb_ref, dalog_ref, dd_ref, dnw_ref), total[1:]):
            _acc_rows(ref, part, first)

    consts = [cw, cb, dtb, alog, dskip_exp, nw, expand, expand_t, tril, triu]
    deps = [] if dep is None else [dep]
    n_in = 7 + len(consts)

    def body_skipping_dep(*refs):
        body(*refs[:n_in], *refs[n_in + len(deps):])

    acc = lambda n: jax.ShapeDtypeStruct((1, n), F32)
    sd = lambda n: jax.ShapeDtypeStruct((nb, seq, n), BF16)
    dz, dxbc, ddt, *small_grads = pl.pallas_call(
        body_skipping_dep, name="ssd_bwd", grid=(nc,),
        out_shape=(sd(SSM_WIDTH), sd(CONV_CH), sd(CHUNK), jax.ShapeDtypeStruct((8, CONV_CH), F32), acc(CONV_CH),
                   acc(CHUNK), acc(CHUNK), acc(CHUNK), acc(SSM_WIDTH)),
        in_specs=[row(SSM_WIDTH, col=1), row(SSM_WIDTH), row(CONV_CH), row(CONV_CH), row(CHUNK), row(SSM_WIDTH),
                  states_spec]
        + [_full(a.shape) for a in consts] + [pl.BlockSpec(memory_space=pl.ANY)] * len(deps),
        out_specs=(row(SSM_WIDTH), row(CONV_CH), row(CHUNK), _full((8, CONV_CH)), _full((1, CONV_CH)),
                   _full((1, CHUNK)), _full((1, CHUNK)), _full((1, CHUNK)), _full((1, SSM_WIDTH))),
        scratch_shapes=[pltpu.VMEM((nb, 8, CONV_CH), F32), pltpu.VMEM((nb, N_STATE, SSM_WIDTH), F32)],
        compiler_params=_params("arbitrary"))(
            fold(dmix), fold(z), fold(xbc), fold(pre), fold(dtr), fold(y), states, *consts, *deps)
    return (unfold(dz), unfold(dxbc), unfold(ddt), *small_grads)


def _in_bwd(du, dv, dz, dxbc, ddt, w_in, x, dx2, g1, tm, me, riders=(), dep=None):
    t_tok = x.shape[0]
    steps = t_tok // tm

    n_in = [5 + ("mask" in rd) for rd in riders]
    first_in = [sum(n_in[:r]) for r in range(len(riders))]

    def body(me_ref, du_ref, dv_ref, dz_ref, dxbc_ref, ddt_ref, w_ref, x_ref, dx2_ref, g_ref, *rest):
        outs = rest[len(rest) - 2 - 4 * len(riders):]
        gx_ref, dg_ref = outs[:2]
        i = pl.program_id(0)
        dh = None
        for (a, b), ref in zip(_IN_SPLITS, (du_ref, dv_ref, dz_ref, dxbc_ref, ddt_ref)):
            part = _dot(ref[...], w_ref[a:b, :])
            dh = part if dh is None else dh + part
        dn, dg = _rms_bwd(x_ref[...], g_ref[...], dh)
        gx_ref[...] = dx2_ref[...] + dn
        _acc_rows(dg_ref, dg, i == 0)
        for r in range(len(riders)):
            p_ref, own_ref, w_ref_r, m_ref_r, v_ref_r = rest[first_in[r]:first_in[r] + 5]
            g = _sum_parts(me_ref[0], p_ref, own_ref[0])
            if n_in[r] == 6:
                g = g * rest[first_in[r] + 5][...]
            d, mn, vn = _adamw_math(w_ref_r[...], g, m_ref_r[...], v_ref_r[...])
            for o_ref, val in zip(outs[2 + 4 * r:6 + 4 * r], (g, d, mn, vn)):
                o_ref[...] = val

    row = lambda n: pl.BlockSpec((tm, n), lambda i, me_ref: (i, 0))
    whole = lambda shape: pl.BlockSpec(shape, lambda i, me_ref: (0,) * len(shape))
    widths = [b - a for a, b in _IN_SPLITS]
    deps = [] if dep is None else [dep]
    rider_args, rider_specs, rider_out_shapes, rider_out_specs = [], [], [], []
    for rd in riders:
        rows, cols = rd["w"].shape[0] // steps, rd["w"].shape[1]
        blk = pl.BlockSpec((rows, cols), lambda i, me_ref: (i, 0))
        rider_args += [rd["parts"], rd["own"], rd["w"], rd["m"], rd["v"]]
        rider_specs += [pl.BlockSpec((N_DEV, rows, cols), lambda i, me_ref: (0, i, 0)),
                        pl.BlockSpec((1, rows, cols), lambda i, me_ref: (me_ref[0], i, 0)), blk, blk, blk]
        if "mask" in rd:
            rider_args.append(rd["mask"])
            rider_specs.append(whole((rows, cols)))
        rider_out_shapes += [jax.ShapeDtypeStruct(rd["w"].shape, F32)] * 4
        rider_out_specs += [blk] * 4
    outs = pl.pallas_call(
        body, name="in_bwd",
        out_shape=(jax.ShapeDtypeStruct((t_tok, D_MODEL), F32), jax.ShapeDtypeStruct((1, D_MODEL), F32),
                   *rider_out_shapes),
        grid_spec=pltpu.PrefetchScalarGridSpec(
            num_scalar_prefetch=1, grid=(steps,),
            in_specs=[row(n) for n in widths] + [whole((IN_PAD, D_MODEL)), row(D_MODEL), row(D_MODEL),
                                                 whole((1, D_MODEL))] + rider_specs
            + [pl.BlockSpec(memory_space=pl.ANY)] * len(deps),
            out_specs=(row(D_MODEL), whole((1, D_MODEL)), *rider_out_specs)),
        compiler_params=_params("arbitrary"))(me, du, dv, dz, dxbc, ddt, w_in, x, dx2, g1, *rider_args, *deps)
    return outs[0], outs[1], [tuple(outs[2 + 4 * r:6 + 4 * r]) for r in range(len(riders))]


def _pad_lanes(a, n):
    return jnp.pad(a, ((0, 0), (0, n - a.shape[1])))


def _local_step(x, target, seq, small, hooks, first_dep=None):
    t_tok = x.shape[0]
    tm = min(TOKEN_TILE, t_tok)
    avg, expand, expand_t, tril, triu = _const_mats()
    g1, g2, g3, g4 = (small[k].reshape(1, D_MODEL) for k in
                      ("norm_mix_pre", "norm_mix_post", "norm_ffn_pre", "norm_ffn_post"))
    tie = (lambda a: a) if first_dep is None else (lambda a: a + first_dep[0, 0])
    lnw = tie(small["gm_ln_w"]).reshape(1, GM_WIDTH)
    lnb = tie(small["gm_ln_b"]).reshape(1, GM_WIDTH)
    causal = jnp.tril(jnp.ones((CHUNK, CHUNK), F32))
    wm = tie(small["gm_w_s"]) * causal
    pair = lambda w: w.reshape(4, 2, CHUNK, CHUNK).transpose(0, 2, 1, 3).reshape(4, CHUNK, 2 * CHUNK).astype(BF16)
    wcat = pair(wm)
    wtcat = pair(jnp.swapaxes(wm, 1, 2))
    bias = jnp.repeat(tie(small["gm_b_s"]).T, HEAD_DIM, axis=1)
    cb = small["conv_b"].reshape(1, CONV_CH)
    dtb = _pad_lanes(tie(small["dt_bias"]).reshape(1, N_HEADS), CHUNK)
    alog = _pad_lanes(tie(small["a_log"]).reshape(1, N_HEADS), CHUNK)
    dskip_exp = jnp.repeat(tie(small["d_skip"]).reshape(1, N_HEADS), HEAD_DIM, axis=1)
    nw = small["ssm_norm_w"].reshape(1, SSM_WIDTH)

    h1 = _prenorm(x, g1, tm, hooks.get("prenorm_after", first_dep))
    w_in_t, conv_w = hooks["mixer_weights"]((h1, lnw, lnb, wcat, wtcat, bias, dtb, alog, dskip_exp))
    tall = min(2 * tm, t_tok)
    u, v, z, xbc, dtr = _in_proj(h1, w_in_t, tall)
    mix_a = _gmlp_fwd(u, v, lnw, lnb, wcat, bias, avg)
    dep = hooks["gmlp_done"](mix_a) if "gmlp_done" in hooks else None
    mix_b, y_pre, states, pre = _ssd_fwd(z, xbc, dtr, conv_w, cb, dtb, alog, dskip_exp, nw, expand, tril, seq, dep)
    w_out, dep = hooks["mixers_done"](mix_b)
    o, x2, h3 = _out_proj(mix_a, mix_b, w_out, x, g2, g3, tall, dep)
    w_up, w_down = hooks["mlp_weights"](h3)
    tf = FF_TILE
    ra, dd, dy, dg4, loss = _mlp_fwd(h3, w_up, w_down, x2, target, g4, tm, tf)

    da, dx2, do, dg3, dg2 = _mlp_bwd(dd, w_down, ra, w_up, x2, dy, o, g3, g2, tm, tf)
    g_w_down = _wgrad(ra, dd, None, WGRAD_TILE, D_MODEL, t_tok, True, "wgrad_down")
    g_w_up = _wgrad(h3, da, N_DEV, D_MODEL, D_FF // N_DEV, t_tok, False, "wgrad_up")
    dep = hooks["mlp_grads"](g_w_down, g_w_up)
    dmix = _dmix(do, w_out, tall, dep)
    g_w_out = _wgrad_pieces(do, (mix_a, mix_b), WGRAD_TILE, "wgrad_out", dep)
    du, dv, dws, dbt, dlnw, dlnb = _gmlp_bwd(dmix, u, v, lnw, lnb, wcat, wtcat, bias, avg, expand_t)
    dep = hooks["gmlp_grads"](g_w_out, dws)
    dz, dxbc, ddt, dcw, dcb, ddtb, dalog, ddsk, dnw = _ssd_bwd(
        dmix, z, xbc, pre, dtr, y_pre, states, conv_w, cb, dtb, alog, dskip_exp, nw, expand, expand_t, tril, triu, seq,
        dep)
    g_w_in = _wgrad_in_chunked(h1, (du, dv, dz, dxbc, ddt), WGRAD_TILE, t_tok // 2, dep)
    dep = hooks["in_grads"](g_w_in, dcw[0:4])
    riders = hooks["arrived_updates"](dep) if "arrived_updates" in hooks else []
    me = hooks.get("me", jnp.zeros((1,), jnp.int32))
    grad_x, dg1, updates = _in_bwd(du, dv, dz, dxbc, ddt, w_in_t, x, dx2, g1, tm, me, riders, dep)

    grads = dict(
        updates=updates,
        w_in=g_w_in, w_out=g_w_out, w_up=g_w_up, w_down=g_w_down, conv_w=dcw[0:4],
        norm_mix_pre=dg1, norm_mix_post=dg2, norm_ffn_pre=dg3, norm_ffn_post=dg4, gm_ln_w=dlnw, gm_ln_b=dlnb,
        gm_w_s=dws, gm_b_s=dbt, conv_b=dcb, dt_bias=ddtb, a_log=dalog, d_skip=ddsk, ssm_norm_w=dnw)
    return loss[0, 0], grad_x, grads


_WEIGHTS = ("norm_mix_pre", "w_in", "gm_ln_w", "gm_ln_b", "gm_w_s", "gm_b_s", "conv_w", "conv_b", "dt_bias", "a_log",
            "d_skip", "ssm_norm_w", "w_out", "norm_mix_post", "norm_ffn_pre", "w_up", "w_down", "norm_ffn_post")
_SLAB_ROWS = (("norm_mix_pre", 1024), ("norm_mix_post", 1024), ("norm_ffn_pre", 1024), ("norm_ffn_post", 1024),
              ("conv_b", 1024), ("ssm_norm_w", 512), ("gm_ln_w", 512), ("gm_ln_b", 512), ("dt_bias", 8), ("a_log", 8),
              ("d_skip", 8))
_SLAB_LOSS_ROW = len(_SLAB_ROWS)
_SLAB_BS_ROW = 16
_SMALL_PARAMS = tuple(name for name, _ in _SLAB_ROWS) + ("gm_b_s",)
_LN_PARAMS = ("gm_ln_w", "gm_ln_b")


_SLAB_CONV_ROW = _SLAB_LOSS_ROW + 1


def _pack_slab(g, loss_part):
    rows = [_pad_lanes(g[name], D_MODEL) for name, _ in _SLAB_ROWS]
    rows.append(jnp.broadcast_to(loss_part, (1, D_MODEL)))
    rows.append(g["conv_w"])
    assert sum(r.shape[0] for r in rows) == _SLAB_BS_ROW
    rows.append(_pad_lanes(g["gm_b_s"].T[0:N_HEADS], D_MODEL))
    return jnp.concatenate(rows, axis=0)


def _adamw_slab(parts, me, w, m, v):
    names = _SMALL_PARAMS + ("conv_w",)
    shapes = [w[k].shape for k in names]
    unfold = np.zeros((GM_WIDTH, HEAD_DIM), np.float32)
    for h in range(N_HEADS):
        unfold[h * HEAD_DIM:(h + 1) * HEAD_DIM, :] = np.eye(HEAD_DIM)
    unfold = jnp.asarray(unfold, dtype=BF16)
    n = len(names)
    shard = CONV_CH // N_DEV

    def body(me_ref, p_ref, unfold_ref, *refs):
        w_refs, m_refs, v_refs = refs[:n], refs[n:2 * n], refs[2 * n:3 * n]
        outs = refs[3 * n:]
        g_all = p_ref[0]
        for j in range(1, N_DEV):
            g_all = g_all + p_ref[j]
        lane = lax.broadcasted_iota(jnp.int32, (N_HEADS, GM_WIDTH), 1)
        head = lax.broadcasted_iota(jnp.int32, (N_HEADS, GM_WIDTH), 0)
        own_lanes = jnp.logical_and(lane >= head * HEAD_DIM, lane < (head + 1) * HEAD_DIM)
        mine = pl.ds(pl.multiple_of(me_ref[0] * shard, shard), shard)
        for i, name in enumerate(names):
            if name == "gm_b_s":
                g = g_all[_SLAB_BS_ROW:_SLAB_BS_ROW + N_HEADS, 0:CHUNK]
            elif name == "conv_w":
                g = p_ref[0, _SLAB_CONV_ROW:_SLAB_CONV_ROW + 4, mine]
                for j in range(1, N_DEV):
                    g = g + p_ref[j, _SLAB_CONV_ROW:_SLAB_CONV_ROW + 4, mine]
            else:
                row = [r for r, (k, _) in enumerate(_SLAB_ROWS) if k == name][0]
                g = g_all[row:row + 1, 0:dict(_SLAB_ROWS)[name]]
                if name in _LN_PARAMS:
                    g = _split_dot(jnp.where(own_lanes, g, 0.0), unfold_ref[...], 3)
            d, mn, vn = _adamw_math(w_refs[i][...], g, m_refs[i][...], v_refs[i][...])
            for o_ref, val in zip(outs[4 * i:4 * i + 4], (g, d, mn, vn)):
                o_ref[...] = val
        outs[-1][...] = g_all[_SLAB_LOSS_ROW:_SLAB_LOSS_ROW + 1, 0:128]

    def whole(shape):
        nd = len(shape)
        return pl.BlockSpec(shape, lambda i, me_ref: (0,) * nd)

    ins = [parts, unfold] + [d[k] for d in (w, m, v) for k in names]
    out_shape = tuple(jax.ShapeDtypeStruct(s, F32) for s in shapes for _ in range(4)) + (
        jax.ShapeDtypeStruct((1, 128), F32),)
    outs = pl.pallas_call(
        body, name="adamw_small", out_shape=out_shape,
        grid_spec=pltpu.PrefetchScalarGridSpec(
            num_scalar_prefetch=1, grid=(1,), in_specs=[whole(a.shape) for a in ins],
            out_specs=tuple(whole(s.shape) for s in out_shape)),
        compiler_params=_params("arbitrary"))(me, *ins)
    return {k: tuple(outs[4 * i:4 * i + 4]) for i, k in enumerate(names)}, outs[-1][0, 0]


def kernel(x, norm_mix_pre, w_in, gm_ln_w, gm_ln_b, gm_w_s, gm_b_s, conv_w, conv_b, dt_bias, a_log, d_skip, ssm_norm_w, w_out, norm_mix_post, norm_ffn_pre, w_up, w_down, norm_ffn_post, loss_target, m_norm_mix_pre, m_w_in, m_gm_ln_w, m_gm_ln_b, m_gm_w_s, m_gm_b_s, m_conv_w, m_conv_b, m_dt_bias, m_a_log, m_d_skip, m_ssm_norm_w, m_w_out, m_norm_mix_post, m_norm_ffn_pre, m_w_up, m_w_down, m_norm_ffn_post, v_norm_mix_pre, v_w_in, v_gm_ln_w, v_gm_ln_b, v_gm_w_s, v_gm_b_s, v_conv_w, v_conv_b, v_dt_bias, v_a_log, v_d_skip, v_ssm_norm_w, v_w_out, v_norm_mix_post, v_norm_ffn_pre, v_w_up, v_w_down, v_norm_ffn_post):
    w = dict(norm_mix_pre=norm_mix_pre, w_in=w_in, gm_ln_w=gm_ln_w, gm_ln_b=gm_ln_b, gm_w_s=gm_w_s, gm_b_s=gm_b_s, conv_w=conv_w, conv_b=conv_b, dt_bias=dt_bias, a_log=a_log, d_skip=d_skip, ssm_norm_w=ssm_norm_w, w_out=w_out, norm_mix_post=norm_mix_post, norm_ffn_pre=norm_ffn_pre, w_up=w_up, w_down=w_down, norm_ffn_post=norm_ffn_post)
    m = dict(norm_mix_pre=m_norm_mix_pre, w_in=m_w_in, gm_ln_w=m_gm_ln_w, gm_ln_b=m_gm_ln_b, gm_w_s=m_gm_w_s, gm_b_s=m_gm_b_s, conv_w=m_conv_w, conv_b=m_conv_b, dt_bias=m_dt_bias, a_log=m_a_log, d_skip=m_d_skip, ssm_norm_w=m_ssm_norm_w, w_out=m_w_out, norm_mix_post=m_norm_mix_post, norm_ffn_pre=m_norm_ffn_pre, w_up=m_w_up, w_down=m_w_down, norm_ffn_post=m_norm_ffn_post)
    v = dict(norm_mix_pre=v_norm_mix_pre, w_in=v_w_in, gm_ln_w=v_gm_ln_w, gm_ln_b=v_gm_ln_b, gm_w_s=v_gm_w_s, gm_b_s=v_gm_b_s, conv_w=v_conv_w, conv_b=v_conv_b, dt_bias=v_dt_bias, a_log=v_a_log, d_skip=v_d_skip, ssm_norm_w=v_ssm_norm_w, w_out=v_w_out, norm_mix_post=v_norm_mix_post, norm_ffn_pre=v_norm_ffn_pre, w_up=v_w_up, w_down=v_w_down, norm_ffn_post=v_norm_ffn_post)
    n_batch, seq, _ = x.shape
    shard_in = IN_COLS // N_DEV

    me = (4 * lax.axis_index("x") + 2 * lax.axis_index("y") + lax.axis_index("c")).astype(jnp.int32).reshape(1)

    def in_slot(own):
        return lax.dynamic_update_slice(lax.empty((N_DEV,) + own.shape, own.dtype), own[None],
                                        (me[0],) + (0,) * own.ndim)

    lying = lambda t: jnp.transpose(t, (2, 0, 1))
    first = [_cast_to_slot(lying(w_in), me, shard_in, "cast_w_in"), in_slot(conv_w[0])]
    ici_1, tok_ici_1 = _exchange_start(first, [True] * 2, _SAME_CORE_PEERS, "gather_mix_ici_start")
    cast_out = _cast_to_slot(w_out[0], me, 128, "cast_w_out", dep=tok_ici_1)
    cast_up = _cast_to_slot(w_up[0], me, 1024, "cast_w_up", cols=True, dep=cast_out)
    second = [cast_out, cast_up, _cast_to_slot(w_down[0], me, 512, "cast_w_down", dep=cast_up)]
    gathering = {}

    def mixer_weights(after):
        bufs = [buf for buf, _ in _exchange_wait(ici_1, after, "gather_mix_ici_wait")]
        d2d_1, tok_d2d_1 = _exchange_start(bufs, [True] * 2, _SIBLING_FORWARD, "gather_mix_d2d_start")
        gathering["late_ici"], tok_ici_2 = _exchange_start(
            second, [True] * 3, _SAME_CORE_PEERS, "gather_late_ici_start", dep=tok_d2d_1)
        (_, ag_in), (_, ag_conv) = _exchange_wait(d2d_1, tok_ici_2, "gather_mix_d2d_wait")
        w_in_t = _stack_shards(ag_in, IN_PAD, STACK_TILE, "stack_w_in")
        return w_in_t, ag_conv.transpose(1, 0, 2).reshape(4, CONV_CH)

    def gmlp_done(after):
        ((buf, _),) = _exchange_wait(gathering["late_ici"], after, "gather_out_ici_wait", only=(0,))
        gathering["out"], tok = _exchange_start([buf], [True], _SIBLING_FORWARD, "gather_out_d2d_start")
        return tok

    def mixers_done(after):
        bufs = [buf for buf, _ in _exchange_wait(gathering["late_ici"], after, "gather_mlp_ici_wait", only=(1, 2))]
        gathering["mlp"], tok = _exchange_start(bufs, [True] * 2, _SIBLING_FORWARD, "gather_mlp_d2d_start")
        ((_, ag_out),) = _exchange_wait(gathering["out"], tok, "gather_out_d2d_wait")
        return ag_out.reshape(D_MODEL, D_MODEL), tok

    def mlp_weights(after):
        (_, ag_up), (_, ag_down) = _exchange_wait(gathering["mlp"], after, "gather_mlp_d2d_wait")
        return ag_up, ag_down.reshape(D_FF, D_MODEL)

    sent = {}

    def mlp_grads(g_w_down, g_w_up):
        sent["mlp"], tok = _exchange_start(
            [g_w_down.reshape(N_DEV, D_FF // N_DEV, D_MODEL), g_w_up], [False, False], _ALL_PEERS, "grads_mlp_start")
        return tok

    def gmlp_grads(g_w_out, g_w_s):
        sent["gmlp"], tok = _exchange_start(
            [g_w_out.reshape(N_DEV, D_MODEL // N_DEV, D_MODEL), in_slot(g_w_s.astype(BF16))], [False, True], _ALL_PEERS,
            "grads_gmlp_start")
        return tok

    def in_grads(g_w_in_t, g_conv_w):
        sent["in"], tok = _exchange_start([g_w_in_t], [False], _ALL_PEERS, "grads_in_start")
        return tok

    def arrived_updates(after):
        (own_down, p_down), (own_up, p_up) = _exchange_wait(sent["mlp"], after, "grads_mlp_wait")
        (own_out, p_out), (_, p_ws) = _exchange_wait(sent["gmlp"], own_up, "grads_gmlp_wait")
        rows = lambda t: t.reshape(t.shape[:-3] + (N_HEADS * CHUNK, CHUNK))
        return [dict(parts=p_up, own=own_up, w=w_up[0], m=m_w_up[0], v=v_w_up[0]),
                dict(parts=p_down, own=own_down, w=w_down[0], m=m_w_down[0], v=v_w_down[0]),
                dict(parts=p_out, own=own_out, w=w_out[0], m=m_w_out[0], v=v_w_out[0]),
                dict(parts=rows(p_ws), own=rows(p_ws), w=rows(gm_w_s[0]), m=rows(m_gm_w_s[0]), v=rows(v_gm_w_s[0]),
                     mask=jnp.tril(jnp.ones((CHUNK, CHUNK), F32)))]

    small = {k: w[k][0] for k in _SMALL_PARAMS + ("gm_w_s",)}
    loss_part, grad_x, g = _local_step(
        x.reshape(n_batch * seq, D_MODEL), loss_target.reshape(n_batch * seq, D_MODEL), seq, small,
        dict(mixer_weights=mixer_weights, gmlp_done=gmlp_done, mixers_done=mixers_done, mlp_weights=mlp_weights,
             mlp_grads=mlp_grads, gmlp_grads=gmlp_grads, in_grads=in_grads, arrived_updates=arrived_updates, me=me,
             prenorm_after=second[2]), first_dep=tok_ici_1)

    sent_rows, tok_rows = _exchange_start([in_slot(_pack_slab(g, loss_part))], [True], _ALL_PEERS, "grads_rows_start")
    res = dict(zip(("w_up", "w_down", "w_out", "gm_w_s"), g["updates"]))
    ((own_in, p_in),) = _exchange_wait(sent["in"], tok_rows, "grads_in_wait")
    upd_in = _adamw_reduce(p_in, own_in, me, lying(w_in), lying(m_w_in), lying(v_w_in), "adamw_w_in")
    res["w_in"] = tuple(jnp.transpose(t, (1, 2, 0)) for t in upd_in)
    ((_, p_rows),) = _exchange_wait(sent_rows, upd_in[1], "grads_rows_wait")
    flat = lambda t: t[0] if t.ndim == 3 else t
    small_res, loss = _adamw_slab(
        p_rows, me, *({k: flat(d[k]) for k in _SMALL_PARAMS + ("conv_w",)} for d in (w, m, v)))
    res.update(small_res)
    res = {k: tuple(r.reshape(w[k].shape) for r in res[k]) for k in _WEIGHTS}

    outs = [loss, grad_x.reshape(x.shape)]
    for part in range(4):
        outs.extend(res[k][part] for k in _WEIGHTS)
    return tuple(outs)
```

```python
import functools

import jax
import jax.numpy as jnp
import numpy as np
from jax import lax
from jax.experimental import pallas as pl
from jax.experimental.pallas import tpu as pltpu

F32 = jnp.float32
BF16 = jnp.bfloat16

D_MODEL = 1024
GM_WIDTH = 512
SSM_WIDTH = 512
CONV_CH = 1024
N_HEADS = 8
HEAD_DIM = 64
N_STATE = 128
CHUNK = 128
D_FF = 4096
IN_COLS = 2568
IN_PAD = 2688
N_DEV = 8
EPS = 1e-6
ADAM_LR, ADAM_B1, ADAM_B2, ADAM_EPS, ADAM_WD, ADAM_STEP = 0.001, 0.9, 0.999, 1e-08, 0.01, 10
VMEM_LIMIT_BYTES = 56 * 1024 * 1024
TOKEN_TILE = 512
FF_TILE = 2048
WGRAD_TILE = 512
STACK_TILE = 256
_NT = (((1,), (1,)), ((), ()))
_TN = (((0,), (0,)), ((), ()))


def _params(*sem):
    return pltpu.CompilerParams(dimension_semantics=sem or None, vmem_limit_bytes=VMEM_LIMIT_BYTES)


def _dot(a, b, dims=None):
    if dims is None:
        return jnp.dot(a, b, preferred_element_type=F32)
    return lax.dot_general(a, b, dims, preferred_element_type=F32)


def _split_terms(x, terms):
    out, rem = [], x
    for i in range(terms):
        hi = rem.astype(BF16)
        out.append(hi)
        if i + 1 < terms:
            rem = rem - hi.astype(F32)
    return out


def _split_dot(x, m, terms):
    acc = None
    for hi in _split_terms(x, terms):
        part = _dot(hi, m)
        acc = part if acc is None else acc + part
    return acc


def _split_dot_left(m, x, terms):
    acc = None
    for hi in _split_terms(x, terms):
        part = _dot(m, hi)
        acc = part if acc is None else acc + part
    return acc


def _gelu_and_grad(x):
    c = 0.7978845608028654
    inner = c * (x + 0.044715 * x * x * x)
    t = jnp.tanh(inner)
    g = 0.5 * x * (1.0 + t)
    dg = 0.5 * (1.0 + t) + 0.5 * x * (1.0 - t * t) * c * (1.0 + 3.0 * 0.044715 * x * x)
    return g, dg


def _softplus(x):
    return jnp.maximum(x, 0.0) + jnp.log(1.0 + jnp.exp(-jnp.abs(x)))


def _rsum(x):
    return jnp.sum(x, axis=0, keepdims=True)


def _acc_rows(ref, part, first):
    val = jnp.broadcast_to(part, ref.shape)

    @pl.when(first)
    def _():
        ref[...] = val

    @pl.when(jnp.logical_not(first))
    def _():
        ref[...] += val


def _rms_bwd(n, g, dout):
    r = lax.rsqrt(jnp.mean(n * n, axis=-1, keepdims=True) + EPS)
    nh = n * r
    dg = dout * g
    dn = r * (dg - nh * jnp.mean(dg * nh, axis=-1, keepdims=True))
    return dn, _rsum(dout * nh)


def _const_mats():
    avg = np.kron(np.eye(4), np.full((HEAD_DIM, HEAD_DIM), 1.0 / HEAD_DIM))
    expand = np.zeros((CHUNK, SSM_WIDTH), np.float32)
    for h in range(N_HEADS):
        expand[h, h * HEAD_DIM:(h + 1) * HEAD_DIM] = 1.0
    tril = np.tril(np.ones((CHUNK, CHUNK), np.float32))
    as_bf16 = lambda a: jnp.asarray(a, dtype=BF16)
    return as_bf16(avg), as_bf16(expand), as_bf16(expand.T), as_bf16(tril), as_bf16(tril.T)


def _full(shape):
    nd = len(shape)
    return pl.BlockSpec(shape, lambda *_: (0,) * nd)


_HBM = pl.BlockSpec(memory_space=pltpu.HBM)
_SEM = pl.BlockSpec(memory_space=pltpu.SEMAPHORE)
_ALL_PEERS = tuple((k, 0) for k in range(1, N_DEV))
_SAME_CORE_PEERS = ((2, 0), (4, 0), (6, 0))
_SIBLING_FORWARD = ((1, 0), (1, 2), (1, 4), (1, 6))


def _flip(j, k):
    for bit in (4, 2, 1):
        if k & bit:
            j = j + bit - 2 * (j & bit)
    return j


def _copies(src, land, send_sems, recv_sems, hops, slots=None):
    x, y, c = lax.axis_index("x"), lax.axis_index("y"), lax.axis_index("c")
    me = 4 * x + 2 * y + c
    slots = range(len(src)) if slots is None else slots
    out = []
    for t in range(len(src)):
        for i, (k, b) in enumerate(hops):
            pos = (1 - x if k & 4 else x, 1 - y if k & 2 else y, 1 - c if k & 1 else c)
            peer = _flip(me, k)
            sem = slots[t] * len(hops) + i
            mk = functools.partial(pltpu.make_async_remote_copy, send_sem=send_sems.at[sem], recv_sem=recv_sems.at[sem],
                                   device_id=pos, device_id_type=pl.DeviceIdType.MESH)
            if land[t] is None and src[t].shape[0] != N_DEV:
                width = src[t].shape[1] // N_DEV
                slab = lambda j: src[t].at[:, pl.ds(pl.multiple_of(j * width, 128), width)]
                mine = functools.partial(mk, src_ref=slab(_flip(me, b)), dst_ref=slab(_flip(me, b)))
                theirs = functools.partial(mk, src_ref=slab(_flip(peer, b)), dst_ref=slab(_flip(peer, b)))
            elif land[t] is None:
                mine = functools.partial(mk, src_ref=src[t].at[_flip(me, b)], dst_ref=src[t].at[_flip(me, b)])
                theirs = functools.partial(mk, src_ref=src[t].at[_flip(peer, b)], dst_ref=src[t].at[_flip(peer, b)])
            else:
                assert b == 0
                mine = functools.partial(mk, src_ref=src[t].at[peer], dst_ref=land[t].at[me])
                theirs = functools.partial(mk, src_ref=src[t].at[peer], dst_ref=land[t].at[peer])
            out.append((mine, theirs))
    return out


def _exchange_start(srcs, inplace, peers, name, dep=None):
    n = len(srcs)
    lands = [None if ip else pltpu.with_memory_space_constraint(lax.empty(s.shape, s.dtype), pltpu.HBM)
             for s, ip in zip(srcs, inplace)]
    real_lands = [l for l in lands if l is not None]
    n_l = len(real_lands)
    deps = [] if dep is None else [dep]

    def body(*refs):
        src = refs[:n]
        land_refs = list(refs[n:n + n_l])
        send_sems, recv_sems = refs[n + n_l + len(deps)], refs[n + n_l + len(deps) + 1]
        token = refs[-1]
        land = [None if ip else land_refs.pop(0) for ip in inplace]
        for mine, _ in _copies(src, land, send_sems, recv_sems, peers):
            mine().start()
        token[...] = jnp.zeros_like(token)

    sem_t = pltpu.SemaphoreType.DMA((n * len(peers),))
    outs = pl.pallas_call(
        body, name=name,
        out_shape=(sem_t, sem_t) + tuple(pltpu.HBM(a.shape, a.dtype) for a in list(srcs) + real_lands)
        + (jax.ShapeDtypeStruct((8, 128), F32),),
        in_specs=[_HBM] * (n + n_l) + [pl.BlockSpec(memory_space=pl.ANY)] * len(deps),
        out_specs=(_SEM, _SEM) + (_HBM,) * (n + n_l) + (pl.BlockSpec(memory_space=pltpu.VMEM),),
        input_output_aliases={i: 2 + i for i in range(n + n_l)},
        compiler_params=pltpu.CompilerParams(has_side_effects=pltpu.SideEffectType.DATAFLOW_SIDE_EFFECTING),
    )(*[pltpu.with_memory_space_constraint(s, pltpu.HBM) for s in srcs], *real_lands, *deps)
    handle = dict(send=outs[0], recv=outs[1], srcs=outs[2:2 + n], lands=outs[2 + n:2 + n + n_l], inplace=inplace,
                  peers=peers)
    return handle, outs[-1]


def _exchange_wait(handle, after, name, only=None):
    srcs, lands, inplace, peers = handle["srcs"], handle["lands"], handle["inplace"], handle["peers"]
    slots = None
    if only is not None:
        assert all(inplace)
        slots, srcs, inplace = list(only), [srcs[t] for t in only], [True] * len(only)
    n, n_l = len(srcs), len(lands)
    after = after if isinstance(after, tuple) else (after,)

    def body(*refs):
        src = refs[:n]
        land_refs = list(refs[n:n + n_l])
        send_sems, recv_sems = refs[n + n_l], refs[n + n_l + 1]
        land = [None if ip else land_refs.pop(0) for ip in inplace]
        for mine, theirs in _copies(src, land, send_sems, recv_sems, peers, slots):
            mine().wait_send()
            theirs().wait_recv()

    outs = pl.pallas_call(
        body, name=name, out_shape=tuple(pltpu.HBM(a.shape, a.dtype) for a in list(srcs) + list(lands)),
        in_specs=[_HBM] * (n + n_l) + [_SEM, _SEM] + [pl.BlockSpec(memory_space=pl.ANY)] * len(after),
        out_specs=(_HBM,) * (n + n_l), input_output_aliases={i: i for i in range(n + n_l)},
        compiler_params=pltpu.CompilerParams(has_side_effects=pltpu.SideEffectType.DATAFLOW_SIDE_EFFECTING),
    )(*srcs, *lands, handle["send"], handle["recv"], *after)
    res, land_out = [], list(outs[n:])
    for t in range(n):
        res.append((outs[t], outs[t] if inplace[t] else land_out.pop(0)))
    return res


def _cast_to_slot(w, me, rows, name, cols=False, dep=None):
    r, cdim = w.shape[0], w.shape[-1]
    deps = [] if dep is None else [dep]

    def body(me_ref, w_ref, *rest):
        o_ref = rest[-1]
        if cols:
            o_ref[...] = w_ref[...].astype(BF16)
        else:
            o_ref[0] = w_ref[...].reshape(rows, cdim).astype(BF16)

    if cols:
        out_shape = jax.ShapeDtypeStruct((r, N_DEV * cdim), BF16)
        out_spec = pl.BlockSpec((rows, cdim), lambda i, me_ref: (i, me_ref[0]))
    else:
        out_shape = jax.ShapeDtypeStruct((N_DEV, r, cdim), BF16)
        out_spec = pl.BlockSpec((1, rows, cdim), lambda i, me_ref: (me_ref[0], i, 0))
    return pl.pallas_call(
        body, name=name, out_shape=out_shape,
        grid_spec=pltpu.PrefetchScalarGridSpec(
            num_scalar_prefetch=1, grid=(r // rows,),
            in_specs=[pl.BlockSpec((rows, cdim), lambda i, me_ref: (i, 0)) if w.ndim == 2 else
                      pl.BlockSpec((rows, 1, cdim), lambda i, me_ref: (i, 0, 0))]
            + [pl.BlockSpec(memory_space=pl.ANY)] * len(deps), out_specs=out_spec),
        compiler_params=_params("parallel"))(me, w, *deps)


def _stack_shards(blocks, rows, bn, name):
    n, r, cdim = blocks.shape

    def body(b_ref, o_ref, acc_ref):
        acc_ref[n * r:, :] = jnp.zeros((rows - n * r, bn), F32)
        for j in range(n):
            acc_ref[r * j:r * (j + 1), :] = b_ref[j].astype(F32)
        o_ref[...] = acc_ref[...].astype(BF16)

    return pl.pallas_call(
        body, name=name, grid=(cdim // bn,), out_shape=jax.ShapeDtypeStruct((rows, cdim), BF16),
        in_specs=[pl.BlockSpec((n, r, bn), lambda i: (0, 0, i))], out_specs=pl.BlockSpec((rows, bn), lambda i: (0, i)),
        scratch_shapes=[pltpu.VMEM((rows, bn), F32)], compiler_params=_params("parallel"))(blocks)


def _adamw_math(w, g, m, v):
    m = ADAM_B1 * m + (1.0 - ADAM_B1) * g
    v = ADAM_B2 * v + (1.0 - ADAM_B2) * (g * g)
    m_hat = m / (1.0 - ADAM_B1 ** ADAM_STEP)
    v_hat = v / (1.0 - ADAM_B2 ** ADAM_STEP)
    delta = -ADAM_LR * (m_hat / (jnp.sqrt(v_hat) + ADAM_EPS) + ADAM_WD * w)
    return delta, m, v


def _sum_parts(me, p_ref, own):
    g = None
    for j in range(N_DEV):
        term = (p_ref[j] if own is None else jnp.where(me == j, own, p_ref[j])).astype(F32)
        g = term if g is None else g + term
    return g


def _adamw_reduce(parts, own, me, w, m, v, bn, name):
    r, _, cdim = w.shape

    def body(me_ref, p_ref, own_ref, w_ref, m_ref, v_ref, g_out, d_out, m_out, v_out):
        g = _sum_parts(me_ref[0], p_ref, own_ref[0]).reshape(r, 1, bn)
        d, mn, vn = _adamw_math(w_ref[...], g, m_ref[...], v_ref[...])
        g_out[...] = g
        d_out[...] = d
        m_out[...] = mn
        v_out[...] = vn

    blk = pl.BlockSpec((r, 1, bn), lambda j, me_ref: (0, 0, j))
    return pl.pallas_call(
        body, name=name, out_shape=(jax.ShapeDtypeStruct(w.shape, F32),) * 4,
        grid_spec=pltpu.PrefetchScalarGridSpec(
            num_scalar_prefetch=1, grid=(cdim // bn,),
            in_specs=[pl.BlockSpec((N_DEV, r, bn), lambda j, me_ref: (0, 0, j)),
                      pl.BlockSpec((1, r, bn), lambda j, me_ref: (me_ref[0], 0, j)), blk, blk, blk],
            out_specs=(blk,) * 4),
        compiler_params=_params("parallel"))(me, parts, own, w, m, v)


_IN_SPLITS = ((0, 512), (512, 1024), (1024, 1536), (1536, 2560), (2560, IN_PAD))


def _prenorm(x, g1, tm, dep=None):
    t_tok = x.shape[0]
    deps = [] if dep is None else [dep]

    def body(x_ref, g_ref, *rest):
        xv = x_ref[...]
        r = lax.rsqrt(jnp.mean(xv * xv, axis=-1, keepdims=True) + EPS)
        rest[-1][...] = (xv * r * g_ref[...]).astype(BF16)

    row = pl.BlockSpec((tm, D_MODEL), lambda i: (i, 0))
    return pl.pallas_call(
        body, name="prenorm", grid=(t_tok // tm,), out_shape=jax.ShapeDtypeStruct((t_tok, D_MODEL), BF16),
        in_specs=[row, _full((1, D_MODEL))] + [pl.BlockSpec(memory_space=pl.ANY)] * len(deps), out_specs=row,
        compiler_params=_params("parallel"))(x, g1, *deps)


def _in_proj(h1, w_in, tm):
    t_tok = h1.shape[0]

    def body(h_ref, w_ref, *outs):
        h = h_ref[...]
        for (a, b), o_ref in zip(_IN_SPLITS, outs):
            o_ref[...] = _dot(h, w_ref[a:b, :], _NT).astype(o_ref.dtype)

    row = lambda n: pl.BlockSpec((tm, n), lambda i: (i, 0))
    widths = [b - a for a, b in _IN_SPLITS]
    dtypes = (BF16, BF16, BF16, F32, F32)
    return pl.pallas_call(
        body, name="in_proj", grid=(t_tok // tm,),
        out_shape=tuple(jax.ShapeDtypeStruct((t_tok, n), dt) for n, dt in zip(widths, dtypes)),
        in_specs=[row(D_MODEL), _full((IN_PAD, D_MODEL))], out_specs=tuple(row(n) for n in widths),
        compiler_params=_params("parallel"))(h1, w_in)


def _lane_masks():
    lane = lax.broadcasted_iota(jnp.int32, (1, 2 * HEAD_DIM), 1)
    left = (lane < HEAD_DIM).astype(F32)
    return left, 1.0 - left


def _stack_pair(v, m_l, m_r):
    return jnp.concatenate([v * m_l, v * m_r], axis=0).astype(BF16)


def _head_mean(x, avg):
    n = avg.shape[0]
    return jnp.concatenate([_split_dot(x[:, n * i:n * (i + 1)], avg, 2) for i in range(x.shape[1] // n)], axis=1)


def _gmlp_common(u, v, lnw, lnb, avg, wcat_ref, bias, m_l, m_r):
    ug, dug = _gelu_and_grad(u)
    vg, dvg = _gelu_and_grad(v)
    mu = _head_mean(vg, avg)
    vc = vg - mu
    var = _head_mean(vc * vc, avg)
    rstd = lax.rsqrt(var + EPS)
    vhat = vc * rstd
    vn = vhat * lnw + lnb
    rows = []
    for r in range(u.shape[0] // CHUNK):
        cols = []
        for j in range(N_HEADS // 2):
            pair = vn[CHUNK * r:CHUNK * (r + 1), 128 * j:128 * (j + 1)]
            cols.append(_dot(wcat_ref[j], _stack_pair(pair, m_l, m_r)))
        rows.append(jnp.concatenate(cols, axis=1) + bias)
    mixed = jnp.concatenate(rows, axis=0)
    return ug, dug, dvg, rstd, vhat, vn, mixed


_GMLP_ROWS = 4 * CHUNK


def _gmlp_fwd(u, v, lnw, lnb, wcat, bias, avg):
    t_tok = u.shape[0]
    tm = min(_GMLP_ROWS, t_tok)

    def body(u_ref, v_ref, lnw_ref, lnb_ref, wcat_ref, bias_ref, avg_ref, o_ref):
        m_l, m_r = _lane_masks()
        ug, _, _, _, _, _, mixed = _gmlp_common(
            u_ref[...].astype(F32), v_ref[...].astype(F32), lnw_ref[...], lnb_ref[...], avg_ref[...], wcat_ref,
            bias_ref[...], m_l, m_r)
        o_ref[...] = (ug * mixed).astype(BF16)

    row = pl.BlockSpec((tm, GM_WIDTH), lambda i: (i, 0))
    return pl.pallas_call(
        body, name="gmlp_fwd", grid=(t_tok // tm,), out_shape=jax.ShapeDtypeStruct((t_tok, GM_WIDTH), BF16),
        in_specs=[row, row, _full((1, GM_WIDTH)), _full((1, GM_WIDTH)), _full(wcat.shape), _full(bias.shape),
                  _full(avg.shape)],
        out_specs=row, compiler_params=_params("parallel"))(u, v, lnw, lnb, wcat, bias, avg)


def _shift_rows(x, edge, j, down):
    groups, cols = x.shape[0] // 8, x.shape[1]
    amount = j if down else 8 - j
    rot = pltpu.roll(x.reshape(groups, 8, cols), amount, axis=1)
    edge_rot = pltpu.roll(edge, amount, axis=0)[None]
    sub = lax.broadcasted_iota(jnp.int32, (1, 8, 1), 1)
    if down:
        out = jnp.where(sub < j, jnp.concatenate([edge_rot, rot[:-1]], axis=0), rot)
    else:
        out = jnp.where(sub < 8 - j, rot, jnp.concatenate([rot[1:], edge_rot], axis=0))
    return out.reshape(x.shape)


def _conv_pre(xbc, tail, cw_ref, cb):
    taps = [_shift_rows(xbc, tail, 3 - k, True) for k in range(3)] + [xbc]
    return cb + cw_ref[0:1, :] * taps[0] + cw_ref[1:2, :] * taps[1] + cw_ref[2:3, :] * taps[2] + cw_ref[3:4, :] * taps[3]


def _ssd_common(pre, dtr, dtb, alog, expand, tril):
    q = CHUNK
    sg = jax.nn.sigmoid(pre)
    act = pre * sg
    lane = lax.broadcasted_iota(jnp.int32, (1, CHUNK), 1)
    a_row = jnp.where(lane < N_HEADS, -jnp.exp(alog), 0.0)
    dtp = dtr + dtb
    dt = _softplus(dtp)
    a_cs = _split_dot_left(tril, dt * a_row, 3)
    a_cs_t = a_cs.T
    dt_exp = _split_dot(dt, expand, 3)
    a_exp = _split_dot(a_cs, expand, 3)
    a_end = a_exp[q - 1:q, :]
    li = lax.broadcasted_iota(jnp.int32, (q, q), 0)
    si = lax.broadcasted_iota(jnp.int32, (q, q), 1)
    causal = si <= li
    decay = []
    for h in range(N_HEADS):
        seg = a_cs[:, h:h + 1] - a_cs_t[h:h + 1, :]
        decay.append(jnp.where(causal, jnp.exp(jnp.minimum(seg, 0.0)), 0.0))
    return dict(pre=pre, sg=sg, act=act, a_row=a_row, dtp=dtp, dt=dt, dt_exp=dt_exp, a_exp=a_exp,
                e=jnp.exp(a_exp), w_end=jnp.exp(a_end - a_exp), cd=jnp.exp(a_end), decay=decay)


def _ssd_specs(t_tok, seq, reverse):
    nb, nc = t_tok // seq, seq // CHUNK

    def chunk(c):
        return nc - 1 - c if reverse else c

    def row(n, col=0):
        return pl.BlockSpec((nb, CHUNK, n), lambda c: (0, chunk(c), col))

    tail = pl.BlockSpec((nb, 8, CONV_CH), lambda c: (0, jnp.maximum(chunk(c) * (CHUNK // 8) - 1, 0), 0))
    states = pl.BlockSpec((nb, 1, N_STATE, SSM_WIDTH), lambda c: (0, chunk(c), 0, 0))
    fold = lambda a: a.reshape(nb, seq, a.shape[-1])
    unfold = lambda a: a.reshape(t_tok, a.shape[-1])
    return nb, nc, row, tail, states, fold, unfold


def _ssd_fwd(z, xbc, dtr, cw, cb, dtb, alog, dskip_exp, nw, expand, tril, seq, dep=None):
    t_tok = z.shape[0]
    nb, nc, row, tail, states_spec, fold, unfold = _ssd_specs(t_tok, seq, False)

    def body(z_ref, xbc_ref, tail_ref, dtr_ref, cw_ref, cb_ref, dtb_ref, alog_ref, dsk_ref, nw_ref, exp_ref,
             tril_ref, o_ref, y_ref, st_ref, pre_ref, state_ref):
        c = pl.program_id(0)

        @pl.when(c == 0)
        def _():
            state_ref[...] = jnp.zeros_like(state_ref)

        m_l, m_r = _lane_masks()
        for s in range(nb):
            pre = _conv_pre(xbc_ref[s], jnp.where(c == 0, 0.0, tail_ref[s]), cw_ref, cb_ref[...])
            pre_ref[s] = pre
            f = _ssd_common(pre, dtr_ref[s], dtb_ref[...], alog_ref[...], exp_ref[...], tril_ref[...])
            act = f["act"]
            xs = act[:, :SSM_WIDTH]
            xdt = xs * f["dt_exp"]
            xw = xdt * f["w_end"]
            state = state_ref[s]
            st_ref[s, 0] = state
            ydiag, yoff, snew = [], [], []
            for g in range(2):
                bg = act[:, 512 + 128 * g:640 + 128 * g].astype(BF16)
                cg = act[:, 768 + 128 * g:896 + 128 * g].astype(BF16)
                cb_mat = _dot(cg, bg, _NT)
                for pr in range(2):
                    h0 = 4 * g + 2 * pr
                    gcat = jnp.concatenate(
                        [(cb_mat * f["decay"][h0]).astype(BF16), (cb_mat * f["decay"][h0 + 1]).astype(BF16)], axis=1)
                    ydiag.append(_dot(gcat, _stack_pair(xdt[:, 64 * h0:64 * h0 + 128], m_l, m_r)))
                yoff.append(_dot(cg, state[:, 256 * g:256 * (g + 1)].astype(BF16)))
                snew.append(_dot(bg, xw[:, 256 * g:256 * (g + 1)].astype(BF16), _TN))
            y = jnp.concatenate(ydiag, axis=1) + f["e"] * jnp.concatenate(yoff, axis=1) + dsk_ref[...] * xs
            state_ref[s] = state * f["cd"] + jnp.concatenate(snew, axis=1)
            y_ref[s] = y
            zv = z_ref[s].astype(F32)
            yg = y * (zv * jax.nn.sigmoid(zv))
            outs = []
            for g in range(2):
                ygg = yg[:, 256 * g:256 * (g + 1)]
                outs.append(ygg * lax.rsqrt(jnp.mean(ygg * ygg, axis=-1, keepdims=True) + EPS))
            o_ref[s] = (jnp.concatenate(outs, axis=1) * nw_ref[...]).astype(BF16)

    consts = [cw, cb, dtb, alog, dskip_exp, nw, expand, tril]
    deps = [] if dep is None else [dep]
    n_in = 4 + len(consts)

    def body_skipping_dep(*refs):
        body(*refs[:n_in], *refs[n_in + len(deps):])

    sd = lambda n, dt: jax.ShapeDtypeStruct((nb, seq, n), dt)
    o, y, states, pre = pl.pallas_call(
        body_skipping_dep, name="ssd_fwd", grid=(nc,),
        out_shape=(sd(SSM_WIDTH, BF16), sd(SSM_WIDTH, F32), jax.ShapeDtypeStruct((nb, nc, N_STATE, SSM_WIDTH), F32),
                   sd(CONV_CH, F32)),
        in_specs=[row(SSM_WIDTH), row(CONV_CH), tail, row(CHUNK)] + [_full(a.shape) for a in consts]
        + [pl.BlockSpec(memory_space=pl.ANY)] * len(deps),
        out_specs=(row(SSM_WIDTH), row(SSM_WIDTH), states_spec, row(CONV_CH)),
        scratch_shapes=[pltpu.VMEM((nb, N_STATE, SSM_WIDTH), F32)],
        compiler_params=_params("arbitrary"))(fold(z), fold(xbc), fold(xbc), fold(dtr), *consts, *deps)
    return unfold(o), unfold(y), states, unfold(pre)


def _out_proj(mix_a, mix_b, w_out, x, g2, g3, tm, dep=None):
    t_tok = x.shape[0]
    deps = [] if dep is None else [dep]

    def body(a_ref, b_ref, w_ref, x_ref, g2_ref, g3_ref, *rest):
        o_ref, x2_ref, h3_ref = rest[-3:]
        o = _dot(a_ref[...], w_ref[0:GM_WIDTH, :]) + _dot(b_ref[...], w_ref[GM_WIDTH:, :])
        o_ref[...] = o
        r2 = lax.rsqrt(jnp.mean(o * o, axis=-1, keepdims=True) + EPS)
        x2 = x_ref[...] + o * r2 * g2_ref[...]
        x2_ref[...] = x2
        r3 = lax.rsqrt(jnp.mean(x2 * x2, axis=-1, keepdims=True) + EPS)
        h3_ref[...] = (x2 * r3 * g3_ref[...]).astype(BF16)

    row = lambda n: pl.BlockSpec((tm, n), lambda i: (i, 0))
    sd = lambda dt: jax.ShapeDtypeStruct((t_tok, D_MODEL), dt)
    return pl.pallas_call(
        body, name="out_proj", grid=(t_tok // tm,), out_shape=(sd(F32), sd(F32), sd(BF16)),
        in_specs=[row(GM_WIDTH), row(SSM_WIDTH), _full((D_MODEL, D_MODEL)), row(D_MODEL), _full((1, D_MODEL)),
                  _full((1, D_MODEL))] + [pl.BlockSpec(memory_space=pl.ANY)] * len(deps),
        out_specs=(row(D_MODEL),) * 3, compiler_params=_params("parallel"))(mix_a, mix_b, w_out, x, g2, g3, *deps)


def _mlp_fwd(h3, w_up, w_down, x2, target, g4, tm, tf):
    t_tok = x2.shape[0]

    def up_body(h_ref, wu_ref, ra_ref):
        ra_ref[...] = jnp.maximum(_dot(h_ref[...], wu_ref[...]), 0.0).astype(BF16)

    tu = min(2 * tm, t_tok)
    ra = pl.pallas_call(
        up_body, name="mlp_up", grid=(D_FF // tf, t_tok // tu), out_shape=jax.ShapeDtypeStruct((t_tok, D_FF), BF16),
        in_specs=[pl.BlockSpec((tu, D_MODEL), lambda j, i: (i, 0)), pl.BlockSpec((D_MODEL, tf), lambda j, i: (0, j))],
        out_specs=pl.BlockSpec((tu, tf), lambda j, i: (i, j)), compiler_params=_params("parallel", "parallel"))(h3, w_up)

    def down_body(ra_ref, wd_ref, x2_ref, t_ref, g4_ref, dd_ref, dy_ref, dg4_ref, loss_ref):
        i = pl.program_id(0)
        rav = ra_ref[...]
        dvec = _dot(rav * rav, wd_ref[...])
        r4 = lax.rsqrt(jnp.mean(dvec * dvec, axis=-1, keepdims=True) + EPS)
        dn = dvec * r4
        g4 = g4_ref[...]
        err = x2_ref[...] + dn * g4 - t_ref[...]
        dy = err * (1.0 / D_MODEL)
        dy_ref[...] = dy
        dg = dy * g4
        dd_ref[...] = (r4 * (dg - dn * jnp.mean(dg * dn, axis=-1, keepdims=True))).astype(BF16)
        _acc_rows(dg4_ref, _rsum(dy * dn), i == 0)
        tile_loss = 0.5 * jnp.sum(jnp.sum(err * err, axis=-1, keepdims=True), axis=0, keepdims=True) / D_MODEL
        _acc_rows(loss_ref, jnp.broadcast_to(tile_loss, (1, 128)), i == 0)

    row = pl.BlockSpec((tm, D_MODEL), lambda i: (i, 0))
    dd, dy, dg4, loss = pl.pallas_call(
        down_body, name="mlp_down", grid=(t_tok // tm,),
        out_shape=(jax.ShapeDtypeStruct((t_tok, D_MODEL), BF16), jax.ShapeDtypeStruct((t_tok, D_MODEL), F32),
                   jax.ShapeDtypeStruct((1, D_MODEL), F32), jax.ShapeDtypeStruct((1, 128), F32)),
        in_specs=[pl.BlockSpec((tm, D_FF), lambda i: (i, 0)), _full((D_FF, D_MODEL)), row, row, _full((1, D_MODEL))],
        out_specs=(row, row, _full((1, D_MODEL)), _full((1, 128))),
        compiler_params=_params("arbitrary"))(ra, w_down, x2, target, g4)
    return ra, dd, dy, dg4, loss


def _mlp_bwd(dd, w_down, ra, w_up, x2, dy, o, g3, g2, tm, tf):
    t_tok = x2.shape[0]

    def hidden_body(dd_ref, wd_ref, ra_ref, da_ref):
        df = _dot(dd_ref[...], wd_ref[...], _NT)
        da_ref[...] = (df * (2.0 * ra_ref[...].astype(F32))).astype(BF16)

    tu = min(2 * tm, t_tok)
    da = pl.pallas_call(
        hidden_body, name="mlp_bwd_hidden", grid=(D_FF // tf, t_tok // tu),
        out_shape=jax.ShapeDtypeStruct((t_tok, D_FF), BF16),
        in_specs=[pl.BlockSpec((tu, D_MODEL), lambda j, i: (i, 0)), pl.BlockSpec((tf, D_MODEL), lambda j, i: (j, 0)),
                  pl.BlockSpec((tu, tf), lambda j, i: (i, j))],
        out_specs=pl.BlockSpec((tu, tf), lambda j, i: (i, j)),
        compiler_params=_params("parallel", "parallel"))(dd, w_down, ra)

    def in_body(da_ref, wu_ref, x2_ref, dy_ref, o_ref, g3_ref, g2_ref, dx2_ref, do_ref, dg3_ref, dg2_ref):
        i = pl.program_id(0)
        dh3 = _dot(da_ref[...], wu_ref[...], _NT)
        dn3, dg3 = _rms_bwd(x2_ref[...], g3_ref[...], dh3)
        dx2 = dy_ref[...] + dn3
        dx2_ref[...] = dx2
        do, dg2 = _rms_bwd(o_ref[...], g2_ref[...], dx2)
        do_ref[...] = do.astype(BF16)
        _acc_rows(dg3_ref, dg3, i == 0)
        _acc_rows(dg2_ref, dg2, i == 0)

    row = pl.BlockSpec((tm, D_MODEL), lambda i: (i, 0))
    vec = _full((1, D_MODEL))
    sd = lambda dt: jax.ShapeDtypeStruct((t_tok, D_MODEL), dt)
    dx2, do, dg3, dg2 = pl.pallas_call(
        in_body, name="mlp_bwd_in", grid=(t_tok // tm,),
        out_shape=(sd(F32), sd(BF16), jax.ShapeDtypeStruct((1, D_MODEL), F32), jax.ShapeDtypeStruct((1, D_MODEL), F32)),
        in_specs=[pl.BlockSpec((tm, D_FF), lambda i: (i, 0)), _full((D_MODEL, D_FF)), row, row, row, vec, vec],
        out_specs=(row, row, vec, vec), compiler_params=_params("arbitrary"))(da, w_up, x2, dy, o, g3, g2)
    return da, dx2, do, dg3, dg2


def _wgrad(a, b, out_blocks, bm, bn, bk, square_a, name, dep=None):
    t_tok, m = a.shape
    n = b.shape[1]
    nk = t_tok // bk

    def body(a_ref, b_ref, *rest):
        o_ref, acc_ref = rest[-2:]
        k = pl.program_id(2)
        av = a_ref[...]
        if square_a:
            av = av * av
        part = _dot(av, b_ref[...], _TN)

        def emit(res):
            if out_blocks is None:
                o_ref[...] = res.astype(BF16)
            else:
                o_ref[0] = res.astype(BF16)

        if nk == 1:
            emit(part)
            return

        @pl.when(k == 0)
        def _():
            acc_ref[...] = part

        @pl.when(k > 0)
        def _():
            acc_ref[...] += part

        @pl.when(k == nk - 1)
        def _():
            emit(acc_ref[...])

    if out_blocks is None:
        out_shape = jax.ShapeDtypeStruct((m, n), BF16)
        out_spec = pl.BlockSpec((bm, bn), lambda i, j, k: (i, j))
    else:
        assert n // out_blocks == bn
        out_shape = jax.ShapeDtypeStruct((out_blocks, m, bn), BF16)
        out_spec = pl.BlockSpec((1, bm, bn), lambda i, j, k: (j, i, 0))
    deps = [] if dep is None else [dep]
    return pl.pallas_call(
        body, name=name, grid=(m // bm, n // bn, nk), out_shape=out_shape,
        in_specs=[pl.BlockSpec((bk, bm), lambda i, j, k: (k, i)), pl.BlockSpec((bk, bn), lambda i, j, k: (k, j))]
        + [pl.BlockSpec(memory_space=pl.ANY)] * len(deps),
        out_specs=out_spec, scratch_shapes=[pltpu.VMEM((bm, bn) if nk > 1 else (8, 128), F32)],
        compiler_params=_params("parallel", "parallel", "arbitrary"))(a, b, *deps)


def _wgrad_in_chunked(h1, pieces, bn, bk, dep=None):
    t_tok = h1.shape[0]
    nk = t_tok // bk
    shard = IN_COLS // N_DEV
    widths = [b - a for a, b in _IN_SPLITS]

    def body(h_ref, *rest):
        piece_refs = rest[:len(widths)]
        o_ref, acc_ref = rest[-2:]
        k = pl.program_id(1)
        hv = h_ref[...]
        for (a, b), r in zip(_IN_SPLITS, piece_refs):
            part = _dot(r[...], hv, _TN)

            @pl.when(k == 0)
            def _():
                acc_ref[a:b, :] = part

            @pl.when(k > 0)
            def _():
                acc_ref[a:b, :] += part

        @pl.when(k == nk - 1)
        def _():
            for j in range(N_DEV):
                o_ref[j] = acc_ref[shard * j:shard * (j + 1), :].astype(BF16)

    deps = [] if dep is None else [dep]
    return pl.pallas_call(
        body, name="wgrad_in", grid=(D_MODEL // bn, nk), out_shape=jax.ShapeDtypeStruct((N_DEV, shard, D_MODEL), BF16),
        in_specs=[pl.BlockSpec((bk, bn), lambda j, k: (k, j))] + [pl.BlockSpec((bk, n), lambda j, k: (k, 0)) for n in widths]
        + [pl.BlockSpec(memory_space=pl.ANY)] * len(deps),
        out_specs=pl.BlockSpec((N_DEV, shard, bn), lambda j, k: (0, 0, j)),
        scratch_shapes=[pltpu.VMEM((IN_PAD, bn), F32)],
        compiler_params=_params("parallel", "arbitrary"))(h1, *pieces, *deps)


def _wgrad_pieces(h1, pieces, bn, name, dep=None):
    t_tok = h1.shape[0]
    widths = [p.shape[1] for p in pieces]
    starts = [sum(widths[:i]) for i in range(len(widths))]

    def body(h_ref, *rest):
        piece_refs = rest[:len(widths)]
        o_ref = rest[-1]
        hv = h_ref[...]
        for a, n, r in zip(starts, widths, piece_refs):
            o_ref[a:a + n, :] = _dot(r[...], hv, _TN).astype(BF16)

    deps = [] if dep is None else [dep]
    return pl.pallas_call(
        body, name=name, grid=(D_MODEL // bn,), out_shape=jax.ShapeDtypeStruct((sum(widths), D_MODEL), BF16),
        in_specs=[pl.BlockSpec((t_tok, bn), lambda j: (0, j))] + [pl.BlockSpec((t_tok, n), lambda j: (0, 0)) for n in widths]
        + [pl.BlockSpec(memory_space=pl.ANY)] * len(deps),
        out_specs=pl.BlockSpec((sum(widths), bn), lambda j: (0, j)),
        compiler_params=_params("parallel"))(h1, *pieces, *deps)


def _dmix(do, w_out, tm, dep=None):
    t_tok = do.shape[0]

    def body(d_ref, w_ref, *rest):
        rest[-1][...] = _dot(d_ref[...], w_ref[...], _NT).astype(BF16)

    row = pl.BlockSpec((tm, D_MODEL), lambda i: (i, 0))
    deps = [] if dep is None else [dep]
    return pl.pallas_call(
        body, name="dmix", grid=(t_tok // tm,), out_shape=jax.ShapeDtypeStruct((t_tok, D_MODEL), BF16),
        in_specs=[row, _full((D_MODEL, D_MODEL))] + [pl.BlockSpec(memory_space=pl.ANY)] * len(deps), out_specs=row,
        compiler_params=_params("parallel"))(do, w_out, *deps)


def _gmlp_bwd(dmix, u, v, lnw, lnb, wcat, wtcat, bias, avg, expand_t):
    t_tok = u.shape[0]
    tm = min(_GMLP_ROWS, t_tok)

    def body(dm_ref, u_ref, v_ref, lnw_ref, lnb_ref, wcat_ref, wtcat_ref, bias_ref, avg_ref, expt_ref, du_ref, dv_ref,
             dw_ref, db_ref, dlnw_ref, dlnb_ref):
        i = pl.program_id(0)
        m_l, m_r = _lane_masks()
        avg = avg_ref[...]
        lnw = lnw_ref[...]
        ug, dug, dvg, rstd, vhat, vn, mixed = _gmlp_common(
            u_ref[...].astype(F32), v_ref[...].astype(F32), lnw, lnb_ref[...], avg, wcat_ref, bias_ref[...], m_l, m_r)
        dya = dm_ref[...].astype(F32)
        du_ref[...] = (dya * mixed * dug).astype(BF16)
        dmixed = dya * ug
        dvn_rows, dws, dbt = [], [None] * N_HEADS, None
        for r in range(tm // CHUNK):
            dvn_cols = []
            for j in range(N_HEADS // 2):
                dmp = dmixed[CHUNK * r:CHUNK * (r + 1), 128 * j:128 * (j + 1)]
                dvn_cols.append(_dot(wtcat_ref[j], _stack_pair(dmp, m_l, m_r)))
                vnp = vn[CHUNK * r:CHUNK * (r + 1), 128 * j:128 * (j + 1)].astype(BF16)
                for i_h, mask in enumerate((m_l, m_r)):
                    part = _dot((dmp * mask).astype(BF16), vnp, _NT)
                    dws[2 * j + i_h] = part if r == 0 else dws[2 * j + i_h] + part
            dvn_rows.append(jnp.concatenate(dvn_cols, axis=1))
            part = _split_dot(dmixed[CHUNK * r:CHUNK * (r + 1), :], expt_ref[...], 2)
            dbt = part if r == 0 else dbt + part
        dvn = jnp.concatenate(dvn_rows, axis=0)
        dvh = dvn * lnw
        dvgel = rstd * (dvh - _head_mean(dvh, avg) - vhat * _head_mean(dvh * vhat, avg))
        dv_ref[...] = (dvgel * dvg).astype(BF16)
        first = i == 0

        @pl.when(first)
        def _():
            for h in range(N_HEADS):
                dw_ref[h] = dws[h]
            db_ref[...] = dbt

        @pl.when(jnp.logical_not(first))
        def _():
            for h in range(N_HEADS):
                dw_ref[h] += dws[h]
            db_ref[...] += dbt

        _acc_rows(dlnw_ref, _rsum(dvn * vhat), first)
        _acc_rows(dlnb_ref, _rsum(dvn), first)

    row = pl.BlockSpec((tm, GM_WIDTH), lambda i: (i, 0))
    consts = [lnw, lnb, wcat, wtcat, bias, avg, expand_t]
    return pl.pallas_call(
        body, name="gmlp_bwd", grid=(t_tok // tm,),
        out_shape=(jax.ShapeDtypeStruct((t_tok, GM_WIDTH), BF16), jax.ShapeDtypeStruct((t_tok, GM_WIDTH), BF16),
                   jax.ShapeDtypeStruct((N_HEADS, CHUNK, CHUNK), F32), jax.ShapeDtypeStruct((CHUNK, CHUNK), F32),
                   jax.ShapeDtypeStruct((1, GM_WIDTH), F32), jax.ShapeDtypeStruct((1, GM_WIDTH), F32)),
        in_specs=[row, row, row] + [_full(a.shape) for a in consts],
        out_specs=(row, row, _full((N_HEADS, CHUNK, CHUNK)), _full((CHUNK, CHUNK)), _full((1, GM_WIDTH)),
                   _full((1, GM_WIDTH))),
        compiler_params=_params("arbitrary"))(dmix, u, v, *consts)


def _ssd_bwd(dmix, z, xbc, pre, dtr, y, states, cw, cb, dtb, alog, dskip_exp, nw, expand, expand_t, tril, triu, seq,
             dep=None):
    t_tok = z.shape[0]
    nb, nc, row, _, states_spec, fold, unfold = _ssd_specs(t_tok, seq, True)
    q = CHUNK

    def one_sequence(s, dm_ref, z_ref, xbc_ref, pre_ref, dtr_ref, y_ref, st_ref, cw_ref, dtb_ref, alog_ref, dsk_ref,
                     nw_ref, exp_ref, expt_ref, tril_ref, triu_ref, dz_ref, dxbc_ref, ddt_ref, dhead_ref, dstate_ref):
        m_l, m_r = _lane_masks()
        expt = expt_ref[...]
        f = _ssd_common(pre_ref[s], dtr_ref[s], dtb_ref[...], alog_ref[...], exp_ref[...], tril_ref[...])
        act, pre, sg = f["act"], f["pre"], f["sg"]
        xs = act[:, :SSM_WIDTH]
        xdt = xs * f["dt_exp"]
        xw = xdt * f["w_end"]
        state = st_ref[s, 0]
        dstate = dstate_ref[s]
        zv, yv, dout, nw = z_ref[s].astype(F32), y_ref[s], dm_ref[s].astype(F32), nw_ref[...]
        sz = jax.nn.sigmoid(zv)
        sl = zv * sz
        yg = yv * sl
        tv = dout * nw
        dyg_parts, ygh_parts = [], []
        for g in range(2):
            ygg = yg[:, 256 * g:256 * (g + 1)]
            rr = lax.rsqrt(jnp.mean(ygg * ygg, axis=-1, keepdims=True) + EPS)
            ygh = ygg * rr
            tg = tv[:, 256 * g:256 * (g + 1)]
            dyg_parts.append(rr * (tg - ygh * jnp.mean(tg * ygh, axis=-1, keepdims=True)))
            ygh_parts.append(ygh)
        dyg = jnp.concatenate(dyg_parts, axis=1)
        dnw = _rsum(dout * jnp.concatenate(ygh_parts, axis=1))
        dy = dyg * sl
        dz_ref[s] = (dyg * yv * (sz * (1.0 + zv * (1.0 - sz)))).astype(BF16)
        ddsk = _rsum(dy * xs)
        dye = dy * f["e"]
        lane = lax.broadcasted_iota(jnp.int32, (q, q), 1)
        sub = lax.broadcasted_iota(jnp.int32, (q, q), 0)
        rs_mat = jnp.zeros((q, q), F32)
        cs_mat = jnp.zeros((q, q), F32)
        dxdt_cols, yoff, dst_in, dxw, d_b, d_c = [], [], [], [], [], []
        for g in range(2):
            bg = act[:, 512 + 128 * g:640 + 128 * g].astype(BF16)
            cg = act[:, 768 + 128 * g:896 + 128 * g].astype(BF16)
            cb_mat = _dot(cg, bg, _NT)
            stg = state[:, 256 * g:256 * (g + 1)].astype(BF16)
            dyeg = dye[:, 256 * g:256 * (g + 1)].astype(BF16)
            yoff.append(_dot(cg, stg))
            dcg = _dot(dyeg, stg, _NT)
            dst_in.append(_dot(cg, dyeg, _TN))
            dcb = jnp.zeros((q, q), F32)
            for pr in range(2):
                h0 = 4 * g + 2 * pr
                gf = [cb_mat * f["decay"][h0], cb_mat * f["decay"][h0 + 1]]
                gcat = jnp.concatenate([gf[0].astype(BF16), gf[1].astype(BF16)], axis=1)
                xst = _stack_pair(xdt[:, 64 * h0:64 * h0 + 128], m_l, m_r)
                dyp = dy[:, 64 * h0:64 * h0 + 128].astype(BF16)
                dgcat = _dot(dyp, xst, _NT)
                dxst = _dot(gcat, dyp, _TN)
                dxdt_cols.append(dxst[:q] * m_l + dxst[q:] * m_r)
                for i in range(2):
                    h = h0 + i
                    dg = dgcat[:, q * i:q * (i + 1)]
                    mm = dg * gf[i]
                    rs_mat = rs_mat + jnp.where(lane == h, jnp.sum(mm, axis=1, keepdims=True), 0.0)
                    cs_mat = cs_mat + jnp.where(sub == h, jnp.sum(mm, axis=0, keepdims=True), 0.0)
                    dcb = dcb + dg * f["decay"][h]
            dcb16 = dcb.astype(BF16)
            dstg = dstate[:, 256 * g:256 * (g + 1)].astype(BF16)
            d_c.append(dcg + _dot(dcb16, bg))
            dxw.append(_dot(bg, dstg))
            d_b.append(_dot(dcb16, cg, _TN) + _dot(xw[:, 256 * g:256 * (g + 1)].astype(BF16), dstg, _NT))
        dxw = jnp.concatenate(dxw, axis=1)
        dxdt = jnp.concatenate(dxdt_cols, axis=1) + dxw * f["w_end"]
        qv = dxw * xw
        end_row = _rsum(qv) + _rsum(dstate * state) * f["cd"]
        x2 = dye * jnp.concatenate(yoff, axis=1) - qv
        row_i = lax.broadcasted_iota(jnp.int32, (q, 1), 0)
        x2 = x2 + jnp.where(row_i == q - 1, end_row, 0.0)
        da_cs = _split_dot(x2, expt, 2) + rs_mat - cs_mat.T
        ddt = _split_dot(dxdt * xs, expt, 2)
        dxs = dsk_ref[...] * dy + dxdt * f["dt_exp"]
        dda = _split_dot_left(triu_ref[...], da_cs, 3)
        ddt = ddt + dda * f["a_row"]
        dalog = _rsum(dda * f["dt"]) * f["a_row"]
        draw = ddt * jax.nn.sigmoid(f["dtp"])
        ddt_ref[s] = draw.astype(BF16)
        dact = jnp.concatenate([dxs] + d_b + d_c, axis=1)
        dpre = dact * (sg * (1.0 + pre * (1.0 - sg)))
        dhead = dhead_ref[s]
        xv = xbc_ref[s]
        shifted = [_shift_rows(dpre, dhead, 3 - k, False) for k in range(3)] + [dpre]
        dxbc = cw_ref[3:4, :] * dpre
        for k in range(3):
            dxbc = dxbc + cw_ref[k:k + 1, :] * shifted[k]
        dxbc_ref[s] = dxbc.astype(BF16)
        dhead_ref[s] = dpre[0:8, :]
        dstate_ref[s] = dstate * f["cd"] + jnp.concatenate(dst_in, axis=1)
        row8 = lax.broadcasted_iota(jnp.int32, (8, 1), 0)
        dcw = jnp.zeros((8, CONV_CH), F32)
        for k in range(4):
            dcw = dcw + jnp.where(row8 == k, _rsum(shifted[k] * xv), 0.0)
        return dcw, _rsum(dpre), _rsum(draw), dalog, _split_dot(ddsk, expt, 3), dnw

    def body(dm_ref, z_ref, xbc_ref, pre_ref, dtr_ref, y_ref, st_ref, cw_ref, cb_ref, dtb_ref, alog_ref, dsk_ref,
             nw_ref, exp_ref, expt_ref, tril_ref, triu_ref, dz_ref, dxbc_ref, ddt_ref, dcw_ref, dcb_ref, ddtb_ref,
             dalog_ref, dd_ref, dnw_ref, dhead_ref, dstate_ref):
        c = pl.program_id(0)
        first = c == 0

        @pl.when(first)
        def _():
            dstate_ref[...] = jnp.zeros_like(dstate_ref)
            dhead_ref[...] = jnp.zeros_like(dhead_ref)

        total = None
        for s in range(nb):
            parts = one_sequence(s, dm_ref, z_ref, xbc_ref, pre_ref, dtr_ref, y_ref, st_ref, cw_ref, dtb_ref, alog_ref,
                                 dsk_ref, nw_ref, exp_ref, expt_ref, tril_ref, triu_ref, dz_ref, dxbc_ref, ddt_ref,
                                 dhead_ref, dstate_ref)
            total = parts if total is None else tuple(a + b for a, b in zip(total, parts))
        dcw = total[0]

        @pl.when(first)
        def _():
            dcw_ref[...] = dcw

        @pl.when(jnp.logical_not(first))
        def _():
            dcw_ref[...] += dcw

        for ref, part in zip((dcb_ref, ddtb_ref, dalog_ref, dd_ref, dnw_ref), total[1:]):
            _acc_rows(ref, part, first)

    consts = [cw, cb, dtb, alog, dskip_exp, nw, expand, expand_t, tril, triu]
    deps = [] if dep is None else [dep]
    n_in = 7 + len(consts)

    def body_skipping_dep(*refs):
        body(*refs[:n_in], *refs[n_in + len(deps):])

    acc = lambda n: jax.ShapeDtypeStruct((1, n), F32)
    sd = lambda n: jax.ShapeDtypeStruct((nb, seq, n), BF16)
    dz, dxbc, ddt, *small_grads = pl.pallas_call(
        body_skipping_dep, name="ssd_bwd", grid=(nc,),
        out_shape=(sd(SSM_WIDTH), sd(CONV_CH), sd(CHUNK), jax.ShapeDtypeStruct((8, CONV_CH), F32), acc(CONV_CH),
                   acc(CHUNK), acc(CHUNK), acc(CHUNK), acc(SSM_WIDTH)),
        in_specs=[row(SSM_WIDTH, col=1), row(SSM_WIDTH), row(CONV_CH), row(CONV_CH), row(CHUNK), row(SSM_WIDTH),
                  states_spec]
        + [_full(a.shape) for a in consts] + [pl.BlockSpec(memory_space=pl.ANY)] * len(deps),
        out_specs=(row(SSM_WIDTH), row(CONV_CH), row(CHUNK), _full((8, CONV_CH)), _full((1, CONV_CH)),
                   _full((1, CHUNK)), _full((1, CHUNK)), _full((1, CHUNK)), _full((1, SSM_WIDTH))),
        scratch_shapes=[pltpu.VMEM((nb, 8, CONV_CH), F32), pltpu.VMEM((nb, N_STATE, SSM_WIDTH), F32)],
        compiler_params=_params("arbitrary"))(
            fold(dmix), fold(z), fold(xbc), fold(pre), fold(dtr), fold(y), states, *consts, *deps)
    return (unfold(dz), unfold(dxbc), unfold(ddt), *small_grads)


def _in_bwd(du, dv, dz, dxbc, ddt, w_in, x, dx2, g1, tm, me, riders=(), dep=None):
    t_tok = x.shape[0]
    steps = t_tok // tm

    n_in = [5 + ("mask" in rd) for rd in riders]
    first_in = [sum(n_in[:r]) for r in range(len(riders))]

    def body(me_ref, du_ref, dv_ref, dz_ref, dxbc_ref, ddt_ref, w_ref, x_ref, dx2_ref, g_ref, *rest):
        outs = rest[len(rest) - 2 - 4 * len(riders):]
        gx_ref, dg_ref = outs[:2]
        i = pl.program_id(0)
        dh = None
        for (a, b), ref in zip(_IN_SPLITS, (du_ref, dv_ref, dz_ref, dxbc_ref, ddt_ref)):
            part = _dot(ref[...], w_ref[a:b, :])
            dh = part if dh is None else dh + part
        dn, dg = _rms_bwd(x_ref[...], g_ref[...], dh)
        gx_ref[...] = dx2_ref[...] + dn
        _acc_rows(dg_ref, dg, i == 0)
        for r in range(len(riders)):
            p_ref, own_ref, w_ref_r, m_ref_r, v_ref_r = rest[first_in[r]:first_in[r] + 5]
            g = _sum_parts(me_ref[0], p_ref, own_ref[0])
            if n_in[r] == 6:
                g = g * rest[first_in[r] + 5][...]
            d, mn, vn = _adamw_math(w_ref_r[...], g, m_ref_r[...], v_ref_r[...])
            for o_ref, val in zip(outs[2 + 4 * r:6 + 4 * r], (g, d, mn, vn)):
                o_ref[...] = val

    row = lambda n: pl.BlockSpec((tm, n), lambda i, me_ref: (i, 0))
    whole = lambda shape: pl.BlockSpec(shape, lambda i, me_ref: (0,) * len(shape))
    widths = [b - a for a, b in _IN_SPLITS]
    deps = [] if dep is None else [dep]
    rider_args, rider_specs, rider_out_shapes, rider_out_specs = [], [], [], []
    for rd in riders:
        rows, cols = rd["w"].shape[0] // steps, rd["w"].shape[1]
        blk = pl.BlockSpec((rows, cols), lambda i, me_ref: (i, 0))
        rider_args += [rd["parts"], rd["own"], rd["w"], rd["m"], rd["v"]]
        rider_specs += [pl.BlockSpec((N_DEV, rows, cols), lambda i, me_ref: (0, i, 0)),
                        pl.BlockSpec((1, rows, cols), lambda i, me_ref: (me_ref[0], i, 0)), blk, blk, blk]
        if "mask" in rd:
            rider_args.append(rd["mask"])
            rider_specs.append(whole((rows, cols)))
        rider_out_shapes += [jax.ShapeDtypeStruct(rd["w"].shape, F32)] * 4
        rider_out_specs += [blk] * 4
    outs = pl.pallas_call(
        body, name="in_bwd",
        out_shape=(jax.ShapeDtypeStruct((t_tok, D_MODEL), F32), jax.ShapeDtypeStruct((1, D_MODEL), F32),
                   *rider_out_shapes),
        grid_spec=pltpu.PrefetchScalarGridSpec(
            num_scalar_prefetch=1, grid=(steps,),
            in_specs=[row(n) for n in widths] + [whole((IN_PAD, D_MODEL)), row(D_MODEL), row(D_MODEL),
                                                 whole((1, D_MODEL))] + rider_specs
            + [pl.BlockSpec(memory_space=pl.ANY)] * len(deps),
            out_specs=(row(D_MODEL), whole((1, D_MODEL)), *rider_out_specs)),
        compiler_params=_params("arbitrary"))(me, du, dv, dz, dxbc, ddt, w_in, x, dx2, g1, *rider_args, *deps)
    return outs[0], outs[1], [tuple(outs[2 + 4 * r:6 + 4 * r]) for r in range(len(riders))]


def _pad_lanes(a, n):
    return jnp.pad(a, ((0, 0), (0, n - a.shape[1])))


def _local_step(x, target, seq, small, hooks, first_dep=None):
    t_tok = x.shape[0]
    tm = min(TOKEN_TILE, t_tok)
    avg, expand, expand_t, tril, triu = _const_mats()
    g1, g2, g3, g4 = (small[k].reshape(1, D_MODEL) for k in
                      ("norm_mix_pre", "norm_mix_post", "norm_ffn_pre", "norm_ffn_post"))
    tie = (lambda a: a) if first_dep is None else (lambda a: a + first_dep[0, 0])
    lnw = tie(small["gm_ln_w"]).reshape(1, GM_WIDTH)
    lnb = tie(small["gm_ln_b"]).reshape(1, GM_WIDTH)
    causal = jnp.tril(jnp.ones((CHUNK, CHUNK), F32))
    wm = tie(small["gm_w_s"]) * causal
    pair = lambda w: w.reshape(4, 2, CHUNK, CHUNK).transpose(0, 2, 1, 3).reshape(4, CHUNK, 2 * CHUNK).astype(BF16)
    wcat = pair(wm)
    wtcat = pair(jnp.swapaxes(wm, 1, 2))
    bias = jnp.repeat(tie(small["gm_b_s"]).T, HEAD_DIM, axis=1)
    cb = small["conv_b"].reshape(1, CONV_CH)
    dtb = _pad_lanes(tie(small["dt_bias"]).reshape(1, N_HEADS), CHUNK)
    alog = _pad_lanes(tie(small["a_log"]).reshape(1, N_HEADS), CHUNK)
    dskip_exp = jnp.repeat(tie(small["d_skip"]).reshape(1, N_HEADS), HEAD_DIM, axis=1)
    nw = small["ssm_norm_w"].reshape(1, SSM_WIDTH)

    h1 = _prenorm(x, g1, tm, hooks.get("prenorm_after", first_dep))
    w_in_t, conv_w = hooks["mixer_weights"]((h1, lnw, lnb, wcat, wtcat, bias, dtb, alog, dskip_exp))
    tall = min(2 * tm, t_tok)
    u, v, z, xbc, dtr = _in_proj(h1, w_in_t, tall)
    mix_a = _gmlp_fwd(u, v, lnw, lnb, wcat, bias, avg)
    dep = hooks["gmlp_done"](mix_a) if "gmlp_done" in hooks else None
    mix_b, y_pre, states, pre = _ssd_fwd(z, xbc, dtr, conv_w, cb, dtb, alog, dskip_exp, nw, expand, tril, seq, dep)
    w_out, dep = hooks["mixers_done"](mix_b)
    o, x2, h3 = _out_proj(mix_a, mix_b, w_out, x, g2, g3, tall, dep)
    w_up, w_down = hooks["mlp_weights"](h3)
    tf = FF_TILE
    ra, dd, dy, dg4, loss = _mlp_fwd(h3, w_up, w_down, x2, target, g4, tm, tf)

    da, dx2, do, dg3, dg2 = _mlp_bwd(dd, w_down, ra, w_up, x2, dy, o, g3, g2, tm, tf)
    g_w_down = _wgrad(ra, dd, None, WGRAD_TILE, D_MODEL, t_tok, True, "wgrad_down")
    g_w_up = _wgrad(h3, da, N_DEV, D_MODEL, D_FF // N_DEV, t_tok, False, "wgrad_up")
    dep = hooks["mlp_grads"](g_w_down, g_w_up)
    dmix = _dmix(do, w_out, tall, dep)
    g_w_out = _wgrad_pieces(do, (mix_a, mix_b), WGRAD_TILE, "wgrad_out", dep)
    du, dv, dws, dbt, dlnw, dlnb = _gmlp_bwd(dmix, u, v, lnw, lnb, wcat, wtcat, bias, avg, expand_t)
    dep = hooks["gmlp_grads"](g_w_out, dws)
    dz, dxbc, ddt, dcw, dcb, ddtb, dalog, ddsk, dnw = _ssd_bwd(
        dmix, z, xbc, pre, dtr, y_pre, states, conv_w, cb, dtb, alog, dskip_exp, nw, expand, expand_t, tril, triu, seq,
        dep)
    g_w_in = _wgrad_in_chunked(h1, (du, dv, dz, dxbc, ddt), WGRAD_TILE, t_tok // 2, dep)
    dep = hooks["in_grads"](g_w_in, dcw[0:4])
    riders = hooks["arrived_updates"](dep) if "arrived_updates" in hooks else []
    me = hooks.get("me", jnp.zeros((1,), jnp.int32))
    grad_x, dg1, updates = _in_bwd(du, dv, dz, dxbc, ddt, w_in_t, x, dx2, g1, tm, me, riders, dep)

    grads = dict(
        updates=updates,
        w_in=g_w_in, w_out=g_w_out, w_up=g_w_up, w_down=g_w_down, conv_w=dcw[0:4],
        norm_mix_pre=dg1, norm_mix_post=dg2, norm_ffn_pre=dg3, norm_ffn_post=dg4, gm_ln_w=dlnw, gm_ln_b=dlnb,
        gm_w_s=dws, gm_b_s=dbt, conv_b=dcb, dt_bias=ddtb, a_log=dalog, d_skip=ddsk, ssm_norm_w=dnw)
    return loss[0, 0], grad_x, grads


_WEIGHTS = ("norm_mix_pre", "w_in", "gm_ln_w", "gm_ln_b", "gm_w_s", "gm_b_s", "conv_w", "conv_b", "dt_bias", "a_log",
            "d_skip", "ssm_norm_w", "w_out", "norm_mix_post", "norm_ffn_pre", "w_up", "w_down", "norm_ffn_post")
_SLAB_ROWS = (("norm_mix_pre", 1024), ("norm_mix_post", 1024), ("norm_ffn_pre", 1024), ("norm_ffn_post", 1024),
              ("conv_b", 1024), ("ssm_norm_w", 512), ("gm_ln_w", 512), ("gm_ln_b", 512), ("dt_bias", 8), ("a_log", 8),
              ("d_skip", 8))
_SLAB_LOSS_ROW = len(_SLAB_ROWS)
_SLAB_BS_ROW = 16
_SMALL_PARAMS = tuple(name for name, _ in _SLAB_ROWS) + ("gm_b_s",)
_LN_PARAMS = ("gm_ln_w", "gm_ln_b")


_SLAB_CONV_ROW = _SLAB_LOSS_ROW + 1


def _pack_slab(g, loss_part):
    rows = [_pad_lanes(g[name], D_MODEL) for name, _ in _SLAB_ROWS]
    rows.append(jnp.broadcast_to(loss_part, (1, D_MODEL)))
    rows.append(g["conv_w"])
    assert sum(r.shape[0] for r in rows) == _SLAB_BS_ROW
    rows.append(_pad_lanes(g["gm_b_s"].T[0:N_HEADS], D_MODEL))
    return jnp.concatenate(rows, axis=0)


def _adamw_slab(parts, me, w, m, v):
    names = _SMALL_PARAMS + ("conv_w",)
    shapes = [w[k].shape for k in names]
    unfold = np.zeros((GM_WIDTH, HEAD_DIM), np.float32)
    for h in range(N_HEADS):
        unfold[h * HEAD_DIM:(h + 1) * HEAD_DIM, :] = np.eye(HEAD_DIM)
    unfold = jnp.asarray(unfold, dtype=BF16)
    n = len(names)
    shard = CONV_CH // N_DEV

    def body(me_ref, p_ref, unfold_ref, *refs):
        w_refs, m_refs, v_refs = refs[:n], refs[n:2 * n], refs[2 * n:3 * n]
        outs = refs[3 * n:]
        g_all = p_ref[0]
        for j in range(1, N_DEV):
            g_all = g_all + p_ref[j]
        lane = lax.broadcasted_iota(jnp.int32, (N_HEADS, GM_WIDTH), 1)
        head = lax.broadcasted_iota(jnp.int32, (N_HEADS, GM_WIDTH), 0)
        own_lanes = jnp.logical_and(lane >= head * HEAD_DIM, lane < (head + 1) * HEAD_DIM)
        mine = pl.ds(pl.multiple_of(me_ref[0] * shard, shard), shard)
        for i, name in enumerate(names):
            if name == "gm_b_s":
                g = g_all[_SLAB_BS_ROW:_SLAB_BS_ROW + N_HEADS, 0:CHUNK]
            elif name == "conv_w":
                g = p_ref[0, _SLAB_CONV_ROW:_SLAB_CONV_ROW + 4, mine]
                for j in range(1, N_DEV):
                    g = g + p_ref[j, _SLAB_CONV_ROW:_SLAB_CONV_ROW + 4, mine]
            else:
                row = [r for r, (k, _) in enumerate(_SLAB_ROWS) if k == name][0]
                g = g_all[row:row + 1, 0:dict(_SLAB_ROWS)[name]]
                if name in _LN_PARAMS:
                    g = _split_dot(jnp.where(own_lanes, g, 0.0), unfold_ref[...], 3)
            d, mn, vn = _adamw_math(w_refs[i][...], g, m_refs[i][...], v_refs[i][...])
            for o_ref, val in zip(outs[4 * i:4 * i + 4], (g, d, mn, vn)):
                o_ref[...] = val
        outs[-1][...] = g_all[_SLAB_LOSS_ROW:_SLAB_LOSS_ROW + 1, 0:128]

    def whole(shape):
        nd = len(shape)
        return pl.BlockSpec(shape, lambda i, me_ref: (0,) * nd)

    ins = [parts, unfold] + [d[k] for d in (w, m, v) for k in names]
    out_shape = tuple(jax.ShapeDtypeStruct(s, F32) for s in shapes for _ in range(4)) + (
        jax.ShapeDtypeStruct((1, 128), F32),)
    outs = pl.pallas_call(
        body, name="adamw_small", out_shape=out_shape,
        grid_spec=pltpu.PrefetchScalarGridSpec(
            num_scalar_prefetch=1, grid=(1,), in_specs=[whole(a.shape) for a in ins],
            out_specs=tuple(whole(s.shape) for s in out_shape)),
        compiler_params=_params("arbitrary"))(me, *ins)
    return {k: tuple(outs[4 * i:4 * i + 4]) for i, k in enumerate(names)}, outs[-1][0, 0]


def kernel(x, norm_mix_pre, w_in, gm_ln_w, gm_ln_b, gm_w_s, gm_b_s, conv_w, conv_b, dt_bias, a_log, d_skip, ssm_norm_w, w_out, norm_mix_post, norm_ffn_pre, w_up, w_down, norm_ffn_post, loss_target, m_norm_mix_pre, m_w_in, m_gm_ln_w, m_gm_ln_b, m_gm_w_s, m_gm_b_s, m_conv_w, m_conv_b, m_dt_bias, m_a_log, m_d_skip, m_ssm_norm_w, m_w_out, m_norm_mix_post, m_norm_ffn_pre, m_w_up, m_w_down, m_norm_ffn_post, v_norm_mix_pre, v_w_in, v_gm_ln_w, v_gm_ln_b, v_gm_w_s, v_gm_b_s, v_conv_w, v_conv_b, v_dt_bias, v_a_log, v_d_skip, v_ssm_norm_w, v_w_out, v_norm_mix_post, v_norm_ffn_pre, v_w_up, v_w_down, v_norm_ffn_post):
    w = dict(norm_mix_pre=norm_mix_pre, w_in=w_in, gm_ln_w=gm_ln_w, gm_ln_b=gm_ln_b, gm_w_s=gm_w_s, gm_b_s=gm_b_s, conv_w=conv_w, conv_b=conv_b, dt_bias=dt_bias, a_log=a_log, d_skip=d_skip, ssm_norm_w=ssm_norm_w, w_out=w_out, norm_mix_post=norm_mix_post, norm_ffn_pre=norm_ffn_pre, w_up=w_up, w_down=w_down, norm_ffn_post=norm_ffn_post)
    m = dict(norm_mix_pre=m_norm_mix_pre, w_in=m_w_in, gm_ln_w=m_gm_ln_w, gm_ln_b=m_gm_ln_b, gm_w_s=m_gm_w_s, gm_b_s=m_gm_b_s, conv_w=m_conv_w, conv_b=m_conv_b, dt_bias=m_dt_bias, a_log=m_a_log, d_skip=m_d_skip, ssm_norm_w=m_ssm_norm_w, w_out=m_w_out, norm_mix_post=m_norm_mix_post, norm_ffn_pre=m_norm_ffn_pre, w_up=m_w_up, w_down=m_w_down, norm_ffn_post=m_norm_ffn_post)
    v = dict(norm_mix_pre=v_norm_mix_pre, w_in=v_w_in, gm_ln_w=v_gm_ln_w, gm_ln_b=v_gm_ln_b, gm_w_s=v_gm_w_s, gm_b_s=v_gm_b_s, conv_w=v_conv_w, conv_b=v_conv_b, dt_bias=v_dt_bias, a_log=v_a_log, d_skip=v_d_skip, ssm_norm_w=v_ssm_norm_w, w_out=v_w_out, norm_mix_post=v_norm_mix_post, norm_ffn_pre=v_norm_ffn_pre, w_up=v_w_up, w_down=v_w_down, norm_ffn_post=v_norm_ffn_post)
    n_batch, seq, _ = x.shape
    shard_in = IN_COLS // N_DEV

    me = (4 * lax.axis_index("x") + 2 * lax.axis_index("y") + lax.axis_index("c")).astype(jnp.int32).reshape(1)

    def in_slot(own):
        return lax.dynamic_update_slice(lax.empty((N_DEV,) + own.shape, own.dtype), own[None],
                                        (me[0],) + (0,) * own.ndim)

    lying = lambda t: jnp.transpose(t, (2, 0, 1))
    first = [_cast_to_slot(lying(w_in), me, shard_in, "cast_w_in"), in_slot(conv_w[0])]
    ici_1, tok_ici_1 = _exchange_start(first, [True] * 2, _SAME_CORE_PEERS, "gather_mix_ici_start")
    cast_out = _cast_to_slot(w_out[0], me, 128, "cast_w_out", dep=tok_ici_1)
    cast_up = _cast_to_slot(w_up[0], me, 1024, "cast_w_up", cols=True, dep=cast_out)
    second = [cast_out, cast_up, _cast_to_slot(w_down[0], me, 512, "cast_w_down", dep=cast_up)]
    gathering = {}

    def mixer_weights(after):
        bufs = [buf for buf, _ in _exchange_wait(ici_1, after, "gather_mix_ici_wait")]
        d2d_1, tok_d2d_1 = _exchange_start(bufs, [True] * 2, _SIBLING_FORWARD, "gather_mix_d2d_start")
        gathering["late_ici"], tok_ici_2 = _exchange_start(
            second, [True] * 3, _SAME_CORE_PEERS, "gather_late_ici_start", dep=tok_d2d_1)
        (_, ag_in), (_, ag_conv) = _exchange_wait(d2d_1, tok_ici_2, "gather_mix_d2d_wait")
        w_in_t = _stack_shards(ag_in, IN_PAD, STACK_TILE, "stack_w_in")
        return w_in_t, ag_conv.transpose(1, 0, 2).reshape(4, CONV_CH)

    def gmlp_done(after):
        ((buf, _),) = _exchange_wait(gathering["late_ici"], after, "gather_out_ici_wait", only=(0,))
        gathering["out"], tok = _exchange_start([buf], [True], _SIBLING_FORWARD, "gather_out_d2d_start")
        return tok

    def mixers_done(after):
        bufs = [buf for buf, _ in _exchange_wait(gathering["late_ici"], after, "gather_mlp_ici_wait", only=(1, 2))]
        gathering["mlp"], tok = _exchange_start(bufs, [True] * 2, _SIBLING_FORWARD, "gather_mlp_d2d_start")
        ((_, ag_out),) = _exchange_wait(gathering["out"], tok, "gather_out_d2d_wait")
        return ag_out.reshape(D_MODEL, D_MODEL), tok

    def mlp_weights(after):
        (_, ag_up), (_, ag_down) = _exchange_wait(gathering["mlp"], after, "gather_mlp_d2d_wait")
        return ag_up, ag_down.reshape(D_FF, D_MODEL)

    sent = {}

    def mlp_grads(g_w_down, g_w_up):
        sent["mlp"], tok = _exchange_start(
            [g_w_down.reshape(N_DEV, D_FF // N_DEV, D_MODEL), g_w_up], [False, False], _ALL_PEERS, "grads_mlp_start")
        return tok

    def gmlp_grads(g_w_out, g_w_s):
        sent["gmlp"], tok = _exchange_start(
            [g_w_out.reshape(N_DEV, D_MODEL // N_DEV, D_MODEL), in_slot(g_w_s.astype(BF16))], [False, True], _ALL_PEERS,
            "grads_gmlp_start")
        return tok

    def in_grads(g_w_in_t, g_conv_w):
        sent["in"], tok = _exchange_start([g_w_in_t], [False], _ALL_PEERS, "grads_in_start")
        return tok

    def arrived_updates(after):
        (own_down, p_down), (own_up, p_up) = _exchange_wait(sent["mlp"], after, "grads_mlp_wait")
        (own_out, p_out), (_, p_ws) = _exchange_wait(sent["gmlp"], own_up, "grads_gmlp_wait")
        rows = lambda t: t.reshape(t.shape[:-3] + (N_HEADS * CHUNK, CHUNK))
        return [dict(parts=p_up, own=own_up, w=w_up[0], m=m_w_up[0], v=v_w_up[0]),
                dict(parts=p_down, own=own_down, w=w_down[0], m=m_w_down[0], v=v_w_down[0]),
                dict(parts=p_out, own=own_out, w=w_out[0], m=m_w_out[0], v=v_w_out[0]),
                dict(parts=rows(p_ws), own=rows(p_ws), w=rows(gm_w_s[0]), m=rows(m_gm_w_s[0]), v=rows(v_gm_w_s[0]),
                     mask=jnp.tril(jnp.ones((CHUNK, CHUNK), F32)))]

    small = {k: w[k][0] for k in _SMALL_PARAMS + ("gm_w_s",)}
    loss_part, grad_x, g = _local_step(
        x.reshape(n_batch * seq, D_MODEL), loss_target.reshape(n_batch * seq, D_MODEL), seq, small,
        dict(mixer_weights=mixer_weights, gmlp_done=gmlp_done, mixers_done=mixers_done, mlp_weights=mlp_weights,
             mlp_grads=mlp_grads, gmlp_grads=gmlp_grads, in_grads=in_grads, arrived_updates=arrived_updates, me=me,
             prenorm_after=second[2]), first_dep=tok_ici_1)

    sent_rows, tok_rows = _exchange_start([in_slot(_pack_slab(g, loss_part))], [True], _ALL_PEERS, "grads_rows_start")
    res = dict(zip(("w_up", "w_down", "w_out", "gm_w_s"), g["updates"]))
    ((own_in, p_in),) = _exchange_wait(sent["in"], tok_rows, "grads_in_wait")
    upd_in = _adamw_reduce(p_in, own_in, me, lying(w_in), lying(m_w_in), lying(v_w_in), STACK_TILE, "adamw_w_in")
    res["w_in"] = tuple(jnp.transpose(t, (1, 2, 0)) for t in upd_in)
    ((_, p_rows),) = _exchange_wait(sent_rows, upd_in[1], "grads_rows_wait")
    flat = lambda t: t[0] if t.ndim == 3 else t
    small_res, loss = _adamw_slab(
        p_rows, me, *({k: flat(d[k]) for k in _SMALL_PARAMS + ("conv_w",)} for d in (w, m, v)))
    res.update(small_res)
    res = {k: tuple(r.reshape(w[k].shape) for r in res[k]) for k in _WEIGHTS}

    outs = [loss, grad_x.reshape(x.shape)]
    for part in range(4):
        outs.extend(res[k][part] for k in _WEIGHTS)
    return tuple(outs)
```

```python
import functools

import jax
import jax.numpy as jnp
import numpy as np
from jax import lax
from jax.experimental import pallas as pl
from jax.experimental.pallas import tpu as pltpu

F32 = jnp.float32
BF16 = jnp.bfloat16

D_MODEL = 1024
GM_WIDTH = 512
SSM_WIDTH = 512
CONV_CH = 1024
N_HEADS = 8
HEAD_DIM = 64
N_STATE = 128
CHUNK = 128
D_FF = 4096
IN_COLS = 2568
IN_PAD = 2688
N_DEV = 8
EPS = 1e-6
ADAM_LR, ADAM_B1, ADAM_B2, ADAM_EPS, ADAM_WD, ADAM_STEP = 0.001, 0.9, 0.999, 1e-08, 0.01, 10
VMEM_LIMIT_BYTES = 56 * 1024 * 1024
TOKEN_TILE = 512
FF_TILE = 2048
WGRAD_TILE = 512
STACK_TILE = 256
_NT = (((1,), (1,)), ((), ()))
_TN = (((0,), (0,)), ((), ()))


def _params(*sem):
    return pltpu.CompilerParams(dimension_semantics=sem or None, vmem_limit_bytes=VMEM_LIMIT_BYTES)


def _dot(a, b, dims=None):
    if dims is None:
        return jnp.dot(a, b, preferred_element_type=F32)
    return lax.dot_general(a, b, dims, preferred_element_type=F32)


def _split_terms(x, terms):
    out, rem = [], x
    for i in range(terms):
        hi = rem.astype(BF16)
        out.append(hi)
        if i + 1 < terms:
            rem = rem - hi.astype(F32)
    return out


def _split_dot(x, m, terms):
    acc = None
    for hi in _split_terms(x, terms):
        part = _dot(hi, m)
        acc = part if acc is None else acc + part
    return acc


def _split_dot_left(m, x, terms):
    acc = None
    for hi in _split_terms(x, terms):
        part = _dot(m, hi)
        acc = part if acc is None else acc + part
    return acc


def _gelu_and_grad(x):
    c = 0.7978845608028654
    inner = c * (x + 0.044715 * x * x * x)
    t = jnp.tanh(inner)
    g = 0.5 * x * (1.0 + t)
    dg = 0.5 * (1.0 + t) + 0.5 * x * (1.0 - t * t) * c * (1.0 + 3.0 * 0.044715 * x * x)
    return g, dg


def _softplus(x):
    return jnp.maximum(x, 0.0) + jnp.log(1.0 + jnp.exp(-jnp.abs(x)))


def _rsum(x):
    return jnp.sum(x, axis=0, keepdims=True)


def _acc_rows(ref, part, first):
    val = jnp.broadcast_to(part, ref.shape)

    @pl.when(first)
    def _():
        ref[...] = val

    @pl.when(jnp.logical_not(first))
    def _():
        ref[...] += val


def _rms_bwd(n, g, dout):
    r = lax.rsqrt(jnp.mean(n * n, axis=-1, keepdims=True) + EPS)
    nh = n * r
    dg = dout * g
    dn = r * (dg - nh * jnp.mean(dg * nh, axis=-1, keepdims=True))
    return dn, _rsum(dout * nh)


def _const_mats():
    avg = np.kron(np.eye(4), np.full((HEAD_DIM, HEAD_DIM), 1.0 / HEAD_DIM))
    expand = np.zeros((CHUNK, SSM_WIDTH), np.float32)
    for h in range(N_HEADS):
        expand[h, h * HEAD_DIM:(h + 1) * HEAD_DIM] = 1.0
    tril = np.tril(np.ones((CHUNK, CHUNK), np.float32))
    as_bf16 = lambda a: jnp.asarray(a, dtype=BF16)
    return as_bf16(avg), as_bf16(expand), as_bf16(expand.T), as_bf16(tril), as_bf16(tril.T)


def _full(shape):
    nd = len(shape)
    return pl.BlockSpec(shape, lambda *_: (0,) * nd)


_HBM = pl.BlockSpec(memory_space=pltpu.HBM)
_SEM = pl.BlockSpec(memory_space=pltpu.SEMAPHORE)
_ALL_PEERS = tuple((k, 0) for k in range(1, N_DEV))
_SAME_CORE_PEERS = ((2, 0), (4, 0), (6, 0))
_SIBLING_FORWARD = ((1, 0), (1, 2), (1, 4), (1, 6))


def _flip(j, k):
    for bit in (4, 2, 1):
        if k & bit:
            j = j + bit - 2 * (j & bit)
    return j


def _copies(src, land, send_sems, recv_sems, hops, slots=None):
    x, y, c = lax.axis_index("x"), lax.axis_index("y"), lax.axis_index("c")
    me = 4 * x + 2 * y + c
    slots = range(len(src)) if slots is None else slots
    out = []
    for t in range(len(src)):
        for i, (k, b) in enumerate(hops):
            pos = (1 - x if k & 4 else x, 1 - y if k & 2 else y, 1 - c if k & 1 else c)
            peer = _flip(me, k)
            sem = slots[t] * len(hops) + i
            mk = functools.partial(pltpu.make_async_remote_copy, send_sem=send_sems.at[sem], recv_sem=recv_sems.at[sem],
                                   device_id=pos, device_id_type=pl.DeviceIdType.MESH)
            if land[t] is None and src[t].shape[0] != N_DEV:
                width = src[t].shape[1] // N_DEV
                slab = lambda j: src[t].at[:, pl.ds(pl.multiple_of(j * width, 128), width)]
                mine = functools.partial(mk, src_ref=slab(_flip(me, b)), dst_ref=slab(_flip(me, b)))
                theirs = functools.partial(mk, src_ref=slab(_flip(peer, b)), dst_ref=slab(_flip(peer, b)))
            elif land[t] is None:
                mine = functools.partial(mk, src_ref=src[t].at[_flip(me, b)], dst_ref=src[t].at[_flip(me, b)])
                theirs = functools.partial(mk, src_ref=src[t].at[_flip(peer, b)], dst_ref=src[t].at[_flip(peer, b)])
            else:
                assert b == 0
                mine = functools.partial(mk, src_ref=src[t].at[peer], dst_ref=land[t].at[me])
                theirs = functools.partial(mk, src_ref=src[t].at[peer], dst_ref=land[t].at[peer])
            out.append((mine, theirs))
    return out


def _exchange_start(srcs, inplace, peers, name, dep=None):
    n = len(srcs)
    lands = [None if ip else pltpu.with_memory_space_constraint(lax.empty(s.shape, s.dtype), pltpu.HBM)
             for s, ip in zip(srcs, inplace)]
    real_lands = [l for l in lands if l is not None]
    n_l = len(real_lands)
    deps = [] if dep is None else [dep]

    def body(*refs):
        src = refs[:n]
        land_refs = list(refs[n:n + n_l])
        send_sems, recv_sems = refs[n + n_l + len(deps)], refs[n + n_l + len(deps) + 1]
        token = refs[-1]
        land = [None if ip else land_refs.pop(0) for ip in inplace]
        for mine, _ in _copies(src, land, send_sems, recv_sems, peers):
            mine().start()
        token[...] = jnp.zeros_like(token)

    sem_t = pltpu.SemaphoreType.DMA((n * len(peers),))
    outs = pl.pallas_call(
        body, name=name,
        out_shape=(sem_t, sem_t) + tuple(pltpu.HBM(a.shape, a.dtype) for a in list(srcs) + real_lands)
        + (jax.ShapeDtypeStruct((8, 128), F32),),
        in_specs=[_HBM] * (n + n_l) + [pl.BlockSpec(memory_space=pl.ANY)] * len(deps),
        out_specs=(_SEM, _SEM) + (_HBM,) * (n + n_l) + (pl.BlockSpec(memory_space=pltpu.VMEM),),
        input_output_aliases={i: 2 + i for i in range(n + n_l)},
        compiler_params=pltpu.CompilerParams(has_side_effects=pltpu.SideEffectType.DATAFLOW_SIDE_EFFECTING),
    )(*[pltpu.with_memory_space_constraint(s, pltpu.HBM) for s in srcs], *real_lands, *deps)
    handle = dict(send=outs[0], recv=outs[1], srcs=outs[2:2 + n], lands=outs[2 + n:2 + n + n_l], inplace=inplace,
                  peers=peers)
    return handle, outs[-1]


def _exchange_wait(handle, after, name, only=None):
    srcs, lands, inplace, peers = handle["srcs"], handle["lands"], handle["inplace"], handle["peers"]
    slots = None
    if only is not None:
        assert all(inplace)
        slots, srcs, inplace = list(only), [srcs[t] for t in only], [True] * len(only)
    n, n_l = len(srcs), len(lands)
    after = after if isinstance(after, tuple) else (after,)

    def body(*refs):
        src = refs[:n]
        land_refs = list(refs[n:n + n_l])
        send_sems, recv_sems = refs[n + n_l], refs[n + n_l + 1]
        land = [None if ip else land_refs.pop(0) for ip in inplace]
        for mine, theirs in _copies(src, land, send_sems, recv_sems, peers, slots):
            mine().wait_send()
            theirs().wait_recv()

    outs = pl.pallas_call(
        body, name=name, out_shape=tuple(pltpu.HBM(a.shape, a.dtype) for a in list(srcs) + list(lands)),
        in_specs=[_HBM] * (n + n_l) + [_SEM, _SEM] + [pl.BlockSpec(memory_space=pl.ANY)] * len(after),
        out_specs=(_HBM,) * (n + n_l), input_output_aliases={i: i for i in range(n + n_l)},
        compiler_params=pltpu.CompilerParams(has_side_effects=pltpu.SideEffectType.DATAFLOW_SIDE_EFFECTING),
    )(*srcs, *lands, handle["send"], handle["recv"], *after)
    res, land_out = [], list(outs[n:])
    for t in range(n):
        res.append((outs[t], outs[t] if inplace[t] else land_out.pop(0)))
    return res


def _cast_to_slot(w, me, rows, name, cols=False, dep=None):
    r, cdim = w.shape[0], w.shape[-1]
    deps = [] if dep is None else [dep]

    def body(me_ref, w_ref, *rest):
        o_ref = rest[-1]
        if cols:
            o_ref[...] = w_ref[...].astype(BF16)
        else:
            o_ref[0] = w_ref[...].reshape(rows, cdim).astype(BF16)

    if cols:
        out_shape = jax.ShapeDtypeStruct((r, N_DEV * cdim), BF16)
        out_spec = pl.BlockSpec((rows, cdim), lambda i, me_ref: (i, me_ref[0]))
    else:
        out_shape = jax.ShapeDtypeStruct((N_DEV, r, cdim), BF16)
        out_spec = pl.BlockSpec((1, rows, cdim), lambda i, me_ref: (me_ref[0], i, 0))
    return pl.pallas_call(
        body, name=name, out_shape=out_shape,
        grid_spec=pltpu.PrefetchScalarGridSpec(
            num_scalar_prefetch=1, grid=(r // rows,),
            in_specs=[pl.BlockSpec((rows, cdim), lambda i, me_ref: (i, 0)) if w.ndim == 2 else
                      pl.BlockSpec((rows, 1, cdim), lambda i, me_ref: (i, 0, 0))]
            + [pl.BlockSpec(memory_space=pl.ANY)] * len(deps), out_specs=out_spec),
        compiler_params=_params("parallel"))(me, w, *deps)


def _stack_shards(blocks, rows, bn, name):
    n, r, cdim = blocks.shape

    def body(b_ref, o_ref, acc_ref):
        acc_ref[n * r:, :] = jnp.zeros((rows - n * r, bn), F32)
        for j in range(n):
            acc_ref[r * j:r * (j + 1), :] = b_ref[j].astype(F32)
        o_ref[...] = acc_ref[...].astype(BF16)

    return pl.pallas_call(
        body, name=name, grid=(cdim // bn,), out_shape=jax.ShapeDtypeStruct((rows, cdim), BF16),
        in_specs=[pl.BlockSpec((n, r, bn), lambda i: (0, 0, i))], out_specs=pl.BlockSpec((rows, bn), lambda i: (0, i)),
        scratch_shapes=[pltpu.VMEM((rows, bn), F32)], compiler_params=_params("parallel"))(blocks)


def _adamw_math(w, g, m, v):
    m = ADAM_B1 * m + (1.0 - ADAM_B1) * g
    v = ADAM_B2 * v + (1.0 - ADAM_B2) * (g * g)
    m_hat = m / (1.0 - ADAM_B1 ** ADAM_STEP)
    v_hat = v / (1.0 - ADAM_B2 ** ADAM_STEP)
    delta = -ADAM_LR * (m_hat / (jnp.sqrt(v_hat) + ADAM_EPS) + ADAM_WD * w)
    return delta, m, v


def _sum_parts(me, p_ref, own):
    g = None
    for j in range(N_DEV):
        term = (p_ref[j] if own is None else jnp.where(me == j, own, p_ref[j])).astype(F32)
        g = term if g is None else g + term
    return g


def _adamw_reduce(parts, own, me, w, m, v, name):
    r, _, cdim = w.shape

    def body(me_ref, p_ref, own_ref, w_ref, m_ref, v_ref, g_out, d_out, m_out, v_out):
        g = _sum_parts(me_ref[0], p_ref, own_ref[0]).reshape(r, 1, cdim)
        d, mn, vn = _adamw_math(w_ref[...], g, m_ref[...], v_ref[...])
        g_out[...] = g
        d_out[...] = d
        m_out[...] = mn
        v_out[...] = vn

    blk = pl.BlockSpec((r, 1, cdim), lambda i, me_ref: (0, 0, 0))
    return pl.pallas_call(
        body, name=name, out_shape=(jax.ShapeDtypeStruct(w.shape, F32),) * 4,
        grid_spec=pltpu.PrefetchScalarGridSpec(
            num_scalar_prefetch=1, grid=(1,),
            in_specs=[pl.BlockSpec((N_DEV, r, cdim), lambda i, me_ref: (0, 0, 0)),
                      pl.BlockSpec((1, r, cdim), lambda i, me_ref: (me_ref[0], 0, 0)), blk, blk, blk],
            out_specs=(blk,) * 4),
        compiler_params=_params("arbitrary"))(me, parts, own, w, m, v)


_IN_SPLITS = ((0, 512), (512, 1024), (1024, 1536), (1536, 2560), (2560, IN_PAD))


def _prenorm(x, g1, tm, dep=None):
    t_tok = x.shape[0]
    deps = [] if dep is None else [dep]

    def body(x_ref, g_ref, *rest):
        xv = x_ref[...]
        r = lax.rsqrt(jnp.mean(xv * xv, axis=-1, keepdims=True) + EPS)
        rest[-1][...] = (xv * r * g_ref[...]).astype(BF16)

    row = pl.BlockSpec((tm, D_MODEL), lambda i: (i, 0))
    return pl.pallas_call(
        body, name="prenorm", grid=(t_tok // tm,), out_shape=jax.ShapeDtypeStruct((t_tok, D_MODEL), BF16),
        in_specs=[row, _full((1, D_MODEL))] + [pl.BlockSpec(memory_space=pl.ANY)] * len(deps), out_specs=row,
        compiler_params=_params("parallel"))(x, g1, *deps)


def _in_proj(h1, w_in, tm):
    t_tok = h1.shape[0]

    def body(h_ref, w_ref, *outs):
        h = h_ref[...]
        for (a, b), o_ref in zip(_IN_SPLITS, outs):
            o_ref[...] = _dot(h, w_ref[a:b, :], _NT).astype(o_ref.dtype)

    row = lambda n: pl.BlockSpec((tm, n), lambda i: (i, 0))
    widths = [b - a for a, b in _IN_SPLITS]
    dtypes = (BF16, BF16, BF16, F32, F32)
    return pl.pallas_call(
        body, name="in_proj", grid=(t_tok // tm,),
        out_shape=tuple(jax.ShapeDtypeStruct((t_tok, n), dt) for n, dt in zip(widths, dtypes)),
        in_specs=[row(D_MODEL), _full((IN_PAD, D_MODEL))], out_specs=tuple(row(n) for n in widths),
        compiler_params=_params("parallel"))(h1, w_in)


def _lane_masks():
    lane = lax.broadcasted_iota(jnp.int32, (1, 2 * HEAD_DIM), 1)
    left = (lane < HEAD_DIM).astype(F32)
    return left, 1.0 - left


def _stack_pair(v, m_l, m_r):
    return jnp.concatenate([v * m_l, v * m_r], axis=0).astype(BF16)


def _head_mean(x, avg):
    n = avg.shape[0]
    return jnp.concatenate([_split_dot(x[:, n * i:n * (i + 1)], avg, 2) for i in range(x.shape[1] // n)], axis=1)


def _gmlp_common(u, v, lnw, lnb, avg, wcat_ref, bias, m_l, m_r):
    ug, dug = _gelu_and_grad(u)
    vg, dvg = _gelu_and_grad(v)
    mu = _head_mean(vg, avg)
    vc = vg - mu
    var = _head_mean(vc * vc, avg)
    rstd = lax.rsqrt(var + EPS)
    vhat = vc * rstd
    vn = vhat * lnw + lnb
    rows = []
    for r in range(u.shape[0] // CHUNK):
        cols = []
        for j in range(N_HEADS // 2):
            pair = vn[CHUNK * r:CHUNK * (r + 1), 128 * j:128 * (j + 1)]
            cols.append(_dot(wcat_ref[j], _stack_pair(pair, m_l, m_r)))
        rows.append(jnp.concatenate(cols, axis=1) + bias)
    mixed = jnp.concatenate(rows, axis=0)
    return ug, dug, dvg, rstd, vhat, vn, mixed


_GMLP_ROWS = 4 * CHUNK


def _gmlp_fwd(u, v, lnw, lnb, wcat, bias, avg):
    t_tok = u.shape[0]
    tm = min(_GMLP_ROWS, t_tok)

    def body(u_ref, v_ref, lnw_ref, lnb_ref, wcat_ref, bias_ref, avg_ref, o_ref):
        m_l, m_r = _lane_masks()
        ug, _, _, _, _, _, mixed = _gmlp_common(
            u_ref[...].astype(F32), v_ref[...].astype(F32), lnw_ref[...], lnb_ref[...], avg_ref[...], wcat_ref,
            bias_ref[...], m_l, m_r)
        o_ref[...] = (ug * mixed).astype(BF16)

    row = pl.BlockSpec((tm, GM_WIDTH), lambda i: (i, 0))
    return pl.pallas_call(
        body, name="gmlp_fwd", grid=(t_tok // tm,), out_shape=jax.ShapeDtypeStruct((t_tok, GM_WIDTH), BF16),
        in_specs=[row, row, _full((1, GM_WIDTH)), _full((1, GM_WIDTH)), _full(wcat.shape), _full(bias.shape),
                  _full(avg.shape)],
        out_specs=row, compiler_params=_params("parallel"))(u, v, lnw, lnb, wcat, bias, avg)


def _shift_rows(x, edge, j, down):
    groups, cols = x.shape[0] // 8, x.shape[1]
    amount = j if down else 8 - j
    rot = pltpu.roll(x.reshape(groups, 8, cols), amount, axis=1)
    edge_rot = pltpu.roll(edge, amount, axis=0)[None]
    sub = lax.broadcasted_iota(jnp.int32, (1, 8, 1), 1)
    if down:
        out = jnp.where(sub < j, jnp.concatenate([edge_rot, rot[:-1]], axis=0), rot)
    else:
        out = jnp.where(sub < 8 - j, rot, jnp.concatenate([rot[1:], edge_rot], axis=0))
    return out.reshape(x.shape)


def _conv_pre(xbc, tail, cw_ref, cb):
    taps = [_shift_rows(xbc, tail, 3 - k, True) for k in range(3)] + [xbc]
    return cb + cw_ref[0:1, :] * taps[0] + cw_ref[1:2, :] * taps[1] + cw_ref[2:3, :] * taps[2] + cw_ref[3:4, :] * taps[3]


def _ssd_common(pre, dtr, dtb, alog, expand, tril):
    q = CHUNK
    sg = jax.nn.sigmoid(pre)
    act = pre * sg
    lane = lax.broadcasted_iota(jnp.int32, (1, CHUNK), 1)
    a_row = jnp.where(lane < N_HEADS, -jnp.exp(alog), 0.0)
    dtp = dtr + dtb
    dt = _softplus(dtp)
    a_cs = _split_dot_left(tril, dt * a_row, 3)
    a_cs_t = a_cs.T
    dt_exp = _split_dot(dt, expand, 3)
    a_exp = _split_dot(a_cs, expand, 3)
    a_end = a_exp[q - 1:q, :]
    li = lax.broadcasted_iota(jnp.int32, (q, q), 0)
    si = lax.broadcasted_iota(jnp.int32, (q, q), 1)
    causal = si <= li
    decay = []
    for h in range(N_HEADS):
        seg = a_cs[:, h:h + 1] - a_cs_t[h:h + 1, :]
        decay.append(jnp.where(causal, jnp.exp(jnp.minimum(seg, 0.0)), 0.0))
    return dict(pre=pre, sg=sg, act=act, a_row=a_row, dtp=dtp, dt=dt, dt_exp=dt_exp, a_exp=a_exp,
                e=jnp.exp(a_exp), w_end=jnp.exp(a_end - a_exp), cd=jnp.exp(a_end), decay=decay)


def _ssd_specs(t_tok, seq, reverse):
    nb, nc = t_tok // seq, seq // CHUNK

    def chunk(c):
        return nc - 1 - c if reverse else c

    def row(n, col=0):
        return pl.BlockSpec((nb, CHUNK, n), lambda c: (0, chunk(c), col))

    tail = pl.BlockSpec((nb, 8, CONV_CH), lambda c: (0, jnp.maximum(chunk(c) * (CHUNK // 8) - 1, 0), 0))
    states = pl.BlockSpec((nb, 1, N_STATE, SSM_WIDTH), lambda c: (0, chunk(c), 0, 0))
    fold = lambda a: a.reshape(nb, seq, a.shape[-1])
    unfold = lambda a: a.reshape(t_tok, a.shape[-1])
    return nb, nc, row, tail, states, fold, unfold


def _ssd_fwd(z, xbc, dtr, cw, cb, dtb, alog, dskip_exp, nw, expand, tril, seq, dep=None):
    t_tok = z.shape[0]
    nb, nc, row, tail, states_spec, fold, unfold = _ssd_specs(t_tok, seq, False)

    def body(z_ref, xbc_ref, tail_ref, dtr_ref, cw_ref, cb_ref, dtb_ref, alog_ref, dsk_ref, nw_ref, exp_ref,
             tril_ref, o_ref, y_ref, st_ref, pre_ref, state_ref):
        c = pl.program_id(0)

        @pl.when(c == 0)
        def _():
            state_ref[...] = jnp.zeros_like(state_ref)

        m_l, m_r = _lane_masks()
        for s in range(nb):
            pre = _conv_pre(xbc_ref[s], jnp.where(c == 0, 0.0, tail_ref[s]), cw_ref, cb_ref[...])
            pre_ref[s] = pre
            f = _ssd_common(pre, dtr_ref[s], dtb_ref[...], alog_ref[...], exp_ref[...], tril_ref[...])
            act = f["act"]
            xs = act[:, :SSM_WIDTH]
            xdt = xs * f["dt_exp"]
            xw = xdt * f["w_end"]
            state = state_ref[s]
            st_ref[s, 0] = state
            ydiag, yoff, snew = [], [], []
            for g in range(2):
                bg = act[:, 512 + 128 * g:640 + 128 * g].astype(BF16)
                cg = act[:, 768 + 128 * g:896 + 128 * g].astype(BF16)
                cb_mat = _dot(cg, bg, _NT)
                for pr in range(2):
                    h0 = 4 * g + 2 * pr
                    gcat = jnp.concatenate(
                        [(cb_mat * f["decay"][h0]).astype(BF16), (cb_mat * f["decay"][h0 + 1]).astype(BF16)], axis=1)
                    ydiag.append(_dot(gcat, _stack_pair(xdt[:, 64 * h0:64 * h0 + 128], m_l, m_r)))
                yoff.append(_dot(cg, state[:, 256 * g:256 * (g + 1)].astype(BF16)))
                snew.append(_dot(bg, xw[:, 256 * g:256 * (g + 1)].astype(BF16), _TN))
            y = jnp.concatenate(ydiag, axis=1) + f["e"] * jnp.concatenate(yoff, axis=1) + dsk_ref[...] * xs
            state_ref[s] = state * f["cd"] + jnp.concatenate(snew, axis=1)
            y_ref[s] = y
            zv = z_ref[s].astype(F32)
            yg = y * (zv * jax.nn.sigmoid(zv))
            outs = []
            for g in range(2):
                ygg = yg[:, 256 * g:256 * (g + 1)]
                outs.append(ygg * lax.rsqrt(jnp.mean(ygg * ygg, axis=-1, keepdims=True) + EPS))
            o_ref[s] = (jnp.concatenate(outs, axis=1) * nw_ref[...]).astype(BF16)

    consts = [cw, cb, dtb, alog, dskip_exp, nw, expand, tril]
    deps = [] if dep is None else [dep]
    n_in = 4 + len(consts)

    def body_skipping_dep(*refs):
        body(*refs[:n_in], *refs[n_in + len(deps):])

    sd = lambda n, dt: jax.ShapeDtypeStruct((nb, seq, n), dt)
    o, y, states, pre = pl.pallas_call(
        body_skipping_dep, name="ssd_fwd", grid=(nc,),
        out_shape=(sd(SSM_WIDTH, BF16), sd(SSM_WIDTH, F32), jax.ShapeDtypeStruct((nb, nc, N_STATE, SSM_WIDTH), F32),
                   sd(CONV_CH, F32)),
        in_specs=[row(SSM_WIDTH), row(CONV_CH), tail, row(CHUNK)] + [_full(a.shape) for a in consts]
        + [pl.BlockSpec(memory_space=pl.ANY)] * len(deps),
        out_specs=(row(SSM_WIDTH), row(SSM_WIDTH), states_spec, row(CONV_CH)),
        scratch_shapes=[pltpu.VMEM((nb, N_STATE, SSM_WIDTH), F32)],
        compiler_params=_params("arbitrary"))(fold(z), fold(xbc), fold(xbc), fold(dtr), *consts, *deps)
    return unfold(o), unfold(y), states, unfold(pre)


def _out_proj(mix_a, mix_b, w_out, x, g2, g3, tm, dep=None):
    t_tok = x.shape[0]
    deps = [] if dep is None else [dep]

    def body(a_ref, b_ref, w_ref, x_ref, g2_ref, g3_ref, *rest):
        o_ref, x2_ref, h3_ref = rest[-3:]
        o = _dot(a_ref[...], w_ref[0:GM_WIDTH, :]) + _dot(b_ref[...], w_ref[GM_WIDTH:, :])
        o_ref[...] = o
        r2 = lax.rsqrt(jnp.mean(o * o, axis=-1, keepdims=True) + EPS)
        x2 = x_ref[...] + o * r2 * g2_ref[...]
        x2_ref[...] = x2
        r3 = lax.rsqrt(jnp.mean(x2 * x2, axis=-1, keepdims=True) + EPS)
        h3_ref[...] = (x2 * r3 * g3_ref[...]).astype(BF16)

    row = lambda n: pl.BlockSpec((tm, n), lambda i: (i, 0))
    sd = lambda dt: jax.ShapeDtypeStruct((t_tok, D_MODEL), dt)
    return pl.pallas_call(
        body, name="out_proj", grid=(t_tok // tm,), out_shape=(sd(F32), sd(F32), sd(BF16)),
        in_specs=[row(GM_WIDTH), row(SSM_WIDTH), _full((D_MODEL, D_MODEL)), row(D_MODEL), _full((1, D_MODEL)),
                  _full((1, D_MODEL))] + [pl.BlockSpec(memory_space=pl.ANY)] * len(deps),
        out_specs=(row(D_MODEL),) * 3, compiler_params=_params("parallel"))(mix_a, mix_b, w_out, x, g2, g3, *deps)


def _mlp_fwd(h3, w_up, w_down, x2, target, g4, tm, tf):
    t_tok = x2.shape[0]

    def up_body(h_ref, wu_ref, ra_ref):
        ra_ref[...] = jnp.maximum(_dot(h_ref[...], wu_ref[...]), 0.0).astype(BF16)

    tu = min(2 * tm, t_tok)
    ra = pl.pallas_call(
        up_body, name="mlp_up", grid=(D_FF // tf, t_tok // tu), out_shape=jax.ShapeDtypeStruct((t_tok, D_FF), BF16),
        in_specs=[pl.BlockSpec((tu, D_MODEL), lambda j, i: (i, 0)), pl.BlockSpec((D_MODEL, tf), lambda j, i: (0, j))],
        out_specs=pl.BlockSpec((tu, tf), lambda j, i: (i, j)), compiler_params=_params("parallel", "parallel"))(h3, w_up)

    def down_body(ra_ref, wd_ref, x2_ref, t_ref, g4_ref, dd_ref, dy_ref, dg4_ref, loss_ref):
        i = pl.program_id(0)
        rav = ra_ref[...]
        dvec = _dot(rav * rav, wd_ref[...])
        r4 = lax.rsqrt(jnp.mean(dvec * dvec, axis=-1, keepdims=True) + EPS)
        dn = dvec * r4
        g4 = g4_ref[...]
        err = x2_ref[...] + dn * g4 - t_ref[...]
        dy = err * (1.0 / D_MODEL)
        dy_ref[...] = dy
        dg = dy * g4
        dd_ref[...] = (r4 * (dg - dn * jnp.mean(dg * dn, axis=-1, keepdims=True))).astype(BF16)
        _acc_rows(dg4_ref, _rsum(dy * dn), i == 0)
        tile_loss = 0.5 * jnp.sum(jnp.sum(err * err, axis=-1, keepdims=True), axis=0, keepdims=True) / D_MODEL
        _acc_rows(loss_ref, jnp.broadcast_to(tile_loss, (1, 128)), i == 0)

    row = pl.BlockSpec((tm, D_MODEL), lambda i: (i, 0))
    dd, dy, dg4, loss = pl.pallas_call(
        down_body, name="mlp_down", grid=(t_tok // tm,),
        out_shape=(jax.ShapeDtypeStruct((t_tok, D_MODEL), BF16), jax.ShapeDtypeStruct((t_tok, D_MODEL), F32),
                   jax.ShapeDtypeStruct((1, D_MODEL), F32), jax.ShapeDtypeStruct((1, 128), F32)),
        in_specs=[pl.BlockSpec((tm, D_FF), lambda i: (i, 0)), _full((D_FF, D_MODEL)), row, row, _full((1, D_MODEL))],
        out_specs=(row, row, _full((1, D_MODEL)), _full((1, 128))),
        compiler_params=_params("arbitrary"))(ra, w_down, x2, target, g4)
    return ra, dd, dy, dg4, loss


def _mlp_bwd(dd, w_down, ra, w_up, x2, dy, o, g3, g2, tm, tf):
    t_tok = x2.shape[0]

    def hidden_body(dd_ref, wd_ref, ra_ref, da_ref):
        df = _dot(dd_ref[...], wd_ref[...], _NT)
        da_ref[...] = (df * (2.0 * ra_ref[...].astype(F32))).astype(BF16)

    tu = min(2 * tm, t_tok)
    da = pl.pallas_call(
        hidden_body, name="mlp_bwd_hidden", grid=(D_FF // tf, t_tok // tu),
        out_shape=jax.ShapeDtypeStruct((t_tok, D_FF), BF16),
        in_specs=[pl.BlockSpec((tu, D_MODEL), lambda j, i: (i, 0)), pl.BlockSpec((tf, D_MODEL), lambda j, i: (j, 0)),
                  pl.BlockSpec((tu, tf), lambda j, i: (i, j))],
        out_specs=pl.BlockSpec((tu, tf), lambda j, i: (i, j)),
        compiler_params=_params("parallel", "parallel"))(dd, w_down, ra)

    def in_body(da_ref, wu_ref, x2_ref, dy_ref, o_ref, g3_ref, g2_ref, dx2_ref, do_ref, dg3_ref, dg2_ref):
        i = pl.program_id(0)
        dh3 = _dot(da_ref[...], wu_ref[...], _NT)
        dn3, dg3 = _rms_bwd(x2_ref[...], g3_ref[...], dh3)
        dx2 = dy_ref[...] + dn3
        dx2_ref[...] = dx2
        do, dg2 = _rms_bwd(o_ref[...], g2_ref[...], dx2)
        do_ref[...] = do.astype(BF16)
        _acc_rows(dg3_ref, dg3, i == 0)
        _acc_rows(dg2_ref, dg2, i == 0)

    row = pl.BlockSpec((tm, D_MODEL), lambda i: (i, 0))
    vec = _full((1, D_MODEL))
    sd = lambda dt: jax.ShapeDtypeStruct((t_tok, D_MODEL), dt)
    dx2, do, dg3, dg2 = pl.pallas_call(
        in_body, name="mlp_bwd_in", grid=(t_tok // tm,),
        out_shape=(sd(F32), sd(BF16), jax.ShapeDtypeStruct((1, D_MODEL), F32), jax.ShapeDtypeStruct((1, D_MODEL), F32)),
        in_specs=[pl.BlockSpec((tm, D_FF), lambda i: (i, 0)), _full((D_MODEL, D_FF)), row, row, row, vec, vec],
        out_specs=(row, row, vec, vec), compiler_params=_params("arbitrary"))(da, w_up, x2, dy, o, g3, g2)
    return da, dx2, do, dg3, dg2


def _wgrad(a, b, out_blocks, bm, bn, bk, square_a, name, dep=None):
    t_tok, m = a.shape
    n = b.shape[1]
    nk = t_tok // bk

    def body(a_ref, b_ref, *rest):
        o_ref, acc_ref = rest[-2:]
        k = pl.program_id(2)
        av = a_ref[...]
        if square_a:
            av = av * av
        part = _dot(av, b_ref[...], _TN)

        def emit(res):
            if out_blocks is None:
                o_ref[...] = res.astype(BF16)
            else:
                o_ref[0] = res.astype(BF16)

        if nk == 1:
            emit(part)
            return

        @pl.when(k == 0)
        def _():
            acc_ref[...] = part

        @pl.when(k > 0)
        def _():
            acc_ref[...] += part

        @pl.when(k == nk - 1)
        def _():
            emit(acc_ref[...])

    if out_blocks is None:
        out_shape = jax.ShapeDtypeStruct((m, n), BF16)
        out_spec = pl.BlockSpec((bm, bn), lambda i, j, k: (i, j))
    else:
        assert n // out_blocks == bn
        out_shape = jax.ShapeDtypeStruct((out_blocks, m, bn), BF16)
        out_spec = pl.BlockSpec((1, bm, bn), lambda i, j, k: (j, i, 0))
    deps = [] if dep is None else [dep]
    return pl.pallas_call(
        body, name=name, grid=(m // bm, n // bn, nk), out_shape=out_shape,
        in_specs=[pl.BlockSpec((bk, bm), lambda i, j, k: (k, i)), pl.BlockSpec((bk, bn), lambda i, j, k: (k, j))]
        + [pl.BlockSpec(memory_space=pl.ANY)] * len(deps),
        out_specs=out_spec, scratch_shapes=[pltpu.VMEM((bm, bn) if nk > 1 else (8, 128), F32)],
        compiler_params=_params("parallel", "parallel", "arbitrary"))(a, b, *deps)


def _wgrad_in_chunked(h1, pieces, bn, bk, dep=None):
    t_tok = h1.shape[0]
    nk = t_tok // bk
    shard = IN_COLS // N_DEV
    widths = [b - a for a, b in _IN_SPLITS]

    def body(h_ref, *rest):
        piece_refs = rest[:len(widths)]
        o_ref, acc_ref = rest[-2:]
        k = pl.program_id(1)
        hv = h_ref[...]
        for (a, b), r in zip(_IN_SPLITS, piece_refs):
            part = _dot(r[...], hv, _TN)

            @pl.when(k == 0)
            def _():
                acc_ref[a:b, :] = part

            @pl.when(k > 0)
            def _():
                acc_ref[a:b, :] += part

        @pl.when(k == nk - 1)
        def _():
            for j in range(N_DEV):
                o_ref[j] = acc_ref[shard * j:shard * (j + 1), :].astype(BF16)

    deps = [] if dep is None else [dep]
    return pl.pallas_call(
        body, name="wgrad_in", grid=(D_MODEL // bn, nk), out_shape=jax.ShapeDtypeStruct((N_DEV, shard, D_MODEL), BF16),
        in_specs=[pl.BlockSpec((bk, bn), lambda j, k: (k, j))] + [pl.BlockSpec((bk, n), lambda j, k: (k, 0)) for n in widths]
        + [pl.BlockSpec(memory_space=pl.ANY)] * len(deps),
        out_specs=pl.BlockSpec((N_DEV, shard, bn), lambda j, k: (0, 0, j)),
        scratch_shapes=[pltpu.VMEM((IN_PAD, bn), F32)],
        compiler_params=_params("parallel", "arbitrary"))(h1, *pieces, *deps)


def _wgrad_pieces(h1, pieces, bn, name, dep=None):
    t_tok = h1.shape[0]
    widths = [p.shape[1] for p in pieces]
    starts = [sum(widths[:i]) for i in range(len(widths))]

    def body(h_ref, *rest):
        piece_refs = rest[:len(widths)]
        o_ref = rest[-1]
        hv = h_ref[...]
        for a, n, r in zip(starts, widths, piece_refs):
            o_ref[a:a + n, :] = _dot(r[...], hv, _TN).astype(BF16)

    deps = [] if dep is None else [dep]
    return pl.pallas_call(
        body, name=name, grid=(D_MODEL // bn,), out_shape=jax.ShapeDtypeStruct((sum(widths), D_MODEL), BF16),
        in_specs=[pl.BlockSpec((t_tok, bn), lambda j: (0, j))] + [pl.BlockSpec((t_tok, n), lambda j: (0, 0)) for n in widths]
        + [pl.BlockSpec(memory_space=pl.ANY)] * len(deps),
        out_specs=pl.BlockSpec((sum(widths), bn), lambda j: (0, j)),
        compiler_params=_params("parallel"))(h1, *pieces, *deps)


def _dmix(do, w_out, tm, dep=None):
    t_tok = do.shape[0]

    def body(d_ref, w_ref, *rest):
        rest[-1][...] = _dot(d_ref[...], w_ref[...], _NT).astype(BF16)

    row = pl.BlockSpec((tm, D_MODEL), lambda i: (i, 0))
    deps = [] if dep is None else [dep]
    return pl.pallas_call(
        body, name="dmix", grid=(t_tok // tm,), out_shape=jax.ShapeDtypeStruct((t_tok, D_MODEL), BF16),
        in_specs=[row, _full((D_MODEL, D_MODEL))] + [pl.BlockSpec(memory_space=pl.ANY)] * len(deps), out_specs=row,
        compiler_params=_params("parallel"))(do, w_out, *deps)


def _dmix_wgrad_out(do, w_out, mix_a, mix_b, tm, dep=None):
    t_tok = do.shape[0]
    steps = t_tok // tm
    deps = [] if dep is None else [dep]

    def body(d_ref, w_ref, a_ref, b_ref, *rest):
        dm_ref, g_ref, acc_ref = rest[-3:]
        i = pl.program_id(0)
        dov = d_ref[...]
        dm_ref[...] = _dot(dov, w_ref[...], _NT).astype(BF16)
        for (lo, hi), r in zip(((0, GM_WIDTH), (GM_WIDTH, D_MODEL)), (a_ref, b_ref)):
            part = _dot(r[...], dov, _TN)

            @pl.when(i == 0)
            def _():
                acc_ref[lo:hi, :] = part

            @pl.when(i > 0)
            def _():
                acc_ref[lo:hi, :] += part

        @pl.when(i == steps - 1)
        def _():
            g_ref[...] = acc_ref[...].astype(BF16)

    row = lambda n: pl.BlockSpec((tm, n), lambda i: (i, 0))
    return pl.pallas_call(
        body, name="dmix_wgrad_out", grid=(steps,),
        out_shape=(jax.ShapeDtypeStruct((t_tok, D_MODEL), BF16), jax.ShapeDtypeStruct((D_MODEL, D_MODEL), BF16)),
        in_specs=[row(D_MODEL), _full((D_MODEL, D_MODEL)), row(GM_WIDTH), row(SSM_WIDTH)]
        + [pl.BlockSpec(memory_space=pl.ANY)] * len(deps),
        out_specs=(row(D_MODEL), _full((D_MODEL, D_MODEL))), scratch_shapes=[pltpu.VMEM((D_MODEL, D_MODEL), F32)],
        compiler_params=_params("arbitrary"))(do, w_out, mix_a, mix_b, *deps)


def _gmlp_bwd(dmix, u, v, lnw, lnb, wcat, wtcat, bias, avg, expand_t):
    t_tok = u.shape[0]
    tm = min(_GMLP_ROWS, t_tok)

    def body(dm_ref, u_ref, v_ref, lnw_ref, lnb_ref, wcat_ref, wtcat_ref, bias_ref, avg_ref, expt_ref, du_ref, dv_ref,
             dw_ref, db_ref, dlnw_ref, dlnb_ref):
        i = pl.program_id(0)
        m_l, m_r = _lane_masks()
        avg = avg_ref[...]
        lnw = lnw_ref[...]
        ug, dug, dvg, rstd, vhat, vn, mixed = _gmlp_common(
            u_ref[...].astype(F32), v_ref[...].astype(F32), lnw, lnb_ref[...], avg, wcat_ref, bias_ref[...], m_l, m_r)
        dya = dm_ref[...].astype(F32)
        du_ref[...] = (dya * mixed * dug).astype(BF16)
        dmixed = dya * ug
        dvn_rows, dws, dbt = [], [None] * N_HEADS, None
        for r in range(tm // CHUNK):
            dvn_cols = []
            for j in range(N_HEADS // 2):
                dmp = dmixed[CHUNK * r:CHUNK * (r + 1), 128 * j:128 * (j + 1)]
                dvn_cols.append(_dot(wtcat_ref[j], _stack_pair(dmp, m_l, m_r)))
                vnp = vn[CHUNK * r:CHUNK * (r + 1), 128 * j:128 * (j + 1)].astype(BF16)
                for i_h, mask in enumerate((m_l, m_r)):
                    part = _dot((dmp * mask).astype(BF16), vnp, _NT)
                    dws[2 * j + i_h] = part if r == 0 else dws[2 * j + i_h] + part
            dvn_rows.append(jnp.concatenate(dvn_cols, axis=1))
            part = _split_dot(dmixed[CHUNK * r:CHUNK * (r + 1), :], expt_ref[...], 2)
            dbt = part if r == 0 else dbt + part
        dvn = jnp.concatenate(dvn_rows, axis=0)
        dvh = dvn * lnw
        dvgel = rstd * (dvh - _head_mean(dvh, avg) - vhat * _head_mean(dvh * vhat, avg))
        dv_ref[...] = (dvgel * dvg).astype(BF16)
        first = i == 0

        @pl.when(first)
        def _():
            for h in range(N_HEADS):
                dw_ref[h] = dws[h]
            db_ref[...] = dbt

        @pl.when(jnp.logical_not(first))
        def _():
            for h in range(N_HEADS):
                dw_ref[h] += dws[h]
            db_ref[...] += dbt

        _acc_rows(dlnw_ref, _rsum(dvn * vhat), first)
        _acc_rows(dlnb_ref, _rsum(dvn), first)

    row = pl.BlockSpec((tm, GM_WIDTH), lambda i: (i, 0))
    consts = [lnw, lnb, wcat, wtcat, bias, avg, expand_t]
    return pl.pallas_call(
        body, name="gmlp_bwd", grid=(t_tok // tm,),
        out_shape=(jax.ShapeDtypeStruct((t_tok, GM_WIDTH), BF16), jax.ShapeDtypeStruct((t_tok, GM_WIDTH), BF16),
                   jax.ShapeDtypeStruct((N_HEADS, CHUNK, CHUNK), F32), jax.ShapeDtypeStruct((CHUNK, CHUNK), F32),
                   jax.ShapeDtypeStruct((1, GM_WIDTH), F32), jax.ShapeDtypeStruct((1, GM_WIDTH), F32)),
        in_specs=[row, row, row] + [_full(a.shape) for a in consts],
        out_specs=(row, row, _full((N_HEADS, CHUNK, CHUNK)), _full((CHUNK, CHUNK)), _full((1, GM_WIDTH)),
                   _full((1, GM_WIDTH))),
        compiler_params=_params("arbitrary"))(dmix, u, v, *consts)


def _ssd_bwd(dmix, z, xbc, pre, dtr, y, states, cw, cb, dtb, alog, dskip_exp, nw, expand, expand_t, tril, triu, seq,
             dep=None):
    t_tok = z.shape[0]
    nb, nc, row, _, states_spec, fold, unfold = _ssd_specs(t_tok, seq, True)
    q = CHUNK

    def one_sequence(s, dm_ref, z_ref, xbc_ref, pre_ref, dtr_ref, y_ref, st_ref, cw_ref, dtb_ref, alog_ref, dsk_ref,
                     nw_ref, exp_ref, expt_ref, tril_ref, triu_ref, dz_ref, dxbc_ref, ddt_ref, dhead_ref, dstate_ref):
        m_l, m_r = _lane_masks()
        expt = expt_ref[...]
        f = _ssd_common(pre_ref[s], dtr_ref[s], dtb_ref[...], alog_ref[...], exp_ref[...], tril_ref[...])
        act, pre, sg = f["act"], f["pre"], f["sg"]
        xs = act[:, :SSM_WIDTH]
        xdt = xs * f["dt_exp"]
        xw = xdt * f["w_end"]
        state = st_ref[s, 0]
        dstate = dstate_ref[s]
        zv, yv, dout, nw = z_ref[s].astype(F32), y_ref[s], dm_ref[s].astype(F32), nw_ref[...]
        sz = jax.nn.sigmoid(zv)
        sl = zv * sz
        yg = yv * sl
        tv = dout * nw
        dyg_parts, ygh_parts = [], []
        for g in range(2):
            ygg = yg[:, 256 * g:256 * (g + 1)]
            rr = lax.rsqrt(jnp.mean(ygg * ygg, axis=-1, keepdims=True) + EPS)
            ygh = ygg * rr
            tg = tv[:, 256 * g:256 * (g + 1)]
            dyg_parts.append(rr * (tg - ygh * jnp.mean(tg * ygh, axis=-1, keepdims=True)))
            ygh_parts.append(ygh)
        dyg = jnp.concatenate(dyg_parts, axis=1)
        dnw = _rsum(dout * jnp.concatenate(ygh_parts, axis=1))
        dy = dyg * sl
        dz_ref[s] = (dyg * yv * (sz * (1.0 + zv * (1.0 - sz)))).astype(BF16)
        ddsk = _rsum(dy * xs)
        dye = dy * f["e"]
        lane = lax.broadcasted_iota(jnp.int32, (q, q), 1)
        sub = lax.broadcasted_iota(jnp.int32, (q, q), 0)
        rs_mat = jnp.zeros((q, q), F32)
        cs_mat = jnp.zeros((q, q), F32)
        dxdt_cols, yoff, dst_in, dxw, d_b, d_c = [], [], [], [], [], []
        for g in range(2):
            bg = act[:, 512 + 128 * g:640 + 128 * g].astype(BF16)
            cg = act[:, 768 + 128 * g:896 + 128 * g].astype(BF16)
            cb_mat = _dot(cg, bg, _NT)
            stg = state[:, 256 * g:256 * (g + 1)].astype(BF16)
            dyeg = dye[:, 256 * g:256 * (g + 1)].astype(BF16)
            yoff.append(_dot(cg, stg))
            dcg = _dot(dyeg, stg, _NT)
            dst_in.append(_dot(cg, dyeg, _TN))
            dcb = jnp.zeros((q, q), F32)
            for pr in range(2):
                h0 = 4 * g + 2 * pr
                gf = [cb_mat * f["decay"][h0], cb_mat * f["decay"][h0 + 1]]
                gcat = jnp.concatenate([gf[0].astype(BF16), gf[1].astype(BF16)], axis=1)
                xst = _stack_pair(xdt[:, 64 * h0:64 * h0 + 128], m_l, m_r)
                dyp = dy[:, 64 * h0:64 * h0 + 128].astype(BF16)
                dgcat = _dot(dyp, xst, _NT)
                dxst = _dot(gcat, dyp, _TN)
                dxdt_cols.append(dxst[:q] * m_l + dxst[q:] * m_r)
                for i in range(2):
                    h = h0 + i
                    dg = dgcat[:, q * i:q * (i + 1)]
                    mm = dg * gf[i]
                    rs_mat = rs_mat + jnp.where(lane == h, jnp.sum(mm, axis=1, keepdims=True), 0.0)
                    cs_mat = cs_mat + jnp.where(sub == h, jnp.sum(mm, axis=0, keepdims=True), 0.0)
                    dcb = dcb + dg * f["decay"][h]
            dcb16 = dcb.astype(BF16)
            dstg = dstate[:, 256 * g:256 * (g + 1)].astype(BF16)
            d_c.append(dcg + _dot(dcb16, bg))
            dxw.append(_dot(bg, dstg))
            d_b.append(_dot(dcb16, cg, _TN) + _dot(xw[:, 256 * g:256 * (g + 1)].astype(BF16), dstg, _NT))
        dxw = jnp.concatenate(dxw, axis=1)
        dxdt = jnp.concatenate(dxdt_cols, axis=1) + dxw * f["w_end"]
        qv = dxw * xw
        end_row = _rsum(qv) + _rsum(dstate * state) * f["cd"]
        x2 = dye * jnp.concatenate(yoff, axis=1) - qv
        row_i = lax.broadcasted_iota(jnp.int32, (q, 1), 0)
        x2 = x2 + jnp.where(row_i == q - 1, end_row, 0.0)
        da_cs = _split_dot(x2, expt, 2) + rs_mat - cs_mat.T
        ddt = _split_dot(dxdt * xs, expt, 2)
        dxs = dsk_ref[...] * dy + dxdt * f["dt_exp"]
        dda = _split_dot_left(triu_ref[...], da_cs, 3)
        ddt = ddt + dda * f["a_row"]
        dalog = _rsum(dda * f["dt"]) * f["a_row"]
        draw = ddt * jax.nn.sigmoid(f["dtp"])
        ddt_ref[s] = draw.astype(BF16)
        dact = jnp.concatenate([dxs] + d_b + d_c, axis=1)
        dpre = dact * (sg * (1.0 + pre * (1.0 - sg)))
        dhead = dhead_ref[s]
        xv = xbc_ref[s]
        shifted = [_shift_rows(dpre, dhead, 3 - k, False) for k in range(3)] + [dpre]
        dxbc = cw_ref[3:4, :] * dpre
        for k in range(3):
            dxbc = dxbc + cw_ref[k:k + 1, :] * shifted[k]
        dxbc_ref[s] = dxbc.astype(BF16)
        dhead_ref[s] = dpre[0:8, :]
        dstate_ref[s] = dstate * f["cd"] + jnp.concatenate(dst_in, axis=1)
        row8 = lax.broadcasted_iota(jnp.int32, (8, 1), 0)
        dcw = jnp.zeros((8, CONV_CH), F32)
        for k in range(4):
            dcw = dcw + jnp.where(row8 == k, _rsum(shifted[k] * xv), 0.0)
        return dcw, _rsum(dpre), _rsum(draw), dalog, _split_dot(ddsk, expt, 3), dnw

    def body(dm_ref, z_ref, xbc_ref, pre_ref, dtr_ref, y_ref, st_ref, cw_ref, cb_ref, dtb_ref, alog_ref, dsk_ref,
             nw_ref, exp_ref, expt_ref, tril_ref, triu_ref, dz_ref, dxbc_ref, ddt_ref, dcw_ref, dcb_ref, ddtb_ref,
             dalog_ref, dd_ref, dnw_ref, dhead_ref, dstate_ref):
        c = pl.program_id(0)
        first = c == 0

        @pl.when(first)
        def _():
            dstate_ref[...] = jnp.zeros_like(dstate_ref)
            dhead_ref[...] = jnp.zeros_like(dhead_ref)

        total = None
        for s in range(nb):
            parts = one_sequence(s, dm_ref, z_ref, xbc_ref, pre_ref, dtr_ref, y_ref, st_ref, cw_ref, dtb_ref, alog_ref,
                                 dsk_ref, nw_ref, exp_ref, expt_ref, tril_ref, triu_ref, dz_ref, dxbc_ref, ddt_ref,
                                 dhead_ref, dstate_ref)
            total = parts if total is None else tuple(a + b for a, b in zip(total, parts))
        dcw = total[0]

        @pl.when(first)
        def _():
            dcw_ref[...] = dcw

        @pl.when(jnp.logical_not(first))
        def _():
            dcw_ref[...] += dcw

        for ref, part in zip((dcb_ref, ddtb_ref, dalog_ref, dd_ref, dnw_ref), total[1:]):
            _acc_rows(ref, part, first)

    consts = [cw, cb, dtb, alog, dskip_exp, nw, expand, expand_t, tril, triu]
    deps = [] if dep is None else [dep]
    n_in = 7 + len(consts)

    def body_skipping_dep(*refs):
        body(*refs[:n_in], *refs[n_in + len(deps):])

    acc = lambda n: jax.ShapeDtypeStruct((1, n), F32)
    sd = lambda n: jax.ShapeDtypeStruct((nb, seq, n), BF16)
    dz, dxbc, ddt, *small_grads = pl.pallas_call(
        body_skipping_dep, name="ssd_bwd", grid=(nc,),
        out_shape=(sd(SSM_WIDTH), sd(CONV_CH), sd(CHUNK), jax.ShapeDtypeStruct((8, CONV_CH), F32), acc(CONV_CH),
                   acc(CHUNK), acc(CHUNK), acc(CHUNK), acc(SSM_WIDTH)),
        in_specs=[row(SSM_WIDTH, col=1), row(SSM_WIDTH), row(CONV_CH), row(CONV_CH), row(CHUNK), row(SSM_WIDTH),
                  states_spec]
        + [_full(a.shape) for a in consts] + [pl.BlockSpec(memory_space=pl.ANY)] * len(deps),
        out_specs=(row(SSM_WIDTH), row(CONV_CH), row(CHUNK), _full((8, CONV_CH)), _full((1, CONV_CH)),
                   _full((1, CHUNK)), _full((1, CHUNK)), _full((1, CHUNK)), _full((1, SSM_WIDTH))),
        scratch_shapes=[pltpu.VMEM((nb, 8, CONV_CH), F32), pltpu.VMEM((nb, N_STATE, SSM_WIDTH), F32)],
        compiler_params=_params("arbitrary"))(
            fold(dmix), fold(z), fold(xbc), fold(pre), fold(dtr), fold(y), states, *consts, *deps)
    return (unfold(dz), unfold(dxbc), unfold(ddt), *small_grads)


def _in_bwd(du, dv, dz, dxbc, ddt, w_in, x, dx2, g1, tm, me, riders=(), dep=None):
    t_tok = x.shape[0]
    steps = t_tok // tm

    n_in = [5 + ("mask" in rd) for rd in riders]
    first_in = [sum(n_in[:r]) for r in range(len(riders))]

    def body(me_ref, du_ref, dv_ref, dz_ref, dxbc_ref, ddt_ref, w_ref, x_ref, dx2_ref, g_ref, *rest):
        outs = rest[len(rest) - 2 - 4 * len(riders):]
        gx_ref, dg_ref = outs[:2]
        i = pl.program_id(0)
        dh = None
        for (a, b), ref in zip(_IN_SPLITS, (du_ref, dv_ref, dz_ref, dxbc_ref, ddt_ref)):
            part = _dot(ref[...], w_ref[a:b, :])
            dh = part if dh is None else dh + part
        dn, dg = _rms_bwd(x_ref[...], g_ref[...], dh)
        gx_ref[...] = dx2_ref[...] + dn
        _acc_rows(dg_ref, dg, i == 0)
        for r in range(len(riders)):
            p_ref, own_ref, w_ref_r, m_ref_r, v_ref_r = rest[first_in[r]:first_in[r] + 5]
            g = _sum_parts(me_ref[0], p_ref, own_ref[0])
            if n_in[r] == 6:
                g = g * rest[first_in[r] + 5][...]
            d, mn, vn = _adamw_math(w_ref_r[...], g, m_ref_r[...], v_ref_r[...])
            for o_ref, val in zip(outs[2 + 4 * r:6 + 4 * r], (g, d, mn, vn)):
                o_ref[...] = val

    row = lambda n: pl.BlockSpec((tm, n), lambda i, me_ref: (i, 0))
    whole = lambda shape: pl.BlockSpec(shape, lambda i, me_ref: (0,) * len(shape))
    widths = [b - a for a, b in _IN_SPLITS]
    deps = [] if dep is None else [dep]
    rider_args, rider_specs, rider_out_shapes, rider_out_specs = [], [], [], []
    for rd in riders:
        rows, cols = rd["w"].shape[0] // steps, rd["w"].shape[1]
        blk = pl.BlockSpec((rows, cols), lambda i, me_ref: (i, 0))
        rider_args += [rd["parts"], rd["own"], rd["w"], rd["m"], rd["v"]]
        rider_specs += [pl.BlockSpec((N_DEV, rows, cols), lambda i, me_ref: (0, i, 0)),
                        pl.BlockSpec((1, rows, cols), lambda i, me_ref: (me_ref[0], i, 0)), blk, blk, blk]
        if "mask" in rd:
            rider_args.append(rd["mask"])
            rider_specs.append(whole((rows, cols)))
        rider_out_shapes += [jax.ShapeDtypeStruct(rd["w"].shape, F32)] * 4
        rider_out_specs += [blk] * 4
    outs = pl.pallas_call(
        body, name="in_bwd",
        out_shape=(jax.ShapeDtypeStruct((t_tok, D_MODEL), F32), jax.ShapeDtypeStruct((1, D_MODEL), F32),
                   *rider_out_shapes),
        grid_spec=pltpu.PrefetchScalarGridSpec(
            num_scalar_prefetch=1, grid=(steps,),
            in_specs=[row(n) for n in widths] + [whole((IN_PAD, D_MODEL)), row(D_MODEL), row(D_MODEL),
                                                 whole((1, D_MODEL))] + rider_specs
            + [pl.BlockSpec(memory_space=pl.ANY)] * len(deps),
            out_specs=(row(D_MODEL), whole((1, D_MODEL)), *rider_out_specs)),
        compiler_params=_params("arbitrary"))(me, du, dv, dz, dxbc, ddt, w_in, x, dx2, g1, *rider_args, *deps)
    return outs[0], outs[1], [tuple(outs[2 + 4 * r:6 + 4 * r]) for r in range(len(riders))]


def _pad_lanes(a, n):
    return jnp.pad(a, ((0, 0), (0, n - a.shape[1])))


def _local_step(x, target, seq, small, hooks, first_dep=None):
    t_tok = x.shape[0]
    tm = min(TOKEN_TILE, t_tok)
    avg, expand, expand_t, tril, triu = _const_mats()
    g1, g2, g3, g4 = (small[k].reshape(1, D_MODEL) for k in
                      ("norm_mix_pre", "norm_mix_post", "norm_ffn_pre", "norm_ffn_post"))
    tie = (lambda a: a) if first_dep is None else (lambda a: a + first_dep[0, 0])
    lnw = tie(small["gm_ln_w"]).reshape(1, GM_WIDTH)
    lnb = tie(small["gm_ln_b"]).reshape(1, GM_WIDTH)
    causal = jnp.tril(jnp.ones((CHUNK, CHUNK), F32))
    wm = tie(small["gm_w_s"]) * causal
    pair = lambda w: w.reshape(4, 2, CHUNK, CHUNK).transpose(0, 2, 1, 3).reshape(4, CHUNK, 2 * CHUNK).astype(BF16)
    wcat = pair(wm)
    wtcat = pair(jnp.swapaxes(wm, 1, 2))
    bias = jnp.repeat(tie(small["gm_b_s"]).T, HEAD_DIM, axis=1)
    cb = small["conv_b"].reshape(1, CONV_CH)
    dtb = _pad_lanes(tie(small["dt_bias"]).reshape(1, N_HEADS), CHUNK)
    alog = _pad_lanes(tie(small["a_log"]).reshape(1, N_HEADS), CHUNK)
    dskip_exp = jnp.repeat(tie(small["d_skip"]).reshape(1, N_HEADS), HEAD_DIM, axis=1)
    nw = small["ssm_norm_w"].reshape(1, SSM_WIDTH)

    h1 = _prenorm(x, g1, tm, hooks.get("prenorm_after", first_dep))
    w_in_t, conv_w = hooks["mixer_weights"]((h1, lnw, lnb, wcat, wtcat, bias, dtb, alog, dskip_exp))
    tall = min(2 * tm, t_tok)
    u, v, z, xbc, dtr = _in_proj(h1, w_in_t, tall)
    mix_a = _gmlp_fwd(u, v, lnw, lnb, wcat, bias, avg)
    dep = hooks["gmlp_done"](mix_a) if "gmlp_done" in hooks else None
    mix_b, y_pre, states, pre = _ssd_fwd(z, xbc, dtr, conv_w, cb, dtb, alog, dskip_exp, nw, expand, tril, seq, dep)
    w_out, dep = hooks["mixers_done"](mix_b)
    o, x2, h3 = _out_proj(mix_a, mix_b, w_out, x, g2, g3, tall, dep)
    w_up, w_down = hooks["mlp_weights"](h3)
    tf = FF_TILE
    ra, dd, dy, dg4, loss = _mlp_fwd(h3, w_up, w_down, x2, target, g4, tm, tf)

    da, dx2, do, dg3, dg2 = _mlp_bwd(dd, w_down, ra, w_up, x2, dy, o, g3, g2, tm, tf)
    g_w_down = _wgrad(ra, dd, None, WGRAD_TILE, D_MODEL, t_tok, True, "wgrad_down")
    g_w_up = _wgrad(h3, da, N_DEV, D_MODEL, D_FF // N_DEV, t_tok, False, "wgrad_up")
    dep = hooks["mlp_grads"](g_w_down, g_w_up)
    dmix, g_w_out = _dmix_wgrad_out(do, w_out, mix_a, mix_b, tall, dep)
    du, dv, dws, dbt, dlnw, dlnb = _gmlp_bwd(dmix, u, v, lnw, lnb, wcat, wtcat, bias, avg, expand_t)
    dep = hooks["gmlp_grads"](g_w_out, dws)
    dz, dxbc, ddt, dcw, dcb, ddtb, dalog, ddsk, dnw = _ssd_bwd(
        dmix, z, xbc, pre, dtr, y_pre, states, conv_w, cb, dtb, alog, dskip_exp, nw, expand, expand_t, tril, triu, seq,
        dep)
    g_w_in = _wgrad_in_chunked(h1, (du, dv, dz, dxbc, ddt), WGRAD_TILE, t_tok // 2, dep)
    dep = hooks["in_grads"](g_w_in, dcw[0:4])
    riders = hooks["arrived_updates"](dep) if "arrived_updates" in hooks else []
    me = hooks.get("me", jnp.zeros((1,), jnp.int32))
    grad_x, dg1, updates = _in_bwd(du, dv, dz, dxbc, ddt, w_in_t, x, dx2, g1, tm, me, riders, dep)

    grads = dict(
        updates=updates,
        w_in=g_w_in, w_out=g_w_out, w_up=g_w_up, w_down=g_w_down, conv_w=dcw[0:4],
        norm_mix_pre=dg1, norm_mix_post=dg2, norm_ffn_pre=dg3, norm_ffn_post=dg4, gm_ln_w=dlnw, gm_ln_b=dlnb,
        gm_w_s=dws, gm_b_s=dbt, conv_b=dcb, dt_bias=ddtb, a_log=dalog, d_skip=ddsk, ssm_norm_w=dnw)
    return loss[0, 0], grad_x, grads


_WEIGHTS = ("norm_mix_pre", "w_in", "gm_ln_w", "gm_ln_b", "gm_w_s", "gm_b_s", "conv_w", "conv_b", "dt_bias", "a_log",
            "d_skip", "ssm_norm_w", "w_out", "norm_mix_post", "norm_ffn_pre", "w_up", "w_down", "norm_ffn_post")
_SLAB_ROWS = (("norm_mix_pre", 1024), ("norm_mix_post", 1024), ("norm_ffn_pre", 1024), ("norm_ffn_post", 1024),
              ("conv_b", 1024), ("ssm_norm_w", 512), ("gm_ln_w", 512), ("gm_ln_b", 512), ("dt_bias", 8), ("a_log", 8),
              ("d_skip", 8))
_SLAB_LOSS_ROW = len(_SLAB_ROWS)
_SLAB_BS_ROW = 16
_SMALL_PARAMS = tuple(name for name, _ in _SLAB_ROWS) + ("gm_b_s",)
_LN_PARAMS = ("gm_ln_w", "gm_ln_b")


_SLAB_CONV_ROW = _SLAB_LOSS_ROW + 1


def _pack_slab(g, loss_part):
    rows = [_pad_lanes(g[name], D_MODEL) for name, _ in _SLAB_ROWS]
    rows.append(jnp.broadcast_to(loss_part, (1, D_MODEL)))
    rows.append(g["conv_w"])
    assert sum(r.shape[0] for r in rows) == _SLAB_BS_ROW
    rows.append(_pad_lanes(g["gm_b_s"].T[0:N_HEADS], D_MODEL))
    return jnp.concatenate(rows, axis=0)


def _adamw_slab(parts, me, w, m, v):
    names = _SMALL_PARAMS + ("conv_w",)
    shapes = [w[k].shape for k in names]
    unfold = np.zeros((GM_WIDTH, HEAD_DIM), np.float32)
    for h in range(N_HEADS):
        unfold[h * HEAD_DIM:(h + 1) * HEAD_DIM, :] = np.eye(HEAD_DIM)
    unfold = jnp.asarray(unfold, dtype=BF16)
    n = len(names)
    shard = CONV_CH // N_DEV

    def body(me_ref, p_ref, unfold_ref, *refs):
        w_refs, m_refs, v_refs = refs[:n], refs[n:2 * n], refs[2 * n:3 * n]
        outs = refs[3 * n:]
        g_all = p_ref[0]
        for j in range(1, N_DEV):
            g_all = g_all + p_ref[j]
        lane = lax.broadcasted_iota(jnp.int32, (N_HEADS, GM_WIDTH), 1)
        head = lax.broadcasted_iota(jnp.int32, (N_HEADS, GM_WIDTH), 0)
        own_lanes = jnp.logical_and(lane >= head * HEAD_DIM, lane < (head + 1) * HEAD_DIM)
        mine = pl.ds(pl.multiple_of(me_ref[0] * shard, shard), shard)
        for i, name in enumerate(names):
            if name == "gm_b_s":
                g = g_all[_SLAB_BS_ROW:_SLAB_BS_ROW + N_HEADS, 0:CHUNK]
            elif name == "conv_w":
                g = p_ref[0, _SLAB_CONV_ROW:_SLAB_CONV_ROW + 4, mine]
                for j in range(1, N_DEV):
                    g = g + p_ref[j, _SLAB_CONV_ROW:_SLAB_CONV_ROW + 4, mine]
            else:
                row = [r for r, (k, _) in enumerate(_SLAB_ROWS) if k == name][0]
                g = g_all[row:row + 1, 0:dict(_SLAB_ROWS)[name]]
                if name in _LN_PARAMS:
                    g = _split_dot(jnp.where(own_lanes, g, 0.0), unfold_ref[...], 3)
            d, mn, vn = _adamw_math(w_refs[i][...], g, m_refs[i][...], v_refs[i][...])
            for o_ref, val in zip(outs[4 * i:4 * i + 4], (g, d, mn, vn)):
                o_ref[...] = val
        outs[-1][...] = g_all[_SLAB_LOSS_ROW:_SLAB_LOSS_ROW + 1, 0:128]

    def whole(shape):
        nd = len(shape)
        return pl.BlockSpec(shape, lambda i, me_ref: (0,) * nd)

    ins = [parts, unfold] + [d[k] for d in (w, m, v) for k in names]
    out_shape = tuple(jax.ShapeDtypeStruct(s, F32) for s in shapes for _ in range(4)) + (
        jax.ShapeDtypeStruct((1, 128), F32),)
    outs = pl.pallas_call(
        body, name="adamw_small", out_shape=out_shape,
        grid_spec=pltpu.PrefetchScalarGridSpec(
            num_scalar_prefetch=1, grid=(1,), in_specs=[whole(a.shape) for a in ins],
            out_specs=tuple(whole(s.shape) for s in out_shape)),
        compiler_params=_params("arbitrary"))(me, *ins)
    return {k: tuple(outs[4 * i:4 * i + 4]) for i, k in enumerate(names)}, outs[-1][0, 0]


def kernel(x, norm_mix_pre, w_in, gm_ln_w, gm_ln_b, gm_w_s, gm_b_s, conv_w, conv_b, dt_bias, a_log, d_skip, ssm_norm_w, w_out, norm_mix_post, norm_ffn_pre, w_up, w_down, norm_ffn_post, loss_target, m_norm_mix_pre, m_w_in, m_gm_ln_w, m_gm_ln_b, m_gm_w_s, m_gm_b_s, m_conv_w, m_conv_b, m_dt_bias, m_a_log, m_d_skip, m_ssm_norm_w, m_w_out, m_norm_mix_post, m_norm_ffn_pre, m_w_up, m_w_down, m_norm_ffn_post, v_norm_mix_pre, v_w_in, v_gm_ln_w, v_gm_ln_b, v_gm_w_s, v_gm_b_s, v_conv_w, v_conv_b, v_dt_bias, v_a_log, v_d_skip, v_ssm_norm_w, v_w_out, v_norm_mix_post, v_norm_ffn_pre, v_w_up, v_w_down, v_norm_ffn_post):
    w = dict(norm_mix_pre=norm_mix_pre, w_in=w_in, gm_ln_w=gm_ln_w, gm_ln_b=gm_ln_b, gm_w_s=gm_w_s, gm_b_s=gm_b_s, conv_w=conv_w, conv_b=conv_b, dt_bias=dt_bias, a_log=a_log, d_skip=d_skip, ssm_norm_w=ssm_norm_w, w_out=w_out, norm_mix_post=norm_mix_post, norm_ffn_pre=norm_ffn_pre, w_up=w_up, w_down=w_down, norm_ffn_post=norm_ffn_post)
    m = dict(norm_mix_pre=m_norm_mix_pre, w_in=m_w_in, gm_ln_w=m_gm_ln_w, gm_ln_b=m_gm_ln_b, gm_w_s=m_gm_w_s, gm_b_s=m_gm_b_s, conv_w=m_conv_w, conv_b=m_conv_b, dt_bias=m_dt_bias, a_log=m_a_log, d_skip=m_d_skip, ssm_norm_w=m_ssm_norm_w, w_out=m_w_out, norm_mix_post=m_norm_mix_post, norm_ffn_pre=m_norm_ffn_pre, w_up=m_w_up, w_down=m_w_down, norm_ffn_post=m_norm_ffn_post)
    v = dict(norm_mix_pre=v_norm_mix_pre, w_in=v_w_in, gm_ln_w=v_gm_ln_w, gm_ln_b=v_gm_ln_b, gm_w_s=v_gm_w_s, gm_b_s=v_gm_b_s, conv_w=v_conv_w, conv_b=v_conv_b, dt_bias=v_dt_bias, a_log=v_a_log, d_skip=v_d_skip, ssm_norm_w=v_ssm_norm_w, w_out=v_w_out, norm_mix_post=v_norm_mix_post, norm_ffn_pre=v_norm_ffn_pre, w_up=v_w_up, w_down=v_w_down, norm_ffn_post=v_norm_ffn_post)
    n_batch, seq, _ = x.shape
    shard_in = IN_COLS // N_DEV

    me = (4 * lax.axis_index("x") + 2 * lax.axis_index("y") + lax.axis_index("c")).astype(jnp.int32).reshape(1)

    def in_slot(own):
        return lax.dynamic_update_slice(lax.empty((N_DEV,) + own.shape, own.dtype), own[None],
                                        (me[0],) + (0,) * own.ndim)

    lying = lambda t: jnp.transpose(t, (2, 0, 1))
    first = [_cast_to_slot(lying(w_in), me, shard_in, "cast_w_in"), in_slot(conv_w[0])]
    ici_1, tok_ici_1 = _exchange_start(first, [True] * 2, _SAME_CORE_PEERS, "gather_mix_ici_start")
    cast_out = _cast_to_slot(w_out[0], me, 128, "cast_w_out", dep=tok_ici_1)
    cast_up = _cast_to_slot(w_up[0], me, 1024, "cast_w_up", cols=True, dep=cast_out)
    second = [cast_out, cast_up, _cast_to_slot(w_down[0], me, 512, "cast_w_down", dep=cast_up)]
    gathering = {}

    def mixer_weights(after):
        bufs = [buf for buf, _ in _exchange_wait(ici_1, after, "gather_mix_ici_wait")]
        d2d_1, tok_d2d_1 = _exchange_start(bufs, [True] * 2, _SIBLING_FORWARD, "gather_mix_d2d_start")
        gathering["late_ici"], tok_ici_2 = _exchange_start(
            second, [True] * 3, _SAME_CORE_PEERS, "gather_late_ici_start", dep=tok_d2d_1)
        (_, ag_in), (_, ag_conv) = _exchange_wait(d2d_1, tok_ici_2, "gather_mix_d2d_wait")
        w_in_t = _stack_shards(ag_in, IN_PAD, STACK_TILE, "stack_w_in")
        return w_in_t, ag_conv.transpose(1, 0, 2).reshape(4, CONV_CH)

    def gmlp_done(after):
        ((buf, _),) = _exchange_wait(gathering["late_ici"], after, "gather_out_ici_wait", only=(0,))
        gathering["out"], tok = _exchange_start([buf], [True], _SIBLING_FORWARD, "gather_out_d2d_start")
        return tok

    def mixers_done(after):
        bufs = [buf for buf, _ in _exchange_wait(gathering["late_ici"], after, "gather_mlp_ici_wait", only=(1, 2))]
        gathering["mlp"], tok = _exchange_start(bufs, [True] * 2, _SIBLING_FORWARD, "gather_mlp_d2d_start")
        ((_, ag_out),) = _exchange_wait(gathering["out"], tok, "gather_out_d2d_wait")
        return ag_out.reshape(D_MODEL, D_MODEL), tok

    def mlp_weights(after):
        (_, ag_up), (_, ag_down) = _exchange_wait(gathering["mlp"], after, "gather_mlp_d2d_wait")
        return ag_up, ag_down.reshape(D_FF, D_MODEL)

    sent = {}

    def mlp_grads(g_w_down, g_w_up):
        sent["mlp"], tok = _exchange_start(
            [g_w_down.reshape(N_DEV, D_FF // N_DEV, D_MODEL), g_w_up], [False, False], _ALL_PEERS, "grads_mlp_start")
        return tok

    def gmlp_grads(g_w_out, g_w_s):
        sent["gmlp"], tok = _exchange_start(
            [g_w_out.reshape(N_DEV, D_MODEL // N_DEV, D_MODEL), in_slot(g_w_s.astype(BF16))], [False, True], _ALL_PEERS,
            "grads_gmlp_start")
        return tok

    def in_grads(g_w_in_t, g_conv_w):
        sent["in"], tok = _exchange_start([g_w_in_t], [False], _ALL_PEERS, "grads_in_start")
        return tok

    def arrived_updates(after):
        (own_down, p_down), (own_up, p_up) = _exchange_wait(sent["mlp"], after, "grads_mlp_wait")
        (own_out, p_out), (_, p_ws) = _exchange_wait(sent["gmlp"], own_up, "grads_gmlp_wait")
        rows = lambda t: t.reshape(t.shape[:-3] + (N_HEADS * CHUNK, CHUNK))
        return [dict(parts=p_up, own=own_up, w=w_up[0], m=m_w_up[0], v=v_w_up[0]),
                dict(parts=p_down, own=own_down, w=w_down[0], m=m_w_down[0], v=v_w_down[0]),
                dict(parts=p_out, own=own_out, w=w_out[0], m=m_w_out[0], v=v_w_out[0]),
                dict(parts=rows(p_ws), own=rows(p_ws), w=rows(gm_w_s[0]), m=rows(m_gm_w_s[0]), v=rows(v_gm_w_s[0]),
                     mask=jnp.tril(jnp.ones((CHUNK, CHUNK), F32)))]

    small = {k: w[k][0] for k in _SMALL_PARAMS + ("gm_w_s",)}
    loss_part, grad_x, g = _local_step(
        x.reshape(n_batch * seq, D_MODEL), loss_target.reshape(n_batch * seq, D_MODEL), seq, small,
        dict(mixer_weights=mixer_weights, gmlp_done=gmlp_done, mixers_done=mixers_done, mlp_weights=mlp_weights,
             mlp_grads=mlp_grads, gmlp_grads=gmlp_grads, in_grads=in_grads, arrived_updates=arrived_updates, me=me,
             prenorm_after=second[2]), first_dep=tok_ici_1)

    sent_rows, tok_rows = _exchange_start([in_slot(_pack_slab(g, loss_part))], [True], _ALL_PEERS, "grads_rows_start")
    res = dict(zip(("w_up", "w_down", "w_out", "gm_w_s"), g["updates"]))
    ((own_in, p_in),) = _exchange_wait(sent["in"], tok_rows, "grads_in_wait")
    upd_in = _adamw_reduce(p_in, own_in, me, lying(w_in), lying(m_w_in), lying(v_w_in), "adamw_w_in")
    res["w_in"] = tuple(jnp.transpose(t, (1, 2, 0)) for t in upd_in)
    ((_, p_rows),) = _exchange_wait(sent_rows, upd_in[1], "grads_rows_wait")
    flat = lambda t: t[0] if t.ndim == 3 else t
    small_res, loss = _adamw_slab(
        p_rows, me, *({k: flat(d[k]) for k in _SMALL_PARAMS + ("conv_w",)} for d in (w, m, v)))
    res.update(small_res)
    res = {k: tuple(r.reshape(w[k].shape) for r in res[k]) for k in _WEIGHTS}

    outs = [loss, grad_x.reshape(x.shape)]
    for part in range(4):
        outs.extend(res[k][part] for k in _WEIGHTS)
    return tuple(outs)
```

```python
import functools

import jax
import jax.numpy as jnp
import numpy as np
from jax import lax
from jax.experimental import pallas as pl
from jax.experimental.pallas import tpu as pltpu

F32 = jnp.float32
BF16 = jnp.bfloat16

D_MODEL = 1024
GM_WIDTH = 512
SSM_WIDTH = 512
CONV_CH = 1024
N_HEADS = 8
HEAD_DIM = 64
N_STATE = 128
CHUNK = 128
D_FF = 4096
IN_COLS = 2568
IN_PAD = 2688
N_DEV = 8
EPS = 1e-6
ADAM_LR, ADAM_B1, ADAM_B2, ADAM_EPS, ADAM_WD, ADAM_STEP = 0.001, 0.9, 0.999, 1e-08, 0.01, 10
VMEM_LIMIT_BYTES = 56 * 1024 * 1024
TOKEN_TILE = 512
FF_TILE = 2048
WGRAD_TILE = 512
STACK_TILE = 256
_NT = (((1,), (1,)), ((), ()))
_TN = (((0,), (0,)), ((), ()))


def _params(*sem):
    return pltpu.CompilerParams(dimension_semantics=sem or None, vmem_limit_bytes=VMEM_LIMIT_BYTES)


def _dot(a, b, dims=None):
    if dims is None:
        return jnp.dot(a, b, preferred_element_type=F32)
    return lax.dot_general(a, b, dims, preferred_element_type=F32)


def _split_terms(x, terms):
    out, rem = [], x
    for i in range(terms):
        hi = rem.astype(BF16)
        out.append(hi)
        if i + 1 < terms:
            rem = rem - hi.astype(F32)
    return out


def _split_dot(x, m, terms):
    acc = None
    for hi in _split_terms(x, terms):
        part = _dot(hi, m)
        acc = part if acc is None else acc + part
    return acc


def _split_dot_left(m, x, terms):
    acc = None
    for hi in _split_terms(x, terms):
        part = _dot(m, hi)
        acc = part if acc is None else acc + part
    return acc


def _gelu_and_grad(x):
    c = 0.7978845608028654
    inner = c * (x + 0.044715 * x * x * x)
    t = jnp.tanh(inner)
    g = 0.5 * x * (1.0 + t)
    dg = 0.5 * (1.0 + t) + 0.5 * x * (1.0 - t * t) * c * (1.0 + 3.0 * 0.044715 * x * x)
    return g, dg


def _softplus(x):
    return jnp.maximum(x, 0.0) + jnp.log(1.0 + jnp.exp(-jnp.abs(x)))


def _rsum(x):
    return jnp.sum(x, axis=0, keepdims=True)


def _acc_rows(ref, part, first):
    val = jnp.broadcast_to(part, ref.shape)

    @pl.when(first)
    def _():
        ref[...] = val

    @pl.when(jnp.logical_not(first))
    def _():
        ref[...] += val


def _rms_bwd(n, g, dout):
    r = lax.rsqrt(jnp.mean(n * n, axis=-1, keepdims=True) + EPS)
    nh = n * r
    dg = dout * g
    dn = r * (dg - nh * jnp.mean(dg * nh, axis=-1, keepdims=True))
    return dn, _rsum(dout * nh)


def _const_mats():
    avg = np.kron(np.eye(4), np.full((HEAD_DIM, HEAD_DIM), 1.0 / HEAD_DIM))
    expand = np.zeros((CHUNK, SSM_WIDTH), np.float32)
    for h in range(N_HEADS):
        expand[h, h * HEAD_DIM:(h + 1) * HEAD_DIM] = 1.0
    tril = np.tril(np.ones((CHUNK, CHUNK), np.float32))
    as_bf16 = lambda a: jnp.asarray(a, dtype=BF16)
    return as_bf16(avg), as_bf16(expand), as_bf16(expand.T), as_bf16(tril), as_bf16(tril.T)


def _full(shape):
    nd = len(shape)
    return pl.BlockSpec(shape, lambda *_: (0,) * nd)


_HBM = pl.BlockSpec(memory_space=pltpu.HBM)
_SEM = pl.BlockSpec(memory_space=pltpu.SEMAPHORE)
_ALL_PEERS = tuple((k, 0) for k in range(1, N_DEV))
_SAME_CORE_PEERS = ((2, 0), (4, 0), (6, 0))
_SIBLING_FORWARD = ((1, 0), (1, 2), (1, 4), (1, 6))


def _flip(j, k):
    for bit in (4, 2, 1):
        if k & bit:
            j = j + bit - 2 * (j & bit)
    return j


def _copies(src, land, send_sems, recv_sems, hops, slots=None):
    x, y, c = lax.axis_index("x"), lax.axis_index("y"), lax.axis_index("c")
    me = 4 * x + 2 * y + c
    slots = range(len(src)) if slots is None else slots
    out = []
    for t in range(len(src)):
        for i, (k, b) in enumerate(hops):
            pos = (1 - x if k & 4 else x, 1 - y if k & 2 else y, 1 - c if k & 1 else c)
            peer = _flip(me, k)
            sem = slots[t] * len(hops) + i
            mk = functools.partial(pltpu.make_async_remote_copy, send_sem=send_sems.at[sem], recv_sem=recv_sems.at[sem],
                                   device_id=pos, device_id_type=pl.DeviceIdType.MESH)
            if land[t] is None and src[t].shape[0] != N_DEV:
                width = src[t].shape[1] // N_DEV
                slab = lambda j: src[t].at[:, pl.ds(pl.multiple_of(j * width, 128), width)]
                mine = functools.partial(mk, src_ref=slab(_flip(me, b)), dst_ref=slab(_flip(me, b)))
                theirs = functools.partial(mk, src_ref=slab(_flip(peer, b)), dst_ref=slab(_flip(peer, b)))
            elif land[t] is None:
                mine = functools.partial(mk, src_ref=src[t].at[_flip(me, b)], dst_ref=src[t].at[_flip(me, b)])
                theirs = functools.partial(mk, src_ref=src[t].at[_flip(peer, b)], dst_ref=src[t].at[_flip(peer, b)])
            else:
                assert b == 0
                mine = functools.partial(mk, src_ref=src[t].at[peer], dst_ref=land[t].at[me])
                theirs = functools.partial(mk, src_ref=src[t].at[peer], dst_ref=land[t].at[peer])
            out.append((mine, theirs))
    return out


def _exchange_start(srcs, inplace, peers, name, dep=None):
    n = len(srcs)
    lands = [None if ip else pltpu.with_memory_space_constraint(lax.empty(s.shape, s.dtype), pltpu.HBM)
             for s, ip in zip(srcs, inplace)]
    real_lands = [l for l in lands if l is not None]
    n_l = len(real_lands)
    deps = [] if dep is None else [dep]

    def body(*refs):
        src = refs[:n]
        land_refs = list(refs[n:n + n_l])
        send_sems, recv_sems = refs[n + n_l + len(deps)], refs[n + n_l + len(deps) + 1]
        token = refs[-1]
        land = [None if ip else land_refs.pop(0) for ip in inplace]
        for mine, _ in _copies(src, land, send_sems, recv_sems, peers):
            mine().start()
        token[...] = jnp.zeros_like(token)

    sem_t = pltpu.SemaphoreType.DMA((n * len(peers),))
    outs = pl.pallas_call(
        body, name=name,
        out_shape=(sem_t, sem_t) + tuple(pltpu.HBM(a.shape, a.dtype) for a in list(srcs) + real_lands)
        + (jax.ShapeDtypeStruct((8, 128), F32),),
        in_specs=[_HBM] * (n + n_l) + [pl.BlockSpec(memory_space=pl.ANY)] * len(deps),
        out_specs=(_SEM, _SEM) + (_HBM,) * (n + n_l) + (pl.BlockSpec(memory_space=pltpu.VMEM),),
        input_output_aliases={i: 2 + i for i in range(n + n_l)},
        compiler_params=pltpu.CompilerParams(has_side_effects=pltpu.SideEffectType.DATAFLOW_SIDE_EFFECTING),
    )(*[pltpu.with_memory_space_constraint(s, pltpu.HBM) for s in srcs], *real_lands, *deps)
    handle = dict(send=outs[0], recv=outs[1], srcs=outs[2:2 + n], lands=outs[2 + n:2 + n + n_l], inplace=inplace,
                  peers=peers)
    return handle, outs[-1]


def _exchange_wait(handle, after, name, only=None):
    srcs, lands, inplace, peers = handle["srcs"], handle["lands"], handle["inplace"], handle["peers"]
    slots = None
    if only is not None:
        assert all(inplace)
        slots, srcs, inplace = list(only), [srcs[t] for t in only], [True] * len(only)
    n, n_l = len(srcs), len(lands)
    after = after if isinstance(after, tuple) else (after,)

    def body(*refs):
        src = refs[:n]
        land_refs = list(refs[n:n + n_l])
        send_sems, recv_sems = refs[n + n_l], refs[n + n_l + 1]
        land = [None if ip else land_refs.pop(0) for ip in inplace]
        for mine, theirs in _copies(src, land, send_sems, recv_sems, peers, slots):
            mine().wait_send()
            theirs().wait_recv()

    outs = pl.pallas_call(
        body, name=name, out_shape=tuple(pltpu.HBM(a.shape, a.dtype) for a in list(srcs) + list(lands)),
        in_specs=[_HBM] * (n + n_l) + [_SEM, _SEM] + [pl.BlockSpec(memory_space=pl.ANY)] * len(after),
        out_specs=(_HBM,) * (n + n_l), input_output_aliases={i: i for i in range(n + n_l)},
        compiler_params=pltpu.CompilerParams(has_side_effects=pltpu.SideEffectType.DATAFLOW_SIDE_EFFECTING),
    )(*srcs, *lands, handle["send"], handle["recv"], *after)
    res, land_out = [], list(outs[n:])
    for t in range(n):
        res.append((outs[t], outs[t] if inplace[t] else land_out.pop(0)))
    return res


def _cast_to_slot(w, me, rows, name, cols=False, dep=None):
    r, cdim = w.shape[0], w.shape[-1]
    deps = [] if dep is None else [dep]

    def body(me_ref, w_ref, *rest):
        o_ref = rest[-1]
        if cols:
            o_ref[...] = w_ref[...].astype(BF16)
        else:
            o_ref[0] = w_ref[...].reshape(rows, cdim).astype(BF16)

    if cols:
        out_shape = jax.ShapeDtypeStruct((r, N_DEV * cdim), BF16)
        out_spec = pl.BlockSpec((rows, cdim), lambda i, me_ref: (i, me_ref[0]))
    else:
        out_shape = jax.ShapeDtypeStruct((N_DEV, r, cdim), BF16)
        out_spec = pl.BlockSpec((1, rows, cdim), lambda i, me_ref: (me_ref[0], i, 0))
    return pl.pallas_call(
        body, name=name, out_shape=out_shape,
        grid_spec=pltpu.PrefetchScalarGridSpec(
            num_scalar_prefetch=1, grid=(r // rows,),
            in_specs=[pl.BlockSpec((rows, cdim), lambda i, me_ref: (i, 0)) if w.ndim == 2 else
                      pl.BlockSpec((rows, 1, cdim), lambda i, me_ref: (i, 0, 0))]
            + [pl.BlockSpec(memory_space=pl.ANY)] * len(deps), out_specs=out_spec),
        compiler_params=_params("parallel"))(me, w, *deps)


def _stack_shards(blocks, rows, bn, name):
    n, r, cdim = blocks.shape

    def body(b_ref, o_ref, acc_ref):
        acc_ref[n * r:, :] = jnp.zeros((rows - n * r, bn), F32)
        for j in range(n):
            acc_ref[r * j:r * (j + 1), :] = b_ref[j].astype(F32)
        o_ref[...] = acc_ref[...].astype(BF16)

    return pl.pallas_call(
        body, name=name, grid=(cdim // bn,), out_shape=jax.ShapeDtypeStruct((rows, cdim), BF16),
        in_specs=[pl.BlockSpec((n, r, bn), lambda i: (0, 0, i))], out_specs=pl.BlockSpec((rows, bn), lambda i: (0, i)),
        scratch_shapes=[pltpu.VMEM((rows, bn), F32)], compiler_params=_params("parallel"))(blocks)


def _adamw_math(w, g, m, v):
    m = ADAM_B1 * m + (1.0 - ADAM_B1) * g
    v = ADAM_B2 * v + (1.0 - ADAM_B2) * (g * g)
    m_hat = m / (1.0 - ADAM_B1 ** ADAM_STEP)
    v_hat = v / (1.0 - ADAM_B2 ** ADAM_STEP)
    delta = -ADAM_LR * (m_hat / (jnp.sqrt(v_hat) + ADAM_EPS) + ADAM_WD * w)
    return delta, m, v


def _sum_parts(me, p_ref, own):
    g = None
    for j in range(N_DEV):
        term = (p_ref[j] if own is None else jnp.where(me == j, own, p_ref[j])).astype(F32)
        g = term if g is None else g + term
    return g


def _adamw_reduce(parts, own, me, w, m, v, name):
    r, _, cdim = w.shape

    def body(me_ref, p_ref, own_ref, w_ref, m_ref, v_ref, g_out, d_out, m_out, v_out):
        g = _sum_parts(me_ref[0], p_ref, own_ref[0]).reshape(r, 1, cdim)
        d, mn, vn = _adamw_math(w_ref[...], g, m_ref[...], v_ref[...])
        g_out[...] = g
        d_out[...] = d
        m_out[...] = mn
        v_out[...] = vn

    blk = pl.BlockSpec((r, 1, cdim), lambda i, me_ref: (0, 0, 0))
    return pl.pallas_call(
        body, name=name, out_shape=(jax.ShapeDtypeStruct(w.shape, F32),) * 4,
        grid_spec=pltpu.PrefetchScalarGridSpec(
            num_scalar_prefetch=1, grid=(1,),
            in_specs=[pl.BlockSpec((N_DEV, r, cdim), lambda i, me_ref: (0, 0, 0)),
                      pl.BlockSpec((1, r, cdim), lambda i, me_ref: (me_ref[0], 0, 0)), blk, blk, blk],
            out_specs=(blk,) * 4),
        compiler_params=_params("arbitrary"))(me, parts, own, w, m, v)


_IN_SPLITS = ((0, 512), (512, 1024), (1024, 1536), (1536, 2560), (2560, IN_PAD))


def _prenorm(x, g1, tm, dep=None):
    t_tok = x.shape[0]
    deps = [] if dep is None else [dep]

    def body(x_ref, g_ref, *rest):
        xv = x_ref[...]
        r = lax.rsqrt(jnp.mean(xv * xv, axis=-1, keepdims=True) + EPS)
        rest[-1][...] = (xv * r * g_ref[...]).astype(BF16)

    row = pl.BlockSpec((tm, D_MODEL), lambda i: (i, 0))
    return pl.pallas_call(
        body, name="prenorm", grid=(t_tok // tm,), out_shape=jax.ShapeDtypeStruct((t_tok, D_MODEL), BF16),
        in_specs=[row, _full((1, D_MODEL))] + [pl.BlockSpec(memory_space=pl.ANY)] * len(deps), out_specs=row,
        compiler_params=_params("parallel"))(x, g1, *deps)


def _in_proj(h1, w_in, tm):
    t_tok = h1.shape[0]

    def body(h_ref, w_ref, *outs):
        h = h_ref[...]
        for (a, b), o_ref in zip(_IN_SPLITS, outs):
            o_ref[...] = _dot(h, w_ref[a:b, :], _NT).astype(o_ref.dtype)

    row = lambda n: pl.BlockSpec((tm, n), lambda i: (i, 0))
    widths = [b - a for a, b in _IN_SPLITS]
    dtypes = (BF16, BF16, BF16, F32, F32)
    return pl.pallas_call(
        body, name="in_proj", grid=(t_tok // tm,),
        out_shape=tuple(jax.ShapeDtypeStruct((t_tok, n), dt) for n, dt in zip(widths, dtypes)),
        in_specs=[row(D_MODEL), _full((IN_PAD, D_MODEL))], out_specs=tuple(row(n) for n in widths),
        compiler_params=_params("parallel"))(h1, w_in)


def _lane_masks():
    lane = lax.broadcasted_iota(jnp.int32, (1, 2 * HEAD_DIM), 1)
    left = (lane < HEAD_DIM).astype(F32)
    return left, 1.0 - left


def _stack_pair(v, m_l, m_r):
    return jnp.concatenate([v * m_l, v * m_r], axis=0).astype(BF16)


def _head_mean(x, avg):
    n = avg.shape[0]
    return jnp.concatenate([_split_dot(x[:, n * i:n * (i + 1)], avg, 2) for i in range(x.shape[1] // n)], axis=1)


def _gmlp_common(u, v, lnw, lnb, avg, wcat_ref, bias, m_l, m_r):
    ug, dug = _gelu_and_grad(u)
    vg, dvg = _gelu_and_grad(v)
    mu = _head_mean(vg, avg)
    vc = vg - mu
    var = _head_mean(vc * vc, avg)
    rstd = lax.rsqrt(var + EPS)
    vhat = vc * rstd
    vn = vhat * lnw + lnb
    rows = []
    for r in range(u.shape[0] // CHUNK):
        cols = []
        for j in range(N_HEADS // 2):
            pair = vn[CHUNK * r:CHUNK * (r + 1), 128 * j:128 * (j + 1)]
            cols.append(_dot(wcat_ref[j], _stack_pair(pair, m_l, m_r)))
        rows.append(jnp.concatenate(cols, axis=1) + bias)
    mixed = jnp.concatenate(rows, axis=0)
    return ug, dug, dvg, rstd, vhat, vn, mixed


_GMLP_ROWS = 4 * CHUNK


def _gmlp_fwd(u, v, lnw, lnb, wcat, bias, avg):
    t_tok = u.shape[0]
    tm = min(_GMLP_ROWS, t_tok)

    def body(u_ref, v_ref, lnw_ref, lnb_ref, wcat_ref, bias_ref, avg_ref, o_ref):
        m_l, m_r = _lane_masks()
        ug, _, _, _, _, _, mixed = _gmlp_common(
            u_ref[...].astype(F32), v_ref[...].astype(F32), lnw_ref[...], lnb_ref[...], avg_ref[...], wcat_ref,
            bias_ref[...], m_l, m_r)
        o_ref[...] = (ug * mixed).astype(BF16)

    row = pl.BlockSpec((tm, GM_WIDTH), lambda i: (i, 0))
    return pl.pallas_call(
        body, name="gmlp_fwd", grid=(t_tok // tm,), out_shape=jax.ShapeDtypeStruct((t_tok, GM_WIDTH), BF16),
        in_specs=[row, row, _full((1, GM_WIDTH)), _full((1, GM_WIDTH)), _full(wcat.shape), _full(bias.shape),
                  _full(avg.shape)],
        out_specs=row, compiler_params=_params("parallel"))(u, v, lnw, lnb, wcat, bias, avg)


def _shift_rows(x, edge, j, down):
    groups, cols = x.shape[0] // 8, x.shape[1]
    amount = j if down else 8 - j
    rot = pltpu.roll(x.reshape(groups, 8, cols), amount, axis=1)
    edge_rot = pltpu.roll(edge, amount, axis=0)[None]
    sub = lax.broadcasted_iota(jnp.int32, (1, 8, 1), 1)
    if down:
        out = jnp.where(sub < j, jnp.concatenate([edge_rot, rot[:-1]], axis=0), rot)
    else:
        out = jnp.where(sub < 8 - j, rot, jnp.concatenate([rot[1:], edge_rot], axis=0))
    return out.reshape(x.shape)


def _conv_pre(xbc, tail, cw_ref, cb):
    taps = [_shift_rows(xbc, tail, 3 - k, True) for k in range(3)] + [xbc]
    return cb + cw_ref[0:1, :] * taps[0] + cw_ref[1:2, :] * taps[1] + cw_ref[2:3, :] * taps[2] + cw_ref[3:4, :] * taps[3]


def _ssd_common(pre, dtr, dtb, alog, expand, tril):
    q = CHUNK
    sg = jax.nn.sigmoid(pre)
    act = pre * sg
    lane = lax.broadcasted_iota(jnp.int32, (1, CHUNK), 1)
    a_row = jnp.where(lane < N_HEADS, -jnp.exp(alog), 0.0)
    dtp = dtr + dtb
    dt = _softplus(dtp)
    a_cs = _split_dot_left(tril, dt * a_row, 3)
    a_cs_t = a_cs.T
    dt_exp = _split_dot(dt, expand, 3)
    a_exp = _split_dot(a_cs, expand, 3)
    a_end = a_exp[q - 1:q, :]
    li = lax.broadcasted_iota(jnp.int32, (q, q), 0)
    si = lax.broadcasted_iota(jnp.int32, (q, q), 1)
    causal = si <= li
    decay = []
    for h in range(N_HEADS):
        seg = a_cs[:, h:h + 1] - a_cs_t[h:h + 1, :]
        decay.append(jnp.where(causal, jnp.exp(jnp.minimum(seg, 0.0)), 0.0))
    return dict(pre=pre, sg=sg, act=act, a_row=a_row, dtp=dtp, dt=dt, dt_exp=dt_exp, a_exp=a_exp,
                e=jnp.exp(a_exp), w_end=jnp.exp(a_end - a_exp), cd=jnp.exp(a_end), decay=decay)


def _ssd_specs(t_tok, seq, reverse):
    nb, nc = t_tok // seq, seq // CHUNK

    def chunk(c):
        return nc - 1 - c if reverse else c

    def row(n, col=0):
        return pl.BlockSpec((nb, CHUNK, n), lambda c: (0, chunk(c), col))

    tail = pl.BlockSpec((nb, 8, CONV_CH), lambda c: (0, jnp.maximum(chunk(c) * (CHUNK // 8) - 1, 0), 0))
    states = pl.BlockSpec((nb, 1, N_STATE, SSM_WIDTH), lambda c: (0, chunk(c), 0, 0))
    fold = lambda a: a.reshape(nb, seq, a.shape[-1])
    unfold = lambda a: a.reshape(t_tok, a.shape[-1])
    return nb, nc, row, tail, states, fold, unfold


def _ssd_fwd(z, xbc, dtr, cw, cb, dtb, alog, dskip_exp, nw, expand, tril, seq, dep=None):
    t_tok = z.shape[0]
    nb, nc, row, tail, states_spec, fold, unfold = _ssd_specs(t_tok, seq, False)

    def body(z_ref, xbc_ref, tail_ref, dtr_ref, cw_ref, cb_ref, dtb_ref, alog_ref, dsk_ref, nw_ref, exp_ref,
             tril_ref, o_ref, y_ref, st_ref, pre_ref, state_ref):
        c = pl.program_id(0)

        @pl.when(c == 0)
        def _():
            state_ref[...] = jnp.zeros_like(state_ref)

        m_l, m_r = _lane_masks()
        for s in range(nb):
            pre = _conv_pre(xbc_ref[s], jnp.where(c == 0, 0.0, tail_ref[s]), cw_ref, cb_ref[...])
            pre_ref[s] = pre
            f = _ssd_common(pre, dtr_ref[s], dtb_ref[...], alog_ref[...], exp_ref[...], tril_ref[...])
            act = f["act"]
            xs = act[:, :SSM_WIDTH]
            xdt = xs * f["dt_exp"]
            xw = xdt * f["w_end"]
            state = state_ref[s]
            st_ref[s, 0] = state
            ydiag, yoff, snew = [], [], []
            for g in range(2):
                bg = act[:, 512 + 128 * g:640 + 128 * g].astype(BF16)
                cg = act[:, 768 + 128 * g:896 + 128 * g].astype(BF16)
                cb_mat = _dot(cg, bg, _NT)
                for pr in range(2):
                    h0 = 4 * g + 2 * pr
                    gcat = jnp.concatenate(
                        [(cb_mat * f["decay"][h0]).astype(BF16), (cb_mat * f["decay"][h0 + 1]).astype(BF16)], axis=1)
                    ydiag.append(_dot(gcat, _stack_pair(xdt[:, 64 * h0:64 * h0 + 128], m_l, m_r)))
                yoff.append(_dot(cg, state[:, 256 * g:256 * (g + 1)].astype(BF16)))
                snew.append(_dot(bg, xw[:, 256 * g:256 * (g + 1)].astype(BF16), _TN))
            y = jnp.concatenate(ydiag, axis=1) + f["e"] * jnp.concatenate(yoff, axis=1) + dsk_ref[...] * xs
            state_ref[s] = state * f["cd"] + jnp.concatenate(snew, axis=1)
            y_ref[s] = y
            zv = z_ref[s].astype(F32)
            yg = y * (zv * jax.nn.sigmoid(zv))
            outs = []
            for g in range(2):
                ygg = yg[:, 256 * g:256 * (g + 1)]
                outs.append(ygg * lax.rsqrt(jnp.mean(ygg * ygg, axis=-1, keepdims=True) + EPS))
            o_ref[s] = (jnp.concatenate(outs, axis=1) * nw_ref[...]).astype(BF16)

    consts = [cw, cb, dtb, alog, dskip_exp, nw, expand, tril]
    deps = [] if dep is None else [dep]
    n_in = 4 + len(consts)

    def body_skipping_dep(*refs):
        body(*refs[:n_in], *refs[n_in + len(deps):])

    sd = lambda n, dt: jax.ShapeDtypeStruct((nb, seq, n), dt)
    o, y, states, pre = pl.pallas_call(
        body_skipping_dep, name="ssd_fwd", grid=(nc,),
        out_shape=(sd(SSM_WIDTH, BF16), sd(SSM_WIDTH, F32), jax.ShapeDtypeStruct((nb, nc, N_STATE, SSM_WIDTH), F32),
                   sd(CONV_CH, F32)),
        in_specs=[row(SSM_WIDTH), row(CONV_CH), tail, row(CHUNK)] + [_full(a.shape) for a in consts]
        + [pl.BlockSpec(memory_space=pl.ANY)] * len(deps),
        out_specs=(row(SSM_WIDTH), row(SSM_WIDTH), states_spec, row(CONV_CH)),
        scratch_shapes=[pltpu.VMEM((nb, N_STATE, SSM_WIDTH), F32)],
        compiler_params=_params("arbitrary"))(fold(z), fold(xbc), fold(xbc), fold(dtr), *consts, *deps)
    return unfold(o), unfold(y), states, unfold(pre)


def _out_proj(mix_a, mix_b, w_out, x, g2, g3, tm, dep=None):
    t_tok = x.shape[0]
    deps = [] if dep is None else [dep]

    def body(a_ref, b_ref, w_ref, x_ref, g2_ref, g3_ref, *rest):
        o_ref, x2_ref, h3_ref = rest[-3:]
        o = _dot(a_ref[...], w_ref[0:GM_WIDTH, :]) + _dot(b_ref[...], w_ref[GM_WIDTH:, :])
        o_ref[...] = o
        r2 = lax.rsqrt(jnp.mean(o * o, axis=-1, keepdims=True) + EPS)
        x2 = x_ref[...] + o * r2 * g2_ref[...]
        x2_ref[...] = x2
        r3 = lax.rsqrt(jnp.mean(x2 * x2, axis=-1, keepdims=True) + EPS)
        h3_ref[...] = (x2 * r3 * g3_ref[...]).astype(BF16)

    row = lambda n: pl.BlockSpec((tm, n), lambda i: (i, 0))
    sd = lambda dt: jax.ShapeDtypeStruct((t_tok, D_MODEL), dt)
    return pl.pallas_call(
        body, name="out_proj", grid=(t_tok // tm,), out_shape=(sd(F32), sd(F32), sd(BF16)),
        in_specs=[row(GM_WIDTH), row(SSM_WIDTH), _full((D_MODEL, D_MODEL)), row(D_MODEL), _full((1, D_MODEL)),
                  _full((1, D_MODEL))] + [pl.BlockSpec(memory_space=pl.ANY)] * len(deps),
        out_specs=(row(D_MODEL),) * 3, compiler_params=_params("parallel"))(mix_a, mix_b, w_out, x, g2, g3, *deps)


def _mlp_fwd(h3, w_up, w_down, x2, target, g4, tm, tf):
    t_tok = x2.shape[0]

    def up_body(h_ref, wu_ref, ra_ref):
        ra_ref[...] = jnp.maximum(_dot(h_ref[...], wu_ref[...]), 0.0).astype(BF16)

    tu = min(2 * tm, t_tok)
    ra = pl.pallas_call(
        up_body, name="mlp_up", grid=(D_FF // tf, t_tok // tu), out_shape=jax.ShapeDtypeStruct((t_tok, D_FF), BF16),
        in_specs=[pl.BlockSpec((tu, D_MODEL), lambda j, i: (i, 0)), pl.BlockSpec((D_MODEL, tf), lambda j, i: (0, j))],
        out_specs=pl.BlockSpec((tu, tf), lambda j, i: (i, j)), compiler_params=_params("parallel", "parallel"))(h3, w_up)

    def down_body(ra_ref, wd_ref, x2_ref, t_ref, g4_ref, dd_ref, dy_ref, dg4_ref, loss_ref):
        i = pl.program_id(0)
        rav = ra_ref[...]
        dvec = _dot(rav * rav, wd_ref[...])
        r4 = lax.rsqrt(jnp.mean(dvec * dvec, axis=-1, keepdims=True) + EPS)
        dn = dvec * r4
        g4 = g4_ref[...]
        err = x2_ref[...] + dn * g4 - t_ref[...]
        dy = err * (1.0 / D_MODEL)
        dy_ref[...] = dy
        dg = dy * g4
        dd_ref[...] = (r4 * (dg - dn * jnp.mean(dg * dn, axis=-1, keepdims=True))).astype(BF16)
        _acc_rows(dg4_ref, _rsum(dy * dn), i == 0)
        tile_loss = 0.5 * jnp.sum(jnp.sum(err * err, axis=-1, keepdims=True), axis=0, keepdims=True) / D_MODEL
        _acc_rows(loss_ref, jnp.broadcast_to(tile_loss, (1, 128)), i == 0)

    row = pl.BlockSpec((tm, D_MODEL), lambda i: (i, 0))
    dd, dy, dg4, loss = pl.pallas_call(
        down_body, name="mlp_down", grid=(t_tok // tm,),
        out_shape=(jax.ShapeDtypeStruct((t_tok, D_MODEL), BF16), jax.ShapeDtypeStruct((t_tok, D_MODEL), F32),
                   jax.ShapeDtypeStruct((1, D_MODEL), F32), jax.ShapeDtypeStruct((1, 128), F32)),
        in_specs=[pl.BlockSpec((tm, D_FF), lambda i: (i, 0)), _full((D_FF, D_MODEL)), row, row, _full((1, D_MODEL))],
        out_specs=(row, row, _full((1, D_MODEL)), _full((1, 128))),
        compiler_params=_params("arbitrary"))(ra, w_down, x2, target, g4)
    return ra, dd, dy, dg4, loss


def _mlp_bwd(dd, w_down, ra, w_up, x2, dy, o, g3, g2, tm, tf):
    t_tok = x2.shape[0]

    def hidden_body(dd_ref, wd_ref, ra_ref, da_ref):
        df = _dot(dd_ref[...], wd_ref[...], _NT)
        da_ref[...] = (df * (2.0 * ra_ref[...].astype(F32))).astype(BF16)

    tu = min(2 * tm, t_tok)
    da = pl.pallas_call(
        hidden_body, name="mlp_bwd_hidden", grid=(D_FF // tf, t_tok // tu),
        out_shape=jax.ShapeDtypeStruct((t_tok, D_FF), BF16),
        in_specs=[pl.BlockSpec((tu, D_MODEL), lambda j, i: (i, 0)), pl.BlockSpec((tf, D_MODEL), lambda j, i: (j, 0)),
                  pl.BlockSpec((tu, tf), lambda j, i: (i, j))],
        out_specs=pl.BlockSpec((tu, tf), lambda j, i: (i, j)),
        compiler_params=_params("parallel", "parallel"))(dd, w_down, ra)

    def in_body(da_ref, wu_ref, x2_ref, dy_ref, o_ref, g3_ref, g2_ref, dx2_ref, do_ref, dg3_ref, dg2_ref):
        i = pl.program_id(0)
        dh3 = _dot(da_ref[...], wu_ref[...], _NT)
        dn3, dg3 = _rms_bwd(x2_ref[...], g3_ref[...], dh3)
        dx2 = dy_ref[...] + dn3
        dx2_ref[...] = dx2
        do, dg2 = _rms_bwd(o_ref[...], g2_ref[...], dx2)
        do_ref[...] = do.astype(BF16)
        _acc_rows(dg3_ref, dg3, i == 0)
        _acc_rows(dg2_ref, dg2, i == 0)

    row = pl.BlockSpec((tm, D_MODEL), lambda i: (i, 0))
    vec = _full((1, D_MODEL))
    sd = lambda dt: jax.ShapeDtypeStruct((t_tok, D_MODEL), dt)
    dx2, do, dg3, dg2 = pl.pallas_call(
        in_body, name="mlp_bwd_in", grid=(t_tok // tm,),
        out_shape=(sd(F32), sd(BF16), jax.ShapeDtypeStruct((1, D_MODEL), F32), jax.ShapeDtypeStruct((1, D_MODEL), F32)),
        in_specs=[pl.BlockSpec((tm, D_FF), lambda i: (i, 0)), _full((D_MODEL, D_FF)), row, row, row, vec, vec],
        out_specs=(row, row, vec, vec), compiler_params=_params("arbitrary"))(da, w_up, x2, dy, o, g3, g2)
    return da, dx2, do, dg3, dg2


def _wgrad(a, b, out_blocks, bm, bn, bk, square_a, name, dep=None):
    t_tok, m = a.shape
    n = b.shape[1]
    nk = t_tok // bk

    def body(a_ref, b_ref, *rest):
        o_ref, acc_ref = rest[-2:]
        k = pl.program_id(2)
        av = a_ref[...]
        if square_a:
            av = av * av
        part = _dot(av, b_ref[...], _TN)

        def emit(res):
            if out_blocks is None:
                o_ref[...] = res.astype(BF16)
            else:
                o_ref[0] = res.astype(BF16)

        if nk == 1:
            emit(part)
            return

        @pl.when(k == 0)
        def _():
            acc_ref[...] = part

        @pl.when(k > 0)
        def _():
            acc_ref[...] += part

        @pl.when(k == nk - 1)
        def _():
            emit(acc_ref[...])

    if out_blocks is None:
        out_shape = jax.ShapeDtypeStruct((m, n), BF16)
        out_spec = pl.BlockSpec((bm, bn), lambda i, j, k: (i, j))
    else:
        assert n // out_blocks == bn
        out_shape = jax.ShapeDtypeStruct((out_blocks, m, bn), BF16)
        out_spec = pl.BlockSpec((1, bm, bn), lambda i, j, k: (j, i, 0))
    deps = [] if dep is None else [dep]
    return pl.pallas_call(
        body, name=name, grid=(m // bm, n // bn, nk), out_shape=out_shape,
        in_specs=[pl.BlockSpec((bk, bm), lambda i, j, k: (k, i)), pl.BlockSpec((bk, bn), lambda i, j, k: (k, j))]
        + [pl.BlockSpec(memory_space=pl.ANY)] * len(deps),
        out_specs=out_spec, scratch_shapes=[pltpu.VMEM((bm, bn) if nk > 1 else (8, 128), F32)],
        compiler_params=_params("parallel", "parallel", "arbitrary"))(a, b, *deps)


def _wgrad_in_chunked(h1, pieces, bn, bk, dep=None):
    t_tok = h1.shape[0]
    nk = t_tok // bk
    shard = IN_COLS // N_DEV
    widths = [b - a for a, b in _IN_SPLITS]

    def body(h_ref, *rest):
        piece_refs = rest[:len(widths)]
        o_ref, acc_ref = rest[-2:]
        k = pl.program_id(1)
        hv = h_ref[...]
        for (a, b), r in zip(_IN_SPLITS, piece_refs):
            part = _dot(r[...], hv, _TN)

            @pl.when(k == 0)
            def _():
                acc_ref[a:b, :] = part

            @pl.when(k > 0)
            def _():
                acc_ref[a:b, :] += part

        @pl.when(k == nk - 1)
        def _():
            for j in range(N_DEV):
                o_ref[j] = acc_ref[shard * j:shard * (j + 1), :].astype(BF16)

    deps = [] if dep is None else [dep]
    return pl.pallas_call(
        body, name="wgrad_in", grid=(D_MODEL // bn, nk), out_shape=jax.ShapeDtypeStruct((N_DEV, shard, D_MODEL), BF16),
        in_specs=[pl.BlockSpec((bk, bn), lambda j, k: (k, j))] + [pl.BlockSpec((bk, n), lambda j, k: (k, 0)) for n in widths]
        + [pl.BlockSpec(memory_space=pl.ANY)] * len(deps),
        out_specs=pl.BlockSpec((N_DEV, shard, bn), lambda j, k: (0, 0, j)),
        scratch_shapes=[pltpu.VMEM((IN_PAD, bn), F32)],
        compiler_params=_params("parallel", "arbitrary"))(h1, *pieces, *deps)


def _dmix_wgrad_out(do, w_out, mix_a, mix_b, tm, dep=None):
    t_tok = do.shape[0]
    steps = t_tok // tm
    deps = [] if dep is None else [dep]

    def body(d_ref, w_ref, a_ref, b_ref, *rest):
        dm_ref, g_ref, acc_ref = rest[-3:]
        i = pl.program_id(0)
        dov = d_ref[...]
        dm_ref[...] = _dot(dov, w_ref[...], _NT).astype(BF16)
        for (lo, hi), r in zip(((0, GM_WIDTH), (GM_WIDTH, D_MODEL)), (a_ref, b_ref)):
            part = _dot(r[...], dov, _TN)

            @pl.when(i == 0)
            def _():
                acc_ref[lo:hi, :] = part

            @pl.when(i > 0)
            def _():
                acc_ref[lo:hi, :] += part

        @pl.when(i == steps - 1)
        def _():
            g_ref[...] = acc_ref[...].astype(BF16)

    row = lambda n: pl.BlockSpec((tm, n), lambda i: (i, 0))
    return pl.pallas_call(
        body, name="dmix_wgrad_out", grid=(steps,),
        out_shape=(jax.ShapeDtypeStruct((t_tok, D_MODEL), BF16), jax.ShapeDtypeStruct((D_MODEL, D_MODEL), BF16)),
        in_specs=[row(D_MODEL), _full((D_MODEL, D_MODEL)), row(GM_WIDTH), row(SSM_WIDTH)]
        + [pl.BlockSpec(memory_space=pl.ANY)] * len(deps),
        out_specs=(row(D_MODEL), _full((D_MODEL, D_MODEL))), scratch_shapes=[pltpu.VMEM((D_MODEL, D_MODEL), F32)],
        compiler_params=_params("arbitrary"))(do, w_out, mix_a, mix_b, *deps)


def _gmlp_bwd(dmix, u, v, lnw, lnb, wcat, wtcat, bias, avg, expand_t):
    t_tok = u.shape[0]
    tm = min(_GMLP_ROWS, t_tok)

    def body(dm_ref, u_ref, v_ref, lnw_ref, lnb_ref, wcat_ref, wtcat_ref, bias_ref, avg_ref, expt_ref, du_ref, dv_ref,
             dw_ref, db_ref, dlnw_ref, dlnb_ref):
        i = pl.program_id(0)
        m_l, m_r = _lane_masks()
        avg = avg_ref[...]
        lnw = lnw_ref[...]
        ug, dug, dvg, rstd, vhat, vn, mixed = _gmlp_common(
            u_ref[...].astype(F32), v_ref[...].astype(F32), lnw, lnb_ref[...], avg, wcat_ref, bias_ref[...], m_l, m_r)
        dya = dm_ref[...].astype(F32)
        du_ref[...] = (dya * mixed * dug).astype(BF16)
        dmixed = dya * ug
        dvn_rows, dws, dbt = [], [None] * N_HEADS, None
        for r in range(tm // CHUNK):
            dvn_cols = []
            for j in range(N_HEADS // 2):
                dmp = dmixed[CHUNK * r:CHUNK * (r + 1), 128 * j:128 * (j + 1)]
                dvn_cols.append(_dot(wtcat_ref[j], _stack_pair(dmp, m_l, m_r)))
                vnp = vn[CHUNK * r:CHUNK * (r + 1), 128 * j:128 * (j + 1)].astype(BF16)
                for i_h, mask in enumerate((m_l, m_r)):
                    part = _dot((dmp * mask).astype(BF16), vnp, _NT)
                    dws[2 * j + i_h] = part if r == 0 else dws[2 * j + i_h] + part
            dvn_rows.append(jnp.concatenate(dvn_cols, axis=1))
            part = _split_dot(dmixed[CHUNK * r:CHUNK * (r + 1), :], expt_ref[...], 2)
            dbt = part if r == 0 else dbt + part
        dvn = jnp.concatenate(dvn_rows, axis=0)
        dvh = dvn * lnw
        dvgel = rstd * (dvh - _head_mean(dvh, avg) - vhat * _head_mean(dvh * vhat, avg))
        dv_ref[...] = (dvgel * dvg).astype(BF16)
        first = i == 0

        @pl.when(first)
        def _():
            for h in range(N_HEADS):
                dw_ref[h] = dws[h]
            db_ref[...] = dbt

        @pl.when(jnp.logical_not(first))
        def _():
            for h in range(N_HEADS):
                dw_ref[h] += dws[h]
            db_ref[...] += dbt

        _acc_rows(dlnw_ref, _rsum(dvn * vhat), first)
        _acc_rows(dlnb_ref, _rsum(dvn), first)

    row = pl.BlockSpec((tm, GM_WIDTH), lambda i: (i, 0))
    consts = [lnw, lnb, wcat, wtcat, bias, avg, expand_t]
    return pl.pallas_call(
        body, name="gmlp_bwd", grid=(t_tok // tm,),
        out_shape=(jax.ShapeDtypeStruct((t_tok, GM_WIDTH), BF16), jax.ShapeDtypeStruct((t_tok, GM_WIDTH), BF16),
                   jax.ShapeDtypeStruct((N_HEADS, CHUNK, CHUNK), F32), jax.ShapeDtypeStruct((CHUNK, CHUNK), F32),
                   jax.ShapeDtypeStruct((1, GM_WIDTH), F32), jax.ShapeDtypeStruct((1, GM_WIDTH), F32)),
        in_specs=[row, row, row] + [_full(a.shape) for a in consts],
        out_specs=(row, row, _full((N_HEADS, CHUNK, CHUNK)), _full((CHUNK, CHUNK)), _full((1, GM_WIDTH)),
                   _full((1, GM_WIDTH))),
        compiler_params=_params("arbitrary"))(dmix, u, v, *consts)


def _ssd_bwd(dmix, z, xbc, pre, dtr, y, states, cw, cb, dtb, alog, dskip_exp, nw, expand, expand_t, tril, triu, seq,
             dep=None):
    t_tok = z.shape[0]
    nb, nc, row, _, states_spec, fold, unfold = _ssd_specs(t_tok, seq, True)
    q = CHUNK

    def one_sequence(s, dm_ref, z_ref, xbc_ref, pre_ref, dtr_ref, y_ref, st_ref, cw_ref, dtb_ref, alog_ref, dsk_ref,
                     nw_ref, exp_ref, expt_ref, tril_ref, triu_ref, dz_ref, dxbc_ref, ddt_ref, dhead_ref, dstate_ref):
        m_l, m_r = _lane_masks()
        expt = expt_ref[...]
        f = _ssd_common(pre_ref[s], dtr_ref[s], dtb_ref[...], alog_ref[...], exp_ref[...], tril_ref[...])
        act, pre, sg = f["act"], f["pre"], f["sg"]
        xs = act[:, :SSM_WIDTH]
        xdt = xs * f["dt_exp"]
        xw = xdt * f["w_end"]
        state = st_ref[s, 0]
        dstate = dstate_ref[s]
        zv, yv, dout, nw = z_ref[s].astype(F32), y_ref[s], dm_ref[s].astype(F32), nw_ref[...]
        sz = jax.nn.sigmoid(zv)
        sl = zv * sz
        yg = yv * sl
        tv = dout * nw
        dyg_parts, ygh_parts = [], []
        for g in range(2):
            ygg = yg[:, 256 * g:256 * (g + 1)]
            rr = lax.rsqrt(jnp.mean(ygg * ygg, axis=-1, keepdims=True) + EPS)
            ygh = ygg * rr
            tg = tv[:, 256 * g:256 * (g + 1)]
            dyg_parts.append(rr * (tg - ygh * jnp.mean(tg * ygh, axis=-1, keepdims=True)))
            ygh_parts.append(ygh)
        dyg = jnp.concatenate(dyg_parts, axis=1)
        dnw = _rsum(dout * jnp.concatenate(ygh_parts, axis=1))
        dy = dyg * sl
        dz_ref[s] = (dyg * yv * (sz * (1.0 + zv * (1.0 - sz)))).astype(BF16)
        ddsk = _rsum(dy * xs)
        dye = dy * f["e"]
        lane = lax.broadcasted_iota(jnp.int32, (q, q), 1)
        sub = lax.broadcasted_iota(jnp.int32, (q, q), 0)
        rs_mat = jnp.zeros((q, q), F32)
        cs_mat = jnp.zeros((q, q), F32)
        dxdt_cols, yoff, dst_in, dxw, d_b, d_c = [], [], [], [], [], []
        for g in range(2):
            bg = act[:, 512 + 128 * g:640 + 128 * g].astype(BF16)
            cg = act[:, 768 + 128 * g:896 + 128 * g].astype(BF16)
            cb_mat = _dot(cg, bg, _NT)
            stg = state[:, 256 * g:256 * (g + 1)].astype(BF16)
            dyeg = dye[:, 256 * g:256 * (g + 1)].astype(BF16)
            yoff.append(_dot(cg, stg))
            dcg = _dot(dyeg, stg, _NT)
            dst_in.append(_dot(cg, dyeg, _TN))
            dcb = jnp.zeros((q, q), F32)
            for pr in range(2):
                h0 = 4 * g + 2 * pr
                gf = [cb_mat * f["decay"][h0], cb_mat * f["decay"][h0 + 1]]
                gcat = jnp.concatenate([gf[0].astype(BF16), gf[1].astype(BF16)], axis=1)
                xst = _stack_pair(xdt[:, 64 * h0:64 * h0 + 128], m_l, m_r)
                dyp = dy[:, 64 * h0:64 * h0 + 128].astype(BF16)
                dgcat = _dot(dyp, xst, _NT)
                dxst = _dot(gcat, dyp, _TN)
                dxdt_cols.append(dxst[:q] * m_l + dxst[q:] * m_r)
                for i in range(2):
                    h = h0 + i
                    dg = dgcat[:, q * i:q * (i + 1)]
                    mm = dg * gf[i]
                    rs_mat = rs_mat + jnp.where(lane == h, jnp.sum(mm, axis=1, keepdims=True), 0.0)
                    cs_mat = cs_mat + jnp.where(sub == h, jnp.sum(mm, axis=0, keepdims=True), 0.0)
                    dcb = dcb + dg * f["decay"][h]
            dcb16 = dcb.astype(BF16)
            dstg = dstate[:, 256 * g:256 * (g + 1)].astype(BF16)
            d_c.append(dcg + _dot(dcb16, bg))
            dxw.append(_dot(bg, dstg))
            d_b.append(_dot(dcb16, cg, _TN) + _dot(xw[:, 256 * g:256 * (g + 1)].astype(BF16), dstg, _NT))
        dxw = jnp.concatenate(dxw, axis=1)
        dxdt = jnp.concatenate(dxdt_cols, axis=1) + dxw * f["w_end"]
        qv = dxw * xw
        end_row = _rsum(qv) + _rsum(dstate * state) * f["cd"]
        x2 = dye * jnp.concatenate(yoff, axis=1) - qv
        row_i = lax.broadcasted_iota(jnp.int32, (q, 1), 0)
        x2 = x2 + jnp.where(row_i == q - 1, end_row, 0.0)
        da_cs = _split_dot(x2, expt, 2) + rs_mat - cs_mat.T
        ddt = _split_dot(dxdt * xs, expt, 2)
        dxs = dsk_ref[...] * dy + dxdt * f["dt_exp"]
        dda = _split_dot_left(triu_ref[...], da_cs, 3)
        ddt = ddt + dda * f["a_row"]
        dalog = _rsum(dda * f["dt"]) * f["a_row"]
        draw = ddt * jax.nn.sigmoid(f["dtp"])
        ddt_ref[s] = draw.astype(BF16)
        dact = jnp.concatenate([dxs] + d_b + d_c, axis=1)
        dpre = dact * (sg * (1.0 + pre * (1.0 - sg)))
        dhead = dhead_ref[s]
        xv = xbc_ref[s]
        shifted = [_shift_rows(dpre, dhead, 3 - k, False) for k in range(3)] + [dpre]
        dxbc = cw_ref[3:4, :] * dpre
        for k in range(3):
            dxbc = dxbc + cw_ref[k:k + 1, :] * shifted[k]
        dxbc_ref[s] = dxbc.astype(BF16)
        dhead_ref[s] = dpre[0:8, :]
        dstate_ref[s] = dstate * f["cd"] + jnp.concatenate(dst_in, axis=1)
        row8 = lax.broadcasted_iota(jnp.int32, (8, 1), 0)
        dcw = jnp.zeros((8, CONV_CH), F32)
        for k in range(4):
            dcw = dcw + jnp.where(row8 == k, _rsum(shifted[k] * xv), 0.0)
        return dcw, _rsum(dpre), _rsum(draw), dalog, _split_dot(ddsk, expt, 3), dnw

    def body(dm_ref, z_ref, xbc_ref, pre_ref, dtr_ref, y_ref, st_ref, cw_ref, cb_ref, dtb_ref, alog_ref, dsk_ref,
             nw_ref, exp_ref, expt_ref, tril_ref, triu_ref, dz_ref, dxbc_ref, ddt_ref, dcw_ref, dcb_ref, ddtb_ref,
             dalog_ref, dd_ref, dnw_ref, dhead_ref, dstate_ref):
        c = pl.program_id(0)
        first = c == 0

        @pl.when(first)
        def _():
            dstate_ref[...] = jnp.zeros_like(dstate_ref)
            dhead_ref[...] = jnp.zeros_like(dhead_ref)

        total = None
        for s in range(nb):
            parts = one_sequence(s, dm_ref, z_ref, xbc_ref, pre_ref, dtr_ref, y_ref, st_ref, cw_ref, dtb_ref, alog_ref,
                                 dsk_ref, nw_ref, exp_ref, expt_ref, tril_ref, triu_ref, dz_ref, dxbc_ref, ddt_ref,
                                 dhead_ref, dstate_ref)
            total = parts if total is None else tuple(a + b for a, b in zip(total, parts))
        dcw = total[0]

        @pl.when(first)
        def _():
            dcw_ref[...] = dcw

        @pl.when(jnp.logical_not(first))
        def _():
            dcw_ref[...] += dcw

        for ref, part in zip((dcb_ref, ddtb_ref, dalog_ref, dd_ref, dnw_ref), total[1:]):
            _acc_rows(ref, part, first)

    consts = [cw, cb, dtb, alog, dskip_exp, nw, expand, expand_t, tril, triu]
    deps = [] if dep is None else [dep]
    n_in = 7 + len(consts)

    def body_skipping_dep(*refs):
        body(*refs[:n_in], *refs[n_in + len(deps):])

    acc = lambda n: jax.ShapeDtypeStruct((1, n), F32)
    sd = lambda n: jax.ShapeDtypeStruct((nb, seq, n), BF16)
    dz, dxbc, ddt, *small_grads = pl.pallas_call(
        body_skipping_dep, name="ssd_bwd", grid=(nc,),
        out_shape=(sd(SSM_WIDTH), sd(CONV_CH), sd(CHUNK), jax.ShapeDtypeStruct((8, CONV_CH), F32), acc(CONV_CH),
                   acc(CHUNK), acc(CHUNK), acc(CHUNK), acc(SSM_WIDTH)),
        in_specs=[row(SSM_WIDTH, col=1), row(SSM_WIDTH), row(CONV_CH), row(CONV_CH), row(CHUNK), row(SSM_WIDTH),
                  states_spec]
        + [_full(a.shape) for a in consts] + [pl.BlockSpec(memory_space=pl.ANY)] * len(deps),
        out_specs=(row(SSM_WIDTH), row(CONV_CH), row(CHUNK), _full((8, CONV_CH)), _full((1, CONV_CH)),
                   _full((1, CHUNK)), _full((1, CHUNK)), _full((1, CHUNK)), _full((1, SSM_WIDTH))),
        scratch_shapes=[pltpu.VMEM((nb, 8, CONV_CH), F32), pltpu.VMEM((nb, N_STATE, SSM_WIDTH), F32)],
        compiler_params=_params("arbitrary"))(
            fold(dmix), fold(z), fold(xbc), fold(pre), fold(dtr), fold(y), states, *consts, *deps)
    return (unfold(dz), unfold(dxbc), unfold(ddt), *small_grads)


def _in_bwd(du, dv, dz, dxbc, ddt, w_in, x, dx2, g1, tm, me, riders=(), dep=None):
    t_tok = x.shape[0]
    steps = t_tok // tm

    n_in = [5 + ("mask" in rd) for rd in riders]
    first_in = [sum(n_in[:r]) for r in range(len(riders))]

    def body(me_ref, du_ref, dv_ref, dz_ref, dxbc_ref, ddt_ref, w_ref, x_ref, dx2_ref, g_ref, *rest):
        outs = rest[len(rest) - 2 - 4 * len(riders):]
        gx_ref, dg_ref = outs[:2]
        i = pl.program_id(0)
        dh = None
        for (a, b), ref in zip(_IN_SPLITS, (du_ref, dv_ref, dz_ref, dxbc_ref, ddt_ref)):
            part = _dot(ref[...], w_ref[a:b, :])
            dh = part if dh is None else dh + part
        dn, dg = _rms_bwd(x_ref[...], g_ref[...], dh)
        gx_ref[...] = dx2_ref[...] + dn
        _acc_rows(dg_ref, dg, i == 0)
        for r in range(len(riders)):
            p_ref, own_ref, w_ref_r, m_ref_r, v_ref_r = rest[first_in[r]:first_in[r] + 5]
            g = _sum_parts(me_ref[0], p_ref, own_ref[0])
            if n_in[r] == 6:
                g = g * rest[first_in[r] + 5][...]
            d, mn, vn = _adamw_math(w_ref_r[...], g, m_ref_r[...], v_ref_r[...])
            for o_ref, val in zip(outs[2 + 4 * r:6 + 4 * r], (g, d, mn, vn)):
                o_ref[...] = val

    row = lambda n: pl.BlockSpec((tm, n), lambda i, me_ref: (i, 0))
    whole = lambda shape: pl.BlockSpec(shape, lambda i, me_ref: (0,) * len(shape))
    widths = [b - a for a, b in _IN_SPLITS]
    deps = [] if dep is None else [dep]
    rider_args, rider_specs, rider_out_shapes, rider_out_specs = [], [], [], []
    for rd in riders:
        rows, cols = rd["w"].shape[0] // steps, rd["w"].shape[1]
        blk = pl.BlockSpec((rows, cols), lambda i, me_ref: (i, 0))
        rider_args += [rd["parts"], rd["own"], rd["w"], rd["m"], rd["v"]]
        rider_specs += [pl.BlockSpec((N_DEV, rows, cols), lambda i, me_ref: (0, i, 0)),
                        pl.BlockSpec((1, rows, cols), lambda i, me_ref: (me_ref[0], i, 0)), blk, blk, blk]
        if "mask" in rd:
            rider_args.append(rd["mask"])
            rider_specs.append(whole((rows, cols)))
        rider_out_shapes += [jax.ShapeDtypeStruct(rd["w"].shape, F32)] * 4
        rider_out_specs += [blk] * 4
    outs = pl.pallas_call(
        body, name="in_bwd",
        out_shape=(jax.ShapeDtypeStruct((t_tok, D_MODEL), F32), jax.ShapeDtypeStruct((1, D_MODEL), F32),
                   *rider_out_shapes),
        grid_spec=pltpu.PrefetchScalarGridSpec(
            num_scalar_prefetch=1, grid=(steps,),
            in_specs=[row(n) for n in widths] + [whole((IN_PAD, D_MODEL)), row(D_MODEL), row(D_MODEL),
                                                 whole((1, D_MODEL))] + rider_specs
            + [pl.BlockSpec(memory_space=pl.ANY)] * len(deps),
            out_specs=(row(D_MODEL), whole((1, D_MODEL)), *rider_out_specs)),
        compiler_params=_params("arbitrary"))(me, du, dv, dz, dxbc, ddt, w_in, x, dx2, g1, *rider_args, *deps)
    return outs[0], outs[1], [tuple(outs[2 + 4 * r:6 + 4 * r]) for r in range(len(riders))]


def _pad_lanes(a, n):
    return jnp.pad(a, ((0, 0), (0, n - a.shape[1])))


def _local_step(x, target, seq, small, hooks, first_dep=None):
    t_tok = x.shape[0]
    tm = min(TOKEN_TILE, t_tok)
    avg, expand, expand_t, tril, triu = _const_mats()
    g1, g2, g3, g4 = (small[k].reshape(1, D_MODEL) for k in
                      ("norm_mix_pre", "norm_mix_post", "norm_ffn_pre", "norm_ffn_post"))
    tie = (lambda a: a) if first_dep is None else (lambda a: a + first_dep[0, 0])
    lnw = tie(small["gm_ln_w"]).reshape(1, GM_WIDTH)
    lnb = tie(small["gm_ln_b"]).reshape(1, GM_WIDTH)
    causal = jnp.tril(jnp.ones((CHUNK, CHUNK), F32))
    wm = tie(small["gm_w_s"]) * causal
    pair = lambda w: w.reshape(4, 2, CHUNK, CHUNK).transpose(0, 2, 1, 3).reshape(4, CHUNK, 2 * CHUNK).astype(BF16)
    wcat = pair(wm)
    wtcat = pair(jnp.swapaxes(wm, 1, 2))
    bias = jnp.repeat(tie(small["gm_b_s"]).T, HEAD_DIM, axis=1)
    cb = small["conv_b"].reshape(1, CONV_CH)
    dtb = _pad_lanes(tie(small["dt_bias"]).reshape(1, N_HEADS), CHUNK)
    alog = _pad_lanes(tie(small["a_log"]).reshape(1, N_HEADS), CHUNK)
    dskip_exp = jnp.repeat(tie(small["d_skip"]).reshape(1, N_HEADS), HEAD_DIM, axis=1)
    nw = small["ssm_norm_w"].reshape(1, SSM_WIDTH)

    h1 = _prenorm(x, g1, tm, hooks.get("prenorm_after", first_dep))
    w_in_t, conv_w = hooks["mixer_weights"]((h1, lnw, lnb, wcat, wtcat, bias, dtb, alog, dskip_exp))
    tall = min(2 * tm, t_tok)
    u, v, z, xbc, dtr = _in_proj(h1, w_in_t, tall)
    mix_a = _gmlp_fwd(u, v, lnw, lnb, wcat, bias, avg)
    dep = hooks["gmlp_done"](mix_a) if "gmlp_done" in hooks else None
    mix_b, y_pre, states, pre = _ssd_fwd(z, xbc, dtr, conv_w, cb, dtb, alog, dskip_exp, nw, expand, tril, seq, dep)
    w_out, dep = hooks["mixers_done"](mix_b)
    o, x2, h3 = _out_proj(mix_a, mix_b, w_out, x, g2, g3, tall, dep)
    w_up, w_down = hooks["mlp_weights"](h3)
    tf = FF_TILE
    ra, dd, dy, dg4, loss = _mlp_fwd(h3, w_up, w_down, x2, target, g4, tm, tf)

    da, dx2, do, dg3, dg2 = _mlp_bwd(dd, w_down, ra, w_up, x2, dy, o, g3, g2, tm, tf)
    g_w_down = _wgrad(ra, dd, None, WGRAD_TILE, D_MODEL, t_tok, True, "wgrad_down")
    g_w_up = _wgrad(h3, da, N_DEV, D_MODEL, D_FF // N_DEV, t_tok, False, "wgrad_up")
    dep = hooks["mlp_grads"](g_w_down, g_w_up)
    dmix, g_w_out = _dmix_wgrad_out(do, w_out, mix_a, mix_b, min(2 * tall, t_tok), dep)
    du, dv, dws, dbt, dlnw, dlnb = _gmlp_bwd(dmix, u, v, lnw, lnb, wcat, wtcat, bias, avg, expand_t)
    dep = hooks["gmlp_grads"](g_w_out, dws)
    dz, dxbc, ddt, dcw, dcb, ddtb, dalog, ddsk, dnw = _ssd_bwd(
        dmix, z, xbc, pre, dtr, y_pre, states, conv_w, cb, dtb, alog, dskip_exp, nw, expand, expand_t, tril, triu, seq,
        dep)
    g_w_in = _wgrad_in_chunked(h1, (du, dv, dz, dxbc, ddt), WGRAD_TILE, t_tok // 2, dep)
    dep = hooks["in_grads"](g_w_in, dcw[0:4])
    riders = hooks["arrived_updates"](dep) if "arrived_updates" in hooks else []
    me = hooks.get("me", jnp.zeros((1,), jnp.int32))
    grad_x, dg1, updates = _in_bwd(du, dv, dz, dxbc, ddt, w_in_t, x, dx2, g1, tm, me, riders, dep)

    grads = dict(
        updates=updates,
        w_in=g_w_in, w_out=g_w_out, w_up=g_w_up, w_down=g_w_down, conv_w=dcw[0:4],
        norm_mix_pre=dg1, norm_mix_post=dg2, norm_ffn_pre=dg3, norm_ffn_post=dg4, gm_ln_w=dlnw, gm_ln_b=dlnb,
        gm_w_s=dws, gm_b_s=dbt, conv_b=dcb, dt_bias=ddtb, a_log=dalog, d_skip=ddsk, ssm_norm_w=dnw)
    return loss[0, 0], grad_x, grads


_WEIGHTS = ("norm_mix_pre", "w_in", "gm_ln_w", "gm_ln_b", "gm_w_s", "gm_b_s", "conv_w", "conv_b", "dt_bias", "a_log",
            "d_skip", "ssm_norm_w", "w_out", "norm_mix_post", "norm_ffn_pre", "w_up", "w_down", "norm_ffn_post")
_SLAB_ROWS = (("norm_mix_pre", 1024), ("norm_mix_post", 1024), ("norm_ffn_pre", 1024), ("norm_ffn_post", 1024),
              ("conv_b", 1024), ("ssm_norm_w", 512), ("gm_ln_w", 512), ("gm_ln_b", 512), ("dt_bias", 8), ("a_log", 8),
              ("d_skip", 8))
_SLAB_LOSS_ROW = len(_SLAB_ROWS)
_SLAB_BS_ROW = 16
_SMALL_PARAMS = tuple(name for name, _ in _SLAB_ROWS) + ("gm_b_s",)
_LN_PARAMS = ("gm_ln_w", "gm_ln_b")


_SLAB_CONV_ROW = _SLAB_LOSS_ROW + 1


def _pack_slab(g, loss_part):
    rows = [_pad_lanes(g[name], D_MODEL) for name, _ in _SLAB_ROWS]
    rows.append(jnp.broadcast_to(loss_part, (1, D_MODEL)))
    rows.append(g["conv_w"])
    assert sum(r.shape[0] for r in rows) == _SLAB_BS_ROW
    rows.append(_pad_lanes(g["gm_b_s"].T[0:N_HEADS], D_MODEL))
    return jnp.concatenate(rows, axis=0)


def _adamw_slab(parts, me, w, m, v):
    names = _SMALL_PARAMS + ("conv_w",)
    shapes = [w[k].shape for k in names]
    unfold = np.zeros((GM_WIDTH, HEAD_DIM), np.float32)
    for h in range(N_HEADS):
        unfold[h * HEAD_DIM:(h + 1) * HEAD_DIM, :] = np.eye(HEAD_DIM)
    unfold = jnp.asarray(unfold, dtype=BF16)
    n = len(names)
    shard = CONV_CH // N_DEV

    def body(me_ref, p_ref, unfold_ref, *refs):
        w_refs, m_refs, v_refs = refs[:n], refs[n:2 * n], refs[2 * n:3 * n]
        outs = refs[3 * n:]
        g_all = p_ref[0]
        for j in range(1, N_DEV):
            g_all = g_all + p_ref[j]
        lane = lax.broadcasted_iota(jnp.int32, (N_HEADS, GM_WIDTH), 1)
        head = lax.broadcasted_iota(jnp.int32, (N_HEADS, GM_WIDTH), 0)
        own_lanes = jnp.logical_and(lane >= head * HEAD_DIM, lane < (head + 1) * HEAD_DIM)
        mine = pl.ds(pl.multiple_of(me_ref[0] * shard, shard), shard)
        for i, name in enumerate(names):
            if name == "gm_b_s":
                g = g_all[_SLAB_BS_ROW:_SLAB_BS_ROW + N_HEADS, 0:CHUNK]
            elif name == "conv_w":
                g = p_ref[0, _SLAB_CONV_ROW:_SLAB_CONV_ROW + 4, mine]
                for j in range(1, N_DEV):
                    g = g + p_ref[j, _SLAB_CONV_ROW:_SLAB_CONV_ROW + 4, mine]
            else:
                row = [r for r, (k, _) in enumerate(_SLAB_ROWS) if k == name][0]
                g = g_all[row:row + 1, 0:dict(_SLAB_ROWS)[name]]
                if name in _LN_PARAMS:
                    g = _split_dot(jnp.where(own_lanes, g, 0.0), unfold_ref[...], 3)
            d, mn, vn = _adamw_math(w_refs[i][...], g, m_refs[i][...], v_refs[i][...])
            for o_ref, val in zip(outs[4 * i:4 * i + 4], (g, d, mn, vn)):
                o_ref[...] = val
        outs[-1][...] = g_all[_SLAB_LOSS_ROW:_SLAB_LOSS_ROW + 1, 0:128]

    def whole(shape):
        nd = len(shape)
        return pl.BlockSpec(shape, lambda i, me_ref: (0,) * nd)

    ins = [parts, unfold] + [d[k] for d in (w, m, v) for k in names]
    out_shape = tuple(jax.ShapeDtypeStruct(s, F32) for s in shapes for _ in range(4)) + (
        jax.ShapeDtypeStruct((1, 128), F32),)
    outs = pl.pallas_call(
        body, name="adamw_small", out_shape=out_shape,
        grid_spec=pltpu.PrefetchScalarGridSpec(
            num_scalar_prefetch=1, grid=(1,), in_specs=[whole(a.shape) for a in ins],
            out_specs=tuple(whole(s.shape) for s in out_shape)),
        compiler_params=_params("arbitrary"))(me, *ins)
    return {k: tuple(outs[4 * i:4 * i + 4]) for i, k in enumerate(names)}, outs[-1][0, 0]


def kernel(x, norm_mix_pre, w_in, gm_ln_w, gm_ln_b, gm_w_s, gm_b_s, conv_w, conv_b, dt_bias, a_log, d_skip, ssm_norm_w, w_out, norm_mix_post, norm_ffn_pre, w_up, w_down, norm_ffn_post, loss_target, m_norm_mix_pre, m_w_in, m_gm_ln_w, m_gm_ln_b, m_gm_w_s, m_gm_b_s, m_conv_w, m_conv_b, m_dt_bias, m_a_log, m_d_skip, m_ssm_norm_w, m_w_out, m_norm_mix_post, m_norm_ffn_pre, m_w_up, m_w_down, m_norm_ffn_post, v_norm_mix_pre, v_w_in, v_gm_ln_w, v_gm_ln_b, v_gm_w_s, v_gm_b_s, v_conv_w, v_conv_b, v_dt_bias, v_a_log, v_d_skip, v_ssm_norm_w, v_w_out, v_norm_mix_post, v_norm_ffn_pre, v_w_up, v_w_down, v_norm_ffn_post):
    w = dict(norm_mix_pre=norm_mix_pre, w_in=w_in, gm_ln_w=gm_ln_w, gm_ln_b=gm_ln_b, gm_w_s=gm_w_s, gm_b_s=gm_b_s, conv_w=conv_w, conv_b=conv_b, dt_bias=dt_bias, a_log=a_log, d_skip=d_skip, ssm_norm_w=ssm_norm_w, w_out=w_out, norm_mix_post=norm_mix_post, norm_ffn_pre=norm_ffn_pre, w_up=w_up, w_down=w_down, norm_ffn_post=norm_ffn_post)
    m = dict(norm_mix_pre=m_norm_mix_pre, w_in=m_w_in, gm_ln_w=m_gm_ln_w, gm_ln_b=m_gm_ln_b, gm_w_s=m_gm_w_s, gm_b_s=m_gm_b_s, conv_w=m_conv_w, conv_b=m_conv_b, dt_bias=m_dt_bias, a_log=m_a_log, d_skip=m_d_skip, ssm_norm_w=m_ssm_norm_w, w_out=m_w_out, norm_mix_post=m_norm_mix_post, norm_ffn_pre=m_norm_ffn_pre, w_up=m_w_up, w_down=m_w_down, norm_ffn_post=m_norm_ffn_post)
    v = dict(norm_mix_pre=v_norm_mix_pre, w_in=v_w_in, gm_ln_w=v_gm_ln_w, gm_ln_b=v_gm_ln_b, gm_w_s=v_gm_w_s, gm_b_s=v_gm_b_s, conv_w=v_conv_w, conv_b=v_conv_b, dt_bias=v_dt_bias, a_log=v_a_log, d_skip=v_d_skip, ssm_norm_w=v_ssm_norm_w, w_out=v_w_out, norm_mix_post=v_norm_mix_post, norm_ffn_pre=v_norm_ffn_pre, w_up=v_w_up, w_down=v_w_down, norm_ffn_post=v_norm_ffn_post)
    n_batch, seq, _ = x.shape
    shard_in = IN_COLS // N_DEV

    me = (4 * lax.axis_index("x") + 2 * lax.axis_index("y") + lax.axis_index("c")).astype(jnp.int32).reshape(1)

    def in_slot(own):
        return lax.dynamic_update_slice(lax.empty((N_DEV,) + own.shape, own.dtype), own[None],
                                        (me[0],) + (0,) * own.ndim)

    lying = lambda t: jnp.transpose(t, (2, 0, 1))
    first = [_cast_to_slot(lying(w_in), me, shard_in, "cast_w_in"), in_slot(conv_w[0])]
    ici_1, tok_ici_1 = _exchange_start(first, [True] * 2, _SAME_CORE_PEERS, "gather_mix_ici_start")
    cast_out = _cast_to_slot(w_out[0], me, 128, "cast_w_out", dep=tok_ici_1)
    cast_up = _cast_to_slot(w_up[0], me, 1024, "cast_w_up", cols=True, dep=cast_out)
    second = [cast_out, cast_up, _cast_to_slot(w_down[0], me, 512, "cast_w_down", dep=cast_up)]
    gathering = {}

    def mixer_weights(after):
        bufs = [buf for buf, _ in _exchange_wait(ici_1, after, "gather_mix_ici_wait")]
        d2d_1, tok_d2d_1 = _exchange_start(bufs, [True] * 2, _SIBLING_FORWARD, "gather_mix_d2d_start")
        gathering["late_ici"], tok_ici_2 = _exchange_start(
            second, [True] * 3, _SAME_CORE_PEERS, "gather_late_ici_start", dep=tok_d2d_1)
        (_, ag_in), (_, ag_conv) = _exchange_wait(d2d_1, tok_ici_2, "gather_mix_d2d_wait")
        w_in_t = _stack_shards(ag_in, IN_PAD, STACK_TILE, "stack_w_in")
        return w_in_t, ag_conv.transpose(1, 0, 2).reshape(4, CONV_CH)

    def gmlp_done(after):
        ((buf, _),) = _exchange_wait(gathering["late_ici"], after, "gather_out_ici_wait", only=(0,))
        gathering["out"], tok = _exchange_start([buf], [True], _SIBLING_FORWARD, "gather_out_d2d_start")
        return tok

    def mixers_done(after):
        bufs = [buf for buf, _ in _exchange_wait(gathering["late_ici"], after, "gather_mlp_ici_wait", only=(1, 2))]
        gathering["mlp"], tok = _exchange_start(bufs, [True] * 2, _SIBLING_FORWARD, "gather_mlp_d2d_start")
        ((_, ag_out),) = _exchange_wait(gathering["out"], tok, "gather_out_d2d_wait")
        return ag_out.reshape(D_MODEL, D_MODEL), tok

    def mlp_weights(after):
        (_, ag_up), (_, ag_down) = _exchange_wait(gathering["mlp"], after, "gather_mlp_d2d_wait")
        return ag_up, ag_down.reshape(D_FF, D_MODEL)

    sent = {}

    def mlp_grads(g_w_down, g_w_up):
        sent["mlp"], tok = _exchange_start(
            [g_w_down.reshape(N_DEV, D_FF // N_DEV, D_MODEL), g_w_up], [False, False], _ALL_PEERS, "grads_mlp_start")
        return tok

    def gmlp_grads(g_w_out, g_w_s):
        sent["gmlp"], tok = _exchange_start(
            [g_w_out.reshape(N_DEV, D_MODEL // N_DEV, D_MODEL), in_slot(g_w_s.astype(BF16))], [False, True], _ALL_PEERS,
            "grads_gmlp_start")
        return tok

    def in_grads(g_w_in_t, g_conv_w):
        sent["in"], tok = _exchange_start([g_w_in_t], [False], _ALL_PEERS, "grads_in_start")
        return tok

    def arrived_updates(after):
        (own_down, p_down), (own_up, p_up) = _exchange_wait(sent["mlp"], after, "grads_mlp_wait")
        (own_out, p_out), (_, p_ws) = _exchange_wait(sent["gmlp"], own_up, "grads_gmlp_wait")
        rows = lambda t: t.reshape(t.shape[:-3] + (N_HEADS * CHUNK, CHUNK))
        return [dict(parts=p_up, own=own_up, w=w_up[0], m=m_w_up[0], v=v_w_up[0]),
                dict(parts=p_down, own=own_down, w=w_down[0], m=m_w_down[0], v=v_w_down[0]),
                dict(parts=p_out, own=own_out, w=w_out[0], m=m_w_out[0], v=v_w_out[0]),
                dict(parts=rows(p_ws), own=rows(p_ws), w=rows(gm_w_s[0]), m=rows(m_gm_w_s[0]), v=rows(v_gm_w_s[0]),
                     mask=jnp.tril(jnp.ones((CHUNK, CHUNK), F32)))]

    small = {k: w[k][0] for k in _SMALL_PARAMS + ("gm_w_s",)}
    loss_part, grad_x, g = _local_step(
        x.reshape(n_batch * seq, D_MODEL), loss_target.reshape(n_batch * seq, D_MODEL), seq, small,
        dict(mixer_weights=mixer_weights, gmlp_done=gmlp_done, mixers_done=mixers_done, mlp_weights=mlp_weights,
             mlp_grads=mlp_grads, gmlp_grads=gmlp_grads, in_grads=in_grads, arrived_updates=arrived_updates, me=me,
             prenorm_after=second[2]), first_dep=tok_ici_1)

    sent_rows, tok_rows = _exchange_start([in_slot(_pack_slab(g, loss_part))], [True], _ALL_PEERS, "grads_rows_start")
    res = dict(zip(("w_up", "w_down", "w_out", "gm_w_s"), g["updates"]))
    ((own_in, p_in),) = _exchange_wait(sent["in"], tok_rows, "grads_in_wait")
    upd_in = _adamw_reduce(p_in, own_in, me, lying(w_in), lying(m_w_in), lying(v_w_in), "adamw_w_in")
    res["w_in"] = tuple(jnp.transpose(t, (1, 2, 0)) for t in upd_in)
    ((_, p_rows),) = _exchange_wait(sent_rows, upd_in[1], "grads_rows_wait")
    flat = lambda t: t[0] if t.ndim == 3 else t
    small_res, loss = _adamw_slab(
        p_rows, me, *({k: flat(d[k]) for k in _SMALL_PARAMS + ("conv_w",)} for d in (w, m, v)))
    res.update(small_res)
    res = {k: tuple(r.reshape(w[k].shape) for r in res[k]) for k in _WEIGHTS}

    outs = [loss, grad_x.reshape(x.shape)]
    for part in range(4):
        outs.extend(res[k][part] for k in _WEIGHTS)
    return tuple(outs)
```

```python
import functools

import jax
import jax.numpy as jnp
import numpy as np
from jax import lax
from jax.experimental import pallas as pl
from jax.experimental.pallas import tpu as pltpu

F32 = jnp.float32
BF16 = jnp.bfloat16

D_MODEL = 1024
GM_WIDTH = 512
SSM_WIDTH = 512
CONV_CH = 1024
N_HEADS = 8
HEAD_DIM = 64
N_STATE = 128
CHUNK = 128
D_FF = 4096
IN_COLS = 2568
IN_PAD = 2688
N_DEV = 8
EPS = 1e-6
ADAM_LR, ADAM_B1, ADAM_B2, ADAM_EPS, ADAM_WD, ADAM_STEP = 0.001, 0.9, 0.999, 1e-08, 0.01, 10
VMEM_LIMIT_BYTES = 56 * 1024 * 1024
TOKEN_TILE = 512
FF_TILE = 2048
WGRAD_TILE = 512
STACK_TILE = 256
_NT = (((1,), (1,)), ((), ()))
_TN = (((0,), (0,)), ((), ()))


def _params(*sem):
    return pltpu.CompilerParams(dimension_semantics=sem or None, vmem_limit_bytes=VMEM_LIMIT_BYTES)


def _dot(a, b, dims=None):
    if dims is None:
        return jnp.dot(a, b, preferred_element_type=F32)
    return lax.dot_general(a, b, dims, preferred_element_type=F32)


def _split_terms(x, terms):
    out, rem = [], x
    for i in range(terms):
        hi = rem.astype(BF16)
        out.append(hi)
        if i + 1 < terms:
            rem = rem - hi.astype(F32)
    return out


def _split_dot(x, m, terms):
    acc = None
    for hi in _split_terms(x, terms):
        part = _dot(hi, m)
        acc = part if acc is None else acc + part
    return acc


def _split_dot_left(m, x, terms):
    acc = None
    for hi in _split_terms(x, terms):
        part = _dot(m, hi)
        acc = part if acc is None else acc + part
    return acc


def _gelu_and_grad(x):
    c = 0.7978845608028654
    inner = c * (x + 0.044715 * x * x * x)
    t = jnp.tanh(inner)
    g = 0.5 * x * (1.0 + t)
    dg = 0.5 * (1.0 + t) + 0.5 * x * (1.0 - t * t) * c * (1.0 + 3.0 * 0.044715 * x * x)
    return g, dg


def _softplus(x):
    return jnp.maximum(x, 0.0) + jnp.log(1.0 + jnp.exp(-jnp.abs(x)))


def _rsum(x):
    return jnp.sum(x, axis=0, keepdims=True)


def _acc_rows(ref, part, first):
    val = jnp.broadcast_to(part, ref.shape)

    @pl.when(first)
    def _():
        ref[...] = val

    @pl.when(jnp.logical_not(first))
    def _():
        ref[...] += val


def _rms_bwd(n, g, dout):
    r = lax.rsqrt(jnp.mean(n * n, axis=-1, keepdims=True) + EPS)
    nh = n * r
    dg = dout * g
    dn = r * (dg - nh * jnp.mean(dg * nh, axis=-1, keepdims=True))
    return dn, _rsum(dout * nh)


def _const_mats():
    avg = np.kron(np.eye(4), np.full((HEAD_DIM, HEAD_DIM), 1.0 / HEAD_DIM))
    expand = np.zeros((CHUNK, SSM_WIDTH), np.float32)
    for h in range(N_HEADS):
        expand[h, h * HEAD_DIM:(h + 1) * HEAD_DIM] = 1.0
    tril = np.tril(np.ones((CHUNK, CHUNK), np.float32))
    as_bf16 = lambda a: jnp.asarray(a, dtype=BF16)
    return as_bf16(avg), as_bf16(expand), as_bf16(expand.T), as_bf16(tril), as_bf16(tril.T)


def _full(shape):
    nd = len(shape)
    return pl.BlockSpec(shape, lambda *_: (0,) * nd)


_HBM = pl.BlockSpec(memory_space=pltpu.HBM)
_SEM = pl.BlockSpec(memory_space=pltpu.SEMAPHORE)
_ALL_PEERS = tuple((k, 0) for k in range(1, N_DEV))
_SAME_CORE_PEERS = ((2, 0), (4, 0), (6, 0))
_SIBLING_FORWARD = ((1, 0), (1, 2), (1, 4), (1, 6))


def _flip(j, k):
    for bit in (4, 2, 1):
        if k & bit:
            j = j + bit - 2 * (j & bit)
    return j


def _copies(src, land, send_sems, recv_sems, hops, slots=None):
    x, y, c = lax.axis_index("x"), lax.axis_index("y"), lax.axis_index("c")
    me = 4 * x + 2 * y + c
    slots = range(len(src)) if slots is None else slots
    out = []
    for t in range(len(src)):
        for i, (k, b) in enumerate(hops):
            pos = (1 - x if k & 4 else x, 1 - y if k & 2 else y, 1 - c if k & 1 else c)
            peer = _flip(me, k)
            sem = slots[t] * len(hops) + i
            mk = functools.partial(pltpu.make_async_remote_copy, send_sem=send_sems.at[sem], recv_sem=recv_sems.at[sem],
                                   device_id=pos, device_id_type=pl.DeviceIdType.MESH)
            if land[t] is None and src[t].shape[0] != N_DEV:
                width = src[t].shape[1] // N_DEV
                slab = lambda j: src[t].at[:, pl.ds(pl.multiple_of(j * width, 128), width)]
                mine = functools.partial(mk, src_ref=slab(_flip(me, b)), dst_ref=slab(_flip(me, b)))
                theirs = functools.partial(mk, src_ref=slab(_flip(peer, b)), dst_ref=slab(_flip(peer, b)))
            elif land[t] is None:
                mine = functools.partial(mk, src_ref=src[t].at[_flip(me, b)], dst_ref=src[t].at[_flip(me, b)])
                theirs = functools.partial(mk, src_ref=src[t].at[_flip(peer, b)], dst_ref=src[t].at[_flip(peer, b)])
            else:
                assert b == 0
                mine = functools.partial(mk, src_ref=src[t].at[peer], dst_ref=land[t].at[me])
                theirs = functools.partial(mk, src_ref=src[t].at[peer], dst_ref=land[t].at[peer])
            out.append((mine, theirs))
    return out


def _exchange_start(srcs, inplace, peers, name, dep=None):
    n = len(srcs)
    lands = [None if ip else pltpu.with_memory_space_constraint(lax.empty(s.shape, s.dtype), pltpu.HBM)
             for s, ip in zip(srcs, inplace)]
    real_lands = [l for l in lands if l is not None]
    n_l = len(real_lands)
    deps = [] if dep is None else [dep]

    def body(*refs):
        src = refs[:n]
        land_refs = list(refs[n:n + n_l])
        send_sems, recv_sems = refs[n + n_l + len(deps)], refs[n + n_l + len(deps) + 1]
        token = refs[-1]
        land = [None if ip else land_refs.pop(0) for ip in inplace]
        for mine, _ in _copies(src, land, send_sems, recv_sems, peers):
            mine().start()
        token[...] = jnp.zeros_like(token)

    sem_t = pltpu.SemaphoreType.DMA((n * len(peers),))
    outs = pl.pallas_call(
        body, name=name,
        out_shape=(sem_t, sem_t) + tuple(pltpu.HBM(a.shape, a.dtype) for a in list(srcs) + real_lands)
        + (jax.ShapeDtypeStruct((8, 128), F32),),
        in_specs=[_HBM] * (n + n_l) + [pl.BlockSpec(memory_space=pl.ANY)] * len(deps),
        out_specs=(_SEM, _SEM) + (_HBM,) * (n + n_l) + (pl.BlockSpec(memory_space=pltpu.VMEM),),
        input_output_aliases={i: 2 + i for i in range(n + n_l)},
        compiler_params=pltpu.CompilerParams(has_side_effects=pltpu.SideEffectType.DATAFLOW_SIDE_EFFECTING),
    )(*[pltpu.with_memory_space_constraint(s, pltpu.HBM) for s in srcs], *real_lands, *deps)
    handle = dict(send=outs[0], recv=outs[1], srcs=outs[2:2 + n], lands=outs[2 + n:2 + n + n_l], inplace=inplace,
                  peers=peers)
    return handle, outs[-1]


def _exchange_wait(handle, after, name, only=None):
    srcs, lands, inplace, peers = handle["srcs"], handle["lands"], handle["inplace"], handle["peers"]
    slots = None
    if only is not None:
        assert all(inplace)
        slots, srcs, inplace = list(only), [srcs[t] for t in only], [True] * len(only)
    n, n_l = len(srcs), len(lands)
    after = after if isinstance(after, tuple) else (after,)

    def body(*refs):
        src = refs[:n]
        land_refs = list(refs[n:n + n_l])
        send_sems, recv_sems = refs[n + n_l], refs[n + n_l + 1]
        land = [None if ip else land_refs.pop(0) for ip in inplace]
        for mine, theirs in _copies(src, land, send_sems, recv_sems, peers, slots):
            mine().wait_send()
            theirs().wait_recv()

    outs = pl.pallas_call(
        body, name=name, out_shape=tuple(pltpu.HBM(a.shape, a.dtype) for a in list(srcs) + list(lands)),
        in_specs=[_HBM] * (n + n_l) + [_SEM, _SEM] + [pl.BlockSpec(memory_space=pl.ANY)] * len(after),
        out_specs=(_HBM,) * (n + n_l), input_output_aliases={i: i for i in range(n + n_l)},
        compiler_params=pltpu.CompilerParams(has_side_effects=pltpu.SideEffectType.DATAFLOW_SIDE_EFFECTING),
    )(*srcs, *lands, handle["send"], handle["recv"], *after)
    res, land_out = [], list(outs[n:])
    for t in range(n):
        res.append((outs[t], outs[t] if inplace[t] else land_out.pop(0)))
    return res


def _cast_to_slot(w, me, rows, name, cols=False, dep=None):
    r, cdim = w.shape[0], w.shape[-1]
    deps = [] if dep is None else [dep]

    def body(me_ref, w_ref, *rest):
        o_ref = rest[-1]
        if cols:
            o_ref[...] = w_ref[...].astype(BF16)
        else:
            o_ref[0] = w_ref[...].reshape(rows, cdim).astype(BF16)

    if cols:
        out_shape = jax.ShapeDtypeStruct((r, N_DEV * cdim), BF16)
        out_spec = pl.BlockSpec((rows, cdim), lambda i, me_ref: (i, me_ref[0]))
    else:
        out_shape = jax.ShapeDtypeStruct((N_DEV, r, cdim), BF16)
        out_spec = pl.BlockSpec((1, rows, cdim), lambda i, me_ref: (me_ref[0], i, 0))
    return pl.pallas_call(
        body, name=name, out_shape=out_shape,
        grid_spec=pltpu.PrefetchScalarGridSpec(
            num_scalar_prefetch=1, grid=(r // rows,),
            in_specs=[pl.BlockSpec((rows, cdim), lambda i, me_ref: (i, 0)) if w.ndim == 2 else
                      pl.BlockSpec((rows, 1, cdim), lambda i, me_ref: (i, 0, 0))]
            + [pl.BlockSpec(memory_space=pl.ANY)] * len(deps), out_specs=out_spec),
        compiler_params=_params("parallel"))(me, w, *deps)


def _stack_shards(blocks, rows, bn, name):
    n, r, cdim = blocks.shape

    def body(b_ref, o_ref, acc_ref):
        acc_ref[n * r:, :] = jnp.zeros((rows - n * r, bn), F32)
        for j in range(n):
            acc_ref[r * j:r * (j + 1), :] = b_ref[j].astype(F32)
        o_ref[...] = acc_ref[...].astype(BF16)

    return pl.pallas_call(
        body, name=name, grid=(cdim // bn,), out_shape=jax.ShapeDtypeStruct((rows, cdim), BF16),
        in_specs=[pl.BlockSpec((n, r, bn), lambda i: (0, 0, i))], out_specs=pl.BlockSpec((rows, bn), lambda i: (0, i)),
        scratch_shapes=[pltpu.VMEM((rows, bn), F32)], compiler_params=_params("parallel"))(blocks)


def _adamw_math(w, g, m, v):
    m = ADAM_B1 * m + (1.0 - ADAM_B1) * g
    v = ADAM_B2 * v + (1.0 - ADAM_B2) * (g * g)
    m_hat = m / (1.0 - ADAM_B1 ** ADAM_STEP)
    v_hat = v / (1.0 - ADAM_B2 ** ADAM_STEP)
    delta = -ADAM_LR * (m_hat / (jnp.sqrt(v_hat) + ADAM_EPS) + ADAM_WD * w)
    return delta, m, v


def _sum_parts(me, p_ref, own):
    g = None
    for j in range(N_DEV):
        term = (p_ref[j] if own is None else jnp.where(me == j, own, p_ref[j])).astype(F32)
        g = term if g is None else g + term
    return g


def _adamw_reduce(parts, own, me, w, m, v, name):
    r, _, cdim = w.shape

    def body(me_ref, p_ref, own_ref, w_ref, m_ref, v_ref, g_out, d_out, m_out, v_out):
        g = _sum_parts(me_ref[0], p_ref, own_ref[0]).reshape(r, 1, cdim)
        d, mn, vn = _adamw_math(w_ref[...], g, m_ref[...], v_ref[...])
        g_out[...] = g
        d_out[...] = d
        m_out[...] = mn
        v_out[...] = vn

    blk = pl.BlockSpec((r, 1, cdim), lambda i, me_ref: (0, 0, 0))
    return pl.pallas_call(
        body, name=name, out_shape=(jax.ShapeDtypeStruct(w.shape, F32),) * 4,
        grid_spec=pltpu.PrefetchScalarGridSpec(
            num_scalar_prefetch=1, grid=(1,),
            in_specs=[pl.BlockSpec((N_DEV, r, cdim), lambda i, me_ref: (0, 0, 0)),
                      pl.BlockSpec((1, r, cdim), lambda i, me_ref: (me_ref[0], 0, 0)), blk, blk, blk],
            out_specs=(blk,) * 4),
        compiler_params=_params("arbitrary"))(me, parts, own, w, m, v)


_IN_SPLITS = ((0, 512), (512, 1024), (1024, 1536), (1536, 2560), (2560, IN_PAD))


def _prenorm(x, g1, tm, dep=None):
    t_tok = x.shape[0]
    deps = [] if dep is None else [dep]

    def body(x_ref, g_ref, *rest):
        xv = x_ref[...]
        r = lax.rsqrt(jnp.mean(xv * xv, axis=-1, keepdims=True) + EPS)
        rest[-1][...] = (xv * r * g_ref[...]).astype(BF16)

    row = pl.BlockSpec((tm, D_MODEL), lambda i: (i, 0))
    return pl.pallas_call(
        body, name="prenorm", grid=(t_tok // tm,), out_shape=jax.ShapeDtypeStruct((t_tok, D_MODEL), BF16),
        in_specs=[row, _full((1, D_MODEL))] + [pl.BlockSpec(memory_space=pl.ANY)] * len(deps), out_specs=row,
        compiler_params=_params("parallel"))(x, g1, *deps)


def _in_proj(h1, w_in, tm):
    t_tok = h1.shape[0]

    def body(h_ref, w_ref, *outs):
        h = h_ref[...]
        for (a, b), o_ref in zip(_IN_SPLITS, outs):
            o_ref[...] = _dot(h, w_ref[a:b, :], _NT).astype(o_ref.dtype)

    row = lambda n: pl.BlockSpec((tm, n), lambda i: (i, 0))
    widths = [b - a for a, b in _IN_SPLITS]
    dtypes = (BF16, BF16, BF16, F32, F32)
    return pl.pallas_call(
        body, name="in_proj", grid=(t_tok // tm,),
        out_shape=tuple(jax.ShapeDtypeStruct((t_tok, n), dt) for n, dt in zip(widths, dtypes)),
        in_specs=[row(D_MODEL), _full((IN_PAD, D_MODEL))], out_specs=tuple(row(n) for n in widths),
        compiler_params=_params("parallel"))(h1, w_in)


def _lane_masks():
    lane = lax.broadcasted_iota(jnp.int32, (1, 2 * HEAD_DIM), 1)
    left = (lane < HEAD_DIM).astype(F32)
    return left, 1.0 - left


def _stack_pair(v, m_l, m_r):
    return jnp.concatenate([v * m_l, v * m_r], axis=0).astype(BF16)


def _head_mean(x, avg):
    n = avg.shape[0]
    return jnp.concatenate([_split_dot(x[:, n * i:n * (i + 1)], avg, 2) for i in range(x.shape[1] // n)], axis=1)


def _gmlp_common(u, v, lnw, lnb, avg, wcat_ref, bias, m_l, m_r):
    ug, dug = _gelu_and_grad(u)
    vg, dvg = _gelu_and_grad(v)
    mu = _head_mean(vg, avg)
    vc = vg - mu
    var = _head_mean(vc * vc, avg)
    rstd = lax.rsqrt(var + EPS)
    vhat = vc * rstd
    vn = vhat * lnw + lnb
    rows = []
    for r in range(u.shape[0] // CHUNK):
        cols = []
        for j in range(N_HEADS // 2):
            pair = vn[CHUNK * r:CHUNK * (r + 1), 128 * j:128 * (j + 1)]
            cols.append(_dot(wcat_ref[j], _stack_pair(pair, m_l, m_r)))
        rows.append(jnp.concatenate(cols, axis=1) + bias)
    mixed = jnp.concatenate(rows, axis=0)
    return ug, dug, dvg, rstd, vhat, vn, mixed


_GMLP_ROWS = 4 * CHUNK


def _gmlp_fwd(u, v, lnw, lnb, wcat, bias, avg):
    t_tok = u.shape[0]
    tm = min(_GMLP_ROWS, t_tok)

    def body(u_ref, v_ref, lnw_ref, lnb_ref, wcat_ref, bias_ref, avg_ref, o_ref):
        m_l, m_r = _lane_masks()
        ug, _, _, _, _, _, mixed = _gmlp_common(
            u_ref[...].astype(F32), v_ref[...].astype(F32), lnw_ref[...], lnb_ref[...], avg_ref[...], wcat_ref,
            bias_ref[...], m_l, m_r)
        o_ref[...] = (ug * mixed).astype(BF16)

    row = pl.BlockSpec((tm, GM_WIDTH), lambda i: (i, 0))
    return pl.pallas_call(
        body, name="gmlp_fwd", grid=(t_tok // tm,), out_shape=jax.ShapeDtypeStruct((t_tok, GM_WIDTH), BF16),
        in_specs=[row, row, _full((1, GM_WIDTH)), _full((1, GM_WIDTH)), _full(wcat.shape), _full(bias.shape),
                  _full(avg.shape)],
        out_specs=row, compiler_params=_params("parallel"))(u, v, lnw, lnb, wcat, bias, avg)


def _shift_rows(x, edge, j, down):
    groups, cols = x.shape[0] // 8, x.shape[1]
    amount = j if down else 8 - j
    rot = pltpu.roll(x.reshape(groups, 8, cols), amount, axis=1)
    edge_rot = pltpu.roll(edge, amount, axis=0)[None]
    sub = lax.broadcasted_iota(jnp.int32, (1, 8, 1), 1)
    if down:
        out = jnp.where(sub < j, jnp.concatenate([edge_rot, rot[:-1]], axis=0), rot)
    else:
        out = jnp.where(sub < 8 - j, rot, jnp.concatenate([rot[1:], edge_rot], axis=0))
    return out.reshape(x.shape)


def _conv_pre(xbc, tail, cw_ref, cb):
    taps = [_shift_rows(xbc, tail, 3 - k, True) for k in range(3)] + [xbc]
    return cb + cw_ref[0:1, :] * taps[0] + cw_ref[1:2, :] * taps[1] + cw_ref[2:3, :] * taps[2] + cw_ref[3:4, :] * taps[3]


def _ssd_common(pre, dtr, dtb, alog, expand, tril):
    q = CHUNK
    sg = jax.nn.sigmoid(pre)
    act = pre * sg
    lane = lax.broadcasted_iota(jnp.int32, (1, CHUNK), 1)
    a_row = jnp.where(lane < N_HEADS, -jnp.exp(alog), 0.0)
    dtp = dtr + dtb
    dt = _softplus(dtp)
    a_cs = _split_dot_left(tril, dt * a_row, 3)
    a_cs_t = a_cs.T
    dt_exp = _split_dot(dt, expand, 3)
    a_exp = _split_dot(a_cs, expand, 3)
    a_end = a_exp[q - 1:q, :]
    li = lax.broadcasted_iota(jnp.int32, (q, q), 0)
    si = lax.broadcasted_iota(jnp.int32, (q, q), 1)
    causal = si <= li
    decay = []
    for h in range(N_HEADS):
        seg = a_cs[:, h:h + 1] - a_cs_t[h:h + 1, :]
        decay.append(jnp.where(causal, jnp.exp(jnp.minimum(seg, 0.0)), 0.0))
    return dict(pre=pre, sg=sg, act=act, a_row=a_row, dtp=dtp, dt=dt, dt_exp=dt_exp, a_exp=a_exp,
                e=jnp.exp(a_exp), w_end=jnp.exp(a_end - a_exp), cd=jnp.exp(a_end), decay=decay)


def _ssd_specs(t_tok, seq, reverse):
    nb, nc = t_tok // seq, seq // CHUNK

    def chunk(c):
        return nc - 1 - c if reverse else c

    def row(n, col=0):
        return pl.BlockSpec((nb, CHUNK, n), lambda c: (0, chunk(c), col))

    tail = pl.BlockSpec((nb, 8, CONV_CH), lambda c: (0, jnp.maximum(chunk(c) * (CHUNK // 8) - 1, 0), 0))
    states = pl.BlockSpec((nb, 1, N_STATE, SSM_WIDTH), lambda c: (0, chunk(c), 0, 0))
    fold = lambda a: a.reshape(nb, seq, a.shape[-1])
    unfold = lambda a: a.reshape(t_tok, a.shape[-1])
    return nb, nc, row, tail, states, fold, unfold


def _ssd_fwd(z, xbc, dtr, cw, cb, dtb, alog, dskip_exp, nw, expand, tril, seq, dep=None):
    t_tok = z.shape[0]
    nb, nc, row, tail, states_spec, fold, unfold = _ssd_specs(t_tok, seq, False)

    def body(z_ref, xbc_ref, tail_ref, dtr_ref, cw_ref, cb_ref, dtb_ref, alog_ref, dsk_ref, nw_ref, exp_ref,
             tril_ref, o_ref, y_ref, st_ref, pre_ref, state_ref):
        c = pl.program_id(0)

        @pl.when(c == 0)
        def _():
            state_ref[...] = jnp.zeros_like(state_ref)

        m_l, m_r = _lane_masks()
        for s in range(nb):
            pre = _conv_pre(xbc_ref[s], jnp.where(c == 0, 0.0, tail_ref[s]), cw_ref, cb_ref[...])
            pre_ref[s] = pre
            f = _ssd_common(pre, dtr_ref[s], dtb_ref[...], alog_ref[...], exp_ref[...], tril_ref[...])
            act = f["act"]
            xs = act[:, :SSM_WIDTH]
            xdt = xs * f["dt_exp"]
            xw = xdt * f["w_end"]
            state = state_ref[s]
            st_ref[s, 0] = state
            ydiag, yoff, snew = [], [], []
            for g in range(2):
                bg = act[:, 512 + 128 * g:640 + 128 * g].astype(BF16)
                cg = act[:, 768 + 128 * g:896 + 128 * g].astype(BF16)
                cb_mat = _dot(cg, bg, _NT)
                for pr in range(2):
                    h0 = 4 * g + 2 * pr
                    gcat = jnp.concatenate(
                        [(cb_mat * f["decay"][h0]).astype(BF16), (cb_mat * f["decay"][h0 + 1]).astype(BF16)], axis=1)
                    ydiag.append(_dot(gcat, _stack_pair(xdt[:, 64 * h0:64 * h0 + 128], m_l, m_r)))
                yoff.append(_dot(cg, state[:, 256 * g:256 * (g + 1)].astype(BF16)))
                snew.append(_dot(bg, xw[:, 256 * g:256 * (g + 1)].astype(BF16), _TN))
            y = jnp.concatenate(ydiag, axis=1) + f["e"] * jnp.concatenate(yoff, axis=1) + dsk_ref[...] * xs
            state_ref[s] = state * f["cd"] + jnp.concatenate(snew, axis=1)
            y_ref[s] = y
            zv = z_ref[s].astype(F32)
            yg = y * (zv * jax.nn.sigmoid(zv))
            outs = []
            for g in range(2):
                ygg = yg[:, 256 * g:256 * (g + 1)]
                outs.append(ygg * lax.rsqrt(jnp.mean(ygg * ygg, axis=-1, keepdims=True) + EPS))
            o_ref[s] = (jnp.concatenate(outs, axis=1) * nw_ref[...]).astype(BF16)

    consts = [cw, cb, dtb, alog, dskip_exp, nw, expand, tril]
    deps = [] if dep is None else [dep]
    n_in = 4 + len(consts)

    def body_skipping_dep(*refs):
        body(*refs[:n_in], *refs[n_in + len(deps):])

    sd = lambda n, dt: jax.ShapeDtypeStruct((nb, seq, n), dt)
    o, y, states, pre = pl.pallas_call(
        body_skipping_dep, name="ssd_fwd", grid=(nc,),
        out_shape=(sd(SSM_WIDTH, BF16), sd(SSM_WIDTH, F32), jax.ShapeDtypeStruct((nb, nc, N_STATE, SSM_WIDTH), F32),
                   sd(CONV_CH, F32)),
        in_specs=[row(SSM_WIDTH), row(CONV_CH), tail, row(CHUNK)] + [_full(a.shape) for a in consts]
        + [pl.BlockSpec(memory_space=pl.ANY)] * len(deps),
        out_specs=(row(SSM_WIDTH), row(SSM_WIDTH), states_spec, row(CONV_CH)),
        scratch_shapes=[pltpu.VMEM((nb, N_STATE, SSM_WIDTH), F32)],
        compiler_params=_params("arbitrary"))(fold(z), fold(xbc), fold(xbc), fold(dtr), *consts, *deps)
    return unfold(o), unfold(y), states, unfold(pre)


def _out_proj(mix_a, mix_b, w_out, x, g2, g3, tm, dep=None):
    t_tok = x.shape[0]
    deps = [] if dep is None else [dep]

    def body(a_ref, b_ref, w_ref, x_ref, g2_ref, g3_ref, *rest):
        o_ref, x2_ref, h3_ref = rest[-3:]
        o = _dot(a_ref[...], w_ref[0:GM_WIDTH, :]) + _dot(b_ref[...], w_ref[GM_WIDTH:, :])
        o_ref[...] = o
        r2 = lax.rsqrt(jnp.mean(o * o, axis=-1, keepdims=True) + EPS)
        x2 = x_ref[...] + o * r2 * g2_ref[...]
        x2_ref[...] = x2
        r3 = lax.rsqrt(jnp.mean(x2 * x2, axis=-1, keepdims=True) + EPS)
        h3_ref[...] = (x2 * r3 * g3_ref[...]).astype(BF16)

    row = lambda n: pl.BlockSpec((tm, n), lambda i: (i, 0))
    sd = lambda dt: jax.ShapeDtypeStruct((t_tok, D_MODEL), dt)
    return pl.pallas_call(
        body, name="out_proj", grid=(t_tok // tm,), out_shape=(sd(F32), sd(F32), sd(BF16)),
        in_specs=[row(GM_WIDTH), row(SSM_WIDTH), _full((D_MODEL, D_MODEL)), row(D_MODEL), _full((1, D_MODEL)),
                  _full((1, D_MODEL))] + [pl.BlockSpec(memory_space=pl.ANY)] * len(deps),
        out_specs=(row(D_MODEL),) * 3, compiler_params=_params("parallel"))(mix_a, mix_b, w_out, x, g2, g3, *deps)


def _mlp_fwd(h3, w_up, w_down, x2, target, g4, tm, tf):
    t_tok = x2.shape[0]

    def body(h_ref, wu_ref, wd_hbm, x2_ref, t_ref, g4_ref, ra_ref, dd_ref, dy_ref, dg4_ref, loss_ref, wd_ref, sem):
        i = pl.program_id(0)
        w_down_copy = pltpu.make_async_copy(wd_hbm, wd_ref, sem.at[0])

        @pl.when(i == 0)
        def _():
            w_down_copy.start()

        hv = h_ref[...]
        for j in range(D_FF // tf):
            ra_ref[:, j * tf:(j + 1) * tf] = jnp.maximum(_dot(hv, wu_ref[:, j * tf:(j + 1) * tf]), 0.0).astype(BF16)

        @pl.when(i == 0)
        def _():
            w_down_copy.wait()

        rav = ra_ref[...]
        dvec = _dot(rav * rav, wd_ref[...])
        r4 = lax.rsqrt(jnp.mean(dvec * dvec, axis=-1, keepdims=True) + EPS)
        dn = dvec * r4
        g4 = g4_ref[...]
        err = x2_ref[...] + dn * g4 - t_ref[...]
        dy = err * (1.0 / D_MODEL)
        dy_ref[...] = dy
        dg = dy * g4
        dd_ref[...] = (r4 * (dg - dn * jnp.mean(dg * dn, axis=-1, keepdims=True))).astype(BF16)
        _acc_rows(dg4_ref, _rsum(dy * dn), i == 0)
        tile_loss = 0.5 * jnp.sum(jnp.sum(err * err, axis=-1, keepdims=True), axis=0, keepdims=True) / D_MODEL
        _acc_rows(loss_ref, jnp.broadcast_to(tile_loss, (1, 128)), i == 0)

    row = pl.BlockSpec((tm, D_MODEL), lambda i: (i, 0))
    wide = pl.BlockSpec((tm, D_FF), lambda i: (i, 0))
    w_up_once = pl.BlockSpec((D_MODEL, D_FF), lambda i: (0, 0), pipeline_mode=pl.Buffered(1))
    ra, dd, dy, dg4, loss = pl.pallas_call(
        body, name="mlp_fwd", grid=(t_tok // tm,),
        out_shape=(jax.ShapeDtypeStruct((t_tok, D_FF), BF16), jax.ShapeDtypeStruct((t_tok, D_MODEL), BF16),
                   jax.ShapeDtypeStruct((t_tok, D_MODEL), F32), jax.ShapeDtypeStruct((1, D_MODEL), F32),
                   jax.ShapeDtypeStruct((1, 128), F32)),
        in_specs=[row, w_up_once, pl.BlockSpec(memory_space=pl.ANY), row, row, _full((1, D_MODEL))],
        out_specs=(wide, row, row, _full((1, D_MODEL)), _full((1, 128))),
        scratch_shapes=[pltpu.VMEM((D_FF, D_MODEL), BF16), pltpu.SemaphoreType.DMA((1,))],
        compiler_params=_params("arbitrary"))(h3, w_up, w_down, x2, target, g4)
    return ra, dd, dy, dg4, loss


def _mlp_bwd(dd, w_down, ra, w_up, x2, dy, o, g3, g2, tm, tf):
    t_tok = x2.shape[0]

    def hidden_body(dd_ref, wd_ref, ra_ref, da_ref):
        df = _dot(dd_ref[...], wd_ref[...], _NT)
        da_ref[...] = (df * (2.0 * ra_ref[...].astype(F32))).astype(BF16)

    tu = min(2 * tm, t_tok)
    da = pl.pallas_call(
        hidden_body, name="mlp_bwd_hidden", grid=(D_FF // tf, t_tok // tu),
        out_shape=jax.ShapeDtypeStruct((t_tok, D_FF), BF16),
        in_specs=[pl.BlockSpec((tu, D_MODEL), lambda j, i: (i, 0)), pl.BlockSpec((tf, D_MODEL), lambda j, i: (j, 0)),
                  pl.BlockSpec((tu, tf), lambda j, i: (i, j))],
        out_specs=pl.BlockSpec((tu, tf), lambda j, i: (i, j)),
        compiler_params=_params("parallel", "parallel"))(dd, w_down, ra)

    def in_body(da_ref, wu_ref, x2_ref, dy_ref, o_ref, g3_ref, g2_ref, dx2_ref, do_ref, dg3_ref, dg2_ref):
        i = pl.program_id(0)
        dh3 = _dot(da_ref[...], wu_ref[...], _NT)
        dn3, dg3 = _rms_bwd(x2_ref[...], g3_ref[...], dh3)
        dx2 = dy_ref[...] + dn3
        dx2_ref[...] = dx2
        do, dg2 = _rms_bwd(o_ref[...], g2_ref[...], dx2)
        do_ref[...] = do.astype(BF16)
        _acc_rows(dg3_ref, dg3, i == 0)
        _acc_rows(dg2_ref, dg2, i == 0)

    row = pl.BlockSpec((tm, D_MODEL), lambda i: (i, 0))
    vec = _full((1, D_MODEL))
    sd = lambda dt: jax.ShapeDtypeStruct((t_tok, D_MODEL), dt)
    dx2, do, dg3, dg2 = pl.pallas_call(
        in_body, name="mlp_bwd_in", grid=(t_tok // tm,),
        out_shape=(sd(F32), sd(BF16), jax.ShapeDtypeStruct((1, D_MODEL), F32), jax.ShapeDtypeStruct((1, D_MODEL), F32)),
        in_specs=[pl.BlockSpec((tm, D_FF), lambda i: (i, 0)), _full((D_MODEL, D_FF)), row, row, row, vec, vec],
        out_specs=(row, row, vec, vec), compiler_params=_params("arbitrary"))(da, w_up, x2, dy, o, g3, g2)
    return da, dx2, do, dg3, dg2


def _wgrad(a, b, out_blocks, bm, bn, bk, square_a, name, dep=None):
    t_tok, m = a.shape
    n = b.shape[1]
    nk = t_tok // bk

    def body(a_ref, b_ref, *rest):
        o_ref, acc_ref = rest[-2:]
        k = pl.program_id(2)
        av = a_ref[...]
        if square_a:
            av = av * av
        part = _dot(av, b_ref[...], _TN)

        def emit(res):
            if out_blocks is None:
                o_ref[...] = res.astype(BF16)
            else:
                o_ref[0] = res.astype(BF16)

        if nk == 1:
            emit(part)
            return

        @pl.when(k == 0)
        def _():
            acc_ref[...] = part

        @pl.when(k > 0)
        def _():
            acc_ref[...] += part

        @pl.when(k == nk - 1)
        def _():
            emit(acc_ref[...])

    if out_blocks is None:
        out_shape = jax.ShapeDtypeStruct((m, n), BF16)
        out_spec = pl.BlockSpec((bm, bn), lambda i, j, k: (i, j))
    else:
        assert n // out_blocks == bn
        out_shape = jax.ShapeDtypeStruct((out_blocks, m, bn), BF16)
        out_spec = pl.BlockSpec((1, bm, bn), lambda i, j, k: (j, i, 0))
    deps = [] if dep is None else [dep]
    return pl.pallas_call(
        body, name=name, grid=(m // bm, n // bn, nk), out_shape=out_shape,
        in_specs=[pl.BlockSpec((bk, bm), lambda i, j, k: (k, i)), pl.BlockSpec((bk, bn), lambda i, j, k: (k, j))]
        + [pl.BlockSpec(memory_space=pl.ANY)] * len(deps),
        out_specs=out_spec, scratch_shapes=[pltpu.VMEM((bm, bn) if nk > 1 else (8, 128), F32)],
        compiler_params=_params("parallel", "parallel", "arbitrary"))(a, b, *deps)


def _wgrad_in_chunked(h1, pieces, bn, bk, dep=None):
    t_tok = h1.shape[0]
    nk = t_tok // bk
    shard = IN_COLS // N_DEV
    widths = [b - a for a, b in _IN_SPLITS]

    def body(h_ref, *rest):
        piece_refs = rest[:len(widths)]
        o_ref, acc_ref = rest[-2:]
        k = pl.program_id(1)
        hv = h_ref[...]
        for (a, b), r in zip(_IN_SPLITS, piece_refs):
            part = _dot(r[...], hv, _TN)

            @pl.when(k == 0)
            def _():
                acc_ref[a:b, :] = part

            @pl.when(k > 0)
            def _():
                acc_ref[a:b, :] += part

        @pl.when(k == nk - 1)
        def _():
            for j in range(N_DEV):
                o_ref[j] = acc_ref[shard * j:shard * (j + 1), :].astype(BF16)

    deps = [] if dep is None else [dep]
    return pl.pallas_call(
        body, name="wgrad_in", grid=(D_MODEL // bn, nk), out_shape=jax.ShapeDtypeStruct((N_DEV, shard, D_MODEL), BF16),
        in_specs=[pl.BlockSpec((bk, bn), lambda j, k: (k, j))] + [pl.BlockSpec((bk, n), lambda j, k: (k, 0)) for n in widths]
        + [pl.BlockSpec(memory_space=pl.ANY)] * len(deps),
        out_specs=pl.BlockSpec((N_DEV, shard, bn), lambda j, k: (0, 0, j)),
        scratch_shapes=[pltpu.VMEM((IN_PAD, bn), F32)],
        compiler_params=_params("parallel", "arbitrary"))(h1, *pieces, *deps)


def _dmix_wgrad_out(do, w_out, mix_a, mix_b, tm, dep=None):
    t_tok = do.shape[0]
    steps = t_tok // tm
    deps = [] if dep is None else [dep]

    def body(d_ref, w_ref, a_ref, b_ref, *rest):
        dm_ref, g_ref, acc_ref = rest[-3:]
        i = pl.program_id(0)
        dov = d_ref[...]
        dm_ref[...] = _dot(dov, w_ref[...], _NT).astype(BF16)
        for (lo, hi), r in zip(((0, GM_WIDTH), (GM_WIDTH, D_MODEL)), (a_ref, b_ref)):
            part = _dot(r[...], dov, _TN)

            @pl.when(i == 0)
            def _():
                acc_ref[lo:hi, :] = part

            @pl.when(i > 0)
            def _():
                acc_ref[lo:hi, :] += part

        @pl.when(i == steps - 1)
        def _():
            g_ref[...] = acc_ref[...].astype(BF16)

    row = lambda n: pl.BlockSpec((tm, n), lambda i: (i, 0))
    return pl.pallas_call(
        body, name="dmix_wgrad_out", grid=(steps,),
        out_shape=(jax.ShapeDtypeStruct((t_tok, D_MODEL), BF16), jax.ShapeDtypeStruct((D_MODEL, D_MODEL), BF16)),
        in_specs=[row(D_MODEL), _full((D_MODEL, D_MODEL)), row(GM_WIDTH), row(SSM_WIDTH)]
        + [pl.BlockSpec(memory_space=pl.ANY)] * len(deps),
        out_specs=(row(D_MODEL), _full((D_MODEL, D_MODEL))), scratch_shapes=[pltpu.VMEM((D_MODEL, D_MODEL), F32)],
        compiler_params=_params("arbitrary"))(do, w_out, mix_a, mix_b, *deps)


def _gmlp_bwd(dmix, u, v, lnw, lnb, wcat, wtcat, bias, avg, expand_t):
    t_tok = u.shape[0]
    tm = min(_GMLP_ROWS, t_tok)

    def body(dm_ref, u_ref, v_ref, lnw_ref, lnb_ref, wcat_ref, wtcat_ref, bias_ref, avg_ref, expt_ref, du_ref, dv_ref,
             dw_ref, db_ref, dlnw_ref, dlnb_ref):
        i = pl.program_id(0)
        m_l, m_r = _lane_masks()
        avg = avg_ref[...]
        lnw = lnw_ref[...]
        ug, dug, dvg, rstd, vhat, vn, mixed = _gmlp_common(
            u_ref[...].astype(F32), v_ref[...].astype(F32), lnw, lnb_ref[...], avg, wcat_ref, bias_ref[...], m_l, m_r)
        dya = dm_ref[...].astype(F32)
        du_ref[...] = (dya * mixed * dug).astype(BF16)
        dmixed = dya * ug
        dvn_rows, dws, dbt = [], [None] * N_HEADS, None
        for r in range(tm // CHUNK):
            dvn_cols = []
            for j in range(N_HEADS // 2):
                dmp = dmixed[CHUNK * r:CHUNK * (r + 1), 128 * j:128 * (j + 1)]
                dvn_cols.append(_dot(wtcat_ref[j], _stack_pair(dmp, m_l, m_r)))
                vnp = vn[CHUNK * r:CHUNK * (r + 1), 128 * j:128 * (j + 1)].astype(BF16)
                for i_h, mask in enumerate((m_l, m_r)):
                    part = _dot((dmp * mask).astype(BF16), vnp, _NT)
                    dws[2 * j + i_h] = part if r == 0 else dws[2 * j + i_h] + part
            dvn_rows.append(jnp.concatenate(dvn_cols, axis=1))
            part = _split_dot(dmixed[CHUNK * r:CHUNK * (r + 1), :], expt_ref[...], 2)
            dbt = part if r == 0 else dbt + part
        dvn = jnp.concatenate(dvn_rows, axis=0)
        dvh = dvn * lnw
        dvgel = rstd * (dvh - _head_mean(dvh, avg) - vhat * _head_mean(dvh * vhat, avg))
        dv_ref[...] = (dvgel * dvg).astype(BF16)
        first = i == 0

        @pl.when(first)
        def _():
            for h in range(N_HEADS):
                dw_ref[h] = dws[h]
            db_ref[...] = dbt

        @pl.when(jnp.logical_not(first))
        def _():
            for h in range(N_HEADS):
                dw_ref[h] += dws[h]
            db_ref[...] += dbt

        _acc_rows(dlnw_ref, _rsum(dvn * vhat), first)
        _acc_rows(dlnb_ref, _rsum(dvn), first)

    row = pl.BlockSpec((tm, GM_WIDTH), lambda i: (i, 0))
    consts = [lnw, lnb, wcat, wtcat, bias, avg, expand_t]
    return pl.pallas_call(
        body, name="gmlp_bwd", grid=(t_tok // tm,),
        out_shape=(jax.ShapeDtypeStruct((t_tok, GM_WIDTH), BF16), jax.ShapeDtypeStruct((t_tok, GM_WIDTH), BF16),
                   jax.ShapeDtypeStruct((N_HEADS, CHUNK, CHUNK), F32), jax.ShapeDtypeStruct((CHUNK, CHUNK), F32),
                   jax.ShapeDtypeStruct((1, GM_WIDTH), F32), jax.ShapeDtypeStruct((1, GM_WIDTH), F32)),
        in_specs=[row, row, row] + [_full(a.shape) for a in consts],
        out_specs=(row, row, _full((N_HEADS, CHUNK, CHUNK)), _full((CHUNK, CHUNK)), _full((1, GM_WIDTH)),
                   _full((1, GM_WIDTH))),
        compiler_params=_params("arbitrary"))(dmix, u, v, *consts)


def _ssd_bwd(dmix, z, xbc, pre, dtr, y, states, cw, cb, dtb, alog, dskip_exp, nw, expand, expand_t, tril, triu, seq,
             dep=None):
    t_tok = z.shape[0]
    nb, nc, row, _, states_spec, fold, unfold = _ssd_specs(t_tok, seq, True)
    q = CHUNK

    def one_sequence(s, dm_ref, z_ref, xbc_ref, pre_ref, dtr_ref, y_ref, st_ref, cw_ref, dtb_ref, alog_ref, dsk_ref,
                     nw_ref, exp_ref, expt_ref, tril_ref, triu_ref, dz_ref, dxbc_ref, ddt_ref, dhead_ref, dstate_ref):
        m_l, m_r = _lane_masks()
        expt = expt_ref[...]
        f = _ssd_common(pre_ref[s], dtr_ref[s], dtb_ref[...], alog_ref[...], exp_ref[...], tril_ref[...])
        act, pre, sg = f["act"], f["pre"], f["sg"]
        xs = act[:, :SSM_WIDTH]
        xdt = xs * f["dt_exp"]
        xw = xdt * f["w_end"]
        state = st_ref[s, 0]
        dstate = dstate_ref[s]
        zv, yv, dout, nw = z_ref[s].astype(F32), y_ref[s], dm_ref[s].astype(F32), nw_ref[...]
        sz = jax.nn.sigmoid(zv)
        sl = zv * sz
        yg = yv * sl
        tv = dout * nw
        dyg_parts, ygh_parts = [], []
        for g in range(2):
            ygg = yg[:, 256 * g:256 * (g + 1)]
            rr = lax.rsqrt(jnp.mean(ygg * ygg, axis=-1, keepdims=True) + EPS)
            ygh = ygg * rr
            tg = tv[:, 256 * g:256 * (g + 1)]
            dyg_parts.append(rr * (tg - ygh * jnp.mean(tg * ygh, axis=-1, keepdims=True)))
            ygh_parts.append(ygh)
        dyg = jnp.concatenate(dyg_parts, axis=1)
        dnw = _rsum(dout * jnp.concatenate(ygh_parts, axis=1))
        dy = dyg * sl
        dz_ref[s] = (dyg * yv * (sz * (1.0 + zv * (1.0 - sz)))).astype(BF16)
        ddsk = _rsum(dy * xs)
        dye = dy * f["e"]
        lane = lax.broadcasted_iota(jnp.int32, (q, q), 1)
        sub = lax.broadcasted_iota(jnp.int32, (q, q), 0)
        rs_mat = jnp.zeros((q, q), F32)
        cs_mat = jnp.zeros((q, q), F32)
        dxdt_cols, yoff, dst_in, dxw, d_b, d_c = [], [], [], [], [], []
        for g in range(2):
            bg = act[:, 512 + 128 * g:640 + 128 * g].astype(BF16)
            cg = act[:, 768 + 128 * g:896 + 128 * g].astype(BF16)
            cb_mat = _dot(cg, bg, _NT)
            stg = state[:, 256 * g:256 * (g + 1)].astype(BF16)
            dyeg = dye[:, 256 * g:256 * (g + 1)].astype(BF16)
            yoff.append(_dot(cg, stg))
            dcg = _dot(dyeg, stg, _NT)
            dst_in.append(_dot(cg, dyeg, _TN))
            dcb = jnp.zeros((q, q), F32)
            for pr in range(2):
                h0 = 4 * g + 2 * pr
                gf = [cb_mat * f["decay"][h0], cb_mat * f["decay"][h0 + 1]]
                gcat = jnp.concatenate([gf[0].astype(BF16), gf[1].astype(BF16)], axis=1)
                xst = _stack_pair(xdt[:, 64 * h0:64 * h0 + 128], m_l, m_r)
                dyp = dy[:, 64 * h0:64 * h0 + 128].astype(BF16)
                dgcat = _dot(dyp, xst, _NT)
                dxst = _dot(gcat, dyp, _TN)
                dxdt_cols.append(dxst[:q] * m_l + dxst[q:] * m_r)
                for i in range(2):
                    h = h0 + i
                    dg = dgcat[:, q * i:q * (i + 1)]
                    mm = dg * gf[i]
                    rs_mat = rs_mat + jnp.where(lane == h, jnp.sum(mm, axis=1, keepdims=True), 0.0)
                    cs_mat = cs_mat + jnp.where(sub == h, jnp.sum(mm, axis=0, keepdims=True), 0.0)
                    dcb = dcb + dg * f["decay"][h]
            dcb16 = dcb.astype(BF16)
            dstg = dstate[:, 256 * g:256 * (g + 1)].astype(BF16)
            d_c.append(dcg + _dot(dcb16, bg))
            dxw.append(_dot(bg, dstg))
            d_b.append(_dot(dcb16, cg, _TN) + _dot(xw[:, 256 * g:256 * (g + 1)].astype(BF16), dstg, _NT))
        dxw = jnp.concatenate(dxw, axis=1)
        dxdt = jnp.concatenate(dxdt_cols, axis=1) + dxw * f["w_end"]
        qv = dxw * xw
        end_row = _rsum(qv) + _rsum(dstate * state) * f["cd"]
        x2 = dye * jnp.concatenate(yoff, axis=1) - qv
        row_i = lax.broadcasted_iota(jnp.int32, (q, 1), 0)
        x2 = x2 + jnp.where(row_i == q - 1, end_row, 0.0)
        da_cs = _split_dot(x2, expt, 2) + rs_mat - cs_mat.T
        ddt = _split_dot(dxdt * xs, expt, 2)
        dxs = dsk_ref[...] * dy + dxdt * f["dt_exp"]
        dda = _split_dot_left(triu_ref[...], da_cs, 3)
        ddt = ddt + dda * f["a_row"]
        dalog = _rsum(dda * f["dt"]) * f["a_row"]
        draw = ddt * jax.nn.sigmoid(f["dtp"])
        ddt_ref[s] = draw.astype(BF16)
        dact = jnp.concatenate([dxs] + d_b + d_c, axis=1)
        dpre = dact * (sg * (1.0 + pre * (1.0 - sg)))
        dhead = dhead_ref[s]
        xv = xbc_ref[s]
        shifted = [_shift_rows(dpre, dhead, 3 - k, False) for k in range(3)] + [dpre]
        dxbc = cw_ref[3:4, :] * dpre
        for k in range(3):
            dxbc = dxbc + cw_ref[k:k + 1, :] * shifted[k]
        dxbc_ref[s] = dxbc.astype(BF16)
        dhead_ref[s] = dpre[0:8, :]
        dstate_ref[s] = dstate * f["cd"] + jnp.concatenate(dst_in, axis=1)
        row8 = lax.broadcasted_iota(jnp.int32, (8, 1), 0)
        dcw = jnp.zeros((8, CONV_CH), F32)
        for k in range(4):
            dcw = dcw + jnp.where(row8 == k, _rsum(shifted[k] * xv), 0.0)
        return dcw, _rsum(dpre), _rsum(draw), dalog, _split_dot(ddsk, expt, 3), dnw

    def body(dm_ref, z_ref, xbc_ref, pre_ref, dtr_ref, y_ref, st_ref, cw_ref, cb_ref, dtb_ref, alog_ref, dsk_ref,
             nw_ref, exp_ref, expt_ref, tril_ref, triu_ref, dz_ref, dxbc_ref, ddt_ref, dcw_ref, dcb_ref, ddtb_ref,
             dalog_ref, dd_ref, dnw_ref, dhead_ref, dstate_ref):
        c = pl.program_id(0)
        first = c == 0

        @pl.when(first)
        def _():
            dstate_ref[...] = jnp.zeros_like(dstate_ref)
            dhead_ref[...] = jnp.zeros_like(dhead_ref)

        total = None
        for s in range(nb):
            parts = one_sequence(s, dm_ref, z_ref, xbc_ref, pre_ref, dtr_ref, y_ref, st_ref, cw_ref, dtb_ref, alog_ref,
                                 dsk_ref, nw_ref, exp_ref, expt_ref, tril_ref, triu_ref, dz_ref, dxbc_ref, ddt_ref,
                                 dhead_ref, dstate_ref)
            total = parts if total is None else tuple(a + b for a, b in zip(total, parts))
        dcw = total[0]

        @pl.when(first)
        def _():
            dcw_ref[...] = dcw

        @pl.when(jnp.logical_not(first))
        def _():
            dcw_ref[...] += dcw

        for ref, part in zip((dcb_ref, ddtb_ref, dalog_ref, dd_ref, dnw_ref), total[1:]):
            _acc_rows(ref, part, first)

    consts = [cw, cb, dtb, alog, dskip_exp, nw, expand, expand_t, tril, triu]
    deps = [] if dep is None else [dep]
    n_in = 7 + len(consts)

    def body_skipping_dep(*refs):
        body(*refs[:n_in], *refs[n_in + len(deps):])

    acc = lambda n: jax.ShapeDtypeStruct((1, n), F32)
    sd = lambda n: jax.ShapeDtypeStruct((nb, seq, n), BF16)
    dz, dxbc, ddt, *small_grads = pl.pallas_call(
        body_skipping_dep, name="ssd_bwd", grid=(nc,),
        out_shape=(sd(SSM_WIDTH), sd(CONV_CH), sd(CHUNK), jax.ShapeDtypeStruct((8, CONV_CH), F32), acc(CONV_CH),
                   acc(CHUNK), acc(CHUNK), acc(CHUNK), acc(SSM_WIDTH)),
        in_specs=[row(SSM_WIDTH, col=1), row(SSM_WIDTH), row(CONV_CH), row(CONV_CH), row(CHUNK), row(SSM_WIDTH),
                  states_spec]
        + [_full(a.shape) for a in consts] + [pl.BlockSpec(memory_space=pl.ANY)] * len(deps),
        out_specs=(row(SSM_WIDTH), row(CONV_CH), row(CHUNK), _full((8, CONV_CH)), _full((1, CONV_CH)),
                   _full((1, CHUNK)), _full((1, CHUNK)), _full((1, CHUNK)), _full((1, SSM_WIDTH))),
        scratch_shapes=[pltpu.VMEM((nb, 8, CONV_CH), F32), pltpu.VMEM((nb, N_STATE, SSM_WIDTH), F32)],
        compiler_params=_params("arbitrary"))(
            fold(dmix), fold(z), fold(xbc), fold(pre), fold(dtr), fold(y), states, *consts, *deps)
    return (unfold(dz), unfold(dxbc), unfold(ddt), *small_grads)


def _in_bwd(du, dv, dz, dxbc, ddt, w_in, x, dx2, g1, tm, me, riders=(), dep=None):
    t_tok = x.shape[0]
    steps = t_tok // tm

    n_in = [5 + ("mask" in rd) for rd in riders]
    first_in = [sum(n_in[:r]) for r in range(len(riders))]

    def body(me_ref, du_ref, dv_ref, dz_ref, dxbc_ref, ddt_ref, w_ref, x_ref, dx2_ref, g_ref, *rest):
        outs = rest[len(rest) - 2 - 4 * len(riders):]
        gx_ref, dg_ref = outs[:2]
        i = pl.program_id(0)
        dh = None
        for (a, b), ref in zip(_IN_SPLITS, (du_ref, dv_ref, dz_ref, dxbc_ref, ddt_ref)):
            part = _dot(ref[...], w_ref[a:b, :])
            dh = part if dh is None else dh + part
        dn, dg = _rms_bwd(x_ref[...], g_ref[...], dh)
        gx_ref[...] = dx2_ref[...] + dn
        _acc_rows(dg_ref, dg, i == 0)
        for r in range(len(riders)):
            p_ref, own_ref, w_ref_r, m_ref_r, v_ref_r = rest[first_in[r]:first_in[r] + 5]
            g = _sum_parts(me_ref[0], p_ref, own_ref[0])
            if n_in[r] == 6:
                g = g * rest[first_in[r] + 5][...]
            d, mn, vn = _adamw_math(w_ref_r[...], g, m_ref_r[...], v_ref_r[...])
            for o_ref, val in zip(outs[2 + 4 * r:6 + 4 * r], (g, d, mn, vn)):
                o_ref[...] = val

    row = lambda n: pl.BlockSpec((tm, n), lambda i, me_ref: (i, 0))
    whole = lambda shape: pl.BlockSpec(shape, lambda i, me_ref: (0,) * len(shape))
    widths = [b - a for a, b in _IN_SPLITS]
    deps = [] if dep is None else [dep]
    rider_args, rider_specs, rider_out_shapes, rider_out_specs = [], [], [], []
    for rd in riders:
        rows, cols = rd["w"].shape[0] // steps, rd["w"].shape[1]
        blk = pl.BlockSpec((rows, cols), lambda i, me_ref: (i, 0))
        rider_args += [rd["parts"], rd["own"], rd["w"], rd["m"], rd["v"]]
        rider_specs += [pl.BlockSpec((N_DEV, rows, cols), lambda i, me_ref: (0, i, 0)),
                        pl.BlockSpec((1, rows, cols), lambda i, me_ref: (me_ref[0], i, 0)), blk, blk, blk]
        if "mask" in rd:
            rider_args.append(rd["mask"])
            rider_specs.append(whole((rows, cols)))
        rider_out_shapes += [jax.ShapeDtypeStruct(rd["w"].shape, F32)] * 4
        rider_out_specs += [blk] * 4
    outs = pl.pallas_call(
        body, name="in_bwd",
        out_shape=(jax.ShapeDtypeStruct((t_tok, D_MODEL), F32), jax.ShapeDtypeStruct((1, D_MODEL), F32),
                   *rider_out_shapes),
        grid_spec=pltpu.PrefetchScalarGridSpec(
            num_scalar_prefetch=1, grid=(steps,),
            in_specs=[row(n) for n in widths] + [whole((IN_PAD, D_MODEL)), row(D_MODEL), row(D_MODEL),
                                                 whole((1, D_MODEL))] + rider_specs
            + [pl.BlockSpec(memory_space=pl.ANY)] * len(deps),
            out_specs=(row(D_MODEL), whole((1, D_MODEL)), *rider_out_specs)),
        compiler_params=_params("arbitrary"))(me, du, dv, dz, dxbc, ddt, w_in, x, dx2, g1, *rider_args, *deps)
    return outs[0], outs[1], [tuple(outs[2 + 4 * r:6 + 4 * r]) for r in range(len(riders))]


def _pad_lanes(a, n):
    return jnp.pad(a, ((0, 0), (0, n - a.shape[1])))


def _local_step(x, target, seq, small, hooks, first_dep=None):
    t_tok = x.shape[0]
    tm = min(TOKEN_TILE, t_tok)
    avg, expand, expand_t, tril, triu = _const_mats()
    g1, g2, g3, g4 = (small[k].reshape(1, D_MODEL) for k in
                      ("norm_mix_pre", "norm_mix_post", "norm_ffn_pre", "norm_ffn_post"))
    tie = (lambda a: a) if first_dep is None else (lambda a: a + first_dep[0, 0])
    lnw = tie(small["gm_ln_w"]).reshape(1, GM_WIDTH)
    lnb = tie(small["gm_ln_b"]).reshape(1, GM_WIDTH)
    causal = jnp.tril(jnp.ones((CHUNK, CHUNK), F32))
    wm = tie(small["gm_w_s"]) * causal
    pair = lambda w: w.reshape(4, 2, CHUNK, CHUNK).transpose(0, 2, 1, 3).reshape(4, CHUNK, 2 * CHUNK).astype(BF16)
    wcat = pair(wm)
    wtcat = pair(jnp.swapaxes(wm, 1, 2))
    bias = jnp.repeat(tie(small["gm_b_s"]).T, HEAD_DIM, axis=1)
    cb = small["conv_b"].reshape(1, CONV_CH)
    dtb = _pad_lanes(tie(small["dt_bias"]).reshape(1, N_HEADS), CHUNK)
    alog = _pad_lanes(tie(small["a_log"]).reshape(1, N_HEADS), CHUNK)
    dskip_exp = jnp.repeat(tie(small["d_skip"]).reshape(1, N_HEADS), HEAD_DIM, axis=1)
    nw = small["ssm_norm_w"].reshape(1, SSM_WIDTH)

    h1 = _prenorm(x, g1, tm, hooks.get("prenorm_after", first_dep))
    w_in_t, conv_w = hooks["mixer_weights"]((h1, lnw, lnb, wcat, wtcat, bias, dtb, alog, dskip_exp))
    tall = min(2 * tm, t_tok)
    u, v, z, xbc, dtr = _in_proj(h1, w_in_t, tall)
    mix_a = _gmlp_fwd(u, v, lnw, lnb, wcat, bias, avg)
    dep = hooks["gmlp_done"](mix_a) if "gmlp_done" in hooks else None
    mix_b, y_pre, states, pre = _ssd_fwd(z, xbc, dtr, conv_w, cb, dtb, alog, dskip_exp, nw, expand, tril, seq, dep)
    w_out, dep = hooks["mixers_done"](mix_b)
    o, x2, h3 = _out_proj(mix_a, mix_b, w_out, x, g2, g3, tall, dep)
    w_up, w_down = hooks["mlp_weights"](h3)
    tf = FF_TILE
    ra, dd, dy, dg4, loss = _mlp_fwd(h3, w_up, w_down, x2, target, g4, tm, tf)

    da, dx2, do, dg3, dg2 = _mlp_bwd(dd, w_down, ra, w_up, x2, dy, o, g3, g2, tm, tf)
    g_w_down = _wgrad(ra, dd, None, WGRAD_TILE, D_MODEL, t_tok, True, "wgrad_down")
    g_w_up = _wgrad(h3, da, N_DEV, D_MODEL, D_FF // N_DEV, t_tok, False, "wgrad_up")
    dep = hooks["mlp_grads"](g_w_down, g_w_up)
    dmix, g_w_out = _dmix_wgrad_out(do, w_out, mix_a, mix_b, tall, dep)
    du, dv, dws, dbt, dlnw, dlnb = _gmlp_bwd(dmix, u, v, lnw, lnb, wcat, wtcat, bias, avg, expand_t)
    dep = hooks["gmlp_grads"](g_w_out, dws)
    dz, dxbc, ddt, dcw, dcb, ddtb, dalog, ddsk, dnw = _ssd_bwd(
        dmix, z, xbc, pre, dtr, y_pre, states, conv_w, cb, dtb, alog, dskip_exp, nw, expand, expand_t, tril, triu, seq,
        dep)
    g_w_in = _wgrad_in_chunked(h1, (du, dv, dz, dxbc, ddt), WGRAD_TILE, t_tok // 2, dep)
    dep = hooks["in_grads"](g_w_in, dcw[0:4])
    riders = hooks["arrived_updates"](dep) if "arrived_updates" in hooks else []
    me = hooks.get("me", jnp.zeros((1,), jnp.int32))
    grad_x, dg1, updates = _in_bwd(du, dv, dz, dxbc, ddt, w_in_t, x, dx2, g1, tm, me, riders, dep)

    grads = dict(
        updates=updates,
        w_in=g_w_in, w_out=g_w_out, w_up=g_w_up, w_down=g_w_down, conv_w=dcw[0:4],
        norm_mix_pre=dg1, norm_mix_post=dg2, norm_ffn_pre=dg3, norm_ffn_post=dg4, gm_ln_w=dlnw, gm_ln_b=dlnb,
        gm_w_s=dws, gm_b_s=dbt, conv_b=dcb, dt_bias=ddtb, a_log=dalog, d_skip=ddsk, ssm_norm_w=dnw)
    return loss[0, 0], grad_x, grads


_WEIGHTS = ("norm_mix_pre", "w_in", "gm_ln_w", "gm_ln_b", "gm_w_s", "gm_b_s", "conv_w", "conv_b", "dt_bias", "a_log",
            "d_skip", "ssm_norm_w", "w_out", "norm_mix_post", "norm_ffn_pre", "w_up", "w_down", "norm_ffn_post")
_SLAB_ROWS = (("norm_mix_pre", 1024), ("norm_mix_post", 1024), ("norm_ffn_pre", 1024), ("norm_ffn_post", 1024),
              ("conv_b", 1024), ("ssm_norm_w", 512), ("gm_ln_w", 512), ("gm_ln_b", 512), ("dt_bias", 8), ("a_log", 8),
              ("d_skip", 8))
_SLAB_LOSS_ROW = len(_SLAB_ROWS)
_SLAB_BS_ROW = 16
_SMALL_PARAMS = tuple(name for name, _ in _SLAB_ROWS) + ("gm_b_s",)
_LN_PARAMS = ("gm_ln_w", "gm_ln_b")


_SLAB_CONV_ROW = _SLAB_LOSS_ROW + 1


def _pack_slab(g, loss_part):
    rows = [_pad_lanes(g[name], D_MODEL) for name, _ in _SLAB_ROWS]
    rows.append(jnp.broadcast_to(loss_part, (1, D_MODEL)))
    rows.append(g["conv_w"])
    assert sum(r.shape[0] for r in rows) == _SLAB_BS_ROW
    rows.append(_pad_lanes(g["gm_b_s"].T[0:N_HEADS], D_MODEL))
    return jnp.concatenate(rows, axis=0)


def _adamw_slab(parts, me, w, m, v):
    names = _SMALL_PARAMS + ("conv_w",)
    shapes = [w[k].shape for k in names]
    unfold = np.zeros((GM_WIDTH, HEAD_DIM), np.float32)
    for h in range(N_HEADS):
        unfold[h * HEAD_DIM:(h + 1) * HEAD_DIM, :] = np.eye(HEAD_DIM)
    unfold = jnp.asarray(unfold, dtype=BF16)
    n = len(names)
    shard = CONV_CH // N_DEV

    def body(me_ref, p_ref, unfold_ref, *refs):
        w_refs, m_refs, v_refs = refs[:n], refs[n:2 * n], refs[2 * n:3 * n]
        outs = refs[3 * n:]
        g_all = p_ref[0]
        for j in range(1, N_DEV):
            g_all = g_all + p_ref[j]
        lane = lax.broadcasted_iota(jnp.int32, (N_HEADS, GM_WIDTH), 1)
        head = lax.broadcasted_iota(jnp.int32, (N_HEADS, GM_WIDTH), 0)
        own_lanes = jnp.logical_and(lane >= head * HEAD_DIM, lane < (head + 1) * HEAD_DIM)
        mine = pl.ds(pl.multiple_of(me_ref[0] * shard, shard), shard)
        for i, name in enumerate(names):
            if name == "gm_b_s":
                g = g_all[_SLAB_BS_ROW:_SLAB_BS_ROW + N_HEADS, 0:CHUNK]
            elif name == "conv_w":
                g = p_ref[0, _SLAB_CONV_ROW:_SLAB_CONV_ROW + 4, mine]
                for j in range(1, N_DEV):
                    g = g + p_ref[j, _SLAB_CONV_ROW:_SLAB_CONV_ROW + 4, mine]
            else:
                row = [r for r, (k, _) in enumerate(_SLAB_ROWS) if k == name][0]
                g = g_all[row:row + 1, 0:dict(_SLAB_ROWS)[name]]
                if name in _LN_PARAMS:
                    g = _split_dot(jnp.where(own_lanes, g, 0.0), unfold_ref[...], 3)
            d, mn, vn = _adamw_math(w_refs[i][...], g, m_refs[i][...], v_refs[i][...])
            for o_ref, val in zip(outs[4 * i:4 * i + 4], (g, d, mn, vn)):
                o_ref[...] = val
        outs[-1][...] = g_all[_SLAB_LOSS_ROW:_SLAB_LOSS_ROW + 1, 0:128]

    def whole(shape):
        nd = len(shape)
        return pl.BlockSpec(shape, lambda i, me_ref: (0,) * nd)

    ins = [parts, unfold] + [d[k] for d in (w, m, v) for k in names]
    out_shape = tuple(jax.ShapeDtypeStruct(s, F32) for s in shapes for _ in range(4)) + (
        jax.ShapeDtypeStruct((1, 128), F32),)
    outs = pl.pallas_call(
        body, name="adamw_small", out_shape=out_shape,
        grid_spec=pltpu.PrefetchScalarGridSpec(
            num_scalar_prefetch=1, grid=(1,), in_specs=[whole(a.shape) for a in ins],
            out_specs=tuple(whole(s.shape) for s in out_shape)),
        compiler_params=_params("arbitrary"))(me, *ins)
    return {k: tuple(outs[4 * i:4 * i + 4]) for i, k in enumerate(names)}, outs[-1][0, 0]


def kernel(x, norm_mix_pre, w_in, gm_ln_w, gm_ln_b, gm_w_s, gm_b_s, conv_w, conv_b, dt_bias, a_log, d_skip, ssm_norm_w, w_out, norm_mix_post, norm_ffn_pre, w_up, w_down, norm_ffn_post, loss_target, m_norm_mix_pre, m_w_in, m_gm_ln_w, m_gm_ln_b, m_gm_w_s, m_gm_b_s, m_conv_w, m_conv_b, m_dt_bias, m_a_log, m_d_skip, m_ssm_norm_w, m_w_out, m_norm_mix_post, m_norm_ffn_pre, m_w_up, m_w_down, m_norm_ffn_post, v_norm_mix_pre, v_w_in, v_gm_ln_w, v_gm_ln_b, v_gm_w_s, v_gm_b_s, v_conv_w, v_conv_b, v_dt_bias, v_a_log, v_d_skip, v_ssm_norm_w, v_w_out, v_norm_mix_post, v_norm_ffn_pre, v_w_up, v_w_down, v_norm_ffn_post):
    w = dict(norm_mix_pre=norm_mix_pre, w_in=w_in, gm_ln_w=gm_ln_w, gm_ln_b=gm_ln_b, gm_w_s=gm_w_s, gm_b_s=gm_b_s, conv_w=conv_w, conv_b=conv_b, dt_bias=dt_bias, a_log=a_log, d_skip=d_skip, ssm_norm_w=ssm_norm_w, w_out=w_out, norm_mix_post=norm_mix_post, norm_ffn_pre=norm_ffn_pre, w_up=w_up, w_down=w_down, norm_ffn_post=norm_ffn_post)
    m = dict(norm_mix_pre=m_norm_mix_pre, w_in=m_w_in, gm_ln_w=m_gm_ln_w, gm_ln_b=m_gm_ln_b, gm_w_s=m_gm_w_s, gm_b_s=m_gm_b_s, conv_w=m_conv_w, conv_b=m_conv_b, dt_bias=m_dt_bias, a_log=m_a_log, d_skip=m_d_skip, ssm_norm_w=m_ssm_norm_w, w_out=m_w_out, norm_mix_post=m_norm_mix_post, norm_ffn_pre=m_norm_ffn_pre, w_up=m_w_up, w_down=m_w_down, norm_ffn_post=m_norm_ffn_post)
    v = dict(norm_mix_pre=v_norm_mix_pre, w_in=v_w_in, gm_ln_w=v_gm_ln_w, gm_ln_b=v_gm_ln_b, gm_w_s=v_gm_w_s, gm_b_s=v_gm_b_s, conv_w=v_conv_w, conv_b=v_conv_b, dt_bias=v_dt_bias, a_log=v_a_log, d_skip=v_d_skip, ssm_norm_w=v_ssm_norm_w, w_out=v_w_out, norm_mix_post=v_norm_mix_post, norm_ffn_pre=v_norm_ffn_pre, w_up=v_w_up, w_down=v_w_down, norm_ffn_post=v_norm_ffn_post)
    n_batch, seq, _ = x.shape
    shard_in = IN_COLS // N_DEV

    me = (4 * lax.axis_index("x") + 2 * lax.axis_index("y") + lax.axis_index("c")).astype(jnp.int32).reshape(1)

    def in_slot(own):
        return lax.dynamic_update_slice(lax.empty((N_DEV,) + own.shape, own.dtype), own[None],
                                        (me[0],) + (0,) * own.ndim)

    lying = lambda t: jnp.transpose(t, (2, 0, 1))
    first = [_cast_to_slot(lying(w_in), me, shard_in, "cast_w_in"), in_slot(conv_w[0])]
    ici_1, tok_ici_1 = _exchange_start(first, [True] * 2, _SAME_CORE_PEERS, "gather_mix_ici_start")
    cast_out = _cast_to_slot(w_out[0], me, 128, "cast_w_out", dep=tok_ici_1)
    cast_up = _cast_to_slot(w_up[0], me, 1024, "cast_w_up", cols=True, dep=cast_out)
    second = [cast_out, cast_up, _cast_to_slot(w_down[0], me, 512, "cast_w_down", dep=cast_up)]
    gathering = {}

    def mixer_weights(after):
        bufs = [buf for buf, _ in _exchange_wait(ici_1, after, "gather_mix_ici_wait")]
        d2d_1, tok_d2d_1 = _exchange_start(bufs, [True] * 2, _SIBLING_FORWARD, "gather_mix_d2d_start")
        gathering["late_ici"], tok_ici_2 = _exchange_start(
            second, [True] * 3, _SAME_CORE_PEERS, "gather_late_ici_start", dep=tok_d2d_1)
        (_, ag_in), (_, ag_conv) = _exchange_wait(d2d_1, tok_ici_2, "gather_mix_d2d_wait")
        w_in_t = _stack_shards(ag_in, IN_PAD, STACK_TILE, "stack_w_in")
        return w_in_t, ag_conv.transpose(1, 0, 2).reshape(4, CONV_CH)

    def gmlp_done(after):
        ((buf, _),) = _exchange_wait(gathering["late_ici"], after, "gather_out_ici_wait", only=(0,))
        gathering["out"], tok = _exchange_start([buf], [True], _SIBLING_FORWARD, "gather_out_d2d_start")
        return tok

    def mixers_done(after):
        bufs = [buf for buf, _ in _exchange_wait(gathering["late_ici"], after, "gather_mlp_ici_wait", only=(1, 2))]
        gathering["mlp"], tok = _exchange_start(bufs, [True] * 2, _SIBLING_FORWARD, "gather_mlp_d2d_start")
        ((_, ag_out),) = _exchange_wait(gathering["out"], tok, "gather_out_d2d_wait")
        return ag_out.reshape(D_MODEL, D_MODEL), tok

    def mlp_weights(after):
        (_, ag_up), (_, ag_down) = _exchange_wait(gathering["mlp"], after, "gather_mlp_d2d_wait")
        return ag_up, ag_down.reshape(D_FF, D_MODEL)

    sent = {}

    def mlp_grads(g_w_down, g_w_up):
        sent["mlp"], tok = _exchange_start(
            [g_w_down.reshape(N_DEV, D_FF // N_DEV, D_MODEL), g_w_up], [False, False], _ALL_PEERS, "grads_mlp_start")
        return tok

    def gmlp_grads(g_w_out, g_w_s):
        sent["gmlp"], tok = _exchange_start(
            [g_w_out.reshape(N_DEV, D_MODEL // N_DEV, D_MODEL), in_slot(g_w_s.astype(BF16))], [False, True], _ALL_PEERS,
            "grads_gmlp_start")
        return tok

    def in_grads(g_w_in_t, g_conv_w):
        sent["in"], tok = _exchange_start([g_w_in_t], [False], _ALL_PEERS, "grads_in_start")
        return tok

    def arrived_updates(after):
        (own_down, p_down), (own_up, p_up) = _exchange_wait(sent["mlp"], after, "grads_mlp_wait")
        (own_out, p_out), (_, p_ws) = _exchange_wait(sent["gmlp"], own_up, "grads_gmlp_wait")
        rows = lambda t: t.reshape(t.shape[:-3] + (N_HEADS * CHUNK, CHUNK))
        return [dict(parts=p_up, own=own_up, w=w_up[0], m=m_w_up[0], v=v_w_up[0]),
                dict(parts=p_down, own=own_down, w=w_down[0], m=m_w_down[0], v=v_w_down[0]),
                dict(parts=p_out, own=own_out, w=w_out[0], m=m_w_out[0], v=v_w_out[0]),
                dict(parts=rows(p_ws), own=rows(p_ws), w=rows(gm_w_s[0]), m=rows(m_gm_w_s[0]), v=rows(v_gm_w_s[0]),
                     mask=jnp.tril(jnp.ones((CHUNK, CHUNK), F32)))]

    small = {k: w[k][0] for k in _SMALL_PARAMS + ("gm_w_s",)}
    loss_part, grad_x, g = _local_step(
        x.reshape(n_batch * seq, D_MODEL), loss_target.reshape(n_batch * seq, D_MODEL), seq, small,
        dict(mixer_weights=mixer_weights, gmlp_done=gmlp_done, mixers_done=mixers_done, mlp_weights=mlp_weights,
             mlp_grads=mlp_grads, gmlp_grads=gmlp_grads, in_grads=in_grads, arrived_updates=arrived_updates, me=me,
             prenorm_after=second[2]), first_dep=tok_ici_1)

    sent_rows, tok_rows = _exchange_start([in_slot(_pack_slab(g, loss_part))], [True], _ALL_PEERS, "grads_rows_start")
    res = dict(zip(("w_up", "w_down", "w_out", "gm_w_s"), g["updates"]))
    ((own_in, p_in),) = _exchange_wait(sent["in"], tok_rows, "grads_in_wait")
    upd_in = _adamw_reduce(p_in, own_in, me, lying(w_in), lying(m_w_in), lying(v_w_in), "adamw_w_in")
    res["w_in"] = tuple(jnp.transpose(t, (1, 2, 0)) for t in upd_in)
    ((_, p_rows),) = _exchange_wait(sent_rows, upd_in[1], "grads_rows_wait")
    flat = lambda t: t[0] if t.ndim == 3 else t
    small_res, loss = _adamw_slab(
        p_rows, me, *({k: flat(d[k]) for k in _SMALL_PARAMS + ("conv_w",)} for d in (w, m, v)))
    res.update(small_res)
    res = {k: tuple(r.reshape(w[k].shape) for r in res[k]) for k in _WEIGHTS}

    outs = [loss, grad_x.reshape(x.shape)]
    for part in range(4):
        outs.extend(res[k][part] for k in _WEIGHTS)
    return tuple(outs)
```

```python
import functools

import jax
import jax.numpy as jnp
import numpy as np
from jax import lax
from jax.experimental import pallas as pl
from jax.experimental.pallas import tpu as pltpu

F32 = jnp.float32
BF16 = jnp.bfloat16

D_MODEL = 1024
GM_WIDTH = 512
SSM_WIDTH = 512
CONV_CH = 1024
N_HEADS = 8
HEAD_DIM = 64
N_STATE = 128
CHUNK = 128
D_FF = 4096
IN_COLS = 2568
IN_PAD = 2688
N_DEV = 8
EPS = 1e-6
ADAM_LR, ADAM_B1, ADAM_B2, ADAM_EPS, ADAM_WD, ADAM_STEP = 0.001, 0.9, 0.999, 1e-08, 0.01, 10
VMEM_LIMIT_BYTES = 56 * 1024 * 1024
TOKEN_TILE = 512
FF_TILE = 2048
WGRAD_TILE = 512
STACK_TILE = 256
_NT = (((1,), (1,)), ((), ()))
_TN = (((0,), (0,)), ((), ()))


def _params(*sem):
    return pltpu.CompilerParams(dimension_semantics=sem or None, vmem_limit_bytes=VMEM_LIMIT_BYTES)


def _dot(a, b, dims=None):
    if dims is None:
        return jnp.dot(a, b, preferred_element_type=F32)
    return lax.dot_general(a, b, dims, preferred_element_type=F32)


def _split_terms(x, terms):
    out, rem = [], x
    for i in range(terms):
        hi = rem.astype(BF16)
        out.append(hi)
        if i + 1 < terms:
            rem = rem - hi.astype(F32)
    return out


def _split_dot(x, m, terms):
    acc = None
    for hi in _split_terms(x, terms):
        part = _dot(hi, m)
        acc = part if acc is None else acc + part
    return acc


def _split_dot_left(m, x, terms):
    acc = None
    for hi in _split_terms(x, terms):
        part = _dot(m, hi)
        acc = part if acc is None else acc + part
    return acc


def _gelu_and_grad(x):
    c = 0.7978845608028654
    inner = c * (x + 0.044715 * x * x * x)
    t = jnp.tanh(inner)
    g = 0.5 * x * (1.0 + t)
    dg = 0.5 * (1.0 + t) + 0.5 * x * (1.0 - t * t) * c * (1.0 + 3.0 * 0.044715 * x * x)
    return g, dg


def _softplus(x):
    return jnp.maximum(x, 0.0) + jnp.log(1.0 + jnp.exp(-jnp.abs(x)))


def _rsum(x):
    return jnp.sum(x, axis=0, keepdims=True)


def _acc_rows(ref, part, first):
    val = jnp.broadcast_to(part, ref.shape)

    @pl.when(first)
    def _():
        ref[...] = val

    @pl.when(jnp.logical_not(first))
    def _():
        ref[...] += val


def _rms_bwd(n, g, dout):
    r = lax.rsqrt(jnp.mean(n * n, axis=-1, keepdims=True) + EPS)
    nh = n * r
    dg = dout * g
    dn = r * (dg - nh * jnp.mean(dg * nh, axis=-1, keepdims=True))
    return dn, _rsum(dout * nh)


def _const_mats():
    avg = np.kron(np.eye(4), np.full((HEAD_DIM, HEAD_DIM), 1.0 / HEAD_DIM))
    expand = np.zeros((CHUNK, SSM_WIDTH), np.float32)
    for h in range(N_HEADS):
        expand[h, h * HEAD_DIM:(h + 1) * HEAD_DIM] = 1.0
    tril = np.tril(np.ones((CHUNK, CHUNK), np.float32))
    as_bf16 = lambda a: jnp.asarray(a, dtype=BF16)
    return as_bf16(avg), as_bf16(expand), as_bf16(expand.T), as_bf16(tril), as_bf16(tril.T)


def _full(shape):
    nd = len(shape)
    return pl.BlockSpec(shape, lambda *_: (0,) * nd)


_HBM = pl.BlockSpec(memory_space=pltpu.HBM)
_SEM = pl.BlockSpec(memory_space=pltpu.SEMAPHORE)
_ALL_PEERS = tuple((k, 0) for k in range(1, N_DEV))
_SAME_CORE_PEERS = ((2, 0), (4, 0), (6, 0))
_SIBLING_FORWARD = ((1, 0), (1, 2), (1, 4), (1, 6))


def _flip(j, k):
    for bit in (4, 2, 1):
        if k & bit:
            j = j + bit - 2 * (j & bit)
    return j


def _copies(src, land, send_sems, recv_sems, hops, slots=None):
    x, y, c = lax.axis_index("x"), lax.axis_index("y"), lax.axis_index("c")
    me = 4 * x + 2 * y + c
    slots = range(len(src)) if slots is None else slots
    out = []
    for t in range(len(src)):
        for i, (k, b) in enumerate(hops):
            pos = (1 - x if k & 4 else x, 1 - y if k & 2 else y, 1 - c if k & 1 else c)
            peer = _flip(me, k)
            sem = slots[t] * len(hops) + i
            mk = functools.partial(pltpu.make_async_remote_copy, send_sem=send_sems.at[sem], recv_sem=recv_sems.at[sem],
                                   device_id=pos, device_id_type=pl.DeviceIdType.MESH)
            if land[t] is None and src[t].shape[0] != N_DEV:
                width = src[t].shape[1] // N_DEV
                slab = lambda j: src[t].at[:, pl.ds(pl.multiple_of(j * width, 128), width)]
                mine = functools.partial(mk, src_ref=slab(_flip(me, b)), dst_ref=slab(_flip(me, b)))
                theirs = functools.partial(mk, src_ref=slab(_flip(peer, b)), dst_ref=slab(_flip(peer, b)))
            elif land[t] is None:
                mine = functools.partial(mk, src_ref=src[t].at[_flip(me, b)], dst_ref=src[t].at[_flip(me, b)])
                theirs = functools.partial(mk, src_ref=src[t].at[_flip(peer, b)], dst_ref=src[t].at[_flip(peer, b)])
            else:
                assert b == 0
                mine = functools.partial(mk, src_ref=src[t].at[peer], dst_ref=land[t].at[me])
                theirs = functools.partial(mk, src_ref=src[t].at[peer], dst_ref=land[t].at[peer])
            out.append((mine, theirs))
    return out


def _exchange_start(srcs, inplace, peers, name, dep=None):
    n = len(srcs)
    lands = [None if ip else pltpu.with_memory_space_constraint(lax.empty(s.shape, s.dtype), pltpu.HBM)
             for s, ip in zip(srcs, inplace)]
    real_lands = [l for l in lands if l is not None]
    n_l = len(real_lands)
    deps = [] if dep is None else [dep]

    def body(*refs):
        src = refs[:n]
        land_refs = list(refs[n:n + n_l])
        send_sems, recv_sems = refs[n + n_l + len(deps)], refs[n + n_l + len(deps) + 1]
        token = refs[-1]
        land = [None if ip else land_refs.pop(0) for ip in inplace]
        for mine, _ in _copies(src, land, send_sems, recv_sems, peers):
            mine().start()
        token[...] = jnp.zeros_like(token)

    sem_t = pltpu.SemaphoreType.DMA((n * len(peers),))
    outs = pl.pallas_call(
        body, name=name,
        out_shape=(sem_t, sem_t) + tuple(pltpu.HBM(a.shape, a.dtype) for a in list(srcs) + real_lands)
        + (jax.ShapeDtypeStruct((8, 128), F32),),
        in_specs=[_HBM] * (n + n_l) + [pl.BlockSpec(memory_space=pl.ANY)] * len(deps),
        out_specs=(_SEM, _SEM) + (_HBM,) * (n + n_l) + (pl.BlockSpec(memory_space=pltpu.VMEM),),
        input_output_aliases={i: 2 + i for i in range(n + n_l)},
        compiler_params=pltpu.CompilerParams(has_side_effects=pltpu.SideEffectType.DATAFLOW_SIDE_EFFECTING),
    )(*[pltpu.with_memory_space_constraint(s, pltpu.HBM) for s in srcs], *real_lands, *deps)
    handle = dict(send=outs[0], recv=outs[1], srcs=outs[2:2 + n], lands=outs[2 + n:2 + n + n_l], inplace=inplace,
                  peers=peers)
    return handle, outs[-1]


def _exchange_wait(handle, after, name, only=None):
    srcs, lands, inplace, peers = handle["srcs"], handle["lands"], handle["inplace"], handle["peers"]
    slots = None
    if only is not None:
        assert all(inplace)
        slots, srcs, inplace = list(only), [srcs[t] for t in only], [True] * len(only)
    n, n_l = len(srcs), len(lands)
    after = after if isinstance(after, tuple) else (after,)

    def body(*refs):
        src = refs[:n]
        land_refs = list(refs[n:n + n_l])
        send_sems, recv_sems = refs[n + n_l], refs[n + n_l + 1]
        land = [None if ip else land_refs.pop(0) for ip in inplace]
        for mine, theirs in _copies(src, land, send_sems, recv_sems, peers, slots):
            mine().wait_send()
            theirs().wait_recv()

    outs = pl.pallas_call(
        body, name=name, out_shape=tuple(pltpu.HBM(a.shape, a.dtype) for a in list(srcs) + list(lands)),
        in_specs=[_HBM] * (n + n_l) + [_SEM, _SEM] + [pl.BlockSpec(memory_space=pl.ANY)] * len(after),
        out_specs=(_HBM,) * (n + n_l), input_output_aliases={i: i for i in range(n + n_l)},
        compiler_params=pltpu.CompilerParams(has_side_effects=pltpu.SideEffectType.DATAFLOW_SIDE_EFFECTING),
    )(*srcs, *lands, handle["send"], handle["recv"], *after)
    res, land_out = [], list(outs[n:])
    for t in range(n):
        res.append((outs[t], outs[t] if inplace[t] else land_out.pop(0)))
    return res


def _cast_to_slot(w, me, rows, name, cols=False, dep=None):
    r, cdim = w.shape[0], w.shape[-1]
    deps = [] if dep is None else [dep]

    def body(me_ref, w_ref, *rest):
        o_ref = rest[-1]
        if cols:
            o_ref[...] = w_ref[...].astype(BF16)
        else:
            o_ref[0] = w_ref[...].reshape(rows, cdim).astype(BF16)

    if cols:
        out_shape = jax.ShapeDtypeStruct((r, N_DEV * cdim), BF16)
        out_spec = pl.BlockSpec((rows, cdim), lambda i, me_ref: (i, me_ref[0]))
    else:
        out_shape = jax.ShapeDtypeStruct((N_DEV, r, cdim), BF16)
        out_spec = pl.BlockSpec((1, rows, cdim), lambda i, me_ref: (me_ref[0], i, 0))
    return pl.pallas_call(
        body, name=name, out_shape=out_shape,
        grid_spec=pltpu.PrefetchScalarGridSpec(
            num_scalar_prefetch=1, grid=(r // rows,),
            in_specs=[pl.BlockSpec((rows, cdim), lambda i, me_ref: (i, 0)) if w.ndim == 2 else
                      pl.BlockSpec((rows, 1, cdim), lambda i, me_ref: (i, 0, 0))]
            + [pl.BlockSpec(memory_space=pl.ANY)] * len(deps), out_specs=out_spec),
        compiler_params=_params("parallel"))(me, w, *deps)


def _stack_shards(blocks, rows, bn, name):
    n, r, cdim = blocks.shape

    def body(b_ref, o_ref, acc_ref):
        acc_ref[n * r:, :] = jnp.zeros((rows - n * r, bn), F32)
        for j in range(n):
            acc_ref[r * j:r * (j + 1), :] = b_ref[j].astype(F32)
        o_ref[...] = acc_ref[...].astype(BF16)

    return pl.pallas_call(
        body, name=name, grid=(cdim // bn,), out_shape=jax.ShapeDtypeStruct((rows, cdim), BF16),
        in_specs=[pl.BlockSpec((n, r, bn), lambda i: (0, 0, i))], out_specs=pl.BlockSpec((rows, bn), lambda i: (0, i)),
        scratch_shapes=[pltpu.VMEM((rows, bn), F32)], compiler_params=_params("parallel"))(blocks)


def _adamw_math(w, g, m, v):
    m = ADAM_B1 * m + (1.0 - ADAM_B1) * g
    v = ADAM_B2 * v + (1.0 - ADAM_B2) * (g * g)
    m_hat = m / (1.0 - ADAM_B1 ** ADAM_STEP)
    v_hat = v / (1.0 - ADAM_B2 ** ADAM_STEP)
    delta = -ADAM_LR * (m_hat / (jnp.sqrt(v_hat) + ADAM_EPS) + ADAM_WD * w)
    return delta, m, v


def _sum_parts(me, p_ref, own):
    g = None
    for j in range(N_DEV):
        term = (p_ref[j] if own is None else jnp.where(me == j, own, p_ref[j])).astype(F32)
        g = term if g is None else g + term
    return g


def _adamw_reduce(parts, own, me, w, m, v, name):
    r, _, cdim = w.shape

    def body(me_ref, p_ref, own_ref, w_ref, m_ref, v_ref, g_out, d_out, m_out, v_out):
        g = _sum_parts(me_ref[0], p_ref, own_ref[0]).reshape(r, 1, cdim)
        d, mn, vn = _adamw_math(w_ref[...], g, m_ref[...], v_ref[...])
        g_out[...] = g
        d_out[...] = d
        m_out[...] = mn
        v_out[...] = vn

    blk = pl.BlockSpec((r, 1, cdim), lambda i, me_ref: (0, 0, 0))
    return pl.pallas_call(
        body, name=name, out_shape=(jax.ShapeDtypeStruct(w.shape, F32),) * 4,
        grid_spec=pltpu.PrefetchScalarGridSpec(
            num_scalar_prefetch=1, grid=(1,),
            in_specs=[pl.BlockSpec((N_DEV, r, cdim), lambda i, me_ref: (0, 0, 0)),
                      pl.BlockSpec((1, r, cdim), lambda i, me_ref: (me_ref[0], 0, 0)), blk, blk, blk],
            out_specs=(blk,) * 4),
        compiler_params=_params("arbitrary"))(me, parts, own, w, m, v)


_IN_SPLITS = ((0, 512), (512, 1024), (1024, 1536), (1536, 2560), (2560, IN_PAD))


def _prenorm(x, g1, tm, dep=None):
    t_tok = x.shape[0]
    deps = [] if dep is None else [dep]

    def body(x_ref, g_ref, *rest):
        xv = x_ref[...]
        r = lax.rsqrt(jnp.mean(xv * xv, axis=-1, keepdims=True) + EPS)
        rest[-1][...] = (xv * r * g_ref[...]).astype(BF16)

    row = pl.BlockSpec((tm, D_MODEL), lambda i: (i, 0))
    return pl.pallas_call(
        body, name="prenorm", grid=(t_tok // tm,), out_shape=jax.ShapeDtypeStruct((t_tok, D_MODEL), BF16),
        in_specs=[row, _full((1, D_MODEL))] + [pl.BlockSpec(memory_space=pl.ANY)] * len(deps), out_specs=row,
        compiler_params=_params("parallel"))(x, g1, *deps)


def _in_proj(h1, w_in, tm):
    t_tok = h1.shape[0]

    def body(h_ref, w_ref, *outs):
        h = h_ref[...]
        for (a, b), o_ref in zip(_IN_SPLITS, outs):
            o_ref[...] = _dot(h, w_ref[a:b, :], _NT).astype(o_ref.dtype)

    row = lambda n: pl.BlockSpec((tm, n), lambda i: (i, 0))
    widths = [b - a for a, b in _IN_SPLITS]
    dtypes = (BF16, BF16, BF16, F32, F32)
    return pl.pallas_call(
        body, name="in_proj", grid=(t_tok // tm,),
        out_shape=tuple(jax.ShapeDtypeStruct((t_tok, n), dt) for n, dt in zip(widths, dtypes)),
        in_specs=[row(D_MODEL), _full((IN_PAD, D_MODEL))], out_specs=tuple(row(n) for n in widths),
        compiler_params=_params("parallel"))(h1, w_in)


def _lane_masks():
    lane = lax.broadcasted_iota(jnp.int32, (1, 2 * HEAD_DIM), 1)
    left = (lane < HEAD_DIM).astype(F32)
    return left, 1.0 - left


def _stack_pair(v, m_l, m_r):
    return jnp.concatenate([v * m_l, v * m_r], axis=0).astype(BF16)


def _head_mean(x, avg):
    n = avg.shape[0]
    return jnp.concatenate([_split_dot(x[:, n * i:n * (i + 1)], avg, 2) for i in range(x.shape[1] // n)], axis=1)


def _gmlp_common(u, v, lnw, lnb, avg, wcat_ref, bias, m_l, m_r):
    ug, dug = _gelu_and_grad(u)
    vg, dvg = _gelu_and_grad(v)
    mu = _head_mean(vg, avg)
    vc = vg - mu
    var = _head_mean(vc * vc, avg)
    rstd = lax.rsqrt(var + EPS)
    vhat = vc * rstd
    vn = vhat * lnw + lnb
    rows = []
    for r in range(u.shape[0] // CHUNK):
        cols = []
        for j in range(N_HEADS // 2):
            pair = vn[CHUNK * r:CHUNK * (r + 1), 128 * j:128 * (j + 1)]
            cols.append(_dot(wcat_ref[j], _stack_pair(pair, m_l, m_r)))
        rows.append(jnp.concatenate(cols, axis=1) + bias)
    mixed = jnp.concatenate(rows, axis=0)
    return ug, dug, dvg, rstd, vhat, vn, mixed


_GMLP_ROWS = 4 * CHUNK


def _gmlp_fwd(u, v, lnw, lnb, wcat, bias, avg):
    t_tok = u.shape[0]
    tm = min(_GMLP_ROWS, t_tok)

    def body(u_ref, v_ref, lnw_ref, lnb_ref, wcat_ref, bias_ref, avg_ref, o_ref):
        m_l, m_r = _lane_masks()
        ug, _, _, _, _, _, mixed = _gmlp_common(
            u_ref[...].astype(F32), v_ref[...].astype(F32), lnw_ref[...], lnb_ref[...], avg_ref[...], wcat_ref,
            bias_ref[...], m_l, m_r)
        o_ref[...] = (ug * mixed).astype(BF16)

    row = pl.BlockSpec((tm, GM_WIDTH), lambda i: (i, 0))
    return pl.pallas_call(
        body, name="gmlp_fwd", grid=(t_tok // tm,), out_shape=jax.ShapeDtypeStruct((t_tok, GM_WIDTH), BF16),
        in_specs=[row, row, _full((1, GM_WIDTH)), _full((1, GM_WIDTH)), _full(wcat.shape), _full(bias.shape),
                  _full(avg.shape)],
        out_specs=row, compiler_params=_params("parallel"))(u, v, lnw, lnb, wcat, bias, avg)


def _shift_rows(x, edge, j, down):
    groups, cols = x.shape[0] // 8, x.shape[1]
    amount = j if down else 8 - j
    rot = pltpu.roll(x.reshape(groups, 8, cols), amount, axis=1)
    edge_rot = pltpu.roll(edge, amount, axis=0)[None]
    sub = lax.broadcasted_iota(jnp.int32, (1, 8, 1), 1)
    if down:
        out = jnp.where(sub < j, jnp.concatenate([edge_rot, rot[:-1]], axis=0), rot)
    else:
        out = jnp.where(sub < 8 - j, rot, jnp.concatenate([rot[1:], edge_rot], axis=0))
    return out.reshape(x.shape)


def _conv_pre(xbc, tail, cw_ref, cb):
    taps = [_shift_rows(xbc, tail, 3 - k, True) for k in range(3)] + [xbc]
    return cb + cw_ref[0:1, :] * taps[0] + cw_ref[1:2, :] * taps[1] + cw_ref[2:3, :] * taps[2] + cw_ref[3:4, :] * taps[3]


def _ssd_common(pre, dtr, dtb, alog, expand, tril):
    q = CHUNK
    sg = jax.nn.sigmoid(pre)
    act = pre * sg
    lane = lax.broadcasted_iota(jnp.int32, (1, CHUNK), 1)
    a_row = jnp.where(lane < N_HEADS, -jnp.exp(alog), 0.0)
    dtp = dtr + dtb
    dt = _softplus(dtp)
    a_cs = _split_dot_left(tril, dt * a_row, 3)
    a_cs_t = a_cs.T
    dt_exp = _split_dot(dt, expand, 3)
    a_exp = _split_dot(a_cs, expand, 3)
    a_end = a_exp[q - 1:q, :]
    li = lax.broadcasted_iota(jnp.int32, (q, q), 0)
    si = lax.broadcasted_iota(jnp.int32, (q, q), 1)
    causal = si <= li
    decay = []
    for h in range(N_HEADS):
        seg = a_cs[:, h:h + 1] - a_cs_t[h:h + 1, :]
        decay.append(jnp.where(causal, jnp.exp(jnp.minimum(seg, 0.0)), 0.0))
    return dict(pre=pre, sg=sg, act=act, a_row=a_row, dtp=dtp, dt=dt, dt_exp=dt_exp, a_exp=a_exp,
                e=jnp.exp(a_exp), w_end=jnp.exp(a_end - a_exp), cd=jnp.exp(a_end), decay=decay)


def _ssd_specs(t_tok, seq, reverse):
    nb, nc = t_tok // seq, seq // CHUNK

    def chunk(c):
        return nc - 1 - c if reverse else c

    def row(n, col=0):
        return pl.BlockSpec((nb, CHUNK, n), lambda c: (0, chunk(c), col))

    tail = pl.BlockSpec((nb, 8, CONV_CH), lambda c: (0, jnp.maximum(chunk(c) * (CHUNK // 8) - 1, 0), 0))
    states = pl.BlockSpec((nb, 1, N_STATE, SSM_WIDTH), lambda c: (0, chunk(c), 0, 0))
    fold = lambda a: a.reshape(nb, seq, a.shape[-1])
    unfold = lambda a: a.reshape(t_tok, a.shape[-1])
    return nb, nc, row, tail, states, fold, unfold


def _ssd_fwd(z, xbc, dtr, cw, cb, dtb, alog, dskip_exp, nw, expand, tril, seq, dep=None):
    t_tok = z.shape[0]
    nb, nc, row, tail, states_spec, fold, unfold = _ssd_specs(t_tok, seq, False)

    def body(z_ref, xbc_ref, tail_ref, dtr_ref, cw_ref, cb_ref, dtb_ref, alog_ref, dsk_ref, nw_ref, exp_ref,
             tril_ref, o_ref, y_ref, st_ref, pre_ref, state_ref):
        c = pl.program_id(0)

        @pl.when(c == 0)
        def _():
            state_ref[...] = jnp.zeros_like(state_ref)

        m_l, m_r = _lane_masks()
        for s in range(nb):
            pre = _conv_pre(xbc_ref[s], jnp.where(c == 0, 0.0, tail_ref[s]), cw_ref, cb_ref[...])
            pre_ref[s] = pre
            f = _ssd_common(pre, dtr_ref[s], dtb_ref[...], alog_ref[...], exp_ref[...], tril_ref[...])
            act = f["act"]
            xs = act[:, :SSM_WIDTH]
            xdt = xs * f["dt_exp"]
            xw = xdt * f["w_end"]
            state = state_ref[s]
            st_ref[s, 0] = state
            ydiag, yoff, snew = [], [], []
            for g in range(2):
                bg = act[:, 512 + 128 * g:640 + 128 * g].astype(BF16)
                cg = act[:, 768 + 128 * g:896 + 128 * g].astype(BF16)
                cb_mat = _dot(cg, bg, _NT)
                for pr in range(2):
                    h0 = 4 * g + 2 * pr
                    gcat = jnp.concatenate(
                        [(cb_mat * f["decay"][h0]).astype(BF16), (cb_mat * f["decay"][h0 + 1]).astype(BF16)], axis=1)
                    ydiag.append(_dot(gcat, _stack_pair(xdt[:, 64 * h0:64 * h0 + 128], m_l, m_r)))
                yoff.append(_dot(cg, state[:, 256 * g:256 * (g + 1)].astype(BF16)))
                snew.append(_dot(bg, xw[:, 256 * g:256 * (g + 1)].astype(BF16), _TN))
            y = jnp.concatenate(ydiag, axis=1) + f["e"] * jnp.concatenate(yoff, axis=1) + dsk_ref[...] * xs
            state_ref[s] = state * f["cd"] + jnp.concatenate(snew, axis=1)
            y_ref[s] = y
            zv = z_ref[s].astype(F32)
            yg = y * (zv * jax.nn.sigmoid(zv))
            outs = []
            for g in range(2):
                ygg = yg[:, 256 * g:256 * (g + 1)]
                outs.append(ygg * lax.rsqrt(jnp.mean(ygg * ygg, axis=-1, keepdims=True) + EPS))
            o_ref[s] = (jnp.concatenate(outs, axis=1) * nw_ref[...]).astype(BF16)

    consts = [cw, cb, dtb, alog, dskip_exp, nw, expand, tril]
    deps = [] if dep is None else [dep]
    n_in = 4 + len(consts)

    def body_skipping_dep(*refs):
        body(*refs[:n_in], *refs[n_in + len(deps):])

    sd = lambda n, dt: jax.ShapeDtypeStruct((nb, seq, n), dt)
    o, y, states, pre = pl.pallas_call(
        body_skipping_dep, name="ssd_fwd", grid=(nc,),
        out_shape=(sd(SSM_WIDTH, BF16), sd(SSM_WIDTH, F32), jax.ShapeDtypeStruct((nb, nc, N_STATE, SSM_WIDTH), F32),
                   sd(CONV_CH, F32)),
        in_specs=[row(SSM_WIDTH), row(CONV_CH), tail, row(CHUNK)] + [_full(a.shape) for a in consts]
        + [pl.BlockSpec(memory_space=pl.ANY)] * len(deps),
        out_specs=(row(SSM_WIDTH), row(SSM_WIDTH), states_spec, row(CONV_CH)),
        scratch_shapes=[pltpu.VMEM((nb, N_STATE, SSM_WIDTH), F32)],
        compiler_params=_params("arbitrary"))(fold(z), fold(xbc), fold(xbc), fold(dtr), *consts, *deps)
    return unfold(o), unfold(y), states, unfold(pre)


def _out_proj(mix_a, mix_b, w_out, x, g2, g3, tm, dep=None):
    t_tok = x.shape[0]
    deps = [] if dep is None else [dep]

    def body(a_ref, b_ref, w_ref, x_ref, g2_ref, g3_ref, *rest):
        o_ref, x2_ref, h3_ref = rest[-3:]
        o = _dot(a_ref[...], w_ref[0:GM_WIDTH, :]) + _dot(b_ref[...], w_ref[GM_WIDTH:, :])
        o_ref[...] = o
        r2 = lax.rsqrt(jnp.mean(o * o, axis=-1, keepdims=True) + EPS)
        x2 = x_ref[...] + o * r2 * g2_ref[...]
        x2_ref[...] = x2
        r3 = lax.rsqrt(jnp.mean(x2 * x2, axis=-1, keepdims=True) + EPS)
        h3_ref[...] = (x2 * r3 * g3_ref[...]).astype(BF16)

    row = lambda n: pl.BlockSpec((tm, n), lambda i: (i, 0))
    sd = lambda dt: jax.ShapeDtypeStruct((t_tok, D_MODEL), dt)
    return pl.pallas_call(
        body, name="out_proj", grid=(t_tok // tm,), out_shape=(sd(F32), sd(F32), sd(BF16)),
        in_specs=[row(GM_WIDTH), row(SSM_WIDTH), _full((D_MODEL, D_MODEL)), row(D_MODEL), _full((1, D_MODEL)),
                  _full((1, D_MODEL))] + [pl.BlockSpec(memory_space=pl.ANY)] * len(deps),
        out_specs=(row(D_MODEL),) * 3, compiler_params=_params("parallel"))(mix_a, mix_b, w_out, x, g2, g3, *deps)


def _mlp_fwd(h3, w_up, w_down, x2, target, g4, tm, tf):
    t_tok = x2.shape[0]

    def body(h_ref, wu_ref, wd_hbm, x2_ref, t_ref, g4_ref, ra_ref, dd_ref, dy_ref, dg4_ref, loss_ref, wd_ref, sem):
        i = pl.program_id(0)
        w_down_copy = pltpu.make_async_copy(wd_hbm, wd_ref, sem.at[0])

        @pl.when(i == 0)
        def _():
            w_down_copy.start()

        hv = h_ref[...]
        for j in range(D_FF // tf):
            ra_ref[:, j * tf:(j + 1) * tf] = jnp.maximum(_dot(hv, wu_ref[:, j * tf:(j + 1) * tf]), 0.0).astype(BF16)

        @pl.when(i == 0)
        def _():
            w_down_copy.wait()

        rav = ra_ref[...]
        dvec = _dot(rav * rav, wd_ref[...])
        r4 = lax.rsqrt(jnp.mean(dvec * dvec, axis=-1, keepdims=True) + EPS)
        dn = dvec * r4
        g4 = g4_ref[...]
        err = x2_ref[...] + dn * g4 - t_ref[...]
        dy = err * (1.0 / D_MODEL)
        dy_ref[...] = dy
        dg = dy * g4
        dd_ref[...] = (r4 * (dg - dn * jnp.mean(dg * dn, axis=-1, keepdims=True))).astype(BF16)
        _acc_rows(dg4_ref, _rsum(dy * dn), i == 0)
        tile_loss = 0.5 * jnp.sum(jnp.sum(err * err, axis=-1, keepdims=True), axis=0, keepdims=True) / D_MODEL
        _acc_rows(loss_ref, jnp.broadcast_to(tile_loss, (1, 128)), i == 0)

    row = pl.BlockSpec((tm, D_MODEL), lambda i: (i, 0))
    wide = pl.BlockSpec((tm, D_FF), lambda i: (i, 0))
    w_up_once = pl.BlockSpec((D_MODEL, D_FF), lambda i: (0, 0), pipeline_mode=pl.Buffered(1))
    ra, dd, dy, dg4, loss = pl.pallas_call(
        body, name="mlp_fwd", grid=(t_tok // tm,),
        out_shape=(jax.ShapeDtypeStruct((t_tok, D_FF), BF16), jax.ShapeDtypeStruct((t_tok, D_MODEL), BF16),
                   jax.ShapeDtypeStruct((t_tok, D_MODEL), F32), jax.ShapeDtypeStruct((1, D_MODEL), F32),
                   jax.ShapeDtypeStruct((1, 128), F32)),
        in_specs=[row, w_up_once, pl.BlockSpec(memory_space=pl.ANY), row, row, _full((1, D_MODEL))],
        out_specs=(wide, row, row, _full((1, D_MODEL)), _full((1, 128))),
        scratch_shapes=[pltpu.VMEM((D_FF, D_MODEL), BF16), pltpu.SemaphoreType.DMA((1,))],
        compiler_params=_params("arbitrary"))(h3, w_up, w_down, x2, target, g4)
    return ra, dd, dy, dg4, loss


def _mlp_bwd(dd, w_down, ra, w_up, x2, dy, o, g3, g2, tm, tf):
    t_tok = x2.shape[0]
    tm = tm // 2
    tc = tf // 2

    def body(dd_ref, wd_ref, ra_ref, wu_hbm, x2_ref, dy_ref, o_ref, g3_ref, g2_ref, da_ref, dx2_ref, do_ref, dg3_ref,
             dg2_ref, wu_ref, sem):
        i = pl.program_id(0)
        w_up_copy = pltpu.make_async_copy(wu_hbm, wu_ref, sem.at[0])

        @pl.when(i == 0)
        def _():
            w_up_copy.start()

        ddv = dd_ref[...]
        for j in range(D_FF // tc):
            df = _dot(ddv, wd_ref[j * tc:(j + 1) * tc, :], _NT)
            da_ref[:, j * tc:(j + 1) * tc] = (df * (2.0 * ra_ref[:, j * tc:(j + 1) * tc].astype(F32))).astype(BF16)

        @pl.when(i == 0)
        def _():
            w_up_copy.wait()

        dh3 = _dot(da_ref[...], wu_ref[...], _NT)
        dn3, dg3 = _rms_bwd(x2_ref[...], g3_ref[...], dh3)
        dx2 = dy_ref[...] + dn3
        dx2_ref[...] = dx2
        do, dg2 = _rms_bwd(o_ref[...], g2_ref[...], dx2)
        do_ref[...] = do.astype(BF16)
        _acc_rows(dg3_ref, dg3, i == 0)
        _acc_rows(dg2_ref, dg2, i == 0)

    row = pl.BlockSpec((tm, D_MODEL), lambda i: (i, 0))
    vec = _full((1, D_MODEL))
    sd = lambda dt: jax.ShapeDtypeStruct((t_tok, D_MODEL), dt)
    wide = pl.BlockSpec((tm, D_FF), lambda i: (i, 0))
    w_down_once = pl.BlockSpec((D_FF, D_MODEL), lambda i: (0, 0), pipeline_mode=pl.Buffered(1))
    da, dx2, do, dg3, dg2 = pl.pallas_call(
        body, name="mlp_bwd", grid=(t_tok // tm,),
        out_shape=(jax.ShapeDtypeStruct((t_tok, D_FF), BF16), sd(F32), sd(BF16),
                   jax.ShapeDtypeStruct((1, D_MODEL), F32), jax.ShapeDtypeStruct((1, D_MODEL), F32)),
        in_specs=[row, w_down_once, wide, pl.BlockSpec(memory_space=pl.ANY), row, row, row, vec, vec],
        out_specs=(wide, row, row, vec, vec),
        scratch_shapes=[pltpu.VMEM((D_MODEL, D_FF), BF16), pltpu.SemaphoreType.DMA((1,))],
        compiler_params=_params("arbitrary"))(dd, w_down, ra, w_up, x2, dy, o, g3, g2)
    return da, dx2, do, dg3, dg2


def _wgrad(a, b, out_blocks, bm, bn, bk, square_a, name, dep=None):
    t_tok, m = a.shape
    n = b.shape[1]
    nk = t_tok // bk

    def body(a_ref, b_ref, *rest):
        o_ref, acc_ref = rest[-2:]
        k = pl.program_id(2)
        av = a_ref[...]
        if square_a:
            av = av * av
        part = _dot(av, b_ref[...], _TN)

        def emit(res):
            if out_blocks is None:
                o_ref[...] = res.astype(BF16)
            else:
                o_ref[0] = res.astype(BF16)

        if nk == 1:
            emit(part)
            return

        @pl.when(k == 0)
        def _():
            acc_ref[...] = part

        @pl.when(k > 0)
        def _():
            acc_ref[...] += part

        @pl.when(k == nk - 1)
        def _():
            emit(acc_ref[...])

    if out_blocks is None:
        out_shape = jax.ShapeDtypeStruct((m, n), BF16)
        out_spec = pl.BlockSpec((bm, bn), lambda i, j, k: (i, j))
    else:
        assert n // out_blocks == bn
        out_shape = jax.ShapeDtypeStruct((out_blocks, m, bn), BF16)
        out_spec = pl.BlockSpec((1, bm, bn), lambda i, j, k: (j, i, 0))
    deps = [] if dep is None else [dep]
    return pl.pallas_call(
        body, name=name, grid=(m // bm, n // bn, nk), out_shape=out_shape,
        in_specs=[pl.BlockSpec((bk, bm), lambda i, j, k: (k, i)), pl.BlockSpec((bk, bn), lambda i, j, k: (k, j))]
        + [pl.BlockSpec(memory_space=pl.ANY)] * len(deps),
        out_specs=out_spec, scratch_shapes=[pltpu.VMEM((bm, bn) if nk > 1 else (8, 128), F32)],
        compiler_params=_params("parallel", "parallel", "arbitrary"))(a, b, *deps)


def _wgrad_in_chunked(h1, pieces, bn, bk, dep=None):
    t_tok = h1.shape[0]
    nk = t_tok // bk
    shard = IN_COLS // N_DEV
    widths = [b - a for a, b in _IN_SPLITS]

    def body(h_ref, *rest):
        piece_refs = rest[:len(widths)]
        o_ref, acc_ref = rest[-2:]
        k = pl.program_id(1)
        hv = h_ref[...]
        for (a, b), r in zip(_IN_SPLITS, piece_refs):
            part = _dot(r[...], hv, _TN)

            @pl.when(k == 0)
            def _():
                acc_ref[a:b, :] = part

            @pl.when(k > 0)
            def _():
                acc_ref[a:b, :] += part

        @pl.when(k == nk - 1)
        def _():
            for j in range(N_DEV):
                o_ref[j] = acc_ref[shard * j:shard * (j + 1), :].astype(BF16)

    deps = [] if dep is None else [dep]
    return pl.pallas_call(
        body, name="wgrad_in", grid=(D_MODEL // bn, nk), out_shape=jax.ShapeDtypeStruct((N_DEV, shard, D_MODEL), BF16),
        in_specs=[pl.BlockSpec((bk, bn), lambda j, k: (k, j))] + [pl.BlockSpec((bk, n), lambda j, k: (k, 0)) for n in widths]
        + [pl.BlockSpec(memory_space=pl.ANY)] * len(deps),
        out_specs=pl.BlockSpec((N_DEV, shard, bn), lambda j, k: (0, 0, j)),
        scratch_shapes=[pltpu.VMEM((IN_PAD, bn), F32)],
        compiler_params=_params("parallel", "arbitrary"))(h1, *pieces, *deps)


def _dmix_wgrad_out(do, w_out, mix_a, mix_b, tm, dep=None):
    t_tok = do.shape[0]
    steps = t_tok // tm
    deps = [] if dep is None else [dep]

    def body(d_ref, w_ref, a_ref, b_ref, *rest):
        dm_ref, g_ref, acc_ref = rest[-3:]
        i = pl.program_id(0)
        dov = d_ref[...]
        dm_ref[...] = _dot(dov, w_ref[...], _NT).astype(BF16)
        for (lo, hi), r in zip(((0, GM_WIDTH), (GM_WIDTH, D_MODEL)), (a_ref, b_ref)):
            part = _dot(r[...], dov, _TN)

            @pl.when(i == 0)
            def _():
                acc_ref[lo:hi, :] = part

            @pl.when(i > 0)
            def _():
                acc_ref[lo:hi, :] += part

        @pl.when(i == steps - 1)
        def _():
            g_ref[...] = acc_ref[...].astype(BF16)

    row = lambda n: pl.BlockSpec((tm, n), lambda i: (i, 0))
    return pl.pallas_call(
        body, name="dmix_wgrad_out", grid=(steps,),
        out_shape=(jax.ShapeDtypeStruct((t_tok, D_MODEL), BF16), jax.ShapeDtypeStruct((D_MODEL, D_MODEL), BF16)),
        in_specs=[row(D_MODEL), _full((D_MODEL, D_MODEL)), row(GM_WIDTH), row(SSM_WIDTH)]
        + [pl.BlockSpec(memory_space=pl.ANY)] * len(deps),
        out_specs=(row(D_MODEL), _full((D_MODEL, D_MODEL))), scratch_shapes=[pltpu.VMEM((D_MODEL, D_MODEL), F32)],
        compiler_params=_params("arbitrary"))(do, w_out, mix_a, mix_b, *deps)


def _gmlp_bwd(dmix, u, v, lnw, lnb, wcat, wtcat, bias, avg, expand_t):
    t_tok = u.shape[0]
    tm = min(_GMLP_ROWS, t_tok)

    def body(dm_ref, u_ref, v_ref, lnw_ref, lnb_ref, wcat_ref, wtcat_ref, bias_ref, avg_ref, expt_ref, du_ref, dv_ref,
             dw_ref, db_ref, dlnw_ref, dlnb_ref):
        i = pl.program_id(0)
        m_l, m_r = _lane_masks()
        avg = avg_ref[...]
        lnw = lnw_ref[...]
        ug, dug, dvg, rstd, vhat, vn, mixed = _gmlp_common(
            u_ref[...].astype(F32), v_ref[...].astype(F32), lnw, lnb_ref[...], avg, wcat_ref, bias_ref[...], m_l, m_r)
        dya = dm_ref[...].astype(F32)
        du_ref[...] = (dya * mixed * dug).astype(BF16)
        dmixed = dya * ug
        dvn_rows, dws, dbt = [], [None] * N_HEADS, None
        for r in range(tm // CHUNK):
            dvn_cols = []
            for j in range(N_HEADS // 2):
                dmp = dmixed[CHUNK * r:CHUNK * (r + 1), 128 * j:128 * (j + 1)]
                dvn_cols.append(_dot(wtcat_ref[j], _stack_pair(dmp, m_l, m_r)))
                vnp = vn[CHUNK * r:CHUNK * (r + 1), 128 * j:128 * (j + 1)].astype(BF16)
                for i_h, mask in enumerate((m_l, m_r)):
                    part = _dot((dmp * mask).astype(BF16), vnp, _NT)
                    dws[2 * j + i_h] = part if r == 0 else dws[2 * j + i_h] + part
            dvn_rows.append(jnp.concatenate(dvn_cols, axis=1))
            part = _split_dot(dmixed[CHUNK * r:CHUNK * (r + 1), :], expt_ref[...], 2)
            dbt = part if r == 0 else dbt + part
        dvn = jnp.concatenate(dvn_rows, axis=0)
        dvh = dvn * lnw
        dvgel = rstd * (dvh - _head_mean(dvh, avg) - vhat * _head_mean(dvh * vhat, avg))
        dv_ref[...] = (dvgel * dvg).astype(BF16)
        first = i == 0

        @pl.when(first)
        def _():
            for h in range(N_HEADS):
                dw_ref[h] = dws[h]
            db_ref[...] = dbt

        @pl.when(jnp.logical_not(first))
        def _():
            for h in range(N_HEADS):
                dw_ref[h] += dws[h]
            db_ref[...] += dbt

        _acc_rows(dlnw_ref, _rsum(dvn * vhat), first)
        _acc_rows(dlnb_ref, _rsum(dvn), first)

    row = pl.BlockSpec((tm, GM_WIDTH), lambda i: (i, 0))
    consts = [lnw, lnb, wcat, wtcat, bias, avg, expand_t]
    return pl.pallas_call(
        body, name="gmlp_bwd", grid=(t_tok // tm,),
        out_shape=(jax.ShapeDtypeStruct((t_tok, GM_WIDTH), BF16), jax.ShapeDtypeStruct((t_tok, GM_WIDTH), BF16),
                   jax.ShapeDtypeStruct((N_HEADS, CHUNK, CHUNK), F32), jax.ShapeDtypeStruct((CHUNK, CHUNK), F32),
                   jax.ShapeDtypeStruct((1, GM_WIDTH), F32), jax.ShapeDtypeStruct((1, GM_WIDTH), F32)),
        in_specs=[row, row, row] + [_full(a.shape) for a in consts],
        out_specs=(row, row, _full((N_HEADS, CHUNK, CHUNK)), _full((CHUNK, CHUNK)), _full((1, GM_WIDTH)),
                   _full((1, GM_WIDTH))),
        compiler_params=_params("arbitrary"))(dmix, u, v, *consts)


def _ssd_bwd(dmix, z, xbc, pre, dtr, y, states, cw, cb, dtb, alog, dskip_exp, nw, expand, expand_t, tril, triu, seq,
             dep=None):
    t_tok = z.shape[0]
    nb, nc, row, _, states_spec, fold, unfold = _ssd_specs(t_tok, seq, True)
    q = CHUNK

    def one_sequence(s, dm_ref, z_ref, xbc_ref, pre_ref, dtr_ref, y_ref, st_ref, cw_ref, dtb_ref, alog_ref, dsk_ref,
                     nw_ref, exp_ref, expt_ref, tril_ref, triu_ref, dz_ref, dxbc_ref, ddt_ref, dhead_ref, dstate_ref):
        m_l, m_r = _lane_masks()
        expt = expt_ref[...]
        f = _ssd_common(pre_ref[s], dtr_ref[s], dtb_ref[...], alog_ref[...], exp_ref[...], tril_ref[...])
        act, pre, sg = f["act"], f["pre"], f["sg"]
        xs = act[:, :SSM_WIDTH]
        xdt = xs * f["dt_exp"]
        xw = xdt * f["w_end"]
        state = st_ref[s, 0]
        dstate = dstate_ref[s]
        zv, yv, dout, nw = z_ref[s].astype(F32), y_ref[s], dm_ref[s].astype(F32), nw_ref[...]
        sz = jax.nn.sigmoid(zv)
        sl = zv * sz
        yg = yv * sl
        tv = dout * nw
        dyg_parts, ygh_parts = [], []
        for g in range(2):
            ygg = yg[:, 256 * g:256 * (g + 1)]
            rr = lax.rsqrt(jnp.mean(ygg * ygg, axis=-1, keepdims=True) + EPS)
            ygh = ygg * rr
            tg = tv[:, 256 * g:256 * (g + 1)]
            dyg_parts.append(rr * (tg - ygh * jnp.mean(tg * ygh, axis=-1, keepdims=True)))
            ygh_parts.append(ygh)
        dyg = jnp.concatenate(dyg_parts, axis=1)
        dnw = _rsum(dout * jnp.concatenate(ygh_parts, axis=1))
        dy = dyg * sl
        dz_ref[s] = (dyg * yv * (sz * (1.0 + zv * (1.0 - sz)))).astype(BF16)
        ddsk = _rsum(dy * xs)
        dye = dy * f["e"]
        lane = lax.broadcasted_iota(jnp.int32, (q, q), 1)
        sub = lax.broadcasted_iota(jnp.int32, (q, q), 0)
        rs_mat = jnp.zeros((q, q), F32)
        cs_mat = jnp.zeros((q, q), F32)
        dxdt_cols, yoff, dst_in, dxw, d_b, d_c = [], [], [], [], [], []
        for g in range(2):
            bg = act[:, 512 + 128 * g:640 + 128 * g].astype(BF16)
            cg = act[:, 768 + 128 * g:896 + 128 * g].astype(BF16)
            cb_mat = _dot(cg, bg, _NT)
            stg = state[:, 256 * g:256 * (g + 1)].astype(BF16)
            dyeg = dye[:, 256 * g:256 * (g + 1)].astype(BF16)
            yoff.append(_dot(cg, stg))
            dcg = _dot(dyeg, stg, _NT)
            dst_in.append(_dot(cg, dyeg, _TN))
            dcb = jnp.zeros((q, q), F32)
            for pr in range(2):
                h0 = 4 * g + 2 * pr
                gf = [cb_mat * f["decay"][h0], cb_mat * f["decay"][h0 + 1]]
                gcat = jnp.concatenate([gf[0].astype(BF16), gf[1].astype(BF16)], axis=1)
                xst = _stack_pair(xdt[:, 64 * h0:64 * h0 + 128], m_l, m_r)
                dyp = dy[:, 64 * h0:64 * h0 + 128].astype(BF16)
                dgcat = _dot(dyp, xst, _NT)
                dxst = _dot(gcat, dyp, _TN)
                dxdt_cols.append(dxst[:q] * m_l + dxst[q:] * m_r)
                for i in range(2):
                    h = h0 + i
                    dg = dgcat[:, q * i:q * (i + 1)]
                    mm = dg * gf[i]
                    rs_mat = rs_mat + jnp.where(lane == h, jnp.sum(mm, axis=1, keepdims=True), 0.0)
                    cs_mat = cs_mat + jnp.where(sub == h, jnp.sum(mm, axis=0, keepdims=True), 0.0)
                    dcb = dcb + dg * f["decay"][h]
            dcb16 = dcb.astype(BF16)
            dstg = dstate[:, 256 * g:256 * (g + 1)].astype(BF16)
            d_c.append(dcg + _dot(dcb16, bg))
            dxw.append(_dot(bg, dstg))
            d_b.append(_dot(dcb16, cg, _TN) + _dot(xw[:, 256 * g:256 * (g + 1)].astype(BF16), dstg, _NT))
        dxw = jnp.concatenate(dxw, axis=1)
        dxdt = jnp.concatenate(dxdt_cols, axis=1) + dxw * f["w_end"]
        qv = dxw * xw
        end_row = _rsum(qv) + _rsum(dstate * state) * f["cd"]
        x2 = dye * jnp.concatenate(yoff, axis=1) - qv
        row_i = lax.broadcasted_iota(jnp.int32, (q, 1), 0)
        x2 = x2 + jnp.where(row_i == q - 1, end_row, 0.0)
        da_cs = _split_dot(x2, expt, 2) + rs_mat - cs_mat.T
        ddt = _split_dot(dxdt * xs, expt, 2)
        dxs = dsk_ref[...] * dy + dxdt * f["dt_exp"]
        dda = _split_dot_left(triu_ref[...], da_cs, 3)
        ddt = ddt + dda * f["a_row"]
        dalog = _rsum(dda * f["dt"]) * f["a_row"]
        draw = ddt * jax.nn.sigmoid(f["dtp"])
        ddt_ref[s] = draw.astype(BF16)
        dact = jnp.concatenate([dxs] + d_b + d_c, axis=1)
        dpre = dact * (sg * (1.0 + pre * (1.0 - sg)))
        dhead = dhead_ref[s]
        xv = xbc_ref[s]
        shifted = [_shift_rows(dpre, dhead, 3 - k, False) for k in range(3)] + [dpre]
        dxbc = cw_ref[3:4, :] * dpre
        for k in range(3):
            dxbc = dxbc + cw_ref[k:k + 1, :] * shifted[k]
        dxbc_ref[s] = dxbc.astype(BF16)
        dhead_ref[s] = dpre[0:8, :]
        dstate_ref[s] = dstate * f["cd"] + jnp.concatenate(dst_in, axis=1)
        row8 = lax.broadcasted_iota(jnp.int32, (8, 1), 0)
        dcw = jnp.zeros((8, CONV_CH), F32)
        for k in range(4):
            dcw = dcw + jnp.where(row8 == k, _rsum(shifted[k] * xv), 0.0)
        return dcw, _rsum(dpre), _rsum(draw), dalog, _split_dot(ddsk, expt, 3), dnw

    def body(dm_ref, z_ref, xbc_ref, pre_ref, dtr_ref, y_ref, st_ref, cw_ref, cb_ref, dtb_ref, alog_ref, dsk_ref,
             nw_ref, exp_ref, expt_ref, tril_ref, triu_ref, dz_ref, dxbc_ref, ddt_ref, dcw_ref, dcb_ref, ddtb_ref,
             dalog_ref, dd_ref, dnw_ref, dhead_ref, dstate_ref):
        c = pl.program_id(0)
        first = c == 0

        @pl.when(first)
        def _():
            dstate_ref[...] = jnp.zeros_like(dstate_ref)
            dhead_ref[...] = jnp.zeros_like(dhead_ref)

        total = None
        for s in range(nb):
            parts = one_sequence(s, dm_ref, z_ref, xbc_ref, pre_ref, dtr_ref, y_ref, st_ref, cw_ref, dtb_ref, alog_ref,
                                 dsk_ref, nw_ref, exp_ref, expt_ref, tril_ref, triu_ref, dz_ref, dxbc_ref, ddt_ref,
                                 dhead_ref, dstate_ref)
            total = parts if total is None else tuple(a + b for a, b in zip(total, parts))
        dcw = total[0]

        @pl.when(first)
        def _():
            dcw_ref[...] = dcw

        @pl.when(jnp.logical_not(first))
        def _():
            dcw_ref[...] += dcw

        for ref, part in zip((dcb_ref, ddtb_ref, dalog_ref, dd_ref, dnw_ref), total[1:]):
            _acc_rows(ref, part, first)

    consts = [cw, cb, dtb, alog, dskip_exp, nw, expand, expand_t, tril, triu]
    deps = [] if dep is None else [dep]
    n_in = 7 + len(consts)

    def body_skipping_dep(*refs):
        body(*refs[:n_in], *refs[n_in + len(deps):])

    acc = lambda n: jax.ShapeDtypeStruct((1, n), F32)
    sd = lambda n: jax.ShapeDtypeStruct((nb, seq, n), BF16)
    dz, dxbc, ddt, *small_grads = pl.pallas_call(
        body_skipping_dep, name="ssd_bwd", grid=(nc,),
        out_shape=(sd(SSM_WIDTH), sd(CONV_CH), sd(CHUNK), jax.ShapeDtypeStruct((8, CONV_CH), F32), acc(CONV_CH),
                   acc(CHUNK), acc(CHUNK), acc(CHUNK), acc(SSM_WIDTH)),
        in_specs=[row(SSM_WIDTH, col=1), row(SSM_WIDTH), row(CONV_CH), row(CONV_CH), row(CHUNK), row(SSM_WIDTH),
                  states_spec]
        + [_full(a.shape) for a in consts] + [pl.BlockSpec(memory_space=pl.ANY)] * len(deps),
        out_specs=(row(SSM_WIDTH), row(CONV_CH), row(CHUNK), _full((8, CONV_CH)), _full((1, CONV_CH)),
                   _full((1, CHUNK)), _full((1, CHUNK)), _full((1, CHUNK)), _full((1, SSM_WIDTH))),
        scratch_shapes=[pltpu.VMEM((nb, 8, CONV_CH), F32), pltpu.VMEM((nb, N_STATE, SSM_WIDTH), F32)],
        compiler_params=_params("arbitrary"))(
            fold(dmix), fold(z), fold(xbc), fold(pre), fold(dtr), fold(y), states, *consts, *deps)
    return (unfold(dz), unfold(dxbc), unfold(ddt), *small_grads)


def _in_bwd(du, dv, dz, dxbc, ddt, w_in, x, dx2, g1, tm, me, riders=(), dep=None):
    t_tok = x.shape[0]
    steps = t_tok // tm

    n_in = [5 + ("mask" in rd) for rd in riders]
    first_in = [sum(n_in[:r]) for r in range(len(riders))]

    def body(me_ref, du_ref, dv_ref, dz_ref, dxbc_ref, ddt_ref, w_ref, x_ref, dx2_ref, g_ref, *rest):
        outs = rest[len(rest) - 2 - 4 * len(riders):]
        gx_ref, dg_ref = outs[:2]
        i = pl.program_id(0)
        dh = None
        for (a, b), ref in zip(_IN_SPLITS, (du_ref, dv_ref, dz_ref, dxbc_ref, ddt_ref)):
            part = _dot(ref[...], w_ref[a:b, :])
            dh = part if dh is None else dh + part
        dn, dg = _rms_bwd(x_ref[...], g_ref[...], dh)
        gx_ref[...] = dx2_ref[...] + dn
        _acc_rows(dg_ref, dg, i == 0)
        for r in range(len(riders)):
            p_ref, own_ref, w_ref_r, m_ref_r, v_ref_r = rest[first_in[r]:first_in[r] + 5]
            g = _sum_parts(me_ref[0], p_ref, own_ref[0])
            if n_in[r] == 6:
                g = g * rest[first_in[r] + 5][...]
            d, mn, vn = _adamw_math(w_ref_r[...], g, m_ref_r[...], v_ref_r[...])
            for o_ref, val in zip(outs[2 + 4 * r:6 + 4 * r], (g, d, mn, vn)):
                o_ref[...] = val

    row = lambda n: pl.BlockSpec((tm, n), lambda i, me_ref: (i, 0))
    whole = lambda shape: pl.BlockSpec(shape, lambda i, me_ref: (0,) * len(shape))
    widths = [b - a for a, b in _IN_SPLITS]
    deps = [] if dep is None else [dep]
    rider_args, rider_specs, rider_out_shapes, rider_out_specs = [], [], [], []
    for rd in riders:
        rows, cols = rd["w"].shape[0] // steps, rd["w"].shape[1]
        blk = pl.BlockSpec((rows, cols), lambda i, me_ref: (i, 0))
        rider_args += [rd["parts"], rd["own"], rd["w"], rd["m"], rd["v"]]
        rider_specs += [pl.BlockSpec((N_DEV, rows, cols), lambda i, me_ref: (0, i, 0)),
                        pl.BlockSpec((1, rows, cols), lambda i, me_ref: (me_ref[0], i, 0)), blk, blk, blk]
        if "mask" in rd:
            rider_args.append(rd["mask"])
            rider_specs.append(whole((rows, cols)))
        rider_out_shapes += [jax.ShapeDtypeStruct(rd["w"].shape, F32)] * 4
        rider_out_specs += [blk] * 4
    outs = pl.pallas_call(
        body, name="in_bwd",
        out_shape=(jax.ShapeDtypeStruct((t_tok, D_MODEL), F32), jax.ShapeDtypeStruct((1, D_MODEL), F32),
                   *rider_out_shapes),
        grid_spec=pltpu.PrefetchScalarGridSpec(
            num_scalar_prefetch=1, grid=(steps,),
            in_specs=[row(n) for n in widths] + [whole((IN_PAD, D_MODEL)), row(D_MODEL), row(D_MODEL),
                                                 whole((1, D_MODEL))] + rider_specs
            + [pl.BlockSpec(memory_space=pl.ANY)] * len(deps),
            out_specs=(row(D_MODEL), whole((1, D_MODEL)), *rider_out_specs)),
        compiler_params=_params("arbitrary"))(me, du, dv, dz, dxbc, ddt, w_in, x, dx2, g1, *rider_args, *deps)
    return outs[0], outs[1], [tuple(outs[2 + 4 * r:6 + 4 * r]) for r in range(len(riders))]


def _pad_lanes(a, n):
    return jnp.pad(a, ((0, 0), (0, n - a.shape[1])))


def _local_step(x, target, seq, small, hooks, first_dep=None):
    t_tok = x.shape[0]
    tm = min(TOKEN_TILE, t_tok)
    avg, expand, expand_t, tril, triu = _const_mats()
    g1, g2, g3, g4 = (small[k].reshape(1, D_MODEL) for k in
                      ("norm_mix_pre", "norm_mix_post", "norm_ffn_pre", "norm_ffn_post"))
    tie = (lambda a: a) if first_dep is None else (lambda a: a + first_dep[0, 0])
    lnw = tie(small["gm_ln_w"]).reshape(1, GM_WIDTH)
    lnb = tie(small["gm_ln_b"]).reshape(1, GM_WIDTH)
    causal = jnp.tril(jnp.ones((CHUNK, CHUNK), F32))
    wm = tie(small["gm_w_s"]) * causal
    pair = lambda w: w.reshape(4, 2, CHUNK, CHUNK).transpose(0, 2, 1, 3).reshape(4, CHUNK, 2 * CHUNK).astype(BF16)
    wcat = pair(wm)
    wtcat = pair(jnp.swapaxes(wm, 1, 2))
    bias = jnp.repeat(tie(small["gm_b_s"]).T, HEAD_DIM, axis=1)
    cb = small["conv_b"].reshape(1, CONV_CH)
    dtb = _pad_lanes(tie(small["dt_bias"]).reshape(1, N_HEADS), CHUNK)
    alog = _pad_lanes(tie(small["a_log"]).reshape(1, N_HEADS), CHUNK)
    dskip_exp = jnp.repeat(tie(small["d_skip"]).reshape(1, N_HEADS), HEAD_DIM, axis=1)
    nw = small["ssm_norm_w"].reshape(1, SSM_WIDTH)

    h1 = _prenorm(x, g1, tm, hooks.get("prenorm_after", first_dep))
    w_in_t, conv_w = hooks["mixer_weights"]((h1, lnw, lnb, wcat, wtcat, bias, dtb, alog, dskip_exp))
    tall = min(2 * tm, t_tok)
    u, v, z, xbc, dtr = _in_proj(h1, w_in_t, tall)
    mix_a = _gmlp_fwd(u, v, lnw, lnb, wcat, bias, avg)
    dep = hooks["gmlp_done"](mix_a) if "gmlp_done" in hooks else None
    mix_b, y_pre, states, pre = _ssd_fwd(z, xbc, dtr, conv_w, cb, dtb, alog, dskip_exp, nw, expand, tril, seq, dep)
    w_out, dep = hooks["mixers_done"](mix_b)
    o, x2, h3 = _out_proj(mix_a, mix_b, w_out, x, g2, g3, tall, dep)
    w_up, w_down = hooks["mlp_weights"](h3)
    tf = FF_TILE
    ra, dd, dy, dg4, loss = _mlp_fwd(h3, w_up, w_down, x2, target, g4, tm, tf)

    da, dx2, do, dg3, dg2 = _mlp_bwd(dd, w_down, ra, w_up, x2, dy, o, g3, g2, tm, tf)
    g_w_down = _wgrad(ra, dd, None, WGRAD_TILE, D_MODEL, t_tok, True, "wgrad_down")
    g_w_up = _wgrad(h3, da, N_DEV, D_MODEL, D_FF // N_DEV, t_tok, False, "wgrad_up")
    dep = hooks["mlp_grads"](g_w_down, g_w_up)
    dmix, g_w_out = _dmix_wgrad_out(do, w_out, mix_a, mix_b, tall, dep)
    du, dv, dws, dbt, dlnw, dlnb = _gmlp_bwd(dmix, u, v, lnw, lnb, wcat, wtcat, bias, avg, expand_t)
    dep = hooks["gmlp_grads"](g_w_out, dws)
    dz, dxbc, ddt, dcw, dcb, ddtb, dalog, ddsk, dnw = _ssd_bwd(
        dmix, z, xbc, pre, dtr, y_pre, states, conv_w, cb, dtb, alog, dskip_exp, nw, expand, expand_t, tril, triu, seq,
        dep)
    g_w_in = _wgrad_in_chunked(h1, (du, dv, dz, dxbc, ddt), WGRAD_TILE, t_tok // 2, dep)
    dep = hooks["in_grads"](g_w_in, dcw[0:4])
    riders = hooks["arrived_updates"](dep) if "arrived_updates" in hooks else []
    me = hooks.get("me", jnp.zeros((1,), jnp.int32))
    grad_x, dg1, updates = _in_bwd(du, dv, dz, dxbc, ddt, w_in_t, x, dx2, g1, tm, me, riders, dep)

    grads = dict(
        updates=updates,
        w_in=g_w_in, w_out=g_w_out, w_up=g_w_up, w_down=g_w_down, conv_w=dcw[0:4],
        norm_mix_pre=dg1, norm_mix_post=dg2, norm_ffn_pre=dg3, norm_ffn_post=dg4, gm_ln_w=dlnw, gm_ln_b=dlnb,
        gm_w_s=dws, gm_b_s=dbt, conv_b=dcb, dt_bias=ddtb, a_log=dalog, d_skip=ddsk, ssm_norm_w=dnw)
    return loss[0, 0], grad_x, grads


_WEIGHTS = ("norm_mix_pre", "w_in", "gm_ln_w", "gm_ln_b", "gm_w_s", "gm_b_s", "conv_w", "conv_b", "dt_bias", "a_log",
            "d_skip", "ssm_norm_w", "w_out", "norm_mix_post", "norm_ffn_pre", "w_up", "w_down", "norm_ffn_post")
_SLAB_ROWS = (("norm_mix_pre", 1024), ("norm_mix_post", 1024), ("norm_ffn_pre", 1024), ("norm_ffn_post", 1024),
              ("conv_b", 1024), ("ssm_norm_w", 512), ("gm_ln_w", 512), ("gm_ln_b", 512), ("dt_bias", 8), ("a_log", 8),
              ("d_skip", 8))
_SLAB_LOSS_ROW = len(_SLAB_ROWS)
_SLAB_BS_ROW = 16
_SMALL_PARAMS = tuple(name for name, _ in _SLAB_ROWS) + ("gm_b_s",)
_LN_PARAMS = ("gm_ln_w", "gm_ln_b")


_SLAB_CONV_ROW = _SLAB_LOSS_ROW + 1


def _pack_slab(g, loss_part):
    rows = [_pad_lanes(g[name], D_MODEL) for name, _ in _SLAB_ROWS]
    rows.append(jnp.broadcast_to(loss_part, (1, D_MODEL)))
    rows.append(g["conv_w"])
    assert sum(r.shape[0] for r in rows) == _SLAB_BS_ROW
    rows.append(_pad_lanes(g["gm_b_s"].T[0:N_HEADS], D_MODEL))
    return jnp.concatenate(rows, axis=0)


def _adamw_slab(parts, me, w, m, v):
    names = _SMALL_PARAMS + ("conv_w",)
    shapes = [w[k].shape for k in names]
    unfold = np.zeros((GM_WIDTH, HEAD_DIM), np.float32)
    for h in range(N_HEADS):
        unfold[h * HEAD_DIM:(h + 1) * HEAD_DIM, :] = np.eye(HEAD_DIM)
    unfold = jnp.asarray(unfold, dtype=BF16)
    n = len(names)
    shard = CONV_CH // N_DEV

    def body(me_ref, p_ref, unfold_ref, *refs):
        w_refs, m_refs, v_refs = refs[:n], refs[n:2 * n], refs[2 * n:3 * n]
        outs = refs[3 * n:]
        g_all = p_ref[0]
        for j in range(1, N_DEV):
            g_all = g_all + p_ref[j]
        lane = lax.broadcasted_iota(jnp.int32, (N_HEADS, GM_WIDTH), 1)
        head = lax.broadcasted_iota(jnp.int32, (N_HEADS, GM_WIDTH), 0)
        own_lanes = jnp.logical_and(lane >= head * HEAD_DIM, lane < (head + 1) * HEAD_DIM)
        mine = pl.ds(pl.multiple_of(me_ref[0] * shard, shard), shard)
        for i, name in enumerate(names):
            if name == "gm_b_s":
                g = g_all[_SLAB_BS_ROW:_SLAB_BS_ROW + N_HEADS, 0:CHUNK]
            elif name == "conv_w":
                g = p_ref[0, _SLAB_CONV_ROW:_SLAB_CONV_ROW + 4, mine]
                for j in range(1, N_DEV):
                    g = g + p_ref[j, _SLAB_CONV_ROW:_SLAB_CONV_ROW + 4, mine]
            else:
                row = [r for r, (k, _) in enumerate(_SLAB_ROWS) if k == name][0]
                g = g_all[row:row + 1, 0:dict(_SLAB_ROWS)[name]]
                if name in _LN_PARAMS:
                    g = _split_dot(jnp.where(own_lanes, g, 0.0), unfold_ref[...], 3)
            d, mn, vn = _adamw_math(w_refs[i][...], g, m_refs[i][...], v_refs[i][...])
            for o_ref, val in zip(outs[4 * i:4 * i + 4], (g, d, mn, vn)):
                o_ref[...] = val
        outs[-1][...] = g_all[_SLAB_LOSS_ROW:_SLAB_LOSS_ROW + 1, 0:128]

    def whole(shape):
        nd = len(shape)
        return pl.BlockSpec(shape, lambda i, me_ref: (0,) * nd)

    ins = [parts, unfold] + [d[k] for d in (w, m, v) for k in names]
    out_shape = tuple(jax.ShapeDtypeStruct(s, F32) for s in shapes for _ in range(4)) + (
        jax.ShapeDtypeStruct((1, 128), F32),)
    outs = pl.pallas_call(
        body, name="adamw_small", out_shape=out_shape,
        grid_spec=pltpu.PrefetchScalarGridSpec(
            num_scalar_prefetch=1, grid=(1,), in_specs=[whole(a.shape) for a in ins],
            out_specs=tuple(whole(s.shape) for s in out_shape)),
        compiler_params=_params("arbitrary"))(me, *ins)
    return {k: tuple(outs[4 * i:4 * i + 4]) for i, k in enumerate(names)}, outs[-1][0, 0]


def kernel(x, norm_mix_pre, w_in, gm_ln_w, gm_ln_b, gm_w_s, gm_b_s, conv_w, conv_b, dt_bias, a_log, d_skip, ssm_norm_w, w_out, norm_mix_post, norm_ffn_pre, w_up, w_down, norm_ffn_post, loss_target, m_norm_mix_pre, m_w_in, m_gm_ln_w, m_gm_ln_b, m_gm_w_s, m_gm_b_s, m_conv_w, m_conv_b, m_dt_bias, m_a_log, m_d_skip, m_ssm_norm_w, m_w_out, m_norm_mix_post, m_norm_ffn_pre, m_w_up, m_w_down, m_norm_ffn_post, v_norm_mix_pre, v_w_in, v_gm_ln_w, v_gm_ln_b, v_gm_w_s, v_gm_b_s, v_conv_w, v_conv_b, v_dt_bias, v_a_log, v_d_skip, v_ssm_norm_w, v_w_out, v_norm_mix_post, v_norm_ffn_pre, v_w_up, v_w_down, v_norm_ffn_post):
    w = dict(norm_mix_pre=norm_mix_pre, w_in=w_in, gm_ln_w=gm_ln_w, gm_ln_b=gm_ln_b, gm_w_s=gm_w_s, gm_b_s=gm_b_s, conv_w=conv_w, conv_b=conv_b, dt_bias=dt_bias, a_log=a_log, d_skip=d_skip, ssm_norm_w=ssm_norm_w, w_out=w_out, norm_mix_post=norm_mix_post, norm_ffn_pre=norm_ffn_pre, w_up=w_up, w_down=w_down, norm_ffn_post=norm_ffn_post)
    m = dict(norm_mix_pre=m_norm_mix_pre, w_in=m_w_in, gm_ln_w=m_gm_ln_w, gm_ln_b=m_gm_ln_b, gm_w_s=m_gm_w_s, gm_b_s=m_gm_b_s, conv_w=m_conv_w, conv_b=m_conv_b, dt_bias=m_dt_bias, a_log=m_a_log, d_skip=m_d_skip, ssm_norm_w=m_ssm_norm_w, w_out=m_w_out, norm_mix_post=m_norm_mix_post, norm_ffn_pre=m_norm_ffn_pre, w_up=m_w_up, w_down=m_w_down, norm_ffn_post=m_norm_ffn_post)
    v = dict(norm_mix_pre=v_norm_mix_pre, w_in=v_w_in, gm_ln_w=v_gm_ln_w, gm_ln_b=v_gm_ln_b, gm_w_s=v_gm_w_s, gm_b_s=v_gm_b_s, conv_w=v_conv_w, conv_b=v_conv_b, dt_bias=v_dt_bias, a_log=v_a_log, d_skip=v_d_skip, ssm_norm_w=v_ssm_norm_w, w_out=v_w_out, norm_mix_post=v_norm_mix_post, norm_ffn_pre=v_norm_ffn_pre, w_up=v_w_up, w_down=v_w_down, norm_ffn_post=v_norm_ffn_post)
    n_batch, seq, _ = x.shape
    shard_in = IN_COLS // N_DEV

    me = (4 * lax.axis_index("x") + 2 * lax.axis_index("y") + lax.axis_index("c")).astype(jnp.int32).reshape(1)

    def in_slot(own):
        return lax.dynamic_update_slice(lax.empty((N_DEV,) + own.shape, own.dtype), own[None],
                                        (me[0],) + (0,) * own.ndim)

    lying = lambda t: jnp.transpose(t, (2, 0, 1))
    first = [_cast_to_slot(lying(w_in), me, shard_in, "cast_w_in"), in_slot(conv_w[0])]
    ici_1, tok_ici_1 = _exchange_start(first, [True] * 2, _SAME_CORE_PEERS, "gather_mix_ici_start")
    cast_out = _cast_to_slot(w_out[0], me, 128, "cast_w_out", dep=tok_ici_1)
    cast_up = _cast_to_slot(w_up[0], me, 1024, "cast_w_up", cols=True, dep=cast_out)
    second = [cast_out, cast_up, _cast_to_slot(w_down[0], me, 512, "cast_w_down", dep=cast_up)]
    gathering = {}

    def mixer_weights(after):
        bufs = [buf for buf, _ in _exchange_wait(ici_1, after, "gather_mix_ici_wait")]
        d2d_1, tok_d2d_1 = _exchange_start(bufs, [True] * 2, _SIBLING_FORWARD, "gather_mix_d2d_start")
        gathering["late_ici"], tok_ici_2 = _exchange_start(
            second, [True] * 3, _SAME_CORE_PEERS, "gather_late_ici_start", dep=tok_d2d_1)
        (_, ag_in), (_, ag_conv) = _exchange_wait(d2d_1, tok_ici_2, "gather_mix_d2d_wait")
        w_in_t = _stack_shards(ag_in, IN_PAD, STACK_TILE, "stack_w_in")
        return w_in_t, ag_conv.transpose(1, 0, 2).reshape(4, CONV_CH)

    def gmlp_done(after):
        ((buf, _),) = _exchange_wait(gathering["late_ici"], after, "gather_out_ici_wait", only=(0,))
        gathering["out"], tok = _exchange_start([buf], [True], _SIBLING_FORWARD, "gather_out_d2d_start")
        return tok

    def mixers_done(after):
        bufs = [buf for buf, _ in _exchange_wait(gathering["late_ici"], after, "gather_mlp_ici_wait", only=(1, 2))]
        gathering["mlp"], tok = _exchange_start(bufs, [True] * 2, _SIBLING_FORWARD, "gather_mlp_d2d_start")
        ((_, ag_out),) = _exchange_wait(gathering["out"], tok, "gather_out_d2d_wait")
        return ag_out.reshape(D_MODEL, D_MODEL), tok

    def mlp_weights(after):
        (_, ag_up), (_, ag_down) = _exchange_wait(gathering["mlp"], after, "gather_mlp_d2d_wait")
        return ag_up, ag_down.reshape(D_FF, D_MODEL)

    sent = {}

    def mlp_grads(g_w_down, g_w_up):
        sent["mlp"], tok = _exchange_start(
            [g_w_down.reshape(N_DEV, D_FF // N_DEV, D_MODEL), g_w_up], [False, False], _ALL_PEERS, "grads_mlp_start")
        return tok

    def gmlp_grads(g_w_out, g_w_s):
        sent["gmlp"], tok = _exchange_start(
            [g_w_out.reshape(N_DEV, D_MODEL // N_DEV, D_MODEL), in_slot(g_w_s.astype(BF16))], [False, True], _ALL_PEERS,
            "grads_gmlp_start")
        return tok

    def in_grads(g_w_in_t, g_conv_w):
        sent["in"], tok = _exchange_start([g_w_in_t], [False], _ALL_PEERS, "grads_in_start")
        return tok

    def arrived_updates(after):
        (own_down, p_down), (own_up, p_up) = _exchange_wait(sent["mlp"], after, "grads_mlp_wait")
        (own_out, p_out), (_, p_ws) = _exchange_wait(sent["gmlp"], own_up, "grads_gmlp_wait")
        rows = lambda t: t.reshape(t.shape[:-3] + (N_HEADS * CHUNK, CHUNK))
        return [dict(parts=p_up, own=own_up, w=w_up[0], m=m_w_up[0], v=v_w_up[0]),
                dict(parts=p_down, own=own_down, w=w_down[0], m=m_w_down[0], v=v_w_down[0]),
                dict(parts=p_out, own=own_out, w=w_out[0], m=m_w_out[0], v=v_w_out[0]),
                dict(parts=rows(p_ws), own=rows(p_ws), w=rows(gm_w_s[0]), m=rows(m_gm_w_s[0]), v=rows(v_gm_w_s[0]),
                     mask=jnp.tril(jnp.ones((CHUNK, CHUNK), F32)))]

    small = {k: w[k][0] for k in _SMALL_PARAMS + ("gm_w_s",)}
    loss_part, grad_x, g = _local_step(
        x.reshape(n_batch * seq, D_MODEL), loss_target.reshape(n_batch * seq, D_MODEL), seq, small,
        dict(mixer_weights=mixer_weights, gmlp_done=gmlp_done, mixers_done=mixers_done, mlp_weights=mlp_weights,
             mlp_grads=mlp_grads, gmlp_grads=gmlp_grads, in_grads=in_grads, arrived_updates=arrived_updates, me=me,
             prenorm_after=second[2]), first_dep=tok_ici_1)

    sent_rows, tok_rows = _exchange_start([in_slot(_pack_slab(g, loss_part))], [True], _ALL_PEERS, "grads_rows_start")
    res = dict(zip(("w_up", "w_down", "w_out", "gm_w_s"), g["updates"]))
    ((own_in, p_in),) = _exchange_wait(sent["in"], tok_rows, "grads_in_wait")
    upd_in = _adamw_reduce(p_in, own_in, me, lying(w_in), lying(m_w_in), lying(v_w_in), "adamw_w_in")
    res["w_in"] = tuple(jnp.transpose(t, (1, 2, 0)) for t in upd_in)
    ((_, p_rows),) = _exchange_wait(sent_rows, upd_in[1], "grads_rows_wait")
    flat = lambda t: t[0] if t.ndim == 3 else t
    small_res, loss = _adamw_slab(
        p_rows, me, *({k: flat(d[k]) for k in _SMALL_PARAMS + ("conv_w",)} for d in (w, m, v)))
    res.update(small_res)
    res = {k: tuple(r.reshape(w[k].shape) for r in res[k]) for k in _WEIGHTS}

    outs = [loss, grad_x.reshape(x.shape)]
    for part in range(4):
        outs.extend(res[k][part] for k in _WEIGHTS)
    return tuple(outs)
```

```python
import functools

import jax
import jax.numpy as jnp
import numpy as np
from jax import lax
from jax.experimental import pallas as pl
from jax.experimental.pallas import tpu as pltpu

F32 = jnp.float32
BF16 = jnp.bfloat16

D_MODEL = 1024
GM_WIDTH = 512
SSM_WIDTH = 512
CONV_CH = 1024
N_HEADS = 8
HEAD_DIM = 64
N_STATE = 128
CHUNK = 128
D_FF = 4096
IN_COLS = 2568
IN_PAD = 2688
N_DEV = 8
EPS = 1e-6
ADAM_LR, ADAM_B1, ADAM_B2, ADAM_EPS, ADAM_WD, ADAM_STEP = 0.001, 0.9, 0.999, 1e-08, 0.01, 10
VMEM_LIMIT_BYTES = 56 * 1024 * 1024
TOKEN_TILE = 512
FF_TILE = 2048
WGRAD_TILE = 512
STACK_TILE = 256
_NT = (((1,), (1,)), ((), ()))
_TN = (((0,), (0,)), ((), ()))


def _params(*sem):
    return pltpu.CompilerParams(dimension_semantics=sem or None, vmem_limit_bytes=VMEM_LIMIT_BYTES)


def _dot(a, b, dims=None):
    if dims is None:
        return jnp.dot(a, b, preferred_element_type=F32)
    return lax.dot_general(a, b, dims, preferred_element_type=F32)


def _split_terms(x, terms):
    out, rem = [], x
    for i in range(terms):
        hi = rem.astype(BF16)
        out.append(hi)
        if i + 1 < terms:
            rem = rem - hi.astype(F32)
    return out


def _split_dot(x, m, terms):
    acc = None
    for hi in _split_terms(x, terms):
        part = _dot(hi, m)
        acc = part if acc is None else acc + part
    return acc


def _split_dot_left(m, x, terms):
    acc = None
    for hi in _split_terms(x, terms):
        part = _dot(m, hi)
        acc = part if acc is None else acc + part
    return acc


def _gelu_and_grad(x):
    c = 0.7978845608028654
    inner = c * (x + 0.044715 * x * x * x)
    t = jnp.tanh(inner)
    g = 0.5 * x * (1.0 + t)
    dg = 0.5 * (1.0 + t) + 0.5 * x * (1.0 - t * t) * c * (1.0 + 3.0 * 0.044715 * x * x)
    return g, dg


def _softplus(x):
    return jnp.maximum(x, 0.0) + jnp.log(1.0 + jnp.exp(-jnp.abs(x)))


def _rsum(x):
    return jnp.sum(x, axis=0, keepdims=True)


def _acc_rows(ref, part, first):
    val = jnp.broadcast_to(part, ref.shape)

    @pl.when(first)
    def _():
        ref[...] = val

    @pl.when(jnp.logical_not(first))
    def _():
        ref[...] += val


def _rms_bwd(n, g, dout):
    r = lax.rsqrt(jnp.mean(n * n, axis=-1, keepdims=True) + EPS)
    nh = n * r
    dg = dout * g
    dn = r * (dg - nh * jnp.mean(dg * nh, axis=-1, keepdims=True))
    return dn, _rsum(dout * nh)


def _const_mats():
    avg = np.kron(np.eye(4), np.full((HEAD_DIM, HEAD_DIM), 1.0 / HEAD_DIM))
    expand = np.zeros((CHUNK, SSM_WIDTH), np.float32)
    for h in range(N_HEADS):
        expand[h, h * HEAD_DIM:(h + 1) * HEAD_DIM] = 1.0
    tril = np.tril(np.ones((CHUNK, CHUNK), np.float32))
    as_bf16 = lambda a: jnp.asarray(a, dtype=BF16)
    return as_bf16(avg), as_bf16(expand), as_bf16(expand.T), as_bf16(tril), as_bf16(tril.T)


def _full(shape):
    nd = len(shape)
    return pl.BlockSpec(shape, lambda *_: (0,) * nd)


_HBM = pl.BlockSpec(memory_space=pltpu.HBM)
_SEM = pl.BlockSpec(memory_space=pltpu.SEMAPHORE)
_ALL_PEERS = tuple((k, 0) for k in range(1, N_DEV))
_SAME_CORE_PEERS = ((2, 0), (4, 0), (6, 0))
_SIBLING_FORWARD = ((1, 0), (1, 2), (1, 4), (1, 6))


def _flip(j, k):
    for bit in (4, 2, 1):
        if k & bit:
            j = j + bit - 2 * (j & bit)
    return j


def _copies(src, land, send_sems, recv_sems, hops, slots=None):
    x, y, c = lax.axis_index("x"), lax.axis_index("y"), lax.axis_index("c")
    me = 4 * x + 2 * y + c
    slots = range(len(src)) if slots is None else slots
    out = []
    for t in range(len(src)):
        for i, (k, b) in enumerate(hops):
            pos = (1 - x if k & 4 else x, 1 - y if k & 2 else y, 1 - c if k & 1 else c)
            peer = _flip(me, k)
            sem = slots[t] * len(hops) + i
            mk = functools.partial(pltpu.make_async_remote_copy, send_sem=send_sems.at[sem], recv_sem=recv_sems.at[sem],
                                   device_id=pos, device_id_type=pl.DeviceIdType.MESH)
            if land[t] is None and src[t].shape[0] != N_DEV:
                width = src[t].shape[1] // N_DEV
                slab = lambda j: src[t].at[:, pl.ds(pl.multiple_of(j * width, 128), width)]
                mine = functools.partial(mk, src_ref=slab(_flip(me, b)), dst_ref=slab(_flip(me, b)))
                theirs = functools.partial(mk, src_ref=slab(_flip(peer, b)), dst_ref=slab(_flip(peer, b)))
            elif land[t] is None:
                mine = functools.partial(mk, src_ref=src[t].at[_flip(me, b)], dst_ref=src[t].at[_flip(me, b)])
                theirs = functools.partial(mk, src_ref=src[t].at[_flip(peer, b)], dst_ref=src[t].at[_flip(peer, b)])
            else:
                assert b == 0
                mine = functools.partial(mk, src_ref=src[t].at[peer], dst_ref=land[t].at[me])
                theirs = functools.partial(mk, src_ref=src[t].at[peer], dst_ref=land[t].at[peer])
            out.append((mine, theirs))
    return out


def _exchange_start(srcs, inplace, peers, name, dep=None):
    n = len(srcs)
    lands = [None if ip else pltpu.with_memory_space_constraint(lax.empty(s.shape, s.dtype), pltpu.HBM)
             for s, ip in zip(srcs, inplace)]
    real_lands = [l for l in lands if l is not None]
    n_l = len(real_lands)
    deps = [] if dep is None else [dep]

    def body(*refs):
        src = refs[:n]
        land_refs = list(refs[n:n + n_l])
        send_sems, recv_sems = refs[n + n_l + len(deps)], refs[n + n_l + len(deps) + 1]
        token = refs[-1]
        land = [None if ip else land_refs.pop(0) for ip in inplace]
        for mine, _ in _copies(src, land, send_sems, recv_sems, peers):
            mine().start()
        token[...] = jnp.zeros_like(token)

    sem_t = pltpu.SemaphoreType.DMA((n * len(peers),))
    outs = pl.pallas_call(
        body, name=name,
        out_shape=(sem_t, sem_t) + tuple(pltpu.HBM(a.shape, a.dtype) for a in list(srcs) + real_lands)
        + (jax.ShapeDtypeStruct((8, 128), F32),),
        in_specs=[_HBM] * (n + n_l) + [pl.BlockSpec(memory_space=pl.ANY)] * len(deps),
        out_specs=(_SEM, _SEM) + (_HBM,) * (n + n_l) + (pl.BlockSpec(memory_space=pltpu.VMEM),),
        input_output_aliases={i: 2 + i for i in range(n + n_l)},
        compiler_params=pltpu.CompilerParams(has_side_effects=pltpu.SideEffectType.DATAFLOW_SIDE_EFFECTING),
    )(*[pltpu.with_memory_space_constraint(s, pltpu.HBM) for s in srcs], *real_lands, *deps)
    handle = dict(send=outs[0], recv=outs[1], srcs=outs[2:2 + n], lands=outs[2 + n:2 + n + n_l], inplace=inplace,
                  peers=peers)
    return handle, outs[-1]


def _exchange_wait(handle, after, name, only=None):
    srcs, lands, inplace, peers = handle["srcs"], handle["lands"], handle["inplace"], handle["peers"]
    slots = None
    if only is not None:
        assert all(inplace)
        slots, srcs, inplace = list(only), [srcs[t] for t in only], [True] * len(only)
    n, n_l = len(srcs), len(lands)
    after = after if isinstance(after, tuple) else (after,)

    def body(*refs):
        src = refs[:n]
        land_refs = list(refs[n:n + n_l])
        send_sems, recv_sems = refs[n + n_l], refs[n + n_l + 1]
        land = [None if ip else land_refs.pop(0) for ip in inplace]
        for mine, theirs in _copies(src, land, send_sems, recv_sems, peers, slots):
            mine().wait_send()
            theirs().wait_recv()

    outs = pl.pallas_call(
        body, name=name, out_shape=tuple(pltpu.HBM(a.shape, a.dtype) for a in list(srcs) + list(lands)),
        in_specs=[_HBM] * (n + n_l) + [_SEM, _SEM] + [pl.BlockSpec(memory_space=pl.ANY)] * len(after),
        out_specs=(_HBM,) * (n + n_l), input_output_aliases={i: i for i in range(n + n_l)},
        compiler_params=pltpu.CompilerParams(has_side_effects=pltpu.SideEffectType.DATAFLOW_SIDE_EFFECTING),
    )(*srcs, *lands, handle["send"], handle["recv"], *after)
    res, land_out = [], list(outs[n:])
    for t in range(n):
        res.append((outs[t], outs[t] if inplace[t] else land_out.pop(0)))
    return res


def _cast_to_slot(w, me, rows, name, cols=False, dep=None):
    r, cdim = w.shape[0], w.shape[-1]
    deps = [] if dep is None else [dep]

    def body(me_ref, w_ref, *rest):
        o_ref = rest[-1]
        if cols:
            o_ref[...] = w_ref[...].astype(BF16)
        else:
            o_ref[0] = w_ref[...].reshape(rows, cdim).astype(BF16)

    if cols:
        out_shape = jax.ShapeDtypeStruct((r, N_DEV * cdim), BF16)
        out_spec = pl.BlockSpec((rows, cdim), lambda i, me_ref: (i, me_ref[0]))
    else:
        out_shape = jax.ShapeDtypeStruct((N_DEV, r, cdim), BF16)
        out_spec = pl.BlockSpec((1, rows, cdim), lambda i, me_ref: (me_ref[0], i, 0))
    return pl.pallas_call(
        body, name=name, out_shape=out_shape,
        grid_spec=pltpu.PrefetchScalarGridSpec(
            num_scalar_prefetch=1, grid=(r // rows,),
            in_specs=[pl.BlockSpec((rows, cdim), lambda i, me_ref: (i, 0)) if w.ndim == 2 else
                      pl.BlockSpec((rows, 1, cdim), lambda i, me_ref: (i, 0, 0))]
            + [pl.BlockSpec(memory_space=pl.ANY)] * len(deps), out_specs=out_spec),
        compiler_params=_params("parallel"))(me, w, *deps)


def _stack_shards(blocks, rows, bn, name):
    n, r, cdim = blocks.shape

    def body(b_ref, o_ref, acc_ref):
        acc_ref[n * r:, :] = jnp.zeros((rows - n * r, bn), F32)
        for j in range(n):
            acc_ref[r * j:r * (j + 1), :] = b_ref[j].astype(F32)
        o_ref[...] = acc_ref[...].astype(BF16)

    return pl.pallas_call(
        body, name=name, grid=(cdim // bn,), out_shape=jax.ShapeDtypeStruct((rows, cdim), BF16),
        in_specs=[pl.BlockSpec((n, r, bn), lambda i: (0, 0, i))], out_specs=pl.BlockSpec((rows, bn), lambda i: (0, i)),
        scratch_shapes=[pltpu.VMEM((rows, bn), F32)], compiler_params=_params("parallel"))(blocks)


def _adamw_math(w, g, m, v):
    m = ADAM_B1 * m + (1.0 - ADAM_B1) * g
    v = ADAM_B2 * v + (1.0 - ADAM_B2) * (g * g)
    m_hat = m / (1.0 - ADAM_B1 ** ADAM_STEP)
    v_hat = v / (1.0 - ADAM_B2 ** ADAM_STEP)
    delta = -ADAM_LR * (m_hat / (jnp.sqrt(v_hat) + ADAM_EPS) + ADAM_WD * w)
    return delta, m, v


def _sum_parts(me, p_ref, own):
    g = None
    for j in range(N_DEV):
        term = (p_ref[j] if own is None else jnp.where(me == j, own, p_ref[j])).astype(F32)
        g = term if g is None else g + term
    return g


def _adamw_reduce(parts, own, me, w, m, v, name):
    r, _, cdim = w.shape

    def body(me_ref, p_ref, own_ref, w_ref, m_ref, v_ref, g_out, d_out, m_out, v_out):
        g = _sum_parts(me_ref[0], p_ref, own_ref[0]).reshape(r, 1, cdim)
        d, mn, vn = _adamw_math(w_ref[...], g, m_ref[...], v_ref[...])
        g_out[...] = g
        d_out[...] = d
        m_out[...] = mn
        v_out[...] = vn

    blk = pl.BlockSpec((r, 1, cdim), lambda i, me_ref: (0, 0, 0))
    return pl.pallas_call(
        body, name=name, out_shape=(jax.ShapeDtypeStruct(w.shape, F32),) * 4,
        grid_spec=pltpu.PrefetchScalarGridSpec(
            num_scalar_prefetch=1, grid=(1,),
            in_specs=[pl.BlockSpec((N_DEV, r, cdim), lambda i, me_ref: (0, 0, 0)),
                      pl.BlockSpec((1, r, cdim), lambda i, me_ref: (me_ref[0], 0, 0)), blk, blk, blk],
            out_specs=(blk,) * 4),
        compiler_params=_params("arbitrary"))(me, parts, own, w, m, v)


_IN_SPLITS = ((0, 512), (512, 1024), (1024, 1536), (1536, 2560), (2560, IN_PAD))


def _prenorm(x, g1, tm, dep=None):
    t_tok = x.shape[0]
    deps = [] if dep is None else [dep]

    def body(x_ref, g_ref, *rest):
        xv = x_ref[...]
        r = lax.rsqrt(jnp.mean(xv * xv, axis=-1, keepdims=True) + EPS)
        rest[-1][...] = (xv * r * g_ref[...]).astype(BF16)

    row = pl.BlockSpec((tm, D_MODEL), lambda i: (i, 0))
    return pl.pallas_call(
        body, name="prenorm", grid=(t_tok // tm,), out_shape=jax.ShapeDtypeStruct((t_tok, D_MODEL), BF16),
        in_specs=[row, _full((1, D_MODEL))] + [pl.BlockSpec(memory_space=pl.ANY)] * len(deps), out_specs=row,
        compiler_params=_params("parallel"))(x, g1, *deps)


def _in_proj(h1, w_in, tm):
    t_tok = h1.shape[0]

    def body(h_ref, w_ref, *outs):
        h = h_ref[...]
        for (a, b), o_ref in zip(_IN_SPLITS, outs):
            o_ref[...] = _dot(h, w_ref[a:b, :], _NT).astype(o_ref.dtype)

    row = lambda n: pl.BlockSpec((tm, n), lambda i: (i, 0))
    widths = [b - a for a, b in _IN_SPLITS]
    dtypes = (BF16, BF16, BF16, F32, F32)
    return pl.pallas_call(
        body, name="in_proj", grid=(t_tok // tm,),
        out_shape=tuple(jax.ShapeDtypeStruct((t_tok, n), dt) for n, dt in zip(widths, dtypes)),
        in_specs=[row(D_MODEL), _full((IN_PAD, D_MODEL))], out_specs=tuple(row(n) for n in widths),
        compiler_params=_params("parallel"))(h1, w_in)


def _lane_masks():
    lane = lax.broadcasted_iota(jnp.int32, (1, 2 * HEAD_DIM), 1)
    left = (lane < HEAD_DIM).astype(F32)
    return left, 1.0 - left


def _stack_pair(v, m_l, m_r):
    return jnp.concatenate([v * m_l, v * m_r], axis=0).astype(BF16)


def _head_mean(x, avg):
    n = avg.shape[0]
    return jnp.concatenate([_split_dot(x[:, n * i:n * (i + 1)], avg, 2) for i in range(x.shape[1] // n)], axis=1)


def _gmlp_common(u, v, lnw, lnb, avg, wcat_ref, bias, m_l, m_r):
    ug, dug = _gelu_and_grad(u)
    vg, dvg = _gelu_and_grad(v)
    mu = _head_mean(vg, avg)
    vc = vg - mu
    var = _head_mean(vc * vc, avg)
    rstd = lax.rsqrt(var + EPS)
    vhat = vc * rstd
    vn = vhat * lnw + lnb
    rows = []
    for r in range(u.shape[0] // CHUNK):
        cols = []
        for j in range(N_HEADS // 2):
            pair = vn[CHUNK * r:CHUNK * (r + 1), 128 * j:128 * (j + 1)]
            cols.append(_dot(wcat_ref[j], _stack_pair(pair, m_l, m_r)))
        rows.append(jnp.concatenate(cols, axis=1) + bias)
    mixed = jnp.concatenate(rows, axis=0)
    return ug, dug, dvg, rstd, vhat, vn, mixed


_GMLP_ROWS = 4 * CHUNK


def _gmlp_fwd(u, v, lnw, lnb, wcat, bias, avg):
    t_tok = u.shape[0]
    tm = min(_GMLP_ROWS, t_tok)

    def body(u_ref, v_ref, lnw_ref, lnb_ref, wcat_ref, bias_ref, avg_ref, o_ref):
        m_l, m_r = _lane_masks()
        ug, _, _, _, _, _, mixed = _gmlp_common(
            u_ref[...].astype(F32), v_ref[...].astype(F32), lnw_ref[...], lnb_ref[...], avg_ref[...], wcat_ref,
            bias_ref[...], m_l, m_r)
        o_ref[...] = (ug * mixed).astype(BF16)

    row = pl.BlockSpec((tm, GM_WIDTH), lambda i: (i, 0))
    return pl.pallas_call(
        body, name="gmlp_fwd", grid=(t_tok // tm,), out_shape=jax.ShapeDtypeStruct((t_tok, GM_WIDTH), BF16),
        in_specs=[row, row, _full((1, GM_WIDTH)), _full((1, GM_WIDTH)), _full(wcat.shape), _full(bias.shape),
                  _full(avg.shape)],
        out_specs=row, compiler_params=_params("parallel"))(u, v, lnw, lnb, wcat, bias, avg)


def _shift_rows(x, edge, j, down):
    groups, cols = x.shape[0] // 8, x.shape[1]
    amount = j if down else 8 - j
    rot = pltpu.roll(x.reshape(groups, 8, cols), amount, axis=1)
    edge_rot = pltpu.roll(edge, amount, axis=0)[None]
    sub = lax.broadcasted_iota(jnp.int32, (1, 8, 1), 1)
    if down:
        out = jnp.where(sub < j, jnp.concatenate([edge_rot, rot[:-1]], axis=0), rot)
    else:
        out = jnp.where(sub < 8 - j, rot, jnp.concatenate([rot[1:], edge_rot], axis=0))
    return out.reshape(x.shape)


def _conv_pre(xbc, tail, cw_ref, cb):
    taps = [_shift_rows(xbc, tail, 3 - k, True) for k in range(3)] + [xbc]
    return cb + cw_ref[0:1, :] * taps[0] + cw_ref[1:2, :] * taps[1] + cw_ref[2:3, :] * taps[2] + cw_ref[3:4, :] * taps[3]


def _ssd_common(pre, dtr, dtb, alog, expand, tril):
    q = CHUNK
    sg = jax.nn.sigmoid(pre)
    act = pre * sg
    lane = lax.broadcasted_iota(jnp.int32, (1, CHUNK), 1)
    a_row = jnp.where(lane < N_HEADS, -jnp.exp(alog), 0.0)
    dtp = dtr + dtb
    dt = _softplus(dtp)
    a_cs = _split_dot_left(tril, dt * a_row, 3)
    a_cs_t = a_cs.T
    dt_exp = _split_dot(dt, expand, 3)
    a_exp = _split_dot(a_cs, expand, 3)
    a_end = a_exp[q - 1:q, :]
    li = lax.broadcasted_iota(jnp.int32, (q, q), 0)
    si = lax.broadcasted_iota(jnp.int32, (q, q), 1)
    causal = si <= li
    decay = []
    for h in range(N_HEADS):
        seg = a_cs[:, h:h + 1] - a_cs_t[h:h + 1, :]
        decay.append(jnp.where(causal, jnp.exp(jnp.minimum(seg, 0.0)), 0.0))
    return dict(pre=pre, sg=sg, act=act, a_row=a_row, dtp=dtp, dt=dt, dt_exp=dt_exp, a_exp=a_exp,
                e=jnp.exp(a_exp), w_end=jnp.exp(a_end - a_exp), cd=jnp.exp(a_end), decay=decay)


def _ssd_specs(t_tok, seq, reverse):
    nb, nc = t_tok // seq, seq // CHUNK

    def chunk(c):
        return nc - 1 - c if reverse else c

    def row(n, col=0):
        return pl.BlockSpec((nb, CHUNK, n), lambda c: (0, chunk(c), col))

    tail = pl.BlockSpec((nb, 8, CONV_CH), lambda c: (0, jnp.maximum(chunk(c) * (CHUNK // 8) - 1, 0), 0))
    states = pl.BlockSpec((nb, 1, N_STATE, SSM_WIDTH), lambda c: (0, chunk(c), 0, 0))
    fold = lambda a: a.reshape(nb, seq, a.shape[-1])
    unfold = lambda a: a.reshape(t_tok, a.shape[-1])
    return nb, nc, row, tail, states, fold, unfold


def _ssd_fwd(z, xbc, dtr, cw, cb, dtb, alog, dskip_exp, nw, expand, tril, seq, dep=None):
    t_tok = z.shape[0]
    nb, nc, row, tail, states_spec, fold, unfold = _ssd_specs(t_tok, seq, False)

    def body(z_ref, xbc_ref, tail_ref, dtr_ref, cw_ref, cb_ref, dtb_ref, alog_ref, dsk_ref, nw_ref, exp_ref,
             tril_ref, o_ref, y_ref, st_ref, pre_ref, state_ref):
        c = pl.program_id(0)

        @pl.when(c == 0)
        def _():
            state_ref[...] = jnp.zeros_like(state_ref)

        m_l, m_r = _lane_masks()
        for s in range(nb):
            pre = _conv_pre(xbc_ref[s], jnp.where(c == 0, 0.0, tail_ref[s]), cw_ref, cb_ref[...])
            pre_ref[s] = pre
            f = _ssd_common(pre, dtr_ref[s], dtb_ref[...], alog_ref[...], exp_ref[...], tril_ref[...])
            act = f["act"]
            xs = act[:, :SSM_WIDTH]
            xdt = xs * f["dt_exp"]
            xw = xdt * f["w_end"]
            state = state_ref[s]
            st_ref[s, 0] = state
            ydiag, yoff, snew = [], [], []
            for g in range(2):
                bg = act[:, 512 + 128 * g:640 + 128 * g].astype(BF16)
                cg = act[:, 768 + 128 * g:896 + 128 * g].astype(BF16)
                cb_mat = _dot(cg, bg, _NT)
                for pr in range(2):
                    h0 = 4 * g + 2 * pr
                    gcat = jnp.concatenate(
                        [(cb_mat * f["decay"][h0]).astype(BF16), (cb_mat * f["decay"][h0 + 1]).astype(BF16)], axis=1)
                    ydiag.append(_dot(gcat, _stack_pair(xdt[:, 64 * h0:64 * h0 + 128], m_l, m_r)))
                yoff.append(_dot(cg, state[:, 256 * g:256 * (g + 1)].astype(BF16)))
                snew.append(_dot(bg, xw[:, 256 * g:256 * (g + 1)].astype(BF16), _TN))
            y = jnp.concatenate(ydiag, axis=1) + f["e"] * jnp.concatenate(yoff, axis=1) + dsk_ref[...] * xs
            state_ref[s] = state * f["cd"] + jnp.concatenate(snew, axis=1)
            y_ref[s] = y
            zv = z_ref[s].astype(F32)
            yg = y * (zv * jax.nn.sigmoid(zv))
            outs = []
            for g in range(2):
                ygg = yg[:, 256 * g:256 * (g + 1)]
                outs.append(ygg * lax.rsqrt(jnp.mean(ygg * ygg, axis=-1, keepdims=True) + EPS))
            o_ref[s] = (jnp.concatenate(outs, axis=1) * nw_ref[...]).astype(BF16)

    consts = [cw, cb, dtb, alog, dskip_exp, nw, expand, tril]
    deps = [] if dep is None else [dep]
    n_in = 4 + len(consts)

    def body_skipping_dep(*refs):
        body(*refs[:n_in], *refs[n_in + len(deps):])

    sd = lambda n, dt: jax.ShapeDtypeStruct((nb, seq, n), dt)
    o, y, states, pre = pl.pallas_call(
        body_skipping_dep, name="ssd_fwd", grid=(nc,),
        out_shape=(sd(SSM_WIDTH, BF16), sd(SSM_WIDTH, F32), jax.ShapeDtypeStruct((nb, nc, N_STATE, SSM_WIDTH), F32),
                   sd(CONV_CH, F32)),
        in_specs=[row(SSM_WIDTH), row(CONV_CH), tail, row(CHUNK)] + [_full(a.shape) for a in consts]
        + [pl.BlockSpec(memory_space=pl.ANY)] * len(deps),
        out_specs=(row(SSM_WIDTH), row(SSM_WIDTH), states_spec, row(CONV_CH)),
        scratch_shapes=[pltpu.VMEM((nb, N_STATE, SSM_WIDTH), F32)],
        compiler_params=_params("arbitrary"))(fold(z), fold(xbc), fold(xbc), fold(dtr), *consts, *deps)
    return unfold(o), unfold(y), states, unfold(pre)


def _out_proj(mix_a, mix_b, w_out, x, g2, g3, tm, dep=None):
    t_tok = x.shape[0]
    deps = [] if dep is None else [dep]

    def body(a_ref, b_ref, w_ref, x_ref, g2_ref, g3_ref, *rest):
        o_ref, x2_ref, h3_ref = rest[-3:]
        o = _dot(a_ref[...], w_ref[0:GM_WIDTH, :]) + _dot(b_ref[...], w_ref[GM_WIDTH:, :])
        o_ref[...] = o
        r2 = lax.rsqrt(jnp.mean(o * o, axis=-1, keepdims=True) + EPS)
        x2 = x_ref[...] + o * r2 * g2_ref[...]
        x2_ref[...] = x2
        r3 = lax.rsqrt(jnp.mean(x2 * x2, axis=-1, keepdims=True) + EPS)
        h3_ref[...] = (x2 * r3 * g3_ref[...]).astype(BF16)

    row = lambda n: pl.BlockSpec((tm, n), lambda i: (i, 0))
    sd = lambda dt: jax.ShapeDtypeStruct((t_tok, D_MODEL), dt)
    return pl.pallas_call(
        body, name="out_proj", grid=(t_tok // tm,), out_shape=(sd(F32), sd(F32), sd(BF16)),
        in_specs=[row(GM_WIDTH), row(SSM_WIDTH), _full((D_MODEL, D_MODEL)), row(D_MODEL), _full((1, D_MODEL)),
                  _full((1, D_MODEL))] + [pl.BlockSpec(memory_space=pl.ANY)] * len(deps),
        out_specs=(row(D_MODEL),) * 3, compiler_params=_params("parallel"))(mix_a, mix_b, w_out, x, g2, g3, *deps)


def _mlp_fwd(h3, w_up, w_down, x2, target, g4, tm, tf):
    t_tok = x2.shape[0]

    def body(h_ref, wu_ref, wu_hbm, wd_hbm, x2_ref, t_ref, g4_ref, ra_ref, dd_ref, dy_ref, dg4_ref, loss_ref, wu_rest_ref,
             wd_ref, sem):
        i = pl.program_id(0)
        w_up_rest_copy = pltpu.make_async_copy(wu_hbm.at[:, pl.ds(tf, D_FF - tf)], wu_rest_ref, sem.at[1])
        w_down_copy = pltpu.make_async_copy(wd_hbm, wd_ref, sem.at[0])

        @pl.when(i == 0)
        def _():
            w_up_rest_copy.start()
            w_down_copy.start()

        hv = h_ref[...]
        ra_ref[:, 0:tf] = jnp.maximum(_dot(hv, wu_ref[...]), 0.0).astype(BF16)

        @pl.when(i == 0)
        def _():
            w_up_rest_copy.wait()

        for j in range(1, D_FF // tf):
            ra_ref[:, j * tf:(j + 1) * tf] = jnp.maximum(
                _dot(hv, wu_rest_ref[:, (j - 1) * tf:j * tf]), 0.0).astype(BF16)

        @pl.when(i == 0)
        def _():
            w_down_copy.wait()

        rav = ra_ref[...]
        dvec = _dot(rav * rav, wd_ref[...])
        r4 = lax.rsqrt(jnp.mean(dvec * dvec, axis=-1, keepdims=True) + EPS)
        dn = dvec * r4
        g4 = g4_ref[...]
        err = x2_ref[...] + dn * g4 - t_ref[...]
        dy = err * (1.0 / D_MODEL)
        dy_ref[...] = dy
        dg = dy * g4
        dd_ref[...] = (r4 * (dg - dn * jnp.mean(dg * dn, axis=-1, keepdims=True))).astype(BF16)
        _acc_rows(dg4_ref, _rsum(dy * dn), i == 0)
        tile_loss = 0.5 * jnp.sum(jnp.sum(err * err, axis=-1, keepdims=True), axis=0, keepdims=True) / D_MODEL
        _acc_rows(loss_ref, jnp.broadcast_to(tile_loss, (1, 128)), i == 0)

    row = pl.BlockSpec((tm, D_MODEL), lambda i: (i, 0))
    wide = pl.BlockSpec((tm, D_FF), lambda i: (i, 0))
    w_up_once = pl.BlockSpec((D_MODEL, tf), lambda i: (0, 0), pipeline_mode=pl.Buffered(1))
    whole = pl.BlockSpec(memory_space=pl.ANY)
    ra, dd, dy, dg4, loss = pl.pallas_call(
        body, name="mlp_fwd", grid=(t_tok // tm,),
        out_shape=(jax.ShapeDtypeStruct((t_tok, D_FF), BF16), jax.ShapeDtypeStruct((t_tok, D_MODEL), BF16),
                   jax.ShapeDtypeStruct((t_tok, D_MODEL), F32), jax.ShapeDtypeStruct((1, D_MODEL), F32),
                   jax.ShapeDtypeStruct((1, 128), F32)),
        in_specs=[row, w_up_once, whole, whole, row, row, _full((1, D_MODEL))],
        out_specs=(wide, row, row, _full((1, D_MODEL)), _full((1, 128))),
        scratch_shapes=[pltpu.VMEM((D_MODEL, D_FF - tf), BF16), pltpu.VMEM((D_FF, D_MODEL), BF16),
                        pltpu.SemaphoreType.DMA((2,))],
        compiler_params=_params("arbitrary"))(h3, w_up, w_up, w_down, x2, target, g4)
    return ra, dd, dy, dg4, loss


def _mlp_bwd(dd, w_down, ra, w_up, x2, dy, o, g3, g2, tm, tf):
    t_tok = x2.shape[0]

    def hidden_body(dd_ref, wd_ref, ra_ref, da_ref):
        df = _dot(dd_ref[...], wd_ref[...], _NT)
        da_ref[...] = (df * (2.0 * ra_ref[...].astype(F32))).astype(BF16)

    tu = min(2 * tm, t_tok)
    da = pl.pallas_call(
        hidden_body, name="mlp_bwd_hidden", grid=(D_FF // tf, t_tok // tu),
        out_shape=jax.ShapeDtypeStruct((t_tok, D_FF), BF16),
        in_specs=[pl.BlockSpec((tu, D_MODEL), lambda j, i: (i, 0)), pl.BlockSpec((tf, D_MODEL), lambda j, i: (j, 0)),
                  pl.BlockSpec((tu, tf), lambda j, i: (i, j))],
        out_specs=pl.BlockSpec((tu, tf), lambda j, i: (i, j)),
        compiler_params=_params("parallel", "parallel"))(dd, w_down, ra)

    def in_body(da_ref, wu_ref, x2_ref, dy_ref, o_ref, g3_ref, g2_ref, dx2_ref, do_ref, dg3_ref, dg2_ref):
        i = pl.program_id(0)
        dh3 = _dot(da_ref[...], wu_ref[...], _NT)
        dn3, dg3 = _rms_bwd(x2_ref[...], g3_ref[...], dh3)
        dx2 = dy_ref[...] + dn3
        dx2_ref[...] = dx2
        do, dg2 = _rms_bwd(o_ref[...], g2_ref[...], dx2)
        do_ref[...] = do.astype(BF16)
        _acc_rows(dg3_ref, dg3, i == 0)
        _acc_rows(dg2_ref, dg2, i == 0)

    row = pl.BlockSpec((tm, D_MODEL), lambda i: (i, 0))
    vec = _full((1, D_MODEL))
    sd = lambda dt: jax.ShapeDtypeStruct((t_tok, D_MODEL), dt)
    dx2, do, dg3, dg2 = pl.pallas_call(
        in_body, name="mlp_bwd_in", grid=(t_tok // tm,),
        out_shape=(sd(F32), sd(BF16), jax.ShapeDtypeStruct((1, D_MODEL), F32), jax.ShapeDtypeStruct((1, D_MODEL), F32)),
        in_specs=[pl.BlockSpec((tm, D_FF), lambda i: (i, 0)), _full((D_MODEL, D_FF)), row, row, row, vec, vec],
        out_specs=(row, row, vec, vec), compiler_params=_params("arbitrary"))(da, w_up, x2, dy, o, g3, g2)
    return da, dx2, do, dg3, dg2


def _wgrad(a, b, out_blocks, bm, bn, bk, square_a, name, dep=None):
    t_tok, m = a.shape
    n = b.shape[1]
    nk = t_tok // bk

    def body(a_ref, b_ref, *rest):
        o_ref, acc_ref = rest[-2:]
        k = pl.program_id(2)
        av = a_ref[...]
        if square_a:
            av = av * av
        part = _dot(av, b_ref[...], _TN)

        def emit(res):
            if out_blocks is None:
                o_ref[...] = res.astype(BF16)
            else:
                o_ref[0] = res.astype(BF16)

        if nk == 1:
            emit(part)
            return

        @pl.when(k == 0)
        def _():
            acc_ref[...] = part

        @pl.when(k > 0)
        def _():
            acc_ref[...] += part

        @pl.when(k == nk - 1)
        def _():
            emit(acc_ref[...])

    if out_blocks is None:
        out_shape = jax.ShapeDtypeStruct((m, n), BF16)
        out_spec = pl.BlockSpec((bm, bn), lambda i, j, k: (i, j))
    else:
        assert n // out_blocks == bn
        out_shape = jax.ShapeDtypeStruct((out_blocks, m, bn), BF16)
        out_spec = pl.BlockSpec((1, bm, bn), lambda i, j, k: (j, i, 0))
    deps = [] if dep is None else [dep]
    return pl.pallas_call(
        body, name=name, grid=(m // bm, n // bn, nk), out_shape=out_shape,
        in_specs=[pl.BlockSpec((bk, bm), lambda i, j, k: (k, i)), pl.BlockSpec((bk, bn), lambda i, j, k: (k, j))]
        + [pl.BlockSpec(memory_space=pl.ANY)] * len(deps),
        out_specs=out_spec, scratch_shapes=[pltpu.VMEM((bm, bn) if nk > 1 else (8, 128), F32)],
        compiler_params=_params("parallel", "parallel", "arbitrary"))(a, b, *deps)


def _wgrad_in_chunked(h1, pieces, bn, bk, dep=None):
    t_tok = h1.shape[0]
    nk = t_tok // bk
    shard = IN_COLS // N_DEV
    widths = [b - a for a, b in _IN_SPLITS]

    def body(h_ref, *rest):
        piece_refs = rest[:len(widths)]
        o_ref, acc_ref = rest[-2:]
        k = pl.program_id(1)
        hv = h_ref[...]
        for (a, b), r in zip(_IN_SPLITS, piece_refs):
            part = _dot(r[...], hv, _TN)

            @pl.when(k == 0)
            def _():
                acc_ref[a:b, :] = part

            @pl.when(k > 0)
            def _():
                acc_ref[a:b, :] += part

        @pl.when(k == nk - 1)
        def _():
            for j in range(N_DEV):
                o_ref[j] = acc_ref[shard * j:shard * (j + 1), :].astype(BF16)

    deps = [] if dep is None else [dep]
    return pl.pallas_call(
        body, name="wgrad_in", grid=(D_MODEL // bn, nk), out_shape=jax.ShapeDtypeStruct((N_DEV, shard, D_MODEL), BF16),
        in_specs=[pl.BlockSpec((bk, bn), lambda j, k: (k, j))] + [pl.BlockSpec((bk, n), lambda j, k: (k, 0)) for n in widths]
        + [pl.BlockSpec(memory_space=pl.ANY)] * len(deps),
        out_specs=pl.BlockSpec((N_DEV, shard, bn), lambda j, k: (0, 0, j)),
        scratch_shapes=[pltpu.VMEM((IN_PAD, bn), F32)],
        compiler_params=_params("parallel", "arbitrary"))(h1, *pieces, *deps)


def _dmix_wgrad_out(do, w_out, mix_a, mix_b, tm, dep=None):
    t_tok = do.shape[0]
    steps = t_tok // tm
    deps = [] if dep is None else [dep]

    def body(d_ref, w_ref, a_ref, b_ref, *rest):
        dm_ref, g_ref, acc_ref = rest[-3:]
        i = pl.program_id(0)
        dov = d_ref[...]
        dm_ref[...] = _dot(dov, w_ref[...], _NT).astype(BF16)
        for (lo, hi), r in zip(((0, GM_WIDTH), (GM_WIDTH, D_MODEL)), (a_ref, b_ref)):
            part = _dot(r[...], dov, _TN)

            @pl.when(i == 0)
            def _():
                acc_ref[lo:hi, :] = part

            @pl.when(i > 0)
            def _():
                acc_ref[lo:hi, :] += part

        @pl.when(i == steps - 1)
        def _():
            g_ref[...] = acc_ref[...].astype(BF16)

    row = lambda n: pl.BlockSpec((tm, n), lambda i: (i, 0))
    return pl.pallas_call(
        body, name="dmix_wgrad_out", grid=(steps,),
        out_shape=(jax.ShapeDtypeStruct((t_tok, D_MODEL), BF16), jax.ShapeDtypeStruct((D_MODEL, D_MODEL), BF16)),
        in_specs=[row(D_MODEL), _full((D_MODEL, D_MODEL)), row(GM_WIDTH), row(SSM_WIDTH)]
        + [pl.BlockSpec(memory_space=pl.ANY)] * len(deps),
        out_specs=(row(D_MODEL), _full((D_MODEL, D_MODEL))), scratch_shapes=[pltpu.VMEM((D_MODEL, D_MODEL), F32)],
        compiler_params=_params("arbitrary"))(do, w_out, mix_a, mix_b, *deps)


def _gmlp_bwd(dmix, u, v, lnw, lnb, wcat, wtcat, bias, avg, expand_t):
    t_tok = u.shape[0]
    tm = min(_GMLP_ROWS, t_tok)

    def body(dm_ref, u_ref, v_ref, lnw_ref, lnb_ref, wcat_ref, wtcat_ref, bias_ref, avg_ref, expt_ref, du_ref, dv_ref,
             dw_ref, db_ref, dlnw_ref, dlnb_ref):
        i = pl.program_id(0)
        m_l, m_r = _lane_masks()
        avg = avg_ref[...]
        lnw = lnw_ref[...]
        ug, dug, dvg, rstd, vhat, vn, mixed = _gmlp_common(
            u_ref[...].astype(F32), v_ref[...].astype(F32), lnw, lnb_ref[...], avg, wcat_ref, bias_ref[...], m_l, m_r)
        dya = dm_ref[...].astype(F32)
        du_ref[...] = (dya * mixed * dug).astype(BF16)
        dmixed = dya * ug
        dvn_rows, dws, dbt = [], [None] * N_HEADS, None
        for r in range(tm // CHUNK):
            dvn_cols = []
            for j in range(N_HEADS // 2):
                dmp = dmixed[CHUNK * r:CHUNK * (r + 1), 128 * j:128 * (j + 1)]
                dvn_cols.append(_dot(wtcat_ref[j], _stack_pair(dmp, m_l, m_r)))
                vnp = vn[CHUNK * r:CHUNK * (r + 1), 128 * j:128 * (j + 1)].astype(BF16)
                for i_h, mask in enumerate((m_l, m_r)):
                    part = _dot((dmp * mask).astype(BF16), vnp, _NT)
                    dws[2 * j + i_h] = part if r == 0 else dws[2 * j + i_h] + part
            dvn_rows.append(jnp.concatenate(dvn_cols, axis=1))
            part = _split_dot(dmixed[CHUNK * r:CHUNK * (r + 1), :], expt_ref[...], 2)
            dbt = part if r == 0 else dbt + part
        dvn = jnp.concatenate(dvn_rows, axis=0)
        dvh = dvn * lnw
        dvgel = rstd * (dvh - _head_mean(dvh, avg) - vhat * _head_mean(dvh * vhat, avg))
        dv_ref[...] = (dvgel * dvg).astype(BF16)
        first = i == 0

        @pl.when(first)
        def _():
            for h in range(N_HEADS):
                dw_ref[h] = dws[h]
            db_ref[...] = dbt

        @pl.when(jnp.logical_not(first))
        def _():
            for h in range(N_HEADS):
                dw_ref[h] += dws[h]
            db_ref[...] += dbt

        _acc_rows(dlnw_ref, _rsum(dvn * vhat), first)
        _acc_rows(dlnb_ref, _rsum(dvn), first)

    row = pl.BlockSpec((tm, GM_WIDTH), lambda i: (i, 0))
    consts = [lnw, lnb, wcat, wtcat, bias, avg, expand_t]
    return pl.pallas_call(
        body, name="gmlp_bwd", grid=(t_tok // tm,),
        out_shape=(jax.ShapeDtypeStruct((t_tok, GM_WIDTH), BF16), jax.ShapeDtypeStruct((t_tok, GM_WIDTH), BF16),
                   jax.ShapeDtypeStruct((N_HEADS, CHUNK, CHUNK), F32), jax.ShapeDtypeStruct((CHUNK, CHUNK), F32),
                   jax.ShapeDtypeStruct((1, GM_WIDTH), F32), jax.ShapeDtypeStruct((1, GM_WIDTH), F32)),
        in_specs=[row, row, row] + [_full(a.shape) for a in consts],
        out_specs=(row, row, _full((N_HEADS, CHUNK, CHUNK)), _full((CHUNK, CHUNK)), _full((1, GM_WIDTH)),
                   _full((1, GM_WIDTH))),
        compiler_params=_params("arbitrary"))(dmix, u, v, *consts)


def _ssd_bwd(dmix, z, xbc, pre, dtr, y, states, cw, cb, dtb, alog, dskip_exp, nw, expand, expand_t, tril, triu, seq,
             dep=None):
    t_tok = z.shape[0]
    nb, nc, row, _, states_spec, fold, unfold = _ssd_specs(t_tok, seq, True)
    q = CHUNK

    def one_sequence(s, dm_ref, z_ref, xbc_ref, pre_ref, dtr_ref, y_ref, st_ref, cw_ref, dtb_ref, alog_ref, dsk_ref,
                     nw_ref, exp_ref, expt_ref, tril_ref, triu_ref, dz_ref, dxbc_ref, ddt_ref, dhead_ref, dstate_ref):
        m_l, m_r = _lane_masks()
        expt = expt_ref[...]
        f = _ssd_common(pre_ref[s], dtr_ref[s], dtb_ref[...], alog_ref[...], exp_ref[...], tril_ref[...])
        act, pre, sg = f["act"], f["pre"], f["sg"]
        xs = act[:, :SSM_WIDTH]
        xdt = xs * f["dt_exp"]
        xw = xdt * f["w_end"]
        state = st_ref[s, 0]
        dstate = dstate_ref[s]
        zv, yv, dout, nw = z_ref[s].astype(F32), y_ref[s], dm_ref[s].astype(F32), nw_ref[...]
        sz = jax.nn.sigmoid(zv)
        sl = zv * sz
        yg = yv * sl
        tv = dout * nw
        dyg_parts, ygh_parts = [], []
        for g in range(2):
            ygg = yg[:, 256 * g:256 * (g + 1)]
            rr = lax.rsqrt(jnp.mean(ygg * ygg, axis=-1, keepdims=True) + EPS)
            ygh = ygg * rr
            tg = tv[:, 256 * g:256 * (g + 1)]
            dyg_parts.append(rr * (tg - ygh * jnp.mean(tg * ygh, axis=-1, keepdims=True)))
            ygh_parts.append(ygh)
        dyg = jnp.concatenate(dyg_parts, axis=1)
        dnw = _rsum(dout * jnp.concatenate(ygh_parts, axis=1))
        dy = dyg * sl
        dz_ref[s] = (dyg * yv * (sz * (1.0 + zv * (1.0 - sz)))).astype(BF16)
        ddsk = _rsum(dy * xs)
        dye = dy * f["e"]
        lane = lax.broadcasted_iota(jnp.int32, (q, q), 1)
        sub = lax.broadcasted_iota(jnp.int32, (q, q), 0)
        rs_mat = jnp.zeros((q, q), F32)
        cs_mat = jnp.zeros((q, q), F32)
        dxdt_cols, yoff, dst_in, dxw, d_b, d_c = [], [], [], [], [], []
        for g in range(2):
            bg = act[:, 512 + 128 * g:640 + 128 * g].astype(BF16)
            cg = act[:, 768 + 128 * g:896 + 128 * g].astype(BF16)
            cb_mat = _dot(cg, bg, _NT)
            stg = state[:, 256 * g:256 * (g + 1)].astype(BF16)
            dyeg = dye[:, 256 * g:256 * (g + 1)].astype(BF16)
            yoff.append(_dot(cg, stg))
            dcg = _dot(dyeg, stg, _NT)
            dst_in.append(_dot(cg, dyeg, _TN))
            dcb = jnp.zeros((q, q), F32)
            for pr in range(2):
                h0 = 4 * g + 2 * pr
                gf = [cb_mat * f["decay"][h0], cb_mat * f["decay"][h0 + 1]]
                gcat = jnp.concatenate([gf[0].astype(BF16), gf[1].astype(BF16)], axis=1)
                xst = _stack_pair(xdt[:, 64 * h0:64 * h0 + 128], m_l, m_r)
                dyp = dy[:, 64 * h0:64 * h0 + 128].astype(BF16)
                dgcat = _dot(dyp, xst, _NT)
                dxst = _dot(gcat, dyp, _TN)
                dxdt_cols.append(dxst[:q] * m_l + dxst[q:] * m_r)
                for i in range(2):
                    h = h0 + i
                    dg = dgcat[:, q * i:q * (i + 1)]
                    mm = dg * gf[i]
                    rs_mat = rs_mat + jnp.where(lane == h, jnp.sum(mm, axis=1, keepdims=True), 0.0)
                    cs_mat = cs_mat + jnp.where(sub == h, jnp.sum(mm, axis=0, keepdims=True), 0.0)
                    dcb = dcb + dg * f["decay"][h]
            dcb16 = dcb.astype(BF16)
            dstg = dstate[:, 256 * g:256 * (g + 1)].astype(BF16)
            d_c.append(dcg + _dot(dcb16, bg))
            dxw.append(_dot(bg, dstg))
            d_b.append(_dot(dcb16, cg, _TN) + _dot(xw[:, 256 * g:256 * (g + 1)].astype(BF16), dstg, _NT))
        dxw = jnp.concatenate(dxw, axis=1)
        dxdt = jnp.concatenate(dxdt_cols, axis=1) + dxw * f["w_end"]
        qv = dxw * xw
        end_row = _rsum(qv) + _rsum(dstate * state) * f["cd"]
        x2 = dye * jnp.concatenate(yoff, axis=1) - qv
        row_i = lax.broadcasted_iota(jnp.int32, (q, 1), 0)
        x2 = x2 + jnp.where(row_i == q - 1, end_row, 0.0)
        da_cs = _split_dot(x2, expt, 2) + rs_mat - cs_mat.T
        ddt = _split_dot(dxdt * xs, expt, 2)
        dxs = dsk_ref[...] * dy + dxdt * f["dt_exp"]
        dda = _split_dot_left(triu_ref[...], da_cs, 3)
        ddt = ddt + dda * f["a_row"]
        dalog = _rsum(dda * f["dt"]) * f["a_row"]
        draw = ddt * jax.nn.sigmoid(f["dtp"])
        ddt_ref[s] = draw.astype(BF16)
        dact = jnp.concatenate([dxs] + d_b + d_c, axis=1)
        dpre = dact * (sg * (1.0 + pre * (1.0 - sg)))
        dhead = dhead_ref[s]
        xv = xbc_ref[s]
        shifted = [_shift_rows(dpre, dhead, 3 - k, False) for k in range(3)] + [dpre]
        dxbc = cw_ref[3:4, :] * dpre
        for k in range(3):
            dxbc = dxbc + cw_ref[k:k + 1, :] * shifted[k]
        dxbc_ref[s] = dxbc.astype(BF16)
        dhead_ref[s] = dpre[0:8, :]
        dstate_ref[s] = dstate * f["cd"] + jnp.concatenate(dst_in, axis=1)
        row8 = lax.broadcasted_iota(jnp.int32, (8, 1), 0)
        dcw = jnp.zeros((8, CONV_CH), F32)
        for k in range(4):
            dcw = dcw + jnp.where(row8 == k, _rsum(shifted[k] * xv), 0.0)
        return dcw, _rsum(dpre), _rsum(draw), dalog, _split_dot(ddsk, expt, 3), dnw

    def body(dm_ref, z_ref, xbc_ref, pre_ref, dtr_ref, y_ref, st_ref, cw_ref, cb_ref, dtb_ref, alog_ref, dsk_ref,
             nw_ref, exp_ref, expt_ref, tril_ref, triu_ref, dz_ref, dxbc_ref, ddt_ref, dcw_ref, dcb_ref, ddtb_ref,
             dalog_ref, dd_ref, dnw_ref, dhead_ref, dstate_ref):
        c = pl.program_id(0)
        first = c == 0

        @pl.when(first)
        def _():
            dstate_ref[...] = jnp.zeros_like(dstate_ref)
            dhead_ref[...] = jnp.zeros_like(dhead_ref)

        total = None
        for s in range(nb):
            parts = one_sequence(s, dm_ref, z_ref, xbc_ref, pre_ref, dtr_ref, y_ref, st_ref, cw_ref, dtb_ref, alog_ref,
                                 dsk_ref, nw_ref, exp_ref, expt_ref, tril_ref, triu_ref, dz_ref, dxbc_ref, ddt_ref,
                                 dhead_ref, dstate_ref)
            total = parts if total is None else tuple(a + b for a, b in zip(total, parts))
        dcw = total[0]

        @pl.when(first)
        def _():
            dcw_ref[...] = dcw

        @pl.when(jnp.logical_not(first))
        def _():
            dcw_ref[...] += dcw

        for ref, part in zip((dcb_ref, ddtb_ref, dalog_ref, dd_ref, dnw_ref), total[1:]):
            _acc_rows(ref, part, first)

    consts = [cw, cb, dtb, alog, dskip_exp, nw, expand, expand_t, tril, triu]
    deps = [] if dep is None else [dep]
    n_in = 7 + len(consts)

    def body_skipping_dep(*refs):
        body(*refs[:n_in], *refs[n_in + len(deps):])

    acc = lambda n: jax.ShapeDtypeStruct((1, n), F32)
    sd = lambda n: jax.ShapeDtypeStruct((nb, seq, n), BF16)
    dz, dxbc, ddt, *small_grads = pl.pallas_call(
        body_skipping_dep, name="ssd_bwd", grid=(nc,),
        out_shape=(sd(SSM_WIDTH), sd(CONV_CH), sd(CHUNK), jax.ShapeDtypeStruct((8, CONV_CH), F32), acc(CONV_CH),
                   acc(CHUNK), acc(CHUNK), acc(CHUNK), acc(SSM_WIDTH)),
        in_specs=[row(SSM_WIDTH, col=1), row(SSM_WIDTH), row(CONV_CH), row(CONV_CH), row(CHUNK), row(SSM_WIDTH),
                  states_spec]
        + [_full(a.shape) for a in consts] + [pl.BlockSpec(memory_space=pl.ANY)] * len(deps),
        out_specs=(row(SSM_WIDTH), row(CONV_CH), row(CHUNK), _full((8, CONV_CH)), _full((1, CONV_CH)),
                   _full((1, CHUNK)), _full((1, CHUNK)), _full((1, CHUNK)), _full((1, SSM_WIDTH))),
        scratch_shapes=[pltpu.VMEM((nb, 8, CONV_CH), F32), pltpu.VMEM((nb, N_STATE, SSM_WIDTH), F32)],
        compiler_params=_params("arbitrary"))(
            fold(dmix), fold(z), fold(xbc), fold(pre), fold(dtr), fold(y), states, *consts, *deps)
    return (unfold(dz), unfold(dxbc), unfold(ddt), *small_grads)


def _in_bwd(du, dv, dz, dxbc, ddt, w_in, x, dx2, g1, tm, me, riders=(), dep=None):
    t_tok = x.shape[0]
    steps = t_tok // tm

    n_in = [5 + ("mask" in rd) for rd in riders]
    first_in = [sum(n_in[:r]) for r in range(len(riders))]

    def body(me_ref, du_ref, dv_ref, dz_ref, dxbc_ref, ddt_ref, w_ref, x_ref, dx2_ref, g_ref, *rest):
        outs = rest[len(rest) - 2 - 4 * len(riders):]
        gx_ref, dg_ref = outs[:2]
        i = pl.program_id(0)
        dh = None
        for (a, b), ref in zip(_IN_SPLITS, (du_ref, dv_ref, dz_ref, dxbc_ref, ddt_ref)):
            part = _dot(ref[...], w_ref[a:b, :])
            dh = part if dh is None else dh + part
        dn, dg = _rms_bwd(x_ref[...], g_ref[...], dh)
        gx_ref[...] = dx2_ref[...] + dn
        _acc_rows(dg_ref, dg, i == 0)
        for r in range(len(riders)):
            p_ref, own_ref, w_ref_r, m_ref_r, v_ref_r = rest[first_in[r]:first_in[r] + 5]
            g = _sum_parts(me_ref[0], p_ref, own_ref[0])
            if n_in[r] == 6:
                g = g * rest[first_in[r] + 5][...]
            d, mn, vn = _adamw_math(w_ref_r[...], g, m_ref_r[...], v_ref_r[...])
            for o_ref, val in zip(outs[2 + 4 * r:6 + 4 * r], (g, d, mn, vn)):
                o_ref[...] = val

    row = lambda n: pl.BlockSpec((tm, n), lambda i, me_ref: (i, 0))
    whole = lambda shape: pl.BlockSpec(shape, lambda i, me_ref: (0,) * len(shape))
    widths = [b - a for a, b in _IN_SPLITS]
    deps = [] if dep is None else [dep]
    rider_args, rider_specs, rider_out_shapes, rider_out_specs = [], [], [], []
    for rd in riders:
        rows, cols = rd["w"].shape[0] // steps, rd["w"].shape[1]
        blk = pl.BlockSpec((rows, cols), lambda i, me_ref: (i, 0))
        rider_args += [rd["parts"], rd["own"], rd["w"], rd["m"], rd["v"]]
        rider_specs += [pl.BlockSpec((N_DEV, rows, cols), lambda i, me_ref: (0, i, 0)),
                        pl.BlockSpec((1, rows, cols), lambda i, me_ref: (me_ref[0], i, 0)), blk, blk, blk]
        if "mask" in rd:
            rider_args.append(rd["mask"])
            rider_specs.append(whole((rows, cols)))
        rider_out_shapes += [jax.ShapeDtypeStruct(rd["w"].shape, F32)] * 4
        rider_out_specs += [blk] * 4
    outs = pl.pallas_call(
        body, name="in_bwd",
        out_shape=(jax.ShapeDtypeStruct((t_tok, D_MODEL), F32), jax.ShapeDtypeStruct((1, D_MODEL), F32),
                   *rider_out_shapes),
        grid_spec=pltpu.PrefetchScalarGridSpec(
            num_scalar_prefetch=1, grid=(steps,),
            in_specs=[row(n) for n in widths] + [whole((IN_PAD, D_MODEL)), row(D_MODEL), row(D_MODEL),
                                                 whole((1, D_MODEL))] + rider_specs
            + [pl.BlockSpec(memory_space=pl.ANY)] * len(deps),
            out_specs=(row(D_MODEL), whole((1, D_MODEL)), *rider_out_specs)),
        compiler_params=_params("arbitrary"))(me, du, dv, dz, dxbc, ddt, w_in, x, dx2, g1, *rider_args, *deps)
    return outs[0], outs[1], [tuple(outs[2 + 4 * r:6 + 4 * r]) for r in range(len(riders))]


def _pad_lanes(a, n):
    return jnp.pad(a, ((0, 0), (0, n - a.shape[1])))


def _local_step(x, target, seq, small, hooks, first_dep=None):
    t_tok = x.shape[0]
    tm = min(TOKEN_TILE, t_tok)
    avg, expand, expand_t, tril, triu = _const_mats()
    g1, g2, g3, g4 = (small[k].reshape(1, D_MODEL) for k in
                      ("norm_mix_pre", "norm_mix_post", "norm_ffn_pre", "norm_ffn_post"))
    tie = (lambda a: a) if first_dep is None else (lambda a: a + first_dep[0, 0])
    lnw = tie(small["gm_ln_w"]).reshape(1, GM_WIDTH)
    lnb = tie(small["gm_ln_b"]).reshape(1, GM_WIDTH)
    causal = jnp.tril(jnp.ones((CHUNK, CHUNK), F32))
    wm = tie(small["gm_w_s"]) * causal
    pair = lambda w: w.reshape(4, 2, CHUNK, CHUNK).transpose(0, 2, 1, 3).reshape(4, CHUNK, 2 * CHUNK).astype(BF16)
    wcat = pair(wm)
    wtcat = pair(jnp.swapaxes(wm, 1, 2))
    bias = jnp.repeat(tie(small["gm_b_s"]).T, HEAD_DIM, axis=1)
    cb = small["conv_b"].reshape(1, CONV_CH)
    dtb = _pad_lanes(tie(small["dt_bias"]).reshape(1, N_HEADS), CHUNK)
    alog = _pad_lanes(tie(small["a_log"]).reshape(1, N_HEADS), CHUNK)
    dskip_exp = jnp.repeat(tie(small["d_skip"]).reshape(1, N_HEADS), HEAD_DIM, axis=1)
    nw = small["ssm_norm_w"].reshape(1, SSM_WIDTH)

    h1 = _prenorm(x, g1, tm, hooks.get("prenorm_after", first_dep))
    w_in_t, conv_w = hooks["mixer_weights"]((h1, lnw, lnb, wcat, wtcat, bias, dtb, alog, dskip_exp))
    tall = min(2 * tm, t_tok)
    u, v, z, xbc, dtr = _in_proj(h1, w_in_t, tall)
    mix_a = _gmlp_fwd(u, v, lnw, lnb, wcat, bias, avg)
    dep = hooks["gmlp_done"](mix_a) if "gmlp_done" in hooks else None
    mix_b, y_pre, states, pre = _ssd_fwd(z, xbc, dtr, conv_w, cb, dtb, alog, dskip_exp, nw, expand, tril, seq, dep)
    w_out, dep = hooks["mixers_done"](mix_b)
    o, x2, h3 = _out_proj(mix_a, mix_b, w_out, x, g2, g3, tall, dep)
    w_up, w_down = hooks["mlp_weights"](h3)
    tf = FF_TILE
    ra, dd, dy, dg4, loss = _mlp_fwd(h3, w_up, w_down, x2, target, g4, tm, tf)

    da, dx2, do, dg3, dg2 = _mlp_bwd(dd, w_down, ra, w_up, x2, dy, o, g3, g2, tm, tf)
    g_w_down = _wgrad(ra, dd, None, WGRAD_TILE, D_MODEL, t_tok, True, "wgrad_down")
    g_w_up = _wgrad(h3, da, N_DEV, D_MODEL, D_FF // N_DEV, t_tok, False, "wgrad_up")
    dep = hooks["mlp_grads"](g_w_down, g_w_up)
    dmix, g_w_out = _dmix_wgrad_out(do, w_out, mix_a, mix_b, tall, dep)
    du, dv, dws, dbt, dlnw, dlnb = _gmlp_bwd(dmix, u, v, lnw, lnb, wcat, wtcat, bias, avg, expand_t)
    dep = hooks["gmlp_grads"](g_w_out, dws)
    dz, dxbc, ddt, dcw, dcb, ddtb, dalog, ddsk, dnw = _ssd_bwd(
        dmix, z, xbc, pre, dtr, y_pre, states, conv_w, cb, dtb, alog, dskip_exp, nw, expand, expand_t, tril, triu, seq,
        dep)
    g_w_in = _wgrad_in_chunked(h1, (du, dv, dz, dxbc, ddt), WGRAD_TILE, t_tok // 2, dep)
    dep = hooks["in_grads"](g_w_in, dcw[0:4])
    riders = hooks["arrived_updates"](dep) if "arrived_updates" in hooks else []
    me = hooks.get("me", jnp.zeros((1,), jnp.int32))
    grad_x, dg1, updates = _in_bwd(du, dv, dz, dxbc, ddt, w_in_t, x, dx2, g1, tm, me, riders, dep)

    grads = dict(
        updates=updates,
        w_in=g_w_in, w_out=g_w_out, w_up=g_w_up, w_down=g_w_down, conv_w=dcw[0:4],
        norm_mix_pre=dg1, norm_mix_post=dg2, norm_ffn_pre=dg3, norm_ffn_post=dg4, gm_ln_w=dlnw, gm_ln_b=dlnb,
        gm_w_s=dws, gm_b_s=dbt, conv_b=dcb, dt_bias=ddtb, a_log=dalog, d_skip=ddsk, ssm_norm_w=dnw)
    return loss[0, 0], grad_x, grads


_WEIGHTS = ("norm_mix_pre", "w_in", "gm_ln_w", "gm_ln_b", "gm_w_s", "gm_b_s", "conv_w", "conv_b", "dt_bias", "a_log",
            "d_skip", "ssm_norm_w", "w_out", "norm_mix_post", "norm_ffn_pre", "w_up", "w_down", "norm_ffn_post")
_SLAB_ROWS = (("norm_mix_pre", 1024), ("norm_mix_post", 1024), ("norm_ffn_pre", 1024), ("norm_ffn_post", 1024),
              ("conv_b", 1024), ("ssm_norm_w", 512), ("gm_ln_w", 512), ("gm_ln_b", 512), ("dt_bias", 8), ("a_log", 8),
              ("d_skip", 8))
_SLAB_LOSS_ROW = len(_SLAB_ROWS)
_SLAB_BS_ROW = 16
_SMALL_PARAMS = tuple(name for name, _ in _SLAB_ROWS) + ("gm_b_s",)
_LN_PARAMS = ("gm_ln_w", "gm_ln_b")


_SLAB_CONV_ROW = _SLAB_LOSS_ROW + 1


def _pack_slab(g, loss_part):
    rows = [_pad_lanes(g[name], D_MODEL) for name, _ in _SLAB_ROWS]
    rows.append(jnp.broadcast_to(loss_part, (1, D_MODEL)))
    rows.append(g["conv_w"])
    assert sum(r.shape[0] for r in rows) == _SLAB_BS_ROW
    rows.append(_pad_lanes(g["gm_b_s"].T[0:N_HEADS], D_MODEL))
    return jnp.concatenate(rows, axis=0)


def _adamw_slab(parts, me, w, m, v):
    names = _SMALL_PARAMS + ("conv_w",)
    shapes = [w[k].shape for k in names]
    unfold = np.zeros((GM_WIDTH, HEAD_DIM), np.float32)
    for h in range(N_HEADS):
        unfold[h * HEAD_DIM:(h + 1) * HEAD_DIM, :] = np.eye(HEAD_DIM)
    unfold = jnp.asarray(unfold, dtype=BF16)
    n = len(names)
    shard = CONV_CH // N_DEV

    def body(me_ref, p_ref, unfold_ref, *refs):
        w_refs, m_refs, v_refs = refs[:n], refs[n:2 * n], refs[2 * n:3 * n]
        outs = refs[3 * n:]
        g_all = p_ref[0]
        for j in range(1, N_DEV):
            g_all = g_all + p_ref[j]
        lane = lax.broadcasted_iota(jnp.int32, (N_HEADS, GM_WIDTH), 1)
        head = lax.broadcasted_iota(jnp.int32, (N_HEADS, GM_WIDTH), 0)
        own_lanes = jnp.logical_and(lane >= head * HEAD_DIM, lane < (head + 1) * HEAD_DIM)
        mine = pl.ds(pl.multiple_of(me_ref[0] * shard, shard), shard)
        for i, name in enumerate(names):
            if name == "gm_b_s":
                g = g_all[_SLAB_BS_ROW:_SLAB_BS_ROW + N_HEADS, 0:CHUNK]
            elif name == "conv_w":
                g = p_ref[0, _SLAB_CONV_ROW:_SLAB_CONV_ROW + 4, mine]
                for j in range(1, N_DEV):
                    g = g + p_ref[j, _SLAB_CONV_ROW:_SLAB_CONV_ROW + 4, mine]
            else:
                row = [r for r, (k, _) in enumerate(_SLAB_ROWS) if k == name][0]
                g = g_all[row:row + 1, 0:dict(_SLAB_ROWS)[name]]
                if name in _LN_PARAMS:
                    g = _split_dot(jnp.where(own_lanes, g, 0.0), unfold_ref[...], 3)
            d, mn, vn = _adamw_math(w_refs[i][...], g, m_refs[i][...], v_refs[i][...])
            for o_ref, val in zip(outs[4 * i:4 * i + 4], (g, d, mn, vn)):
                o_ref[...] = val
        outs[-1][...] = g_all[_SLAB_LOSS_ROW:_SLAB_LOSS_ROW + 1, 0:128]

    def whole(shape):
        nd = len(shape)
        return pl.BlockSpec(shape, lambda i, me_ref: (0,) * nd)

    ins = [parts, unfold] + [d[k] for d in (w, m, v) for k in names]
    out_shape = tuple(jax.ShapeDtypeStruct(s, F32) for s in shapes for _ in range(4)) + (
        jax.ShapeDtypeStruct((1, 128), F32),)
    outs = pl.pallas_call(
        body, name="adamw_small", out_shape=out_shape,
        grid_spec=pltpu.PrefetchScalarGridSpec(
            num_scalar_prefetch=1, grid=(1,), in_specs=[whole(a.shape) for a in ins],
            out_specs=tuple(whole(s.shape) for s in out_shape)),
        compiler_params=_params("arbitrary"))(me, *ins)
    return {k: tuple(outs[4 * i:4 * i + 4]) for i, k in enumerate(names)}, outs[-1][0, 0]


def kernel(x, norm_mix_pre, w_in, gm_ln_w, gm_ln_b, gm_w_s, gm_b_s, conv_w, conv_b, dt_bias, a_log, d_skip, ssm_norm_w, w_out, norm_mix_post, norm_ffn_pre, w_up, w_down, norm_ffn_post, loss_target, m_norm_mix_pre, m_w_in, m_gm_ln_w, m_gm_ln_b, m_gm_w_s, m_gm_b_s, m_conv_w, m_conv_b, m_dt_bias, m_a_log, m_d_skip, m_ssm_norm_w, m_w_out, m_norm_mix_post, m_norm_ffn_pre, m_w_up, m_w_down, m_norm_ffn_post, v_norm_mix_pre, v_w_in, v_gm_ln_w, v_gm_ln_b, v_gm_w_s, v_gm_b_s, v_conv_w, v_conv_b, v_dt_bias, v_a_log, v_d_skip, v_ssm_norm_w, v_w_out, v_norm_mix_post, v_norm_ffn_pre, v_w_up, v_w_down, v_norm_ffn_post):
    w = dict(norm_mix_pre=norm_mix_pre, w_in=w_in, gm_ln_w=gm_ln_w, gm_ln_b=gm_ln_b, gm_w_s=gm_w_s, gm_b_s=gm_b_s, conv_w=conv_w, conv_b=conv_b, dt_bias=dt_bias, a_log=a_log, d_skip=d_skip, ssm_norm_w=ssm_norm_w, w_out=w_out, norm_mix_post=norm_mix_post, norm_ffn_pre=norm_ffn_pre, w_up=w_up, w_down=w_down, norm_ffn_post=norm_ffn_post)
    m = dict(norm_mix_pre=m_norm_mix_pre, w_in=m_w_in, gm_ln_w=m_gm_ln_w, gm_ln_b=m_gm_ln_b, gm_w_s=m_gm_w_s, gm_b_s=m_gm_b_s, conv_w=m_conv_w, conv_b=m_conv_b, dt_bias=m_dt_bias, a_log=m_a_log, d_skip=m_d_skip, ssm_norm_w=m_ssm_norm_w, w_out=m_w_out, norm_mix_post=m_norm_mix_post, norm_ffn_pre=m_norm_ffn_pre, w_up=m_w_up, w_down=m_w_down, norm_ffn_post=m_norm_ffn_post)
    v = dict(norm_mix_pre=v_norm_mix_pre, w_in=v_w_in, gm_ln_w=v_gm_ln_w, gm_ln_b=v_gm_ln_b, gm_w_s=v_gm_w_s, gm_b_s=v_gm_b_s, conv_w=v_conv_w, conv_b=v_conv_b, dt_bias=v_dt_bias, a_log=v_a_log, d_skip=v_d_skip, ssm_norm_w=v_ssm_norm_w, w_out=v_w_out, norm_mix_post=v_norm_mix_post, norm_ffn_pre=v_norm_ffn_pre, w_up=v_w_up, w_down=v_w_down, norm_ffn_post=v_norm_ffn_post)
    n_batch, seq, _ = x.shape
    shard_in = IN_COLS // N_DEV

    me = (4 * lax.axis_index("x") + 2 * lax.axis_index("y") + lax.axis_index("c")).astype(jnp.int32).reshape(1)

    def in_slot(own):
        return lax.dynamic_update_slice(lax.empty((N_DEV,) + own.shape, own.dtype), own[None],
                                        (me[0],) + (0,) * own.ndim)

    lying = lambda t: jnp.transpose(t, (2, 0, 1))
    first = [_cast_to_slot(lying(w_in), me, shard_in, "cast_w_in"), in_slot(conv_w[0])]
    ici_1, tok_ici_1 = _exchange_start(first, [True] * 2, _SAME_CORE_PEERS, "gather_mix_ici_start")
    cast_out = _cast_to_slot(w_out[0], me, 128, "cast_w_out", dep=tok_ici_1)
    cast_up = _cast_to_slot(w_up[0], me, 1024, "cast_w_up", cols=True, dep=cast_out)
    second = [cast_out, cast_up, _cast_to_slot(w_down[0], me, 512, "cast_w_down", dep=cast_up)]
    gathering = {}

    def mixer_weights(after):
        bufs = [buf for buf, _ in _exchange_wait(ici_1, after, "gather_mix_ici_wait")]
        d2d_1, tok_d2d_1 = _exchange_start(bufs, [True] * 2, _SIBLING_FORWARD, "gather_mix_d2d_start")
        gathering["late_ici"], tok_ici_2 = _exchange_start(
            second, [True] * 3, _SAME_CORE_PEERS, "gather_late_ici_start", dep=tok_d2d_1)
        (_, ag_in), (_, ag_conv) = _exchange_wait(d2d_1, tok_ici_2, "gather_mix_d2d_wait")
        w_in_t = _stack_shards(ag_in, IN_PAD, STACK_TILE, "stack_w_in")
        return w_in_t, ag_conv.transpose(1, 0, 2).reshape(4, CONV_CH)

    def gmlp_done(after):
        ((buf, _),) = _exchange_wait(gathering["late_ici"], after, "gather_out_ici_wait", only=(0,))
        gathering["out"], tok = _exchange_start([buf], [True], _SIBLING_FORWARD, "gather_out_d2d_start")
        return tok

    def mixers_done(after):
        bufs = [buf for buf, _ in _exchange_wait(gathering["late_ici"], after, "gather_mlp_ici_wait", only=(1, 2))]
        gathering["mlp"], tok = _exchange_start(bufs, [True] * 2, _SIBLING_FORWARD, "gather_mlp_d2d_start")
        ((_, ag_out),) = _exchange_wait(gathering["out"], tok, "gather_out_d2d_wait")
        return ag_out.reshape(D_MODEL, D_MODEL), tok

    def mlp_weights(after):
        (_, ag_up), (_, ag_down) = _exchange_wait(gathering["mlp"], after, "gather_mlp_d2d_wait")
        return ag_up, ag_down.reshape(D_FF, D_MODEL)

    sent = {}

    def mlp_grads(g_w_down, g_w_up):
        sent["mlp"], tok = _exchange_start(
            [g_w_down.reshape(N_DEV, D_FF // N_DEV, D_MODEL), g_w_up], [False, False], _ALL_PEERS, "grads_mlp_start")
        return tok

    def gmlp_grads(g_w_out, g_w_s):
        sent["gmlp"], tok = _exchange_start(
            [g_w_out.reshape(N_DEV, D_MODEL // N_DEV, D_MODEL), in_slot(g_w_s.astype(BF16))], [False, True], _ALL_PEERS,
            "grads_gmlp_start")
        return tok

    def in_grads(g_w_in_t, g_conv_w):
        sent["in"], tok = _exchange_start([g_w_in_t], [False], _ALL_PEERS, "grads_in_start")
        return tok

    def arrived_updates(after):
        (own_down, p_down), (own_up, p_up) = _exchange_wait(sent["mlp"], after, "grads_mlp_wait")
        (own_out, p_out), (_, p_ws) = _exchange_wait(sent["gmlp"], own_up, "grads_gmlp_wait")
        rows = lambda t: t.reshape(t.shape[:-3] + (N_HEADS * CHUNK, CHUNK))
        return [dict(parts=p_up, own=own_up, w=w_up[0], m=m_w_up[0], v=v_w_up[0]),
                dict(parts=p_down, own=own_down, w=w_down[0], m=m_w_down[0], v=v_w_down[0]),
                dict(parts=p_out, own=own_out, w=w_out[0], m=m_w_out[0], v=v_w_out[0]),
                dict(parts=rows(p_ws), own=rows(p_ws), w=rows(gm_w_s[0]), m=rows(m_gm_w_s[0]), v=rows(v_gm_w_s[0]),
                     mask=jnp.tril(jnp.ones((CHUNK, CHUNK), F32)))]

    small = {k: w[k][0] for k in _SMALL_PARAMS + ("gm_w_s",)}
    loss_part, grad_x, g = _local_step(
        x.reshape(n_batch * seq, D_MODEL), loss_target.reshape(n_batch * seq, D_MODEL), seq, small,
        dict(mixer_weights=mixer_weights, gmlp_done=gmlp_done, mixers_done=mixers_done, mlp_weights=mlp_weights,
             mlp_grads=mlp_grads, gmlp_grads=gmlp_grads, in_grads=in_grads, arrived_updates=arrived_updates, me=me,
             prenorm_after=second[2]), first_dep=tok_ici_1)

    sent_rows, tok_rows = _exchange_start([in_slot(_pack_slab(g, loss_part))], [True], _ALL_PEERS, "grads_rows_start")
    res = dict(zip(("w_up", "w_down", "w_out", "gm_w_s"), g["updates"]))
    ((own_in, p_in),) = _exchange_wait(sent["in"], tok_rows, "grads_in_wait")
    upd_in = _adamw_reduce(p_in, own_in, me, lying(w_in), lying(m_w_in), lying(v_w_in), "adamw_w_in")
    res["w_in"] = tuple(jnp.transpose(t, (1, 2, 0)) for t in upd_in)
    ((_, p_rows),) = _exchange_wait(sent_rows, upd_in[1], "grads_rows_wait")
    flat = lambda t: t[0] if t.ndim == 3 else t
    small_res, loss = _adamw_slab(
        p_rows, me, *({k: flat(d[k]) for k in _SMALL_PARAMS + ("conv_w",)} for d in (w, m, v)))
    res.update(small_res)
    res = {k: tuple(r.reshape(w[k].shape) for r in res[k]) for k in _WEIGHTS}

    outs = [loss, grad_x.reshape(x.shape)]
    for part in range(4):
        outs.extend(res[k][part] for k in _WEIGHTS)
    return tuple(outs)
```

```python
import functools

import jax
import jax.numpy as jnp
import numpy as np
from jax import lax
from jax.experimental import pallas as pl
from jax.experimental.pallas import tpu as pltpu

F32 = jnp.float32
BF16 = jnp.bfloat16

D_MODEL = 1024
GM_WIDTH = 512
SSM_WIDTH = 512
CONV_CH = 1024
N_HEADS = 8
HEAD_DIM = 64
N_STATE = 128
CHUNK = 128
D_FF = 4096
IN_COLS = 2568
IN_PAD = 2688
N_DEV = 8
EPS = 1e-6
ADAM_LR, ADAM_B1, ADAM_B2, ADAM_EPS, ADAM_WD, ADAM_STEP = 0.001, 0.9, 0.999, 1e-08, 0.01, 10
VMEM_LIMIT_BYTES = 56 * 1024 * 1024
TOKEN_TILE = 512
FF_TILE = 2048
WGRAD_TILE = 512
STACK_TILE = 256
_NT = (((1,), (1,)), ((), ()))
_TN = (((0,), (0,)), ((), ()))


def _params(*sem):
    return pltpu.CompilerParams(dimension_semantics=sem or None, vmem_limit_bytes=VMEM_LIMIT_BYTES)


def _dot(a, b, dims=None):
    if dims is None:
        return jnp.dot(a, b, preferred_element_type=F32)
    return lax.dot_general(a, b, dims, preferred_element_type=F32)


def _split_terms(x, terms):
    out, rem = [], x
    for i in range(terms):
        hi = rem.astype(BF16)
        out.append(hi)
        if i + 1 < terms:
            rem = rem - hi.astype(F32)
    return out


def _split_dot(x, m, terms):
    acc = None
    for hi in _split_terms(x, terms):
        part = _dot(hi, m)
        acc = part if acc is None else acc + part
    return acc


def _split_dot_left(m, x, terms):
    acc = None
    for hi in _split_terms(x, terms):
        part = _dot(m, hi)
        acc = part if acc is None else acc + part
    return acc


def _gelu_and_grad(x):
    c = 0.7978845608028654
    inner = c * (x + 0.044715 * x * x * x)
    t = jnp.tanh(inner)
    g = 0.5 * x * (1.0 + t)
    dg = 0.5 * (1.0 + t) + 0.5 * x * (1.0 - t * t) * c * (1.0 + 3.0 * 0.044715 * x * x)
    return g, dg


def _softplus(x):
    return jnp.maximum(x, 0.0) + jnp.log(1.0 + jnp.exp(-jnp.abs(x)))


def _rsum(x):
    return jnp.sum(x, axis=0, keepdims=True)


def _acc_rows(ref, part, first):
    val = jnp.broadcast_to(part, ref.shape)

    @pl.when(first)
    def _():
        ref[...] = val

    @pl.when(jnp.logical_not(first))
    def _():
        ref[...] += val


def _rms_bwd(n, g, dout):
    r = lax.rsqrt(jnp.mean(n * n, axis=-1, keepdims=True) + EPS)
    nh = n * r
    dg = dout * g
    dn = r * (dg - nh * jnp.mean(dg * nh, axis=-1, keepdims=True))
    return dn, _rsum(dout * nh)


def _const_mats():
    avg = np.kron(np.eye(4), np.full((HEAD_DIM, HEAD_DIM), 1.0 / HEAD_DIM))
    expand = np.zeros((CHUNK, SSM_WIDTH), np.float32)
    for h in range(N_HEADS):
        expand[h, h * HEAD_DIM:(h + 1) * HEAD_DIM] = 1.0
    tril = np.tril(np.ones((CHUNK, CHUNK), np.float32))
    as_bf16 = lambda a: jnp.asarray(a, dtype=BF16)
    return as_bf16(avg), as_bf16(expand), as_bf16(expand.T), as_bf16(tril), as_bf16(tril.T)


def _full(shape):
    nd = len(shape)
    return pl.BlockSpec(shape, lambda *_: (0,) * nd)


_HBM = pl.BlockSpec(memory_space=pltpu.HBM)
_SEM = pl.BlockSpec(memory_space=pltpu.SEMAPHORE)
_ALL_PEERS = tuple((k, 0) for k in range(1, N_DEV))
_SAME_CORE_PEERS = ((2, 0), (4, 0), (6, 0))
_SIBLING_FORWARD = ((1, 0), (1, 2), (1, 4), (1, 6))


def _flip(j, k):
    for bit in (4, 2, 1):
        if k & bit:
            j = j + bit - 2 * (j & bit)
    return j


def _copies(src, land, send_sems, recv_sems, hops, slots=None):
    x, y, c = lax.axis_index("x"), lax.axis_index("y"), lax.axis_index("c")
    me = 4 * x + 2 * y + c
    slots = range(len(src)) if slots is None else slots
    out = []
    for t in range(len(src)):
        for i, (k, b) in enumerate(hops):
            pos = (1 - x if k & 4 else x, 1 - y if k & 2 else y, 1 - c if k & 1 else c)
            peer = _flip(me, k)
            sem = slots[t] * len(hops) + i
            mk = functools.partial(pltpu.make_async_remote_copy, send_sem=send_sems.at[sem], recv_sem=recv_sems.at[sem],
                                   device_id=pos, device_id_type=pl.DeviceIdType.MESH)
            if land[t] is None and src[t].shape[0] != N_DEV:
                width = src[t].shape[1] // N_DEV
                slab = lambda j: src[t].at[:, pl.ds(pl.multiple_of(j * width, 128), width)]
                mine = functools.partial(mk, src_ref=slab(_flip(me, b)), dst_ref=slab(_flip(me, b)))
                theirs = functools.partial(mk, src_ref=slab(_flip(peer, b)), dst_ref=slab(_flip(peer, b)))
            elif land[t] is None:
                mine = functools.partial(mk, src_ref=src[t].at[_flip(me, b)], dst_ref=src[t].at[_flip(me, b)])
                theirs = functools.partial(mk, src_ref=src[t].at[_flip(peer, b)], dst_ref=src[t].at[_flip(peer, b)])
            else:
                assert b == 0
                mine = functools.partial(mk, src_ref=src[t].at[peer], dst_ref=land[t].at[me])
                theirs = functools.partial(mk, src_ref=src[t].at[peer], dst_ref=land[t].at[peer])
            out.append((mine, theirs))
    return out


def _exchange_start(srcs, inplace, peers, name, dep=None):
    n = len(srcs)
    lands = [None if ip else pltpu.with_memory_space_constraint(lax.empty(s.shape, s.dtype), pltpu.HBM)
             for s, ip in zip(srcs, inplace)]
    real_lands = [l for l in lands if l is not None]
    n_l = len(real_lands)
    deps = [] if dep is None else [dep]

    def body(*refs):
        src = refs[:n]
        land_refs = list(refs[n:n + n_l])
        send_sems, recv_sems = refs[n + n_l + len(deps)], refs[n + n_l + len(deps) + 1]
        token = refs[-1]
        land = [None if ip else land_refs.pop(0) for ip in inplace]
        for mine, _ in _copies(src, land, send_sems, recv_sems, peers):
            mine().start()
        token[...] = jnp.zeros_like(token)

    sem_t = pltpu.SemaphoreType.DMA((n * len(peers),))
    outs = pl.pallas_call(
        body, name=name,
        out_shape=(sem_t, sem_t) + tuple(pltpu.HBM(a.shape, a.dtype) for a in list(srcs) + real_lands)
        + (jax.ShapeDtypeStruct((8, 128), F32),),
        in_specs=[_HBM] * (n + n_l) + [pl.BlockSpec(memory_space=pl.ANY)] * len(deps),
        out_specs=(_SEM, _SEM) + (_HBM,) * (n + n_l) + (pl.BlockSpec(memory_space=pltpu.VMEM),),
        input_output_aliases={i: 2 + i for i in range(n + n_l)},
        compiler_params=pltpu.CompilerParams(has_side_effects=pltpu.SideEffectType.DATAFLOW_SIDE_EFFECTING),
    )(*[pltpu.with_memory_space_constraint(s, pltpu.HBM) for s in srcs], *real_lands, *deps)
    handle = dict(send=outs[0], recv=outs[1], srcs=outs[2:2 + n], lands=outs[2 + n:2 + n + n_l], inplace=inplace,
                  peers=peers)
    return handle, outs[-1]


def _exchange_wait(handle, after, name, only=None):
    srcs, lands, inplace, peers = handle["srcs"], handle["lands"], handle["inplace"], handle["peers"]
    slots = None
    if only is not None:
        assert all(inplace)
        slots, srcs, inplace = list(only), [srcs[t] for t in only], [True] * len(only)
    n, n_l = len(srcs), len(lands)
    after = after if isinstance(after, tuple) else (after,)

    def body(*refs):
        src = refs[:n]
        land_refs = list(refs[n:n + n_l])
        send_sems, recv_sems = refs[n + n_l], refs[n + n_l + 1]
        land = [None if ip else land_refs.pop(0) for ip in inplace]
        for mine, theirs in _copies(src, land, send_sems, recv_sems, peers, slots):
            mine().wait_send()
            theirs().wait_recv()

    outs = pl.pallas_call(
        body, name=name, out_shape=tuple(pltpu.HBM(a.shape, a.dtype) for a in list(srcs) + list(lands)),
        in_specs=[_HBM] * (n + n_l) + [_SEM, _SEM] + [pl.BlockSpec(memory_space=pl.ANY)] * len(after),
        out_specs=(_HBM,) * (n + n_l), input_output_aliases={i: i for i in range(n + n_l)},
        compiler_params=pltpu.CompilerParams(has_side_effects=pltpu.SideEffectType.DATAFLOW_SIDE_EFFECTING),
    )(*srcs, *lands, handle["send"], handle["recv"], *after)
    res, land_out = [], list(outs[n:])
    for t in range(n):
        res.append((outs[t], outs[t] if inplace[t] else land_out.pop(0)))
    return res


def _cast_to_slot(w, me, rows, name, cols=False, dep=None):
    r, cdim = w.shape[0], w.shape[-1]
    deps = [] if dep is None else [dep]

    def body(me_ref, w_ref, *rest):
        o_ref = rest[-1]
        if cols:
            o_ref[...] = w_ref[...].astype(BF16)
        else:
            o_ref[0] = w_ref[...].reshape(rows, cdim).astype(BF16)

    if cols:
        out_shape = jax.ShapeDtypeStruct((r, N_DEV * cdim), BF16)
        out_spec = pl.BlockSpec((rows, cdim), lambda i, me_ref: (i, me_ref[0]))
    else:
        out_shape = jax.ShapeDtypeStruct((N_DEV, r, cdim), BF16)
        out_spec = pl.BlockSpec((1, rows, cdim), lambda i, me_ref: (me_ref[0], i, 0))
    return pl.pallas_call(
        body, name=name, out_shape=out_shape,
        grid_spec=pltpu.PrefetchScalarGridSpec(
            num_scalar_prefetch=1, grid=(r // rows,),
            in_specs=[pl.BlockSpec((rows, cdim), lambda i, me_ref: (i, 0)) if w.ndim == 2 else
                      pl.BlockSpec((rows, 1, cdim), lambda i, me_ref: (i, 0, 0))]
            + [pl.BlockSpec(memory_space=pl.ANY)] * len(deps), out_specs=out_spec),
        compiler_params=_params("parallel"))(me, w, *deps)


def _stack_shards(blocks, rows, bn, name):
    n, r, cdim = blocks.shape

    def body(b_ref, o_ref, acc_ref):
        acc_ref[n * r:, :] = jnp.zeros((rows - n * r, bn), F32)
        for j in range(n):
            acc_ref[r * j:r * (j + 1), :] = b_ref[j].astype(F32)
        o_ref[...] = acc_ref[...].astype(BF16)

    return pl.pallas_call(
        body, name=name, grid=(cdim // bn,), out_shape=jax.ShapeDtypeStruct((rows, cdim), BF16),
        in_specs=[pl.BlockSpec((n, r, bn), lambda i: (0, 0, i))], out_specs=pl.BlockSpec((rows, bn), lambda i: (0, i)),
        scratch_shapes=[pltpu.VMEM((rows, bn), F32)], compiler_params=_params("parallel"))(blocks)


def _adamw_math(w, g, m, v):
    m = ADAM_B1 * m + (1.0 - ADAM_B1) * g
    v = ADAM_B2 * v + (1.0 - ADAM_B2) * (g * g)
    m_hat = m / (1.0 - ADAM_B1 ** ADAM_STEP)
    v_hat = v / (1.0 - ADAM_B2 ** ADAM_STEP)
    delta = -ADAM_LR * (m_hat / (jnp.sqrt(v_hat) + ADAM_EPS) + ADAM_WD * w)
    return delta, m, v


def _sum_parts(me, p_ref, own):
    g = None
    for j in range(N_DEV):
        term = (p_ref[j] if own is None else jnp.where(me == j, own, p_ref[j])).astype(F32)
        g = term if g is None else g + term
    return g


def _adamw_reduce(parts, own, me, w, m, v, name):
    r, _, cdim = w.shape

    def body(me_ref, p_ref, own_ref, w_ref, m_ref, v_ref, g_out, d_out, m_out, v_out):
        g = _sum_parts(me_ref[0], p_ref, own_ref[0]).reshape(r, 1, cdim)
        d, mn, vn = _adamw_math(w_ref[...], g, m_ref[...], v_ref[...])
        g_out[...] = g
        d_out[...] = d
        m_out[...] = mn
        v_out[...] = vn

    blk = pl.BlockSpec((r, 1, cdim), lambda i, me_ref: (0, 0, 0))
    return pl.pallas_call(
        body, name=name, out_shape=(jax.ShapeDtypeStruct(w.shape, F32),) * 4,
        grid_spec=pltpu.PrefetchScalarGridSpec(
            num_scalar_prefetch=1, grid=(1,),
            in_specs=[pl.BlockSpec((N_DEV, r, cdim), lambda i, me_ref: (0, 0, 0)),
                      pl.BlockSpec((1, r, cdim), lambda i, me_ref: (me_ref[0], 0, 0)), blk, blk, blk],
            out_specs=(blk,) * 4),
        compiler_params=_params("arbitrary"))(me, parts, own, w, m, v)


_IN_SPLITS = ((0, 512), (512, 1024), (1024, 1536), (1536, 2560), (2560, IN_PAD))


def _prenorm(x, g1, tm, dep=None):
    t_tok = x.shape[0]
    deps = [] if dep is None else [dep]

    def body(x_ref, g_ref, *rest):
        xv = x_ref[...]
        r = lax.rsqrt(jnp.mean(xv * xv, axis=-1, keepdims=True) + EPS)
        rest[-1][...] = (xv * r * g_ref[...]).astype(BF16)

    row = pl.BlockSpec((tm, D_MODEL), lambda i: (i, 0))
    return pl.pallas_call(
        body, name="prenorm", grid=(t_tok // tm,), out_shape=jax.ShapeDtypeStruct((t_tok, D_MODEL), BF16),
        in_specs=[row, _full((1, D_MODEL))] + [pl.BlockSpec(memory_space=pl.ANY)] * len(deps), out_specs=row,
        compiler_params=_params("parallel"))(x, g1, *deps)


def _in_proj(h1, w_in, tm):
    t_tok = h1.shape[0]

    def body(h_ref, w_ref, *outs):
        h = h_ref[...]
        for (a, b), o_ref in zip(_IN_SPLITS, outs):
            o_ref[...] = _dot(h, w_ref[a:b, :], _NT).astype(o_ref.dtype)

    row = lambda n: pl.BlockSpec((tm, n), lambda i: (i, 0))
    widths = [b - a for a, b in _IN_SPLITS]
    dtypes = (BF16, BF16, BF16, F32, F32)
    return pl.pallas_call(
        body, name="in_proj", grid=(t_tok // tm,),
        out_shape=tuple(jax.ShapeDtypeStruct((t_tok, n), dt) for n, dt in zip(widths, dtypes)),
        in_specs=[row(D_MODEL), _full((IN_PAD, D_MODEL))], out_specs=tuple(row(n) for n in widths),
        compiler_params=_params("parallel"))(h1, w_in)


def _lane_masks():
    lane = lax.broadcasted_iota(jnp.int32, (1, 2 * HEAD_DIM), 1)
    left = (lane < HEAD_DIM).astype(F32)
    return left, 1.0 - left


def _stack_pair(v, m_l, m_r):
    return jnp.concatenate([v * m_l, v * m_r], axis=0).astype(BF16)


def _head_mean(x, avg):
    n = avg.shape[0]
    return jnp.concatenate([_split_dot(x[:, n * i:n * (i + 1)], avg, 2) for i in range(x.shape[1] // n)], axis=1)


def _gmlp_common(u, v, lnw, lnb, avg, wcat_ref, bias, m_l, m_r):
    ug, dug = _gelu_and_grad(u)
    vg, dvg = _gelu_and_grad(v)
    mu = _head_mean(vg, avg)
    vc = vg - mu
    var = _head_mean(vc * vc, avg)
    rstd = lax.rsqrt(var + EPS)
    vhat = vc * rstd
    vn = vhat * lnw + lnb
    rows = []
    for r in range(u.shape[0] // CHUNK):
        cols = []
        for j in range(N_HEADS // 2):
            pair = vn[CHUNK * r:CHUNK * (r + 1), 128 * j:128 * (j + 1)]
            cols.append(_dot(wcat_ref[j], _stack_pair(pair, m_l, m_r)))
        rows.append(jnp.concatenate(cols, axis=1) + bias)
    mixed = jnp.concatenate(rows, axis=0)
    return ug, dug, dvg, rstd, vhat, vn, mixed


_GMLP_ROWS = 4 * CHUNK


def _gmlp_fwd(u, v, lnw, lnb, wcat, bias, avg):
    t_tok = u.shape[0]
    tm = min(_GMLP_ROWS, t_tok)

    def body(u_ref, v_ref, lnw_ref, lnb_ref, wcat_ref, bias_ref, avg_ref, o_ref):
        m_l, m_r = _lane_masks()
        ug, _, _, _, _, _, mixed = _gmlp_common(
            u_ref[...].astype(F32), v_ref[...].astype(F32), lnw_ref[...], lnb_ref[...], avg_ref[...], wcat_ref,
            bias_ref[...], m_l, m_r)
        o_ref[...] = (ug * mixed).astype(BF16)

    row = pl.BlockSpec((tm, GM_WIDTH), lambda i: (i, 0))
    return pl.pallas_call(
        body, name="gmlp_fwd", grid=(t_tok // tm,), out_shape=jax.ShapeDtypeStruct((t_tok, GM_WIDTH), BF16),
        in_specs=[row, row, _full((1, GM_WIDTH)), _full((1, GM_WIDTH)), _full(wcat.shape), _full(bias.shape),
                  _full(avg.shape)],
        out_specs=row, compiler_params=_params("parallel"))(u, v, lnw, lnb, wcat, bias, avg)


def _shift_rows(x, edge, j, down):
    groups, cols = x.shape[0] // 8, x.shape[1]
    amount = j if down else 8 - j
    rot = pltpu.roll(x.reshape(groups, 8, cols), amount, axis=1)
    edge_rot = pltpu.roll(edge, amount, axis=0)[None]
    sub = lax.broadcasted_iota(jnp.int32, (1, 8, 1), 1)
    if down:
        out = jnp.where(sub < j, jnp.concatenate([edge_rot, rot[:-1]], axis=0), rot)
    else:
        out = jnp.where(sub < 8 - j, rot, jnp.concatenate([rot[1:], edge_rot], axis=0))
    return out.reshape(x.shape)


def _conv_pre(xbc, tail, cw_ref, cb):
    taps = [_shift_rows(xbc, tail, 3 - k, True) for k in range(3)] + [xbc]
    return cb + cw_ref[0:1, :] * taps[0] + cw_ref[1:2, :] * taps[1] + cw_ref[2:3, :] * taps[2] + cw_ref[3:4, :] * taps[3]


def _ssd_common(pre, dtr, dtb, alog, expand, tril):
    q = CHUNK
    sg = jax.nn.sigmoid(pre)
    act = pre * sg
    lane = lax.broadcasted_iota(jnp.int32, (1, CHUNK), 1)
    a_row = jnp.where(lane < N_HEADS, -jnp.exp(alog), 0.0)
    dtp = dtr + dtb
    dt = _softplus(dtp)
    a_cs = _split_dot_left(tril, dt * a_row, 3)
    a_cs_t = a_cs.T
    dt_exp = _split_dot(dt, expand, 3)
    a_exp = _split_dot(a_cs, expand, 3)
    a_end = a_exp[q - 1:q, :]
    li = lax.broadcasted_iota(jnp.int32, (q, q), 0)
    si = lax.broadcasted_iota(jnp.int32, (q, q), 1)
    causal = si <= li
    decay = []
    for h in range(N_HEADS):
        seg = a_cs[:, h:h + 1] - a_cs_t[h:h + 1, :]
        decay.append(jnp.where(causal, jnp.exp(jnp.minimum(seg, 0.0)), 0.0))
    return dict(pre=pre, sg=sg, act=act, a_row=a_row, dtp=dtp, dt=dt, dt_exp=dt_exp, a_exp=a_exp,
                e=jnp.exp(a_exp), w_end=jnp.exp(a_end - a_exp), cd=jnp.exp(a_end), decay=decay)


def _ssd_specs(t_tok, seq, reverse):
    nb, nc = t_tok // seq, seq // CHUNK

    def chunk(c):
        return nc - 1 - c if reverse else c

    def row(n, col=0):
        return pl.BlockSpec((nb, CHUNK, n), lambda c: (0, chunk(c), col))

    tail = pl.BlockSpec((nb, 8, CONV_CH), lambda c: (0, jnp.maximum(chunk(c) * (CHUNK // 8) - 1, 0), 0))
    states = pl.BlockSpec((nb, 1, N_STATE, SSM_WIDTH), lambda c: (0, chunk(c), 0, 0))
    fold = lambda a: a.reshape(nb, seq, a.shape[-1])
    unfold = lambda a: a.reshape(t_tok, a.shape[-1])
    return nb, nc, row, tail, states, fold, unfold


def _ssd_fwd(z, xbc, dtr, cw, cb, dtb, alog, dskip_exp, nw, expand, tril, seq, dep=None):
    t_tok = z.shape[0]
    nb, nc, row, tail, states_spec, fold, unfold = _ssd_specs(t_tok, seq, False)

    def body(z_ref, xbc_ref, tail_ref, dtr_ref, cw_ref, cb_ref, dtb_ref, alog_ref, dsk_ref, nw_ref, exp_ref,
             tril_ref, o_ref, y_ref, st_ref, pre_ref, state_ref):
        c = pl.program_id(0)

        @pl.when(c == 0)
        def _():
            state_ref[...] = jnp.zeros_like(state_ref)

        m_l, m_r = _lane_masks()
        for s in range(nb):
            pre = _conv_pre(xbc_ref[s], jnp.where(c == 0, 0.0, tail_ref[s]), cw_ref, cb_ref[...])
            pre_ref[s] = pre
            f = _ssd_common(pre, dtr_ref[s], dtb_ref[...], alog_ref[...], exp_ref[...], tril_ref[...])
            act = f["act"]
            xs = act[:, :SSM_WIDTH]
            xdt = xs * f["dt_exp"]
            xw = xdt * f["w_end"]
            state = state_ref[s]
            st_ref[s, 0] = state
            ydiag, yoff, snew = [], [], []
            for g in range(2):
                bg = act[:, 512 + 128 * g:640 + 128 * g].astype(BF16)
                cg = act[:, 768 + 128 * g:896 + 128 * g].astype(BF16)
                cb_mat = _dot(cg, bg, _NT)
                for pr in range(2):
                    h0 = 4 * g + 2 * pr
                    gcat = jnp.concatenate(
                        [(cb_mat * f["decay"][h0]).astype(BF16), (cb_mat * f["decay"][h0 + 1]).astype(BF16)], axis=1)
                    ydiag.append(_dot(gcat, _stack_pair(xdt[:, 64 * h0:64 * h0 + 128], m_l, m_r)))
                yoff.append(_dot(cg, state[:, 256 * g:256 * (g + 1)].astype(BF16)))
                snew.append(_dot(bg, xw[:, 256 * g:256 * (g + 1)].astype(BF16), _TN))
            y = jnp.concatenate(ydiag, axis=1) + f["e"] * jnp.concatenate(yoff, axis=1) + dsk_ref[...] * xs
            state_ref[s] = state * f["cd"] + jnp.concatenate(snew, axis=1)
            y_ref[s] = y
            zv = z_ref[s].astype(F32)
            yg = y * (zv * jax.nn.sigmoid(zv))
            outs = []
            for g in range(2):
                ygg = yg[:, 256 * g:256 * (g + 1)]
                outs.append(ygg * lax.rsqrt(jnp.mean(ygg * ygg, axis=-1, keepdims=True) + EPS))
            o_ref[s] = (jnp.concatenate(outs, axis=1) * nw_ref[...]).astype(BF16)

    consts = [cw, cb, dtb, alog, dskip_exp, nw, expand, tril]
    deps = [] if dep is None else [dep]
    n_in = 4 + len(consts)

    def body_skipping_dep(*refs):
        body(*refs[:n_in], *refs[n_in + len(deps):])

    sd = lambda n, dt: jax.ShapeDtypeStruct((nb, seq, n), dt)
    o, y, states, pre = pl.pallas_call(
        body_skipping_dep, name="ssd_fwd", grid=(nc,),
        out_shape=(sd(SSM_WIDTH, BF16), sd(SSM_WIDTH, F32), jax.ShapeDtypeStruct((nb, nc, N_STATE, SSM_WIDTH), F32),
                   sd(CONV_CH, F32)),
        in_specs=[row(SSM_WIDTH), row(CONV_CH), tail, row(CHUNK)] + [_full(a.shape) for a in consts]
        + [pl.BlockSpec(memory_space=pl.ANY)] * len(deps),
        out_specs=(row(SSM_WIDTH), row(SSM_WIDTH), states_spec, row(CONV_CH)),
        scratch_shapes=[pltpu.VMEM((nb, N_STATE, SSM_WIDTH), F32)],
        compiler_params=_params("arbitrary"))(fold(z), fold(xbc), fold(xbc), fold(dtr), *consts, *deps)
    return unfold(o), unfold(y), states, unfold(pre)


def _out_proj(mix_a, mix_b, w_out, x, g2, g3, tm, dep=None):
    t_tok = x.shape[0]
    deps = [] if dep is None else [dep]

    def body(a_ref, b_ref, w_ref, x_ref, g2_ref, g3_ref, *rest):
        o_ref, x2_ref, h3_ref = rest[-3:]
        o = _dot(a_ref[...], w_ref[0:GM_WIDTH, :]) + _dot(b_ref[...], w_ref[GM_WIDTH:, :])
        o_ref[...] = o
        r2 = lax.rsqrt(jnp.mean(o * o, axis=-1, keepdims=True) + EPS)
        x2 = x_ref[...] + o * r2 * g2_ref[...]
        x2_ref[...] = x2
        r3 = lax.rsqrt(jnp.mean(x2 * x2, axis=-1, keepdims=True) + EPS)
        h3_ref[...] = (x2 * r3 * g3_ref[...]).astype(BF16)

    row = lambda n: pl.BlockSpec((tm, n), lambda i: (i, 0))
    sd = lambda dt: jax.ShapeDtypeStruct((t_tok, D_MODEL), dt)
    return pl.pallas_call(
        body, name="out_proj", grid=(t_tok // tm,), out_shape=(sd(F32), sd(F32), sd(BF16)),
        in_specs=[row(GM_WIDTH), row(SSM_WIDTH), _full((D_MODEL, D_MODEL)), row(D_MODEL), _full((1, D_MODEL)),
                  _full((1, D_MODEL))] + [pl.BlockSpec(memory_space=pl.ANY)] * len(deps),
        out_specs=(row(D_MODEL),) * 3, compiler_params=_params("parallel"))(mix_a, mix_b, w_out, x, g2, g3, *deps)


def _mlp_fwd(h3, w_up, w_down, x2, target, g4, tm, tf):
    t_tok = x2.shape[0]

    def body(h_ref, wu_ref, wd_hbm, x2_ref, t_ref, g4_ref, ra_ref, dd_ref, dy_ref, dg4_ref, loss_ref, wd_ref, sem):
        i = pl.program_id(0)
        w_down_copy = pltpu.make_async_copy(wd_hbm, wd_ref, sem.at[0])

        @pl.when(i == 0)
        def _():
            w_down_copy.start()

        hv = h_ref[...]
        for j in range(D_FF // tf):
            ra_ref[:, j * tf:(j + 1) * tf] = jnp.maximum(_dot(hv, wu_ref[:, j * tf:(j + 1) * tf]), 0.0).astype(BF16)

        @pl.when(i == 0)
        def _():
            w_down_copy.wait()

        rav = ra_ref[...]
        dvec = _dot(rav * rav, wd_ref[...])
        r4 = lax.rsqrt(jnp.mean(dvec * dvec, axis=-1, keepdims=True) + EPS)
        dn = dvec * r4
        g4 = g4_ref[...]
        err = x2_ref[...] + dn * g4 - t_ref[...]
        dy = err * (1.0 / D_MODEL)
        dy_ref[...] = dy
        dg = dy * g4
        dd_ref[...] = (r4 * (dg - dn * jnp.mean(dg * dn, axis=-1, keepdims=True))).astype(BF16)
        _acc_rows(dg4_ref, _rsum(dy * dn), i == 0)
        tile_loss = 0.5 * jnp.sum(jnp.sum(err * err, axis=-1, keepdims=True), axis=0, keepdims=True) / D_MODEL
        _acc_rows(loss_ref, jnp.broadcast_to(tile_loss, (1, 128)), i == 0)

    row = pl.BlockSpec((tm, D_MODEL), lambda i: (i, 0))
    wide = pl.BlockSpec((tm, D_FF), lambda i: (i, 0))
    w_up_once = pl.BlockSpec((D_MODEL, D_FF), lambda i: (0, 0), pipeline_mode=pl.Buffered(1))
    ra, dd, dy, dg4, loss = pl.pallas_call(
        body, name="mlp_fwd", grid=(t_tok // tm,),
        out_shape=(jax.ShapeDtypeStruct((t_tok, D_FF), BF16), jax.ShapeDtypeStruct((t_tok, D_MODEL), BF16),
                   jax.ShapeDtypeStruct((t_tok, D_MODEL), F32), jax.ShapeDtypeStruct((1, D_MODEL), F32),
                   jax.ShapeDtypeStruct((1, 128), F32)),
        in_specs=[row, w_up_once, pl.BlockSpec(memory_space=pl.ANY), row, row, _full((1, D_MODEL))],
        out_specs=(wide, row, row, _full((1, D_MODEL)), _full((1, 128))),
        scratch_shapes=[pltpu.VMEM((D_FF, D_MODEL), BF16), pltpu.SemaphoreType.DMA((1,))],
        compiler_params=_params("arbitrary"))(h3, w_up, w_down, x2, target, g4)
    return ra, dd, dy, dg4, loss


def _mlp_bwd(dd, w_down, ra, w_up, x2, dy, o, g3, g2, mixer, tm, tf):
    t_tok = x2.shape[0]

    def hidden_body(dd_ref, wd_ref, ra_ref, da_ref):
        df = _dot(dd_ref[...], wd_ref[...], _NT)
        da_ref[...] = (df * (2.0 * ra_ref[...].astype(F32))).astype(BF16)

    tu = min(2 * tm, t_tok)
    da = pl.pallas_call(
        hidden_body, name="mlp_bwd_hidden", grid=(D_FF // tf, t_tok // tu),
        out_shape=jax.ShapeDtypeStruct((t_tok, D_FF), BF16),
        in_specs=[pl.BlockSpec((tu, D_MODEL), lambda j, i: (i, 0)), pl.BlockSpec((tf, D_MODEL), lambda j, i: (j, 0)),
                  pl.BlockSpec((tu, tf), lambda j, i: (i, j))],
        out_specs=pl.BlockSpec((tu, tf), lambda j, i: (i, j)),
        compiler_params=_params("parallel", "parallel"))(dd, w_down, ra)

    steps = t_tok // tm

    def in_body(da_ref, wu_ref, x2_ref, dy_ref, o_ref, g3_ref, g2_ref, wo_ref, a_ref, b_ref, dx2_ref, dm_ref, gwo_ref,
                dg3_ref, dg2_ref, acc_ref):
        i = pl.program_id(0)
        dh3 = _dot(da_ref[...], wu_ref[...], _NT)
        dn3, dg3 = _rms_bwd(x2_ref[...], g3_ref[...], dh3)
        dx2 = dy_ref[...] + dn3
        dx2_ref[...] = dx2
        do, dg2 = _rms_bwd(o_ref[...], g2_ref[...], dx2)
        dov = do.astype(BF16)
        dm_ref[...] = _dot(dov, wo_ref[...], _NT).astype(BF16)
        for (lo, hi), r in zip(((0, GM_WIDTH), (GM_WIDTH, D_MODEL)), (a_ref, b_ref)):
            part = _dot(r[...], dov, _TN)

            @pl.when(i == 0)
            def _():
                acc_ref[lo:hi, :] = part

            @pl.when(i > 0)
            def _():
                acc_ref[lo:hi, :] += part

        @pl.when(i == steps - 1)
        def _():
            gwo_ref[...] = acc_ref[...].astype(BF16)

        _acc_rows(dg3_ref, dg3, i == 0)
        _acc_rows(dg2_ref, dg2, i == 0)

    row = pl.BlockSpec((tm, D_MODEL), lambda i: (i, 0))
    half = pl.BlockSpec((tm, GM_WIDTH), lambda i: (i, 0))
    vec = _full((1, D_MODEL))
    once = lambda shape: pl.BlockSpec(shape, lambda i: (0, 0), pipeline_mode=pl.Buffered(1))
    sd = lambda dt: jax.ShapeDtypeStruct((t_tok, D_MODEL), dt)
    w_out, mix_a, mix_b = mixer
    dx2, dmix, g_w_out, dg3, dg2 = pl.pallas_call(
        in_body, name="mlp_bwd_in", grid=(steps,),
        out_shape=(sd(F32), sd(BF16), jax.ShapeDtypeStruct((D_MODEL, D_MODEL), BF16),
                   jax.ShapeDtypeStruct((1, D_MODEL), F32), jax.ShapeDtypeStruct((1, D_MODEL), F32)),
        in_specs=[pl.BlockSpec((tm, D_FF), lambda i: (i, 0)), once((D_MODEL, D_FF)), row, row, row, vec, vec,
                  once((D_MODEL, D_MODEL)), half, half],
        out_specs=(row, row, _full((D_MODEL, D_MODEL)), vec, vec),
        scratch_shapes=[pltpu.VMEM((D_MODEL, D_MODEL), F32)],
        compiler_params=_params("arbitrary"))(da, w_up, x2, dy, o, g3, g2, w_out, mix_a, mix_b)
    return da, dx2, dmix, g_w_out, dg3, dg2


def _wgrad(a, b, out_blocks, bm, bn, bk, square_a, name, dep=None):
    t_tok, m = a.shape
    n = b.shape[1]
    nk = t_tok // bk

    def body(a_ref, b_ref, *rest):
        o_ref, acc_ref = rest[-2:]
        k = pl.program_id(2)
        av = a_ref[...]
        if square_a:
            av = av * av
        part = _dot(av, b_ref[...], _TN)

        def emit(res):
            if out_blocks is None:
                o_ref[...] = res.astype(BF16)
            else:
                o_ref[0] = res.astype(BF16)

        if nk == 1:
            emit(part)
            return

        @pl.when(k == 0)
        def _():
            acc_ref[...] = part

        @pl.when(k > 0)
        def _():
            acc_ref[...] += part

        @pl.when(k == nk - 1)
        def _():
            emit(acc_ref[...])

    if out_blocks is None:
        out_shape = jax.ShapeDtypeStruct((m, n), BF16)
        out_spec = pl.BlockSpec((bm, bn), lambda i, j, k: (i, j))
    else:
        assert n // out_blocks == bn
        out_shape = jax.ShapeDtypeStruct((out_blocks, m, bn), BF16)
        out_spec = pl.BlockSpec((1, bm, bn), lambda i, j, k: (j, i, 0))
    deps = [] if dep is None else [dep]
    return pl.pallas_call(
        body, name=name, grid=(m // bm, n // bn, nk), out_shape=out_shape,
        in_specs=[pl.BlockSpec((bk, bm), lambda i, j, k: (k, i)), pl.BlockSpec((bk, bn), lambda i, j, k: (k, j))]
        + [pl.BlockSpec(memory_space=pl.ANY)] * len(deps),
        out_specs=out_spec, scratch_shapes=[pltpu.VMEM((bm, bn) if nk > 1 else (8, 128), F32)],
        compiler_params=_params("parallel", "parallel", "arbitrary"))(a, b, *deps)


def _wgrad_in_chunked(h1, pieces, bn, bk, dep=None):
    t_tok = h1.shape[0]
    nk = t_tok // bk
    shard = IN_COLS // N_DEV
    widths = [b - a for a, b in _IN_SPLITS]

    def body(h_ref, *rest):
        piece_refs = rest[:len(widths)]
        o_ref, acc_ref = rest[-2:]
        k = pl.program_id(1)
        hv = h_ref[...]
        for (a, b), r in zip(_IN_SPLITS, piece_refs):
            part = _dot(r[...], hv, _TN)

            @pl.when(k == 0)
            def _():
                acc_ref[a:b, :] = part

            @pl.when(k > 0)
            def _():
                acc_ref[a:b, :] += part

        @pl.when(k == nk - 1)
        def _():
            for j in range(N_DEV):
                o_ref[j] = acc_ref[shard * j:shard * (j + 1), :].astype(BF16)

    deps = [] if dep is None else [dep]
    return pl.pallas_call(
        body, name="wgrad_in", grid=(D_MODEL // bn, nk), out_shape=jax.ShapeDtypeStruct((N_DEV, shard, D_MODEL), BF16),
        in_specs=[pl.BlockSpec((bk, bn), lambda j, k: (k, j))] + [pl.BlockSpec((bk, n), lambda j, k: (k, 0)) for n in widths]
        + [pl.BlockSpec(memory_space=pl.ANY)] * len(deps),
        out_specs=pl.BlockSpec((N_DEV, shard, bn), lambda j, k: (0, 0, j)),
        scratch_shapes=[pltpu.VMEM((IN_PAD, bn), F32)],
        compiler_params=_params("parallel", "arbitrary"))(h1, *pieces, *deps)


def _dmix_wgrad_out(do, w_out, mix_a, mix_b, tm, dep=None):
    t_tok = do.shape[0]
    steps = t_tok // tm
    deps = [] if dep is None else [dep]

    def body(d_ref, w_ref, a_ref, b_ref, *rest):
        dm_ref, g_ref, acc_ref = rest[-3:]
        i = pl.program_id(0)
        dov = d_ref[...]
        dm_ref[...] = _dot(dov, w_ref[...], _NT).astype(BF16)
        for (lo, hi), r in zip(((0, GM_WIDTH), (GM_WIDTH, D_MODEL)), (a_ref, b_ref)):
            part = _dot(r[...], dov, _TN)

            @pl.when(i == 0)
            def _():
                acc_ref[lo:hi, :] = part

            @pl.when(i > 0)
            def _():
                acc_ref[lo:hi, :] += part

        @pl.when(i == steps - 1)
        def _():
            g_ref[...] = acc_ref[...].astype(BF16)

    row = lambda n: pl.BlockSpec((tm, n), lambda i: (i, 0))
    return pl.pallas_call(
        body, name="dmix_wgrad_out", grid=(steps,),
        out_shape=(jax.ShapeDtypeStruct((t_tok, D_MODEL), BF16), jax.ShapeDtypeStruct((D_MODEL, D_MODEL), BF16)),
        in_specs=[row(D_MODEL), _full((D_MODEL, D_MODEL)), row(GM_WIDTH), row(SSM_WIDTH)]
        + [pl.BlockSpec(memory_space=pl.ANY)] * len(deps),
        out_specs=(row(D_MODEL), _full((D_MODEL, D_MODEL))), scratch_shapes=[pltpu.VMEM((D_MODEL, D_MODEL), F32)],
        compiler_params=_params("arbitrary"))(do, w_out, mix_a, mix_b, *deps)


def _gmlp_bwd(dmix, u, v, lnw, lnb, wcat, wtcat, bias, avg, expand_t):
    t_tok = u.shape[0]
    tm = min(_GMLP_ROWS, t_tok)

    def body(dm_ref, u_ref, v_ref, lnw_ref, lnb_ref, wcat_ref, wtcat_ref, bias_ref, avg_ref, expt_ref, du_ref, dv_ref,
             dw_ref, db_ref, dlnw_ref, dlnb_ref):
        i = pl.program_id(0)
        m_l, m_r = _lane_masks()
        avg = avg_ref[...]
        lnw = lnw_ref[...]
        ug, dug, dvg, rstd, vhat, vn, mixed = _gmlp_common(
            u_ref[...].astype(F32), v_ref[...].astype(F32), lnw, lnb_ref[...], avg, wcat_ref, bias_ref[...], m_l, m_r)
        dya = dm_ref[...].astype(F32)
        du_ref[...] = (dya * mixed * dug).astype(BF16)
        dmixed = dya * ug
        dvn_rows, dws, dbt = [], [None] * N_HEADS, None
        for r in range(tm // CHUNK):
            dvn_cols = []
            for j in range(N_HEADS // 2):
                dmp = dmixed[CHUNK * r:CHUNK * (r + 1), 128 * j:128 * (j + 1)]
                dvn_cols.append(_dot(wtcat_ref[j], _stack_pair(dmp, m_l, m_r)))
                vnp = vn[CHUNK * r:CHUNK * (r + 1), 128 * j:128 * (j + 1)].astype(BF16)
                for i_h, mask in enumerate((m_l, m_r)):
                    part = _dot((dmp * mask).astype(BF16), vnp, _NT)
                    dws[2 * j + i_h] = part if r == 0 else dws[2 * j + i_h] + part
            dvn_rows.append(jnp.concatenate(dvn_cols, axis=1))
            part = _split_dot(dmixed[CHUNK * r:CHUNK * (r + 1), :], expt_ref[...], 2)
            dbt = part if r == 0 else dbt + part
        dvn = jnp.concatenate(dvn_rows, axis=0)
        dvh = dvn * lnw
        dvgel = rstd * (dvh - _head_mean(dvh, avg) - vhat * _head_mean(dvh * vhat, avg))
        dv_ref[...] = (dvgel * dvg).astype(BF16)
        first = i == 0

        @pl.when(first)
        def _():
            for h in range(N_HEADS):
                dw_ref[h] = dws[h]
            db_ref[...] = dbt

        @pl.when(jnp.logical_not(first))
        def _():
            for h in range(N_HEADS):
                dw_ref[h] += dws[h]
            db_ref[...] += dbt

        _acc_rows(dlnw_ref, _rsum(dvn * vhat), first)
        _acc_rows(dlnb_ref, _rsum(dvn), first)

    row = pl.BlockSpec((tm, GM_WIDTH), lambda i: (i, 0))
    consts = [lnw, lnb, wcat, wtcat, bias, avg, expand_t]
    return pl.pallas_call(
        body, name="gmlp_bwd", grid=(t_tok // tm,),
        out_shape=(jax.ShapeDtypeStruct((t_tok, GM_WIDTH), BF16), jax.ShapeDtypeStruct((t_tok, GM_WIDTH), BF16),
                   jax.ShapeDtypeStruct((N_HEADS, CHUNK, CHUNK), F32), jax.ShapeDtypeStruct((CHUNK, CHUNK), F32),
                   jax.ShapeDtypeStruct((1, GM_WIDTH), F32), jax.ShapeDtypeStruct((1, GM_WIDTH), F32)),
        in_specs=[row, row, row] + [_full(a.shape) for a in consts],
        out_specs=(row, row, _full((N_HEADS, CHUNK, CHUNK)), _full((CHUNK, CHUNK)), _full((1, GM_WIDTH)),
                   _full((1, GM_WIDTH))),
        compiler_params=_params("arbitrary"))(dmix, u, v, *consts)


def _ssd_bwd(dmix, z, xbc, pre, dtr, y, states, cw, cb, dtb, alog, dskip_exp, nw, expand, expand_t, tril, triu, seq,
             dep=None):
    t_tok = z.shape[0]
    nb, nc, row, _, states_spec, fold, unfold = _ssd_specs(t_tok, seq, True)
    q = CHUNK

    def one_sequence(s, dm_ref, z_ref, xbc_ref, pre_ref, dtr_ref, y_ref, st_ref, cw_ref, dtb_ref, alog_ref, dsk_ref,
                     nw_ref, exp_ref, expt_ref, tril_ref, triu_ref, dz_ref, dxbc_ref, ddt_ref, dhead_ref, dstate_ref):
        m_l, m_r = _lane_masks()
        expt = expt_ref[...]
        f = _ssd_common(pre_ref[s], dtr_ref[s], dtb_ref[...], alog_ref[...], exp_ref[...], tril_ref[...])
        act, pre, sg = f["act"], f["pre"], f["sg"]
        xs = act[:, :SSM_WIDTH]
        xdt = xs * f["dt_exp"]
        xw = xdt * f["w_end"]
        state = st_ref[s, 0]
        dstate = dstate_ref[s]
        zv, yv, dout, nw = z_ref[s].astype(F32), y_ref[s], dm_ref[s].astype(F32), nw_ref[...]
        sz = jax.nn.sigmoid(zv)
        sl = zv * sz
        yg = yv * sl
        tv = dout * nw
        dyg_parts, ygh_parts = [], []
        for g in range(2):
            ygg = yg[:, 256 * g:256 * (g + 1)]
            rr = lax.rsqrt(jnp.mean(ygg * ygg, axis=-1, keepdims=True) + EPS)
            ygh = ygg * rr
            tg = tv[:, 256 * g:256 * (g + 1)]
            dyg_parts.append(rr * (tg - ygh * jnp.mean(tg * ygh, axis=-1, keepdims=True)))
            ygh_parts.append(ygh)
        dyg = jnp.concatenate(dyg_parts, axis=1)
        dnw = _rsum(dout * jnp.concatenate(ygh_parts, axis=1))
        dy = dyg * sl
        dz_ref[s] = (dyg * yv * (sz * (1.0 + zv * (1.0 - sz)))).astype(BF16)
        ddsk = _rsum(dy * xs)
        dye = dy * f["e"]
        lane = lax.broadcasted_iota(jnp.int32, (q, q), 1)
        sub = lax.broadcasted_iota(jnp.int32, (q, q), 0)
        rs_mat = jnp.zeros((q, q), F32)
        cs_mat = jnp.zeros((q, q), F32)
        dxdt_cols, yoff, dst_in, dxw, d_b, d_c = [], [], [], [], [], []
        for g in range(2):
            bg = act[:, 512 + 128 * g:640 + 128 * g].astype(BF16)
            cg = act[:, 768 + 128 * g:896 + 128 * g].astype(BF16)
            cb_mat = _dot(cg, bg, _NT)
            stg = state[:, 256 * g:256 * (g + 1)].astype(BF16)
            dyeg = dye[:, 256 * g:256 * (g + 1)].astype(BF16)
            yoff.append(_dot(cg, stg))
            dcg = _dot(dyeg, stg, _NT)
            dst_in.append(_dot(cg, dyeg, _TN))
            dcb = jnp.zeros((q, q), F32)
            for pr in range(2):
                h0 = 4 * g + 2 * pr
                gf = [cb_mat * f["decay"][h0], cb_mat * f["decay"][h0 + 1]]
                gcat = jnp.concatenate([gf[0].astype(BF16), gf[1].astype(BF16)], axis=1)
                xst = _stack_pair(xdt[:, 64 * h0:64 * h0 + 128], m_l, m_r)
                dyp = dy[:, 64 * h0:64 * h0 + 128].astype(BF16)
                dgcat = _dot(dyp, xst, _NT)
                dxst = _dot(gcat, dyp, _TN)
                dxdt_cols.append(dxst[:q] * m_l + dxst[q:] * m_r)
                for i in range(2):
                    h = h0 + i
                    dg = dgcat[:, q * i:q * (i + 1)]
                    mm = dg * gf[i]
                    rs_mat = rs_mat + jnp.where(lane == h, jnp.sum(mm, axis=1, keepdims=True), 0.0)
                    cs_mat = cs_mat + jnp.where(sub == h, jnp.sum(mm, axis=0, keepdims=True), 0.0)
                    dcb = dcb + dg * f["decay"][h]
            dcb16 = dcb.astype(BF16)
            dstg = dstate[:, 256 * g:256 * (g + 1)].astype(BF16)
            d_c.append(dcg + _dot(dcb16, bg))
            dxw.append(_dot(bg, dstg))
            d_b.append(_dot(dcb16, cg, _TN) + _dot(xw[:, 256 * g:256 * (g + 1)].astype(BF16), dstg, _NT))
        dxw = jnp.concatenate(dxw, axis=1)
        dxdt = jnp.concatenate(dxdt_cols, axis=1) + dxw * f["w_end"]
        qv = dxw * xw
        end_row = _rsum(qv) + _rsum(dstate * state) * f["cd"]
        x2 = dye * jnp.concatenate(yoff, axis=1) - qv
        row_i = lax.broadcasted_iota(jnp.int32, (q, 1), 0)
        x2 = x2 + jnp.where(row_i == q - 1, end_row, 0.0)
        da_cs = _split_dot(x2, expt, 2) + rs_mat - cs_mat.T
        ddt = _split_dot(dxdt * xs, expt, 2)
        dxs = dsk_ref[...] * dy + dxdt * f["dt_exp"]
        dda = _split_dot_left(triu_ref[...], da_cs, 3)
        ddt = ddt + dda * f["a_row"]
        dalog = _rsum(dda * f["dt"]) * f["a_row"]
        draw = ddt * jax.nn.sigmoid(f["dtp"])
        ddt_ref[s] = draw.astype(BF16)
        dact = jnp.concatenate([dxs] + d_b + d_c, axis=1)
        dpre = dact * (sg * (1.0 + pre * (1.0 - sg)))
        dhead = dhead_ref[s]
        xv = xbc_ref[s]
        shifted = [_shift_rows(dpre, dhead, 3 - k, False) for k in range(3)] + [dpre]
        dxbc = cw_ref[3:4, :] * dpre
        for k in range(3):
            dxbc = dxbc + cw_ref[k:k + 1, :] * shifted[k]
        dxbc_ref[s] = dxbc.astype(BF16)
        dhead_ref[s] = dpre[0:8, :]
        dstate_ref[s] = dstate * f["cd"] + jnp.concatenate(dst_in, axis=1)
        row8 = lax.broadcasted_iota(jnp.int32, (8, 1), 0)
        dcw = jnp.zeros((8, CONV_CH), F32)
        for k in range(4):
            dcw = dcw + jnp.where(row8 == k, _rsum(shifted[k] * xv), 0.0)
        return dcw, _rsum(dpre), _rsum(draw), dalog, _split_dot(ddsk, expt, 3), dnw

    def body(dm_ref, z_ref, xbc_ref, pre_ref, dtr_ref, y_ref, st_ref, cw_ref, cb_ref, dtb_ref, alog_ref, dsk_ref,
             nw_ref, exp_ref, expt_ref, tril_ref, triu_ref, dz_ref, dxbc_ref, ddt_ref, dcw_ref, dcb_ref, ddtb_ref,
             dalog_ref, dd_ref, dnw_ref, dhead_ref, dstate_ref):
        c = pl.program_id(0)
        first = c == 0

        @pl.when(first)
        def _():
            dstate_ref[...] = jnp.zeros_like(dstate_ref)
            dhead_ref[...] = jnp.zeros_like(dhead_ref)

        total = None
        for s in range(nb):
            parts = one_sequence(s, dm_ref, z_ref, xbc_ref, pre_ref, dtr_ref, y_ref, st_ref, cw_ref, dtb_ref, alog_ref,
                                 dsk_ref, nw_ref, exp_ref, expt_ref, tril_ref, triu_ref, dz_ref, dxbc_ref, ddt_ref,
                                 dhead_ref, dstate_ref)
            total = parts if total is None else tuple(a + b for a, b in zip(total, parts))
        dcw = total[0]

        @pl.when(first)
        def _():
            dcw_ref[...] = dcw

        @pl.when(jnp.logical_not(first))
        def _():
            dcw_ref[...] += dcw

        for ref, part in zip((dcb_ref, ddtb_ref, dalog_ref, dd_ref, dnw_ref), total[1:]):
            _acc_rows(ref, part, first)

    consts = [cw, cb, dtb, alog, dskip_exp, nw, expand, expand_t, tril, triu]
    deps = [] if dep is None else [dep]
    n_in = 7 + len(consts)

    def body_skipping_dep(*refs):
        body(*refs[:n_in], *refs[n_in + len(deps):])

    acc = lambda n: jax.ShapeDtypeStruct((1, n), F32)
    sd = lambda n: jax.ShapeDtypeStruct((nb, seq, n), BF16)
    dz, dxbc, ddt, *small_grads = pl.pallas_call(
        body_skipping_dep, name="ssd_bwd", grid=(nc,),
        out_shape=(sd(SSM_WIDTH), sd(CONV_CH), sd(CHUNK), jax.ShapeDtypeStruct((8, CONV_CH), F32), acc(CONV_CH),
                   acc(CHUNK), acc(CHUNK), acc(CHUNK), acc(SSM_WIDTH)),
        in_specs=[row(SSM_WIDTH, col=1), row(SSM_WIDTH), row(CONV_CH), row(CONV_CH), row(CHUNK), row(SSM_WIDTH),
                  states_spec]
        + [_full(a.shape) for a in consts] + [pl.BlockSpec(memory_space=pl.ANY)] * len(deps),
        out_specs=(row(SSM_WIDTH), row(CONV_CH), row(CHUNK), _full((8, CONV_CH)), _full((1, CONV_CH)),
                   _full((1, CHUNK)), _full((1, CHUNK)), _full((1, CHUNK)), _full((1, SSM_WIDTH))),
        scratch_shapes=[pltpu.VMEM((nb, 8, CONV_CH), F32), pltpu.VMEM((nb, N_STATE, SSM_WIDTH), F32)],
        compiler_params=_params("arbitrary"))(
            fold(dmix), fold(z), fold(xbc), fold(pre), fold(dtr), fold(y), states, *consts, *deps)
    return (unfold(dz), unfold(dxbc), unfold(ddt), *small_grads)


def _in_bwd(du, dv, dz, dxbc, ddt, w_in, x, dx2, g1, tm, me, riders=(), dep=None):
    t_tok = x.shape[0]
    steps = t_tok // tm

    n_in = [5 + ("mask" in rd) for rd in riders]
    first_in = [sum(n_in[:r]) for r in range(len(riders))]

    def body(me_ref, du_ref, dv_ref, dz_ref, dxbc_ref, ddt_ref, w_ref, x_ref, dx2_ref, g_ref, *rest):
        outs = rest[len(rest) - 2 - 4 * len(riders):]
        gx_ref, dg_ref = outs[:2]
        i = pl.program_id(0)
        dh = None
        for (a, b), ref in zip(_IN_SPLITS, (du_ref, dv_ref, dz_ref, dxbc_ref, ddt_ref)):
            part = _dot(ref[...], w_ref[a:b, :])
            dh = part if dh is None else dh + part
        dn, dg = _rms_bwd(x_ref[...], g_ref[...], dh)
        gx_ref[...] = dx2_ref[...] + dn
        _acc_rows(dg_ref, dg, i == 0)
        for r in range(len(riders)):
            p_ref, own_ref, w_ref_r, m_ref_r, v_ref_r = rest[first_in[r]:first_in[r] + 5]
            g = _sum_parts(me_ref[0], p_ref, own_ref[0])
            if n_in[r] == 6:
                g = g * rest[first_in[r] + 5][...]
            d, mn, vn = _adamw_math(w_ref_r[...], g, m_ref_r[...], v_ref_r[...])
            for o_ref, val in zip(outs[2 + 4 * r:6 + 4 * r], (g, d, mn, vn)):
                o_ref[...] = val

    row = lambda n: pl.BlockSpec((tm, n), lambda i, me_ref: (i, 0))
    whole = lambda shape: pl.BlockSpec(shape, lambda i, me_ref: (0,) * len(shape))
    widths = [b - a for a, b in _IN_SPLITS]
    deps = [] if dep is None else [dep]
    rider_args, rider_specs, rider_out_shapes, rider_out_specs = [], [], [], []
    for rd in riders:
        rows, cols = rd["w"].shape[0] // steps, rd["w"].shape[1]
        blk = pl.BlockSpec((rows, cols), lambda i, me_ref: (i, 0))
        rider_args += [rd["parts"], rd["own"], rd["w"], rd["m"], rd["v"]]
        rider_specs += [pl.BlockSpec((N_DEV, rows, cols), lambda i, me_ref: (0, i, 0)),
                        pl.BlockSpec((1, rows, cols), lambda i, me_ref: (me_ref[0], i, 0)), blk, blk, blk]
        if "mask" in rd:
            rider_args.append(rd["mask"])
            rider_specs.append(whole((rows, cols)))
        rider_out_shapes += [jax.ShapeDtypeStruct(rd["w"].shape, F32)] * 4
        rider_out_specs += [blk] * 4
    outs = pl.pallas_call(
        body, name="in_bwd",
        out_shape=(jax.ShapeDtypeStruct((t_tok, D_MODEL), F32), jax.ShapeDtypeStruct((1, D_MODEL), F32),
                   *rider_out_shapes),
        grid_spec=pltpu.PrefetchScalarGridSpec(
            num_scalar_prefetch=1, grid=(steps,),
            in_specs=[row(n) for n in widths] + [whole((IN_PAD, D_MODEL)), row(D_MODEL), row(D_MODEL),
                                                 whole((1, D_MODEL))] + rider_specs
            + [pl.BlockSpec(memory_space=pl.ANY)] * len(deps),
            out_specs=(row(D_MODEL), whole((1, D_MODEL)), *rider_out_specs)),
        compiler_params=_params("arbitrary"))(me, du, dv, dz, dxbc, ddt, w_in, x, dx2, g1, *rider_args, *deps)
    return outs[0], outs[1], [tuple(outs[2 + 4 * r:6 + 4 * r]) for r in range(len(riders))]


def _pad_lanes(a, n):
    return jnp.pad(a, ((0, 0), (0, n - a.shape[1])))


def _local_step(x, target, seq, small, hooks, first_dep=None):
    t_tok = x.shape[0]
    tm = min(TOKEN_TILE, t_tok)
    avg, expand, expand_t, tril, triu = _const_mats()
    g1, g2, g3, g4 = (small[k].reshape(1, D_MODEL) for k in
                      ("norm_mix_pre", "norm_mix_post", "norm_ffn_pre", "norm_ffn_post"))
    tie = (lambda a: a) if first_dep is None else (lambda a: a + first_dep[0, 0])
    lnw = tie(small["gm_ln_w"]).reshape(1, GM_WIDTH)
    lnb = tie(small["gm_ln_b"]).reshape(1, GM_WIDTH)
    causal = jnp.tril(jnp.ones((CHUNK, CHUNK), F32))
    wm = tie(small["gm_w_s"]) * causal
    pair = lambda w: w.reshape(4, 2, CHUNK, CHUNK).transpose(0, 2, 1, 3).reshape(4, CHUNK, 2 * CHUNK).astype(BF16)
    wcat = pair(wm)
    wtcat = pair(jnp.swapaxes(wm, 1, 2))
    bias = jnp.repeat(tie(small["gm_b_s"]).T, HEAD_DIM, axis=1)
    cb = small["conv_b"].reshape(1, CONV_CH)
    dtb = _pad_lanes(tie(small["dt_bias"]).reshape(1, N_HEADS), CHUNK)
    alog = _pad_lanes(tie(small["a_log"]).reshape(1, N_HEADS), CHUNK)
    dskip_exp = jnp.repeat(tie(small["d_skip"]).reshape(1, N_HEADS), HEAD_DIM, axis=1)
    nw = small["ssm_norm_w"].reshape(1, SSM_WIDTH)

    h1 = _prenorm(x, g1, tm, hooks.get("prenorm_after", first_dep))
    w_in_t, conv_w = hooks["mixer_weights"]((h1, lnw, lnb, wcat, wtcat, bias, dtb, alog, dskip_exp))
    tall = min(2 * tm, t_tok)
    u, v, z, xbc, dtr = _in_proj(h1, w_in_t, tall)
    mix_a = _gmlp_fwd(u, v, lnw, lnb, wcat, bias, avg)
    dep = hooks["gmlp_done"](mix_a) if "gmlp_done" in hooks else None
    mix_b, y_pre, states, pre = _ssd_fwd(z, xbc, dtr, conv_w, cb, dtb, alog, dskip_exp, nw, expand, tril, seq, dep)
    w_out, dep = hooks["mixers_done"](mix_b)
    o, x2, h3 = _out_proj(mix_a, mix_b, w_out, x, g2, g3, tall, dep)
    w_up, w_down = hooks["mlp_weights"](h3)
    tf = FF_TILE
    ra, dd, dy, dg4, loss = _mlp_fwd(h3, w_up, w_down, x2, target, g4, tm, tf)

    da, dx2, dmix, g_w_out, dg3, dg2 = _mlp_bwd(dd, w_down, ra, w_up, x2, dy, o, g3, g2, (w_out, mix_a, mix_b), tm, tf)
    g_w_down = _wgrad(ra, dd, None, WGRAD_TILE, D_MODEL, t_tok, True, "wgrad_down")
    g_w_up = _wgrad(h3, da, N_DEV, D_MODEL, D_FF // N_DEV, t_tok, False, "wgrad_up")
    dep = hooks["mlp_grads"](g_w_down, g_w_up)
    du, dv, dws, dbt, dlnw, dlnb = _gmlp_bwd(dmix, u, v, lnw, lnb, wcat, wtcat, bias, avg, expand_t)
    dep = hooks["gmlp_grads"](g_w_out, dws)
    dz, dxbc, ddt, dcw, dcb, ddtb, dalog, ddsk, dnw = _ssd_bwd(
        dmix, z, xbc, pre, dtr, y_pre, states, conv_w, cb, dtb, alog, dskip_exp, nw, expand, expand_t, tril, triu, seq,
        dep)
    g_w_in = _wgrad_in_chunked(h1, (du, dv, dz, dxbc, ddt), WGRAD_TILE, t_tok // 2, dep)
    dep = hooks["in_grads"](g_w_in, dcw[0:4])
    riders = hooks["arrived_updates"](dep) if "arrived_updates" in hooks else []
    me = hooks.get("me", jnp.zeros((1,), jnp.int32))
    grad_x, dg1, updates = _in_bwd(du, dv, dz, dxbc, ddt, w_in_t, x, dx2, g1, tm, me, riders, dep)

    grads = dict(
        updates=updates,
        w_in=g_w_in, w_out=g_w_out, w_up=g_w_up, w_down=g_w_down, conv_w=dcw[0:4],
        norm_mix_pre=dg1, norm_mix_post=dg2, norm_ffn_pre=dg3, norm_ffn_post=dg4, gm_ln_w=dlnw, gm_ln_b=dlnb,
        gm_w_s=dws, gm_b_s=dbt, conv_b=dcb, dt_bias=ddtb, a_log=dalog, d_skip=ddsk, ssm_norm_w=dnw)
    return loss[0, 0], grad_x, grads


_WEIGHTS = ("norm_mix_pre", "w_in", "gm_ln_w", "gm_ln_b", "gm_w_s", "gm_b_s", "conv_w", "conv_b", "dt_bias", "a_log",
            "d_skip", "ssm_norm_w", "w_out", "norm_mix_post", "norm_ffn_pre", "w_up", "w_down", "norm_ffn_post")
_SLAB_ROWS = (("norm_mix_pre", 1024), ("norm_mix_post", 1024), ("norm_ffn_pre", 1024), ("norm_ffn_post", 1024),
              ("conv_b", 1024), ("ssm_norm_w", 512), ("gm_ln_w", 512), ("gm_ln_b", 512), ("dt_bias", 8), ("a_log", 8),
              ("d_skip", 8))
_SLAB_LOSS_ROW = len(_SLAB_ROWS)
_SLAB_BS_ROW = 16
_SMALL_PARAMS = tuple(name for name, _ in _SLAB_ROWS) + ("gm_b_s",)
_LN_PARAMS = ("gm_ln_w", "gm_ln_b")


_SLAB_CONV_ROW = _SLAB_LOSS_ROW + 1


def _pack_slab(g, loss_part):
    rows = [_pad_lanes(g[name], D_MODEL) for name, _ in _SLAB_ROWS]
    rows.append(jnp.broadcast_to(loss_part, (1, D_MODEL)))
    rows.append(g["conv_w"])
    assert sum(r.shape[0] for r in rows) == _SLAB_BS_ROW
    rows.append(_pad_lanes(g["gm_b_s"].T[0:N_HEADS], D_MODEL))
    return jnp.concatenate(rows, axis=0)


def _adamw_slab(parts, me, w, m, v):
    names = _SMALL_PARAMS + ("conv_w",)
    shapes = [w[k].shape for k in names]
    unfold = np.zeros((GM_WIDTH, HEAD_DIM), np.float32)
    for h in range(N_HEADS):
        unfold[h * HEAD_DIM:(h + 1) * HEAD_DIM, :] = np.eye(HEAD_DIM)
    unfold = jnp.asarray(unfold, dtype=BF16)
    n = len(names)
    shard = CONV_CH // N_DEV

    def body(me_ref, p_ref, unfold_ref, *refs):
        w_refs, m_refs, v_refs = refs[:n], refs[n:2 * n], refs[2 * n:3 * n]
        outs = refs[3 * n:]
        g_all = p_ref[0]
        for j in range(1, N_DEV):
            g_all = g_all + p_ref[j]
        lane = lax.broadcasted_iota(jnp.int32, (N_HEADS, GM_WIDTH), 1)
        head = lax.broadcasted_iota(jnp.int32, (N_HEADS, GM_WIDTH), 0)
        own_lanes = jnp.logical_and(lane >= head * HEAD_DIM, lane < (head + 1) * HEAD_DIM)
        mine = pl.ds(pl.multiple_of(me_ref[0] * shard, shard), shard)
        for i, name in enumerate(names):
            if name == "gm_b_s":
                g = g_all[_SLAB_BS_ROW:_SLAB_BS_ROW + N_HEADS, 0:CHUNK]
            elif name == "conv_w":
                g = p_ref[0, _SLAB_CONV_ROW:_SLAB_CONV_ROW + 4, mine]
                for j in range(1, N_DEV):
                    g = g + p_ref[j, _SLAB_CONV_ROW:_SLAB_CONV_ROW + 4, mine]
            else:
                row = [r for r, (k, _) in enumerate(_SLAB_ROWS) if k == name][0]
                g = g_all[row:row + 1, 0:dict(_SLAB_ROWS)[name]]
                if name in _LN_PARAMS:
                    g = _split_dot(jnp.where(own_lanes, g, 0.0), unfold_ref[...], 3)
            d, mn, vn = _adamw_math(w_refs[i][...], g, m_refs[i][...], v_refs[i][...])
            for o_ref, val in zip(outs[4 * i:4 * i + 4], (g, d, mn, vn)):
                o_ref[...] = val
        outs[-1][...] = g_all[_SLAB_LOSS_ROW:_SLAB_LOSS_ROW + 1, 0:128]

    def whole(shape):
        nd = len(shape)
        return pl.BlockSpec(shape, lambda i, me_ref: (0,) * nd)

    ins = [parts, unfold] + [d[k] for d in (w, m, v) for k in names]
    out_shape = tuple(jax.ShapeDtypeStruct(s, F32) for s in shapes for _ in range(4)) + (
        jax.ShapeDtypeStruct((1, 128), F32),)
    outs = pl.pallas_call(
        body, name="adamw_small", out_shape=out_shape,
        grid_spec=pltpu.PrefetchScalarGridSpec(
            num_scalar_prefetch=1, grid=(1,), in_specs=[whole(a.shape) for a in ins],
            out_specs=tuple(whole(s.shape) for s in out_shape)),
        compiler_params=_params("arbitrary"))(me, *ins)
    return {k: tuple(outs[4 * i:4 * i + 4]) for i, k in enumerate(names)}, outs[-1][0, 0]


def kernel(x, norm_mix_pre, w_in, gm_ln_w, gm_ln_b, gm_w_s, gm_b_s, conv_w, conv_b, dt_bias, a_log, d_skip, ssm_norm_w, w_out, norm_mix_post, norm_ffn_pre, w_up, w_down, norm_ffn_post, loss_target, m_norm_mix_pre, m_w_in, m_gm_ln_w, m_gm_ln_b, m_gm_w_s, m_gm_b_s, m_conv_w, m_conv_b, m_dt_bias, m_a_log, m_d_skip, m_ssm_norm_w, m_w_out, m_norm_mix_post, m_norm_ffn_pre, m_w_up, m_w_down, m_norm_ffn_post, v_norm_mix_pre, v_w_in, v_gm_ln_w, v_gm_ln_b, v_gm_w_s, v_gm_b_s, v_conv_w, v_conv_b, v_dt_bias, v_a_log, v_d_skip, v_ssm_norm_w, v_w_out, v_norm_mix_post, v_norm_ffn_pre, v_w_up, v_w_down, v_norm_ffn_post):
    w = dict(norm_mix_pre=norm_mix_pre, w_in=w_in, gm_ln_w=gm_ln_w, gm_ln_b=gm_ln_b, gm_w_s=gm_w_s, gm_b_s=gm_b_s, conv_w=conv_w, conv_b=conv_b, dt_bias=dt_bias, a_log=a_log, d_skip=d_skip, ssm_norm_w=ssm_norm_w, w_out=w_out, norm_mix_post=norm_mix_post, norm_ffn_pre=norm_ffn_pre, w_up=w_up, w_down=w_down, norm_ffn_post=norm_ffn_post)
    m = dict(norm_mix_pre=m_norm_mix_pre, w_in=m_w_in, gm_ln_w=m_gm_ln_w, gm_ln_b=m_gm_ln_b, gm_w_s=m_gm_w_s, gm_b_s=m_gm_b_s, conv_w=m_conv_w, conv_b=m_conv_b, dt_bias=m_dt_bias, a_log=m_a_log, d_skip=m_d_skip, ssm_norm_w=m_ssm_norm_w, w_out=m_w_out, norm_mix_post=m_norm_mix_post, norm_ffn_pre=m_norm_ffn_pre, w_up=m_w_up, w_down=m_w_down, norm_ffn_post=m_norm_ffn_post)
    v = dict(norm_mix_pre=v_norm_mix_pre, w_in=v_w_in, gm_ln_w=v_gm_ln_w, gm_ln_b=v_gm_ln_b, gm_w_s=v_gm_w_s, gm_b_s=v_gm_b_s, conv_w=v_conv_w, conv_b=v_conv_b, dt_bias=v_dt_bias, a_log=v_a_log, d_skip=v_d_skip, ssm_norm_w=v_ssm_norm_w, w_out=v_w_out, norm_mix_post=v_norm_mix_post, norm_ffn_pre=v_norm_ffn_pre, w_up=v_w_up, w_down=v_w_down, norm_ffn_post=v_norm_ffn_post)
    n_batch, seq, _ = x.shape
    shard_in = IN_COLS // N_DEV

    me = (4 * lax.axis_index("x") + 2 * lax.axis_index("y") + lax.axis_index("c")).astype(jnp.int32).reshape(1)

    def in_slot(own):
        return lax.dynamic_update_slice(lax.empty((N_DEV,) + own.shape, own.dtype), own[None],
                                        (me[0],) + (0,) * own.ndim)

    lying = lambda t: jnp.transpose(t, (2, 0, 1))
    first = [_cast_to_slot(lying(w_in), me, shard_in, "cast_w_in"), in_slot(conv_w[0])]
    ici_1, tok_ici_1 = _exchange_start(first, [True] * 2, _SAME_CORE_PEERS, "gather_mix_ici_start")
    cast_out = _cast_to_slot(w_out[0], me, 128, "cast_w_out", dep=tok_ici_1)
    cast_up = _cast_to_slot(w_up[0], me, 1024, "cast_w_up", cols=True, dep=cast_out)
    second = [cast_out, cast_up, _cast_to_slot(w_down[0], me, 512, "cast_w_down", dep=cast_up)]
    gathering = {}

    def mixer_weights(after):
        bufs = [buf for buf, _ in _exchange_wait(ici_1, after, "gather_mix_ici_wait")]
        d2d_1, tok_d2d_1 = _exchange_start(bufs, [True] * 2, _SIBLING_FORWARD, "gather_mix_d2d_start")
        gathering["late_ici"], tok_ici_2 = _exchange_start(
            second, [True] * 3, _SAME_CORE_PEERS, "gather_late_ici_start", dep=tok_d2d_1)
        (_, ag_in), (_, ag_conv) = _exchange_wait(d2d_1, tok_ici_2, "gather_mix_d2d_wait")
        w_in_t = _stack_shards(ag_in, IN_PAD, STACK_TILE, "stack_w_in")
        return w_in_t, ag_conv.transpose(1, 0, 2).reshape(4, CONV_CH)

    def gmlp_done(after):
        ((buf, _),) = _exchange_wait(gathering["late_ici"], after, "gather_out_ici_wait", only=(0,))
        gathering["out"], tok = _exchange_start([buf], [True], _SIBLING_FORWARD, "gather_out_d2d_start")
        return tok

    def mixers_done(after):
        bufs = [buf for buf, _ in _exchange_wait(gathering["late_ici"], after, "gather_mlp_ici_wait", only=(1, 2))]
        gathering["mlp"], tok = _exchange_start(bufs, [True] * 2, _SIBLING_FORWARD, "gather_mlp_d2d_start")
        ((_, ag_out),) = _exchange_wait(gathering["out"], tok, "gather_out_d2d_wait")
        return ag_out.reshape(D_MODEL, D_MODEL), tok

    def mlp_weights(after):
        (_, ag_up), (_, ag_down) = _exchange_wait(gathering["mlp"], after, "gather_mlp_d2d_wait")
        return ag_up, ag_down.reshape(D_FF, D_MODEL)

    sent = {}

    def mlp_grads(g_w_down, g_w_up):
        sent["mlp"], tok = _exchange_start(
            [g_w_down.reshape(N_DEV, D_FF // N_DEV, D_MODEL), g_w_up], [False, False], _ALL_PEERS, "grads_mlp_start")
        return tok

    def gmlp_grads(g_w_out, g_w_s):
        sent["gmlp"], tok = _exchange_start(
            [g_w_out.reshape(N_DEV, D_MODEL // N_DEV, D_MODEL), in_slot(g_w_s.astype(BF16))], [False, True], _ALL_PEERS,
            "grads_gmlp_start")
        return tok

    def in_grads(g_w_in_t, g_conv_w):
        sent["in"], tok = _exchange_start([g_w_in_t], [False], _ALL_PEERS, "grads_in_start")
        return tok

    def arrived_updates(after):
        (own_down, p_down), (own_up, p_up) = _exchange_wait(sent["mlp"], after, "grads_mlp_wait")
        (own_out, p_out), (_, p_ws) = _exchange_wait(sent["gmlp"], own_up, "grads_gmlp_wait")
        rows = lambda t: t.reshape(t.shape[:-3] + (N_HEADS * CHUNK, CHUNK))
        return [dict(parts=p_up, own=own_up, w=w_up[0], m=m_w_up[0], v=v_w_up[0]),
                dict(parts=p_down, own=own_down, w=w_down[0], m=m_w_down[0], v=v_w_down[0]),
                dict(parts=p_out, own=own_out, w=w_out[0], m=m_w_out[0], v=v_w_out[0]),
                dict(parts=rows(p_ws), own=rows(p_ws), w=rows(gm_w_s[0]), m=rows(m_gm_w_s[0]), v=rows(v_gm_w_s[0]),
                     mask=jnp.tril(jnp.ones((CHUNK, CHUNK), F32)))]

    small = {k: w[k][0] for k in _SMALL_PARAMS + ("gm_w_s",)}
    loss_part, grad_x, g = _local_step(
        x.reshape(n_batch * seq, D_MODEL), loss_target.reshape(n_batch * seq, D_MODEL), seq, small,
        dict(mixer_weights=mixer_weights, gmlp_done=gmlp_done, mixers_done=mixers_done, mlp_weights=mlp_weights,
             mlp_grads=mlp_grads, gmlp_grads=gmlp_grads, in_grads=in_grads, arrived_updates=arrived_updates, me=me,
             prenorm_after=second[2]), first_dep=tok_ici_1)

    sent_rows, tok_rows = _exchange_start([in_slot(_pack_slab(g, loss_part))], [True], _ALL_PEERS, "grads_rows_start")
    res = dict(zip(("w_up", "w_down", "w_out", "gm_w_s"), g["updates"]))
    ((own_in, p_in),) = _exchange_wait(sent["in"], tok_rows, "grads_in_wait")
    upd_in = _adamw_reduce(p_in, own_in, me, lying(w_in), lying(m_w_in), lying(v_w_in), "adamw_w_in")
    res["w_in"] = tuple(jnp.transpose(t, (1, 2, 0)) for t in upd_in)
    ((_, p_rows),) = _exchange_wait(sent_rows, upd_in[1], "grads_rows_wait")
    flat = lambda t: t[0] if t.ndim == 3 else t
    small_res, loss = _adamw_slab(
        p_rows, me, *({k: flat(d[k]) for k in _SMALL_PARAMS + ("conv_w",)} for d in (w, m, v)))
    res.update(small_res)
    res = {k: tuple(r.reshape(w[k].shape) for r in res[k]) for k in _WEIGHTS}

    outs = [loss, grad_x.reshape(x.shape)]
    for part in range(4):
        outs.extend(res[k][part] for k in _WEIGHTS)
    return tuple(outs)
```
